```python
import math
import jax, jax.numpy as jnp
from jax import lax
import numpy as np

D_MODEL = 1024
BATCH = 8
SEQ = 2048
DEPTH = 1

N_META = 16
BLOCK = 128
WINDOW = 128
PAD = (-N_META) % BLOCK
NORM_EPS = 1e-6

ATT_HEAD_DIM = 64
ATT_Q_HEADS = D_MODEL // ATT_HEAD_DIM
ATT_KV_HEADS = 4
ATT_GROUP = ATT_Q_HEADS // ATT_KV_HEADS
ATT_WIDTH = ATT_Q_HEADS * ATT_HEAD_DIM
KV_WIDTH = ATT_KV_HEADS * ATT_HEAD_DIM

SSM_INNER = 2 * D_MODEL
SSM_HEAD_DIM = 64
SSM_HEADS = SSM_INNER // SSM_HEAD_DIM
SSM_GROUPS = 4
SSM_HEADS_PER_GROUP = SSM_HEADS // SSM_GROUPS
SSM_STATE = 128
CONV_WIDTH = 4
CONV_DIM = SSM_INNER + 2 * SSM_GROUPS * SSM_STATE

SPLIT_SIZES = (ATT_WIDTH, KV_WIDTH, KV_WIDTH, ATT_WIDTH, SSM_INNER, CONV_DIM, SSM_HEADS, D_MODEL, D_MODEL)
SPLIT_POINTS = tuple(int(s) for s in np.cumsum(SPLIT_SIZES)[:-1])
IN_PROJ_DIM = int(sum(SPLIT_SIZES))

kernel_name = 'hybrid_swa_sink_alibi_ssd_gated_merge'


def rmsnorm(x, g):
    xf = x.astype(jnp.float32)
    y = xf * lax.rsqrt(jnp.mean(xf * xf, axis=-1, keepdims=True) + NORM_EPS) * g.astype(jnp.float32)
    return y.astype(x.dtype)


def alibi_slopes():
    return jnp.asarray(np.array([2.0 ** (-8.0 * (h + 1) / ATT_Q_HEADS) for h in range(ATT_Q_HEADS)], np.float32))


def sliding_window_attention(q, k, v, sinks):
    b, lp, _ = q.shape
    nb = lp // BLOCK
    km = k[:, PAD:PAD + N_META].reshape(b, N_META, ATT_KV_HEADS, ATT_HEAD_DIM)
    vm = v[:, PAD:PAD + N_META].reshape(b, N_META, ATT_KV_HEADS, ATT_HEAD_DIM)
    qb = q.reshape(b, nb, BLOCK, ATT_KV_HEADS, ATT_GROUP, ATT_HEAD_DIM) * (ATT_HEAD_DIM ** -0.5)

    def with_prev(t):
        t = t.reshape(b, nb, BLOCK, ATT_KV_HEADS, ATT_HEAD_DIM)
        prev = jnp.concatenate([jnp.zeros_like(t[:, :1]), t[:, :-1]], axis=1)
        return jnp.concatenate([prev, t], axis=2)

    kb, vb = with_prev(k), with_prev(v)

    q_pos = jnp.arange(nb)[:, None] * BLOCK + jnp.arange(BLOCK)[None, :]
    k_pos = jnp.arange(nb)[:, None] * BLOCK - BLOCK + jnp.arange(2 * BLOCK)[None, :]
    rel = q_pos[:, :, None] - k_pos[:, None, :]
    band_ok = (rel >= 0) & (rel < WINDOW) & (k_pos[:, None, :] >= PAD + N_META)
    meta_pos = PAD + jnp.arange(N_META)
    meta_ok = meta_pos[None, None, :] <= q_pos[:, :, None]

    slopes = alibi_slopes().reshape(1, 1, ATT_KV_HEADS, ATT_GROUP, 1, 1)
    s_band = jnp.einsum('bnqkgd,bnskd->bnkgqs', qb, kb).astype(jnp.float32)
    s_band = s_band - slopes * rel[None, :, None, None].astype(jnp.float32)
    s_band = jnp.where(band_ok[None, :, None, None], s_band, -jnp.inf)
    s_meta = jnp.einsum('bnqkgd,bmkd->bnkgqm', qb, km).astype(jnp.float32)
    s_meta = jnp.where(meta_ok[None, :, None, None], s_meta, -jnp.inf)
    sink = jnp.broadcast_to(sinks.astype(jnp.float32).reshape(1, 1, ATT_KV_HEADS, ATT_GROUP, 1, 1),
                            s_meta.shape[:-1] + (1,))
    p = jax.nn.softmax(jnp.concatenate([sink, s_meta, s_band], axis=-1), axis=-1).astype(v.dtype)
    p_meta, p_band = p[..., 1:1 + N_META], p[..., 1 + N_META:]
    o = jnp.einsum('bnkgqm,bmkd->bnqkgd', p_meta, vm) + jnp.einsum('bnkgqs,bnskd->bnqkgd', p_band, vb)
    return o.reshape(b, lp, ATT_WIDTH)


def causal_depthwise_conv(u, w, bias):
    out = lax.conv_general_dilated(u, w[:, None, :].astype(u.dtype), window_strides=(1,),
                                   padding=[(CONV_WIDTH - 1, 0)],
                                   dimension_numbers=('NWC', 'WIO', 'NWC'),
                                   feature_group_count=u.shape[-1])
    return out + bias.astype(u.dtype)


def segsum(a):
    T = a.shape[-1]
    cs = jnp.cumsum(a, axis=-1)
    diff = cs[..., :, None] - cs[..., None, :]
    return jnp.where(jnp.tril(jnp.ones((T, T), bool)), diff, -jnp.inf)


def ssd_chunked(x, dt, A, Bm, Cm):
    b, L = x.shape[:2]
    nc = L // BLOCK
    G, R, P, N = SSM_GROUPS, SSM_HEADS_PER_GROUP, SSM_HEAD_DIM, SSM_STATE
    xr = (x * dt[..., None]).reshape(b, nc, BLOCK, G, R, P)
    a = (dt * A).reshape(b, nc, BLOCK, G, R).transpose(0, 1, 3, 4, 2)
    Br = Bm.reshape(b, nc, BLOCK, G, N)
    Cr = Cm.reshape(b, nc, BLOCK, G, N)
    a_cs = jnp.cumsum(a, axis=-1)
    decay = jnp.exp(segsum(a))
    cb = jnp.einsum('bclgn,bcsgn->bcgls', Cr, Br)
    y_diag = jnp.einsum('bcgls,bcgrls,bcsgrp->bclgrp', cb, decay, xr)
    decay_states = jnp.exp(a_cs[..., -1:] - a_cs)
    states = jnp.einsum('bclgn,bcgrl,bclgrp->bcgrpn', Br, decay_states, xr)
    chunk_decay = jnp.exp(a_cs[..., -1])

    def step(carry, inp):
        s_c, d_c = inp
        return carry * d_c[..., None, None] + s_c, carry

    init = jnp.zeros((b, G, R, P, N), jnp.float32)
    _, prev = lax.scan(step, init, (jnp.moveaxis(states, 1, 0), jnp.moveaxis(chunk_decay, 1, 0)))
    prev = jnp.moveaxis(prev, 0, 1)
    y_off = jnp.einsum('bclgn,bcgrpn,bcgrl->bclgrp', Cr, prev, jnp.exp(a_cs))
    return (y_diag + y_off).reshape(b, L, SSM_HEADS, P)


def ssd_branch(z, xbc, dt_raw, conv_w, conv_b, dt_bias, a_log, d_skip, g_norm, valid):
    b, L, _ = xbc.shape
    xbc = jax.nn.silu(causal_depthwise_conv(xbc, conv_w, conv_b)) * valid[None, :, None]
    xbc = xbc.astype(jnp.float32)
    xs = xbc[..., :SSM_INNER].reshape(b, L, SSM_HEADS, SSM_HEAD_DIM)
    Bm = xbc[..., SSM_INNER:SSM_INNER + SSM_GROUPS * SSM_STATE].reshape(b, L, SSM_GROUPS, SSM_STATE)
    Cm = xbc[..., SSM_INNER + SSM_GROUPS * SSM_STATE:].reshape(b, L, SSM_GROUPS, SSM_STATE)
    dt = jax.nn.softplus(dt_raw.astype(jnp.float32) + dt_bias.astype(jnp.float32))
    A = -jnp.exp(a_log.astype(jnp.float32))
    y = ssd_chunked(xs, dt, A, Bm, Cm) + d_skip.astype(jnp.float32)[:, None] * xs
    y = y.reshape(b, L, SSM_INNER) * jax.nn.silu(z.astype(jnp.float32))
    yg = y.reshape(b, L, SSM_GROUPS, SSM_INNER // SSM_GROUPS)
    yg = yg * lax.rsqrt(jnp.mean(yg * yg, axis=-1, keepdims=True) + NORM_EPS)
    y = yg.reshape(b, L, SSM_INNER) * g_norm.astype(jnp.float32)
    return y.astype(z.dtype)


def hybrid_layer(h, valid, g_pre, w_in, conv_w, conv_b, dt_bias, a_log, d_skip, attn_sinks,
                 g_ssm_norm, w_out_att, w_out_ssm, w_out, g_post):
    u = rmsnorm(h, g_pre)
    proj = u @ w_in
    q, k, v, z_att, z_ssm, xbc, dt_raw, gate_att, gate_ssm = jnp.split(proj, SPLIT_POINTS, axis=-1)
    y_att = (sliding_window_attention(q, k, v, attn_sinks) * jax.nn.silu(z_att)) @ w_out_att
    y_ssm = ssd_branch(z_ssm, xbc, dt_raw, conv_w, conv_b, dt_bias, a_log, d_skip, g_ssm_norm, valid) @ w_out_ssm
    merged = jax.nn.sigmoid(gate_att) * y_att + jax.nn.sigmoid(gate_ssm) * y_ssm
    out = merged @ w_out
    return h + rmsnorm(out, g_post) * valid[None, :, None]


def _fwd_setup_inputs(seed: int = 0) -> dict:
    key = jax.random.key(seed)
    ks = jax.random.split(key, 16)
    f32 = jnp.float32

    def nrm(k, shape, scale):
        return jax.random.normal(k, shape, f32) * scale

    x = nrm(ks[0], (BATCH, SEQ, D_MODEL), 1.0)
    meta_tokens = nrm(ks[1], (N_META, D_MODEL), 1.0)
    g_pre = 1.0 + nrm(ks[2], (DEPTH, D_MODEL), 0.01)
    w_in = nrm(ks[3], (DEPTH, D_MODEL, IN_PROJ_DIM), D_MODEL ** -0.5)
    conv_w = nrm(ks[4], (DEPTH, CONV_WIDTH, CONV_DIM), CONV_WIDTH ** -0.5)
    conv_b = nrm(ks[5], (DEPTH, CONV_DIM), 0.02)
    dt0 = jnp.exp(jax.random.uniform(ks[6], (DEPTH, SSM_HEADS), f32, math.log(1e-3), math.log(1e-1)))
    dt_bias = dt0 + jnp.log(-jnp.expm1(-dt0))
    a_log = jnp.log(jax.random.uniform(ks[7], (DEPTH, SSM_HEADS), f32, 1.0, 16.0))
    d_skip = 1.0 + nrm(ks[8], (DEPTH, SSM_HEADS), 0.1)
    attn_sinks = nrm(ks[9], (DEPTH, ATT_Q_HEADS), 0.5)
    g_ssm_norm = 1.0 + nrm(ks[10], (DEPTH, SSM_INNER), 0.01)
    w_out_att = nrm(ks[11], (DEPTH, ATT_WIDTH, D_MODEL), ATT_WIDTH ** -0.5)
    w_out_ssm = nrm(ks[12], (DEPTH, SSM_INNER, D_MODEL), SSM_INNER ** -0.5)
    w_out = nrm(ks[13], (DEPTH, D_MODEL, D_MODEL), D_MODEL ** -0.5)
    g_post = 1.0 + nrm(ks[14], (DEPTH, D_MODEL), 0.01)
    return {'x': x, 'meta_tokens': meta_tokens, 'g_pre': g_pre, 'w_in': w_in, 'conv_w': conv_w,
            'conv_b': conv_b, 'dt_bias': dt_bias, 'a_log': a_log, 'd_skip': d_skip,
            'attn_sinks': attn_sinks, 'g_ssm_norm': g_ssm_norm, 'w_out_att': w_out_att,
            'w_out_ssm': w_out_ssm, 'w_out': w_out, 'g_post': g_post}


def _fwd_reference(x, meta_tokens, g_pre, w_in, conv_w, conv_b, dt_bias, a_log, d_skip, attn_sinks,
              g_ssm_norm, w_out_att, w_out_ssm, w_out, g_post):
    b = x.shape[0]
    lp = PAD + N_META + x.shape[1]
    h = jnp.concatenate([jnp.zeros((b, PAD, D_MODEL), x.dtype),
                         jnp.broadcast_to(meta_tokens.astype(x.dtype)[None], (b, N_META, D_MODEL)),
                         x], axis=1)
    valid = (jnp.arange(lp) >= PAD).astype(x.dtype)
    for i in range(DEPTH):
        h = hybrid_layer(h, valid, g_pre[i], w_in[i], conv_w[i], conv_b[i], dt_bias[i], a_log[i],
                         d_skip[i], attn_sinks[i], g_ssm_norm[i], w_out_att[i], w_out_ssm[i],
                         w_out[i], g_post[i])
    return h[:, PAD + N_META:]


import jax as _jax
import jax.numpy as _jnp

TWIN_FORMAT = 'train_step'
FWD_PARAMS = ['x', 'meta_tokens', 'g_pre', 'w_in', 'conv_w', 'conv_b', 'dt_bias', 'a_log', 'd_skip', 'attn_sinks', 'g_ssm_norm', 'w_out_att', 'w_out_ssm', 'w_out', 'g_post']
TWIN_WEIGHTS = ['meta_tokens', 'g_pre', 'w_in', 'conv_w', 'conv_b', 'dt_bias', 'a_log', 'd_skip', 'attn_sinks', 'g_ssm_norm', 'w_out_att', 'w_out_ssm', 'w_out', 'g_post']
TWIN_DIFF_INPUT = 'x'
TWIN_INPUTS = ['x', 'meta_tokens', 'g_pre', 'w_in', 'conv_w', 'conv_b', 'dt_bias', 'a_log', 'd_skip', 'attn_sinks', 'g_ssm_norm', 'w_out_att', 'w_out_ssm', 'w_out', 'g_post', 'loss_target', 'm_meta_tokens', 'm_g_pre', 'm_w_in', 'm_conv_w', 'm_conv_b', 'm_dt_bias', 'm_a_log', 'm_d_skip', 'm_attn_sinks', 'm_g_ssm_norm', 'm_w_out_att', 'm_w_out_ssm', 'm_w_out', 'm_g_post', 'v_meta_tokens', 'v_g_pre', 'v_w_in', 'v_conv_w', 'v_conv_b', 'v_dt_bias', 'v_a_log', 'v_d_skip', 'v_attn_sinks', 'v_g_ssm_norm', 'v_w_out_att', 'v_w_out_ssm', 'v_w_out', 'v_g_post']
TWIN_OUTPUTS = ['loss', 'grad_x', 'grad_meta_tokens', 'grad_g_pre', 'grad_w_in', 'grad_conv_w', 'grad_conv_b', 'grad_dt_bias', 'grad_a_log', 'grad_d_skip', 'grad_attn_sinks', 'grad_g_ssm_norm', 'grad_w_out_att', 'grad_w_out_ssm', 'grad_w_out', 'grad_g_post', 'delta_meta_tokens', 'delta_g_pre', 'delta_w_in', 'delta_conv_w', 'delta_conv_b', 'delta_dt_bias', 'delta_a_log', 'delta_d_skip', 'delta_attn_sinks', 'delta_g_ssm_norm', 'delta_w_out_att', 'delta_w_out_ssm', 'delta_w_out', 'delta_g_post', 'new_m_meta_tokens', 'new_m_g_pre', 'new_m_w_in', 'new_m_conv_w', 'new_m_conv_b', 'new_m_dt_bias', 'new_m_a_log', 'new_m_d_skip', 'new_m_attn_sinks', 'new_m_g_ssm_norm', 'new_m_w_out_att', 'new_m_w_out_ssm', 'new_m_w_out', 'new_m_g_post', 'new_v_meta_tokens', 'new_v_g_pre', 'new_v_w_in', 'new_v_conv_w', 'new_v_conv_b', 'new_v_dt_bias', 'new_v_a_log', 'new_v_d_skip', 'new_v_attn_sinks', 'new_v_g_ssm_norm', 'new_v_w_out_att', 'new_v_w_out_ssm', 'new_v_w_out', 'new_v_g_post']
TWIN_LEAF_KINDS = {'loss': 'loss', 'grad_x': 'grad_x', 'grad_meta_tokens': 'grad_w', 'grad_g_pre': 'grad_w', 'grad_w_in': 'grad_w', 'grad_conv_w': 'grad_w', 'grad_conv_b': 'grad_w', 'grad_dt_bias': 'grad_w', 'grad_a_log': 'grad_w', 'grad_d_skip': 'grad_w', 'grad_attn_sinks': 'grad_w', 'grad_g_ssm_norm': 'grad_w', 'grad_w_out_att': 'grad_w', 'grad_w_out_ssm': 'grad_w', 'grad_w_out': 'grad_w', 'grad_g_post': 'grad_w', 'delta_meta_tokens': 'delta_w', 'delta_g_pre': 'delta_w', 'delta_w_in': 'delta_w', 'delta_conv_w': 'delta_w', 'delta_conv_b': 'delta_w', 'delta_dt_bias': 'delta_w', 'delta_a_log': 'delta_w', 'delta_d_skip': 'delta_w', 'delta_attn_sinks': 'delta_w', 'delta_g_ssm_norm': 'delta_w', 'delta_w_out_att': 'delta_w', 'delta_w_out_ssm': 'delta_w', 'delta_w_out': 'delta_w', 'delta_g_post': 'delta_w', 'new_m_meta_tokens': 'new_m', 'new_m_g_pre': 'new_m', 'new_m_w_in': 'new_m', 'new_m_conv_w': 'new_m', 'new_m_conv_b': 'new_m', 'new_m_dt_bias': 'new_m', 'new_m_a_log': 'new_m', 'new_m_d_skip': 'new_m', 'new_m_attn_sinks': 'new_m', 'new_m_g_ssm_norm': 'new_m', 'new_m_w_out_att': 'new_m', 'new_m_w_out_ssm': 'new_m', 'new_m_w_out': 'new_m', 'new_m_g_post': 'new_m', 'new_v_meta_tokens': 'new_v', 'new_v_g_pre': 'new_v', 'new_v_w_in': 'new_v', 'new_v_conv_w': 'new_v', 'new_v_conv_b': 'new_v', 'new_v_dt_bias': 'new_v', 'new_v_a_log': 'new_v', 'new_v_d_skip': 'new_v', 'new_v_attn_sinks': 'new_v', 'new_v_g_ssm_norm': 'new_v', 'new_v_w_out_att': 'new_v', 'new_v_w_out_ssm': 'new_v', 'new_v_w_out': 'new_v', 'new_v_g_post': 'new_v'}


def _forward(args):
    return _fwd_reference(*[args[k] for k in FWD_PARAMS])


def _output_shape():
    out = _jax.eval_shape(lambda: _forward(_fwd_setup_inputs(0)))
    return out.shape, out.dtype

N_MICROBATCH = 1
ADAM_LR = 0.001
ADAM_B1 = 0.9
ADAM_B2 = 0.999
ADAM_EPS = 1e-08
ADAM_WD = 0.01
ADAM_STEP = 10
PER_EXAMPLE_BATCH_AXIS = {'x': 0, 'loss_target': 0}
SHARED_INPUTS = []
_WEIGHT_DTYPES = {'meta_tokens': _jnp.float32, 'g_pre': _jnp.float32, 'w_in': _jnp.float32, 'conv_w': _jnp.float32, 'conv_b': _jnp.float32, 'dt_bias': _jnp.float32, 'a_log': _jnp.float32, 'd_skip': _jnp.float32, 'attn_sinks': _jnp.float32, 'g_ssm_norm': _jnp.float32, 'w_out_att': _jnp.float32, 'w_out_ssm': _jnp.float32, 'w_out': _jnp.float32, 'g_post': _jnp.float32}
MOMENT_SCALE = {'meta_tokens': 9.007526e-03, 'g_pre': 3.289400e-01, 'w_in': 1.036923e-01, 'conv_w': 1.239294e-01, 'conv_b': 2.564392e-01, 'dt_bias': 2.658116e-01, 'a_log': 3.085174e-01, 'd_skip': 8.043705e-01, 'attn_sinks': 8.958505e-03, 'g_ssm_norm': 1.630091e-01, 'w_out_att': 3.505851e-02, 'w_out_ssm': 2.407267e-01, 'w_out': 2.434227e-01, 'g_post': 1.602224e+01}


def _to_microbatches(a, axis):
    t = _jnp.moveaxis(a, axis, 0)
    t = t.reshape((N_MICROBATCH, t.shape[0] // N_MICROBATCH) + t.shape[1:])
    return _jnp.moveaxis(t, 1, axis + 1)


def setup_inputs(seed: int = 0) -> dict:
    inp = _fwd_setup_inputs(seed)
    key = _jax.random.fold_in(_jax.random.key(seed), 7919)
    shape, _ = _output_shape()
    out = dict(inp)
    out["loss_target"] = _jax.random.normal(_jax.random.fold_in(key, 0), shape, _jnp.float32)
    for i, name in enumerate(TWIN_WEIGHTS):
        w = inp[name].astype(_jnp.float32)
        if MOMENT_SCALE is None:
            s = _jnp.sqrt(_jnp.mean(_jnp.square(w)) + 1e-30)
        else:
            s = MOMENT_SCALE[name]
        km, kv = _jax.random.split(_jax.random.fold_in(key, i + 1))
        out[name] = w
        out["m_" + name] = s * _jax.random.normal(km, w.shape, _jnp.float32)
        out["v_" + name] = (s * s) * _jax.random.uniform(kv, w.shape, _jnp.float32, 0.5, 1.5)
    if N_MICROBATCH > 1:
        for name, axis in PER_EXAMPLE_BATCH_AXIS.items():
            out[name] = _to_microbatches(out[name], axis)
    return {'x': out['x'], 'meta_tokens': out['meta_tokens'], 'g_pre': out['g_pre'], 'w_in': out['w_in'], 'conv_w': out['conv_w'], 'conv_b': out['conv_b'], 'dt_bias': out['dt_bias'], 'a_log': out['a_log'], 'd_skip': out['d_skip'], 'attn_sinks': out['attn_sinks'], 'g_ssm_norm': out['g_ssm_norm'], 'w_out_att': out['w_out_att'], 'w_out_ssm': out['w_out_ssm'], 'w_out': out['w_out'], 'g_post': out['g_post'], 'loss_target': out['loss_target'], 'm_meta_tokens': out['m_meta_tokens'], 'm_g_pre': out['m_g_pre'], 'm_w_in': out['m_w_in'], 'm_conv_w': out['m_conv_w'], 'm_conv_b': out['m_conv_b'], 'm_dt_bias': out['m_dt_bias'], 'm_a_log': out['m_a_log'], 'm_d_skip': out['m_d_skip'], 'm_attn_sinks': out['m_attn_sinks'], 'm_g_ssm_norm': out['m_g_ssm_norm'], 'm_w_out_att': out['m_w_out_att'], 'm_w_out_ssm': out['m_w_out_ssm'], 'm_w_out': out['m_w_out'], 'm_g_post': out['m_g_post'], 'v_meta_tokens': out['v_meta_tokens'], 'v_g_pre': out['v_g_pre'], 'v_w_in': out['v_w_in'], 'v_conv_w': out['v_conv_w'], 'v_conv_b': out['v_conv_b'], 'v_dt_bias': out['v_dt_bias'], 'v_a_log': out['v_a_log'], 'v_d_skip': out['v_d_skip'], 'v_attn_sinks': out['v_attn_sinks'], 'v_g_ssm_norm': out['v_g_ssm_norm'], 'v_w_out_att': out['v_w_out_att'], 'v_w_out_ssm': out['v_w_out_ssm'], 'v_w_out': out['v_w_out'], 'v_g_post': out['v_g_post']}


def _loss(weights, diff, rest, loss_target):
    with _jax.named_scope("forward"):
        args = {**rest, TWIN_DIFF_INPUT: diff, **{k: w.astype(_WEIGHT_DTYPES[k]) for k, w in weights.items()}}
        y = _forward(args)
    with _jax.named_scope("loss_head"):
        err = _jnp.square(y.astype(_jnp.float32) - loss_target)
        return 0.5 * _jnp.sum(_jnp.mean(err, axis=-1)) if err.ndim else 0.5 * err


def _adamw(w, g, m, v):
    m = ADAM_B1 * m + (1.0 - ADAM_B1) * g
    v = ADAM_B2 * v + (1.0 - ADAM_B2) * _jnp.square(g)
    m_hat = m / (1.0 - ADAM_B1 ** ADAM_STEP)
    v_hat = v / (1.0 - ADAM_B2 ** ADAM_STEP)
    delta = -ADAM_LR * (m_hat / (_jnp.sqrt(v_hat) + ADAM_EPS) + ADAM_WD * w)
    return delta, m, v


def reference(x, meta_tokens, g_pre, w_in, conv_w, conv_b, dt_bias, a_log, d_skip, attn_sinks, g_ssm_norm, w_out_att, w_out_ssm, w_out, g_post, loss_target, m_meta_tokens, m_g_pre, m_w_in, m_conv_w, m_conv_b, m_dt_bias, m_a_log, m_d_skip, m_attn_sinks, m_g_ssm_norm, m_w_out_att, m_w_out_ssm, m_w_out, m_g_post, v_meta_tokens, v_g_pre, v_w_in, v_conv_w, v_conv_b, v_dt_bias, v_a_log, v_d_skip, v_attn_sinks, v_g_ssm_norm, v_w_out_att, v_w_out_ssm, v_w_out, v_g_post):
    given = dict(x=x, meta_tokens=meta_tokens, g_pre=g_pre, w_in=w_in, conv_w=conv_w, conv_b=conv_b, dt_bias=dt_bias, a_log=a_log, d_skip=d_skip, attn_sinks=attn_sinks, g_ssm_norm=g_ssm_norm, w_out_att=w_out_att, w_out_ssm=w_out_ssm, w_out=w_out, g_post=g_post, loss_target=loss_target, m_meta_tokens=m_meta_tokens, m_g_pre=m_g_pre, m_w_in=m_w_in, m_conv_w=m_conv_w, m_conv_b=m_conv_b, m_dt_bias=m_dt_bias, m_a_log=m_a_log, m_d_skip=m_d_skip, m_attn_sinks=m_attn_sinks, m_g_ssm_norm=m_g_ssm_norm, m_w_out_att=m_w_out_att, m_w_out_ssm=m_w_out_ssm, m_w_out=m_w_out, m_g_post=m_g_post, v_meta_tokens=v_meta_tokens, v_g_pre=v_g_pre, v_w_in=v_w_in, v_conv_w=v_conv_w, v_conv_b=v_conv_b, v_dt_bias=v_dt_bias, v_a_log=v_a_log, v_d_skip=v_d_skip, v_attn_sinks=v_attn_sinks, v_g_ssm_norm=v_g_ssm_norm, v_w_out_att=v_w_out_att, v_w_out_ssm=v_w_out_ssm, v_w_out=v_w_out, v_g_post=v_g_post)
    weights = {n: given[n] for n in TWIN_WEIGHTS}
    shared = {n: given[n] for n in SHARED_INPUTS}
    per_example = {n: given[n] for n in ['x']}
    grad_fn = _jax.value_and_grad(_loss, argnums=(0, 1))

    def one_microbatch(ex, loss_target):
        ex = dict(ex)
        diff = ex.pop(TWIN_DIFF_INPUT)
        return grad_fn(weights, diff, {**shared, **ex}, loss_target)

    if N_MICROBATCH == 1:
        loss, (grad_w, grad_x) = one_microbatch(per_example, given["loss_target"])
    else:
        def body(carry, xs):
            loss_sum, grad_sum = carry
            l_k, (gw_k, gx_k) = one_microbatch(xs[0], xs[1])
            with _jax.named_scope("update"):
                return (loss_sum + l_k, _jax.tree.map(_jnp.add, grad_sum, gw_k)), gx_k

        init = (_jnp.zeros((), _jnp.float32), _jax.tree.map(_jnp.zeros_like, weights))
        (loss, grad_w), grad_x = _jax.lax.scan(body, init, (per_example, given["loss_target"]))
    with _jax.named_scope("update"):
        delta_w, new_m, new_v = {}, {}, {}
        for n in TWIN_WEIGHTS:
            delta_w[n], new_m[n], new_v[n] = _adamw(weights[n], grad_w[n], given["m_" + n], given["v_" + n])
    return (loss, grad_x, *[grad_w[n] for n in TWIN_WEIGHTS], *[delta_w[n] for n in TWIN_WEIGHTS],
            *[new_m[n] for n in TWIN_WEIGHTS], *[new_v[n] for n in TWIN_WEIGHTS])
```

```python
import functools

import numpy as np
import jax
import jax.numpy as jnp
from jax import lax
from jax.experimental import pallas as pl
from jax.experimental.pallas import tpu as pltpu

F32 = jnp.float32
BF16 = jnp.bfloat16
HI = lax.Precision.HIGHEST

D_MODEL = 1024
N_META = 16
BLOCK = 128
PAD_ROWS = BLOCK - N_META
NORM_EPS = 1e-6
HEAD_DIM = 64
ATT_Q_HEADS = 16
ATT_KV_HEADS = 4
ATT_GROUP = 4
SSM_INNER = 2048
SSM_HEADS = 32
SSM_GROUPS = 4
SSM_HEADS_PER_GROUP = 8
SSM_STATE = 128
CONV_WIDTH = 4
CONV_DIM = 3072
LANES = 128

ADAM_LR = 0.001
ADAM_B1 = 0.9
ADAM_B2 = 0.999
ADAM_EPS = 1e-08
ADAM_WD = 0.01
ADAM_STEP = 10

VMEM_LIMIT = 48 * 1024 * 1024

SHARD_W = 2440
PACK_W = 2560
SHARD_STRIDE = 2432
N_ALIGNED = 9856
SEG = {
    "q": (0, 1024, 5120), "k": (1024, 256, 9216), "v": (1280, 256, 9472), "z_att": (1536, 1024, 6144),
    "z_ssm": (2560, 2048, 0), "xbc": (4608, 3072, 2048), "dt": (7680, 128, 9728),
    "gate_att": (7808, 1024, 7168), "gate_ssm": (8832, 1024, 8192),
}
DT_STORED_START = 7680
DT_PAD = LANES - SSM_HEADS


def _act_col(aligned_col):
    for a0, w, p0 in SEG.values():
        if a0 <= aligned_col < a0 + w:
            return p0 + aligned_col - a0
    raise ValueError(aligned_col)


def _call(body, *, name, out_shape, in_specs, out_specs, grid=(), scratch=(), sem=None, aliases=None):
    return pl.pallas_call(
        body, out_shape=out_shape, grid=grid, in_specs=in_specs, out_specs=out_specs, scratch_shapes=list(scratch),
        name=name, input_output_aliases=aliases or {},
        compiler_params=pltpu.CompilerParams(dimension_semantics=sem, vmem_limit_bytes=VMEM_LIMIT))


def _full(shape):
    n = len(shape)
    return pl.BlockSpec(shape, lambda *_: (0,) * n)


def _chip_index():
    return lax.axis_index("x") * 2 + lax.axis_index("y")


def _silu(z):
    return z * jax.nn.sigmoid(z)


def _rms(x, g):
    return x * lax.rsqrt(jnp.mean(x * x, axis=-1, keepdims=True) + NORM_EPS) * g


def _peer(mask):
    x, y, c = lax.axis_index("x"), lax.axis_index("y"), lax.axis_index("c")
    return ((1 - x) if mask & 4 else x, (1 - y) if mask & 2 else y, (1 - c) if mask & 1 else c)


def exchange(name, arrays, by_target, masks, slot_kind, nslots):
    n, nk = len(arrays), len(masks)
    piece_shapes = [a.shape[2:] if bt else a.shape for a, bt in zip(arrays, by_target)]

    def body(*refs):
        ins, outs = refs[:n], refs[n:2 * n]
        send_sems, recv_sems, local_sems = refs[2 * n:]
        x, y, c = lax.axis_index("x"), lax.axis_index("y"), lax.axis_index("c")
        slot = {"chip": 2 * x + y, "dev": 4 * x + 2 * y + c, "core": c}[slot_kind]

        def piece(a, dev):
            return ins[a].at[2 * dev[0] + dev[1], dev[2]] if by_target[a] else ins[a]

        local = [pltpu.make_async_copy(piece(a, (x, y, c)), outs[a].at[slot], local_sems.at[a]) for a in range(n)]
        for cp in local:
            cp.start()
        remote = []
        for a in range(n):
            for ki, mask in enumerate(masks):
                dev = _peer(mask)
                remote.append(pltpu.make_async_remote_copy(
                    src_ref=piece(a, dev), dst_ref=outs[a].at[slot], send_sem=send_sems.at[a * nk + ki],
                    recv_sem=recv_sems.at[a * nk + ki], device_id=dev, device_id_type=pl.DeviceIdType.MESH))
        for cp in remote:
            cp.start()
        for cp in remote:
            cp.wait()
        for cp in local:
            cp.wait()

    anyspec = pl.BlockSpec(memory_space=pl.ANY)
    return pl.pallas_call(
        body, name=name,
        out_shape=[jax.ShapeDtypeStruct((nslots,) + tuple(s), a.dtype) for s, a in zip(piece_shapes, arrays)],
        in_specs=[anyspec] * n, out_specs=[anyspec] * n,
        scratch_shapes=[pltpu.SemaphoreType.DMA((n * nk,)), pltpu.SemaphoreType.DMA((n * nk,)),
                        pltpu.SemaphoreType.DMA((n,))],
    )(*arrays)


def _shift_right(p, chip, col):
    if chip == 0:
        return p
    if chip < 3:
        return pltpu.roll(p, 8 * chip, 1)
    tail0 = DT_STORED_START + SSM_HEADS - 3 * SHARD_W
    head = pltpu.roll(p, 24, 1)
    tail = pltpu.roll(p, 24 + DT_PAD, 1)
    return jnp.where(col < tail0 + 24, head, jnp.where(col >= tail0 + 24 + DT_PAD, tail, 0.0))


def _shift_left(g, chip, col):
    if chip == 0:
        return g
    if chip < 3:
        return pltpu.roll(g, PACK_W - 8 * chip, 1)
    tail0 = DT_STORED_START + SSM_HEADS - 3 * SHARD_W
    return jnp.where(col < tail0, pltpu.roll(g, PACK_W - 24, 1), pltpu.roll(g, PACK_W - 24 - DT_PAD, 1))


def pack_w_in(w):
    rows = BLOCK

    def body(w_ref, o_ref, pad_ref):
        chip = _chip_index()
        pad_ref[...] = jnp.zeros_like(pad_ref)
        pad_ref[:, 0:SHARD_W] = w_ref[...]
        p = pad_ref[...]
        col = lax.broadcasted_iota(jnp.int32, p.shape, 1)
        for cv in range(4):
            @pl.when(chip == cv)
            def _():
                o_ref[...] = _shift_right(p, cv, col).astype(BF16)

    return _call(body, name="pack_w_in", grid=(D_MODEL // rows,),
                 in_specs=[pl.BlockSpec((rows, SHARD_W), lambda i: (i, 0))],
                 out_specs=pl.BlockSpec((rows, PACK_W), lambda i: (i, 0)),
                 out_shape=jax.ShapeDtypeStruct((D_MODEL, PACK_W), BF16),
                 scratch=[pltpu.VMEM((rows, PACK_W), F32)], sem=("parallel",))(w)


def _tile_runs():
    runs, fix = [], []
    for t in range(N_ALIGNED // LANES):
        s = min(t // 19, 3)
        j = t - 19 * s
        p = _act_col(t * LANES)
        if runs and runs[-1][1] == s and runs[-1][0] + runs[-1][3] == p and runs[-1][2] + runs[-1][3] == j * LANES:
            runs[-1][3] += LANES
        else:
            runs.append([p, s, j * LANES, LANES])
        if j == 0 and s > 0:
            fix.append((p, s - 1))
    return runs, fix


def unpack_w_in(bg):
    rows = BLOCK
    runs, fix = _tile_runs()

    def body(b_ref, o_ref):
        for p, s, j, w in runs:
            o_ref[:, p:p + w] = b_ref[s, :, j:j + w]
        for p, s in fix:
            o_ref[:, p:p + LANES] = o_ref[:, p:p + LANES] + b_ref[s, :, SHARD_STRIDE:PACK_W]

    return _call(body, name="unpack_w_in", grid=(D_MODEL // rows,),
                 in_specs=[pl.BlockSpec((4, rows, PACK_W), lambda i: (0, i, 0))],
                 out_specs=pl.BlockSpec((rows, N_ALIGNED), lambda i: (i, 0)),
                 out_shape=jax.ShapeDtypeStruct((D_MODEL, N_ALIGNED), BF16), sem=("parallel",))(bg)


def pack_grad_w_in(dw):
    rows = BLOCK

    def body(g_ref, o_ref):
        for s in range(4):
            for j in range(PACK_W // LANES):
                p = _act_col((19 * s + j) * LANES)
                o_ref[s, :, j * LANES:(j + 1) * LANES] = g_ref[:, p:p + LANES].astype(BF16)

    return _call(body, name="pack_grad_w_in", grid=(D_MODEL // rows,),
                 in_specs=[pl.BlockSpec((rows, N_ALIGNED), lambda i: (i, 0))],
                 out_specs=pl.BlockSpec((4, rows, PACK_W), lambda i: (0, i, 0)),
                 out_shape=jax.ShapeDtypeStruct((4, D_MODEL, PACK_W), BF16), sem=("parallel",))(dw)


def _adamw(w, g, m, v):
    m = ADAM_B1 * m + (1.0 - ADAM_B1) * g
    v = ADAM_B2 * v + (1.0 - ADAM_B2) * jnp.square(g)
    m_hat = m / (1.0 - ADAM_B1 ** ADAM_STEP)
    v_hat = v / (1.0 - ADAM_B2 ** ADAM_STEP)
    delta = -ADAM_LR * (m_hat / (jnp.sqrt(v_hat) + ADAM_EPS) + ADAM_WD * w)
    return delta, m, v


def adamw_w_in(g_packed, w, m, v):
    rows = BLOCK

    def body(g_ref, w_ref, m_ref, v_ref, go_ref, d_ref, mo_ref, vo_ref, tmp_ref):
        chip = _chip_index()
        gp = g_ref[...]
        col = lax.broadcasted_iota(jnp.int32, gp.shape, 1)
        for cv in range(4):
            @pl.when(chip == cv)
            def _():
                tmp_ref[...] = _shift_left(gp, cv, col)
        g = tmp_ref[:, 0:SHARD_W]
        d, mn, vn = _adamw(w_ref[...], g, m_ref[...], v_ref[...])
        go_ref[...] = g
        d_ref[...] = d
        mo_ref[...] = mn
        vo_ref[...] = vn

    spec = pl.BlockSpec((rows, SHARD_W), lambda i: (i, 0))
    shp = jax.ShapeDtypeStruct((D_MODEL, SHARD_W), F32)
    return _call(body, name="adamw_w_in", grid=(D_MODEL // rows,),
                 in_specs=[pl.BlockSpec((rows, PACK_W), lambda i: (i, 0)), spec, spec, spec],
                 out_specs=[spec] * 4, out_shape=[shp] * 4,
                 scratch=[pltpu.VMEM((rows, PACK_W), F32)], sem=("parallel",))(g_packed, w, m, v)


def adamw_rows(name, g, w, m, v):
    r, c = g.shape
    rows = min(r, BLOCK)

    def body(g_ref, w_ref, m_ref, v_ref, d_ref, mo_ref, vo_ref):
        d_ref[...], mo_ref[...], vo_ref[...] = _adamw(w_ref[...], g_ref[...], m_ref[...], v_ref[...])

    spec = pl.BlockSpec((rows, c), lambda i: (i, 0))
    shp = jax.ShapeDtypeStruct((r, c), F32)
    return _call(body, name=name, grid=(r // rows,), in_specs=[spec] * 4, out_specs=[spec] * 3, out_shape=[shp] * 3,
                 sem=("parallel",))(g, w, m, v)


def adamw_small(gs, ws, ms, vs):
    n = len(gs)

    def body(*refs):
        g, w, m, v = refs[:n], refs[n:2 * n], refs[2 * n:3 * n], refs[3 * n:4 * n]
        outs = refs[4 * n:]
        for i in range(n):
            d, mn, vn = _adamw(w[i][...], g[i][...], m[i][...], v[i][...])
            outs[3 * i][...] = d
            outs[3 * i + 1][...] = mn
            outs[3 * i + 2][...] = vn

    specs = [_full(a.shape) for a in gs]
    res = _call(body, name="adamw_small", in_specs=specs * 4,
                out_specs=[s for s in specs for _ in range(3)],
                out_shape=[jax.ShapeDtypeStruct(a.shape, F32) for a in gs for _ in range(3)])(*gs, *ws, *ms, *vs)
    return [tuple(res[3 * i:3 * i + 3]) for i in range(n)]


def sum_slots(name, r):
    s, rr, c = r.shape
    rows = min(rr, BLOCK)

    def body(r_ref, o_ref):
        acc = r_ref[0].astype(F32)
        for k in range(1, s):
            acc = acc + r_ref[k].astype(F32)
        o_ref[...] = acc

    return _call(body, name=name, grid=(rr // rows,), in_specs=[pl.BlockSpec((s, rows, c), lambda i: (0, i, 0))],
                 out_specs=pl.BlockSpec((rows, c), lambda i: (i, 0)), out_shape=jax.ShapeDtypeStruct((rr, c), F32),
                 sem=("parallel",))(r)


def _col_tile(n, k):
    if n % 896 == 0 and k <= 1024:
        return 896
    return min(n, 512)


def mm_nn(name, x, w, out_dtype=F32):
    m, k = x.shape
    n = w.shape[1]
    tn = _col_tile(n, k)

    def body(x_ref, w_ref, o_ref):
        o_ref[...] = jnp.dot(x_ref[...].astype(BF16), w_ref[...].astype(BF16),
                             preferred_element_type=F32).astype(out_dtype)

    return _call(body, name=name, grid=(n // tn,), in_specs=[_full((m, k)), pl.BlockSpec((k, tn), lambda j: (0, j))],
                 out_specs=pl.BlockSpec((m, tn), lambda j: (0, j)), out_shape=jax.ShapeDtypeStruct((m, n), out_dtype),
                 sem=("parallel",))(x, w)


def mm_nt(name, dy, w, out_dtype=F32):
    m, n = dy.shape
    k = w.shape[0]
    tm = m // 2
    tn = _col_tile(n, k)
    steps = n // tn

    def body(dy_ref, w_ref, o_ref, acc_ref):
        j = pl.program_id(1)

        @pl.when(j == 0)
        def _():
            acc_ref[...] = jnp.zeros_like(acc_ref)

        acc_ref[...] += lax.dot_general(dy_ref[...].astype(BF16), w_ref[...].astype(BF16), (((1,), (1,)), ((), ())),
                                        preferred_element_type=F32)

        @pl.when(j == steps - 1)
        def _():
            o_ref[...] = acc_ref[...].astype(out_dtype)

    return _call(body, name=name, grid=(m // tm, steps),
                 in_specs=[pl.BlockSpec((tm, tn), lambda i, j: (i, j)), pl.BlockSpec((k, tn), lambda i, j: (0, j))],
                 out_specs=pl.BlockSpec((tm, k), lambda i, j: (i, 0)), out_shape=jax.ShapeDtypeStruct((m, k), out_dtype),
                 scratch=[pltpu.VMEM((tm, k), F32)], sem=("parallel", "arbitrary"))(dy, w)


def mm_tn(name, x, dy):
    m, k = x.shape
    n = dy.shape[1]
    tm = m // 2
    tn = _col_tile(n, k)

    def body(x_ref, dy_ref, o_ref):
        @pl.when(pl.program_id(1) == 0)
        def _():
            o_ref[...] = jnp.zeros_like(o_ref)

        o_ref[...] += lax.dot_general(x_ref[...].astype(BF16), dy_ref[...].astype(BF16), (((0,), (0,)), ((), ())),
                                      preferred_element_type=F32)

    return _call(body, name=name, grid=(n // tn, m // tm),
                 in_specs=[pl.BlockSpec((tm, k), lambda i, j: (j, 0)), pl.BlockSpec((tm, tn), lambda i, j: (j, i))],
                 out_specs=pl.BlockSpec((k, tn), lambda i, j: (0, i)), out_shape=jax.ShapeDtypeStruct((k, n), F32),
                 sem=("parallel", "arbitrary"))(x, dy)


def _row_spec(width, col_block=0):
    return pl.BlockSpec((BLOCK, width), lambda i: (i, col_block))


def _x_spec():
    return pl.BlockSpec((None, BLOCK, D_MODEL), lambda i: (0, jnp.maximum(i - 1, 0), 0))


def prep(x, meta, g_pre):
    nb = x.shape[1] // BLOCK + 1

    def body(x_ref, meta_ref, g_ref, h_ref, u_ref):
        i = pl.program_id(0)

        @pl.when(i == 0)
        def _():
            h_ref[0:PAD_ROWS, :] = jnp.zeros((PAD_ROWS, D_MODEL), F32)
            h_ref[PAD_ROWS:BLOCK, :] = meta_ref[...]

        @pl.when(i > 0)
        def _():
            h_ref[...] = x_ref[...]

        u_ref[...] = _rms(h_ref[...], g_ref[...]).astype(BF16)

    return _call(body, name="prep", grid=(nb,), in_specs=[_x_spec(), _full((N_META, D_MODEL)), _full((1, D_MODEL))],
                 out_specs=[_row_spec(D_MODEL), _row_spec(D_MODEL)],
                 out_shape=[jax.ShapeDtypeStruct((nb * BLOCK, D_MODEL), F32),
                            jax.ShapeDtypeStruct((nb * BLOCK, D_MODEL), BF16)], sem=("parallel",))(x, meta, g_pre)


def prep_bwd(h, du, dres, g_pre):
    nb = h.shape[0] // BLOCK

    def body(h_ref, du_ref, dres_ref, g_ref, gx_ref, gm_ref, gg_ref):
        i = pl.program_id(0)
        _, vjp = jax.vjp(_rms, h_ref[...], g_ref[...])
        dh, dg = vjp(du_ref[...])

        @pl.when(i == 0)
        def _():
            gm_ref[...] = dh[PAD_ROWS:BLOCK, :]
            gg_ref[...] = dg

        @pl.when(i > 0)
        def _():
            gg_ref[...] += dg

        gx_ref[...] = dh + dres_ref[...]

    return _call(body, name="prep_bwd", grid=(nb,),
                 in_specs=[_row_spec(D_MODEL), _row_spec(D_MODEL), _row_spec(D_MODEL), _full((1, D_MODEL))],
                 out_specs=[_x_spec(), _full((N_META, D_MODEL)), _full((1, D_MODEL))],
                 out_shape=[jax.ShapeDtypeStruct((1, (nb - 1) * BLOCK, D_MODEL), F32),
                            jax.ShapeDtypeStruct((N_META, D_MODEL), F32), jax.ShapeDtypeStruct((1, D_MODEL), F32)],
                 sem=("arbitrary",))(h, du, dres, g_pre)


def gate_fwd(name, o, proj, z_block):
    nb = o.shape[0] // BLOCK

    def body(o_ref, z_ref, a_ref):
        a_ref[...] = (o_ref[...] * _silu(z_ref[...])).astype(BF16)

    return _call(body, name=name, grid=(nb,), in_specs=[_row_spec(D_MODEL), _row_spec(D_MODEL, z_block)],
                 out_specs=_row_spec(D_MODEL), out_shape=jax.ShapeDtypeStruct(o.shape, BF16), sem=("parallel",))(o, proj)


def gate_bwd(name, da, o, proj, z_block):
    nb = o.shape[0] // BLOCK

    def body(da_ref, o_ref, z_ref, do_ref, dz_ref):
        _, vjp = jax.vjp(lambda o, z: o * _silu(z), o_ref[...], z_ref[...])
        do, dz = vjp(da_ref[...])
        do_ref[...] = do
        dz_ref[...] = dz.astype(BF16)

    return _call(body, name=name, grid=(nb,),
                 in_specs=[_row_spec(D_MODEL), _row_spec(D_MODEL), _row_spec(D_MODEL, z_block)],
                 out_specs=[_row_spec(D_MODEL), _row_spec(D_MODEL)],
                 out_shape=[jax.ShapeDtypeStruct(o.shape, F32), jax.ShapeDtypeStruct(o.shape, BF16)],
                 sem=("parallel",))(da, o, proj)


GROUP_W = SSM_INNER // SSM_GROUPS


def _gated_norm(y, z, g):
    t = y * _silu(z)
    return t * lax.rsqrt(jnp.mean(t * t, axis=-1, keepdims=True) + NORM_EPS) * g


def ssm_norm_fwd(y, proj, g_norm):
    nb = y.shape[0] // BLOCK
    spec = pl.BlockSpec((BLOCK, GROUP_W), lambda i, g: (i, g))

    def body(y_ref, z_ref, g_ref, o_ref):
        o_ref[...] = _gated_norm(y_ref[...], z_ref[...], g_ref[...]).astype(BF16)

    return _call(body, name="ssm_norm_fwd", grid=(nb, SSM_GROUPS),
                 in_specs=[spec, spec, pl.BlockSpec((1, GROUP_W), lambda i, g: (0, g))], out_specs=spec,
                 out_shape=jax.ShapeDtypeStruct(y.shape, BF16), sem=("parallel", "parallel"))(y, proj, g_norm)


def ssm_norm_bwd(dyn, y, proj, g_norm):
    nb = y.shape[0] // BLOCK
    spec = pl.BlockSpec((BLOCK, GROUP_W), lambda g, i: (i, g))
    gspec = pl.BlockSpec((1, GROUP_W), lambda g, i: (0, g))

    def body(d_ref, y_ref, z_ref, g_ref, dy_ref, dz_ref, dg_ref):
        _, vjp = jax.vjp(_gated_norm, y_ref[...], z_ref[...], g_ref[...])
        dy, dz, dg = vjp(d_ref[...])
        dy_ref[...] = dy
        dz_ref[...] = dz.astype(BF16)

        @pl.when(pl.program_id(1) == 0)
        def _():
            dg_ref[...] = dg

        @pl.when(pl.program_id(1) > 0)
        def _():
            dg_ref[...] += dg

    return _call(body, name="ssm_norm_bwd", grid=(SSM_GROUPS, nb), in_specs=[spec, spec, spec, gspec],
                 out_specs=[spec, spec, gspec],
                 out_shape=[jax.ShapeDtypeStruct(y.shape, F32), jax.ShapeDtypeStruct(y.shape, BF16),
                            jax.ShapeDtypeStruct((1, SSM_INNER), F32)],
                 sem=("parallel", "arbitrary"))(dyn, y, proj, g_norm)


def _merge(ga, gs, ya, ys):
    return jax.nn.sigmoid(ga) * ya + jax.nn.sigmoid(gs) * ys


GATE_ATT_BLOCK = SEG["gate_att"][2] // D_MODEL
GATE_SSM_BLOCK = SEG["gate_ssm"][2] // D_MODEL


def merge_fwd(proj, y_att, y_ssm):
    nb = y_att.shape[0] // BLOCK

    def body(ga_ref, gs_ref, ya_ref, ys_ref, o_ref):
        o_ref[...] = _merge(ga_ref[...], gs_ref[...], ya_ref[...], ys_ref[...]).astype(BF16)

    return _call(body, name="merge_fwd", grid=(nb,),
                 in_specs=[_row_spec(D_MODEL, GATE_ATT_BLOCK), _row_spec(D_MODEL, GATE_SSM_BLOCK), _row_spec(D_MODEL),
                           _row_spec(D_MODEL)],
                 out_specs=_row_spec(D_MODEL), out_shape=jax.ShapeDtypeStruct(y_att.shape, BF16),
                 sem=("parallel",))(proj, proj, y_att, y_ssm)


def merge_bwd(dm, proj, y_att, y_ssm):
    nb = y_att.shape[0] // BLOCK

    def body(d_ref, ga_ref, gs_ref, ya_ref, ys_ref, dga_ref, dgs_ref, dya_ref, dys_ref):
        _, vjp = jax.vjp(_merge, ga_ref[...], gs_ref[...], ya_ref[...], ys_ref[...])
        dga, dgs, dya, dys = vjp(d_ref[...])
        dga_ref[...] = dga.astype(BF16)
        dgs_ref[...] = dgs.astype(BF16)
        dya_ref[...] = dya.astype(BF16)
        dys_ref[...] = dys.astype(BF16)

    return _call(body, name="merge_bwd", grid=(nb,),
                 in_specs=[_row_spec(D_MODEL), _row_spec(D_MODEL, GATE_ATT_BLOCK), _row_spec(D_MODEL, GATE_SSM_BLOCK),
                           _row_spec(D_MODEL), _row_spec(D_MODEL)],
                 out_specs=[_row_spec(D_MODEL)] * 4, out_shape=[jax.ShapeDtypeStruct(y_att.shape, BF16)] * 4,
                 sem=("parallel",))(dm, proj, proj, y_att, y_ssm)


def final_loss(out, x, target, g_post):
    nb = out.shape[0] // BLOCK

    def body(out_ref, x_ref, t_ref, g_ref, do_ref, dres_ref, loss_ref, dg_ref):
        i = pl.program_id(0)

        @pl.when(i == 0)
        def _():
            do_ref[...] = jnp.zeros_like(do_ref)
            dres_ref[...] = jnp.zeros_like(dres_ref)
            loss_ref[...] = jnp.zeros_like(loss_ref)
            dg_ref[...] = jnp.zeros_like(dg_ref)

        @pl.when(i > 0)
        def _():
            n, vjp = jax.vjp(_rms, out_ref[...], g_ref[...])
            diff = x_ref[...] + n - t_ref[...]
            loss_ref[...] += 0.5 * jnp.sum(diff * diff) / D_MODEL
            dn = diff * (1.0 / D_MODEL)
            do, dg = vjp(dn)
            do_ref[...] = do.astype(BF16)
            dres_ref[...] = dn
            dg_ref[...] += dg

    return _call(body, name="final_loss", grid=(nb,),
                 in_specs=[_row_spec(D_MODEL), _x_spec(), _x_spec(), _full((1, D_MODEL))],
                 out_specs=[_row_spec(D_MODEL), _row_spec(D_MODEL), _full((8, LANES)), _full((1, D_MODEL))],
                 out_shape=[jax.ShapeDtypeStruct(out.shape, BF16), jax.ShapeDtypeStruct(out.shape, F32),
                            jax.ShapeDtypeStruct((8, LANES), F32), jax.ShapeDtypeStruct((1, D_MODEL), F32)],
                 sem=("arbitrary",))(out, x, target, g_post)


_NT = (((1,), (1,)), ((), ()))


def _attn_block(q4, kp, kc, vp, vc, km, vm, sk4, sl4, n):
    i = lax.broadcasted_iota(jnp.int32, (BLOCK, BLOCK), 0)
    j = lax.broadcasted_iota(jnp.int32, (BLOCK, BLOCK), 1)
    rel_c = (i - j).astype(F32)
    rel_p = rel_c + float(BLOCK)
    nv = jnp.zeros((BLOCK, BLOCK), jnp.int32) + n
    ok_c = (i >= j) & (nv >= 1)
    ok_p = (j > i) & (nv >= 2)
    im = lax.broadcasted_iota(jnp.int32, (BLOCK, N_META), 0)
    jm = lax.broadcasted_iota(jnp.int32, (BLOCK, N_META), 1)
    ok_m = ((jnp.zeros((BLOCK, N_META), jnp.int32) + n) >= 1) | (im >= PAD_ROWS + jm)
    kcb, kpb, kmb = kc.astype(BF16), kp.astype(BF16), km.astype(BF16)
    vcb, vpb, vmb = vc.astype(BF16), vp.astype(BF16), vm.astype(BF16)
    neg = -jnp.inf
    outs = []
    for g in range(ATT_GROUP):
        qg = (q4[g] * HEAD_DIM ** -0.5).astype(BF16)
        sc = lax.dot_general(qg, kcb, _NT, preferred_element_type=F32) - sl4[g] * rel_c
        sp = lax.dot_general(qg, kpb, _NT, preferred_element_type=F32) - sl4[g] * rel_p
        sm = lax.dot_general(qg, kmb, _NT, preferred_element_type=F32)
        sc = jnp.where(ok_c, sc, neg)
        sp = jnp.where(ok_p, sp, neg)
        sm = jnp.where(ok_m, sm, neg)
        mx = jnp.maximum(jnp.maximum(jnp.max(sc, axis=1, keepdims=True), jnp.max(sp, axis=1, keepdims=True)),
                         jnp.maximum(jnp.max(sm, axis=1, keepdims=True), sk4[g]))
        mx = lax.stop_gradient(mx)
        ec, ep, em, es = jnp.exp(sc - mx), jnp.exp(sp - mx), jnp.exp(sm - mx), jnp.exp(sk4[g] - mx)
        den = (es + jnp.sum(ec, axis=1, keepdims=True) + jnp.sum(ep, axis=1, keepdims=True)
               + jnp.sum(em, axis=1, keepdims=True))
        inv = 1.0 / den
        outs.append(jnp.dot((ec * inv).astype(BF16), vcb, preferred_element_type=F32)
                    + jnp.dot((ep * inv).astype(BF16), vpb, preferred_element_type=F32)
                    + jnp.dot((em * inv).astype(BF16), vmb, preferred_element_type=F32))
    return tuple(outs)


def attn_fwd(qh, kh, vh, kmeta, vmeta, sinks, slopes):
    nb = qh.shape[1] // BLOCK
    qspec = pl.BlockSpec((ATT_GROUP, BLOCK, HEAD_DIM), lambda k, n: (k, n, 0))
    cur = pl.BlockSpec((None, BLOCK, HEAD_DIM), lambda k, n: (k, n, 0))
    prev = pl.BlockSpec((None, BLOCK, HEAD_DIM), lambda k, n: (k, jnp.maximum(n - 1, 0), 0))
    meta = pl.BlockSpec((None, N_META, HEAD_DIM), lambda k, n: (k, 0, 0))
    par = pl.BlockSpec((ATT_GROUP, 1, 1), lambda k, n: (k, 0, 0))

    def body(q_ref, kp_ref, kc_ref, vp_ref, vc_ref, km_ref, vm_ref, sk_ref, sl_ref, o_ref):
        n = pl.program_id(1)
        outs = _attn_block(tuple(q_ref[g] for g in range(ATT_GROUP)), kp_ref[...], kc_ref[...], vp_ref[...],
                           vc_ref[...], km_ref[...], vm_ref[...], tuple(sk_ref[g] for g in range(ATT_GROUP)),
                           tuple(sl_ref[g] for g in range(ATT_GROUP)), n)
        for g in range(ATT_GROUP):
            o_ref[g] = outs[g]

    return _call(body, name="attn_fwd", grid=(ATT_KV_HEADS, nb),
                 in_specs=[qspec, prev, cur, prev, cur, meta, meta, par, par], out_specs=qspec,
                 out_shape=jax.ShapeDtypeStruct(qh.shape, F32), sem=("parallel", "parallel"))(
                     qh, kh, kh, vh, vh, kmeta, vmeta, sinks, slopes)


def attn_bwd(do, qh, kh, vh, kmeta, vmeta, sinks, slopes):
    nb = qh.shape[1] // BLOCK
    last = nb - 1
    qspec = pl.BlockSpec((ATT_GROUP, BLOCK, HEAD_DIM), lambda k, t: (k, jnp.minimum(t, last), 0))
    cur = pl.BlockSpec((None, BLOCK, HEAD_DIM), lambda k, t: (k, jnp.minimum(t, last), 0))
    prev = pl.BlockSpec((None, BLOCK, HEAD_DIM), lambda k, t: (k, jnp.maximum(jnp.minimum(t, last) - 1, 0), 0))
    done = pl.BlockSpec((None, BLOCK, HEAD_DIM), lambda k, t: (k, jnp.maximum(t - 1, 0), 0))
    meta = pl.BlockSpec((None, N_META, HEAD_DIM), lambda k, t: (k, 0, 0))
    par = pl.BlockSpec((ATT_GROUP, 1, 1), lambda k, t: (k, 0, 0))

    def body(do_ref, q_ref, kp_ref, kc_ref, vp_ref, vc_ref, km_ref, vm_ref, sk_ref, sl_ref,
             dq_ref, dk_ref, dv_ref, dkm_ref, dvm_ref, dsk_ref, ck_ref, cv_ref):
        t = pl.program_id(1)

        @pl.when(t == 0)
        def _():
            ck_ref[...] = jnp.zeros_like(ck_ref)
            cv_ref[...] = jnp.zeros_like(cv_ref)
            dkm_ref[...] = jnp.zeros_like(dkm_ref)
            dvm_ref[...] = jnp.zeros_like(dvm_ref)
            dsk_ref[...] = jnp.zeros_like(dsk_ref)

        @pl.when(t < nb)
        def _():
            sl4 = tuple(sl_ref[g] for g in range(ATT_GROUP))

            def f(q4, kp, kc, vp, vc, km, vm, sk4):
                return _attn_block(q4, kp, kc, vp, vc, km, vm, sk4, sl4, t)

            _, vjp = jax.vjp(f, tuple(q_ref[g] for g in range(ATT_GROUP)), kp_ref[...], kc_ref[...], vp_ref[...],
                             vc_ref[...], km_ref[...], vm_ref[...], tuple(sk_ref[g] for g in range(ATT_GROUP)))
            dq4, dkp, dkc, dvp, dvc, dkm, dvm, dsk4 = vjp(tuple(do_ref[g] for g in range(ATT_GROUP)))
            for g in range(ATT_GROUP):
                dq_ref[g] = dq4[g]
                dsk_ref[g] += dsk4[g]
            dk_ref[...] = ck_ref[...] + dkp
            dv_ref[...] = cv_ref[...] + dvp
            ck_ref[...] = dkc
            cv_ref[...] = dvc
            dkm_ref[...] += dkm
            dvm_ref[...] += dvm

        @pl.when(t == nb)
        def _():
            dk_ref[...] = ck_ref[...]
            dv_ref[...] = cv_ref[...]

    return _call(body, name="attn_bwd", grid=(ATT_KV_HEADS, nb + 1),
                 in_specs=[qspec, qspec, prev, cur, prev, cur, meta, meta, par, par],
                 out_specs=[qspec, done, done, meta, meta, par],
                 out_shape=[jax.ShapeDtypeStruct(qh.shape, F32), jax.ShapeDtypeStruct(kh.shape, F32),
                            jax.ShapeDtypeStruct(vh.shape, F32), jax.ShapeDtypeStruct(kmeta.shape, F32),
                            jax.ShapeDtypeStruct(vmeta.shape, F32), jax.ShapeDtypeStruct(sinks.shape, F32)],
                 scratch=[pltpu.VMEM((BLOCK, HEAD_DIM), F32), pltpu.VMEM((BLOCK, HEAD_DIM), F32)],
                 sem=("parallel", "arbitrary"))(do, qh, kh, kh, vh, vh, kmeta, vmeta, sinks, slopes)


XBC_BLOCK0 = SEG["xbc"][2] // D_MODEL
CONV_COL_BLOCKS = CONV_DIM // D_MODEL
DT_TILE = SEG["dt"][2] // LANES


def _shift_rows(cur, prev, j, row):
    if j == 0:
        return cur
    return jnp.where(row >= j, pltpu.roll(cur, j, 0), pltpu.roll(prev, j, 0))


def _conv_pre(cur, prev, w_ref, b_ref, row):
    pre = b_ref[...] + w_ref[CONV_WIDTH - 1:CONV_WIDTH, :] * cur
    for k in range(CONV_WIDTH - 1):
        pre = pre + w_ref[k:k + 1, :] * _shift_rows(cur, prev, CONV_WIDTH - 1 - k, row)
    return pre


def conv_fwd(proj, conv_w, conv_b):
    nb = proj.shape[0] // BLOCK
    cur = pl.BlockSpec((BLOCK, D_MODEL), lambda j, i: (i, XBC_BLOCK0 + j))
    prev = pl.BlockSpec((BLOCK, D_MODEL), lambda j, i: (jnp.maximum(i - 1, 0), XBC_BLOCK0 + j))

    def body(c_ref, p_ref, w_ref, b_ref, o_ref):
        i = pl.program_id(1)
        row = lax.broadcasted_iota(jnp.int32, (BLOCK, D_MODEL), 0)
        prevv = p_ref[...] * jnp.where(i > 0, 1.0, 0.0)
        pre = _conv_pre(c_ref[...], prevv, w_ref, b_ref, row)
        valid = jnp.maximum((row >= PAD_ROWS).astype(F32), jnp.where(i > 0, 1.0, 0.0))
        o_ref[...] = _silu(pre) * valid

    return _call(body, name="conv_fwd", grid=(CONV_COL_BLOCKS, nb),
                 in_specs=[cur, prev, pl.BlockSpec((CONV_WIDTH, D_MODEL), lambda j, i: (0, j)),
                           pl.BlockSpec((1, D_MODEL), lambda j, i: (0, j))],
                 out_specs=pl.BlockSpec((BLOCK, D_MODEL), lambda j, i: (i, j)),
                 out_shape=jax.ShapeDtypeStruct((nb * BLOCK, CONV_DIM), F32),
                 sem=("parallel", "parallel"))(proj, proj, conv_w, conv_b)


def conv_bwd(dxbc, proj, conv_w, conv_b):
    nb = proj.shape[0] // BLOCK
    last = nb - 1
    cur = pl.BlockSpec((BLOCK, D_MODEL), lambda j, i: (i, XBC_BLOCK0 + j))
    prev = pl.BlockSpec((BLOCK, D_MODEL), lambda j, i: (jnp.maximum(i - 1, 0), XBC_BLOCK0 + j))
    nxt = pl.BlockSpec((BLOCK, D_MODEL), lambda j, i: (jnp.minimum(i + 1, last), XBC_BLOCK0 + j))
    dcur = pl.BlockSpec((BLOCK, D_MODEL), lambda j, i: (i, j))
    dnxt = pl.BlockSpec((BLOCK, D_MODEL), lambda j, i: (jnp.minimum(i + 1, last), j))
    wspec = pl.BlockSpec((CONV_WIDTH, D_MODEL), lambda j, i: (0, j))
    bspec = pl.BlockSpec((1, D_MODEL), lambda j, i: (0, j))

    def body(dc_ref, dn_ref, c_ref, p_ref, n_ref, w_ref, b_ref, du_ref, dw_ref, db_ref):
        i = pl.program_id(1)
        row = lax.broadcasted_iota(jnp.int32, (BLOCK, D_MODEL), 0)
        curv = c_ref[...]
        prevv = p_ref[...] * jnp.where(i > 0, 1.0, 0.0)

        def dpre_of(pre, d, valid):
            s = jax.nn.sigmoid(pre)
            return d * valid * (s * (1.0 + pre * (1.0 - s)))

        valid = jnp.maximum((row >= PAD_ROWS).astype(F32), jnp.where(i > 0, 1.0, 0.0))
        dp_c = dpre_of(_conv_pre(curv, prevv, w_ref, b_ref, row), dc_ref[...], valid)
        has_next = jnp.where(i < last, 1.0, 0.0)
        dp_n = dpre_of(_conv_pre(n_ref[...], curv, w_ref, b_ref, row), dn_ref[...], has_next)
        du = w_ref[CONV_WIDTH - 1:CONV_WIDTH, :] * dp_c
        for j in range(1, CONV_WIDTH):
            up = jnp.where(row < BLOCK - j, pltpu.roll(dp_c, BLOCK - j, 0), pltpu.roll(dp_n, BLOCK - j, 0))
            du = du + w_ref[CONV_WIDTH - 1 - j:CONV_WIDTH - j, :] * up
        du_ref[...] = du.astype(BF16)

        @pl.when(i == 0)
        def _():
            dw_ref[...] = jnp.zeros_like(dw_ref)
            db_ref[...] = jnp.zeros_like(db_ref)

        for k in range(CONV_WIDTH):
            dw_ref[k:k + 1, :] += jnp.sum(dp_c * _shift_rows(curv, prevv, CONV_WIDTH - 1 - k, row), axis=0,
                                          keepdims=True)
        db_ref[...] += jnp.sum(dp_c, axis=0, keepdims=True)

    return _call(body, name="conv_bwd", grid=(CONV_COL_BLOCKS, nb),
                 in_specs=[dcur, dnxt, cur, prev, nxt, wspec, bspec], out_specs=[dcur, wspec, bspec],
                 out_shape=[jax.ShapeDtypeStruct((nb * BLOCK, CONV_DIM), BF16),
                            jax.ShapeDtypeStruct((CONV_WIDTH, CONV_DIM), F32), jax.ShapeDtypeStruct((1, CONV_DIM), F32)],
                 sem=("parallel", "arbitrary"))(dxbc, dxbc, proj, proj, proj, conv_w, conv_b)


def _head_expand():
    e = np.zeros((LANES, SSM_INNER), np.float32)
    for h in range(SSM_HEADS):
        e[h, h * HEAD_DIM:(h + 1) * HEAD_DIM] = 1.0
    return jnp.asarray(e)


def _softplus(x):
    return jnp.maximum(x, 0.0) + jnp.log(1.0 + jnp.exp(-jnp.abs(x)))


def dt_fwd(proj, expand, bias_x):
    nb = proj.shape[0] // BLOCK

    def body(t_ref, e_ref, b_ref, o_ref):
        raw = jnp.dot(t_ref[...], e_ref[...], precision=HI, preferred_element_type=F32)
        o_ref[...] = _softplus(raw + b_ref[...])

    return _call(body, name="dt_fwd", grid=(nb,),
                 in_specs=[_row_spec(LANES, DT_TILE), _full((LANES, SSM_INNER)), _full((1, SSM_INNER))],
                 out_specs=_row_spec(SSM_INNER), out_shape=jax.ShapeDtypeStruct((nb * BLOCK, SSM_INNER), F32),
                 sem=("parallel",))(proj, expand, bias_x)


def dt_bwd(ddt_x, proj, expand, bias_x):
    nb = proj.shape[0] // BLOCK

    def body(d_ref, t_ref, e_ref, b_ref, o_ref, db_ref):
        raw = jnp.dot(t_ref[...], e_ref[...], precision=HI, preferred_element_type=F32)
        draw = d_ref[...] * jax.nn.sigmoid(raw + b_ref[...])
        dt = lax.dot_general(draw, e_ref[...], _NT, precision=HI, preferred_element_type=F32)
        o_ref[...] = dt.astype(BF16)

        @pl.when(pl.program_id(0) == 0)
        def _():
            db_ref[...] = jnp.zeros_like(db_ref)

        db_ref[...] += jnp.sum(dt, axis=0, keepdims=True)

    return _call(body, name="dt_bwd", grid=(nb,),
                 in_specs=[_row_spec(SSM_INNER), _row_spec(LANES, DT_TILE), _full((LANES, SSM_INNER)),
                           _full((1, SSM_INNER))],
                 out_specs=[_row_spec(LANES), _full((1, LANES))],
                 out_shape=[jax.ShapeDtypeStruct((nb * BLOCK, LANES), BF16), jax.ShapeDtypeStruct((1, LANES), F32)],
                 sem=("arbitrary",))(ddt_x, proj, expand, bias_x)


def _ssd_chunk(xs, dtx, bg, cg, alog, dsk, state):
    l = lax.broadcasted_iota(jnp.int32, (BLOCK, BLOCK), 0)
    s = lax.broadcasted_iota(jnp.int32, (BLOCK, BLOCK), 1)
    tril = (l >= s).astype(F32)
    a = dtx * (-jnp.exp(alog))
    cs = jnp.dot(tril, a, precision=HI, preferred_element_type=F32)
    cs_t = lax.dot_general(a, tril, (((0,), (1,)), ((), ())), precision=HI, preferred_element_type=F32)
    cs_col = jnp.max(cs, axis=1, keepdims=True)
    cs_row = jnp.max(cs_t, axis=0, keepdims=True)
    decay = jnp.exp(jnp.where(l >= s, cs_col - cs_row, -jnp.inf))
    bb, cb16 = bg.astype(BF16), cg.astype(BF16)
    cb = lax.dot_general(cb16, bb, _NT, preferred_element_type=F32)
    xr = xs * dtx
    y_diag = jnp.dot((cb * decay).astype(BF16), xr.astype(BF16), preferred_element_type=F32)
    tot = jnp.sum(a, axis=0, keepdims=True)
    st = lax.dot_general(bb, (xr * jnp.exp(tot - cs)).astype(BF16), (((0,), (0,)), ((), ())),
                         preferred_element_type=F32)
    new_state = state * jnp.exp(tot) + st
    y_off = jnp.dot(cb16, state.astype(BF16), preferred_element_type=F32) * jnp.exp(cs)
    return y_diag + y_off + dsk * xs, new_state


B_TILE0 = SSM_INNER // LANES
C_TILE0 = B_TILE0 + SSM_GROUPS


def ssd_fwd(xh, dth, xbc, alog_x, dsk_x):
    nb = xh.shape[1] // BLOCK
    hp = SSM_HEADS_PER_GROUP
    hspec = pl.BlockSpec((hp, BLOCK, HEAD_DIM), lambda g, c: (g, c, 0))
    pspec = pl.BlockSpec((hp, 1, HEAD_DIM), lambda g, c: (g, 0, 0))

    def body(x_ref, dt_ref, b_ref, c_ref, al_ref, dk_ref, y_ref, sp_ref, st_ref):
        @pl.when(pl.program_id(1) == 0)
        def _():
            st_ref[...] = jnp.zeros_like(st_ref)

        bg, cg = b_ref[...], c_ref[...]

        def head(j, carry):
            state = st_ref[j]
            sp_ref[j, 0] = state
            y, new_state = _ssd_chunk(x_ref[j], dt_ref[j], bg, cg, al_ref[j], dk_ref[j], state)
            y_ref[j] = y
            st_ref[j] = new_state
            return carry

        lax.fori_loop(0, hp, head, 0)

    return _call(body, name="ssd_fwd", grid=(SSM_GROUPS, nb),
                 in_specs=[hspec, hspec, pl.BlockSpec((BLOCK, SSM_STATE), lambda g, c: (c, B_TILE0 + g)),
                           pl.BlockSpec((BLOCK, SSM_STATE), lambda g, c: (c, C_TILE0 + g)), pspec, pspec],
                 out_specs=[hspec, pl.BlockSpec((hp, 1, SSM_STATE, HEAD_DIM), lambda g, c: (g, c, 0, 0))],
                 out_shape=[jax.ShapeDtypeStruct(xh.shape, F32),
                            jax.ShapeDtypeStruct((SSM_HEADS, nb, SSM_STATE, HEAD_DIM), F32)],
                 scratch=[pltpu.VMEM((hp, SSM_STATE, HEAD_DIM), F32)],
                 sem=("parallel", "arbitrary"))(xh, dth, xbc, xbc, alog_x, dsk_x)


def ssd_bwd(dy, xh, dth, xbc, alog_x, dsk_x, states):
    nb = xh.shape[1] // BLOCK
    hp = SSM_HEADS_PER_GROUP
    last = nb - 1
    hspec = pl.BlockSpec((hp, BLOCK, HEAD_DIM), lambda g, c: (g, last - c, 0))
    pspec = pl.BlockSpec((hp, 1, HEAD_DIM), lambda g, c: (g, 0, 0))
    gspec = pl.BlockSpec((None, BLOCK, SSM_STATE), lambda g, c: (g, last - c, 0))

    def body(dy_ref, x_ref, dt_ref, b_ref, c_ref, al_ref, dk_ref, sp_ref,
             dx_ref, ddt_ref, db_ref, dc_ref, dal_ref, ddk_ref, ds_ref):
        @pl.when(pl.program_id(1) == 0)
        def _():
            ds_ref[...] = jnp.zeros_like(ds_ref)
            dal_ref[...] = jnp.zeros_like(dal_ref)
            ddk_ref[...] = jnp.zeros_like(ddk_ref)

        db_ref[...] = jnp.zeros_like(db_ref)
        dc_ref[...] = jnp.zeros_like(dc_ref)
        bg, cg = b_ref[...], c_ref[...]

        def head(j, carry):
            _, vjp = jax.vjp(_ssd_chunk, x_ref[j], dt_ref[j], bg, cg, al_ref[j], dk_ref[j], sp_ref[j, 0])
            dx, ddt, dbg, dcg, dal, ddk, dstate = vjp((dy_ref[j], ds_ref[j]))
            dx_ref[j] = dx
            ddt_ref[j] = ddt
            db_ref[...] += dbg
            dc_ref[...] += dcg
            dal_ref[j] += jnp.broadcast_to(jnp.sum(dal, axis=1, keepdims=True), dal.shape)
            ddk_ref[j] += jnp.broadcast_to(jnp.sum(ddk, axis=1, keepdims=True), ddk.shape)
            ds_ref[j] = dstate
            return carry

        lax.fori_loop(0, hp, head, 0)

    return _call(body, name="ssd_bwd", grid=(SSM_GROUPS, nb),
                 in_specs=[hspec, hspec, hspec, pl.BlockSpec((BLOCK, SSM_STATE), lambda g, c: (last - c, B_TILE0 + g)),
                           pl.BlockSpec((BLOCK, SSM_STATE), lambda g, c: (last - c, C_TILE0 + g)), pspec, pspec,
                           pl.BlockSpec((hp, 1, SSM_STATE, HEAD_DIM), lambda g, c: (g, last - c, 0, 0))],
                 out_specs=[hspec, hspec, gspec, gspec, pspec, pspec],
                 out_shape=[jax.ShapeDtypeStruct(xh.shape, F32), jax.ShapeDtypeStruct(xh.shape, F32),
                            jax.ShapeDtypeStruct((SSM_GROUPS, nb * BLOCK, SSM_STATE), F32),
                            jax.ShapeDtypeStruct((SSM_GROUPS, nb * BLOCK, SSM_STATE), F32),
                            jax.ShapeDtypeStruct(alog_x.shape, F32), jax.ShapeDtypeStruct(dsk_x.shape, F32)],
                 scratch=[pltpu.VMEM((hp, SSM_STATE, HEAD_DIM), F32)],
                 sem=("parallel", "arbitrary"))(dy, xh, dth, xbc, xbc, alog_x, dsk_x, states)


SLAB_ROWS = 24
SLAB_META_ROW = 8


def pack_small(dcw, dcb, dgpre, dgpost, ddtb, dalog, ddsk, dsinks, dgn, dmeta):
    def body(cw, cb, gpre, gpost, dtb, al, dk, sk, gn, meta, o_ref):
        o_ref[...] = jnp.zeros_like(o_ref)
        o_ref[0:CONV_WIDTH, :] = cw[...]
        o_ref[4:5, :] = cb[...]
        o_ref[5:6, 0:1024] = gpre[...]
        o_ref[5:6, 1024:2048] = gpost[...]
        o_ref[5:6, 2048:2176] = dtb[...]
        o_ref[5:6, 2176:2304] = al[...]
        o_ref[5:6, 2304:2432] = dk[...]
        o_ref[5:6, 2432:2560] = sk[...]
        o_ref[6:7, 0:SSM_INNER] = gn[...]
        o_ref[SLAB_META_ROW:SLAB_META_ROW + N_META, 0:D_MODEL] = meta[...]

    args = (dcw, dcb, dgpre, dgpost, ddtb, dalog, ddsk, dsinks, dgn, dmeta)
    return _call(body, name="pack_small", in_specs=[_full(a.shape) for a in args],
                 out_specs=_full((SLAB_ROWS, CONV_DIM)), out_shape=jax.ShapeDtypeStruct((SLAB_ROWS, CONV_DIM), F32))(*args)


def _head_major(a, heads):
    return a.reshape(a.shape[0], heads, HEAD_DIM).transpose(1, 0, 2)


def _row_major(a):
    return a.transpose(1, 0, 2).reshape(a.shape[1], a.shape[0] * a.shape[2])


def _lane_tile(v):
    return jnp.pad(v, ((0, 0), (0, LANES - v.shape[1])))


def kernel(x, meta_tokens, g_pre, w_in, conv_w, conv_b, dt_bias, a_log, d_skip, attn_sinks, g_ssm_norm, w_out_att, w_out_ssm, w_out, g_post, loss_target, m_meta_tokens, m_g_pre, m_w_in, m_conv_w, m_conv_b, m_dt_bias, m_a_log, m_d_skip, m_attn_sinks, m_g_ssm_norm, m_w_out_att, m_w_out_ssm, m_w_out, m_g_post, v_meta_tokens, v_g_pre, v_w_in, v_conv_w, v_conv_b, v_dt_bias, v_a_log, v_d_skip, v_attn_sinks, v_g_ssm_norm, v_w_out_att, v_w_out_ssm, v_w_out, v_g_post):
    seq = x.shape[1]
    rows = seq + BLOCK
    chip = _chip_index()

    gathered = exchange(
        "gather_weights",
        [pack_w_in(w_in[0]), w_out_att[0].astype(BF16), w_out_ssm[0].astype(BF16), w_out[0].astype(BF16),
         conv_w[0], meta_tokens],
        [False] * 6, masks=(4, 2, 6), slot_kind="chip", nslots=4)
    w_all = unpack_w_in(gathered[0])
    woa = gathered[1].reshape(D_MODEL, D_MODEL)
    wos = gathered[2].reshape(SSM_INNER, D_MODEL)
    wo = gathered[3].reshape(D_MODEL, D_MODEL)
    cw_full = gathered[4].transpose(1, 0, 2).reshape(CONV_WIDTH, CONV_DIM)
    meta_full = gathered[5].transpose(1, 0, 2).reshape(N_META, D_MODEL)

    h, u = prep(x, meta_full, g_pre)
    proj = mm_nn("in_proj", u, w_all)

    def seg(name):
        _, w, p0 = SEG[name]
        return proj[:, p0:p0 + w]

    qh, kh, vh = _head_major(seg("q"), ATT_Q_HEADS), _head_major(seg("k"), ATT_KV_HEADS), _head_major(seg("v"), ATT_KV_HEADS)
    kmeta, vmeta = kh[:, PAD_ROWS:BLOCK], vh[:, PAD_ROWS:BLOCK]
    sinks3 = attn_sinks.reshape(ATT_Q_HEADS, 1, 1)
    slopes3 = jnp.asarray(np.array([2.0 ** (-8.0 * (i + 1) / ATT_Q_HEADS) for i in range(ATT_Q_HEADS)], np.float32)
                          ).reshape(ATT_Q_HEADS, 1, 1)
    o_att = _row_major(attn_fwd(qh, kh, vh, kmeta, vmeta, sinks3, slopes3))
    z_att_block = SEG["z_att"][2] // D_MODEL
    a_att = gate_fwd("att_gate_fwd", o_att, proj, z_att_block)
    y_att = mm_nn("att_out", a_att, woa)

    xbc = conv_fwd(proj, cw_full, conv_b)
    expand = _head_expand()
    bias_x = jnp.repeat(dt_bias, HEAD_DIM, axis=1)
    dt_x = dt_fwd(proj, expand, bias_x)
    xh = _head_major(xbc[:, :SSM_INNER], SSM_HEADS)
    dth = _head_major(dt_x, SSM_HEADS)
    alog_x = jnp.broadcast_to(a_log.reshape(SSM_HEADS, 1, 1), (SSM_HEADS, 1, HEAD_DIM))
    dsk_x = jnp.broadcast_to(d_skip.reshape(SSM_HEADS, 1, 1), (SSM_HEADS, 1, HEAD_DIM))
    y_hm, states = ssd_fwd(xh, dth, xbc, alog_x, dsk_x)
    y_ssd = _row_major(y_hm)
    yn = ssm_norm_fwd(y_ssd, proj, g_ssm_norm)
    y_ssm = mm_nn("ssm_out", yn, wos)

    merged = merge_fwd(proj, y_att, y_ssm)
    out = mm_nn("out_proj", merged, wo)
    dout, dres, loss_tile, dg_post = final_loss(out, x, loss_target, g_post)
    loss = lax.psum(loss_tile[0, 0], ("x", "y", "c"))

    dmerged = mm_nt("out_proj_dx", dout, wo)
    dwo = mm_tn("out_proj_dw", merged, dout)
    dga, dgs, dy_att, dy_ssm = merge_bwd(dmerged, proj, y_att, y_ssm)

    da_att = mm_nt("att_out_dx", dy_att, woa)
    dwoa = mm_tn("att_out_dw", a_att, dy_att)
    do_att, dz_att = gate_bwd("att_gate_bwd", da_att, o_att, proj, z_att_block)
    dqh, dkh, dvh, dkmeta, dvmeta, dsinks3 = attn_bwd(_head_major(do_att, ATT_Q_HEADS), qh, kh, vh, kmeta, vmeta,
                                                      sinks3, slopes3)
    dkh = dkh.at[:, PAD_ROWS:BLOCK].add(dkmeta)
    dvh = dvh.at[:, PAD_ROWS:BLOCK].add(dvmeta)
    dq, dk, dv = _row_major(dqh).astype(BF16), _row_major(dkh).astype(BF16), _row_major(dvh).astype(BF16)

    dyn = mm_nt("ssm_out_dx", dy_ssm, wos)
    dwos = mm_tn("ssm_out_dw", yn, dy_ssm)
    dy_ssd, dz_ssm, dgn = ssm_norm_bwd(dyn, y_ssd, proj, g_ssm_norm)
    dxh, ddth, dbg, dcg, dalog_x, ddsk_x = ssd_bwd(_head_major(dy_ssd, SSM_HEADS), xh, dth, xbc, alog_x, dsk_x, states)
    dxbc = jnp.concatenate([_row_major(dxh), _row_major(dbg), _row_major(dcg)], axis=1)
    dxbc_raw, dcw, dcb = conv_bwd(dxbc, proj, cw_full, conv_b)
    ddt_tile, ddtb_tile = dt_bwd(_row_major(ddth), proj, expand, bias_x)

    dproj = jnp.concatenate([dz_ssm, dxbc_raw, dq, dz_att, dga, dgs, dk, dv, ddt_tile], axis=1)
    du = mm_nt("in_proj_dx", dproj, w_all)
    dw_all = mm_tn("in_proj_dw", u, dproj)
    grad_x, dmeta, dg_pre = prep_bwd(h, du, dres, g_pre)

    dalog = dalog_x[:, 0, 0].reshape(1, SSM_HEADS)
    ddsk = ddsk_x[:, 0, 0].reshape(1, SSM_HEADS)
    slab = pack_small(dcw, dcb, dg_pre, dg_post, ddtb_tile, _lane_tile(dalog), _lane_tile(ddsk),
                      _lane_tile(dsinks3.reshape(1, ATT_Q_HEADS)), dgn, dmeta)

    def pieces(g):
        return g.astype(BF16).reshape(4, 2, g.shape[0] // 8, g.shape[1])

    sent = exchange(
        "reduce_grads",
        [pack_grad_w_in(dw_all).reshape(4, 2, D_MODEL // 2, PACK_W), pieces(dwoa), pieces(dwos), pieces(dwo), slab],
        [True, True, True, True, False], masks=(1, 2, 3, 4, 5, 6, 7), slot_kind="dev", nslots=8)
    halves = [sum_slots("sum_" + nm, r) for nm, r in zip(("w_in", "w_out_att", "w_out_ssm", "w_out"), sent[:4])]
    small = sum_slots("sum_small", sent[4])
    full = exchange("share_halves", halves, [False] * 4, masks=(1,), slot_kind="core", nslots=2)
    g_w_in_packed, g_woa, g_wos, g_wo = [f.reshape(2 * f.shape[1], f.shape[2]) for f in full]

    g_w_in, d_w_in, nm_w_in, nv_w_in = adamw_w_in(g_w_in_packed, w_in[0], m_w_in[0], v_w_in[0])
    d_woa, nm_woa, nv_woa = adamw_rows("adamw_w_out_att", g_woa, w_out_att[0], m_w_out_att[0], v_w_out_att[0])
    d_wos, nm_wos, nv_wos = adamw_rows("adamw_w_out_ssm", g_wos, w_out_ssm[0], m_w_out_ssm[0], v_w_out_ssm[0])
    d_wo, nm_wo, nv_wo = adamw_rows("adamw_w_out", g_wo, w_out[0], m_w_out[0], v_w_out[0])

    cw_cols = CONV_DIM // 4
    meta_cols = D_MODEL // 4
    g_small = {
        "meta_tokens": lax.dynamic_slice(small, (SLAB_META_ROW, chip * meta_cols), (N_META, meta_cols)),
        "g_pre": small[5:6, 0:1024],
        "conv_w": lax.dynamic_slice(small, (0, chip * cw_cols), (CONV_WIDTH, cw_cols)),
        "conv_b": small[4:5, :],
        "dt_bias": small[5:6, 2048:2048 + SSM_HEADS],
        "a_log": small[5:6, 2176:2176 + SSM_HEADS],
        "d_skip": small[5:6, 2304:2304 + SSM_HEADS],
        "attn_sinks": small[5:6, 2432:2432 + ATT_Q_HEADS],
        "g_ssm_norm": small[6:7, 0:SSM_INNER],
        "g_post": small[5:6, 1024:2048],
    }
    names = list(g_small)
    w_small = dict(meta_tokens=meta_tokens, g_pre=g_pre, conv_w=conv_w[0], conv_b=conv_b, dt_bias=dt_bias, a_log=a_log,
                   d_skip=d_skip, attn_sinks=attn_sinks, g_ssm_norm=g_ssm_norm, g_post=g_post)
    m_small = dict(meta_tokens=m_meta_tokens, g_pre=m_g_pre, conv_w=m_conv_w[0], conv_b=m_conv_b, dt_bias=m_dt_bias,
                   a_log=m_a_log, d_skip=m_d_skip, attn_sinks=m_attn_sinks, g_ssm_norm=m_g_ssm_norm, g_post=m_g_post)
    v_small = dict(meta_tokens=v_meta_tokens, g_pre=v_g_pre, conv_w=v_conv_w[0], conv_b=v_conv_b, dt_bias=v_dt_bias,
                   a_log=v_a_log, d_skip=v_d_skip, attn_sinks=v_attn_sinks, g_ssm_norm=v_g_ssm_norm, g_post=v_g_post)
    upd = dict(zip(names, adamw_small([g_small[k] for k in names], [w_small[k] for k in names],
                                      [m_small[k] for k in names], [v_small[k] for k in names])))

    lead = {"conv_w"}

    def shaped(name, a):
        return a[None] if name in lead else a

    grads = dict(g_small, w_in=g_w_in, w_out_att=g_woa, w_out_ssm=g_wos, w_out=g_wo)
    deltas = dict({k: upd[k][0] for k in names}, w_in=d_w_in, w_out_att=d_woa, w_out_ssm=d_wos, w_out=d_wo)
    new_m = dict({k: upd[k][1] for k in names}, w_in=nm_w_in, w_out_att=nm_woa, w_out_ssm=nm_wos, w_out=nm_wo)
    new_v = dict({k: upd[k][2] for k in names}, w_in=nv_w_in, w_out_att=nv_woa, w_out_ssm=nv_wos, w_out=nv_wo)
    lead |= {"w_in", "w_out_att", "w_out_ssm", "w_out"}
    order = ["meta_tokens", "g_pre", "w_in", "conv_w", "conv_b", "dt_bias", "a_log", "d_skip", "attn_sinks",
             "g_ssm_norm", "w_out_att", "w_out_ssm", "w_out", "g_post"]
    outs = [loss, grad_x]
    for group in (grads, deltas, new_m, new_v):
        outs += [shaped(k, group[k]) for k in order]
    return tuple(outs)
```

```python
import functools

import numpy as np
import jax
import jax.numpy as jnp
from jax import lax
from jax.experimental import pallas as pl
from jax.experimental.pallas import tpu as pltpu

F32 = jnp.float32
BF16 = jnp.bfloat16
HI = lax.Precision.HIGHEST

D_MODEL = 1024
N_META = 16
BLOCK = 128
PAD_ROWS = BLOCK - N_META
NORM_EPS = 1e-6
HEAD_DIM = 64
ATT_Q_HEADS = 16
ATT_KV_HEADS = 4
ATT_GROUP = 4
SSM_INNER = 2048
SSM_HEADS = 32
SSM_GROUPS = 4
SSM_HEADS_PER_GROUP = 8
SSM_STATE = 128
CONV_WIDTH = 4
CONV_DIM = 3072
LANES = 128

ADAM_LR = 0.001
ADAM_B1 = 0.9
ADAM_B2 = 0.999
ADAM_EPS = 1e-08
ADAM_WD = 0.01
ADAM_STEP = 10

VMEM_LIMIT = 48 * 1024 * 1024

SHARD_W = 2440
PACK_W = 2560
SHARD_STRIDE = 2432
N_ALIGNED = 9856
SEG = {
    "q": (0, 1024, 5120), "k": (1024, 256, 9216), "v": (1280, 256, 9472), "z_att": (1536, 1024, 6144),
    "z_ssm": (2560, 2048, 0), "xbc": (4608, 3072, 2048), "dt": (7680, 128, 9728),
    "gate_att": (7808, 1024, 7168), "gate_ssm": (8832, 1024, 8192),
}
DT_STORED_START = 7680
DT_PAD = LANES - SSM_HEADS


def _act_col(aligned_col):
    for a0, w, p0 in SEG.values():
        if a0 <= aligned_col < a0 + w:
            return p0 + aligned_col - a0
    raise ValueError(aligned_col)


def _call(body, *, name, out_shape, in_specs, out_specs, grid=(), scratch=(), sem=None, aliases=None):
    return pl.pallas_call(
        body, out_shape=out_shape, grid=grid, in_specs=in_specs, out_specs=out_specs, scratch_shapes=list(scratch),
        name=name, input_output_aliases=aliases or {},
        compiler_params=pltpu.CompilerParams(dimension_semantics=sem, vmem_limit_bytes=VMEM_LIMIT))


def _full(shape):
    n = len(shape)
    return pl.BlockSpec(shape, lambda *_: (0,) * n)


def _chip_index():
    return lax.axis_index("x") * 2 + lax.axis_index("y")


def _silu(z):
    return z * jax.nn.sigmoid(z)


def _rms(x, g):
    return x * lax.rsqrt(jnp.mean(x * x, axis=-1, keepdims=True) + NORM_EPS) * g


def _peer(mask):
    x, y, c = lax.axis_index("x"), lax.axis_index("y"), lax.axis_index("c")
    return ((1 - x) if mask & 4 else x, (1 - y) if mask & 2 else y, (1 - c) if mask & 1 else c)


def exchange(name, arrays, by_target, masks, slot_kind, nslots):
    n, nk = len(arrays), len(masks)
    piece_shapes = [a.shape[2:] if bt else a.shape for a, bt in zip(arrays, by_target)]

    def body(*refs):
        ins, outs = refs[:n], refs[n:2 * n]
        send_sems, recv_sems, local_sems = refs[2 * n:]
        x, y, c = lax.axis_index("x"), lax.axis_index("y"), lax.axis_index("c")
        slot = {"chip": 2 * x + y, "dev": 4 * x + 2 * y + c, "core": c}[slot_kind]

        def piece(a, dev):
            return ins[a].at[2 * dev[0] + dev[1], dev[2]] if by_target[a] else ins[a]

        local = [pltpu.make_async_copy(piece(a, (x, y, c)), outs[a].at[slot], local_sems.at[a]) for a in range(n)]
        for cp in local:
            cp.start()
        remote = []
        for a in range(n):
            for ki, mask in enumerate(masks):
                dev = _peer(mask)
                remote.append(pltpu.make_async_remote_copy(
                    src_ref=piece(a, dev), dst_ref=outs[a].at[slot], send_sem=send_sems.at[a * nk + ki],
                    recv_sem=recv_sems.at[a * nk + ki], device_id=dev, device_id_type=pl.DeviceIdType.MESH))
        for cp in remote:
            cp.start()
        for cp in remote:
            cp.wait()
        for cp in local:
            cp.wait()

    anyspec = pl.BlockSpec(memory_space=pl.ANY)
    return pl.pallas_call(
        body, name=name,
        out_shape=[jax.ShapeDtypeStruct((nslots,) + tuple(s), a.dtype) for s, a in zip(piece_shapes, arrays)],
        in_specs=[anyspec] * n, out_specs=[anyspec] * n,
        scratch_shapes=[pltpu.SemaphoreType.DMA((n * nk,)), pltpu.SemaphoreType.DMA((n * nk,)),
                        pltpu.SemaphoreType.DMA((n,))],
    )(*arrays)


def _shift_right(p, chip, col):
    if chip == 0:
        return p
    if chip < 3:
        return pltpu.roll(p, 8 * chip, 1)
    tail0 = DT_STORED_START + SSM_HEADS - 3 * SHARD_W
    head = pltpu.roll(p, 24, 1)
    tail = pltpu.roll(p, 24 + DT_PAD, 1)
    return jnp.where(col < tail0 + 24, head, jnp.where(col >= tail0 + 24 + DT_PAD, tail, 0.0))


def _shift_left(g, chip, col):
    if chip == 0:
        return g
    if chip < 3:
        return pltpu.roll(g, PACK_W - 8 * chip, 1)
    tail0 = DT_STORED_START + SSM_HEADS - 3 * SHARD_W
    return jnp.where(col < tail0, pltpu.roll(g, PACK_W - 24, 1), pltpu.roll(g, PACK_W - 24 - DT_PAD, 1))


def pack_w_in(w):
    rows = BLOCK

    def body(w_ref, o_ref, pad_ref):
        chip = _chip_index()
        pad_ref[...] = jnp.zeros_like(pad_ref)
        pad_ref[:, 0:SHARD_W] = w_ref[...]
        p = pad_ref[...]
        col = lax.broadcasted_iota(jnp.int32, p.shape, 1)
        for cv in range(4):
            @pl.when(chip == cv)
            def _():
                o_ref[...] = _shift_right(p, cv, col).astype(BF16)

    return _call(body, name="pack_w_in", grid=(D_MODEL // rows,),
                 in_specs=[pl.BlockSpec((rows, SHARD_W), lambda i: (i, 0))],
                 out_specs=pl.BlockSpec((rows, PACK_W), lambda i: (i, 0)),
                 out_shape=jax.ShapeDtypeStruct((D_MODEL, PACK_W), BF16),
                 scratch=[pltpu.VMEM((rows, PACK_W), F32)], sem=("parallel",))(w)


def _tile_runs():
    runs, fix = [], []
    for t in range(N_ALIGNED // LANES):
        s = min(t // 19, 3)
        j = t - 19 * s
        p = _act_col(t * LANES)
        if runs and runs[-1][1] == s and runs[-1][0] + runs[-1][3] == p and runs[-1][2] + runs[-1][3] == j * LANES:
            runs[-1][3] += LANES
        else:
            runs.append([p, s, j * LANES, LANES])
        if j == 0 and s > 0:
            fix.append((p, s - 1))
    return runs, fix


def unpack_w_in(bg):
    rows = BLOCK
    runs, fix = _tile_runs()

    def body(b_ref, o_ref):
        for p, s, j, w in runs:
            o_ref[:, p:p + w] = b_ref[s, :, j:j + w]
        for p, s in fix:
            o_ref[:, p:p + LANES] = o_ref[:, p:p + LANES] + b_ref[s, :, SHARD_STRIDE:PACK_W]

    return _call(body, name="unpack_w_in", grid=(D_MODEL // rows,),
                 in_specs=[pl.BlockSpec((4, rows, PACK_W), lambda i: (0, i, 0))],
                 out_specs=pl.BlockSpec((rows, N_ALIGNED), lambda i: (i, 0)),
                 out_shape=jax.ShapeDtypeStruct((D_MODEL, N_ALIGNED), BF16), sem=("parallel",))(bg)


def pack_grad_w_in(dw):
    rows = BLOCK

    def body(g_ref, o_ref):
        for s in range(4):
            for j in range(PACK_W // LANES):
                p = _act_col((19 * s + j) * LANES)
                o_ref[s, :, j * LANES:(j + 1) * LANES] = g_ref[:, p:p + LANES].astype(BF16)

    return _call(body, name="pack_grad_w_in", grid=(D_MODEL // rows,),
                 in_specs=[pl.BlockSpec((rows, N_ALIGNED), lambda i: (i, 0))],
                 out_specs=pl.BlockSpec((4, rows, PACK_W), lambda i: (0, i, 0)),
                 out_shape=jax.ShapeDtypeStruct((4, D_MODEL, PACK_W), BF16), sem=("parallel",))(dw)


def _adamw(w, g, m, v):
    m = ADAM_B1 * m + (1.0 - ADAM_B1) * g
    v = ADAM_B2 * v + (1.0 - ADAM_B2) * jnp.square(g)
    m_hat = m / (1.0 - ADAM_B1 ** ADAM_STEP)
    v_hat = v / (1.0 - ADAM_B2 ** ADAM_STEP)
    delta = -ADAM_LR * (m_hat / (jnp.sqrt(v_hat) + ADAM_EPS) + ADAM_WD * w)
    return delta, m, v


def adamw_w_in(g_packed, w, m, v):
    rows = BLOCK

    def body(g_ref, w_ref, m_ref, v_ref, go_ref, d_ref, mo_ref, vo_ref, tmp_ref):
        chip = _chip_index()
        gp = g_ref[...]
        col = lax.broadcasted_iota(jnp.int32, gp.shape, 1)
        for cv in range(4):
            @pl.when(chip == cv)
            def _():
                tmp_ref[...] = _shift_left(gp, cv, col)
        g = tmp_ref[:, 0:SHARD_W]
        d, mn, vn = _adamw(w_ref[...], g, m_ref[...], v_ref[...])
        go_ref[...] = g
        d_ref[...] = d
        mo_ref[...] = mn
        vo_ref[...] = vn

    spec = pl.BlockSpec((rows, SHARD_W), lambda i: (i, 0))
    shp = jax.ShapeDtypeStruct((D_MODEL, SHARD_W), F32)
    return _call(body, name="adamw_w_in", grid=(D_MODEL // rows,),
                 in_specs=[pl.BlockSpec((rows, PACK_W), lambda i: (i, 0)), spec, spec, spec],
                 out_specs=[spec] * 4, out_shape=[shp] * 4,
                 scratch=[pltpu.VMEM((rows, PACK_W), F32)], sem=("parallel",))(g_packed, w, m, v)


def adamw_rows(name, g, w, m, v):
    r, c = g.shape
    rows = min(r, BLOCK)

    def body(g_ref, w_ref, m_ref, v_ref, d_ref, mo_ref, vo_ref):
        d_ref[...], mo_ref[...], vo_ref[...] = _adamw(w_ref[...], g_ref[...], m_ref[...], v_ref[...])

    spec = pl.BlockSpec((rows, c), lambda i: (i, 0))
    shp = jax.ShapeDtypeStruct((r, c), F32)
    return _call(body, name=name, grid=(r // rows,), in_specs=[spec] * 4, out_specs=[spec] * 3, out_shape=[shp] * 3,
                 sem=("parallel",))(g, w, m, v)


def adamw_small(gs, ws, ms, vs):
    n = len(gs)

    def body(*refs):
        g, w, m, v = refs[:n], refs[n:2 * n], refs[2 * n:3 * n], refs[3 * n:4 * n]
        outs = refs[4 * n:]
        for i in range(n):
            d, mn, vn = _adamw(w[i][...], g[i][...], m[i][...], v[i][...])
            outs[3 * i][...] = d
            outs[3 * i + 1][...] = mn
            outs[3 * i + 2][...] = vn

    specs = [_full(a.shape) for a in gs]
    res = _call(body, name="adamw_small", in_specs=specs * 4,
                out_specs=[s for s in specs for _ in range(3)],
                out_shape=[jax.ShapeDtypeStruct(a.shape, F32) for a in gs for _ in range(3)])(*gs, *ws, *ms, *vs)
    return [tuple(res[3 * i:3 * i + 3]) for i in range(n)]


def sum_slots(name, r):
    s, rr, c = r.shape
    rows = min(rr, BLOCK)

    def body(r_ref, o_ref):
        acc = r_ref[0].astype(F32)
        for k in range(1, s):
            acc = acc + r_ref[k].astype(F32)
        o_ref[...] = acc

    return _call(body, name=name, grid=(rr // rows,), in_specs=[pl.BlockSpec((s, rows, c), lambda i: (0, i, 0))],
                 out_specs=pl.BlockSpec((rows, c), lambda i: (i, 0)), out_shape=jax.ShapeDtypeStruct((rr, c), F32),
                 sem=("parallel",))(r)


def _col_tile(n, k):
    if n % 896 == 0 and k <= 1024:
        return 896
    return min(n, 512)


def mm_nn(name, x, w, out_dtype=F32):
    m, k = x.shape
    n = w.shape[1]
    tn = _col_tile(n, k)

    def body(x_ref, w_ref, o_ref):
        o_ref[...] = jnp.dot(x_ref[...].astype(BF16), w_ref[...].astype(BF16),
                             preferred_element_type=F32).astype(out_dtype)

    return _call(body, name=name, grid=(n // tn,), in_specs=[_full((m, k)), pl.BlockSpec((k, tn), lambda j: (0, j))],
                 out_specs=pl.BlockSpec((m, tn), lambda j: (0, j)), out_shape=jax.ShapeDtypeStruct((m, n), out_dtype),
                 sem=("parallel",))(x, w)


def mm_nt(name, dy, w, out_dtype=F32):
    m, n = dy.shape
    k = w.shape[0]
    tm = m // 2
    tn = _col_tile(n, k)
    steps = n // tn

    def body(dy_ref, w_ref, o_ref, acc_ref):
        j = pl.program_id(1)

        @pl.when(j == 0)
        def _():
            acc_ref[...] = jnp.zeros_like(acc_ref)

        acc_ref[...] += lax.dot_general(dy_ref[...].astype(BF16), w_ref[...].astype(BF16), (((1,), (1,)), ((), ())),
                                        preferred_element_type=F32)

        @pl.when(j == steps - 1)
        def _():
            o_ref[...] = acc_ref[...].astype(out_dtype)

    return _call(body, name=name, grid=(m // tm, steps),
                 in_specs=[pl.BlockSpec((tm, tn), lambda i, j: (i, j)), pl.BlockSpec((k, tn), lambda i, j: (0, j))],
                 out_specs=pl.BlockSpec((tm, k), lambda i, j: (i, 0)), out_shape=jax.ShapeDtypeStruct((m, k), out_dtype),
                 scratch=[pltpu.VMEM((tm, k), F32)], sem=("parallel", "arbitrary"))(dy, w)


def mm_tn(name, x, dy):
    m, k = x.shape
    n = dy.shape[1]
    tm = m // 2
    tn = _col_tile(n, k)

    def body(x_ref, dy_ref, o_ref):
        @pl.when(pl.program_id(1) == 0)
        def _():
            o_ref[...] = jnp.zeros_like(o_ref)

        o_ref[...] += lax.dot_general(x_ref[...].astype(BF16), dy_ref[...].astype(BF16), (((0,), (0,)), ((), ())),
                                      preferred_element_type=F32)

    return _call(body, name=name, grid=(n // tn, m // tm),
                 in_specs=[pl.BlockSpec((tm, k), lambda i, j: (j, 0)), pl.BlockSpec((tm, tn), lambda i, j: (j, i))],
                 out_specs=pl.BlockSpec((k, tn), lambda i, j: (0, i)), out_shape=jax.ShapeDtypeStruct((k, n), F32),
                 sem=("parallel", "arbitrary"))(x, dy)


def _row_spec(width, col_block=0):
    return pl.BlockSpec((BLOCK, width), lambda i: (i, col_block))


def _x_spec():
    return pl.BlockSpec((None, BLOCK, D_MODEL), lambda i: (0, jnp.maximum(i - 1, 0), 0))


def prep(x, meta, g_pre):
    nb = x.shape[1] // BLOCK + 1

    def body(x_ref, meta_ref, g_ref, h_ref, u_ref):
        i = pl.program_id(0)

        @pl.when(i == 0)
        def _():
            h_ref[0:PAD_ROWS, :] = jnp.zeros((PAD_ROWS, D_MODEL), F32)
            h_ref[PAD_ROWS:BLOCK, :] = meta_ref[...]

        @pl.when(i > 0)
        def _():
            h_ref[...] = x_ref[...]

        u_ref[...] = _rms(h_ref[...], g_ref[...]).astype(BF16)

    return _call(body, name="prep", grid=(nb,), in_specs=[_x_spec(), _full((N_META, D_MODEL)), _full((1, D_MODEL))],
                 out_specs=[_row_spec(D_MODEL), _row_spec(D_MODEL)],
                 out_shape=[jax.ShapeDtypeStruct((nb * BLOCK, D_MODEL), F32),
                            jax.ShapeDtypeStruct((nb * BLOCK, D_MODEL), BF16)], sem=("parallel",))(x, meta, g_pre)


def prep_bwd(h, du, dres, g_pre):
    nb = h.shape[0] // BLOCK

    def body(h_ref, du_ref, dres_ref, g_ref, gx_ref, gm_ref, gg_ref):
        i = pl.program_id(0)
        _, vjp = jax.vjp(_rms, h_ref[...], g_ref[...])
        dh, dg = vjp(du_ref[...])

        @pl.when(i == 0)
        def _():
            gm_ref[...] = dh[PAD_ROWS:BLOCK, :]
            gg_ref[...] = dg

        @pl.when(i > 0)
        def _():
            gg_ref[...] += dg

        gx_ref[...] = dh + dres_ref[...]

    return _call(body, name="prep_bwd", grid=(nb,),
                 in_specs=[_row_spec(D_MODEL), _row_spec(D_MODEL), _row_spec(D_MODEL), _full((1, D_MODEL))],
                 out_specs=[_x_spec(), _full((N_META, D_MODEL)), _full((1, D_MODEL))],
                 out_shape=[jax.ShapeDtypeStruct((1, (nb - 1) * BLOCK, D_MODEL), F32),
                            jax.ShapeDtypeStruct((N_META, D_MODEL), F32), jax.ShapeDtypeStruct((1, D_MODEL), F32)],
                 sem=("arbitrary",))(h, du, dres, g_pre)


GROUP_W = SSM_INNER // SSM_GROUPS


def _gated_norm(y, z, g):
    t = y * _silu(z)
    return t * lax.rsqrt(jnp.mean(t * t, axis=-1, keepdims=True) + NORM_EPS) * g


def ssm_norm_fwd(y, proj, g_norm):
    nb = y.shape[0] // BLOCK
    spec = pl.BlockSpec((BLOCK, GROUP_W), lambda i, g: (i, g))

    def body(y_ref, z_ref, g_ref, o_ref):
        o_ref[...] = _gated_norm(y_ref[...], z_ref[...], g_ref[...]).astype(BF16)

    return _call(body, name="ssm_norm_fwd", grid=(nb, SSM_GROUPS),
                 in_specs=[spec, spec, pl.BlockSpec((1, GROUP_W), lambda i, g: (0, g))], out_specs=spec,
                 out_shape=jax.ShapeDtypeStruct(y.shape, BF16), sem=("parallel", "parallel"))(y, proj, g_norm)


def ssm_norm_bwd(dyn, y, proj, g_norm):
    nb = y.shape[0] // BLOCK
    spec = pl.BlockSpec((BLOCK, GROUP_W), lambda g, i: (i, g))
    gspec = pl.BlockSpec((1, GROUP_W), lambda g, i: (0, g))

    def body(d_ref, y_ref, z_ref, g_ref, dy_ref, dz_ref, dg_ref):
        _, vjp = jax.vjp(_gated_norm, y_ref[...], z_ref[...], g_ref[...])
        dy, dz, dg = vjp(d_ref[...])
        dy_ref[...] = dy
        dz_ref[...] = dz.astype(BF16)

        @pl.when(pl.program_id(1) == 0)
        def _():
            dg_ref[...] = dg

        @pl.when(pl.program_id(1) > 0)
        def _():
            dg_ref[...] += dg

    return _call(body, name="ssm_norm_bwd", grid=(SSM_GROUPS, nb), in_specs=[spec, spec, spec, gspec],
                 out_specs=[spec, spec, gspec],
                 out_shape=[jax.ShapeDtypeStruct(y.shape, F32), jax.ShapeDtypeStruct(y.shape, BF16),
                            jax.ShapeDtypeStruct((1, SSM_INNER), F32)],
                 sem=("parallel", "arbitrary"))(dyn, y, proj, g_norm)


def _merge(ga, gs, ya, ys):
    return jax.nn.sigmoid(ga) * ya + jax.nn.sigmoid(gs) * ys


GATE_ATT_BLOCK = SEG["gate_att"][2] // D_MODEL
GATE_SSM_BLOCK = SEG["gate_ssm"][2] // D_MODEL


def merge_fwd(proj, y_att, y_ssm):
    nb = y_att.shape[0] // BLOCK

    def body(ga_ref, gs_ref, ya_ref, ys_ref, o_ref):
        o_ref[...] = _merge(ga_ref[...], gs_ref[...], ya_ref[...], ys_ref[...]).astype(BF16)

    return _call(body, name="merge_fwd", grid=(nb,),
                 in_specs=[_row_spec(D_MODEL, GATE_ATT_BLOCK), _row_spec(D_MODEL, GATE_SSM_BLOCK), _row_spec(D_MODEL),
                           _row_spec(D_MODEL)],
                 out_specs=_row_spec(D_MODEL), out_shape=jax.ShapeDtypeStruct(y_att.shape, BF16),
                 sem=("parallel",))(proj, proj, y_att, y_ssm)


def merge_bwd(dm, proj, y_att, y_ssm):
    nb = y_att.shape[0] // BLOCK

    def body(d_ref, ga_ref, gs_ref, ya_ref, ys_ref, dga_ref, dgs_ref, dya_ref, dys_ref):
        _, vjp = jax.vjp(_merge, ga_ref[...], gs_ref[...], ya_ref[...], ys_ref[...])
        dga, dgs, dya, dys = vjp(d_ref[...])
        dga_ref[...] = dga.astype(BF16)
        dgs_ref[...] = dgs.astype(BF16)
        dya_ref[...] = dya.astype(BF16)
        dys_ref[...] = dys.astype(BF16)

    return _call(body, name="merge_bwd", grid=(nb,),
                 in_specs=[_row_spec(D_MODEL), _row_spec(D_MODEL, GATE_ATT_BLOCK), _row_spec(D_MODEL, GATE_SSM_BLOCK),
                           _row_spec(D_MODEL), _row_spec(D_MODEL)],
                 out_specs=[_row_spec(D_MODEL)] * 4, out_shape=[jax.ShapeDtypeStruct(y_att.shape, BF16)] * 4,
                 sem=("parallel",))(dm, proj, proj, y_att, y_ssm)


def final_loss(out, x, target, g_post):
    nb = out.shape[0] // BLOCK

    def body(out_ref, x_ref, t_ref, g_ref, do_ref, dres_ref, loss_ref, dg_ref):
        i = pl.program_id(0)

        @pl.when(i == 0)
        def _():
            do_ref[...] = jnp.zeros_like(do_ref)
            dres_ref[...] = jnp.zeros_like(dres_ref)
            loss_ref[...] = jnp.zeros_like(loss_ref)
            dg_ref[...] = jnp.zeros_like(dg_ref)

        @pl.when(i > 0)
        def _():
            n, vjp = jax.vjp(_rms, out_ref[...], g_ref[...])
            diff = x_ref[...] + n - t_ref[...]
            loss_ref[...] += 0.5 * jnp.sum(diff * diff) / D_MODEL
            dn = diff * (1.0 / D_MODEL)
            do, dg = vjp(dn)
            do_ref[...] = do.astype(BF16)
            dres_ref[...] = dn
            dg_ref[...] += dg

    return _call(body, name="final_loss", grid=(nb,),
                 in_specs=[_row_spec(D_MODEL), _x_spec(), _x_spec(), _full((1, D_MODEL))],
                 out_specs=[_row_spec(D_MODEL), _row_spec(D_MODEL), _full((8, LANES)), _full((1, D_MODEL))],
                 out_shape=[jax.ShapeDtypeStruct(out.shape, BF16), jax.ShapeDtypeStruct(out.shape, F32),
                            jax.ShapeDtypeStruct((8, LANES), F32), jax.ShapeDtypeStruct((1, D_MODEL), F32)],
                 sem=("arbitrary",))(out, x, target, g_post)


_NT = (((1,), (1,)), ((), ()))
ALIBI_SLOPES = tuple(2.0 ** (-8.0 * (h + 1) / ATT_Q_HEADS) for h in range(ATT_Q_HEADS))
KV_WIDTH = ATT_KV_HEADS * HEAD_DIM
Q_BLOCK = SEG["q"][2] // D_MODEL
Z_ATT_BLOCK = SEG["z_att"][2] // D_MODEL
K_BLOCK = SEG["k"][2] // KV_WIDTH
V_BLOCK = SEG["v"][2] // KV_WIDTH
META_ROW_BLOCK = PAD_ROWS // N_META


@jax.custom_vjp
def _swap_halves(x):
    return pltpu.roll(x, HEAD_DIM, 1)


_swap_halves.defvjp(lambda x: (pltpu.roll(x, HEAD_DIM, 1), None), lambda _, g: (pltpu.roll(g, HEAD_DIM, 1),))


def _both_halves(t, half):
    first = lax.broadcasted_iota(jnp.int32, t.shape, 1) < HEAD_DIM
    sw = _swap_halves(t)
    return jnp.where(first, t, sw) if half == 0 else jnp.where(first, sw, t)


def _attn_rows(q, z, kp, kc, vp, vc, km, vm, sinks, n):
    rows = ATT_GROUP * BLOCK
    i = lax.broadcasted_iota(jnp.int32, (rows, BLOCK), 0) & (BLOCK - 1)
    j = lax.broadcasted_iota(jnp.int32, (rows, BLOCK), 1)
    rel_c = (i - j).astype(F32)
    rel_p = rel_c + float(BLOCK)
    nv = jnp.zeros((rows, BLOCK), jnp.int32) + n
    ok_c = (i >= j) & (nv >= 1)
    ok_p = (j > i) & (nv >= 2)
    im = lax.broadcasted_iota(jnp.int32, (rows, N_META), 0) & (BLOCK - 1)
    jm = lax.broadcasted_iota(jnp.int32, (rows, N_META), 1)
    ok_m = ((jnp.zeros((rows, N_META), jnp.int32) + n) >= 1) | (im >= PAD_ROWS + jm)
    first = lax.broadcasted_iota(jnp.int32, (BLOCK, LANES), 1) < HEAD_DIM
    neg = -jnp.inf
    outs = []
    for kv in range(ATT_KV_HEADS):
        tile, half = divmod(kv, 2)
        lanes = slice(tile * LANES, (tile + 1) * LANES)
        kc2, kp2, km2 = (_both_halves(t[:, lanes], half).astype(BF16) for t in (kc, kp, km))
        vc2, vp2, vm2 = (_both_halves(t[:, lanes], half).astype(BF16) for t in (vc, vp, vm))
        qs, slope, sk = [], [], []
        for pair in range(ATT_GROUP // 2):
            c0 = (kv * ATT_GROUP + 2 * pair) * HEAD_DIM
            qp = q[:, c0:c0 + LANES] * HEAD_DIM ** -0.5
            qs += [jnp.where(first, qp, 0.0), jnp.where(first, 0.0, qp)]
        for g in range(ATT_GROUP):
            slope.append(jnp.full((BLOCK, 1), ALIBI_SLOPES[kv * ATT_GROUP + g], F32))
            sk.append(jnp.broadcast_to(sinks[kv * ATT_GROUP + g], (BLOCK, 1)))
        qs = jnp.concatenate(qs, axis=0).astype(BF16)
        slope = jnp.concatenate(slope, axis=0)
        sk = jnp.concatenate(sk, axis=0)
        sc = jnp.where(ok_c, lax.dot_general(qs, kc2, _NT, preferred_element_type=F32) - slope * rel_c, neg)
        sp = jnp.where(ok_p, lax.dot_general(qs, kp2, _NT, preferred_element_type=F32) - slope * rel_p, neg)
        sm = jnp.where(ok_m, lax.dot_general(qs, km2, _NT, preferred_element_type=F32), neg)
        mx = jnp.maximum(jnp.maximum(jnp.max(sc, axis=1, keepdims=True), jnp.max(sp, axis=1, keepdims=True)),
                         jnp.maximum(jnp.max(sm, axis=1, keepdims=True), sk))
        mx = lax.stop_gradient(mx)
        ec, ep, em, es = jnp.exp(sc - mx), jnp.exp(sp - mx), jnp.exp(sm - mx), jnp.exp(sk - mx)
        den = (es + jnp.sum(ec, axis=1, keepdims=True) + jnp.sum(ep, axis=1, keepdims=True)
               + jnp.sum(em, axis=1, keepdims=True))
        inv = 1.0 / den
        o = (jnp.dot((ec * inv).astype(BF16), vc2, preferred_element_type=F32)
             + jnp.dot((ep * inv).astype(BF16), vp2, preferred_element_type=F32)
             + jnp.dot((em * inv).astype(BF16), vm2, preferred_element_type=F32))
        for pair in range(ATT_GROUP // 2):
            r0 = 2 * pair * BLOCK
            outs.append(jnp.where(first, o[r0:r0 + BLOCK], o[r0 + BLOCK:r0 + 2 * BLOCK]))
    return jnp.concatenate(outs, axis=1) * _silu(z)


def _attn_specs(nb, steps_clamped):
    def blk(t):
        return jnp.minimum(t, nb - 1) if steps_clamped else t

    wide = lambda col: pl.BlockSpec((BLOCK, D_MODEL), lambda t: (blk(t), col))
    cur = lambda col: pl.BlockSpec((BLOCK, KV_WIDTH), lambda t: (blk(t), col))
    prev = lambda col: pl.BlockSpec((BLOCK, KV_WIDTH), lambda t: (jnp.maximum(blk(t) - 1, 0), col))
    meta = lambda col: pl.BlockSpec((N_META, KV_WIDTH), lambda t: (META_ROW_BLOCK, col))
    sinks = pl.BlockSpec((ATT_Q_HEADS, 1, 1), lambda t: (0, 0, 0))
    return [wide(Q_BLOCK), wide(Z_ATT_BLOCK), prev(K_BLOCK), cur(K_BLOCK), prev(V_BLOCK), cur(V_BLOCK),
            meta(K_BLOCK), meta(V_BLOCK), sinks]


def attn_fwd(proj, sinks):
    nb = proj.shape[0] // BLOCK

    def body(q_ref, z_ref, kp_ref, kc_ref, vp_ref, vc_ref, km_ref, vm_ref, sk_ref, o_ref):
        o_ref[...] = _attn_rows(q_ref[...], z_ref[...], kp_ref[...], kc_ref[...], vp_ref[...], vc_ref[...],
                                km_ref[...], vm_ref[...], tuple(sk_ref[h] for h in range(ATT_Q_HEADS)),
                                pl.program_id(0)).astype(BF16)

    return _call(body, name="attn_fwd", grid=(nb,), in_specs=_attn_specs(nb, False), out_specs=_row_spec(D_MODEL),
                 out_shape=jax.ShapeDtypeStruct((nb * BLOCK, D_MODEL), BF16), sem=("parallel",))(*([proj] * 8), sinks)


def attn_bwd(da, proj, sinks):
    nb = proj.shape[0] // BLOCK
    last = nb - 1
    wide = pl.BlockSpec((BLOCK, D_MODEL), lambda t: (jnp.minimum(t, last), 0))
    done = pl.BlockSpec((BLOCK, KV_WIDTH), lambda t: (jnp.maximum(t - 1, 0), 0))
    meta = _full((N_META, KV_WIDTH))
    par = _full((ATT_Q_HEADS, 1, 1))

    def body(da_ref, q_ref, z_ref, kp_ref, kc_ref, vp_ref, vc_ref, km_ref, vm_ref, sk_ref,
             dq_ref, dz_ref, dk_ref, dv_ref, dkm_ref, dvm_ref, dsk_ref, ck_ref, cv_ref):
        t = pl.program_id(0)

        @pl.when(t == 0)
        def _():
            ck_ref[...] = jnp.zeros_like(ck_ref)
            cv_ref[...] = jnp.zeros_like(cv_ref)
            dkm_ref[...] = jnp.zeros_like(dkm_ref)
            dvm_ref[...] = jnp.zeros_like(dvm_ref)
            dsk_ref[...] = jnp.zeros_like(dsk_ref)

        @pl.when(t < nb)
        def _():
            def f(q, z, kp, kc, vp, vc, km, vm, sk):
                return _attn_rows(q, z, kp, kc, vp, vc, km, vm, sk, t)

            _, vjp = jax.vjp(f, q_ref[...], z_ref[...], kp_ref[...], kc_ref[...], vp_ref[...], vc_ref[...],
                             km_ref[...], vm_ref[...], tuple(sk_ref[h] for h in range(ATT_Q_HEADS)))
            dq, dz, dkp, dkc, dvp, dvc, dkm, dvm, dsk = vjp(da_ref[...])
            dq_ref[...] = dq.astype(BF16)
            dz_ref[...] = dz.astype(BF16)
            for h in range(ATT_Q_HEADS):
                dsk_ref[h] += dsk[h]
            dk_ref[...] = ck_ref[...] + dkp
            dv_ref[...] = cv_ref[...] + dvp
            ck_ref[...] = dkc
            cv_ref[...] = dvc
            dkm_ref[...] += dkm
            dvm_ref[...] += dvm

        @pl.when(t == nb)
        def _():
            dk_ref[...] = ck_ref[...]
            dv_ref[...] = cv_ref[...]

    rows = nb * BLOCK
    return _call(body, name="attn_bwd", grid=(nb + 1,), in_specs=[wide] + _attn_specs(nb, True),
                 out_specs=[wide, wide, done, done, meta, meta, par],
                 out_shape=[jax.ShapeDtypeStruct((rows, D_MODEL), BF16), jax.ShapeDtypeStruct((rows, D_MODEL), BF16),
                            jax.ShapeDtypeStruct((rows, KV_WIDTH), F32), jax.ShapeDtypeStruct((rows, KV_WIDTH), F32),
                            jax.ShapeDtypeStruct((N_META, KV_WIDTH), F32), jax.ShapeDtypeStruct((N_META, KV_WIDTH), F32),
                            jax.ShapeDtypeStruct(sinks.shape, F32)],
                 scratch=[pltpu.VMEM((BLOCK, KV_WIDTH), F32), pltpu.VMEM((BLOCK, KV_WIDTH), F32)],
                 sem=("arbitrary",))(da, *([proj] * 8), sinks)


XBC_BLOCK0 = SEG["xbc"][2] // D_MODEL
CONV_COL_BLOCKS = CONV_DIM // D_MODEL
DT_TILE = SEG["dt"][2] // LANES


def _shift_rows(cur, prev, j, row):
    if j == 0:
        return cur
    return jnp.where(row >= j, pltpu.roll(cur, j, 0), pltpu.roll(prev, j, 0))


def _conv_pre(cur, prev, w_ref, b_ref, row):
    pre = b_ref[...] + w_ref[CONV_WIDTH - 1:CONV_WIDTH, :] * cur
    for k in range(CONV_WIDTH - 1):
        pre = pre + w_ref[k:k + 1, :] * _shift_rows(cur, prev, CONV_WIDTH - 1 - k, row)
    return pre


def conv_fwd(proj, conv_w, conv_b):
    nb = proj.shape[0] // BLOCK
    cur = pl.BlockSpec((BLOCK, D_MODEL), lambda j, i: (i, XBC_BLOCK0 + j))
    prev = pl.BlockSpec((BLOCK, D_MODEL), lambda j, i: (jnp.maximum(i - 1, 0), XBC_BLOCK0 + j))

    def body(c_ref, p_ref, w_ref, b_ref, o_ref):
        i = pl.program_id(1)
        row = lax.broadcasted_iota(jnp.int32, (BLOCK, D_MODEL), 0)
        prevv = p_ref[...] * jnp.where(i > 0, 1.0, 0.0)
        pre = _conv_pre(c_ref[...], prevv, w_ref, b_ref, row)
        valid = jnp.maximum((row >= PAD_ROWS).astype(F32), jnp.where(i > 0, 1.0, 0.0))
        o_ref[...] = _silu(pre) * valid

    return _call(body, name="conv_fwd", grid=(CONV_COL_BLOCKS, nb),
                 in_specs=[cur, prev, pl.BlockSpec((CONV_WIDTH, D_MODEL), lambda j, i: (0, j)),
                           pl.BlockSpec((1, D_MODEL), lambda j, i: (0, j))],
                 out_specs=pl.BlockSpec((BLOCK, D_MODEL), lambda j, i: (i, j)),
                 out_shape=jax.ShapeDtypeStruct((nb * BLOCK, CONV_DIM), F32),
                 sem=("parallel", "parallel"))(proj, proj, conv_w, conv_b)


def conv_bwd(dxbc, proj, conv_w, conv_b):
    nb = proj.shape[0] // BLOCK
    last = nb - 1
    cur = pl.BlockSpec((BLOCK, D_MODEL), lambda j, i: (i, XBC_BLOCK0 + j))
    prev = pl.BlockSpec((BLOCK, D_MODEL), lambda j, i: (jnp.maximum(i - 1, 0), XBC_BLOCK0 + j))
    nxt = pl.BlockSpec((BLOCK, D_MODEL), lambda j, i: (jnp.minimum(i + 1, last), XBC_BLOCK0 + j))
    dcur = pl.BlockSpec((BLOCK, D_MODEL), lambda j, i: (i, j))
    dnxt = pl.BlockSpec((BLOCK, D_MODEL), lambda j, i: (jnp.minimum(i + 1, last), j))
    wspec = pl.BlockSpec((CONV_WIDTH, D_MODEL), lambda j, i: (0, j))
    bspec = pl.BlockSpec((1, D_MODEL), lambda j, i: (0, j))

    def body(dc_ref, dn_ref, c_ref, p_ref, n_ref, w_ref, b_ref, du_ref, dw_ref, db_ref):
        i = pl.program_id(1)
        row = lax.broadcasted_iota(jnp.int32, (BLOCK, D_MODEL), 0)
        curv = c_ref[...]
        prevv = p_ref[...] * jnp.where(i > 0, 1.0, 0.0)

        def dpre_of(pre, d, valid):
            s = jax.nn.sigmoid(pre)
            return d * valid * (s * (1.0 + pre * (1.0 - s)))

        valid = jnp.maximum((row >= PAD_ROWS).astype(F32), jnp.where(i > 0, 1.0, 0.0))
        dp_c = dpre_of(_conv_pre(curv, prevv, w_ref, b_ref, row), dc_ref[...], valid)
        has_next = jnp.where(i < last, 1.0, 0.0)
        dp_n = dpre_of(_conv_pre(n_ref[...], curv, w_ref, b_ref, row), dn_ref[...], has_next)
        du = w_ref[CONV_WIDTH - 1:CONV_WIDTH, :] * dp_c
        for j in range(1, CONV_WIDTH):
            up = jnp.where(row < BLOCK - j, pltpu.roll(dp_c, BLOCK - j, 0), pltpu.roll(dp_n, BLOCK - j, 0))
            du = du + w_ref[CONV_WIDTH - 1 - j:CONV_WIDTH - j, :] * up
        du_ref[...] = du.astype(BF16)

        @pl.when(i == 0)
        def _():
            dw_ref[...] = jnp.zeros_like(dw_ref)
            db_ref[...] = jnp.zeros_like(db_ref)

        for k in range(CONV_WIDTH):
            dw_ref[k:k + 1, :] += jnp.sum(dp_c * _shift_rows(curv, prevv, CONV_WIDTH - 1 - k, row), axis=0,
                                          keepdims=True)
        db_ref[...] += jnp.sum(dp_c, axis=0, keepdims=True)

    return _call(body, name="conv_bwd", grid=(CONV_COL_BLOCKS, nb),
                 in_specs=[dcur, dnxt, cur, prev, nxt, wspec, bspec], out_specs=[dcur, wspec, bspec],
                 out_shape=[jax.ShapeDtypeStruct((nb * BLOCK, CONV_DIM), BF16),
                            jax.ShapeDtypeStruct((CONV_WIDTH, CONV_DIM), F32), jax.ShapeDtypeStruct((1, CONV_DIM), F32)],
                 sem=("parallel", "arbitrary"))(dxbc, dxbc, proj, proj, proj, conv_w, conv_b)


def _head_expand():
    e = np.zeros((LANES, SSM_INNER), np.float32)
    for h in range(SSM_HEADS):
        e[h, h * HEAD_DIM:(h + 1) * HEAD_DIM] = 1.0
    return jnp.asarray(e)


def _softplus(x):
    return jnp.maximum(x, 0.0) + jnp.log(1.0 + jnp.exp(-jnp.abs(x)))


def dt_fwd(proj, expand, bias_x):
    nb = proj.shape[0] // BLOCK

    def body(t_ref, e_ref, b_ref, o_ref):
        raw = jnp.dot(t_ref[...], e_ref[...], precision=HI, preferred_element_type=F32)
        o_ref[...] = _softplus(raw + b_ref[...])

    return _call(body, name="dt_fwd", grid=(nb,),
                 in_specs=[_row_spec(LANES, DT_TILE), _full((LANES, SSM_INNER)), _full((1, SSM_INNER))],
                 out_specs=_row_spec(SSM_INNER), out_shape=jax.ShapeDtypeStruct((nb * BLOCK, SSM_INNER), F32),
                 sem=("parallel",))(proj, expand, bias_x)


def dt_bwd(ddt_x, proj, expand, bias_x):
    nb = proj.shape[0] // BLOCK

    def body(d_ref, t_ref, e_ref, b_ref, o_ref, db_ref):
        raw = jnp.dot(t_ref[...], e_ref[...], precision=HI, preferred_element_type=F32)
        draw = d_ref[...] * jax.nn.sigmoid(raw + b_ref[...])
        dt = lax.dot_general(draw, e_ref[...], _NT, precision=HI, preferred_element_type=F32)
        o_ref[...] = dt.astype(BF16)

        @pl.when(pl.program_id(0) == 0)
        def _():
            db_ref[...] = jnp.zeros_like(db_ref)

        db_ref[...] += jnp.sum(dt, axis=0, keepdims=True)

    return _call(body, name="dt_bwd", grid=(nb,),
                 in_specs=[_row_spec(SSM_INNER), _row_spec(LANES, DT_TILE), _full((LANES, SSM_INNER)),
                           _full((1, SSM_INNER))],
                 out_specs=[_row_spec(LANES), _full((1, LANES))],
                 out_shape=[jax.ShapeDtypeStruct((nb * BLOCK, LANES), BF16), jax.ShapeDtypeStruct((1, LANES), F32)],
                 sem=("arbitrary",))(ddt_x, proj, expand, bias_x)


def _ssd_group(xs, dtx, bg, cg, alog, dsk, state):
    l = lax.broadcasted_iota(jnp.int32, (BLOCK, BLOCK), 0)
    s = lax.broadcasted_iota(jnp.int32, (BLOCK, BLOCK), 1)
    causal = l >= s
    first_head = s < HEAD_DIM
    a = dtx * (-jnp.exp(alog))
    cs = jnp.dot(causal.astype(F32), a, precision=HI, preferred_element_type=F32)
    tot = jnp.sum(a, axis=0, keepdims=True)
    bb, cb16 = bg.astype(BF16), cg.astype(BF16)
    cb = lax.dot_general(cb16, bb, _NT, preferred_element_type=F32)
    xr = xs * dtx
    y_diag = []
    for p in range(GROUP_W // LANES):
        lanes = slice(p * LANES, (p + 1) * LANES)
        c_pair = cs[:, lanes]
        c_swap = _swap_halves(c_pair)
        m = []
        for c_head in (jnp.where(first_head, c_pair, c_swap), jnp.where(first_head, c_swap, c_pair)):
            m.append(cb * jnp.exp(jnp.where(causal, c_head - c_head.T, -jnp.inf)))
        x_pair = xr[:, lanes]
        x_diag = jnp.concatenate([jnp.where(first_head, x_pair, 0.0), jnp.where(first_head, 0.0, x_pair)], axis=0)
        y_diag.append(jnp.dot(jnp.concatenate(m, axis=1).astype(BF16), x_diag.astype(BF16),
                              preferred_element_type=F32))
    st = lax.dot_general(bb, (xr * jnp.exp(tot - cs)).astype(BF16), (((0,), (0,)), ((), ())),
                         preferred_element_type=F32)
    new_state = state * jnp.exp(tot) + st
    y_off = jnp.dot(cb16, state.astype(BF16), preferred_element_type=F32) * jnp.exp(cs)
    return jnp.concatenate(y_diag, axis=1) + y_off + dsk * xs, new_state


B_TILE0 = SSM_INNER // LANES
C_TILE0 = B_TILE0 + SSM_GROUPS


def ssd_fwd(xbc, dt_x, alog_x, dsk_x):
    nb = xbc.shape[0] // BLOCK
    gspec = pl.BlockSpec((BLOCK, GROUP_W), lambda g, c: (c, g))
    pspec = pl.BlockSpec((1, GROUP_W), lambda g, c: (0, g))

    def body(x_ref, dt_ref, b_ref, c_ref, al_ref, dk_ref, y_ref, sp_ref, st_ref):
        @pl.when(pl.program_id(1) == 0)
        def _():
            st_ref[...] = jnp.zeros_like(st_ref)

        state = st_ref[...]
        sp_ref[...] = state
        y_ref[...], st_ref[...] = _ssd_group(x_ref[...], dt_ref[...], b_ref[...], c_ref[...], al_ref[...],
                                             dk_ref[...], state)

    return _call(body, name="ssd_fwd", grid=(SSM_GROUPS, nb),
                 in_specs=[gspec, gspec, pl.BlockSpec((BLOCK, SSM_STATE), lambda g, c: (c, B_TILE0 + g)),
                           pl.BlockSpec((BLOCK, SSM_STATE), lambda g, c: (c, C_TILE0 + g)), pspec, pspec],
                 out_specs=[gspec, pl.BlockSpec((None, SSM_STATE, GROUP_W), lambda g, c: (c, 0, g))],
                 out_shape=[jax.ShapeDtypeStruct((nb * BLOCK, SSM_INNER), F32),
                            jax.ShapeDtypeStruct((nb, SSM_STATE, SSM_INNER), F32)],
                 scratch=[pltpu.VMEM((SSM_STATE, GROUP_W), F32)],
                 sem=("parallel", "arbitrary"))(xbc, dt_x, xbc, xbc, alog_x, dsk_x)


def ssd_bwd(dy, xbc, dt_x, alog_x, dsk_x, states):
    nb = xbc.shape[0] // BLOCK
    last = nb - 1
    gspec = pl.BlockSpec((BLOCK, GROUP_W), lambda g, c: (last - c, g))
    pspec = pl.BlockSpec((1, GROUP_W), lambda g, c: (0, g))
    nspec = pl.BlockSpec((BLOCK, SSM_STATE), lambda g, c: (last - c, g))

    def body(dy_ref, x_ref, dt_ref, b_ref, c_ref, al_ref, dk_ref, sp_ref,
             dx_ref, ddt_ref, db_ref, dc_ref, dal_ref, ddk_ref, ds_ref):
        @pl.when(pl.program_id(1) == 0)
        def _():
            ds_ref[...] = jnp.zeros_like(ds_ref)
            dal_ref[...] = jnp.zeros_like(dal_ref)
            ddk_ref[...] = jnp.zeros_like(ddk_ref)

        _, vjp = jax.vjp(_ssd_group, x_ref[...], dt_ref[...], b_ref[...], c_ref[...], al_ref[...], dk_ref[...],
                         sp_ref[...])
        dx_ref[...], ddt_ref[...], db_ref[...], dc_ref[...], dal, ddk, ds_ref[...] = vjp((dy_ref[...], ds_ref[...]))
        dal_ref[...] += dal
        ddk_ref[...] += ddk

    return _call(body, name="ssd_bwd", grid=(SSM_GROUPS, nb),
                 in_specs=[gspec, gspec, gspec, pl.BlockSpec((BLOCK, SSM_STATE), lambda g, c: (last - c, B_TILE0 + g)),
                           pl.BlockSpec((BLOCK, SSM_STATE), lambda g, c: (last - c, C_TILE0 + g)), pspec, pspec,
                           pl.BlockSpec((None, SSM_STATE, GROUP_W), lambda g, c: (last - c, 0, g))],
                 out_specs=[gspec, gspec, nspec, nspec, pspec, pspec],
                 out_shape=[jax.ShapeDtypeStruct((nb * BLOCK, SSM_INNER), F32),
                            jax.ShapeDtypeStruct((nb * BLOCK, SSM_INNER), F32),
                            jax.ShapeDtypeStruct((nb * BLOCK, SSM_GROUPS * SSM_STATE), F32),
                            jax.ShapeDtypeStruct((nb * BLOCK, SSM_GROUPS * SSM_STATE), F32),
                            jax.ShapeDtypeStruct((1, SSM_INNER), F32), jax.ShapeDtypeStruct((1, SSM_INNER), F32)],
                 scratch=[pltpu.VMEM((SSM_STATE, GROUP_W), F32)],
                 sem=("parallel", "arbitrary"))(dy, xbc, dt_x, xbc, xbc, alog_x, dsk_x, states)


SLAB_ROWS = 24
SLAB_META_ROW = 8


def pack_small(dcw, dcb, dgpre, dgpost, ddtb, dalog_x, ddsk_x, dsinks, dgn, dmeta, expand):
    def body(cw, cb, gpre, gpost, dtb, al, dk, sk, gn, meta, e_ref, o_ref):
        def per_head(v):
            rows = jnp.broadcast_to(v[...], (8, SSM_INNER))
            return lax.dot_general(rows, e_ref[...], _NT, precision=HI, preferred_element_type=F32)[0:1]

        o_ref[...] = jnp.zeros_like(o_ref)
        o_ref[0:CONV_WIDTH, :] = cw[...]
        o_ref[4:5, :] = cb[...]
        o_ref[5:6, 0:1024] = gpre[...]
        o_ref[5:6, 1024:2048] = gpost[...]
        o_ref[5:6, 2048:2176] = dtb[...]
        o_ref[5:6, 2176:2304] = per_head(al)
        o_ref[5:6, 2304:2432] = per_head(dk)
        o_ref[5:6, 2432:2560] = sk[...]
        o_ref[6:7, 0:SSM_INNER] = gn[...]
        o_ref[SLAB_META_ROW:SLAB_META_ROW + N_META, 0:D_MODEL] = meta[...]

    args = (dcw, dcb, dgpre, dgpost, ddtb, dalog_x, ddsk_x, dsinks, dgn, dmeta, expand)
    return _call(body, name="pack_small", in_specs=[_full(a.shape) for a in args],
                 out_specs=_full((SLAB_ROWS, CONV_DIM)), out_shape=jax.ShapeDtypeStruct((SLAB_ROWS, CONV_DIM), F32))(*args)


def _lane_tile(v):
    return jnp.pad(v, ((0, 0), (0, LANES - v.shape[1])))


def kernel(x, meta_tokens, g_pre, w_in, conv_w, conv_b, dt_bias, a_log, d_skip, attn_sinks, g_ssm_norm, w_out_att, w_out_ssm, w_out, g_post, loss_target, m_meta_tokens, m_g_pre, m_w_in, m_conv_w, m_conv_b, m_dt_bias, m_a_log, m_d_skip, m_attn_sinks, m_g_ssm_norm, m_w_out_att, m_w_out_ssm, m_w_out, m_g_post, v_meta_tokens, v_g_pre, v_w_in, v_conv_w, v_conv_b, v_dt_bias, v_a_log, v_d_skip, v_attn_sinks, v_g_ssm_norm, v_w_out_att, v_w_out_ssm, v_w_out, v_g_post):
    seq = x.shape[1]
    rows = seq + BLOCK
    chip = _chip_index()

    gathered = exchange(
        "gather_weights",
        [pack_w_in(w_in[0]), w_out_att[0].astype(BF16), w_out_ssm[0].astype(BF16), w_out[0].astype(BF16),
         conv_w[0], meta_tokens],
        [False] * 6, masks=(4, 2, 6), slot_kind="chip", nslots=4)
    w_all = unpack_w_in(gathered[0])
    woa = gathered[1].reshape(D_MODEL, D_MODEL)
    wos = gathered[2].reshape(SSM_INNER, D_MODEL)
    wo = gathered[3].reshape(D_MODEL, D_MODEL)
    cw_full = gathered[4].transpose(1, 0, 2).reshape(CONV_WIDTH, CONV_DIM)
    meta_full = gathered[5].transpose(1, 0, 2).reshape(N_META, D_MODEL)

    h, u = prep(x, meta_full, g_pre)
    proj = mm_nn("in_proj", u, w_all)

    sinks3 = attn_sinks.reshape(ATT_Q_HEADS, 1, 1)
    a_att = attn_fwd(proj, sinks3)
    y_att = mm_nn("att_out", a_att, woa)

    xbc = conv_fwd(proj, cw_full, conv_b)
    expand = _head_expand()
    bias_x = jnp.repeat(dt_bias, HEAD_DIM, axis=1)
    dt_x = dt_fwd(proj, expand, bias_x)
    alog_x = jnp.repeat(a_log, HEAD_DIM, axis=1)
    dsk_x = jnp.repeat(d_skip, HEAD_DIM, axis=1)
    y_ssd, states = ssd_fwd(xbc, dt_x, alog_x, dsk_x)
    yn = ssm_norm_fwd(y_ssd, proj, g_ssm_norm)
    y_ssm = mm_nn("ssm_out", yn, wos)

    merged = merge_fwd(proj, y_att, y_ssm)
    out = mm_nn("out_proj", merged, wo)
    dout, dres, loss_tile, dg_post = final_loss(out, x, loss_target, g_post)
    loss = lax.psum(loss_tile[0, 0], ("x", "y", "c"))

    dmerged = mm_nt("out_proj_dx", dout, wo)
    dwo = mm_tn("out_proj_dw", merged, dout)
    dga, dgs, dy_att, dy_ssm = merge_bwd(dmerged, proj, y_att, y_ssm)

    da_att = mm_nt("att_out_dx", dy_att, woa)
    dwoa = mm_tn("att_out_dw", a_att, dy_att)
    dq, dz_att, dk, dv, dkmeta, dvmeta, dsinks3 = attn_bwd(da_att, proj, sinks3)
    dk = dk.at[PAD_ROWS:BLOCK].add(dkmeta).astype(BF16)
    dv = dv.at[PAD_ROWS:BLOCK].add(dvmeta).astype(BF16)

    dyn = mm_nt("ssm_out_dx", dy_ssm, wos)
    dwos = mm_tn("ssm_out_dw", yn, dy_ssm)
    dy_ssd, dz_ssm, dgn = ssm_norm_bwd(dyn, y_ssd, proj, g_ssm_norm)
    dxs, ddt_x, dbg, dcg, dalog_x, ddsk_x = ssd_bwd(dy_ssd, xbc, dt_x, alog_x, dsk_x, states)
    dxbc = jnp.concatenate([dxs, dbg, dcg], axis=1)
    dxbc_raw, dcw, dcb = conv_bwd(dxbc, proj, cw_full, conv_b)
    ddt_tile, ddtb_tile = dt_bwd(ddt_x, proj, expand, bias_x)

    dproj = jnp.concatenate([dz_ssm, dxbc_raw, dq, dz_att, dga, dgs, dk, dv, ddt_tile], axis=1)
    du = mm_nt("in_proj_dx", dproj, w_all)
    dw_all = mm_tn("in_proj_dw", u, dproj)
    grad_x, dmeta, dg_pre = prep_bwd(h, du, dres, g_pre)

    slab = pack_small(dcw, dcb, dg_pre, dg_post, ddtb_tile, dalog_x, ddsk_x,
                      _lane_tile(dsinks3.reshape(1, ATT_Q_HEADS)), dgn, dmeta, expand)

    def pieces(g):
        return g.astype(BF16).reshape(4, 2, g.shape[0] // 8, g.shape[1])

    sent = exchange(
        "reduce_grads",
        [pack_grad_w_in(dw_all).reshape(4, 2, D_MODEL // 2, PACK_W), pieces(dwoa), pieces(dwos), pieces(dwo), slab],
        [True, True, True, True, False], masks=(1, 2, 3, 4, 5, 6, 7), slot_kind="dev", nslots=8)
    halves = [sum_slots("sum_" + nm, r) for nm, r in zip(("w_in", "w_out_att", "w_out_ssm", "w_out"), sent[:4])]
    small = sum_slots("sum_small", sent[4])
    full = exchange("share_halves", halves, [False] * 4, masks=(1,), slot_kind="core", nslots=2)
    g_w_in_packed, g_woa, g_wos, g_wo = [f.reshape(2 * f.shape[1], f.shape[2]) for f in full]

    g_w_in, d_w_in, nm_w_in, nv_w_in = adamw_w_in(g_w_in_packed, w_in[0], m_w_in[0], v_w_in[0])
    d_woa, nm_woa, nv_woa = adamw_rows("adamw_w_out_att", g_woa, w_out_att[0], m_w_out_att[0], v_w_out_att[0])
    d_wos, nm_wos, nv_wos = adamw_rows("adamw_w_out_ssm", g_wos, w_out_ssm[0], m_w_out_ssm[0], v_w_out_ssm[0])
    d_wo, nm_wo, nv_wo = adamw_rows("adamw_w_out", g_wo, w_out[0], m_w_out[0], v_w_out[0])

    cw_cols = CONV_DIM // 4
    meta_cols = D_MODEL // 4
    g_small = {
        "meta_tokens": lax.dynamic_slice(small, (SLAB_META_ROW, chip * meta_cols), (N_META, meta_cols)),
        "g_pre": small[5:6, 0:1024],
        "conv_w": lax.dynamic_slice(small, (0, chip * cw_cols), (CONV_WIDTH, cw_cols)),
        "conv_b": small[4:5, :],
        "dt_bias": small[5:6, 2048:2048 + SSM_HEADS],
        "a_log": small[5:6, 2176:2176 + SSM_HEADS],
        "d_skip": small[5:6, 2304:2304 + SSM_HEADS],
        "attn_sinks": small[5:6, 2432:2432 + ATT_Q_HEADS],
        "g_ssm_norm": small[6:7, 0:SSM_INNER],
        "g_post": small[5:6, 1024:2048],
    }
    names = list(g_small)
    w_small = dict(meta_tokens=meta_tokens, g_pre=g_pre, conv_w=conv_w[0], conv_b=conv_b, dt_bias=dt_bias, a_log=a_log,
                   d_skip=d_skip, attn_sinks=attn_sinks, g_ssm_norm=g_ssm_norm, g_post=g_post)
    m_small = dict(meta_tokens=m_meta_tokens, g_pre=m_g_pre, conv_w=m_conv_w[0], conv_b=m_conv_b, dt_bias=m_dt_bias,
                   a_log=m_a_log, d_skip=m_d_skip, attn_sinks=m_attn_sinks, g_ssm_norm=m_g_ssm_norm, g_post=m_g_post)
    v_small = dict(meta_tokens=v_meta_tokens, g_pre=v_g_pre, conv_w=v_conv_w[0], conv_b=v_conv_b, dt_bias=v_dt_bias,
                   a_log=v_a_log, d_skip=v_d_skip, attn_sinks=v_attn_sinks, g_ssm_norm=v_g_ssm_norm, g_post=v_g_post)
    upd = dict(zip(names, adamw_small([g_small[k] for k in names], [w_small[k] for k in names],
                                      [m_small[k] for k in names], [v_small[k] for k in names])))

    lead = {"conv_w"}

    def shaped(name, a):
        return a[None] if name in lead else a

    grads = dict(g_small, w_in=g_w_in, w_out_att=g_woa, w_out_ssm=g_wos, w_out=g_wo)
    deltas = dict({k: upd[k][0] for k in names}, w_in=d_w_in, w_out_att=d_woa, w_out_ssm=d_wos, w_out=d_wo)
    new_m = dict({k: upd[k][1] for k in names}, w_in=nm_w_in, w_out_att=nm_woa, w_out_ssm=nm_wos, w_out=nm_wo)
    new_v = dict({k: upd[k][2] for k in names}, w_in=nv_w_in, w_out_att=nv_woa, w_out_ssm=nv_wos, w_out=nv_wo)
    lead |= {"w_in", "w_out_att", "w_out_ssm", "w_out"}
    order = ["meta_tokens", "g_pre", "w_in", "conv_w", "conv_b", "dt_bias", "a_log", "d_skip", "attn_sinks",
             "g_ssm_norm", "w_out_att", "w_out_ssm", "w_out", "g_post"]
    outs = [loss, grad_x]
    for group in (grads, deltas, new_m, new_v):
        outs += [shaped(k, group[k]) for k in order]
    return tuple(outs)
```

```python
import functools

import numpy as np
import jax
import jax.numpy as jnp
from jax import lax
from jax.experimental import pallas as pl
from jax.experimental.pallas import tpu as pltpu

F32 = jnp.float32
BF16 = jnp.bfloat16
HI = lax.Precision.HIGHEST

D_MODEL = 1024
N_META = 16
BLOCK = 128
PAD_ROWS = BLOCK - N_META
NORM_EPS = 1e-6
HEAD_DIM = 64
ATT_Q_HEADS = 16
ATT_KV_HEADS = 4
ATT_GROUP = 4
SSM_INNER = 2048
SSM_HEADS = 32
SSM_GROUPS = 4
SSM_HEADS_PER_GROUP = 8
SSM_STATE = 128
CONV_WIDTH = 4
CONV_DIM = 3072
LANES = 128

ADAM_LR = 0.001
ADAM_B1 = 0.9
ADAM_B2 = 0.999
ADAM_EPS = 1e-08
ADAM_WD = 0.01
ADAM_STEP = 10

VMEM_LIMIT = 48 * 1024 * 1024

SHARD_W = 2440
PACK_W = 2560
SHARD_STRIDE = 2432
N_ALIGNED = 9856
SEG = {
    "q": (0, 1024, 5120), "k": (1024, 256, 9216), "v": (1280, 256, 9472), "z_att": (1536, 1024, 6144),
    "z_ssm": (2560, 2048, 0), "xbc": (4608, 3072, 2048), "dt": (7680, 128, 9728),
    "gate_att": (7808, 1024, 7168), "gate_ssm": (8832, 1024, 8192),
}
DT_STORED_START = 7680
DT_PAD = LANES - SSM_HEADS


def _act_col(aligned_col):
    for a0, w, p0 in SEG.values():
        if a0 <= aligned_col < a0 + w:
            return p0 + aligned_col - a0
    raise ValueError(aligned_col)


def _call(body, *, name, out_shape, in_specs, out_specs, grid=(), scratch=(), sem=None, aliases=None):
    return pl.pallas_call(
        body, out_shape=out_shape, grid=grid, in_specs=in_specs, out_specs=out_specs, scratch_shapes=list(scratch),
        name=name, input_output_aliases=aliases or {},
        compiler_params=pltpu.CompilerParams(dimension_semantics=sem, vmem_limit_bytes=VMEM_LIMIT))


def _full(shape):
    n = len(shape)
    return pl.BlockSpec(shape, lambda *_: (0,) * n)


def _chip_index():
    return lax.axis_index("x") * 2 + lax.axis_index("y")


def _silu(z):
    return z * jax.nn.sigmoid(z)


def _rms(x, g):
    return x * lax.rsqrt(jnp.mean(x * x, axis=-1, keepdims=True) + NORM_EPS) * g


def _peer(mask):
    x, y, c = lax.axis_index("x"), lax.axis_index("y"), lax.axis_index("c")
    return ((1 - x) if mask & 4 else x, (1 - y) if mask & 2 else y, (1 - c) if mask & 1 else c)


def _me():
    return lax.axis_index("x"), lax.axis_index("y"), lax.axis_index("c")


def _chip_of(dev):
    return 2 * dev[0] + dev[1]


CHIP_MASKS = (4, 2, 6)
ALL_MASKS = (1, 2, 3, 4, 5, 6, 7)
SIBLING_MASK = (1,)


class DirectExchange:
    def __init__(self, arrays, pieces, masks, slot_kind, nslots):
        self.arrays, self.pieces, self.masks, self.slot_kind = list(arrays), list(pieces), masks, slot_kind
        n, nk = len(arrays), len(masks)
        shapes = [a.shape if p is None else p[1] for a, p in zip(arrays, pieces)]
        self.out_shape = [jax.ShapeDtypeStruct((nslots,) + tuple(s), a.dtype) for s, a in zip(shapes, arrays)]
        self.sems = [pltpu.SemaphoreType.DMA((n * nk,)), pltpu.SemaphoreType.DMA((n * nk,)),
                     pltpu.SemaphoreType.DMA((n,))]
        self.has_mid = False

    def _copies(self, ins, outs, sems):
        send_sems, recv_sems, local_sems = sems
        me = _me()
        slot = {"chip": _chip_of(me), "dev": 4 * me[0] + 2 * me[1] + me[2], "core": me[2]}[self.slot_kind]
        nk = len(self.masks)

        def piece(a, dev):
            return ins[a] if self.pieces[a] is None else self.pieces[a][0](ins[a], dev)

        local = [pltpu.make_async_copy(piece(a, me), outs[a].at[slot], local_sems.at[a]) for a in range(len(ins))]
        remote = []
        for a in range(len(ins)):
            for ki, mask in enumerate(self.masks):
                dev = _peer(mask)
                remote.append(pltpu.make_async_remote_copy(
                    src_ref=piece(a, dev), dst_ref=outs[a].at[slot], send_sem=send_sems.at[a * nk + ki],
                    recv_sem=recv_sems.at[a * nk + ki], device_id=dev, device_id_type=pl.DeviceIdType.MESH))
        return local, remote

    def start(self, ins, outs, sems):
        local, remote = self._copies(ins, outs, sems)
        for cp in local + remote:
            cp.start()

    def finish(self, ins, outs, sems):
        local, remote = self._copies(ins, outs, sems)
        for cp in remote + local:
            cp.wait()


class TwoLevelGather:
    def __init__(self, arrays):
        self.arrays = list(arrays)
        n, nk = len(arrays), len(CHIP_MASKS)
        self.out_shape = [jax.ShapeDtypeStruct((4,) + a.shape, a.dtype) for a in arrays]
        self.sems = [pltpu.SemaphoreType.DMA((n * nk,)) for _ in range(4)] + [pltpu.SemaphoreType.DMA((n,))]
        self.has_mid = True

    def _copies(self, ins, outs, sems):
        ici_send, ici_recv, fwd_send, fwd_recv, local_sems = sems
        me = _me()
        sibling = _peer(1)
        nk = len(CHIP_MASKS)
        local, ici, fwd = [], [], []
        for a in range(len(ins)):
            half = ins[a].shape[0] // 2
            mine = pl.ds(me[2] * half, half)
            local.append(pltpu.make_async_copy(ins[a], outs[a].at[_chip_of(me)], local_sems.at[a]))
            for ki, mask in enumerate(CHIP_MASKS):
                dev = _peer(mask)
                k = a * nk + ki
                ici.append(pltpu.make_async_remote_copy(
                    src_ref=ins[a].at[mine], dst_ref=outs[a].at[_chip_of(me), mine], send_sem=ici_send.at[k],
                    recv_sem=ici_recv.at[k], device_id=dev, device_id_type=pl.DeviceIdType.MESH))
                arrived = outs[a].at[_chip_of(dev), mine]
                fwd.append(pltpu.make_async_remote_copy(
                    src_ref=arrived, dst_ref=arrived, send_sem=fwd_send.at[k], recv_sem=fwd_recv.at[k],
                    device_id=sibling, device_id_type=pl.DeviceIdType.MESH))
        return local, ici, fwd

    def start(self, ins, outs, sems):
        local, ici, _ = self._copies(ins, outs, sems)
        for cp in local + ici:
            cp.start()

    def mid(self, ins, outs, sems):
        _, ici, fwd = self._copies(ins, outs, sems)
        for arrival, onward in zip(ici, fwd):
            arrival.wait_recv()
            onward.start()

    def finish(self, ins, outs, sems):
        local, ici, fwd = self._copies(ins, outs, sems)
        for cp in fwd:
            cp.wait_recv()
        for cp in ici + fwd:
            cp.wait_send()
        for cp in local:
            cp.wait()


class Both:
    def __init__(self, a, b):
        self.a, self.b = a, b
        self.arrays, self.out_shape, self.sems = a.arrays + b.arrays, a.out_shape + b.out_shape, a.sems + b.sems
        self.has_mid = False
        assert not (a.has_mid or b.has_mid)

    def _parts(self, ins, outs, sems):
        na, sa = len(self.a.arrays), len(self.a.sems)
        return (ins[:na], outs[:na], sems[:sa]), (ins[na:], outs[na:], sems[sa:])

    def start(self, ins, outs, sems):
        pa, pb = self._parts(ins, outs, sems)
        self.a.start(*pa)
        self.b.start(*pb)

    def finish(self, ins, outs, sems):
        pa, pb = self._parts(ins, outs, sems)
        self.a.finish(*pa)
        self.b.finish(*pb)


_ANY = pl.BlockSpec(memory_space=pl.ANY)


def run_comm(name, comm):
    n = len(comm.arrays)

    def body(*refs):
        ins, outs, sems = refs[:n], refs[n:2 * n], refs[2 * n:]
        comm.start(ins, outs, sems)
        if comm.has_mid:
            comm.mid(ins, outs, sems)
        comm.finish(ins, outs, sems)

    return pl.pallas_call(body, name=name, out_shape=comm.out_shape, in_specs=[_ANY] * n, out_specs=[_ANY] * n,
                          scratch_shapes=comm.sems)(*comm.arrays)


def _call_with_comm(body, comm, steps, args, *, name, out_shape, in_specs, out_specs, grid, scratch=()):
    ni, no, ns, nc = len(in_specs), len(out_specs), len(scratch), len(comm.arrays)

    def full_body(*refs):
        ins, cins = refs[:ni], refs[ni:ni + nc]
        outs, couts = refs[ni + nc:ni + nc + no], refs[ni + nc + no:ni + 2 * nc + no]
        scr, csems = refs[ni + 2 * nc + no:ni + 2 * nc + no + ns], refs[ni + 2 * nc + no + ns:]
        first, middle, last = steps()
        pl.when(first)(lambda: comm.start(cins, couts, csems))
        if comm.has_mid:
            pl.when(middle)(lambda: comm.mid(cins, couts, csems))
        body(*ins, *outs, *scr)
        pl.when(last)(lambda: comm.finish(cins, couts, csems))

    res = pl.pallas_call(
        full_body, name=name, out_shape=list(out_shape) + comm.out_shape, grid=grid,
        in_specs=list(in_specs) + [_ANY] * nc, out_specs=list(out_specs) + [_ANY] * nc,
        scratch_shapes=list(scratch) + comm.sems,
        compiler_params=pltpu.CompilerParams(dimension_semantics=("arbitrary",) * len(grid),
                                             vmem_limit_bytes=VMEM_LIMIT))(*args, *comm.arrays)
    return res[:no], res[no:]


def _shift_right(p, chip, col):
    if chip == 0:
        return p
    if chip < 3:
        return pltpu.roll(p, 8 * chip, 1)
    tail0 = DT_STORED_START + SSM_HEADS - 3 * SHARD_W
    head = pltpu.roll(p, 24, 1)
    tail = pltpu.roll(p, 24 + DT_PAD, 1)
    return jnp.where(col < tail0 + 24, head, jnp.where(col >= tail0 + 24 + DT_PAD, tail, 0.0))


def _shift_left(g, chip, col):
    if chip == 0:
        return g
    if chip < 3:
        return pltpu.roll(g, PACK_W - 8 * chip, 1)
    tail0 = DT_STORED_START + SSM_HEADS - 3 * SHARD_W
    return jnp.where(col < tail0, pltpu.roll(g, PACK_W - 24, 1), pltpu.roll(g, PACK_W - 24 - DT_PAD, 1))


def pack_w_in(w):
    rows = BLOCK

    def body(w_ref, o_ref, pad_ref):
        chip = _chip_index()
        pad_ref[...] = jnp.zeros_like(pad_ref)
        pad_ref[:, 0:SHARD_W] = w_ref[...]
        p = pad_ref[...]
        col = lax.broadcasted_iota(jnp.int32, p.shape, 1)
        for cv in range(4):
            @pl.when(chip == cv)
            def _():
                o_ref[...] = _shift_right(p, cv, col).astype(BF16)

    return _call(body, name="pack_w_in", grid=(D_MODEL // rows,),
                 in_specs=[pl.BlockSpec((rows, SHARD_W), lambda i: (i, 0))],
                 out_specs=pl.BlockSpec((rows, PACK_W), lambda i: (i, 0)),
                 out_shape=jax.ShapeDtypeStruct((D_MODEL, PACK_W), BF16),
                 scratch=[pltpu.VMEM((rows, PACK_W), F32)], sem=("parallel",))(w)


def _tile_runs():
    runs, fix = [], []
    for t in range(N_ALIGNED // LANES):
        s = min(t // 19, 3)
        j = t - 19 * s
        p = _act_col(t * LANES)
        if runs and runs[-1][1] == s and runs[-1][0] + runs[-1][3] == p and runs[-1][2] + runs[-1][3] == j * LANES:
            runs[-1][3] += LANES
        else:
            runs.append([p, s, j * LANES, LANES])
        if j == 0 and s > 0:
            fix.append((p, s - 1))
    return runs, fix


def unpack_w_in(bg):
    rows = BLOCK
    runs, fix = _tile_runs()

    def body(b_ref, o_ref):
        for p, s, j, w in runs:
            o_ref[:, p:p + w] = b_ref[s, :, j:j + w]
        for p, s in fix:
            o_ref[:, p:p + LANES] = o_ref[:, p:p + LANES] + b_ref[s, :, SHARD_STRIDE:PACK_W]

    return _call(body, name="unpack_w_in", grid=(D_MODEL // rows,),
                 in_specs=[pl.BlockSpec((4, rows, PACK_W), lambda i: (0, i, 0))],
                 out_specs=pl.BlockSpec((rows, N_ALIGNED), lambda i: (i, 0)),
                 out_shape=jax.ShapeDtypeStruct((D_MODEL, N_ALIGNED), BF16), sem=("parallel",))(bg)


def pack_grad_w_in(dw):
    rows = BLOCK

    def body(g_ref, o_ref):
        for s in range(4):
            for j in range(PACK_W // LANES):
                p = _act_col((19 * s + j) * LANES)
                o_ref[s, :, j * LANES:(j + 1) * LANES] = g_ref[:, p:p + LANES].astype(BF16)

    return _call(body, name="pack_grad_w_in", grid=(D_MODEL // rows,),
                 in_specs=[pl.BlockSpec((rows, N_ALIGNED), lambda i: (i, 0))],
                 out_specs=pl.BlockSpec((4, rows, PACK_W), lambda i: (0, i, 0)),
                 out_shape=jax.ShapeDtypeStruct((4, D_MODEL, PACK_W), BF16), sem=("parallel",))(dw)


def _adamw(w, g, m, v):
    m = ADAM_B1 * m + (1.0 - ADAM_B1) * g
    v = ADAM_B2 * v + (1.0 - ADAM_B2) * jnp.square(g)
    m_hat = m / (1.0 - ADAM_B1 ** ADAM_STEP)
    v_hat = v / (1.0 - ADAM_B2 ** ADAM_STEP)
    delta = -ADAM_LR * (m_hat / (jnp.sqrt(v_hat) + ADAM_EPS) + ADAM_WD * w)
    return delta, m, v


def adamw_w_in(g_packed, w, m, v):
    rows = BLOCK

    def body(g_ref, w_ref, m_ref, v_ref, go_ref, d_ref, mo_ref, vo_ref, tmp_ref):
        chip = _chip_index()
        gp = g_ref[...]
        col = lax.broadcasted_iota(jnp.int32, gp.shape, 1)
        for cv in range(4):
            @pl.when(chip == cv)
            def _():
                tmp_ref[...] = _shift_left(gp, cv, col)
        g = tmp_ref[:, 0:SHARD_W]
        d, mn, vn = _adamw(w_ref[...], g, m_ref[...], v_ref[...])
        go_ref[...] = g
        d_ref[...] = d
        mo_ref[...] = mn
        vo_ref[...] = vn

    spec = pl.BlockSpec((rows, SHARD_W), lambda i: (i, 0))
    shp = jax.ShapeDtypeStruct((D_MODEL, SHARD_W), F32)
    return _call(body, name="adamw_w_in", grid=(D_MODEL // rows,),
                 in_specs=[pl.BlockSpec((rows, PACK_W), lambda i: (i, 0)), spec, spec, spec],
                 out_specs=[spec] * 4, out_shape=[shp] * 4,
                 scratch=[pltpu.VMEM((rows, PACK_W), F32)], sem=("parallel",))(g_packed, w, m, v)


def adamw_rows(name, g, w, m, v):
    r, c = g.shape
    rows = min(r, BLOCK)

    def body(g_ref, w_ref, m_ref, v_ref, d_ref, mo_ref, vo_ref):
        d_ref[...], mo_ref[...], vo_ref[...] = _adamw(w_ref[...], g_ref[...], m_ref[...], v_ref[...])

    spec = pl.BlockSpec((rows, c), lambda i: (i, 0))
    shp = jax.ShapeDtypeStruct((r, c), F32)
    return _call(body, name=name, grid=(r // rows,), in_specs=[spec] * 4, out_specs=[spec] * 3, out_shape=[shp] * 3,
                 sem=("parallel",))(g, w, m, v)


def adamw_small(gs, ws, ms, vs):
    n = len(gs)

    def body(*refs):
        g, w, m, v = refs[:n], refs[n:2 * n], refs[2 * n:3 * n], refs[3 * n:4 * n]
        outs = refs[4 * n:]
        for i in range(n):
            d, mn, vn = _adamw(w[i][...], g[i][...], m[i][...], v[i][...])
            outs[3 * i][...] = d
            outs[3 * i + 1][...] = mn
            outs[3 * i + 2][...] = vn

    specs = [_full(a.shape) for a in gs]
    res = _call(body, name="adamw_small", in_specs=specs * 4,
                out_specs=[s for s in specs for _ in range(3)],
                out_shape=[jax.ShapeDtypeStruct(a.shape, F32) for a in gs for _ in range(3)])(*gs, *ws, *ms, *vs)
    return [tuple(res[3 * i:3 * i + 3]) for i in range(n)]


def sum_slots(name, r, out_dtype=F32):
    s, rr, c = r.shape
    rows = min(rr, BLOCK)

    def body(r_ref, o_ref):
        acc = r_ref[0].astype(F32)
        for k in range(1, s):
            acc = acc + r_ref[k].astype(F32)
        o_ref[...] = acc.astype(out_dtype)

    return _call(body, name=name, grid=(rr // rows,), in_specs=[pl.BlockSpec((s, rows, c), lambda i: (0, i, 0))],
                 out_specs=pl.BlockSpec((rows, c), lambda i: (i, 0)), out_shape=jax.ShapeDtypeStruct((rr, c), out_dtype),
                 sem=("parallel",))(r)


def _col_tile(n, k):
    if n % 896 == 0 and k <= 1024:
        return 896
    return min(n, 512)


def mm_nn(name, x, w, out_dtype=F32, comm=None):
    m, k = x.shape
    n = w.shape[1]
    tn = _col_tile(n, k)
    steps = n // tn

    def body(x_ref, w_ref, o_ref):
        o_ref[...] = jnp.dot(x_ref[...].astype(BF16), w_ref[...].astype(BF16),
                             preferred_element_type=F32).astype(out_dtype)

    kw = dict(name=name, grid=(steps,), in_specs=[_full((m, k)), pl.BlockSpec((k, tn), lambda j: (0, j))])
    if comm is None:
        return _call(body, out_specs=pl.BlockSpec((m, tn), lambda j: (0, j)),
                     out_shape=jax.ShapeDtypeStruct((m, n), out_dtype), sem=("parallel",), **kw)(x, w)

    def at():
        j = pl.program_id(0)
        return j == 0, j == steps // 2, j == steps - 1

    (res,), moved = _call_with_comm(body, comm, at, (x, w), out_specs=[pl.BlockSpec((m, tn), lambda j: (0, j))],
                                    out_shape=[jax.ShapeDtypeStruct((m, n), out_dtype)], **kw)
    return res, moved


def mm_nt(name, dy, w, out_dtype=F32, comm=None):
    m, n = dy.shape
    k = w.shape[0]
    tm = m // 2
    tn = _col_tile(n, k)
    steps = n // tn

    def at():
        i, j = pl.program_id(0), pl.program_id(1)
        return (i == 0) & (j == 0), (i == 0) & (j == steps - 1), (i == m // tm - 1) & (j == steps - 1)

    def body(dy_ref, w_ref, o_ref, acc_ref):
        j = pl.program_id(1)

        @pl.when(j == 0)
        def _():
            acc_ref[...] = jnp.zeros_like(acc_ref)

        acc_ref[...] += lax.dot_general(dy_ref[...].astype(BF16), w_ref[...].astype(BF16), (((1,), (1,)), ((), ())),
                                        preferred_element_type=F32)

        @pl.when(j == steps - 1)
        def _():
            o_ref[...] = acc_ref[...].astype(out_dtype)

    kw = dict(name=name, grid=(m // tm, steps), scratch=[pltpu.VMEM((tm, k), F32)],
              in_specs=[pl.BlockSpec((tm, tn), lambda i, j: (i, j)), pl.BlockSpec((k, tn), lambda i, j: (0, j))])
    if comm is None:
        return _call(body, out_specs=pl.BlockSpec((tm, k), lambda i, j: (i, 0)),
                     out_shape=jax.ShapeDtypeStruct((m, k), out_dtype), sem=("parallel", "arbitrary"), **kw)(dy, w)
    (res,), moved = _call_with_comm(body, comm, at, (dy, w), out_specs=[pl.BlockSpec((tm, k), lambda i, j: (i, 0))],
                                    out_shape=[jax.ShapeDtypeStruct((m, k), out_dtype)], **kw)
    return res, moved


def mm_tn(name, x, dy):
    m, k = x.shape
    n = dy.shape[1]
    tm = m // 2
    tn = _col_tile(n, k)

    def body(x_ref, dy_ref, o_ref):
        @pl.when(pl.program_id(1) == 0)
        def _():
            o_ref[...] = jnp.zeros_like(o_ref)

        o_ref[...] += lax.dot_general(x_ref[...].astype(BF16), dy_ref[...].astype(BF16), (((0,), (0,)), ((), ())),
                                      preferred_element_type=F32)

    return _call(body, name=name, grid=(n // tn, m // tm),
                 in_specs=[pl.BlockSpec((tm, k), lambda i, j: (j, 0)), pl.BlockSpec((tm, tn), lambda i, j: (j, i))],
                 out_specs=pl.BlockSpec((k, tn), lambda i, j: (0, i)), out_shape=jax.ShapeDtypeStruct((k, n), F32),
                 sem=("parallel", "arbitrary"))(x, dy)


def _row_spec(width, col_block=0):
    return pl.BlockSpec((BLOCK, width), lambda i: (i, col_block))


def _x_spec():
    return pl.BlockSpec((None, BLOCK, D_MODEL), lambda i: (0, jnp.maximum(i - 1, 0), 0))


def prep(x, meta, g_pre):
    nb = x.shape[1] // BLOCK + 1

    def body(x_ref, meta_ref, g_ref, h_ref, u_ref):
        i = pl.program_id(0)

        @pl.when(i == 0)
        def _():
            h_ref[0:PAD_ROWS, :] = jnp.zeros((PAD_ROWS, D_MODEL), F32)
            h_ref[PAD_ROWS:BLOCK, :] = meta_ref[...]

        @pl.when(i > 0)
        def _():
            h_ref[...] = x_ref[...]

        u_ref[...] = _rms(h_ref[...], g_ref[...]).astype(BF16)

    return _call(body, name="prep", grid=(nb,), in_specs=[_x_spec(), _full((N_META, D_MODEL)), _full((1, D_MODEL))],
                 out_specs=[_row_spec(D_MODEL), _row_spec(D_MODEL)],
                 out_shape=[jax.ShapeDtypeStruct((nb * BLOCK, D_MODEL), F32),
                            jax.ShapeDtypeStruct((nb * BLOCK, D_MODEL), BF16)], sem=("parallel",))(x, meta, g_pre)


def prep_bwd(h, du, dres, g_pre):
    nb = h.shape[0] // BLOCK

    def body(h_ref, du_ref, dres_ref, g_ref, gx_ref, gm_ref, gg_ref):
        i = pl.program_id(0)
        _, vjp = jax.vjp(_rms, h_ref[...], g_ref[...])
        dh, dg = vjp(du_ref[...])

        @pl.when(i == 0)
        def _():
            gm_ref[...] = dh[PAD_ROWS:BLOCK, :]
            gg_ref[...] = dg

        @pl.when(i > 0)
        def _():
            gg_ref[...] += dg

        gx_ref[...] = dh + dres_ref[...]

    return _call(body, name="prep_bwd", grid=(nb,),
                 in_specs=[_row_spec(D_MODEL), _row_spec(D_MODEL), _row_spec(D_MODEL), _full((1, D_MODEL))],
                 out_specs=[_x_spec(), _full((N_META, D_MODEL)), _full((1, D_MODEL))],
                 out_shape=[jax.ShapeDtypeStruct((1, (nb - 1) * BLOCK, D_MODEL), F32),
                            jax.ShapeDtypeStruct((N_META, D_MODEL), F32), jax.ShapeDtypeStruct((1, D_MODEL), F32)],
                 sem=("arbitrary",))(h, du, dres, g_pre)


GROUP_W = SSM_INNER // SSM_GROUPS


def _gated_norm(y, z, g):
    t = y * _silu(z)
    return t * lax.rsqrt(jnp.mean(t * t, axis=-1, keepdims=True) + NORM_EPS) * g


def ssm_norm_fwd(y, proj, g_norm):
    nb = y.shape[0] // BLOCK
    spec = pl.BlockSpec((BLOCK, GROUP_W), lambda i, g: (i, g))

    def body(y_ref, z_ref, g_ref, o_ref):
        o_ref[...] = _gated_norm(y_ref[...], z_ref[...], g_ref[...]).astype(BF16)

    return _call(body, name="ssm_norm_fwd", grid=(nb, SSM_GROUPS),
                 in_specs=[spec, spec, pl.BlockSpec((1, GROUP_W), lambda i, g: (0, g))], out_specs=spec,
                 out_shape=jax.ShapeDtypeStruct(y.shape, BF16), sem=("parallel", "parallel"))(y, proj, g_norm)


def ssm_norm_bwd(dyn, y, proj, g_norm):
    nb = y.shape[0] // BLOCK
    spec = pl.BlockSpec((BLOCK, GROUP_W), lambda g, i: (i, g))
    gspec = pl.BlockSpec((1, GROUP_W), lambda g, i: (0, g))

    def body(d_ref, y_ref, z_ref, g_ref, dy_ref, dz_ref, dg_ref):
        _, vjp = jax.vjp(_gated_norm, y_ref[...], z_ref[...], g_ref[...])
        dy, dz, dg = vjp(d_ref[...])
        dy_ref[...] = dy
        dz_ref[...] = dz.astype(BF16)

        @pl.when(pl.program_id(1) == 0)
        def _():
            dg_ref[...] = dg

        @pl.when(pl.program_id(1) > 0)
        def _():
            dg_ref[...] += dg

    return _call(body, name="ssm_norm_bwd", grid=(SSM_GROUPS, nb), in_specs=[spec, spec, spec, gspec],
                 out_specs=[spec, spec, gspec],
                 out_shape=[jax.ShapeDtypeStruct(y.shape, F32), jax.ShapeDtypeStruct(y.shape, BF16),
                            jax.ShapeDtypeStruct((1, SSM_INNER), F32)],
                 sem=("parallel", "arbitrary"))(dyn, y, proj, g_norm)


def _merge(ga, gs, ya, ys):
    return jax.nn.sigmoid(ga) * ya + jax.nn.sigmoid(gs) * ys


GATE_ATT_BLOCK = SEG["gate_att"][2] // D_MODEL
GATE_SSM_BLOCK = SEG["gate_ssm"][2] // D_MODEL


def merge_fwd(proj, y_att, y_ssm):
    nb = y_att.shape[0] // BLOCK

    def body(ga_ref, gs_ref, ya_ref, ys_ref, o_ref):
        o_ref[...] = _merge(ga_ref[...], gs_ref[...], ya_ref[...], ys_ref[...]).astype(BF16)

    return _call(body, name="merge_fwd", grid=(nb,),
                 in_specs=[_row_spec(D_MODEL, GATE_ATT_BLOCK), _row_spec(D_MODEL, GATE_SSM_BLOCK), _row_spec(D_MODEL),
                           _row_spec(D_MODEL)],
                 out_specs=_row_spec(D_MODEL), out_shape=jax.ShapeDtypeStruct(y_att.shape, BF16),
                 sem=("parallel",))(proj, proj, y_att, y_ssm)


def merge_bwd(dm, proj, y_att, y_ssm):
    nb = y_att.shape[0] // BLOCK

    def body(d_ref, ga_ref, gs_ref, ya_ref, ys_ref, dga_ref, dgs_ref, dya_ref, dys_ref):
        _, vjp = jax.vjp(_merge, ga_ref[...], gs_ref[...], ya_ref[...], ys_ref[...])
        dga, dgs, dya, dys = vjp(d_ref[...])
        dga_ref[...] = dga.astype(BF16)
        dgs_ref[...] = dgs.astype(BF16)
        dya_ref[...] = dya.astype(BF16)
        dys_ref[...] = dys.astype(BF16)

    return _call(body, name="merge_bwd", grid=(nb,),
                 in_specs=[_row_spec(D_MODEL), _row_spec(D_MODEL, GATE_ATT_BLOCK), _row_spec(D_MODEL, GATE_SSM_BLOCK),
                           _row_spec(D_MODEL), _row_spec(D_MODEL)],
                 out_specs=[_row_spec(D_MODEL)] * 4, out_shape=[jax.ShapeDtypeStruct(y_att.shape, BF16)] * 4,
                 sem=("parallel",))(dm, proj, proj, y_att, y_ssm)


def final_loss(out, x, target, g_post):
    nb = out.shape[0] // BLOCK

    def body(out_ref, x_ref, t_ref, g_ref, do_ref, dres_ref, loss_ref, dg_ref):
        i = pl.program_id(0)

        @pl.when(i == 0)
        def _():
            do_ref[...] = jnp.zeros_like(do_ref)
            dres_ref[...] = jnp.zeros_like(dres_ref)
            loss_ref[...] = jnp.zeros_like(loss_ref)
            dg_ref[...] = jnp.zeros_like(dg_ref)

        @pl.when(i > 0)
        def _():
            n, vjp = jax.vjp(_rms, out_ref[...], g_ref[...])
            diff = x_ref[...] + n - t_ref[...]
            loss_ref[...] += 0.5 * jnp.sum(diff * diff) / D_MODEL
            dn = diff * (1.0 / D_MODEL)
            do, dg = vjp(dn)
            do_ref[...] = do.astype(BF16)
            dres_ref[...] = dn
            dg_ref[...] += dg

    return _call(body, name="final_loss", grid=(nb,),
                 in_specs=[_row_spec(D_MODEL), _x_spec(), _x_spec(), _full((1, D_MODEL))],
                 out_specs=[_row_spec(D_MODEL), _row_spec(D_MODEL), _full((8, LANES)), _full((1, D_MODEL))],
                 out_shape=[jax.ShapeDtypeStruct(out.shape, BF16), jax.ShapeDtypeStruct(out.shape, F32),
                            jax.ShapeDtypeStruct((8, LANES), F32), jax.ShapeDtypeStruct((1, D_MODEL), F32)],
                 sem=("arbitrary",))(out, x, target, g_post)


_NT = (((1,), (1,)), ((), ()))
ALIBI_SLOPES = tuple(2.0 ** (-8.0 * (h + 1) / ATT_Q_HEADS) for h in range(ATT_Q_HEADS))
KV_WIDTH = ATT_KV_HEADS * HEAD_DIM
Q_BLOCK = SEG["q"][2] // D_MODEL
Z_ATT_BLOCK = SEG["z_att"][2] // D_MODEL
K_BLOCK = SEG["k"][2] // KV_WIDTH
V_BLOCK = SEG["v"][2] // KV_WIDTH
META_ROW_BLOCK = PAD_ROWS // N_META


@jax.custom_vjp
def _swap_halves(x):
    return pltpu.roll(x, HEAD_DIM, 1)


_swap_halves.defvjp(lambda x: (pltpu.roll(x, HEAD_DIM, 1), None), lambda _, g: (pltpu.roll(g, HEAD_DIM, 1),))


def _both_halves(t, half):
    first = lax.broadcasted_iota(jnp.int32, t.shape, 1) < HEAD_DIM
    sw = _swap_halves(t)
    return jnp.where(first, t, sw) if half == 0 else jnp.where(first, sw, t)


def _attn_rows(q, z, kp, kc, vp, vc, km, vm, sinks, n):
    rows = ATT_GROUP * BLOCK
    i = lax.broadcasted_iota(jnp.int32, (rows, BLOCK), 0) & (BLOCK - 1)
    j = lax.broadcasted_iota(jnp.int32, (rows, BLOCK), 1)
    rel_c = (i - j).astype(F32)
    rel_p = rel_c + float(BLOCK)
    nv = jnp.zeros((rows, BLOCK), jnp.int32) + n
    ok_c = (i >= j) & (nv >= 1)
    ok_p = (j > i) & (nv >= 2)
    im = lax.broadcasted_iota(jnp.int32, (rows, N_META), 0) & (BLOCK - 1)
    jm = lax.broadcasted_iota(jnp.int32, (rows, N_META), 1)
    ok_m = ((jnp.zeros((rows, N_META), jnp.int32) + n) >= 1) | (im >= PAD_ROWS + jm)
    first = lax.broadcasted_iota(jnp.int32, (BLOCK, LANES), 1) < HEAD_DIM
    neg = -jnp.inf
    outs = []
    for kv in range(ATT_KV_HEADS):
        tile, half = divmod(kv, 2)
        lanes = slice(tile * LANES, (tile + 1) * LANES)
        kc2, kp2, km2 = (_both_halves(t[:, lanes], half).astype(BF16) for t in (kc, kp, km))
        vc2, vp2, vm2 = (_both_halves(t[:, lanes], half).astype(BF16) for t in (vc, vp, vm))
        qs, slope, sk = [], [], []
        for pair in range(ATT_GROUP // 2):
            c0 = (kv * ATT_GROUP + 2 * pair) * HEAD_DIM
            qp = q[:, c0:c0 + LANES] * HEAD_DIM ** -0.5
            qs += [jnp.where(first, qp, 0.0), jnp.where(first, 0.0, qp)]
        for g in range(ATT_GROUP):
            slope.append(jnp.full((BLOCK, 1), ALIBI_SLOPES[kv * ATT_GROUP + g], F32))
            sk.append(jnp.broadcast_to(sinks[kv * ATT_GROUP + g], (BLOCK, 1)))
        qs = jnp.concatenate(qs, axis=0).astype(BF16)
        slope = jnp.concatenate(slope, axis=0)
        sk = jnp.concatenate(sk, axis=0)
        sc = jnp.where(ok_c, lax.dot_general(qs, kc2, _NT, preferred_element_type=F32) - slope * rel_c, neg)
        sp = jnp.where(ok_p, lax.dot_general(qs, kp2, _NT, preferred_element_type=F32) - slope * rel_p, neg)
        sm = jnp.where(ok_m, lax.dot_general(qs, km2, _NT, preferred_element_type=F32), neg)
        mx = jnp.maximum(jnp.maximum(jnp.max(sc, axis=1, keepdims=True), jnp.max(sp, axis=1, keepdims=True)),
                         jnp.maximum(jnp.max(sm, axis=1, keepdims=True), sk))
        mx = lax.stop_gradient(mx)
        ec, ep, em, es = jnp.exp(sc - mx), jnp.exp(sp - mx), jnp.exp(sm - mx), jnp.exp(sk - mx)
        den = (es + jnp.sum(ec, axis=1, keepdims=True) + jnp.sum(ep, axis=1, keepdims=True)
               + jnp.sum(em, axis=1, keepdims=True))
        inv = 1.0 / den
        o = (jnp.dot((ec * inv).astype(BF16), vc2, preferred_element_type=F32)
             + jnp.dot((ep * inv).astype(BF16), vp2, preferred_element_type=F32)
             + jnp.dot((em * inv).astype(BF16), vm2, preferred_element_type=F32))
        for pair in range(ATT_GROUP // 2):
            r0 = 2 * pair * BLOCK
            outs.append(jnp.where(first, o[r0:r0 + BLOCK], o[r0 + BLOCK:r0 + 2 * BLOCK]))
    return jnp.concatenate(outs, axis=1) * _silu(z)


def _attn_specs(nb, steps_clamped):
    def blk(t):
        return jnp.minimum(t, nb - 1) if steps_clamped else t

    wide = lambda col: pl.BlockSpec((BLOCK, D_MODEL), lambda t: (blk(t), col))
    cur = lambda col: pl.BlockSpec((BLOCK, KV_WIDTH), lambda t: (blk(t), col))
    prev = lambda col: pl.BlockSpec((BLOCK, KV_WIDTH), lambda t: (jnp.maximum(blk(t) - 1, 0), col))
    meta = lambda col: pl.BlockSpec((N_META, KV_WIDTH), lambda t: (META_ROW_BLOCK, col))
    sinks = pl.BlockSpec((ATT_Q_HEADS, 1, 1), lambda t: (0, 0, 0))
    return [wide(Q_BLOCK), wide(Z_ATT_BLOCK), prev(K_BLOCK), cur(K_BLOCK), prev(V_BLOCK), cur(V_BLOCK),
            meta(K_BLOCK), meta(V_BLOCK), sinks]


def attn_fwd(proj, sinks):
    nb = proj.shape[0] // BLOCK

    def body(q_ref, z_ref, kp_ref, kc_ref, vp_ref, vc_ref, km_ref, vm_ref, sk_ref, o_ref):
        o_ref[...] = _attn_rows(q_ref[...], z_ref[...], kp_ref[...], kc_ref[...], vp_ref[...], vc_ref[...],
                                km_ref[...], vm_ref[...], tuple(sk_ref[h] for h in range(ATT_Q_HEADS)),
                                pl.program_id(0)).astype(BF16)

    return _call(body, name="attn_fwd", grid=(nb,), in_specs=_attn_specs(nb, False), out_specs=_row_spec(D_MODEL),
                 out_shape=jax.ShapeDtypeStruct((nb * BLOCK, D_MODEL), BF16), sem=("parallel",))(*([proj] * 8), sinks)


def attn_bwd(da, proj, sinks):
    nb = proj.shape[0] // BLOCK
    last = nb - 1
    wide = pl.BlockSpec((BLOCK, D_MODEL), lambda t: (jnp.minimum(t, last), 0))
    done = pl.BlockSpec((BLOCK, KV_WIDTH), lambda t: (jnp.maximum(t - 1, 0), 0))
    meta = _full((N_META, KV_WIDTH))
    par = _full((ATT_Q_HEADS, 1, 1))

    def body(da_ref, q_ref, z_ref, kp_ref, kc_ref, vp_ref, vc_ref, km_ref, vm_ref, sk_ref,
             dq_ref, dz_ref, dk_ref, dv_ref, dkm_ref, dvm_ref, dsk_ref, ck_ref, cv_ref):
        t = pl.program_id(0)

        @pl.when(t == 0)
        def _():
            ck_ref[...] = jnp.zeros_like(ck_ref)
            cv_ref[...] = jnp.zeros_like(cv_ref)
            dkm_ref[...] = jnp.zeros_like(dkm_ref)
            dvm_ref[...] = jnp.zeros_like(dvm_ref)
            dsk_ref[...] = jnp.zeros_like(dsk_ref)

        @pl.when(t < nb)
        def _():
            def f(q, z, kp, kc, vp, vc, km, vm, sk):
                return _attn_rows(q, z, kp, kc, vp, vc, km, vm, sk, t)

            _, vjp = jax.vjp(f, q_ref[...], z_ref[...], kp_ref[...], kc_ref[...], vp_ref[...], vc_ref[...],
                             km_ref[...], vm_ref[...], tuple(sk_ref[h] for h in range(ATT_Q_HEADS)))
            dq, dz, dkp, dkc, dvp, dvc, dkm, dvm, dsk = vjp(da_ref[...])
            dq_ref[...] = dq.astype(BF16)
            dz_ref[...] = dz.astype(BF16)
            for h in range(ATT_Q_HEADS):
                dsk_ref[h] += dsk[h]
            dk_ref[...] = ck_ref[...] + dkp
            dv_ref[...] = cv_ref[...] + dvp
            ck_ref[...] = dkc
            cv_ref[...] = dvc
            dkm_ref[...] += dkm
            dvm_ref[...] += dvm

        @pl.when(t == nb)
        def _():
            dk_ref[...] = ck_ref[...]
            dv_ref[...] = cv_ref[...]

    rows = nb * BLOCK
    return _call(body, name="attn_bwd", grid=(nb + 1,), in_specs=[wide] + _attn_specs(nb, True),
                 out_specs=[wide, wide, done, done, meta, meta, par],
                 out_shape=[jax.ShapeDtypeStruct((rows, D_MODEL), BF16), jax.ShapeDtypeStruct((rows, D_MODEL), BF16),
                            jax.ShapeDtypeStruct((rows, KV_WIDTH), F32), jax.ShapeDtypeStruct((rows, KV_WIDTH), F32),
                            jax.ShapeDtypeStruct((N_META, KV_WIDTH), F32), jax.ShapeDtypeStruct((N_META, KV_WIDTH), F32),
                            jax.ShapeDtypeStruct(sinks.shape, F32)],
                 scratch=[pltpu.VMEM((BLOCK, KV_WIDTH), F32), pltpu.VMEM((BLOCK, KV_WIDTH), F32)],
                 sem=("arbitrary",))(da, *([proj] * 8), sinks)


XBC_BLOCK0 = SEG["xbc"][2] // D_MODEL
CONV_COL_BLOCKS = CONV_DIM // D_MODEL
DT_TILE = SEG["dt"][2] // LANES


def _shift_rows(cur, prev, j, row):
    if j == 0:
        return cur
    return jnp.where(row >= j, pltpu.roll(cur, j, 0), pltpu.roll(prev, j, 0))


def _conv_pre(cur, prev, w_ref, b_ref, row):
    pre = b_ref[...] + w_ref[CONV_WIDTH - 1:CONV_WIDTH, :] * cur
    for k in range(CONV_WIDTH - 1):
        pre = pre + w_ref[k:k + 1, :] * _shift_rows(cur, prev, CONV_WIDTH - 1 - k, row)
    return pre


def conv_fwd(proj, conv_w, conv_b):
    nb = proj.shape[0] // BLOCK
    cur = pl.BlockSpec((BLOCK, D_MODEL), lambda j, i: (i, XBC_BLOCK0 + j))
    prev = pl.BlockSpec((BLOCK, D_MODEL), lambda j, i: (jnp.maximum(i - 1, 0), XBC_BLOCK0 + j))

    def body(c_ref, p_ref, w_ref, b_ref, o_ref):
        i = pl.program_id(1)
        row = lax.broadcasted_iota(jnp.int32, (BLOCK, D_MODEL), 0)
        prevv = p_ref[...] * jnp.where(i > 0, 1.0, 0.0)
        pre = _conv_pre(c_ref[...], prevv, w_ref, b_ref, row)
        valid = jnp.maximum((row >= PAD_ROWS).astype(F32), jnp.where(i > 0, 1.0, 0.0))
        o_ref[...] = _silu(pre) * valid

    return _call(body, name="conv_fwd", grid=(CONV_COL_BLOCKS, nb),
                 in_specs=[cur, prev, pl.BlockSpec((CONV_WIDTH, D_MODEL), lambda j, i: (0, j)),
                           pl.BlockSpec((1, D_MODEL), lambda j, i: (0, j))],
                 out_specs=pl.BlockSpec((BLOCK, D_MODEL), lambda j, i: (i, j)),
                 out_shape=jax.ShapeDtypeStruct((nb * BLOCK, CONV_DIM), F32),
                 sem=("parallel", "parallel"))(proj, proj, conv_w, conv_b)


def conv_bwd(dxbc, proj, conv_w, conv_b):
    nb = proj.shape[0] // BLOCK
    last = nb - 1
    cur = pl.BlockSpec((BLOCK, D_MODEL), lambda j, i: (i, XBC_BLOCK0 + j))
    prev = pl.BlockSpec((BLOCK, D_MODEL), lambda j, i: (jnp.maximum(i - 1, 0), XBC_BLOCK0 + j))
    nxt = pl.BlockSpec((BLOCK, D_MODEL), lambda j, i: (jnp.minimum(i + 1, last), XBC_BLOCK0 + j))
    dcur = pl.BlockSpec((BLOCK, D_MODEL), lambda j, i: (i, j))
    dnxt = pl.BlockSpec((BLOCK, D_MODEL), lambda j, i: (jnp.minimum(i + 1, last), j))
    wspec = pl.BlockSpec((CONV_WIDTH, D_MODEL), lambda j, i: (0, j))
    bspec = pl.BlockSpec((1, D_MODEL), lambda j, i: (0, j))

    def body(dc_ref, dn_ref, c_ref, p_ref, n_ref, w_ref, b_ref, du_ref, dw_ref, db_ref):
        i = pl.program_id(1)
        row = lax.broadcasted_iota(jnp.int32, (BLOCK, D_MODEL), 0)
        curv = c_ref[...]
        prevv = p_ref[...] * jnp.where(i > 0, 1.0, 0.0)

        def dpre_of(pre, d, valid):
            s = jax.nn.sigmoid(pre)
            return d * valid * (s * (1.0 + pre * (1.0 - s)))

        valid = jnp.maximum((row >= PAD_ROWS).astype(F32), jnp.where(i > 0, 1.0, 0.0))
        dp_c = dpre_of(_conv_pre(curv, prevv, w_ref, b_ref, row), dc_ref[...], valid)
        has_next = jnp.where(i < last, 1.0, 0.0)
        dp_n = dpre_of(_conv_pre(n_ref[...], curv, w_ref, b_ref, row), dn_ref[...], has_next)
        du = w_ref[CONV_WIDTH - 1:CONV_WIDTH, :] * dp_c
        for j in range(1, CONV_WIDTH):
            up = jnp.where(row < BLOCK - j, pltpu.roll(dp_c, BLOCK - j, 0), pltpu.roll(dp_n, BLOCK - j, 0))
            du = du + w_ref[CONV_WIDTH - 1 - j:CONV_WIDTH - j, :] * up
        du_ref[...] = du.astype(BF16)

        @pl.when(i == 0)
        def _():
            dw_ref[...] = jnp.zeros_like(dw_ref)
            db_ref[...] = jnp.zeros_like(db_ref)

        for k in range(CONV_WIDTH):
            dw_ref[k:k + 1, :] += jnp.sum(dp_c * _shift_rows(curv, prevv, CONV_WIDTH - 1 - k, row), axis=0,
                                          keepdims=True)
        db_ref[...] += jnp.sum(dp_c, axis=0, keepdims=True)

    return _call(body, name="conv_bwd", grid=(CONV_COL_BLOCKS, nb),
                 in_specs=[dcur, dnxt, cur, prev, nxt, wspec, bspec], out_specs=[dcur, wspec, bspec],
                 out_shape=[jax.ShapeDtypeStruct((nb * BLOCK, CONV_DIM), BF16),
                            jax.ShapeDtypeStruct((CONV_WIDTH, CONV_DIM), F32), jax.ShapeDtypeStruct((1, CONV_DIM), F32)],
                 sem=("parallel", "arbitrary"))(dxbc, dxbc, proj, proj, proj, conv_w, conv_b)


def _head_expand():
    e = np.zeros((LANES, SSM_INNER), np.float32)
    for h in range(SSM_HEADS):
        e[h, h * HEAD_DIM:(h + 1) * HEAD_DIM] = 1.0
    return jnp.asarray(e)


def _softplus(x):
    return jnp.maximum(x, 0.0) + jnp.log(1.0 + jnp.exp(-jnp.abs(x)))


def dt_fwd(proj, expand, bias_x):
    nb = proj.shape[0] // BLOCK

    def body(t_ref, e_ref, b_ref, o_ref):
        raw = jnp.dot(t_ref[...], e_ref[...], precision=HI, preferred_element_type=F32)
        o_ref[...] = _softplus(raw + b_ref[...])

    return _call(body, name="dt_fwd", grid=(nb,),
                 in_specs=[_row_spec(LANES, DT_TILE), _full((LANES, SSM_INNER)), _full((1, SSM_INNER))],
                 out_specs=_row_spec(SSM_INNER), out_shape=jax.ShapeDtypeStruct((nb * BLOCK, SSM_INNER), F32),
                 sem=("parallel",))(proj, expand, bias_x)


def dt_bwd(ddt_x, proj, expand, bias_x):
    nb = proj.shape[0] // BLOCK

    def body(d_ref, t_ref, e_ref, b_ref, o_ref, db_ref):
        raw = jnp.dot(t_ref[...], e_ref[...], precision=HI, preferred_element_type=F32)
        draw = d_ref[...] * jax.nn.sigmoid(raw + b_ref[...])
        dt = lax.dot_general(draw, e_ref[...], _NT, precision=HI, preferred_element_type=F32)
        o_ref[...] = dt.astype(BF16)

        @pl.when(pl.program_id(0) == 0)
        def _():
            db_ref[...] = jnp.zeros_like(db_ref)

        db_ref[...] += jnp.sum(dt, axis=0, keepdims=True)

    return _call(body, name="dt_bwd", grid=(nb,),
                 in_specs=[_row_spec(SSM_INNER), _row_spec(LANES, DT_TILE), _full((LANES, SSM_INNER)),
                           _full((1, SSM_INNER))],
                 out_specs=[_row_spec(LANES), _full((1, LANES))],
                 out_shape=[jax.ShapeDtypeStruct((nb * BLOCK, LANES), BF16), jax.ShapeDtypeStruct((1, LANES), F32)],
                 sem=("arbitrary",))(ddt_x, proj, expand, bias_x)


def _ssd_group(xs, dtx, bg, cg, alog, dsk, state):
    l = lax.broadcasted_iota(jnp.int32, (BLOCK, BLOCK), 0)
    s = lax.broadcasted_iota(jnp.int32, (BLOCK, BLOCK), 1)
    causal = l >= s
    first_head = s < HEAD_DIM
    a = dtx * (-jnp.exp(alog))
    cs = jnp.dot(causal.astype(F32), a, precision=HI, preferred_element_type=F32)
    tot = jnp.sum(a, axis=0, keepdims=True)
    bb, cb16 = bg.astype(BF16), cg.astype(BF16)
    cb = lax.dot_general(cb16, bb, _NT, preferred_element_type=F32)
    xr = xs * dtx
    y_diag = []
    for p in range(GROUP_W // LANES):
        lanes = slice(p * LANES, (p + 1) * LANES)
        c_pair = cs[:, lanes]
        c_swap = _swap_halves(c_pair)
        m = []
        for c_head in (jnp.where(first_head, c_pair, c_swap), jnp.where(first_head, c_swap, c_pair)):
            m.append(cb * jnp.exp(jnp.where(causal, c_head - c_head.T, -jnp.inf)))
        x_pair = xr[:, lanes]
        x_diag = jnp.concatenate([jnp.where(first_head, x_pair, 0.0), jnp.where(first_head, 0.0, x_pair)], axis=0)
        y_diag.append(jnp.dot(jnp.concatenate(m, axis=1).astype(BF16), x_diag.astype(BF16),
                              preferred_element_type=F32))
    st = lax.dot_general(bb, (xr * jnp.exp(tot - cs)).astype(BF16), (((0,), (0,)), ((), ())),
                         preferred_element_type=F32)
    new_state = state * jnp.exp(tot) + st
    y_off = jnp.dot(cb16, state.astype(BF16), preferred_element_type=F32) * jnp.exp(cs)
    return jnp.concatenate(y_diag, axis=1) + y_off + dsk * xs, new_state


B_TILE0 = SSM_INNER // LANES
C_TILE0 = B_TILE0 + SSM_GROUPS


def ssd_fwd(xbc, dt_x, alog_x, dsk_x):
    nb = xbc.shape[0] // BLOCK
    gspec = pl.BlockSpec((BLOCK, GROUP_W), lambda g, c: (c, g))
    pspec = pl.BlockSpec((1, GROUP_W), lambda g, c: (0, g))

    def body(x_ref, dt_ref, b_ref, c_ref, al_ref, dk_ref, y_ref, sp_ref, st_ref):
        @pl.when(pl.program_id(1) == 0)
        def _():
            st_ref[...] = jnp.zeros_like(st_ref)

        state = st_ref[...]
        sp_ref[...] = state
        y_ref[...], st_ref[...] = _ssd_group(x_ref[...], dt_ref[...], b_ref[...], c_ref[...], al_ref[...],
                                             dk_ref[...], state)

    return _call(body, name="ssd_fwd", grid=(SSM_GROUPS, nb),
                 in_specs=[gspec, gspec, pl.BlockSpec((BLOCK, SSM_STATE), lambda g, c: (c, B_TILE0 + g)),
                           pl.BlockSpec((BLOCK, SSM_STATE), lambda g, c: (c, C_TILE0 + g)), pspec, pspec],
                 out_specs=[gspec, pl.BlockSpec((None, SSM_STATE, GROUP_W), lambda g, c: (c, 0, g))],
                 out_shape=[jax.ShapeDtypeStruct((nb * BLOCK, SSM_INNER), F32),
                            jax.ShapeDtypeStruct((nb, SSM_STATE, SSM_INNER), F32)],
                 scratch=[pltpu.VMEM((SSM_STATE, GROUP_W), F32)],
                 sem=("parallel", "arbitrary"))(xbc, dt_x, xbc, xbc, alog_x, dsk_x)


def ssd_bwd(dy, xbc, dt_x, alog_x, dsk_x, states, comm):
    nb = xbc.shape[0] // BLOCK
    last = nb - 1
    gspec = pl.BlockSpec((BLOCK, GROUP_W), lambda g, c: (last - c, g))
    pspec = pl.BlockSpec((1, GROUP_W), lambda g, c: (0, g))
    nspec = pl.BlockSpec((BLOCK, SSM_STATE), lambda g, c: (last - c, g))

    def body(dy_ref, x_ref, dt_ref, b_ref, c_ref, al_ref, dk_ref, sp_ref,
             dx_ref, ddt_ref, db_ref, dc_ref, dal_ref, ddk_ref, ds_ref):
        @pl.when(pl.program_id(1) == 0)
        def _():
            ds_ref[...] = jnp.zeros_like(ds_ref)
            dal_ref[...] = jnp.zeros_like(dal_ref)
            ddk_ref[...] = jnp.zeros_like(ddk_ref)

        _, vjp = jax.vjp(_ssd_group, x_ref[...], dt_ref[...], b_ref[...], c_ref[...], al_ref[...], dk_ref[...],
                         sp_ref[...])
        dx_ref[...], ddt_ref[...], db_ref[...], dc_ref[...], dal, ddk, ds_ref[...] = vjp((dy_ref[...], ds_ref[...]))
        dal_ref[...] += dal
        ddk_ref[...] += ddk

    def at():
        g, c = pl.program_id(0), pl.program_id(1)
        return (g == 0) & (c == 0), (g == 0) & (c == last), (g == SSM_GROUPS - 1) & (c == last)

    return _call_with_comm(
        body, comm, at, (dy, xbc, dt_x, xbc, xbc, alog_x, dsk_x, states), name="ssd_bwd", grid=(SSM_GROUPS, nb),
        in_specs=[gspec, gspec, gspec, pl.BlockSpec((BLOCK, SSM_STATE), lambda g, c: (last - c, B_TILE0 + g)),
                  pl.BlockSpec((BLOCK, SSM_STATE), lambda g, c: (last - c, C_TILE0 + g)), pspec, pspec,
                  pl.BlockSpec((None, SSM_STATE, GROUP_W), lambda g, c: (last - c, 0, g))],
        out_specs=[gspec, gspec, nspec, nspec, pspec, pspec],
        out_shape=[jax.ShapeDtypeStruct((nb * BLOCK, SSM_INNER), F32),
                   jax.ShapeDtypeStruct((nb * BLOCK, SSM_INNER), F32),
                   jax.ShapeDtypeStruct((nb * BLOCK, SSM_GROUPS * SSM_STATE), F32),
                   jax.ShapeDtypeStruct((nb * BLOCK, SSM_GROUPS * SSM_STATE), F32),
                   jax.ShapeDtypeStruct((1, SSM_INNER), F32), jax.ShapeDtypeStruct((1, SSM_INNER), F32)],
        scratch=[pltpu.VMEM((SSM_STATE, GROUP_W), F32)])


SLAB_ROWS = 24
SLAB_META_ROW = 8


def pack_small(dcw, dcb, dgpre, dgpost, ddtb, dalog_x, ddsk_x, dsinks, dgn, dmeta, expand):
    def body(cw, cb, gpre, gpost, dtb, al, dk, sk, gn, meta, e_ref, o_ref):
        def per_head(v):
            rows = jnp.broadcast_to(v[...], (8, SSM_INNER))
            return lax.dot_general(rows, e_ref[...], _NT, precision=HI, preferred_element_type=F32)[0:1]

        o_ref[...] = jnp.zeros_like(o_ref)
        o_ref[0:CONV_WIDTH, :] = cw[...]
        o_ref[4:5, :] = cb[...]
        o_ref[5:6, 0:1024] = gpre[...]
        o_ref[5:6, 1024:2048] = gpost[...]
        o_ref[5:6, 2048:2176] = dtb[...]
        o_ref[5:6, 2176:2304] = per_head(al)
        o_ref[5:6, 2304:2432] = per_head(dk)
        o_ref[5:6, 2432:2560] = sk[...]
        o_ref[6:7, 0:SSM_INNER] = gn[...]
        o_ref[SLAB_META_ROW:SLAB_META_ROW + N_META, 0:D_MODEL] = meta[...]

    args = (dcw, dcb, dgpre, dgpost, ddtb, dalog_x, ddsk_x, dsinks, dgn, dmeta, expand)
    return _call(body, name="pack_small", in_specs=[_full(a.shape) for a in args],
                 out_specs=_full((SLAB_ROWS, CONV_DIM)), out_shape=jax.ShapeDtypeStruct((SLAB_ROWS, CONV_DIM), F32))(*args)


def _lane_tile(v):
    return jnp.pad(v, ((0, 0), (0, LANES - v.shape[1])))


def kernel(x, meta_tokens, g_pre, w_in, conv_w, conv_b, dt_bias, a_log, d_skip, attn_sinks, g_ssm_norm, w_out_att, w_out_ssm, w_out, g_post, loss_target, m_meta_tokens, m_g_pre, m_w_in, m_conv_w, m_conv_b, m_dt_bias, m_a_log, m_d_skip, m_attn_sinks, m_g_ssm_norm, m_w_out_att, m_w_out_ssm, m_w_out, m_g_post, v_meta_tokens, v_g_pre, v_w_in, v_conv_w, v_conv_b, v_dt_bias, v_a_log, v_d_skip, v_attn_sinks, v_g_ssm_norm, v_w_out_att, v_w_out_ssm, v_w_out, v_g_post):
    chip = _chip_index()

    conv_w_rows = jnp.pad(conv_w[0], ((0, 2 * 8 - CONV_WIDTH), (0, 0)))
    g_w_in, g_conv_w, g_meta = run_comm("gather_w_in", TwoLevelGather([pack_w_in(w_in[0]), conv_w_rows, meta_tokens]))
    w_all = unpack_w_in(g_w_in)
    cw_full = g_conv_w[:, :CONV_WIDTH].transpose(1, 0, 2).reshape(CONV_WIDTH, CONV_DIM)
    meta_full = g_meta.transpose(1, 0, 2).reshape(N_META, D_MODEL)

    h, u = prep(x, meta_full, g_pre)
    proj, g_w_out = mm_nn("in_proj", u, w_all, comm=TwoLevelGather(
        [w_out_att[0].astype(BF16), w_out_ssm[0].astype(BF16), w_out[0].astype(BF16)]))
    woa = g_w_out[0].reshape(D_MODEL, D_MODEL)
    wos = g_w_out[1].reshape(SSM_INNER, D_MODEL)
    wo = g_w_out[2].reshape(D_MODEL, D_MODEL)

    sinks3 = attn_sinks.reshape(ATT_Q_HEADS, 1, 1)
    a_att = attn_fwd(proj, sinks3)
    y_att = mm_nn("att_out", a_att, woa)

    xbc = conv_fwd(proj, cw_full, conv_b)
    expand = _head_expand()
    bias_x = jnp.repeat(dt_bias, HEAD_DIM, axis=1)
    dt_x = dt_fwd(proj, expand, bias_x)
    alog_x = jnp.repeat(a_log, HEAD_DIM, axis=1)
    dsk_x = jnp.repeat(d_skip, HEAD_DIM, axis=1)
    y_ssd, states = ssd_fwd(xbc, dt_x, alog_x, dsk_x)
    yn = ssm_norm_fwd(y_ssd, proj, g_ssm_norm)
    y_ssm = mm_nn("ssm_out", yn, wos)

    merged = merge_fwd(proj, y_att, y_ssm)
    out = mm_nn("out_proj", merged, wo)
    dout, dres, loss_tile, dg_post = final_loss(out, x, loss_target, g_post)
    loss = lax.psum(loss_tile[0, 0], ("x", "y", "c"))

    dmerged = mm_nt("out_proj_dx", dout, wo)
    dwo = mm_tn("out_proj_dw", merged, dout)
    dga, dgs, dy_att, dy_ssm = merge_bwd(dmerged, proj, y_att, y_ssm)

    da_att = mm_nt("att_out_dx", dy_att, woa)
    dwoa = mm_tn("att_out_dw", a_att, dy_att)
    dq, dz_att, dk, dv, dkmeta, dvmeta, dsinks3 = attn_bwd(da_att, proj, sinks3)
    dk = dk.at[PAD_ROWS:BLOCK].add(dkmeta).astype(BF16)
    dv = dv.at[PAD_ROWS:BLOCK].add(dvmeta).astype(BF16)

    dyn = mm_nt("ssm_out_dx", dy_ssm, wos)
    dwos = mm_tn("ssm_out_dw", yn, dy_ssm)
    dy_ssd, dz_ssm, dgn = ssm_norm_bwd(dyn, y_ssd, proj, g_ssm_norm)

    def pieces(g):
        return g.astype(BF16).reshape(4, 2, g.shape[0] // 8, g.shape[1])

    def to_owner(g):
        return (lambda ref, dev: ref.at[_chip_of(dev), dev[2]], (g.shape[0] // 8, g.shape[1]))

    (dxs, ddt_x, dbg, dcg, dalog_x, ddsk_x), sent_w_out = ssd_bwd(
        dy_ssd, xbc, dt_x, alog_x, dsk_x, states,
        DirectExchange([pieces(dwoa), pieces(dwos), pieces(dwo)], [to_owner(dwoa), to_owner(dwos), to_owner(dwo)],
                       ALL_MASKS, "dev", 8))
    dxbc = jnp.concatenate([dxs, dbg, dcg], axis=1)
    dxbc_raw, dcw, dcb = conv_bwd(dxbc, proj, cw_full, conv_b)
    ddt_tile, ddtb_tile = dt_bwd(ddt_x, proj, expand, bias_x)

    dproj = jnp.concatenate([dz_ssm, dxbc_raw, dq, dz_att, dga, dgs, dk, dv, ddt_tile], axis=1)
    dw_all = mm_tn("in_proj_dw", u, dproj)

    half_rows = D_MODEL // 2
    mine_and_siblings, = run_comm("pair_grads", DirectExchange(
        [pack_grad_w_in(dw_all).reshape(4, 2, half_rows, PACK_W)],
        [(lambda ref, dev: ref.at[pl.ds(0, 4), dev[2]], (4, half_rows, PACK_W))], SIBLING_MASK, "core", 2))
    chip_sum = sum_slots("sum_pair", mine_and_siblings.reshape(2, 4 * half_rows, PACK_W), BF16)
    du, (sent_w_in,) = mm_nt("in_proj_dx", dproj, w_all, comm=DirectExchange(
        [chip_sum.reshape(4, half_rows, PACK_W)], [(lambda ref, dev: ref.at[_chip_of(dev)], (half_rows, PACK_W))],
        CHIP_MASKS, "chip", 4))
    grad_x, dmeta, dg_pre = prep_bwd(h, du, dres, g_pre)

    slab = pack_small(dcw, dcb, dg_pre, dg_post, ddtb_tile, dalog_x, ddsk_x,
                      _lane_tile(dsinks3.reshape(1, ATT_Q_HEADS)), dgn, dmeta, expand)
    halves = [sum_slots("sum_" + nm, r)
              for nm, r in zip(("w_in", "w_out_att", "w_out_ssm", "w_out"), [sent_w_in] + list(sent_w_out))]
    shared = run_comm("share_grads", Both(DirectExchange(halves, [None] * 4, SIBLING_MASK, "core", 2),
                                          DirectExchange([slab], [None], ALL_MASKS, "dev", 8)))
    g_w_in_packed, g_woa, g_wos, g_wo = [f.reshape(2 * f.shape[1], f.shape[2]) for f in shared[:4]]
    small = sum_slots("sum_small", shared[4])

    g_w_in, d_w_in, nm_w_in, nv_w_in = adamw_w_in(g_w_in_packed, w_in[0], m_w_in[0], v_w_in[0])
    d_woa, nm_woa, nv_woa = adamw_rows("adamw_w_out_att", g_woa, w_out_att[0], m_w_out_att[0], v_w_out_att[0])
    d_wos, nm_wos, nv_wos = adamw_rows("adamw_w_out_ssm", g_wos, w_out_ssm[0], m_w_out_ssm[0], v_w_out_ssm[0])
    d_wo, nm_wo, nv_wo = adamw_rows("adamw_w_out", g_wo, w_out[0], m_w_out[0], v_w_out[0])

    cw_cols = CONV_DIM // 4
    meta_cols = D_MODEL // 4
    g_small = {
        "meta_tokens": lax.dynamic_slice(small, (SLAB_META_ROW, chip * meta_cols), (N_META, meta_cols)),
        "g_pre": small[5:6, 0:1024],
        "conv_w": lax.dynamic_slice(small, (0, chip * cw_cols), (CONV_WIDTH, cw_cols)),
        "conv_b": small[4:5, :],
        "dt_bias": small[5:6, 2048:2048 + SSM_HEADS],
        "a_log": small[5:6, 2176:2176 + SSM_HEADS],
        "d_skip": small[5:6, 2304:2304 + SSM_HEADS],
        "attn_sinks": small[5:6, 2432:2432 + ATT_Q_HEADS],
        "g_ssm_norm": small[6:7, 0:SSM_INNER],
        "g_post": small[5:6, 1024:2048],
    }
    names = list(g_small)
    w_small = dict(meta_tokens=meta_tokens, g_pre=g_pre, conv_w=conv_w[0], conv_b=conv_b, dt_bias=dt_bias, a_log=a_log,
                   d_skip=d_skip, attn_sinks=attn_sinks, g_ssm_norm=g_ssm_norm, g_post=g_post)
    m_small = dict(meta_tokens=m_meta_tokens, g_pre=m_g_pre, conv_w=m_conv_w[0], conv_b=m_conv_b, dt_bias=m_dt_bias,
                   a_log=m_a_log, d_skip=m_d_skip, attn_sinks=m_attn_sinks, g_ssm_norm=m_g_ssm_norm, g_post=m_g_post)
    v_small = dict(meta_tokens=v_meta_tokens, g_pre=v_g_pre, conv_w=v_conv_w[0], conv_b=v_conv_b, dt_bias=v_dt_bias,
                   a_log=v_a_log, d_skip=v_d_skip, attn_sinks=v_attn_sinks, g_ssm_norm=v_g_ssm_norm, g_post=v_g_post)
    upd = dict(zip(names, adamw_small([g_small[k] for k in names], [w_small[k] for k in names],
                                      [m_small[k] for k in names], [v_small[k] for k in names])))

    lead = {"conv_w"}

    def shaped(name, a):
        return a[None] if name in lead else a

    grads = dict(g_small, w_in=g_w_in, w_out_att=g_woa, w_out_ssm=g_wos, w_out=g_wo)
    deltas = dict({k: upd[k][0] for k in names}, w_in=d_w_in, w_out_att=d_woa, w_out_ssm=d_wos, w_out=d_wo)
    new_m = dict({k: upd[k][1] for k in names}, w_in=nm_w_in, w_out_att=nm_woa, w_out_ssm=nm_wos, w_out=nm_wo)
    new_v = dict({k: upd[k][2] for k in names}, w_in=nv_w_in, w_out_att=nv_woa, w_out_ssm=nv_wos, w_out=nv_wo)
    lead |= {"w_in", "w_out_att", "w_out_ssm", "w_out"}
    order = ["meta_tokens", "g_pre", "w_in", "conv_w", "conv_b", "dt_bias", "a_log", "d_skip", "attn_sinks",
             "g_ssm_norm", "w_out_att", "w_out_ssm", "w_out", "g_post"]
    outs = [loss, grad_x]
    for group in (grads, deltas, new_m, new_v):
        outs += [shaped(k, group[k]) for k in order]
    return tuple(outs)
```

```python
import functools

import numpy as np
import jax
import jax.numpy as jnp
from jax import lax
from jax.experimental import pallas as pl
from jax.experimental.pallas import tpu as pltpu

F32 = jnp.float32
BF16 = jnp.bfloat16
HI = lax.Precision.HIGHEST

D_MODEL = 1024
N_META = 16
BLOCK = 128
PAD_ROWS = BLOCK - N_META
NORM_EPS = 1e-6
HEAD_DIM = 64
ATT_Q_HEADS = 16
ATT_KV_HEADS = 4
ATT_GROUP = 4
SSM_INNER = 2048
SSM_HEADS = 32
SSM_GROUPS = 4
SSM_HEADS_PER_GROUP = 8
SSM_STATE = 128
CONV_WIDTH = 4
CONV_DIM = 3072
LANES = 128

ADAM_LR = 0.001
ADAM_B1 = 0.9
ADAM_B2 = 0.999
ADAM_EPS = 1e-08
ADAM_WD = 0.01
ADAM_STEP = 10

VMEM_LIMIT = 48 * 1024 * 1024

SHARD_W = 2440
PACK_W = 2560
SHARD_STRIDE = 2432
N_ALIGNED = 9856
SEG = {
    "q": (0, 1024, 5120), "k": (1024, 256, 9216), "v": (1280, 256, 9472), "z_att": (1536, 1024, 6144),
    "z_ssm": (2560, 2048, 0), "xbc": (4608, 3072, 2048), "dt": (7680, 128, 9728),
    "gate_att": (7808, 1024, 7168), "gate_ssm": (8832, 1024, 8192),
}
DT_STORED_START = 7680
DT_PAD = LANES - SSM_HEADS


def _act_col(aligned_col):
    for a0, w, p0 in SEG.values():
        if a0 <= aligned_col < a0 + w:
            return p0 + aligned_col - a0
    raise ValueError(aligned_col)


def _call(body, *, name, out_shape, in_specs, out_specs, grid=(), scratch=(), sem=None, aliases=None):
    return pl.pallas_call(
        body, out_shape=out_shape, grid=grid, in_specs=in_specs, out_specs=out_specs, scratch_shapes=list(scratch),
        name=name, input_output_aliases=aliases or {},
        compiler_params=pltpu.CompilerParams(dimension_semantics=sem, vmem_limit_bytes=VMEM_LIMIT))


def _full(shape):
    n = len(shape)
    return pl.BlockSpec(shape, lambda *_: (0,) * n)


def _chip_index():
    return lax.axis_index("x") * 2 + lax.axis_index("y")


def _silu(z):
    return z * jax.nn.sigmoid(z)


def _rms(x, g):
    return x * lax.rsqrt(jnp.mean(x * x, axis=-1, keepdims=True) + NORM_EPS) * g


def _peer(mask):
    x, y, c = lax.axis_index("x"), lax.axis_index("y"), lax.axis_index("c")
    return ((1 - x) if mask & 4 else x, (1 - y) if mask & 2 else y, (1 - c) if mask & 1 else c)


def _me():
    return lax.axis_index("x"), lax.axis_index("y"), lax.axis_index("c")


def _chip_of(dev):
    return 2 * dev[0] + dev[1]


CHIP_MASKS = (4, 2, 6)
ALL_MASKS = (1, 2, 3, 4, 5, 6, 7)
SIBLING_MASK = (1,)


SPLIT_MIN_BYTES = 1 << 20
SPLIT_PARTS = 8


def _row_parts(ref):
    shape = ref.shape
    if int(np.prod(shape)) * jnp.dtype(ref.dtype).itemsize < SPLIT_MIN_BYTES:
        return [ref]
    if len(shape) == 3:
        per = max(SPLIT_PARTS // shape[0], 1)
        rows = shape[1] // per
        return [ref.at[i, pl.ds(j * rows, rows)] for i in range(shape[0]) for j in range(per)]
    rows = shape[0] // SPLIT_PARTS
    return [ref.at[pl.ds(j * rows, rows)] for j in range(SPLIT_PARTS)]


class _Copy:
    def __init__(self, src, dst, send_sem, recv_sem=None, dev=None):
        def make(s, d):
            if dev is None:
                return pltpu.make_async_copy(s, d, send_sem)
            return pltpu.make_async_remote_copy(src_ref=s, dst_ref=d, send_sem=send_sem, recv_sem=recv_sem,
                                                device_id=dev, device_id_type=pl.DeviceIdType.MESH)

        self.whole = make(src, dst)
        self.parts = [make(s, d) for s, d in zip(_row_parts(src), _row_parts(dst))]

    def start(self):
        for p in self.parts:
            p.start()

    def wait(self):
        self.whole.wait()

    def wait_send(self):
        self.whole.wait_send()

    def wait_recv(self):
        self.whole.wait_recv()


class DirectExchange:
    def __init__(self, arrays, pieces, masks, slot_kind, nslots):
        self.arrays, self.pieces, self.masks, self.slot_kind = list(arrays), list(pieces), masks, slot_kind
        n, nk = len(arrays), len(masks)
        shapes = [a.shape if p is None else p[1] for a, p in zip(arrays, pieces)]
        self.out_shape = [jax.ShapeDtypeStruct((nslots,) + tuple(s), a.dtype) for s, a in zip(shapes, arrays)]
        self.sems = [pltpu.SemaphoreType.DMA((n * nk,)), pltpu.SemaphoreType.DMA((n * nk,)),
                     pltpu.SemaphoreType.DMA((n,))]
        self.has_mid = False

    def _copies(self, ins, outs, sems):
        send_sems, recv_sems, local_sems = sems
        me = _me()
        slot = {"chip": _chip_of(me), "dev": 4 * me[0] + 2 * me[1] + me[2], "core": me[2]}[self.slot_kind]
        nk = len(self.masks)

        def piece(a, dev):
            return ins[a] if self.pieces[a] is None else self.pieces[a][0](ins[a], dev)

        local = [_Copy(piece(a, me), outs[a].at[slot], local_sems.at[a]) for a in range(len(ins))]
        remote = []
        for a in range(len(ins)):
            for ki, mask in enumerate(self.masks):
                dev = _peer(mask)
                remote.append(_Copy(piece(a, dev), outs[a].at[slot], send_sems.at[a * nk + ki],
                                    recv_sems.at[a * nk + ki], dev))
        return local, remote

    def start(self, ins, outs, sems):
        local, remote = self._copies(ins, outs, sems)
        for cp in local + remote:
            cp.start()

    def finish(self, ins, outs, sems):
        local, remote = self._copies(ins, outs, sems)
        for cp in remote + local:
            cp.wait()


class TwoLevelGather:
    def __init__(self, arrays):
        self.arrays = list(arrays)
        n, nk = len(arrays), len(CHIP_MASKS)
        self.out_shape = [jax.ShapeDtypeStruct((4,) + a.shape, a.dtype) for a in arrays]
        self.sems = [pltpu.SemaphoreType.DMA((n * nk,)) for _ in range(4)] + [pltpu.SemaphoreType.DMA((n,))]
        self.has_mid = True

    def _copies(self, ins, outs, sems):
        ici_send, ici_recv, fwd_send, fwd_recv, local_sems = sems
        me = _me()
        sibling = _peer(1)
        nk = len(CHIP_MASKS)
        local, ici, fwd = [], [], []
        for a in range(len(ins)):
            half = ins[a].shape[0] // 2
            mine = pl.ds(me[2] * half, half)
            local.append(_Copy(ins[a], outs[a].at[_chip_of(me)], local_sems.at[a]))
            for ki, mask in enumerate(CHIP_MASKS):
                dev = _peer(mask)
                k = a * nk + ki
                ici.append(_Copy(ins[a].at[mine], outs[a].at[_chip_of(me), mine], ici_send.at[k], ici_recv.at[k], dev))
                arrived = outs[a].at[_chip_of(dev), mine]
                fwd.append(_Copy(arrived, arrived, fwd_send.at[k], fwd_recv.at[k], sibling))
        return local, ici, fwd

    def start(self, ins, outs, sems):
        local, ici, _ = self._copies(ins, outs, sems)
        for cp in local + ici:
            cp.start()

    def mid(self, ins, outs, sems):
        _, ici, fwd = self._copies(ins, outs, sems)
        for arrival, onward in zip(ici, fwd):
            arrival.wait_recv()
            onward.start()

    def finish(self, ins, outs, sems):
        local, ici, fwd = self._copies(ins, outs, sems)
        for cp in fwd:
            cp.wait_recv()
        for cp in ici + fwd:
            cp.wait_send()
        for cp in local:
            cp.wait()


class Both:
    def __init__(self, a, b):
        self.a, self.b = a, b
        self.arrays, self.out_shape, self.sems = a.arrays + b.arrays, a.out_shape + b.out_shape, a.sems + b.sems
        self.has_mid = False
        assert not (a.has_mid or b.has_mid)

    def _parts(self, ins, outs, sems):
        na, sa = len(self.a.arrays), len(self.a.sems)
        return (ins[:na], outs[:na], sems[:sa]), (ins[na:], outs[na:], sems[sa:])

    def start(self, ins, outs, sems):
        pa, pb = self._parts(ins, outs, sems)
        self.a.start(*pa)
        self.b.start(*pb)

    def finish(self, ins, outs, sems):
        pa, pb = self._parts(ins, outs, sems)
        self.a.finish(*pa)
        self.b.finish(*pb)


_ANY = pl.BlockSpec(memory_space=pl.ANY)


def run_comm(name, comm):
    n = len(comm.arrays)

    def body(*refs):
        ins, outs, sems = refs[:n], refs[n:2 * n], refs[2 * n:]
        comm.start(ins, outs, sems)
        if comm.has_mid:
            comm.mid(ins, outs, sems)
        comm.finish(ins, outs, sems)

    return pl.pallas_call(body, name=name, out_shape=comm.out_shape, in_specs=[_ANY] * n, out_specs=[_ANY] * n,
                          scratch_shapes=comm.sems)(*comm.arrays)


def _call_with_comm(body, comm, steps, args, *, name, out_shape, in_specs, out_specs, grid, scratch=()):
    ni, no, ns, nc = len(in_specs), len(out_specs), len(scratch), len(comm.arrays)

    def full_body(*refs):
        ins, cins = refs[:ni], refs[ni:ni + nc]
        outs, couts = refs[ni + nc:ni + nc + no], refs[ni + nc + no:ni + 2 * nc + no]
        scr, csems = refs[ni + 2 * nc + no:ni + 2 * nc + no + ns], refs[ni + 2 * nc + no + ns:]
        first, middle, last = steps()
        pl.when(first)(lambda: comm.start(cins, couts, csems))
        if comm.has_mid:
            pl.when(middle)(lambda: comm.mid(cins, couts, csems))
        body(*ins, *outs, *scr)
        pl.when(last)(lambda: comm.finish(cins, couts, csems))

    res = pl.pallas_call(
        full_body, name=name, out_shape=list(out_shape) + comm.out_shape, grid=grid,
        in_specs=list(in_specs) + [_ANY] * nc, out_specs=list(out_specs) + [_ANY] * nc,
        scratch_shapes=list(scratch) + comm.sems,
        compiler_params=pltpu.CompilerParams(dimension_semantics=("arbitrary",) * len(grid),
                                             vmem_limit_bytes=VMEM_LIMIT))(*args, *comm.arrays)
    return res[:no], res[no:]


def _shift_right(p, chip, col):
    if chip == 0:
        return p
    if chip < 3:
        return pltpu.roll(p, 8 * chip, 1)
    tail0 = DT_STORED_START + SSM_HEADS - 3 * SHARD_W
    head = pltpu.roll(p, 24, 1)
    tail = pltpu.roll(p, 24 + DT_PAD, 1)
    return jnp.where(col < tail0 + 24, head, jnp.where(col >= tail0 + 24 + DT_PAD, tail, 0.0))


def _shift_left(g, chip, col):
    if chip == 0:
        return g
    if chip < 3:
        return pltpu.roll(g, PACK_W - 8 * chip, 1)
    tail0 = DT_STORED_START + SSM_HEADS - 3 * SHARD_W
    return jnp.where(col < tail0, pltpu.roll(g, PACK_W - 24, 1), pltpu.roll(g, PACK_W - 24 - DT_PAD, 1))


def pack_w_in(w):
    rows = BLOCK

    def body(w_ref, o_ref, pad_ref):
        chip = _chip_index()
        pad_ref[...] = jnp.zeros_like(pad_ref)
        pad_ref[:, 0:SHARD_W] = w_ref[...]
        p = pad_ref[...]
        col = lax.broadcasted_iota(jnp.int32, p.shape, 1)
        for cv in range(4):
            @pl.when(chip == cv)
            def _():
                o_ref[...] = _shift_right(p, cv, col).astype(BF16)

    return _call(body, name="pack_w_in", grid=(D_MODEL // rows,),
                 in_specs=[pl.BlockSpec((rows, SHARD_W), lambda i: (i, 0))],
                 out_specs=pl.BlockSpec((rows, PACK_W), lambda i: (i, 0)),
                 out_shape=jax.ShapeDtypeStruct((D_MODEL, PACK_W), BF16),
                 scratch=[pltpu.VMEM((rows, PACK_W), F32)], sem=("parallel",))(w)


def _tile_runs():
    runs, fix = [], []
    for t in range(N_ALIGNED // LANES):
        s = min(t // 19, 3)
        j = t - 19 * s
        p = _act_col(t * LANES)
        if runs and runs[-1][1] == s and runs[-1][0] + runs[-1][3] == p and runs[-1][2] + runs[-1][3] == j * LANES:
            runs[-1][3] += LANES
        else:
            runs.append([p, s, j * LANES, LANES])
        if j == 0 and s > 0:
            fix.append((p, s - 1))
    return runs, fix


def unpack_w_in(bg):
    rows = BLOCK
    runs, fix = _tile_runs()

    def body(b_ref, o_ref):
        for p, s, j, w in runs:
            o_ref[:, p:p + w] = b_ref[s, :, j:j + w]
        for p, s in fix:
            o_ref[:, p:p + LANES] = o_ref[:, p:p + LANES] + b_ref[s, :, SHARD_STRIDE:PACK_W]

    return _call(body, name="unpack_w_in", grid=(D_MODEL // rows,),
                 in_specs=[pl.BlockSpec((4, rows, PACK_W), lambda i: (0, i, 0))],
                 out_specs=pl.BlockSpec((rows, N_ALIGNED), lambda i: (i, 0)),
                 out_shape=jax.ShapeDtypeStruct((D_MODEL, N_ALIGNED), BF16), sem=("parallel",))(bg)


def pack_grad_w_in(dw):
    rows = BLOCK

    def body(g_ref, o_ref):
        for s in range(4):
            for j in range(PACK_W // LANES):
                p = _act_col((19 * s + j) * LANES)
                o_ref[s, :, j * LANES:(j + 1) * LANES] = g_ref[:, p:p + LANES].astype(BF16)

    return _call(body, name="pack_grad_w_in", grid=(D_MODEL // rows,),
                 in_specs=[pl.BlockSpec((rows, N_ALIGNED), lambda i: (i, 0))],
                 out_specs=pl.BlockSpec((4, rows, PACK_W), lambda i: (0, i, 0)),
                 out_shape=jax.ShapeDtypeStruct((4, D_MODEL, PACK_W), BF16), sem=("parallel",))(dw)


def _adamw(w, g, m, v):
    m = ADAM_B1 * m + (1.0 - ADAM_B1) * g
    v = ADAM_B2 * v + (1.0 - ADAM_B2) * jnp.square(g)
    m_hat = m / (1.0 - ADAM_B1 ** ADAM_STEP)
    v_hat = v / (1.0 - ADAM_B2 ** ADAM_STEP)
    delta = -ADAM_LR * (m_hat / (jnp.sqrt(v_hat) + ADAM_EPS) + ADAM_WD * w)
    return delta, m, v


def adamw_w_in(g_packed, w, m, v):
    rows = BLOCK

    def body(g_ref, w_ref, m_ref, v_ref, go_ref, d_ref, mo_ref, vo_ref, tmp_ref):
        chip = _chip_index()
        gp = g_ref[...]
        col = lax.broadcasted_iota(jnp.int32, gp.shape, 1)
        for cv in range(4):
            @pl.when(chip == cv)
            def _():
                tmp_ref[...] = _shift_left(gp, cv, col)
        g = tmp_ref[:, 0:SHARD_W]
        d, mn, vn = _adamw(w_ref[...], g, m_ref[...], v_ref[...])
        go_ref[...] = g
        d_ref[...] = d
        mo_ref[...] = mn
        vo_ref[...] = vn

    spec = pl.BlockSpec((rows, SHARD_W), lambda i: (i, 0))
    shp = jax.ShapeDtypeStruct((D_MODEL, SHARD_W), F32)
    return _call(body, name="adamw_w_in", grid=(D_MODEL // rows,),
                 in_specs=[pl.BlockSpec((rows, PACK_W), lambda i: (i, 0)), spec, spec, spec],
                 out_specs=[spec] * 4, out_shape=[shp] * 4,
                 scratch=[pltpu.VMEM((rows, PACK_W), F32)], sem=("parallel",))(g_packed, w, m, v)


def adamw_rows(name, g, w, m, v):
    r, c = g.shape
    rows = min(r, BLOCK)

    def body(g_ref, w_ref, m_ref, v_ref, d_ref, mo_ref, vo_ref):
        d_ref[...], mo_ref[...], vo_ref[...] = _adamw(w_ref[...], g_ref[...], m_ref[...], v_ref[...])

    spec = pl.BlockSpec((rows, c), lambda i: (i, 0))
    shp = jax.ShapeDtypeStruct((r, c), F32)
    return _call(body, name=name, grid=(r // rows,), in_specs=[spec] * 4, out_specs=[spec] * 3, out_shape=[shp] * 3,
                 sem=("parallel",))(g, w, m, v)


def adamw_small(gs, ws, ms, vs):
    n = len(gs)

    def body(*refs):
        g, w, m, v = refs[:n], refs[n:2 * n], refs[2 * n:3 * n], refs[3 * n:4 * n]
        outs = refs[4 * n:]
        for i in range(n):
            d, mn, vn = _adamw(w[i][...], g[i][...], m[i][...], v[i][...])
            outs[3 * i][...] = d
            outs[3 * i + 1][...] = mn
            outs[3 * i + 2][...] = vn

    specs = [_full(a.shape) for a in gs]
    res = _call(body, name="adamw_small", in_specs=specs * 4,
                out_specs=[s for s in specs for _ in range(3)],
                out_shape=[jax.ShapeDtypeStruct(a.shape, F32) for a in gs for _ in range(3)])(*gs, *ws, *ms, *vs)
    return [tuple(res[3 * i:3 * i + 3]) for i in range(n)]


def sum_slots(name, r, out_dtype=F32):
    s, rr, c = r.shape
    rows = min(rr, BLOCK)

    def body(r_ref, o_ref):
        acc = r_ref[0].astype(F32)
        for k in range(1, s):
            acc = acc + r_ref[k].astype(F32)
        o_ref[...] = acc.astype(out_dtype)

    return _call(body, name=name, grid=(rr // rows,), in_specs=[pl.BlockSpec((s, rows, c), lambda i: (0, i, 0))],
                 out_specs=pl.BlockSpec((rows, c), lambda i: (i, 0)), out_shape=jax.ShapeDtypeStruct((rr, c), out_dtype),
                 sem=("parallel",))(r)


def _col_tile(n, k):
    if n % 896 == 0 and k <= 1024:
        return 896
    return min(n, 512)


def mm_nn(name, x, w, out_dtype=F32, comm=None):
    m, k = x.shape
    n = w.shape[1]
    tn = _col_tile(n, k)
    steps = n // tn

    def body(x_ref, w_ref, o_ref):
        o_ref[...] = jnp.dot(x_ref[...].astype(BF16), w_ref[...].astype(BF16),
                             preferred_element_type=F32).astype(out_dtype)

    kw = dict(name=name, grid=(steps,), in_specs=[_full((m, k)), pl.BlockSpec((k, tn), lambda j: (0, j))])
    if comm is None:
        return _call(body, out_specs=pl.BlockSpec((m, tn), lambda j: (0, j)),
                     out_shape=jax.ShapeDtypeStruct((m, n), out_dtype), sem=("parallel",), **kw)(x, w)

    def at():
        j = pl.program_id(0)
        return j == 0, j == steps // 2, j == steps - 1

    (res,), moved = _call_with_comm(body, comm, at, (x, w), out_specs=[pl.BlockSpec((m, tn), lambda j: (0, j))],
                                    out_shape=[jax.ShapeDtypeStruct((m, n), out_dtype)], **kw)
    return res, moved


def mm_nt(name, dy, w, out_dtype=F32, comm=None):
    m, n = dy.shape
    k = w.shape[0]
    tm = m // 2
    tn = _col_tile(n, k)
    steps = n // tn

    def at():
        i, j = pl.program_id(0), pl.program_id(1)
        return (i == 0) & (j == 0), (i == 0) & (j == steps - 1), (i == m // tm - 1) & (j == steps - 1)

    def body(dy_ref, w_ref, o_ref, acc_ref):
        j = pl.program_id(1)

        @pl.when(j == 0)
        def _():
            acc_ref[...] = jnp.zeros_like(acc_ref)

        acc_ref[...] += lax.dot_general(dy_ref[...].astype(BF16), w_ref[...].astype(BF16), (((1,), (1,)), ((), ())),
                                        preferred_element_type=F32)

        @pl.when(j == steps - 1)
        def _():
            o_ref[...] = acc_ref[...].astype(out_dtype)

    kw = dict(name=name, grid=(m // tm, steps), scratch=[pltpu.VMEM((tm, k), F32)],
              in_specs=[pl.BlockSpec((tm, tn), lambda i, j: (i, j)), pl.BlockSpec((k, tn), lambda i, j: (0, j))])
    if comm is None:
        return _call(body, out_specs=pl.BlockSpec((tm, k), lambda i, j: (i, 0)),
                     out_shape=jax.ShapeDtypeStruct((m, k), out_dtype), sem=("parallel", "arbitrary"), **kw)(dy, w)
    (res,), moved = _call_with_comm(body, comm, at, (dy, w), out_specs=[pl.BlockSpec((tm, k), lambda i, j: (i, 0))],
                                    out_shape=[jax.ShapeDtypeStruct((m, k), out_dtype)], **kw)
    return res, moved


def mm_tn(name, x, dy):
    m, k = x.shape
    n = dy.shape[1]
    tm = m // 2
    tn = _col_tile(n, k)

    def body(x_ref, dy_ref, o_ref):
        @pl.when(pl.program_id(1) == 0)
        def _():
            o_ref[...] = jnp.zeros_like(o_ref)

        o_ref[...] += lax.dot_general(x_ref[...].astype(BF16), dy_ref[...].astype(BF16), (((0,), (0,)), ((), ())),
                                      preferred_element_type=F32)

    return _call(body, name=name, grid=(n // tn, m // tm),
                 in_specs=[pl.BlockSpec((tm, k), lambda i, j: (j, 0)), pl.BlockSpec((tm, tn), lambda i, j: (j, i))],
                 out_specs=pl.BlockSpec((k, tn), lambda i, j: (0, i)), out_shape=jax.ShapeDtypeStruct((k, n), F32),
                 sem=("parallel", "arbitrary"))(x, dy)


def _row_spec(width, col_block=0):
    return pl.BlockSpec((BLOCK, width), lambda i: (i, col_block))


def _x_spec():
    return pl.BlockSpec((None, BLOCK, D_MODEL), lambda i: (0, jnp.maximum(i - 1, 0), 0))


def prep(x, meta, g_pre):
    nb = x.shape[1] // BLOCK + 1

    def body(x_ref, meta_ref, g_ref, h_ref, u_ref):
        i = pl.program_id(0)

        @pl.when(i == 0)
        def _():
            h_ref[0:PAD_ROWS, :] = jnp.zeros((PAD_ROWS, D_MODEL), F32)
            h_ref[PAD_ROWS:BLOCK, :] = meta_ref[...]

        @pl.when(i > 0)
        def _():
            h_ref[...] = x_ref[...]

        u_ref[...] = _rms(h_ref[...], g_ref[...]).astype(BF16)

    return _call(body, name="prep", grid=(nb,), in_specs=[_x_spec(), _full((N_META, D_MODEL)), _full((1, D_MODEL))],
                 out_specs=[_row_spec(D_MODEL), _row_spec(D_MODEL)],
                 out_shape=[jax.ShapeDtypeStruct((nb * BLOCK, D_MODEL), F32),
                            jax.ShapeDtypeStruct((nb * BLOCK, D_MODEL), BF16)], sem=("parallel",))(x, meta, g_pre)


def prep_bwd(h, du, dres, g_pre):
    nb = h.shape[0] // BLOCK

    def body(h_ref, du_ref, dres_ref, g_ref, gx_ref, gm_ref, gg_ref):
        i = pl.program_id(0)
        _, vjp = jax.vjp(_rms, h_ref[...], g_ref[...])
        dh, dg = vjp(du_ref[...])

        @pl.when(i == 0)
        def _():
            gm_ref[...] = dh[PAD_ROWS:BLOCK, :]
            gg_ref[...] = dg

        @pl.when(i > 0)
        def _():
            gg_ref[...] += dg

        gx_ref[...] = dh + dres_ref[...]

    return _call(body, name="prep_bwd", grid=(nb,),
                 in_specs=[_row_spec(D_MODEL), _row_spec(D_MODEL), _row_spec(D_MODEL), _full((1, D_MODEL))],
                 out_specs=[_x_spec(), _full((N_META, D_MODEL)), _full((1, D_MODEL))],
                 out_shape=[jax.ShapeDtypeStruct((1, (nb - 1) * BLOCK, D_MODEL), F32),
                            jax.ShapeDtypeStruct((N_META, D_MODEL), F32), jax.ShapeDtypeStruct((1, D_MODEL), F32)],
                 sem=("arbitrary",))(h, du, dres, g_pre)


GROUP_W = SSM_INNER // SSM_GROUPS


def _gated_norm(y, z, g):
    t = y * _silu(z)
    return t * lax.rsqrt(jnp.mean(t * t, axis=-1, keepdims=True) + NORM_EPS) * g


def ssm_norm_fwd(y, proj, g_norm):
    nb = y.shape[0] // BLOCK
    spec = pl.BlockSpec((BLOCK, GROUP_W), lambda i, g: (i, g))

    def body(y_ref, z_ref, g_ref, o_ref):
        o_ref[...] = _gated_norm(y_ref[...], z_ref[...], g_ref[...]).astype(BF16)

    return _call(body, name="ssm_norm_fwd", grid=(nb, SSM_GROUPS),
                 in_specs=[spec, spec, pl.BlockSpec((1, GROUP_W), lambda i, g: (0, g))], out_specs=spec,
                 out_shape=jax.ShapeDtypeStruct(y.shape, BF16), sem=("parallel", "parallel"))(y, proj, g_norm)


def ssm_norm_bwd(dyn, y, proj, g_norm):
    nb = y.shape[0] // BLOCK
    spec = pl.BlockSpec((BLOCK, GROUP_W), lambda g, i: (i, g))
    gspec = pl.BlockSpec((1, GROUP_W), lambda g, i: (0, g))

    def body(d_ref, y_ref, z_ref, g_ref, dy_ref, dz_ref, dg_ref):
        _, vjp = jax.vjp(_gated_norm, y_ref[...], z_ref[...], g_ref[...])
        dy, dz, dg = vjp(d_ref[...])
        dy_ref[...] = dy
        dz_ref[...] = dz.astype(BF16)

        @pl.when(pl.program_id(1) == 0)
        def _():
            dg_ref[...] = dg

        @pl.when(pl.program_id(1) > 0)
        def _():
            dg_ref[...] += dg

    return _call(body, name="ssm_norm_bwd", grid=(SSM_GROUPS, nb), in_specs=[spec, spec, spec, gspec],
                 out_specs=[spec, spec, gspec],
                 out_shape=[jax.ShapeDtypeStruct(y.shape, F32), jax.ShapeDtypeStruct(y.shape, BF16),
                            jax.ShapeDtypeStruct((1, SSM_INNER), F32)],
                 sem=("parallel", "arbitrary"))(dyn, y, proj, g_norm)


def _merge(ga, gs, ya, ys):
    return jax.nn.sigmoid(ga) * ya + jax.nn.sigmoid(gs) * ys


GATE_ATT_BLOCK = SEG["gate_att"][2] // D_MODEL
GATE_SSM_BLOCK = SEG["gate_ssm"][2] // D_MODEL


def merge_fwd(proj, y_att, y_ssm):
    nb = y_att.shape[0] // BLOCK

    def body(ga_ref, gs_ref, ya_ref, ys_ref, o_ref):
        o_ref[...] = _merge(ga_ref[...], gs_ref[...], ya_ref[...], ys_ref[...]).astype(BF16)

    return _call(body, name="merge_fwd", grid=(nb,),
                 in_specs=[_row_spec(D_MODEL, GATE_ATT_BLOCK), _row_spec(D_MODEL, GATE_SSM_BLOCK), _row_spec(D_MODEL),
                           _row_spec(D_MODEL)],
                 out_specs=_row_spec(D_MODEL), out_shape=jax.ShapeDtypeStruct(y_att.shape, BF16),
                 sem=("parallel",))(proj, proj, y_att, y_ssm)


def merge_bwd(dm, proj, y_att, y_ssm):
    nb = y_att.shape[0] // BLOCK

    def body(d_ref, ga_ref, gs_ref, ya_ref, ys_ref, dga_ref, dgs_ref, dya_ref, dys_ref):
        _, vjp = jax.vjp(_merge, ga_ref[...], gs_ref[...], ya_ref[...], ys_ref[...])
        dga, dgs, dya, dys = vjp(d_ref[...])
        dga_ref[...] = dga.astype(BF16)
        dgs_ref[...] = dgs.astype(BF16)
        dya_ref[...] = dya.astype(BF16)
        dys_ref[...] = dys.astype(BF16)

    return _call(body, name="merge_bwd", grid=(nb,),
                 in_specs=[_row_spec(D_MODEL), _row_spec(D_MODEL, GATE_ATT_BLOCK), _row_spec(D_MODEL, GATE_SSM_BLOCK),
                           _row_spec(D_MODEL), _row_spec(D_MODEL)],
                 out_specs=[_row_spec(D_MODEL)] * 4, out_shape=[jax.ShapeDtypeStruct(y_att.shape, BF16)] * 4,
                 sem=("parallel",))(dm, proj, proj, y_att, y_ssm)


def final_loss(out, x, target, g_post):
    nb = out.shape[0] // BLOCK

    def body(out_ref, x_ref, t_ref, g_ref, do_ref, dres_ref, loss_ref, dg_ref):
        i = pl.program_id(0)

        @pl.when(i == 0)
        def _():
            do_ref[...] = jnp.zeros_like(do_ref)
            dres_ref[...] = jnp.zeros_like(dres_ref)
            loss_ref[...] = jnp.zeros_like(loss_ref)
            dg_ref[...] = jnp.zeros_like(dg_ref)

        @pl.when(i > 0)
        def _():
            n, vjp = jax.vjp(_rms, out_ref[...], g_ref[...])
            diff = x_ref[...] + n - t_ref[...]
            loss_ref[...] += 0.5 * jnp.sum(diff * diff) / D_MODEL
            dn = diff * (1.0 / D_MODEL)
            do, dg = vjp(dn)
            do_ref[...] = do.astype(BF16)
            dres_ref[...] = dn
            dg_ref[...] += dg

    return _call(body, name="final_loss", grid=(nb,),
                 in_specs=[_row_spec(D_MODEL), _x_spec(), _x_spec(), _full((1, D_MODEL))],
                 out_specs=[_row_spec(D_MODEL), _row_spec(D_MODEL), _full((8, LANES)), _full((1, D_MODEL))],
                 out_shape=[jax.ShapeDtypeStruct(out.shape, BF16), jax.ShapeDtypeStruct(out.shape, F32),
                            jax.ShapeDtypeStruct((8, LANES), F32), jax.ShapeDtypeStruct((1, D_MODEL), F32)],
                 sem=("arbitrary",))(out, x, target, g_post)


_NT = (((1,), (1,)), ((), ()))
ALIBI_SLOPES = tuple(2.0 ** (-8.0 * (h + 1) / ATT_Q_HEADS) for h in range(ATT_Q_HEADS))
KV_WIDTH = ATT_KV_HEADS * HEAD_DIM
Q_BLOCK = SEG["q"][2] // D_MODEL
Z_ATT_BLOCK = SEG["z_att"][2] // D_MODEL
K_BLOCK = SEG["k"][2] // KV_WIDTH
V_BLOCK = SEG["v"][2] // KV_WIDTH
META_ROW_BLOCK = PAD_ROWS // N_META


@jax.custom_vjp
def _swap_halves(x):
    return pltpu.roll(x, HEAD_DIM, 1)


_swap_halves.defvjp(lambda x: (pltpu.roll(x, HEAD_DIM, 1), None), lambda _, g: (pltpu.roll(g, HEAD_DIM, 1),))


def _both_halves(t, half):
    first = lax.broadcasted_iota(jnp.int32, t.shape, 1) < HEAD_DIM
    sw = _swap_halves(t)
    return jnp.where(first, t, sw) if half == 0 else jnp.where(first, sw, t)


def _attn_rows(q, z, kp, kc, vp, vc, km, vm, sinks, n):
    rows = ATT_GROUP * BLOCK
    i = lax.broadcasted_iota(jnp.int32, (rows, BLOCK), 0) & (BLOCK - 1)
    j = lax.broadcasted_iota(jnp.int32, (rows, BLOCK), 1)
    rel_c = (i - j).astype(F32)
    rel_p = rel_c + float(BLOCK)
    nv = jnp.zeros((rows, BLOCK), jnp.int32) + n
    ok_c = (i >= j) & (nv >= 1)
    ok_p = (j > i) & (nv >= 2)
    im = lax.broadcasted_iota(jnp.int32, (rows, N_META), 0) & (BLOCK - 1)
    jm = lax.broadcasted_iota(jnp.int32, (rows, N_META), 1)
    ok_m = ((jnp.zeros((rows, N_META), jnp.int32) + n) >= 1) | (im >= PAD_ROWS + jm)
    first = lax.broadcasted_iota(jnp.int32, (BLOCK, LANES), 1) < HEAD_DIM
    neg = -jnp.inf
    outs = []
    for kv in range(ATT_KV_HEADS):
        tile, half = divmod(kv, 2)
        lanes = slice(tile * LANES, (tile + 1) * LANES)
        kc2, kp2, km2 = (_both_halves(t[:, lanes], half).astype(BF16) for t in (kc, kp, km))
        vc2, vp2, vm2 = (_both_halves(t[:, lanes], half).astype(BF16) for t in (vc, vp, vm))
        qs, slope, sk = [], [], []
        for pair in range(ATT_GROUP // 2):
            c0 = (kv * ATT_GROUP + 2 * pair) * HEAD_DIM
            qp = q[:, c0:c0 + LANES] * HEAD_DIM ** -0.5
            qs += [jnp.where(first, qp, 0.0), jnp.where(first, 0.0, qp)]
        for g in range(ATT_GROUP):
            slope.append(jnp.full((BLOCK, 1), ALIBI_SLOPES[kv * ATT_GROUP + g], F32))
            sk.append(jnp.broadcast_to(sinks[kv * ATT_GROUP + g], (BLOCK, 1)))
        qs = jnp.concatenate(qs, axis=0).astype(BF16)
        slope = jnp.concatenate(slope, axis=0)
        sk = jnp.concatenate(sk, axis=0)
        sc = jnp.where(ok_c, lax.dot_general(qs, kc2, _NT, preferred_element_type=F32) - slope * rel_c, neg)
        sp = jnp.where(ok_p, lax.dot_general(qs, kp2, _NT, preferred_element_type=F32) - slope * rel_p, neg)
        sm = jnp.where(ok_m, lax.dot_general(qs, km2, _NT, preferred_element_type=F32), neg)
        mx = jnp.maximum(jnp.maximum(jnp.max(sc, axis=1, keepdims=True), jnp.max(sp, axis=1, keepdims=True)),
                         jnp.maximum(jnp.max(sm, axis=1, keepdims=True), sk))
        mx = lax.stop_gradient(mx)
        ec, ep, em, es = jnp.exp(sc - mx), jnp.exp(sp - mx), jnp.exp(sm - mx), jnp.exp(sk - mx)
        den = (es + jnp.sum(ec, axis=1, keepdims=True) + jnp.sum(ep, axis=1, keepdims=True)
               + jnp.sum(em, axis=1, keepdims=True))
        inv = 1.0 / den
        o = (jnp.dot((ec * inv).astype(BF16), vc2, preferred_element_type=F32)
             + jnp.dot((ep * inv).astype(BF16), vp2, preferred_element_type=F32)
             + jnp.dot((em * inv).astype(BF16), vm2, preferred_element_type=F32))
        for pair in range(ATT_GROUP // 2):
            r0 = 2 * pair * BLOCK
            outs.append(jnp.where(first, o[r0:r0 + BLOCK], o[r0 + BLOCK:r0 + 2 * BLOCK]))
    return jnp.concatenate(outs, axis=1) * _silu(z)


def _attn_specs(nb, steps_clamped):
    def blk(t):
        return jnp.minimum(t, nb - 1) if steps_clamped else t

    wide = lambda col: pl.BlockSpec((BLOCK, D_MODEL), lambda t: (blk(t), col))
    cur = lambda col: pl.BlockSpec((BLOCK, KV_WIDTH), lambda t: (blk(t), col))
    prev = lambda col: pl.BlockSpec((BLOCK, KV_WIDTH), lambda t: (jnp.maximum(blk(t) - 1, 0), col))
    meta = lambda col: pl.BlockSpec((N_META, KV_WIDTH), lambda t: (META_ROW_BLOCK, col))
    sinks = pl.BlockSpec((ATT_Q_HEADS, 1, 1), lambda t: (0, 0, 0))
    return [wide(Q_BLOCK), wide(Z_ATT_BLOCK), prev(K_BLOCK), cur(K_BLOCK), prev(V_BLOCK), cur(V_BLOCK),
            meta(K_BLOCK), meta(V_BLOCK), sinks]


def attn_fwd(proj, sinks):
    nb = proj.shape[0] // BLOCK

    def body(q_ref, z_ref, kp_ref, kc_ref, vp_ref, vc_ref, km_ref, vm_ref, sk_ref, o_ref):
        o_ref[...] = _attn_rows(q_ref[...], z_ref[...], kp_ref[...], kc_ref[...], vp_ref[...], vc_ref[...],
                                km_ref[...], vm_ref[...], tuple(sk_ref[h] for h in range(ATT_Q_HEADS)),
                                pl.program_id(0)).astype(BF16)

    return _call(body, name="attn_fwd", grid=(nb,), in_specs=_attn_specs(nb, False), out_specs=_row_spec(D_MODEL),
                 out_shape=jax.ShapeDtypeStruct((nb * BLOCK, D_MODEL), BF16), sem=("parallel",))(*([proj] * 8), sinks)


def attn_bwd(da, proj, sinks):
    nb = proj.shape[0] // BLOCK
    last = nb - 1
    wide = pl.BlockSpec((BLOCK, D_MODEL), lambda t: (jnp.minimum(t, last), 0))
    done = pl.BlockSpec((BLOCK, KV_WIDTH), lambda t: (jnp.maximum(t - 1, 0), 0))
    meta = _full((N_META, KV_WIDTH))
    par = _full((ATT_Q_HEADS, 1, 1))

    def body(da_ref, q_ref, z_ref, kp_ref, kc_ref, vp_ref, vc_ref, km_ref, vm_ref, sk_ref,
             dq_ref, dz_ref, dk_ref, dv_ref, dkm_ref, dvm_ref, dsk_ref, ck_ref, cv_ref):
        t = pl.program_id(0)

        @pl.when(t == 0)
        def _():
            ck_ref[...] = jnp.zeros_like(ck_ref)
            cv_ref[...] = jnp.zeros_like(cv_ref)
            dkm_ref[...] = jnp.zeros_like(dkm_ref)
            dvm_ref[...] = jnp.zeros_like(dvm_ref)
            dsk_ref[...] = jnp.zeros_like(dsk_ref)

        @pl.when(t < nb)
        def _():
            def f(q, z, kp, kc, vp, vc, km, vm, sk):
                return _attn_rows(q, z, kp, kc, vp, vc, km, vm, sk, t)

            _, vjp = jax.vjp(f, q_ref[...], z_ref[...], kp_ref[...], kc_ref[...], vp_ref[...], vc_ref[...],
                             km_ref[...], vm_ref[...], tuple(sk_ref[h] for h in range(ATT_Q_HEADS)))
            dq, dz, dkp, dkc, dvp, dvc, dkm, dvm, dsk = vjp(da_ref[...])
            dq_ref[...] = dq.astype(BF16)
            dz_ref[...] = dz.astype(BF16)
            for h in range(ATT_Q_HEADS):
                dsk_ref[h] += dsk[h]
            dk_ref[...] = ck_ref[...] + dkp
            dv_ref[...] = cv_ref[...] + dvp
            ck_ref[...] = dkc
            cv_ref[...] = dvc
            dkm_ref[...] += dkm
            dvm_ref[...] += dvm

        @pl.when(t == nb)
        def _():
            dk_ref[...] = ck_ref[...]
            dv_ref[...] = cv_ref[...]

    rows = nb * BLOCK
    return _call(body, name="attn_bwd", grid=(nb + 1,), in_specs=[wide] + _attn_specs(nb, True),
                 out_specs=[wide, wide, done, done, meta, meta, par],
                 out_shape=[jax.ShapeDtypeStruct((rows, D_MODEL), BF16), jax.ShapeDtypeStruct((rows, D_MODEL), BF16),
                            jax.ShapeDtypeStruct((rows, KV_WIDTH), F32), jax.ShapeDtypeStruct((rows, KV_WIDTH), F32),
                            jax.ShapeDtypeStruct((N_META, KV_WIDTH), F32), jax.ShapeDtypeStruct((N_META, KV_WIDTH), F32),
                            jax.ShapeDtypeStruct(sinks.shape, F32)],
                 scratch=[pltpu.VMEM((BLOCK, KV_WIDTH), F32), pltpu.VMEM((BLOCK, KV_WIDTH), F32)],
                 sem=("arbitrary",))(da, *([proj] * 8), sinks)


XBC_BLOCK0 = SEG["xbc"][2] // D_MODEL
CONV_COL_BLOCKS = CONV_DIM // D_MODEL
DT_TILE = SEG["dt"][2] // LANES


def _shift_rows(cur, prev, j, row):
    if j == 0:
        return cur
    return jnp.where(row >= j, pltpu.roll(cur, j, 0), pltpu.roll(prev, j, 0))


def _conv_pre(cur, prev, w_ref, b_ref, row):
    pre = b_ref[...] + w_ref[CONV_WIDTH - 1:CONV_WIDTH, :] * cur
    for k in range(CONV_WIDTH - 1):
        pre = pre + w_ref[k:k + 1, :] * _shift_rows(cur, prev, CONV_WIDTH - 1 - k, row)
    return pre


def conv_fwd(proj, conv_w, conv_b):
    nb = proj.shape[0] // BLOCK
    cur = pl.BlockSpec((BLOCK, D_MODEL), lambda j, i: (i, XBC_BLOCK0 + j))
    prev = pl.BlockSpec((BLOCK, D_MODEL), lambda j, i: (jnp.maximum(i - 1, 0), XBC_BLOCK0 + j))

    def body(c_ref, p_ref, w_ref, b_ref, o_ref):
        i = pl.program_id(1)
        row = lax.broadcasted_iota(jnp.int32, (BLOCK, D_MODEL), 0)
        prevv = p_ref[...] * jnp.where(i > 0, 1.0, 0.0)
        pre = _conv_pre(c_ref[...], prevv, w_ref, b_ref, row)
        valid = jnp.maximum((row >= PAD_ROWS).astype(F32), jnp.where(i > 0, 1.0, 0.0))
        o_ref[...] = _silu(pre) * valid

    return _call(body, name="conv_fwd", grid=(CONV_COL_BLOCKS, nb),
                 in_specs=[cur, prev, pl.BlockSpec((CONV_WIDTH, D_MODEL), lambda j, i: (0, j)),
                           pl.BlockSpec((1, D_MODEL), lambda j, i: (0, j))],
                 out_specs=pl.BlockSpec((BLOCK, D_MODEL), lambda j, i: (i, j)),
                 out_shape=jax.ShapeDtypeStruct((nb * BLOCK, CONV_DIM), F32),
                 sem=("parallel", "parallel"))(proj, proj, conv_w, conv_b)


def conv_bwd(dxbc, proj, conv_w, conv_b):
    nb = proj.shape[0] // BLOCK
    last = nb - 1
    cur = pl.BlockSpec((BLOCK, D_MODEL), lambda j, i: (i, XBC_BLOCK0 + j))
    prev = pl.BlockSpec((BLOCK, D_MODEL), lambda j, i: (jnp.maximum(i - 1, 0), XBC_BLOCK0 + j))
    nxt = pl.BlockSpec((BLOCK, D_MODEL), lambda j, i: (jnp.minimum(i + 1, last), XBC_BLOCK0 + j))
    dcur = pl.BlockSpec((BLOCK, D_MODEL), lambda j, i: (i, j))
    dnxt = pl.BlockSpec((BLOCK, D_MODEL), lambda j, i: (jnp.minimum(i + 1, last), j))
    wspec = pl.BlockSpec((CONV_WIDTH, D_MODEL), lambda j, i: (0, j))
    bspec = pl.BlockSpec((1, D_MODEL), lambda j, i: (0, j))

    def body(dc_ref, dn_ref, c_ref, p_ref, n_ref, w_ref, b_ref, du_ref, dw_ref, db_ref):
        i = pl.program_id(1)
        row = lax.broadcasted_iota(jnp.int32, (BLOCK, D_MODEL), 0)
        curv = c_ref[...]
        prevv = p_ref[...] * jnp.where(i > 0, 1.0, 0.0)

        def dpre_of(pre, d, valid):
            s = jax.nn.sigmoid(pre)
            return d * valid * (s * (1.0 + pre * (1.0 - s)))

        valid = jnp.maximum((row >= PAD_ROWS).astype(F32), jnp.where(i > 0, 1.0, 0.0))
        dp_c = dpre_of(_conv_pre(curv, prevv, w_ref, b_ref, row), dc_ref[...], valid)
        has_next = jnp.where(i < last, 1.0, 0.0)
        dp_n = dpre_of(_conv_pre(n_ref[...], curv, w_ref, b_ref, row), dn_ref[...], has_next)
        du = w_ref[CONV_WIDTH - 1:CONV_WIDTH, :] * dp_c
        for j in range(1, CONV_WIDTH):
            up = jnp.where(row < BLOCK - j, pltpu.roll(dp_c, BLOCK - j, 0), pltpu.roll(dp_n, BLOCK - j, 0))
            du = du + w_ref[CONV_WIDTH - 1 - j:CONV_WIDTH - j, :] * up
        du_ref[...] = du.astype(BF16)

        @pl.when(i == 0)
        def _():
            dw_ref[...] = jnp.zeros_like(dw_ref)
            db_ref[...] = jnp.zeros_like(db_ref)

        for k in range(CONV_WIDTH):
            dw_ref[k:k + 1, :] += jnp.sum(dp_c * _shift_rows(curv, prevv, CONV_WIDTH - 1 - k, row), axis=0,
                                          keepdims=True)
        db_ref[...] += jnp.sum(dp_c, axis=0, keepdims=True)

    return _call(body, name="conv_bwd", grid=(CONV_COL_BLOCKS, nb),
                 in_specs=[dcur, dnxt, cur, prev, nxt, wspec, bspec], out_specs=[dcur, wspec, bspec],
                 out_shape=[jax.ShapeDtypeStruct((nb * BLOCK, CONV_DIM), BF16),
                            jax.ShapeDtypeStruct((CONV_WIDTH, CONV_DIM), F32), jax.ShapeDtypeStruct((1, CONV_DIM), F32)],
                 sem=("parallel", "arbitrary"))(dxbc, dxbc, proj, proj, proj, conv_w, conv_b)


def _head_expand():
    e = np.zeros((LANES, SSM_INNER), np.float32)
    for h in range(SSM_HEADS):
        e[h, h * HEAD_DIM:(h + 1) * HEAD_DIM] = 1.0
    return jnp.asarray(e)


def _softplus(x):
    return jnp.maximum(x, 0.0) + jnp.log(1.0 + jnp.exp(-jnp.abs(x)))


def dt_fwd(proj, expand, bias_x):
    nb = proj.shape[0] // BLOCK

    def body(t_ref, e_ref, b_ref, o_ref):
        raw = jnp.dot(t_ref[...], e_ref[...], precision=HI, preferred_element_type=F32)
        o_ref[...] = _softplus(raw + b_ref[...])

    return _call(body, name="dt_fwd", grid=(nb,),
                 in_specs=[_row_spec(LANES, DT_TILE), _full((LANES, SSM_INNER)), _full((1, SSM_INNER))],
                 out_specs=_row_spec(SSM_INNER), out_shape=jax.ShapeDtypeStruct((nb * BLOCK, SSM_INNER), F32),
                 sem=("parallel",))(proj, expand, bias_x)


def dt_bwd(ddt_x, proj, expand, bias_x):
    nb = proj.shape[0] // BLOCK

    def body(d_ref, t_ref, e_ref, b_ref, o_ref, db_ref):
        raw = jnp.dot(t_ref[...], e_ref[...], precision=HI, preferred_element_type=F32)
        draw = d_ref[...] * jax.nn.sigmoid(raw + b_ref[...])
        dt = lax.dot_general(draw, e_ref[...], _NT, precision=HI, preferred_element_type=F32)
        o_ref[...] = dt.astype(BF16)

        @pl.when(pl.program_id(0) == 0)
        def _():
            db_ref[...] = jnp.zeros_like(db_ref)

        db_ref[...] += jnp.sum(dt, axis=0, keepdims=True)

    return _call(body, name="dt_bwd", grid=(nb,),
                 in_specs=[_row_spec(SSM_INNER), _row_spec(LANES, DT_TILE), _full((LANES, SSM_INNER)),
                           _full((1, SSM_INNER))],
                 out_specs=[_row_spec(LANES), _full((1, LANES))],
                 out_shape=[jax.ShapeDtypeStruct((nb * BLOCK, LANES), BF16), jax.ShapeDtypeStruct((1, LANES), F32)],
                 sem=("arbitrary",))(ddt_x, proj, expand, bias_x)


def _ssd_group(xs, dtx, bg, cg, alog, dsk, state):
    l = lax.broadcasted_iota(jnp.int32, (BLOCK, BLOCK), 0)
    s = lax.broadcasted_iota(jnp.int32, (BLOCK, BLOCK), 1)
    causal = l >= s
    first_head = s < HEAD_DIM
    a = dtx * (-jnp.exp(alog))
    cs = jnp.dot(causal.astype(F32), a, precision=HI, preferred_element_type=F32)
    tot = jnp.sum(a, axis=0, keepdims=True)
    bb, cb16 = bg.astype(BF16), cg.astype(BF16)
    cb = lax.dot_general(cb16, bb, _NT, preferred_element_type=F32)
    xr = xs * dtx
    y_diag = []
    for p in range(GROUP_W // LANES):
        lanes = slice(p * LANES, (p + 1) * LANES)
        c_pair = cs[:, lanes]
        c_swap = _swap_halves(c_pair)
        m = []
        for c_head in (jnp.where(first_head, c_pair, c_swap), jnp.where(first_head, c_swap, c_pair)):
            m.append(cb * jnp.exp(jnp.where(causal, c_head - c_head.T, -jnp.inf)))
        x_pair = xr[:, lanes]
        x_diag = jnp.concatenate([jnp.where(first_head, x_pair, 0.0), jnp.where(first_head, 0.0, x_pair)], axis=0)
        y_diag.append(jnp.dot(jnp.concatenate(m, axis=1).astype(BF16), x_diag.astype(BF16),
                              preferred_element_type=F32))
    st = lax.dot_general(bb, (xr * jnp.exp(tot - cs)).astype(BF16), (((0,), (0,)), ((), ())),
                         preferred_element_type=F32)
    new_state = state * jnp.exp(tot) + st
    y_off = jnp.dot(cb16, state.astype(BF16), preferred_element_type=F32) * jnp.exp(cs)
    return jnp.concatenate(y_diag, axis=1) + y_off + dsk * xs, new_state


B_TILE0 = SSM_INNER // LANES
C_TILE0 = B_TILE0 + SSM_GROUPS


def ssd_fwd(xbc, dt_x, alog_x, dsk_x):
    nb = xbc.shape[0] // BLOCK
    gspec = pl.BlockSpec((BLOCK, GROUP_W), lambda g, c: (c, g))
    pspec = pl.BlockSpec((1, GROUP_W), lambda g, c: (0, g))

    def body(x_ref, dt_ref, b_ref, c_ref, al_ref, dk_ref, y_ref, sp_ref, st_ref):
        @pl.when(pl.program_id(1) == 0)
        def _():
            st_ref[...] = jnp.zeros_like(st_ref)

        state = st_ref[...]
        sp_ref[...] = state
        y_ref[...], st_ref[...] = _ssd_group(x_ref[...], dt_ref[...], b_ref[...], c_ref[...], al_ref[...],
                                             dk_ref[...], state)

    return _call(body, name="ssd_fwd", grid=(SSM_GROUPS, nb),
                 in_specs=[gspec, gspec, pl.BlockSpec((BLOCK, SSM_STATE), lambda g, c: (c, B_TILE0 + g)),
                           pl.BlockSpec((BLOCK, SSM_STATE), lambda g, c: (c, C_TILE0 + g)), pspec, pspec],
                 out_specs=[gspec, pl.BlockSpec((None, SSM_STATE, GROUP_W), lambda g, c: (c, 0, g))],
                 out_shape=[jax.ShapeDtypeStruct((nb * BLOCK, SSM_INNER), F32),
                            jax.ShapeDtypeStruct((nb, SSM_STATE, SSM_INNER), F32)],
                 scratch=[pltpu.VMEM((SSM_STATE, GROUP_W), F32)],
                 sem=("parallel", "arbitrary"))(xbc, dt_x, xbc, xbc, alog_x, dsk_x)


def ssd_bwd(dy, xbc, dt_x, alog_x, dsk_x, states, comm):
    nb = xbc.shape[0] // BLOCK
    last = nb - 1
    gspec = pl.BlockSpec((BLOCK, GROUP_W), lambda g, c: (last - c, g))
    pspec = pl.BlockSpec((1, GROUP_W), lambda g, c: (0, g))
    nspec = pl.BlockSpec((BLOCK, SSM_STATE), lambda g, c: (last - c, g))

    def body(dy_ref, x_ref, dt_ref, b_ref, c_ref, al_ref, dk_ref, sp_ref,
             dx_ref, ddt_ref, db_ref, dc_ref, dal_ref, ddk_ref, ds_ref):
        @pl.when(pl.program_id(1) == 0)
        def _():
            ds_ref[...] = jnp.zeros_like(ds_ref)
            dal_ref[...] = jnp.zeros_like(dal_ref)
            ddk_ref[...] = jnp.zeros_like(ddk_ref)

        _, vjp = jax.vjp(_ssd_group, x_ref[...], dt_ref[...], b_ref[...], c_ref[...], al_ref[...], dk_ref[...],
                         sp_ref[...])
        dx_ref[...], ddt_ref[...], db_ref[...], dc_ref[...], dal, ddk, ds_ref[...] = vjp((dy_ref[...], ds_ref[...]))
        dal_ref[...] += dal
        ddk_ref[...] += ddk

    def at():
        g, c = pl.program_id(0), pl.program_id(1)
        return (g == 0) & (c == 0), (g == 0) & (c == last), (g == SSM_GROUPS - 1) & (c == last)

    return _call_with_comm(
        body, comm, at, (dy, xbc, dt_x, xbc, xbc, alog_x, dsk_x, states), name="ssd_bwd", grid=(SSM_GROUPS, nb),
        in_specs=[gspec, gspec, gspec, pl.BlockSpec((BLOCK, SSM_STATE), lambda g, c: (last - c, B_TILE0 + g)),
                  pl.BlockSpec((BLOCK, SSM_STATE), lambda g, c: (last - c, C_TILE0 + g)), pspec, pspec,
                  pl.BlockSpec((None, SSM_STATE, GROUP_W), lambda g, c: (last - c, 0, g))],
        out_specs=[gspec, gspec, nspec, nspec, pspec, pspec],
        out_shape=[jax.ShapeDtypeStruct((nb * BLOCK, SSM_INNER), F32),
                   jax.ShapeDtypeStruct((nb * BLOCK, SSM_INNER), F32),
                   jax.ShapeDtypeStruct((nb * BLOCK, SSM_GROUPS * SSM_STATE), F32),
                   jax.ShapeDtypeStruct((nb * BLOCK, SSM_GROUPS * SSM_STATE), F32),
                   jax.ShapeDtypeStruct((1, SSM_INNER), F32), jax.ShapeDtypeStruct((1, SSM_INNER), F32)],
        scratch=[pltpu.VMEM((SSM_STATE, GROUP_W), F32)])


SLAB_ROWS = 24
SLAB_META_ROW = 8


def pack_small(dcw, dcb, dgpre, dgpost, ddtb, dalog_x, ddsk_x, dsinks, dgn, dmeta, expand):
    def body(cw, cb, gpre, gpost, dtb, al, dk, sk, gn, meta, e_ref, o_ref):
        def per_head(v):
            rows = jnp.broadcast_to(v[...], (8, SSM_INNER))
            return lax.dot_general(rows, e_ref[...], _NT, precision=HI, preferred_element_type=F32)[0:1]

        o_ref[...] = jnp.zeros_like(o_ref)
        o_ref[0:CONV_WIDTH, :] = cw[...]
        o_ref[4:5, :] = cb[...]
        o_ref[5:6, 0:1024] = gpre[...]
        o_ref[5:6, 1024:2048] = gpost[...]
        o_ref[5:6, 2048:2176] = dtb[...]
        o_ref[5:6, 2176:2304] = per_head(al)
        o_ref[5:6, 2304:2432] = per_head(dk)
        o_ref[5:6, 2432:2560] = sk[...]
        o_ref[6:7, 0:SSM_INNER] = gn[...]
        o_ref[SLAB_META_ROW:SLAB_META_ROW + N_META, 0:D_MODEL] = meta[...]

    args = (dcw, dcb, dgpre, dgpost, ddtb, dalog_x, ddsk_x, dsinks, dgn, dmeta, expand)
    return _call(body, name="pack_small", in_specs=[_full(a.shape) for a in args],
                 out_specs=_full((SLAB_ROWS, CONV_DIM)), out_shape=jax.ShapeDtypeStruct((SLAB_ROWS, CONV_DIM), F32))(*args)


def _lane_tile(v):
    return jnp.pad(v, ((0, 0), (0, LANES - v.shape[1])))


def kernel(x, meta_tokens, g_pre, w_in, conv_w, conv_b, dt_bias, a_log, d_skip, attn_sinks, g_ssm_norm, w_out_att, w_out_ssm, w_out, g_post, loss_target, m_meta_tokens, m_g_pre, m_w_in, m_conv_w, m_conv_b, m_dt_bias, m_a_log, m_d_skip, m_attn_sinks, m_g_ssm_norm, m_w_out_att, m_w_out_ssm, m_w_out, m_g_post, v_meta_tokens, v_g_pre, v_w_in, v_conv_w, v_conv_b, v_dt_bias, v_a_log, v_d_skip, v_attn_sinks, v_g_ssm_norm, v_w_out_att, v_w_out_ssm, v_w_out, v_g_post):
    chip = _chip_index()

    conv_w_rows = jnp.pad(conv_w[0], ((0, 2 * 8 - CONV_WIDTH), (0, 0)))
    g_w_in, g_conv_w, g_meta = run_comm("gather_w_in", TwoLevelGather([pack_w_in(w_in[0]), conv_w_rows, meta_tokens]))
    w_all = unpack_w_in(g_w_in)
    cw_full = g_conv_w[:, :CONV_WIDTH].transpose(1, 0, 2).reshape(CONV_WIDTH, CONV_DIM)
    meta_full = g_meta.transpose(1, 0, 2).reshape(N_META, D_MODEL)

    h, u = prep(x, meta_full, g_pre)
    proj, g_w_out = mm_nn("in_proj", u, w_all, comm=TwoLevelGather(
        [w_out_att[0].astype(BF16), w_out_ssm[0].astype(BF16), w_out[0].astype(BF16)]))
    woa = g_w_out[0].reshape(D_MODEL, D_MODEL)
    wos = g_w_out[1].reshape(SSM_INNER, D_MODEL)
    wo = g_w_out[2].reshape(D_MODEL, D_MODEL)

    sinks3 = attn_sinks.reshape(ATT_Q_HEADS, 1, 1)
    a_att = attn_fwd(proj, sinks3)
    y_att = mm_nn("att_out", a_att, woa)

    xbc = conv_fwd(proj, cw_full, conv_b)
    expand = _head_expand()
    bias_x = jnp.repeat(dt_bias, HEAD_DIM, axis=1)
    dt_x = dt_fwd(proj, expand, bias_x)
    alog_x = jnp.repeat(a_log, HEAD_DIM, axis=1)
    dsk_x = jnp.repeat(d_skip, HEAD_DIM, axis=1)
    y_ssd, states = ssd_fwd(xbc, dt_x, alog_x, dsk_x)
    yn = ssm_norm_fwd(y_ssd, proj, g_ssm_norm)
    y_ssm = mm_nn("ssm_out", yn, wos)

    merged = merge_fwd(proj, y_att, y_ssm)
    out = mm_nn("out_proj", merged, wo)
    dout, dres, loss_tile, dg_post = final_loss(out, x, loss_target, g_post)
    loss = lax.psum(loss_tile[0, 0], ("x", "y", "c"))

    dmerged = mm_nt("out_proj_dx", dout, wo)
    dwo = mm_tn("out_proj_dw", merged, dout)
    dga, dgs, dy_att, dy_ssm = merge_bwd(dmerged, proj, y_att, y_ssm)

    da_att = mm_nt("att_out_dx", dy_att, woa)
    dwoa = mm_tn("att_out_dw", a_att, dy_att)
    dq, dz_att, dk, dv, dkmeta, dvmeta, dsinks3 = attn_bwd(da_att, proj, sinks3)
    dk = dk.at[PAD_ROWS:BLOCK].add(dkmeta).astype(BF16)
    dv = dv.at[PAD_ROWS:BLOCK].add(dvmeta).astype(BF16)

    dyn = mm_nt("ssm_out_dx", dy_ssm, wos)
    dwos = mm_tn("ssm_out_dw", yn, dy_ssm)
    dy_ssd, dz_ssm, dgn = ssm_norm_bwd(dyn, y_ssd, proj, g_ssm_norm)

    def pieces(g):
        return g.astype(BF16).reshape(4, 2, g.shape[0] // 8, g.shape[1])

    def to_owner(g):
        return (lambda ref, dev: ref.at[_chip_of(dev), dev[2]], (g.shape[0] // 8, g.shape[1]))

    (dxs, ddt_x, dbg, dcg, dalog_x, ddsk_x), sent_w_out = ssd_bwd(
        dy_ssd, xbc, dt_x, alog_x, dsk_x, states,
        DirectExchange([pieces(dwoa), pieces(dwos), pieces(dwo)], [to_owner(dwoa), to_owner(dwos), to_owner(dwo)],
                       ALL_MASKS, "dev", 8))
    dxbc = jnp.concatenate([dxs, dbg, dcg], axis=1)
    dxbc_raw, dcw, dcb = conv_bwd(dxbc, proj, cw_full, conv_b)
    ddt_tile, ddtb_tile = dt_bwd(ddt_x, proj, expand, bias_x)

    dproj = jnp.concatenate([dz_ssm, dxbc_raw, dq, dz_att, dga, dgs, dk, dv, ddt_tile], axis=1)
    dw_all = mm_tn("in_proj_dw", u, dproj)

    half_rows = D_MODEL // 2
    mine_and_siblings, = run_comm("pair_grads", DirectExchange(
        [pack_grad_w_in(dw_all).reshape(4, 2, half_rows, PACK_W)],
        [(lambda ref, dev: ref.at[pl.ds(0, 4), dev[2]], (4, half_rows, PACK_W))], SIBLING_MASK, "core", 2))
    chip_sum = sum_slots("sum_pair", mine_and_siblings.reshape(2, 4 * half_rows, PACK_W), BF16)
    du, (sent_w_in,) = mm_nt("in_proj_dx", dproj, w_all, comm=DirectExchange(
        [chip_sum.reshape(4, half_rows, PACK_W)], [(lambda ref, dev: ref.at[_chip_of(dev)], (half_rows, PACK_W))],
        CHIP_MASKS, "chip", 4))
    grad_x, dmeta, dg_pre = prep_bwd(h, du, dres, g_pre)

    slab = pack_small(dcw, dcb, dg_pre, dg_post, ddtb_tile, dalog_x, ddsk_x,
                      _lane_tile(dsinks3.reshape(1, ATT_Q_HEADS)), dgn, dmeta, expand)
    halves = [sum_slots("sum_" + nm, r)
              for nm, r in zip(("w_in", "w_out_att", "w_out_ssm", "w_out"), [sent_w_in] + list(sent_w_out))]
    shared = run_comm("share_grads", Both(DirectExchange(halves, [None] * 4, SIBLING_MASK, "core", 2),
                                          DirectExchange([slab], [None], ALL_MASKS, "dev", 8)))
    g_w_in_packed, g_woa, g_wos, g_wo = [f.reshape(2 * f.shape[1], f.shape[2]) for f in shared[:4]]
    small = sum_slots("sum_small", shared[4])

    g_w_in, d_w_in, nm_w_in, nv_w_in = adamw_w_in(g_w_in_packed, w_in[0], m_w_in[0], v_w_in[0])
    d_woa, nm_woa, nv_woa = adamw_rows("adamw_w_out_att", g_woa, w_out_att[0], m_w_out_att[0], v_w_out_att[0])
    d_wos, nm_wos, nv_wos = adamw_rows("adamw_w_out_ssm", g_wos, w_out_ssm[0], m_w_out_ssm[0], v_w_out_ssm[0])
    d_wo, nm_wo, nv_wo = adamw_rows("adamw_w_out", g_wo, w_out[0], m_w_out[0], v_w_out[0])

    cw_cols = CONV_DIM // 4
    meta_cols = D_MODEL // 4
    g_small = {
        "meta_tokens": lax.dynamic_slice(small, (SLAB_META_ROW, chip * meta_cols), (N_META, meta_cols)),
        "g_pre": small[5:6, 0:1024],
        "conv_w": lax.dynamic_slice(small, (0, chip * cw_cols), (CONV_WIDTH, cw_cols)),
        "conv_b": small[4:5, :],
        "dt_bias": small[5:6, 2048:2048 + SSM_HEADS],
        "a_log": small[5:6, 2176:2176 + SSM_HEADS],
        "d_skip": small[5:6, 2304:2304 + SSM_HEADS],
        "attn_sinks": small[5:6, 2432:2432 + ATT_Q_HEADS],
        "g_ssm_norm": small[6:7, 0:SSM_INNER],
        "g_post": small[5:6, 1024:2048],
    }
    names = list(g_small)
    w_small = dict(meta_tokens=meta_tokens, g_pre=g_pre, conv_w=conv_w[0], conv_b=conv_b, dt_bias=dt_bias, a_log=a_log,
                   d_skip=d_skip, attn_sinks=attn_sinks, g_ssm_norm=g_ssm_norm, g_post=g_post)
    m_small = dict(meta_tokens=m_meta_tokens, g_pre=m_g_pre, conv_w=m_conv_w[0], conv_b=m_conv_b, dt_bias=m_dt_bias,
                   a_log=m_a_log, d_skip=m_d_skip, attn_sinks=m_attn_sinks, g_ssm_norm=m_g_ssm_norm, g_post=m_g_post)
    v_small = dict(meta_tokens=v_meta_tokens, g_pre=v_g_pre, conv_w=v_conv_w[0], conv_b=v_conv_b, dt_bias=v_dt_bias,
                   a_log=v_a_log, d_skip=v_d_skip, attn_sinks=v_attn_sinks, g_ssm_norm=v_g_ssm_norm, g_post=v_g_post)
    upd = dict(zip(names, adamw_small([g_small[k] for k in names], [w_small[k] for k in names],
                                      [m_small[k] for k in names], [v_small[k] for k in names])))

    lead = {"conv_w"}

    def shaped(name, a):
        return a[None] if name in lead else a

    grads = dict(g_small, w_in=g_w_in, w_out_att=g_woa, w_out_ssm=g_wos, w_out=g_wo)
    deltas = dict({k: upd[k][0] for k in names}, w_in=d_w_in, w_out_att=d_woa, w_out_ssm=d_wos, w_out=d_wo)
    new_m = dict({k: upd[k][1] for k in names}, w_in=nm_w_in, w_out_att=nm_woa, w_out_ssm=nm_wos, w_out=nm_wo)
    new_v = dict({k: upd[k][2] for k in names}, w_in=nv_w_in, w_out_att=nv_woa, w_out_ssm=nv_wos, w_out=nv_wo)
    lead |= {"w_in", "w_out_att", "w_out_ssm", "w_out"}
    order = ["meta_tokens", "g_pre", "w_in", "conv_w", "conv_b", "dt_bias", "a_log", "d_skip", "attn_sinks",
             "g_ssm_norm", "w_out_att", "w_out_ssm", "w_out", "g_post"]
    outs = [loss, grad_x]
    for group in (grads, deltas, new_m, new_v):
        outs += [shaped(k, group[k]) for k in order]
    return tuple(outs)
```

```python
import functools

import numpy as np
import jax
import jax.numpy as jnp
from jax import lax
from jax.experimental import pallas as pl
from jax.experimental.pallas import tpu as pltpu

F32 = jnp.float32
BF16 = jnp.bfloat16
HI = lax.Precision.HIGHEST

D_MODEL = 1024
N_META = 16
BLOCK = 128
PAD_ROWS = BLOCK - N_META
NORM_EPS = 1e-6
HEAD_DIM = 64
ATT_Q_HEADS = 16
ATT_KV_HEADS = 4
ATT_GROUP = 4
SSM_INNER = 2048
SSM_HEADS = 32
SSM_GROUPS = 4
SSM_HEADS_PER_GROUP = 8
SSM_STATE = 128
CONV_WIDTH = 4
CONV_DIM = 3072
LANES = 128

ADAM_LR = 0.001
ADAM_B1 = 0.9
ADAM_B2 = 0.999
ADAM_EPS = 1e-08
ADAM_WD = 0.01
ADAM_STEP = 10

VMEM_LIMIT = 48 * 1024 * 1024

SHARD_W = 2440
PACK_W = 2560
SHARD_STRIDE = 2432
N_ALIGNED = 9856
SEG = {
    "q": (0, 1024, 5120), "k": (1024, 256, 9216), "v": (1280, 256, 9472), "z_att": (1536, 1024, 6144),
    "z_ssm": (2560, 2048, 0), "xbc": (4608, 3072, 2048), "dt": (7680, 128, 9728),
    "gate_att": (7808, 1024, 7168), "gate_ssm": (8832, 1024, 8192),
}
DT_STORED_START = 7680
DT_PAD = LANES - SSM_HEADS


def _act_col(aligned_col):
    for a0, w, p0 in SEG.values():
        if a0 <= aligned_col < a0 + w:
            return p0 + aligned_col - a0
    raise ValueError(aligned_col)


def _call(body, *, name, out_shape, in_specs, out_specs, grid=(), scratch=(), sem=None, aliases=None):
    return pl.pallas_call(
        body, out_shape=out_shape, grid=grid, in_specs=in_specs, out_specs=out_specs, scratch_shapes=list(scratch),
        name=name, input_output_aliases=aliases or {},
        compiler_params=pltpu.CompilerParams(dimension_semantics=sem, vmem_limit_bytes=VMEM_LIMIT))


def _full(shape):
    n = len(shape)
    return pl.BlockSpec(shape, lambda *_: (0,) * n)


def _chip_index():
    return lax.axis_index("x") * 2 + lax.axis_index("y")


def _silu(z):
    return z * jax.nn.sigmoid(z)


def _rms(x, g):
    return x * lax.rsqrt(jnp.mean(x * x, axis=-1, keepdims=True) + NORM_EPS) * g


def _peer(mask):
    x, y, c = lax.axis_index("x"), lax.axis_index("y"), lax.axis_index("c")
    return ((1 - x) if mask & 4 else x, (1 - y) if mask & 2 else y, (1 - c) if mask & 1 else c)


def _me():
    return lax.axis_index("x"), lax.axis_index("y"), lax.axis_index("c")


def _chip_of(dev):
    return 2 * dev[0] + dev[1]


CHIP_MASKS = (4, 2, 6)
ALL_MASKS = (1, 2, 3, 4, 5, 6, 7)
SIBLING_MASK = (1,)


def _remote(src, dst, send_sem, recv_sem, dev):
    return pltpu.make_async_remote_copy(src_ref=src, dst_ref=dst, send_sem=send_sem, recv_sem=recv_sem,
                                        device_id=dev, device_id_type=pl.DeviceIdType.MESH)


class _StagedCopy:
    def __init__(self, src, stage, dst, load_sem, store_sem):
        self.load = pltpu.make_async_copy(src, stage, load_sem)
        self.store = pltpu.make_async_copy(stage, dst, store_sem)

    def start(self):
        self.load.start()
        self.load.wait()
        self.store.start()

    def wait(self):
        self.store.wait()


class DirectExchange:
    def __init__(self, arrays, pieces, masks, slot_kind, nslots, keep_own=True):
        self.arrays, self.pieces, self.masks, self.slot_kind = list(arrays), list(pieces), masks, slot_kind
        self.keep_own = keep_own
        n, nk = len(arrays), len(masks)
        shapes = [a.shape if p is None else p[1] for a, p in zip(arrays, pieces)]
        self.out_shape = [jax.ShapeDtypeStruct((nslots,) + tuple(s), a.dtype) for s, a in zip(shapes, arrays)]
        self.scratch = [pltpu.SemaphoreType.DMA((n * nk,)), pltpu.SemaphoreType.DMA((n * nk,))]
        if keep_own:
            self.scratch += [pltpu.SemaphoreType.DMA((2 * n,))] + [pltpu.VMEM(s, a.dtype) for s, a in zip(shapes, arrays)]
        self.has_mid = False

    def _copies(self, ins, outs, scratch):
        send_sems, recv_sems = scratch[:2]
        me = _me()
        slot = {"chip": _chip_of(me), "dev": 4 * me[0] + 2 * me[1] + me[2], "core": me[2]}[self.slot_kind]
        nk = len(self.masks)

        def piece(a, dev):
            return ins[a] if self.pieces[a] is None else self.pieces[a][0](ins[a], dev)

        local = []
        if self.keep_own:
            local_sems, stages = scratch[2], scratch[3:]
            local = [_StagedCopy(piece(a, me), stages[a], outs[a].at[slot], local_sems.at[2 * a], local_sems.at[2 * a + 1])
                     for a in range(len(ins))]
        remote = []
        for a in range(len(ins)):
            for ki, mask in enumerate(self.masks):
                dev = _peer(mask)
                remote.append(_remote(piece(a, dev), outs[a].at[slot], send_sems.at[a * nk + ki],
                                      recv_sems.at[a * nk + ki], dev))
        return local, remote

    def start(self, ins, outs, scratch):
        local, remote = self._copies(ins, outs, scratch)
        for cp in remote + local:
            cp.start()

    def finish(self, ins, outs, scratch):
        local, remote = self._copies(ins, outs, scratch)
        for cp in remote + local:
            cp.wait()


class TwoLevelGather:
    def __init__(self, arrays):
        self.arrays = list(arrays)
        n, nk = len(arrays), len(CHIP_MASKS)
        self.out_shape = [jax.ShapeDtypeStruct((4,) + a.shape, a.dtype) for a in arrays]
        self.scratch = ([pltpu.SemaphoreType.DMA((n * nk,)) for _ in range(4)] + [pltpu.SemaphoreType.DMA((2 * n,))]
                        + [pltpu.VMEM(a.shape, a.dtype) for a in arrays])
        self.has_mid = True

    def _copies(self, ins, outs, scratch):
        ici_send, ici_recv, fwd_send, fwd_recv, local_sems = scratch[:5]
        stages = scratch[5:]
        me = _me()
        sibling = _peer(1)
        nk = len(CHIP_MASKS)
        local, ici, fwd = [], [], []
        for a in range(len(ins)):
            half = ins[a].shape[0] // 2
            mine = pl.ds(me[2] * half, half)
            local.append(_StagedCopy(ins[a], stages[a], outs[a].at[_chip_of(me)], local_sems.at[2 * a],
                                     local_sems.at[2 * a + 1]))
            for ki, mask in enumerate(CHIP_MASKS):
                dev = _peer(mask)
                k = a * nk + ki
                ici.append(_remote(ins[a].at[mine], outs[a].at[_chip_of(me), mine], ici_send.at[k], ici_recv.at[k], dev))
                arrived = outs[a].at[_chip_of(dev), mine]
                fwd.append(_remote(arrived, arrived, fwd_send.at[k], fwd_recv.at[k], sibling))
        return local, ici, fwd

    def start(self, ins, outs, scratch):
        local, ici, _ = self._copies(ins, outs, scratch)
        for cp in ici + local:
            cp.start()

    def mid(self, ins, outs, scratch):
        _, ici, fwd = self._copies(ins, outs, scratch)
        for arrival, onward in zip(ici, fwd):
            arrival.wait_recv()
            onward.start()

    def finish(self, ins, outs, scratch):
        local, ici, fwd = self._copies(ins, outs, scratch)
        for cp in fwd:
            cp.wait_recv()
        for cp in ici + fwd:
            cp.wait_send()
        for cp in local:
            cp.wait()


class Both:
    def __init__(self, a, b):
        self.a, self.b = a, b
        self.arrays, self.out_shape = a.arrays + b.arrays, a.out_shape + b.out_shape
        self.scratch = a.scratch + b.scratch
        self.has_mid = False
        assert not (a.has_mid or b.has_mid)

    def _parts(self, ins, outs, sems):
        na, sa = len(self.a.arrays), len(self.a.scratch)
        return (ins[:na], outs[:na], sems[:sa]), (ins[na:], outs[na:], sems[sa:])

    def start(self, ins, outs, sems):
        pa, pb = self._parts(ins, outs, sems)
        self.a.start(*pa)
        self.b.start(*pb)

    def finish(self, ins, outs, sems):
        pa, pb = self._parts(ins, outs, sems)
        self.a.finish(*pa)
        self.b.finish(*pb)


_ANY = pl.BlockSpec(memory_space=pl.ANY)


def run_comm(name, comm):
    n = len(comm.arrays)

    def body(*refs):
        ins, outs, sems = refs[:n], refs[n:2 * n], refs[2 * n:]
        comm.start(ins, outs, sems)
        if comm.has_mid:
            comm.mid(ins, outs, sems)
        comm.finish(ins, outs, sems)

    return pl.pallas_call(body, name=name, out_shape=comm.out_shape, in_specs=[_ANY] * n, out_specs=[_ANY] * n,
                          scratch_shapes=comm.scratch,
                          compiler_params=pltpu.CompilerParams(vmem_limit_bytes=VMEM_LIMIT))(*comm.arrays)


def _call_with_comm(body, comm, steps, args, *, name, out_shape, in_specs, out_specs, grid, scratch=()):
    ni, no, ns, nc = len(in_specs), len(out_specs), len(scratch), len(comm.arrays)

    def full_body(*refs):
        ins, cins = refs[:ni], refs[ni:ni + nc]
        outs, couts = refs[ni + nc:ni + nc + no], refs[ni + nc + no:ni + 2 * nc + no]
        scr, csems = refs[ni + 2 * nc + no:ni + 2 * nc + no + ns], refs[ni + 2 * nc + no + ns:]
        first, middle, last = steps()
        pl.when(first)(lambda: comm.start(cins, couts, csems))
        if comm.has_mid:
            pl.when(middle)(lambda: comm.mid(cins, couts, csems))
        body(*ins, *outs, *scr)
        pl.when(last)(lambda: comm.finish(cins, couts, csems))

    res = pl.pallas_call(
        full_body, name=name, out_shape=list(out_shape) + comm.out_shape, grid=grid,
        in_specs=list(in_specs) + [_ANY] * nc, out_specs=list(out_specs) + [_ANY] * nc,
        scratch_shapes=list(scratch) + comm.scratch,
        compiler_params=pltpu.CompilerParams(dimension_semantics=("arbitrary",) * len(grid),
                                             vmem_limit_bytes=VMEM_LIMIT))(*args, *comm.arrays)
    return res[:no], res[no:]


def _shift_right(p, chip, col):
    if chip == 0:
        return p
    if chip < 3:
        return pltpu.roll(p, 8 * chip, 1)
    tail0 = DT_STORED_START + SSM_HEADS - 3 * SHARD_W
    head = pltpu.roll(p, 24, 1)
    tail = pltpu.roll(p, 24 + DT_PAD, 1)
    return jnp.where(col < tail0 + 24, head, jnp.where(col >= tail0 + 24 + DT_PAD, tail, 0.0))


def _shift_left(g, chip, col):
    if chip == 0:
        return g
    if chip < 3:
        return pltpu.roll(g, PACK_W - 8 * chip, 1)
    tail0 = DT_STORED_START + SSM_HEADS - 3 * SHARD_W
    return jnp.where(col < tail0, pltpu.roll(g, PACK_W - 24, 1), pltpu.roll(g, PACK_W - 24 - DT_PAD, 1))


def pack_w_in(w):
    rows = BLOCK

    def body(w_ref, o_ref, pad_ref):
        chip = _chip_index()
        pad_ref[...] = jnp.zeros_like(pad_ref)
        pad_ref[:, 0:SHARD_W] = w_ref[...]
        p = pad_ref[...]
        col = lax.broadcasted_iota(jnp.int32, p.shape, 1)
        for cv in range(4):
            @pl.when(chip == cv)
            def _():
                o_ref[...] = _shift_right(p, cv, col).astype(BF16)

    return _call(body, name="pack_w_in", grid=(D_MODEL // rows,),
                 in_specs=[pl.BlockSpec((rows, SHARD_W), lambda i: (i, 0))],
                 out_specs=pl.BlockSpec((rows, PACK_W), lambda i: (i, 0)),
                 out_shape=jax.ShapeDtypeStruct((D_MODEL, PACK_W), BF16),
                 scratch=[pltpu.VMEM((rows, PACK_W), F32)], sem=("parallel",))(w)


def _tile_runs():
    runs, fix = [], []
    for t in range(N_ALIGNED // LANES):
        s = min(t // 19, 3)
        j = t - 19 * s
        p = _act_col(t * LANES)
        if runs and runs[-1][1] == s and runs[-1][0] + runs[-1][3] == p and runs[-1][2] + runs[-1][3] == j * LANES:
            runs[-1][3] += LANES
        else:
            runs.append([p, s, j * LANES, LANES])
        if j == 0 and s > 0:
            fix.append((p, s - 1))
    return runs, fix


def unpack_w_in(bg):
    rows = BLOCK
    runs, fix = _tile_runs()

    def body(b_ref, o_ref):
        for p, s, j, w in runs:
            o_ref[:, p:p + w] = b_ref[s, :, j:j + w]
        for p, s in fix:
            o_ref[:, p:p + LANES] = o_ref[:, p:p + LANES] + b_ref[s, :, SHARD_STRIDE:PACK_W]

    return _call(body, name="unpack_w_in", grid=(D_MODEL // rows,),
                 in_specs=[pl.BlockSpec((4, rows, PACK_W), lambda i: (0, i, 0))],
                 out_specs=pl.BlockSpec((rows, N_ALIGNED), lambda i: (i, 0)),
                 out_shape=jax.ShapeDtypeStruct((D_MODEL, N_ALIGNED), BF16), sem=("parallel",))(bg)


def pack_grad_w_in(dw):
    rows = BLOCK

    def body(g_ref, o_ref):
        for s in range(4):
            for j in range(PACK_W // LANES):
                p = _act_col((19 * s + j) * LANES)
                o_ref[s, :, j * LANES:(j + 1) * LANES] = g_ref[:, p:p + LANES].astype(BF16)

    return _call(body, name="pack_grad_w_in", grid=(D_MODEL // rows,),
                 in_specs=[pl.BlockSpec((rows, N_ALIGNED), lambda i: (i, 0))],
                 out_specs=pl.BlockSpec((4, rows, PACK_W), lambda i: (0, i, 0)),
                 out_shape=jax.ShapeDtypeStruct((4, D_MODEL, PACK_W), BF16), sem=("parallel",))(dw)


def _adamw(w, g, m, v):
    m = ADAM_B1 * m + (1.0 - ADAM_B1) * g
    v = ADAM_B2 * v + (1.0 - ADAM_B2) * jnp.square(g)
    m_hat = m / (1.0 - ADAM_B1 ** ADAM_STEP)
    v_hat = v / (1.0 - ADAM_B2 ** ADAM_STEP)
    delta = -ADAM_LR * (m_hat / (jnp.sqrt(v_hat) + ADAM_EPS) + ADAM_WD * w)
    return delta, m, v


def adamw_w_in(g_packed, w, m, v):
    rows = BLOCK

    def body(g_ref, w_ref, m_ref, v_ref, go_ref, d_ref, mo_ref, vo_ref, tmp_ref):
        chip = _chip_index()
        gp = g_ref[...]
        col = lax.broadcasted_iota(jnp.int32, gp.shape, 1)
        for cv in range(4):
            @pl.when(chip == cv)
            def _():
                tmp_ref[...] = _shift_left(gp, cv, col)
        g = tmp_ref[:, 0:SHARD_W]
        d, mn, vn = _adamw(w_ref[...], g, m_ref[...], v_ref[...])
        go_ref[...] = g
        d_ref[...] = d
        mo_ref[...] = mn
        vo_ref[...] = vn

    spec = pl.BlockSpec((rows, SHARD_W), lambda i: (i, 0))
    shp = jax.ShapeDtypeStruct((D_MODEL, SHARD_W), F32)
    return _call(body, name="adamw_w_in", grid=(D_MODEL // rows,),
                 in_specs=[pl.BlockSpec((rows, PACK_W), lambda i: (i, 0)), spec, spec, spec],
                 out_specs=[spec] * 4, out_shape=[shp] * 4,
                 scratch=[pltpu.VMEM((rows, PACK_W), F32)], sem=("parallel",))(g_packed, w, m, v)


def adamw_rows(name, g, w, m, v):
    r, c = g.shape
    rows = min(r, BLOCK)

    def body(g_ref, w_ref, m_ref, v_ref, d_ref, mo_ref, vo_ref):
        d_ref[...], mo_ref[...], vo_ref[...] = _adamw(w_ref[...], g_ref[...], m_ref[...], v_ref[...])

    spec = pl.BlockSpec((rows, c), lambda i: (i, 0))
    shp = jax.ShapeDtypeStruct((r, c), F32)
    return _call(body, name=name, grid=(r // rows,), in_specs=[spec] * 4, out_specs=[spec] * 3, out_shape=[shp] * 3,
                 sem=("parallel",))(g, w, m, v)


def adamw_small(gs, ws, ms, vs):
    n = len(gs)

    def body(*refs):
        g, w, m, v = refs[:n], refs[n:2 * n], refs[2 * n:3 * n], refs[3 * n:4 * n]
        outs = refs[4 * n:]
        for i in range(n):
            d, mn, vn = _adamw(w[i][...], g[i][...], m[i][...], v[i][...])
            outs[3 * i][...] = d
            outs[3 * i + 1][...] = mn
            outs[3 * i + 2][...] = vn

    specs = [_full(a.shape) for a in gs]
    res = _call(body, name="adamw_small", in_specs=specs * 4,
                out_specs=[s for s in specs for _ in range(3)],
                out_shape=[jax.ShapeDtypeStruct(a.shape, F32) for a in gs for _ in range(3)])(*gs, *ws, *ms, *vs)
    return [tuple(res[3 * i:3 * i + 3]) for i in range(n)]


def sum_slots(name, r):
    s, rr, c = r.shape
    rows = min(rr, BLOCK)

    def body(r_ref, o_ref):
        acc = r_ref[0].astype(F32)
        for k in range(1, s):
            acc = acc + r_ref[k].astype(F32)
        o_ref[...] = acc

    return _call(body, name=name, grid=(rr // rows,), in_specs=[pl.BlockSpec((s, rows, c), lambda i: (0, i, 0))],
                 out_specs=pl.BlockSpec((rows, c), lambda i: (i, 0)), out_shape=jax.ShapeDtypeStruct((rr, c), F32),
                 sem=("parallel",))(r)


def sum_pair(partial, from_sibling):
    s, _, rr, cols = partial.shape

    def body(p_ref, r_ref, o_ref):
        c = lax.axis_index("c")
        o_ref[...] = (p_ref[c].astype(F32) + r_ref[1 - c].astype(F32)).astype(BF16)

    return _call(body, name="sum_pair", grid=(s, rr // BLOCK),
                 in_specs=[pl.BlockSpec((None, 2, BLOCK, cols), lambda k, i: (k, 0, i, 0)),
                           pl.BlockSpec((2, None, BLOCK, cols), lambda k, i: (0, k, i, 0))],
                 out_specs=pl.BlockSpec((None, BLOCK, cols), lambda k, i: (k, i, 0)),
                 out_shape=jax.ShapeDtypeStruct((s, rr, cols), BF16), sem=("parallel", "parallel"))(partial, from_sibling)


def _col_tile(n, k):
    if n % 896 == 0 and k <= 1024:
        return 896
    return min(n, 512)


def mm_nn(name, x, w, out_dtype=F32, comm=None):
    m, k = x.shape
    n = w.shape[1]
    tn = _col_tile(n, k)
    steps = n // tn

    def body(x_ref, w_ref, o_ref):
        o_ref[...] = jnp.dot(x_ref[...].astype(BF16), w_ref[...].astype(BF16),
                             preferred_element_type=F32).astype(out_dtype)

    kw = dict(name=name, grid=(steps,), in_specs=[_full((m, k)), pl.BlockSpec((k, tn), lambda j: (0, j))])
    if comm is None:
        return _call(body, out_specs=pl.BlockSpec((m, tn), lambda j: (0, j)),
                     out_shape=jax.ShapeDtypeStruct((m, n), out_dtype), sem=("parallel",), **kw)(x, w)

    def at():
        j = pl.program_id(0)
        return j == 0, j == steps // 2, j == steps - 1

    (res,), moved = _call_with_comm(body, comm, at, (x, w), out_specs=[pl.BlockSpec((m, tn), lambda j: (0, j))],
                                    out_shape=[jax.ShapeDtypeStruct((m, n), out_dtype)], **kw)
    return res, moved


def mm_nt(name, dy, w, out_dtype=F32, comm=None):
    m, n = dy.shape
    k = w.shape[0]
    tm = m // 2
    tn = _col_tile(n, k)
    steps = n // tn

    def at():
        i, j = pl.program_id(0), pl.program_id(1)
        return (i == 0) & (j == 0), (i == 0) & (j == steps - 1), (i == m // tm - 1) & (j == steps - 1)

    def body(dy_ref, w_ref, o_ref, acc_ref):
        j = pl.program_id(1)

        @pl.when(j == 0)
        def _():
            acc_ref[...] = jnp.zeros_like(acc_ref)

        acc_ref[...] += lax.dot_general(dy_ref[...].astype(BF16), w_ref[...].astype(BF16), (((1,), (1,)), ((), ())),
                                        preferred_element_type=F32)

        @pl.when(j == steps - 1)
        def _():
            o_ref[...] = acc_ref[...].astype(out_dtype)

    kw = dict(name=name, grid=(m // tm, steps), scratch=[pltpu.VMEM((tm, k), F32)],
              in_specs=[pl.BlockSpec((tm, tn), lambda i, j: (i, j)), pl.BlockSpec((k, tn), lambda i, j: (0, j))])
    if comm is None:
        return _call(body, out_specs=pl.BlockSpec((tm, k), lambda i, j: (i, 0)),
                     out_shape=jax.ShapeDtypeStruct((m, k), out_dtype), sem=("parallel", "arbitrary"), **kw)(dy, w)
    (res,), moved = _call_with_comm(body, comm, at, (dy, w), out_specs=[pl.BlockSpec((tm, k), lambda i, j: (i, 0))],
                                    out_shape=[jax.ShapeDtypeStruct((m, k), out_dtype)], **kw)
    return res, moved


def mm_tn(name, x, dy):
    m, k = x.shape
    n = dy.shape[1]
    tm = m // 2
    tn = _col_tile(n, k)

    def body(x_ref, dy_ref, o_ref):
        @pl.when(pl.program_id(1) == 0)
        def _():
            o_ref[...] = jnp.zeros_like(o_ref)

        o_ref[...] += lax.dot_general(x_ref[...].astype(BF16), dy_ref[...].astype(BF16), (((0,), (0,)), ((), ())),
                                      preferred_element_type=F32)

    return _call(body, name=name, grid=(n // tn, m // tm),
                 in_specs=[pl.BlockSpec((tm, k), lambda i, j: (j, 0)), pl.BlockSpec((tm, tn), lambda i, j: (j, i))],
                 out_specs=pl.BlockSpec((k, tn), lambda i, j: (0, i)), out_shape=jax.ShapeDtypeStruct((k, n), F32),
                 sem=("parallel", "arbitrary"))(x, dy)


def _row_spec(width, col_block=0):
    return pl.BlockSpec((BLOCK, width), lambda i: (i, col_block))


def _x_spec():
    return pl.BlockSpec((None, BLOCK, D_MODEL), lambda i: (0, jnp.maximum(i - 1, 0), 0))


def prep(x, meta, g_pre):
    nb = x.shape[1] // BLOCK + 1

    def body(x_ref, meta_ref, g_ref, h_ref, u_ref):
        i = pl.program_id(0)

        @pl.when(i == 0)
        def _():
            h_ref[0:PAD_ROWS, :] = jnp.zeros((PAD_ROWS, D_MODEL), F32)
            h_ref[PAD_ROWS:BLOCK, :] = meta_ref[...]

        @pl.when(i > 0)
        def _():
            h_ref[...] = x_ref[...]

        u_ref[...] = _rms(h_ref[...], g_ref[...]).astype(BF16)

    return _call(body, name="prep", grid=(nb,), in_specs=[_x_spec(), _full((N_META, D_MODEL)), _full((1, D_MODEL))],
                 out_specs=[_row_spec(D_MODEL), _row_spec(D_MODEL)],
                 out_shape=[jax.ShapeDtypeStruct((nb * BLOCK, D_MODEL), F32),
                            jax.ShapeDtypeStruct((nb * BLOCK, D_MODEL), BF16)], sem=("parallel",))(x, meta, g_pre)


def prep_bwd(h, du, dres, g_pre):
    nb = h.shape[0] // BLOCK

    def body(h_ref, du_ref, dres_ref, g_ref, gx_ref, gm_ref, gg_ref):
        i = pl.program_id(0)
        _, vjp = jax.vjp(_rms, h_ref[...], g_ref[...])
        dh, dg = vjp(du_ref[...])

        @pl.when(i == 0)
        def _():
            gm_ref[...] = dh[PAD_ROWS:BLOCK, :]
            gg_ref[...] = dg

        @pl.when(i > 0)
        def _():
            gg_ref[...] += dg

        gx_ref[...] = dh + dres_ref[...]

    return _call(body, name="prep_bwd", grid=(nb,),
                 in_specs=[_row_spec(D_MODEL), _row_spec(D_MODEL), _row_spec(D_MODEL), _full((1, D_MODEL))],
                 out_specs=[_x_spec(), _full((N_META, D_MODEL)), _full((1, D_MODEL))],
                 out_shape=[jax.ShapeDtypeStruct((1, (nb - 1) * BLOCK, D_MODEL), F32),
                            jax.ShapeDtypeStruct((N_META, D_MODEL), F32), jax.ShapeDtypeStruct((1, D_MODEL), F32)],
                 sem=("arbitrary",))(h, du, dres, g_pre)


GROUP_W = SSM_INNER // SSM_GROUPS


def _gated_norm(y, z, g):
    t = y * _silu(z)
    return t * lax.rsqrt(jnp.mean(t * t, axis=-1, keepdims=True) + NORM_EPS) * g


def ssm_norm_fwd(y, proj, g_norm):
    nb = y.shape[0] // BLOCK
    spec = pl.BlockSpec((BLOCK, GROUP_W), lambda i, g: (i, g))

    def body(y_ref, z_ref, g_ref, o_ref):
        o_ref[...] = _gated_norm(y_ref[...], z_ref[...], g_ref[...]).astype(BF16)

    return _call(body, name="ssm_norm_fwd", grid=(nb, SSM_GROUPS),
                 in_specs=[spec, spec, pl.BlockSpec((1, GROUP_W), lambda i, g: (0, g))], out_specs=spec,
                 out_shape=jax.ShapeDtypeStruct(y.shape, BF16), sem=("parallel", "parallel"))(y, proj, g_norm)


def ssm_norm_bwd(dyn, y, proj, g_norm):
    nb = y.shape[0] // BLOCK
    spec = pl.BlockSpec((BLOCK, GROUP_W), lambda g, i: (i, g))
    gspec = pl.BlockSpec((1, GROUP_W), lambda g, i: (0, g))

    def body(d_ref, y_ref, z_ref, g_ref, dy_ref, dz_ref, dg_ref):
        _, vjp = jax.vjp(_gated_norm, y_ref[...], z_ref[...], g_ref[...])
        dy, dz, dg = vjp(d_ref[...])
        dy_ref[...] = dy
        dz_ref[...] = dz.astype(BF16)

        @pl.when(pl.program_id(1) == 0)
        def _():
            dg_ref[...] = dg

        @pl.when(pl.program_id(1) > 0)
        def _():
            dg_ref[...] += dg

    return _call(body, name="ssm_norm_bwd", grid=(SSM_GROUPS, nb), in_specs=[spec, spec, spec, gspec],
                 out_specs=[spec, spec, gspec],
                 out_shape=[jax.ShapeDtypeStruct(y.shape, F32), jax.ShapeDtypeStruct(y.shape, BF16),
                            jax.ShapeDtypeStruct((1, SSM_INNER), F32)],
                 sem=("parallel", "arbitrary"))(dyn, y, proj, g_norm)


def _merge(ga, gs, ya, ys):
    return jax.nn.sigmoid(ga) * ya + jax.nn.sigmoid(gs) * ys


GATE_ATT_BLOCK = SEG["gate_att"][2] // D_MODEL
GATE_SSM_BLOCK = SEG["gate_ssm"][2] // D_MODEL


def merge_fwd(proj, y_att, y_ssm):
    nb = y_att.shape[0] // BLOCK

    def body(ga_ref, gs_ref, ya_ref, ys_ref, o_ref):
        o_ref[...] = _merge(ga_ref[...], gs_ref[...], ya_ref[...], ys_ref[...]).astype(BF16)

    return _call(body, name="merge_fwd", grid=(nb,),
                 in_specs=[_row_spec(D_MODEL, GATE_ATT_BLOCK), _row_spec(D_MODEL, GATE_SSM_BLOCK), _row_spec(D_MODEL),
                           _row_spec(D_MODEL)],
                 out_specs=_row_spec(D_MODEL), out_shape=jax.ShapeDtypeStruct(y_att.shape, BF16),
                 sem=("parallel",))(proj, proj, y_att, y_ssm)


def merge_bwd(dm, proj, y_att, y_ssm):
    nb = y_att.shape[0] // BLOCK

    def body(d_ref, ga_ref, gs_ref, ya_ref, ys_ref, dga_ref, dgs_ref, dya_ref, dys_ref):
        _, vjp = jax.vjp(_merge, ga_ref[...], gs_ref[...], ya_ref[...], ys_ref[...])
        dga, dgs, dya, dys = vjp(d_ref[...])
        dga_ref[...] = dga.astype(BF16)
        dgs_ref[...] = dgs.astype(BF16)
        dya_ref[...] = dya.astype(BF16)
        dys_ref[...] = dys.astype(BF16)

    return _call(body, name="merge_bwd", grid=(nb,),
                 in_specs=[_row_spec(D_MODEL), _row_spec(D_MODEL, GATE_ATT_BLOCK), _row_spec(D_MODEL, GATE_SSM_BLOCK),
                           _row_spec(D_MODEL), _row_spec(D_MODEL)],
                 out_specs=[_row_spec(D_MODEL)] * 4, out_shape=[jax.ShapeDtypeStruct(y_att.shape, BF16)] * 4,
                 sem=("parallel",))(dm, proj, proj, y_att, y_ssm)


def final_loss(out, x, target, g_post):
    nb = out.shape[0] // BLOCK

    def body(out_ref, x_ref, t_ref, g_ref, do_ref, dres_ref, loss_ref, dg_ref):
        i = pl.program_id(0)

        @pl.when(i == 0)
        def _():
            do_ref[...] = jnp.zeros_like(do_ref)
            dres_ref[...] = jnp.zeros_like(dres_ref)
            loss_ref[...] = jnp.zeros_like(loss_ref)
            dg_ref[...] = jnp.zeros_like(dg_ref)

        @pl.when(i > 0)
        def _():
            n, vjp = jax.vjp(_rms, out_ref[...], g_ref[...])
            diff = x_ref[...] + n - t_ref[...]
            loss_ref[...] += 0.5 * jnp.sum(diff * diff) / D_MODEL
            dn = diff * (1.0 / D_MODEL)
            do, dg = vjp(dn)
            do_ref[...] = do.astype(BF16)
            dres_ref[...] = dn
            dg_ref[...] += dg

    return _call(body, name="final_loss", grid=(nb,),
                 in_specs=[_row_spec(D_MODEL), _x_spec(), _x_spec(), _full((1, D_MODEL))],
                 out_specs=[_row_spec(D_MODEL), _row_spec(D_MODEL), _full((8, LANES)), _full((1, D_MODEL))],
                 out_shape=[jax.ShapeDtypeStruct(out.shape, BF16), jax.ShapeDtypeStruct(out.shape, F32),
                            jax.ShapeDtypeStruct((8, LANES), F32), jax.ShapeDtypeStruct((1, D_MODEL), F32)],
                 sem=("arbitrary",))(out, x, target, g_post)


_NT = (((1,), (1,)), ((), ()))
ALIBI_SLOPES = tuple(2.0 ** (-8.0 * (h + 1) / ATT_Q_HEADS) for h in range(ATT_Q_HEADS))
KV_WIDTH = ATT_KV_HEADS * HEAD_DIM
Q_BLOCK = SEG["q"][2] // D_MODEL
Z_ATT_BLOCK = SEG["z_att"][2] // D_MODEL
K_BLOCK = SEG["k"][2] // KV_WIDTH
V_BLOCK = SEG["v"][2] // KV_WIDTH
META_ROW_BLOCK = PAD_ROWS // N_META


@jax.custom_vjp
def _swap_halves(x):
    return pltpu.roll(x, HEAD_DIM, 1)


_swap_halves.defvjp(lambda x: (pltpu.roll(x, HEAD_DIM, 1), None), lambda _, g: (pltpu.roll(g, HEAD_DIM, 1),))


def _both_halves(t, half):
    first = lax.broadcasted_iota(jnp.int32, t.shape, 1) < HEAD_DIM
    sw = _swap_halves(t)
    return jnp.where(first, t, sw) if half == 0 else jnp.where(first, sw, t)


def _attn_rows(q, z, kp, kc, vp, vc, km, vm, sinks, n):
    rows = ATT_GROUP * BLOCK
    i = lax.broadcasted_iota(jnp.int32, (rows, BLOCK), 0) & (BLOCK - 1)
    j = lax.broadcasted_iota(jnp.int32, (rows, BLOCK), 1)
    rel_c = (i - j).astype(F32)
    rel_p = rel_c + float(BLOCK)
    nv = jnp.zeros((rows, BLOCK), jnp.int32) + n
    ok_c = (i >= j) & (nv >= 1)
    ok_p = (j > i) & (nv >= 2)
    im = lax.broadcasted_iota(jnp.int32, (rows, N_META), 0) & (BLOCK - 1)
    jm = lax.broadcasted_iota(jnp.int32, (rows, N_META), 1)
    ok_m = ((jnp.zeros((rows, N_META), jnp.int32) + n) >= 1) | (im >= PAD_ROWS + jm)
    first = lax.broadcasted_iota(jnp.int32, (BLOCK, LANES), 1) < HEAD_DIM
    neg = -jnp.inf
    outs = []
    for kv in range(ATT_KV_HEADS):
        tile, half = divmod(kv, 2)
        lanes = slice(tile * LANES, (tile + 1) * LANES)
        kc2, kp2, km2 = (_both_halves(t[:, lanes], half).astype(BF16) for t in (kc, kp, km))
        vc2, vp2, vm2 = (_both_halves(t[:, lanes], half).astype(BF16) for t in (vc, vp, vm))
        qs, slope, sk = [], [], []
        for pair in range(ATT_GROUP // 2):
            c0 = (kv * ATT_GROUP + 2 * pair) * HEAD_DIM
            qp = q[:, c0:c0 + LANES] * HEAD_DIM ** -0.5
            qs += [jnp.where(first, qp, 0.0), jnp.where(first, 0.0, qp)]
        for g in range(ATT_GROUP):
            slope.append(jnp.full((BLOCK, 1), ALIBI_SLOPES[kv * ATT_GROUP + g], F32))
            sk.append(jnp.broadcast_to(sinks[kv * ATT_GROUP + g], (BLOCK, 1)))
        qs = jnp.concatenate(qs, axis=0).astype(BF16)
        slope = jnp.concatenate(slope, axis=0)
        sk = jnp.concatenate(sk, axis=0)
        sc = jnp.where(ok_c, lax.dot_general(qs, kc2, _NT, preferred_element_type=F32) - slope * rel_c, neg)
        sp = jnp.where(ok_p, lax.dot_general(qs, kp2, _NT, preferred_element_type=F32) - slope * rel_p, neg)
        sm = jnp.where(ok_m, lax.dot_general(qs, km2, _NT, preferred_element_type=F32), neg)
        mx = jnp.maximum(jnp.maximum(jnp.max(sc, axis=1, keepdims=True), jnp.max(sp, axis=1, keepdims=True)),
                         jnp.maximum(jnp.max(sm, axis=1, keepdims=True), sk))
        mx = lax.stop_gradient(mx)
        ec, ep, em, es = jnp.exp(sc - mx), jnp.exp(sp - mx), jnp.exp(sm - mx), jnp.exp(sk - mx)
        den = (es + jnp.sum(ec, axis=1, keepdims=True) + jnp.sum(ep, axis=1, keepdims=True)
               + jnp.sum(em, axis=1, keepdims=True))
        inv = 1.0 / den
        o = (jnp.dot((ec * inv).astype(BF16), vc2, preferred_element_type=F32)
             + jnp.dot((ep * inv).astype(BF16), vp2, preferred_element_type=F32)
             + jnp.dot((em * inv).astype(BF16), vm2, preferred_element_type=F32))
        for pair in range(ATT_GROUP // 2):
            r0 = 2 * pair * BLOCK
            outs.append(jnp.where(first, o[r0:r0 + BLOCK], o[r0 + BLOCK:r0 + 2 * BLOCK]))
    return jnp.concatenate(outs, axis=1) * _silu(z)


def _attn_specs(nb, steps_clamped):
    def blk(t):
        return jnp.minimum(t, nb - 1) if steps_clamped else t

    wide = lambda col: pl.BlockSpec((BLOCK, D_MODEL), lambda t: (blk(t), col))
    cur = lambda col: pl.BlockSpec((BLOCK, KV_WIDTH), lambda t: (blk(t), col))
    prev = lambda col: pl.BlockSpec((BLOCK, KV_WIDTH), lambda t: (jnp.maximum(blk(t) - 1, 0), col))
    meta = lambda col: pl.BlockSpec((N_META, KV_WIDTH), lambda t: (META_ROW_BLOCK, col))
    sinks = pl.BlockSpec((ATT_Q_HEADS, 1, 1), lambda t: (0, 0, 0))
    return [wide(Q_BLOCK), wide(Z_ATT_BLOCK), prev(K_BLOCK), cur(K_BLOCK), prev(V_BLOCK), cur(V_BLOCK),
            meta(K_BLOCK), meta(V_BLOCK), sinks]


def attn_fwd(proj, sinks):
    nb = proj.shape[0] // BLOCK

    def body(q_ref, z_ref, kp_ref, kc_ref, vp_ref, vc_ref, km_ref, vm_ref, sk_ref, o_ref):
        o_ref[...] = _attn_rows(q_ref[...], z_ref[...], kp_ref[...], kc_ref[...], vp_ref[...], vc_ref[...],
                                km_ref[...], vm_ref[...], tuple(sk_ref[h] for h in range(ATT_Q_HEADS)),
                                pl.program_id(0)).astype(BF16)

    return _call(body, name="attn_fwd", grid=(nb,), in_specs=_attn_specs(nb, False), out_specs=_row_spec(D_MODEL),
                 out_shape=jax.ShapeDtypeStruct((nb * BLOCK, D_MODEL), BF16), sem=("parallel",))(*([proj] * 8), sinks)


def attn_bwd(da, proj, sinks):
    nb = proj.shape[0] // BLOCK
    last = nb - 1
    wide = pl.BlockSpec((BLOCK, D_MODEL), lambda t: (jnp.minimum(t, last), 0))
    done = pl.BlockSpec((BLOCK, KV_WIDTH), lambda t: (jnp.maximum(t - 1, 0), 0))
    meta = _full((N_META, KV_WIDTH))
    par = _full((ATT_Q_HEADS, 1, 1))

    def body(da_ref, q_ref, z_ref, kp_ref, kc_ref, vp_ref, vc_ref, km_ref, vm_ref, sk_ref,
             dq_ref, dz_ref, dk_ref, dv_ref, dkm_ref, dvm_ref, dsk_ref, ck_ref, cv_ref):
        t = pl.program_id(0)

        @pl.when(t == 0)
        def _():
            ck_ref[...] = jnp.zeros_like(ck_ref)
            cv_ref[...] = jnp.zeros_like(cv_ref)
            dkm_ref[...] = jnp.zeros_like(dkm_ref)
            dvm_ref[...] = jnp.zeros_like(dvm_ref)
            dsk_ref[...] = jnp.zeros_like(dsk_ref)

        @pl.when(t < nb)
        def _():
            def f(q, z, kp, kc, vp, vc, km, vm, sk):
                return _attn_rows(q, z, kp, kc, vp, vc, km, vm, sk, t)

            _, vjp = jax.vjp(f, q_ref[...], z_ref[...], kp_ref[...], kc_ref[...], vp_ref[...], vc_ref[...],
                             km_ref[...], vm_ref[...], tuple(sk_ref[h] for h in range(ATT_Q_HEADS)))
            dq, dz, dkp, dkc, dvp, dvc, dkm, dvm, dsk = vjp(da_ref[...])
            dq_ref[...] = dq.astype(BF16)
            dz_ref[...] = dz.astype(BF16)
            for h in range(ATT_Q_HEADS):
                dsk_ref[h] += dsk[h]
            dk_ref[...] = ck_ref[...] + dkp
            dv_ref[...] = cv_ref[...] + dvp
            ck_ref[...] = dkc
            cv_ref[...] = dvc
            dkm_ref[...] += dkm
            dvm_ref[...] += dvm

        @pl.when(t == nb)
        def _():
            dk_ref[...] = ck_ref[...]
            dv_ref[...] = cv_ref[...]

    rows = nb * BLOCK
    return _call(body, name="attn_bwd", grid=(nb + 1,), in_specs=[wide] + _attn_specs(nb, True),
                 out_specs=[wide, wide, done, done, meta, meta, par],
                 out_shape=[jax.ShapeDtypeStruct((rows, D_MODEL), BF16), jax.ShapeDtypeStruct((rows, D_MODEL), BF16),
                            jax.ShapeDtypeStruct((rows, KV_WIDTH), F32), jax.ShapeDtypeStruct((rows, KV_WIDTH), F32),
                            jax.ShapeDtypeStruct((N_META, KV_WIDTH), F32), jax.ShapeDtypeStruct((N_META, KV_WIDTH), F32),
                            jax.ShapeDtypeStruct(sinks.shape, F32)],
                 scratch=[pltpu.VMEM((BLOCK, KV_WIDTH), F32), pltpu.VMEM((BLOCK, KV_WIDTH), F32)],
                 sem=("arbitrary",))(da, *([proj] * 8), sinks)


XBC_BLOCK0 = SEG["xbc"][2] // D_MODEL
CONV_COL_BLOCKS = CONV_DIM // D_MODEL
DT_TILE = SEG["dt"][2] // LANES


def _shift_rows(cur, prev, j, row):
    if j == 0:
        return cur
    return jnp.where(row >= j, pltpu.roll(cur, j, 0), pltpu.roll(prev, j, 0))


def _conv_pre(cur, prev, w_ref, b_ref, row):
    pre = b_ref[...] + w_ref[CONV_WIDTH - 1:CONV_WIDTH, :] * cur
    for k in range(CONV_WIDTH - 1):
        pre = pre + w_ref[k:k + 1, :] * _shift_rows(cur, prev, CONV_WIDTH - 1 - k, row)
    return pre


def conv_fwd(proj, conv_w, conv_b):
    nb = proj.shape[0] // BLOCK
    cur = pl.BlockSpec((BLOCK, D_MODEL), lambda j, i: (i, XBC_BLOCK0 + j))
    prev = pl.BlockSpec((BLOCK, D_MODEL), lambda j, i: (jnp.maximum(i - 1, 0), XBC_BLOCK0 + j))

    def body(c_ref, p_ref, w_ref, b_ref, o_ref):
        i = pl.program_id(1)
        row = lax.broadcasted_iota(jnp.int32, (BLOCK, D_MODEL), 0)
        prevv = p_ref[...] * jnp.where(i > 0, 1.0, 0.0)
        pre = _conv_pre(c_ref[...], prevv, w_ref, b_ref, row)
        valid = jnp.maximum((row >= PAD_ROWS).astype(F32), jnp.where(i > 0, 1.0, 0.0))
        o_ref[...] = _silu(pre) * valid

    return _call(body, name="conv_fwd", grid=(CONV_COL_BLOCKS, nb),
                 in_specs=[cur, prev, pl.BlockSpec((CONV_WIDTH, D_MODEL), lambda j, i: (0, j)),
                           pl.BlockSpec((1, D_MODEL), lambda j, i: (0, j))],
                 out_specs=pl.BlockSpec((BLOCK, D_MODEL), lambda j, i: (i, j)),
                 out_shape=jax.ShapeDtypeStruct((nb * BLOCK, CONV_DIM), F32),
                 sem=("parallel", "parallel"))(proj, proj, conv_w, conv_b)


def conv_bwd(dxbc, proj, conv_w, conv_b):
    nb = proj.shape[0] // BLOCK
    last = nb - 1
    cur = pl.BlockSpec((BLOCK, D_MODEL), lambda j, i: (i, XBC_BLOCK0 + j))
    prev = pl.BlockSpec((BLOCK, D_MODEL), lambda j, i: (jnp.maximum(i - 1, 0), XBC_BLOCK0 + j))
    nxt = pl.BlockSpec((BLOCK, D_MODEL), lambda j, i: (jnp.minimum(i + 1, last), XBC_BLOCK0 + j))
    dcur = pl.BlockSpec((BLOCK, D_MODEL), lambda j, i: (i, j))
    dnxt = pl.BlockSpec((BLOCK, D_MODEL), lambda j, i: (jnp.minimum(i + 1, last), j))
    wspec = pl.BlockSpec((CONV_WIDTH, D_MODEL), lambda j, i: (0, j))
    bspec = pl.BlockSpec((1, D_MODEL), lambda j, i: (0, j))

    def body(dc_ref, dn_ref, c_ref, p_ref, n_ref, w_ref, b_ref, du_ref, dw_ref, db_ref):
        i = pl.program_id(1)
        row = lax.broadcasted_iota(jnp.int32, (BLOCK, D_MODEL), 0)
        curv = c_ref[...]
        prevv = p_ref[...] * jnp.where(i > 0, 1.0, 0.0)

        def dpre_of(pre, d, valid):
            s = jax.nn.sigmoid(pre)
            return d * valid * (s * (1.0 + pre * (1.0 - s)))

        valid = jnp.maximum((row >= PAD_ROWS).astype(F32), jnp.where(i > 0, 1.0, 0.0))
        dp_c = dpre_of(_conv_pre(curv, prevv, w_ref, b_ref, row), dc_ref[...], valid)
        has_next = jnp.where(i < last, 1.0, 0.0)
        dp_n = dpre_of(_conv_pre(n_ref[...], curv, w_ref, b_ref, row), dn_ref[...], has_next)
        du = w_ref[CONV_WIDTH - 1:CONV_WIDTH, :] * dp_c
        for j in range(1, CONV_WIDTH):
            up = jnp.where(row < BLOCK - j, pltpu.roll(dp_c, BLOCK - j, 0), pltpu.roll(dp_n, BLOCK - j, 0))
            du = du + w_ref[CONV_WIDTH - 1 - j:CONV_WIDTH - j, :] * up
        du_ref[...] = du.astype(BF16)

        @pl.when(i == 0)
        def _():
            dw_ref[...] = jnp.zeros_like(dw_ref)
            db_ref[...] = jnp.zeros_like(db_ref)

        for k in range(CONV_WIDTH):
            dw_ref[k:k + 1, :] += jnp.sum(dp_c * _shift_rows(curv, prevv, CONV_WIDTH - 1 - k, row), axis=0,
                                          keepdims=True)
        db_ref[...] += jnp.sum(dp_c, axis=0, keepdims=True)

    return _call(body, name="conv_bwd", grid=(CONV_COL_BLOCKS, nb),
                 in_specs=[dcur, dnxt, cur, prev, nxt, wspec, bspec], out_specs=[dcur, wspec, bspec],
                 out_shape=[jax.ShapeDtypeStruct((nb * BLOCK, CONV_DIM), BF16),
                            jax.ShapeDtypeStruct((CONV_WIDTH, CONV_DIM), F32), jax.ShapeDtypeStruct((1, CONV_DIM), F32)],
                 sem=("parallel", "arbitrary"))(dxbc, dxbc, proj, proj, proj, conv_w, conv_b)


def _head_expand():
    e = np.zeros((LANES, SSM_INNER), np.float32)
    for h in range(SSM_HEADS):
        e[h, h * HEAD_DIM:(h + 1) * HEAD_DIM] = 1.0
    return jnp.asarray(e)


def _softplus(x):
    return jnp.maximum(x, 0.0) + jnp.log(1.0 + jnp.exp(-jnp.abs(x)))


def dt_fwd(proj, expand, bias_x):
    nb = proj.shape[0] // BLOCK

    def body(t_ref, e_ref, b_ref, o_ref):
        raw = jnp.dot(t_ref[...], e_ref[...], precision=HI, preferred_element_type=F32)
        o_ref[...] = _softplus(raw + b_ref[...])

    return _call(body, name="dt_fwd", grid=(nb,),
                 in_specs=[_row_spec(LANES, DT_TILE), _full((LANES, SSM_INNER)), _full((1, SSM_INNER))],
                 out_specs=_row_spec(SSM_INNER), out_shape=jax.ShapeDtypeStruct((nb * BLOCK, SSM_INNER), F32),
                 sem=("parallel",))(proj, expand, bias_x)


def dt_bwd(ddt_x, proj, expand, bias_x):
    nb = proj.shape[0] // BLOCK

    def body(d_ref, t_ref, e_ref, b_ref, o_ref, db_ref):
        raw = jnp.dot(t_ref[...], e_ref[...], precision=HI, preferred_element_type=F32)
        draw = d_ref[...] * jax.nn.sigmoid(raw + b_ref[...])
        dt = lax.dot_general(draw, e_ref[...], _NT, precision=HI, preferred_element_type=F32)
        o_ref[...] = dt.astype(BF16)

        @pl.when(pl.program_id(0) == 0)
        def _():
            db_ref[...] = jnp.zeros_like(db_ref)

        db_ref[...] += jnp.sum(dt, axis=0, keepdims=True)

    return _call(body, name="dt_bwd", grid=(nb,),
                 in_specs=[_row_spec(SSM_INNER), _row_spec(LANES, DT_TILE), _full((LANES, SSM_INNER)),
                           _full((1, SSM_INNER))],
                 out_specs=[_row_spec(LANES), _full((1, LANES))],
                 out_shape=[jax.ShapeDtypeStruct((nb * BLOCK, LANES), BF16), jax.ShapeDtypeStruct((1, LANES), F32)],
                 sem=("arbitrary",))(ddt_x, proj, expand, bias_x)


def _ssd_group(xs, dtx, bg, cg, alog, dsk, state):
    l = lax.broadcasted_iota(jnp.int32, (BLOCK, BLOCK), 0)
    s = lax.broadcasted_iota(jnp.int32, (BLOCK, BLOCK), 1)
    causal = l >= s
    first_head = s < HEAD_DIM
    a = dtx * (-jnp.exp(alog))
    cs = jnp.dot(causal.astype(F32), a, precision=HI, preferred_element_type=F32)
    tot = jnp.sum(a, axis=0, keepdims=True)
    bb, cb16 = bg.astype(BF16), cg.astype(BF16)
    cb = lax.dot_general(cb16, bb, _NT, preferred_element_type=F32)
    xr = xs * dtx
    y_diag = []
    for p in range(GROUP_W // LANES):
        lanes = slice(p * LANES, (p + 1) * LANES)
        c_pair = cs[:, lanes]
        c_swap = _swap_halves(c_pair)
        m = []
        for c_head in (jnp.where(first_head, c_pair, c_swap), jnp.where(first_head, c_swap, c_pair)):
            m.append(cb * jnp.exp(jnp.where(causal, c_head - c_head.T, -jnp.inf)))
        x_pair = xr[:, lanes]
        x_diag = jnp.concatenate([jnp.where(first_head, x_pair, 0.0), jnp.where(first_head, 0.0, x_pair)], axis=0)
        y_diag.append(jnp.dot(jnp.concatenate(m, axis=1).astype(BF16), x_diag.astype(BF16),
                              preferred_element_type=F32))
    st = lax.dot_general(bb, (xr * jnp.exp(tot - cs)).astype(BF16), (((0,), (0,)), ((), ())),
                         preferred_element_type=F32)
    new_state = state * jnp.exp(tot) + st
    y_off = jnp.dot(cb16, state.astype(BF16), preferred_element_type=F32) * jnp.exp(cs)
    return jnp.concatenate(y_diag, axis=1) + y_off + dsk * xs, new_state


B_TILE0 = SSM_INNER // LANES
C_TILE0 = B_TILE0 + SSM_GROUPS


def ssd_fwd(xbc, dt_x, alog_x, dsk_x):
    nb = xbc.shape[0] // BLOCK
    gspec = pl.BlockSpec((BLOCK, GROUP_W), lambda g, c: (c, g))
    pspec = pl.BlockSpec((1, GROUP_W), lambda g, c: (0, g))

    def body(x_ref, dt_ref, b_ref, c_ref, al_ref, dk_ref, y_ref, sp_ref, st_ref):
        @pl.when(pl.program_id(1) == 0)
        def _():
            st_ref[...] = jnp.zeros_like(st_ref)

        state = st_ref[...]
        sp_ref[...] = state
        y_ref[...], st_ref[...] = _ssd_group(x_ref[...], dt_ref[...], b_ref[...], c_ref[...], al_ref[...],
                                             dk_ref[...], state)

    return _call(body, name="ssd_fwd", grid=(SSM_GROUPS, nb),
                 in_specs=[gspec, gspec, pl.BlockSpec((BLOCK, SSM_STATE), lambda g, c: (c, B_TILE0 + g)),
                           pl.BlockSpec((BLOCK, SSM_STATE), lambda g, c: (c, C_TILE0 + g)), pspec, pspec],
                 out_specs=[gspec, pl.BlockSpec((None, SSM_STATE, GROUP_W), lambda g, c: (c, 0, g))],
                 out_shape=[jax.ShapeDtypeStruct((nb * BLOCK, SSM_INNER), F32),
                            jax.ShapeDtypeStruct((nb, SSM_STATE, SSM_INNER), F32)],
                 scratch=[pltpu.VMEM((SSM_STATE, GROUP_W), F32)],
                 sem=("parallel", "arbitrary"))(xbc, dt_x, xbc, xbc, alog_x, dsk_x)


def ssd_bwd(dy, xbc, dt_x, alog_x, dsk_x, states, comm):
    nb = xbc.shape[0] // BLOCK
    last = nb - 1
    gspec = pl.BlockSpec((BLOCK, GROUP_W), lambda g, c: (last - c, g))
    pspec = pl.BlockSpec((1, GROUP_W), lambda g, c: (0, g))
    nspec = pl.BlockSpec((BLOCK, SSM_STATE), lambda g, c: (last - c, g))

    def body(dy_ref, x_ref, dt_ref, b_ref, c_ref, al_ref, dk_ref, sp_ref,
             dx_ref, ddt_ref, db_ref, dc_ref, dal_ref, ddk_ref, ds_ref):
        @pl.when(pl.program_id(1) == 0)
        def _():
            ds_ref[...] = jnp.zeros_like(ds_ref)
            dal_ref[...] = jnp.zeros_like(dal_ref)
            ddk_ref[...] = jnp.zeros_like(ddk_ref)

        _, vjp = jax.vjp(_ssd_group, x_ref[...], dt_ref[...], b_ref[...], c_ref[...], al_ref[...], dk_ref[...],
                         sp_ref[...])
        dx_ref[...], ddt_ref[...], db_ref[...], dc_ref[...], dal, ddk, ds_ref[...] = vjp((dy_ref[...], ds_ref[...]))
        dal_ref[...] += dal
        ddk_ref[...] += ddk

    def at():
        g, c = pl.program_id(0), pl.program_id(1)
        return (g == 0) & (c == 0), (g == 0) & (c == last), (g == SSM_GROUPS - 1) & (c == last)

    return _call_with_comm(
        body, comm, at, (dy, xbc, dt_x, xbc, xbc, alog_x, dsk_x, states), name="ssd_bwd", grid=(SSM_GROUPS, nb),
        in_specs=[gspec, gspec, gspec, pl.BlockSpec((BLOCK, SSM_STATE), lambda g, c: (last - c, B_TILE0 + g)),
                  pl.BlockSpec((BLOCK, SSM_STATE), lambda g, c: (last - c, C_TILE0 + g)), pspec, pspec,
                  pl.BlockSpec((None, SSM_STATE, GROUP_W), lambda g, c: (last - c, 0, g))],
        out_specs=[gspec, gspec, nspec, nspec, pspec, pspec],
        out_shape=[jax.ShapeDtypeStruct((nb * BLOCK, SSM_INNER), F32),
                   jax.ShapeDtypeStruct((nb * BLOCK, SSM_INNER), F32),
                   jax.ShapeDtypeStruct((nb * BLOCK, SSM_GROUPS * SSM_STATE), F32),
                   jax.ShapeDtypeStruct((nb * BLOCK, SSM_GROUPS * SSM_STATE), F32),
                   jax.ShapeDtypeStruct((1, SSM_INNER), F32), jax.ShapeDtypeStruct((1, SSM_INNER), F32)],
        scratch=[pltpu.VMEM((SSM_STATE, GROUP_W), F32)])


SLAB_ROWS = 24
SLAB_META_ROW = 8


def pack_small(dcw, dcb, dgpre, dgpost, ddtb, dalog_x, ddsk_x, dsinks, dgn, dmeta, expand):
    def body(cw, cb, gpre, gpost, dtb, al, dk, sk, gn, meta, e_ref, o_ref):
        def per_head(v):
            rows = jnp.broadcast_to(v[...], (8, SSM_INNER))
            return lax.dot_general(rows, e_ref[...], _NT, precision=HI, preferred_element_type=F32)[0:1]

        o_ref[...] = jnp.zeros_like(o_ref)
        o_ref[0:CONV_WIDTH, :] = cw[...]
        o_ref[4:5, :] = cb[...]
        o_ref[5:6, 0:1024] = gpre[...]
        o_ref[5:6, 1024:2048] = gpost[...]
        o_ref[5:6, 2048:2176] = dtb[...]
        o_ref[5:6, 2176:2304] = per_head(al)
        o_ref[5:6, 2304:2432] = per_head(dk)
        o_ref[5:6, 2432:2560] = sk[...]
        o_ref[6:7, 0:SSM_INNER] = gn[...]
        o_ref[SLAB_META_ROW:SLAB_META_ROW + N_META, 0:D_MODEL] = meta[...]

    args = (dcw, dcb, dgpre, dgpost, ddtb, dalog_x, ddsk_x, dsinks, dgn, dmeta, expand)
    return _call(body, name="pack_small", in_specs=[_full(a.shape) for a in args],
                 out_specs=_full((SLAB_ROWS, CONV_DIM)), out_shape=jax.ShapeDtypeStruct((SLAB_ROWS, CONV_DIM), F32))(*args)


def _lane_tile(v):
    return jnp.pad(v, ((0, 0), (0, LANES - v.shape[1])))


def kernel(x, meta_tokens, g_pre, w_in, conv_w, conv_b, dt_bias, a_log, d_skip, attn_sinks, g_ssm_norm, w_out_att, w_out_ssm, w_out, g_post, loss_target, m_meta_tokens, m_g_pre, m_w_in, m_conv_w, m_conv_b, m_dt_bias, m_a_log, m_d_skip, m_attn_sinks, m_g_ssm_norm, m_w_out_att, m_w_out_ssm, m_w_out, m_g_post, v_meta_tokens, v_g_pre, v_w_in, v_conv_w, v_conv_b, v_dt_bias, v_a_log, v_d_skip, v_attn_sinks, v_g_ssm_norm, v_w_out_att, v_w_out_ssm, v_w_out, v_g_post):
    chip = _chip_index()

    conv_w_rows = jnp.pad(conv_w[0], ((0, 2 * 8 - CONV_WIDTH), (0, 0)))
    g_w_in, g_conv_w, g_meta = run_comm("gather_w_in", TwoLevelGather([pack_w_in(w_in[0]), conv_w_rows, meta_tokens]))
    w_all = unpack_w_in(g_w_in)
    cw_full = g_conv_w[:, :CONV_WIDTH].transpose(1, 0, 2).reshape(CONV_WIDTH, CONV_DIM)
    meta_full = g_meta.transpose(1, 0, 2).reshape(N_META, D_MODEL)

    h, u = prep(x, meta_full, g_pre)
    proj, g_w_out = mm_nn("in_proj", u, w_all, comm=TwoLevelGather(
        [w_out_att[0].astype(BF16), w_out_ssm[0].astype(BF16), w_out[0].astype(BF16)]))
    woa = g_w_out[0].reshape(D_MODEL, D_MODEL)
    wos = g_w_out[1].reshape(SSM_INNER, D_MODEL)
    wo = g_w_out[2].reshape(D_MODEL, D_MODEL)

    sinks3 = attn_sinks.reshape(ATT_Q_HEADS, 1, 1)
    a_att = attn_fwd(proj, sinks3)
    y_att = mm_nn("att_out", a_att, woa)

    xbc = conv_fwd(proj, cw_full, conv_b)
    expand = _head_expand()
    bias_x = jnp.repeat(dt_bias, HEAD_DIM, axis=1)
    dt_x = dt_fwd(proj, expand, bias_x)
    alog_x = jnp.repeat(a_log, HEAD_DIM, axis=1)
    dsk_x = jnp.repeat(d_skip, HEAD_DIM, axis=1)
    y_ssd, states = ssd_fwd(xbc, dt_x, alog_x, dsk_x)
    yn = ssm_norm_fwd(y_ssd, proj, g_ssm_norm)
    y_ssm = mm_nn("ssm_out", yn, wos)

    merged = merge_fwd(proj, y_att, y_ssm)
    out = mm_nn("out_proj", merged, wo)
    dout, dres, loss_tile, dg_post = final_loss(out, x, loss_target, g_post)
    loss = lax.psum(loss_tile[0, 0], ("x", "y", "c"))

    dmerged = mm_nt("out_proj_dx", dout, wo)
    dwo = mm_tn("out_proj_dw", merged, dout)
    dga, dgs, dy_att, dy_ssm = merge_bwd(dmerged, proj, y_att, y_ssm)

    da_att = mm_nt("att_out_dx", dy_att, woa)
    dwoa = mm_tn("att_out_dw", a_att, dy_att)
    dq, dz_att, dk, dv, dkmeta, dvmeta, dsinks3 = attn_bwd(da_att, proj, sinks3)
    dk = dk.at[PAD_ROWS:BLOCK].add(dkmeta).astype(BF16)
    dv = dv.at[PAD_ROWS:BLOCK].add(dvmeta).astype(BF16)

    dyn = mm_nt("ssm_out_dx", dy_ssm, wos)
    dwos = mm_tn("ssm_out_dw", yn, dy_ssm)
    dy_ssd, dz_ssm, dgn = ssm_norm_bwd(dyn, y_ssd, proj, g_ssm_norm)

    def pieces(g):
        return g.astype(BF16).reshape(4, 2, g.shape[0] // 8, g.shape[1])

    def to_owner(g):
        return (lambda ref, dev: ref.at[_chip_of(dev), dev[2]], (g.shape[0] // 8, g.shape[1]))

    (dxs, ddt_x, dbg, dcg, dalog_x, ddsk_x), sent_w_out = ssd_bwd(
        dy_ssd, xbc, dt_x, alog_x, dsk_x, states,
        DirectExchange([pieces(dwoa), pieces(dwos), pieces(dwo)], [to_owner(dwoa), to_owner(dwos), to_owner(dwo)],
                       ALL_MASKS, "dev", 8))
    dxbc = jnp.concatenate([dxs, dbg, dcg], axis=1)
    dxbc_raw, dcw, dcb = conv_bwd(dxbc, proj, cw_full, conv_b)
    ddt_tile, ddtb_tile = dt_bwd(ddt_x, proj, expand, bias_x)

    dproj = jnp.concatenate([dz_ssm, dxbc_raw, dq, dz_att, dga, dgs, dk, dv, ddt_tile], axis=1)
    dw_all = mm_tn("in_proj_dw", u, dproj)

    half_rows = D_MODEL // 2
    partial = pack_grad_w_in(dw_all).reshape(4, 2, half_rows, PACK_W)
    from_sibling, = run_comm("pair_grads", DirectExchange(
        [partial], [(lambda ref, dev: ref.at[pl.ds(0, 4), dev[2]], (4, half_rows, PACK_W))], SIBLING_MASK, "core", 2,
        keep_own=False))
    chip_sum = sum_pair(partial, from_sibling)
    du, (sent_w_in,) = mm_nt("in_proj_dx", dproj, w_all, comm=DirectExchange(
        [chip_sum], [(lambda ref, dev: ref.at[_chip_of(dev)], (half_rows, PACK_W))], CHIP_MASKS, "chip", 4))
    grad_x, dmeta, dg_pre = prep_bwd(h, du, dres, g_pre)

    slab = pack_small(dcw, dcb, dg_pre, dg_post, ddtb_tile, dalog_x, ddsk_x,
                      _lane_tile(dsinks3.reshape(1, ATT_Q_HEADS)), dgn, dmeta, expand)
    halves = [sum_slots("sum_" + nm, r)
              for nm, r in zip(("w_in", "w_out_att", "w_out_ssm", "w_out"), [sent_w_in] + list(sent_w_out))]
    shared = run_comm("share_grads", Both(DirectExchange(halves, [None] * 4, SIBLING_MASK, "core", 2),
                                          DirectExchange([slab], [None], ALL_MASKS, "dev", 8)))
    g_w_in_packed, g_woa, g_wos, g_wo = [f.reshape(2 * f.shape[1], f.shape[2]) for f in shared[:4]]
    small = sum_slots("sum_small", shared[4])

    g_w_in, d_w_in, nm_w_in, nv_w_in = adamw_w_in(g_w_in_packed, w_in[0], m_w_in[0], v_w_in[0])
    d_woa, nm_woa, nv_woa = adamw_rows("adamw_w_out_att", g_woa, w_out_att[0], m_w_out_att[0], v_w_out_att[0])
    d_wos, nm_wos, nv_wos = adamw_rows("adamw_w_out_ssm", g_wos, w_out_ssm[0], m_w_out_ssm[0], v_w_out_ssm[0])
    d_wo, nm_wo, nv_wo = adamw_rows("adamw_w_out", g_wo, w_out[0], m_w_out[0], v_w_out[0])

    cw_cols = CONV_DIM // 4
    meta_cols = D_MODEL // 4
    g_small = {
        "meta_tokens": lax.dynamic_slice(small, (SLAB_META_ROW, chip * meta_cols), (N_META, meta_cols)),
        "g_pre": small[5:6, 0:1024],
        "conv_w": lax.dynamic_slice(small, (0, chip * cw_cols), (CONV_WIDTH, cw_cols)),
        "conv_b": small[4:5, :],
        "dt_bias": small[5:6, 2048:2048 + SSM_HEADS],
        "a_log": small[5:6, 2176:2176 + SSM_HEADS],
        "d_skip": small[5:6, 2304:2304 + SSM_HEADS],
        "attn_sinks": small[5:6, 2432:2432 + ATT_Q_HEADS],
        "g_ssm_norm": small[6:7, 0:SSM_INNER],
        "g_post": small[5:6, 1024:2048],
    }
    names = list(g_small)
    w_small = dict(meta_tokens=meta_tokens, g_pre=g_pre, conv_w=conv_w[0], conv_b=conv_b, dt_bias=dt_bias, a_log=a_log,
                   d_skip=d_skip, attn_sinks=attn_sinks, g_ssm_norm=g_ssm_norm, g_post=g_post)
    m_small = dict(meta_tokens=m_meta_tokens, g_pre=m_g_pre, conv_w=m_conv_w[0], conv_b=m_conv_b, dt_bias=m_dt_bias,
                   a_log=m_a_log, d_skip=m_d_skip, attn_sinks=m_attn_sinks, g_ssm_norm=m_g_ssm_norm, g_post=m_g_post)
    v_small = dict(meta_tokens=v_meta_tokens, g_pre=v_g_pre, conv_w=v_conv_w[0], conv_b=v_conv_b, dt_bias=v_dt_bias,
                   a_log=v_a_log, d_skip=v_d_skip, attn_sinks=v_attn_sinks, g_ssm_norm=v_g_ssm_norm, g_post=v_g_post)
    upd = dict(zip(names, adamw_small([g_small[k] for k in names], [w_small[k] for k in names],
                                      [m_small[k] for k in names], [v_small[k] for k in names])))

    lead = {"conv_w"}

    def shaped(name, a):
        return a[None] if name in lead else a

    grads = dict(g_small, w_in=g_w_in, w_out_att=g_woa, w_out_ssm=g_wos, w_out=g_wo)
    deltas = dict({k: upd[k][0] for k in names}, w_in=d_w_in, w_out_att=d_woa, w_out_ssm=d_wos, w_out=d_wo)
    new_m = dict({k: upd[k][1] for k in names}, w_in=nm_w_in, w_out_att=nm_woa, w_out_ssm=nm_wos, w_out=nm_wo)
    new_v = dict({k: upd[k][2] for k in names}, w_in=nv_w_in, w_out_att=nv_woa, w_out_ssm=nv_wos, w_out=nv_wo)
    lead |= {"w_in", "w_out_att", "w_out_ssm", "w_out"}
    order = ["meta_tokens", "g_pre", "w_in", "conv_w", "conv_b", "dt_bias", "a_log", "d_skip", "attn_sinks",
             "g_ssm_norm", "w_out_att", "w_out_ssm", "w_out", "g_post"]
    outs = [loss, grad_x]
    for group in (grads, deltas, new_m, new_v):
        outs += [shaped(k, group[k]) for k in order]
    return tuple(outs)
```

```python
import functools

import numpy as np
import jax
import jax.numpy as jnp
from jax import lax
from jax.experimental import pallas as pl
from jax.experimental.pallas import tpu as pltpu

F32 = jnp.float32
BF16 = jnp.bfloat16
HI = lax.Precision.HIGHEST

D_MODEL = 1024
N_META = 16
BLOCK = 128
PAD_ROWS = BLOCK - N_META
NORM_EPS = 1e-6
HEAD_DIM = 64
ATT_Q_HEADS = 16
ATT_KV_HEADS = 4
ATT_GROUP = 4
SSM_INNER = 2048
SSM_HEADS = 32
SSM_GROUPS = 4
SSM_HEADS_PER_GROUP = 8
SSM_STATE = 128
CONV_WIDTH = 4
CONV_DIM = 3072
LANES = 128

ADAM_LR = 0.001
ADAM_B1 = 0.9
ADAM_B2 = 0.999
ADAM_EPS = 1e-08
ADAM_WD = 0.01
ADAM_STEP = 10

VMEM_LIMIT = 48 * 1024 * 1024

SHARD_W = 2440
PACK_W = 2560
SHARD_STRIDE = 2432
N_ALIGNED = 9856
SEG = {
    "q": (0, 1024, 5120), "k": (1024, 256, 9216), "v": (1280, 256, 9472), "z_att": (1536, 1024, 6144),
    "z_ssm": (2560, 2048, 0), "xbc": (4608, 3072, 2048), "dt": (7680, 128, 9728),
    "gate_att": (7808, 1024, 7168), "gate_ssm": (8832, 1024, 8192),
}
DT_STORED_START = 7680
DT_PAD = LANES - SSM_HEADS


def _act_col(aligned_col):
    for a0, w, p0 in SEG.values():
        if a0 <= aligned_col < a0 + w:
            return p0 + aligned_col - a0
    raise ValueError(aligned_col)


def _call(body, *, name, out_shape, in_specs, out_specs, grid=(), scratch=(), sem=None, aliases=None):
    return pl.pallas_call(
        body, out_shape=out_shape, grid=grid, in_specs=in_specs, out_specs=out_specs, scratch_shapes=list(scratch),
        name=name, input_output_aliases=aliases or {},
        compiler_params=pltpu.CompilerParams(dimension_semantics=sem, vmem_limit_bytes=VMEM_LIMIT))


def _full(shape):
    n = len(shape)
    return pl.BlockSpec(shape, lambda *_: (0,) * n)


def _chip_index():
    return lax.axis_index("x") * 2 + lax.axis_index("y")


def _silu(z):
    return z * jax.nn.sigmoid(z)


def _rms(x, g):
    return x * lax.rsqrt(jnp.mean(x * x, axis=-1, keepdims=True) + NORM_EPS) * g


def _peer(mask):
    x, y, c = lax.axis_index("x"), lax.axis_index("y"), lax.axis_index("c")
    return ((1 - x) if mask & 4 else x, (1 - y) if mask & 2 else y, (1 - c) if mask & 1 else c)


def _me():
    return lax.axis_index("x"), lax.axis_index("y"), lax.axis_index("c")


def _chip_of(dev):
    return 2 * dev[0] + dev[1]


CHIP_MASKS = (4, 2, 6)
ALL_MASKS = (1, 2, 3, 4, 5, 6, 7)
SIBLING_MASK = (1,)


def _remote(src, dst, send_sem, recv_sem, dev):
    return pltpu.make_async_remote_copy(src_ref=src, dst_ref=dst, send_sem=send_sem, recv_sem=recv_sem,
                                        device_id=dev, device_id_type=pl.DeviceIdType.MESH)


class _StagedCopy:
    def __init__(self, src, stage, dst, load_sem, store_sem):
        self.load = pltpu.make_async_copy(src, stage, load_sem)
        self.store = pltpu.make_async_copy(stage, dst, store_sem)

    def start(self):
        self.load.start()
        self.load.wait()
        self.store.start()

    def wait(self):
        self.store.wait()


class DirectExchange:
    def __init__(self, arrays, pieces, masks, slot_kind, nslots, keep_own=True):
        self.arrays, self.pieces, self.masks, self.slot_kind = list(arrays), list(pieces), masks, slot_kind
        self.keep_own = keep_own
        n, nk = len(arrays), len(masks)
        shapes = [a.shape if p is None else p[1] for a, p in zip(arrays, pieces)]
        self.out_shape = [jax.ShapeDtypeStruct((nslots,) + tuple(s), a.dtype) for s, a in zip(shapes, arrays)]
        self.scratch = [pltpu.SemaphoreType.DMA((n * nk,)), pltpu.SemaphoreType.DMA((n * nk,))]
        if keep_own:
            self.scratch += [pltpu.SemaphoreType.DMA((2 * n,))] + [pltpu.VMEM(s, a.dtype) for s, a in zip(shapes, arrays)]
        self.has_mid = False

    def _copies(self, ins, outs, scratch):
        send_sems, recv_sems = scratch[:2]
        me = _me()
        slot = {"chip": _chip_of(me), "dev": 4 * me[0] + 2 * me[1] + me[2], "core": me[2]}[self.slot_kind]
        nk = len(self.masks)

        def piece(a, dev):
            return ins[a] if self.pieces[a] is None else self.pieces[a][0](ins[a], dev)

        local = []
        if self.keep_own:
            local_sems, stages = scratch[2], scratch[3:]
            local = [_StagedCopy(piece(a, me), stages[a], outs[a].at[slot], local_sems.at[2 * a], local_sems.at[2 * a + 1])
                     for a in range(len(ins))]
        remote = []
        for a in range(len(ins)):
            for ki, mask in enumerate(self.masks):
                dev = _peer(mask)
                remote.append(_remote(piece(a, dev), outs[a].at[slot], send_sems.at[a * nk + ki],
                                      recv_sems.at[a * nk + ki], dev))
        return local, remote

    def start(self, ins, outs, scratch):
        local, remote = self._copies(ins, outs, scratch)
        for cp in remote + local:
            cp.start()

    def finish(self, ins, outs, scratch):
        local, remote = self._copies(ins, outs, scratch)
        for cp in remote + local:
            cp.wait()


class TwoLevelGather:
    def __init__(self, arrays):
        self.arrays = list(arrays)
        n, nk = len(arrays), len(CHIP_MASKS)
        self.out_shape = [jax.ShapeDtypeStruct((4,) + a.shape, a.dtype) for a in arrays]
        self.scratch = ([pltpu.SemaphoreType.DMA((n * nk,)) for _ in range(4)] + [pltpu.SemaphoreType.DMA((2 * n,))]
                        + [pltpu.VMEM(a.shape, a.dtype) for a in arrays])
        self.has_mid = True

    def _copies(self, ins, outs, scratch):
        ici_send, ici_recv, fwd_send, fwd_recv, local_sems = scratch[:5]
        stages = scratch[5:]
        me = _me()
        sibling = _peer(1)
        nk = len(CHIP_MASKS)
        local, ici, fwd = [], [], []
        for a in range(len(ins)):
            half = ins[a].shape[0] // 2
            mine = pl.ds(me[2] * half, half)
            local.append(_StagedCopy(ins[a], stages[a], outs[a].at[_chip_of(me)], local_sems.at[2 * a],
                                     local_sems.at[2 * a + 1]))
            for ki, mask in enumerate(CHIP_MASKS):
                dev = _peer(mask)
                k = a * nk + ki
                ici.append(_remote(ins[a].at[mine], outs[a].at[_chip_of(me), mine], ici_send.at[k], ici_recv.at[k], dev))
                arrived = outs[a].at[_chip_of(dev), mine]
                fwd.append(_remote(arrived, arrived, fwd_send.at[k], fwd_recv.at[k], sibling))
        return local, ici, fwd

    def start(self, ins, outs, scratch):
        local, ici, _ = self._copies(ins, outs, scratch)
        for cp in ici + local:
            cp.start()

    def mid(self, ins, outs, scratch):
        _, ici, fwd = self._copies(ins, outs, scratch)
        for arrival, onward in zip(ici, fwd):
            arrival.wait_recv()
            onward.start()

    def finish(self, ins, outs, scratch):
        local, ici, fwd = self._copies(ins, outs, scratch)
        for cp in fwd:
            cp.wait_recv()
        for cp in ici + fwd:
            cp.wait_send()
        for cp in local:
            cp.wait()


class Both:
    def __init__(self, a, b):
        self.a, self.b = a, b
        self.arrays, self.out_shape = a.arrays + b.arrays, a.out_shape + b.out_shape
        self.scratch = a.scratch + b.scratch
        self.has_mid = False
        assert not (a.has_mid or b.has_mid)

    def _parts(self, ins, outs, sems):
        na, sa = len(self.a.arrays), len(self.a.scratch)
        return (ins[:na], outs[:na], sems[:sa]), (ins[na:], outs[na:], sems[sa:])

    def start(self, ins, outs, sems):
        pa, pb = self._parts(ins, outs, sems)
        self.a.start(*pa)
        self.b.start(*pb)

    def finish(self, ins, outs, sems):
        pa, pb = self._parts(ins, outs, sems)
        self.a.finish(*pa)
        self.b.finish(*pb)


_ANY = pl.BlockSpec(memory_space=pl.ANY)


def run_comm(name, comm):
    n = len(comm.arrays)

    def body(*refs):
        ins, outs, sems = refs[:n], refs[n:2 * n], refs[2 * n:]
        comm.start(ins, outs, sems)
        if comm.has_mid:
            comm.mid(ins, outs, sems)
        comm.finish(ins, outs, sems)

    return pl.pallas_call(body, name=name, out_shape=comm.out_shape, in_specs=[_ANY] * n, out_specs=[_ANY] * n,
                          scratch_shapes=comm.scratch,
                          compiler_params=pltpu.CompilerParams(vmem_limit_bytes=VMEM_LIMIT))(*comm.arrays)


def _call_with_comm(body, comm, steps, args, *, name, out_shape, in_specs, out_specs, grid, scratch=()):
    ni, no, ns, nc = len(in_specs), len(out_specs), len(scratch), len(comm.arrays)

    def full_body(*refs):
        ins, cins = refs[:ni], refs[ni:ni + nc]
        outs, couts = refs[ni + nc:ni + nc + no], refs[ni + nc + no:ni + 2 * nc + no]
        scr, csems = refs[ni + 2 * nc + no:ni + 2 * nc + no + ns], refs[ni + 2 * nc + no + ns:]
        first, middle, last = steps()
        pl.when(first)(lambda: comm.start(cins, couts, csems))
        if comm.has_mid:
            pl.when(middle)(lambda: comm.mid(cins, couts, csems))
        body(*ins, *outs, *scr)
        pl.when(last)(lambda: comm.finish(cins, couts, csems))

    res = pl.pallas_call(
        full_body, name=name, out_shape=list(out_shape) + comm.out_shape, grid=grid,
        in_specs=list(in_specs) + [_ANY] * nc, out_specs=list(out_specs) + [_ANY] * nc,
        scratch_shapes=list(scratch) + comm.scratch,
        compiler_params=pltpu.CompilerParams(dimension_semantics=("arbitrary",) * len(grid),
                                             vmem_limit_bytes=VMEM_LIMIT))(*args, *comm.arrays)
    return res[:no], res[no:]


def _shift_right(p, chip, col):
    if chip == 0:
        return p
    if chip < 3:
        return pltpu.roll(p, 8 * chip, 1)
    tail0 = DT_STORED_START + SSM_HEADS - 3 * SHARD_W
    head = pltpu.roll(p, 24, 1)
    tail = pltpu.roll(p, 24 + DT_PAD, 1)
    return jnp.where(col < tail0 + 24, head, jnp.where(col >= tail0 + 24 + DT_PAD, tail, 0.0))


def _shift_left(g, chip, col):
    if chip == 0:
        return g
    if chip < 3:
        return pltpu.roll(g, PACK_W - 8 * chip, 1)
    tail0 = DT_STORED_START + SSM_HEADS - 3 * SHARD_W
    return jnp.where(col < tail0, pltpu.roll(g, PACK_W - 24, 1), pltpu.roll(g, PACK_W - 24 - DT_PAD, 1))


def pack_w_in(w):
    rows = BLOCK

    def body(w_ref, o_ref, pad_ref):
        chip = _chip_index()
        pad_ref[...] = jnp.zeros_like(pad_ref)
        pad_ref[:, 0:SHARD_W] = w_ref[...]
        p = pad_ref[...]
        col = lax.broadcasted_iota(jnp.int32, p.shape, 1)
        for cv in range(4):
            @pl.when(chip == cv)
            def _():
                o_ref[...] = _shift_right(p, cv, col).astype(BF16)

    return _call(body, name="pack_w_in", grid=(D_MODEL // rows,),
                 in_specs=[pl.BlockSpec((rows, SHARD_W), lambda i: (i, 0))],
                 out_specs=pl.BlockSpec((rows, PACK_W), lambda i: (i, 0)),
                 out_shape=jax.ShapeDtypeStruct((D_MODEL, PACK_W), BF16),
                 scratch=[pltpu.VMEM((rows, PACK_W), F32)], sem=("parallel",))(w)


def _tile_runs():
    runs, fix = [], []
    for t in range(N_ALIGNED // LANES):
        s = min(t // 19, 3)
        j = t - 19 * s
        p = _act_col(t * LANES)
        if runs and runs[-1][1] == s and runs[-1][0] + runs[-1][3] == p and runs[-1][2] + runs[-1][3] == j * LANES:
            runs[-1][3] += LANES
        else:
            runs.append([p, s, j * LANES, LANES])
        if j == 0 and s > 0:
            fix.append((p, s - 1))
    return runs, fix


def unpack_w_in(bg):
    rows = BLOCK
    runs, fix = _tile_runs()

    def body(b_ref, o_ref):
        for p, s, j, w in runs:
            o_ref[:, p:p + w] = b_ref[s, :, j:j + w]
        for p, s in fix:
            o_ref[:, p:p + LANES] = o_ref[:, p:p + LANES] + b_ref[s, :, SHARD_STRIDE:PACK_W]

    return _call(body, name="unpack_w_in", grid=(D_MODEL // rows,),
                 in_specs=[pl.BlockSpec((4, rows, PACK_W), lambda i: (0, i, 0))],
                 out_specs=pl.BlockSpec((rows, N_ALIGNED), lambda i: (i, 0)),
                 out_shape=jax.ShapeDtypeStruct((D_MODEL, N_ALIGNED), BF16), sem=("parallel",))(bg)


def pack_grad_w_in(dw):
    rows = BLOCK

    def body(g_ref, o_ref):
        for s in range(4):
            for j in range(PACK_W // LANES):
                p = _act_col((19 * s + j) * LANES)
                o_ref[s, :, j * LANES:(j + 1) * LANES] = g_ref[:, p:p + LANES].astype(BF16)

    return _call(body, name="pack_grad_w_in", grid=(D_MODEL // rows,),
                 in_specs=[pl.BlockSpec((rows, N_ALIGNED), lambda i: (i, 0))],
                 out_specs=pl.BlockSpec((4, rows, PACK_W), lambda i: (0, i, 0)),
                 out_shape=jax.ShapeDtypeStruct((4, D_MODEL, PACK_W), BF16), sem=("parallel",))(dw)


def _adamw(w, g, m, v):
    m = ADAM_B1 * m + (1.0 - ADAM_B1) * g
    v = ADAM_B2 * v + (1.0 - ADAM_B2) * jnp.square(g)
    m_hat = m / (1.0 - ADAM_B1 ** ADAM_STEP)
    v_hat = v / (1.0 - ADAM_B2 ** ADAM_STEP)
    delta = -ADAM_LR * (m_hat / (jnp.sqrt(v_hat) + ADAM_EPS) + ADAM_WD * w)
    return delta, m, v


def adamw_w_in(g_packed, w, m, v):
    rows = BLOCK

    def body(g_ref, w_ref, m_ref, v_ref, go_ref, d_ref, mo_ref, vo_ref, tmp_ref):
        chip = _chip_index()
        gp = g_ref[...]
        col = lax.broadcasted_iota(jnp.int32, gp.shape, 1)
        for cv in range(4):
            @pl.when(chip == cv)
            def _():
                tmp_ref[...] = _shift_left(gp, cv, col)
        g = tmp_ref[:, 0:SHARD_W]
        d, mn, vn = _adamw(w_ref[...], g, m_ref[...], v_ref[...])
        go_ref[...] = g
        d_ref[...] = d
        mo_ref[...] = mn
        vo_ref[...] = vn

    spec = pl.BlockSpec((rows, SHARD_W), lambda i: (i, 0))
    shp = jax.ShapeDtypeStruct((D_MODEL, SHARD_W), F32)
    return _call(body, name="adamw_w_in", grid=(D_MODEL // rows,),
                 in_specs=[pl.BlockSpec((rows, PACK_W), lambda i: (i, 0)), spec, spec, spec],
                 out_specs=[spec] * 4, out_shape=[shp] * 4,
                 scratch=[pltpu.VMEM((rows, PACK_W), F32)], sem=("parallel",))(g_packed, w, m, v)


def adamw_rows(name, g, w, m, v):
    r, c = g.shape
    rows = min(r, BLOCK)

    def body(g_ref, w_ref, m_ref, v_ref, d_ref, mo_ref, vo_ref):
        d_ref[...], mo_ref[...], vo_ref[...] = _adamw(w_ref[...], g_ref[...], m_ref[...], v_ref[...])

    spec = pl.BlockSpec((rows, c), lambda i: (i, 0))
    shp = jax.ShapeDtypeStruct((r, c), F32)
    return _call(body, name=name, grid=(r // rows,), in_specs=[spec] * 4, out_specs=[spec] * 3, out_shape=[shp] * 3,
                 sem=("parallel",))(g, w, m, v)


def adamw_small(gs, ws, ms, vs):
    n = len(gs)

    def body(*refs):
        g, w, m, v = refs[:n], refs[n:2 * n], refs[2 * n:3 * n], refs[3 * n:4 * n]
        outs = refs[4 * n:]
        for i in range(n):
            d, mn, vn = _adamw(w[i][...], g[i][...], m[i][...], v[i][...])
            outs[3 * i][...] = d
            outs[3 * i + 1][...] = mn
            outs[3 * i + 2][...] = vn

    specs = [_full(a.shape) for a in gs]
    res = _call(body, name="adamw_small", in_specs=specs * 4,
                out_specs=[s for s in specs for _ in range(3)],
                out_shape=[jax.ShapeDtypeStruct(a.shape, F32) for a in gs for _ in range(3)])(*gs, *ws, *ms, *vs)
    return [tuple(res[3 * i:3 * i + 3]) for i in range(n)]


def sum_slots(name, r):
    s, rr, c = r.shape
    rows = min(rr, BLOCK)

    def body(r_ref, o_ref):
        acc = r_ref[0].astype(F32)
        for k in range(1, s):
            acc = acc + r_ref[k].astype(F32)
        o_ref[...] = acc

    return _call(body, name=name, grid=(rr // rows,), in_specs=[pl.BlockSpec((s, rows, c), lambda i: (0, i, 0))],
                 out_specs=pl.BlockSpec((rows, c), lambda i: (i, 0)), out_shape=jax.ShapeDtypeStruct((rr, c), F32),
                 sem=("parallel",))(r)


def sum_pair(partial, from_sibling):
    s, _, rr, cols = partial.shape

    def body(p_ref, r_ref, o_ref):
        c = lax.axis_index("c")
        o_ref[...] = (p_ref[c].astype(F32) + r_ref[1 - c].astype(F32)).astype(BF16)

    return _call(body, name="sum_pair", grid=(s, rr // BLOCK),
                 in_specs=[pl.BlockSpec((None, 2, BLOCK, cols), lambda k, i: (k, 0, i, 0)),
                           pl.BlockSpec((2, None, BLOCK, cols), lambda k, i: (0, k, i, 0))],
                 out_specs=pl.BlockSpec((None, BLOCK, cols), lambda k, i: (k, i, 0)),
                 out_shape=jax.ShapeDtypeStruct((s, rr, cols), BF16), sem=("parallel", "parallel"))(partial, from_sibling)


def _col_tile(n, k):
    if n % 896 == 0 and k <= 1024:
        return 896
    return min(n, 512)


def mm_nn(name, x, w, out_dtype=F32, comm=None):
    m, k = x.shape
    n = w.shape[1]
    tn = _col_tile(n, k)
    steps = n // tn

    def body(x_ref, w_ref, o_ref):
        o_ref[...] = jnp.dot(x_ref[...].astype(BF16), w_ref[...].astype(BF16),
                             preferred_element_type=F32).astype(out_dtype)

    kw = dict(name=name, grid=(steps,), in_specs=[_full((m, k)), pl.BlockSpec((k, tn), lambda j: (0, j))])
    if comm is None:
        return _call(body, out_specs=pl.BlockSpec((m, tn), lambda j: (0, j)),
                     out_shape=jax.ShapeDtypeStruct((m, n), out_dtype), sem=("parallel",), **kw)(x, w)

    def at():
        j = pl.program_id(0)
        return j == 0, j == steps // 2, j == steps - 1

    (res,), moved = _call_with_comm(body, comm, at, (x, w), out_specs=[pl.BlockSpec((m, tn), lambda j: (0, j))],
                                    out_shape=[jax.ShapeDtypeStruct((m, n), out_dtype)], **kw)
    return res, moved


def mm_nt(name, dy, w, out_dtype=F32, comm=None):
    m, n = dy.shape
    k = w.shape[0]
    tm = m // 2
    tn = _col_tile(n, k)
    steps = n // tn

    def at():
        i, j = pl.program_id(0), pl.program_id(1)
        return (i == 0) & (j == 0), (i == 0) & (j == steps - 1), (i == m // tm - 1) & (j == steps - 1)

    def body(dy_ref, w_ref, o_ref, acc_ref):
        j = pl.program_id(1)

        @pl.when(j == 0)
        def _():
            acc_ref[...] = jnp.zeros_like(acc_ref)

        acc_ref[...] += lax.dot_general(dy_ref[...].astype(BF16), w_ref[...].astype(BF16), (((1,), (1,)), ((), ())),
                                        preferred_element_type=F32)

        @pl.when(j == steps - 1)
        def _():
            o_ref[...] = acc_ref[...].astype(out_dtype)

    kw = dict(name=name, grid=(m // tm, steps), scratch=[pltpu.VMEM((tm, k), F32)],
              in_specs=[pl.BlockSpec((tm, tn), lambda i, j: (i, j)), pl.BlockSpec((k, tn), lambda i, j: (0, j))])
    if comm is None:
        return _call(body, out_specs=pl.BlockSpec((tm, k), lambda i, j: (i, 0)),
                     out_shape=jax.ShapeDtypeStruct((m, k), out_dtype), sem=("parallel", "arbitrary"), **kw)(dy, w)
    (res,), moved = _call_with_comm(body, comm, at, (dy, w), out_specs=[pl.BlockSpec((tm, k), lambda i, j: (i, 0))],
                                    out_shape=[jax.ShapeDtypeStruct((m, k), out_dtype)], **kw)
    return res, moved


def mm_tn(name, x, dy):
    m, k = x.shape
    n = dy.shape[1]
    tm = m // 2
    tn = _col_tile(n, k)

    def body(x_ref, dy_ref, o_ref):
        @pl.when(pl.program_id(1) == 0)
        def _():
            o_ref[...] = jnp.zeros_like(o_ref)

        o_ref[...] += lax.dot_general(x_ref[...].astype(BF16), dy_ref[...].astype(BF16), (((0,), (0,)), ((), ())),
                                      preferred_element_type=F32)

    return _call(body, name=name, grid=(n // tn, m // tm),
                 in_specs=[pl.BlockSpec((tm, k), lambda i, j: (j, 0)), pl.BlockSpec((tm, tn), lambda i, j: (j, i))],
                 out_specs=pl.BlockSpec((k, tn), lambda i, j: (0, i)), out_shape=jax.ShapeDtypeStruct((k, n), F32),
                 sem=("parallel", "arbitrary"))(x, dy)


def _row_spec(width, col_block=0):
    return pl.BlockSpec((BLOCK, width), lambda i: (i, col_block))


def _x_spec():
    return pl.BlockSpec((None, BLOCK, D_MODEL), lambda i: (0, jnp.maximum(i - 1, 0), 0))


def prep(x, meta, g_pre):
    nb = x.shape[1] // BLOCK + 1

    def body(x_ref, meta_ref, g_ref, h_ref, u_ref):
        i = pl.program_id(0)

        @pl.when(i == 0)
        def _():
            h_ref[0:PAD_ROWS, :] = jnp.zeros((PAD_ROWS, D_MODEL), F32)
            h_ref[PAD_ROWS:BLOCK, :] = meta_ref[...]

        @pl.when(i > 0)
        def _():
            h_ref[...] = x_ref[...]

        u_ref[...] = _rms(h_ref[...], g_ref[...]).astype(BF16)

    return _call(body, name="prep", grid=(nb,), in_specs=[_x_spec(), _full((N_META, D_MODEL)), _full((1, D_MODEL))],
                 out_specs=[_row_spec(D_MODEL), _row_spec(D_MODEL)],
                 out_shape=[jax.ShapeDtypeStruct((nb * BLOCK, D_MODEL), F32),
                            jax.ShapeDtypeStruct((nb * BLOCK, D_MODEL), BF16)], sem=("parallel",))(x, meta, g_pre)


def prep_bwd(h, du, dres, g_pre):
    nb = h.shape[0] // BLOCK

    def body(h_ref, du_ref, dres_ref, g_ref, gx_ref, gm_ref, gg_ref):
        i = pl.program_id(0)
        _, vjp = jax.vjp(_rms, h_ref[...], g_ref[...])
        dh, dg = vjp(du_ref[...])

        @pl.when(i == 0)
        def _():
            gm_ref[...] = dh[PAD_ROWS:BLOCK, :]
            gg_ref[...] = dg

        @pl.when(i > 0)
        def _():
            gg_ref[...] += dg

        gx_ref[...] = dh + dres_ref[...]

    return _call(body, name="prep_bwd", grid=(nb,),
                 in_specs=[_row_spec(D_MODEL), _row_spec(D_MODEL), _row_spec(D_MODEL), _full((1, D_MODEL))],
                 out_specs=[_x_spec(), _full((N_META, D_MODEL)), _full((1, D_MODEL))],
                 out_shape=[jax.ShapeDtypeStruct((1, (nb - 1) * BLOCK, D_MODEL), F32),
                            jax.ShapeDtypeStruct((N_META, D_MODEL), F32), jax.ShapeDtypeStruct((1, D_MODEL), F32)],
                 sem=("arbitrary",))(h, du, dres, g_pre)


GROUP_W = SSM_INNER // SSM_GROUPS


def _gated_norm(y, z, g):
    t = y * _silu(z)
    return t * lax.rsqrt(jnp.mean(t * t, axis=-1, keepdims=True) + NORM_EPS) * g


def _gated_norm_groups(y, z, g):
    groups = [slice(k * GROUP_W, (k + 1) * GROUP_W) for k in range(SSM_GROUPS)]
    return jnp.concatenate([_gated_norm(y[:, s], z[:, s], g[:, s]) for s in groups], axis=1)


def _merge(ga, gs, ya, ys):
    return jax.nn.sigmoid(ga) * ya + jax.nn.sigmoid(gs) * ys


GATE_ATT_BLOCK = SEG["gate_att"][2] // D_MODEL
GATE_SSM_BLOCK = SEG["gate_ssm"][2] // D_MODEL


def _row_loss(out, g_post, x, target):
    diff = x + _rms(out, g_post) - target
    return 0.5 * jnp.sum(diff * diff) / D_MODEL


def tail(y_ssd, proj, a_att, x, target, woa, wos, wo, g_norm, g_post):
    nb = y_ssd.shape[0] // BLOCK
    rows = nb * BLOCK

    def body(y_ref, z_ref, ga_ref, gs_ref, a_ref, x_ref, t_ref, woa_ref, wos_ref, wo_ref, gn_ref, gp_ref,
             yn_ref, mg_ref, dout_ref, dya_ref, dys_ref, da_ref, dy_ref, dz_ref, dga_ref, dgs_ref, dres_ref,
             loss_ref, dgp_ref, dgn_ref):
        i = pl.program_id(0)
        yn, norm_vjp = jax.vjp(_gated_norm_groups, y_ref[...], z_ref[...], gn_ref[...])
        yn16 = yn.astype(BF16)
        y_ssm = jnp.dot(yn16, wos_ref[...], preferred_element_type=F32)
        y_att = jnp.dot(a_ref[...], woa_ref[...], preferred_element_type=F32)
        merged, merge_vjp = jax.vjp(_merge, ga_ref[...], gs_ref[...], y_att, y_ssm)
        merged16 = merged.astype(BF16)
        out = jnp.dot(merged16, wo_ref[...], preferred_element_type=F32)
        loss, loss_vjp = jax.vjp(_row_loss, out, gp_ref[...], x_ref[...], t_ref[...])
        counted = jnp.where(i > 0, 1.0, 0.0)
        dout, dgp, dres, _ = loss_vjp(counted)
        dout16 = dout.astype(BF16)
        dmerged = lax.dot_general(dout16, wo_ref[...], _NT, preferred_element_type=F32)
        dga, dgs, dya, dys = merge_vjp(dmerged)
        dya16, dys16 = dya.astype(BF16), dys.astype(BF16)
        dyn = lax.dot_general(dys16, wos_ref[...], _NT, preferred_element_type=F32)
        dy, dz, dgn = norm_vjp(dyn)

        yn_ref[...] = yn16
        mg_ref[...] = merged16
        dout_ref[...] = dout16
        dya_ref[...] = dya16
        dys_ref[...] = dys16
        da_ref[...] = lax.dot_general(dya16, woa_ref[...], _NT, preferred_element_type=F32)
        dy_ref[...] = dy
        dz_ref[...] = dz.astype(BF16)
        dga_ref[...] = dga.astype(BF16)
        dgs_ref[...] = dgs.astype(BF16)
        dres_ref[...] = dres

        @pl.when(i == 0)
        def _():
            loss_ref[...] = jnp.zeros_like(loss_ref)
            dgp_ref[...] = jnp.zeros_like(dgp_ref)
            dgn_ref[...] = jnp.zeros_like(dgn_ref)

        loss_ref[...] += loss * counted
        dgp_ref[...] += dgp
        dgn_ref[...] += dgn

    wide, narrow = _row_spec(SSM_INNER), _row_spec(D_MODEL)
    resident = pl.BlockSpec(memory_space=pltpu.VMEM)
    bf = lambda w: jax.ShapeDtypeStruct((rows, w), BF16)
    f32 = lambda w: jax.ShapeDtypeStruct((rows, w), F32)
    return _call(body, name="tail", grid=(nb,),
                 in_specs=[wide, wide, _row_spec(D_MODEL, GATE_ATT_BLOCK), _row_spec(D_MODEL, GATE_SSM_BLOCK), narrow,
                           _x_spec(), _x_spec(), resident, resident, resident, _full((1, SSM_INNER)),
                           _full((1, D_MODEL))],
                 out_specs=[wide, narrow, narrow, narrow, narrow, narrow, wide, wide, narrow, narrow, narrow,
                            _full((8, LANES)), _full((1, D_MODEL)), _full((1, SSM_INNER))],
                 out_shape=[bf(SSM_INNER), bf(D_MODEL), bf(D_MODEL), bf(D_MODEL), bf(D_MODEL), f32(D_MODEL),
                            f32(SSM_INNER), bf(SSM_INNER), bf(D_MODEL), bf(D_MODEL), f32(D_MODEL),
                            jax.ShapeDtypeStruct((8, LANES), F32), jax.ShapeDtypeStruct((1, D_MODEL), F32),
                            jax.ShapeDtypeStruct((1, SSM_INNER), F32)],
                 sem=("arbitrary",))(y_ssd, proj, proj, proj, a_att, x, target, woa, wos, wo, g_norm, g_post)


_NT = (((1,), (1,)), ((), ()))
ALIBI_SLOPES = tuple(2.0 ** (-8.0 * (h + 1) / ATT_Q_HEADS) for h in range(ATT_Q_HEADS))
KV_WIDTH = ATT_KV_HEADS * HEAD_DIM
Q_BLOCK = SEG["q"][2] // D_MODEL
Z_ATT_BLOCK = SEG["z_att"][2] // D_MODEL
K_BLOCK = SEG["k"][2] // KV_WIDTH
V_BLOCK = SEG["v"][2] // KV_WIDTH
META_ROW_BLOCK = PAD_ROWS // N_META


@jax.custom_vjp
def _swap_halves(x):
    return pltpu.roll(x, HEAD_DIM, 1)


_swap_halves.defvjp(lambda x: (pltpu.roll(x, HEAD_DIM, 1), None), lambda _, g: (pltpu.roll(g, HEAD_DIM, 1),))


def _both_halves(t, half):
    first = lax.broadcasted_iota(jnp.int32, t.shape, 1) < HEAD_DIM
    sw = _swap_halves(t)
    return jnp.where(first, t, sw) if half == 0 else jnp.where(first, sw, t)


def _attn_rows(q, z, kp, kc, vp, vc, km, vm, sinks, n):
    rows = ATT_GROUP * BLOCK
    i = lax.broadcasted_iota(jnp.int32, (rows, BLOCK), 0) & (BLOCK - 1)
    j = lax.broadcasted_iota(jnp.int32, (rows, BLOCK), 1)
    rel_c = (i - j).astype(F32)
    rel_p = rel_c + float(BLOCK)
    nv = jnp.zeros((rows, BLOCK), jnp.int32) + n
    ok_c = (i >= j) & (nv >= 1)
    ok_p = (j > i) & (nv >= 2)
    im = lax.broadcasted_iota(jnp.int32, (rows, N_META), 0) & (BLOCK - 1)
    jm = lax.broadcasted_iota(jnp.int32, (rows, N_META), 1)
    ok_m = ((jnp.zeros((rows, N_META), jnp.int32) + n) >= 1) | (im >= PAD_ROWS + jm)
    first = lax.broadcasted_iota(jnp.int32, (BLOCK, LANES), 1) < HEAD_DIM
    neg = -jnp.inf
    outs = []
    for kv in range(ATT_KV_HEADS):
        tile, half = divmod(kv, 2)
        lanes = slice(tile * LANES, (tile + 1) * LANES)
        kc2, kp2, km2 = (_both_halves(t[:, lanes], half).astype(BF16) for t in (kc, kp, km))
        vc2, vp2, vm2 = (_both_halves(t[:, lanes], half).astype(BF16) for t in (vc, vp, vm))
        qs, slope, sk = [], [], []
        for pair in range(ATT_GROUP // 2):
            c0 = (kv * ATT_GROUP + 2 * pair) * HEAD_DIM
            qp = q[:, c0:c0 + LANES] * HEAD_DIM ** -0.5
            qs += [jnp.where(first, qp, 0.0), jnp.where(first, 0.0, qp)]
        for g in range(ATT_GROUP):
            slope.append(jnp.full((BLOCK, 1), ALIBI_SLOPES[kv * ATT_GROUP + g], F32))
            sk.append(jnp.broadcast_to(sinks[kv * ATT_GROUP + g], (BLOCK, 1)))
        qs = jnp.concatenate(qs, axis=0).astype(BF16)
        slope = jnp.concatenate(slope, axis=0)
        sk = jnp.concatenate(sk, axis=0)
        sc = jnp.where(ok_c, lax.dot_general(qs, kc2, _NT, preferred_element_type=F32) - slope * rel_c, neg)
        sp = jnp.where(ok_p, lax.dot_general(qs, kp2, _NT, preferred_element_type=F32) - slope * rel_p, neg)
        sm = jnp.where(ok_m, lax.dot_general(qs, km2, _NT, preferred_element_type=F32), neg)
        mx = jnp.maximum(jnp.maximum(jnp.max(sc, axis=1, keepdims=True), jnp.max(sp, axis=1, keepdims=True)),
                         jnp.maximum(jnp.max(sm, axis=1, keepdims=True), sk))
        mx = lax.stop_gradient(mx)
        ec, ep, em, es = jnp.exp(sc - mx), jnp.exp(sp - mx), jnp.exp(sm - mx), jnp.exp(sk - mx)
        den = (es + jnp.sum(ec, axis=1, keepdims=True) + jnp.sum(ep, axis=1, keepdims=True)
               + jnp.sum(em, axis=1, keepdims=True))
        inv = 1.0 / den
        o = (jnp.dot((ec * inv).astype(BF16), vc2, preferred_element_type=F32)
             + jnp.dot((ep * inv).astype(BF16), vp2, preferred_element_type=F32)
             + jnp.dot((em * inv).astype(BF16), vm2, preferred_element_type=F32))
        for pair in range(ATT_GROUP // 2):
            r0 = 2 * pair * BLOCK
            outs.append(jnp.where(first, o[r0:r0 + BLOCK], o[r0 + BLOCK:r0 + 2 * BLOCK]))
    return jnp.concatenate(outs, axis=1) * _silu(z)


def _attn_specs(nb, steps_clamped):
    def blk(t):
        return jnp.minimum(t, nb - 1) if steps_clamped else t

    wide = lambda col: pl.BlockSpec((BLOCK, D_MODEL), lambda t: (blk(t), col))
    cur = lambda col: pl.BlockSpec((BLOCK, KV_WIDTH), lambda t: (blk(t), col))
    prev = lambda col: pl.BlockSpec((BLOCK, KV_WIDTH), lambda t: (jnp.maximum(blk(t) - 1, 0), col))
    meta = lambda col: pl.BlockSpec((N_META, KV_WIDTH), lambda t: (META_ROW_BLOCK, col))
    sinks = pl.BlockSpec((ATT_Q_HEADS, 1, 1), lambda t: (0, 0, 0))
    return [wide(Q_BLOCK), wide(Z_ATT_BLOCK), prev(K_BLOCK), cur(K_BLOCK), prev(V_BLOCK), cur(V_BLOCK),
            meta(K_BLOCK), meta(V_BLOCK), sinks]


def attn_fwd(proj, sinks):
    nb = proj.shape[0] // BLOCK

    def body(q_ref, z_ref, kp_ref, kc_ref, vp_ref, vc_ref, km_ref, vm_ref, sk_ref, o_ref):
        o_ref[...] = _attn_rows(q_ref[...], z_ref[...], kp_ref[...], kc_ref[...], vp_ref[...], vc_ref[...],
                                km_ref[...], vm_ref[...], tuple(sk_ref[h] for h in range(ATT_Q_HEADS)),
                                pl.program_id(0)).astype(BF16)

    return _call(body, name="attn_fwd", grid=(nb,), in_specs=_attn_specs(nb, False), out_specs=_row_spec(D_MODEL),
                 out_shape=jax.ShapeDtypeStruct((nb * BLOCK, D_MODEL), BF16), sem=("parallel",))(*([proj] * 8), sinks)


def attn_bwd(da, proj, sinks):
    nb = proj.shape[0] // BLOCK
    last = nb - 1
    wide = pl.BlockSpec((BLOCK, D_MODEL), lambda t: (jnp.minimum(t, last), 0))
    done = pl.BlockSpec((BLOCK, KV_WIDTH), lambda t: (jnp.maximum(t - 1, 0), 0))
    meta = _full((N_META, KV_WIDTH))
    par = _full((ATT_Q_HEADS, 1, 1))

    def body(da_ref, q_ref, z_ref, kp_ref, kc_ref, vp_ref, vc_ref, km_ref, vm_ref, sk_ref,
             dq_ref, dz_ref, dk_ref, dv_ref, dkm_ref, dvm_ref, dsk_ref, ck_ref, cv_ref):
        t = pl.program_id(0)

        @pl.when(t == 0)
        def _():
            ck_ref[...] = jnp.zeros_like(ck_ref)
            cv_ref[...] = jnp.zeros_like(cv_ref)
            dkm_ref[...] = jnp.zeros_like(dkm_ref)
            dvm_ref[...] = jnp.zeros_like(dvm_ref)
            dsk_ref[...] = jnp.zeros_like(dsk_ref)

        @pl.when(t < nb)
        def _():
            def f(q, z, kp, kc, vp, vc, km, vm, sk):
                return _attn_rows(q, z, kp, kc, vp, vc, km, vm, sk, t)

            _, vjp = jax.vjp(f, q_ref[...], z_ref[...], kp_ref[...], kc_ref[...], vp_ref[...], vc_ref[...],
                             km_ref[...], vm_ref[...], tuple(sk_ref[h] for h in range(ATT_Q_HEADS)))
            dq, dz, dkp, dkc, dvp, dvc, dkm, dvm, dsk = vjp(da_ref[...])
            dq_ref[...] = dq.astype(BF16)
            dz_ref[...] = dz.astype(BF16)
            for h in range(ATT_Q_HEADS):
                dsk_ref[h] += dsk[h]
            dk_ref[...] = ck_ref[...] + dkp
            dv_ref[...] = cv_ref[...] + dvp
            ck_ref[...] = dkc
            cv_ref[...] = dvc
            dkm_ref[...] += dkm
            dvm_ref[...] += dvm

        @pl.when(t == nb)
        def _():
            dk_ref[...] = ck_ref[...]
            dv_ref[...] = cv_ref[...]

    rows = nb * BLOCK
    return _call(body, name="attn_bwd", grid=(nb + 1,), in_specs=[wide] + _attn_specs(nb, True),
                 out_specs=[wide, wide, done, done, meta, meta, par],
                 out_shape=[jax.ShapeDtypeStruct((rows, D_MODEL), BF16), jax.ShapeDtypeStruct((rows, D_MODEL), BF16),
                            jax.ShapeDtypeStruct((rows, KV_WIDTH), F32), jax.ShapeDtypeStruct((rows, KV_WIDTH), F32),
                            jax.ShapeDtypeStruct((N_META, KV_WIDTH), F32), jax.ShapeDtypeStruct((N_META, KV_WIDTH), F32),
                            jax.ShapeDtypeStruct(sinks.shape, F32)],
                 scratch=[pltpu.VMEM((BLOCK, KV_WIDTH), F32), pltpu.VMEM((BLOCK, KV_WIDTH), F32)],
                 sem=("arbitrary",))(da, *([proj] * 8), sinks)


XBC_BLOCK0 = SEG["xbc"][2] // D_MODEL
CONV_COL_BLOCKS = CONV_DIM // D_MODEL
DT_TILE = SEG["dt"][2] // LANES


def _shift_rows(cur, prev, j, row):
    if j == 0:
        return cur
    return jnp.where(row >= j, pltpu.roll(cur, j, 0), pltpu.roll(prev, j, 0))


def _conv_pre(cur, prev, w_ref, b_ref, row):
    pre = b_ref[...] + w_ref[CONV_WIDTH - 1:CONV_WIDTH, :] * cur
    for k in range(CONV_WIDTH - 1):
        pre = pre + w_ref[k:k + 1, :] * _shift_rows(cur, prev, CONV_WIDTH - 1 - k, row)
    return pre


def conv_fwd(proj, conv_w, conv_b):
    nb = proj.shape[0] // BLOCK
    cur = pl.BlockSpec((BLOCK, D_MODEL), lambda j, i: (i, XBC_BLOCK0 + j))
    prev = pl.BlockSpec((BLOCK, D_MODEL), lambda j, i: (jnp.maximum(i - 1, 0), XBC_BLOCK0 + j))

    def body(c_ref, p_ref, w_ref, b_ref, o_ref):
        i = pl.program_id(1)
        row = lax.broadcasted_iota(jnp.int32, (BLOCK, D_MODEL), 0)
        prevv = p_ref[...] * jnp.where(i > 0, 1.0, 0.0)
        pre = _conv_pre(c_ref[...], prevv, w_ref, b_ref, row)
        valid = jnp.maximum((row >= PAD_ROWS).astype(F32), jnp.where(i > 0, 1.0, 0.0))
        o_ref[...] = _silu(pre) * valid

    return _call(body, name="conv_fwd", grid=(CONV_COL_BLOCKS, nb),
                 in_specs=[cur, prev, pl.BlockSpec((CONV_WIDTH, D_MODEL), lambda j, i: (0, j)),
                           pl.BlockSpec((1, D_MODEL), lambda j, i: (0, j))],
                 out_specs=pl.BlockSpec((BLOCK, D_MODEL), lambda j, i: (i, j)),
                 out_shape=jax.ShapeDtypeStruct((nb * BLOCK, CONV_DIM), F32),
                 sem=("parallel", "parallel"))(proj, proj, conv_w, conv_b)


def conv_bwd(dxbc, proj, conv_w, conv_b):
    nb = proj.shape[0] // BLOCK
    last = nb - 1
    cur = pl.BlockSpec((BLOCK, D_MODEL), lambda j, i: (i, XBC_BLOCK0 + j))
    prev = pl.BlockSpec((BLOCK, D_MODEL), lambda j, i: (jnp.maximum(i - 1, 0), XBC_BLOCK0 + j))
    nxt = pl.BlockSpec((BLOCK, D_MODEL), lambda j, i: (jnp.minimum(i + 1, last), XBC_BLOCK0 + j))
    dcur = pl.BlockSpec((BLOCK, D_MODEL), lambda j, i: (i, j))
    dnxt = pl.BlockSpec((BLOCK, D_MODEL), lambda j, i: (jnp.minimum(i + 1, last), j))
    wspec = pl.BlockSpec((CONV_WIDTH, D_MODEL), lambda j, i: (0, j))
    bspec = pl.BlockSpec((1, D_MODEL), lambda j, i: (0, j))

    def body(dc_ref, dn_ref, c_ref, p_ref, n_ref, w_ref, b_ref, du_ref, dw_ref, db_ref):
        i = pl.program_id(1)
        row = lax.broadcasted_iota(jnp.int32, (BLOCK, D_MODEL), 0)
        curv = c_ref[...]
        prevv = p_ref[...] * jnp.where(i > 0, 1.0, 0.0)

        def dpre_of(pre, d, valid):
            s = jax.nn.sigmoid(pre)
            return d * valid * (s * (1.0 + pre * (1.0 - s)))

        valid = jnp.maximum((row >= PAD_ROWS).astype(F32), jnp.where(i > 0, 1.0, 0.0))
        dp_c = dpre_of(_conv_pre(curv, prevv, w_ref, b_ref, row), dc_ref[...], valid)
        has_next = jnp.where(i < last, 1.0, 0.0)
        dp_n = dpre_of(_conv_pre(n_ref[...], curv, w_ref, b_ref, row), dn_ref[...], has_next)
        du = w_ref[CONV_WIDTH - 1:CONV_WIDTH, :] * dp_c
        for j in range(1, CONV_WIDTH):
            up = jnp.where(row < BLOCK - j, pltpu.roll(dp_c, BLOCK - j, 0), pltpu.roll(dp_n, BLOCK - j, 0))
            du = du + w_ref[CONV_WIDTH - 1 - j:CONV_WIDTH - j, :] * up
        du_ref[...] = du.astype(BF16)

        @pl.when(i == 0)
        def _():
            dw_ref[...] = jnp.zeros_like(dw_ref)
            db_ref[...] = jnp.zeros_like(db_ref)

        for k in range(CONV_WIDTH):
            dw_ref[k:k + 1, :] += jnp.sum(dp_c * _shift_rows(curv, prevv, CONV_WIDTH - 1 - k, row), axis=0,
                                          keepdims=True)
        db_ref[...] += jnp.sum(dp_c, axis=0, keepdims=True)

    return _call(body, name="conv_bwd", grid=(CONV_COL_BLOCKS, nb),
                 in_specs=[dcur, dnxt, cur, prev, nxt, wspec, bspec], out_specs=[dcur, wspec, bspec],
                 out_shape=[jax.ShapeDtypeStruct((nb * BLOCK, CONV_DIM), BF16),
                            jax.ShapeDtypeStruct((CONV_WIDTH, CONV_DIM), F32), jax.ShapeDtypeStruct((1, CONV_DIM), F32)],
                 sem=("parallel", "arbitrary"))(dxbc, dxbc, proj, proj, proj, conv_w, conv_b)


def _head_expand():
    e = np.zeros((LANES, SSM_INNER), np.float32)
    for h in range(SSM_HEADS):
        e[h, h * HEAD_DIM:(h + 1) * HEAD_DIM] = 1.0
    return jnp.asarray(e)


def _softplus(x):
    return jnp.maximum(x, 0.0) + jnp.log(1.0 + jnp.exp(-jnp.abs(x)))


def dt_fwd(proj, expand, bias_x):
    nb = proj.shape[0] // BLOCK

    def body(t_ref, e_ref, b_ref, o_ref):
        raw = jnp.dot(t_ref[...], e_ref[...], precision=HI, preferred_element_type=F32)
        o_ref[...] = _softplus(raw + b_ref[...])

    return _call(body, name="dt_fwd", grid=(nb,),
                 in_specs=[_row_spec(LANES, DT_TILE), _full((LANES, SSM_INNER)), _full((1, SSM_INNER))],
                 out_specs=_row_spec(SSM_INNER), out_shape=jax.ShapeDtypeStruct((nb * BLOCK, SSM_INNER), F32),
                 sem=("parallel",))(proj, expand, bias_x)


def dt_bwd(ddt_x, proj, expand, bias_x):
    nb = proj.shape[0] // BLOCK

    def body(d_ref, t_ref, e_ref, b_ref, o_ref, db_ref):
        raw = jnp.dot(t_ref[...], e_ref[...], precision=HI, preferred_element_type=F32)
        draw = d_ref[...] * jax.nn.sigmoid(raw + b_ref[...])
        dt = lax.dot_general(draw, e_ref[...], _NT, precision=HI, preferred_element_type=F32)
        o_ref[...] = dt.astype(BF16)

        @pl.when(pl.program_id(0) == 0)
        def _():
            db_ref[...] = jnp.zeros_like(db_ref)

        db_ref[...] += jnp.sum(dt, axis=0, keepdims=True)

    return _call(body, name="dt_bwd", grid=(nb,),
                 in_specs=[_row_spec(SSM_INNER), _row_spec(LANES, DT_TILE), _full((LANES, SSM_INNER)),
                           _full((1, SSM_INNER))],
                 out_specs=[_row_spec(LANES), _full((1, LANES))],
                 out_shape=[jax.ShapeDtypeStruct((nb * BLOCK, LANES), BF16), jax.ShapeDtypeStruct((1, LANES), F32)],
                 sem=("arbitrary",))(ddt_x, proj, expand, bias_x)


def _ssd_group(xs, dtx, bg, cg, alog, dsk, state):
    l = lax.broadcasted_iota(jnp.int32, (BLOCK, BLOCK), 0)
    s = lax.broadcasted_iota(jnp.int32, (BLOCK, BLOCK), 1)
    causal = l >= s
    first_head = s < HEAD_DIM
    a = dtx * (-jnp.exp(alog))
    cs = jnp.dot(causal.astype(F32), a, precision=HI, preferred_element_type=F32)
    tot = jnp.sum(a, axis=0, keepdims=True)
    bb, cb16 = bg.astype(BF16), cg.astype(BF16)
    cb = lax.dot_general(cb16, bb, _NT, preferred_element_type=F32)
    xr = xs * dtx
    y_diag = []
    for p in range(GROUP_W // LANES):
        lanes = slice(p * LANES, (p + 1) * LANES)
        c_pair = cs[:, lanes]
        c_swap = _swap_halves(c_pair)
        m = []
        for c_head in (jnp.where(first_head, c_pair, c_swap), jnp.where(first_head, c_swap, c_pair)):
            m.append(cb * jnp.exp(jnp.where(causal, c_head - c_head.T, -jnp.inf)))
        x_pair = xr[:, lanes]
        x_diag = jnp.concatenate([jnp.where(first_head, x_pair, 0.0), jnp.where(first_head, 0.0, x_pair)], axis=0)
        y_diag.append(jnp.dot(jnp.concatenate(m, axis=1).astype(BF16), x_diag.astype(BF16),
                              preferred_element_type=F32))
    st = lax.dot_general(bb, (xr * jnp.exp(tot - cs)).astype(BF16), (((0,), (0,)), ((), ())),
                         preferred_element_type=F32)
    new_state = state * jnp.exp(tot) + st
    y_off = jnp.dot(cb16, state.astype(BF16), preferred_element_type=F32) * jnp.exp(cs)
    return jnp.concatenate(y_diag, axis=1) + y_off + dsk * xs, new_state


B_TILE0 = SSM_INNER // LANES
C_TILE0 = B_TILE0 + SSM_GROUPS


def ssd_fwd(xbc, dt_x, alog_x, dsk_x):
    nb = xbc.shape[0] // BLOCK
    gspec = pl.BlockSpec((BLOCK, GROUP_W), lambda g, c: (c, g))
    pspec = pl.BlockSpec((1, GROUP_W), lambda g, c: (0, g))

    def body(x_ref, dt_ref, b_ref, c_ref, al_ref, dk_ref, y_ref, sp_ref, st_ref):
        @pl.when(pl.program_id(1) == 0)
        def _():
            st_ref[...] = jnp.zeros_like(st_ref)

        state = st_ref[...]
        sp_ref[...] = state
        y_ref[...], st_ref[...] = _ssd_group(x_ref[...], dt_ref[...], b_ref[...], c_ref[...], al_ref[...],
                                             dk_ref[...], state)

    return _call(body, name="ssd_fwd", grid=(SSM_GROUPS, nb),
                 in_specs=[gspec, gspec, pl.BlockSpec((BLOCK, SSM_STATE), lambda g, c: (c, B_TILE0 + g)),
                           pl.BlockSpec((BLOCK, SSM_STATE), lambda g, c: (c, C_TILE0 + g)), pspec, pspec],
                 out_specs=[gspec, pl.BlockSpec((None, SSM_STATE, GROUP_W), lambda g, c: (c, 0, g))],
                 out_shape=[jax.ShapeDtypeStruct((nb * BLOCK, SSM_INNER), F32),
                            jax.ShapeDtypeStruct((nb, SSM_STATE, SSM_INNER), F32)],
                 scratch=[pltpu.VMEM((SSM_STATE, GROUP_W), F32)],
                 sem=("parallel", "arbitrary"))(xbc, dt_x, xbc, xbc, alog_x, dsk_x)


def ssd_bwd(dy, xbc, dt_x, alog_x, dsk_x, states, comm):
    nb = xbc.shape[0] // BLOCK
    last = nb - 1
    gspec = pl.BlockSpec((BLOCK, GROUP_W), lambda g, c: (last - c, g))
    pspec = pl.BlockSpec((1, GROUP_W), lambda g, c: (0, g))
    nspec = pl.BlockSpec((BLOCK, SSM_STATE), lambda g, c: (last - c, g))

    def body(dy_ref, x_ref, dt_ref, b_ref, c_ref, al_ref, dk_ref, sp_ref,
             dx_ref, ddt_ref, db_ref, dc_ref, dal_ref, ddk_ref, ds_ref):
        @pl.when(pl.program_id(1) == 0)
        def _():
            ds_ref[...] = jnp.zeros_like(ds_ref)
            dal_ref[...] = jnp.zeros_like(dal_ref)
            ddk_ref[...] = jnp.zeros_like(ddk_ref)

        _, vjp = jax.vjp(_ssd_group, x_ref[...], dt_ref[...], b_ref[...], c_ref[...], al_ref[...], dk_ref[...],
                         sp_ref[...])
        dx_ref[...], ddt_ref[...], db_ref[...], dc_ref[...], dal, ddk, ds_ref[...] = vjp((dy_ref[...], ds_ref[...]))
        dal_ref[...] += dal
        ddk_ref[...] += ddk

    def at():
        g, c = pl.program_id(0), pl.program_id(1)
        return (g == 0) & (c == 0), (g == 0) & (c == last), (g == SSM_GROUPS - 1) & (c == last)

    return _call_with_comm(
        body, comm, at, (dy, xbc, dt_x, xbc, xbc, alog_x, dsk_x, states), name="ssd_bwd", grid=(SSM_GROUPS, nb),
        in_specs=[gspec, gspec, gspec, pl.BlockSpec((BLOCK, SSM_STATE), lambda g, c: (last - c, B_TILE0 + g)),
                  pl.BlockSpec((BLOCK, SSM_STATE), lambda g, c: (last - c, C_TILE0 + g)), pspec, pspec,
                  pl.BlockSpec((None, SSM_STATE, GROUP_W), lambda g, c: (last - c, 0, g))],
        out_specs=[gspec, gspec, nspec, nspec, pspec, pspec],
        out_shape=[jax.ShapeDtypeStruct((nb * BLOCK, SSM_INNER), F32),
                   jax.ShapeDtypeStruct((nb * BLOCK, SSM_INNER), F32),
                   jax.ShapeDtypeStruct((nb * BLOCK, SSM_GROUPS * SSM_STATE), F32),
                   jax.ShapeDtypeStruct((nb * BLOCK, SSM_GROUPS * SSM_STATE), F32),
                   jax.ShapeDtypeStruct((1, SSM_INNER), F32), jax.ShapeDtypeStruct((1, SSM_INNER), F32)],
        scratch=[pltpu.VMEM((SSM_STATE, GROUP_W), F32)])


SLAB_ROWS = 24
SLAB_META_ROW = 8


def pack_small(dcw, dcb, dgpre, dgpost, ddtb, dalog_x, ddsk_x, dsinks, dgn, dmeta, expand):
    def body(cw, cb, gpre, gpost, dtb, al, dk, sk, gn, meta, e_ref, o_ref):
        def per_head(v):
            rows = jnp.broadcast_to(v[...], (8, SSM_INNER))
            return lax.dot_general(rows, e_ref[...], _NT, precision=HI, preferred_element_type=F32)[0:1]

        o_ref[...] = jnp.zeros_like(o_ref)
        o_ref[0:CONV_WIDTH, :] = cw[...]
        o_ref[4:5, :] = cb[...]
        o_ref[5:6, 0:1024] = gpre[...]
        o_ref[5:6, 1024:2048] = gpost[...]
        o_ref[5:6, 2048:2176] = dtb[...]
        o_ref[5:6, 2176:2304] = per_head(al)
        o_ref[5:6, 2304:2432] = per_head(dk)
        o_ref[5:6, 2432:2560] = sk[...]
        o_ref[6:7, 0:SSM_INNER] = gn[...]
        o_ref[SLAB_META_ROW:SLAB_META_ROW + N_META, 0:D_MODEL] = meta[...]

    args = (dcw, dcb, dgpre, dgpost, ddtb, dalog_x, ddsk_x, dsinks, dgn, dmeta, expand)
    return _call(body, name="pack_small", in_specs=[_full(a.shape) for a in args],
                 out_specs=_full((SLAB_ROWS, CONV_DIM)), out_shape=jax.ShapeDtypeStruct((SLAB_ROWS, CONV_DIM), F32))(*args)


def _lane_tile(v):
    return jnp.pad(v, ((0, 0), (0, LANES - v.shape[1])))


def kernel(x, meta_tokens, g_pre, w_in, conv_w, conv_b, dt_bias, a_log, d_skip, attn_sinks, g_ssm_norm, w_out_att, w_out_ssm, w_out, g_post, loss_target, m_meta_tokens, m_g_pre, m_w_in, m_conv_w, m_conv_b, m_dt_bias, m_a_log, m_d_skip, m_attn_sinks, m_g_ssm_norm, m_w_out_att, m_w_out_ssm, m_w_out, m_g_post, v_meta_tokens, v_g_pre, v_w_in, v_conv_w, v_conv_b, v_dt_bias, v_a_log, v_d_skip, v_attn_sinks, v_g_ssm_norm, v_w_out_att, v_w_out_ssm, v_w_out, v_g_post):
    chip = _chip_index()

    conv_w_rows = jnp.pad(conv_w[0], ((0, 2 * 8 - CONV_WIDTH), (0, 0)))
    g_w_in, g_conv_w, g_meta = run_comm("gather_w_in", TwoLevelGather([pack_w_in(w_in[0]), conv_w_rows, meta_tokens]))
    w_all = unpack_w_in(g_w_in)
    cw_full = g_conv_w[:, :CONV_WIDTH].transpose(1, 0, 2).reshape(CONV_WIDTH, CONV_DIM)
    meta_full = g_meta.transpose(1, 0, 2).reshape(N_META, D_MODEL)

    h, u = prep(x, meta_full, g_pre)
    proj, g_w_out = mm_nn("in_proj", u, w_all, comm=TwoLevelGather(
        [w_out_att[0].astype(BF16), w_out_ssm[0].astype(BF16), w_out[0].astype(BF16)]))
    woa = g_w_out[0].reshape(D_MODEL, D_MODEL)
    wos = g_w_out[1].reshape(SSM_INNER, D_MODEL)
    wo = g_w_out[2].reshape(D_MODEL, D_MODEL)

    sinks3 = attn_sinks.reshape(ATT_Q_HEADS, 1, 1)
    a_att = attn_fwd(proj, sinks3)

    xbc = conv_fwd(proj, cw_full, conv_b)
    expand = _head_expand()
    bias_x = jnp.repeat(dt_bias, HEAD_DIM, axis=1)
    dt_x = dt_fwd(proj, expand, bias_x)
    alog_x = jnp.repeat(a_log, HEAD_DIM, axis=1)
    dsk_x = jnp.repeat(d_skip, HEAD_DIM, axis=1)
    y_ssd, states = ssd_fwd(xbc, dt_x, alog_x, dsk_x)

    (yn, merged, dout, dy_att, dy_ssm, da_att, dy_ssd, dz_ssm, dga, dgs, dres, loss_tile, dg_post, dgn) = tail(
        y_ssd, proj, a_att, x, loss_target, woa, wos, wo, g_ssm_norm, g_post)
    loss = lax.psum(loss_tile[0, 0], ("x", "y", "c"))

    dwo = mm_tn("out_proj_dw", merged, dout)
    dwoa = mm_tn("att_out_dw", a_att, dy_att)
    dwos = mm_tn("ssm_out_dw", yn, dy_ssm)
    dq, dz_att, dk, dv, dkmeta, dvmeta, dsinks3 = attn_bwd(da_att, proj, sinks3)
    dk = dk.at[PAD_ROWS:BLOCK].add(dkmeta).astype(BF16)
    dv = dv.at[PAD_ROWS:BLOCK].add(dvmeta).astype(BF16)

    def pieces(g):
        return g.astype(BF16).reshape(4, 2, g.shape[0] // 8, g.shape[1])

    def to_owner(g):
        return (lambda ref, dev: ref.at[_chip_of(dev), dev[2]], (g.shape[0] // 8, g.shape[1]))

    (dxs, ddt_x, dbg, dcg, dalog_x, ddsk_x), sent_w_out = ssd_bwd(
        dy_ssd, xbc, dt_x, alog_x, dsk_x, states,
        DirectExchange([pieces(dwoa), pieces(dwos), pieces(dwo)], [to_owner(dwoa), to_owner(dwos), to_owner(dwo)],
                       ALL_MASKS, "dev", 8))
    dxbc = jnp.concatenate([dxs, dbg, dcg], axis=1)
    dxbc_raw, dcw, dcb = conv_bwd(dxbc, proj, cw_full, conv_b)
    ddt_tile, ddtb_tile = dt_bwd(ddt_x, proj, expand, bias_x)

    dproj = jnp.concatenate([dz_ssm, dxbc_raw, dq, dz_att, dga, dgs, dk, dv, ddt_tile], axis=1)
    dw_all = mm_tn("in_proj_dw", u, dproj)

    half_rows = D_MODEL // 2
    partial = pack_grad_w_in(dw_all).reshape(4, 2, half_rows, PACK_W)
    from_sibling, = run_comm("pair_grads", DirectExchange(
        [partial], [(lambda ref, dev: ref.at[pl.ds(0, 4), dev[2]], (4, half_rows, PACK_W))], SIBLING_MASK, "core", 2,
        keep_own=False))
    chip_sum = sum_pair(partial, from_sibling)
    du, (sent_w_in,) = mm_nt("in_proj_dx", dproj, w_all, comm=DirectExchange(
        [chip_sum], [(lambda ref, dev: ref.at[_chip_of(dev)], (half_rows, PACK_W))], CHIP_MASKS, "chip", 4))
    grad_x, dmeta, dg_pre = prep_bwd(h, du, dres, g_pre)

    slab = pack_small(dcw, dcb, dg_pre, dg_post, ddtb_tile, dalog_x, ddsk_x,
                      _lane_tile(dsinks3.reshape(1, ATT_Q_HEADS)), dgn, dmeta, expand)
    halves = [sum_slots("sum_" + nm, r)
              for nm, r in zip(("w_in", "w_out_att", "w_out_ssm", "w_out"), [sent_w_in] + list(sent_w_out))]
    shared = run_comm("share_grads", Both(DirectExchange(halves, [None] * 4, SIBLING_MASK, "core", 2),
                                          DirectExchange([slab], [None], ALL_MASKS, "dev", 8)))
    g_w_in_packed, g_woa, g_wos, g_wo = [f.reshape(2 * f.shape[1], f.shape[2]) for f in shared[:4]]
    small = sum_slots("sum_small", shared[4])

    g_w_in, d_w_in, nm_w_in, nv_w_in = adamw_w_in(g_w_in_packed, w_in[0], m_w_in[0], v_w_in[0])
    d_woa, nm_woa, nv_woa = adamw_rows("adamw_w_out_att", g_woa, w_out_att[0], m_w_out_att[0], v_w_out_att[0])
    d_wos, nm_wos, nv_wos = adamw_rows("adamw_w_out_ssm", g_wos, w_out_ssm[0], m_w_out_ssm[0], v_w_out_ssm[0])
    d_wo, nm_wo, nv_wo = adamw_rows("adamw_w_out", g_wo, w_out[0], m_w_out[0], v_w_out[0])

    cw_cols = CONV_DIM // 4
    meta_cols = D_MODEL // 4
    g_small = {
        "meta_tokens": lax.dynamic_slice(small, (SLAB_META_ROW, chip * meta_cols), (N_META, meta_cols)),
        "g_pre": small[5:6, 0:1024],
        "conv_w": lax.dynamic_slice(small, (0, chip * cw_cols), (CONV_WIDTH, cw_cols)),
        "conv_b": small[4:5, :],
        "dt_bias": small[5:6, 2048:2048 + SSM_HEADS],
        "a_log": small[5:6, 2176:2176 + SSM_HEADS],
        "d_skip": small[5:6, 2304:2304 + SSM_HEADS],
        "attn_sinks": small[5:6, 2432:2432 + ATT_Q_HEADS],
        "g_ssm_norm": small[6:7, 0:SSM_INNER],
        "g_post": small[5:6, 1024:2048],
    }
    names = list(g_small)
    w_small = dict(meta_tokens=meta_tokens, g_pre=g_pre, conv_w=conv_w[0], conv_b=conv_b, dt_bias=dt_bias, a_log=a_log,
                   d_skip=d_skip, attn_sinks=attn_sinks, g_ssm_norm=g_ssm_norm, g_post=g_post)
    m_small = dict(meta_tokens=m_meta_tokens, g_pre=m_g_pre, conv_w=m_conv_w[0], conv_b=m_conv_b, dt_bias=m_dt_bias,
                   a_log=m_a_log, d_skip=m_d_skip, attn_sinks=m_attn_sinks, g_ssm_norm=m_g_ssm_norm, g_post=m_g_post)
    v_small = dict(meta_tokens=v_meta_tokens, g_pre=v_g_pre, conv_w=v_conv_w[0], conv_b=v_conv_b, dt_bias=v_dt_bias,
                   a_log=v_a_log, d_skip=v_d_skip, attn_sinks=v_attn_sinks, g_ssm_norm=v_g_ssm_norm, g_post=v_g_post)
    upd = dict(zip(names, adamw_small([g_small[k] for k in names], [w_small[k] for k in names],
                                      [m_small[k] for k in names], [v_small[k] for k in names])))

    lead = {"conv_w"}

    def shaped(name, a):
        return a[None] if name in lead else a

    grads = dict(g_small, w_in=g_w_in, w_out_att=g_woa, w_out_ssm=g_wos, w_out=g_wo)
    deltas = dict({k: upd[k][0] for k in names}, w_in=d_w_in, w_out_att=d_woa, w_out_ssm=d_wos, w_out=d_wo)
    new_m = dict({k: upd[k][1] for k in names}, w_in=nm_w_in, w_out_att=nm_woa, w_out_ssm=nm_wos, w_out=nm_wo)
    new_v = dict({k: upd[k][2] for k in names}, w_in=nv_w_in, w_out_att=nv_woa, w_out_ssm=nv_wos, w_out=nv_wo)
    lead |= {"w_in", "w_out_att", "w_out_ssm", "w_out"}
    order = ["meta_tokens", "g_pre", "w_in", "conv_w", "conv_b", "dt_bias", "a_log", "d_skip", "attn_sinks",
             "g_ssm_norm", "w_out_att", "w_out_ssm", "w_out", "g_post"]
    outs = [loss, grad_x]
    for group in (grads, deltas, new_m, new_v):
        outs += [shaped(k, group[k]) for k in order]
    return tuple(outs)
```

```python
import functools

import numpy as np
import jax
import jax.numpy as jnp
from jax import lax
from jax.experimental import pallas as pl
from jax.experimental.pallas import tpu as pltpu

F32 = jnp.float32
BF16 = jnp.bfloat16
HI = lax.Precision.HIGHEST

D_MODEL = 1024
N_META = 16
BLOCK = 128
PAD_ROWS = BLOCK - N_META
NORM_EPS = 1e-6
HEAD_DIM = 64
ATT_Q_HEADS = 16
ATT_KV_HEADS = 4
ATT_GROUP = 4
SSM_INNER = 2048
SSM_HEADS = 32
SSM_GROUPS = 4
SSM_HEADS_PER_GROUP = 8
SSM_STATE = 128
CONV_WIDTH = 4
CONV_DIM = 3072
LANES = 128

ADAM_LR = 0.001
ADAM_B1 = 0.9
ADAM_B2 = 0.999
ADAM_EPS = 1e-08
ADAM_WD = 0.01
ADAM_STEP = 10

VMEM_LIMIT = 48 * 1024 * 1024

SHARD_W = 2440
PACK_W = 2560
SHARD_STRIDE = 2432
N_ALIGNED = 9856
SEG = {
    "q": (0, 1024, 5120), "k": (1024, 256, 9216), "v": (1280, 256, 9472), "z_att": (1536, 1024, 6144),
    "z_ssm": (2560, 2048, 0), "xbc": (4608, 3072, 2048), "dt": (7680, 128, 9728),
    "gate_att": (7808, 1024, 7168), "gate_ssm": (8832, 1024, 8192),
}
DT_STORED_START = 7680
DT_PAD = LANES - SSM_HEADS


def _act_col(aligned_col):
    for a0, w, p0 in SEG.values():
        if a0 <= aligned_col < a0 + w:
            return p0 + aligned_col - a0
    raise ValueError(aligned_col)


def _call(body, *, name, out_shape, in_specs, out_specs, grid=(), scratch=(), sem=None, aliases=None):
    return pl.pallas_call(
        body, out_shape=out_shape, grid=grid, in_specs=in_specs, out_specs=out_specs, scratch_shapes=list(scratch),
        name=name, input_output_aliases=aliases or {},
        compiler_params=pltpu.CompilerParams(dimension_semantics=sem, vmem_limit_bytes=VMEM_LIMIT))


def _full(shape):
    n = len(shape)
    return pl.BlockSpec(shape, lambda *_: (0,) * n)


def _chip_index():
    return lax.axis_index("x") * 2 + lax.axis_index("y")


def _silu(z):
    return z * jax.nn.sigmoid(z)


def _rms(x, g):
    return x * lax.rsqrt(jnp.mean(x * x, axis=-1, keepdims=True) + NORM_EPS) * g


def _peer(mask):
    x, y, c = lax.axis_index("x"), lax.axis_index("y"), lax.axis_index("c")
    return ((1 - x) if mask & 4 else x, (1 - y) if mask & 2 else y, (1 - c) if mask & 1 else c)


def _me():
    return lax.axis_index("x"), lax.axis_index("y"), lax.axis_index("c")


def _chip_of(dev):
    return 2 * dev[0] + dev[1]


CHIP_MASKS = (4, 2, 6)
ALL_MASKS = (1, 2, 3, 4, 5, 6, 7)
SIBLING_MASK = (1,)


def _remote(src, dst, send_sem, recv_sem, dev):
    return pltpu.make_async_remote_copy(src_ref=src, dst_ref=dst, send_sem=send_sem, recv_sem=recv_sem,
                                        device_id=dev, device_id_type=pl.DeviceIdType.MESH)


class _StagedCopy:
    def __init__(self, src, stage, dst, load_sem, store_sem):
        self.load = pltpu.make_async_copy(src, stage, load_sem)
        self.store = pltpu.make_async_copy(stage, dst, store_sem)

    def start(self):
        self.load.start()
        self.load.wait()
        self.store.start()

    def wait(self):
        self.store.wait()


class DirectExchange:
    def __init__(self, arrays, pieces, masks, slot_kind, nslots, keep_own=True):
        self.arrays, self.pieces, self.masks, self.slot_kind = list(arrays), list(pieces), masks, slot_kind
        self.keep_own = keep_own
        n, nk = len(arrays), len(masks)
        shapes = [a.shape if p is None else p[1] for a, p in zip(arrays, pieces)]
        self.out_shape = [jax.ShapeDtypeStruct((nslots,) + tuple(s), a.dtype) for s, a in zip(shapes, arrays)]
        self.scratch = [pltpu.SemaphoreType.DMA((n * nk,)), pltpu.SemaphoreType.DMA((n * nk,))]
        if keep_own:
            self.scratch += [pltpu.SemaphoreType.DMA((2 * n,))] + [pltpu.VMEM(s, a.dtype) for s, a in zip(shapes, arrays)]
        self.has_mid = False

    def _copies(self, ins, outs, scratch):
        send_sems, recv_sems = scratch[:2]
        me = _me()
        slot = {"chip": _chip_of(me), "dev": 4 * me[0] + 2 * me[1] + me[2], "core": me[2]}[self.slot_kind]
        nk = len(self.masks)

        def piece(a, dev):
            return ins[a] if self.pieces[a] is None else self.pieces[a][0](ins[a], dev)

        local = []
        if self.keep_own:
            local_sems, stages = scratch[2], scratch[3:]
            local = [_StagedCopy(piece(a, me), stages[a], outs[a].at[slot], local_sems.at[2 * a], local_sems.at[2 * a + 1])
                     for a in range(len(ins))]
        remote = []
        for a in range(len(ins)):
            for ki, mask in enumerate(self.masks):
                dev = _peer(mask)
                remote.append(_remote(piece(a, dev), outs[a].at[slot], send_sems.at[a * nk + ki],
                                      recv_sems.at[a * nk + ki], dev))
        return local, remote

    def start(self, ins, outs, scratch):
        local, remote = self._copies(ins, outs, scratch)
        for cp in remote + local:
            cp.start()

    def finish(self, ins, outs, scratch):
        local, remote = self._copies(ins, outs, scratch)
        for cp in remote + local:
            cp.wait()


class TwoLevelGather:
    def __init__(self, arrays):
        self.arrays = list(arrays)
        n, nk = len(arrays), len(CHIP_MASKS)
        self.out_shape = [jax.ShapeDtypeStruct((4,) + a.shape, a.dtype) for a in arrays]
        self.scratch = ([pltpu.SemaphoreType.DMA((n * nk,)) for _ in range(4)] + [pltpu.SemaphoreType.DMA((2 * n,))]
                        + [pltpu.VMEM(a.shape, a.dtype) for a in arrays])
        self.has_mid = True

    def _copies(self, ins, outs, scratch):
        ici_send, ici_recv, fwd_send, fwd_recv, local_sems = scratch[:5]
        stages = scratch[5:]
        me = _me()
        sibling = _peer(1)
        nk = len(CHIP_MASKS)
        local, ici, fwd = [], [], []
        for a in range(len(ins)):
            half = ins[a].shape[0] // 2
            mine = pl.ds(me[2] * half, half)
            local.append(_StagedCopy(ins[a], stages[a], outs[a].at[_chip_of(me)], local_sems.at[2 * a],
                                     local_sems.at[2 * a + 1]))
            for ki, mask in enumerate(CHIP_MASKS):
                dev = _peer(mask)
                k = a * nk + ki
                ici.append(_remote(ins[a].at[mine], outs[a].at[_chip_of(me), mine], ici_send.at[k], ici_recv.at[k], dev))
                arrived = outs[a].at[_chip_of(dev), mine]
                fwd.append(_remote(arrived, arrived, fwd_send.at[k], fwd_recv.at[k], sibling))
        return local, ici, fwd

    def start(self, ins, outs, scratch):
        local, ici, _ = self._copies(ins, outs, scratch)
        for cp in ici + local:
            cp.start()

    def mid(self, ins, outs, scratch):
        _, ici, fwd = self._copies(ins, outs, scratch)
        for arrival, onward in zip(ici, fwd):
            arrival.wait_recv()
            onward.start()

    def finish(self, ins, outs, scratch):
        local, ici, fwd = self._copies(ins, outs, scratch)
        for cp in fwd:
            cp.wait_recv()
        for cp in ici + fwd:
            cp.wait_send()
        for cp in local:
            cp.wait()


class Both:
    def __init__(self, a, b):
        self.a, self.b = a, b
        self.arrays, self.out_shape = a.arrays + b.arrays, a.out_shape + b.out_shape
        self.scratch = a.scratch + b.scratch
        self.has_mid = False
        assert not (a.has_mid or b.has_mid)

    def _parts(self, ins, outs, sems):
        na, sa = len(self.a.arrays), len(self.a.scratch)
        return (ins[:na], outs[:na], sems[:sa]), (ins[na:], outs[na:], sems[sa:])

    def start(self, ins, outs, sems):
        pa, pb = self._parts(ins, outs, sems)
        self.a.start(*pa)
        self.b.start(*pb)

    def finish(self, ins, outs, sems):
        pa, pb = self._parts(ins, outs, sems)
        self.a.finish(*pa)
        self.b.finish(*pb)


_ANY = pl.BlockSpec(memory_space=pl.ANY)


def run_comm(name, comm):
    n = len(comm.arrays)

    def body(*refs):
        ins, outs, sems = refs[:n], refs[n:2 * n], refs[2 * n:]
        comm.start(ins, outs, sems)
        if comm.has_mid:
            comm.mid(ins, outs, sems)
        comm.finish(ins, outs, sems)

    return pl.pallas_call(body, name=name, out_shape=comm.out_shape, in_specs=[_ANY] * n, out_specs=[_ANY] * n,
                          scratch_shapes=comm.scratch,
                          compiler_params=pltpu.CompilerParams(vmem_limit_bytes=VMEM_LIMIT))(*comm.arrays)


def _call_with_comm(body, comm, steps, args, *, name, out_shape, in_specs, out_specs, grid, scratch=()):
    ni, no, ns, nc = len(in_specs), len(out_specs), len(scratch), len(comm.arrays)

    def full_body(*refs):
        ins, cins = refs[:ni], refs[ni:ni + nc]
        outs, couts = refs[ni + nc:ni + nc + no], refs[ni + nc + no:ni + 2 * nc + no]
        scr, csems = refs[ni + 2 * nc + no:ni + 2 * nc + no + ns], refs[ni + 2 * nc + no + ns:]
        first, middle, last = steps()
        pl.when(first)(lambda: comm.start(cins, couts, csems))
        if comm.has_mid:
            pl.when(middle)(lambda: comm.mid(cins, couts, csems))
        body(*ins, *outs, *scr)
        pl.when(last)(lambda: comm.finish(cins, couts, csems))

    res = pl.pallas_call(
        full_body, name=name, out_shape=list(out_shape) + comm.out_shape, grid=grid,
        in_specs=list(in_specs) + [_ANY] * nc, out_specs=list(out_specs) + [_ANY] * nc,
        scratch_shapes=list(scratch) + comm.scratch,
        compiler_params=pltpu.CompilerParams(dimension_semantics=("arbitrary",) * len(grid),
                                             vmem_limit_bytes=VMEM_LIMIT))(*args, *comm.arrays)
    return res[:no], res[no:]


def _shift_right(p, chip, col):
    if chip == 0:
        return p
    if chip < 3:
        return pltpu.roll(p, 8 * chip, 1)
    tail0 = DT_STORED_START + SSM_HEADS - 3 * SHARD_W
    head = pltpu.roll(p, 24, 1)
    tail = pltpu.roll(p, 24 + DT_PAD, 1)
    return jnp.where(col < tail0 + 24, head, jnp.where(col >= tail0 + 24 + DT_PAD, tail, 0.0))


def _shift_left(g, chip, col):
    if chip == 0:
        return g
    if chip < 3:
        return pltpu.roll(g, PACK_W - 8 * chip, 1)
    tail0 = DT_STORED_START + SSM_HEADS - 3 * SHARD_W
    return jnp.where(col < tail0, pltpu.roll(g, PACK_W - 24, 1), pltpu.roll(g, PACK_W - 24 - DT_PAD, 1))


def pack_w_in(w):
    rows = BLOCK

    def body(w_ref, o_ref, pad_ref):
        chip = _chip_index()
        pad_ref[...] = jnp.zeros_like(pad_ref)
        pad_ref[:, 0:SHARD_W] = w_ref[...]
        p = pad_ref[...]
        col = lax.broadcasted_iota(jnp.int32, p.shape, 1)
        for cv in range(4):
            @pl.when(chip == cv)
            def _():
                o_ref[...] = _shift_right(p, cv, col).astype(BF16)

    return _call(body, name="pack_w_in", grid=(D_MODEL // rows,),
                 in_specs=[pl.BlockSpec((rows, SHARD_W), lambda i: (i, 0))],
                 out_specs=pl.BlockSpec((rows, PACK_W), lambda i: (i, 0)),
                 out_shape=jax.ShapeDtypeStruct((D_MODEL, PACK_W), BF16),
                 scratch=[pltpu.VMEM((rows, PACK_W), F32)], sem=("parallel",))(w)


def _tile_runs():
    runs, fix = [], []
    for t in range(N_ALIGNED // LANES):
        s = min(t // 19, 3)
        j = t - 19 * s
        p = _act_col(t * LANES)
        if runs and runs[-1][1] == s and runs[-1][0] + runs[-1][3] == p and runs[-1][2] + runs[-1][3] == j * LANES:
            runs[-1][3] += LANES
        else:
            runs.append([p, s, j * LANES, LANES])
        if j == 0 and s > 0:
            fix.append((p, s - 1))
    return runs, fix


def unpack_w_in(bg):
    rows = BLOCK
    runs, fix = _tile_runs()

    def body(b_ref, o_ref):
        for p, s, j, w in runs:
            o_ref[:, p:p + w] = b_ref[s, :, j:j + w]
        for p, s in fix:
            o_ref[:, p:p + LANES] = o_ref[:, p:p + LANES] + b_ref[s, :, SHARD_STRIDE:PACK_W]

    return _call(body, name="unpack_w_in", grid=(D_MODEL // rows,),
                 in_specs=[pl.BlockSpec((4, rows, PACK_W), lambda i: (0, i, 0))],
                 out_specs=pl.BlockSpec((rows, N_ALIGNED), lambda i: (i, 0)),
                 out_shape=jax.ShapeDtypeStruct((D_MODEL, N_ALIGNED), BF16), sem=("parallel",))(bg)


def pack_grad_w_in(dw):
    rows = BLOCK

    def body(g_ref, o_ref):
        for s in range(4):
            for j in range(PACK_W // LANES):
                p = _act_col((19 * s + j) * LANES)
                o_ref[s, :, j * LANES:(j + 1) * LANES] = g_ref[:, p:p + LANES].astype(BF16)

    return _call(body, name="pack_grad_w_in", grid=(D_MODEL // rows,),
                 in_specs=[pl.BlockSpec((rows, N_ALIGNED), lambda i: (i, 0))],
                 out_specs=pl.BlockSpec((4, rows, PACK_W), lambda i: (0, i, 0)),
                 out_shape=jax.ShapeDtypeStruct((4, D_MODEL, PACK_W), BF16), sem=("parallel",))(dw)


def _adamw(w, g, m, v):
    m = ADAM_B1 * m + (1.0 - ADAM_B1) * g
    v = ADAM_B2 * v + (1.0 - ADAM_B2) * jnp.square(g)
    m_hat = m / (1.0 - ADAM_B1 ** ADAM_STEP)
    v_hat = v / (1.0 - ADAM_B2 ** ADAM_STEP)
    delta = -ADAM_LR * (m_hat / (jnp.sqrt(v_hat) + ADAM_EPS) + ADAM_WD * w)
    return delta, m, v


def adamw_w_in(g_packed, w, m, v):
    rows = BLOCK

    def body(g_ref, w_ref, m_ref, v_ref, go_ref, d_ref, mo_ref, vo_ref, tmp_ref):
        chip = _chip_index()
        gp = g_ref[...]
        col = lax.broadcasted_iota(jnp.int32, gp.shape, 1)
        for cv in range(4):
            @pl.when(chip == cv)
            def _():
                tmp_ref[...] = _shift_left(gp, cv, col)
        g = tmp_ref[:, 0:SHARD_W]
        d, mn, vn = _adamw(w_ref[...], g, m_ref[...], v_ref[...])
        go_ref[...] = g
        d_ref[...] = d
        mo_ref[...] = mn
        vo_ref[...] = vn

    spec = pl.BlockSpec((rows, SHARD_W), lambda i: (i, 0))
    shp = jax.ShapeDtypeStruct((D_MODEL, SHARD_W), F32)
    return _call(body, name="adamw_w_in", grid=(D_MODEL // rows,),
                 in_specs=[pl.BlockSpec((rows, PACK_W), lambda i: (i, 0)), spec, spec, spec],
                 out_specs=[spec] * 4, out_shape=[shp] * 4,
                 scratch=[pltpu.VMEM((rows, PACK_W), F32)], sem=("parallel",))(g_packed, w, m, v)


def adamw_rows(name, g, w, m, v):
    r, c = g.shape
    rows = min(r, BLOCK)

    def body(g_ref, w_ref, m_ref, v_ref, d_ref, mo_ref, vo_ref):
        d_ref[...], mo_ref[...], vo_ref[...] = _adamw(w_ref[...], g_ref[...], m_ref[...], v_ref[...])

    spec = pl.BlockSpec((rows, c), lambda i: (i, 0))
    shp = jax.ShapeDtypeStruct((r, c), F32)
    return _call(body, name=name, grid=(r // rows,), in_specs=[spec] * 4, out_specs=[spec] * 3, out_shape=[shp] * 3,
                 sem=("parallel",))(g, w, m, v)


def adamw_small(gs, ws, ms, vs):
    n = len(gs)

    def body(*refs):
        g, w, m, v = refs[:n], refs[n:2 * n], refs[2 * n:3 * n], refs[3 * n:4 * n]
        outs = refs[4 * n:]
        for i in range(n):
            d, mn, vn = _adamw(w[i][...], g[i][...], m[i][...], v[i][...])
            outs[3 * i][...] = d
            outs[3 * i + 1][...] = mn
            outs[3 * i + 2][...] = vn

    specs = [_full(a.shape) for a in gs]
    res = _call(body, name="adamw_small", in_specs=specs * 4,
                out_specs=[s for s in specs for _ in range(3)],
                out_shape=[jax.ShapeDtypeStruct(a.shape, F32) for a in gs for _ in range(3)])(*gs, *ws, *ms, *vs)
    return [tuple(res[3 * i:3 * i + 3]) for i in range(n)]


def sum_slots(name, r):
    s, rr, c = r.shape
    rows = min(rr, BLOCK)

    def body(r_ref, o_ref):
        acc = r_ref[0].astype(F32)
        for k in range(1, s):
            acc = acc + r_ref[k].astype(F32)
        o_ref[...] = acc

    return _call(body, name=name, grid=(rr // rows,), in_specs=[pl.BlockSpec((s, rows, c), lambda i: (0, i, 0))],
                 out_specs=pl.BlockSpec((rows, c), lambda i: (i, 0)), out_shape=jax.ShapeDtypeStruct((rr, c), F32),
                 sem=("parallel",))(r)


def sum_pair(partial, from_sibling):
    s, _, rr, cols = partial.shape

    def body(p_ref, r_ref, o_ref):
        c = lax.axis_index("c")
        o_ref[...] = (p_ref[c].astype(F32) + r_ref[1 - c].astype(F32)).astype(BF16)

    return _call(body, name="sum_pair", grid=(s, rr // BLOCK),
                 in_specs=[pl.BlockSpec((None, 2, BLOCK, cols), lambda k, i: (k, 0, i, 0)),
                           pl.BlockSpec((2, None, BLOCK, cols), lambda k, i: (0, k, i, 0))],
                 out_specs=pl.BlockSpec((None, BLOCK, cols), lambda k, i: (k, i, 0)),
                 out_shape=jax.ShapeDtypeStruct((s, rr, cols), BF16), sem=("parallel", "parallel"))(partial, from_sibling)


def _col_tile(n, k):
    if n % 896 == 0 and k <= 1024:
        return 896
    return min(n, 512)


def mm_nn(name, x, w, out_dtype=F32, comm=None):
    m, k = x.shape
    n = w.shape[1]
    tn = _col_tile(n, k)
    steps = n // tn

    def body(x_ref, w_ref, o_ref):
        o_ref[...] = jnp.dot(x_ref[...].astype(BF16), w_ref[...].astype(BF16),
                             preferred_element_type=F32).astype(out_dtype)

    kw = dict(name=name, grid=(steps,), in_specs=[_full((m, k)), pl.BlockSpec((k, tn), lambda j: (0, j))])
    if comm is None:
        return _call(body, out_specs=pl.BlockSpec((m, tn), lambda j: (0, j)),
                     out_shape=jax.ShapeDtypeStruct((m, n), out_dtype), sem=("parallel",), **kw)(x, w)

    def at():
        j = pl.program_id(0)
        return j == 0, j == steps // 2, j == steps - 1

    (res,), moved = _call_with_comm(body, comm, at, (x, w), out_specs=[pl.BlockSpec((m, tn), lambda j: (0, j))],
                                    out_shape=[jax.ShapeDtypeStruct((m, n), out_dtype)], **kw)
    return res, moved


def mm_nt(name, dy, w, out_dtype=F32, comm=None):
    m, n = dy.shape
    k = w.shape[0]
    tm = m // 2
    tn = _col_tile(n, k)
    steps = n // tn

    def at():
        i, j = pl.program_id(0), pl.program_id(1)
        return (i == 0) & (j == 0), (i == 0) & (j == steps - 1), (i == m // tm - 1) & (j == steps - 1)

    def body(dy_ref, w_ref, o_ref, acc_ref):
        j = pl.program_id(1)

        @pl.when(j == 0)
        def _():
            acc_ref[...] = jnp.zeros_like(acc_ref)

        acc_ref[...] += lax.dot_general(dy_ref[...].astype(BF16), w_ref[...].astype(BF16), (((1,), (1,)), ((), ())),
                                        preferred_element_type=F32)

        @pl.when(j == steps - 1)
        def _():
            o_ref[...] = acc_ref[...].astype(out_dtype)

    kw = dict(name=name, grid=(m // tm, steps), scratch=[pltpu.VMEM((tm, k), F32)],
              in_specs=[pl.BlockSpec((tm, tn), lambda i, j: (i, j)), pl.BlockSpec((k, tn), lambda i, j: (0, j))])
    if comm is None:
        return _call(body, out_specs=pl.BlockSpec((tm, k), lambda i, j: (i, 0)),
                     out_shape=jax.ShapeDtypeStruct((m, k), out_dtype), sem=("parallel", "arbitrary"), **kw)(dy, w)
    (res,), moved = _call_with_comm(body, comm, at, (dy, w), out_specs=[pl.BlockSpec((tm, k), lambda i, j: (i, 0))],
                                    out_shape=[jax.ShapeDtypeStruct((m, k), out_dtype)], **kw)
    return res, moved


def mm_tn(name, x, dy):
    m, k = x.shape
    n = dy.shape[1]
    tm = m // 2
    tn = _col_tile(n, k)

    def body(x_ref, dy_ref, o_ref):
        @pl.when(pl.program_id(1) == 0)
        def _():
            o_ref[...] = jnp.zeros_like(o_ref)

        o_ref[...] += lax.dot_general(x_ref[...].astype(BF16), dy_ref[...].astype(BF16), (((0,), (0,)), ((), ())),
                                      preferred_element_type=F32)

    return _call(body, name=name, grid=(n // tn, m // tm),
                 in_specs=[pl.BlockSpec((tm, k), lambda i, j: (j, 0)), pl.BlockSpec((tm, tn), lambda i, j: (j, i))],
                 out_specs=pl.BlockSpec((k, tn), lambda i, j: (0, i)), out_shape=jax.ShapeDtypeStruct((k, n), F32),
                 sem=("parallel", "arbitrary"))(x, dy)


def _row_spec(width, col_block=0):
    return pl.BlockSpec((BLOCK, width), lambda i: (i, col_block))


def _x_spec():
    return pl.BlockSpec((None, BLOCK, D_MODEL), lambda i: (0, jnp.maximum(i - 1, 0), 0))


def prep(x, meta, g_pre):
    nb = x.shape[1] // BLOCK + 1

    def body(x_ref, meta_ref, g_ref, h_ref, u_ref):
        i = pl.program_id(0)

        @pl.when(i == 0)
        def _():
            h_ref[0:PAD_ROWS, :] = jnp.zeros((PAD_ROWS, D_MODEL), F32)
            h_ref[PAD_ROWS:BLOCK, :] = meta_ref[...]

        @pl.when(i > 0)
        def _():
            h_ref[...] = x_ref[...]

        u_ref[...] = _rms(h_ref[...], g_ref[...]).astype(BF16)

    return _call(body, name="prep", grid=(nb,), in_specs=[_x_spec(), _full((N_META, D_MODEL)), _full((1, D_MODEL))],
                 out_specs=[_row_spec(D_MODEL), _row_spec(D_MODEL)],
                 out_shape=[jax.ShapeDtypeStruct((nb * BLOCK, D_MODEL), F32),
                            jax.ShapeDtypeStruct((nb * BLOCK, D_MODEL), BF16)], sem=("parallel",))(x, meta, g_pre)


def prep_bwd(h, du, dres, g_pre):
    nb = h.shape[0] // BLOCK

    def body(h_ref, du_ref, dres_ref, g_ref, gx_ref, gm_ref, gg_ref):
        i = pl.program_id(0)
        _, vjp = jax.vjp(_rms, h_ref[...], g_ref[...])
        dh, dg = vjp(du_ref[...])

        @pl.when(i == 0)
        def _():
            gm_ref[...] = dh[PAD_ROWS:BLOCK, :]
            gg_ref[...] = dg

        @pl.when(i > 0)
        def _():
            gg_ref[...] += dg

        gx_ref[...] = dh + dres_ref[...]

    return _call(body, name="prep_bwd", grid=(nb,),
                 in_specs=[_row_spec(D_MODEL), _row_spec(D_MODEL), _row_spec(D_MODEL), _full((1, D_MODEL))],
                 out_specs=[_x_spec(), _full((N_META, D_MODEL)), _full((1, D_MODEL))],
                 out_shape=[jax.ShapeDtypeStruct((1, (nb - 1) * BLOCK, D_MODEL), F32),
                            jax.ShapeDtypeStruct((N_META, D_MODEL), F32), jax.ShapeDtypeStruct((1, D_MODEL), F32)],
                 sem=("arbitrary",))(h, du, dres, g_pre)


GROUP_W = SSM_INNER // SSM_GROUPS


def _gated_norm(y, z, g):
    t = y * _silu(z)
    return t * lax.rsqrt(jnp.mean(t * t, axis=-1, keepdims=True) + NORM_EPS) * g


def _gated_norm_groups(y, z, g):
    groups = [slice(k * GROUP_W, (k + 1) * GROUP_W) for k in range(SSM_GROUPS)]
    return jnp.concatenate([_gated_norm(y[:, s], z[:, s], g[:, s]) for s in groups], axis=1)


def _merge(ga, gs, ya, ys):
    return jax.nn.sigmoid(ga) * ya + jax.nn.sigmoid(gs) * ys


GATE_ATT_BLOCK = SEG["gate_att"][2] // D_MODEL
GATE_SSM_BLOCK = SEG["gate_ssm"][2] // D_MODEL


def _row_loss(out, g_post, x, target):
    diff = x + _rms(out, g_post) - target
    return 0.5 * jnp.sum(diff * diff) / D_MODEL


def tail(y_ssd, proj, a_att, x, target, woa, wos, wo, g_norm, g_post):
    nb = y_ssd.shape[0] // BLOCK
    rows = nb * BLOCK

    def body(y_ref, z_ref, ga_ref, gs_ref, a_ref, x_ref, t_ref, woa_ref, wos_ref, wo_ref, gn_ref, gp_ref,
             yn_ref, mg_ref, dout_ref, dya_ref, dys_ref, da_ref, dy_ref, dz_ref, dga_ref, dgs_ref, dres_ref,
             loss_ref, dgp_ref, dgn_ref):
        i = pl.program_id(0)
        yn, norm_vjp = jax.vjp(_gated_norm_groups, y_ref[...], z_ref[...], gn_ref[...])
        yn16 = yn.astype(BF16)
        y_ssm = jnp.dot(yn16, wos_ref[...], preferred_element_type=F32)
        y_att = jnp.dot(a_ref[...], woa_ref[...], preferred_element_type=F32)
        merged, merge_vjp = jax.vjp(_merge, ga_ref[...], gs_ref[...], y_att, y_ssm)
        merged16 = merged.astype(BF16)
        out = jnp.dot(merged16, wo_ref[...], preferred_element_type=F32)
        loss, loss_vjp = jax.vjp(_row_loss, out, gp_ref[...], x_ref[...], t_ref[...])
        counted = jnp.where(i > 0, 1.0, 0.0)
        dout, dgp, dres, _ = loss_vjp(counted)
        dout16 = dout.astype(BF16)
        dmerged = lax.dot_general(dout16, wo_ref[...], _NT, preferred_element_type=F32)
        dga, dgs, dya, dys = merge_vjp(dmerged)
        dya16, dys16 = dya.astype(BF16), dys.astype(BF16)
        dyn = lax.dot_general(dys16, wos_ref[...], _NT, preferred_element_type=F32)
        dy, dz, dgn = norm_vjp(dyn)

        yn_ref[...] = yn16
        mg_ref[...] = merged16
        dout_ref[...] = dout16
        dya_ref[...] = dya16
        dys_ref[...] = dys16
        da_ref[...] = lax.dot_general(dya16, woa_ref[...], _NT, preferred_element_type=F32)
        dy_ref[...] = dy
        dz_ref[...] = dz.astype(BF16)
        dga_ref[...] = dga.astype(BF16)
        dgs_ref[...] = dgs.astype(BF16)
        dres_ref[...] = dres

        @pl.when(i == 0)
        def _():
            loss_ref[...] = jnp.zeros_like(loss_ref)
            dgp_ref[...] = jnp.zeros_like(dgp_ref)
            dgn_ref[...] = jnp.zeros_like(dgn_ref)

        loss_ref[...] += loss * counted
        dgp_ref[...] += dgp
        dgn_ref[...] += dgn

    wide, narrow = _row_spec(SSM_INNER), _row_spec(D_MODEL)
    resident = pl.BlockSpec(memory_space=pltpu.VMEM)
    bf = lambda w: jax.ShapeDtypeStruct((rows, w), BF16)
    f32 = lambda w: jax.ShapeDtypeStruct((rows, w), F32)
    return _call(body, name="tail", grid=(nb,),
                 in_specs=[wide, wide, _row_spec(D_MODEL, GATE_ATT_BLOCK), _row_spec(D_MODEL, GATE_SSM_BLOCK), narrow,
                           _x_spec(), _x_spec(), resident, resident, resident, _full((1, SSM_INNER)),
                           _full((1, D_MODEL))],
                 out_specs=[wide, narrow, narrow, narrow, narrow, narrow, wide, wide, narrow, narrow, narrow,
                            _full((8, LANES)), _full((1, D_MODEL)), _full((1, SSM_INNER))],
                 out_shape=[bf(SSM_INNER), bf(D_MODEL), bf(D_MODEL), bf(D_MODEL), bf(D_MODEL), f32(D_MODEL),
                            f32(SSM_INNER), bf(SSM_INNER), bf(D_MODEL), bf(D_MODEL), f32(D_MODEL),
                            jax.ShapeDtypeStruct((8, LANES), F32), jax.ShapeDtypeStruct((1, D_MODEL), F32),
                            jax.ShapeDtypeStruct((1, SSM_INNER), F32)],
                 sem=("arbitrary",))(y_ssd, proj, proj, proj, a_att, x, target, woa, wos, wo, g_norm, g_post)


_NT = (((1,), (1,)), ((), ()))
ALIBI_SLOPES = tuple(2.0 ** (-8.0 * (h + 1) / ATT_Q_HEADS) for h in range(ATT_Q_HEADS))
KV_WIDTH = ATT_KV_HEADS * HEAD_DIM
Q_BLOCK = SEG["q"][2] // D_MODEL
Z_ATT_BLOCK = SEG["z_att"][2] // D_MODEL
K_BLOCK = SEG["k"][2] // KV_WIDTH
V_BLOCK = SEG["v"][2] // KV_WIDTH
META_ROW_BLOCK = PAD_ROWS // N_META


@jax.custom_vjp
def _swap_halves(x):
    return pltpu.roll(x, HEAD_DIM, 1)


_swap_halves.defvjp(lambda x: (pltpu.roll(x, HEAD_DIM, 1), None), lambda _, g: (pltpu.roll(g, HEAD_DIM, 1),))


def _both_halves(t, half):
    first = lax.broadcasted_iota(jnp.int32, t.shape, 1) < HEAD_DIM
    sw = _swap_halves(t)
    return jnp.where(first, t, sw) if half == 0 else jnp.where(first, sw, t)


def _attn_rows(q, z, kp, kc, vp, vc, km, vm, sinks, n):
    rows = ATT_GROUP * BLOCK
    i = lax.broadcasted_iota(jnp.int32, (rows, BLOCK), 0) & (BLOCK - 1)
    j = lax.broadcasted_iota(jnp.int32, (rows, BLOCK), 1)
    rel_c = (i - j).astype(F32)
    rel_p = rel_c + float(BLOCK)
    nv = jnp.zeros((rows, BLOCK), jnp.int32) + n
    ok_c = (i >= j) & (nv >= 1)
    ok_p = (j > i) & (nv >= 2)
    im = lax.broadcasted_iota(jnp.int32, (rows, N_META), 0) & (BLOCK - 1)
    jm = lax.broadcasted_iota(jnp.int32, (rows, N_META), 1)
    ok_m = ((jnp.zeros((rows, N_META), jnp.int32) + n) >= 1) | (im >= PAD_ROWS + jm)
    first = lax.broadcasted_iota(jnp.int32, (BLOCK, LANES), 1) < HEAD_DIM
    neg = -jnp.inf
    outs = []
    for kv in range(ATT_KV_HEADS):
        tile, half = divmod(kv, 2)
        lanes = slice(tile * LANES, (tile + 1) * LANES)
        kc2, kp2, km2 = (_both_halves(t[:, lanes], half).astype(BF16) for t in (kc, kp, km))
        vc2, vp2, vm2 = (_both_halves(t[:, lanes], half).astype(BF16) for t in (vc, vp, vm))
        qs, slope, sk = [], [], []
        for pair in range(ATT_GROUP // 2):
            c0 = (kv * ATT_GROUP + 2 * pair) * HEAD_DIM
            qp = q[:, c0:c0 + LANES] * HEAD_DIM ** -0.5
            qs += [jnp.where(first, qp, 0.0), jnp.where(first, 0.0, qp)]
        for g in range(ATT_GROUP):
            slope.append(jnp.full((BLOCK, 1), ALIBI_SLOPES[kv * ATT_GROUP + g], F32))
            sk.append(jnp.broadcast_to(sinks[kv * ATT_GROUP + g], (BLOCK, 1)))
        qs = jnp.concatenate(qs, axis=0).astype(BF16)
        slope = jnp.concatenate(slope, axis=0)
        sk = jnp.concatenate(sk, axis=0)
        sc = jnp.where(ok_c, lax.dot_general(qs, kc2, _NT, preferred_element_type=F32) - slope * rel_c, neg)
        sp = jnp.where(ok_p, lax.dot_general(qs, kp2, _NT, preferred_element_type=F32) - slope * rel_p, neg)
        sm = jnp.where(ok_m, lax.dot_general(qs, km2, _NT, preferred_element_type=F32), neg)
        mx = jnp.maximum(jnp.maximum(jnp.max(sc, axis=1, keepdims=True), jnp.max(sp, axis=1, keepdims=True)),
                         jnp.maximum(jnp.max(sm, axis=1, keepdims=True), sk))
        mx = lax.stop_gradient(mx)
        ec, ep, em, es = jnp.exp(sc - mx), jnp.exp(sp - mx), jnp.exp(sm - mx), jnp.exp(sk - mx)
        den = (es + jnp.sum(ec, axis=1, keepdims=True) + jnp.sum(ep, axis=1, keepdims=True)
               + jnp.sum(em, axis=1, keepdims=True))
        inv = 1.0 / den
        o = (jnp.dot((ec * inv).astype(BF16), vc2, preferred_element_type=F32)
             + jnp.dot((ep * inv).astype(BF16), vp2, preferred_element_type=F32)
             + jnp.dot((em * inv).astype(BF16), vm2, preferred_element_type=F32))
        for pair in range(ATT_GROUP // 2):
            r0 = 2 * pair * BLOCK
            outs.append(jnp.where(first, o[r0:r0 + BLOCK], o[r0 + BLOCK:r0 + 2 * BLOCK]))
    return jnp.concatenate(outs, axis=1) * _silu(z)


def _attn_specs(nb, steps_clamped):
    def blk(t):
        return jnp.minimum(t, nb - 1) if steps_clamped else t

    wide = lambda col: pl.BlockSpec((BLOCK, D_MODEL), lambda t: (blk(t), col))
    cur = lambda col: pl.BlockSpec((BLOCK, KV_WIDTH), lambda t: (blk(t), col))
    prev = lambda col: pl.BlockSpec((BLOCK, KV_WIDTH), lambda t: (jnp.maximum(blk(t) - 1, 0), col))
    meta = lambda col: pl.BlockSpec((N_META, KV_WIDTH), lambda t: (META_ROW_BLOCK, col))
    sinks = pl.BlockSpec((ATT_Q_HEADS, 1, 1), lambda t: (0, 0, 0))
    return [wide(Q_BLOCK), wide(Z_ATT_BLOCK), prev(K_BLOCK), cur(K_BLOCK), prev(V_BLOCK), cur(V_BLOCK),
            meta(K_BLOCK), meta(V_BLOCK), sinks]


def attn_fwd(proj, sinks):
    nb = proj.shape[0] // BLOCK

    def body(q_ref, z_ref, kp_ref, kc_ref, vp_ref, vc_ref, km_ref, vm_ref, sk_ref, o_ref):
        o_ref[...] = _attn_rows(q_ref[...], z_ref[...], kp_ref[...], kc_ref[...], vp_ref[...], vc_ref[...],
                                km_ref[...], vm_ref[...], tuple(sk_ref[h] for h in range(ATT_Q_HEADS)),
                                pl.program_id(0)).astype(BF16)

    return _call(body, name="attn_fwd", grid=(nb,), in_specs=_attn_specs(nb, False), out_specs=_row_spec(D_MODEL),
                 out_shape=jax.ShapeDtypeStruct((nb * BLOCK, D_MODEL), BF16), sem=("parallel",))(*([proj] * 8), sinks)


def attn_bwd(da, proj, sinks):
    nb = proj.shape[0] // BLOCK
    last = nb - 1
    wide = pl.BlockSpec((BLOCK, D_MODEL), lambda t: (jnp.minimum(t, last), 0))
    done = pl.BlockSpec((BLOCK, KV_WIDTH), lambda t: (jnp.maximum(t - 1, 0), 0))
    meta = _full((N_META, KV_WIDTH))
    par = _full((ATT_Q_HEADS, 1, 1))

    def body(da_ref, q_ref, z_ref, kp_ref, kc_ref, vp_ref, vc_ref, km_ref, vm_ref, sk_ref,
             dq_ref, dz_ref, dk_ref, dv_ref, dkm_ref, dvm_ref, dsk_ref, ck_ref, cv_ref):
        t = pl.program_id(0)

        @pl.when(t == 0)
        def _():
            ck_ref[...] = jnp.zeros_like(ck_ref)
            cv_ref[...] = jnp.zeros_like(cv_ref)
            dkm_ref[...] = jnp.zeros_like(dkm_ref)
            dvm_ref[...] = jnp.zeros_like(dvm_ref)
            dsk_ref[...] = jnp.zeros_like(dsk_ref)

        @pl.when(t < nb)
        def _():
            def f(q, z, kp, kc, vp, vc, km, vm, sk):
                return _attn_rows(q, z, kp, kc, vp, vc, km, vm, sk, t)

            _, vjp = jax.vjp(f, q_ref[...], z_ref[...], kp_ref[...], kc_ref[...], vp_ref[...], vc_ref[...],
                             km_ref[...], vm_ref[...], tuple(sk_ref[h] for h in range(ATT_Q_HEADS)))
            dq, dz, dkp, dkc, dvp, dvc, dkm, dvm, dsk = vjp(da_ref[...])
            dq_ref[...] = dq.astype(BF16)
            dz_ref[...] = dz.astype(BF16)
            for h in range(ATT_Q_HEADS):
                dsk_ref[h] += dsk[h]
            dk_ref[...] = ck_ref[...] + dkp
            dv_ref[...] = cv_ref[...] + dvp
            ck_ref[...] = dkc
            cv_ref[...] = dvc
            dkm_ref[...] += dkm
            dvm_ref[...] += dvm

        @pl.when(t == nb)
        def _():
            dk_ref[...] = ck_ref[...]
            dv_ref[...] = cv_ref[...]

    rows = nb * BLOCK
    return _call(body, name="attn_bwd", grid=(nb + 1,), in_specs=[wide] + _attn_specs(nb, True),
                 out_specs=[wide, wide, done, done, meta, meta, par],
                 out_shape=[jax.ShapeDtypeStruct((rows, D_MODEL), BF16), jax.ShapeDtypeStruct((rows, D_MODEL), BF16),
                            jax.ShapeDtypeStruct((rows, KV_WIDTH), F32), jax.ShapeDtypeStruct((rows, KV_WIDTH), F32),
                            jax.ShapeDtypeStruct((N_META, KV_WIDTH), F32), jax.ShapeDtypeStruct((N_META, KV_WIDTH), F32),
                            jax.ShapeDtypeStruct(sinks.shape, F32)],
                 scratch=[pltpu.VMEM((BLOCK, KV_WIDTH), F32), pltpu.VMEM((BLOCK, KV_WIDTH), F32)],
                 sem=("arbitrary",))(da, *([proj] * 8), sinks)


XBC_BLOCK0 = SEG["xbc"][2] // D_MODEL
CONV_COL_BLOCKS = CONV_DIM // D_MODEL
DT_TILE = SEG["dt"][2] // LANES


HALO = 8
HALOS_PER_BLOCK = BLOCK // HALO


def _shift_rows(cur, before, j):
    if j == 0:
        return cur
    n = cur.shape[0]
    row = lax.broadcasted_iota(jnp.int32, cur.shape, 0)
    head = pltpu.roll(before, j, 0)
    if n > HALO:
        head = jnp.concatenate([head, jnp.zeros((n - HALO, cur.shape[1]), cur.dtype)], axis=0)
    return jnp.where(row >= j, pltpu.roll(cur, j, 0), head)


def _conv_pre(cur, before, w_ref, b_ref):
    pre = b_ref[...] + w_ref[CONV_WIDTH - 1:CONV_WIDTH, :] * cur
    for k in range(CONV_WIDTH - 1):
        pre = pre + w_ref[k:k + 1, :] * _shift_rows(cur, before, CONV_WIDTH - 1 - k)
    return pre


def _conv_specs(nb):
    cur = pl.BlockSpec((BLOCK, D_MODEL), lambda j, i: (i, XBC_BLOCK0 + j))
    before = pl.BlockSpec((HALO, D_MODEL), lambda j, i: (jnp.maximum(i * HALOS_PER_BLOCK - 1, 0), XBC_BLOCK0 + j))
    after = pl.BlockSpec((HALO, D_MODEL),
                         lambda j, i: (jnp.minimum(i + 1, nb - 1) * HALOS_PER_BLOCK, XBC_BLOCK0 + j))
    return cur, before, after


def _valid_rows(i):
    row = lax.broadcasted_iota(jnp.int32, (BLOCK, D_MODEL), 0)
    return jnp.maximum((row >= PAD_ROWS).astype(F32), jnp.where(i > 0, 1.0, 0.0))


def conv_fwd(proj, conv_w, conv_b):
    nb = proj.shape[0] // BLOCK
    cur, before, _ = _conv_specs(nb)

    def body(c_ref, p_ref, w_ref, b_ref, o_ref):
        i = pl.program_id(1)
        pre = _conv_pre(c_ref[...], p_ref[...] * jnp.where(i > 0, 1.0, 0.0), w_ref, b_ref)
        o_ref[...] = _silu(pre) * _valid_rows(i)

    return _call(body, name="conv_fwd", grid=(CONV_COL_BLOCKS, nb),
                 in_specs=[cur, before, pl.BlockSpec((CONV_WIDTH, D_MODEL), lambda j, i: (0, j)),
                           pl.BlockSpec((1, D_MODEL), lambda j, i: (0, j))],
                 out_specs=pl.BlockSpec((BLOCK, D_MODEL), lambda j, i: (i, j)),
                 out_shape=jax.ShapeDtypeStruct((nb * BLOCK, CONV_DIM), F32),
                 sem=("parallel", "parallel"))(proj, proj, conv_w, conv_b)


def conv_bwd(dxbc, proj, conv_w, conv_b):
    nb = proj.shape[0] // BLOCK
    last = nb - 1
    cur, before, after = _conv_specs(nb)
    dcur = pl.BlockSpec((BLOCK, D_MODEL), lambda j, i: (i, j))
    dafter = pl.BlockSpec((HALO, D_MODEL), lambda j, i: (jnp.minimum(i + 1, last) * HALOS_PER_BLOCK, j))
    wspec = pl.BlockSpec((CONV_WIDTH, D_MODEL), lambda j, i: (0, j))
    bspec = pl.BlockSpec((1, D_MODEL), lambda j, i: (0, j))

    def body(dc_ref, da_ref, c_ref, p_ref, a_ref, w_ref, b_ref, du_ref, dw_ref, db_ref):
        i = pl.program_id(1)
        row = lax.broadcasted_iota(jnp.int32, (BLOCK, D_MODEL), 0)
        curv = c_ref[...]
        beforev = p_ref[...] * jnp.where(i > 0, 1.0, 0.0)

        def dpre_of(pre, d):
            s = jax.nn.sigmoid(pre)
            return d * (s * (1.0 + pre * (1.0 - s)))

        dp_c = dpre_of(_conv_pre(curv, beforev, w_ref, b_ref), dc_ref[...] * _valid_rows(i))
        dp_a = dpre_of(_conv_pre(a_ref[...], curv[BLOCK - HALO:], w_ref, b_ref),
                       da_ref[...] * jnp.where(i < last, 1.0, 0.0))
        du = w_ref[CONV_WIDTH - 1:CONV_WIDTH, :] * dp_c
        for j in range(1, CONV_WIDTH):
            tail = jnp.concatenate([jnp.zeros((BLOCK - HALO, D_MODEL), F32), pltpu.roll(dp_a, HALO - j, 0)], axis=0)
            up = jnp.where(row < BLOCK - j, pltpu.roll(dp_c, BLOCK - j, 0), tail)
            du = du + w_ref[CONV_WIDTH - 1 - j:CONV_WIDTH - j, :] * up
        du_ref[...] = du.astype(BF16)

        @pl.when(i == 0)
        def _():
            dw_ref[...] = jnp.zeros_like(dw_ref)
            db_ref[...] = jnp.zeros_like(db_ref)

        for k in range(CONV_WIDTH):
            dw_ref[k:k + 1, :] += jnp.sum(dp_c * _shift_rows(curv, beforev, CONV_WIDTH - 1 - k), axis=0, keepdims=True)
        db_ref[...] += jnp.sum(dp_c, axis=0, keepdims=True)

    return _call(body, name="conv_bwd", grid=(CONV_COL_BLOCKS, nb),
                 in_specs=[dcur, dafter, cur, before, after, wspec, bspec], out_specs=[dcur, wspec, bspec],
                 out_shape=[jax.ShapeDtypeStruct((nb * BLOCK, CONV_DIM), BF16),
                            jax.ShapeDtypeStruct((CONV_WIDTH, CONV_DIM), F32), jax.ShapeDtypeStruct((1, CONV_DIM), F32)],
                 sem=("parallel", "arbitrary"))(dxbc, dxbc, proj, proj, proj, conv_w, conv_b)


def _head_expand():
    e = np.zeros((LANES, SSM_INNER), np.float32)
    for h in range(SSM_HEADS):
        e[h, h * HEAD_DIM:(h + 1) * HEAD_DIM] = 1.0
    return jnp.asarray(e)


def _softplus(x):
    return jnp.maximum(x, 0.0) + jnp.log(1.0 + jnp.exp(-jnp.abs(x)))


def _ssd_group(xs, dt_tile, expand, bias, bg, cg, alog, dsk, state):
    l = lax.broadcasted_iota(jnp.int32, (BLOCK, BLOCK), 0)
    s = lax.broadcasted_iota(jnp.int32, (BLOCK, BLOCK), 1)
    causal = l >= s
    first_head = s < HEAD_DIM
    dtx = _softplus(jnp.dot(dt_tile, expand, precision=HI, preferred_element_type=F32) + bias)
    a = dtx * (-jnp.exp(alog))
    cs = jnp.dot(causal.astype(F32), a, precision=HI, preferred_element_type=F32)
    tot = jnp.sum(a, axis=0, keepdims=True)
    bb, cb16 = bg.astype(BF16), cg.astype(BF16)
    cb = lax.dot_general(cb16, bb, _NT, preferred_element_type=F32)
    xr = xs * dtx
    y_diag = []
    for p in range(GROUP_W // LANES):
        lanes = slice(p * LANES, (p + 1) * LANES)
        c_pair = cs[:, lanes]
        c_swap = _swap_halves(c_pair)
        m = []
        for c_head in (jnp.where(first_head, c_pair, c_swap), jnp.where(first_head, c_swap, c_pair)):
            m.append(cb * jnp.exp(jnp.where(causal, c_head - c_head.T, -jnp.inf)))
        x_pair = xr[:, lanes]
        x_diag = jnp.concatenate([jnp.where(first_head, x_pair, 0.0), jnp.where(first_head, 0.0, x_pair)], axis=0)
        y_diag.append(jnp.dot(jnp.concatenate(m, axis=1).astype(BF16), x_diag.astype(BF16),
                              preferred_element_type=F32))
    st = lax.dot_general(bb, (xr * jnp.exp(tot - cs)).astype(BF16), (((0,), (0,)), ((), ())),
                         preferred_element_type=F32)
    new_state = state * jnp.exp(tot) + st
    y_off = jnp.dot(cb16, state.astype(BF16), preferred_element_type=F32) * jnp.exp(cs)
    return jnp.concatenate(y_diag, axis=1) + y_off + dsk * xs, new_state


B_TILE0 = SSM_INNER // LANES
C_TILE0 = B_TILE0 + SSM_GROUPS


def _ssd_specs(chunk):
    xs = pl.BlockSpec((BLOCK, GROUP_W), lambda c, g: (chunk(c), g))
    dt = pl.BlockSpec((BLOCK, LANES), lambda c, g: (chunk(c), DT_TILE))
    expand = pl.BlockSpec((LANES, GROUP_W), lambda c, g: (0, g))
    b = pl.BlockSpec((BLOCK, SSM_STATE), lambda c, g: (chunk(c), B_TILE0 + g))
    cc = pl.BlockSpec((BLOCK, SSM_STATE), lambda c, g: (chunk(c), C_TILE0 + g))
    par = pl.BlockSpec((1, GROUP_W), lambda c, g: (0, g))
    state = pl.BlockSpec((None, SSM_STATE, GROUP_W), lambda c, g: (chunk(c), 0, g))
    return xs, dt, expand, b, cc, par, state


def ssd_fwd(xbc, proj, expand, bias_x, alog_x, dsk_x):
    nb = xbc.shape[0] // BLOCK
    xs, dt, ex, b, cc, par, state = _ssd_specs(lambda c: c)

    def body(x_ref, dt_ref, e_ref, bi_ref, b_ref, c_ref, al_ref, dk_ref, y_ref, sp_ref, st_ref):
        g = pl.program_id(1)

        @pl.when(pl.program_id(0) == 0)
        def _():
            st_ref[g] = jnp.zeros((SSM_STATE, GROUP_W), F32)

        entering = st_ref[g]
        sp_ref[...] = entering
        y_ref[...], st_ref[g] = _ssd_group(x_ref[...], dt_ref[...], e_ref[...], bi_ref[...], b_ref[...], c_ref[...],
                                           al_ref[...], dk_ref[...], entering)

    return _call(body, name="ssd_fwd", grid=(nb, SSM_GROUPS), in_specs=[xs, dt, ex, par, b, cc, par, par],
                 out_specs=[xs, state],
                 out_shape=[jax.ShapeDtypeStruct((nb * BLOCK, SSM_INNER), F32),
                            jax.ShapeDtypeStruct((nb, SSM_STATE, SSM_INNER), F32)],
                 scratch=[pltpu.VMEM((SSM_GROUPS, SSM_STATE, GROUP_W), F32)],
                 sem=("arbitrary", "arbitrary"))(xbc, proj, expand, bias_x, xbc, xbc, alog_x, dsk_x)


def ssd_bwd(dy, xbc, proj, expand, bias_x, alog_x, dsk_x, states, comm):
    nb = xbc.shape[0] // BLOCK
    last = nb - 1
    xs, dt, ex, b, cc, par, state = _ssd_specs(lambda c: last - c)
    tile = pl.BlockSpec((BLOCK, LANES), lambda c, g: (last - c, 0))
    nspec = pl.BlockSpec((BLOCK, SSM_STATE), lambda c, g: (last - c, g))
    acc = _full((SSM_GROUPS, 1, GROUP_W))

    def body(dy_ref, x_ref, dt_ref, e_ref, bi_ref, b_ref, c_ref, al_ref, dk_ref, sp_ref,
             dx_ref, ddt_ref, db_ref, dc_ref, dbi_ref, dal_ref, ddk_ref, ds_ref, ddt_acc):
        c, g = pl.program_id(0), pl.program_id(1)

        @pl.when(c == 0)
        def _():
            ds_ref[g] = jnp.zeros((SSM_STATE, GROUP_W), F32)
            dbi_ref[g] = jnp.zeros((1, GROUP_W), F32)
            dal_ref[g] = jnp.zeros((1, GROUP_W), F32)
            ddk_ref[g] = jnp.zeros((1, GROUP_W), F32)

        expand_rows = e_ref[...]

        def f(xs, dt_tile, bias, bg, cg, alog, dsk, state):
            return _ssd_group(xs, dt_tile, expand_rows, bias, bg, cg, alog, dsk, state)

        _, vjp = jax.vjp(f, x_ref[...], dt_ref[...], bi_ref[...], b_ref[...], c_ref[...], al_ref[...], dk_ref[...],
                         sp_ref[...])
        dx_ref[...], ddt, dbi, db_ref[...], dc_ref[...], dal, ddk, ds_ref[g] = vjp((dy_ref[...], ds_ref[g]))
        dbi_ref[g] += dbi
        dal_ref[g] += dal
        ddk_ref[g] += ddk

        @pl.when(g == 0)
        def _():
            ddt_acc[...] = ddt

        @pl.when(g > 0)
        def _():
            ddt_acc[...] += ddt

        @pl.when(g == SSM_GROUPS - 1)
        def _():
            ddt_ref[...] = ddt_acc[...].astype(BF16)

    def at():
        c, g = pl.program_id(0), pl.program_id(1)
        return (c == 0) & (g == 0), (c == 0) & (g == 1), (c == last) & (g == SSM_GROUPS - 1)

    par_shape = jax.ShapeDtypeStruct((SSM_GROUPS, 1, GROUP_W), F32)
    return _call_with_comm(
        body, comm, at, (dy, xbc, proj, expand, bias_x, xbc, xbc, alog_x, dsk_x, states), name="ssd_bwd",
        grid=(nb, SSM_GROUPS), in_specs=[xs, xs, dt, ex, par, b, cc, par, par, state],
        out_specs=[xs, tile, nspec, nspec, acc, acc, acc],
        out_shape=[jax.ShapeDtypeStruct((nb * BLOCK, SSM_INNER), F32), jax.ShapeDtypeStruct((nb * BLOCK, LANES), BF16),
                   jax.ShapeDtypeStruct((nb * BLOCK, SSM_GROUPS * SSM_STATE), F32),
                   jax.ShapeDtypeStruct((nb * BLOCK, SSM_GROUPS * SSM_STATE), F32), par_shape, par_shape, par_shape],
        scratch=[pltpu.VMEM((SSM_GROUPS, SSM_STATE, GROUP_W), F32), pltpu.VMEM((BLOCK, LANES), F32)])


SLAB_ROWS = 24
SLAB_META_ROW = 8


def pack_small(dcw, dcb, dgpre, dgpost, dbias_x, dalog_x, ddsk_x, dsinks, dgn, dmeta, expand):
    def body(cw, cb, gpre, gpost, dtb, al, dk, sk, gn, meta, e_ref, o_ref):
        def per_head(v):
            rows = jnp.broadcast_to(v[...], (8, SSM_INNER))
            return lax.dot_general(rows, e_ref[...], _NT, precision=HI, preferred_element_type=F32)[0:1]

        o_ref[...] = jnp.zeros_like(o_ref)
        o_ref[0:CONV_WIDTH, :] = cw[...]
        o_ref[4:5, :] = cb[...]
        o_ref[5:6, 0:1024] = gpre[...]
        o_ref[5:6, 1024:2048] = gpost[...]
        o_ref[5:6, 2048:2176] = per_head(dtb)
        o_ref[5:6, 2176:2304] = per_head(al)
        o_ref[5:6, 2304:2432] = per_head(dk)
        o_ref[5:6, 2432:2560] = sk[...]
        o_ref[6:7, 0:SSM_INNER] = gn[...]
        o_ref[SLAB_META_ROW:SLAB_META_ROW + N_META, 0:D_MODEL] = meta[...]

    args = (dcw, dcb, dgpre, dgpost, dbias_x, dalog_x, ddsk_x, dsinks, dgn, dmeta, expand)
    return _call(body, name="pack_small", in_specs=[_full(a.shape) for a in args],
                 out_specs=_full((SLAB_ROWS, CONV_DIM)), out_shape=jax.ShapeDtypeStruct((SLAB_ROWS, CONV_DIM), F32))(*args)


def _lane_tile(v):
    return jnp.pad(v, ((0, 0), (0, LANES - v.shape[1])))


def kernel(x, meta_tokens, g_pre, w_in, conv_w, conv_b, dt_bias, a_log, d_skip, attn_sinks, g_ssm_norm, w_out_att, w_out_ssm, w_out, g_post, loss_target, m_meta_tokens, m_g_pre, m_w_in, m_conv_w, m_conv_b, m_dt_bias, m_a_log, m_d_skip, m_attn_sinks, m_g_ssm_norm, m_w_out_att, m_w_out_ssm, m_w_out, m_g_post, v_meta_tokens, v_g_pre, v_w_in, v_conv_w, v_conv_b, v_dt_bias, v_a_log, v_d_skip, v_attn_sinks, v_g_ssm_norm, v_w_out_att, v_w_out_ssm, v_w_out, v_g_post):
    chip = _chip_index()

    conv_w_rows = jnp.pad(conv_w[0], ((0, 2 * 8 - CONV_WIDTH), (0, 0)))
    g_w_in, g_conv_w, g_meta = run_comm("gather_w_in", TwoLevelGather([pack_w_in(w_in[0]), conv_w_rows, meta_tokens]))
    w_all = unpack_w_in(g_w_in)
    cw_full = g_conv_w[:, :CONV_WIDTH].transpose(1, 0, 2).reshape(CONV_WIDTH, CONV_DIM)
    meta_full = g_meta.transpose(1, 0, 2).reshape(N_META, D_MODEL)

    h, u = prep(x, meta_full, g_pre)
    proj, g_w_out = mm_nn("in_proj", u, w_all, comm=TwoLevelGather(
        [w_out_att[0].astype(BF16), w_out_ssm[0].astype(BF16), w_out[0].astype(BF16)]))
    woa = g_w_out[0].reshape(D_MODEL, D_MODEL)
    wos = g_w_out[1].reshape(SSM_INNER, D_MODEL)
    wo = g_w_out[2].reshape(D_MODEL, D_MODEL)

    sinks3 = attn_sinks.reshape(ATT_Q_HEADS, 1, 1)
    a_att = attn_fwd(proj, sinks3)

    xbc = conv_fwd(proj, cw_full, conv_b)
    expand = _head_expand()
    bias_x = jnp.repeat(dt_bias, HEAD_DIM, axis=1)
    alog_x = jnp.repeat(a_log, HEAD_DIM, axis=1)
    dsk_x = jnp.repeat(d_skip, HEAD_DIM, axis=1)
    y_ssd, states = ssd_fwd(xbc, proj, expand, bias_x, alog_x, dsk_x)

    (yn, merged, dout, dy_att, dy_ssm, da_att, dy_ssd, dz_ssm, dga, dgs, dres, loss_tile, dg_post, dgn) = tail(
        y_ssd, proj, a_att, x, loss_target, woa, wos, wo, g_ssm_norm, g_post)
    loss = lax.psum(loss_tile[0, 0], ("x", "y", "c"))

    dwo = mm_tn("out_proj_dw", merged, dout)
    dwoa = mm_tn("att_out_dw", a_att, dy_att)
    dwos = mm_tn("ssm_out_dw", yn, dy_ssm)
    dq, dz_att, dk, dv, dkmeta, dvmeta, dsinks3 = attn_bwd(da_att, proj, sinks3)
    dk = dk.at[PAD_ROWS:BLOCK].add(dkmeta).astype(BF16)
    dv = dv.at[PAD_ROWS:BLOCK].add(dvmeta).astype(BF16)

    def pieces(g):
        return g.astype(BF16).reshape(4, 2, g.shape[0] // 8, g.shape[1])

    def to_owner(g):
        return (lambda ref, dev: ref.at[_chip_of(dev), dev[2]], (g.shape[0] // 8, g.shape[1]))

    (dxs, ddt_tile, dbg, dcg, dbias_x, dalog_x, ddsk_x), sent_w_out = ssd_bwd(
        dy_ssd, xbc, proj, expand, bias_x, alog_x, dsk_x, states,
        DirectExchange([pieces(dwoa), pieces(dwos), pieces(dwo)], [to_owner(dwoa), to_owner(dwos), to_owner(dwo)],
                       ALL_MASKS, "dev", 8))
    dxbc = jnp.concatenate([dxs, dbg, dcg], axis=1)
    dxbc_raw, dcw, dcb = conv_bwd(dxbc, proj, cw_full, conv_b)

    dproj = jnp.concatenate([dz_ssm, dxbc_raw, dq, dz_att, dga, dgs, dk, dv, ddt_tile], axis=1)
    dw_all = mm_tn("in_proj_dw", u, dproj)

    half_rows = D_MODEL // 2
    partial = pack_grad_w_in(dw_all).reshape(4, 2, half_rows, PACK_W)
    from_sibling, = run_comm("pair_grads", DirectExchange(
        [partial], [(lambda ref, dev: ref.at[pl.ds(0, 4), dev[2]], (4, half_rows, PACK_W))], SIBLING_MASK, "core", 2,
        keep_own=False))
    chip_sum = sum_pair(partial, from_sibling)
    du, (sent_w_in,) = mm_nt("in_proj_dx", dproj, w_all, comm=DirectExchange(
        [chip_sum], [(lambda ref, dev: ref.at[_chip_of(dev)], (half_rows, PACK_W))], CHIP_MASKS, "chip", 4))
    grad_x, dmeta, dg_pre = prep_bwd(h, du, dres, g_pre)

    per_lane = [v.reshape(1, SSM_INNER) for v in (dbias_x, dalog_x, ddsk_x)]
    slab = pack_small(dcw, dcb, dg_pre, dg_post, *per_lane, _lane_tile(dsinks3.reshape(1, ATT_Q_HEADS)), dgn, dmeta,
                      expand)
    halves = [sum_slots("sum_" + nm, r)
              for nm, r in zip(("w_in", "w_out_att", "w_out_ssm", "w_out"), [sent_w_in] + list(sent_w_out))]
    shared = run_comm("share_grads", Both(DirectExchange(halves, [None] * 4, SIBLING_MASK, "core", 2),
                                          DirectExchange([slab], [None], ALL_MASKS, "dev", 8)))
    g_w_in_packed, g_woa, g_wos, g_wo = [f.reshape(2 * f.shape[1], f.shape[2]) for f in shared[:4]]
    small = sum_slots("sum_small", shared[4])

    g_w_in, d_w_in, nm_w_in, nv_w_in = adamw_w_in(g_w_in_packed, w_in[0], m_w_in[0], v_w_in[0])
    d_woa, nm_woa, nv_woa = adamw_rows("adamw_w_out_att", g_woa, w_out_att[0], m_w_out_att[0], v_w_out_att[0])
    d_wos, nm_wos, nv_wos = adamw_rows("adamw_w_out_ssm", g_wos, w_out_ssm[0], m_w_out_ssm[0], v_w_out_ssm[0])
    d_wo, nm_wo, nv_wo = adamw_rows("adamw_w_out", g_wo, w_out[0], m_w_out[0], v_w_out[0])

    cw_cols = CONV_DIM // 4
    meta_cols = D_MODEL // 4
    g_small = {
        "meta_tokens": lax.dynamic_slice(small, (SLAB_META_ROW, chip * meta_cols), (N_META, meta_cols)),
        "g_pre": small[5:6, 0:1024],
        "conv_w": lax.dynamic_slice(small, (0, chip * cw_cols), (CONV_WIDTH, cw_cols)),
        "conv_b": small[4:5, :],
        "dt_bias": small[5:6, 2048:2048 + SSM_HEADS],
        "a_log": small[5:6, 2176:2176 + SSM_HEADS],
        "d_skip": small[5:6, 2304:2304 + SSM_HEADS],
        "attn_sinks": small[5:6, 2432:2432 + ATT_Q_HEADS],
        "g_ssm_norm": small[6:7, 0:SSM_INNER],
        "g_post": small[5:6, 1024:2048],
    }
    names = list(g_small)
    w_small = dict(meta_tokens=meta_tokens, g_pre=g_pre, conv_w=conv_w[0], conv_b=conv_b, dt_bias=dt_bias, a_log=a_log,
                   d_skip=d_skip, attn_sinks=attn_sinks, g_ssm_norm=g_ssm_norm, g_post=g_post)
    m_small = dict(meta_tokens=m_meta_tokens, g_pre=m_g_pre, conv_w=m_conv_w[0], conv_b=m_conv_b, dt_bias=m_dt_bias,
                   a_log=m_a_log, d_skip=m_d_skip, attn_sinks=m_attn_sinks, g_ssm_norm=m_g_ssm_norm, g_post=m_g_post)
    v_small = dict(meta_tokens=v_meta_tokens, g_pre=v_g_pre, conv_w=v_conv_w[0], conv_b=v_conv_b, dt_bias=v_dt_bias,
                   a_log=v_a_log, d_skip=v_d_skip, attn_sinks=v_attn_sinks, g_ssm_norm=v_g_ssm_norm, g_post=v_g_post)
    upd = dict(zip(names, adamw_small([g_small[k] for k in names], [w_small[k] for k in names],
                                      [m_small[k] for k in names], [v_small[k] for k in names])))

    lead = {"conv_w"}

    def shaped(name, a):
        return a[None] if name in lead else a

    grads = dict(g_small, w_in=g_w_in, w_out_att=g_woa, w_out_ssm=g_wos, w_out=g_wo)
    deltas = dict({k: upd[k][0] for k in names}, w_in=d_w_in, w_out_att=d_woa, w_out_ssm=d_wos, w_out=d_wo)
    new_m = dict({k: upd[k][1] for k in names}, w_in=nm_w_in, w_out_att=nm_woa, w_out_ssm=nm_wos, w_out=nm_wo)
    new_v = dict({k: upd[k][2] for k in names}, w_in=nv_w_in, w_out_att=nv_woa, w_out_ssm=nv_wos, w_out=nv_wo)
    lead |= {"w_in", "w_out_att", "w_out_ssm", "w_out"}
    order = ["meta_tokens", "g_pre", "w_in", "conv_w", "conv_b", "dt_bias", "a_log", "d_skip", "attn_sinks",
             "g_ssm_norm", "w_out_att", "w_out_ssm", "w_out", "g_post"]
    outs = [loss, grad_x]
    for group in (grads, deltas, new_m, new_v):
        outs += [shaped(k, group[k]) for k in order]
    return tuple(outs)
```

```python
import functools

import numpy as np
import jax
import jax.numpy as jnp
from jax import lax
from jax.experimental import pallas as pl
from jax.experimental.pallas import tpu as pltpu

F32 = jnp.float32
BF16 = jnp.bfloat16
HI = lax.Precision.HIGHEST

D_MODEL = 1024
N_META = 16
BLOCK = 128
PAD_ROWS = BLOCK - N_META
NORM_EPS = 1e-6
HEAD_DIM = 64
ATT_Q_HEADS = 16
ATT_KV_HEADS = 4
ATT_GROUP = 4
SSM_INNER = 2048
SSM_HEADS = 32
SSM_GROUPS = 4
SSM_HEADS_PER_GROUP = 8
SSM_STATE = 128
CONV_WIDTH = 4
CONV_DIM = 3072
LANES = 128

ADAM_LR = 0.001
ADAM_B1 = 0.9
ADAM_B2 = 0.999
ADAM_EPS = 1e-08
ADAM_WD = 0.01
ADAM_STEP = 10

VMEM_LIMIT = 48 * 1024 * 1024

SHARD_W = 2440
PACK_W = 2560
SHARD_STRIDE = 2432
N_ALIGNED = 9856
SEG = {
    "q": (0, 1024, 5120), "k": (1024, 256, 9216), "v": (1280, 256, 9472), "z_att": (1536, 1024, 6144),
    "z_ssm": (2560, 2048, 0), "xbc": (4608, 3072, 2048), "dt": (7680, 128, 9728),
    "gate_att": (7808, 1024, 7168), "gate_ssm": (8832, 1024, 8192),
}
DT_STORED_START = 7680
DT_PAD = LANES - SSM_HEADS


def _act_col(aligned_col):
    for a0, w, p0 in SEG.values():
        if a0 <= aligned_col < a0 + w:
            return p0 + aligned_col - a0
    raise ValueError(aligned_col)


def _call(body, *, name, out_shape, in_specs, out_specs, grid=(), scratch=(), sem=None, aliases=None):
    return pl.pallas_call(
        body, out_shape=out_shape, grid=grid, in_specs=in_specs, out_specs=out_specs, scratch_shapes=list(scratch),
        name=name, input_output_aliases=aliases or {},
        compiler_params=pltpu.CompilerParams(dimension_semantics=sem, vmem_limit_bytes=VMEM_LIMIT))


def _full(shape):
    n = len(shape)
    return pl.BlockSpec(shape, lambda *_: (0,) * n)


def _chip_index():
    return lax.axis_index("x") * 2 + lax.axis_index("y")


def _silu(z):
    return z * jax.nn.sigmoid(z)


def _rms(x, g):
    return x * lax.rsqrt(jnp.mean(x * x, axis=-1, keepdims=True) + NORM_EPS) * g


def _peer(mask):
    x, y, c = lax.axis_index("x"), lax.axis_index("y"), lax.axis_index("c")
    return ((1 - x) if mask & 4 else x, (1 - y) if mask & 2 else y, (1 - c) if mask & 1 else c)


def _me():
    return lax.axis_index("x"), lax.axis_index("y"), lax.axis_index("c")


def _chip_of(dev):
    return 2 * dev[0] + dev[1]


CHIP_MASKS = (4, 2, 6)
ALL_MASKS = (1, 2, 3, 4, 5, 6, 7)
SIBLING_MASK = (1,)


def _remote(src, dst, send_sem, recv_sem, dev):
    return pltpu.make_async_remote_copy(src_ref=src, dst_ref=dst, send_sem=send_sem, recv_sem=recv_sem,
                                        device_id=dev, device_id_type=pl.DeviceIdType.MESH)


class _StagedCopy:
    def __init__(self, src, stage, dst, load_sem, store_sem):
        self.load = pltpu.make_async_copy(src, stage, load_sem)
        self.store = pltpu.make_async_copy(stage, dst, store_sem)

    def start(self):
        self.load.start()
        self.load.wait()
        self.store.start()

    def wait(self):
        self.store.wait()


class DirectExchange:
    def __init__(self, arrays, pieces, masks, slot_kind, nslots, keep_own=True):
        self.arrays, self.pieces, self.masks, self.slot_kind = list(arrays), list(pieces), masks, slot_kind
        self.keep_own = keep_own
        n, nk = len(arrays), len(masks)
        shapes = [a.shape if p is None else p[1] for a, p in zip(arrays, pieces)]
        self.out_shape = [jax.ShapeDtypeStruct((nslots,) + tuple(s), a.dtype) for s, a in zip(shapes, arrays)]
        self.scratch = [pltpu.SemaphoreType.DMA((n * nk,)), pltpu.SemaphoreType.DMA((n * nk,))]
        if keep_own:
            self.scratch += [pltpu.SemaphoreType.DMA((2 * n,))] + [pltpu.VMEM(s, a.dtype) for s, a in zip(shapes, arrays)]
        self.has_mid = False

    def _copies(self, ins, outs, scratch):
        send_sems, recv_sems = scratch[:2]
        me = _me()
        slot = {"chip": _chip_of(me), "dev": 4 * me[0] + 2 * me[1] + me[2], "core": me[2]}[self.slot_kind]
        nk = len(self.masks)

        def piece(a, dev):
            return ins[a] if self.pieces[a] is None else self.pieces[a][0](ins[a], dev)

        local = []
        if self.keep_own:
            local_sems, stages = scratch[2], scratch[3:]
            local = [_StagedCopy(piece(a, me), stages[a], outs[a].at[slot], local_sems.at[2 * a], local_sems.at[2 * a + 1])
                     for a in range(len(ins))]
        remote = []
        for a in range(len(ins)):
            for ki, mask in enumerate(self.masks):
                dev = _peer(mask)
                remote.append(_remote(piece(a, dev), outs[a].at[slot], send_sems.at[a * nk + ki],
                                      recv_sems.at[a * nk + ki], dev))
        return local, remote

    def start(self, ins, outs, scratch):
        local, remote = self._copies(ins, outs, scratch)
        for cp in remote + local:
            cp.start()

    def finish(self, ins, outs, scratch):
        local, remote = self._copies(ins, outs, scratch)
        for cp in remote + local:
            cp.wait()


class TwoLevelGather:
    def __init__(self, arrays):
        self.arrays = list(arrays)
        n, nk = len(arrays), len(CHIP_MASKS)
        self.out_shape = [jax.ShapeDtypeStruct((4,) + a.shape, a.dtype) for a in arrays]
        self.scratch = ([pltpu.SemaphoreType.DMA((n * nk,)) for _ in range(4)] + [pltpu.SemaphoreType.DMA((2 * n,))]
                        + [pltpu.VMEM(a.shape, a.dtype) for a in arrays])
        self.has_mid = True

    def _copies(self, ins, outs, scratch):
        ici_send, ici_recv, fwd_send, fwd_recv, local_sems = scratch[:5]
        stages = scratch[5:]
        me = _me()
        sibling = _peer(1)
        nk = len(CHIP_MASKS)
        local, ici, fwd = [], [], []
        for a in range(len(ins)):
            half = ins[a].shape[0] // 2
            mine = pl.ds(me[2] * half, half)
            local.append(_StagedCopy(ins[a], stages[a], outs[a].at[_chip_of(me)], local_sems.at[2 * a],
                                     local_sems.at[2 * a + 1]))
            for ki, mask in enumerate(CHIP_MASKS):
                dev = _peer(mask)
                k = a * nk + ki
                ici.append(_remote(ins[a].at[mine], outs[a].at[_chip_of(me), mine], ici_send.at[k], ici_recv.at[k], dev))
                arrived = outs[a].at[_chip_of(dev), mine]
                fwd.append(_remote(arrived, arrived, fwd_send.at[k], fwd_recv.at[k], sibling))
        return local, ici, fwd

    def start(self, ins, outs, scratch):
        local, ici, _ = self._copies(ins, outs, scratch)
        for cp in ici + local:
            cp.start()

    def mid(self, ins, outs, scratch):
        _, ici, fwd = self._copies(ins, outs, scratch)
        for arrival, onward in zip(ici, fwd):
            arrival.wait_recv()
            onward.start()

    def finish(self, ins, outs, scratch):
        local, ici, fwd = self._copies(ins, outs, scratch)
        for cp in fwd:
            cp.wait_recv()
        for cp in ici + fwd:
            cp.wait_send()
        for cp in local:
            cp.wait()


class Both:
    def __init__(self, a, b):
        self.a, self.b = a, b
        self.arrays, self.out_shape = a.arrays + b.arrays, a.out_shape + b.out_shape
        self.scratch = a.scratch + b.scratch
        self.has_mid = False
        assert not (a.has_mid or b.has_mid)

    def _parts(self, ins, outs, sems):
        na, sa = len(self.a.arrays), len(self.a.scratch)
        return (ins[:na], outs[:na], sems[:sa]), (ins[na:], outs[na:], sems[sa:])

    def start(self, ins, outs, sems):
        pa, pb = self._parts(ins, outs, sems)
        self.a.start(*pa)
        self.b.start(*pb)

    def finish(self, ins, outs, sems):
        pa, pb = self._parts(ins, outs, sems)
        self.a.finish(*pa)
        self.b.finish(*pb)


_ANY = pl.BlockSpec(memory_space=pl.ANY)


def run_comm(name, comm):
    n = len(comm.arrays)

    def body(*refs):
        ins, outs, sems = refs[:n], refs[n:2 * n], refs[2 * n:]
        comm.start(ins, outs, sems)
        if comm.has_mid:
            comm.mid(ins, outs, sems)
        comm.finish(ins, outs, sems)

    return pl.pallas_call(body, name=name, out_shape=comm.out_shape, in_specs=[_ANY] * n, out_specs=[_ANY] * n,
                          scratch_shapes=comm.scratch,
                          compiler_params=pltpu.CompilerParams(vmem_limit_bytes=VMEM_LIMIT))(*comm.arrays)


def _call_with_comm(body, comm, steps, args, *, name, out_shape, in_specs, out_specs, grid, scratch=()):
    ni, no, ns, nc = len(in_specs), len(out_specs), len(scratch), len(comm.arrays)

    def full_body(*refs):
        ins, cins = refs[:ni], refs[ni:ni + nc]
        outs, couts = refs[ni + nc:ni + nc + no], refs[ni + nc + no:ni + 2 * nc + no]
        scr, csems = refs[ni + 2 * nc + no:ni + 2 * nc + no + ns], refs[ni + 2 * nc + no + ns:]
        first, middle, last = steps()
        pl.when(first)(lambda: comm.start(cins, couts, csems))
        if comm.has_mid:
            pl.when(middle)(lambda: comm.mid(cins, couts, csems))
        body(*ins, *outs, *scr)
        pl.when(last)(lambda: comm.finish(cins, couts, csems))

    res = pl.pallas_call(
        full_body, name=name, out_shape=list(out_shape) + comm.out_shape, grid=grid,
        in_specs=list(in_specs) + [_ANY] * nc, out_specs=list(out_specs) + [_ANY] * nc,
        scratch_shapes=list(scratch) + comm.scratch,
        compiler_params=pltpu.CompilerParams(dimension_semantics=("arbitrary",) * len(grid),
                                             vmem_limit_bytes=VMEM_LIMIT))(*args, *comm.arrays)
    return res[:no], res[no:]


def _shift_right(p, chip, col):
    if chip == 0:
        return p
    if chip < 3:
        return pltpu.roll(p, 8 * chip, 1)
    tail0 = DT_STORED_START + SSM_HEADS - 3 * SHARD_W
    head = pltpu.roll(p, 24, 1)
    tail = pltpu.roll(p, 24 + DT_PAD, 1)
    return jnp.where(col < tail0 + 24, head, jnp.where(col >= tail0 + 24 + DT_PAD, tail, 0.0))


def _shift_left(g, chip, col):
    if chip == 0:
        return g
    if chip < 3:
        return pltpu.roll(g, PACK_W - 8 * chip, 1)
    tail0 = DT_STORED_START + SSM_HEADS - 3 * SHARD_W
    return jnp.where(col < tail0, pltpu.roll(g, PACK_W - 24, 1), pltpu.roll(g, PACK_W - 24 - DT_PAD, 1))


def pack_w_in(w):
    rows = BLOCK

    def body(w_ref, o_ref, pad_ref):
        chip = _chip_index()
        pad_ref[...] = jnp.zeros_like(pad_ref)
        pad_ref[:, 0:SHARD_W] = w_ref[...]
        p = pad_ref[...]
        col = lax.broadcasted_iota(jnp.int32, p.shape, 1)
        for cv in range(4):
            @pl.when(chip == cv)
            def _():
                o_ref[...] = _shift_right(p, cv, col).astype(BF16)

    return _call(body, name="pack_w_in", grid=(D_MODEL // rows,),
                 in_specs=[pl.BlockSpec((rows, SHARD_W), lambda i: (i, 0))],
                 out_specs=pl.BlockSpec((rows, PACK_W), lambda i: (i, 0)),
                 out_shape=jax.ShapeDtypeStruct((D_MODEL, PACK_W), BF16),
                 scratch=[pltpu.VMEM((rows, PACK_W), F32)], sem=("parallel",))(w)


def _tile_runs():
    runs, fix = [], []
    for t in range(N_ALIGNED // LANES):
        s = min(t // 19, 3)
        j = t - 19 * s
        p = _act_col(t * LANES)
        if runs and runs[-1][1] == s and runs[-1][0] + runs[-1][3] == p and runs[-1][2] + runs[-1][3] == j * LANES:
            runs[-1][3] += LANES
        else:
            runs.append([p, s, j * LANES, LANES])
        if j == 0 and s > 0:
            fix.append((p, s - 1))
    return runs, fix


def unpack_w_in(bg):
    rows = BLOCK
    runs, fix = _tile_runs()

    def body(b_ref, o_ref):
        for p, s, j, w in runs:
            o_ref[:, p:p + w] = b_ref[s, :, j:j + w]
        for p, s in fix:
            o_ref[:, p:p + LANES] = o_ref[:, p:p + LANES] + b_ref[s, :, SHARD_STRIDE:PACK_W]

    return _call(body, name="unpack_w_in", grid=(D_MODEL // rows,),
                 in_specs=[pl.BlockSpec((4, rows, PACK_W), lambda i: (0, i, 0))],
                 out_specs=pl.BlockSpec((rows, N_ALIGNED), lambda i: (i, 0)),
                 out_shape=jax.ShapeDtypeStruct((D_MODEL, N_ALIGNED), BF16), sem=("parallel",))(bg)


def pack_grad_w_in(dw):
    rows = BLOCK

    def body(g_ref, o_ref):
        for s in range(4):
            for j in range(PACK_W // LANES):
                p = _act_col((19 * s + j) * LANES)
                o_ref[s, :, j * LANES:(j + 1) * LANES] = g_ref[:, p:p + LANES].astype(BF16)

    return _call(body, name="pack_grad_w_in", grid=(D_MODEL // rows,),
                 in_specs=[pl.BlockSpec((rows, N_ALIGNED), lambda i: (i, 0))],
                 out_specs=pl.BlockSpec((4, rows, PACK_W), lambda i: (0, i, 0)),
                 out_shape=jax.ShapeDtypeStruct((4, D_MODEL, PACK_W), BF16), sem=("parallel",))(dw)


def _adamw(w, g, m, v):
    m = ADAM_B1 * m + (1.0 - ADAM_B1) * g
    v = ADAM_B2 * v + (1.0 - ADAM_B2) * jnp.square(g)
    m_hat = m / (1.0 - ADAM_B1 ** ADAM_STEP)
    v_hat = v / (1.0 - ADAM_B2 ** ADAM_STEP)
    delta = -ADAM_LR * (m_hat / (jnp.sqrt(v_hat) + ADAM_EPS) + ADAM_WD * w)
    return delta, m, v


def adamw_w_in(g_packed, w, m, v):
    rows = BLOCK

    def body(g_ref, w_ref, m_ref, v_ref, go_ref, d_ref, mo_ref, vo_ref, tmp_ref):
        chip = _chip_index()
        gp = g_ref[...]
        col = lax.broadcasted_iota(jnp.int32, gp.shape, 1)
        for cv in range(4):
            @pl.when(chip == cv)
            def _():
                tmp_ref[...] = _shift_left(gp, cv, col)
        g = tmp_ref[:, 0:SHARD_W]
        d, mn, vn = _adamw(w_ref[...], g, m_ref[...], v_ref[...])
        go_ref[...] = g
        d_ref[...] = d
        mo_ref[...] = mn
        vo_ref[...] = vn

    spec = pl.BlockSpec((rows, SHARD_W), lambda i: (i, 0))
    shp = jax.ShapeDtypeStruct((D_MODEL, SHARD_W), F32)
    return _call(body, name="adamw_w_in", grid=(D_MODEL // rows,),
                 in_specs=[pl.BlockSpec((rows, PACK_W), lambda i: (i, 0)), spec, spec, spec],
                 out_specs=[spec] * 4, out_shape=[shp] * 4,
                 scratch=[pltpu.VMEM((rows, PACK_W), F32)], sem=("parallel",))(g_packed, w, m, v)


def adamw_rows(name, g, w, m, v):
    r, c = g.shape
    rows = min(r, BLOCK)

    def body(g_ref, w_ref, m_ref, v_ref, d_ref, mo_ref, vo_ref):
        d_ref[...], mo_ref[...], vo_ref[...] = _adamw(w_ref[...], g_ref[...], m_ref[...], v_ref[...])

    spec = pl.BlockSpec((rows, c), lambda i: (i, 0))
    shp = jax.ShapeDtypeStruct((r, c), F32)
    return _call(body, name=name, grid=(r // rows,), in_specs=[spec] * 4, out_specs=[spec] * 3, out_shape=[shp] * 3,
                 sem=("parallel",))(g, w, m, v)


def adamw_small(gs, ws, ms, vs):
    n = len(gs)

    def body(*refs):
        g, w, m, v = refs[:n], refs[n:2 * n], refs[2 * n:3 * n], refs[3 * n:4 * n]
        outs = refs[4 * n:]
        for i in range(n):
            d, mn, vn = _adamw(w[i][...], g[i][...], m[i][...], v[i][...])
            outs[3 * i][...] = d
            outs[3 * i + 1][...] = mn
            outs[3 * i + 2][...] = vn

    specs = [_full(a.shape) for a in gs]
    res = _call(body, name="adamw_small", in_specs=specs * 4,
                out_specs=[s for s in specs for _ in range(3)],
                out_shape=[jax.ShapeDtypeStruct(a.shape, F32) for a in gs for _ in range(3)])(*gs, *ws, *ms, *vs)
    return [tuple(res[3 * i:3 * i + 3]) for i in range(n)]


def sum_slots(name, r):
    s, rr, c = r.shape
    rows = min(rr, BLOCK)

    def body(r_ref, o_ref):
        acc = r_ref[0].astype(F32)
        for k in range(1, s):
            acc = acc + r_ref[k].astype(F32)
        o_ref[...] = acc

    return _call(body, name=name, grid=(rr // rows,), in_specs=[pl.BlockSpec((s, rows, c), lambda i: (0, i, 0))],
                 out_specs=pl.BlockSpec((rows, c), lambda i: (i, 0)), out_shape=jax.ShapeDtypeStruct((rr, c), F32),
                 sem=("parallel",))(r)


def sum_pair(partial, from_sibling):
    s, _, rr, cols = partial.shape

    def body(p_ref, r_ref, o_ref):
        c = lax.axis_index("c")
        o_ref[...] = (p_ref[c].astype(F32) + r_ref[1 - c].astype(F32)).astype(BF16)

    return _call(body, name="sum_pair", grid=(s, rr // BLOCK),
                 in_specs=[pl.BlockSpec((None, 2, BLOCK, cols), lambda k, i: (k, 0, i, 0)),
                           pl.BlockSpec((2, None, BLOCK, cols), lambda k, i: (0, k, i, 0))],
                 out_specs=pl.BlockSpec((None, BLOCK, cols), lambda k, i: (k, i, 0)),
                 out_shape=jax.ShapeDtypeStruct((s, rr, cols), BF16), sem=("parallel", "parallel"))(partial, from_sibling)


def _col_tile(n, k):
    if n % 896 == 0 and k <= 1024:
        return 896
    return min(n, 512)


def mm_nn(name, x, w, out_dtype=F32, comm=None):
    m, k = x.shape
    n = w.shape[1]
    tn = _col_tile(n, k)
    steps = n // tn

    def body(x_ref, w_ref, o_ref):
        o_ref[...] = jnp.dot(x_ref[...].astype(BF16), w_ref[...].astype(BF16),
                             preferred_element_type=F32).astype(out_dtype)

    kw = dict(name=name, grid=(steps,), in_specs=[_full((m, k)), pl.BlockSpec((k, tn), lambda j: (0, j))])
    if comm is None:
        return _call(body, out_specs=pl.BlockSpec((m, tn), lambda j: (0, j)),
                     out_shape=jax.ShapeDtypeStruct((m, n), out_dtype), sem=("parallel",), **kw)(x, w)

    def at():
        j = pl.program_id(0)
        return j == 0, j == steps // 2, j == steps - 1

    (res,), moved = _call_with_comm(body, comm, at, (x, w), out_specs=[pl.BlockSpec((m, tn), lambda j: (0, j))],
                                    out_shape=[jax.ShapeDtypeStruct((m, n), out_dtype)], **kw)
    return res, moved


def mm_nt(name, dy, w, out_dtype=F32, comm=None):
    m, n = dy.shape
    k = w.shape[0]
    tm = m // 2
    tn = _col_tile(n, k)
    steps = n // tn

    def at():
        i, j = pl.program_id(0), pl.program_id(1)
        return (i == 0) & (j == 0), (i == 0) & (j == steps - 1), (i == m // tm - 1) & (j == steps - 1)

    def body(dy_ref, w_ref, o_ref, acc_ref):
        j = pl.program_id(1)

        @pl.when(j == 0)
        def _():
            acc_ref[...] = jnp.zeros_like(acc_ref)

        acc_ref[...] += lax.dot_general(dy_ref[...].astype(BF16), w_ref[...].astype(BF16), (((1,), (1,)), ((), ())),
                                        preferred_element_type=F32)

        @pl.when(j == steps - 1)
        def _():
            o_ref[...] = acc_ref[...].astype(out_dtype)

    kw = dict(name=name, grid=(m // tm, steps), scratch=[pltpu.VMEM((tm, k), F32)],
              in_specs=[pl.BlockSpec((tm, tn), lambda i, j: (i, j)), pl.BlockSpec((k, tn), lambda i, j: (0, j))])
    if comm is None:
        return _call(body, out_specs=pl.BlockSpec((tm, k), lambda i, j: (i, 0)),
                     out_shape=jax.ShapeDtypeStruct((m, k), out_dtype), sem=("parallel", "arbitrary"), **kw)(dy, w)
    (res,), moved = _call_with_comm(body, comm, at, (dy, w), out_specs=[pl.BlockSpec((tm, k), lambda i, j: (i, 0))],
                                    out_shape=[jax.ShapeDtypeStruct((m, k), out_dtype)], **kw)
    return res, moved


def mm_tn(name, x, dy):
    m, k = x.shape
    n = dy.shape[1]
    tm = m // 2
    tn = _col_tile(n, k)

    def body(x_ref, dy_ref, o_ref):
        @pl.when(pl.program_id(1) == 0)
        def _():
            o_ref[...] = jnp.zeros_like(o_ref)

        o_ref[...] += lax.dot_general(x_ref[...].astype(BF16), dy_ref[...].astype(BF16), (((0,), (0,)), ((), ())),
                                      preferred_element_type=F32)

    return _call(body, name=name, grid=(n // tn, m // tm),
                 in_specs=[pl.BlockSpec((tm, k), lambda i, j: (j, 0)), pl.BlockSpec((tm, tn), lambda i, j: (j, i))],
                 out_specs=pl.BlockSpec((k, tn), lambda i, j: (0, i)), out_shape=jax.ShapeDtypeStruct((k, n), F32),
                 sem=("parallel", "arbitrary"))(x, dy)


def _row_spec(width, col_block=0):
    return pl.BlockSpec((BLOCK, width), lambda i: (i, col_block))


def _x_spec():
    return pl.BlockSpec((None, BLOCK, D_MODEL), lambda i: (0, jnp.maximum(i - 1, 0), 0))


def prep(x, meta, g_pre):
    nb = x.shape[1] // BLOCK + 1

    def body(x_ref, meta_ref, g_ref, h_ref, u_ref):
        i = pl.program_id(0)

        @pl.when(i == 0)
        def _():
            h_ref[0:PAD_ROWS, :] = jnp.zeros((PAD_ROWS, D_MODEL), F32)
            h_ref[PAD_ROWS:BLOCK, :] = meta_ref[...]

        @pl.when(i > 0)
        def _():
            h_ref[...] = x_ref[...]

        u_ref[...] = _rms(h_ref[...], g_ref[...]).astype(BF16)

    return _call(body, name="prep", grid=(nb,), in_specs=[_x_spec(), _full((N_META, D_MODEL)), _full((1, D_MODEL))],
                 out_specs=[_row_spec(D_MODEL), _row_spec(D_MODEL)],
                 out_shape=[jax.ShapeDtypeStruct((nb * BLOCK, D_MODEL), F32),
                            jax.ShapeDtypeStruct((nb * BLOCK, D_MODEL), BF16)], sem=("parallel",))(x, meta, g_pre)


def prep_bwd(h, du, dres, g_pre):
    nb = h.shape[0] // BLOCK

    def body(h_ref, du_ref, dres_ref, g_ref, gx_ref, gm_ref, gg_ref):
        i = pl.program_id(0)
        _, vjp = jax.vjp(_rms, h_ref[...], g_ref[...])
        dh, dg = vjp(du_ref[...])

        @pl.when(i == 0)
        def _():
            gm_ref[...] = dh[PAD_ROWS:BLOCK, :]
            gg_ref[...] = dg

        @pl.when(i > 0)
        def _():
            gg_ref[...] += dg

        gx_ref[...] = dh + dres_ref[...]

    return _call(body, name="prep_bwd", grid=(nb,),
                 in_specs=[_row_spec(D_MODEL), _row_spec(D_MODEL), _row_spec(D_MODEL), _full((1, D_MODEL))],
                 out_specs=[_x_spec(), _full((N_META, D_MODEL)), _full((1, D_MODEL))],
                 out_shape=[jax.ShapeDtypeStruct((1, (nb - 1) * BLOCK, D_MODEL), F32),
                            jax.ShapeDtypeStruct((N_META, D_MODEL), F32), jax.ShapeDtypeStruct((1, D_MODEL), F32)],
                 sem=("arbitrary",))(h, du, dres, g_pre)


GROUP_W = SSM_INNER // SSM_GROUPS


def _gated_norm(y, z, g):
    t = y * _silu(z)
    return t * lax.rsqrt(jnp.mean(t * t, axis=-1, keepdims=True) + NORM_EPS) * g


def _gated_norm_groups(y, z, g):
    groups = [slice(k * GROUP_W, (k + 1) * GROUP_W) for k in range(SSM_GROUPS)]
    return jnp.concatenate([_gated_norm(y[:, s], z[:, s], g[:, s]) for s in groups], axis=1)


def _merge(ga, gs, ya, ys):
    return jax.nn.sigmoid(ga) * ya + jax.nn.sigmoid(gs) * ys


GATE_ATT_BLOCK = SEG["gate_att"][2] // D_MODEL
GATE_SSM_BLOCK = SEG["gate_ssm"][2] // D_MODEL


def _row_loss(out, g_post, x, target):
    diff = x + _rms(out, g_post) - target
    return 0.5 * jnp.sum(diff * diff) / D_MODEL


def tail(y_ssd, proj, a_att, x, target, woa, wos, wo, g_norm, g_post):
    nb = y_ssd.shape[0] // BLOCK
    rows = nb * BLOCK

    def body(y_ref, z_ref, ga_ref, gs_ref, a_ref, x_ref, t_ref, woa_ref, wos_ref, wo_ref, gn_ref, gp_ref,
             yn_ref, mg_ref, dout_ref, dya_ref, dys_ref, da_ref, dy_ref, dz_ref, dga_ref, dgs_ref, dres_ref,
             loss_ref, dgp_ref, dgn_ref):
        i = pl.program_id(0)
        yn, norm_vjp = jax.vjp(_gated_norm_groups, y_ref[...], z_ref[...], gn_ref[...])
        yn16 = yn.astype(BF16)
        y_ssm = jnp.dot(yn16, wos_ref[...], preferred_element_type=F32)
        y_att = jnp.dot(a_ref[...], woa_ref[...], preferred_element_type=F32)
        merged, merge_vjp = jax.vjp(_merge, ga_ref[...], gs_ref[...], y_att, y_ssm)
        merged16 = merged.astype(BF16)
        out = jnp.dot(merged16, wo_ref[...], preferred_element_type=F32)
        loss, loss_vjp = jax.vjp(_row_loss, out, gp_ref[...], x_ref[...], t_ref[...])
        counted = jnp.where(i > 0, 1.0, 0.0)
        dout, dgp, dres, _ = loss_vjp(counted)
        dout16 = dout.astype(BF16)
        dmerged = lax.dot_general(dout16, wo_ref[...], _NT, preferred_element_type=F32)
        dga, dgs, dya, dys = merge_vjp(dmerged)
        dya16, dys16 = dya.astype(BF16), dys.astype(BF16)
        dyn = lax.dot_general(dys16, wos_ref[...], _NT, preferred_element_type=F32)
        dy, dz, dgn = norm_vjp(dyn)

        yn_ref[...] = yn16
        mg_ref[...] = merged16
        dout_ref[...] = dout16
        dya_ref[...] = dya16
        dys_ref[...] = dys16
        da_ref[...] = lax.dot_general(dya16, woa_ref[...], _NT, preferred_element_type=F32)
        dy_ref[...] = dy
        dz_ref[...] = dz.astype(BF16)
        dga_ref[...] = dga.astype(BF16)
        dgs_ref[...] = dgs.astype(BF16)
        dres_ref[...] = dres

        @pl.when(i == 0)
        def _():
            loss_ref[...] = jnp.zeros_like(loss_ref)
            dgp_ref[...] = jnp.zeros_like(dgp_ref)
            dgn_ref[...] = jnp.zeros_like(dgn_ref)

        loss_ref[...] += loss * counted
        dgp_ref[...] += dgp
        dgn_ref[...] += dgn

    wide, narrow = _row_spec(SSM_INNER), _row_spec(D_MODEL)
    resident = pl.BlockSpec(memory_space=pltpu.VMEM)
    bf = lambda w: jax.ShapeDtypeStruct((rows, w), BF16)
    f32 = lambda w: jax.ShapeDtypeStruct((rows, w), F32)
    return _call(body, name="tail", grid=(nb,),
                 in_specs=[wide, wide, _row_spec(D_MODEL, GATE_ATT_BLOCK), _row_spec(D_MODEL, GATE_SSM_BLOCK), narrow,
                           _x_spec(), _x_spec(), resident, resident, resident, _full((1, SSM_INNER)),
                           _full((1, D_MODEL))],
                 out_specs=[wide, narrow, narrow, narrow, narrow, narrow, wide, wide, narrow, narrow, narrow,
                            _full((8, LANES)), _full((1, D_MODEL)), _full((1, SSM_INNER))],
                 out_shape=[bf(SSM_INNER), bf(D_MODEL), bf(D_MODEL), bf(D_MODEL), bf(D_MODEL), f32(D_MODEL),
                            f32(SSM_INNER), bf(SSM_INNER), bf(D_MODEL), bf(D_MODEL), f32(D_MODEL),
                            jax.ShapeDtypeStruct((8, LANES), F32), jax.ShapeDtypeStruct((1, D_MODEL), F32),
                            jax.ShapeDtypeStruct((1, SSM_INNER), F32)],
                 sem=("arbitrary",))(y_ssd, proj, proj, proj, a_att, x, target, woa, wos, wo, g_norm, g_post)


_NT = (((1,), (1,)), ((), ()))
ALIBI_SLOPES = tuple(2.0 ** (-8.0 * (h + 1) / ATT_Q_HEADS) for h in range(ATT_Q_HEADS))
KV_WIDTH = ATT_KV_HEADS * HEAD_DIM
Q_BLOCK = SEG["q"][2] // D_MODEL
Z_ATT_BLOCK = SEG["z_att"][2] // D_MODEL
K_BLOCK = SEG["k"][2] // KV_WIDTH
V_BLOCK = SEG["v"][2] // KV_WIDTH
META_ROW_BLOCK = PAD_ROWS // N_META


@jax.custom_vjp
def _swap_halves(x):
    return pltpu.roll(x, HEAD_DIM, 1)


_swap_halves.defvjp(lambda x: (pltpu.roll(x, HEAD_DIM, 1), None), lambda _, g: (pltpu.roll(g, HEAD_DIM, 1),))


def _both_halves(t, half):
    first = lax.broadcasted_iota(jnp.int32, t.shape, 1) < HEAD_DIM
    sw = _swap_halves(t)
    return jnp.where(first, t, sw) if half == 0 else jnp.where(first, sw, t)


def _attn_rows(q, z, kp, kc, vp, vc, km, vm, sinks, n):
    rows = ATT_GROUP * BLOCK
    i = lax.broadcasted_iota(jnp.int32, (rows, BLOCK), 0) & (BLOCK - 1)
    j = lax.broadcasted_iota(jnp.int32, (rows, BLOCK), 1)
    rel_c = (i - j).astype(F32)
    rel_p = rel_c + float(BLOCK)
    nv = jnp.zeros((rows, BLOCK), jnp.int32) + n
    ok_c = (i >= j) & (nv >= 1)
    ok_p = (j > i) & (nv >= 2)
    im = lax.broadcasted_iota(jnp.int32, (rows, N_META), 0) & (BLOCK - 1)
    jm = lax.broadcasted_iota(jnp.int32, (rows, N_META), 1)
    ok_m = ((jnp.zeros((rows, N_META), jnp.int32) + n) >= 1) | (im >= PAD_ROWS + jm)
    first = lax.broadcasted_iota(jnp.int32, (BLOCK, LANES), 1) < HEAD_DIM
    neg = -jnp.inf
    outs = []
    for kv in range(ATT_KV_HEADS):
        tile, half = divmod(kv, 2)
        lanes = slice(tile * LANES, (tile + 1) * LANES)
        kc2, kp2, km2 = (_both_halves(t[:, lanes], half).astype(BF16) for t in (kc, kp, km))
        vc2, vp2, vm2 = (_both_halves(t[:, lanes], half).astype(BF16) for t in (vc, vp, vm))
        qs, slope, sk = [], [], []
        for pair in range(ATT_GROUP // 2):
            c0 = (kv * ATT_GROUP + 2 * pair) * HEAD_DIM
            qp = q[:, c0:c0 + LANES] * HEAD_DIM ** -0.5
            qs += [jnp.where(first, qp, 0.0), jnp.where(first, 0.0, qp)]
        for g in range(ATT_GROUP):
            slope.append(jnp.full((BLOCK, 1), ALIBI_SLOPES[kv * ATT_GROUP + g], F32))
            sk.append(jnp.broadcast_to(sinks[kv * ATT_GROUP + g], (BLOCK, 1)))
        qs = jnp.concatenate(qs, axis=0).astype(BF16)
        slope = jnp.concatenate(slope, axis=0)
        sk = jnp.concatenate(sk, axis=0)
        sc = jnp.where(ok_c, lax.dot_general(qs, kc2, _NT, preferred_element_type=F32) - slope * rel_c, neg)
        sp = jnp.where(ok_p, lax.dot_general(qs, kp2, _NT, preferred_element_type=F32) - slope * rel_p, neg)
        sm = jnp.where(ok_m, lax.dot_general(qs, km2, _NT, preferred_element_type=F32), neg)
        mx = jnp.maximum(jnp.maximum(jnp.max(sc, axis=1, keepdims=True), jnp.max(sp, axis=1, keepdims=True)),
                         jnp.maximum(jnp.max(sm, axis=1, keepdims=True), sk))
        mx = lax.stop_gradient(mx)
        ec, ep, em, es = jnp.exp(sc - mx), jnp.exp(sp - mx), jnp.exp(sm - mx), jnp.exp(sk - mx)
        den = (es + jnp.sum(ec, axis=1, keepdims=True) + jnp.sum(ep, axis=1, keepdims=True)
               + jnp.sum(em, axis=1, keepdims=True))
        inv = 1.0 / den
        o = (jnp.dot((ec * inv).astype(BF16), vc2, preferred_element_type=F32)
             + jnp.dot((ep * inv).astype(BF16), vp2, preferred_element_type=F32)
             + jnp.dot((em * inv).astype(BF16), vm2, preferred_element_type=F32))
        for pair in range(ATT_GROUP // 2):
            r0 = 2 * pair * BLOCK
            outs.append(jnp.where(first, o[r0:r0 + BLOCK], o[r0 + BLOCK:r0 + 2 * BLOCK]))
    return jnp.concatenate(outs, axis=1) * _silu(z)


def _attn_specs(nb, steps_clamped):
    def blk(t):
        return jnp.minimum(t, nb - 1) if steps_clamped else t

    wide = lambda col: pl.BlockSpec((BLOCK, D_MODEL), lambda t: (blk(t), col))
    cur = lambda col: pl.BlockSpec((BLOCK, KV_WIDTH), lambda t: (blk(t), col))
    prev = lambda col: pl.BlockSpec((BLOCK, KV_WIDTH), lambda t: (jnp.maximum(blk(t) - 1, 0), col))
    meta = lambda col: pl.BlockSpec((N_META, KV_WIDTH), lambda t: (META_ROW_BLOCK, col))
    sinks = pl.BlockSpec((ATT_Q_HEADS, 1, 1), lambda t: (0, 0, 0))
    return [wide(Q_BLOCK), wide(Z_ATT_BLOCK), prev(K_BLOCK), cur(K_BLOCK), prev(V_BLOCK), cur(V_BLOCK),
            meta(K_BLOCK), meta(V_BLOCK), sinks]


def attn_fwd(proj, sinks):
    nb = proj.shape[0] // BLOCK

    def body(q_ref, z_ref, kp_ref, kc_ref, vp_ref, vc_ref, km_ref, vm_ref, sk_ref, o_ref):
        o_ref[...] = _attn_rows(q_ref[...], z_ref[...], kp_ref[...], kc_ref[...], vp_ref[...], vc_ref[...],
                                km_ref[...], vm_ref[...], tuple(sk_ref[h] for h in range(ATT_Q_HEADS)),
                                pl.program_id(0)).astype(BF16)

    return _call(body, name="attn_fwd", grid=(nb,), in_specs=_attn_specs(nb, False), out_specs=_row_spec(D_MODEL),
                 out_shape=jax.ShapeDtypeStruct((nb * BLOCK, D_MODEL), BF16), sem=("parallel",))(*([proj] * 8), sinks)


def attn_bwd(da, proj, sinks):
    nb = proj.shape[0] // BLOCK
    last = nb - 1
    wide = pl.BlockSpec((BLOCK, D_MODEL), lambda t: (jnp.minimum(t, last), 0))
    done = pl.BlockSpec((BLOCK, KV_WIDTH), lambda t: (jnp.maximum(t - 1, 0), 0))
    meta = _full((N_META, KV_WIDTH))
    par = _full((ATT_Q_HEADS, 1, 1))

    def body(da_ref, q_ref, z_ref, kp_ref, kc_ref, vp_ref, vc_ref, km_ref, vm_ref, sk_ref,
             dq_ref, dz_ref, dk_ref, dv_ref, dkm_ref, dvm_ref, dsk_ref, ck_ref, cv_ref):
        t = pl.program_id(0)

        @pl.when(t == 0)
        def _():
            ck_ref[...] = jnp.zeros_like(ck_ref)
            cv_ref[...] = jnp.zeros_like(cv_ref)
            dkm_ref[...] = jnp.zeros_like(dkm_ref)
            dvm_ref[...] = jnp.zeros_like(dvm_ref)
            dsk_ref[...] = jnp.zeros_like(dsk_ref)

        @pl.when(t < nb)
        def _():
            def f(q, z, kp, kc, vp, vc, km, vm, sk):
                return _attn_rows(q, z, kp, kc, vp, vc, km, vm, sk, t)

            _, vjp = jax.vjp(f, q_ref[...], z_ref[...], kp_ref[...], kc_ref[...], vp_ref[...], vc_ref[...],
                             km_ref[...], vm_ref[...], tuple(sk_ref[h] for h in range(ATT_Q_HEADS)))
            dq, dz, dkp, dkc, dvp, dvc, dkm, dvm, dsk = vjp(da_ref[...])
            dq_ref[...] = dq.astype(BF16)
            dz_ref[...] = dz.astype(BF16)
            for h in range(ATT_Q_HEADS):
                dsk_ref[h] += dsk[h]
            dk_ref[...] = ck_ref[...] + dkp
            dv_ref[...] = cv_ref[...] + dvp
            ck_ref[...] = dkc
            cv_ref[...] = dvc
            dkm_ref[...] += dkm
            dvm_ref[...] += dvm

        @pl.when(t == nb)
        def _():
            dk_ref[...] = ck_ref[...]
            dv_ref[...] = cv_ref[...]

    rows = nb * BLOCK
    return _call(body, name="attn_bwd", grid=(nb + 1,), in_specs=[wide] + _attn_specs(nb, True),
                 out_specs=[wide, wide, done, done, meta, meta, par],
                 out_shape=[jax.ShapeDtypeStruct((rows, D_MODEL), BF16), jax.ShapeDtypeStruct((rows, D_MODEL), BF16),
                            jax.ShapeDtypeStruct((rows, KV_WIDTH), F32), jax.ShapeDtypeStruct((rows, KV_WIDTH), F32),
                            jax.ShapeDtypeStruct((N_META, KV_WIDTH), F32), jax.ShapeDtypeStruct((N_META, KV_WIDTH), F32),
                            jax.ShapeDtypeStruct(sinks.shape, F32)],
                 scratch=[pltpu.VMEM((BLOCK, KV_WIDTH), F32), pltpu.VMEM((BLOCK, KV_WIDTH), F32)],
                 sem=("arbitrary",))(da, *([proj] * 8), sinks)


XBC_BLOCK0 = SEG["xbc"][2] // D_MODEL
CONV_COL_BLOCKS = CONV_DIM // D_MODEL
DT_TILE = SEG["dt"][2] // LANES


HALO = 8
HALOS_PER_BLOCK = BLOCK // HALO


def _shift_rows(cur, before, j):
    if j == 0:
        return cur
    n = cur.shape[0]
    row = lax.broadcasted_iota(jnp.int32, cur.shape, 0)
    head = pltpu.roll(before, j, 0)
    if n > HALO:
        head = jnp.concatenate([head, jnp.zeros((n - HALO, cur.shape[1]), cur.dtype)], axis=0)
    return jnp.where(row >= j, pltpu.roll(cur, j, 0), head)


def _conv_pre(cur, before, w_ref, b_ref):
    pre = b_ref[...] + w_ref[CONV_WIDTH - 1:CONV_WIDTH, :] * cur
    for k in range(CONV_WIDTH - 1):
        pre = pre + w_ref[k:k + 1, :] * _shift_rows(cur, before, CONV_WIDTH - 1 - k)
    return pre


def _conv_specs(nb):
    cur = pl.BlockSpec((BLOCK, D_MODEL), lambda j, i: (i, XBC_BLOCK0 + j))
    before = pl.BlockSpec((HALO, D_MODEL), lambda j, i: (jnp.maximum(i * HALOS_PER_BLOCK - 1, 0), XBC_BLOCK0 + j))
    after = pl.BlockSpec((HALO, D_MODEL),
                         lambda j, i: (jnp.minimum(i + 1, nb - 1) * HALOS_PER_BLOCK, XBC_BLOCK0 + j))
    return cur, before, after


def _valid_rows(i):
    row = lax.broadcasted_iota(jnp.int32, (BLOCK, D_MODEL), 0)
    return jnp.maximum((row >= PAD_ROWS).astype(F32), jnp.where(i > 0, 1.0, 0.0))


def conv_fwd(proj, conv_w, conv_b):
    nb = proj.shape[0] // BLOCK
    cur, before, _ = _conv_specs(nb)

    def body(c_ref, p_ref, w_ref, b_ref, o_ref):
        i = pl.program_id(1)
        pre = _conv_pre(c_ref[...], p_ref[...] * jnp.where(i > 0, 1.0, 0.0), w_ref, b_ref)
        o_ref[...] = _silu(pre) * _valid_rows(i)

    return _call(body, name="conv_fwd", grid=(CONV_COL_BLOCKS, nb),
                 in_specs=[cur, before, pl.BlockSpec((CONV_WIDTH, D_MODEL), lambda j, i: (0, j)),
                           pl.BlockSpec((1, D_MODEL), lambda j, i: (0, j))],
                 out_specs=pl.BlockSpec((BLOCK, D_MODEL), lambda j, i: (i, j)),
                 out_shape=jax.ShapeDtypeStruct((nb * BLOCK, CONV_DIM), F32),
                 sem=("parallel", "parallel"))(proj, proj, conv_w, conv_b)


def conv_bwd(dxbc, proj, conv_w, conv_b):
    nb = proj.shape[0] // BLOCK
    last = nb - 1
    cur, before, after = _conv_specs(nb)
    dcur = pl.BlockSpec((BLOCK, D_MODEL), lambda j, i: (i, j))
    dafter = pl.BlockSpec((HALO, D_MODEL), lambda j, i: (jnp.minimum(i + 1, last) * HALOS_PER_BLOCK, j))
    wspec = pl.BlockSpec((CONV_WIDTH, D_MODEL), lambda j, i: (0, j))
    bspec = pl.BlockSpec((1, D_MODEL), lambda j, i: (0, j))

    def body(dc_ref, da_ref, c_ref, p_ref, a_ref, w_ref, b_ref, du_ref, dw_ref, db_ref):
        i = pl.program_id(1)
        row = lax.broadcasted_iota(jnp.int32, (BLOCK, D_MODEL), 0)
        curv = c_ref[...]
        beforev = p_ref[...] * jnp.where(i > 0, 1.0, 0.0)

        def dpre_of(pre, d):
            s = jax.nn.sigmoid(pre)
            return d * (s * (1.0 + pre * (1.0 - s)))

        dp_c = dpre_of(_conv_pre(curv, beforev, w_ref, b_ref), dc_ref[...] * _valid_rows(i))
        dp_a = dpre_of(_conv_pre(a_ref[...], curv[BLOCK - HALO:], w_ref, b_ref),
                       da_ref[...] * jnp.where(i < last, 1.0, 0.0))
        du = w_ref[CONV_WIDTH - 1:CONV_WIDTH, :] * dp_c
        for j in range(1, CONV_WIDTH):
            tail = jnp.concatenate([jnp.zeros((BLOCK - HALO, D_MODEL), F32), pltpu.roll(dp_a, HALO - j, 0)], axis=0)
            up = jnp.where(row < BLOCK - j, pltpu.roll(dp_c, BLOCK - j, 0), tail)
            du = du + w_ref[CONV_WIDTH - 1 - j:CONV_WIDTH - j, :] * up
        du_ref[...] = du.astype(BF16)

        @pl.when(i == 0)
        def _():
            dw_ref[...] = jnp.zeros_like(dw_ref)
            db_ref[...] = jnp.zeros_like(db_ref)

        for k in range(CONV_WIDTH):
            dw_ref[k:k + 1, :] += jnp.sum(dp_c * _shift_rows(curv, beforev, CONV_WIDTH - 1 - k), axis=0, keepdims=True)
        db_ref[...] += jnp.sum(dp_c, axis=0, keepdims=True)

    return _call(body, name="conv_bwd", grid=(CONV_COL_BLOCKS, nb),
                 in_specs=[dcur, dafter, cur, before, after, wspec, bspec], out_specs=[dcur, wspec, bspec],
                 out_shape=[jax.ShapeDtypeStruct((nb * BLOCK, CONV_DIM), BF16),
                            jax.ShapeDtypeStruct((CONV_WIDTH, CONV_DIM), F32), jax.ShapeDtypeStruct((1, CONV_DIM), F32)],
                 sem=("parallel", "arbitrary"))(dxbc, dxbc, proj, proj, proj, conv_w, conv_b)


def _head_expand():
    e = np.zeros((LANES, SSM_INNER), np.float32)
    for h in range(SSM_HEADS):
        e[h, h * HEAD_DIM:(h + 1) * HEAD_DIM] = 1.0
    return jnp.asarray(e, dtype=BF16)


def _softplus(x):
    return jnp.maximum(x, 0.0) + jnp.log(1.0 + jnp.exp(-jnp.abs(x)))


def _bf16_parts(x):
    hi = x.astype(BF16)
    rest = x - hi.astype(F32)
    mid = rest.astype(BF16)
    return hi, mid, (rest - mid.astype(F32)).astype(BF16)


@jax.custom_vjp
def _times_01(x, m):
    return sum(jnp.dot(p, m, preferred_element_type=F32) for p in _bf16_parts(x))


def _times_01_bwd(m, g):
    return sum(lax.dot_general(p, m, _NT, preferred_element_type=F32) for p in _bf16_parts(g)), jnp.zeros_like(m)


_times_01.defvjp(lambda x, m: (_times_01(x, m), m), _times_01_bwd)


def _causal_ones():
    l = lax.broadcasted_iota(jnp.int32, (BLOCK, BLOCK), 0)
    s = lax.broadcasted_iota(jnp.int32, (BLOCK, BLOCK), 1)
    return (l >= s).astype(BF16)


@jax.custom_vjp
def _cumsum_rows(a):
    return sum(jnp.dot(_causal_ones(), p, preferred_element_type=F32) for p in _bf16_parts(a))


def _cumsum_rows_bwd(_, g):
    tn = (((0,), (0,)), ((), ()))
    return (sum(lax.dot_general(_causal_ones(), p, tn, preferred_element_type=F32) for p in _bf16_parts(g)),)


_cumsum_rows.defvjp(lambda a: (_cumsum_rows(a), None), _cumsum_rows_bwd)


def _ssd_group(xs, dt_tile, expand, bias, alog, dsk, bg, cg, state):
    l = lax.broadcasted_iota(jnp.int32, (BLOCK, BLOCK), 0)
    s = lax.broadcasted_iota(jnp.int32, (BLOCK, BLOCK), 1)
    causal = l >= s
    first_head = s < HEAD_DIM
    dt = _softplus(dt_tile + bias)
    a = dt * (-jnp.exp(alog))
    one_row = lambda v: jnp.broadcast_to(v, (HALO, LANES))
    per_lane = _times_01(jnp.concatenate([dt, _cumsum_rows(a), one_row(jnp.sum(a, axis=0, keepdims=True)),
                                          one_row(dsk)], axis=0), expand)
    dtx, cs = per_lane[0:BLOCK], per_lane[BLOCK:2 * BLOCK]
    tot, dsk = per_lane[2 * BLOCK:2 * BLOCK + 1], per_lane[2 * BLOCK + HALO:2 * BLOCK + HALO + 1]
    bb, cb16 = bg.astype(BF16), cg.astype(BF16)
    cb = lax.dot_general(cb16, bb, _NT, preferred_element_type=F32)
    xr = xs * dtx
    y_diag = []
    for p in range(GROUP_W // LANES):
        lanes = slice(p * LANES, (p + 1) * LANES)
        c_pair = cs[:, lanes]
        c_swap = _swap_halves(c_pair)
        m = []
        for c_head in (jnp.where(first_head, c_pair, c_swap), jnp.where(first_head, c_swap, c_pair)):
            m.append(cb * jnp.exp(jnp.where(causal, c_head - c_head.T, -jnp.inf)))
        x_pair = xr[:, lanes]
        x_diag = jnp.concatenate([jnp.where(first_head, x_pair, 0.0), jnp.where(first_head, 0.0, x_pair)], axis=0)
        y_diag.append(jnp.dot(jnp.concatenate(m, axis=1).astype(BF16), x_diag.astype(BF16),
                              preferred_element_type=F32))
    st = lax.dot_general(bb, (xr * jnp.exp(tot - cs)).astype(BF16), (((0,), (0,)), ((), ())),
                         preferred_element_type=F32)
    new_state = state * jnp.exp(tot) + st
    y_off = jnp.dot(cb16, state.astype(BF16), preferred_element_type=F32) * jnp.exp(cs)
    return jnp.concatenate(y_diag, axis=1) + y_off + dsk * xs, new_state


B_TILE0 = SSM_INNER // LANES
C_TILE0 = B_TILE0 + SSM_GROUPS


def _ssd_specs(chunk):
    xs = pl.BlockSpec((BLOCK, GROUP_W), lambda c, g: (chunk(c), g))
    dt = pl.BlockSpec((BLOCK, LANES), lambda c, g: (chunk(c), DT_TILE))
    expand = pl.BlockSpec((LANES, GROUP_W), lambda c, g: (0, g))
    b = pl.BlockSpec((BLOCK, SSM_STATE), lambda c, g: (chunk(c), B_TILE0 + g))
    cc = pl.BlockSpec((BLOCK, SSM_STATE), lambda c, g: (chunk(c), C_TILE0 + g))
    par = _full((1, LANES))
    state = pl.BlockSpec((None, SSM_STATE, GROUP_W), lambda c, g: (chunk(c), 0, g))
    return xs, dt, expand, b, cc, par, state


def ssd_fwd(xbc, proj, expand, bias, alog, dsk):
    nb = xbc.shape[0] // BLOCK
    xs, dt, ex, b, cc, par, state = _ssd_specs(lambda c: c)

    def body(x_ref, dt_ref, e_ref, bi_ref, al_ref, dk_ref, b_ref, c_ref, y_ref, sp_ref, st_ref):
        g = pl.program_id(1)

        @pl.when(pl.program_id(0) == 0)
        def _():
            st_ref[g] = jnp.zeros((SSM_STATE, GROUP_W), F32)

        entering = st_ref[g]
        sp_ref[...] = entering
        y_ref[...], st_ref[g] = _ssd_group(x_ref[...], dt_ref[...], e_ref[...], bi_ref[...], al_ref[...], dk_ref[...],
                                           b_ref[...], c_ref[...], entering)

    return _call(body, name="ssd_fwd", grid=(nb, SSM_GROUPS), in_specs=[xs, dt, ex, par, par, par, b, cc],
                 out_specs=[xs, state],
                 out_shape=[jax.ShapeDtypeStruct((nb * BLOCK, SSM_INNER), F32),
                            jax.ShapeDtypeStruct((nb, SSM_STATE, SSM_INNER), F32)],
                 scratch=[pltpu.VMEM((SSM_GROUPS, SSM_STATE, GROUP_W), F32)],
                 sem=("arbitrary", "arbitrary"))(xbc, proj, expand, bias, alog, dsk, xbc, xbc)


def ssd_bwd(dy, xbc, proj, expand, bias, alog, dsk, states, comm):
    nb = xbc.shape[0] // BLOCK
    last = nb - 1
    xs, dt, ex, b, cc, par, state = _ssd_specs(lambda c: last - c)
    tile = pl.BlockSpec((BLOCK, LANES), lambda c, g: (last - c, 0))
    nspec = pl.BlockSpec((BLOCK, SSM_STATE), lambda c, g: (last - c, g))

    def body(dy_ref, x_ref, dt_ref, e_ref, bi_ref, al_ref, dk_ref, b_ref, c_ref, sp_ref,
             dx_ref, ddt_ref, db_ref, dc_ref, dbi_ref, dal_ref, ddk_ref, ds_ref, ddt_acc):
        c, g = pl.program_id(0), pl.program_id(1)

        @pl.when(c == 0)
        def _():
            ds_ref[g] = jnp.zeros((SSM_STATE, GROUP_W), F32)

        @pl.when((c == 0) & (g == 0))
        def _():
            dbi_ref[...] = jnp.zeros_like(dbi_ref)
            dal_ref[...] = jnp.zeros_like(dal_ref)
            ddk_ref[...] = jnp.zeros_like(ddk_ref)

        expand_rows = e_ref[...]

        def f(xs, dt_tile, bias, alog, dsk, bg, cg, state):
            return _ssd_group(xs, dt_tile, expand_rows, bias, alog, dsk, bg, cg, state)

        _, vjp = jax.vjp(f, x_ref[...], dt_ref[...], bi_ref[...], al_ref[...], dk_ref[...], b_ref[...], c_ref[...],
                         sp_ref[...])
        dx_ref[...], ddt, dbi, dal, ddk, db_ref[...], dc_ref[...], ds_ref[g] = vjp((dy_ref[...], ds_ref[g]))
        dbi_ref[...] += dbi
        dal_ref[...] += dal
        ddk_ref[...] += ddk

        @pl.when(g == 0)
        def _():
            ddt_acc[...] = ddt

        @pl.when(g > 0)
        def _():
            ddt_acc[...] += ddt

        @pl.when(g == SSM_GROUPS - 1)
        def _():
            ddt_ref[...] = ddt_acc[...].astype(BF16)

    def at():
        c, g = pl.program_id(0), pl.program_id(1)
        return (c == 0) & (g == 0), (c == 0) & (g == 1), (c == last) & (g == SSM_GROUPS - 1)

    par_shape = jax.ShapeDtypeStruct((1, LANES), F32)
    return _call_with_comm(
        body, comm, at, (dy, xbc, proj, expand, bias, alog, dsk, xbc, xbc, states), name="ssd_bwd",
        grid=(nb, SSM_GROUPS), in_specs=[xs, xs, dt, ex, par, par, par, b, cc, state],
        out_specs=[xs, tile, nspec, nspec, par, par, par],
        out_shape=[jax.ShapeDtypeStruct((nb * BLOCK, SSM_INNER), F32), jax.ShapeDtypeStruct((nb * BLOCK, LANES), BF16),
                   jax.ShapeDtypeStruct((nb * BLOCK, SSM_GROUPS * SSM_STATE), F32),
                   jax.ShapeDtypeStruct((nb * BLOCK, SSM_GROUPS * SSM_STATE), F32), par_shape, par_shape, par_shape],
        scratch=[pltpu.VMEM((SSM_GROUPS, SSM_STATE, GROUP_W), F32), pltpu.VMEM((BLOCK, LANES), F32)])


SLAB_ROWS = 24
SLAB_META_ROW = 8


def pack_small(dcw, dcb, dgpre, dgpost, dbias, dalog, ddsk, dsinks, dgn, dmeta):
    def body(cw, cb, gpre, gpost, dtb, al, dk, sk, gn, meta, o_ref):
        o_ref[...] = jnp.zeros_like(o_ref)
        o_ref[0:CONV_WIDTH, :] = cw[...]
        o_ref[4:5, :] = cb[...]
        o_ref[5:6, 0:1024] = gpre[...]
        o_ref[5:6, 1024:2048] = gpost[...]
        o_ref[5:6, 2048:2176] = dtb[...]
        o_ref[5:6, 2176:2304] = al[...]
        o_ref[5:6, 2304:2432] = dk[...]
        o_ref[5:6, 2432:2560] = sk[...]
        o_ref[6:7, 0:SSM_INNER] = gn[...]
        o_ref[SLAB_META_ROW:SLAB_META_ROW + N_META, 0:D_MODEL] = meta[...]

    args = (dcw, dcb, dgpre, dgpost, dbias, dalog, ddsk, dsinks, dgn, dmeta)
    return _call(body, name="pack_small", in_specs=[_full(a.shape) for a in args],
                 out_specs=_full((SLAB_ROWS, CONV_DIM)), out_shape=jax.ShapeDtypeStruct((SLAB_ROWS, CONV_DIM), F32))(*args)


def _lane_tile(v):
    return jnp.pad(v, ((0, 0), (0, LANES - v.shape[1])))


def kernel(x, meta_tokens, g_pre, w_in, conv_w, conv_b, dt_bias, a_log, d_skip, attn_sinks, g_ssm_norm, w_out_att, w_out_ssm, w_out, g_post, loss_target, m_meta_tokens, m_g_pre, m_w_in, m_conv_w, m_conv_b, m_dt_bias, m_a_log, m_d_skip, m_attn_sinks, m_g_ssm_norm, m_w_out_att, m_w_out_ssm, m_w_out, m_g_post, v_meta_tokens, v_g_pre, v_w_in, v_conv_w, v_conv_b, v_dt_bias, v_a_log, v_d_skip, v_attn_sinks, v_g_ssm_norm, v_w_out_att, v_w_out_ssm, v_w_out, v_g_post):
    chip = _chip_index()

    conv_w_rows = jnp.pad(conv_w[0], ((0, 2 * 8 - CONV_WIDTH), (0, 0)))
    g_w_in, g_conv_w, g_meta = run_comm("gather_w_in", TwoLevelGather([pack_w_in(w_in[0]), conv_w_rows, meta_tokens]))
    w_all = unpack_w_in(g_w_in)
    cw_full = g_conv_w[:, :CONV_WIDTH].transpose(1, 0, 2).reshape(CONV_WIDTH, CONV_DIM)
    meta_full = g_meta.transpose(1, 0, 2).reshape(N_META, D_MODEL)

    h, u = prep(x, meta_full, g_pre)
    proj, g_w_out = mm_nn("in_proj", u, w_all, comm=TwoLevelGather(
        [w_out_att[0].astype(BF16), w_out_ssm[0].astype(BF16), w_out[0].astype(BF16)]))
    woa = g_w_out[0].reshape(D_MODEL, D_MODEL)
    wos = g_w_out[1].reshape(SSM_INNER, D_MODEL)
    wo = g_w_out[2].reshape(D_MODEL, D_MODEL)

    sinks3 = attn_sinks.reshape(ATT_Q_HEADS, 1, 1)
    a_att = attn_fwd(proj, sinks3)

    xbc = conv_fwd(proj, cw_full, conv_b)
    expand = _head_expand()
    head_pars = (_lane_tile(dt_bias), _lane_tile(a_log), _lane_tile(d_skip))
    y_ssd, states = ssd_fwd(xbc, proj, expand, *head_pars)

    (yn, merged, dout, dy_att, dy_ssm, da_att, dy_ssd, dz_ssm, dga, dgs, dres, loss_tile, dg_post, dgn) = tail(
        y_ssd, proj, a_att, x, loss_target, woa, wos, wo, g_ssm_norm, g_post)
    loss = lax.psum(loss_tile[0, 0], ("x", "y", "c"))

    dwo = mm_tn("out_proj_dw", merged, dout)
    dwoa = mm_tn("att_out_dw", a_att, dy_att)
    dwos = mm_tn("ssm_out_dw", yn, dy_ssm)
    dq, dz_att, dk, dv, dkmeta, dvmeta, dsinks3 = attn_bwd(da_att, proj, sinks3)
    dk = dk.at[PAD_ROWS:BLOCK].add(dkmeta).astype(BF16)
    dv = dv.at[PAD_ROWS:BLOCK].add(dvmeta).astype(BF16)

    def pieces(g):
        return g.astype(BF16).reshape(4, 2, g.shape[0] // 8, g.shape[1])

    def to_owner(g):
        return (lambda ref, dev: ref.at[_chip_of(dev), dev[2]], (g.shape[0] // 8, g.shape[1]))

    (dxs, ddt_tile, dbg, dcg, dbias, dalog, ddsk), sent_w_out = ssd_bwd(
        dy_ssd, xbc, proj, expand, *head_pars, states,
        DirectExchange([pieces(dwoa), pieces(dwos), pieces(dwo)], [to_owner(dwoa), to_owner(dwos), to_owner(dwo)],
                       ALL_MASKS, "dev", 8))
    dxbc = jnp.concatenate([dxs, dbg, dcg], axis=1)
    dxbc_raw, dcw, dcb = conv_bwd(dxbc, proj, cw_full, conv_b)

    dproj = jnp.concatenate([dz_ssm, dxbc_raw, dq, dz_att, dga, dgs, dk, dv, ddt_tile], axis=1)
    dw_all = mm_tn("in_proj_dw", u, dproj)

    half_rows = D_MODEL // 2
    partial = pack_grad_w_in(dw_all).reshape(4, 2, half_rows, PACK_W)
    from_sibling, = run_comm("pair_grads", DirectExchange(
        [partial], [(lambda ref, dev: ref.at[pl.ds(0, 4), dev[2]], (4, half_rows, PACK_W))], SIBLING_MASK, "core", 2,
        keep_own=False))
    chip_sum = sum_pair(partial, from_sibling)
    du, (sent_w_in,) = mm_nt("in_proj_dx", dproj, w_all, comm=DirectExchange(
        [chip_sum], [(lambda ref, dev: ref.at[_chip_of(dev)], (half_rows, PACK_W))], CHIP_MASKS, "chip", 4))
    grad_x, dmeta, dg_pre = prep_bwd(h, du, dres, g_pre)

    slab = pack_small(dcw, dcb, dg_pre, dg_post, dbias, dalog, ddsk, _lane_tile(dsinks3.reshape(1, ATT_Q_HEADS)), dgn,
                      dmeta)
    halves = [sum_slots("sum_" + nm, r)
              for nm, r in zip(("w_in", "w_out_att", "w_out_ssm", "w_out"), [sent_w_in] + list(sent_w_out))]
    shared = run_comm("share_grads", Both(DirectExchange(halves, [None] * 4, SIBLING_MASK, "core", 2),
                                          DirectExchange([slab], [None], ALL_MASKS, "dev", 8)))
    g_w_in_packed, g_woa, g_wos, g_wo = [f.reshape(2 * f.shape[1], f.shape[2]) for f in shared[:4]]
    small = sum_slots("sum_small", shared[4])

    g_w_in, d_w_in, nm_w_in, nv_w_in = adamw_w_in(g_w_in_packed, w_in[0], m_w_in[0], v_w_in[0])
    d_woa, nm_woa, nv_woa = adamw_rows("adamw_w_out_att", g_woa, w_out_att[0], m_w_out_att[0], v_w_out_att[0])
    d_wos, nm_wos, nv_wos = adamw_rows("adamw_w_out_ssm", g_wos, w_out_ssm[0], m_w_out_ssm[0], v_w_out_ssm[0])
    d_wo, nm_wo, nv_wo = adamw_rows("adamw_w_out", g_wo, w_out[0], m_w_out[0], v_w_out[0])

    cw_cols = CONV_DIM // 4
    meta_cols = D_MODEL // 4
    g_small = {
        "meta_tokens": lax.dynamic_slice(small, (SLAB_META_ROW, chip * meta_cols), (N_META, meta_cols)),
        "g_pre": small[5:6, 0:1024],
        "conv_w": lax.dynamic_slice(small, (0, chip * cw_cols), (CONV_WIDTH, cw_cols)),
        "conv_b": small[4:5, :],
        "dt_bias": small[5:6, 2048:2048 + SSM_HEADS],
        "a_log": small[5:6, 2176:2176 + SSM_HEADS],
        "d_skip": small[5:6, 2304:2304 + SSM_HEADS],
        "attn_sinks": small[5:6, 2432:2432 + ATT_Q_HEADS],
        "g_ssm_norm": small[6:7, 0:SSM_INNER],
        "g_post": small[5:6, 1024:2048],
    }
    names = list(g_small)
    w_small = dict(meta_tokens=meta_tokens, g_pre=g_pre, conv_w=conv_w[0], conv_b=conv_b, dt_bias=dt_bias, a_log=a_log,
                   d_skip=d_skip, attn_sinks=attn_sinks, g_ssm_norm=g_ssm_norm, g_post=g_post)
    m_small = dict(meta_tokens=m_meta_tokens, g_pre=m_g_pre, conv_w=m_conv_w[0], conv_b=m_conv_b, dt_bias=m_dt_bias,
                   a_log=m_a_log, d_skip=m_d_skip, attn_sinks=m_attn_sinks, g_ssm_norm=m_g_ssm_norm, g_post=m_g_post)
    v_small = dict(meta_tokens=v_meta_tokens, g_pre=v_g_pre, conv_w=v_conv_w[0], conv_b=v_conv_b, dt_bias=v_dt_bias,
                   a_log=v_a_log, d_skip=v_d_skip, attn_sinks=v_attn_sinks, g_ssm_norm=v_g_ssm_norm, g_post=v_g_post)
    upd = dict(zip(names, adamw_small([g_small[k] for k in names], [w_small[k] for k in names],
                                      [m_small[k] for k in names], [v_small[k] for k in names])))

    lead = {"conv_w"}

    def shaped(name, a):
        return a[None] if name in lead else a

    grads = dict(g_small, w_in=g_w_in, w_out_att=g_woa, w_out_ssm=g_wos, w_out=g_wo)
    deltas = dict({k: upd[k][0] for k in names}, w_in=d_w_in, w_out_att=d_woa, w_out_ssm=d_wos, w_out=d_wo)
    new_m = dict({k: upd[k][1] for k in names}, w_in=nm_w_in, w_out_att=nm_woa, w_out_ssm=nm_wos, w_out=nm_wo)
    new_v = dict({k: upd[k][2] for k in names}, w_in=nv_w_in, w_out_att=nv_woa, w_out_ssm=nv_wos, w_out=nv_wo)
    lead |= {"w_in", "w_out_att", "w_out_ssm", "w_out"}
    order = ["meta_tokens", "g_pre", "w_in", "conv_w", "conv_b", "dt_bias", "a_log", "d_skip", "attn_sinks",
             "g_ssm_norm", "w_out_att", "w_out_ssm", "w_out", "g_post"]
    outs = [loss, grad_x]
    for group in (grads, deltas, new_m, new_v):
        outs += [shaped(k, group[k]) for k in order]
    return tuple(outs)
```

```python
import functools

import numpy as np
import jax
import jax.numpy as jnp
from jax import lax
from jax.experimental import pallas as pl
from jax.experimental.pallas import tpu as pltpu

F32 = jnp.float32
BF16 = jnp.bfloat16
HI = lax.Precision.HIGHEST

D_MODEL = 1024
N_META = 16
BLOCK = 128
PAD_ROWS = BLOCK - N_META
NORM_EPS = 1e-6
HEAD_DIM = 64
ATT_Q_HEADS = 16
ATT_KV_HEADS = 4
ATT_GROUP = 4
SSM_INNER = 2048
SSM_HEADS = 32
SSM_GROUPS = 4
SSM_HEADS_PER_GROUP = 8
SSM_STATE = 128
CONV_WIDTH = 4
CONV_DIM = 3072
LANES = 128

ADAM_LR = 0.001
ADAM_B1 = 0.9
ADAM_B2 = 0.999
ADAM_EPS = 1e-08
ADAM_WD = 0.01
ADAM_STEP = 10

VMEM_LIMIT = 48 * 1024 * 1024

SHARD_W = 2440
PACK_W = 2560
SHARD_STRIDE = 2432
N_ALIGNED = 9856
SEG = {
    "q": (0, 1024, 5120), "k": (1024, 256, 9216), "v": (1280, 256, 9472), "z_att": (1536, 1024, 6144),
    "z_ssm": (2560, 2048, 0), "xbc": (4608, 3072, 2048), "dt": (7680, 128, 9728),
    "gate_att": (7808, 1024, 7168), "gate_ssm": (8832, 1024, 8192),
}
DT_STORED_START = 7680
DT_PAD = LANES - SSM_HEADS


def _act_col(aligned_col):
    for a0, w, p0 in SEG.values():
        if a0 <= aligned_col < a0 + w:
            return p0 + aligned_col - a0
    raise ValueError(aligned_col)


def _call(body, *, name, out_shape, in_specs, out_specs, grid=(), scratch=(), sem=None, aliases=None):
    return pl.pallas_call(
        body, out_shape=out_shape, grid=grid, in_specs=in_specs, out_specs=out_specs, scratch_shapes=list(scratch),
        name=name, input_output_aliases=aliases or {},
        compiler_params=pltpu.CompilerParams(dimension_semantics=sem, vmem_limit_bytes=VMEM_LIMIT))


def _full(shape):
    n = len(shape)
    return pl.BlockSpec(shape, lambda *_: (0,) * n)


def _chip_index():
    return lax.axis_index("x") * 2 + lax.axis_index("y")


def _silu(z):
    return z * jax.nn.sigmoid(z)


def _rms(x, g):
    return x * lax.rsqrt(jnp.mean(x * x, axis=-1, keepdims=True) + NORM_EPS) * g


def _peer(mask):
    x, y, c = lax.axis_index("x"), lax.axis_index("y"), lax.axis_index("c")
    return ((1 - x) if mask & 4 else x, (1 - y) if mask & 2 else y, (1 - c) if mask & 1 else c)


def _me():
    return lax.axis_index("x"), lax.axis_index("y"), lax.axis_index("c")


def _chip_of(dev):
    return 2 * dev[0] + dev[1]


CHIP_MASKS = (4, 2, 6)
ALL_MASKS = (1, 2, 3, 4, 5, 6, 7)
SIBLING_MASK = (1,)


def _remote(src, dst, send_sem, recv_sem, dev):
    return pltpu.make_async_remote_copy(src_ref=src, dst_ref=dst, send_sem=send_sem, recv_sem=recv_sem,
                                        device_id=dev, device_id_type=pl.DeviceIdType.MESH)


class _StagedCopy:
    def __init__(self, src, stage, dst, load_sem, store_sem):
        self.load = pltpu.make_async_copy(src, stage, load_sem)
        self.store = pltpu.make_async_copy(stage, dst, store_sem)

    def start(self):
        self.load.start()
        self.load.wait()
        self.store.start()

    def wait(self):
        self.store.wait()


class DirectExchange:
    def __init__(self, arrays, pieces, masks, slot_kind, nslots, keep_own=True):
        self.arrays, self.pieces, self.masks, self.slot_kind = list(arrays), list(pieces), masks, slot_kind
        self.keep_own = keep_own
        n, nk = len(arrays), len(masks)
        shapes = [a.shape if p is None else p[1] for a, p in zip(arrays, pieces)]
        self.out_shape = [jax.ShapeDtypeStruct((nslots,) + tuple(s), a.dtype) for s, a in zip(shapes, arrays)]
        self.scratch = [pltpu.SemaphoreType.DMA((n * nk,)), pltpu.SemaphoreType.DMA((n * nk,))]
        if keep_own:
            self.scratch += [pltpu.SemaphoreType.DMA((2 * n,))] + [pltpu.VMEM(s, a.dtype) for s, a in zip(shapes, arrays)]
        self.has_mid = False

    def _copies(self, ins, outs, scratch):
        send_sems, recv_sems = scratch[:2]
        me = _me()
        slot = {"chip": _chip_of(me), "dev": 4 * me[0] + 2 * me[1] + me[2], "core": me[2]}[self.slot_kind]
        nk = len(self.masks)

        def piece(a, dev):
            return ins[a] if self.pieces[a] is None else self.pieces[a][0](ins[a], dev)

        local = []
        if self.keep_own:
            local_sems, stages = scratch[2], scratch[3:]
            local = [_StagedCopy(piece(a, me), stages[a], outs[a].at[slot], local_sems.at[2 * a], local_sems.at[2 * a + 1])
                     for a in range(len(ins))]
        remote = []
        for a in range(len(ins)):
            for ki, mask in enumerate(self.masks):
                dev = _peer(mask)
                remote.append(_remote(piece(a, dev), outs[a].at[slot], send_sems.at[a * nk + ki],
                                      recv_sems.at[a * nk + ki], dev))
        return local, remote

    def start(self, ins, outs, scratch):
        local, remote = self._copies(ins, outs, scratch)
        for cp in remote + local:
            cp.start()

    def finish(self, ins, outs, scratch):
        local, remote = self._copies(ins, outs, scratch)
        for cp in remote + local:
            cp.wait()


class TwoLevelGather:
    def __init__(self, arrays):
        self.arrays = list(arrays)
        n, nk = len(arrays), len(CHIP_MASKS)
        self.out_shape = [jax.ShapeDtypeStruct((4,) + a.shape, a.dtype) for a in arrays]
        self.scratch = ([pltpu.SemaphoreType.DMA((n * nk,)) for _ in range(4)] + [pltpu.SemaphoreType.DMA((2 * n,))]
                        + [pltpu.VMEM(a.shape, a.dtype) for a in arrays])
        self.has_mid = True

    def _copies(self, ins, outs, scratch):
        ici_send, ici_recv, fwd_send, fwd_recv, local_sems = scratch[:5]
        stages = scratch[5:]
        me = _me()
        sibling = _peer(1)
        nk = len(CHIP_MASKS)
        local, ici, fwd = [], [], []
        for a in range(len(ins)):
            half = ins[a].shape[0] // 2
            mine = pl.ds(me[2] * half, half)
            local.append(_StagedCopy(ins[a], stages[a], outs[a].at[_chip_of(me)], local_sems.at[2 * a],
                                     local_sems.at[2 * a + 1]))
            for ki, mask in enumerate(CHIP_MASKS):
                dev = _peer(mask)
                k = a * nk + ki
                ici.append(_remote(ins[a].at[mine], outs[a].at[_chip_of(me), mine], ici_send.at[k], ici_recv.at[k], dev))
                arrived = outs[a].at[_chip_of(dev), mine]
                fwd.append(_remote(arrived, arrived, fwd_send.at[k], fwd_recv.at[k], sibling))
        return local, ici, fwd

    def start(self, ins, outs, scratch):
        local, ici, _ = self._copies(ins, outs, scratch)
        for cp in ici + local:
            cp.start()

    def mid(self, ins, outs, scratch):
        _, ici, fwd = self._copies(ins, outs, scratch)
        for arrival, onward in zip(ici, fwd):
            arrival.wait_recv()
            onward.start()

    def finish(self, ins, outs, scratch):
        local, ici, fwd = self._copies(ins, outs, scratch)
        for cp in fwd:
            cp.wait_recv()
        for cp in ici + fwd:
            cp.wait_send()
        for cp in local:
            cp.wait()


class Both:
    def __init__(self, a, b):
        self.a, self.b = a, b
        self.arrays, self.out_shape = a.arrays + b.arrays, a.out_shape + b.out_shape
        self.scratch = a.scratch + b.scratch
        self.has_mid = False
        assert not (a.has_mid or b.has_mid)

    def _parts(self, ins, outs, sems):
        na, sa = len(self.a.arrays), len(self.a.scratch)
        return (ins[:na], outs[:na], sems[:sa]), (ins[na:], outs[na:], sems[sa:])

    def start(self, ins, outs, sems):
        pa, pb = self._parts(ins, outs, sems)
        self.a.start(*pa)
        self.b.start(*pb)

    def finish(self, ins, outs, sems):
        pa, pb = self._parts(ins, outs, sems)
        self.a.finish(*pa)
        self.b.finish(*pb)


_ANY = pl.BlockSpec(memory_space=pl.ANY)


def run_comm(name, comm):
    n = len(comm.arrays)

    def body(*refs):
        ins, outs, sems = refs[:n], refs[n:2 * n], refs[2 * n:]
        comm.start(ins, outs, sems)
        if comm.has_mid:
            comm.mid(ins, outs, sems)
        comm.finish(ins, outs, sems)

    return pl.pallas_call(body, name=name, out_shape=comm.out_shape, in_specs=[_ANY] * n, out_specs=[_ANY] * n,
                          scratch_shapes=comm.scratch,
                          compiler_params=pltpu.CompilerParams(vmem_limit_bytes=VMEM_LIMIT))(*comm.arrays)


def _call_with_comm(body, comm, steps, args, *, name, out_shape, in_specs, out_specs, grid, scratch=()):
    ni, no, ns, nc = len(in_specs), len(out_specs), len(scratch), len(comm.arrays)

    def full_body(*refs):
        ins, cins = refs[:ni], refs[ni:ni + nc]
        outs, couts = refs[ni + nc:ni + nc + no], refs[ni + nc + no:ni + 2 * nc + no]
        scr, csems = refs[ni + 2 * nc + no:ni + 2 * nc + no + ns], refs[ni + 2 * nc + no + ns:]
        first, middle, last = steps()
        pl.when(first)(lambda: comm.start(cins, couts, csems))
        if comm.has_mid:
            pl.when(middle)(lambda: comm.mid(cins, couts, csems))
        body(*ins, *outs, *scr)
        pl.when(last)(lambda: comm.finish(cins, couts, csems))

    res = pl.pallas_call(
        full_body, name=name, out_shape=list(out_shape) + comm.out_shape, grid=grid,
        in_specs=list(in_specs) + [_ANY] * nc, out_specs=list(out_specs) + [_ANY] * nc,
        scratch_shapes=list(scratch) + comm.scratch,
        compiler_params=pltpu.CompilerParams(dimension_semantics=("arbitrary",) * len(grid),
                                             vmem_limit_bytes=VMEM_LIMIT))(*args, *comm.arrays)
    return res[:no], res[no:]


def _shift_right(p, chip, col):
    if chip == 0:
        return p
    if chip < 3:
        return pltpu.roll(p, 8 * chip, 1)
    tail0 = DT_STORED_START + SSM_HEADS - 3 * SHARD_W
    head = pltpu.roll(p, 24, 1)
    tail = pltpu.roll(p, 24 + DT_PAD, 1)
    return jnp.where(col < tail0 + 24, head, jnp.where(col >= tail0 + 24 + DT_PAD, tail, 0.0))


def _shift_left(g, chip, col):
    if chip == 0:
        return g
    if chip < 3:
        return pltpu.roll(g, PACK_W - 8 * chip, 1)
    tail0 = DT_STORED_START + SSM_HEADS - 3 * SHARD_W
    return jnp.where(col < tail0, pltpu.roll(g, PACK_W - 24, 1), pltpu.roll(g, PACK_W - 24 - DT_PAD, 1))


def pack_w_in(w):
    rows = BLOCK

    def body(w_ref, o_ref, pad_ref):
        chip = _chip_index()
        pad_ref[...] = jnp.zeros_like(pad_ref)
        pad_ref[:, 0:SHARD_W] = w_ref[...]
        p = pad_ref[...]
        col = lax.broadcasted_iota(jnp.int32, p.shape, 1)
        for cv in range(4):
            @pl.when(chip == cv)
            def _():
                o_ref[...] = _shift_right(p, cv, col).astype(BF16)

    return _call(body, name="pack_w_in", grid=(D_MODEL // rows,),
                 in_specs=[pl.BlockSpec((rows, SHARD_W), lambda i: (i, 0))],
                 out_specs=pl.BlockSpec((rows, PACK_W), lambda i: (i, 0)),
                 out_shape=jax.ShapeDtypeStruct((D_MODEL, PACK_W), BF16),
                 scratch=[pltpu.VMEM((rows, PACK_W), F32)], sem=("parallel",))(w)


def _tile_runs():
    runs, fix = [], []
    for t in range(N_ALIGNED // LANES):
        s = min(t // 19, 3)
        j = t - 19 * s
        p = _act_col(t * LANES)
        if runs and runs[-1][1] == s and runs[-1][0] + runs[-1][3] == p and runs[-1][2] + runs[-1][3] == j * LANES:
            runs[-1][3] += LANES
        else:
            runs.append([p, s, j * LANES, LANES])
        if j == 0 and s > 0:
            fix.append((p, s - 1))
    return runs, fix


def unpack_w_in(bg):
    rows = BLOCK
    runs, fix = _tile_runs()

    def body(b_ref, o_ref):
        for p, s, j, w in runs:
            o_ref[:, p:p + w] = b_ref[s, :, j:j + w]
        for p, s in fix:
            o_ref[:, p:p + LANES] = o_ref[:, p:p + LANES] + b_ref[s, :, SHARD_STRIDE:PACK_W]

    return _call(body, name="unpack_w_in", grid=(D_MODEL // rows,),
                 in_specs=[pl.BlockSpec((4, rows, PACK_W), lambda i: (0, i, 0))],
                 out_specs=pl.BlockSpec((rows, N_ALIGNED), lambda i: (i, 0)),
                 out_shape=jax.ShapeDtypeStruct((D_MODEL, N_ALIGNED), BF16), sem=("parallel",))(bg)


def pack_grad_w_in(dw):
    rows = BLOCK

    def body(g_ref, o_ref):
        for s in range(4):
            for j in range(PACK_W // LANES):
                p = _act_col((19 * s + j) * LANES)
                o_ref[s, :, j * LANES:(j + 1) * LANES] = g_ref[:, p:p + LANES].astype(BF16)

    return _call(body, name="pack_grad_w_in", grid=(D_MODEL // rows,),
                 in_specs=[pl.BlockSpec((rows, N_ALIGNED), lambda i: (i, 0))],
                 out_specs=pl.BlockSpec((4, rows, PACK_W), lambda i: (0, i, 0)),
                 out_shape=jax.ShapeDtypeStruct((4, D_MODEL, PACK_W), BF16), sem=("parallel",))(dw)


def _adamw(w, g, m, v):
    m = ADAM_B1 * m + (1.0 - ADAM_B1) * g
    v = ADAM_B2 * v + (1.0 - ADAM_B2) * jnp.square(g)
    m_hat = m / (1.0 - ADAM_B1 ** ADAM_STEP)
    v_hat = v / (1.0 - ADAM_B2 ** ADAM_STEP)
    delta = -ADAM_LR * (m_hat / (jnp.sqrt(v_hat) + ADAM_EPS) + ADAM_WD * w)
    return delta, m, v


def adamw_w_in(g_packed, w, m, v):
    rows = BLOCK

    def body(g_ref, w_ref, m_ref, v_ref, go_ref, d_ref, mo_ref, vo_ref, tmp_ref):
        chip = _chip_index()
        gp = g_ref[...]
        col = lax.broadcasted_iota(jnp.int32, gp.shape, 1)
        for cv in range(4):
            @pl.when(chip == cv)
            def _():
                tmp_ref[...] = _shift_left(gp, cv, col)
        g = tmp_ref[:, 0:SHARD_W]
        d, mn, vn = _adamw(w_ref[...], g, m_ref[...], v_ref[...])
        go_ref[...] = g
        d_ref[...] = d
        mo_ref[...] = mn
        vo_ref[...] = vn

    spec = pl.BlockSpec((rows, SHARD_W), lambda i: (i, 0))
    shp = jax.ShapeDtypeStruct((D_MODEL, SHARD_W), F32)
    return _call(body, name="adamw_w_in", grid=(D_MODEL // rows,),
                 in_specs=[pl.BlockSpec((rows, PACK_W), lambda i: (i, 0)), spec, spec, spec],
                 out_specs=[spec] * 4, out_shape=[shp] * 4,
                 scratch=[pltpu.VMEM((rows, PACK_W), F32)], sem=("parallel",))(g_packed, w, m, v)


def adamw_rows(name, g, w, m, v):
    r, c = g.shape
    rows = min(r, BLOCK)

    def body(g_ref, w_ref, m_ref, v_ref, d_ref, mo_ref, vo_ref):
        d_ref[...], mo_ref[...], vo_ref[...] = _adamw(w_ref[...], g_ref[...], m_ref[...], v_ref[...])

    spec = pl.BlockSpec((rows, c), lambda i: (i, 0))
    shp = jax.ShapeDtypeStruct((r, c), F32)
    return _call(body, name=name, grid=(r // rows,), in_specs=[spec] * 4, out_specs=[spec] * 3, out_shape=[shp] * 3,
                 sem=("parallel",))(g, w, m, v)


def adamw_small(gs, ws, ms, vs):
    n = len(gs)

    def body(*refs):
        g, w, m, v = refs[:n], refs[n:2 * n], refs[2 * n:3 * n], refs[3 * n:4 * n]
        outs = refs[4 * n:]
        for i in range(n):
            d, mn, vn = _adamw(w[i][...], g[i][...], m[i][...], v[i][...])
            outs[3 * i][...] = d
            outs[3 * i + 1][...] = mn
            outs[3 * i + 2][...] = vn

    specs = [_full(a.shape) for a in gs]
    res = _call(body, name="adamw_small", in_specs=specs * 4,
                out_specs=[s for s in specs for _ in range(3)],
                out_shape=[jax.ShapeDtypeStruct(a.shape, F32) for a in gs for _ in range(3)])(*gs, *ws, *ms, *vs)
    return [tuple(res[3 * i:3 * i + 3]) for i in range(n)]


def sum_slots(name, r):
    s, rr, c = r.shape
    rows = min(rr, BLOCK)

    def body(r_ref, o_ref):
        acc = r_ref[0].astype(F32)
        for k in range(1, s):
            acc = acc + r_ref[k].astype(F32)
        o_ref[...] = acc

    return _call(body, name=name, grid=(rr // rows,), in_specs=[pl.BlockSpec((s, rows, c), lambda i: (0, i, 0))],
                 out_specs=pl.BlockSpec((rows, c), lambda i: (i, 0)), out_shape=jax.ShapeDtypeStruct((rr, c), F32),
                 sem=("parallel",))(r)


def sum_pair(partial, from_sibling):
    s, _, rr, cols = partial.shape

    def body(p_ref, r_ref, o_ref):
        c = lax.axis_index("c")
        o_ref[...] = (p_ref[c].astype(F32) + r_ref[1 - c].astype(F32)).astype(BF16)

    return _call(body, name="sum_pair", grid=(s, rr // BLOCK),
                 in_specs=[pl.BlockSpec((None, 2, BLOCK, cols), lambda k, i: (k, 0, i, 0)),
                           pl.BlockSpec((2, None, BLOCK, cols), lambda k, i: (0, k, i, 0))],
                 out_specs=pl.BlockSpec((None, BLOCK, cols), lambda k, i: (k, i, 0)),
                 out_shape=jax.ShapeDtypeStruct((s, rr, cols), BF16), sem=("parallel", "parallel"))(partial, from_sibling)


def _col_tile(n, k):
    if n % 896 == 0 and k <= 1024:
        return 896
    return min(n, 512)


def mm_nn(name, x, w, out_dtype=F32, comm=None):
    m, k = x.shape
    n = w.shape[1]
    tn = _col_tile(n, k)
    steps = n // tn

    def body(x_ref, w_ref, o_ref):
        o_ref[...] = jnp.dot(x_ref[...].astype(BF16), w_ref[...].astype(BF16),
                             preferred_element_type=F32).astype(out_dtype)

    kw = dict(name=name, grid=(steps,), in_specs=[_full((m, k)), pl.BlockSpec((k, tn), lambda j: (0, j))])
    if comm is None:
        return _call(body, out_specs=pl.BlockSpec((m, tn), lambda j: (0, j)),
                     out_shape=jax.ShapeDtypeStruct((m, n), out_dtype), sem=("parallel",), **kw)(x, w)

    def at():
        j = pl.program_id(0)
        return j == 0, j == (3 * steps) // 4, j == steps - 1

    (res,), moved = _call_with_comm(body, comm, at, (x, w), out_specs=[pl.BlockSpec((m, tn), lambda j: (0, j))],
                                    out_shape=[jax.ShapeDtypeStruct((m, n), out_dtype)], **kw)
    return res, moved


def mm_nt(name, dy, w, out_dtype=F32, comm=None):
    m, n = dy.shape
    k = w.shape[0]
    tm = m // 2
    tn = _col_tile(n, k)
    steps = n // tn

    def at():
        i, j = pl.program_id(0), pl.program_id(1)
        return (i == 0) & (j == 0), (i == 0) & (j == steps - 1), (i == m // tm - 1) & (j == steps - 1)

    def body(dy_ref, w_ref, o_ref, acc_ref):
        j = pl.program_id(1)

        @pl.when(j == 0)
        def _():
            acc_ref[...] = jnp.zeros_like(acc_ref)

        acc_ref[...] += lax.dot_general(dy_ref[...].astype(BF16), w_ref[...].astype(BF16), (((1,), (1,)), ((), ())),
                                        preferred_element_type=F32)

        @pl.when(j == steps - 1)
        def _():
            o_ref[...] = acc_ref[...].astype(out_dtype)

    kw = dict(name=name, grid=(m // tm, steps), scratch=[pltpu.VMEM((tm, k), F32)],
              in_specs=[pl.BlockSpec((tm, tn), lambda i, j: (i, j)), pl.BlockSpec((k, tn), lambda i, j: (0, j))])
    if comm is None:
        return _call(body, out_specs=pl.BlockSpec((tm, k), lambda i, j: (i, 0)),
                     out_shape=jax.ShapeDtypeStruct((m, k), out_dtype), sem=("parallel", "arbitrary"), **kw)(dy, w)
    (res,), moved = _call_with_comm(body, comm, at, (dy, w), out_specs=[pl.BlockSpec((tm, k), lambda i, j: (i, 0))],
                                    out_shape=[jax.ShapeDtypeStruct((m, k), out_dtype)], **kw)
    return res, moved


def mm_tn(name, x, dy):
    m, k = x.shape
    n = dy.shape[1]
    tm = m // 2
    tn = _col_tile(n, k)

    def body(x_ref, dy_ref, o_ref, acc_ref):
        part = lax.dot_general(x_ref[...].astype(BF16), dy_ref[...].astype(BF16), (((0,), (0,)), ((), ())),
                               preferred_element_type=F32)

        @pl.when(pl.program_id(1) == 0)
        def _():
            acc_ref[...] = part

        @pl.when(pl.program_id(1) == 1)
        def _():
            o_ref[...] = (acc_ref[...] + part).astype(BF16)

    return _call(body, name=name, grid=(n // tn, 2),
                 in_specs=[pl.BlockSpec((tm, k), lambda i, j: (j, 0)), pl.BlockSpec((tm, tn), lambda i, j: (j, i))],
                 out_specs=pl.BlockSpec((k, tn), lambda i, j: (0, i)), out_shape=jax.ShapeDtypeStruct((k, n), BF16),
                 scratch=[pltpu.VMEM((k, tn), F32)], sem=("parallel", "arbitrary"))(x, dy)


def _row_spec(width, col_block=0):
    return pl.BlockSpec((BLOCK, width), lambda i: (i, col_block))


def _x_spec():
    return pl.BlockSpec((None, BLOCK, D_MODEL), lambda i: (0, jnp.maximum(i - 1, 0), 0))


def prep(x, meta, g_pre):
    nb = x.shape[1] // BLOCK + 1

    def body(x_ref, meta_ref, g_ref, h_ref, u_ref):
        i = pl.program_id(0)

        @pl.when(i == 0)
        def _():
            h_ref[0:PAD_ROWS, :] = jnp.zeros((PAD_ROWS, D_MODEL), F32)
            h_ref[PAD_ROWS:BLOCK, :] = meta_ref[...]

        @pl.when(i > 0)
        def _():
            h_ref[...] = x_ref[...]

        u_ref[...] = _rms(h_ref[...], g_ref[...]).astype(BF16)

    return _call(body, name="prep", grid=(nb,), in_specs=[_x_spec(), _full((N_META, D_MODEL)), _full((1, D_MODEL))],
                 out_specs=[_row_spec(D_MODEL), _row_spec(D_MODEL)],
                 out_shape=[jax.ShapeDtypeStruct((nb * BLOCK, D_MODEL), F32),
                            jax.ShapeDtypeStruct((nb * BLOCK, D_MODEL), BF16)], sem=("parallel",))(x, meta, g_pre)


def prep_bwd(h, du, dres, g_pre):
    nb = h.shape[0] // BLOCK

    def body(h_ref, du_ref, dres_ref, g_ref, gx_ref, gm_ref, gg_ref):
        i = pl.program_id(0)
        _, vjp = jax.vjp(_rms, h_ref[...], g_ref[...])
        dh, dg = vjp(du_ref[...])

        @pl.when(i == 0)
        def _():
            gm_ref[...] = dh[PAD_ROWS:BLOCK, :]
            gg_ref[...] = dg

        @pl.when(i > 0)
        def _():
            gg_ref[...] += dg

        gx_ref[...] = dh + dres_ref[...]

    return _call(body, name="prep_bwd", grid=(nb,),
                 in_specs=[_row_spec(D_MODEL), _row_spec(D_MODEL), _row_spec(D_MODEL), _full((1, D_MODEL))],
                 out_specs=[_x_spec(), _full((N_META, D_MODEL)), _full((1, D_MODEL))],
                 out_shape=[jax.ShapeDtypeStruct((1, (nb - 1) * BLOCK, D_MODEL), F32),
                            jax.ShapeDtypeStruct((N_META, D_MODEL), F32), jax.ShapeDtypeStruct((1, D_MODEL), F32)],
                 sem=("arbitrary",))(h, du, dres, g_pre)


GROUP_W = SSM_INNER // SSM_GROUPS


def _gated_norm(y, z, g):
    t = y * _silu(z)
    return t * lax.rsqrt(jnp.mean(t * t, axis=-1, keepdims=True) + NORM_EPS) * g


def _gated_norm_groups(y, z, g):
    groups = [slice(k * GROUP_W, (k + 1) * GROUP_W) for k in range(SSM_GROUPS)]
    return jnp.concatenate([_gated_norm(y[:, s], z[:, s], g[:, s]) for s in groups], axis=1)


def _merge(ga, gs, ya, ys):
    return jax.nn.sigmoid(ga) * ya + jax.nn.sigmoid(gs) * ys


GATE_ATT_BLOCK = SEG["gate_att"][2] // D_MODEL
GATE_SSM_BLOCK = SEG["gate_ssm"][2] // D_MODEL


def _row_loss(out, g_post, x, target):
    diff = x + _rms(out, g_post) - target
    return 0.5 * jnp.sum(diff * diff) / D_MODEL


def tail(y_ssd, proj, a_att, x, target, woa, wos, wo, g_norm, g_post):
    nb = y_ssd.shape[0] // BLOCK
    rows = nb * BLOCK

    def body(y_ref, z_ref, ga_ref, gs_ref, a_ref, x_ref, t_ref, woa_ref, wos_ref, wo_ref, gn_ref, gp_ref,
             yn_ref, mg_ref, dout_ref, dya_ref, dys_ref, da_ref, dy_ref, dz_ref, dga_ref, dgs_ref, dres_ref,
             loss_ref, dgp_ref, dgn_ref):
        i = pl.program_id(0)
        yn, norm_vjp = jax.vjp(_gated_norm_groups, y_ref[...], z_ref[...], gn_ref[...])
        yn16 = yn.astype(BF16)
        y_ssm = jnp.dot(yn16, wos_ref[...], preferred_element_type=F32)
        y_att = jnp.dot(a_ref[...], woa_ref[...], preferred_element_type=F32)
        merged, merge_vjp = jax.vjp(_merge, ga_ref[...], gs_ref[...], y_att, y_ssm)
        merged16 = merged.astype(BF16)
        out = jnp.dot(merged16, wo_ref[...], preferred_element_type=F32)
        loss, loss_vjp = jax.vjp(_row_loss, out, gp_ref[...], x_ref[...], t_ref[...])
        counted = jnp.where(i > 0, 1.0, 0.0)
        dout, dgp, dres, _ = loss_vjp(counted)
        dout16 = dout.astype(BF16)
        dmerged = lax.dot_general(dout16, wo_ref[...], _NT, preferred_element_type=F32)
        dga, dgs, dya, dys = merge_vjp(dmerged)
        dya16, dys16 = dya.astype(BF16), dys.astype(BF16)
        dyn = lax.dot_general(dys16, wos_ref[...], _NT, preferred_element_type=F32)
        dy, dz, dgn = norm_vjp(dyn)

        yn_ref[...] = yn16
        mg_ref[...] = merged16
        dout_ref[...] = dout16
        dya_ref[...] = dya16
        dys_ref[...] = dys16
        da_ref[...] = lax.dot_general(dya16, woa_ref[...], _NT, preferred_element_type=F32)
        dy_ref[...] = dy
        dz_ref[...] = dz.astype(BF16)
        dga_ref[...] = dga.astype(BF16)
        dgs_ref[...] = dgs.astype(BF16)
        dres_ref[...] = dres

        @pl.when(i == 0)
        def _():
            loss_ref[...] = jnp.zeros_like(loss_ref)
            dgp_ref[...] = jnp.zeros_like(dgp_ref)
            dgn_ref[...] = jnp.zeros_like(dgn_ref)

        loss_ref[...] += loss * counted
        dgp_ref[...] += dgp
        dgn_ref[...] += dgn

    wide, narrow = _row_spec(SSM_INNER), _row_spec(D_MODEL)
    resident = pl.BlockSpec(memory_space=pltpu.VMEM)
    bf = lambda w: jax.ShapeDtypeStruct((rows, w), BF16)
    f32 = lambda w: jax.ShapeDtypeStruct((rows, w), F32)
    return _call(body, name="tail", grid=(nb,),
                 in_specs=[wide, wide, _row_spec(D_MODEL, GATE_ATT_BLOCK), _row_spec(D_MODEL, GATE_SSM_BLOCK), narrow,
                           _x_spec(), _x_spec(), resident, resident, resident, _full((1, SSM_INNER)),
                           _full((1, D_MODEL))],
                 out_specs=[wide, narrow, narrow, narrow, narrow, narrow, wide, wide, narrow, narrow, narrow,
                            _full((8, LANES)), _full((1, D_MODEL)), _full((1, SSM_INNER))],
                 out_shape=[bf(SSM_INNER), bf(D_MODEL), bf(D_MODEL), bf(D_MODEL), bf(D_MODEL), f32(D_MODEL),
                            f32(SSM_INNER), bf(SSM_INNER), bf(D_MODEL), bf(D_MODEL), f32(D_MODEL),
                            jax.ShapeDtypeStruct((8, LANES), F32), jax.ShapeDtypeStruct((1, D_MODEL), F32),
                            jax.ShapeDtypeStruct((1, SSM_INNER), F32)],
                 sem=("arbitrary",))(y_ssd, proj, proj, proj, a_att, x, target, woa, wos, wo, g_norm, g_post)


_NT = (((1,), (1,)), ((), ()))
ALIBI_SLOPES = tuple(2.0 ** (-8.0 * (h + 1) / ATT_Q_HEADS) for h in range(ATT_Q_HEADS))
KV_WIDTH = ATT_KV_HEADS * HEAD_DIM
Q_BLOCK = SEG["q"][2] // D_MODEL
Z_ATT_BLOCK = SEG["z_att"][2] // D_MODEL
K_BLOCK = SEG["k"][2] // KV_WIDTH
V_BLOCK = SEG["v"][2] // KV_WIDTH
META_ROW_BLOCK = PAD_ROWS // N_META


@jax.custom_vjp
def _swap_halves(x):
    return pltpu.roll(x, HEAD_DIM, 1)


_swap_halves.defvjp(lambda x: (pltpu.roll(x, HEAD_DIM, 1), None), lambda _, g: (pltpu.roll(g, HEAD_DIM, 1),))


def _both_halves(t, half):
    first = lax.broadcasted_iota(jnp.int32, t.shape, 1) < HEAD_DIM
    sw = _swap_halves(t)
    return jnp.where(first, t, sw) if half == 0 else jnp.where(first, sw, t)


def _attn_rows(q, z, kp, kc, vp, vc, km, vm, sinks, n):
    rows = ATT_GROUP * BLOCK
    i = lax.broadcasted_iota(jnp.int32, (rows, BLOCK), 0) & (BLOCK - 1)
    j = lax.broadcasted_iota(jnp.int32, (rows, BLOCK), 1)
    rel_c = (i - j).astype(F32)
    rel_p = rel_c + float(BLOCK)
    nv = jnp.zeros((rows, BLOCK), jnp.int32) + n
    ok_c = (i >= j) & (nv >= 1)
    ok_p = (j > i) & (nv >= 2)
    im = lax.broadcasted_iota(jnp.int32, (rows, N_META), 0) & (BLOCK - 1)
    jm = lax.broadcasted_iota(jnp.int32, (rows, N_META), 1)
    ok_m = ((jnp.zeros((rows, N_META), jnp.int32) + n) >= 1) | (im >= PAD_ROWS + jm)
    first = lax.broadcasted_iota(jnp.int32, (BLOCK, LANES), 1) < HEAD_DIM
    neg = -jnp.inf
    outs = []
    for kv in range(ATT_KV_HEADS):
        tile, half = divmod(kv, 2)
        lanes = slice(tile * LANES, (tile + 1) * LANES)
        kc2, kp2, km2 = (_both_halves(t[:, lanes], half).astype(BF16) for t in (kc, kp, km))
        vc2, vp2, vm2 = (_both_halves(t[:, lanes], half).astype(BF16) for t in (vc, vp, vm))
        qs, slope, sk = [], [], []
        for pair in range(ATT_GROUP // 2):
            c0 = (kv * ATT_GROUP + 2 * pair) * HEAD_DIM
            qp = q[:, c0:c0 + LANES] * HEAD_DIM ** -0.5
            qs += [jnp.where(first, qp, 0.0), jnp.where(first, 0.0, qp)]
        for g in range(ATT_GROUP):
            slope.append(jnp.full((BLOCK, 1), ALIBI_SLOPES[kv * ATT_GROUP + g], F32))
            sk.append(jnp.broadcast_to(sinks[kv * ATT_GROUP + g], (BLOCK, 1)))
        qs = jnp.concatenate(qs, axis=0).astype(BF16)
        slope = jnp.concatenate(slope, axis=0)
        sk = jnp.concatenate(sk, axis=0)
        sc = jnp.where(ok_c, lax.dot_general(qs, kc2, _NT, preferred_element_type=F32) - slope * rel_c, neg)
        sp = jnp.where(ok_p, lax.dot_general(qs, kp2, _NT, preferred_element_type=F32) - slope * rel_p, neg)
        sm = jnp.where(ok_m, lax.dot_general(qs, km2, _NT, preferred_element_type=F32), neg)
        mx = jnp.maximum(jnp.maximum(jnp.max(sc, axis=1, keepdims=True), jnp.max(sp, axis=1, keepdims=True)),
                         jnp.maximum(jnp.max(sm, axis=1, keepdims=True), sk))
        mx = lax.stop_gradient(mx)
        ec, ep, em, es = jnp.exp(sc - mx), jnp.exp(sp - mx), jnp.exp(sm - mx), jnp.exp(sk - mx)
        den = (es + jnp.sum(ec, axis=1, keepdims=True) + jnp.sum(ep, axis=1, keepdims=True)
               + jnp.sum(em, axis=1, keepdims=True))
        inv = 1.0 / den
        o = (jnp.dot((ec * inv).astype(BF16), vc2, preferred_element_type=F32)
             + jnp.dot((ep * inv).astype(BF16), vp2, preferred_element_type=F32)
             + jnp.dot((em * inv).astype(BF16), vm2, preferred_element_type=F32))
        for pair in range(ATT_GROUP // 2):
            r0 = 2 * pair * BLOCK
            outs.append(jnp.where(first, o[r0:r0 + BLOCK], o[r0 + BLOCK:r0 + 2 * BLOCK]))
    return jnp.concatenate(outs, axis=1) * _silu(z)


def _attn_specs(nb, steps_clamped):
    def blk(t):
        return jnp.minimum(t, nb - 1) if steps_clamped else t

    wide = lambda col: pl.BlockSpec((BLOCK, D_MODEL), lambda t: (blk(t), col))
    cur = lambda col: pl.BlockSpec((BLOCK, KV_WIDTH), lambda t: (blk(t), col))
    prev = lambda col: pl.BlockSpec((BLOCK, KV_WIDTH), lambda t: (jnp.maximum(blk(t) - 1, 0), col))
    meta = lambda col: pl.BlockSpec((N_META, KV_WIDTH), lambda t: (META_ROW_BLOCK, col))
    sinks = pl.BlockSpec((ATT_Q_HEADS, 1, 1), lambda t: (0, 0, 0))
    return [wide(Q_BLOCK), wide(Z_ATT_BLOCK), prev(K_BLOCK), cur(K_BLOCK), prev(V_BLOCK), cur(V_BLOCK),
            meta(K_BLOCK), meta(V_BLOCK), sinks]


def attn_fwd(proj, sinks):
    nb = proj.shape[0] // BLOCK

    def body(q_ref, z_ref, kp_ref, kc_ref, vp_ref, vc_ref, km_ref, vm_ref, sk_ref, o_ref):
        o_ref[...] = _attn_rows(q_ref[...], z_ref[...], kp_ref[...], kc_ref[...], vp_ref[...], vc_ref[...],
                                km_ref[...], vm_ref[...], tuple(sk_ref[h] for h in range(ATT_Q_HEADS)),
                                pl.program_id(0)).astype(BF16)

    return _call(body, name="attn_fwd", grid=(nb,), in_specs=_attn_specs(nb, False), out_specs=_row_spec(D_MODEL),
                 out_shape=jax.ShapeDtypeStruct((nb * BLOCK, D_MODEL), BF16), sem=("parallel",))(*([proj] * 8), sinks)


def attn_bwd(da, proj, sinks):
    nb = proj.shape[0] // BLOCK
    last = nb - 1
    wide = pl.BlockSpec((BLOCK, D_MODEL), lambda t: (jnp.minimum(t, last), 0))
    done = pl.BlockSpec((BLOCK, KV_WIDTH), lambda t: (jnp.maximum(t - 1, 0), 0))
    meta = _full((N_META, KV_WIDTH))
    par = _full((ATT_Q_HEADS, 1, 1))

    def body(da_ref, q_ref, z_ref, kp_ref, kc_ref, vp_ref, vc_ref, km_ref, vm_ref, sk_ref,
             dq_ref, dz_ref, dk_ref, dv_ref, dkm_ref, dvm_ref, dsk_ref, ck_ref, cv_ref):
        t = pl.program_id(0)

        @pl.when(t == 0)
        def _():
            ck_ref[...] = jnp.zeros_like(ck_ref)
            cv_ref[...] = jnp.zeros_like(cv_ref)
            dkm_ref[...] = jnp.zeros_like(dkm_ref)
            dvm_ref[...] = jnp.zeros_like(dvm_ref)
            dsk_ref[...] = jnp.zeros_like(dsk_ref)

        @pl.when(t < nb)
        def _():
            def f(q, z, kp, kc, vp, vc, km, vm, sk):
                return _attn_rows(q, z, kp, kc, vp, vc, km, vm, sk, t)

            _, vjp = jax.vjp(f, q_ref[...], z_ref[...], kp_ref[...], kc_ref[...], vp_ref[...], vc_ref[...],
                             km_ref[...], vm_ref[...], tuple(sk_ref[h] for h in range(ATT_Q_HEADS)))
            dq, dz, dkp, dkc, dvp, dvc, dkm, dvm, dsk = vjp(da_ref[...])
            dq_ref[...] = dq.astype(BF16)
            dz_ref[...] = dz.astype(BF16)
            for h in range(ATT_Q_HEADS):
                dsk_ref[h] += dsk[h]
            dk_ref[...] = ck_ref[...] + dkp
            dv_ref[...] = cv_ref[...] + dvp
            ck_ref[...] = dkc
            cv_ref[...] = dvc
            dkm_ref[...] += dkm
            dvm_ref[...] += dvm

        @pl.when(t == nb)
        def _():
            dk_ref[...] = ck_ref[...]
            dv_ref[...] = cv_ref[...]

    rows = nb * BLOCK
    return _call(body, name="attn_bwd", grid=(nb + 1,), in_specs=[wide] + _attn_specs(nb, True),
                 out_specs=[wide, wide, done, done, meta, meta, par],
                 out_shape=[jax.ShapeDtypeStruct((rows, D_MODEL), BF16), jax.ShapeDtypeStruct((rows, D_MODEL), BF16),
                            jax.ShapeDtypeStruct((rows, KV_WIDTH), F32), jax.ShapeDtypeStruct((rows, KV_WIDTH), F32),
                            jax.ShapeDtypeStruct((N_META, KV_WIDTH), F32), jax.ShapeDtypeStruct((N_META, KV_WIDTH), F32),
                            jax.ShapeDtypeStruct(sinks.shape, F32)],
                 scratch=[pltpu.VMEM((BLOCK, KV_WIDTH), F32), pltpu.VMEM((BLOCK, KV_WIDTH), F32)],
                 sem=("arbitrary",))(da, *([proj] * 8), sinks)


XBC_BLOCK0 = SEG["xbc"][2] // D_MODEL
CONV_COL_BLOCKS = CONV_DIM // D_MODEL
DT_TILE = SEG["dt"][2] // LANES


HALO = 8
HALOS_PER_BLOCK = BLOCK // HALO


def _shift_rows(cur, before, j):
    if j == 0:
        return cur
    n = cur.shape[0]
    row = lax.broadcasted_iota(jnp.int32, cur.shape, 0)
    head = pltpu.roll(before, j, 0)
    if n > HALO:
        head = jnp.concatenate([head, jnp.zeros((n - HALO, cur.shape[1]), cur.dtype)], axis=0)
    return jnp.where(row >= j, pltpu.roll(cur, j, 0), head)


def _conv_pre(cur, before, w_ref, b_ref):
    pre = b_ref[...] + w_ref[CONV_WIDTH - 1:CONV_WIDTH, :] * cur
    for k in range(CONV_WIDTH - 1):
        pre = pre + w_ref[k:k + 1, :] * _shift_rows(cur, before, CONV_WIDTH - 1 - k)
    return pre


def _conv_specs(nb, col0=0):
    first = XBC_BLOCK0 + col0
    cur = pl.BlockSpec((BLOCK, D_MODEL), lambda j, i: (i, first + j))
    before = pl.BlockSpec((HALO, D_MODEL), lambda j, i: (jnp.maximum(i * HALOS_PER_BLOCK - 1, 0), first + j))
    after = pl.BlockSpec((HALO, D_MODEL), lambda j, i: (jnp.minimum(i + 1, nb - 1) * HALOS_PER_BLOCK, first + j))
    return cur, before, after


def _valid_rows(i):
    row = lax.broadcasted_iota(jnp.int32, (BLOCK, D_MODEL), 0)
    return jnp.maximum((row >= PAD_ROWS).astype(F32), jnp.where(i > 0, 1.0, 0.0))


def conv_fwd(proj, conv_w, conv_b):
    nb = proj.shape[0] // BLOCK
    cur, before, _ = _conv_specs(nb)

    def body(c_ref, p_ref, w_ref, b_ref, o_ref):
        i = pl.program_id(1)
        pre = _conv_pre(c_ref[...], p_ref[...] * jnp.where(i > 0, 1.0, 0.0), w_ref, b_ref)
        o_ref[...] = _silu(pre) * _valid_rows(i)

    return _call(body, name="conv_fwd", grid=(CONV_COL_BLOCKS, nb),
                 in_specs=[cur, before, pl.BlockSpec((CONV_WIDTH, D_MODEL), lambda j, i: (0, j)),
                           pl.BlockSpec((1, D_MODEL), lambda j, i: (0, j))],
                 out_specs=pl.BlockSpec((BLOCK, D_MODEL), lambda j, i: (i, j)),
                 out_shape=jax.ShapeDtypeStruct((nb * BLOCK, CONV_DIM), F32),
                 sem=("parallel", "parallel"))(proj, proj, conv_w, conv_b)


def conv_bwd(name, dparts, col0, proj, conv_w, conv_b):
    nb = proj.shape[0] // BLOCK
    last = nb - 1
    ncol = sum(d.shape[1] for d in dparts) // D_MODEL
    np_ = len(dparts)
    cur, before, after = _conv_specs(nb, col0)
    dcur = [pl.BlockSpec((BLOCK, d.shape[1] // ncol), lambda j, i: (i, j)) for d in dparts]
    dafter = [pl.BlockSpec((HALO, d.shape[1] // ncol), lambda j, i: (jnp.minimum(i + 1, last) * HALOS_PER_BLOCK, j))
              for d in dparts]
    out_cur = pl.BlockSpec((BLOCK, D_MODEL), lambda j, i: (i, j))
    wspec = pl.BlockSpec((CONV_WIDTH, D_MODEL), lambda j, i: (0, col0 + j))
    bspec = pl.BlockSpec((1, D_MODEL), lambda j, i: (0, col0 + j))
    wout = pl.BlockSpec((CONV_WIDTH, D_MODEL), lambda j, i: (0, j))
    bout = pl.BlockSpec((1, D_MODEL), lambda j, i: (0, j))

    def body(*refs):
        dc_refs, da_refs = refs[:np_], refs[np_:2 * np_]
        c_ref, p_ref, a_ref, w_ref, b_ref, du_ref, dw_ref, db_ref = refs[2 * np_:]
        i = pl.program_id(1)
        row = lax.broadcasted_iota(jnp.int32, (BLOCK, D_MODEL), 0)
        curv = c_ref[...]
        beforev = p_ref[...] * jnp.where(i > 0, 1.0, 0.0)
        side_by_side = lambda rs: rs[0][...] if np_ == 1 else jnp.concatenate([r[...] for r in rs], axis=1)

        def dpre_of(pre, d):
            s = jax.nn.sigmoid(pre)
            return d * (s * (1.0 + pre * (1.0 - s)))

        dp_c = dpre_of(_conv_pre(curv, beforev, w_ref, b_ref), side_by_side(dc_refs) * _valid_rows(i))
        dp_a = dpre_of(_conv_pre(a_ref[...], curv[BLOCK - HALO:], w_ref, b_ref),
                       side_by_side(da_refs) * jnp.where(i < last, 1.0, 0.0))
        du = w_ref[CONV_WIDTH - 1:CONV_WIDTH, :] * dp_c
        for j in range(1, CONV_WIDTH):
            tail = jnp.concatenate([jnp.zeros((BLOCK - HALO, D_MODEL), F32), pltpu.roll(dp_a, HALO - j, 0)], axis=0)
            up = jnp.where(row < BLOCK - j, pltpu.roll(dp_c, BLOCK - j, 0), tail)
            du = du + w_ref[CONV_WIDTH - 1 - j:CONV_WIDTH - j, :] * up
        du_ref[...] = du.astype(BF16)

        @pl.when(i == 0)
        def _():
            dw_ref[...] = jnp.zeros_like(dw_ref)
            db_ref[...] = jnp.zeros_like(db_ref)

        for k in range(CONV_WIDTH):
            dw_ref[k:k + 1, :] += jnp.sum(dp_c * _shift_rows(curv, beforev, CONV_WIDTH - 1 - k), axis=0, keepdims=True)
        db_ref[...] += jnp.sum(dp_c, axis=0, keepdims=True)

    width = ncol * D_MODEL
    return _call(body, name=name, grid=(ncol, nb),
                 in_specs=dcur + dafter + [cur, before, after, wspec, bspec], out_specs=[out_cur, wout, bout],
                 out_shape=[jax.ShapeDtypeStruct((nb * BLOCK, width), BF16),
                            jax.ShapeDtypeStruct((CONV_WIDTH, width), F32), jax.ShapeDtypeStruct((1, width), F32)],
                 sem=("parallel", "arbitrary"))(*dparts, *dparts, proj, proj, proj, conv_w, conv_b)


def _head_expand():
    e = np.zeros((LANES, SSM_INNER), np.float32)
    for h in range(SSM_HEADS):
        e[h, h * HEAD_DIM:(h + 1) * HEAD_DIM] = 1.0
    return jnp.asarray(e, dtype=BF16)


def _softplus(x):
    return jnp.maximum(x, 0.0) + jnp.log(1.0 + jnp.exp(-jnp.abs(x)))


def _bf16_parts(x):
    hi = x.astype(BF16)
    rest = x - hi.astype(F32)
    mid = rest.astype(BF16)
    return hi, mid, (rest - mid.astype(F32)).astype(BF16)


@jax.custom_vjp
def _times_01(x, m):
    return sum(jnp.dot(p, m, preferred_element_type=F32) for p in _bf16_parts(x))


def _times_01_bwd(m, g):
    return sum(lax.dot_general(p, m, _NT, preferred_element_type=F32) for p in _bf16_parts(g)), jnp.zeros_like(m)


_times_01.defvjp(lambda x, m: (_times_01(x, m), m), _times_01_bwd)


def _causal_ones():
    l = lax.broadcasted_iota(jnp.int32, (BLOCK, BLOCK), 0)
    s = lax.broadcasted_iota(jnp.int32, (BLOCK, BLOCK), 1)
    return (l >= s).astype(BF16)


@jax.custom_vjp
def _cumsum_rows(a):
    return sum(jnp.dot(_causal_ones(), p, preferred_element_type=F32) for p in _bf16_parts(a))


def _cumsum_rows_bwd(_, g):
    tn = (((0,), (0,)), ((), ()))
    return (sum(lax.dot_general(_causal_ones(), p, tn, preferred_element_type=F32) for p in _bf16_parts(g)),)


_cumsum_rows.defvjp(lambda a: (_cumsum_rows(a), None), _cumsum_rows_bwd)


def _ssd_group(xs, dt_tile, expand, bias, alog, dsk, bg, cg, state):
    l = lax.broadcasted_iota(jnp.int32, (BLOCK, BLOCK), 0)
    s = lax.broadcasted_iota(jnp.int32, (BLOCK, BLOCK), 1)
    causal = l >= s
    first_head = s < HEAD_DIM
    dt = _softplus(dt_tile + bias)
    a = dt * (-jnp.exp(alog))
    one_row = lambda v: jnp.broadcast_to(v, (HALO, LANES))
    per_lane = _times_01(jnp.concatenate([dt, _cumsum_rows(a), one_row(jnp.sum(a, axis=0, keepdims=True)),
                                          one_row(dsk)], axis=0), expand)
    dtx, cs = per_lane[0:BLOCK], per_lane[BLOCK:2 * BLOCK]
    tot, dsk = per_lane[2 * BLOCK:2 * BLOCK + 1], per_lane[2 * BLOCK + HALO:2 * BLOCK + HALO + 1]
    bb, cb16 = bg.astype(BF16), cg.astype(BF16)
    cb = lax.dot_general(cb16, bb, _NT, preferred_element_type=F32)
    xr = xs * dtx
    y_diag = []
    for p in range(GROUP_W // LANES):
        lanes = slice(p * LANES, (p + 1) * LANES)
        c_pair = cs[:, lanes]
        c_swap = _swap_halves(c_pair)
        m = []
        for c_head in (jnp.where(first_head, c_pair, c_swap), jnp.where(first_head, c_swap, c_pair)):
            m.append(cb * jnp.exp(jnp.where(causal, c_head - c_head.T, -jnp.inf)))
        x_pair = xr[:, lanes]
        x_diag = jnp.concatenate([jnp.where(first_head, x_pair, 0.0), jnp.where(first_head, 0.0, x_pair)], axis=0)
        y_diag.append(jnp.dot(jnp.concatenate(m, axis=1).astype(BF16), x_diag.astype(BF16),
                              preferred_element_type=F32))
    st = lax.dot_general(bb, (xr * jnp.exp(tot - cs)).astype(BF16), (((0,), (0,)), ((), ())),
                         preferred_element_type=F32)
    new_state = state * jnp.exp(tot) + st
    y_off = jnp.dot(cb16, state.astype(BF16), preferred_element_type=F32) * jnp.exp(cs)
    return jnp.concatenate(y_diag, axis=1) + y_off + dsk * xs, new_state


B_TILE0 = SSM_INNER // LANES
C_TILE0 = B_TILE0 + SSM_GROUPS


def _ssd_specs(chunk):
    xs = pl.BlockSpec((BLOCK, GROUP_W), lambda c, g: (chunk(c), g))
    dt = pl.BlockSpec((BLOCK, LANES), lambda c, g: (chunk(c), DT_TILE))
    expand = pl.BlockSpec((LANES, GROUP_W), lambda c, g: (0, g))
    b = pl.BlockSpec((BLOCK, SSM_STATE), lambda c, g: (chunk(c), B_TILE0 + g))
    cc = pl.BlockSpec((BLOCK, SSM_STATE), lambda c, g: (chunk(c), C_TILE0 + g))
    par = _full((1, LANES))
    state = pl.BlockSpec((None, SSM_STATE, GROUP_W), lambda c, g: (chunk(c), 0, g))
    return xs, dt, expand, b, cc, par, state


def ssd_fwd(xbc, proj, expand, bias, alog, dsk):
    nb = xbc.shape[0] // BLOCK
    xs, dt, ex, b, cc, par, state = _ssd_specs(lambda c: c)

    def body(x_ref, dt_ref, e_ref, bi_ref, al_ref, dk_ref, b_ref, c_ref, y_ref, sp_ref, st_ref):
        g = pl.program_id(1)

        @pl.when(pl.program_id(0) == 0)
        def _():
            st_ref[g] = jnp.zeros((SSM_STATE, GROUP_W), F32)

        entering = st_ref[g]
        sp_ref[...] = entering
        y_ref[...], st_ref[g] = _ssd_group(x_ref[...], dt_ref[...], e_ref[...], bi_ref[...], al_ref[...], dk_ref[...],
                                           b_ref[...], c_ref[...], entering)

    return _call(body, name="ssd_fwd", grid=(nb, SSM_GROUPS), in_specs=[xs, dt, ex, par, par, par, b, cc],
                 out_specs=[xs, state],
                 out_shape=[jax.ShapeDtypeStruct((nb * BLOCK, SSM_INNER), F32),
                            jax.ShapeDtypeStruct((nb, SSM_STATE, SSM_INNER), F32)],
                 scratch=[pltpu.VMEM((SSM_GROUPS, SSM_STATE, GROUP_W), F32)],
                 sem=("arbitrary", "arbitrary"))(xbc, proj, expand, bias, alog, dsk, xbc, xbc)


def ssd_bwd(dy, xbc, proj, expand, bias, alog, dsk, states, comm):
    nb = xbc.shape[0] // BLOCK
    last = nb - 1
    xs, dt, ex, b, cc, par, state = _ssd_specs(lambda c: last - c)
    tile = pl.BlockSpec((BLOCK, LANES), lambda c, g: (last - c, 0))
    nspec = pl.BlockSpec((BLOCK, SSM_STATE), lambda c, g: (last - c, g))

    def body(dy_ref, x_ref, dt_ref, e_ref, bi_ref, al_ref, dk_ref, b_ref, c_ref, sp_ref,
             dx_ref, ddt_ref, db_ref, dc_ref, dbi_ref, dal_ref, ddk_ref, ds_ref, ddt_acc):
        c, g = pl.program_id(0), pl.program_id(1)

        @pl.when(c == 0)
        def _():
            ds_ref[g] = jnp.zeros((SSM_STATE, GROUP_W), F32)

        @pl.when((c == 0) & (g == 0))
        def _():
            dbi_ref[...] = jnp.zeros_like(dbi_ref)
            dal_ref[...] = jnp.zeros_like(dal_ref)
            ddk_ref[...] = jnp.zeros_like(ddk_ref)

        expand_rows = e_ref[...]

        def f(xs, dt_tile, bias, alog, dsk, bg, cg, state):
            return _ssd_group(xs, dt_tile, expand_rows, bias, alog, dsk, bg, cg, state)

        _, vjp = jax.vjp(f, x_ref[...], dt_ref[...], bi_ref[...], al_ref[...], dk_ref[...], b_ref[...], c_ref[...],
                         sp_ref[...])
        dx_ref[...], ddt, dbi, dal, ddk, db_ref[...], dc_ref[...], ds_ref[g] = vjp((dy_ref[...], ds_ref[g]))
        dbi_ref[...] += dbi
        dal_ref[...] += dal
        ddk_ref[...] += ddk

        @pl.when(g == 0)
        def _():
            ddt_acc[...] = ddt

        @pl.when(g > 0)
        def _():
            ddt_acc[...] += ddt

        @pl.when(g == SSM_GROUPS - 1)
        def _():
            ddt_ref[...] = ddt_acc[...].astype(BF16)

    def at():
        c, g = pl.program_id(0), pl.program_id(1)
        return (c == 0) & (g == 0), (c == 0) & (g == 1), (c == last) & (g == SSM_GROUPS - 1)

    par_shape = jax.ShapeDtypeStruct((1, LANES), F32)
    return _call_with_comm(
        body, comm, at, (dy, xbc, proj, expand, bias, alog, dsk, xbc, xbc, states), name="ssd_bwd",
        grid=(nb, SSM_GROUPS), in_specs=[xs, xs, dt, ex, par, par, par, b, cc, state],
        out_specs=[xs, tile, nspec, nspec, par, par, par],
        out_shape=[jax.ShapeDtypeStruct((nb * BLOCK, SSM_INNER), F32), jax.ShapeDtypeStruct((nb * BLOCK, LANES), BF16),
                   jax.ShapeDtypeStruct((nb * BLOCK, SSM_GROUPS * SSM_STATE), F32),
                   jax.ShapeDtypeStruct((nb * BLOCK, SSM_GROUPS * SSM_STATE), F32), par_shape, par_shape, par_shape],
        scratch=[pltpu.VMEM((SSM_GROUPS, SSM_STATE, GROUP_W), F32), pltpu.VMEM((BLOCK, LANES), F32)])


SLAB_ROWS = 24
SLAB_META_ROW = 8


SLAB_LOSS_ROW = 7


def pack_small(dcw, dcb, dgpre, dgpost, dbias, dalog, ddsk, dsinks, dgn, dmeta, loss_tile):
    def body(cw, cb, gpre, gpost, dtb, al, dk, sk, gn, meta, loss, o_ref):
        o_ref[...] = jnp.zeros_like(o_ref)
        o_ref[SLAB_LOSS_ROW:SLAB_LOSS_ROW + 1, 0:LANES] = loss[0:1, :]
        o_ref[0:CONV_WIDTH, :] = cw[...]
        o_ref[4:5, :] = cb[...]
        o_ref[5:6, 0:1024] = gpre[...]
        o_ref[5:6, 1024:2048] = gpost[...]
        o_ref[5:6, 2048:2176] = dtb[...]
        o_ref[5:6, 2176:2304] = al[...]
        o_ref[5:6, 2304:2432] = dk[...]
        o_ref[5:6, 2432:2560] = sk[...]
        o_ref[6:7, 0:SSM_INNER] = gn[...]
        o_ref[SLAB_META_ROW:SLAB_META_ROW + N_META, 0:D_MODEL] = meta[...]

    args = (dcw, dcb, dgpre, dgpost, dbias, dalog, ddsk, dsinks, dgn, dmeta, loss_tile)
    return _call(body, name="pack_small", in_specs=[_full(a.shape) for a in args],
                 out_specs=_full((SLAB_ROWS, CONV_DIM)), out_shape=jax.ShapeDtypeStruct((SLAB_ROWS, CONV_DIM), F32))(*args)


def _lane_tile(v):
    return jnp.pad(v, ((0, 0), (0, LANES - v.shape[1])))


def kernel(x, meta_tokens, g_pre, w_in, conv_w, conv_b, dt_bias, a_log, d_skip, attn_sinks, g_ssm_norm, w_out_att, w_out_ssm, w_out, g_post, loss_target, m_meta_tokens, m_g_pre, m_w_in, m_conv_w, m_conv_b, m_dt_bias, m_a_log, m_d_skip, m_attn_sinks, m_g_ssm_norm, m_w_out_att, m_w_out_ssm, m_w_out, m_g_post, v_meta_tokens, v_g_pre, v_w_in, v_conv_w, v_conv_b, v_dt_bias, v_a_log, v_d_skip, v_attn_sinks, v_g_ssm_norm, v_w_out_att, v_w_out_ssm, v_w_out, v_g_post):
    chip = _chip_index()

    conv_w_rows = jnp.pad(conv_w[0], ((0, 2 * 8 - CONV_WIDTH), (0, 0)))
    g_w_in, g_conv_w, g_meta = run_comm("gather_w_in", TwoLevelGather([pack_w_in(w_in[0]), conv_w_rows, meta_tokens]))
    w_all = unpack_w_in(g_w_in)
    cw_full = g_conv_w[:, :CONV_WIDTH].transpose(1, 0, 2).reshape(CONV_WIDTH, CONV_DIM)
    meta_full = g_meta.transpose(1, 0, 2).reshape(N_META, D_MODEL)

    h, u = prep(x, meta_full, g_pre)
    proj, g_w_out = mm_nn("in_proj", u, w_all, comm=TwoLevelGather(
        [w_out_att[0].astype(BF16), w_out_ssm[0].astype(BF16), w_out[0].astype(BF16)]))
    woa = g_w_out[0].reshape(D_MODEL, D_MODEL)
    wos = g_w_out[1].reshape(SSM_INNER, D_MODEL)
    wo = g_w_out[2].reshape(D_MODEL, D_MODEL)

    sinks3 = attn_sinks.reshape(ATT_Q_HEADS, 1, 1)
    a_att = attn_fwd(proj, sinks3)

    xbc = conv_fwd(proj, cw_full, conv_b)
    expand = _head_expand()
    head_pars = (_lane_tile(dt_bias), _lane_tile(a_log), _lane_tile(d_skip))
    y_ssd, states = ssd_fwd(xbc, proj, expand, *head_pars)

    (yn, merged, dout, dy_att, dy_ssm, da_att, dy_ssd, dz_ssm, dga, dgs, dres, loss_tile, dg_post, dgn) = tail(
        y_ssd, proj, a_att, x, loss_target, woa, wos, wo, g_ssm_norm, g_post)

    dwo = mm_tn("out_proj_dw", merged, dout)
    dwoa = mm_tn("att_out_dw", a_att, dy_att)
    dwos = mm_tn("ssm_out_dw", yn, dy_ssm)
    dq, dz_att, dk, dv, dkmeta, dvmeta, dsinks3 = attn_bwd(da_att, proj, sinks3)
    dk = dk.at[PAD_ROWS:BLOCK].add(dkmeta).astype(BF16)
    dv = dv.at[PAD_ROWS:BLOCK].add(dvmeta).astype(BF16)

    def pieces(g):
        return g.reshape(4, 2, g.shape[0] // 8, g.shape[1])

    def to_owner(g):
        return (lambda ref, dev: ref.at[_chip_of(dev), dev[2]], (g.shape[0] // 8, g.shape[1]))

    (dxs, ddt_tile, dbg, dcg, dbias, dalog, ddsk), sent_w_out = ssd_bwd(
        dy_ssd, xbc, proj, expand, *head_pars, states,
        DirectExchange([pieces(dwoa), pieces(dwos), pieces(dwo)], [to_owner(dwoa), to_owner(dwos), to_owner(dwo)],
                       ALL_MASKS, "dev", 8))
    dxs_raw, dcw_xs, dcb_xs = conv_bwd("conv_bwd_x", [dxs], 0, proj, cw_full, conv_b)
    dbc_raw, dcw_bc, dcb_bc = conv_bwd("conv_bwd_bc", [dbg, dcg], SSM_INNER // D_MODEL, proj, cw_full, conv_b)
    dcw = jnp.concatenate([dcw_xs, dcw_bc], axis=1)
    dcb = jnp.concatenate([dcb_xs, dcb_bc], axis=1)

    dproj = jnp.concatenate([dz_ssm, dxs_raw, dbc_raw, dq, dz_att, dga, dgs, dk, dv, ddt_tile], axis=1)
    dw_all = mm_tn("in_proj_dw", u, dproj)

    half_rows = D_MODEL // 2
    partial = pack_grad_w_in(dw_all).reshape(4, 2, half_rows, PACK_W)
    from_sibling, = run_comm("pair_grads", DirectExchange(
        [partial], [(lambda ref, dev: ref.at[pl.ds(0, 4), dev[2]], (4, half_rows, PACK_W))], SIBLING_MASK, "core", 2,
        keep_own=False))
    chip_sum = sum_pair(partial, from_sibling)
    du, (sent_w_in,) = mm_nt("in_proj_dx", dproj, w_all, comm=DirectExchange(
        [chip_sum], [(lambda ref, dev: ref.at[_chip_of(dev)], (half_rows, PACK_W))], CHIP_MASKS, "chip", 4))
    grad_x, dmeta, dg_pre = prep_bwd(h, du, dres, g_pre)

    slab = pack_small(dcw, dcb, dg_pre, dg_post, dbias, dalog, ddsk, _lane_tile(dsinks3.reshape(1, ATT_Q_HEADS)), dgn,
                      dmeta, loss_tile)
    halves = [sum_slots("sum_" + nm, r)
              for nm, r in zip(("w_in", "w_out_att", "w_out_ssm", "w_out"), [sent_w_in] + list(sent_w_out))]
    shared = run_comm("share_grads", Both(DirectExchange(halves, [None] * 4, SIBLING_MASK, "core", 2),
                                          DirectExchange([slab], [None], ALL_MASKS, "dev", 8)))
    g_w_in_packed, g_woa, g_wos, g_wo = [f.reshape(2 * f.shape[1], f.shape[2]) for f in shared[:4]]
    small = sum_slots("sum_small", shared[4])
    loss = small[SLAB_LOSS_ROW, 0]

    g_w_in, d_w_in, nm_w_in, nv_w_in = adamw_w_in(g_w_in_packed, w_in[0], m_w_in[0], v_w_in[0])
    d_woa, nm_woa, nv_woa = adamw_rows("adamw_w_out_att", g_woa, w_out_att[0], m_w_out_att[0], v_w_out_att[0])
    d_wos, nm_wos, nv_wos = adamw_rows("adamw_w_out_ssm", g_wos, w_out_ssm[0], m_w_out_ssm[0], v_w_out_ssm[0])
    d_wo, nm_wo, nv_wo = adamw_rows("adamw_w_out", g_wo, w_out[0], m_w_out[0], v_w_out[0])

    cw_cols = CONV_DIM // 4
    meta_cols = D_MODEL // 4
    g_small = {
        "meta_tokens": lax.dynamic_slice(small, (SLAB_META_ROW, chip * meta_cols), (N_META, meta_cols)),
        "g_pre": small[5:6, 0:1024],
        "conv_w": lax.dynamic_slice(small, (0, chip * cw_cols), (CONV_WIDTH, cw_cols)),
        "conv_b": small[4:5, :],
        "dt_bias": small[5:6, 2048:2048 + SSM_HEADS],
        "a_log": small[5:6, 2176:2176 + SSM_HEADS],
        "d_skip": small[5:6, 2304:2304 + SSM_HEADS],
        "attn_sinks": small[5:6, 2432:2432 + ATT_Q_HEADS],
        "g_ssm_norm": small[6:7, 0:SSM_INNER],
        "g_post": small[5:6, 1024:2048],
    }
    names = list(g_small)
    w_small = dict(meta_tokens=meta_tokens, g_pre=g_pre, conv_w=conv_w[0], conv_b=conv_b, dt_bias=dt_bias, a_log=a_log,
                   d_skip=d_skip, attn_sinks=attn_sinks, g_ssm_norm=g_ssm_norm, g_post=g_post)
    m_small = dict(meta_tokens=m_meta_tokens, g_pre=m_g_pre, conv_w=m_conv_w[0], conv_b=m_conv_b, dt_bias=m_dt_bias,
                   a_log=m_a_log, d_skip=m_d_skip, attn_sinks=m_attn_sinks, g_ssm_norm=m_g_ssm_norm, g_post=m_g_post)
    v_small = dict(meta_tokens=v_meta_tokens, g_pre=v_g_pre, conv_w=v_conv_w[0], conv_b=v_conv_b, dt_bias=v_dt_bias,
                   a_log=v_a_log, d_skip=v_d_skip, attn_sinks=v_attn_sinks, g_ssm_norm=v_g_ssm_norm, g_post=v_g_post)
    upd = dict(zip(names, adamw_small([g_small[k] for k in names], [w_small[k] for k in names],
                                      [m_small[k] for k in names], [v_small[k] for k in names])))

    lead = {"conv_w"}

    def shaped(name, a):
        return a[None] if name in lead else a

    grads = dict(g_small, w_in=g_w_in, w_out_att=g_woa, w_out_ssm=g_wos, w_out=g_wo)
    deltas = dict({k: upd[k][0] for k in names}, w_in=d_w_in, w_out_att=d_woa, w_out_ssm=d_wos, w_out=d_wo)
    new_m = dict({k: upd[k][1] for k in names}, w_in=nm_w_in, w_out_att=nm_woa, w_out_ssm=nm_wos, w_out=nm_wo)
    new_v = dict({k: upd[k][2] for k in names}, w_in=nv_w_in, w_out_att=nv_woa, w_out_ssm=nv_wos, w_out=nv_wo)
    lead |= {"w_in", "w_out_att", "w_out_ssm", "w_out"}
    order = ["meta_tokens", "g_pre", "w_in", "conv_w", "conv_b", "dt_bias", "a_log", "d_skip", "attn_sinks",
             "g_ssm_norm", "w_out_att", "w_out_ssm", "w_out", "g_post"]
    outs = [loss, grad_x]
    for group in (grads, deltas, new_m, new_v):
        outs += [shaped(k, group[k]) for k in order]
    return tuple(outs)
```

```python
import functools

import numpy as np
import jax
import jax.numpy as jnp
from jax import lax
from jax.experimental import pallas as pl
from jax.experimental.pallas import tpu as pltpu

F32 = jnp.float32
BF16 = jnp.bfloat16
HI = lax.Precision.HIGHEST

D_MODEL = 1024
N_META = 16
BLOCK = 128
PAD_ROWS = BLOCK - N_META
NORM_EPS = 1e-6
HEAD_DIM = 64
ATT_Q_HEADS = 16
ATT_KV_HEADS = 4
ATT_GROUP = 4
SSM_INNER = 2048
SSM_HEADS = 32
SSM_GROUPS = 4
SSM_HEADS_PER_GROUP = 8
SSM_STATE = 128
CONV_WIDTH = 4
CONV_DIM = 3072
LANES = 128

ADAM_LR = 0.001
ADAM_B1 = 0.9
ADAM_B2 = 0.999
ADAM_EPS = 1e-08
ADAM_WD = 0.01
ADAM_STEP = 10

VMEM_LIMIT = 48 * 1024 * 1024

SHARD_W = 2440
PACK_W = 2560
SHARD_STRIDE = 2432
N_ALIGNED = 9856
SEG = {
    "q": (0, 1024, 5120), "k": (1024, 256, 9216), "v": (1280, 256, 9472), "z_att": (1536, 1024, 6144),
    "z_ssm": (2560, 2048, 0), "xbc": (4608, 3072, 2048), "dt": (7680, 128, 9728),
    "gate_att": (7808, 1024, 7168), "gate_ssm": (8832, 1024, 8192),
}
DT_STORED_START = 7680
DT_PAD = LANES - SSM_HEADS


def _act_col(aligned_col):
    for a0, w, p0 in SEG.values():
        if a0 <= aligned_col < a0 + w:
            return p0 + aligned_col - a0
    raise ValueError(aligned_col)


def _call(body, *, name, out_shape, in_specs, out_specs, grid=(), scratch=(), sem=None, aliases=None):
    return pl.pallas_call(
        body, out_shape=out_shape, grid=grid, in_specs=in_specs, out_specs=out_specs, scratch_shapes=list(scratch),
        name=name, input_output_aliases=aliases or {},
        compiler_params=pltpu.CompilerParams(dimension_semantics=sem, vmem_limit_bytes=VMEM_LIMIT))


def _full(shape):
    n = len(shape)
    return pl.BlockSpec(shape, lambda *_: (0,) * n)


def _chip_index():
    return lax.axis_index("x") * 2 + lax.axis_index("y")


def _silu(z):
    return z * jax.nn.sigmoid(z)


def _rms(x, g):
    return x * lax.rsqrt(jnp.mean(x * x, axis=-1, keepdims=True) + NORM_EPS) * g


def _peer(mask):
    x, y, c = lax.axis_index("x"), lax.axis_index("y"), lax.axis_index("c")
    return ((1 - x) if mask & 4 else x, (1 - y) if mask & 2 else y, (1 - c) if mask & 1 else c)


def _me():
    return lax.axis_index("x"), lax.axis_index("y"), lax.axis_index("c")


def _chip_of(dev):
    return 2 * dev[0] + dev[1]


CHIP_MASKS = (4, 2, 6)
ALL_MASKS = (1, 2, 3, 4, 5, 6, 7)
SIBLING_MASK = (1,)


def _remote(src, dst, send_sem, recv_sem, dev):
    return pltpu.make_async_remote_copy(src_ref=src, dst_ref=dst, send_sem=send_sem, recv_sem=recv_sem,
                                        device_id=dev, device_id_type=pl.DeviceIdType.MESH)


class _StagedCopy:
    def __init__(self, src, stage, dst, load_sem, store_sem):
        self.load = pltpu.make_async_copy(src, stage, load_sem)
        self.store = pltpu.make_async_copy(stage, dst, store_sem)

    def start(self):
        self.load.start()
        self.load.wait()
        self.store.start()

    def wait(self):
        self.store.wait()


class DirectExchange:
    def __init__(self, arrays, pieces, masks, slot_kind, nslots, keep_own=True):
        self.arrays, self.pieces, self.masks, self.slot_kind = list(arrays), list(pieces), masks, slot_kind
        self.keep_own = keep_own
        n, nk = len(arrays), len(masks)
        shapes = [a.shape if p is None else p[1] for a, p in zip(arrays, pieces)]
        self.out_shape = [jax.ShapeDtypeStruct((nslots,) + tuple(s), a.dtype) for s, a in zip(shapes, arrays)]
        self.scratch = [pltpu.SemaphoreType.DMA((n * nk,)), pltpu.SemaphoreType.DMA((n * nk,))]
        if keep_own:
            self.scratch += [pltpu.SemaphoreType.DMA((2 * n,))] + [pltpu.VMEM(s, a.dtype) for s, a in zip(shapes, arrays)]
        self.has_mid = False

    def _copies(self, ins, outs, scratch):
        send_sems, recv_sems = scratch[:2]
        me = _me()
        slot = {"chip": _chip_of(me), "dev": 4 * me[0] + 2 * me[1] + me[2], "core": me[2]}[self.slot_kind]
        nk = len(self.masks)

        def piece(a, dev):
            return ins[a] if self.pieces[a] is None else self.pieces[a][0](ins[a], dev)

        local = []
        if self.keep_own:
            local_sems, stages = scratch[2], scratch[3:]
            local = [_StagedCopy(piece(a, me), stages[a], outs[a].at[slot], local_sems.at[2 * a], local_sems.at[2 * a + 1])
                     for a in range(len(ins))]
        remote = []
        for a in range(len(ins)):
            for ki, mask in enumerate(self.masks):
                dev = _peer(mask)
                remote.append(_remote(piece(a, dev), outs[a].at[slot], send_sems.at[a * nk + ki],
                                      recv_sems.at[a * nk + ki], dev))
        return local, remote

    def start(self, ins, outs, scratch):
        local, remote = self._copies(ins, outs, scratch)
        for cp in remote + local:
            cp.start()

    def finish(self, ins, outs, scratch):
        local, remote = self._copies(ins, outs, scratch)
        for cp in remote + local:
            cp.wait()


class TwoLevelGather:
    def __init__(self, arrays):
        self.arrays = list(arrays)
        n, nk = len(arrays), len(CHIP_MASKS)
        self.out_shape = [jax.ShapeDtypeStruct((4,) + a.shape, a.dtype) for a in arrays]
        self.scratch = ([pltpu.SemaphoreType.DMA((n * nk,)) for _ in range(4)] + [pltpu.SemaphoreType.DMA((2 * n,))]
                        + [pltpu.VMEM(a.shape, a.dtype) for a in arrays])
        self.has_mid = True

    def _copies(self, ins, outs, scratch):
        ici_send, ici_recv, fwd_send, fwd_recv, local_sems = scratch[:5]
        stages = scratch[5:]
        me = _me()
        sibling = _peer(1)
        nk = len(CHIP_MASKS)
        local, ici, fwd = [], [], []
        for a in range(len(ins)):
            half = ins[a].shape[0] // 2
            mine = pl.ds(me[2] * half, half)
            local.append(_StagedCopy(ins[a], stages[a], outs[a].at[_chip_of(me)], local_sems.at[2 * a],
                                     local_sems.at[2 * a + 1]))
            for ki, mask in enumerate(CHIP_MASKS):
                dev = _peer(mask)
                k = a * nk + ki
                ici.append(_remote(ins[a].at[mine], outs[a].at[_chip_of(me), mine], ici_send.at[k], ici_recv.at[k], dev))
                arrived = outs[a].at[_chip_of(dev), mine]
                fwd.append(_remote(arrived, arrived, fwd_send.at[k], fwd_recv.at[k], sibling))
        return local, ici, fwd

    def start(self, ins, outs, scratch):
        local, ici, _ = self._copies(ins, outs, scratch)
        for cp in ici + local:
            cp.start()

    def mid(self, ins, outs, scratch):
        _, ici, fwd = self._copies(ins, outs, scratch)
        for arrival, onward in zip(ici, fwd):
            arrival.wait_recv()
            onward.start()

    def finish(self, ins, outs, scratch):
        local, ici, fwd = self._copies(ins, outs, scratch)
        for cp in fwd:
            cp.wait_recv()
        for cp in ici + fwd:
            cp.wait_send()
        for cp in local:
            cp.wait()


class Both:
    def __init__(self, a, b):
        self.a, self.b = a, b
        self.arrays, self.out_shape = a.arrays + b.arrays, a.out_shape + b.out_shape
        self.scratch = a.scratch + b.scratch
        self.has_mid = False
        assert not (a.has_mid or b.has_mid)

    def _parts(self, ins, outs, sems):
        na, sa = len(self.a.arrays), len(self.a.scratch)
        return (ins[:na], outs[:na], sems[:sa]), (ins[na:], outs[na:], sems[sa:])

    def start(self, ins, outs, sems):
        pa, pb = self._parts(ins, outs, sems)
        self.a.start(*pa)
        self.b.start(*pb)

    def finish(self, ins, outs, sems):
        pa, pb = self._parts(ins, outs, sems)
        self.a.finish(*pa)
        self.b.finish(*pb)


_ANY = pl.BlockSpec(memory_space=pl.ANY)


def run_comm(name, comm):
    n = len(comm.arrays)

    def body(*refs):
        ins, outs, sems = refs[:n], refs[n:2 * n], refs[2 * n:]
        comm.start(ins, outs, sems)
        if comm.has_mid:
            comm.mid(ins, outs, sems)
        comm.finish(ins, outs, sems)

    return pl.pallas_call(body, name=name, out_shape=comm.out_shape, in_specs=[_ANY] * n, out_specs=[_ANY] * n,
                          scratch_shapes=comm.scratch,
                          compiler_params=pltpu.CompilerParams(vmem_limit_bytes=VMEM_LIMIT))(*comm.arrays)


def _call_with_comm(body, comm, steps, args, *, name, out_shape, in_specs, out_specs, grid, scratch=()):
    ni, no, ns, nc = len(in_specs), len(out_specs), len(scratch), len(comm.arrays)

    def full_body(*refs):
        ins, cins = refs[:ni], refs[ni:ni + nc]
        outs, couts = refs[ni + nc:ni + nc + no], refs[ni + nc + no:ni + 2 * nc + no]
        scr, csems = refs[ni + 2 * nc + no:ni + 2 * nc + no + ns], refs[ni + 2 * nc + no + ns:]
        first, middle, last = steps()
        pl.when(first)(lambda: comm.start(cins, couts, csems))
        if comm.has_mid:
            pl.when(middle)(lambda: comm.mid(cins, couts, csems))
        body(*ins, *outs, *scr)
        pl.when(last)(lambda: comm.finish(cins, couts, csems))

    res = pl.pallas_call(
        full_body, name=name, out_shape=list(out_shape) + comm.out_shape, grid=grid,
        in_specs=list(in_specs) + [_ANY] * nc, out_specs=list(out_specs) + [_ANY] * nc,
        scratch_shapes=list(scratch) + comm.scratch,
        compiler_params=pltpu.CompilerParams(dimension_semantics=("arbitrary",) * len(grid),
                                             vmem_limit_bytes=VMEM_LIMIT))(*args, *comm.arrays)
    return res[:no], res[no:]


def _shard_pieces(chip):
    if chip < 3:
        return [(0, SHARD_W, 8 * chip)]
    behind_dt = DT_STORED_START + SSM_HEADS - 3 * SHARD_W
    return [(0, behind_dt, 24), (behind_dt, SHARD_W - behind_dt, behind_dt + 24 + DT_PAD)]


W_IN_COLS = 256


def pack_w_in(wt):
    def body(w_ref, o_ref, pad_ref):
        chip = _chip_index()
        pad_ref[...] = jnp.zeros_like(pad_ref)
        for cv in range(4):
            @pl.when(chip == cv)
            def _():
                for src, n, dst in _shard_pieces(cv):
                    pad_ref[dst:dst + n, :] = w_ref[src:src + n, :]
        o_ref[...] = pad_ref[...].astype(BF16)

    return _call(body, name="pack_w_in", grid=(D_MODEL // W_IN_COLS,),
                 in_specs=[pl.BlockSpec((SHARD_W, W_IN_COLS), lambda i: (0, i))],
                 out_specs=pl.BlockSpec((PACK_W, W_IN_COLS), lambda i: (0, i)),
                 out_shape=jax.ShapeDtypeStruct((PACK_W, D_MODEL), BF16),
                 scratch=[pltpu.VMEM((PACK_W, W_IN_COLS), F32)], sem=("parallel",))(wt)


def _tile_runs():
    runs, fix = [], []
    for t in range(N_ALIGNED // LANES):
        s = min(t // 19, 3)
        j = t - 19 * s
        p = _act_col(t * LANES)
        if runs and runs[-1][1] == s and runs[-1][0] + runs[-1][3] == p and runs[-1][2] + runs[-1][3] == j * LANES:
            runs[-1][3] += LANES
        else:
            runs.append([p, s, j * LANES, LANES])
        if j == 0 and s > 0:
            fix.append((p, s - 1))
    return runs, fix


def unpack_w_in(bg):
    runs, fix = _tile_runs()

    def body(b_ref, o_ref):
        for p, s, j, w in runs:
            o_ref[p:p + w, :] = b_ref[s, j:j + w, :]
        for p, s in fix:
            o_ref[p:p + LANES, :] = o_ref[p:p + LANES, :] + b_ref[s, SHARD_STRIDE:PACK_W, :]

    return _call(body, name="unpack_w_in", grid=(D_MODEL // W_IN_COLS,),
                 in_specs=[pl.BlockSpec((4, PACK_W, W_IN_COLS), lambda i: (0, 0, i))],
                 out_specs=pl.BlockSpec((N_ALIGNED, W_IN_COLS), lambda i: (0, i)),
                 out_shape=jax.ShapeDtypeStruct((N_ALIGNED, D_MODEL), BF16), sem=("parallel",))(bg)


def pack_grad_w_in(dwt):
    def body(g_ref, o_ref):
        for s in range(4):
            for j in range(PACK_W // LANES):
                p = _act_col((19 * s + j) * LANES)
                o_ref[s, j * LANES:(j + 1) * LANES, :] = g_ref[p:p + LANES, :]

    return _call(body, name="pack_grad_w_in", grid=(D_MODEL // W_IN_COLS,),
                 in_specs=[pl.BlockSpec((N_ALIGNED, W_IN_COLS), lambda i: (0, i))],
                 out_specs=pl.BlockSpec((4, PACK_W, W_IN_COLS), lambda i: (0, 0, i)),
                 out_shape=jax.ShapeDtypeStruct((4, PACK_W, D_MODEL), BF16), sem=("parallel",))(dwt)


def _adamw(w, g, m, v):
    m = ADAM_B1 * m + (1.0 - ADAM_B1) * g
    v = ADAM_B2 * v + (1.0 - ADAM_B2) * jnp.square(g)
    m_hat = m / (1.0 - ADAM_B1 ** ADAM_STEP)
    v_hat = v / (1.0 - ADAM_B2 ** ADAM_STEP)
    delta = -ADAM_LR * (m_hat / (jnp.sqrt(v_hat) + ADAM_EPS) + ADAM_WD * w)
    return delta, m, v


def adamw_w_in(g_packed, wt, mt, vt):
    cols = LANES

    def body(g_ref, w_ref, m_ref, v_ref, go_ref, d_ref, mo_ref, vo_ref):
        chip = _chip_index()
        for cv in range(4):
            @pl.when(chip == cv)
            def _():
                for dst, n, src in _shard_pieces(cv):
                    go_ref[dst:dst + n, :] = g_ref[src:src + n, :]
        d_ref[...], mo_ref[...], vo_ref[...] = _adamw(w_ref[...], go_ref[...], m_ref[...], v_ref[...])

    spec = pl.BlockSpec((SHARD_W, cols), lambda i: (0, i))
    shp = jax.ShapeDtypeStruct((SHARD_W, D_MODEL), F32)
    return _call(body, name="adamw_w_in", grid=(D_MODEL // cols,),
                 in_specs=[pl.BlockSpec((PACK_W, cols), lambda i: (0, i)), spec, spec, spec],
                 out_specs=[spec] * 4, out_shape=[shp] * 4, sem=("parallel",))(g_packed, wt, mt, vt)


def adamw_rows(name, g, w, m, v):
    r, c = g.shape
    rows = min(r, BLOCK)

    def body(g_ref, w_ref, m_ref, v_ref, d_ref, mo_ref, vo_ref):
        d_ref[...], mo_ref[...], vo_ref[...] = _adamw(w_ref[...], g_ref[...], m_ref[...], v_ref[...])

    spec = pl.BlockSpec((rows, c), lambda i: (i, 0))
    shp = jax.ShapeDtypeStruct((r, c), F32)
    return _call(body, name=name, grid=(r // rows,), in_specs=[spec] * 4, out_specs=[spec] * 3, out_shape=[shp] * 3,
                 sem=("parallel",))(g, w, m, v)


def adamw_small(gs, ws, ms, vs):
    n = len(gs)

    def body(*refs):
        g, w, m, v = refs[:n], refs[n:2 * n], refs[2 * n:3 * n], refs[3 * n:4 * n]
        outs = refs[4 * n:]
        for i in range(n):
            d, mn, vn = _adamw(w[i][...], g[i][...], m[i][...], v[i][...])
            outs[3 * i][...] = d
            outs[3 * i + 1][...] = mn
            outs[3 * i + 2][...] = vn

    specs = [_full(a.shape) for a in gs]
    res = _call(body, name="adamw_small", in_specs=specs * 4,
                out_specs=[s for s in specs for _ in range(3)],
                out_shape=[jax.ShapeDtypeStruct(a.shape, F32) for a in gs for _ in range(3)])(*gs, *ws, *ms, *vs)
    return [tuple(res[3 * i:3 * i + 3]) for i in range(n)]


def sum_slots(name, r):
    s, rr, c = r.shape
    rows = min(rr, BLOCK)

    def body(r_ref, o_ref):
        acc = r_ref[0].astype(F32)
        for k in range(1, s):
            acc = acc + r_ref[k].astype(F32)
        o_ref[...] = acc

    return _call(body, name=name, grid=(rr // rows,), in_specs=[pl.BlockSpec((s, rows, c), lambda i: (0, i, 0))],
                 out_specs=pl.BlockSpec((rows, c), lambda i: (i, 0)), out_shape=jax.ShapeDtypeStruct((rr, c), F32),
                 sem=("parallel",))(r)


def sum_pair(partial, from_sibling):
    s, _, rr, cols = partial.shape

    def body(p_ref, r_ref, o_ref):
        c = lax.axis_index("c")
        o_ref[...] = (p_ref[c].astype(F32) + r_ref[1 - c].astype(F32)).astype(BF16)

    return _call(body, name="sum_pair", grid=(s, rr // BLOCK),
                 in_specs=[pl.BlockSpec((None, 2, BLOCK, cols), lambda k, i: (k, 0, i, 0)),
                           pl.BlockSpec((2, None, BLOCK, cols), lambda k, i: (0, k, i, 0))],
                 out_specs=pl.BlockSpec((None, BLOCK, cols), lambda k, i: (k, i, 0)),
                 out_shape=jax.ShapeDtypeStruct((s, rr, cols), BF16), sem=("parallel", "parallel"))(partial, from_sibling)


def _col_tile(n, k):
    if n % 896 == 0 and k <= 1024:
        return 896
    return min(n, 512)


def project(name, x, wt, comm):
    m, k = x.shape
    n = wt.shape[0]
    tn = _col_tile(n, k)
    steps = n // tn

    def body(x_ref, w_ref, o_ref):
        o_ref[...] = lax.dot_general(x_ref[...], w_ref[...], _NT, preferred_element_type=F32)

    def at():
        j = pl.program_id(0)
        return j == 0, j == (3 * steps) // 4, j == steps - 1

    (res,), moved = _call_with_comm(
        body, comm, at, (x, wt), name=name, grid=(steps,),
        in_specs=[_full((m, k)), pl.BlockSpec((tn, k), lambda j: (j, 0))],
        out_specs=[pl.BlockSpec((m, tn), lambda j: (0, j))], out_shape=[jax.ShapeDtypeStruct((m, n), F32)])
    return res, moved


def project_back(name, dy, wt, comm):
    m, n = dy.shape
    k = wt.shape[1]
    tm = m // 2
    tn = _col_tile(n, k)
    steps = n // tn

    def at():
        i, j = pl.program_id(0), pl.program_id(1)
        return (i == 0) & (j == 0), (i == 0) & (j == steps - 1), (i == m // tm - 1) & (j == steps - 1)

    def body(dy_ref, w_ref, o_ref):
        @pl.when(pl.program_id(1) == 0)
        def _():
            o_ref[...] = jnp.zeros_like(o_ref)

        o_ref[...] += jnp.dot(dy_ref[...], w_ref[...], preferred_element_type=F32)

    (res,), moved = _call_with_comm(
        body, comm, at, (dy, wt), name=name, grid=(m // tm, steps),
        in_specs=[pl.BlockSpec((tm, tn), lambda i, j: (i, j)), pl.BlockSpec((tn, k), lambda i, j: (j, 0))],
        out_specs=[pl.BlockSpec((tm, k), lambda i, j: (i, 0))], out_shape=[jax.ShapeDtypeStruct((m, k), F32)])
    return res, moved


def weight_grad_t(name, dy, x):
    m, n = dy.shape
    k = x.shape[1]
    tm = m // 2
    tn = _col_tile(n, k)

    def body(dy_ref, x_ref, o_ref, acc_ref):
        part = lax.dot_general(dy_ref[...], x_ref[...], (((0,), (0,)), ((), ())), preferred_element_type=F32)

        @pl.when(pl.program_id(1) == 0)
        def _():
            acc_ref[...] = part

        @pl.when(pl.program_id(1) == 1)
        def _():
            o_ref[...] = (acc_ref[...] + part).astype(BF16)

    return _call(body, name=name, grid=(n // tn, 2),
                 in_specs=[pl.BlockSpec((tm, tn), lambda i, j: (j, i)), pl.BlockSpec((tm, k), lambda i, j: (j, 0))],
                 out_specs=pl.BlockSpec((tn, k), lambda i, j: (i, 0)), out_shape=jax.ShapeDtypeStruct((n, k), BF16),
                 scratch=[pltpu.VMEM((tn, k), F32)], sem=("parallel", "arbitrary"))(dy, x)


def mm_tn(name, x, dy):
    m, k = x.shape
    n = dy.shape[1]
    tm = m // 2
    tn = _col_tile(n, k)

    def body(x_ref, dy_ref, o_ref, acc_ref):
        part = lax.dot_general(x_ref[...].astype(BF16), dy_ref[...].astype(BF16), (((0,), (0,)), ((), ())),
                               preferred_element_type=F32)

        @pl.when(pl.program_id(1) == 0)
        def _():
            acc_ref[...] = part

        @pl.when(pl.program_id(1) == 1)
        def _():
            o_ref[...] = (acc_ref[...] + part).astype(BF16)

    return _call(body, name=name, grid=(n // tn, 2),
                 in_specs=[pl.BlockSpec((tm, k), lambda i, j: (j, 0)), pl.BlockSpec((tm, tn), lambda i, j: (j, i))],
                 out_specs=pl.BlockSpec((k, tn), lambda i, j: (0, i)), out_shape=jax.ShapeDtypeStruct((k, n), BF16),
                 scratch=[pltpu.VMEM((k, tn), F32)], sem=("parallel", "arbitrary"))(x, dy)


def _row_spec(width, col_block=0):
    return pl.BlockSpec((BLOCK, width), lambda i: (i, col_block))


def _x_spec():
    return pl.BlockSpec((None, BLOCK, D_MODEL), lambda i: (0, jnp.maximum(i - 1, 0), 0))


def prep(x, meta, g_pre):
    nb = x.shape[1] // BLOCK + 1

    def body(x_ref, meta_ref, g_ref, h_ref, u_ref):
        i = pl.program_id(0)

        @pl.when(i == 0)
        def _():
            h_ref[0:PAD_ROWS, :] = jnp.zeros((PAD_ROWS, D_MODEL), F32)
            h_ref[PAD_ROWS:BLOCK, :] = meta_ref[...]

        @pl.when(i > 0)
        def _():
            h_ref[...] = x_ref[...]

        u_ref[...] = _rms(h_ref[...], g_ref[...]).astype(BF16)

    return _call(body, name="prep", grid=(nb,), in_specs=[_x_spec(), _full((N_META, D_MODEL)), _full((1, D_MODEL))],
                 out_specs=[_row_spec(D_MODEL), _row_spec(D_MODEL)],
                 out_shape=[jax.ShapeDtypeStruct((nb * BLOCK, D_MODEL), F32),
                            jax.ShapeDtypeStruct((nb * BLOCK, D_MODEL), BF16)], sem=("parallel",))(x, meta, g_pre)


def prep_bwd(h, du, dres, g_pre):
    nb = h.shape[0] // BLOCK

    def body(h_ref, du_ref, dres_ref, g_ref, gx_ref, gm_ref, gg_ref):
        i = pl.program_id(0)
        _, vjp = jax.vjp(_rms, h_ref[...], g_ref[...])
        dh, dg = vjp(du_ref[...])

        @pl.when(i == 0)
        def _():
            gm_ref[...] = dh[PAD_ROWS:BLOCK, :]
            gg_ref[...] = dg

        @pl.when(i > 0)
        def _():
            gg_ref[...] += dg

        gx_ref[...] = dh + dres_ref[...]

    return _call(body, name="prep_bwd", grid=(nb,),
                 in_specs=[_row_spec(D_MODEL), _row_spec(D_MODEL), _row_spec(D_MODEL), _full((1, D_MODEL))],
                 out_specs=[_x_spec(), _full((N_META, D_MODEL)), _full((1, D_MODEL))],
                 out_shape=[jax.ShapeDtypeStruct((1, (nb - 1) * BLOCK, D_MODEL), F32),
                            jax.ShapeDtypeStruct((N_META, D_MODEL), F32), jax.ShapeDtypeStruct((1, D_MODEL), F32)],
                 sem=("arbitrary",))(h, du, dres, g_pre)


GROUP_W = SSM_INNER // SSM_GROUPS


def _gated_norm(y, z, g):
    t = y * _silu(z)
    return t * lax.rsqrt(jnp.mean(t * t, axis=-1, keepdims=True) + NORM_EPS) * g


def _gated_norm_groups(y, z, g):
    groups = [slice(k * GROUP_W, (k + 1) * GROUP_W) for k in range(SSM_GROUPS)]
    return jnp.concatenate([_gated_norm(y[:, s], z[:, s], g[:, s]) for s in groups], axis=1)


def _merge(ga, gs, ya, ys):
    return jax.nn.sigmoid(ga) * ya + jax.nn.sigmoid(gs) * ys


GATE_ATT_BLOCK = SEG["gate_att"][2] // D_MODEL
GATE_SSM_BLOCK = SEG["gate_ssm"][2] // D_MODEL


def _row_loss(out, g_post, x, target):
    diff = x + _rms(out, g_post) - target
    return 0.5 * jnp.sum(diff * diff) / D_MODEL


def tail(y_ssd, proj, a_att, x, target, woa, wos, wo, g_norm, g_post):
    nb = y_ssd.shape[0] // BLOCK
    rows = nb * BLOCK

    def body(y_ref, z_ref, ga_ref, gs_ref, a_ref, x_ref, t_ref, woa_ref, wos_ref, wo_ref, gn_ref, gp_ref,
             yn_ref, mg_ref, dout_ref, dya_ref, dys_ref, da_ref, dy_ref, dz_ref, dga_ref, dgs_ref, dres_ref,
             loss_ref, dgp_ref, dgn_ref):
        i = pl.program_id(0)
        yn, norm_vjp = jax.vjp(_gated_norm_groups, y_ref[...], z_ref[...], gn_ref[...])
        yn16 = yn.astype(BF16)
        y_ssm = jnp.dot(yn16, wos_ref[...], preferred_element_type=F32)
        y_att = jnp.dot(a_ref[...], woa_ref[...], preferred_element_type=F32)
        merged, merge_vjp = jax.vjp(_merge, ga_ref[...], gs_ref[...], y_att, y_ssm)
        merged16 = merged.astype(BF16)
        out = jnp.dot(merged16, wo_ref[...], preferred_element_type=F32)
        loss, loss_vjp = jax.vjp(_row_loss, out, gp_ref[...], x_ref[...], t_ref[...])
        counted = jnp.where(i > 0, 1.0, 0.0)
        dout, dgp, dres, _ = loss_vjp(counted)
        dout16 = dout.astype(BF16)
        dmerged = lax.dot_general(dout16, wo_ref[...], _NT, preferred_element_type=F32)
        dga, dgs, dya, dys = merge_vjp(dmerged)
        dya16, dys16 = dya.astype(BF16), dys.astype(BF16)
        dyn = lax.dot_general(dys16, wos_ref[...], _NT, preferred_element_type=F32)
        dy, dz, dgn = norm_vjp(dyn)

        yn_ref[...] = yn16
        mg_ref[...] = merged16
        dout_ref[...] = dout16
        dya_ref[...] = dya16
        dys_ref[...] = dys16
        da_ref[...] = lax.dot_general(dya16, woa_ref[...], _NT, preferred_element_type=F32)
        dy_ref[...] = dy
        dz_ref[...] = dz.astype(BF16)
        dga_ref[...] = dga.astype(BF16)
        dgs_ref[...] = dgs.astype(BF16)
        dres_ref[...] = dres

        @pl.when(i == 0)
        def _():
            loss_ref[...] = jnp.zeros_like(loss_ref)
            dgp_ref[...] = jnp.zeros_like(dgp_ref)
            dgn_ref[...] = jnp.zeros_like(dgn_ref)

        loss_ref[...] += loss * counted
        dgp_ref[...] += dgp
        dgn_ref[...] += dgn

    wide, narrow = _row_spec(SSM_INNER), _row_spec(D_MODEL)
    resident = pl.BlockSpec(memory_space=pltpu.VMEM)
    bf = lambda w: jax.ShapeDtypeStruct((rows, w), BF16)
    f32 = lambda w: jax.ShapeDtypeStruct((rows, w), F32)
    return _call(body, name="tail", grid=(nb,),
                 in_specs=[wide, wide, _row_spec(D_MODEL, GATE_ATT_BLOCK), _row_spec(D_MODEL, GATE_SSM_BLOCK), narrow,
                           _x_spec(), _x_spec(), resident, resident, resident, _full((1, SSM_INNER)),
                           _full((1, D_MODEL))],
                 out_specs=[wide, narrow, narrow, narrow, narrow, narrow, wide, wide, narrow, narrow, narrow,
                            _full((8, LANES)), _full((1, D_MODEL)), _full((1, SSM_INNER))],
                 out_shape=[bf(SSM_INNER), bf(D_MODEL), bf(D_MODEL), bf(D_MODEL), bf(D_MODEL), f32(D_MODEL),
                            f32(SSM_INNER), bf(SSM_INNER), bf(D_MODEL), bf(D_MODEL), f32(D_MODEL),
                            jax.ShapeDtypeStruct((8, LANES), F32), jax.ShapeDtypeStruct((1, D_MODEL), F32),
                            jax.ShapeDtypeStruct((1, SSM_INNER), F32)],
                 sem=("arbitrary",))(y_ssd, proj, proj, proj, a_att, x, target, woa, wos, wo, g_norm, g_post)


_NT = (((1,), (1,)), ((), ()))
ALIBI_SLOPES = tuple(2.0 ** (-8.0 * (h + 1) / ATT_Q_HEADS) for h in range(ATT_Q_HEADS))
KV_WIDTH = ATT_KV_HEADS * HEAD_DIM
Q_BLOCK = SEG["q"][2] // D_MODEL
Z_ATT_BLOCK = SEG["z_att"][2] // D_MODEL
K_BLOCK = SEG["k"][2] // KV_WIDTH
V_BLOCK = SEG["v"][2] // KV_WIDTH
META_ROW_BLOCK = PAD_ROWS // N_META


@jax.custom_vjp
def _swap_halves(x):
    return pltpu.roll(x, HEAD_DIM, 1)


_swap_halves.defvjp(lambda x: (pltpu.roll(x, HEAD_DIM, 1), None), lambda _, g: (pltpu.roll(g, HEAD_DIM, 1),))


def _both_halves(t, half):
    first = lax.broadcasted_iota(jnp.int32, t.shape, 1) < HEAD_DIM
    sw = _swap_halves(t)
    return jnp.where(first, t, sw) if half == 0 else jnp.where(first, sw, t)


def _attn_rows(q, z, kp, kc, vp, vc, km, vm, sinks, n):
    rows = ATT_GROUP * BLOCK
    i = lax.broadcasted_iota(jnp.int32, (rows, BLOCK), 0) & (BLOCK - 1)
    j = lax.broadcasted_iota(jnp.int32, (rows, BLOCK), 1)
    rel_c = (i - j).astype(F32)
    rel_p = rel_c + float(BLOCK)
    nv = jnp.zeros((rows, BLOCK), jnp.int32) + n
    ok_c = (i >= j) & (nv >= 1)
    ok_p = (j > i) & (nv >= 2)
    im = lax.broadcasted_iota(jnp.int32, (rows, N_META), 0) & (BLOCK - 1)
    jm = lax.broadcasted_iota(jnp.int32, (rows, N_META), 1)
    ok_m = ((jnp.zeros((rows, N_META), jnp.int32) + n) >= 1) | (im >= PAD_ROWS + jm)
    first = lax.broadcasted_iota(jnp.int32, (BLOCK, LANES), 1) < HEAD_DIM
    neg = -jnp.inf
    outs = []
    for kv in range(ATT_KV_HEADS):
        tile, half = divmod(kv, 2)
        lanes = slice(tile * LANES, (tile + 1) * LANES)
        kc2, kp2, km2 = (_both_halves(t[:, lanes], half).astype(BF16) for t in (kc, kp, km))
        vc2, vp2, vm2 = (_both_halves(t[:, lanes], half).astype(BF16) for t in (vc, vp, vm))
        qs, slope, sk = [], [], []
        for pair in range(ATT_GROUP // 2):
            c0 = (kv * ATT_GROUP + 2 * pair) * HEAD_DIM
            qp = q[:, c0:c0 + LANES] * HEAD_DIM ** -0.5
            qs += [jnp.where(first, qp, 0.0), jnp.where(first, 0.0, qp)]
        for g in range(ATT_GROUP):
            slope.append(jnp.full((BLOCK, 1), ALIBI_SLOPES[kv * ATT_GROUP + g], F32))
            sk.append(jnp.broadcast_to(sinks[kv * ATT_GROUP + g], (BLOCK, 1)))
        qs = jnp.concatenate(qs, axis=0).astype(BF16)
        slope = jnp.concatenate(slope, axis=0)
        sk = jnp.concatenate(sk, axis=0)
        sc = jnp.where(ok_c, lax.dot_general(qs, kc2, _NT, preferred_element_type=F32) - slope * rel_c, neg)
        sp = jnp.where(ok_p, lax.dot_general(qs, kp2, _NT, preferred_element_type=F32) - slope * rel_p, neg)
        sm = jnp.where(ok_m, lax.dot_general(qs, km2, _NT, preferred_element_type=F32), neg)
        mx = jnp.maximum(jnp.maximum(jnp.max(sc, axis=1, keepdims=True), jnp.max(sp, axis=1, keepdims=True)),
                         jnp.maximum(jnp.max(sm, axis=1, keepdims=True), sk))
        mx = lax.stop_gradient(mx)
        ec, ep, em, es = jnp.exp(sc - mx), jnp.exp(sp - mx), jnp.exp(sm - mx), jnp.exp(sk - mx)
        den = (es + jnp.sum(ec, axis=1, keepdims=True) + jnp.sum(ep, axis=1, keepdims=True)
               + jnp.sum(em, axis=1, keepdims=True))
        inv = 1.0 / den
        o = (jnp.dot((ec * inv).astype(BF16), vc2, preferred_element_type=F32)
             + jnp.dot((ep * inv).astype(BF16), vp2, preferred_element_type=F32)
             + jnp.dot((em * inv).astype(BF16), vm2, preferred_element_type=F32))
        for pair in range(ATT_GROUP // 2):
            r0 = 2 * pair * BLOCK
            outs.append(jnp.where(first, o[r0:r0 + BLOCK], o[r0 + BLOCK:r0 + 2 * BLOCK]))
    return jnp.concatenate(outs, axis=1) * _silu(z)


def _attn_specs(nb, steps_clamped):
    def blk(t):
        return jnp.minimum(t, nb - 1) if steps_clamped else t

    wide = lambda col: pl.BlockSpec((BLOCK, D_MODEL), lambda t: (blk(t), col))
    cur = lambda col: pl.BlockSpec((BLOCK, KV_WIDTH), lambda t: (blk(t), col))
    prev = lambda col: pl.BlockSpec((BLOCK, KV_WIDTH), lambda t: (jnp.maximum(blk(t) - 1, 0), col))
    meta = lambda col: pl.BlockSpec((N_META, KV_WIDTH), lambda t: (META_ROW_BLOCK, col))
    sinks = pl.BlockSpec((ATT_Q_HEADS, 1, 1), lambda t: (0, 0, 0))
    return [wide(Q_BLOCK), wide(Z_ATT_BLOCK), prev(K_BLOCK), cur(K_BLOCK), prev(V_BLOCK), cur(V_BLOCK),
            meta(K_BLOCK), meta(V_BLOCK), sinks]


def attn_fwd(proj, sinks):
    nb = proj.shape[0] // BLOCK

    def body(q_ref, z_ref, kp_ref, kc_ref, vp_ref, vc_ref, km_ref, vm_ref, sk_ref, o_ref):
        o_ref[...] = _attn_rows(q_ref[...], z_ref[...], kp_ref[...], kc_ref[...], vp_ref[...], vc_ref[...],
                                km_ref[...], vm_ref[...], tuple(sk_ref[h] for h in range(ATT_Q_HEADS)),
                                pl.program_id(0)).astype(BF16)

    return _call(body, name="attn_fwd", grid=(nb,), in_specs=_attn_specs(nb, False), out_specs=_row_spec(D_MODEL),
                 out_shape=jax.ShapeDtypeStruct((nb * BLOCK, D_MODEL), BF16), sem=("parallel",))(*([proj] * 8), sinks)


def attn_bwd(da, proj, sinks):
    nb = proj.shape[0] // BLOCK
    last = nb - 1
    wide = pl.BlockSpec((BLOCK, D_MODEL), lambda t: (jnp.minimum(t, last), 0))
    done = pl.BlockSpec((BLOCK, KV_WIDTH), lambda t: (jnp.maximum(t - 1, 0), 0))
    meta = _full((N_META, KV_WIDTH))
    par = _full((ATT_Q_HEADS, 1, 1))

    def body(da_ref, q_ref, z_ref, kp_ref, kc_ref, vp_ref, vc_ref, km_ref, vm_ref, sk_ref,
             dq_ref, dz_ref, dk_ref, dv_ref, dkm_ref, dvm_ref, dsk_ref, ck_ref, cv_ref):
        t = pl.program_id(0)

        @pl.when(t == 0)
        def _():
            ck_ref[...] = jnp.zeros_like(ck_ref)
            cv_ref[...] = jnp.zeros_like(cv_ref)
            dkm_ref[...] = jnp.zeros_like(dkm_ref)
            dvm_ref[...] = jnp.zeros_like(dvm_ref)
            dsk_ref[...] = jnp.zeros_like(dsk_ref)

        @pl.when(t < nb)
        def _():
            def f(q, z, kp, kc, vp, vc, km, vm, sk):
                return _attn_rows(q, z, kp, kc, vp, vc, km, vm, sk, t)

            _, vjp = jax.vjp(f, q_ref[...], z_ref[...], kp_ref[...], kc_ref[...], vp_ref[...], vc_ref[...],
                             km_ref[...], vm_ref[...], tuple(sk_ref[h] for h in range(ATT_Q_HEADS)))
            dq, dz, dkp, dkc, dvp, dvc, dkm, dvm, dsk = vjp(da_ref[...])
            dq_ref[...] = dq.astype(BF16)
            dz_ref[...] = dz.astype(BF16)
            for h in range(ATT_Q_HEADS):
                dsk_ref[h] += dsk[h]
            dk_ref[...] = ck_ref[...] + dkp
            dv_ref[...] = cv_ref[...] + dvp
            ck_ref[...] = dkc
            cv_ref[...] = dvc
            dkm_ref[...] += dkm
            dvm_ref[...] += dvm

        @pl.when(t == nb)
        def _():
            dk_ref[...] = ck_ref[...]
            dv_ref[...] = cv_ref[...]

    rows = nb * BLOCK
    return _call(body, name="attn_bwd", grid=(nb + 1,), in_specs=[wide] + _attn_specs(nb, True),
                 out_specs=[wide, wide, done, done, meta, meta, par],
                 out_shape=[jax.ShapeDtypeStruct((rows, D_MODEL), BF16), jax.ShapeDtypeStruct((rows, D_MODEL), BF16),
                            jax.ShapeDtypeStruct((rows, KV_WIDTH), F32), jax.ShapeDtypeStruct((rows, KV_WIDTH), F32),
                            jax.ShapeDtypeStruct((N_META, KV_WIDTH), F32), jax.ShapeDtypeStruct((N_META, KV_WIDTH), F32),
                            jax.ShapeDtypeStruct(sinks.shape, F32)],
                 scratch=[pltpu.VMEM((BLOCK, KV_WIDTH), F32), pltpu.VMEM((BLOCK, KV_WIDTH), F32)],
                 sem=("arbitrary",))(da, *([proj] * 8), sinks)


XBC_BLOCK0 = SEG["xbc"][2] // D_MODEL
CONV_COL_BLOCKS = CONV_DIM // D_MODEL
DT_TILE = SEG["dt"][2] // LANES


HALO = 8
HALOS_PER_BLOCK = BLOCK // HALO


def _shift_rows(cur, before, j):
    if j == 0:
        return cur
    n = cur.shape[0]
    row = lax.broadcasted_iota(jnp.int32, cur.shape, 0)
    head = pltpu.roll(before, j, 0)
    if n > HALO:
        head = jnp.concatenate([head, jnp.zeros((n - HALO, cur.shape[1]), cur.dtype)], axis=0)
    return jnp.where(row >= j, pltpu.roll(cur, j, 0), head)


def _conv_pre(cur, before, w_ref, b_ref):
    pre = b_ref[...] + w_ref[CONV_WIDTH - 1:CONV_WIDTH, :] * cur
    for k in range(CONV_WIDTH - 1):
        pre = pre + w_ref[k:k + 1, :] * _shift_rows(cur, before, CONV_WIDTH - 1 - k)
    return pre


def _conv_specs(nb, col0=0):
    first = XBC_BLOCK0 + col0
    cur = pl.BlockSpec((BLOCK, D_MODEL), lambda j, i: (i, first + j))
    before = pl.BlockSpec((HALO, D_MODEL), lambda j, i: (jnp.maximum(i * HALOS_PER_BLOCK - 1, 0), first + j))
    after = pl.BlockSpec((HALO, D_MODEL), lambda j, i: (jnp.minimum(i + 1, nb - 1) * HALOS_PER_BLOCK, first + j))
    return cur, before, after


def _valid_rows(i):
    row = lax.broadcasted_iota(jnp.int32, (BLOCK, D_MODEL), 0)
    return jnp.maximum((row >= PAD_ROWS).astype(F32), jnp.where(i > 0, 1.0, 0.0))


def conv_fwd(proj, conv_w, conv_b):
    nb = proj.shape[0] // BLOCK
    cur, before, _ = _conv_specs(nb)

    def body(c_ref, p_ref, w_ref, b_ref, o_ref):
        i = pl.program_id(1)
        pre = _conv_pre(c_ref[...], p_ref[...] * jnp.where(i > 0, 1.0, 0.0), w_ref, b_ref)
        o_ref[...] = _silu(pre) * _valid_rows(i)

    return _call(body, name="conv_fwd", grid=(CONV_COL_BLOCKS, nb),
                 in_specs=[cur, before, pl.BlockSpec((CONV_WIDTH, D_MODEL), lambda j, i: (0, j)),
                           pl.BlockSpec((1, D_MODEL), lambda j, i: (0, j))],
                 out_specs=pl.BlockSpec((BLOCK, D_MODEL), lambda j, i: (i, j)),
                 out_shape=jax.ShapeDtypeStruct((nb * BLOCK, CONV_DIM), F32),
                 sem=("parallel", "parallel"))(proj, proj, conv_w, conv_b)


def conv_bwd(name, dparts, col0, proj, conv_w, conv_b):
    nb = proj.shape[0] // BLOCK
    last = nb - 1
    ncol = sum(d.shape[1] for d in dparts) // D_MODEL
    np_ = len(dparts)
    cur, before, after = _conv_specs(nb, col0)
    dcur = [pl.BlockSpec((BLOCK, d.shape[1] // ncol), lambda j, i: (i, j)) for d in dparts]
    dafter = [pl.BlockSpec((HALO, d.shape[1] // ncol), lambda j, i: (jnp.minimum(i + 1, last) * HALOS_PER_BLOCK, j))
              for d in dparts]
    out_cur = pl.BlockSpec((BLOCK, D_MODEL), lambda j, i: (i, j))
    wspec = pl.BlockSpec((CONV_WIDTH, D_MODEL), lambda j, i: (0, col0 + j))
    bspec = pl.BlockSpec((1, D_MODEL), lambda j, i: (0, col0 + j))
    wout = pl.BlockSpec((CONV_WIDTH, D_MODEL), lambda j, i: (0, j))
    bout = pl.BlockSpec((1, D_MODEL), lambda j, i: (0, j))

    def body(*refs):
        dc_refs, da_refs = refs[:np_], refs[np_:2 * np_]
        c_ref, p_ref, a_ref, w_ref, b_ref, du_ref, dw_ref, db_ref = refs[2 * np_:]
        i = pl.program_id(1)
        row = lax.broadcasted_iota(jnp.int32, (BLOCK, D_MODEL), 0)
        curv = c_ref[...]
        beforev = p_ref[...] * jnp.where(i > 0, 1.0, 0.0)
        side_by_side = lambda rs: rs[0][...] if np_ == 1 else jnp.concatenate([r[...] for r in rs], axis=1)

        def dpre_of(pre, d):
            s = jax.nn.sigmoid(pre)
            return d * (s * (1.0 + pre * (1.0 - s)))

        dp_c = dpre_of(_conv_pre(curv, beforev, w_ref, b_ref), side_by_side(dc_refs) * _valid_rows(i))
        dp_a = dpre_of(_conv_pre(a_ref[...], curv[BLOCK - HALO:], w_ref, b_ref),
                       side_by_side(da_refs) * jnp.where(i < last, 1.0, 0.0))
        du = w_ref[CONV_WIDTH - 1:CONV_WIDTH, :] * dp_c
        for j in range(1, CONV_WIDTH):
            tail = jnp.concatenate([jnp.zeros((BLOCK - HALO, D_MODEL), F32), pltpu.roll(dp_a, HALO - j, 0)], axis=0)
            up = jnp.where(row < BLOCK - j, pltpu.roll(dp_c, BLOCK - j, 0), tail)
            du = du + w_ref[CONV_WIDTH - 1 - j:CONV_WIDTH - j, :] * up
        du_ref[...] = du.astype(BF16)

        @pl.when(i == 0)
        def _():
            dw_ref[...] = jnp.zeros_like(dw_ref)
            db_ref[...] = jnp.zeros_like(db_ref)

        for k in range(CONV_WIDTH):
            dw_ref[k:k + 1, :] += jnp.sum(dp_c * _shift_rows(curv, beforev, CONV_WIDTH - 1 - k), axis=0, keepdims=True)
        db_ref[...] += jnp.sum(dp_c, axis=0, keepdims=True)

    width = ncol * D_MODEL
    return _call(body, name=name, grid=(ncol, nb),
                 in_specs=dcur + dafter + [cur, before, after, wspec, bspec], out_specs=[out_cur, wout, bout],
                 out_shape=[jax.ShapeDtypeStruct((nb * BLOCK, width), BF16),
                            jax.ShapeDtypeStruct((CONV_WIDTH, width), F32), jax.ShapeDtypeStruct((1, width), F32)],
                 sem=("parallel", "arbitrary"))(*dparts, *dparts, proj, proj, proj, conv_w, conv_b)


def _head_expand():
    e = np.zeros((LANES, SSM_INNER), np.float32)
    for h in range(SSM_HEADS):
        e[h, h * HEAD_DIM:(h + 1) * HEAD_DIM] = 1.0
    return jnp.asarray(e, dtype=BF16)


def _softplus(x):
    return jnp.maximum(x, 0.0) + jnp.log(1.0 + jnp.exp(-jnp.abs(x)))


def _bf16_parts(x):
    hi = x.astype(BF16)
    rest = x - hi.astype(F32)
    mid = rest.astype(BF16)
    return hi, mid, (rest - mid.astype(F32)).astype(BF16)


@jax.custom_vjp
def _times_01(x, m):
    return sum(jnp.dot(p, m, preferred_element_type=F32) for p in _bf16_parts(x))


def _times_01_bwd(m, g):
    return sum(lax.dot_general(p, m, _NT, preferred_element_type=F32) for p in _bf16_parts(g)), jnp.zeros_like(m)


_times_01.defvjp(lambda x, m: (_times_01(x, m), m), _times_01_bwd)


def _causal_ones():
    l = lax.broadcasted_iota(jnp.int32, (BLOCK, BLOCK), 0)
    s = lax.broadcasted_iota(jnp.int32, (BLOCK, BLOCK), 1)
    return (l >= s).astype(BF16)


@jax.custom_vjp
def _cumsum_rows(a):
    return sum(jnp.dot(_causal_ones(), p, preferred_element_type=F32) for p in _bf16_parts(a))


def _cumsum_rows_bwd(_, g):
    tn = (((0,), (0,)), ((), ()))
    return (sum(lax.dot_general(_causal_ones(), p, tn, preferred_element_type=F32) for p in _bf16_parts(g)),)


_cumsum_rows.defvjp(lambda a: (_cumsum_rows(a), None), _cumsum_rows_bwd)


def _ssd_group(xs, dt_tile, expand, bias, alog, dsk, bg, cg, state):
    l = lax.broadcasted_iota(jnp.int32, (BLOCK, BLOCK), 0)
    s = lax.broadcasted_iota(jnp.int32, (BLOCK, BLOCK), 1)
    causal = l >= s
    first_head = s < HEAD_DIM
    dt = _softplus(dt_tile + bias)
    a = dt * (-jnp.exp(alog))
    one_row = lambda v: jnp.broadcast_to(v, (HALO, LANES))
    per_lane = _times_01(jnp.concatenate([dt, _cumsum_rows(a), one_row(jnp.sum(a, axis=0, keepdims=True)),
                                          one_row(dsk)], axis=0), expand)
    dtx, cs = per_lane[0:BLOCK], per_lane[BLOCK:2 * BLOCK]
    tot, dsk = per_lane[2 * BLOCK:2 * BLOCK + 1], per_lane[2 * BLOCK + HALO:2 * BLOCK + HALO + 1]
    bb, cb16 = bg.astype(BF16), cg.astype(BF16)
    cb = lax.dot_general(cb16, bb, _NT, preferred_element_type=F32)
    xr = xs * dtx
    y_diag = []
    for p in range(GROUP_W // LANES):
        lanes = slice(p * LANES, (p + 1) * LANES)
        c_pair = cs[:, lanes]
        c_swap = _swap_halves(c_pair)
        m = []
        for c_head in (jnp.where(first_head, c_pair, c_swap), jnp.where(first_head, c_swap, c_pair)):
            m.append(cb * jnp.exp(jnp.where(causal, c_head - c_head.T, -jnp.inf)))
        x_pair = xr[:, lanes]
        x_diag = jnp.concatenate([jnp.where(first_head, x_pair, 0.0), jnp.where(first_head, 0.0, x_pair)], axis=0)
        y_diag.append(jnp.dot(jnp.concatenate(m, axis=1).astype(BF16), x_diag.astype(BF16),
                              preferred_element_type=F32))
    st = lax.dot_general(bb, (xr * jnp.exp(tot - cs)).astype(BF16), (((0,), (0,)), ((), ())),
                         preferred_element_type=F32)
    new_state = state * jnp.exp(tot) + st
    y_off = jnp.dot(cb16, state.astype(BF16), preferred_element_type=F32) * jnp.exp(cs)
    return jnp.concatenate(y_diag, axis=1) + y_off + dsk * xs, new_state


B_TILE0 = SSM_INNER // LANES
C_TILE0 = B_TILE0 + SSM_GROUPS


def _ssd_specs(chunk):
    xs = pl.BlockSpec((BLOCK, GROUP_W), lambda c, g: (chunk(c), g))
    dt = pl.BlockSpec((BLOCK, LANES), lambda c, g: (chunk(c), DT_TILE))
    expand = pl.BlockSpec((LANES, GROUP_W), lambda c, g: (0, g))
    b = pl.BlockSpec((BLOCK, SSM_STATE), lambda c, g: (chunk(c), B_TILE0 + g))
    cc = pl.BlockSpec((BLOCK, SSM_STATE), lambda c, g: (chunk(c), C_TILE0 + g))
    par = _full((1, LANES))
    state = pl.BlockSpec((None, SSM_STATE, GROUP_W), lambda c, g: (chunk(c), 0, g))
    return xs, dt, expand, b, cc, par, state


def ssd_fwd(xbc, proj, expand, bias, alog, dsk):
    nb = xbc.shape[0] // BLOCK
    xs, dt, ex, b, cc, par, state = _ssd_specs(lambda c: c)

    def body(x_ref, dt_ref, e_ref, bi_ref, al_ref, dk_ref, b_ref, c_ref, y_ref, sp_ref, st_ref):
        g = pl.program_id(1)

        @pl.when(pl.program_id(0) == 0)
        def _():
            st_ref[g] = jnp.zeros((SSM_STATE, GROUP_W), F32)

        entering = st_ref[g]
        sp_ref[...] = entering
        y_ref[...], st_ref[g] = _ssd_group(x_ref[...], dt_ref[...], e_ref[...], bi_ref[...], al_ref[...], dk_ref[...],
                                           b_ref[...], c_ref[...], entering)

    return _call(body, name="ssd_fwd", grid=(nb, SSM_GROUPS), in_specs=[xs, dt, ex, par, par, par, b, cc],
                 out_specs=[xs, state],
                 out_shape=[jax.ShapeDtypeStruct((nb * BLOCK, SSM_INNER), F32),
                            jax.ShapeDtypeStruct((nb, SSM_STATE, SSM_INNER), F32)],
                 scratch=[pltpu.VMEM((SSM_GROUPS, SSM_STATE, GROUP_W), F32)],
                 sem=("arbitrary", "arbitrary"))(xbc, proj, expand, bias, alog, dsk, xbc, xbc)


def ssd_bwd(dy, xbc, proj, expand, bias, alog, dsk, states, comm):
    nb = xbc.shape[0] // BLOCK
    last = nb - 1
    xs, dt, ex, b, cc, par, state = _ssd_specs(lambda c: last - c)
    tile = pl.BlockSpec((BLOCK, LANES), lambda c, g: (last - c, 0))
    nspec = pl.BlockSpec((BLOCK, SSM_STATE), lambda c, g: (last - c, g))

    def body(dy_ref, x_ref, dt_ref, e_ref, bi_ref, al_ref, dk_ref, b_ref, c_ref, sp_ref,
             dx_ref, ddt_ref, db_ref, dc_ref, dbi_ref, dal_ref, ddk_ref, ds_ref, ddt_acc):
        c, g = pl.program_id(0), pl.program_id(1)

        @pl.when(c == 0)
        def _():
            ds_ref[g] = jnp.zeros((SSM_STATE, GROUP_W), F32)

        @pl.when((c == 0) & (g == 0))
        def _():
            dbi_ref[...] = jnp.zeros_like(dbi_ref)
            dal_ref[...] = jnp.zeros_like(dal_ref)
            ddk_ref[...] = jnp.zeros_like(ddk_ref)

        expand_rows = e_ref[...]

        def f(xs, dt_tile, bias, alog, dsk, bg, cg, state):
            return _ssd_group(xs, dt_tile, expand_rows, bias, alog, dsk, bg, cg, state)

        _, vjp = jax.vjp(f, x_ref[...], dt_ref[...], bi_ref[...], al_ref[...], dk_ref[...], b_ref[...], c_ref[...],
                         sp_ref[...])
        dx_ref[...], ddt, dbi, dal, ddk, db_ref[...], dc_ref[...], ds_ref[g] = vjp((dy_ref[...], ds_ref[g]))
        dbi_ref[...] += dbi
        dal_ref[...] += dal
        ddk_ref[...] += ddk

        @pl.when(g == 0)
        def _():
            ddt_acc[...] = ddt

        @pl.when(g > 0)
        def _():
            ddt_acc[...] += ddt

        @pl.when(g == SSM_GROUPS - 1)
        def _():
            ddt_ref[...] = ddt_acc[...].astype(BF16)

    def at():
        c, g = pl.program_id(0), pl.program_id(1)
        return (c == 0) & (g == 0), (c == 0) & (g == 1), (c == last) & (g == SSM_GROUPS - 1)

    par_shape = jax.ShapeDtypeStruct((1, LANES), F32)
    return _call_with_comm(
        body, comm, at, (dy, xbc, proj, expand, bias, alog, dsk, xbc, xbc, states), name="ssd_bwd",
        grid=(nb, SSM_GROUPS), in_specs=[xs, xs, dt, ex, par, par, par, b, cc, state],
        out_specs=[xs, tile, nspec, nspec, par, par, par],
        out_shape=[jax.ShapeDtypeStruct((nb * BLOCK, SSM_INNER), F32), jax.ShapeDtypeStruct((nb * BLOCK, LANES), BF16),
                   jax.ShapeDtypeStruct((nb * BLOCK, SSM_GROUPS * SSM_STATE), F32),
                   jax.ShapeDtypeStruct((nb * BLOCK, SSM_GROUPS * SSM_STATE), F32), par_shape, par_shape, par_shape],
        scratch=[pltpu.VMEM((SSM_GROUPS, SSM_STATE, GROUP_W), F32), pltpu.VMEM((BLOCK, LANES), F32)])


SLAB_ROWS = 24
SLAB_META_ROW = 8


SLAB_LOSS_ROW = 7


def pack_small(dcw, dcb, dgpre, dgpost, dbias, dalog, ddsk, dsinks, dgn, dmeta, loss_tile):
    def body(cw, cb, gpre, gpost, dtb, al, dk, sk, gn, meta, loss, o_ref):
        o_ref[...] = jnp.zeros_like(o_ref)
        o_ref[SLAB_LOSS_ROW:SLAB_LOSS_ROW + 1, 0:LANES] = loss[0:1, :]
        o_ref[0:CONV_WIDTH, :] = cw[...]
        o_ref[4:5, :] = cb[...]
        o_ref[5:6, 0:1024] = gpre[...]
        o_ref[5:6, 1024:2048] = gpost[...]
        o_ref[5:6, 2048:2176] = dtb[...]
        o_ref[5:6, 2176:2304] = al[...]
        o_ref[5:6, 2304:2432] = dk[...]
        o_ref[5:6, 2432:2560] = sk[...]
        o_ref[6:7, 0:SSM_INNER] = gn[...]
        o_ref[SLAB_META_ROW:SLAB_META_ROW + N_META, 0:D_MODEL] = meta[...]

    args = (dcw, dcb, dgpre, dgpost, dbias, dalog, ddsk, dsinks, dgn, dmeta, loss_tile)
    return _call(body, name="pack_small", in_specs=[_full(a.shape) for a in args],
                 out_specs=_full((SLAB_ROWS, CONV_DIM)), out_shape=jax.ShapeDtypeStruct((SLAB_ROWS, CONV_DIM), F32))(*args)


def _lane_tile(v):
    return jnp.pad(v, ((0, 0), (0, LANES - v.shape[1])))


def kernel(x, meta_tokens, g_pre, w_in, conv_w, conv_b, dt_bias, a_log, d_skip, attn_sinks, g_ssm_norm, w_out_att, w_out_ssm, w_out, g_post, loss_target, m_meta_tokens, m_g_pre, m_w_in, m_conv_w, m_conv_b, m_dt_bias, m_a_log, m_d_skip, m_attn_sinks, m_g_ssm_norm, m_w_out_att, m_w_out_ssm, m_w_out, m_g_post, v_meta_tokens, v_g_pre, v_w_in, v_conv_w, v_conv_b, v_dt_bias, v_a_log, v_d_skip, v_attn_sinks, v_g_ssm_norm, v_w_out_att, v_w_out_ssm, v_w_out, v_g_post):
    chip = _chip_index()

    conv_w_rows = jnp.pad(conv_w[0], ((0, 2 * 8 - CONV_WIDTH), (0, 0)))
    w_in_t, m_w_in_t, v_w_in_t = w_in[0].T, m_w_in[0].T, v_w_in[0].T
    gathered_w_in, g_conv_w, g_meta = run_comm("gather_w_in",
                                               TwoLevelGather([pack_w_in(w_in_t), conv_w_rows, meta_tokens]))
    w_all_t = unpack_w_in(gathered_w_in)
    cw_full = g_conv_w[:, :CONV_WIDTH].transpose(1, 0, 2).reshape(CONV_WIDTH, CONV_DIM)
    meta_full = g_meta.transpose(1, 0, 2).reshape(N_META, D_MODEL)

    h, u = prep(x, meta_full, g_pre)
    proj, g_w_out = project("in_proj", u, w_all_t, TwoLevelGather(
        [w_out_att[0].astype(BF16), w_out_ssm[0].astype(BF16), w_out[0].astype(BF16)]))
    woa = g_w_out[0].reshape(D_MODEL, D_MODEL)
    wos = g_w_out[1].reshape(SSM_INNER, D_MODEL)
    wo = g_w_out[2].reshape(D_MODEL, D_MODEL)

    sinks3 = attn_sinks.reshape(ATT_Q_HEADS, 1, 1)
    a_att = attn_fwd(proj, sinks3)

    xbc = conv_fwd(proj, cw_full, conv_b)
    expand = _head_expand()
    head_pars = (_lane_tile(dt_bias), _lane_tile(a_log), _lane_tile(d_skip))
    y_ssd, states = ssd_fwd(xbc, proj, expand, *head_pars)

    (yn, merged, dout, dy_att, dy_ssm, da_att, dy_ssd, dz_ssm, dga, dgs, dres, loss_tile, dg_post, dgn) = tail(
        y_ssd, proj, a_att, x, loss_target, woa, wos, wo, g_ssm_norm, g_post)

    dwo = mm_tn("out_proj_dw", merged, dout)
    dwoa = mm_tn("att_out_dw", a_att, dy_att)
    dwos = mm_tn("ssm_out_dw", yn, dy_ssm)
    dq, dz_att, dk, dv, dkmeta, dvmeta, dsinks3 = attn_bwd(da_att, proj, sinks3)
    dk = dk.at[PAD_ROWS:BLOCK].add(dkmeta).astype(BF16)
    dv = dv.at[PAD_ROWS:BLOCK].add(dvmeta).astype(BF16)

    def pieces(g):
        return g.reshape(4, 2, g.shape[0] // 8, g.shape[1])

    def to_owner(g):
        return (lambda ref, dev: ref.at[_chip_of(dev), dev[2]], (g.shape[0] // 8, g.shape[1]))

    (dxs, ddt_tile, dbg, dcg, dbias, dalog, ddsk), sent_w_out = ssd_bwd(
        dy_ssd, xbc, proj, expand, *head_pars, states,
        DirectExchange([pieces(dwoa), pieces(dwos), pieces(dwo)], [to_owner(dwoa), to_owner(dwos), to_owner(dwo)],
                       ALL_MASKS, "dev", 8))
    dxs_raw, dcw_xs, dcb_xs = conv_bwd("conv_bwd_x", [dxs], 0, proj, cw_full, conv_b)
    dbc_raw, dcw_bc, dcb_bc = conv_bwd("conv_bwd_bc", [dbg, dcg], SSM_INNER // D_MODEL, proj, cw_full, conv_b)
    dcw = jnp.concatenate([dcw_xs, dcw_bc], axis=1)
    dcb = jnp.concatenate([dcb_xs, dcb_bc], axis=1)

    dproj = jnp.concatenate([dz_ssm, dxs_raw, dbc_raw, dq, dz_att, dga, dgs, dk, dv, ddt_tile], axis=1)
    dw_all_t = weight_grad_t("in_proj_dw", dproj, u)

    half_rows = PACK_W // 2
    partial = pack_grad_w_in(dw_all_t).reshape(4, 2, half_rows, D_MODEL)
    from_sibling, = run_comm("pair_grads", DirectExchange(
        [partial], [(lambda ref, dev: ref.at[pl.ds(0, 4), dev[2]], (4, half_rows, D_MODEL))], SIBLING_MASK, "core", 2,
        keep_own=False))
    chip_sum = sum_pair(partial, from_sibling)
    du, (sent_w_in,) = project_back("in_proj_dx", dproj, w_all_t, DirectExchange(
        [chip_sum], [(lambda ref, dev: ref.at[_chip_of(dev)], (half_rows, D_MODEL))], CHIP_MASKS, "chip", 4))
    grad_x, dmeta, dg_pre = prep_bwd(h, du, dres, g_pre)

    slab = pack_small(dcw, dcb, dg_pre, dg_post, dbias, dalog, ddsk, _lane_tile(dsinks3.reshape(1, ATT_Q_HEADS)), dgn,
                      dmeta, loss_tile)
    halves = [sum_slots("sum_" + nm, r)
              for nm, r in zip(("w_in", "w_out_att", "w_out_ssm", "w_out"), [sent_w_in] + list(sent_w_out))]
    shared = run_comm("share_grads", Both(DirectExchange(halves, [None] * 4, SIBLING_MASK, "core", 2),
                                          DirectExchange([slab], [None], ALL_MASKS, "dev", 8)))
    g_w_in_packed, g_woa, g_wos, g_wo = [f.reshape(2 * f.shape[1], f.shape[2]) for f in shared[:4]]
    small = sum_slots("sum_small", shared[4])
    loss = small[SLAB_LOSS_ROW, 0]

    g_w_in, d_w_in, nm_w_in, nv_w_in = [a.T for a in adamw_w_in(g_w_in_packed, w_in_t, m_w_in_t, v_w_in_t)]
    d_woa, nm_woa, nv_woa = adamw_rows("adamw_w_out_att", g_woa, w_out_att[0], m_w_out_att[0], v_w_out_att[0])
    d_wos, nm_wos, nv_wos = adamw_rows("adamw_w_out_ssm", g_wos, w_out_ssm[0], m_w_out_ssm[0], v_w_out_ssm[0])
    d_wo, nm_wo, nv_wo = adamw_rows("adamw_w_out", g_wo, w_out[0], m_w_out[0], v_w_out[0])

    cw_cols = CONV_DIM // 4
    meta_cols = D_MODEL // 4
    g_small = {
        "meta_tokens": lax.dynamic_slice(small, (SLAB_META_ROW, chip * meta_cols), (N_META, meta_cols)),
        "g_pre": small[5:6, 0:1024],
        "conv_w": lax.dynamic_slice(small, (0, chip * cw_cols), (CONV_WIDTH, cw_cols)),
        "conv_b": small[4:5, :],
        "dt_bias": small[5:6, 2048:2048 + SSM_HEADS],
        "a_log": small[5:6, 2176:2176 + SSM_HEADS],
        "d_skip": small[5:6, 2304:2304 + SSM_HEADS],
        "attn_sinks": small[5:6, 2432:2432 + ATT_Q_HEADS],
        "g_ssm_norm": small[6:7, 0:SSM_INNER],
        "g_post": small[5:6, 1024:2048],
    }
    names = list(g_small)
    w_small = dict(meta_tokens=meta_tokens, g_pre=g_pre, conv_w=conv_w[0], conv_b=conv_b, dt_bias=dt_bias, a_log=a_log,
                   d_skip=d_skip, attn_sinks=attn_sinks, g_ssm_norm=g_ssm_norm, g_post=g_post)
    m_small = dict(meta_tokens=m_meta_tokens, g_pre=m_g_pre, conv_w=m_conv_w[0], conv_b=m_conv_b, dt_bias=m_dt_bias,
                   a_log=m_a_log, d_skip=m_d_skip, attn_sinks=m_attn_sinks, g_ssm_norm=m_g_ssm_norm, g_post=m_g_post)
    v_small = dict(meta_tokens=v_meta_tokens, g_pre=v_g_pre, conv_w=v_conv_w[0], conv_b=v_conv_b, dt_bias=v_dt_bias,
                   a_log=v_a_log, d_skip=v_d_skip, attn_sinks=v_attn_sinks, g_ssm_norm=v_g_ssm_norm, g_post=v_g_post)
    upd = dict(zip(names, adamw_small([g_small[k] for k in names], [w_small[k] for k in names],
                                      [m_small[k] for k in names], [v_small[k] for k in names])))

    lead = {"conv_w"}

    def shaped(name, a):
        return a[None] if name in lead else a

    grads = dict(g_small, w_in=g_w_in, w_out_att=g_woa, w_out_ssm=g_wos, w_out=g_wo)
    deltas = dict({k: upd[k][0] for k in names}, w_in=d_w_in, w_out_att=d_woa, w_out_ssm=d_wos, w_out=d_wo)
    new_m = dict({k: upd[k][1] for k in names}, w_in=nm_w_in, w_out_att=nm_woa, w_out_ssm=nm_wos, w_out=nm_wo)
    new_v = dict({k: upd[k][2] for k in names}, w_in=nv_w_in, w_out_att=nv_woa, w_out_ssm=nv_wos, w_out=nv_wo)
    lead |= {"w_in", "w_out_att", "w_out_ssm", "w_out"}
    order = ["meta_tokens", "g_pre", "w_in", "conv_w", "conv_b", "dt_bias", "a_log", "d_skip", "attn_sinks",
             "g_ssm_norm", "w_out_att", "w_out_ssm", "w_out", "g_post"]
    outs = [loss, grad_x]
    for group in (grads, deltas, new_m, new_v):
        outs += [shaped(k, group[k]) for k in order]
    return tuple(outs)
```

```python
import functools

import numpy as np
import jax
import jax.numpy as jnp
from jax import lax
from jax.experimental import pallas as pl
from jax.experimental.pallas import tpu as pltpu

F32 = jnp.float32
BF16 = jnp.bfloat16
HI = lax.Precision.HIGHEST

D_MODEL = 1024
N_META = 16
BLOCK = 128
PAD_ROWS = BLOCK - N_META
NORM_EPS = 1e-6
HEAD_DIM = 64
ATT_Q_HEADS = 16
ATT_KV_HEADS = 4
ATT_GROUP = 4
SSM_INNER = 2048
SSM_HEADS = 32
SSM_GROUPS = 4
SSM_HEADS_PER_GROUP = 8
SSM_STATE = 128
CONV_WIDTH = 4
CONV_DIM = 3072
LANES = 128

ADAM_LR = 0.001
ADAM_B1 = 0.9
ADAM_B2 = 0.999
ADAM_EPS = 1e-08
ADAM_WD = 0.01
ADAM_STEP = 10

VMEM_LIMIT = 48 * 1024 * 1024

SHARD_W = 2440
PACK_W = 2560
SHARD_STRIDE = 2432
N_ALIGNED = 9856
N_ACT = 10240
SEG = {
    "q": (0, 1024, 5120), "k": (1024, 256, 9216), "v": (1280, 256, 9472), "z_att": (1536, 1024, 6144),
    "z_ssm": (2560, 2048, 0), "xbc": (4608, 3072, 2048), "dt": (7680, 128, 9728),
    "gate_att": (7808, 1024, 7168), "gate_ssm": (8832, 1024, 8192),
}
DT_STORED_START = 7680
DT_PAD = LANES - SSM_HEADS


def _act_col(aligned_col):
    for a0, w, p0 in SEG.values():
        if a0 <= aligned_col < a0 + w:
            return p0 + aligned_col - a0
    raise ValueError(aligned_col)


def _call(body, *, name, out_shape, in_specs, out_specs, grid=(), scratch=(), sem=None, aliases=None):
    return pl.pallas_call(
        body, out_shape=out_shape, grid=grid, in_specs=in_specs, out_specs=out_specs, scratch_shapes=list(scratch),
        name=name, input_output_aliases=aliases or {},
        compiler_params=pltpu.CompilerParams(dimension_semantics=sem, vmem_limit_bytes=VMEM_LIMIT))


def _full(shape):
    n = len(shape)
    return pl.BlockSpec(shape, lambda *_: (0,) * n)


def _chip_index():
    return lax.axis_index("x") * 2 + lax.axis_index("y")


def _sigmoid(z):
    return 0.5 * jnp.tanh(0.5 * z) + 0.5


def _silu(z):
    return z * _sigmoid(z)


def _rms(x, g):
    return x * lax.rsqrt(jnp.mean(x * x, axis=-1, keepdims=True) + NORM_EPS) * g


def _peer(mask):
    x, y, c = lax.axis_index("x"), lax.axis_index("y"), lax.axis_index("c")
    return ((1 - x) if mask & 4 else x, (1 - y) if mask & 2 else y, (1 - c) if mask & 1 else c)


def _me():
    return lax.axis_index("x"), lax.axis_index("y"), lax.axis_index("c")


def _chip_of(dev):
    return 2 * dev[0] + dev[1]


CHIP_MASKS = (4, 2, 6)
ALL_MASKS = (1, 2, 3, 4, 5, 6, 7)
SIBLING_MASK = (1,)


def _remote(src, dst, send_sem, recv_sem, dev):
    return pltpu.make_async_remote_copy(src_ref=src, dst_ref=dst, send_sem=send_sem, recv_sem=recv_sem,
                                        device_id=dev, device_id_type=pl.DeviceIdType.MESH)


class _StagedCopy:
    def __init__(self, src, stage, dst, load_sem, store_sem):
        self.load = pltpu.make_async_copy(src, stage, load_sem)
        self.store = pltpu.make_async_copy(stage, dst, store_sem)

    def start(self):
        self.load.start()
        self.load.wait()
        self.store.start()

    def wait(self):
        self.store.wait()


class DirectExchange:
    def __init__(self, arrays, pieces, masks, slot_kind, nslots, keep_own=True):
        self.arrays, self.pieces, self.masks, self.slot_kind = list(arrays), list(pieces), masks, slot_kind
        self.keep_own = keep_own
        n, nk = len(arrays), len(masks)
        shapes = [a.shape if p is None else p[1] for a, p in zip(arrays, pieces)]
        self.out_shape = [jax.ShapeDtypeStruct((nslots,) + tuple(s), a.dtype) for s, a in zip(shapes, arrays)]
        self.scratch = [pltpu.SemaphoreType.DMA((n * nk,)), pltpu.SemaphoreType.DMA((n * nk,))]
        if keep_own:
            self.scratch += [pltpu.SemaphoreType.DMA((2 * n,))] + [pltpu.VMEM(s, a.dtype) for s, a in zip(shapes, arrays)]
        self.has_mid = False

    def _copies(self, ins, outs, scratch):
        send_sems, recv_sems = scratch[:2]
        me = _me()
        slot = {"chip": _chip_of(me), "dev": 4 * me[0] + 2 * me[1] + me[2], "core": me[2]}[self.slot_kind]
        nk = len(self.masks)

        def piece(a, dev):
            return ins[a] if self.pieces[a] is None else self.pieces[a][0](ins[a], dev)

        local = []
        if self.keep_own:
            local_sems, stages = scratch[2], scratch[3:]
            local = [_StagedCopy(piece(a, me), stages[a], outs[a].at[slot], local_sems.at[2 * a], local_sems.at[2 * a + 1])
                     for a in range(len(ins))]
        remote = []
        for a in range(len(ins)):
            for ki, mask in enumerate(self.masks):
                dev = _peer(mask)
                remote.append(_remote(piece(a, dev), outs[a].at[slot], send_sems.at[a * nk + ki],
                                      recv_sems.at[a * nk + ki], dev))
        return local, remote

    def start(self, ins, outs, scratch):
        local, remote = self._copies(ins, outs, scratch)
        for cp in remote + local:
            cp.start()

    def finish(self, ins, outs, scratch):
        local, remote = self._copies(ins, outs, scratch)
        for cp in remote + local:
            cp.wait()


class TwoLevelGather:
    def __init__(self, arrays):
        self.arrays = list(arrays)
        n, nk = len(arrays), len(CHIP_MASKS)
        self.out_shape = [jax.ShapeDtypeStruct((4,) + a.shape, a.dtype) for a in arrays]
        self.scratch = ([pltpu.SemaphoreType.DMA((n * nk,)) for _ in range(4)] + [pltpu.SemaphoreType.DMA((2 * n,))]
                        + [pltpu.VMEM(a.shape, a.dtype) for a in arrays])
        self.has_mid = True

    def _copies(self, ins, outs, scratch):
        ici_send, ici_recv, fwd_send, fwd_recv, local_sems = scratch[:5]
        stages = scratch[5:]
        me = _me()
        sibling = _peer(1)
        nk = len(CHIP_MASKS)
        local, ici, fwd = [], [], []
        for a in range(len(ins)):
            half = ins[a].shape[0] // 2
            mine = pl.ds(me[2] * half, half)
            local.append(_StagedCopy(ins[a], stages[a], outs[a].at[_chip_of(me)], local_sems.at[2 * a],
                                     local_sems.at[2 * a + 1]))
            for ki, mask in enumerate(CHIP_MASKS):
                dev = _peer(mask)
                k = a * nk + ki
                ici.append(_remote(ins[a].at[mine], outs[a].at[_chip_of(me), mine], ici_send.at[k], ici_recv.at[k], dev))
                arrived = outs[a].at[_chip_of(dev), mine]
                fwd.append(_remote(arrived, arrived, fwd_send.at[k], fwd_recv.at[k], sibling))
        return local, ici, fwd

    def start(self, ins, outs, scratch):
        local, ici, _ = self._copies(ins, outs, scratch)
        for cp in ici + local:
            cp.start()

    def mid(self, ins, outs, scratch):
        _, ici, fwd = self._copies(ins, outs, scratch)
        for arrival, onward in zip(ici, fwd):
            arrival.wait_recv()
            onward.start()

    def finish(self, ins, outs, scratch):
        local, ici, fwd = self._copies(ins, outs, scratch)
        for cp in fwd:
            cp.wait_recv()
        for cp in ici + fwd:
            cp.wait_send()
        for cp in local:
            cp.wait()


class Both:
    def __init__(self, a, b):
        self.a, self.b = a, b
        self.arrays, self.out_shape = a.arrays + b.arrays, a.out_shape + b.out_shape
        self.scratch = a.scratch + b.scratch
        self.has_mid = False
        assert not (a.has_mid or b.has_mid)

    def _parts(self, ins, outs, sems):
        na, sa = len(self.a.arrays), len(self.a.scratch)
        return (ins[:na], outs[:na], sems[:sa]), (ins[na:], outs[na:], sems[sa:])

    def start(self, ins, outs, sems):
        pa, pb = self._parts(ins, outs, sems)
        self.a.start(*pa)
        self.b.start(*pb)

    def finish(self, ins, outs, sems):
        pa, pb = self._parts(ins, outs, sems)
        self.a.finish(*pa)
        self.b.finish(*pb)


_ANY = pl.BlockSpec(memory_space=pl.ANY)


def run_comm(name, comm):
    n = len(comm.arrays)

    def body(*refs):
        ins, outs, sems = refs[:n], refs[n:2 * n], refs[2 * n:]
        comm.start(ins, outs, sems)
        if comm.has_mid:
            comm.mid(ins, outs, sems)
        comm.finish(ins, outs, sems)

    return pl.pallas_call(body, name=name, out_shape=comm.out_shape, in_specs=[_ANY] * n, out_specs=[_ANY] * n,
                          scratch_shapes=comm.scratch,
                          compiler_params=pltpu.CompilerParams(vmem_limit_bytes=VMEM_LIMIT))(*comm.arrays)


def _call_with_comm(body, comm, steps, args, *, name, out_shape, in_specs, out_specs, grid, scratch=()):
    ni, no, ns, nc = len(in_specs), len(out_specs), len(scratch), len(comm.arrays)

    def full_body(*refs):
        ins, cins = refs[:ni], refs[ni:ni + nc]
        outs, couts = refs[ni + nc:ni + nc + no], refs[ni + nc + no:ni + 2 * nc + no]
        scr, csems = refs[ni + 2 * nc + no:ni + 2 * nc + no + ns], refs[ni + 2 * nc + no + ns:]
        first, middle, last = steps()
        pl.when(first)(lambda: comm.start(cins, couts, csems))
        if comm.has_mid:
            pl.when(middle)(lambda: comm.mid(cins, couts, csems))
        body(*ins, *outs, *scr)
        pl.when(last)(lambda: comm.finish(cins, couts, csems))

    res = pl.pallas_call(
        full_body, name=name, out_shape=list(out_shape) + comm.out_shape, grid=grid,
        in_specs=list(in_specs) + [_ANY] * nc, out_specs=list(out_specs) + [_ANY] * nc,
        scratch_shapes=list(scratch) + comm.scratch,
        compiler_params=pltpu.CompilerParams(dimension_semantics=("arbitrary",) * len(grid),
                                             vmem_limit_bytes=VMEM_LIMIT))(*args, *comm.arrays)
    return res[:no], res[no:]


def _shard_pieces(chip):
    if chip < 3:
        return [(0, SHARD_W, 8 * chip)]
    behind_dt = DT_STORED_START + SSM_HEADS - 3 * SHARD_W
    return [(0, behind_dt, 24), (behind_dt, SHARD_W - behind_dt, behind_dt + 24 + DT_PAD)]


W_IN_COLS = 256


def pack_w_in(wt):
    def body(w_ref, o_ref, pad_ref):
        chip = _chip_index()
        pad_ref[...] = jnp.zeros_like(pad_ref)
        for cv in range(4):
            @pl.when(chip == cv)
            def _():
                for src, n, dst in _shard_pieces(cv):
                    pad_ref[dst:dst + n, :] = w_ref[src:src + n, :]
        o_ref[...] = pad_ref[...].astype(BF16)

    return _call(body, name="pack_w_in", grid=(D_MODEL // W_IN_COLS,),
                 in_specs=[pl.BlockSpec((SHARD_W, W_IN_COLS), lambda i: (0, i))],
                 out_specs=pl.BlockSpec((PACK_W, W_IN_COLS), lambda i: (0, i)),
                 out_shape=jax.ShapeDtypeStruct((PACK_W, D_MODEL), BF16),
                 scratch=[pltpu.VMEM((PACK_W, W_IN_COLS), F32)], sem=("parallel",))(wt)


def _tile_runs():
    runs, fix = [], []
    for t in range(N_ALIGNED // LANES):
        s = min(t // 19, 3)
        j = t - 19 * s
        p = _act_col(t * LANES)
        if runs and runs[-1][1] == s and runs[-1][0] + runs[-1][3] == p and runs[-1][2] + runs[-1][3] == j * LANES:
            runs[-1][3] += LANES
        else:
            runs.append([p, s, j * LANES, LANES])
        if j == 0 and s > 0:
            fix.append((p, s - 1))
    return runs, fix


def unpack_w_in(bg):
    runs, fix = _tile_runs()

    def body(b_ref, o_ref):
        for p, s, j, w in runs:
            o_ref[p:p + w, :] = b_ref[s, j:j + w, :]
        for p, s in fix:
            o_ref[p:p + LANES, :] = o_ref[p:p + LANES, :] + b_ref[s, SHARD_STRIDE:PACK_W, :]
        o_ref[N_ALIGNED:N_ACT, :] = jnp.zeros((N_ACT - N_ALIGNED, W_IN_COLS), BF16)

    return _call(body, name="unpack_w_in", grid=(D_MODEL // W_IN_COLS,),
                 in_specs=[pl.BlockSpec((4, PACK_W, W_IN_COLS), lambda i: (0, 0, i))],
                 out_specs=pl.BlockSpec((N_ACT, W_IN_COLS), lambda i: (0, i)),
                 out_shape=jax.ShapeDtypeStruct((N_ACT, D_MODEL), BF16), sem=("parallel",))(bg)


def pack_grad_w_in(dwt):
    def body(g_ref, o_ref):
        for s in range(4):
            for j in range(PACK_W // LANES):
                p = _act_col((19 * s + j) * LANES)
                o_ref[s, j * LANES:(j + 1) * LANES, :] = g_ref[p:p + LANES, :]

    return _call(body, name="pack_grad_w_in", grid=(D_MODEL // W_IN_COLS,),
                 in_specs=[pl.BlockSpec((N_ACT, W_IN_COLS), lambda i: (0, i))],
                 out_specs=pl.BlockSpec((4, PACK_W, W_IN_COLS), lambda i: (0, 0, i)),
                 out_shape=jax.ShapeDtypeStruct((4, PACK_W, D_MODEL), BF16), sem=("parallel",))(dwt)


def _adamw(w, g, m, v):
    m = ADAM_B1 * m + (1.0 - ADAM_B1) * g
    v = ADAM_B2 * v + (1.0 - ADAM_B2) * jnp.square(g)
    m_hat = m / (1.0 - ADAM_B1 ** ADAM_STEP)
    v_hat = v / (1.0 - ADAM_B2 ** ADAM_STEP)
    delta = -ADAM_LR * (m_hat / (jnp.sqrt(v_hat) + ADAM_EPS) + ADAM_WD * w)
    return delta, m, v


def adamw_w_in(g_packed, wt, mt, vt):
    cols = LANES

    def body(g_ref, w_ref, m_ref, v_ref, go_ref, d_ref, mo_ref, vo_ref):
        chip = _chip_index()
        for cv in range(4):
            @pl.when(chip == cv)
            def _():
                for dst, n, src in _shard_pieces(cv):
                    go_ref[dst:dst + n, :] = g_ref[src:src + n, :]
        d_ref[...], mo_ref[...], vo_ref[...] = _adamw(w_ref[...], go_ref[...], m_ref[...], v_ref[...])

    spec = pl.BlockSpec((SHARD_W, cols), lambda i: (0, i))
    shp = jax.ShapeDtypeStruct((SHARD_W, D_MODEL), F32)
    return _call(body, name="adamw_w_in", grid=(D_MODEL // cols,),
                 in_specs=[pl.BlockSpec((PACK_W, cols), lambda i: (0, i)), spec, spec, spec],
                 out_specs=[spec] * 4, out_shape=[shp] * 4, sem=("parallel",))(g_packed, wt, mt, vt)


def adamw_rows(name, g, w, m, v):
    r, c = g.shape
    rows = min(r, BLOCK)

    def body(g_ref, w_ref, m_ref, v_ref, d_ref, mo_ref, vo_ref):
        d_ref[...], mo_ref[...], vo_ref[...] = _adamw(w_ref[...], g_ref[...], m_ref[...], v_ref[...])

    spec = pl.BlockSpec((rows, c), lambda i: (i, 0))
    shp = jax.ShapeDtypeStruct((r, c), F32)
    return _call(body, name=name, grid=(r // rows,), in_specs=[spec] * 4, out_specs=[spec] * 3, out_shape=[shp] * 3,
                 sem=("parallel",))(g, w, m, v)


def adamw_small(gs, ws, ms, vs):
    n = len(gs)

    def body(*refs):
        g, w, m, v = refs[:n], refs[n:2 * n], refs[2 * n:3 * n], refs[3 * n:4 * n]
        outs = refs[4 * n:]
        for i in range(n):
            d, mn, vn = _adamw(w[i][...], g[i][...], m[i][...], v[i][...])
            outs[3 * i][...] = d
            outs[3 * i + 1][...] = mn
            outs[3 * i + 2][...] = vn

    specs = [_full(a.shape) for a in gs]
    res = _call(body, name="adamw_small", in_specs=specs * 4,
                out_specs=[s for s in specs for _ in range(3)],
                out_shape=[jax.ShapeDtypeStruct(a.shape, F32) for a in gs for _ in range(3)])(*gs, *ws, *ms, *vs)
    return [tuple(res[3 * i:3 * i + 3]) for i in range(n)]


def sum_slots(name, r):
    s, rr, c = r.shape
    rows = min(rr, BLOCK)

    def body(r_ref, o_ref):
        acc = r_ref[0].astype(F32)
        for k in range(1, s):
            acc = acc + r_ref[k].astype(F32)
        o_ref[...] = acc

    return _call(body, name=name, grid=(rr // rows,), in_specs=[pl.BlockSpec((s, rows, c), lambda i: (0, i, 0))],
                 out_specs=pl.BlockSpec((rows, c), lambda i: (i, 0)), out_shape=jax.ShapeDtypeStruct((rr, c), F32),
                 sem=("parallel",))(r)


def sum_pair(partial, from_sibling):
    s, _, rr, cols = partial.shape
    rows = rr // 2

    def body(p_ref, r_ref, o_ref):
        c = lax.axis_index("c")
        o_ref[...] = (p_ref[c].astype(F32) + r_ref[1 - c].astype(F32)).astype(BF16)

    return _call(body, name="sum_pair", grid=(s, rr // rows),
                 in_specs=[pl.BlockSpec((None, 2, rows, cols), lambda k, i: (k, 0, i, 0)),
                           pl.BlockSpec((2, None, rows, cols), lambda k, i: (0, k, i, 0))],
                 out_specs=pl.BlockSpec((None, rows, cols), lambda k, i: (k, i, 0)),
                 out_shape=jax.ShapeDtypeStruct((s, rr, cols), BF16), sem=("parallel", "parallel"))(partial, from_sibling)


def _col_tile(n, k):
    if n % 896 == 0 and k <= 1024:
        return 896
    return min(n, 512)


def project(name, x, wt, comm):
    m, k = x.shape
    n = wt.shape[0]
    tn = D_MODEL
    steps = n // tn

    def body(x_ref, w_ref, o_ref):
        o_ref[...] = lax.dot_general(x_ref[...], w_ref[...], _NT, preferred_element_type=F32)

    def at():
        j = pl.program_id(0)
        return j == 0, j == (3 * steps) // 4, j == steps - 1

    (res,), moved = _call_with_comm(
        body, comm, at, (x, wt), name=name, grid=(steps,),
        in_specs=[_full((m, k)), pl.BlockSpec((tn, k), lambda j: (j, 0))],
        out_specs=[pl.BlockSpec((m, tn), lambda j: (0, j))], out_shape=[jax.ShapeDtypeStruct((m, n), F32)])
    return res, moved


def _piece_tiles(pieces):
    spans, start = [], 0
    for p in pieces:
        spans.append((start, p.shape[1] // D_MODEL))
        start += p.shape[1] // D_MODEL
    return spans, start


def _piece_spec(tm, span, rows_of, tile_of):
    first, count = span
    return pl.BlockSpec((tm, D_MODEL), lambda i, j: (rows_of(i, j), jnp.clip(tile_of(i, j) - first, 0, count - 1)))


def project_back(name, pieces, wt, comm):
    m = pieces[0].shape[0]
    k = wt.shape[1]
    tm = m // 2
    spans, steps = _piece_tiles(pieces)

    def at():
        i, j = pl.program_id(0), pl.program_id(1)
        return (i == 0) & (j == 0), (i == 0) & (j == steps - 1), (i == m // tm - 1) & (j == steps - 1)

    def body(*refs):
        w_ref, o_ref = refs[len(pieces)], refs[len(pieces) + 1]
        j = pl.program_id(1)

        @pl.when(j == 0)
        def _():
            o_ref[...] = jnp.zeros_like(o_ref)

        for dy_ref, (first, count) in zip(refs, spans):
            @pl.when((j >= first) & (j < first + count))
            def _():
                o_ref[...] += jnp.dot(dy_ref[...], w_ref[...], preferred_element_type=F32)

    (res,), moved = _call_with_comm(
        body, comm, at, (*pieces, wt), name=name, grid=(m // tm, steps),
        in_specs=[_piece_spec(tm, s, lambda i, j: i, lambda i, j: j) for s in spans]
        + [pl.BlockSpec((D_MODEL, k), lambda i, j: (j, 0))],
        out_specs=[pl.BlockSpec((tm, k), lambda i, j: (i, 0))], out_shape=[jax.ShapeDtypeStruct((m, k), F32)])
    return res, moved


def weight_grad_t(name, pieces, x):
    m = pieces[0].shape[0]
    k = x.shape[1]
    tm = m // 2
    spans, steps = _piece_tiles(pieces)

    def body(*refs):
        x_ref, o_ref, acc_ref = refs[len(pieces):]
        i, half = pl.program_id(0), pl.program_id(1)
        for dy_ref, (first, count) in zip(refs, spans):
            @pl.when((i >= first) & (i < first + count))
            def _():
                part = lax.dot_general(dy_ref[...], x_ref[...], (((0,), (0,)), ((), ())), preferred_element_type=F32)

                @pl.when(half == 0)
                def _():
                    acc_ref[...] = part

                @pl.when(half == 1)
                def _():
                    o_ref[...] = (acc_ref[...] + part).astype(BF16)

    return _call(body, name=name, grid=(steps, 2),
                 in_specs=[_piece_spec(tm, s, lambda i, j: j, lambda i, j: i) for s in spans]
                 + [pl.BlockSpec((tm, k), lambda i, j: (j, 0))],
                 out_specs=pl.BlockSpec((D_MODEL, k), lambda i, j: (i, 0)),
                 out_shape=jax.ShapeDtypeStruct((steps * D_MODEL, k), BF16),
                 scratch=[pltpu.VMEM((D_MODEL, k), F32)], sem=("parallel", "arbitrary"))(*pieces, x)


def mm_tn(name, x, dy):
    m, k = x.shape
    n = dy.shape[1]
    tm = m // 2
    tn = _col_tile(n, k)

    def body(x_ref, dy_ref, o_ref, acc_ref):
        part = lax.dot_general(x_ref[...].astype(BF16), dy_ref[...].astype(BF16), (((0,), (0,)), ((), ())),
                               preferred_element_type=F32)

        @pl.when(pl.program_id(1) == 0)
        def _():
            acc_ref[...] = part

        @pl.when(pl.program_id(1) == 1)
        def _():
            o_ref[...] = (acc_ref[...] + part).astype(BF16)

    return _call(body, name=name, grid=(n // tn, 2),
                 in_specs=[pl.BlockSpec((tm, k), lambda i, j: (j, 0)), pl.BlockSpec((tm, tn), lambda i, j: (j, i))],
                 out_specs=pl.BlockSpec((k, tn), lambda i, j: (0, i)), out_shape=jax.ShapeDtypeStruct((k, n), BF16),
                 scratch=[pltpu.VMEM((k, tn), F32)], sem=("parallel", "arbitrary"))(x, dy)


def _row_spec(width, col_block=0):
    return pl.BlockSpec((BLOCK, width), lambda i: (i, col_block))


def _x_spec():
    return pl.BlockSpec((None, BLOCK, D_MODEL), lambda i: (0, jnp.maximum(i - 1, 0), 0))


def prep(x, meta, g_pre):
    nb = x.shape[1] // BLOCK + 1

    def body(x_ref, meta_ref, g_ref, h_ref, u_ref):
        i = pl.program_id(0)

        @pl.when(i == 0)
        def _():
            h_ref[0:PAD_ROWS, :] = jnp.zeros((PAD_ROWS, D_MODEL), F32)
            h_ref[PAD_ROWS:BLOCK, :] = meta_ref[...]

        @pl.when(i > 0)
        def _():
            h_ref[...] = x_ref[...]

        u_ref[...] = _rms(h_ref[...], g_ref[...]).astype(BF16)

    return _call(body, name="prep", grid=(nb,), in_specs=[_x_spec(), _full((N_META, D_MODEL)), _full((1, D_MODEL))],
                 out_specs=[_row_spec(D_MODEL), _row_spec(D_MODEL)],
                 out_shape=[jax.ShapeDtypeStruct((nb * BLOCK, D_MODEL), F32),
                            jax.ShapeDtypeStruct((nb * BLOCK, D_MODEL), BF16)], sem=("parallel",))(x, meta, g_pre)


def prep_bwd(h, du, dres, g_pre):
    nb = h.shape[0] // BLOCK

    def body(h_ref, du_ref, dres_ref, g_ref, gx_ref, gm_ref, gg_ref):
        i = pl.program_id(0)
        _, vjp = jax.vjp(_rms, h_ref[...], g_ref[...])
        dh, dg = vjp(du_ref[...])

        @pl.when(i == 0)
        def _():
            gm_ref[...] = dh[PAD_ROWS:BLOCK, :]
            gg_ref[...] = dg

        @pl.when(i > 0)
        def _():
            gg_ref[...] += dg

        gx_ref[...] = dh + dres_ref[...]

    return _call(body, name="prep_bwd", grid=(nb,),
                 in_specs=[_row_spec(D_MODEL), _row_spec(D_MODEL), _row_spec(D_MODEL), _full((1, D_MODEL))],
                 out_specs=[_x_spec(), _full((N_META, D_MODEL)), _full((1, D_MODEL))],
                 out_shape=[jax.ShapeDtypeStruct((1, (nb - 1) * BLOCK, D_MODEL), F32),
                            jax.ShapeDtypeStruct((N_META, D_MODEL), F32), jax.ShapeDtypeStruct((1, D_MODEL), F32)],
                 sem=("arbitrary",))(h, du, dres, g_pre)


GROUP_W = SSM_INNER // SSM_GROUPS


def _gated_norm(y, z, g):
    t = y * _silu(z)
    return t * lax.rsqrt(jnp.mean(t * t, axis=-1, keepdims=True) + NORM_EPS) * g


def _gated_norm_groups(y, z, g):
    groups = [slice(k * GROUP_W, (k + 1) * GROUP_W) for k in range(SSM_GROUPS)]
    return jnp.concatenate([_gated_norm(y[:, s], z[:, s], g[:, s]) for s in groups], axis=1)


def _merge(ga, gs, ya, ys):
    return _sigmoid(ga) * ya + _sigmoid(gs) * ys


GATE_ATT_BLOCK = SEG["gate_att"][2] // D_MODEL
GATE_SSM_BLOCK = SEG["gate_ssm"][2] // D_MODEL


def _row_loss(out, g_post, x, target):
    diff = x + _rms(out, g_post) - target
    return 0.5 * jnp.sum(diff * diff) / D_MODEL


def tail(y_ssd, proj, a_att, x, target, woa, wos, wo, g_norm, g_post):
    nb = y_ssd.shape[0] // BLOCK
    rows = nb * BLOCK

    def body(y_ref, z_ref, ga_ref, gs_ref, a_ref, x_ref, t_ref, woa_ref, wos_ref, wo_ref, gn_ref, gp_ref,
             yn_ref, mg_ref, dout_ref, dya_ref, dys_ref, da_ref, dy_ref, dz_ref, dga_ref, dgs_ref, dres_ref,
             loss_ref, dgp_ref, dgn_ref):
        i = pl.program_id(0)
        yn, norm_vjp = jax.vjp(_gated_norm_groups, y_ref[...], z_ref[...], gn_ref[...])
        yn16 = yn.astype(BF16)
        y_ssm = jnp.dot(yn16, wos_ref[...], preferred_element_type=F32)
        y_att = jnp.dot(a_ref[...], woa_ref[...], preferred_element_type=F32)
        merged, merge_vjp = jax.vjp(_merge, ga_ref[...], gs_ref[...], y_att, y_ssm)
        merged16 = merged.astype(BF16)
        out = jnp.dot(merged16, wo_ref[...], preferred_element_type=F32)
        loss, loss_vjp = jax.vjp(_row_loss, out, gp_ref[...], x_ref[...], t_ref[...])
        counted = jnp.where(i > 0, 1.0, 0.0)
        dout, dgp, dres, _ = loss_vjp(counted)
        dout16 = dout.astype(BF16)
        dmerged = lax.dot_general(dout16, wo_ref[...], _NT, preferred_element_type=F32)
        dga, dgs, dya, dys = merge_vjp(dmerged)
        dya16, dys16 = dya.astype(BF16), dys.astype(BF16)
        dyn = lax.dot_general(dys16, wos_ref[...], _NT, preferred_element_type=F32)
        dy, dz, dgn = norm_vjp(dyn)

        yn_ref[...] = yn16
        mg_ref[...] = merged16
        dout_ref[...] = dout16
        dya_ref[...] = dya16
        dys_ref[...] = dys16
        da_ref[...] = lax.dot_general(dya16, woa_ref[...], _NT, preferred_element_type=F32)
        dy_ref[...] = dy
        dz_ref[...] = dz.astype(BF16)
        dga_ref[...] = dga.astype(BF16)
        dgs_ref[...] = dgs.astype(BF16)
        dres_ref[...] = dres

        @pl.when(i == 0)
        def _():
            loss_ref[...] = jnp.zeros_like(loss_ref)
            dgp_ref[...] = jnp.zeros_like(dgp_ref)
            dgn_ref[...] = jnp.zeros_like(dgn_ref)

        loss_ref[...] += loss * counted
        dgp_ref[...] += dgp
        dgn_ref[...] += dgn

    wide, narrow = _row_spec(SSM_INNER), _row_spec(D_MODEL)
    resident = pl.BlockSpec(memory_space=pltpu.VMEM)
    bf = lambda w: jax.ShapeDtypeStruct((rows, w), BF16)
    f32 = lambda w: jax.ShapeDtypeStruct((rows, w), F32)
    return _call(body, name="tail", grid=(nb,),
                 in_specs=[wide, wide, _row_spec(D_MODEL, GATE_ATT_BLOCK), _row_spec(D_MODEL, GATE_SSM_BLOCK), narrow,
                           _x_spec(), _x_spec(), resident, resident, resident, _full((1, SSM_INNER)),
                           _full((1, D_MODEL))],
                 out_specs=[wide, narrow, narrow, narrow, narrow, narrow, wide, wide, narrow, narrow, narrow,
                            _full((8, LANES)), _full((1, D_MODEL)), _full((1, SSM_INNER))],
                 out_shape=[bf(SSM_INNER), bf(D_MODEL), bf(D_MODEL), bf(D_MODEL), bf(D_MODEL), f32(D_MODEL),
                            f32(SSM_INNER), bf(SSM_INNER), bf(D_MODEL), bf(D_MODEL), f32(D_MODEL),
                            jax.ShapeDtypeStruct((8, LANES), F32), jax.ShapeDtypeStruct((1, D_MODEL), F32),
                            jax.ShapeDtypeStruct((1, SSM_INNER), F32)],
                 sem=("arbitrary",))(y_ssd, proj, proj, proj, a_att, x, target, woa, wos, wo, g_norm, g_post)


_NT = (((1,), (1,)), ((), ()))
ALIBI_SLOPES = tuple(2.0 ** (-8.0 * (h + 1) / ATT_Q_HEADS) for h in range(ATT_Q_HEADS))
KV_WIDTH = ATT_KV_HEADS * HEAD_DIM
Q_BLOCK = SEG["q"][2] // D_MODEL
Z_ATT_BLOCK = SEG["z_att"][2] // D_MODEL
K_BLOCK = SEG["k"][2] // KV_WIDTH
V_BLOCK = SEG["v"][2] // KV_WIDTH
META_ROW_BLOCK = PAD_ROWS // N_META


@jax.custom_vjp
def _swap_halves(x):
    return pltpu.roll(x, HEAD_DIM, 1)


_swap_halves.defvjp(lambda x: (pltpu.roll(x, HEAD_DIM, 1), None), lambda _, g: (pltpu.roll(g, HEAD_DIM, 1),))


def _both_halves(t, half):
    first = lax.broadcasted_iota(jnp.int32, t.shape, 1) < HEAD_DIM
    sw = _swap_halves(t)
    return jnp.where(first, t, sw) if half == 0 else jnp.where(first, sw, t)


def _attn_rows(q, z, kp, kc, vp, vc, km, vm, sinks, n):
    rows = ATT_GROUP * BLOCK
    i = lax.broadcasted_iota(jnp.int32, (rows, BLOCK), 0) & (BLOCK - 1)
    j = lax.broadcasted_iota(jnp.int32, (rows, BLOCK), 1)
    rel_c = (i - j).astype(F32)
    rel_p = rel_c + float(BLOCK)
    nv = jnp.zeros((rows, BLOCK), jnp.int32) + n
    ok_c = (i >= j) & (nv >= 1)
    ok_p = (j > i) & (nv >= 2)
    im = lax.broadcasted_iota(jnp.int32, (rows, N_META), 0) & (BLOCK - 1)
    jm = lax.broadcasted_iota(jnp.int32, (rows, N_META), 1)
    ok_m = ((jnp.zeros((rows, N_META), jnp.int32) + n) >= 1) | (im >= PAD_ROWS + jm)
    first = lax.broadcasted_iota(jnp.int32, (BLOCK, LANES), 1) < HEAD_DIM
    neg = -jnp.inf
    outs = []
    for kv in range(ATT_KV_HEADS):
        tile, half = divmod(kv, 2)
        lanes = slice(tile * LANES, (tile + 1) * LANES)
        kc2, kp2, km2 = (_both_halves(t[:, lanes], half).astype(BF16) for t in (kc, kp, km))
        vc2, vp2, vm2 = (_both_halves(t[:, lanes], half).astype(BF16) for t in (vc, vp, vm))
        qs, slope, sk = [], [], []
        for pair in range(ATT_GROUP // 2):
            c0 = (kv * ATT_GROUP + 2 * pair) * HEAD_DIM
            qp = q[:, c0:c0 + LANES] * HEAD_DIM ** -0.5
            qs += [jnp.where(first, qp, 0.0), jnp.where(first, 0.0, qp)]
        for g in range(ATT_GROUP):
            slope.append(jnp.full((BLOCK, 1), ALIBI_SLOPES[kv * ATT_GROUP + g], F32))
            sk.append(jnp.broadcast_to(sinks[kv * ATT_GROUP + g], (BLOCK, 1)))
        qs = jnp.concatenate(qs, axis=0).astype(BF16)
        slope = jnp.concatenate(slope, axis=0)
        sk = jnp.concatenate(sk, axis=0)
        sc = jnp.where(ok_c, lax.dot_general(qs, kc2, _NT, preferred_element_type=F32) - slope * rel_c, neg)
        sp = jnp.where(ok_p, lax.dot_general(qs, kp2, _NT, preferred_element_type=F32) - slope * rel_p, neg)
        sm = jnp.where(ok_m, lax.dot_general(qs, km2, _NT, preferred_element_type=F32), neg)
        mx = jnp.maximum(jnp.maximum(jnp.max(sc, axis=1, keepdims=True), jnp.max(sp, axis=1, keepdims=True)),
                         jnp.maximum(jnp.max(sm, axis=1, keepdims=True), sk))
        mx = lax.stop_gradient(mx)
        ec, ep, em, es = jnp.exp(sc - mx), jnp.exp(sp - mx), jnp.exp(sm - mx), jnp.exp(sk - mx)
        den = (es + jnp.sum(ec, axis=1, keepdims=True) + jnp.sum(ep, axis=1, keepdims=True)
               + jnp.sum(em, axis=1, keepdims=True))
        inv = 1.0 / den
        o = (jnp.dot((ec * inv).astype(BF16), vc2, preferred_element_type=F32)
             + jnp.dot((ep * inv).astype(BF16), vp2, preferred_element_type=F32)
             + jnp.dot((em * inv).astype(BF16), vm2, preferred_element_type=F32))
        for pair in range(ATT_GROUP // 2):
            r0 = 2 * pair * BLOCK
            outs.append(jnp.where(first, o[r0:r0 + BLOCK], o[r0 + BLOCK:r0 + 2 * BLOCK]))
    return jnp.concatenate(outs, axis=1) * _silu(z)


def _attn_specs(nb, steps_clamped):
    def blk(t):
        return jnp.minimum(t, nb - 1) if steps_clamped else t

    wide = lambda col: pl.BlockSpec((BLOCK, D_MODEL), lambda t: (blk(t), col))
    cur = lambda col: pl.BlockSpec((BLOCK, KV_WIDTH), lambda t: (blk(t), col))
    prev = lambda col: pl.BlockSpec((BLOCK, KV_WIDTH), lambda t: (jnp.maximum(blk(t) - 1, 0), col))
    meta = lambda col: pl.BlockSpec((N_META, KV_WIDTH), lambda t: (META_ROW_BLOCK, col))
    sinks = pl.BlockSpec((ATT_Q_HEADS, 1, 1), lambda t: (0, 0, 0))
    return [wide(Q_BLOCK), wide(Z_ATT_BLOCK), prev(K_BLOCK), cur(K_BLOCK), prev(V_BLOCK), cur(V_BLOCK),
            meta(K_BLOCK), meta(V_BLOCK), sinks]


def attn_fwd(proj, sinks):
    nb = proj.shape[0] // BLOCK

    def body(q_ref, z_ref, kp_ref, kc_ref, vp_ref, vc_ref, km_ref, vm_ref, sk_ref, o_ref):
        o_ref[...] = _attn_rows(q_ref[...], z_ref[...], kp_ref[...], kc_ref[...], vp_ref[...], vc_ref[...],
                                km_ref[...], vm_ref[...], tuple(sk_ref[h] for h in range(ATT_Q_HEADS)),
                                pl.program_id(0)).astype(BF16)

    return _call(body, name="attn_fwd", grid=(nb,), in_specs=_attn_specs(nb, False), out_specs=_row_spec(D_MODEL),
                 out_shape=jax.ShapeDtypeStruct((nb * BLOCK, D_MODEL), BF16), sem=("parallel",))(*([proj] * 8), sinks)


def attn_bwd(da, proj, sinks):
    nb = proj.shape[0] // BLOCK
    last = nb - 1
    wide = pl.BlockSpec((BLOCK, D_MODEL), lambda t: (jnp.minimum(t, last), 0))
    done = pl.BlockSpec((BLOCK, KV_WIDTH), lambda t: (jnp.maximum(t - 1, 0), 0))
    meta = _full((N_META, KV_WIDTH))
    par = _full((ATT_Q_HEADS, 1, 1))

    def body(da_ref, q_ref, z_ref, kp_ref, kc_ref, vp_ref, vc_ref, km_ref, vm_ref, sk_ref,
             dq_ref, dz_ref, dk_ref, dv_ref, dkm_ref, dvm_ref, dsk_ref, ck_ref, cv_ref):
        t = pl.program_id(0)

        @pl.when(t == 0)
        def _():
            ck_ref[...] = jnp.zeros_like(ck_ref)
            cv_ref[...] = jnp.zeros_like(cv_ref)
            dkm_ref[...] = jnp.zeros_like(dkm_ref)
            dvm_ref[...] = jnp.zeros_like(dvm_ref)
            dsk_ref[...] = jnp.zeros_like(dsk_ref)

        @pl.when(t < nb)
        def _():
            def f(q, z, kp, kc, vp, vc, km, vm, sk):
                return _attn_rows(q, z, kp, kc, vp, vc, km, vm, sk, t)

            _, vjp = jax.vjp(f, q_ref[...], z_ref[...], kp_ref[...], kc_ref[...], vp_ref[...], vc_ref[...],
                             km_ref[...], vm_ref[...], tuple(sk_ref[h] for h in range(ATT_Q_HEADS)))
            dq, dz, dkp, dkc, dvp, dvc, dkm, dvm, dsk = vjp(da_ref[...])
            dq_ref[...] = dq.astype(BF16)
            dz_ref[...] = dz.astype(BF16)
            for h in range(ATT_Q_HEADS):
                dsk_ref[h] += dsk[h]
            dk_ref[...] = ck_ref[...] + dkp
            dv_ref[...] = cv_ref[...] + dvp
            ck_ref[...] = dkc
            cv_ref[...] = dvc
            dkm_ref[...] += dkm
            dvm_ref[...] += dvm

        @pl.when(t == nb)
        def _():
            dk_ref[...] = ck_ref[...]
            dv_ref[...] = cv_ref[...]

    rows = nb * BLOCK
    return _call(body, name="attn_bwd", grid=(nb + 1,), in_specs=[wide] + _attn_specs(nb, True),
                 out_specs=[wide, wide, done, done, meta, meta, par],
                 out_shape=[jax.ShapeDtypeStruct((rows, D_MODEL), BF16), jax.ShapeDtypeStruct((rows, D_MODEL), BF16),
                            jax.ShapeDtypeStruct((rows, KV_WIDTH), F32), jax.ShapeDtypeStruct((rows, KV_WIDTH), F32),
                            jax.ShapeDtypeStruct((N_META, KV_WIDTH), F32), jax.ShapeDtypeStruct((N_META, KV_WIDTH), F32),
                            jax.ShapeDtypeStruct(sinks.shape, F32)],
                 scratch=[pltpu.VMEM((BLOCK, KV_WIDTH), F32), pltpu.VMEM((BLOCK, KV_WIDTH), F32)],
                 sem=("arbitrary",))(da, *([proj] * 8), sinks)


XBC_BLOCK0 = SEG["xbc"][2] // D_MODEL
CONV_COL_BLOCKS = CONV_DIM // D_MODEL
DT_TILE = SEG["dt"][2] // LANES


HALO = 8
HALOS_PER_BLOCK = BLOCK // HALO


def _shift_rows(cur, before, j):
    if j == 0:
        return cur
    n = cur.shape[0]
    row = lax.broadcasted_iota(jnp.int32, cur.shape, 0)
    head = pltpu.roll(before, j, 0)
    if n > HALO:
        head = jnp.concatenate([head, jnp.zeros((n - HALO, cur.shape[1]), cur.dtype)], axis=0)
    return jnp.where(row >= j, pltpu.roll(cur, j, 0), head)


def _conv_pre(cur, before, w_ref, b_ref):
    pre = b_ref[...] + w_ref[CONV_WIDTH - 1:CONV_WIDTH, :] * cur
    for k in range(CONV_WIDTH - 1):
        pre = pre + w_ref[k:k + 1, :] * _shift_rows(cur, before, CONV_WIDTH - 1 - k)
    return pre


def _conv_specs(nb, col0=0):
    first = XBC_BLOCK0 + col0
    cur = pl.BlockSpec((BLOCK, D_MODEL), lambda j, i: (i, first + j))
    before = pl.BlockSpec((HALO, D_MODEL), lambda j, i: (jnp.maximum(i * HALOS_PER_BLOCK - 1, 0), first + j))
    after = pl.BlockSpec((HALO, D_MODEL), lambda j, i: (jnp.minimum(i + 1, nb - 1) * HALOS_PER_BLOCK, first + j))
    return cur, before, after


def _valid_rows(i):
    row = lax.broadcasted_iota(jnp.int32, (BLOCK, D_MODEL), 0)
    return jnp.maximum((row >= PAD_ROWS).astype(F32), jnp.where(i > 0, 1.0, 0.0))


def conv_fwd(proj, conv_w, conv_b):
    nb = proj.shape[0] // BLOCK
    cur, before, _ = _conv_specs(nb)

    def body(c_ref, p_ref, w_ref, b_ref, o_ref):
        i = pl.program_id(1)
        pre = _conv_pre(c_ref[...], p_ref[...] * jnp.where(i > 0, 1.0, 0.0), w_ref, b_ref)
        o_ref[...] = _silu(pre) * _valid_rows(i)

    return _call(body, name="conv_fwd", grid=(CONV_COL_BLOCKS, nb),
                 in_specs=[cur, before, pl.BlockSpec((CONV_WIDTH, D_MODEL), lambda j, i: (0, j)),
                           pl.BlockSpec((1, D_MODEL), lambda j, i: (0, j))],
                 out_specs=pl.BlockSpec((BLOCK, D_MODEL), lambda j, i: (i, j)),
                 out_shape=jax.ShapeDtypeStruct((nb * BLOCK, CONV_DIM), F32),
                 sem=("parallel", "parallel"))(proj, proj, conv_w, conv_b)


def conv_bwd(name, dparts, col0, proj, conv_w, conv_b):
    nb = proj.shape[0] // BLOCK
    last = nb - 1
    ncol = sum(d.shape[1] for d in dparts) // D_MODEL
    np_ = len(dparts)
    cur, before, after = _conv_specs(nb, col0)
    dcur = [pl.BlockSpec((BLOCK, d.shape[1] // ncol), lambda j, i: (i, j)) for d in dparts]
    dafter = [pl.BlockSpec((HALO, d.shape[1] // ncol), lambda j, i: (jnp.minimum(i + 1, last) * HALOS_PER_BLOCK, j))
              for d in dparts]
    out_cur = pl.BlockSpec((BLOCK, D_MODEL), lambda j, i: (i, j))
    wspec = pl.BlockSpec((CONV_WIDTH, D_MODEL), lambda j, i: (0, col0 + j))
    bspec = pl.BlockSpec((1, D_MODEL), lambda j, i: (0, col0 + j))
    wout = pl.BlockSpec((CONV_WIDTH, D_MODEL), lambda j, i: (0, j))
    bout = pl.BlockSpec((1, D_MODEL), lambda j, i: (0, j))

    def body(*refs):
        dc_refs, da_refs = refs[:np_], refs[np_:2 * np_]
        c_ref, p_ref, a_ref, w_ref, b_ref, du_ref, dw_ref, db_ref = refs[2 * np_:]
        i = pl.program_id(1)
        row = lax.broadcasted_iota(jnp.int32, (BLOCK, D_MODEL), 0)
        curv = c_ref[...]
        beforev = p_ref[...] * jnp.where(i > 0, 1.0, 0.0)
        side_by_side = lambda rs: rs[0][...] if np_ == 1 else jnp.concatenate([r[...] for r in rs], axis=1)

        def dpre_of(pre, d):
            s = _sigmoid(pre)
            return d * (s * (1.0 + pre * (1.0 - s)))

        dp_c = dpre_of(_conv_pre(curv, beforev, w_ref, b_ref), side_by_side(dc_refs) * _valid_rows(i))
        dp_a = dpre_of(_conv_pre(a_ref[...], curv[BLOCK - HALO:], w_ref, b_ref),
                       side_by_side(da_refs) * jnp.where(i < last, 1.0, 0.0))
        du = w_ref[CONV_WIDTH - 1:CONV_WIDTH, :] * dp_c
        for j in range(1, CONV_WIDTH):
            tail = jnp.concatenate([jnp.zeros((BLOCK - HALO, D_MODEL), F32), pltpu.roll(dp_a, HALO - j, 0)], axis=0)
            up = jnp.where(row < BLOCK - j, pltpu.roll(dp_c, BLOCK - j, 0), tail)
            du = du + w_ref[CONV_WIDTH - 1 - j:CONV_WIDTH - j, :] * up
        du_ref[...] = du.astype(BF16)

        @pl.when(i == 0)
        def _():
            dw_ref[...] = jnp.zeros_like(dw_ref)
            db_ref[...] = jnp.zeros_like(db_ref)

        for k in range(CONV_WIDTH):
            dw_ref[k:k + 1, :] += jnp.sum(dp_c * _shift_rows(curv, beforev, CONV_WIDTH - 1 - k), axis=0, keepdims=True)
        db_ref[...] += jnp.sum(dp_c, axis=0, keepdims=True)

    width = ncol * D_MODEL
    return _call(body, name=name, grid=(ncol, nb),
                 in_specs=dcur + dafter + [cur, before, after, wspec, bspec], out_specs=[out_cur, wout, bout],
                 out_shape=[jax.ShapeDtypeStruct((nb * BLOCK, width), BF16),
                            jax.ShapeDtypeStruct((CONV_WIDTH, width), F32), jax.ShapeDtypeStruct((1, width), F32)],
                 sem=("parallel", "arbitrary"))(*dparts, *dparts, proj, proj, proj, conv_w, conv_b)


def _head_expand():
    e = np.zeros((LANES, SSM_INNER), np.float32)
    for h in range(SSM_HEADS):
        e[h, h * HEAD_DIM:(h + 1) * HEAD_DIM] = 1.0
    return jnp.asarray(e, dtype=BF16)


def _softplus(x):
    return jnp.maximum(x, 0.0) + jnp.log(1.0 + jnp.exp(-jnp.abs(x)))


def _bf16_parts(x):
    hi = x.astype(BF16)
    rest = x - hi.astype(F32)
    mid = rest.astype(BF16)
    return hi, mid, (rest - mid.astype(F32)).astype(BF16)


@jax.custom_vjp
def _times_01(x, m):
    return sum(jnp.dot(p, m, preferred_element_type=F32) for p in _bf16_parts(x))


def _times_01_bwd(m, g):
    return sum(lax.dot_general(p, m, _NT, preferred_element_type=F32) for p in _bf16_parts(g)), jnp.zeros_like(m)


_times_01.defvjp(lambda x, m: (_times_01(x, m), m), _times_01_bwd)


def _causal_ones():
    l = lax.broadcasted_iota(jnp.int32, (BLOCK, BLOCK), 0)
    s = lax.broadcasted_iota(jnp.int32, (BLOCK, BLOCK), 1)
    return (l >= s).astype(BF16)


@jax.custom_vjp
def _cumsum_rows(a):
    return sum(jnp.dot(_causal_ones(), p, preferred_element_type=F32) for p in _bf16_parts(a))


def _cumsum_rows_bwd(_, g):
    tn = (((0,), (0,)), ((), ()))
    return (sum(lax.dot_general(_causal_ones(), p, tn, preferred_element_type=F32) for p in _bf16_parts(g)),)


_cumsum_rows.defvjp(lambda a: (_cumsum_rows(a), None), _cumsum_rows_bwd)


def _ssd_group(xs, dt_tile, expand, bias, alog, dsk, bg, cg, state):
    l = lax.broadcasted_iota(jnp.int32, (BLOCK, BLOCK), 0)
    s = lax.broadcasted_iota(jnp.int32, (BLOCK, BLOCK), 1)
    causal = l >= s
    first_head = s < HEAD_DIM
    dt = _softplus(dt_tile + bias)
    a = dt * (-jnp.exp(alog))
    one_row = lambda v: jnp.broadcast_to(v, (HALO, LANES))
    per_lane = _times_01(jnp.concatenate([dt, _cumsum_rows(a), one_row(jnp.sum(a, axis=0, keepdims=True)),
                                          one_row(dsk)], axis=0), expand)
    dtx, cs = per_lane[0:BLOCK], per_lane[BLOCK:2 * BLOCK]
    tot, dsk = per_lane[2 * BLOCK:2 * BLOCK + 1], per_lane[2 * BLOCK + HALO:2 * BLOCK + HALO + 1]
    bb, cb16 = bg.astype(BF16), cg.astype(BF16)
    cb = lax.dot_general(cb16, bb, _NT, preferred_element_type=F32)
    xr = xs * dtx
    y_diag = []
    for p in range(GROUP_W // LANES):
        lanes = slice(p * LANES, (p + 1) * LANES)
        c_pair = cs[:, lanes]
        c_swap = _swap_halves(c_pair)
        m = []
        for c_head in (jnp.where(first_head, c_pair, c_swap), jnp.where(first_head, c_swap, c_pair)):
            m.append(cb * jnp.exp(jnp.where(causal, c_head - c_head.T, -jnp.inf)))
        x_pair = xr[:, lanes]
        x_diag = jnp.concatenate([jnp.where(first_head, x_pair, 0.0), jnp.where(first_head, 0.0, x_pair)], axis=0)
        y_diag.append(jnp.dot(jnp.concatenate(m, axis=1).astype(BF16), x_diag.astype(BF16),
                              preferred_element_type=F32))
    st = lax.dot_general(bb, (xr * jnp.exp(tot - cs)).astype(BF16), (((0,), (0,)), ((), ())),
                         preferred_element_type=F32)
    new_state = state * jnp.exp(tot) + st
    y_off = jnp.dot(cb16, state.astype(BF16), preferred_element_type=F32) * jnp.exp(cs)
    return jnp.concatenate(y_diag, axis=1) + y_off + dsk * xs, new_state


B_TILE0 = SSM_INNER // LANES
C_TILE0 = B_TILE0 + SSM_GROUPS


def _ssd_specs(chunk):
    xs = pl.BlockSpec((BLOCK, GROUP_W), lambda c, g: (chunk(c), g))
    dt = pl.BlockSpec((BLOCK, LANES), lambda c, g: (chunk(c), DT_TILE))
    expand = pl.BlockSpec((LANES, GROUP_W), lambda c, g: (0, g))
    b = pl.BlockSpec((BLOCK, SSM_STATE), lambda c, g: (chunk(c), B_TILE0 + g))
    cc = pl.BlockSpec((BLOCK, SSM_STATE), lambda c, g: (chunk(c), C_TILE0 + g))
    par = _full((1, LANES))
    state = pl.BlockSpec((None, SSM_STATE, GROUP_W), lambda c, g: (chunk(c), 0, g))
    return xs, dt, expand, b, cc, par, state


def ssd_fwd(xbc, proj, expand, bias, alog, dsk):
    nb = xbc.shape[0] // BLOCK
    xs, dt, ex, b, cc, par, state = _ssd_specs(lambda c: c)

    def body(x_ref, dt_ref, e_ref, bi_ref, al_ref, dk_ref, b_ref, c_ref, y_ref, sp_ref, st_ref):
        g = pl.program_id(1)

        @pl.when(pl.program_id(0) == 0)
        def _():
            st_ref[g] = jnp.zeros((SSM_STATE, GROUP_W), F32)

        entering = st_ref[g]
        sp_ref[...] = entering
        y_ref[...], st_ref[g] = _ssd_group(x_ref[...], dt_ref[...], e_ref[...], bi_ref[...], al_ref[...], dk_ref[...],
                                           b_ref[...], c_ref[...], entering)

    return _call(body, name="ssd_fwd", grid=(nb, SSM_GROUPS), in_specs=[xs, dt, ex, par, par, par, b, cc],
                 out_specs=[xs, state],
                 out_shape=[jax.ShapeDtypeStruct((nb * BLOCK, SSM_INNER), F32),
                            jax.ShapeDtypeStruct((nb, SSM_STATE, SSM_INNER), F32)],
                 scratch=[pltpu.VMEM((SSM_GROUPS, SSM_STATE, GROUP_W), F32)],
                 sem=("arbitrary", "arbitrary"))(xbc, proj, expand, bias, alog, dsk, xbc, xbc)


def ssd_bwd(dy, xbc, proj, expand, bias, alog, dsk, states, comm):
    nb = xbc.shape[0] // BLOCK
    last = nb - 1
    xs, dt, ex, b, cc, par, state = _ssd_specs(lambda c: last - c)
    tile = pl.BlockSpec((BLOCK, LANES), lambda c, g: (last - c, 0))
    nspec = pl.BlockSpec((BLOCK, SSM_STATE), lambda c, g: (last - c, g))

    def body(dy_ref, x_ref, dt_ref, e_ref, bi_ref, al_ref, dk_ref, b_ref, c_ref, sp_ref,
             dx_ref, ddt_ref, db_ref, dc_ref, dbi_ref, dal_ref, ddk_ref, ds_ref, ddt_acc):
        c, g = pl.program_id(0), pl.program_id(1)

        @pl.when(c == 0)
        def _():
            ds_ref[g] = jnp.zeros((SSM_STATE, GROUP_W), F32)

        @pl.when((c == 0) & (g == 0))
        def _():
            dbi_ref[...] = jnp.zeros_like(dbi_ref)
            dal_ref[...] = jnp.zeros_like(dal_ref)
            ddk_ref[...] = jnp.zeros_like(ddk_ref)

        expand_rows = e_ref[...]

        def f(xs, dt_tile, bias, alog, dsk, bg, cg, state):
            return _ssd_group(xs, dt_tile, expand_rows, bias, alog, dsk, bg, cg, state)

        _, vjp = jax.vjp(f, x_ref[...], dt_ref[...], bi_ref[...], al_ref[...], dk_ref[...], b_ref[...], c_ref[...],
                         sp_ref[...])
        dx_ref[...], ddt, dbi, dal, ddk, db_ref[...], dc_ref[...], ds_ref[g] = vjp((dy_ref[...], ds_ref[g]))
        dbi_ref[...] += dbi
        dal_ref[...] += dal
        ddk_ref[...] += ddk

        @pl.when(g == 0)
        def _():
            ddt_acc[...] = ddt

        @pl.when(g > 0)
        def _():
            ddt_acc[...] += ddt

        @pl.when(g == SSM_GROUPS - 1)
        def _():
            ddt_ref[...] = ddt_acc[...].astype(BF16)

    def at():
        c, g = pl.program_id(0), pl.program_id(1)
        return (c == 0) & (g == 0), (c == 0) & (g == 1), (c == last) & (g == SSM_GROUPS - 1)

    par_shape = jax.ShapeDtypeStruct((1, LANES), F32)
    return _call_with_comm(
        body, comm, at, (dy, xbc, proj, expand, bias, alog, dsk, xbc, xbc, states), name="ssd_bwd",
        grid=(nb, SSM_GROUPS), in_specs=[xs, xs, dt, ex, par, par, par, b, cc, state],
        out_specs=[xs, tile, nspec, nspec, par, par, par],
        out_shape=[jax.ShapeDtypeStruct((nb * BLOCK, SSM_INNER), F32), jax.ShapeDtypeStruct((nb * BLOCK, LANES), BF16),
                   jax.ShapeDtypeStruct((nb * BLOCK, SSM_GROUPS * SSM_STATE), F32),
                   jax.ShapeDtypeStruct((nb * BLOCK, SSM_GROUPS * SSM_STATE), F32), par_shape, par_shape, par_shape],
        scratch=[pltpu.VMEM((SSM_GROUPS, SSM_STATE, GROUP_W), F32), pltpu.VMEM((BLOCK, LANES), F32)])


SLAB_ROWS = 24
SLAB_META_ROW = 8


SLAB_LOSS_ROW = 7


def pack_small(dcw, dcb, dgpre, dgpost, dbias, dalog, ddsk, dsinks, dgn, dmeta, loss_tile):
    def body(cw, cb, gpre, gpost, dtb, al, dk, sk, gn, meta, loss, o_ref):
        o_ref[...] = jnp.zeros_like(o_ref)
        o_ref[SLAB_LOSS_ROW:SLAB_LOSS_ROW + 1, 0:LANES] = loss[0:1, :]
        o_ref[0:CONV_WIDTH, :] = cw[...]
        o_ref[4:5, :] = cb[...]
        o_ref[5:6, 0:1024] = gpre[...]
        o_ref[5:6, 1024:2048] = gpost[...]
        o_ref[5:6, 2048:2176] = dtb[...]
        o_ref[5:6, 2176:2304] = al[...]
        o_ref[5:6, 2304:2432] = dk[...]
        o_ref[5:6, 2432:2560] = sk[...]
        o_ref[6:7, 0:SSM_INNER] = gn[...]
        o_ref[SLAB_META_ROW:SLAB_META_ROW + N_META, 0:D_MODEL] = meta[...]

    args = (dcw, dcb, dgpre, dgpost, dbias, dalog, ddsk, dsinks, dgn, dmeta, loss_tile)
    return _call(body, name="pack_small", in_specs=[_full(a.shape) for a in args],
                 out_specs=_full((SLAB_ROWS, CONV_DIM)), out_shape=jax.ShapeDtypeStruct((SLAB_ROWS, CONV_DIM), F32))(*args)


def _lane_tile(v):
    return jnp.pad(v, ((0, 0), (0, LANES - v.shape[1])))


def kernel(x, meta_tokens, g_pre, w_in, conv_w, conv_b, dt_bias, a_log, d_skip, attn_sinks, g_ssm_norm, w_out_att, w_out_ssm, w_out, g_post, loss_target, m_meta_tokens, m_g_pre, m_w_in, m_conv_w, m_conv_b, m_dt_bias, m_a_log, m_d_skip, m_attn_sinks, m_g_ssm_norm, m_w_out_att, m_w_out_ssm, m_w_out, m_g_post, v_meta_tokens, v_g_pre, v_w_in, v_conv_w, v_conv_b, v_dt_bias, v_a_log, v_d_skip, v_attn_sinks, v_g_ssm_norm, v_w_out_att, v_w_out_ssm, v_w_out, v_g_post):
    chip = _chip_index()

    conv_w_rows = jnp.pad(conv_w[0], ((0, 2 * 8 - CONV_WIDTH), (0, 0)))
    w_in_t, m_w_in_t, v_w_in_t = w_in[0].T, m_w_in[0].T, v_w_in[0].T
    gathered_w_in, g_conv_w, g_meta = run_comm("gather_w_in",
                                               TwoLevelGather([pack_w_in(w_in_t), conv_w_rows, meta_tokens]))
    w_all_t = unpack_w_in(gathered_w_in)
    cw_full = g_conv_w[:, :CONV_WIDTH].transpose(1, 0, 2).reshape(CONV_WIDTH, CONV_DIM)
    meta_full = g_meta.transpose(1, 0, 2).reshape(N_META, D_MODEL)

    h, u = prep(x, meta_full, g_pre)
    proj, g_w_out = project("in_proj", u, w_all_t, TwoLevelGather(
        [w_out_att[0].astype(BF16), w_out_ssm[0].astype(BF16), w_out[0].astype(BF16)]))
    woa = g_w_out[0].reshape(D_MODEL, D_MODEL)
    wos = g_w_out[1].reshape(SSM_INNER, D_MODEL)
    wo = g_w_out[2].reshape(D_MODEL, D_MODEL)

    sinks3 = attn_sinks.reshape(ATT_Q_HEADS, 1, 1)
    a_att = attn_fwd(proj, sinks3)

    xbc = conv_fwd(proj, cw_full, conv_b)
    expand = _head_expand()
    head_pars = (_lane_tile(dt_bias), _lane_tile(a_log), _lane_tile(d_skip))
    y_ssd, states = ssd_fwd(xbc, proj, expand, *head_pars)

    (yn, merged, dout, dy_att, dy_ssm, da_att, dy_ssd, dz_ssm, dga, dgs, dres, loss_tile, dg_post, dgn) = tail(
        y_ssd, proj, a_att, x, loss_target, woa, wos, wo, g_ssm_norm, g_post)

    dwo = mm_tn("out_proj_dw", merged, dout)
    dwoa = mm_tn("att_out_dw", a_att, dy_att)
    dwos = mm_tn("ssm_out_dw", yn, dy_ssm)
    dq, dz_att, dk, dv, dkmeta, dvmeta, dsinks3 = attn_bwd(da_att, proj, sinks3)
    dk = dk.at[PAD_ROWS:BLOCK].add(dkmeta).astype(BF16)
    dv = dv.at[PAD_ROWS:BLOCK].add(dvmeta).astype(BF16)

    def pieces(g):
        return g.reshape(4, 2, g.shape[0] // 8, g.shape[1])

    def to_owner(g):
        return (lambda ref, dev: ref.at[_chip_of(dev), dev[2]], (g.shape[0] // 8, g.shape[1]))

    (dxs, ddt_tile, dbg, dcg, dbias, dalog, ddsk), sent_w_out = ssd_bwd(
        dy_ssd, xbc, proj, expand, *head_pars, states,
        DirectExchange([pieces(dwoa), pieces(dwos), pieces(dwo)], [to_owner(dwoa), to_owner(dwos), to_owner(dwo)],
                       ALL_MASKS, "dev", 8))
    dxs_raw, dcw_xs, dcb_xs = conv_bwd("conv_bwd_x", [dxs], 0, proj, cw_full, conv_b)
    dbc_raw, dcw_bc, dcb_bc = conv_bwd("conv_bwd_bc", [dbg, dcg], SSM_INNER // D_MODEL, proj, cw_full, conv_b)
    dcw = jnp.concatenate([dcw_xs, dcw_bc], axis=1)
    dcb = jnp.concatenate([dcb_xs, dcb_bc], axis=1)

    narrow = jnp.concatenate([dk, dv, ddt_tile, jnp.zeros((dk.shape[0], N_ACT - N_ALIGNED), BF16)], axis=1)
    dproj = [dz_ssm, dxs_raw, dbc_raw, dq, dz_att, dga, dgs, narrow]
    dw_all_t = weight_grad_t("in_proj_dw", dproj, u)

    half_rows = PACK_W // 2
    partial = pack_grad_w_in(dw_all_t).reshape(4, 2, half_rows, D_MODEL)
    from_sibling, = run_comm("pair_grads", DirectExchange(
        [partial], [(lambda ref, dev: ref.at[pl.ds(0, 4), dev[2]], (4, half_rows, D_MODEL))], SIBLING_MASK, "core", 2,
        keep_own=False))
    chip_sum = sum_pair(partial, from_sibling)
    du, (sent_w_in,) = project_back("in_proj_dx", dproj, w_all_t, DirectExchange(
        [chip_sum], [(lambda ref, dev: ref.at[_chip_of(dev)], (half_rows, D_MODEL))], CHIP_MASKS, "chip", 4))
    grad_x, dmeta, dg_pre = prep_bwd(h, du, dres, g_pre)

    slab = pack_small(dcw, dcb, dg_pre, dg_post, dbias, dalog, ddsk, _lane_tile(dsinks3.reshape(1, ATT_Q_HEADS)), dgn,
                      dmeta, loss_tile)
    halves = [sum_slots("sum_" + nm, r)
              for nm, r in zip(("w_in", "w_out_att", "w_out_ssm", "w_out"), [sent_w_in] + list(sent_w_out))]
    shared = run_comm("share_grads", Both(DirectExchange(halves, [None] * 4, SIBLING_MASK, "core", 2),
                                          DirectExchange([slab], [None], ALL_MASKS, "dev", 8)))
    g_w_in_packed, g_woa, g_wos, g_wo = [f.reshape(2 * f.shape[1], f.shape[2]) for f in shared[:4]]
    small = sum_slots("sum_small", shared[4])
    loss = small[SLAB_LOSS_ROW, 0]

    g_w_in, d_w_in, nm_w_in, nv_w_in = [a.T for a in adamw_w_in(g_w_in_packed, w_in_t, m_w_in_t, v_w_in_t)]
    d_woa, nm_woa, nv_woa = adamw_rows("adamw_w_out_att", g_woa, w_out_att[0], m_w_out_att[0], v_w_out_att[0])
    d_wos, nm_wos, nv_wos = adamw_rows("adamw_w_out_ssm", g_wos, w_out_ssm[0], m_w_out_ssm[0], v_w_out_ssm[0])
    d_wo, nm_wo, nv_wo = adamw_rows("adamw_w_out", g_wo, w_out[0], m_w_out[0], v_w_out[0])

    cw_cols = CONV_DIM // 4
    meta_cols = D_MODEL // 4
    g_small = {
        "meta_tokens": lax.dynamic_slice(small, (SLAB_META_ROW, chip * meta_cols), (N_META, meta_cols)),
        "g_pre": small[5:6, 0:1024],
        "conv_w": lax.dynamic_slice(small, (0, chip * cw_cols), (CONV_WIDTH, cw_cols)),
        "conv_b": small[4:5, :],
        "dt_bias": small[5:6, 2048:2048 + SSM_HEADS],
        "a_log": small[5:6, 2176:2176 + SSM_HEADS],
        "d_skip": small[5:6, 2304:2304 + SSM_HEADS],
        "attn_sinks": small[5:6, 2432:2432 + ATT_Q_HEADS],
        "g_ssm_norm": small[6:7, 0:SSM_INNER],
        "g_post": small[5:6, 1024:2048],
    }
    names = list(g_small)
    w_small = dict(meta_tokens=meta_tokens, g_pre=g_pre, conv_w=conv_w[0], conv_b=conv_b, dt_bias=dt_bias, a_log=a_log,
                   d_skip=d_skip, attn_sinks=attn_sinks, g_ssm_norm=g_ssm_norm, g_post=g_post)
    m_small = dict(meta_tokens=m_meta_tokens, g_pre=m_g_pre, conv_w=m_conv_w[0], conv_b=m_conv_b, dt_bias=m_dt_bias,
                   a_log=m_a_log, d_skip=m_d_skip, attn_sinks=m_attn_sinks, g_ssm_norm=m_g_ssm_norm, g_post=m_g_post)
    v_small = dict(meta_tokens=v_meta_tokens, g_pre=v_g_pre, conv_w=v_conv_w[0], conv_b=v_conv_b, dt_bias=v_dt_bias,
                   a_log=v_a_log, d_skip=v_d_skip, attn_sinks=v_attn_sinks, g_ssm_norm=v_g_ssm_norm, g_post=v_g_post)
    upd = dict(zip(names, adamw_small([g_small[k] for k in names], [w_small[k] for k in names],
                                      [m_small[k] for k in names], [v_small[k] for k in names])))

    lead = {"conv_w"}

    def shaped(name, a):
        return a[None] if name in lead else a

    grads = dict(g_small, w_in=g_w_in, w_out_att=g_woa, w_out_ssm=g_wos, w_out=g_wo)
    deltas = dict({k: upd[k][0] for k in names}, w_in=d_w_in, w_out_att=d_woa, w_out_ssm=d_wos, w_out=d_wo)
    new_m = dict({k: upd[k][1] for k in names}, w_in=nm_w_in, w_out_att=nm_woa, w_out_ssm=nm_wos, w_out=nm_wo)
    new_v = dict({k: upd[k][2] for k in names}, w_in=nv_w_in, w_out_att=nv_woa, w_out_ssm=nv_wos, w_out=nv_wo)
    lead |= {"w_in", "w_out_att", "w_out_ssm", "w_out"}
    order = ["meta_tokens", "g_pre", "w_in", "conv_w", "conv_b", "dt_bias", "a_log", "d_skip", "attn_sinks",
             "g_ssm_norm", "w_out_att", "w_out_ssm", "w_out", "g_post"]
    outs = [loss, grad_x]
    for group in (grads, deltas, new_m, new_v):
        outs += [shaped(k, group[k]) for k in order]
    return tuple(outs)
```

```python
import functools

import numpy as np
import jax
import jax.numpy as jnp
from jax import lax
from jax.experimental import pallas as pl
from jax.experimental.pallas import tpu as pltpu

F32 = jnp.float32
BF16 = jnp.bfloat16
HI = lax.Precision.HIGHEST

D_MODEL = 1024
N_META = 16
BLOCK = 128
PAD_ROWS = BLOCK - N_META
NORM_EPS = 1e-6
HEAD_DIM = 64
ATT_Q_HEADS = 16
ATT_KV_HEADS = 4
ATT_GROUP = 4
SSM_INNER = 2048
SSM_HEADS = 32
SSM_GROUPS = 4
SSM_HEADS_PER_GROUP = 8
SSM_STATE = 128
CONV_WIDTH = 4
CONV_DIM = 3072
LANES = 128

ADAM_LR = 0.001
ADAM_B1 = 0.9
ADAM_B2 = 0.999
ADAM_EPS = 1e-08
ADAM_WD = 0.01
ADAM_STEP = 10

VMEM_LIMIT = 48 * 1024 * 1024

SHARD_W = 2440
PACK_W = 2560
SHARD_STRIDE = 2432
N_ALIGNED = 9856
N_ACT = 10240
SEG = {
    "q": (0, 1024, 5120), "k": (1024, 256, 9216), "v": (1280, 256, 9472), "z_att": (1536, 1024, 6144),
    "z_ssm": (2560, 2048, 0), "xbc": (4608, 3072, 2048), "dt": (7680, 128, 9728),
    "gate_att": (7808, 1024, 7168), "gate_ssm": (8832, 1024, 8192),
}
DT_STORED_START = 7680
DT_PAD = LANES - SSM_HEADS


def _act_col(aligned_col):
    for a0, w, p0 in SEG.values():
        if a0 <= aligned_col < a0 + w:
            return p0 + aligned_col - a0
    raise ValueError(aligned_col)


def _call(body, *, name, out_shape, in_specs, out_specs, grid=(), scratch=(), sem=None, aliases=None):
    return pl.pallas_call(
        body, out_shape=out_shape, grid=grid, in_specs=in_specs, out_specs=out_specs, scratch_shapes=list(scratch),
        name=name, input_output_aliases=aliases or {},
        compiler_params=pltpu.CompilerParams(dimension_semantics=sem, vmem_limit_bytes=VMEM_LIMIT))


def _full(shape):
    n = len(shape)
    return pl.BlockSpec(shape, lambda *_: (0,) * n)


def _chip_index():
    return lax.axis_index("x") * 2 + lax.axis_index("y")


_sigmoid = jax.nn.sigmoid


def _silu(z):
    return z * _sigmoid(z)


def _rms(x, g):
    return x * lax.rsqrt(jnp.mean(x * x, axis=-1, keepdims=True) + NORM_EPS) * g


def _peer(mask):
    x, y, c = lax.axis_index("x"), lax.axis_index("y"), lax.axis_index("c")
    return ((1 - x) if mask & 4 else x, (1 - y) if mask & 2 else y, (1 - c) if mask & 1 else c)


def _me():
    return lax.axis_index("x"), lax.axis_index("y"), lax.axis_index("c")


def _chip_of(dev):
    return 2 * dev[0] + dev[1]


CHIP_MASKS = (4, 2, 6)
ALL_MASKS = (1, 2, 3, 4, 5, 6, 7)
SIBLING_MASK = (1,)


def _remote(src, dst, send_sem, recv_sem, dev):
    return pltpu.make_async_remote_copy(src_ref=src, dst_ref=dst, send_sem=send_sem, recv_sem=recv_sem,
                                        device_id=dev, device_id_type=pl.DeviceIdType.MESH)


class _StagedCopy:
    def __init__(self, src, stage, dst, load_sem, store_sem):
        self.load = pltpu.make_async_copy(src, stage, load_sem)
        self.store = pltpu.make_async_copy(stage, dst, store_sem)

    def start(self):
        self.load.start()
        self.load.wait()
        self.store.start()

    def wait(self):
        self.store.wait()


class DirectExchange:
    def __init__(self, arrays, pieces, masks, slot_kind, nslots, keep_own=True):
        self.arrays, self.pieces, self.masks, self.slot_kind = list(arrays), list(pieces), masks, slot_kind
        self.keep_own = keep_own
        n, nk = len(arrays), len(masks)
        shapes = [a.shape if p is None else p[1] for a, p in zip(arrays, pieces)]
        self.out_shape = [jax.ShapeDtypeStruct((nslots,) + tuple(s), a.dtype) for s, a in zip(shapes, arrays)]
        self.scratch = [pltpu.SemaphoreType.DMA((n * nk,)), pltpu.SemaphoreType.DMA((n * nk,))]
        if keep_own:
            self.scratch += [pltpu.SemaphoreType.DMA((2 * n,))] + [pltpu.VMEM(s, a.dtype) for s, a in zip(shapes, arrays)]
        self.has_mid = False

    def _copies(self, ins, outs, scratch):
        send_sems, recv_sems = scratch[:2]
        me = _me()
        slot = {"chip": _chip_of(me), "dev": 4 * me[0] + 2 * me[1] + me[2], "core": me[2]}[self.slot_kind]
        nk = len(self.masks)

        def piece(a, dev):
            return ins[a] if self.pieces[a] is None else self.pieces[a][0](ins[a], dev)

        local = []
        if self.keep_own:
            local_sems, stages = scratch[2], scratch[3:]
            local = [_StagedCopy(piece(a, me), stages[a], outs[a].at[slot], local_sems.at[2 * a], local_sems.at[2 * a + 1])
                     for a in range(len(ins))]
        remote = []
        for a in range(len(ins)):
            for ki, mask in enumerate(self.masks):
                dev = _peer(mask)
                remote.append(_remote(piece(a, dev), outs[a].at[slot], send_sems.at[a * nk + ki],
                                      recv_sems.at[a * nk + ki], dev))
        return local, remote

    def start(self, ins, outs, scratch):
        local, remote = self._copies(ins, outs, scratch)
        for cp in remote + local:
            cp.start()

    def finish(self, ins, outs, scratch):
        local, remote = self._copies(ins, outs, scratch)
        for cp in remote + local:
            cp.wait()


class TwoLevelGather:
    def __init__(self, arrays):
        self.arrays = list(arrays)
        n, nk = len(arrays), len(CHIP_MASKS)
        self.out_shape = [jax.ShapeDtypeStruct((4,) + a.shape, a.dtype) for a in arrays]
        self.scratch = ([pltpu.SemaphoreType.DMA((n * nk,)) for _ in range(4)] + [pltpu.SemaphoreType.DMA((2 * n,))]
                        + [pltpu.VMEM(a.shape, a.dtype) for a in arrays])
        self.has_mid = True

    def _copies(self, ins, outs, scratch):
        ici_send, ici_recv, fwd_send, fwd_recv, local_sems = scratch[:5]
        stages = scratch[5:]
        me = _me()
        sibling = _peer(1)
        nk = len(CHIP_MASKS)
        local, ici, fwd = [], [], []
        for a in range(len(ins)):
            half = ins[a].shape[0] // 2
            mine = pl.ds(me[2] * half, half)
            local.append(_StagedCopy(ins[a], stages[a], outs[a].at[_chip_of(me)], local_sems.at[2 * a],
                                     local_sems.at[2 * a + 1]))
            for ki, mask in enumerate(CHIP_MASKS):
                dev = _peer(mask)
                k = a * nk + ki
                ici.append(_remote(ins[a].at[mine], outs[a].at[_chip_of(me), mine], ici_send.at[k], ici_recv.at[k], dev))
                arrived = outs[a].at[_chip_of(dev), mine]
                fwd.append(_remote(arrived, arrived, fwd_send.at[k], fwd_recv.at[k], sibling))
        return local, ici, fwd

    def start(self, ins, outs, scratch):
        local, ici, _ = self._copies(ins, outs, scratch)
        for cp in ici + local:
            cp.start()

    def mid(self, ins, outs, scratch):
        _, ici, fwd = self._copies(ins, outs, scratch)
        for arrival, onward in zip(ici, fwd):
            arrival.wait_recv()
            onward.start()

    def finish(self, ins, outs, scratch):
        local, ici, fwd = self._copies(ins, outs, scratch)
        for cp in fwd:
            cp.wait_recv()
        for cp in ici + fwd:
            cp.wait_send()
        for cp in local:
            cp.wait()


class Both:
    def __init__(self, a, b):
        self.a, self.b = a, b
        self.arrays, self.out_shape = a.arrays + b.arrays, a.out_shape + b.out_shape
        self.scratch = a.scratch + b.scratch
        self.has_mid = False
        assert not (a.has_mid or b.has_mid)

    def _parts(self, ins, outs, sems):
        na, sa = len(self.a.arrays), len(self.a.scratch)
        return (ins[:na], outs[:na], sems[:sa]), (ins[na:], outs[na:], sems[sa:])

    def start(self, ins, outs, sems):
        pa, pb = self._parts(ins, outs, sems)
        self.a.start(*pa)
        self.b.start(*pb)

    def finish(self, ins, outs, sems):
        pa, pb = self._parts(ins, outs, sems)
        self.a.finish(*pa)
        self.b.finish(*pb)


_ANY = pl.BlockSpec(memory_space=pl.ANY)


def run_comm(name, comm):
    n = len(comm.arrays)

    def body(*refs):
        ins, outs, sems = refs[:n], refs[n:2 * n], refs[2 * n:]
        comm.start(ins, outs, sems)
        if comm.has_mid:
            comm.mid(ins, outs, sems)
        comm.finish(ins, outs, sems)

    return pl.pallas_call(body, name=name, out_shape=comm.out_shape, in_specs=[_ANY] * n, out_specs=[_ANY] * n,
                          scratch_shapes=comm.scratch,
                          compiler_params=pltpu.CompilerParams(vmem_limit_bytes=VMEM_LIMIT))(*comm.arrays)


def _call_with_comm(body, comm, steps, args, *, name, out_shape, in_specs, out_specs, grid, scratch=()):
    ni, no, ns, nc = len(in_specs), len(out_specs), len(scratch), len(comm.arrays)

    def full_body(*refs):
        ins, cins = refs[:ni], refs[ni:ni + nc]
        outs, couts = refs[ni + nc:ni + nc + no], refs[ni + nc + no:ni + 2 * nc + no]
        scr, csems = refs[ni + 2 * nc + no:ni + 2 * nc + no + ns], refs[ni + 2 * nc + no + ns:]
        first, middle, last = steps()
        pl.when(first)(lambda: comm.start(cins, couts, csems))
        if comm.has_mid:
            pl.when(middle)(lambda: comm.mid(cins, couts, csems))
        body(*ins, *outs, *scr)
        pl.when(last)(lambda: comm.finish(cins, couts, csems))

    res = pl.pallas_call(
        full_body, name=name, out_shape=list(out_shape) + comm.out_shape, grid=grid,
        in_specs=list(in_specs) + [_ANY] * nc, out_specs=list(out_specs) + [_ANY] * nc,
        scratch_shapes=list(scratch) + comm.scratch,
        compiler_params=pltpu.CompilerParams(dimension_semantics=("arbitrary",) * len(grid),
                                             vmem_limit_bytes=VMEM_LIMIT))(*args, *comm.arrays)
    return res[:no], res[no:]


def _shard_pieces(chip):
    if chip < 3:
        return [(0, SHARD_W, 8 * chip)]
    behind_dt = DT_STORED_START + SSM_HEADS - 3 * SHARD_W
    return [(0, behind_dt, 24), (behind_dt, SHARD_W - behind_dt, behind_dt + 24 + DT_PAD)]


W_IN_COLS = 256


def pack_w_in(wt):
    def body(w_ref, o_ref, pad_ref):
        chip = _chip_index()
        pad_ref[...] = jnp.zeros_like(pad_ref)
        for cv in range(4):
            @pl.when(chip == cv)
            def _():
                for src, n, dst in _shard_pieces(cv):
                    pad_ref[dst:dst + n, :] = w_ref[src:src + n, :]
        o_ref[...] = pad_ref[...].astype(BF16)

    return _call(body, name="pack_w_in", grid=(D_MODEL // W_IN_COLS,),
                 in_specs=[pl.BlockSpec((SHARD_W, W_IN_COLS), lambda i: (0, i))],
                 out_specs=pl.BlockSpec((PACK_W, W_IN_COLS), lambda i: (0, i)),
                 out_shape=jax.ShapeDtypeStruct((PACK_W, D_MODEL), BF16),
                 scratch=[pltpu.VMEM((PACK_W, W_IN_COLS), F32)], sem=("parallel",))(wt)


def _tile_runs():
    runs, fix = [], []
    for t in range(N_ALIGNED // LANES):
        s = min(t // 19, 3)
        j = t - 19 * s
        p = _act_col(t * LANES)
        if runs and runs[-1][1] == s and runs[-1][0] + runs[-1][3] == p and runs[-1][2] + runs[-1][3] == j * LANES:
            runs[-1][3] += LANES
        else:
            runs.append([p, s, j * LANES, LANES])
        if j == 0 and s > 0:
            fix.append((p, s - 1))
    return runs, fix


def unpack_w_in(bg):
    runs, fix = _tile_runs()

    def body(b_ref, o_ref):
        for p, s, j, w in runs:
            o_ref[p:p + w, :] = b_ref[s, j:j + w, :]
        for p, s in fix:
            o_ref[p:p + LANES, :] = o_ref[p:p + LANES, :] + b_ref[s, SHARD_STRIDE:PACK_W, :]
        o_ref[N_ALIGNED:N_ACT, :] = jnp.zeros((N_ACT - N_ALIGNED, W_IN_COLS), BF16)

    return _call(body, name="unpack_w_in", grid=(D_MODEL // W_IN_COLS,),
                 in_specs=[pl.BlockSpec((4, PACK_W, W_IN_COLS), lambda i: (0, 0, i))],
                 out_specs=pl.BlockSpec((N_ACT, W_IN_COLS), lambda i: (0, i)),
                 out_shape=jax.ShapeDtypeStruct((N_ACT, D_MODEL), BF16), sem=("parallel",))(bg)


def pack_grad_w_in(dwt):
    def body(g_ref, o_ref):
        for s in range(4):
            for j in range(PACK_W // LANES):
                p = _act_col((19 * s + j) * LANES)
                o_ref[s, j * LANES:(j + 1) * LANES, :] = g_ref[p:p + LANES, :]

    return _call(body, name="pack_grad_w_in", grid=(D_MODEL // W_IN_COLS,),
                 in_specs=[pl.BlockSpec((N_ACT, W_IN_COLS), lambda i: (0, i))],
                 out_specs=pl.BlockSpec((4, PACK_W, W_IN_COLS), lambda i: (0, 0, i)),
                 out_shape=jax.ShapeDtypeStruct((4, PACK_W, D_MODEL), BF16), sem=("parallel",))(dwt)


def _adamw(w, g, m, v):
    m = ADAM_B1 * m + (1.0 - ADAM_B1) * g
    v = ADAM_B2 * v + (1.0 - ADAM_B2) * jnp.square(g)
    m_hat = m / (1.0 - ADAM_B1 ** ADAM_STEP)
    v_hat = v / (1.0 - ADAM_B2 ** ADAM_STEP)
    delta = -ADAM_LR * (m_hat / (jnp.sqrt(v_hat) + ADAM_EPS) + ADAM_WD * w)
    return delta, m, v


def adamw_w_in(g_packed, wt, mt, vt):
    cols = LANES

    def body(g_ref, w_ref, m_ref, v_ref, go_ref, d_ref, mo_ref, vo_ref):
        chip = _chip_index()
        for cv in range(4):
            @pl.when(chip == cv)
            def _():
                for dst, n, src in _shard_pieces(cv):
                    go_ref[dst:dst + n, :] = g_ref[src:src + n, :]
        d_ref[...], mo_ref[...], vo_ref[...] = _adamw(w_ref[...], go_ref[...], m_ref[...], v_ref[...])

    spec = pl.BlockSpec((SHARD_W, cols), lambda i: (0, i))
    shp = jax.ShapeDtypeStruct((SHARD_W, D_MODEL), F32)
    return _call(body, name="adamw_w_in", grid=(D_MODEL // cols,),
                 in_specs=[pl.BlockSpec((PACK_W, cols), lambda i: (0, i)), spec, spec, spec],
                 out_specs=[spec] * 4, out_shape=[shp] * 4, sem=("parallel",))(g_packed, wt, mt, vt)


def adamw_rows(name, g, w, m, v):
    r, c = g.shape
    rows = min(r, BLOCK)

    def body(g_ref, w_ref, m_ref, v_ref, d_ref, mo_ref, vo_ref):
        d_ref[...], mo_ref[...], vo_ref[...] = _adamw(w_ref[...], g_ref[...], m_ref[...], v_ref[...])

    spec = pl.BlockSpec((rows, c), lambda i: (i, 0))
    shp = jax.ShapeDtypeStruct((r, c), F32)
    return _call(body, name=name, grid=(r // rows,), in_specs=[spec] * 4, out_specs=[spec] * 3, out_shape=[shp] * 3,
                 sem=("parallel",))(g, w, m, v)


def adamw_small(gs, ws, ms, vs):
    n = len(gs)

    def body(*refs):
        g, w, m, v = refs[:n], refs[n:2 * n], refs[2 * n:3 * n], refs[3 * n:4 * n]
        outs = refs[4 * n:]
        for i in range(n):
            d, mn, vn = _adamw(w[i][...], g[i][...], m[i][...], v[i][...])
            outs[3 * i][...] = d
            outs[3 * i + 1][...] = mn
            outs[3 * i + 2][...] = vn

    specs = [_full(a.shape) for a in gs]
    res = _call(body, name="adamw_small", in_specs=specs * 4,
                out_specs=[s for s in specs for _ in range(3)],
                out_shape=[jax.ShapeDtypeStruct(a.shape, F32) for a in gs for _ in range(3)])(*gs, *ws, *ms, *vs)
    return [tuple(res[3 * i:3 * i + 3]) for i in range(n)]


def sum_slots(name, r):
    s, rr, c = r.shape
    rows = min(rr, BLOCK)

    def body(r_ref, o_ref):
        acc = r_ref[0].astype(F32)
        for k in range(1, s):
            acc = acc + r_ref[k].astype(F32)
        o_ref[...] = acc

    return _call(body, name=name, grid=(rr // rows,), in_specs=[pl.BlockSpec((s, rows, c), lambda i: (0, i, 0))],
                 out_specs=pl.BlockSpec((rows, c), lambda i: (i, 0)), out_shape=jax.ShapeDtypeStruct((rr, c), F32),
                 sem=("parallel",))(r)


def sum_pair(partial, from_sibling):
    s, _, rr, cols = partial.shape
    rows = rr // 2

    def body(p_ref, r_ref, o_ref):
        c = lax.axis_index("c")
        o_ref[...] = (p_ref[c].astype(F32) + r_ref[1 - c].astype(F32)).astype(BF16)

    return _call(body, name="sum_pair", grid=(s, rr // rows),
                 in_specs=[pl.BlockSpec((None, 2, rows, cols), lambda k, i: (k, 0, i, 0)),
                           pl.BlockSpec((2, None, rows, cols), lambda k, i: (0, k, i, 0))],
                 out_specs=pl.BlockSpec((None, rows, cols), lambda k, i: (k, i, 0)),
                 out_shape=jax.ShapeDtypeStruct((s, rr, cols), BF16), sem=("parallel", "parallel"))(partial, from_sibling)


def _col_tile(n, k):
    if n % 896 == 0 and k <= 1024:
        return 896
    return min(n, 512)


def project(name, x, wt, comm):
    m, k = x.shape
    n = wt.shape[0]
    tn = D_MODEL
    steps = n // tn

    def body(x_ref, w_ref, o_ref):
        o_ref[...] = lax.dot_general(x_ref[...], w_ref[...], _NT, preferred_element_type=F32)

    def at():
        j = pl.program_id(0)
        return j == 0, j == (3 * steps) // 4, j == steps - 1

    (res,), moved = _call_with_comm(
        body, comm, at, (x, wt), name=name, grid=(steps,),
        in_specs=[_full((m, k)), pl.BlockSpec((tn, k), lambda j: (j, 0))],
        out_specs=[pl.BlockSpec((m, tn), lambda j: (0, j))], out_shape=[jax.ShapeDtypeStruct((m, n), F32)])
    return res, moved


def _piece_tiles(pieces):
    spans, start = [], 0
    for p in pieces:
        spans.append((start, p.shape[1] // D_MODEL))
        start += p.shape[1] // D_MODEL
    return spans, start


def _piece_spec(tm, span, rows_of, tile_of):
    first, count = span

    def index(i, j):
        t = tile_of(i, j) - first
        mine = (t >= 0) & (t < count)
        return jnp.where(mine, rows_of(i, j), 0), jnp.clip(t, 0, count - 1)

    return pl.BlockSpec((tm, D_MODEL), index)


def project_back(name, pieces, wt, comm):
    m = pieces[0].shape[0]
    k = wt.shape[1]
    tm = m // 2
    spans, steps = _piece_tiles(pieces)

    def at():
        i, j = pl.program_id(0), pl.program_id(1)
        return (i == 0) & (j == 0), (i == 0) & (j == steps - 1), (i == m // tm - 1) & (j == steps - 1)

    def body(*refs):
        w_ref, o_ref = refs[len(pieces)], refs[len(pieces) + 1]
        j = pl.program_id(1)

        @pl.when(j == 0)
        def _():
            o_ref[...] = jnp.zeros_like(o_ref)

        for dy_ref, (first, count) in zip(refs, spans):
            @pl.when((j >= first) & (j < first + count))
            def _():
                o_ref[...] += jnp.dot(dy_ref[...], w_ref[...], preferred_element_type=F32)

    (res,), moved = _call_with_comm(
        body, comm, at, (*pieces, wt), name=name, grid=(m // tm, steps),
        in_specs=[_piece_spec(tm, s, lambda i, j: i, lambda i, j: j) for s in spans]
        + [pl.BlockSpec((D_MODEL, k), lambda i, j: (j, 0))],
        out_specs=[pl.BlockSpec((tm, k), lambda i, j: (i, 0))], out_shape=[jax.ShapeDtypeStruct((m, k), F32)])
    return res, moved


def weight_grad_t(name, pieces, x):
    m = pieces[0].shape[0]
    k = x.shape[1]
    tm = m // 2
    spans, steps = _piece_tiles(pieces)

    def body(*refs):
        x_ref, o_ref, acc_ref = refs[len(pieces):]
        i, half = pl.program_id(0), pl.program_id(1)
        for dy_ref, (first, count) in zip(refs, spans):
            @pl.when((i >= first) & (i < first + count))
            def _():
                part = lax.dot_general(dy_ref[...], x_ref[...], (((0,), (0,)), ((), ())), preferred_element_type=F32)

                @pl.when(half == 0)
                def _():
                    acc_ref[...] = part

                @pl.when(half == 1)
                def _():
                    o_ref[...] = (acc_ref[...] + part).astype(BF16)

    return _call(body, name=name, grid=(steps, 2),
                 in_specs=[_piece_spec(tm, s, lambda i, j: j, lambda i, j: i) for s in spans]
                 + [pl.BlockSpec((tm, k), lambda i, j: (j, 0))],
                 out_specs=pl.BlockSpec((D_MODEL, k), lambda i, j: (i, 0)),
                 out_shape=jax.ShapeDtypeStruct((steps * D_MODEL, k), BF16),
                 scratch=[pltpu.VMEM((D_MODEL, k), F32)], sem=("parallel", "arbitrary"))(*pieces, x)


def mm_tn(name, x, dy):
    m, k = x.shape
    n = dy.shape[1]
    tm = m // 2
    tn = _col_tile(n, k)

    def body(x_ref, dy_ref, o_ref, acc_ref):
        part = lax.dot_general(x_ref[...].astype(BF16), dy_ref[...].astype(BF16), (((0,), (0,)), ((), ())),
                               preferred_element_type=F32)

        @pl.when(pl.program_id(1) == 0)
        def _():
            acc_ref[...] = part

        @pl.when(pl.program_id(1) == 1)
        def _():
            o_ref[...] = (acc_ref[...] + part).astype(BF16)

    return _call(body, name=name, grid=(n // tn, 2),
                 in_specs=[pl.BlockSpec((tm, k), lambda i, j: (j, 0)), pl.BlockSpec((tm, tn), lambda i, j: (j, i))],
                 out_specs=pl.BlockSpec((k, tn), lambda i, j: (0, i)), out_shape=jax.ShapeDtypeStruct((k, n), BF16),
                 scratch=[pltpu.VMEM((k, tn), F32)], sem=("parallel", "arbitrary"))(x, dy)


def _row_spec(width, col_block=0):
    return pl.BlockSpec((BLOCK, width), lambda i: (i, col_block))


def _x_spec():
    return pl.BlockSpec((None, BLOCK, D_MODEL), lambda i: (0, jnp.maximum(i - 1, 0), 0))


def prep(x, meta, g_pre):
    nb = x.shape[1] // BLOCK + 1

    def body(x_ref, meta_ref, g_ref, h_ref, u_ref):
        i = pl.program_id(0)

        @pl.when(i == 0)
        def _():
            h_ref[0:PAD_ROWS, :] = jnp.zeros((PAD_ROWS, D_MODEL), F32)
            h_ref[PAD_ROWS:BLOCK, :] = meta_ref[...]

        @pl.when(i > 0)
        def _():
            h_ref[...] = x_ref[...]

        u_ref[...] = _rms(h_ref[...], g_ref[...]).astype(BF16)

    return _call(body, name="prep", grid=(nb,), in_specs=[_x_spec(), _full((N_META, D_MODEL)), _full((1, D_MODEL))],
                 out_specs=[_row_spec(D_MODEL), _row_spec(D_MODEL)],
                 out_shape=[jax.ShapeDtypeStruct((nb * BLOCK, D_MODEL), F32),
                            jax.ShapeDtypeStruct((nb * BLOCK, D_MODEL), BF16)], sem=("parallel",))(x, meta, g_pre)


def prep_bwd(h, du, dres, g_pre):
    nb = h.shape[0] // BLOCK

    def body(h_ref, du_ref, dres_ref, g_ref, gx_ref, gm_ref, gg_ref):
        i = pl.program_id(0)
        _, vjp = jax.vjp(_rms, h_ref[...], g_ref[...])
        dh, dg = vjp(du_ref[...])

        @pl.when(i == 0)
        def _():
            gm_ref[...] = dh[PAD_ROWS:BLOCK, :]
            gg_ref[...] = dg

        @pl.when(i > 0)
        def _():
            gg_ref[...] += dg

        gx_ref[...] = dh + dres_ref[...]

    return _call(body, name="prep_bwd", grid=(nb,),
                 in_specs=[_row_spec(D_MODEL), _row_spec(D_MODEL), _row_spec(D_MODEL), _full((1, D_MODEL))],
                 out_specs=[_x_spec(), _full((N_META, D_MODEL)), _full((1, D_MODEL))],
                 out_shape=[jax.ShapeDtypeStruct((1, (nb - 1) * BLOCK, D_MODEL), F32),
                            jax.ShapeDtypeStruct((N_META, D_MODEL), F32), jax.ShapeDtypeStruct((1, D_MODEL), F32)],
                 sem=("arbitrary",))(h, du, dres, g_pre)


GROUP_W = SSM_INNER // SSM_GROUPS


def _gated_norm(y, z, g):
    t = y * _silu(z)
    return t * lax.rsqrt(jnp.mean(t * t, axis=-1, keepdims=True) + NORM_EPS) * g


def _gated_norm_groups(y, z, g):
    groups = [slice(k * GROUP_W, (k + 1) * GROUP_W) for k in range(SSM_GROUPS)]
    return jnp.concatenate([_gated_norm(y[:, s], z[:, s], g[:, s]) for s in groups], axis=1)


def _merge(ga, gs, ya, ys):
    return _sigmoid(ga) * ya + _sigmoid(gs) * ys


GATE_ATT_BLOCK = SEG["gate_att"][2] // D_MODEL
GATE_SSM_BLOCK = SEG["gate_ssm"][2] // D_MODEL


def _row_loss(out, g_post, x, target):
    diff = x + _rms(out, g_post) - target
    return 0.5 * jnp.sum(diff * diff) / D_MODEL


def tail(y_ssd, proj, a_att, x, target, woa, wos, wo, g_norm, g_post):
    nb = y_ssd.shape[0] // BLOCK
    rows = nb * BLOCK

    def body(y_ref, z_ref, ga_ref, gs_ref, a_ref, x_ref, t_ref, woa_ref, wos_ref, wo_ref, gn_ref, gp_ref,
             yn_ref, mg_ref, dout_ref, dya_ref, dys_ref, da_ref, dy_ref, dz_ref, dga_ref, dgs_ref, dres_ref,
             loss_ref, dgp_ref, dgn_ref):
        i = pl.program_id(0)
        yn, norm_vjp = jax.vjp(_gated_norm_groups, y_ref[...], z_ref[...], gn_ref[...])
        yn16 = yn.astype(BF16)
        y_ssm = jnp.dot(yn16, wos_ref[...], preferred_element_type=F32)
        y_att = jnp.dot(a_ref[...], woa_ref[...], preferred_element_type=F32)
        merged, merge_vjp = jax.vjp(_merge, ga_ref[...], gs_ref[...], y_att, y_ssm)
        merged16 = merged.astype(BF16)
        out = jnp.dot(merged16, wo_ref[...], preferred_element_type=F32)
        loss, loss_vjp = jax.vjp(_row_loss, out, gp_ref[...], x_ref[...], t_ref[...])
        counted = jnp.where(i > 0, 1.0, 0.0)
        dout, dgp, dres, _ = loss_vjp(counted)
        dout16 = dout.astype(BF16)
        dmerged = lax.dot_general(dout16, wo_ref[...], _NT, preferred_element_type=F32)
        dga, dgs, dya, dys = merge_vjp(dmerged)
        dya16, dys16 = dya.astype(BF16), dys.astype(BF16)
        dyn = lax.dot_general(dys16, wos_ref[...], _NT, preferred_element_type=F32)
        dy, dz, dgn = norm_vjp(dyn)

        yn_ref[...] = yn16
        mg_ref[...] = merged16
        dout_ref[...] = dout16
        dya_ref[...] = dya16
        dys_ref[...] = dys16
        da_ref[...] = lax.dot_general(dya16, woa_ref[...], _NT, preferred_element_type=F32)
        dy_ref[...] = dy
        dz_ref[...] = dz.astype(BF16)
        dga_ref[...] = dga.astype(BF16)
        dgs_ref[...] = dgs.astype(BF16)
        dres_ref[...] = dres

        @pl.when(i == 0)
        def _():
            loss_ref[...] = jnp.zeros_like(loss_ref)
            dgp_ref[...] = jnp.zeros_like(dgp_ref)
            dgn_ref[...] = jnp.zeros_like(dgn_ref)

        loss_ref[...] += loss * counted
        dgp_ref[...] += dgp
        dgn_ref[...] += dgn

    wide, narrow = _row_spec(SSM_INNER), _row_spec(D_MODEL)
    resident = pl.BlockSpec(memory_space=pltpu.VMEM)
    bf = lambda w: jax.ShapeDtypeStruct((rows, w), BF16)
    f32 = lambda w: jax.ShapeDtypeStruct((rows, w), F32)
    return _call(body, name="tail", grid=(nb,),
                 in_specs=[wide, wide, _row_spec(D_MODEL, GATE_ATT_BLOCK), _row_spec(D_MODEL, GATE_SSM_BLOCK), narrow,
                           _x_spec(), _x_spec(), resident, resident, resident, _full((1, SSM_INNER)),
                           _full((1, D_MODEL))],
                 out_specs=[wide, narrow, narrow, narrow, narrow, narrow, wide, wide, narrow, narrow, narrow,
                            _full((8, LANES)), _full((1, D_MODEL)), _full((1, SSM_INNER))],
                 out_shape=[bf(SSM_INNER), bf(D_MODEL), bf(D_MODEL), bf(D_MODEL), bf(D_MODEL), f32(D_MODEL),
                            f32(SSM_INNER), bf(SSM_INNER), bf(D_MODEL), bf(D_MODEL), f32(D_MODEL),
                            jax.ShapeDtypeStruct((8, LANES), F32), jax.ShapeDtypeStruct((1, D_MODEL), F32),
                            jax.ShapeDtypeStruct((1, SSM_INNER), F32)],
                 sem=("arbitrary",))(y_ssd, proj, proj, proj, a_att, x, target, woa, wos, wo, g_norm, g_post)


_NT = (((1,), (1,)), ((), ()))
ALIBI_SLOPES = tuple(2.0 ** (-8.0 * (h + 1) / ATT_Q_HEADS) for h in range(ATT_Q_HEADS))
KV_WIDTH = ATT_KV_HEADS * HEAD_DIM
Q_BLOCK = SEG["q"][2] // D_MODEL
Z_ATT_BLOCK = SEG["z_att"][2] // D_MODEL
K_BLOCK = SEG["k"][2] // KV_WIDTH
V_BLOCK = SEG["v"][2] // KV_WIDTH
META_ROW_BLOCK = PAD_ROWS // N_META


@jax.custom_vjp
def _swap_halves(x):
    return pltpu.roll(x, HEAD_DIM, 1)


_swap_halves.defvjp(lambda x: (pltpu.roll(x, HEAD_DIM, 1), None), lambda _, g: (pltpu.roll(g, HEAD_DIM, 1),))


def _both_halves(t, half):
    first = lax.broadcasted_iota(jnp.int32, t.shape, 1) < HEAD_DIM
    sw = _swap_halves(t)
    return jnp.where(first, t, sw) if half == 0 else jnp.where(first, sw, t)


def _attn_rows(q, z, kp, kc, vp, vc, km, vm, sinks, n):
    rows = ATT_GROUP * BLOCK
    i = lax.broadcasted_iota(jnp.int32, (rows, BLOCK), 0) & (BLOCK - 1)
    j = lax.broadcasted_iota(jnp.int32, (rows, BLOCK), 1)
    rel_c = (i - j).astype(F32)
    rel_p = rel_c + float(BLOCK)
    nv = jnp.zeros((rows, BLOCK), jnp.int32) + n
    ok_c = (i >= j) & (nv >= 1)
    ok_p = (j > i) & (nv >= 2)
    im = lax.broadcasted_iota(jnp.int32, (rows, N_META), 0) & (BLOCK - 1)
    jm = lax.broadcasted_iota(jnp.int32, (rows, N_META), 1)
    ok_m = ((jnp.zeros((rows, N_META), jnp.int32) + n) >= 1) | (im >= PAD_ROWS + jm)
    first = lax.broadcasted_iota(jnp.int32, (BLOCK, LANES), 1) < HEAD_DIM
    neg = -jnp.inf
    outs = []
    for kv in range(ATT_KV_HEADS):
        tile, half = divmod(kv, 2)
        lanes = slice(tile * LANES, (tile + 1) * LANES)
        kc2, kp2, km2 = (_both_halves(t[:, lanes], half).astype(BF16) for t in (kc, kp, km))
        vc2, vp2, vm2 = (_both_halves(t[:, lanes], half).astype(BF16) for t in (vc, vp, vm))
        qs, slope, sk = [], [], []
        for pair in range(ATT_GROUP // 2):
            c0 = (kv * ATT_GROUP + 2 * pair) * HEAD_DIM
            qp = q[:, c0:c0 + LANES] * HEAD_DIM ** -0.5
            qs += [jnp.where(first, qp, 0.0), jnp.where(first, 0.0, qp)]
        for g in range(ATT_GROUP):
            slope.append(jnp.full((BLOCK, 1), ALIBI_SLOPES[kv * ATT_GROUP + g], F32))
            sk.append(jnp.broadcast_to(sinks[kv * ATT_GROUP + g], (BLOCK, 1)))
        qs = jnp.concatenate(qs, axis=0).astype(BF16)
        slope = jnp.concatenate(slope, axis=0)
        sk = jnp.concatenate(sk, axis=0)
        sc = jnp.where(ok_c, lax.dot_general(qs, kc2, _NT, preferred_element_type=F32) - slope * rel_c, neg)
        sp = jnp.where(ok_p, lax.dot_general(qs, kp2, _NT, preferred_element_type=F32) - slope * rel_p, neg)
        sm = jnp.where(ok_m, lax.dot_general(qs, km2, _NT, preferred_element_type=F32), neg)
        mx = jnp.maximum(jnp.maximum(jnp.max(sc, axis=1, keepdims=True), jnp.max(sp, axis=1, keepdims=True)),
                         jnp.maximum(jnp.max(sm, axis=1, keepdims=True), sk))
        mx = lax.stop_gradient(mx)
        ec, ep, em, es = jnp.exp(sc - mx), jnp.exp(sp - mx), jnp.exp(sm - mx), jnp.exp(sk - mx)
        den = (es + jnp.sum(ec, axis=1, keepdims=True) + jnp.sum(ep, axis=1, keepdims=True)
               + jnp.sum(em, axis=1, keepdims=True))
        inv = 1.0 / den
        o = (jnp.dot((ec * inv).astype(BF16), vc2, preferred_element_type=F32)
             + jnp.dot((ep * inv).astype(BF16), vp2, preferred_element_type=F32)
             + jnp.dot((em * inv).astype(BF16), vm2, preferred_element_type=F32))
        for pair in range(ATT_GROUP // 2):
            r0 = 2 * pair * BLOCK
            outs.append(jnp.where(first, o[r0:r0 + BLOCK], o[r0 + BLOCK:r0 + 2 * BLOCK]))
    return jnp.concatenate(outs, axis=1) * _silu(z)


def _attn_specs(nb, steps_clamped):
    def blk(t):
        return jnp.minimum(t, nb - 1) if steps_clamped else t

    wide = lambda col: pl.BlockSpec((BLOCK, D_MODEL), lambda t: (blk(t), col))
    cur = lambda col: pl.BlockSpec((BLOCK, KV_WIDTH), lambda t: (blk(t), col))
    prev = lambda col: pl.BlockSpec((BLOCK, KV_WIDTH), lambda t: (jnp.maximum(blk(t) - 1, 0), col))
    meta = lambda col: pl.BlockSpec((N_META, KV_WIDTH), lambda t: (META_ROW_BLOCK, col))
    sinks = pl.BlockSpec((ATT_Q_HEADS, 1, 1), lambda t: (0, 0, 0))
    return [wide(Q_BLOCK), wide(Z_ATT_BLOCK), prev(K_BLOCK), cur(K_BLOCK), prev(V_BLOCK), cur(V_BLOCK),
            meta(K_BLOCK), meta(V_BLOCK), sinks]


def attn_fwd(proj, sinks):
    nb = proj.shape[0] // BLOCK

    def body(q_ref, z_ref, kp_ref, kc_ref, vp_ref, vc_ref, km_ref, vm_ref, sk_ref, o_ref):
        o_ref[...] = _attn_rows(q_ref[...], z_ref[...], kp_ref[...], kc_ref[...], vp_ref[...], vc_ref[...],
                                km_ref[...], vm_ref[...], tuple(sk_ref[h] for h in range(ATT_Q_HEADS)),
                                pl.program_id(0)).astype(BF16)

    return _call(body, name="attn_fwd", grid=(nb,), in_specs=_attn_specs(nb, False), out_specs=_row_spec(D_MODEL),
                 out_shape=jax.ShapeDtypeStruct((nb * BLOCK, D_MODEL), BF16), sem=("parallel",))(*([proj] * 8), sinks)


def attn_bwd(da, proj, sinks):
    nb = proj.shape[0] // BLOCK
    last = nb - 1
    wide = pl.BlockSpec((BLOCK, D_MODEL), lambda t: (jnp.minimum(t, last), 0))
    done = pl.BlockSpec((BLOCK, KV_WIDTH), lambda t: (jnp.maximum(t - 1, 0), 0))
    meta = _full((N_META, KV_WIDTH))
    par = _full((ATT_Q_HEADS, 1, 1))

    def body(da_ref, q_ref, z_ref, kp_ref, kc_ref, vp_ref, vc_ref, km_ref, vm_ref, sk_ref,
             dq_ref, dz_ref, dk_ref, dv_ref, dkm_ref, dvm_ref, dsk_ref, ck_ref, cv_ref):
        t = pl.program_id(0)

        @pl.when(t == 0)
        def _():
            ck_ref[...] = jnp.zeros_like(ck_ref)
            cv_ref[...] = jnp.zeros_like(cv_ref)
            dkm_ref[...] = jnp.zeros_like(dkm_ref)
            dvm_ref[...] = jnp.zeros_like(dvm_ref)
            dsk_ref[...] = jnp.zeros_like(dsk_ref)

        @pl.when(t < nb)
        def _():
            def f(q, z, kp, kc, vp, vc, km, vm, sk):
                return _attn_rows(q, z, kp, kc, vp, vc, km, vm, sk, t)

            _, vjp = jax.vjp(f, q_ref[...], z_ref[...], kp_ref[...], kc_ref[...], vp_ref[...], vc_ref[...],
                             km_ref[...], vm_ref[...], tuple(sk_ref[h] for h in range(ATT_Q_HEADS)))
            dq, dz, dkp, dkc, dvp, dvc, dkm, dvm, dsk = vjp(da_ref[...])
            dq_ref[...] = dq.astype(BF16)
            dz_ref[...] = dz.astype(BF16)
            for h in range(ATT_Q_HEADS):
                dsk_ref[h] += dsk[h]
            dk_ref[...] = ck_ref[...] + dkp
            dv_ref[...] = cv_ref[...] + dvp
            ck_ref[...] = dkc
            cv_ref[...] = dvc
            dkm_ref[...] += dkm
            dvm_ref[...] += dvm

        @pl.when(t == nb)
        def _():
            dk_ref[...] = ck_ref[...]
            dv_ref[...] = cv_ref[...]

    rows = nb * BLOCK
    return _call(body, name="attn_bwd", grid=(nb + 1,), in_specs=[wide] + _attn_specs(nb, True),
                 out_specs=[wide, wide, done, done, meta, meta, par],
                 out_shape=[jax.ShapeDtypeStruct((rows, D_MODEL), BF16), jax.ShapeDtypeStruct((rows, D_MODEL), BF16),
                            jax.ShapeDtypeStruct((rows, KV_WIDTH), F32), jax.ShapeDtypeStruct((rows, KV_WIDTH), F32),
                            jax.ShapeDtypeStruct((N_META, KV_WIDTH), F32), jax.ShapeDtypeStruct((N_META, KV_WIDTH), F32),
                            jax.ShapeDtypeStruct(sinks.shape, F32)],
                 scratch=[pltpu.VMEM((BLOCK, KV_WIDTH), F32), pltpu.VMEM((BLOCK, KV_WIDTH), F32)],
                 sem=("arbitrary",))(da, *([proj] * 8), sinks)


XBC_BLOCK0 = SEG["xbc"][2] // D_MODEL
CONV_COL_BLOCKS = CONV_DIM // D_MODEL
DT_TILE = SEG["dt"][2] // LANES


HALO = 8
HALOS_PER_BLOCK = BLOCK // HALO


def _shift_rows(cur, before, j):
    if j == 0:
        return cur
    n = cur.shape[0]
    row = lax.broadcasted_iota(jnp.int32, cur.shape, 0)
    head = pltpu.roll(before, j, 0)
    if n > HALO:
        head = jnp.concatenate([head, jnp.zeros((n - HALO, cur.shape[1]), cur.dtype)], axis=0)
    return jnp.where(row >= j, pltpu.roll(cur, j, 0), head)


def _conv_pre(cur, before, w_ref, b_ref):
    pre = b_ref[...] + w_ref[CONV_WIDTH - 1:CONV_WIDTH, :] * cur
    for k in range(CONV_WIDTH - 1):
        pre = pre + w_ref[k:k + 1, :] * _shift_rows(cur, before, CONV_WIDTH - 1 - k)
    return pre


def _conv_specs(nb, col0=0):
    first = XBC_BLOCK0 + col0
    cur = pl.BlockSpec((BLOCK, D_MODEL), lambda j, i: (i, first + j))
    before = pl.BlockSpec((HALO, D_MODEL), lambda j, i: (jnp.maximum(i * HALOS_PER_BLOCK - 1, 0), first + j))
    after = pl.BlockSpec((HALO, D_MODEL), lambda j, i: (jnp.minimum(i + 1, nb - 1) * HALOS_PER_BLOCK, first + j))
    return cur, before, after


def _valid_rows(i):
    row = lax.broadcasted_iota(jnp.int32, (BLOCK, D_MODEL), 0)
    return jnp.maximum((row >= PAD_ROWS).astype(F32), jnp.where(i > 0, 1.0, 0.0))


def conv_fwd(proj, conv_w, conv_b):
    nb = proj.shape[0] // BLOCK
    cur, before, _ = _conv_specs(nb)

    def body(c_ref, p_ref, w_ref, b_ref, o_ref):
        i = pl.program_id(1)
        pre = _conv_pre(c_ref[...], p_ref[...] * jnp.where(i > 0, 1.0, 0.0), w_ref, b_ref)
        o_ref[...] = _silu(pre) * _valid_rows(i)

    return _call(body, name="conv_fwd", grid=(CONV_COL_BLOCKS, nb),
                 in_specs=[cur, before, pl.BlockSpec((CONV_WIDTH, D_MODEL), lambda j, i: (0, j)),
                           pl.BlockSpec((1, D_MODEL), lambda j, i: (0, j))],
                 out_specs=pl.BlockSpec((BLOCK, D_MODEL), lambda j, i: (i, j)),
                 out_shape=jax.ShapeDtypeStruct((nb * BLOCK, CONV_DIM), F32),
                 sem=("parallel", "parallel"))(proj, proj, conv_w, conv_b)


def conv_bwd(name, dparts, col0, proj, conv_w, conv_b):
    nb = proj.shape[0] // BLOCK
    last = nb - 1
    ncol = sum(d.shape[1] for d in dparts) // D_MODEL
    np_ = len(dparts)
    cur, before, after = _conv_specs(nb, col0)
    dcur = [pl.BlockSpec((BLOCK, d.shape[1] // ncol), lambda j, i: (i, j)) for d in dparts]
    dafter = [pl.BlockSpec((HALO, d.shape[1] // ncol), lambda j, i: (jnp.minimum(i + 1, last) * HALOS_PER_BLOCK, j))
              for d in dparts]
    out_cur = pl.BlockSpec((BLOCK, D_MODEL), lambda j, i: (i, j))
    wspec = pl.BlockSpec((CONV_WIDTH, D_MODEL), lambda j, i: (0, col0 + j))
    bspec = pl.BlockSpec((1, D_MODEL), lambda j, i: (0, col0 + j))
    wout = pl.BlockSpec((CONV_WIDTH, D_MODEL), lambda j, i: (0, j))
    bout = pl.BlockSpec((1, D_MODEL), lambda j, i: (0, j))

    def body(*refs):
        dc_refs, da_refs = refs[:np_], refs[np_:2 * np_]
        c_ref, p_ref, a_ref, w_ref, b_ref, du_ref, dw_ref, db_ref = refs[2 * np_:]
        i = pl.program_id(1)
        row = lax.broadcasted_iota(jnp.int32, (BLOCK, D_MODEL), 0)
        curv = c_ref[...]
        beforev = p_ref[...] * jnp.where(i > 0, 1.0, 0.0)
        side_by_side = lambda rs: rs[0][...] if np_ == 1 else jnp.concatenate([r[...] for r in rs], axis=1)

        def dpre_of(pre, d):
            s = _sigmoid(pre)
            return d * (s * (1.0 + pre * (1.0 - s)))

        dp_c = dpre_of(_conv_pre(curv, beforev, w_ref, b_ref), side_by_side(dc_refs) * _valid_rows(i))
        dp_a = dpre_of(_conv_pre(a_ref[...], curv[BLOCK - HALO:], w_ref, b_ref),
                       side_by_side(da_refs) * jnp.where(i < last, 1.0, 0.0))
        du = w_ref[CONV_WIDTH - 1:CONV_WIDTH, :] * dp_c
        for j in range(1, CONV_WIDTH):
            tail = jnp.concatenate([jnp.zeros((BLOCK - HALO, D_MODEL), F32), pltpu.roll(dp_a, HALO - j, 0)], axis=0)
            up = jnp.where(row < BLOCK - j, pltpu.roll(dp_c, BLOCK - j, 0), tail)
            du = du + w_ref[CONV_WIDTH - 1 - j:CONV_WIDTH - j, :] * up
        du_ref[...] = du.astype(BF16)

        @pl.when(i == 0)
        def _():
            dw_ref[...] = jnp.zeros_like(dw_ref)
            db_ref[...] = jnp.zeros_like(db_ref)

        for k in range(CONV_WIDTH):
            dw_ref[k:k + 1, :] += jnp.sum(dp_c * _shift_rows(curv, beforev, CONV_WIDTH - 1 - k), axis=0, keepdims=True)
        db_ref[...] += jnp.sum(dp_c, axis=0, keepdims=True)

    width = ncol * D_MODEL
    return _call(body, name=name, grid=(ncol, nb),
                 in_specs=dcur + dafter + [cur, before, after, wspec, bspec], out_specs=[out_cur, wout, bout],
                 out_shape=[jax.ShapeDtypeStruct((nb * BLOCK, width), BF16),
                            jax.ShapeDtypeStruct((CONV_WIDTH, width), F32), jax.ShapeDtypeStruct((1, width), F32)],
                 sem=("parallel", "arbitrary"))(*dparts, *dparts, proj, proj, proj, conv_w, conv_b)


def _head_expand():
    e = np.zeros((LANES, SSM_INNER), np.float32)
    for h in range(SSM_HEADS):
        e[h, h * HEAD_DIM:(h + 1) * HEAD_DIM] = 1.0
    return jnp.asarray(e, dtype=BF16)


def _softplus(x):
    return jnp.maximum(x, 0.0) + jnp.log(1.0 + jnp.exp(-jnp.abs(x)))


def _bf16_parts(x):
    hi = x.astype(BF16)
    rest = x - hi.astype(F32)
    mid = rest.astype(BF16)
    return hi, mid, (rest - mid.astype(F32)).astype(BF16)


@jax.custom_vjp
def _times_01(x, m):
    return sum(jnp.dot(p, m, preferred_element_type=F32) for p in _bf16_parts(x))


def _times_01_bwd(m, g):
    return sum(lax.dot_general(p, m, _NT, preferred_element_type=F32) for p in _bf16_parts(g)), jnp.zeros_like(m)


_times_01.defvjp(lambda x, m: (_times_01(x, m), m), _times_01_bwd)


def _causal_ones():
    l = lax.broadcasted_iota(jnp.int32, (BLOCK, BLOCK), 0)
    s = lax.broadcasted_iota(jnp.int32, (BLOCK, BLOCK), 1)
    return (l >= s).astype(BF16)


@jax.custom_vjp
def _cumsum_rows(a):
    return sum(jnp.dot(_causal_ones(), p, preferred_element_type=F32) for p in _bf16_parts(a))


def _cumsum_rows_bwd(_, g):
    tn = (((0,), (0,)), ((), ()))
    return (sum(lax.dot_general(_causal_ones(), p, tn, preferred_element_type=F32) for p in _bf16_parts(g)),)


_cumsum_rows.defvjp(lambda a: (_cumsum_rows(a), None), _cumsum_rows_bwd)


def _ssd_group(xs, dt_tile, expand, bias, alog, dsk, bg, cg, state):
    l = lax.broadcasted_iota(jnp.int32, (BLOCK, BLOCK), 0)
    s = lax.broadcasted_iota(jnp.int32, (BLOCK, BLOCK), 1)
    causal = l >= s
    first_head = s < HEAD_DIM
    dt = _softplus(dt_tile + bias)
    a = dt * (-jnp.exp(alog))
    one_row = lambda v: jnp.broadcast_to(v, (HALO, LANES))
    per_lane = _times_01(jnp.concatenate([dt, _cumsum_rows(a), one_row(jnp.sum(a, axis=0, keepdims=True)),
                                          one_row(dsk)], axis=0), expand)
    dtx, cs = per_lane[0:BLOCK], per_lane[BLOCK:2 * BLOCK]
    tot, dsk = per_lane[2 * BLOCK:2 * BLOCK + 1], per_lane[2 * BLOCK + HALO:2 * BLOCK + HALO + 1]
    bb, cb16 = bg.astype(BF16), cg.astype(BF16)
    cb = lax.dot_general(cb16, bb, _NT, preferred_element_type=F32)
    xr = xs * dtx
    y_diag = []
    for p in range(GROUP_W // LANES):
        lanes = slice(p * LANES, (p + 1) * LANES)
        c_pair = cs[:, lanes]
        c_swap = _swap_halves(c_pair)
        m = []
        for c_head in (jnp.where(first_head, c_pair, c_swap), jnp.where(first_head, c_swap, c_pair)):
            m.append(cb * jnp.exp(jnp.where(causal, c_head - c_head.T, -jnp.inf)))
        x_pair = xr[:, lanes]
        x_diag = jnp.concatenate([jnp.where(first_head, x_pair, 0.0), jnp.where(first_head, 0.0, x_pair)], axis=0)
        y_diag.append(jnp.dot(jnp.concatenate(m, axis=1).astype(BF16), x_diag.astype(BF16),
                              preferred_element_type=F32))
    st = lax.dot_general(bb, (xr * jnp.exp(tot - cs)).astype(BF16), (((0,), (0,)), ((), ())),
                         preferred_element_type=F32)
    new_state = state * jnp.exp(tot) + st
    y_off = jnp.dot(cb16, state.astype(BF16), preferred_element_type=F32) * jnp.exp(cs)
    return jnp.concatenate(y_diag, axis=1) + y_off + dsk * xs, new_state


B_TILE0 = SSM_INNER // LANES
C_TILE0 = B_TILE0 + SSM_GROUPS


def _ssd_specs(chunk):
    xs = pl.BlockSpec((BLOCK, GROUP_W), lambda c, g: (chunk(c), g))
    dt = pl.BlockSpec((BLOCK, LANES), lambda c, g: (chunk(c), DT_TILE))
    expand = pl.BlockSpec((LANES, GROUP_W), lambda c, g: (0, g))
    b = pl.BlockSpec((BLOCK, SSM_STATE), lambda c, g: (chunk(c), B_TILE0 + g))
    cc = pl.BlockSpec((BLOCK, SSM_STATE), lambda c, g: (chunk(c), C_TILE0 + g))
    par = _full((1, LANES))
    state = pl.BlockSpec((None, SSM_STATE, GROUP_W), lambda c, g: (chunk(c), 0, g))
    return xs, dt, expand, b, cc, par, state


def ssd_fwd(xbc, proj, expand, bias, alog, dsk):
    nb = xbc.shape[0] // BLOCK
    xs, dt, ex, b, cc, par, state = _ssd_specs(lambda c: c)

    def body(x_ref, dt_ref, e_ref, bi_ref, al_ref, dk_ref, b_ref, c_ref, y_ref, sp_ref, st_ref):
        g = pl.program_id(1)

        @pl.when(pl.program_id(0) == 0)
        def _():
            st_ref[g] = jnp.zeros((SSM_STATE, GROUP_W), F32)

        entering = st_ref[g]
        sp_ref[...] = entering
        y_ref[...], st_ref[g] = _ssd_group(x_ref[...], dt_ref[...], e_ref[...], bi_ref[...], al_ref[...], dk_ref[...],
                                           b_ref[...], c_ref[...], entering)

    return _call(body, name="ssd_fwd", grid=(nb, SSM_GROUPS), in_specs=[xs, dt, ex, par, par, par, b, cc],
                 out_specs=[xs, state],
                 out_shape=[jax.ShapeDtypeStruct((nb * BLOCK, SSM_INNER), F32),
                            jax.ShapeDtypeStruct((nb, SSM_STATE, SSM_INNER), F32)],
                 scratch=[pltpu.VMEM((SSM_GROUPS, SSM_STATE, GROUP_W), F32)],
                 sem=("arbitrary", "arbitrary"))(xbc, proj, expand, bias, alog, dsk, xbc, xbc)


def ssd_bwd(dy, xbc, proj, expand, bias, alog, dsk, states, comm):
    nb = xbc.shape[0] // BLOCK
    last = nb - 1
    xs, dt, ex, b, cc, par, state = _ssd_specs(lambda c: last - c)
    tile = pl.BlockSpec((BLOCK, LANES), lambda c, g: (last - c, 0))
    nspec = pl.BlockSpec((BLOCK, SSM_STATE), lambda c, g: (last - c, g))

    def body(dy_ref, x_ref, dt_ref, e_ref, bi_ref, al_ref, dk_ref, b_ref, c_ref, sp_ref,
             dx_ref, ddt_ref, db_ref, dc_ref, dbi_ref, dal_ref, ddk_ref, ds_ref, ddt_acc):
        c, g = pl.program_id(0), pl.program_id(1)

        @pl.when(c == 0)
        def _():
            ds_ref[g] = jnp.zeros((SSM_STATE, GROUP_W), F32)

        @pl.when((c == 0) & (g == 0))
        def _():
            dbi_ref[...] = jnp.zeros_like(dbi_ref)
            dal_ref[...] = jnp.zeros_like(dal_ref)
            ddk_ref[...] = jnp.zeros_like(ddk_ref)

        expand_rows = e_ref[...]

        def f(xs, dt_tile, bias, alog, dsk, bg, cg, state):
            return _ssd_group(xs, dt_tile, expand_rows, bias, alog, dsk, bg, cg, state)

        _, vjp = jax.vjp(f, x_ref[...], dt_ref[...], bi_ref[...], al_ref[...], dk_ref[...], b_ref[...], c_ref[...],
                         sp_ref[...])
        dx_ref[...], ddt, dbi, dal, ddk, db_ref[...], dc_ref[...], ds_ref[g] = vjp((dy_ref[...], ds_ref[g]))
        dbi_ref[...] += dbi
        dal_ref[...] += dal
        ddk_ref[...] += ddk

        @pl.when(g == 0)
        def _():
            ddt_acc[...] = ddt

        @pl.when(g > 0)
        def _():
            ddt_acc[...] += ddt

        @pl.when(g == SSM_GROUPS - 1)
        def _():
            ddt_ref[...] = ddt_acc[...].astype(BF16)

    def at():
        c, g = pl.program_id(0), pl.program_id(1)
        return (c == 0) & (g == 0), (c == 0) & (g == 1), (c == last) & (g == SSM_GROUPS - 1)

    par_shape = jax.ShapeDtypeStruct((1, LANES), F32)
    return _call_with_comm(
        body, comm, at, (dy, xbc, proj, expand, bias, alog, dsk, xbc, xbc, states), name="ssd_bwd",
        grid=(nb, SSM_GROUPS), in_specs=[xs, xs, dt, ex, par, par, par, b, cc, state],
        out_specs=[xs, tile, nspec, nspec, par, par, par],
        out_shape=[jax.ShapeDtypeStruct((nb * BLOCK, SSM_INNER), F32), jax.ShapeDtypeStruct((nb * BLOCK, LANES), BF16),
                   jax.ShapeDtypeStruct((nb * BLOCK, SSM_GROUPS * SSM_STATE), F32),
                   jax.ShapeDtypeStruct((nb * BLOCK, SSM_GROUPS * SSM_STATE), F32), par_shape, par_shape, par_shape],
        scratch=[pltpu.VMEM((SSM_GROUPS, SSM_STATE, GROUP_W), F32), pltpu.VMEM((BLOCK, LANES), F32)])


SLAB_ROWS = 24
SLAB_META_ROW = 8


SLAB_LOSS_ROW = 7


def pack_small(dcw, dcb, dgpre, dgpost, dbias, dalog, ddsk, dsinks, dgn, dmeta, loss_tile):
    def body(cw, cb, gpre, gpost, dtb, al, dk, sk, gn, meta, loss, o_ref):
        o_ref[...] = jnp.zeros_like(o_ref)
        o_ref[SLAB_LOSS_ROW:SLAB_LOSS_ROW + 1, 0:LANES] = loss[0:1, :]
        o_ref[0:CONV_WIDTH, :] = cw[...]
        o_ref[4:5, :] = cb[...]
        o_ref[5:6, 0:1024] = gpre[...]
        o_ref[5:6, 1024:2048] = gpost[...]
        o_ref[5:6, 2048:2176] = dtb[...]
        o_ref[5:6, 2176:2304] = al[...]
        o_ref[5:6, 2304:2432] = dk[...]
        o_ref[5:6, 2432:2560] = sk[...]
        o_ref[6:7, 0:SSM_INNER] = gn[...]
        o_ref[SLAB_META_ROW:SLAB_META_ROW + N_META, 0:D_MODEL] = meta[...]

    args = (dcw, dcb, dgpre, dgpost, dbias, dalog, ddsk, dsinks, dgn, dmeta, loss_tile)
    return _call(body, name="pack_small", in_specs=[_full(a.shape) for a in args],
                 out_specs=_full((SLAB_ROWS, CONV_DIM)), out_shape=jax.ShapeDtypeStruct((SLAB_ROWS, CONV_DIM), F32))(*args)


def _lane_tile(v):
    return jnp.pad(v, ((0, 0), (0, LANES - v.shape[1])))


def kernel(x, meta_tokens, g_pre, w_in, conv_w, conv_b, dt_bias, a_log, d_skip, attn_sinks, g_ssm_norm, w_out_att, w_out_ssm, w_out, g_post, loss_target, m_meta_tokens, m_g_pre, m_w_in, m_conv_w, m_conv_b, m_dt_bias, m_a_log, m_d_skip, m_attn_sinks, m_g_ssm_norm, m_w_out_att, m_w_out_ssm, m_w_out, m_g_post, v_meta_tokens, v_g_pre, v_w_in, v_conv_w, v_conv_b, v_dt_bias, v_a_log, v_d_skip, v_attn_sinks, v_g_ssm_norm, v_w_out_att, v_w_out_ssm, v_w_out, v_g_post):
    chip = _chip_index()

    conv_w_rows = jnp.pad(conv_w[0], ((0, 2 * 8 - CONV_WIDTH), (0, 0)))
    w_in_t, m_w_in_t, v_w_in_t = w_in[0].T, m_w_in[0].T, v_w_in[0].T
    gathered_w_in, g_conv_w, g_meta = run_comm("gather_w_in",
                                               TwoLevelGather([pack_w_in(w_in_t), conv_w_rows, meta_tokens]))
    w_all_t = unpack_w_in(gathered_w_in)
    cw_full = g_conv_w[:, :CONV_WIDTH].transpose(1, 0, 2).reshape(CONV_WIDTH, CONV_DIM)
    meta_full = g_meta.transpose(1, 0, 2).reshape(N_META, D_MODEL)

    h, u = prep(x, meta_full, g_pre)
    proj, g_w_out = project("in_proj", u, w_all_t, TwoLevelGather(
        [w_out_att[0].astype(BF16), w_out_ssm[0].astype(BF16), w_out[0].astype(BF16)]))
    woa = g_w_out[0].reshape(D_MODEL, D_MODEL)
    wos = g_w_out[1].reshape(SSM_INNER, D_MODEL)
    wo = g_w_out[2].reshape(D_MODEL, D_MODEL)

    sinks3 = attn_sinks.reshape(ATT_Q_HEADS, 1, 1)
    a_att = attn_fwd(proj, sinks3)

    xbc = conv_fwd(proj, cw_full, conv_b)
    expand = _head_expand()
    head_pars = (_lane_tile(dt_bias), _lane_tile(a_log), _lane_tile(d_skip))
    y_ssd, states = ssd_fwd(xbc, proj, expand, *head_pars)

    (yn, merged, dout, dy_att, dy_ssm, da_att, dy_ssd, dz_ssm, dga, dgs, dres, loss_tile, dg_post, dgn) = tail(
        y_ssd, proj, a_att, x, loss_target, woa, wos, wo, g_ssm_norm, g_post)

    dwo = mm_tn("out_proj_dw", merged, dout)
    dwoa = mm_tn("att_out_dw", a_att, dy_att)
    dwos = mm_tn("ssm_out_dw", yn, dy_ssm)
    dq, dz_att, dk, dv, dkmeta, dvmeta, dsinks3 = attn_bwd(da_att, proj, sinks3)
    dk = dk.at[PAD_ROWS:BLOCK].add(dkmeta).astype(BF16)
    dv = dv.at[PAD_ROWS:BLOCK].add(dvmeta).astype(BF16)

    def pieces(g):
        return g.reshape(4, 2, g.shape[0] // 8, g.shape[1])

    def to_owner(g):
        return (lambda ref, dev: ref.at[_chip_of(dev), dev[2]], (g.shape[0] // 8, g.shape[1]))

    (dxs, ddt_tile, dbg, dcg, dbias, dalog, ddsk), sent_w_out = ssd_bwd(
        dy_ssd, xbc, proj, expand, *head_pars, states,
        DirectExchange([pieces(dwoa), pieces(dwos), pieces(dwo)], [to_owner(dwoa), to_owner(dwos), to_owner(dwo)],
                       ALL_MASKS, "dev", 8))
    dxs_raw, dcw_xs, dcb_xs = conv_bwd("conv_bwd_x", [dxs], 0, proj, cw_full, conv_b)
    dbc_raw, dcw_bc, dcb_bc = conv_bwd("conv_bwd_bc", [dbg, dcg], SSM_INNER // D_MODEL, proj, cw_full, conv_b)
    dcw = jnp.concatenate([dcw_xs, dcw_bc], axis=1)
    dcb = jnp.concatenate([dcb_xs, dcb_bc], axis=1)

    narrow = jnp.concatenate([dk, dv, ddt_tile, jnp.zeros((dk.shape[0], N_ACT - N_ALIGNED), BF16)], axis=1)
    dproj = [dz_ssm, dxs_raw, dbc_raw, dq, dz_att, dga, dgs, narrow]
    dw_all_t = weight_grad_t("in_proj_dw", dproj, u)

    half_rows = PACK_W // 2
    partial = pack_grad_w_in(dw_all_t).reshape(4, 2, half_rows, D_MODEL)
    from_sibling, = run_comm("pair_grads", DirectExchange(
        [partial], [(lambda ref, dev: ref.at[pl.ds(0, 4), dev[2]], (4, half_rows, D_MODEL))], SIBLING_MASK, "core", 2,
        keep_own=False))
    chip_sum = sum_pair(partial, from_sibling)
    du, (sent_w_in,) = project_back("in_proj_dx", dproj, w_all_t, DirectExchange(
        [chip_sum], [(lambda ref, dev: ref.at[_chip_of(dev)], (half_rows, D_MODEL))], CHIP_MASKS, "chip", 4))
    grad_x, dmeta, dg_pre = prep_bwd(h, du, dres, g_pre)

    slab = pack_small(dcw, dcb, dg_pre, dg_post, dbias, dalog, ddsk, _lane_tile(dsinks3.reshape(1, ATT_Q_HEADS)), dgn,
                      dmeta, loss_tile)
    halves = [sum_slots("sum_" + nm, r)
              for nm, r in zip(("w_in", "w_out_att", "w_out_ssm", "w_out"), [sent_w_in] + list(sent_w_out))]
    shared = run_comm("share_grads", Both(DirectExchange(halves, [None] * 4, SIBLING_MASK, "core", 2),
                                          DirectExchange([slab], [None], ALL_MASKS, "dev", 8)))
    g_w_in_packed, g_woa, g_wos, g_wo = [f.reshape(2 * f.shape[1], f.shape[2]) for f in shared[:4]]
    small = sum_slots("sum_small", shared[4])
    loss = small[SLAB_LOSS_ROW, 0]

    g_w_in, d_w_in, nm_w_in, nv_w_in = [a.T for a in adamw_w_in(g_w_in_packed, w_in_t, m_w_in_t, v_w_in_t)]
    d_woa, nm_woa, nv_woa = adamw_rows("adamw_w_out_att", g_woa, w_out_att[0], m_w_out_att[0], v_w_out_att[0])
    d_wos, nm_wos, nv_wos = adamw_rows("adamw_w_out_ssm", g_wos, w_out_ssm[0], m_w_out_ssm[0], v_w_out_ssm[0])
    d_wo, nm_wo, nv_wo = adamw_rows("adamw_w_out", g_wo, w_out[0], m_w_out[0], v_w_out[0])

    cw_cols = CONV_DIM // 4
    meta_cols = D_MODEL // 4
    g_small = {
        "meta_tokens": lax.dynamic_slice(small, (SLAB_META_ROW, chip * meta_cols), (N_META, meta_cols)),
        "g_pre": small[5:6, 0:1024],
        "conv_w": lax.dynamic_slice(small, (0, chip * cw_cols), (CONV_WIDTH, cw_cols)),
        "conv_b": small[4:5, :],
        "dt_bias": small[5:6, 2048:2048 + SSM_HEADS],
        "a_log": small[5:6, 2176:2176 + SSM_HEADS],
        "d_skip": small[5:6, 2304:2304 + SSM_HEADS],
        "attn_sinks": small[5:6, 2432:2432 + ATT_Q_HEADS],
        "g_ssm_norm": small[6:7, 0:SSM_INNER],
        "g_post": small[5:6, 1024:2048],
    }
    names = list(g_small)
    w_small = dict(meta_tokens=meta_tokens, g_pre=g_pre, conv_w=conv_w[0], conv_b=conv_b, dt_bias=dt_bias, a_log=a_log,
                   d_skip=d_skip, attn_sinks=attn_sinks, g_ssm_norm=g_ssm_norm, g_post=g_post)
    m_small = dict(meta_tokens=m_meta_tokens, g_pre=m_g_pre, conv_w=m_conv_w[0], conv_b=m_conv_b, dt_bias=m_dt_bias,
                   a_log=m_a_log, d_skip=m_d_skip, attn_sinks=m_attn_sinks, g_ssm_norm=m_g_ssm_norm, g_post=m_g_post)
    v_small = dict(meta_tokens=v_meta_tokens, g_pre=v_g_pre, conv_w=v_conv_w[0], conv_b=v_conv_b, dt_bias=v_dt_bias,
                   a_log=v_a_log, d_skip=v_d_skip, attn_sinks=v_attn_sinks, g_ssm_norm=v_g_ssm_norm, g_post=v_g_post)
    upd = dict(zip(names, adamw_small([g_small[k] for k in names], [w_small[k] for k in names],
                                      [m_small[k] for k in names], [v_small[k] for k in names])))

    lead = {"conv_w"}

    def shaped(name, a):
        return a[None] if name in lead else a

    grads = dict(g_small, w_in=g_w_in, w_out_att=g_woa, w_out_ssm=g_wos, w_out=g_wo)
    deltas = dict({k: upd[k][0] for k in names}, w_in=d_w_in, w_out_att=d_woa, w_out_ssm=d_wos, w_out=d_wo)
    new_m = dict({k: upd[k][1] for k in names}, w_in=nm_w_in, w_out_att=nm_woa, w_out_ssm=nm_wos, w_out=nm_wo)
    new_v = dict({k: upd[k][2] for k in names}, w_in=nv_w_in, w_out_att=nv_woa, w_out_ssm=nv_wos, w_out=nv_wo)
    lead |= {"w_in", "w_out_att", "w_out_ssm", "w_out"}
    order = ["meta_tokens", "g_pre", "w_in", "conv_w", "conv_b", "dt_bias", "a_log", "d_skip", "attn_sinks",
             "g_ssm_norm", "w_out_att", "w_out_ssm", "w_out", "g_post"]
    outs = [loss, grad_x]
    for group in (grads, deltas, new_m, new_v):
        outs += [shaped(k, group[k]) for k in order]
    return tuple(outs)
```

```python
import functools

import numpy as np
import jax
import jax.numpy as jnp
from jax import lax
from jax.experimental import pallas as pl
from jax.experimental.pallas import tpu as pltpu

F32 = jnp.float32
BF16 = jnp.bfloat16
HI = lax.Precision.HIGHEST

D_MODEL = 1024
N_META = 16
BLOCK = 128
PAD_ROWS = BLOCK - N_META
NORM_EPS = 1e-6
HEAD_DIM = 64
ATT_Q_HEADS = 16
ATT_KV_HEADS = 4
ATT_GROUP = 4
SSM_INNER = 2048
SSM_HEADS = 32
SSM_GROUPS = 4
SSM_HEADS_PER_GROUP = 8
SSM_STATE = 128
CONV_WIDTH = 4
CONV_DIM = 3072
LANES = 128

ADAM_LR = 0.001
ADAM_B1 = 0.9
ADAM_B2 = 0.999
ADAM_EPS = 1e-08
ADAM_WD = 0.01
ADAM_STEP = 10

VMEM_LIMIT = 48 * 1024 * 1024

SHARD_W = 2440
PACK_W = 2560
SHARD_STRIDE = 2432
N_ALIGNED = 9856
N_ACT = 10240
SEG = {
    "q": (0, 1024, 5120), "k": (1024, 256, 9216), "v": (1280, 256, 9472), "z_att": (1536, 1024, 6144),
    "z_ssm": (2560, 2048, 0), "xbc": (4608, 3072, 2048), "dt": (7680, 128, 9728),
    "gate_att": (7808, 1024, 7168), "gate_ssm": (8832, 1024, 8192),
}
DT_STORED_START = 7680
DT_PAD = LANES - SSM_HEADS


def _act_col(aligned_col):
    for a0, w, p0 in SEG.values():
        if a0 <= aligned_col < a0 + w:
            return p0 + aligned_col - a0
    raise ValueError(aligned_col)


def _call(body, *, name, out_shape, in_specs, out_specs, grid=(), scratch=(), sem=None, aliases=None):
    return pl.pallas_call(
        body, out_shape=out_shape, grid=grid, in_specs=in_specs, out_specs=out_specs, scratch_shapes=list(scratch),
        name=name, input_output_aliases=aliases or {},
        compiler_params=pltpu.CompilerParams(dimension_semantics=sem, vmem_limit_bytes=VMEM_LIMIT))


def _full(shape):
    n = len(shape)
    return pl.BlockSpec(shape, lambda *_: (0,) * n)


def _chip_index():
    return lax.axis_index("x") * 2 + lax.axis_index("y")


_sigmoid = jax.nn.sigmoid


def _silu(z):
    return z * _sigmoid(z)


def _rms(x, g):
    return x * lax.rsqrt(jnp.mean(x * x, axis=-1, keepdims=True) + NORM_EPS) * g


def _peer(mask):
    x, y, c = lax.axis_index("x"), lax.axis_index("y"), lax.axis_index("c")
    return ((1 - x) if mask & 4 else x, (1 - y) if mask & 2 else y, (1 - c) if mask & 1 else c)


def _me():
    return lax.axis_index("x"), lax.axis_index("y"), lax.axis_index("c")


def _chip_of(dev):
    return 2 * dev[0] + dev[1]


CHIP_MASKS = (4, 2, 6)
ALL_MASKS = (1, 2, 3, 4, 5, 6, 7)
SIBLING_MASK = (1,)


def _remote(src, dst, send_sem, recv_sem, dev):
    return pltpu.make_async_remote_copy(src_ref=src, dst_ref=dst, send_sem=send_sem, recv_sem=recv_sem,
                                        device_id=dev, device_id_type=pl.DeviceIdType.MESH)


class _StagedCopy:
    def __init__(self, src, stage, dst, load_sem, store_sem):
        self.load = pltpu.make_async_copy(src, stage, load_sem)
        self.store = pltpu.make_async_copy(stage, dst, store_sem)

    def start(self):
        self.load.start()
        self.load.wait()
        self.store.start()

    def wait(self):
        self.store.wait()


class DirectExchange:
    def __init__(self, arrays, pieces, masks, slot_kind, nslots, keep_own=True):
        self.arrays, self.pieces, self.masks, self.slot_kind = list(arrays), list(pieces), masks, slot_kind
        self.keep_own = keep_own
        n, nk = len(arrays), len(masks)
        shapes = [a.shape if p is None else p[1] for a, p in zip(arrays, pieces)]
        self.out_shape = [jax.ShapeDtypeStruct((nslots,) + tuple(s), a.dtype) for s, a in zip(shapes, arrays)]
        self.scratch = [pltpu.SemaphoreType.DMA((n * nk,)), pltpu.SemaphoreType.DMA((n * nk,))]
        if keep_own:
            self.scratch += [pltpu.SemaphoreType.DMA((2 * n,))] + [pltpu.VMEM(s, a.dtype) for s, a in zip(shapes, arrays)]
        self.has_mid = False

    def _copies(self, ins, outs, scratch):
        send_sems, recv_sems = scratch[:2]
        me = _me()
        slot = {"chip": _chip_of(me), "dev": 4 * me[0] + 2 * me[1] + me[2], "core": me[2]}[self.slot_kind]
        nk = len(self.masks)

        def piece(a, dev):
            return ins[a] if self.pieces[a] is None else self.pieces[a][0](ins[a], dev)

        local = []
        if self.keep_own:
            local_sems, stages = scratch[2], scratch[3:]
            local = [_StagedCopy(piece(a, me), stages[a], outs[a].at[slot], local_sems.at[2 * a], local_sems.at[2 * a + 1])
                     for a in range(len(ins))]
        remote = []
        for a in range(len(ins)):
            for ki, mask in enumerate(self.masks):
                dev = _peer(mask)
                remote.append(_remote(piece(a, dev), outs[a].at[slot], send_sems.at[a * nk + ki],
                                      recv_sems.at[a * nk + ki], dev))
        return local, remote

    def start(self, ins, outs, scratch):
        local, remote = self._copies(ins, outs, scratch)
        for cp in remote + local:
            cp.start()

    def finish(self, ins, outs, scratch):
        local, remote = self._copies(ins, outs, scratch)
        for cp in remote + local:
            cp.wait()


class TwoLevelGather:
    def __init__(self, arrays):
        self.arrays = list(arrays)
        n, nk = len(arrays), len(CHIP_MASKS)
        self.out_shape = [jax.ShapeDtypeStruct((4,) + a.shape, a.dtype) for a in arrays]
        self.scratch = ([pltpu.SemaphoreType.DMA((n * nk,)) for _ in range(4)] + [pltpu.SemaphoreType.DMA((2 * n,))]
                        + [pltpu.VMEM(a.shape, a.dtype) for a in arrays])
        self.has_mid = True

    def _copies(self, ins, outs, scratch):
        ici_send, ici_recv, fwd_send, fwd_recv, local_sems = scratch[:5]
        stages = scratch[5:]
        me = _me()
        sibling = _peer(1)
        nk = len(CHIP_MASKS)
        local, ici, fwd = [], [], []
        for a in range(len(ins)):
            half = ins[a].shape[0] // 2
            mine = pl.ds(me[2] * half, half)
            local.append(_StagedCopy(ins[a], stages[a], outs[a].at[_chip_of(me)], local_sems.at[2 * a],
                                     local_sems.at[2 * a + 1]))
            for ki, mask in enumerate(CHIP_MASKS):
                dev = _peer(mask)
                k = a * nk + ki
                ici.append(_remote(ins[a].at[mine], outs[a].at[_chip_of(me), mine], ici_send.at[k], ici_recv.at[k], dev))
                arrived = outs[a].at[_chip_of(dev), mine]
                fwd.append(_remote(arrived, arrived, fwd_send.at[k], fwd_recv.at[k], sibling))
        return local, ici, fwd

    def start(self, ins, outs, scratch):
        local, ici, _ = self._copies(ins, outs, scratch)
        for cp in ici + local:
            cp.start()

    def mid(self, ins, outs, scratch):
        _, ici, fwd = self._copies(ins, outs, scratch)
        for arrival, onward in zip(ici, fwd):
            arrival.wait_recv()
            onward.start()

    def finish(self, ins, outs, scratch):
        local, ici, fwd = self._copies(ins, outs, scratch)
        for cp in fwd:
            cp.wait_recv()
        for cp in ici + fwd:
            cp.wait_send()
        for cp in local:
            cp.wait()


class Both:
    def __init__(self, a, b):
        self.a, self.b = a, b
        self.arrays, self.out_shape = a.arrays + b.arrays, a.out_shape + b.out_shape
        self.scratch = a.scratch + b.scratch
        self.has_mid = False
        assert not (a.has_mid or b.has_mid)

    def _parts(self, ins, outs, sems):
        na, sa = len(self.a.arrays), len(self.a.scratch)
        return (ins[:na], outs[:na], sems[:sa]), (ins[na:], outs[na:], sems[sa:])

    def start(self, ins, outs, sems):
        pa, pb = self._parts(ins, outs, sems)
        self.a.start(*pa)
        self.b.start(*pb)

    def finish(self, ins, outs, sems):
        pa, pb = self._parts(ins, outs, sems)
        self.a.finish(*pa)
        self.b.finish(*pb)


_ANY = pl.BlockSpec(memory_space=pl.ANY)


def run_comm(name, comm):
    n = len(comm.arrays)

    def body(*refs):
        ins, outs, sems = refs[:n], refs[n:2 * n], refs[2 * n:]
        comm.start(ins, outs, sems)
        if comm.has_mid:
            comm.mid(ins, outs, sems)
        comm.finish(ins, outs, sems)

    return pl.pallas_call(body, name=name, out_shape=comm.out_shape, in_specs=[_ANY] * n, out_specs=[_ANY] * n,
                          scratch_shapes=comm.scratch,
                          compiler_params=pltpu.CompilerParams(vmem_limit_bytes=VMEM_LIMIT))(*comm.arrays)


def _call_with_comm(body, comm, steps, args, *, name, out_shape, in_specs, out_specs, grid, scratch=()):
    ni, no, ns, nc = len(in_specs), len(out_specs), len(scratch), len(comm.arrays)

    def full_body(*refs):
        ins, cins = refs[:ni], refs[ni:ni + nc]
        outs, couts = refs[ni + nc:ni + nc + no], refs[ni + nc + no:ni + 2 * nc + no]
        scr, csems = refs[ni + 2 * nc + no:ni + 2 * nc + no + ns], refs[ni + 2 * nc + no + ns:]
        first, middle, last = steps()
        pl.when(first)(lambda: comm.start(cins, couts, csems))
        if comm.has_mid:
            pl.when(middle)(lambda: comm.mid(cins, couts, csems))
        body(*ins, *outs, *scr)
        pl.when(last)(lambda: comm.finish(cins, couts, csems))

    res = pl.pallas_call(
        full_body, name=name, out_shape=list(out_shape) + comm.out_shape, grid=grid,
        in_specs=list(in_specs) + [_ANY] * nc, out_specs=list(out_specs) + [_ANY] * nc,
        scratch_shapes=list(scratch) + comm.scratch,
        compiler_params=pltpu.CompilerParams(dimension_semantics=("arbitrary",) * len(grid),
                                             vmem_limit_bytes=VMEM_LIMIT))(*args, *comm.arrays)
    return res[:no], res[no:]


def _shard_pieces(chip):
    if chip < 3:
        return [(0, SHARD_W, 8 * chip)]
    behind_dt = DT_STORED_START + SSM_HEADS - 3 * SHARD_W
    return [(0, behind_dt, 24), (behind_dt, SHARD_W - behind_dt, behind_dt + 24 + DT_PAD)]


W_IN_COLS = 256


def pack_w_in(wt):
    def body(w_ref, o_ref, pad_ref):
        chip = _chip_index()
        pad_ref[...] = jnp.zeros_like(pad_ref)
        for cv in range(4):
            @pl.when(chip == cv)
            def _():
                for src, n, dst in _shard_pieces(cv):
                    pad_ref[dst:dst + n, :] = w_ref[src:src + n, :]
        o_ref[...] = pad_ref[...].astype(BF16)

    return _call(body, name="pack_w_in", grid=(D_MODEL // W_IN_COLS,),
                 in_specs=[pl.BlockSpec((SHARD_W, W_IN_COLS), lambda i: (0, i))],
                 out_specs=pl.BlockSpec((PACK_W, W_IN_COLS), lambda i: (0, i)),
                 out_shape=jax.ShapeDtypeStruct((PACK_W, D_MODEL), BF16),
                 scratch=[pltpu.VMEM((PACK_W, W_IN_COLS), F32)], sem=("parallel",))(wt)


def _tile_runs():
    runs, fix = [], []
    for t in range(N_ALIGNED // LANES):
        s = min(t // 19, 3)
        j = t - 19 * s
        p = _act_col(t * LANES)
        if runs and runs[-1][1] == s and runs[-1][0] + runs[-1][3] == p and runs[-1][2] + runs[-1][3] == j * LANES:
            runs[-1][3] += LANES
        else:
            runs.append([p, s, j * LANES, LANES])
        if j == 0 and s > 0:
            fix.append((p, s - 1))
    return runs, fix


def unpack_w_in(bg):
    runs, fix = _tile_runs()

    def body(b_ref, o_ref):
        for p, s, j, w in runs:
            o_ref[p:p + w, :] = b_ref[s, j:j + w, :]
        for p, s in fix:
            o_ref[p:p + LANES, :] = o_ref[p:p + LANES, :] + b_ref[s, SHARD_STRIDE:PACK_W, :]
        o_ref[N_ALIGNED:N_ACT, :] = jnp.zeros((N_ACT - N_ALIGNED, W_IN_COLS), BF16)

    return _call(body, name="unpack_w_in", grid=(D_MODEL // W_IN_COLS,),
                 in_specs=[pl.BlockSpec((4, PACK_W, W_IN_COLS), lambda i: (0, 0, i))],
                 out_specs=pl.BlockSpec((N_ACT, W_IN_COLS), lambda i: (0, i)),
                 out_shape=jax.ShapeDtypeStruct((N_ACT, D_MODEL), BF16), sem=("parallel",))(bg)


def pack_grad_w_in(dwt):
    def body(g_ref, o_ref):
        for s in range(4):
            for j in range(PACK_W // LANES):
                p = _act_col((19 * s + j) * LANES)
                o_ref[s, j * LANES:(j + 1) * LANES, :] = g_ref[p:p + LANES, :]

    return _call(body, name="pack_grad_w_in", grid=(D_MODEL // W_IN_COLS,),
                 in_specs=[pl.BlockSpec((N_ACT, W_IN_COLS), lambda i: (0, i))],
                 out_specs=pl.BlockSpec((4, PACK_W, W_IN_COLS), lambda i: (0, 0, i)),
                 out_shape=jax.ShapeDtypeStruct((4, PACK_W, D_MODEL), BF16), sem=("parallel",))(dwt)


def _adamw(w, g, m, v):
    m = ADAM_B1 * m + (1.0 - ADAM_B1) * g
    v = ADAM_B2 * v + (1.0 - ADAM_B2) * jnp.square(g)
    m_hat = m / (1.0 - ADAM_B1 ** ADAM_STEP)
    v_hat = v / (1.0 - ADAM_B2 ** ADAM_STEP)
    delta = -ADAM_LR * (m_hat / (jnp.sqrt(v_hat) + ADAM_EPS) + ADAM_WD * w)
    return delta, m, v


def adamw_w_in(g_packed, wt, mt, vt):
    cols = LANES

    def body(g_ref, w_ref, m_ref, v_ref, go_ref, d_ref, mo_ref, vo_ref):
        chip = _chip_index()
        for cv in range(4):
            @pl.when(chip == cv)
            def _():
                for dst, n, src in _shard_pieces(cv):
                    go_ref[dst:dst + n, :] = g_ref[src:src + n, :]
        d_ref[...], mo_ref[...], vo_ref[...] = _adamw(w_ref[...], go_ref[...], m_ref[...], v_ref[...])

    spec = pl.BlockSpec((SHARD_W, cols), lambda i: (0, i))
    shp = jax.ShapeDtypeStruct((SHARD_W, D_MODEL), F32)
    return _call(body, name="adamw_w_in", grid=(D_MODEL // cols,),
                 in_specs=[pl.BlockSpec((PACK_W, cols), lambda i: (0, i)), spec, spec, spec],
                 out_specs=[spec] * 4, out_shape=[shp] * 4, sem=("parallel",))(g_packed, wt, mt, vt)


def adamw_rows(name, g, w, m, v):
    r, c = g.shape
    rows = min(r, BLOCK)

    def body(g_ref, w_ref, m_ref, v_ref, d_ref, mo_ref, vo_ref):
        d_ref[...], mo_ref[...], vo_ref[...] = _adamw(w_ref[...], g_ref[...], m_ref[...], v_ref[...])

    spec = pl.BlockSpec((rows, c), lambda i: (i, 0))
    shp = jax.ShapeDtypeStruct((r, c), F32)
    return _call(body, name=name, grid=(r // rows,), in_specs=[spec] * 4, out_specs=[spec] * 3, out_shape=[shp] * 3,
                 sem=("parallel",))(g, w, m, v)


def adamw_small(gs, ws, ms, vs):
    n = len(gs)

    def body(*refs):
        g, w, m, v = refs[:n], refs[n:2 * n], refs[2 * n:3 * n], refs[3 * n:4 * n]
        outs = refs[4 * n:]
        for i in range(n):
            d, mn, vn = _adamw(w[i][...], g[i][...], m[i][...], v[i][...])
            outs[3 * i][...] = d
            outs[3 * i + 1][...] = mn
            outs[3 * i + 2][...] = vn

    specs = [_full(a.shape) for a in gs]
    res = _call(body, name="adamw_small", in_specs=specs * 4,
                out_specs=[s for s in specs for _ in range(3)],
                out_shape=[jax.ShapeDtypeStruct(a.shape, F32) for a in gs for _ in range(3)])(*gs, *ws, *ms, *vs)
    return [tuple(res[3 * i:3 * i + 3]) for i in range(n)]


def sum_slots(name, r):
    s, rr, c = r.shape
    rows = min(rr, BLOCK)

    def body(r_ref, o_ref):
        acc = r_ref[0].astype(F32)
        for k in range(1, s):
            acc = acc + r_ref[k].astype(F32)
        o_ref[...] = acc

    return _call(body, name=name, grid=(rr // rows,), in_specs=[pl.BlockSpec((s, rows, c), lambda i: (0, i, 0))],
                 out_specs=pl.BlockSpec((rows, c), lambda i: (i, 0)), out_shape=jax.ShapeDtypeStruct((rr, c), F32),
                 sem=("parallel",))(r)


def sum_pair(partial, from_sibling):
    s, _, rr, cols = partial.shape
    rows = rr // 2

    def body(p_ref, r_ref, o_ref):
        c = lax.axis_index("c")
        o_ref[...] = (p_ref[c].astype(F32) + r_ref[1 - c].astype(F32)).astype(BF16)

    return _call(body, name="sum_pair", grid=(s, rr // rows),
                 in_specs=[pl.BlockSpec((None, 2, rows, cols), lambda k, i: (k, 0, i, 0)),
                           pl.BlockSpec((2, None, rows, cols), lambda k, i: (0, k, i, 0))],
                 out_specs=pl.BlockSpec((None, rows, cols), lambda k, i: (k, i, 0)),
                 out_shape=jax.ShapeDtypeStruct((s, rr, cols), BF16), sem=("parallel", "parallel"))(partial, from_sibling)


def _col_tile(n, k):
    if n % 896 == 0 and k <= 1024:
        return 896
    return min(n, 512)


def project(name, x, wt, comm):
    m, k = x.shape
    n = wt.shape[0]
    tn = D_MODEL
    steps = n // tn

    def body(x_ref, w_ref, o_ref):
        o_ref[...] = lax.dot_general(x_ref[...], w_ref[...], _NT, preferred_element_type=F32)

    def at():
        j = pl.program_id(0)
        return j == 0, j == (3 * steps) // 4, j == steps - 1

    (res,), moved = _call_with_comm(
        body, comm, at, (x, wt), name=name, grid=(steps,),
        in_specs=[_full((m, k)), pl.BlockSpec((tn, k), lambda j: (j, 0))],
        out_specs=[pl.BlockSpec((m, tn), lambda j: (0, j))], out_shape=[jax.ShapeDtypeStruct((m, n), F32)])
    return res, moved


def _piece_tiles(pieces):
    spans, start = [], 0
    for p in pieces:
        spans.append((start, p.shape[1] // D_MODEL))
        start += p.shape[1] // D_MODEL
    return spans, start


def _piece_spec(tm, span, rows_of, tile_of):
    first, count = span

    def index(i, j):
        t = tile_of(i, j) - first
        mine = (t >= 0) & (t < count)
        return jnp.where(mine, rows_of(i, j), 0), jnp.clip(t, 0, count - 1)

    return pl.BlockSpec((tm, D_MODEL), index)


def project_back(name, pieces, wt, comm):
    m = pieces[0].shape[0]
    k = wt.shape[1]
    tm = m // 2
    spans, steps = _piece_tiles(pieces)

    def at():
        i, j = pl.program_id(0), pl.program_id(1)
        return (i == 0) & (j == 0), (i == 0) & (j == steps - 1), (i == m // tm - 1) & (j == steps - 1)

    def body(*refs):
        w_ref, o_ref = refs[len(pieces)], refs[len(pieces) + 1]
        j = pl.program_id(1)

        @pl.when(j == 0)
        def _():
            o_ref[...] = jnp.zeros_like(o_ref)

        for dy_ref, (first, count) in zip(refs, spans):
            @pl.when((j >= first) & (j < first + count))
            def _():
                o_ref[...] += jnp.dot(dy_ref[...], w_ref[...], preferred_element_type=F32)

    (res,), moved = _call_with_comm(
        body, comm, at, (*pieces, wt), name=name, grid=(m // tm, steps),
        in_specs=[_piece_spec(tm, s, lambda i, j: i, lambda i, j: j) for s in spans]
        + [pl.BlockSpec((D_MODEL, k), lambda i, j: (j, 0))],
        out_specs=[pl.BlockSpec((tm, k), lambda i, j: (i, 0))], out_shape=[jax.ShapeDtypeStruct((m, k), F32)])
    return res, moved


def weight_grad_t(name, pieces, x):
    m = pieces[0].shape[0]
    k = x.shape[1]
    tm = m // 2
    spans, steps = _piece_tiles(pieces)

    def body(*refs):
        x_ref, o_ref, acc_ref = refs[len(pieces):]
        i, half = pl.program_id(0), pl.program_id(1)
        for dy_ref, (first, count) in zip(refs, spans):
            @pl.when((i >= first) & (i < first + count))
            def _():
                part = lax.dot_general(dy_ref[...], x_ref[...], (((0,), (0,)), ((), ())), preferred_element_type=F32)

                @pl.when(half == 0)
                def _():
                    acc_ref[...] = part

                @pl.when(half == 1)
                def _():
                    o_ref[...] = (acc_ref[...] + part).astype(BF16)

    return _call(body, name=name, grid=(steps, 2),
                 in_specs=[_piece_spec(tm, s, lambda i, j: j, lambda i, j: i) for s in spans]
                 + [pl.BlockSpec((tm, k), lambda i, j: (j, 0))],
                 out_specs=pl.BlockSpec((D_MODEL, k), lambda i, j: (i, 0)),
                 out_shape=jax.ShapeDtypeStruct((steps * D_MODEL, k), BF16),
                 scratch=[pltpu.VMEM((D_MODEL, k), F32)], sem=("parallel", "arbitrary"))(*pieces, x)


def mm_tn(name, x, dy):
    m, k = x.shape
    n = dy.shape[1]
    tm = m // 2
    tn = _col_tile(n, k)

    def body(x_ref, dy_ref, o_ref, acc_ref):
        part = lax.dot_general(x_ref[...].astype(BF16), dy_ref[...].astype(BF16), (((0,), (0,)), ((), ())),
                               preferred_element_type=F32)

        @pl.when(pl.program_id(1) == 0)
        def _():
            acc_ref[...] = part

        @pl.when(pl.program_id(1) == 1)
        def _():
            o_ref[...] = (acc_ref[...] + part).astype(BF16)

    return _call(body, name=name, grid=(n // tn, 2),
                 in_specs=[pl.BlockSpec((tm, k), lambda i, j: (j, 0)), pl.BlockSpec((tm, tn), lambda i, j: (j, i))],
                 out_specs=pl.BlockSpec((k, tn), lambda i, j: (0, i)), out_shape=jax.ShapeDtypeStruct((k, n), BF16),
                 scratch=[pltpu.VMEM((k, tn), F32)], sem=("parallel", "arbitrary"))(x, dy)


def _row_spec(width, col_block=0):
    return pl.BlockSpec((BLOCK, width), lambda i: (i, col_block))


def _x_spec():
    return pl.BlockSpec((None, BLOCK, D_MODEL), lambda i: (0, jnp.maximum(i - 1, 0), 0))


def prep(x, meta, g_pre):
    nb = x.shape[1] // BLOCK + 1

    def body(x_ref, meta_ref, g_ref, h_ref, u_ref):
        i = pl.program_id(0)

        @pl.when(i == 0)
        def _():
            h_ref[0:PAD_ROWS, :] = jnp.zeros((PAD_ROWS, D_MODEL), F32)
            h_ref[PAD_ROWS:BLOCK, :] = meta_ref[...]

        @pl.when(i > 0)
        def _():
            h_ref[...] = x_ref[...]

        u_ref[...] = _rms(h_ref[...], g_ref[...]).astype(BF16)

    return _call(body, name="prep", grid=(nb,), in_specs=[_x_spec(), _full((N_META, D_MODEL)), _full((1, D_MODEL))],
                 out_specs=[_row_spec(D_MODEL), _row_spec(D_MODEL)],
                 out_shape=[jax.ShapeDtypeStruct((nb * BLOCK, D_MODEL), F32),
                            jax.ShapeDtypeStruct((nb * BLOCK, D_MODEL), BF16)], sem=("parallel",))(x, meta, g_pre)


def prep_bwd(h, du, dres, g_pre):
    nb = h.shape[0] // BLOCK

    def body(h_ref, du_ref, dres_ref, g_ref, gx_ref, gm_ref, gg_ref):
        i = pl.program_id(0)
        _, vjp = jax.vjp(_rms, h_ref[...], g_ref[...])
        dh, dg = vjp(du_ref[...])

        @pl.when(i == 0)
        def _():
            gm_ref[...] = dh[PAD_ROWS:BLOCK, :]
            gg_ref[...] = dg

        @pl.when(i > 0)
        def _():
            gg_ref[...] += dg

        gx_ref[...] = dh + dres_ref[...]

    return _call(body, name="prep_bwd", grid=(nb,),
                 in_specs=[_row_spec(D_MODEL), _row_spec(D_MODEL), _row_spec(D_MODEL), _full((1, D_MODEL))],
                 out_specs=[_x_spec(), _full((N_META, D_MODEL)), _full((1, D_MODEL))],
                 out_shape=[jax.ShapeDtypeStruct((1, (nb - 1) * BLOCK, D_MODEL), F32),
                            jax.ShapeDtypeStruct((N_META, D_MODEL), F32), jax.ShapeDtypeStruct((1, D_MODEL), F32)],
                 sem=("arbitrary",))(h, du, dres, g_pre)


GROUP_W = SSM_INNER // SSM_GROUPS


def _gated_norm(y, z, g):
    t = y * _silu(z)
    return t * lax.rsqrt(jnp.mean(t * t, axis=-1, keepdims=True) + NORM_EPS) * g


def _gated_norm_groups(y, z, g):
    groups = [slice(k * GROUP_W, (k + 1) * GROUP_W) for k in range(SSM_GROUPS)]
    return jnp.concatenate([_gated_norm(y[:, s], z[:, s], g[:, s]) for s in groups], axis=1)


def _merge(ga, gs, ya, ys):
    return _sigmoid(ga) * ya + _sigmoid(gs) * ys


GATE_ATT_BLOCK = SEG["gate_att"][2] // D_MODEL
GATE_SSM_BLOCK = SEG["gate_ssm"][2] // D_MODEL


def _row_loss(out, g_post, x, target):
    diff = x + _rms(out, g_post) - target
    return 0.5 * jnp.sum(diff * diff) / D_MODEL


def tail(y_ssd, proj, a_att, x, target, woa, wos, wo, g_norm, g_post):
    nb = y_ssd.shape[0] // BLOCK
    rows = nb * BLOCK

    def body(y_ref, z_ref, ga_ref, gs_ref, a_ref, x_ref, t_ref, woa_ref, wos_ref, wo_ref, gn_ref, gp_ref,
             yn_ref, mg_ref, dout_ref, dya_ref, dys_ref, da_ref, dy_ref, dz_ref, dga_ref, dgs_ref, dres_ref,
             loss_ref, dgp_ref, dgn_ref):
        i = pl.program_id(0)
        yn, norm_vjp = jax.vjp(_gated_norm_groups, y_ref[...], z_ref[...], gn_ref[...])
        yn16 = yn.astype(BF16)
        y_ssm = jnp.dot(yn16, wos_ref[...], preferred_element_type=F32)
        y_att = jnp.dot(a_ref[...], woa_ref[...], preferred_element_type=F32)
        merged, merge_vjp = jax.vjp(_merge, ga_ref[...], gs_ref[...], y_att, y_ssm)
        merged16 = merged.astype(BF16)
        out = jnp.dot(merged16, wo_ref[...], preferred_element_type=F32)
        loss, loss_vjp = jax.vjp(_row_loss, out, gp_ref[...], x_ref[...], t_ref[...])
        counted = jnp.where(i > 0, 1.0, 0.0)
        dout, dgp, dres, _ = loss_vjp(counted)
        dout16 = dout.astype(BF16)
        dmerged = lax.dot_general(dout16, wo_ref[...], _NT, preferred_element_type=F32)
        dga, dgs, dya, dys = merge_vjp(dmerged)
        dya16, dys16 = dya.astype(BF16), dys.astype(BF16)
        dyn = lax.dot_general(dys16, wos_ref[...], _NT, preferred_element_type=F32)
        dy, dz, dgn = norm_vjp(dyn)

        yn_ref[...] = yn16
        mg_ref[...] = merged16
        dout_ref[...] = dout16
        dya_ref[...] = dya16
        dys_ref[...] = dys16
        da_ref[...] = lax.dot_general(dya16, woa_ref[...], _NT, preferred_element_type=F32)
        dy_ref[...] = dy
        dz_ref[...] = dz.astype(BF16)
        dga_ref[...] = dga.astype(BF16)
        dgs_ref[...] = dgs.astype(BF16)
        dres_ref[...] = dres

        @pl.when(i == 0)
        def _():
            loss_ref[...] = jnp.zeros_like(loss_ref)
            dgp_ref[...] = jnp.zeros_like(dgp_ref)
            dgn_ref[...] = jnp.zeros_like(dgn_ref)

        loss_ref[...] += loss * counted
        dgp_ref[...] += dgp
        dgn_ref[...] += dgn

    wide, narrow = _row_spec(SSM_INNER), _row_spec(D_MODEL)
    resident = pl.BlockSpec(memory_space=pltpu.VMEM)
    bf = lambda w: jax.ShapeDtypeStruct((rows, w), BF16)
    f32 = lambda w: jax.ShapeDtypeStruct((rows, w), F32)
    return _call(body, name="tail", grid=(nb,),
                 in_specs=[wide, wide, _row_spec(D_MODEL, GATE_ATT_BLOCK), _row_spec(D_MODEL, GATE_SSM_BLOCK), narrow,
                           _x_spec(), _x_spec(), resident, resident, resident, _full((1, SSM_INNER)),
                           _full((1, D_MODEL))],
                 out_specs=[wide, narrow, narrow, narrow, narrow, narrow, wide, wide, narrow, narrow, narrow,
                            _full((8, LANES)), _full((1, D_MODEL)), _full((1, SSM_INNER))],
                 out_shape=[bf(SSM_INNER), bf(D_MODEL), bf(D_MODEL), bf(D_MODEL), bf(D_MODEL), f32(D_MODEL),
                            f32(SSM_INNER), bf(SSM_INNER), bf(D_MODEL), bf(D_MODEL), f32(D_MODEL),
                            jax.ShapeDtypeStruct((8, LANES), F32), jax.ShapeDtypeStruct((1, D_MODEL), F32),
                            jax.ShapeDtypeStruct((1, SSM_INNER), F32)],
                 sem=("arbitrary",))(y_ssd, proj, proj, proj, a_att, x, target, woa, wos, wo, g_norm, g_post)


_NT = (((1,), (1,)), ((), ()))
ALIBI_SLOPES = tuple(2.0 ** (-8.0 * (h + 1) / ATT_Q_HEADS) for h in range(ATT_Q_HEADS))
KV_WIDTH = ATT_KV_HEADS * HEAD_DIM
Q_BLOCK = SEG["q"][2] // D_MODEL
Z_ATT_BLOCK = SEG["z_att"][2] // D_MODEL
K_BLOCK = SEG["k"][2] // KV_WIDTH
V_BLOCK = SEG["v"][2] // KV_WIDTH
META_ROW_BLOCK = PAD_ROWS // N_META


@jax.custom_vjp
def _swap_halves(x):
    return pltpu.roll(x, HEAD_DIM, 1)


_swap_halves.defvjp(lambda x: (pltpu.roll(x, HEAD_DIM, 1), None), lambda _, g: (pltpu.roll(g, HEAD_DIM, 1),))


def _both_halves(t, half):
    first = lax.broadcasted_iota(jnp.int32, t.shape, 1) < HEAD_DIM
    sw = _swap_halves(t)
    return jnp.where(first, t, sw) if half == 0 else jnp.where(first, sw, t)


def _attn_rows(q, z, kp, kc, vp, vc, km, vm, sinks, n):
    rows = ATT_GROUP * BLOCK
    i = lax.broadcasted_iota(jnp.int32, (rows, BLOCK), 0) & (BLOCK - 1)
    j = lax.broadcasted_iota(jnp.int32, (rows, BLOCK), 1)
    rel_c = (i - j).astype(F32)
    rel_p = rel_c + float(BLOCK)
    nv = jnp.zeros((rows, BLOCK), jnp.int32) + n
    ok_c = (i >= j) & (nv >= 1)
    ok_p = (j > i) & (nv >= 2)
    im = lax.broadcasted_iota(jnp.int32, (rows, N_META), 0) & (BLOCK - 1)
    jm = lax.broadcasted_iota(jnp.int32, (rows, N_META), 1)
    ok_m = ((jnp.zeros((rows, N_META), jnp.int32) + n) >= 1) | (im >= PAD_ROWS + jm)
    first = lax.broadcasted_iota(jnp.int32, (BLOCK, LANES), 1) < HEAD_DIM
    neg = -jnp.inf
    outs = []
    for kv in range(ATT_KV_HEADS):
        tile, half = divmod(kv, 2)
        lanes = slice(tile * LANES, (tile + 1) * LANES)
        kc2, kp2, km2 = (_both_halves(t[:, lanes], half).astype(BF16) for t in (kc, kp, km))
        vc2, vp2, vm2 = (_both_halves(t[:, lanes], half).astype(BF16) for t in (vc, vp, vm))
        qs, slope, sk = [], [], []
        for pair in range(ATT_GROUP // 2):
            c0 = (kv * ATT_GROUP + 2 * pair) * HEAD_DIM
            qp = q[:, c0:c0 + LANES] * HEAD_DIM ** -0.5
            qs += [jnp.where(first, qp, 0.0), jnp.where(first, 0.0, qp)]
        for g in range(ATT_GROUP):
            slope.append(jnp.full((BLOCK, 1), ALIBI_SLOPES[kv * ATT_GROUP + g], F32))
            sk.append(jnp.broadcast_to(sinks[kv * ATT_GROUP + g], (BLOCK, 1)))
        qs = jnp.concatenate(qs, axis=0).astype(BF16)
        slope = jnp.concatenate(slope, axis=0)
        sk = jnp.concatenate(sk, axis=0)
        sc = jnp.where(ok_c, lax.dot_general(qs, kc2, _NT, preferred_element_type=F32) - slope * rel_c, neg)
        sp = jnp.where(ok_p, lax.dot_general(qs, kp2, _NT, preferred_element_type=F32) - slope * rel_p, neg)
        sm = jnp.where(ok_m, lax.dot_general(qs, km2, _NT, preferred_element_type=F32), neg)
        mx = jnp.maximum(jnp.maximum(jnp.max(sc, axis=1, keepdims=True), jnp.max(sp, axis=1, keepdims=True)),
                         jnp.maximum(jnp.max(sm, axis=1, keepdims=True), sk))
        mx = lax.stop_gradient(mx)
        ec, ep, em, es = jnp.exp(sc - mx), jnp.exp(sp - mx), jnp.exp(sm - mx), jnp.exp(sk - mx)
        den = (es + jnp.sum(ec, axis=1, keepdims=True) + jnp.sum(ep, axis=1, keepdims=True)
               + jnp.sum(em, axis=1, keepdims=True))
        inv = 1.0 / den
        o = (jnp.dot((ec * inv).astype(BF16), vc2, preferred_element_type=F32)
             + jnp.dot((ep * inv).astype(BF16), vp2, preferred_element_type=F32)
             + jnp.dot((em * inv).astype(BF16), vm2, preferred_element_type=F32))
        for pair in range(ATT_GROUP // 2):
            r0 = 2 * pair * BLOCK
            outs.append(jnp.where(first, o[r0:r0 + BLOCK], o[r0 + BLOCK:r0 + 2 * BLOCK]))
    return jnp.concatenate(outs, axis=1) * _silu(z)


def _attn_specs(nb, steps_clamped):
    def blk(t):
        return jnp.minimum(t, nb - 1) if steps_clamped else t

    wide = lambda col: pl.BlockSpec((BLOCK, D_MODEL), lambda t: (blk(t), col))
    cur = lambda col: pl.BlockSpec((BLOCK, KV_WIDTH), lambda t: (blk(t), col))
    prev = lambda col: pl.BlockSpec((BLOCK, KV_WIDTH), lambda t: (jnp.maximum(blk(t) - 1, 0), col))
    meta = lambda col: pl.BlockSpec((N_META, KV_WIDTH), lambda t: (META_ROW_BLOCK, col))
    sinks = pl.BlockSpec((ATT_Q_HEADS, 1, 1), lambda t: (0, 0, 0))
    return [wide(Q_BLOCK), wide(Z_ATT_BLOCK), prev(K_BLOCK), cur(K_BLOCK), prev(V_BLOCK), cur(V_BLOCK),
            meta(K_BLOCK), meta(V_BLOCK), sinks]


def attn_fwd(proj, sinks):
    nb = proj.shape[0] // BLOCK

    def body(q_ref, z_ref, kp_ref, kc_ref, vp_ref, vc_ref, km_ref, vm_ref, sk_ref, o_ref):
        o_ref[...] = _attn_rows(q_ref[...], z_ref[...], kp_ref[...], kc_ref[...], vp_ref[...], vc_ref[...],
                                km_ref[...], vm_ref[...], tuple(sk_ref[h] for h in range(ATT_Q_HEADS)),
                                pl.program_id(0)).astype(BF16)

    return _call(body, name="attn_fwd", grid=(nb,), in_specs=_attn_specs(nb, False), out_specs=_row_spec(D_MODEL),
                 out_shape=jax.ShapeDtypeStruct((nb * BLOCK, D_MODEL), BF16), sem=("parallel",))(*([proj] * 8), sinks)


def attn_bwd(da, proj, sinks):
    nb = proj.shape[0] // BLOCK
    last = nb - 1
    wide = pl.BlockSpec((BLOCK, D_MODEL), lambda t: (jnp.minimum(t, last), 0))
    done = pl.BlockSpec((BLOCK, KV_WIDTH), lambda t: (jnp.maximum(t - 1, 0), 0))
    meta = _full((N_META, KV_WIDTH))
    par = _full((ATT_Q_HEADS, 1, 1))

    def body(da_ref, q_ref, z_ref, kp_ref, kc_ref, vp_ref, vc_ref, km_ref, vm_ref, sk_ref,
             dq_ref, dz_ref, dk_ref, dv_ref, dkm_ref, dvm_ref, dsk_ref, ck_ref, cv_ref):
        t = pl.program_id(0)

        @pl.when(t == 0)
        def _():
            ck_ref[...] = jnp.zeros_like(ck_ref)
            cv_ref[...] = jnp.zeros_like(cv_ref)
            dkm_ref[...] = jnp.zeros_like(dkm_ref)
            dvm_ref[...] = jnp.zeros_like(dvm_ref)
            dsk_ref[...] = jnp.zeros_like(dsk_ref)

        @pl.when(t < nb)
        def _():
            def f(q, z, kp, kc, vp, vc, km, vm, sk):
                return _attn_rows(q, z, kp, kc, vp, vc, km, vm, sk, t)

            _, vjp = jax.vjp(f, q_ref[...], z_ref[...], kp_ref[...], kc_ref[...], vp_ref[...], vc_ref[...],
                             km_ref[...], vm_ref[...], tuple(sk_ref[h] for h in range(ATT_Q_HEADS)))
            dq, dz, dkp, dkc, dvp, dvc, dkm, dvm, dsk = vjp(da_ref[...])
            dq_ref[...] = dq.astype(BF16)
            dz_ref[...] = dz.astype(BF16)
            for h in range(ATT_Q_HEADS):
                dsk_ref[h] += dsk[h]
            dk_ref[...] = ck_ref[...] + dkp
            dv_ref[...] = cv_ref[...] + dvp
            ck_ref[...] = dkc
            cv_ref[...] = dvc
            dkm_ref[...] += dkm
            dvm_ref[...] += dvm

        @pl.when(t == nb)
        def _():
            dk_ref[...] = ck_ref[...]
            dv_ref[...] = cv_ref[...]

    rows = nb * BLOCK
    return _call(body, name="attn_bwd", grid=(nb + 1,), in_specs=[wide] + _attn_specs(nb, True),
                 out_specs=[wide, wide, done, done, meta, meta, par],
                 out_shape=[jax.ShapeDtypeStruct((rows, D_MODEL), BF16), jax.ShapeDtypeStruct((rows, D_MODEL), BF16),
                            jax.ShapeDtypeStruct((rows, KV_WIDTH), F32), jax.ShapeDtypeStruct((rows, KV_WIDTH), F32),
                            jax.ShapeDtypeStruct((N_META, KV_WIDTH), F32), jax.ShapeDtypeStruct((N_META, KV_WIDTH), F32),
                            jax.ShapeDtypeStruct(sinks.shape, F32)],
                 scratch=[pltpu.VMEM((BLOCK, KV_WIDTH), F32), pltpu.VMEM((BLOCK, KV_WIDTH), F32)],
                 sem=("arbitrary",))(da, *([proj] * 8), sinks)


XBC_BLOCK0 = SEG["xbc"][2] // D_MODEL
CONV_COL_BLOCKS = CONV_DIM // D_MODEL
DT_TILE = SEG["dt"][2] // LANES


HALO = 8
HALOS_PER_BLOCK = BLOCK // HALO


def _shift_rows(cur, before, j):
    if j == 0:
        return cur
    n = cur.shape[0]
    row = lax.broadcasted_iota(jnp.int32, cur.shape, 0)
    head = pltpu.roll(before, j, 0)
    if n > HALO:
        head = jnp.concatenate([head, jnp.zeros((n - HALO, cur.shape[1]), cur.dtype)], axis=0)
    return jnp.where(row >= j, pltpu.roll(cur, j, 0), head)


def _conv_pre(cur, before, w_ref, b_ref):
    pre = b_ref[...] + w_ref[CONV_WIDTH - 1:CONV_WIDTH, :] * cur
    for k in range(CONV_WIDTH - 1):
        pre = pre + w_ref[k:k + 1, :] * _shift_rows(cur, before, CONV_WIDTH - 1 - k)
    return pre


def _conv_specs(nb, col0=0):
    first = XBC_BLOCK0 + col0
    cur = pl.BlockSpec((BLOCK, D_MODEL), lambda j, i: (i, first + j))
    before = pl.BlockSpec((HALO, D_MODEL), lambda j, i: (jnp.maximum(i * HALOS_PER_BLOCK - 1, 0), first + j))
    after = pl.BlockSpec((HALO, D_MODEL), lambda j, i: (jnp.minimum(i + 1, nb - 1) * HALOS_PER_BLOCK, first + j))
    return cur, before, after


def _valid_rows(i):
    row = lax.broadcasted_iota(jnp.int32, (BLOCK, D_MODEL), 0)
    return jnp.maximum((row >= PAD_ROWS).astype(F32), jnp.where(i > 0, 1.0, 0.0))


def conv_fwd(proj, conv_w, conv_b):
    nb = proj.shape[0] // BLOCK
    cur, before, _ = _conv_specs(nb)

    def body(c_ref, p_ref, w_ref, b_ref, o_ref):
        i = pl.program_id(1)
        pre = _conv_pre(c_ref[...], p_ref[...] * jnp.where(i > 0, 1.0, 0.0), w_ref, b_ref)
        o_ref[...] = _silu(pre) * _valid_rows(i)

    return _call(body, name="conv_fwd", grid=(CONV_COL_BLOCKS, nb),
                 in_specs=[cur, before, pl.BlockSpec((CONV_WIDTH, D_MODEL), lambda j, i: (0, j)),
                           pl.BlockSpec((1, D_MODEL), lambda j, i: (0, j))],
                 out_specs=pl.BlockSpec((BLOCK, D_MODEL), lambda j, i: (i, j)),
                 out_shape=jax.ShapeDtypeStruct((nb * BLOCK, CONV_DIM), F32),
                 sem=("parallel", "parallel"))(proj, proj, conv_w, conv_b)


def conv_bwd(name, dparts, col0, proj, conv_w, conv_b):
    nb = proj.shape[0] // BLOCK
    last = nb - 1
    ncol = sum(d.shape[1] for d in dparts) // D_MODEL
    np_ = len(dparts)
    cur, before, after = _conv_specs(nb, col0)
    dcur = [pl.BlockSpec((BLOCK, d.shape[1] // ncol), lambda j, i: (i, j)) for d in dparts]
    dafter = [pl.BlockSpec((HALO, d.shape[1] // ncol), lambda j, i: (jnp.minimum(i + 1, last) * HALOS_PER_BLOCK, j))
              for d in dparts]
    out_cur = pl.BlockSpec((BLOCK, D_MODEL), lambda j, i: (i, j))
    wspec = pl.BlockSpec((CONV_WIDTH, D_MODEL), lambda j, i: (0, col0 + j))
    bspec = pl.BlockSpec((1, D_MODEL), lambda j, i: (0, col0 + j))
    wout = pl.BlockSpec((CONV_WIDTH, D_MODEL), lambda j, i: (0, j))
    bout = pl.BlockSpec((1, D_MODEL), lambda j, i: (0, j))

    def body(*refs):
        dc_refs, da_refs = refs[:np_], refs[np_:2 * np_]
        c_ref, p_ref, a_ref, w_ref, b_ref, du_ref, dw_ref, db_ref = refs[2 * np_:]
        i = pl.program_id(1)
        row = lax.broadcasted_iota(jnp.int32, (BLOCK, D_MODEL), 0)
        curv = c_ref[...]
        beforev = p_ref[...] * jnp.where(i > 0, 1.0, 0.0)
        side_by_side = lambda rs: rs[0][...] if np_ == 1 else jnp.concatenate([r[...] for r in rs], axis=1)

        def dpre_of(pre, d):
            s = _sigmoid(pre)
            return d * (s * (1.0 + pre * (1.0 - s)))

        dp_c = dpre_of(_conv_pre(curv, beforev, w_ref, b_ref), side_by_side(dc_refs) * _valid_rows(i))
        dp_a = dpre_of(_conv_pre(a_ref[...], curv[BLOCK - HALO:], w_ref, b_ref),
                       side_by_side(da_refs) * jnp.where(i < last, 1.0, 0.0))
        du = w_ref[CONV_WIDTH - 1:CONV_WIDTH, :] * dp_c
        for j in range(1, CONV_WIDTH):
            tail = jnp.concatenate([jnp.zeros((BLOCK - HALO, D_MODEL), F32), pltpu.roll(dp_a, HALO - j, 0)], axis=0)
            up = jnp.where(row < BLOCK - j, pltpu.roll(dp_c, BLOCK - j, 0), tail)
            du = du + w_ref[CONV_WIDTH - 1 - j:CONV_WIDTH - j, :] * up
        du_ref[...] = du.astype(BF16)

        @pl.when(i == 0)
        def _():
            dw_ref[...] = jnp.zeros_like(dw_ref)
            db_ref[...] = jnp.zeros_like(db_ref)

        for k in range(CONV_WIDTH):
            dw_ref[k:k + 1, :] += jnp.sum(dp_c * _shift_rows(curv, beforev, CONV_WIDTH - 1 - k), axis=0, keepdims=True)
        db_ref[...] += jnp.sum(dp_c, axis=0, keepdims=True)

    width = ncol * D_MODEL
    return _call(body, name=name, grid=(ncol, nb),
                 in_specs=dcur + dafter + [cur, before, after, wspec, bspec], out_specs=[out_cur, wout, bout],
                 out_shape=[jax.ShapeDtypeStruct((nb * BLOCK, width), BF16),
                            jax.ShapeDtypeStruct((CONV_WIDTH, width), F32), jax.ShapeDtypeStruct((1, width), F32)],
                 sem=("parallel", "arbitrary"))(*dparts, *dparts, proj, proj, proj, conv_w, conv_b)


def _head_expand():
    e = np.zeros((LANES, SSM_INNER), np.float32)
    for h in range(SSM_HEADS):
        e[h, h * HEAD_DIM:(h + 1) * HEAD_DIM] = 1.0
    return jnp.asarray(e, dtype=BF16)


def _softplus(x):
    return jnp.maximum(x, 0.0) + jnp.log(1.0 + jnp.exp(-jnp.abs(x)))


def _bf16_parts(x):
    hi = x.astype(BF16)
    rest = x - hi.astype(F32)
    mid = rest.astype(BF16)
    return hi, mid, (rest - mid.astype(F32)).astype(BF16)


@jax.custom_vjp
def _times_01(x, m):
    return sum(jnp.dot(p, m, preferred_element_type=F32) for p in _bf16_parts(x))


def _times_01_bwd(m, g):
    return sum(lax.dot_general(p, m, _NT, preferred_element_type=F32) for p in _bf16_parts(g)), jnp.zeros_like(m)


_times_01.defvjp(lambda x, m: (_times_01(x, m), m), _times_01_bwd)


def _causal_ones():
    l = lax.broadcasted_iota(jnp.int32, (BLOCK, BLOCK), 0)
    s = lax.broadcasted_iota(jnp.int32, (BLOCK, BLOCK), 1)
    return (l >= s).astype(BF16)


@jax.custom_vjp
def _cumsum_rows(a):
    return sum(jnp.dot(_causal_ones(), p, preferred_element_type=F32) for p in _bf16_parts(a))


def _cumsum_rows_bwd(_, g):
    tn = (((0,), (0,)), ((), ()))
    return (sum(lax.dot_general(_causal_ones(), p, tn, preferred_element_type=F32) for p in _bf16_parts(g)),)


_cumsum_rows.defvjp(lambda a: (_cumsum_rows(a), None), _cumsum_rows_bwd)


def _ssd_group(xs, dt_tile, expand, bias, alog, dsk, bg, cg, state):
    l = lax.broadcasted_iota(jnp.int32, (BLOCK, BLOCK), 0)
    s = lax.broadcasted_iota(jnp.int32, (BLOCK, BLOCK), 1)
    causal = l >= s
    first_head = s < HEAD_DIM
    dt = _softplus(dt_tile + bias)
    a = dt * (-jnp.exp(alog))
    one_row = lambda v: jnp.broadcast_to(v, (HALO, LANES))
    per_lane = _times_01(jnp.concatenate([dt, _cumsum_rows(a), one_row(jnp.sum(a, axis=0, keepdims=True)),
                                          one_row(dsk)], axis=0), expand)
    dtx, cs = per_lane[0:BLOCK], per_lane[BLOCK:2 * BLOCK]
    tot, dsk = per_lane[2 * BLOCK:2 * BLOCK + 1], per_lane[2 * BLOCK + HALO:2 * BLOCK + HALO + 1]
    bb, cb16 = bg.astype(BF16), cg.astype(BF16)
    cb = lax.dot_general(cb16, bb, _NT, preferred_element_type=F32)
    xr = xs * dtx
    y_diag = []
    for p in range(GROUP_W // LANES):
        lanes = slice(p * LANES, (p + 1) * LANES)
        c_pair = cs[:, lanes]
        c_swap = _swap_halves(c_pair)
        m = []
        for c_head in (jnp.where(first_head, c_pair, c_swap), jnp.where(first_head, c_swap, c_pair)):
            m.append(cb * jnp.exp(jnp.where(causal, c_head - c_head.T, -jnp.inf)))
        x_pair = xr[:, lanes]
        x_diag = jnp.concatenate([jnp.where(first_head, x_pair, 0.0), jnp.where(first_head, 0.0, x_pair)], axis=0)
        y_diag.append(jnp.dot(jnp.concatenate(m, axis=1).astype(BF16), x_diag.astype(BF16),
                              preferred_element_type=F32))
    st = lax.dot_general(bb, (xr * jnp.exp(tot - cs)).astype(BF16), (((0,), (0,)), ((), ())),
                         preferred_element_type=F32)
    new_state = state * jnp.exp(tot) + st
    y_off = jnp.dot(cb16, state.astype(BF16), preferred_element_type=F32) * jnp.exp(cs)
    return jnp.concatenate(y_diag, axis=1) + y_off + dsk * xs, new_state


BC_WIDTH = SSM_GROUPS * SSM_STATE


def _ssd_specs(chunk):
    xs = pl.BlockSpec((BLOCK, SSM_INNER), lambda c: (chunk(c), 0))
    dt = pl.BlockSpec((BLOCK, LANES), lambda c: (chunk(c), DT_TILE))
    expand = _full((LANES, SSM_INNER))
    b = pl.BlockSpec((BLOCK, BC_WIDTH), lambda c: (chunk(c), SSM_INNER // BC_WIDTH))
    cc = pl.BlockSpec((BLOCK, BC_WIDTH), lambda c: (chunk(c), SSM_INNER // BC_WIDTH + 1))
    par = _full((1, LANES))
    state = pl.BlockSpec((None, SSM_STATE, SSM_INNER), lambda c: (chunk(c), 0, 0))
    return xs, dt, expand, b, cc, par, state


def _group_lanes(g):
    return slice(g * GROUP_W, (g + 1) * GROUP_W), slice(g * SSM_STATE, (g + 1) * SSM_STATE)


def ssd_fwd(xbc, proj, expand, bias, alog, dsk):
    nb = xbc.shape[0] // BLOCK
    xs, dt, ex, b, cc, par, state = _ssd_specs(lambda c: c)

    def body(x_ref, dt_ref, e_ref, bi_ref, al_ref, dk_ref, b_ref, c_ref, y_ref, sp_ref, st_ref):
        @pl.when(pl.program_id(0) == 0)
        def _():
            st_ref[...] = jnp.zeros_like(st_ref)

        for g in range(SSM_GROUPS):
            wide, tile = _group_lanes(g)
            entering = st_ref[:, wide]
            sp_ref[:, wide] = entering
            y_ref[:, wide], st_ref[:, wide] = _ssd_group(
                x_ref[:, wide], dt_ref[...], e_ref[:, wide], bi_ref[...], al_ref[...], dk_ref[...], b_ref[:, tile],
                c_ref[:, tile], entering)

    return _call(body, name="ssd_fwd", grid=(nb,), in_specs=[xs, dt, ex, par, par, par, b, cc],
                 out_specs=[xs, state],
                 out_shape=[jax.ShapeDtypeStruct((nb * BLOCK, SSM_INNER), F32),
                            jax.ShapeDtypeStruct((nb, SSM_STATE, SSM_INNER), F32)],
                 scratch=[pltpu.VMEM((SSM_STATE, SSM_INNER), F32)],
                 sem=("arbitrary",))(xbc, proj, expand, bias, alog, dsk, xbc, xbc)


def ssd_bwd(dy, xbc, proj, expand, bias, alog, dsk, states, comm):
    nb = xbc.shape[0] // BLOCK
    last = nb - 1
    xs, dt, ex, b, cc, par, state = _ssd_specs(lambda c: last - c)
    tile = pl.BlockSpec((BLOCK, LANES), lambda c: (last - c, 0))
    nspec = pl.BlockSpec((BLOCK, BC_WIDTH), lambda c: (last - c, 0))

    def body(dy_ref, x_ref, dt_ref, e_ref, bi_ref, al_ref, dk_ref, b_ref, c_ref, sp_ref,
             dx_ref, ddt_ref, db_ref, dc_ref, dbi_ref, dal_ref, ddk_ref, ds_ref):
        @pl.when(pl.program_id(0) == 0)
        def _():
            ds_ref[...] = jnp.zeros_like(ds_ref)
            dbi_ref[...] = jnp.zeros_like(dbi_ref)
            dal_ref[...] = jnp.zeros_like(dal_ref)
            ddk_ref[...] = jnp.zeros_like(ddk_ref)

        ddt, dbi, dal, ddk = 0.0, 0.0, 0.0, 0.0
        for g in range(SSM_GROUPS):
            wide, tile_lanes = _group_lanes(g)
            expand_rows = e_ref[:, wide]

            def f(xs, dt_tile, bias, alog, dsk, bg, cg, state):
                return _ssd_group(xs, dt_tile, expand_rows, bias, alog, dsk, bg, cg, state)

            _, vjp = jax.vjp(f, x_ref[:, wide], dt_ref[...], bi_ref[...], al_ref[...], dk_ref[...], b_ref[:, tile_lanes],
                             c_ref[:, tile_lanes], sp_ref[:, wide])
            (dx_ref[:, wide], ddt_g, dbi_g, dal_g, ddk_g, db_ref[:, tile_lanes], dc_ref[:, tile_lanes],
             ds_ref[:, wide]) = vjp((dy_ref[:, wide], ds_ref[:, wide]))
            ddt, dbi, dal, ddk = ddt + ddt_g, dbi + dbi_g, dal + dal_g, ddk + ddk_g
        ddt_ref[...] = ddt.astype(BF16)
        dbi_ref[...] += dbi
        dal_ref[...] += dal
        ddk_ref[...] += ddk

    def at():
        c = pl.program_id(0)
        return c == 0, c == 0, c == last

    par_shape = jax.ShapeDtypeStruct((1, LANES), F32)
    return _call_with_comm(
        body, comm, at, (dy, xbc, proj, expand, bias, alog, dsk, xbc, xbc, states), name="ssd_bwd",
        grid=(nb,), in_specs=[xs, xs, dt, ex, par, par, par, b, cc, state],
        out_specs=[xs, tile, nspec, nspec, par, par, par],
        out_shape=[jax.ShapeDtypeStruct((nb * BLOCK, SSM_INNER), F32), jax.ShapeDtypeStruct((nb * BLOCK, LANES), BF16),
                   jax.ShapeDtypeStruct((nb * BLOCK, BC_WIDTH), F32), jax.ShapeDtypeStruct((nb * BLOCK, BC_WIDTH), F32),
                   par_shape, par_shape, par_shape],
        scratch=[pltpu.VMEM((SSM_STATE, SSM_INNER), F32)])


SLAB_ROWS = 24
SLAB_META_ROW = 8


SLAB_LOSS_ROW = 7


def pack_small(dcw, dcb, dgpre, dgpost, dbias, dalog, ddsk, dsinks, dgn, dmeta, loss_tile):
    def body(cw, cb, gpre, gpost, dtb, al, dk, sk, gn, meta, loss, o_ref):
        o_ref[...] = jnp.zeros_like(o_ref)
        o_ref[SLAB_LOSS_ROW:SLAB_LOSS_ROW + 1, 0:LANES] = loss[0:1, :]
        o_ref[0:CONV_WIDTH, :] = cw[...]
        o_ref[4:5, :] = cb[...]
        o_ref[5:6, 0:1024] = gpre[...]
        o_ref[5:6, 1024:2048] = gpost[...]
        o_ref[5:6, 2048:2176] = dtb[...]
        o_ref[5:6, 2176:2304] = al[...]
        o_ref[5:6, 2304:2432] = dk[...]
        o_ref[5:6, 2432:2560] = sk[...]
        o_ref[6:7, 0:SSM_INNER] = gn[...]
        o_ref[SLAB_META_ROW:SLAB_META_ROW + N_META, 0:D_MODEL] = meta[...]

    args = (dcw, dcb, dgpre, dgpost, dbias, dalog, ddsk, dsinks, dgn, dmeta, loss_tile)
    return _call(body, name="pack_small", in_specs=[_full(a.shape) for a in args],
                 out_specs=_full((SLAB_ROWS, CONV_DIM)), out_shape=jax.ShapeDtypeStruct((SLAB_ROWS, CONV_DIM), F32))(*args)


def _lane_tile(v):
    return jnp.pad(v, ((0, 0), (0, LANES - v.shape[1])))


def kernel(x, meta_tokens, g_pre, w_in, conv_w, conv_b, dt_bias, a_log, d_skip, attn_sinks, g_ssm_norm, w_out_att, w_out_ssm, w_out, g_post, loss_target, m_meta_tokens, m_g_pre, m_w_in, m_conv_w, m_conv_b, m_dt_bias, m_a_log, m_d_skip, m_attn_sinks, m_g_ssm_norm, m_w_out_att, m_w_out_ssm, m_w_out, m_g_post, v_meta_tokens, v_g_pre, v_w_in, v_conv_w, v_conv_b, v_dt_bias, v_a_log, v_d_skip, v_attn_sinks, v_g_ssm_norm, v_w_out_att, v_w_out_ssm, v_w_out, v_g_post):
    chip = _chip_index()

    conv_w_rows = jnp.pad(conv_w[0], ((0, 2 * 8 - CONV_WIDTH), (0, 0)))
    w_in_t, m_w_in_t, v_w_in_t = w_in[0].T, m_w_in[0].T, v_w_in[0].T
    gathered_w_in, g_conv_w, g_meta = run_comm("gather_w_in",
                                               TwoLevelGather([pack_w_in(w_in_t), conv_w_rows, meta_tokens]))
    w_all_t = unpack_w_in(gathered_w_in)
    cw_full = g_conv_w[:, :CONV_WIDTH].transpose(1, 0, 2).reshape(CONV_WIDTH, CONV_DIM)
    meta_full = g_meta.transpose(1, 0, 2).reshape(N_META, D_MODEL)

    h, u = prep(x, meta_full, g_pre)
    proj, g_w_out = project("in_proj", u, w_all_t, TwoLevelGather(
        [w_out_att[0].astype(BF16), w_out_ssm[0].astype(BF16), w_out[0].astype(BF16)]))
    woa = g_w_out[0].reshape(D_MODEL, D_MODEL)
    wos = g_w_out[1].reshape(SSM_INNER, D_MODEL)
    wo = g_w_out[2].reshape(D_MODEL, D_MODEL)

    sinks3 = attn_sinks.reshape(ATT_Q_HEADS, 1, 1)
    a_att = attn_fwd(proj, sinks3)

    xbc = conv_fwd(proj, cw_full, conv_b)
    expand = _head_expand()
    head_pars = (_lane_tile(dt_bias), _lane_tile(a_log), _lane_tile(d_skip))
    y_ssd, states = ssd_fwd(xbc, proj, expand, *head_pars)

    (yn, merged, dout, dy_att, dy_ssm, da_att, dy_ssd, dz_ssm, dga, dgs, dres, loss_tile, dg_post, dgn) = tail(
        y_ssd, proj, a_att, x, loss_target, woa, wos, wo, g_ssm_norm, g_post)

    dwo = mm_tn("out_proj_dw", merged, dout)
    dwoa = mm_tn("att_out_dw", a_att, dy_att)
    dwos = mm_tn("ssm_out_dw", yn, dy_ssm)
    dq, dz_att, dk, dv, dkmeta, dvmeta, dsinks3 = attn_bwd(da_att, proj, sinks3)
    dk = dk.at[PAD_ROWS:BLOCK].add(dkmeta).astype(BF16)
    dv = dv.at[PAD_ROWS:BLOCK].add(dvmeta).astype(BF16)

    def pieces(g):
        return g.reshape(4, 2, g.shape[0] // 8, g.shape[1])

    def to_owner(g):
        return (lambda ref, dev: ref.at[_chip_of(dev), dev[2]], (g.shape[0] // 8, g.shape[1]))

    (dxs, ddt_tile, dbg, dcg, dbias, dalog, ddsk), sent_w_out = ssd_bwd(
        dy_ssd, xbc, proj, expand, *head_pars, states,
        DirectExchange([pieces(dwoa), pieces(dwos), pieces(dwo)], [to_owner(dwoa), to_owner(dwos), to_owner(dwo)],
                       ALL_MASKS, "dev", 8))
    dxs_raw, dcw_xs, dcb_xs = conv_bwd("conv_bwd_x", [dxs], 0, proj, cw_full, conv_b)
    dbc_raw, dcw_bc, dcb_bc = conv_bwd("conv_bwd_bc", [dbg, dcg], SSM_INNER // D_MODEL, proj, cw_full, conv_b)
    dcw = jnp.concatenate([dcw_xs, dcw_bc], axis=1)
    dcb = jnp.concatenate([dcb_xs, dcb_bc], axis=1)

    narrow = jnp.concatenate([dk, dv, ddt_tile, jnp.zeros((dk.shape[0], N_ACT - N_ALIGNED), BF16)], axis=1)
    dproj = [dz_ssm, dxs_raw, dbc_raw, dq, dz_att, dga, dgs, narrow]
    dw_all_t = weight_grad_t("in_proj_dw", dproj, u)

    half_rows = PACK_W // 2
    partial = pack_grad_w_in(dw_all_t).reshape(4, 2, half_rows, D_MODEL)
    from_sibling, = run_comm("pair_grads", DirectExchange(
        [partial], [(lambda ref, dev: ref.at[pl.ds(0, 4), dev[2]], (4, half_rows, D_MODEL))], SIBLING_MASK, "core", 2,
        keep_own=False))
    chip_sum = sum_pair(partial, from_sibling)
    du, (sent_w_in,) = project_back("in_proj_dx", dproj, w_all_t, DirectExchange(
        [chip_sum], [(lambda ref, dev: ref.at[_chip_of(dev)], (half_rows, D_MODEL))], CHIP_MASKS, "chip", 4))
    grad_x, dmeta, dg_pre = prep_bwd(h, du, dres, g_pre)

    slab = pack_small(dcw, dcb, dg_pre, dg_post, dbias, dalog, ddsk, _lane_tile(dsinks3.reshape(1, ATT_Q_HEADS)), dgn,
                      dmeta, loss_tile)
    halves = [sum_slots("sum_" + nm, r)
              for nm, r in zip(("w_in", "w_out_att", "w_out_ssm", "w_out"), [sent_w_in] + list(sent_w_out))]
    shared = run_comm("share_grads", Both(DirectExchange(halves, [None] * 4, SIBLING_MASK, "core", 2),
                                          DirectExchange([slab], [None], ALL_MASKS, "dev", 8)))
    g_w_in_packed, g_woa, g_wos, g_wo = [f.reshape(2 * f.shape[1], f.shape[2]) for f in shared[:4]]
    small = sum_slots("sum_small", shared[4])
    loss = small[SLAB_LOSS_ROW, 0]

    g_w_in, d_w_in, nm_w_in, nv_w_in = [a.T for a in adamw_w_in(g_w_in_packed, w_in_t, m_w_in_t, v_w_in_t)]
    d_woa, nm_woa, nv_woa = adamw_rows("adamw_w_out_att", g_woa, w_out_att[0], m_w_out_att[0], v_w_out_att[0])
    d_wos, nm_wos, nv_wos = adamw_rows("adamw_w_out_ssm", g_wos, w_out_ssm[0], m_w_out_ssm[0], v_w_out_ssm[0])
    d_wo, nm_wo, nv_wo = adamw_rows("adamw_w_out", g_wo, w_out[0], m_w_out[0], v_w_out[0])

    cw_cols = CONV_DIM // 4
    meta_cols = D_MODEL // 4
    g_small = {
        "meta_tokens": lax.dynamic_slice(small, (SLAB_META_ROW, chip * meta_cols), (N_META, meta_cols)),
        "g_pre": small[5:6, 0:1024],
        "conv_w": lax.dynamic_slice(small, (0, chip * cw_cols), (CONV_WIDTH, cw_cols)),
        "conv_b": small[4:5, :],
        "dt_bias": small[5:6, 2048:2048 + SSM_HEADS],
        "a_log": small[5:6, 2176:2176 + SSM_HEADS],
        "d_skip": small[5:6, 2304:2304 + SSM_HEADS],
        "attn_sinks": small[5:6, 2432:2432 + ATT_Q_HEADS],
        "g_ssm_norm": small[6:7, 0:SSM_INNER],
        "g_post": small[5:6, 1024:2048],
    }
    names = list(g_small)
    w_small = dict(meta_tokens=meta_tokens, g_pre=g_pre, conv_w=conv_w[0], conv_b=conv_b, dt_bias=dt_bias, a_log=a_log,
                   d_skip=d_skip, attn_sinks=attn_sinks, g_ssm_norm=g_ssm_norm, g_post=g_post)
    m_small = dict(meta_tokens=m_meta_tokens, g_pre=m_g_pre, conv_w=m_conv_w[0], conv_b=m_conv_b, dt_bias=m_dt_bias,
                   a_log=m_a_log, d_skip=m_d_skip, attn_sinks=m_attn_sinks, g_ssm_norm=m_g_ssm_norm, g_post=m_g_post)
    v_small = dict(meta_tokens=v_meta_tokens, g_pre=v_g_pre, conv_w=v_conv_w[0], conv_b=v_conv_b, dt_bias=v_dt_bias,
                   a_log=v_a_log, d_skip=v_d_skip, attn_sinks=v_attn_sinks, g_ssm_norm=v_g_ssm_norm, g_post=v_g_post)
    upd = dict(zip(names, adamw_small([g_small[k] for k in names], [w_small[k] for k in names],
                                      [m_small[k] for k in names], [v_small[k] for k in names])))

    lead = {"conv_w"}

    def shaped(name, a):
        return a[None] if name in lead else a

    grads = dict(g_small, w_in=g_w_in, w_out_att=g_woa, w_out_ssm=g_wos, w_out=g_wo)
    deltas = dict({k: upd[k][0] for k in names}, w_in=d_w_in, w_out_att=d_woa, w_out_ssm=d_wos, w_out=d_wo)
    new_m = dict({k: upd[k][1] for k in names}, w_in=nm_w_in, w_out_att=nm_woa, w_out_ssm=nm_wos, w_out=nm_wo)
    new_v = dict({k: upd[k][2] for k in names}, w_in=nv_w_in, w_out_att=nv_woa, w_out_ssm=nv_wos, w_out=nv_wo)
    lead |= {"w_in", "w_out_att", "w_out_ssm", "w_out"}
    order = ["meta_tokens", "g_pre", "w_in", "conv_w", "conv_b", "dt_bias", "a_log", "d_skip", "attn_sinks",
             "g_ssm_norm", "w_out_att", "w_out_ssm", "w_out", "g_post"]
    outs = [loss, grad_x]
    for group in (grads, deltas, new_m, new_v):
        outs += [shaped(k, group[k]) for k in order]
    return tuple(outs)
```

```python
import functools

import numpy as np
import jax
import jax.numpy as jnp
from jax import lax
from jax.experimental import pallas as pl
from jax.experimental.pallas import tpu as pltpu

F32 = jnp.float32
BF16 = jnp.bfloat16
HI = lax.Precision.HIGHEST

D_MODEL = 1024
N_META = 16
BLOCK = 128
PAD_ROWS = BLOCK - N_META
NORM_EPS = 1e-6
HEAD_DIM = 64
ATT_Q_HEADS = 16
ATT_KV_HEADS = 4
ATT_GROUP = 4
SSM_INNER = 2048
SSM_HEADS = 32
SSM_GROUPS = 4
SSM_HEADS_PER_GROUP = 8
SSM_STATE = 128
CONV_WIDTH = 4
CONV_DIM = 3072
LANES = 128

ADAM_LR = 0.001
ADAM_B1 = 0.9
ADAM_B2 = 0.999
ADAM_EPS = 1e-08
ADAM_WD = 0.01
ADAM_STEP = 10

VMEM_LIMIT = 48 * 1024 * 1024

SHARD_W = 2440
PACK_W = 2560
SHARD_STRIDE = 2432
N_ALIGNED = 9856
N_ACT = 10240
SEG = {
    "q": (0, 1024, 5120), "k": (1024, 256, 9216), "v": (1280, 256, 9472), "z_att": (1536, 1024, 6144),
    "z_ssm": (2560, 2048, 0), "xbc": (4608, 3072, 2048), "dt": (7680, 128, 9728),
    "gate_att": (7808, 1024, 7168), "gate_ssm": (8832, 1024, 8192),
}
DT_STORED_START = 7680
DT_PAD = LANES - SSM_HEADS


def _act_col(aligned_col):
    for a0, w, p0 in SEG.values():
        if a0 <= aligned_col < a0 + w:
            return p0 + aligned_col - a0
    raise ValueError(aligned_col)


def _call(body, *, name, out_shape, in_specs, out_specs, grid=(), scratch=(), sem=None, aliases=None):
    return pl.pallas_call(
        body, out_shape=out_shape, grid=grid, in_specs=in_specs, out_specs=out_specs, scratch_shapes=list(scratch),
        name=name, input_output_aliases=aliases or {},
        compiler_params=pltpu.CompilerParams(dimension_semantics=sem, vmem_limit_bytes=VMEM_LIMIT))


def _full(shape):
    n = len(shape)
    return pl.BlockSpec(shape, lambda *_: (0,) * n)


def _chip_index():
    return lax.axis_index("x") * 2 + lax.axis_index("y")


_sigmoid = jax.nn.sigmoid


def _silu(z):
    return z * _sigmoid(z)


def _rms(x, g):
    return x * lax.rsqrt(jnp.mean(x * x, axis=-1, keepdims=True) + NORM_EPS) * g


def _peer(mask):
    x, y, c = lax.axis_index("x"), lax.axis_index("y"), lax.axis_index("c")
    return ((1 - x) if mask & 4 else x, (1 - y) if mask & 2 else y, (1 - c) if mask & 1 else c)


def _me():
    return lax.axis_index("x"), lax.axis_index("y"), lax.axis_index("c")


def _chip_of(dev):
    return 2 * dev[0] + dev[1]


CHIP_MASKS = (4, 2, 6)
ALL_MASKS = (1, 2, 3, 4, 5, 6, 7)
SIBLING_MASK = (1,)


def _remote(src, dst, send_sem, recv_sem, dev):
    return pltpu.make_async_remote_copy(src_ref=src, dst_ref=dst, send_sem=send_sem, recv_sem=recv_sem,
                                        device_id=dev, device_id_type=pl.DeviceIdType.MESH)


class _StagedCopy:
    def __init__(self, src, stage, dst, load_sem, store_sem):
        self.load = pltpu.make_async_copy(src, stage, load_sem)
        self.store = pltpu.make_async_copy(stage, dst, store_sem)

    def start(self):
        self.load.start()
        self.load.wait()
        self.store.start()

    def wait(self):
        self.store.wait()


class DirectExchange:
    def __init__(self, arrays, pieces, masks, slot_kind, nslots, keep_own=True):
        self.arrays, self.pieces, self.masks, self.slot_kind = list(arrays), list(pieces), masks, slot_kind
        self.keep_own = keep_own
        n, nk = len(arrays), len(masks)
        shapes = [a.shape if p is None else p[1] for a, p in zip(arrays, pieces)]
        self.out_shape = [jax.ShapeDtypeStruct((nslots,) + tuple(s), a.dtype) for s, a in zip(shapes, arrays)]
        self.scratch = [pltpu.SemaphoreType.DMA((n * nk,)), pltpu.SemaphoreType.DMA((n * nk,))]
        if keep_own:
            self.scratch += [pltpu.SemaphoreType.DMA((2 * n,))] + [pltpu.VMEM(s, a.dtype) for s, a in zip(shapes, arrays)]
        self.has_mid = False

    def _copies(self, ins, outs, scratch):
        send_sems, recv_sems = scratch[:2]
        me = _me()
        slot = {"chip": _chip_of(me), "dev": 4 * me[0] + 2 * me[1] + me[2], "core": me[2]}[self.slot_kind]
        nk = len(self.masks)

        def piece(a, dev):
            return ins[a] if self.pieces[a] is None else self.pieces[a][0](ins[a], dev)

        local = []
        if self.keep_own:
            local_sems, stages = scratch[2], scratch[3:]
            local = [_StagedCopy(piece(a, me), stages[a], outs[a].at[slot], local_sems.at[2 * a], local_sems.at[2 * a + 1])
                     for a in range(len(ins))]
        remote = []
        for a in range(len(ins)):
            for ki, mask in enumerate(self.masks):
                dev = _peer(mask)
                remote.append(_remote(piece(a, dev), outs[a].at[slot], send_sems.at[a * nk + ki],
                                      recv_sems.at[a * nk + ki], dev))
        return local, remote

    def start(self, ins, outs, scratch):
        local, remote = self._copies(ins, outs, scratch)
        for cp in remote + local:
            cp.start()

    def finish(self, ins, outs, scratch):
        local, remote = self._copies(ins, outs, scratch)
        for cp in remote + local:
            cp.wait()


class TwoLevelGather:
    def __init__(self, arrays):
        self.arrays = list(arrays)
        n, nk = len(arrays), len(CHIP_MASKS)
        self.out_shape = [jax.ShapeDtypeStruct((4,) + a.shape, a.dtype) for a in arrays]
        self.scratch = ([pltpu.SemaphoreType.DMA((n * nk,)) for _ in range(4)] + [pltpu.SemaphoreType.DMA((2 * n,))]
                        + [pltpu.VMEM(a.shape, a.dtype) for a in arrays])
        self.has_mid = True

    def _copies(self, ins, outs, scratch):
        ici_send, ici_recv, fwd_send, fwd_recv, local_sems = scratch[:5]
        stages = scratch[5:]
        me = _me()
        sibling = _peer(1)
        nk = len(CHIP_MASKS)
        local, ici, fwd = [], [], []
        for a in range(len(ins)):
            half = ins[a].shape[0] // 2
            mine = pl.ds(me[2] * half, half)
            local.append(_StagedCopy(ins[a], stages[a], outs[a].at[_chip_of(me)], local_sems.at[2 * a],
                                     local_sems.at[2 * a + 1]))
            for ki, mask in enumerate(CHIP_MASKS):
                dev = _peer(mask)
                k = a * nk + ki
                ici.append(_remote(ins[a].at[mine], outs[a].at[_chip_of(me), mine], ici_send.at[k], ici_recv.at[k], dev))
                arrived = outs[a].at[_chip_of(dev), mine]
                fwd.append(_remote(arrived, arrived, fwd_send.at[k], fwd_recv.at[k], sibling))
        return local, ici, fwd

    def start(self, ins, outs, scratch):
        local, ici, _ = self._copies(ins, outs, scratch)
        for cp in ici + local:
            cp.start()

    def mid(self, ins, outs, scratch):
        _, ici, fwd = self._copies(ins, outs, scratch)
        for arrival, onward in zip(ici, fwd):
            arrival.wait_recv()
            onward.start()

    def finish(self, ins, outs, scratch):
        local, ici, fwd = self._copies(ins, outs, scratch)
        for cp in fwd:
            cp.wait_recv()
        for cp in ici + fwd:
            cp.wait_send()
        for cp in local:
            cp.wait()


class Both:
    def __init__(self, a, b):
        self.a, self.b = a, b
        self.arrays, self.out_shape = a.arrays + b.arrays, a.out_shape + b.out_shape
        self.scratch = a.scratch + b.scratch
        self.has_mid = False
        assert not (a.has_mid or b.has_mid)

    def _parts(self, ins, outs, sems):
        na, sa = len(self.a.arrays), len(self.a.scratch)
        return (ins[:na], outs[:na], sems[:sa]), (ins[na:], outs[na:], sems[sa:])

    def start(self, ins, outs, sems):
        pa, pb = self._parts(ins, outs, sems)
        self.a.start(*pa)
        self.b.start(*pb)

    def finish(self, ins, outs, sems):
        pa, pb = self._parts(ins, outs, sems)
        self.a.finish(*pa)
        self.b.finish(*pb)


_ANY = pl.BlockSpec(memory_space=pl.ANY)


def run_comm(name, comm):
    n = len(comm.arrays)

    def body(*refs):
        ins, outs, sems = refs[:n], refs[n:2 * n], refs[2 * n:]
        comm.start(ins, outs, sems)
        if comm.has_mid:
            comm.mid(ins, outs, sems)
        comm.finish(ins, outs, sems)

    return pl.pallas_call(body, name=name, out_shape=comm.out_shape, in_specs=[_ANY] * n, out_specs=[_ANY] * n,
                          scratch_shapes=comm.scratch,
                          compiler_params=pltpu.CompilerParams(vmem_limit_bytes=VMEM_LIMIT))(*comm.arrays)


_HBM = pl.BlockSpec(memory_space=pltpu.HBM)
_SEM = pl.BlockSpec(memory_space=pltpu.SEMAPHORE)
_SIDE_EFFECT = pltpu.SideEffectType.DATAFLOW_SIDE_EFFECTING


def chip_exchange_start(pieces):
    shape = pieces.shape
    nk = len(CHIP_MASKS)

    def body(src_ref, land_ref, send_sems, recv_sems, src_thru, land_thru, token, stage, local_sems):
        me = _me()
        for ki, mask in enumerate(CHIP_MASKS):
            dev = _peer(mask)
            _remote(src_ref.at[_chip_of(dev)], land_ref.at[_chip_of(me)], send_sems.at[ki], recv_sems.at[ki], dev).start()
        own = _StagedCopy(src_ref.at[_chip_of(me)], stage, land_ref.at[_chip_of(me)], local_sems.at[0], local_sems.at[1])
        own.start()
        own.wait()
        token[...] = jnp.zeros_like(token)

    return pl.pallas_call(
        body, name="chip_exchange_start",
        out_shape=(pltpu.SemaphoreType.DMA((nk,)), pltpu.SemaphoreType.DMA((nk,)), pltpu.HBM(shape, pieces.dtype),
                   pltpu.HBM(shape, pieces.dtype), jax.ShapeDtypeStruct((8, LANES), F32)),
        in_specs=(_HBM, _HBM), out_specs=(_SEM, _SEM, _HBM, _HBM, pl.BlockSpec(memory_space=pltpu.VMEM)),
        input_output_aliases={0: 2, 1: 3},
        scratch_shapes=[pltpu.VMEM(shape[1:], pieces.dtype), pltpu.SemaphoreType.DMA((2,))],
        compiler_params=pltpu.CompilerParams(has_side_effects=_SIDE_EFFECT, vmem_limit_bytes=VMEM_LIMIT),
    )(pltpu.with_memory_space_constraint(pieces, pltpu.HBM),
      pltpu.with_memory_space_constraint(lax.empty(shape, pieces.dtype), pltpu.HBM))


def chip_exchange_wait(send_sems, recv_sems, src_thru, land_thru, after):
    nk = len(CHIP_MASKS)

    def body(src_ref, land_ref, send_sems, recv_sems, after_ref, src_dead, land_out):
        me = _me()
        for ki, mask in enumerate(CHIP_MASKS):
            dev = _peer(mask)
            copy = _remote(src_ref.at[_chip_of(dev)], land_ref.at[_chip_of(me)], send_sems.at[ki], recv_sems.at[ki], dev)
            copy.wait_send()
            copy.wait_recv()

    return pl.pallas_call(
        body, name="chip_exchange_wait",
        out_shape=(pltpu.HBM(src_thru.shape, src_thru.dtype), pltpu.HBM(land_thru.shape, land_thru.dtype)),
        in_specs=(_HBM, _HBM, _SEM, _SEM, pl.BlockSpec(memory_space=pl.ANY)), out_specs=(_HBM, _HBM),
        input_output_aliases={0: 0, 1: 1},
        compiler_params=pltpu.CompilerParams(has_side_effects=_SIDE_EFFECT),
    )(src_thru, land_thru, send_sems, recv_sems, after)[1]


def _call_with_comm(body, comm, steps, args, *, name, out_shape, in_specs, out_specs, grid, scratch=()):
    ni, no, ns, nc = len(in_specs), len(out_specs), len(scratch), len(comm.arrays)

    def full_body(*refs):
        ins, cins = refs[:ni], refs[ni:ni + nc]
        outs, couts = refs[ni + nc:ni + nc + no], refs[ni + nc + no:ni + 2 * nc + no]
        scr, csems = refs[ni + 2 * nc + no:ni + 2 * nc + no + ns], refs[ni + 2 * nc + no + ns:]
        first, middle, last = steps()
        pl.when(first)(lambda: comm.start(cins, couts, csems))
        if comm.has_mid:
            pl.when(middle)(lambda: comm.mid(cins, couts, csems))
        body(*ins, *outs, *scr)
        pl.when(last)(lambda: comm.finish(cins, couts, csems))

    res = pl.pallas_call(
        full_body, name=name, out_shape=list(out_shape) + comm.out_shape, grid=grid,
        in_specs=list(in_specs) + [_ANY] * nc, out_specs=list(out_specs) + [_ANY] * nc,
        scratch_shapes=list(scratch) + comm.scratch,
        compiler_params=pltpu.CompilerParams(dimension_semantics=("arbitrary",) * len(grid),
                                             vmem_limit_bytes=VMEM_LIMIT))(*args, *comm.arrays)
    return res[:no], res[no:]


def _shard_pieces(chip):
    if chip < 3:
        return [(0, SHARD_W, 8 * chip)]
    behind_dt = DT_STORED_START + SSM_HEADS - 3 * SHARD_W
    return [(0, behind_dt, 24), (behind_dt, SHARD_W - behind_dt, behind_dt + 24 + DT_PAD)]


W_IN_COLS = 256


def pack_w_in(wt):
    def body(w_ref, o_ref, pad_ref):
        chip = _chip_index()
        pad_ref[...] = jnp.zeros_like(pad_ref)
        for cv in range(4):
            @pl.when(chip == cv)
            def _():
                for src, n, dst in _shard_pieces(cv):
                    pad_ref[dst:dst + n, :] = w_ref[src:src + n, :]
        o_ref[...] = pad_ref[...].astype(BF16)

    return _call(body, name="pack_w_in", grid=(D_MODEL // W_IN_COLS,),
                 in_specs=[pl.BlockSpec((SHARD_W, W_IN_COLS), lambda i: (0, i))],
                 out_specs=pl.BlockSpec((PACK_W, W_IN_COLS), lambda i: (0, i)),
                 out_shape=jax.ShapeDtypeStruct((PACK_W, D_MODEL), BF16),
                 scratch=[pltpu.VMEM((PACK_W, W_IN_COLS), F32)], sem=("parallel",))(wt)


def _tile_runs():
    runs, fix = [], []
    for t in range(N_ALIGNED // LANES):
        s = min(t // 19, 3)
        j = t - 19 * s
        p = _act_col(t * LANES)
        if runs and runs[-1][1] == s and runs[-1][0] + runs[-1][3] == p and runs[-1][2] + runs[-1][3] == j * LANES:
            runs[-1][3] += LANES
        else:
            runs.append([p, s, j * LANES, LANES])
        if j == 0 and s > 0:
            fix.append((p, s - 1))
    return runs, fix


def unpack_w_in(bg):
    runs, fix = _tile_runs()

    def body(b_ref, o_ref):
        for p, s, j, w in runs:
            o_ref[p:p + w, :] = b_ref[s, j:j + w, :]
        for p, s in fix:
            o_ref[p:p + LANES, :] = o_ref[p:p + LANES, :] + b_ref[s, SHARD_STRIDE:PACK_W, :]
        o_ref[N_ALIGNED:N_ACT, :] = jnp.zeros((N_ACT - N_ALIGNED, W_IN_COLS), BF16)

    return _call(body, name="unpack_w_in", grid=(D_MODEL // W_IN_COLS,),
                 in_specs=[pl.BlockSpec((4, PACK_W, W_IN_COLS), lambda i: (0, 0, i))],
                 out_specs=pl.BlockSpec((N_ACT, W_IN_COLS), lambda i: (0, i)),
                 out_shape=jax.ShapeDtypeStruct((N_ACT, D_MODEL), BF16), sem=("parallel",))(bg)


def pack_grad_w_in(dwt):
    def body(g_ref, o_ref):
        for s in range(4):
            for j in range(PACK_W // LANES):
                p = _act_col((19 * s + j) * LANES)
                o_ref[s, j * LANES:(j + 1) * LANES, :] = g_ref[p:p + LANES, :]

    return _call(body, name="pack_grad_w_in", grid=(D_MODEL // W_IN_COLS,),
                 in_specs=[pl.BlockSpec((N_ACT, W_IN_COLS), lambda i: (0, i))],
                 out_specs=pl.BlockSpec((4, PACK_W, W_IN_COLS), lambda i: (0, 0, i)),
                 out_shape=jax.ShapeDtypeStruct((4, PACK_W, D_MODEL), BF16), sem=("parallel",))(dwt)


def _adamw(w, g, m, v):
    m = ADAM_B1 * m + (1.0 - ADAM_B1) * g
    v = ADAM_B2 * v + (1.0 - ADAM_B2) * jnp.square(g)
    m_hat = m / (1.0 - ADAM_B1 ** ADAM_STEP)
    v_hat = v / (1.0 - ADAM_B2 ** ADAM_STEP)
    delta = -ADAM_LR * (m_hat / (jnp.sqrt(v_hat) + ADAM_EPS) + ADAM_WD * w)
    return delta, m, v


def adamw_w_in(g_packed, wt, mt, vt):
    cols = LANES

    def body(g_ref, w_ref, m_ref, v_ref, go_ref, d_ref, mo_ref, vo_ref):
        chip = _chip_index()
        for cv in range(4):
            @pl.when(chip == cv)
            def _():
                for dst, n, src in _shard_pieces(cv):
                    go_ref[dst:dst + n, :] = g_ref[src:src + n, :]
        d_ref[...], mo_ref[...], vo_ref[...] = _adamw(w_ref[...], go_ref[...], m_ref[...], v_ref[...])

    spec = pl.BlockSpec((SHARD_W, cols), lambda i: (0, i))
    shp = jax.ShapeDtypeStruct((SHARD_W, D_MODEL), F32)
    return _call(body, name="adamw_w_in", grid=(D_MODEL // cols,),
                 in_specs=[pl.BlockSpec((PACK_W, cols), lambda i: (0, i)), spec, spec, spec],
                 out_specs=[spec] * 4, out_shape=[shp] * 4, sem=("parallel",))(g_packed, wt, mt, vt)


def adamw_rows(name, g, w, m, v):
    r, c = g.shape
    rows = min(r, BLOCK)

    def body(g_ref, w_ref, m_ref, v_ref, d_ref, mo_ref, vo_ref):
        d_ref[...], mo_ref[...], vo_ref[...] = _adamw(w_ref[...], g_ref[...], m_ref[...], v_ref[...])

    spec = pl.BlockSpec((rows, c), lambda i: (i, 0))
    shp = jax.ShapeDtypeStruct((r, c), F32)
    return _call(body, name=name, grid=(r // rows,), in_specs=[spec] * 4, out_specs=[spec] * 3, out_shape=[shp] * 3,
                 sem=("parallel",))(g, w, m, v)


def adamw_small(gs, ws, ms, vs):
    n = len(gs)

    def body(*refs):
        g, w, m, v = refs[:n], refs[n:2 * n], refs[2 * n:3 * n], refs[3 * n:4 * n]
        outs = refs[4 * n:]
        for i in range(n):
            d, mn, vn = _adamw(w[i][...], g[i][...], m[i][...], v[i][...])
            outs[3 * i][...] = d
            outs[3 * i + 1][...] = mn
            outs[3 * i + 2][...] = vn

    specs = [_full(a.shape) for a in gs]
    res = _call(body, name="adamw_small", in_specs=specs * 4,
                out_specs=[s for s in specs for _ in range(3)],
                out_shape=[jax.ShapeDtypeStruct(a.shape, F32) for a in gs for _ in range(3)])(*gs, *ws, *ms, *vs)
    return [tuple(res[3 * i:3 * i + 3]) for i in range(n)]


def sum_slots(name, r):
    s, rr, c = r.shape
    rows = min(rr, BLOCK)

    def body(r_ref, o_ref):
        acc = r_ref[0].astype(F32)
        for k in range(1, s):
            acc = acc + r_ref[k].astype(F32)
        o_ref[...] = acc

    return _call(body, name=name, grid=(rr // rows,), in_specs=[pl.BlockSpec((s, rows, c), lambda i: (0, i, 0))],
                 out_specs=pl.BlockSpec((rows, c), lambda i: (i, 0)), out_shape=jax.ShapeDtypeStruct((rr, c), F32),
                 sem=("parallel",))(r)


def sum_pair(partial, from_sibling):
    s, _, rr, cols = partial.shape
    rows = rr // 2

    def body(p_ref, r_ref, o_ref):
        c = lax.axis_index("c")
        o_ref[...] = (p_ref[c].astype(F32) + r_ref[1 - c].astype(F32)).astype(BF16)

    return _call(body, name="sum_pair", grid=(s, rr // rows),
                 in_specs=[pl.BlockSpec((None, 2, rows, cols), lambda k, i: (k, 0, i, 0)),
                           pl.BlockSpec((2, None, rows, cols), lambda k, i: (0, k, i, 0))],
                 out_specs=pl.BlockSpec((None, rows, cols), lambda k, i: (k, i, 0)),
                 out_shape=jax.ShapeDtypeStruct((s, rr, cols), BF16), sem=("parallel", "parallel"))(partial, from_sibling)


def _col_tile(n, k):
    if n % 896 == 0 and k <= 1024:
        return 896
    return min(n, 512)


def project(name, x, wt, comm):
    m, k = x.shape
    n = wt.shape[0]
    tn = D_MODEL
    steps = n // tn

    def body(x_ref, w_ref, o_ref):
        o_ref[...] = lax.dot_general(x_ref[...], w_ref[...], _NT, preferred_element_type=F32)

    def at():
        j = pl.program_id(0)
        return j == 0, j == (3 * steps) // 4, j == steps - 1

    (res,), moved = _call_with_comm(
        body, comm, at, (x, wt), name=name, grid=(steps,),
        in_specs=[_full((m, k)), pl.BlockSpec((tn, k), lambda j: (j, 0))],
        out_specs=[pl.BlockSpec((m, tn), lambda j: (0, j))], out_shape=[jax.ShapeDtypeStruct((m, n), F32)])
    return res, moved


def _piece_tiles(pieces):
    spans, start = [], 0
    for p in pieces:
        spans.append((start, p.shape[1] // D_MODEL))
        start += p.shape[1] // D_MODEL
    return spans, start


def _piece_spec(tm, span, rows_of, tile_of):
    first, count = span

    def index(i, j):
        t = tile_of(i, j) - first
        mine = (t >= 0) & (t < count)
        return jnp.where(mine, rows_of(i, j), 0), jnp.clip(t, 0, count - 1)

    return pl.BlockSpec((tm, D_MODEL), index)


def project_back(name, pieces, wt, after):
    m = pieces[0].shape[0]
    k = wt.shape[1]
    tm = m // 2
    spans, steps = _piece_tiles(pieces)

    def body(*refs):
        w_ref, o_ref = refs[len(pieces)], refs[len(pieces) + 2]
        j = pl.program_id(1)

        @pl.when(j == 0)
        def _():
            o_ref[...] = jnp.zeros_like(o_ref)

        for dy_ref, (first, count) in zip(refs, spans):
            @pl.when((j >= first) & (j < first + count))
            def _():
                o_ref[...] += jnp.dot(dy_ref[...], w_ref[...], preferred_element_type=F32)

    return _call(body, name=name, grid=(m // tm, steps),
                 in_specs=[_piece_spec(tm, s, lambda i, j: i, lambda i, j: j) for s in spans]
                 + [pl.BlockSpec((D_MODEL, k), lambda i, j: (j, 0)), _full(after.shape)],
                 out_specs=pl.BlockSpec((tm, k), lambda i, j: (i, 0)), out_shape=jax.ShapeDtypeStruct((m, k), F32),
                 sem=("parallel", "arbitrary"))(*pieces, wt, after)


def weight_grad_t(name, pieces, x):
    m = pieces[0].shape[0]
    k = x.shape[1]
    tm = m // 2
    spans, steps = _piece_tiles(pieces)

    def body(*refs):
        x_ref, o_ref, acc_ref = refs[len(pieces):]
        i, half = pl.program_id(0), pl.program_id(1)
        for dy_ref, (first, count) in zip(refs, spans):
            @pl.when((i >= first) & (i < first + count))
            def _():
                part = lax.dot_general(dy_ref[...], x_ref[...], (((0,), (0,)), ((), ())), preferred_element_type=F32)

                @pl.when(half == 0)
                def _():
                    acc_ref[...] = part

                @pl.when(half == 1)
                def _():
                    o_ref[...] = (acc_ref[...] + part).astype(BF16)

    return _call(body, name=name, grid=(steps, 2),
                 in_specs=[_piece_spec(tm, s, lambda i, j: j, lambda i, j: i) for s in spans]
                 + [pl.BlockSpec((tm, k), lambda i, j: (j, 0))],
                 out_specs=pl.BlockSpec((D_MODEL, k), lambda i, j: (i, 0)),
                 out_shape=jax.ShapeDtypeStruct((steps * D_MODEL, k), BF16),
                 scratch=[pltpu.VMEM((D_MODEL, k), F32)], sem=("parallel", "arbitrary"))(*pieces, x)


def mm_tn(name, x, dy):
    m, k = x.shape
    n = dy.shape[1]
    tm = m // 2
    tn = _col_tile(n, k)

    def body(x_ref, dy_ref, o_ref, acc_ref):
        part = lax.dot_general(x_ref[...].astype(BF16), dy_ref[...].astype(BF16), (((0,), (0,)), ((), ())),
                               preferred_element_type=F32)

        @pl.when(pl.program_id(1) == 0)
        def _():
            acc_ref[...] = part

        @pl.when(pl.program_id(1) == 1)
        def _():
            o_ref[...] = (acc_ref[...] + part).astype(BF16)

    return _call(body, name=name, grid=(n // tn, 2),
                 in_specs=[pl.BlockSpec((tm, k), lambda i, j: (j, 0)), pl.BlockSpec((tm, tn), lambda i, j: (j, i))],
                 out_specs=pl.BlockSpec((k, tn), lambda i, j: (0, i)), out_shape=jax.ShapeDtypeStruct((k, n), BF16),
                 scratch=[pltpu.VMEM((k, tn), F32)], sem=("parallel", "arbitrary"))(x, dy)


def _row_spec(width, col_block=0):
    return pl.BlockSpec((BLOCK, width), lambda i: (i, col_block))


def _x_spec():
    return pl.BlockSpec((None, BLOCK, D_MODEL), lambda i: (0, jnp.maximum(i - 1, 0), 0))


def prep(x, meta, g_pre):
    nb = x.shape[1] // BLOCK + 1

    def body(x_ref, meta_ref, g_ref, h_ref, u_ref):
        i = pl.program_id(0)

        @pl.when(i == 0)
        def _():
            h_ref[0:PAD_ROWS, :] = jnp.zeros((PAD_ROWS, D_MODEL), F32)
            h_ref[PAD_ROWS:BLOCK, :] = meta_ref[...]

        @pl.when(i > 0)
        def _():
            h_ref[...] = x_ref[...]

        u_ref[...] = _rms(h_ref[...], g_ref[...]).astype(BF16)

    return _call(body, name="prep", grid=(nb,), in_specs=[_x_spec(), _full((N_META, D_MODEL)), _full((1, D_MODEL))],
                 out_specs=[_row_spec(D_MODEL), _row_spec(D_MODEL)],
                 out_shape=[jax.ShapeDtypeStruct((nb * BLOCK, D_MODEL), F32),
                            jax.ShapeDtypeStruct((nb * BLOCK, D_MODEL), BF16)], sem=("parallel",))(x, meta, g_pre)


def prep_bwd(h, du, dres, g_pre):
    nb = h.shape[0] // BLOCK

    def body(h_ref, du_ref, dres_ref, g_ref, gx_ref, gm_ref, gg_ref):
        i = pl.program_id(0)
        _, vjp = jax.vjp(_rms, h_ref[...], g_ref[...])
        dh, dg = vjp(du_ref[...])

        @pl.when(i == 0)
        def _():
            gm_ref[...] = dh[PAD_ROWS:BLOCK, :]
            gg_ref[...] = dg

        @pl.when(i > 0)
        def _():
            gg_ref[...] += dg

        gx_ref[...] = dh + dres_ref[...]

    return _call(body, name="prep_bwd", grid=(nb,),
                 in_specs=[_row_spec(D_MODEL), _row_spec(D_MODEL), _row_spec(D_MODEL), _full((1, D_MODEL))],
                 out_specs=[_x_spec(), _full((N_META, D_MODEL)), _full((1, D_MODEL))],
                 out_shape=[jax.ShapeDtypeStruct((1, (nb - 1) * BLOCK, D_MODEL), F32),
                            jax.ShapeDtypeStruct((N_META, D_MODEL), F32), jax.ShapeDtypeStruct((1, D_MODEL), F32)],
                 sem=("arbitrary",))(h, du, dres, g_pre)


GROUP_W = SSM_INNER // SSM_GROUPS


def _gated_norm(y, z, g):
    t = y * _silu(z)
    return t * lax.rsqrt(jnp.mean(t * t, axis=-1, keepdims=True) + NORM_EPS) * g


def _gated_norm_groups(y, z, g):
    groups = [slice(k * GROUP_W, (k + 1) * GROUP_W) for k in range(SSM_GROUPS)]
    return jnp.concatenate([_gated_norm(y[:, s], z[:, s], g[:, s]) for s in groups], axis=1)


def _merge(ga, gs, ya, ys):
    return _sigmoid(ga) * ya + _sigmoid(gs) * ys


GATE_ATT_BLOCK = SEG["gate_att"][2] // D_MODEL
GATE_SSM_BLOCK = SEG["gate_ssm"][2] // D_MODEL


def _row_loss(out, g_post, x, target):
    diff = x + _rms(out, g_post) - target
    return 0.5 * jnp.sum(diff * diff) / D_MODEL


def tail(y_ssd, proj, a_att, x, target, woa, wos, wo, g_norm, g_post):
    nb = y_ssd.shape[0] // BLOCK
    rows = nb * BLOCK

    def body(y_ref, z_ref, ga_ref, gs_ref, a_ref, x_ref, t_ref, woa_ref, wos_ref, wo_ref, gn_ref, gp_ref,
             yn_ref, mg_ref, dout_ref, dya_ref, dys_ref, da_ref, dy_ref, dz_ref, dga_ref, dgs_ref, dres_ref,
             loss_ref, dgp_ref, dgn_ref):
        i = pl.program_id(0)
        yn, norm_vjp = jax.vjp(_gated_norm_groups, y_ref[...], z_ref[...], gn_ref[...])
        yn16 = yn.astype(BF16)
        y_ssm = jnp.dot(yn16, wos_ref[...], preferred_element_type=F32)
        y_att = jnp.dot(a_ref[...], woa_ref[...], preferred_element_type=F32)
        merged, merge_vjp = jax.vjp(_merge, ga_ref[...], gs_ref[...], y_att, y_ssm)
        merged16 = merged.astype(BF16)
        out = jnp.dot(merged16, wo_ref[...], preferred_element_type=F32)
        loss, loss_vjp = jax.vjp(_row_loss, out, gp_ref[...], x_ref[...], t_ref[...])
        counted = jnp.where(i > 0, 1.0, 0.0)
        dout, dgp, dres, _ = loss_vjp(counted)
        dout16 = dout.astype(BF16)
        dmerged = lax.dot_general(dout16, wo_ref[...], _NT, preferred_element_type=F32)
        dga, dgs, dya, dys = merge_vjp(dmerged)
        dya16, dys16 = dya.astype(BF16), dys.astype(BF16)
        dyn = lax.dot_general(dys16, wos_ref[...], _NT, preferred_element_type=F32)
        dy, dz, dgn = norm_vjp(dyn)

        yn_ref[...] = yn16
        mg_ref[...] = merged16
        dout_ref[...] = dout16
        dya_ref[...] = dya16
        dys_ref[...] = dys16
        da_ref[...] = lax.dot_general(dya16, woa_ref[...], _NT, preferred_element_type=F32)
        dy_ref[...] = dy
        dz_ref[...] = dz.astype(BF16)
        dga_ref[...] = dga.astype(BF16)
        dgs_ref[...] = dgs.astype(BF16)
        dres_ref[...] = dres

        @pl.when(i == 0)
        def _():
            loss_ref[...] = jnp.zeros_like(loss_ref)
            dgp_ref[...] = jnp.zeros_like(dgp_ref)
            dgn_ref[...] = jnp.zeros_like(dgn_ref)

        loss_ref[...] += loss * counted
        dgp_ref[...] += dgp
        dgn_ref[...] += dgn

    wide, narrow = _row_spec(SSM_INNER), _row_spec(D_MODEL)
    resident = pl.BlockSpec(memory_space=pltpu.VMEM)
    bf = lambda w: jax.ShapeDtypeStruct((rows, w), BF16)
    f32 = lambda w: jax.ShapeDtypeStruct((rows, w), F32)
    return _call(body, name="tail", grid=(nb,),
                 in_specs=[wide, wide, _row_spec(D_MODEL, GATE_ATT_BLOCK), _row_spec(D_MODEL, GATE_SSM_BLOCK), narrow,
                           _x_spec(), _x_spec(), resident, resident, resident, _full((1, SSM_INNER)),
                           _full((1, D_MODEL))],
                 out_specs=[wide, narrow, narrow, narrow, narrow, narrow, wide, wide, narrow, narrow, narrow,
                            _full((8, LANES)), _full((1, D_MODEL)), _full((1, SSM_INNER))],
                 out_shape=[bf(SSM_INNER), bf(D_MODEL), bf(D_MODEL), bf(D_MODEL), bf(D_MODEL), f32(D_MODEL),
                            f32(SSM_INNER), bf(SSM_INNER), bf(D_MODEL), bf(D_MODEL), f32(D_MODEL),
                            jax.ShapeDtypeStruct((8, LANES), F32), jax.ShapeDtypeStruct((1, D_MODEL), F32),
                            jax.ShapeDtypeStruct((1, SSM_INNER), F32)],
                 sem=("arbitrary",))(y_ssd, proj, proj, proj, a_att, x, target, woa, wos, wo, g_norm, g_post)


_NT = (((1,), (1,)), ((), ()))
ALIBI_SLOPES = tuple(2.0 ** (-8.0 * (h + 1) / ATT_Q_HEADS) for h in range(ATT_Q_HEADS))
KV_WIDTH = ATT_KV_HEADS * HEAD_DIM
Q_BLOCK = SEG["q"][2] // D_MODEL
Z_ATT_BLOCK = SEG["z_att"][2] // D_MODEL
K_BLOCK = SEG["k"][2] // KV_WIDTH
V_BLOCK = SEG["v"][2] // KV_WIDTH
META_ROW_BLOCK = PAD_ROWS // N_META


@jax.custom_vjp
def _swap_halves(x):
    return pltpu.roll(x, HEAD_DIM, 1)


_swap_halves.defvjp(lambda x: (pltpu.roll(x, HEAD_DIM, 1), None), lambda _, g: (pltpu.roll(g, HEAD_DIM, 1),))


def _both_halves(t, half):
    first = lax.broadcasted_iota(jnp.int32, t.shape, 1) < HEAD_DIM
    sw = _swap_halves(t)
    return jnp.where(first, t, sw) if half == 0 else jnp.where(first, sw, t)


def _attn_rows(q, z, kp, kc, vp, vc, km, vm, sinks, n):
    rows = ATT_GROUP * BLOCK
    i = lax.broadcasted_iota(jnp.int32, (rows, BLOCK), 0) & (BLOCK - 1)
    j = lax.broadcasted_iota(jnp.int32, (rows, BLOCK), 1)
    rel_c = (i - j).astype(F32)
    rel_p = rel_c + float(BLOCK)
    nv = jnp.zeros((rows, BLOCK), jnp.int32) + n
    ok_c = (i >= j) & (nv >= 1)
    ok_p = (j > i) & (nv >= 2)
    im = lax.broadcasted_iota(jnp.int32, (rows, N_META), 0) & (BLOCK - 1)
    jm = lax.broadcasted_iota(jnp.int32, (rows, N_META), 1)
    ok_m = ((jnp.zeros((rows, N_META), jnp.int32) + n) >= 1) | (im >= PAD_ROWS + jm)
    first = lax.broadcasted_iota(jnp.int32, (BLOCK, LANES), 1) < HEAD_DIM
    neg = -jnp.inf
    outs = []
    for kv in range(ATT_KV_HEADS):
        tile, half = divmod(kv, 2)
        lanes = slice(tile * LANES, (tile + 1) * LANES)
        kc2, kp2, km2 = (_both_halves(t[:, lanes], half).astype(BF16) for t in (kc, kp, km))
        vc2, vp2, vm2 = (_both_halves(t[:, lanes], half).astype(BF16) for t in (vc, vp, vm))
        qs, slope, sk = [], [], []
        for pair in range(ATT_GROUP // 2):
            c0 = (kv * ATT_GROUP + 2 * pair) * HEAD_DIM
            qp = q[:, c0:c0 + LANES] * HEAD_DIM ** -0.5
            qs += [jnp.where(first, qp, 0.0), jnp.where(first, 0.0, qp)]
        for g in range(ATT_GROUP):
            slope.append(jnp.full((BLOCK, 1), ALIBI_SLOPES[kv * ATT_GROUP + g], F32))
            sk.append(jnp.broadcast_to(sinks[kv * ATT_GROUP + g], (BLOCK, 1)))
        qs = jnp.concatenate(qs, axis=0).astype(BF16)
        slope = jnp.concatenate(slope, axis=0)
        sk = jnp.concatenate(sk, axis=0)
        sc = jnp.where(ok_c, lax.dot_general(qs, kc2, _NT, preferred_element_type=F32) - slope * rel_c, neg)
        sp = jnp.where(ok_p, lax.dot_general(qs, kp2, _NT, preferred_element_type=F32) - slope * rel_p, neg)
        sm = jnp.where(ok_m, lax.dot_general(qs, km2, _NT, preferred_element_type=F32), neg)
        mx = jnp.maximum(jnp.maximum(jnp.max(sc, axis=1, keepdims=True), jnp.max(sp, axis=1, keepdims=True)),
                         jnp.maximum(jnp.max(sm, axis=1, keepdims=True), sk))
        mx = lax.stop_gradient(mx)
        ec, ep, em, es = jnp.exp(sc - mx), jnp.exp(sp - mx), jnp.exp(sm - mx), jnp.exp(sk - mx)
        den = (es + jnp.sum(ec, axis=1, keepdims=True) + jnp.sum(ep, axis=1, keepdims=True)
               + jnp.sum(em, axis=1, keepdims=True))
        inv = 1.0 / den
        o = (jnp.dot((ec * inv).astype(BF16), vc2, preferred_element_type=F32)
             + jnp.dot((ep * inv).astype(BF16), vp2, preferred_element_type=F32)
             + jnp.dot((em * inv).astype(BF16), vm2, preferred_element_type=F32))
        for pair in range(ATT_GROUP // 2):
            r0 = 2 * pair * BLOCK
            outs.append(jnp.where(first, o[r0:r0 + BLOCK], o[r0 + BLOCK:r0 + 2 * BLOCK]))
    return jnp.concatenate(outs, axis=1) * _silu(z)


def _attn_specs(nb, steps_clamped):
    def blk(t):
        return jnp.minimum(t, nb - 1) if steps_clamped else t

    wide = lambda col: pl.BlockSpec((BLOCK, D_MODEL), lambda t: (blk(t), col))
    cur = lambda col: pl.BlockSpec((BLOCK, KV_WIDTH), lambda t: (blk(t), col))
    prev = lambda col: pl.BlockSpec((BLOCK, KV_WIDTH), lambda t: (jnp.maximum(blk(t) - 1, 0), col))
    meta = lambda col: pl.BlockSpec((N_META, KV_WIDTH), lambda t: (META_ROW_BLOCK, col))
    sinks = pl.BlockSpec((ATT_Q_HEADS, 1, 1), lambda t: (0, 0, 0))
    return [wide(Q_BLOCK), wide(Z_ATT_BLOCK), prev(K_BLOCK), cur(K_BLOCK), prev(V_BLOCK), cur(V_BLOCK),
            meta(K_BLOCK), meta(V_BLOCK), sinks]


def attn_fwd(proj, sinks):
    nb = proj.shape[0] // BLOCK

    def body(q_ref, z_ref, kp_ref, kc_ref, vp_ref, vc_ref, km_ref, vm_ref, sk_ref, o_ref):
        o_ref[...] = _attn_rows(q_ref[...], z_ref[...], kp_ref[...], kc_ref[...], vp_ref[...], vc_ref[...],
                                km_ref[...], vm_ref[...], tuple(sk_ref[h] for h in range(ATT_Q_HEADS)),
                                pl.program_id(0)).astype(BF16)

    return _call(body, name="attn_fwd", grid=(nb,), in_specs=_attn_specs(nb, False), out_specs=_row_spec(D_MODEL),
                 out_shape=jax.ShapeDtypeStruct((nb * BLOCK, D_MODEL), BF16), sem=("parallel",))(*([proj] * 8), sinks)


def attn_bwd(da, proj, sinks):
    nb = proj.shape[0] // BLOCK
    last = nb - 1
    wide = pl.BlockSpec((BLOCK, D_MODEL), lambda t: (jnp.minimum(t, last), 0))
    done = pl.BlockSpec((BLOCK, KV_WIDTH), lambda t: (jnp.maximum(t - 1, 0), 0))
    meta = _full((N_META, KV_WIDTH))
    par = _full((ATT_Q_HEADS, 1, 1))

    def body(da_ref, q_ref, z_ref, kp_ref, kc_ref, vp_ref, vc_ref, km_ref, vm_ref, sk_ref,
             dq_ref, dz_ref, dk_ref, dv_ref, dkm_ref, dvm_ref, dsk_ref, ck_ref, cv_ref):
        t = pl.program_id(0)

        @pl.when(t == 0)
        def _():
            ck_ref[...] = jnp.zeros_like(ck_ref)
            cv_ref[...] = jnp.zeros_like(cv_ref)
            dkm_ref[...] = jnp.zeros_like(dkm_ref)
            dvm_ref[...] = jnp.zeros_like(dvm_ref)
            dsk_ref[...] = jnp.zeros_like(dsk_ref)

        @pl.when(t < nb)
        def _():
            def f(q, z, kp, kc, vp, vc, km, vm, sk):
                return _attn_rows(q, z, kp, kc, vp, vc, km, vm, sk, t)

            _, vjp = jax.vjp(f, q_ref[...], z_ref[...], kp_ref[...], kc_ref[...], vp_ref[...], vc_ref[...],
                             km_ref[...], vm_ref[...], tuple(sk_ref[h] for h in range(ATT_Q_HEADS)))
            dq, dz, dkp, dkc, dvp, dvc, dkm, dvm, dsk = vjp(da_ref[...])
            dq_ref[...] = dq.astype(BF16)
            dz_ref[...] = dz.astype(BF16)
            for h in range(ATT_Q_HEADS):
                dsk_ref[h] += dsk[h]
            dk_ref[...] = ck_ref[...] + dkp
            dv_ref[...] = cv_ref[...] + dvp
            ck_ref[...] = dkc
            cv_ref[...] = dvc
            dkm_ref[...] += dkm
            dvm_ref[...] += dvm

        @pl.when(t == nb)
        def _():
            dk_ref[...] = ck_ref[...]
            dv_ref[...] = cv_ref[...]

    rows = nb * BLOCK
    return _call(body, name="attn_bwd", grid=(nb + 1,), in_specs=[wide] + _attn_specs(nb, True),
                 out_specs=[wide, wide, done, done, meta, meta, par],
                 out_shape=[jax.ShapeDtypeStruct((rows, D_MODEL), BF16), jax.ShapeDtypeStruct((rows, D_MODEL), BF16),
                            jax.ShapeDtypeStruct((rows, KV_WIDTH), F32), jax.ShapeDtypeStruct((rows, KV_WIDTH), F32),
                            jax.ShapeDtypeStruct((N_META, KV_WIDTH), F32), jax.ShapeDtypeStruct((N_META, KV_WIDTH), F32),
                            jax.ShapeDtypeStruct(sinks.shape, F32)],
                 scratch=[pltpu.VMEM((BLOCK, KV_WIDTH), F32), pltpu.VMEM((BLOCK, KV_WIDTH), F32)],
                 sem=("arbitrary",))(da, *([proj] * 8), sinks)


XBC_BLOCK0 = SEG["xbc"][2] // D_MODEL
CONV_COL_BLOCKS = CONV_DIM // D_MODEL
DT_TILE = SEG["dt"][2] // LANES


HALO = 8
HALOS_PER_BLOCK = BLOCK // HALO


def _shift_rows(cur, before, j):
    if j == 0:
        return cur
    n = cur.shape[0]
    row = lax.broadcasted_iota(jnp.int32, cur.shape, 0)
    head = pltpu.roll(before, j, 0)
    if n > HALO:
        head = jnp.concatenate([head, jnp.zeros((n - HALO, cur.shape[1]), cur.dtype)], axis=0)
    return jnp.where(row >= j, pltpu.roll(cur, j, 0), head)


def _conv_pre(cur, before, w_ref, b_ref):
    pre = b_ref[...] + w_ref[CONV_WIDTH - 1:CONV_WIDTH, :] * cur
    for k in range(CONV_WIDTH - 1):
        pre = pre + w_ref[k:k + 1, :] * _shift_rows(cur, before, CONV_WIDTH - 1 - k)
    return pre


def _conv_specs(nb, col0=0):
    first = XBC_BLOCK0 + col0
    cur = pl.BlockSpec((BLOCK, D_MODEL), lambda j, i: (i, first + j))
    before = pl.BlockSpec((HALO, D_MODEL), lambda j, i: (jnp.maximum(i * HALOS_PER_BLOCK - 1, 0), first + j))
    after = pl.BlockSpec((HALO, D_MODEL), lambda j, i: (jnp.minimum(i + 1, nb - 1) * HALOS_PER_BLOCK, first + j))
    return cur, before, after


def _valid_rows(i):
    row = lax.broadcasted_iota(jnp.int32, (BLOCK, D_MODEL), 0)
    return jnp.maximum((row >= PAD_ROWS).astype(F32), jnp.where(i > 0, 1.0, 0.0))


def conv_fwd(proj, conv_w, conv_b):
    nb = proj.shape[0] // BLOCK
    cur, before, _ = _conv_specs(nb)

    def body(c_ref, p_ref, w_ref, b_ref, o_ref):
        i = pl.program_id(1)
        pre = _conv_pre(c_ref[...], p_ref[...] * jnp.where(i > 0, 1.0, 0.0), w_ref, b_ref)
        o_ref[...] = _silu(pre) * _valid_rows(i)

    return _call(body, name="conv_fwd", grid=(CONV_COL_BLOCKS, nb),
                 in_specs=[cur, before, pl.BlockSpec((CONV_WIDTH, D_MODEL), lambda j, i: (0, j)),
                           pl.BlockSpec((1, D_MODEL), lambda j, i: (0, j))],
                 out_specs=pl.BlockSpec((BLOCK, D_MODEL), lambda j, i: (i, j)),
                 out_shape=jax.ShapeDtypeStruct((nb * BLOCK, CONV_DIM), F32),
                 sem=("parallel", "parallel"))(proj, proj, conv_w, conv_b)


def conv_bwd(name, dparts, col0, proj, conv_w, conv_b):
    nb = proj.shape[0] // BLOCK
    last = nb - 1
    ncol = sum(d.shape[1] for d in dparts) // D_MODEL
    np_ = len(dparts)
    cur, before, after = _conv_specs(nb, col0)
    dcur = [pl.BlockSpec((BLOCK, d.shape[1] // ncol), lambda j, i: (i, j)) for d in dparts]
    dafter = [pl.BlockSpec((HALO, d.shape[1] // ncol), lambda j, i: (jnp.minimum(i + 1, last) * HALOS_PER_BLOCK, j))
              for d in dparts]
    out_cur = pl.BlockSpec((BLOCK, D_MODEL), lambda j, i: (i, j))
    wspec = pl.BlockSpec((CONV_WIDTH, D_MODEL), lambda j, i: (0, col0 + j))
    bspec = pl.BlockSpec((1, D_MODEL), lambda j, i: (0, col0 + j))
    wout = pl.BlockSpec((CONV_WIDTH, D_MODEL), lambda j, i: (0, j))
    bout = pl.BlockSpec((1, D_MODEL), lambda j, i: (0, j))

    def body(*refs):
        dc_refs, da_refs = refs[:np_], refs[np_:2 * np_]
        c_ref, p_ref, a_ref, w_ref, b_ref, du_ref, dw_ref, db_ref = refs[2 * np_:]
        i = pl.program_id(1)
        row = lax.broadcasted_iota(jnp.int32, (BLOCK, D_MODEL), 0)
        curv = c_ref[...]
        beforev = p_ref[...] * jnp.where(i > 0, 1.0, 0.0)
        side_by_side = lambda rs: rs[0][...] if np_ == 1 else jnp.concatenate([r[...] for r in rs], axis=1)

        def dpre_of(pre, d):
            s = _sigmoid(pre)
            return d * (s * (1.0 + pre * (1.0 - s)))

        dp_c = dpre_of(_conv_pre(curv, beforev, w_ref, b_ref), side_by_side(dc_refs) * _valid_rows(i))
        dp_a = dpre_of(_conv_pre(a_ref[...], curv[BLOCK - HALO:], w_ref, b_ref),
                       side_by_side(da_refs) * jnp.where(i < last, 1.0, 0.0))
        du = w_ref[CONV_WIDTH - 1:CONV_WIDTH, :] * dp_c
        for j in range(1, CONV_WIDTH):
            tail = jnp.concatenate([jnp.zeros((BLOCK - HALO, D_MODEL), F32), pltpu.roll(dp_a, HALO - j, 0)], axis=0)
            up = jnp.where(row < BLOCK - j, pltpu.roll(dp_c, BLOCK - j, 0), tail)
            du = du + w_ref[CONV_WIDTH - 1 - j:CONV_WIDTH - j, :] * up
        du_ref[...] = du.astype(BF16)

        @pl.when(i == 0)
        def _():
            dw_ref[...] = jnp.zeros_like(dw_ref)
            db_ref[...] = jnp.zeros_like(db_ref)

        for k in range(CONV_WIDTH):
            dw_ref[k:k + 1, :] += jnp.sum(dp_c * _shift_rows(curv, beforev, CONV_WIDTH - 1 - k), axis=0, keepdims=True)
        db_ref[...] += jnp.sum(dp_c, axis=0, keepdims=True)

    width = ncol * D_MODEL
    return _call(body, name=name, grid=(ncol, nb),
                 in_specs=dcur + dafter + [cur, before, after, wspec, bspec], out_specs=[out_cur, wout, bout],
                 out_shape=[jax.ShapeDtypeStruct((nb * BLOCK, width), BF16),
                            jax.ShapeDtypeStruct((CONV_WIDTH, width), F32), jax.ShapeDtypeStruct((1, width), F32)],
                 sem=("parallel", "arbitrary"))(*dparts, *dparts, proj, proj, proj, conv_w, conv_b)


def _head_expand():
    e = np.zeros((LANES, SSM_INNER), np.float32)
    for h in range(SSM_HEADS):
        e[h, h * HEAD_DIM:(h + 1) * HEAD_DIM] = 1.0
    return jnp.asarray(e, dtype=BF16)


def _softplus(x):
    return jnp.maximum(x, 0.0) + jnp.log(1.0 + jnp.exp(-jnp.abs(x)))


def _bf16_parts(x):
    hi = x.astype(BF16)
    rest = x - hi.astype(F32)
    mid = rest.astype(BF16)
    return hi, mid, (rest - mid.astype(F32)).astype(BF16)


@jax.custom_vjp
def _times_01(x, m):
    return sum(jnp.dot(p, m, preferred_element_type=F32) for p in _bf16_parts(x))


def _times_01_bwd(m, g):
    return sum(lax.dot_general(p, m, _NT, preferred_element_type=F32) for p in _bf16_parts(g)), jnp.zeros_like(m)


_times_01.defvjp(lambda x, m: (_times_01(x, m), m), _times_01_bwd)


def _causal_ones():
    l = lax.broadcasted_iota(jnp.int32, (BLOCK, BLOCK), 0)
    s = lax.broadcasted_iota(jnp.int32, (BLOCK, BLOCK), 1)
    return (l >= s).astype(BF16)


@jax.custom_vjp
def _cumsum_rows(a):
    return sum(jnp.dot(_causal_ones(), p, preferred_element_type=F32) for p in _bf16_parts(a))


def _cumsum_rows_bwd(_, g):
    tn = (((0,), (0,)), ((), ()))
    return (sum(lax.dot_general(_causal_ones(), p, tn, preferred_element_type=F32) for p in _bf16_parts(g)),)


_cumsum_rows.defvjp(lambda a: (_cumsum_rows(a), None), _cumsum_rows_bwd)


def _ssd_group(xs, dt_tile, expand, bias, alog, dsk, bg, cg, state):
    l = lax.broadcasted_iota(jnp.int32, (BLOCK, BLOCK), 0)
    s = lax.broadcasted_iota(jnp.int32, (BLOCK, BLOCK), 1)
    causal = l >= s
    first_head = s < HEAD_DIM
    dt = _softplus(dt_tile + bias)
    a = dt * (-jnp.exp(alog))
    one_row = lambda v: jnp.broadcast_to(v, (HALO, LANES))
    per_lane = _times_01(jnp.concatenate([dt, _cumsum_rows(a), one_row(jnp.sum(a, axis=0, keepdims=True)),
                                          one_row(dsk)], axis=0), expand)
    dtx, cs = per_lane[0:BLOCK], per_lane[BLOCK:2 * BLOCK]
    tot, dsk = per_lane[2 * BLOCK:2 * BLOCK + 1], per_lane[2 * BLOCK + HALO:2 * BLOCK + HALO + 1]
    bb, cb16 = bg.astype(BF16), cg.astype(BF16)
    cb = lax.dot_general(cb16, bb, _NT, preferred_element_type=F32)
    xr = xs * dtx
    y_diag = []
    for p in range(GROUP_W // LANES):
        lanes = slice(p * LANES, (p + 1) * LANES)
        c_pair = cs[:, lanes]
        c_swap = _swap_halves(c_pair)
        m = []
        for c_head in (jnp.where(first_head, c_pair, c_swap), jnp.where(first_head, c_swap, c_pair)):
            m.append(cb * jnp.exp(jnp.where(causal, c_head - c_head.T, -jnp.inf)))
        x_pair = xr[:, lanes]
        x_diag = jnp.concatenate([jnp.where(first_head, x_pair, 0.0), jnp.where(first_head, 0.0, x_pair)], axis=0)
        y_diag.append(jnp.dot(jnp.concatenate(m, axis=1).astype(BF16), x_diag.astype(BF16),
                              preferred_element_type=F32))
    st = lax.dot_general(bb, (xr * jnp.exp(tot - cs)).astype(BF16), (((0,), (0,)), ((), ())),
                         preferred_element_type=F32)
    new_state = state * jnp.exp(tot) + st
    y_off = jnp.dot(cb16, state.astype(BF16), preferred_element_type=F32) * jnp.exp(cs)
    return jnp.concatenate(y_diag, axis=1) + y_off + dsk * xs, new_state


BC_WIDTH = SSM_GROUPS * SSM_STATE


def _ssd_specs(chunk):
    xs = pl.BlockSpec((BLOCK, SSM_INNER), lambda c: (chunk(c), 0))
    dt = pl.BlockSpec((BLOCK, LANES), lambda c: (chunk(c), DT_TILE))
    expand = _full((LANES, SSM_INNER))
    b = pl.BlockSpec((BLOCK, BC_WIDTH), lambda c: (chunk(c), SSM_INNER // BC_WIDTH))
    cc = pl.BlockSpec((BLOCK, BC_WIDTH), lambda c: (chunk(c), SSM_INNER // BC_WIDTH + 1))
    par = _full((1, LANES))
    state = pl.BlockSpec((None, SSM_STATE, SSM_INNER), lambda c: (chunk(c), 0, 0))
    return xs, dt, expand, b, cc, par, state


def _group_lanes(g):
    return slice(g * GROUP_W, (g + 1) * GROUP_W), slice(g * SSM_STATE, (g + 1) * SSM_STATE)


def ssd_fwd(xbc, proj, expand, bias, alog, dsk):
    nb = xbc.shape[0] // BLOCK
    xs, dt, ex, b, cc, par, state = _ssd_specs(lambda c: c)

    def body(x_ref, dt_ref, e_ref, bi_ref, al_ref, dk_ref, b_ref, c_ref, y_ref, sp_ref, st_ref):
        @pl.when(pl.program_id(0) == 0)
        def _():
            st_ref[...] = jnp.zeros_like(st_ref)

        for g in range(SSM_GROUPS):
            wide, tile = _group_lanes(g)
            entering = st_ref[:, wide]
            sp_ref[:, wide] = entering
            y_ref[:, wide], st_ref[:, wide] = _ssd_group(
                x_ref[:, wide], dt_ref[...], e_ref[:, wide], bi_ref[...], al_ref[...], dk_ref[...], b_ref[:, tile],
                c_ref[:, tile], entering)

    return _call(body, name="ssd_fwd", grid=(nb,), in_specs=[xs, dt, ex, par, par, par, b, cc],
                 out_specs=[xs, state],
                 out_shape=[jax.ShapeDtypeStruct((nb * BLOCK, SSM_INNER), F32),
                            jax.ShapeDtypeStruct((nb, SSM_STATE, SSM_INNER), F32)],
                 scratch=[pltpu.VMEM((SSM_STATE, SSM_INNER), F32)],
                 sem=("arbitrary",))(xbc, proj, expand, bias, alog, dsk, xbc, xbc)


def ssd_bwd(dy, xbc, proj, expand, bias, alog, dsk, states, comm):
    nb = xbc.shape[0] // BLOCK
    last = nb - 1
    xs, dt, ex, b, cc, par, state = _ssd_specs(lambda c: last - c)
    tile = pl.BlockSpec((BLOCK, LANES), lambda c: (last - c, 0))
    nspec = pl.BlockSpec((BLOCK, BC_WIDTH), lambda c: (last - c, 0))

    def body(dy_ref, x_ref, dt_ref, e_ref, bi_ref, al_ref, dk_ref, b_ref, c_ref, sp_ref,
             dx_ref, ddt_ref, db_ref, dc_ref, dbi_ref, dal_ref, ddk_ref, ds_ref):
        @pl.when(pl.program_id(0) == 0)
        def _():
            ds_ref[...] = jnp.zeros_like(ds_ref)
            dbi_ref[...] = jnp.zeros_like(dbi_ref)
            dal_ref[...] = jnp.zeros_like(dal_ref)
            ddk_ref[...] = jnp.zeros_like(ddk_ref)

        ddt, dbi, dal, ddk = 0.0, 0.0, 0.0, 0.0
        for g in range(SSM_GROUPS):
            wide, tile_lanes = _group_lanes(g)
            expand_rows = e_ref[:, wide]

            def f(xs, dt_tile, bias, alog, dsk, bg, cg, state):
                return _ssd_group(xs, dt_tile, expand_rows, bias, alog, dsk, bg, cg, state)

            _, vjp = jax.vjp(f, x_ref[:, wide], dt_ref[...], bi_ref[...], al_ref[...], dk_ref[...], b_ref[:, tile_lanes],
                             c_ref[:, tile_lanes], sp_ref[:, wide])
            (dx_ref[:, wide], ddt_g, dbi_g, dal_g, ddk_g, db_ref[:, tile_lanes], dc_ref[:, tile_lanes],
             ds_ref[:, wide]) = vjp((dy_ref[:, wide], ds_ref[:, wide]))
            ddt, dbi, dal, ddk = ddt + ddt_g, dbi + dbi_g, dal + dal_g, ddk + ddk_g
        ddt_ref[...] = ddt.astype(BF16)
        dbi_ref[...] += dbi
        dal_ref[...] += dal
        ddk_ref[...] += ddk

    def at():
        c = pl.program_id(0)
        return c == 0, c == 0, c == last

    par_shape = jax.ShapeDtypeStruct((1, LANES), F32)
    return _call_with_comm(
        body, comm, at, (dy, xbc, proj, expand, bias, alog, dsk, xbc, xbc, states), name="ssd_bwd",
        grid=(nb,), in_specs=[xs, xs, dt, ex, par, par, par, b, cc, state],
        out_specs=[xs, tile, nspec, nspec, par, par, par],
        out_shape=[jax.ShapeDtypeStruct((nb * BLOCK, SSM_INNER), F32), jax.ShapeDtypeStruct((nb * BLOCK, LANES), BF16),
                   jax.ShapeDtypeStruct((nb * BLOCK, BC_WIDTH), F32), jax.ShapeDtypeStruct((nb * BLOCK, BC_WIDTH), F32),
                   par_shape, par_shape, par_shape],
        scratch=[pltpu.VMEM((SSM_STATE, SSM_INNER), F32)])


SLAB_ROWS = 24
SLAB_META_ROW = 8


SLAB_LOSS_ROW = 7


def pack_small(dcw, dcb, dgpre, dgpost, dbias, dalog, ddsk, dsinks, dgn, dmeta, loss_tile):
    def body(cw, cb, gpre, gpost, dtb, al, dk, sk, gn, meta, loss, o_ref):
        o_ref[...] = jnp.zeros_like(o_ref)
        o_ref[SLAB_LOSS_ROW:SLAB_LOSS_ROW + 1, 0:LANES] = loss[0:1, :]
        o_ref[0:CONV_WIDTH, :] = cw[...]
        o_ref[4:5, :] = cb[...]
        o_ref[5:6, 0:1024] = gpre[...]
        o_ref[5:6, 1024:2048] = gpost[...]
        o_ref[5:6, 2048:2176] = dtb[...]
        o_ref[5:6, 2176:2304] = al[...]
        o_ref[5:6, 2304:2432] = dk[...]
        o_ref[5:6, 2432:2560] = sk[...]
        o_ref[6:7, 0:SSM_INNER] = gn[...]
        o_ref[SLAB_META_ROW:SLAB_META_ROW + N_META, 0:D_MODEL] = meta[...]

    args = (dcw, dcb, dgpre, dgpost, dbias, dalog, ddsk, dsinks, dgn, dmeta, loss_tile)
    return _call(body, name="pack_small", in_specs=[_full(a.shape) for a in args],
                 out_specs=_full((SLAB_ROWS, CONV_DIM)), out_shape=jax.ShapeDtypeStruct((SLAB_ROWS, CONV_DIM), F32))(*args)


def _lane_tile(v):
    return jnp.pad(v, ((0, 0), (0, LANES - v.shape[1])))


def kernel(x, meta_tokens, g_pre, w_in, conv_w, conv_b, dt_bias, a_log, d_skip, attn_sinks, g_ssm_norm, w_out_att, w_out_ssm, w_out, g_post, loss_target, m_meta_tokens, m_g_pre, m_w_in, m_conv_w, m_conv_b, m_dt_bias, m_a_log, m_d_skip, m_attn_sinks, m_g_ssm_norm, m_w_out_att, m_w_out_ssm, m_w_out, m_g_post, v_meta_tokens, v_g_pre, v_w_in, v_conv_w, v_conv_b, v_dt_bias, v_a_log, v_d_skip, v_attn_sinks, v_g_ssm_norm, v_w_out_att, v_w_out_ssm, v_w_out, v_g_post):
    chip = _chip_index()

    conv_w_rows = jnp.pad(conv_w[0], ((0, 2 * 8 - CONV_WIDTH), (0, 0)))
    w_in_t, m_w_in_t, v_w_in_t = w_in[0].T, m_w_in[0].T, v_w_in[0].T
    gathered_w_in, g_conv_w, g_meta = run_comm("gather_w_in",
                                               TwoLevelGather([pack_w_in(w_in_t), conv_w_rows, meta_tokens]))
    w_all_t = unpack_w_in(gathered_w_in)
    cw_full = g_conv_w[:, :CONV_WIDTH].transpose(1, 0, 2).reshape(CONV_WIDTH, CONV_DIM)
    meta_full = g_meta.transpose(1, 0, 2).reshape(N_META, D_MODEL)

    h, u = prep(x, meta_full, g_pre)
    proj, g_w_out = project("in_proj", u, w_all_t, TwoLevelGather(
        [w_out_att[0].astype(BF16), w_out_ssm[0].astype(BF16), w_out[0].astype(BF16)]))
    woa = g_w_out[0].reshape(D_MODEL, D_MODEL)
    wos = g_w_out[1].reshape(SSM_INNER, D_MODEL)
    wo = g_w_out[2].reshape(D_MODEL, D_MODEL)

    sinks3 = attn_sinks.reshape(ATT_Q_HEADS, 1, 1)
    a_att = attn_fwd(proj, sinks3)

    xbc = conv_fwd(proj, cw_full, conv_b)
    expand = _head_expand()
    head_pars = (_lane_tile(dt_bias), _lane_tile(a_log), _lane_tile(d_skip))
    y_ssd, states = ssd_fwd(xbc, proj, expand, *head_pars)

    (yn, merged, dout, dy_att, dy_ssm, da_att, dy_ssd, dz_ssm, dga, dgs, dres, loss_tile, dg_post, dgn) = tail(
        y_ssd, proj, a_att, x, loss_target, woa, wos, wo, g_ssm_norm, g_post)

    dwo = mm_tn("out_proj_dw", merged, dout)
    dwoa = mm_tn("att_out_dw", a_att, dy_att)
    dwos = mm_tn("ssm_out_dw", yn, dy_ssm)
    dq, dz_att, dk, dv, dkmeta, dvmeta, dsinks3 = attn_bwd(da_att, proj, sinks3)
    dk = dk.at[PAD_ROWS:BLOCK].add(dkmeta).astype(BF16)
    dv = dv.at[PAD_ROWS:BLOCK].add(dvmeta).astype(BF16)

    def pieces(g):
        return g.reshape(4, 2, g.shape[0] // 8, g.shape[1])

    def to_owner(g):
        return (lambda ref, dev: ref.at[_chip_of(dev), dev[2]], (g.shape[0] // 8, g.shape[1]))

    (dxs, ddt_tile, dbg, dcg, dbias, dalog, ddsk), sent_w_out = ssd_bwd(
        dy_ssd, xbc, proj, expand, *head_pars, states,
        DirectExchange([pieces(dwoa), pieces(dwos), pieces(dwo)], [to_owner(dwoa), to_owner(dwos), to_owner(dwo)],
                       ALL_MASKS, "dev", 8))
    dxs_raw, dcw_xs, dcb_xs = conv_bwd("conv_bwd_x", [dxs], 0, proj, cw_full, conv_b)
    dbc_raw, dcw_bc, dcb_bc = conv_bwd("conv_bwd_bc", [dbg, dcg], SSM_INNER // D_MODEL, proj, cw_full, conv_b)
    dcw = jnp.concatenate([dcw_xs, dcw_bc], axis=1)
    dcb = jnp.concatenate([dcb_xs, dcb_bc], axis=1)

    narrow = jnp.concatenate([dk, dv, ddt_tile, jnp.zeros((dk.shape[0], N_ACT - N_ALIGNED), BF16)], axis=1)
    dproj = [dz_ssm, dxs_raw, dbc_raw, dq, dz_att, dga, dgs, narrow]
    dw_all_t = weight_grad_t("in_proj_dw", dproj, u)

    half_rows = PACK_W // 2
    partial = pack_grad_w_in(dw_all_t).reshape(4, 2, half_rows, D_MODEL)
    from_sibling, = run_comm("pair_grads", DirectExchange(
        [partial], [(lambda ref, dev: ref.at[pl.ds(0, 4), dev[2]], (4, half_rows, D_MODEL))], SIBLING_MASK, "core", 2,
        keep_own=False))
    chip_sum = sum_pair(partial, from_sibling)
    *in_flight, started = chip_exchange_start(chip_sum)
    du = project_back("in_proj_dx", dproj, w_all_t, started)
    grad_x, dmeta, dg_pre = prep_bwd(h, du, dres, g_pre)
    sent_w_in = chip_exchange_wait(*in_flight, dg_pre)

    slab = pack_small(dcw, dcb, dg_pre, dg_post, dbias, dalog, ddsk, _lane_tile(dsinks3.reshape(1, ATT_Q_HEADS)), dgn,
                      dmeta, loss_tile)
    halves = [sum_slots("sum_" + nm, r)
              for nm, r in zip(("w_in", "w_out_att", "w_out_ssm", "w_out"), [sent_w_in] + list(sent_w_out))]
    shared = run_comm("share_grads", Both(DirectExchange(halves, [None] * 4, SIBLING_MASK, "core", 2),
                                          DirectExchange([slab], [None], ALL_MASKS, "dev", 8)))
    g_w_in_packed, g_woa, g_wos, g_wo = [f.reshape(2 * f.shape[1], f.shape[2]) for f in shared[:4]]
    small = sum_slots("sum_small", shared[4])
    loss = small[SLAB_LOSS_ROW, 0]

    g_w_in, d_w_in, nm_w_in, nv_w_in = [a.T for a in adamw_w_in(g_w_in_packed, w_in_t, m_w_in_t, v_w_in_t)]
    d_woa, nm_woa, nv_woa = adamw_rows("adamw_w_out_att", g_woa, w_out_att[0], m_w_out_att[0], v_w_out_att[0])
    d_wos, nm_wos, nv_wos = adamw_rows("adamw_w_out_ssm", g_wos, w_out_ssm[0], m_w_out_ssm[0], v_w_out_ssm[0])
    d_wo, nm_wo, nv_wo = adamw_rows("adamw_w_out", g_wo, w_out[0], m_w_out[0], v_w_out[0])

    cw_cols = CONV_DIM // 4
    meta_cols = D_MODEL // 4
    g_small = {
        "meta_tokens": lax.dynamic_slice(small, (SLAB_META_ROW, chip * meta_cols), (N_META, meta_cols)),
        "g_pre": small[5:6, 0:1024],
        "conv_w": lax.dynamic_slice(small, (0, chip * cw_cols), (CONV_WIDTH, cw_cols)),
        "conv_b": small[4:5, :],
        "dt_bias": small[5:6, 2048:2048 + SSM_HEADS],
        "a_log": small[5:6, 2176:2176 + SSM_HEADS],
        "d_skip": small[5:6, 2304:2304 + SSM_HEADS],
        "attn_sinks": small[5:6, 2432:2432 + ATT_Q_HEADS],
        "g_ssm_norm": small[6:7, 0:SSM_INNER],
        "g_post": small[5:6, 1024:2048],
    }
    names = list(g_small)
    w_small = dict(meta_tokens=meta_tokens, g_pre=g_pre, conv_w=conv_w[0], conv_b=conv_b, dt_bias=dt_bias, a_log=a_log,
                   d_skip=d_skip, attn_sinks=attn_sinks, g_ssm_norm=g_ssm_norm, g_post=g_post)
    m_small = dict(meta_tokens=m_meta_tokens, g_pre=m_g_pre, conv_w=m_conv_w[0], conv_b=m_conv_b, dt_bias=m_dt_bias,
                   a_log=m_a_log, d_skip=m_d_skip, attn_sinks=m_attn_sinks, g_ssm_norm=m_g_ssm_norm, g_post=m_g_post)
    v_small = dict(meta_tokens=v_meta_tokens, g_pre=v_g_pre, conv_w=v_conv_w[0], conv_b=v_conv_b, dt_bias=v_dt_bias,
                   a_log=v_a_log, d_skip=v_d_skip, attn_sinks=v_attn_sinks, g_ssm_norm=v_g_ssm_norm, g_post=v_g_post)
    upd = dict(zip(names, adamw_small([g_small[k] for k in names], [w_small[k] for k in names],
                                      [m_small[k] for k in names], [v_small[k] for k in names])))

    lead = {"conv_w"}

    def shaped(name, a):
        return a[None] if name in lead else a

    grads = dict(g_small, w_in=g_w_in, w_out_att=g_woa, w_out_ssm=g_wos, w_out=g_wo)
    deltas = dict({k: upd[k][0] for k in names}, w_in=d_w_in, w_out_att=d_woa, w_out_ssm=d_wos, w_out=d_wo)
    new_m = dict({k: upd[k][1] for k in names}, w_in=nm_w_in, w_out_att=nm_woa, w_out_ssm=nm_wos, w_out=nm_wo)
    new_v = dict({k: upd[k][2] for k in names}, w_in=nv_w_in, w_out_att=nv_woa, w_out_ssm=nv_wos, w_out=nv_wo)
    lead |= {"w_in", "w_out_att", "w_out_ssm", "w_out"}
    order = ["meta_tokens", "g_pre", "w_in", "conv_w", "conv_b", "dt_bias", "a_log", "d_skip", "attn_sinks",
             "g_ssm_norm", "w_out_att", "w_out_ssm", "w_out", "g_post"]
    outs = [loss, grad_x]
    for group in (grads, deltas, new_m, new_v):
        outs += [shaped(k, group[k]) for k in order]
    return tuple(outs)
```

```python
import functools

import numpy as np
import jax
import jax.numpy as jnp
from jax import lax
from jax.experimental import pallas as pl
from jax.experimental.pallas import tpu as pltpu

F32 = jnp.float32
BF16 = jnp.bfloat16
HI = lax.Precision.HIGHEST

D_MODEL = 1024
N_META = 16
BLOCK = 128
PAD_ROWS = BLOCK - N_META
NORM_EPS = 1e-6
HEAD_DIM = 64
ATT_Q_HEADS = 16
ATT_KV_HEADS = 4
ATT_GROUP = 4
SSM_INNER = 2048
SSM_HEADS = 32
SSM_GROUPS = 4
SSM_HEADS_PER_GROUP = 8
SSM_STATE = 128
CONV_WIDTH = 4
CONV_DIM = 3072
LANES = 128

ADAM_LR = 0.001
ADAM_B1 = 0.9
ADAM_B2 = 0.999
ADAM_EPS = 1e-08
ADAM_WD = 0.01
ADAM_STEP = 10

VMEM_LIMIT = 48 * 1024 * 1024

SHARD_W = 2440
PACK_W = 2560
SHARD_STRIDE = 2432
N_ALIGNED = 9856
N_ACT = 10240
SEG = {
    "q": (0, 1024, 5120), "k": (1024, 256, 9216), "v": (1280, 256, 9472), "z_att": (1536, 1024, 6144),
    "z_ssm": (2560, 2048, 0), "xbc": (4608, 3072, 2048), "dt": (7680, 128, 9728),
    "gate_att": (7808, 1024, 7168), "gate_ssm": (8832, 1024, 8192),
}
DT_STORED_START = 7680
DT_PAD = LANES - SSM_HEADS


def _act_col(aligned_col):
    for a0, w, p0 in SEG.values():
        if a0 <= aligned_col < a0 + w:
            return p0 + aligned_col - a0
    raise ValueError(aligned_col)


def _call(body, *, name, out_shape, in_specs, out_specs, grid=(), scratch=(), sem=None, aliases=None):
    return pl.pallas_call(
        body, out_shape=out_shape, grid=grid, in_specs=in_specs, out_specs=out_specs, scratch_shapes=list(scratch),
        name=name, input_output_aliases=aliases or {},
        compiler_params=pltpu.CompilerParams(dimension_semantics=sem, vmem_limit_bytes=VMEM_LIMIT))


def _full(shape):
    n = len(shape)
    return pl.BlockSpec(shape, lambda *_: (0,) * n)


def _chip_index():
    return lax.axis_index("x") * 2 + lax.axis_index("y")


_sigmoid = jax.nn.sigmoid


def _silu(z):
    return z * _sigmoid(z)


def _rms(x, g):
    return x * lax.rsqrt(jnp.mean(x * x, axis=-1, keepdims=True) + NORM_EPS) * g


def _peer(mask):
    x, y, c = lax.axis_index("x"), lax.axis_index("y"), lax.axis_index("c")
    return ((1 - x) if mask & 4 else x, (1 - y) if mask & 2 else y, (1 - c) if mask & 1 else c)


def _me():
    return lax.axis_index("x"), lax.axis_index("y"), lax.axis_index("c")


def _chip_of(dev):
    return 2 * dev[0] + dev[1]


CHIP_MASKS = (4, 2, 6)
ALL_MASKS = (1, 2, 3, 4, 5, 6, 7)
SIBLING_MASK = (1,)


def _remote(src, dst, send_sem, recv_sem, dev):
    return pltpu.make_async_remote_copy(src_ref=src, dst_ref=dst, send_sem=send_sem, recv_sem=recv_sem,
                                        device_id=dev, device_id_type=pl.DeviceIdType.MESH)


class _StagedCopy:
    def __init__(self, src, stage, dst, load_sem, store_sem):
        self.load = pltpu.make_async_copy(src, stage, load_sem)
        self.store = pltpu.make_async_copy(stage, dst, store_sem)

    def start(self):
        self.load.start()
        self.load.wait()
        self.store.start()

    def wait(self):
        self.store.wait()


class DirectExchange:
    def __init__(self, arrays, pieces, masks, slot_kind, nslots, keep_own=True):
        self.arrays, self.pieces, self.masks, self.slot_kind = list(arrays), list(pieces), masks, slot_kind
        self.keep_own = keep_own
        n, nk = len(arrays), len(masks)
        shapes = [a.shape if p is None else p[1] for a, p in zip(arrays, pieces)]
        self.out_shape = [jax.ShapeDtypeStruct((nslots,) + tuple(s), a.dtype) for s, a in zip(shapes, arrays)]
        self.scratch = [pltpu.SemaphoreType.DMA((n * nk,)), pltpu.SemaphoreType.DMA((n * nk,))]
        if keep_own:
            self.scratch += [pltpu.SemaphoreType.DMA((2 * n,))] + [pltpu.VMEM(s, a.dtype) for s, a in zip(shapes, arrays)]
        self.has_mid = False

    def _copies(self, ins, outs, scratch):
        send_sems, recv_sems = scratch[:2]
        me = _me()
        slot = {"chip": _chip_of(me), "dev": 4 * me[0] + 2 * me[1] + me[2], "core": me[2]}[self.slot_kind]
        nk = len(self.masks)

        def piece(a, dev):
            return ins[a] if self.pieces[a] is None else self.pieces[a][0](ins[a], dev)

        local = []
        if self.keep_own:
            local_sems, stages = scratch[2], scratch[3:]
            local = [_StagedCopy(piece(a, me), stages[a], outs[a].at[slot], local_sems.at[2 * a], local_sems.at[2 * a + 1])
                     for a in range(len(ins))]
        remote = []
        for a in range(len(ins)):
            for ki, mask in enumerate(self.masks):
                dev = _peer(mask)
                remote.append(_remote(piece(a, dev), outs[a].at[slot], send_sems.at[a * nk + ki],
                                      recv_sems.at[a * nk + ki], dev))
        return local, remote

    def start(self, ins, outs, scratch):
        local, remote = self._copies(ins, outs, scratch)
        for cp in remote + local:
            cp.start()

    def finish(self, ins, outs, scratch):
        local, remote = self._copies(ins, outs, scratch)
        for cp in remote + local:
            cp.wait()


class TwoLevelGather:
    def __init__(self, arrays):
        self.arrays = list(arrays)
        n, nk = len(arrays), len(CHIP_MASKS)
        self.out_shape = [jax.ShapeDtypeStruct((4,) + a.shape, a.dtype) for a in arrays]
        self.scratch = ([pltpu.SemaphoreType.DMA((n * nk,)) for _ in range(4)] + [pltpu.SemaphoreType.DMA((2 * n,))]
                        + [pltpu.VMEM(a.shape, a.dtype) for a in arrays])
        self.has_mid = True

    def _copies(self, ins, outs, scratch):
        ici_send, ici_recv, fwd_send, fwd_recv, local_sems = scratch[:5]
        stages = scratch[5:]
        me = _me()
        sibling = _peer(1)
        nk = len(CHIP_MASKS)
        local, ici, fwd = [], [], []
        for a in range(len(ins)):
            half = ins[a].shape[0] // 2
            mine = pl.ds(me[2] * half, half)
            local.append(_StagedCopy(ins[a], stages[a], outs[a].at[_chip_of(me)], local_sems.at[2 * a],
                                     local_sems.at[2 * a + 1]))
            for ki, mask in enumerate(CHIP_MASKS):
                dev = _peer(mask)
                k = a * nk + ki
                ici.append(_remote(ins[a].at[mine], outs[a].at[_chip_of(me), mine], ici_send.at[k], ici_recv.at[k], dev))
                arrived = outs[a].at[_chip_of(dev), mine]
                fwd.append(_remote(arrived, arrived, fwd_send.at[k], fwd_recv.at[k], sibling))
        return local, ici, fwd

    def start(self, ins, outs, scratch):
        local, ici, _ = self._copies(ins, outs, scratch)
        for cp in ici + local:
            cp.start()

    def mid(self, ins, outs, scratch):
        _, ici, fwd = self._copies(ins, outs, scratch)
        for arrival, onward in zip(ici, fwd):
            arrival.wait_recv()
            onward.start()

    def finish(self, ins, outs, scratch):
        local, ici, fwd = self._copies(ins, outs, scratch)
        for cp in fwd:
            cp.wait_recv()
        for cp in ici + fwd:
            cp.wait_send()
        for cp in local:
            cp.wait()


class Both:
    def __init__(self, a, b):
        self.a, self.b = a, b
        self.arrays, self.out_shape = a.arrays + b.arrays, a.out_shape + b.out_shape
        self.scratch = a.scratch + b.scratch
        self.has_mid = False
        assert not (a.has_mid or b.has_mid)

    def _parts(self, ins, outs, sems):
        na, sa = len(self.a.arrays), len(self.a.scratch)
        return (ins[:na], outs[:na], sems[:sa]), (ins[na:], outs[na:], sems[sa:])

    def start(self, ins, outs, sems):
        pa, pb = self._parts(ins, outs, sems)
        self.a.start(*pa)
        self.b.start(*pb)

    def finish(self, ins, outs, sems):
        pa, pb = self._parts(ins, outs, sems)
        self.a.finish(*pa)
        self.b.finish(*pb)


_ANY = pl.BlockSpec(memory_space=pl.ANY)


def run_comm(name, comm):
    n = len(comm.arrays)

    def body(*refs):
        ins, outs, sems = refs[:n], refs[n:2 * n], refs[2 * n:]
        comm.start(ins, outs, sems)
        if comm.has_mid:
            comm.mid(ins, outs, sems)
        comm.finish(ins, outs, sems)

    return pl.pallas_call(body, name=name, out_shape=comm.out_shape, in_specs=[_ANY] * n, out_specs=[_ANY] * n,
                          scratch_shapes=comm.scratch,
                          compiler_params=pltpu.CompilerParams(vmem_limit_bytes=VMEM_LIMIT))(*comm.arrays)


_HBM = pl.BlockSpec(memory_space=pltpu.HBM)
_SEM = pl.BlockSpec(memory_space=pltpu.SEMAPHORE)
_SIDE_EFFECT = pltpu.SideEffectType.DATAFLOW_SIDE_EFFECTING


def chip_exchange_start(pieces):
    shape = pieces.shape
    nk = len(CHIP_MASKS)

    def body(src_ref, land_ref, send_sems, recv_sems, src_thru, land_thru, token, stage, local_sems):
        me = _me()
        for ki, mask in enumerate(CHIP_MASKS):
            dev = _peer(mask)
            _remote(src_ref.at[_chip_of(dev)], land_ref.at[_chip_of(me)], send_sems.at[ki], recv_sems.at[ki], dev).start()
        own = _StagedCopy(src_ref.at[_chip_of(me)], stage, land_ref.at[_chip_of(me)], local_sems.at[0], local_sems.at[1])
        own.start()
        own.wait()
        token[...] = jnp.zeros_like(token)

    return pl.pallas_call(
        body, name="chip_exchange_start",
        out_shape=(pltpu.SemaphoreType.DMA((nk,)), pltpu.SemaphoreType.DMA((nk,)), pltpu.HBM(shape, pieces.dtype),
                   pltpu.HBM(shape, pieces.dtype), jax.ShapeDtypeStruct((8, LANES), F32)),
        in_specs=(_HBM, _HBM), out_specs=(_SEM, _SEM, _HBM, _HBM, pl.BlockSpec(memory_space=pltpu.VMEM)),
        input_output_aliases={0: 2, 1: 3},
        scratch_shapes=[pltpu.VMEM(shape[1:], pieces.dtype), pltpu.SemaphoreType.DMA((2,))],
        compiler_params=pltpu.CompilerParams(has_side_effects=_SIDE_EFFECT, vmem_limit_bytes=VMEM_LIMIT),
    )(pltpu.with_memory_space_constraint(pieces, pltpu.HBM),
      pltpu.with_memory_space_constraint(lax.empty(shape, pieces.dtype), pltpu.HBM))


def chip_exchange_wait(send_sems, recv_sems, src_thru, land_thru, after):
    nk = len(CHIP_MASKS)

    def body(src_ref, land_ref, send_sems, recv_sems, after_ref, src_dead, land_out):
        me = _me()
        for ki, mask in enumerate(CHIP_MASKS):
            dev = _peer(mask)
            copy = _remote(src_ref.at[_chip_of(dev)], land_ref.at[_chip_of(me)], send_sems.at[ki], recv_sems.at[ki], dev)
            copy.wait_send()
            copy.wait_recv()

    return pl.pallas_call(
        body, name="chip_exchange_wait",
        out_shape=(pltpu.HBM(src_thru.shape, src_thru.dtype), pltpu.HBM(land_thru.shape, land_thru.dtype)),
        in_specs=(_HBM, _HBM, _SEM, _SEM, pl.BlockSpec(memory_space=pl.ANY)), out_specs=(_HBM, _HBM),
        input_output_aliases={0: 0, 1: 1},
        compiler_params=pltpu.CompilerParams(has_side_effects=_SIDE_EFFECT),
    )(src_thru, land_thru, send_sems, recv_sems, after)[1]


def _call_with_comm(body, comm, steps, args, *, name, out_shape, in_specs, out_specs, grid, scratch=()):
    ni, no, ns, nc = len(in_specs), len(out_specs), len(scratch), len(comm.arrays)

    def full_body(*refs):
        ins, cins = refs[:ni], refs[ni:ni + nc]
        outs, couts = refs[ni + nc:ni + nc + no], refs[ni + nc + no:ni + 2 * nc + no]
        scr, csems = refs[ni + 2 * nc + no:ni + 2 * nc + no + ns], refs[ni + 2 * nc + no + ns:]
        first, middle, last = steps()
        pl.when(first)(lambda: comm.start(cins, couts, csems))
        if comm.has_mid:
            pl.when(middle)(lambda: comm.mid(cins, couts, csems))
        body(*ins, *outs, *scr)
        pl.when(last)(lambda: comm.finish(cins, couts, csems))

    res = pl.pallas_call(
        full_body, name=name, out_shape=list(out_shape) + comm.out_shape, grid=grid,
        in_specs=list(in_specs) + [_ANY] * nc, out_specs=list(out_specs) + [_ANY] * nc,
        scratch_shapes=list(scratch) + comm.scratch,
        compiler_params=pltpu.CompilerParams(dimension_semantics=("arbitrary",) * len(grid),
                                             vmem_limit_bytes=VMEM_LIMIT))(*args, *comm.arrays)
    return res[:no], res[no:]


def _shard_pieces(chip):
    if chip < 3:
        return [(0, SHARD_W, 8 * chip)]
    behind_dt = DT_STORED_START + SSM_HEADS - 3 * SHARD_W
    return [(0, behind_dt, 24), (behind_dt, SHARD_W - behind_dt, behind_dt + 24 + DT_PAD)]


W_IN_COLS = 256


def pack_w_in(wt):
    def body(w_ref, o_ref, pad_ref):
        chip = _chip_index()
        pad_ref[...] = jnp.zeros_like(pad_ref)
        for cv in range(4):
            @pl.when(chip == cv)
            def _():
                for src, n, dst in _shard_pieces(cv):
                    pad_ref[dst:dst + n, :] = w_ref[src:src + n, :]
        o_ref[...] = pad_ref[...].astype(BF16)

    return _call(body, name="pack_w_in", grid=(D_MODEL // W_IN_COLS,),
                 in_specs=[pl.BlockSpec((SHARD_W, W_IN_COLS), lambda i: (0, i))],
                 out_specs=pl.BlockSpec((PACK_W, W_IN_COLS), lambda i: (0, i)),
                 out_shape=jax.ShapeDtypeStruct((PACK_W, D_MODEL), BF16),
                 scratch=[pltpu.VMEM((PACK_W, W_IN_COLS), F32)], sem=("parallel",))(wt)


def _tile_runs():
    runs, fix = [], []
    for t in range(N_ALIGNED // LANES):
        s = min(t // 19, 3)
        j = t - 19 * s
        p = _act_col(t * LANES)
        if runs and runs[-1][1] == s and runs[-1][0] + runs[-1][3] == p and runs[-1][2] + runs[-1][3] == j * LANES:
            runs[-1][3] += LANES
        else:
            runs.append([p, s, j * LANES, LANES])
        if j == 0 and s > 0:
            fix.append((p, s - 1))
    return runs, fix


def unpack_w_in(bg):
    runs, fix = _tile_runs()

    def body(b_ref, o_ref):
        for p, s, j, w in runs:
            o_ref[p:p + w, :] = b_ref[s, j:j + w, :]
        for p, s in fix:
            o_ref[p:p + LANES, :] = o_ref[p:p + LANES, :] + b_ref[s, SHARD_STRIDE:PACK_W, :]
        o_ref[N_ALIGNED:N_ACT, :] = jnp.zeros((N_ACT - N_ALIGNED, W_IN_COLS), BF16)

    return _call(body, name="unpack_w_in", grid=(D_MODEL // W_IN_COLS,),
                 in_specs=[pl.BlockSpec((4, PACK_W, W_IN_COLS), lambda i: (0, 0, i))],
                 out_specs=pl.BlockSpec((N_ACT, W_IN_COLS), lambda i: (0, i)),
                 out_shape=jax.ShapeDtypeStruct((N_ACT, D_MODEL), BF16), sem=("parallel",))(bg)


def pack_grad_w_in(dwt):
    def body(g_ref, o_ref):
        for s in range(4):
            for j in range(PACK_W // LANES):
                p = _act_col((19 * s + j) * LANES)
                o_ref[s, j * LANES:(j + 1) * LANES, :] = g_ref[p:p + LANES, :]

    return _call(body, name="pack_grad_w_in", grid=(D_MODEL // W_IN_COLS,),
                 in_specs=[pl.BlockSpec((N_ACT, W_IN_COLS), lambda i: (0, i))],
                 out_specs=pl.BlockSpec((4, PACK_W, W_IN_COLS), lambda i: (0, 0, i)),
                 out_shape=jax.ShapeDtypeStruct((4, PACK_W, D_MODEL), BF16), sem=("parallel",))(dwt)


def _adamw(w, g, m, v):
    m = ADAM_B1 * m + (1.0 - ADAM_B1) * g
    v = ADAM_B2 * v + (1.0 - ADAM_B2) * jnp.square(g)
    m_hat = m / (1.0 - ADAM_B1 ** ADAM_STEP)
    v_hat = v / (1.0 - ADAM_B2 ** ADAM_STEP)
    delta = -ADAM_LR * (m_hat / (jnp.sqrt(v_hat) + ADAM_EPS) + ADAM_WD * w)
    return delta, m, v


def adamw_w_in(g_packed, wt, mt, vt):
    cols = LANES

    def body(g_ref, w_ref, m_ref, v_ref, go_ref, d_ref, mo_ref, vo_ref):
        chip = _chip_index()
        for cv in range(4):
            @pl.when(chip == cv)
            def _():
                for dst, n, src in _shard_pieces(cv):
                    go_ref[dst:dst + n, :] = g_ref[src:src + n, :]
        d_ref[...], mo_ref[...], vo_ref[...] = _adamw(w_ref[...], go_ref[...], m_ref[...], v_ref[...])

    spec = pl.BlockSpec((SHARD_W, cols), lambda i: (0, i))
    shp = jax.ShapeDtypeStruct((SHARD_W, D_MODEL), F32)
    return _call(body, name="adamw_w_in", grid=(D_MODEL // cols,),
                 in_specs=[pl.BlockSpec((PACK_W, cols), lambda i: (0, i)), spec, spec, spec],
                 out_specs=[spec] * 4, out_shape=[shp] * 4, sem=("parallel",))(g_packed, wt, mt, vt)


def adamw_rows(name, g, w, m, v):
    r, c = g.shape
    rows = min(r, BLOCK)

    def body(g_ref, w_ref, m_ref, v_ref, d_ref, mo_ref, vo_ref):
        d_ref[...], mo_ref[...], vo_ref[...] = _adamw(w_ref[...], g_ref[...], m_ref[...], v_ref[...])

    spec = pl.BlockSpec((rows, c), lambda i: (i, 0))
    shp = jax.ShapeDtypeStruct((r, c), F32)
    return _call(body, name=name, grid=(r // rows,), in_specs=[spec] * 4, out_specs=[spec] * 3, out_shape=[shp] * 3,
                 sem=("parallel",))(g, w, m, v)


def adamw_small(gs, ws, ms, vs):
    n = len(gs)

    def body(*refs):
        g, w, m, v = refs[:n], refs[n:2 * n], refs[2 * n:3 * n], refs[3 * n:4 * n]
        outs = refs[4 * n:]
        for i in range(n):
            d, mn, vn = _adamw(w[i][...], g[i][...], m[i][...], v[i][...])
            outs[3 * i][...] = d
            outs[3 * i + 1][...] = mn
            outs[3 * i + 2][...] = vn

    specs = [_full(a.shape) for a in gs]
    res = _call(body, name="adamw_small", in_specs=specs * 4,
                out_specs=[s for s in specs for _ in range(3)],
                out_shape=[jax.ShapeDtypeStruct(a.shape, F32) for a in gs for _ in range(3)])(*gs, *ws, *ms, *vs)
    return [tuple(res[3 * i:3 * i + 3]) for i in range(n)]


def sum_slots(name, r):
    s, rr, c = r.shape
    rows = min(rr, BLOCK)

    def body(r_ref, o_ref):
        acc = r_ref[0].astype(F32)
        for k in range(1, s):
            acc = acc + r_ref[k].astype(F32)
        o_ref[...] = acc

    return _call(body, name=name, grid=(rr // rows,), in_specs=[pl.BlockSpec((s, rows, c), lambda i: (0, i, 0))],
                 out_specs=pl.BlockSpec((rows, c), lambda i: (i, 0)), out_shape=jax.ShapeDtypeStruct((rr, c), F32),
                 sem=("parallel",))(r)


def sum_pair(partial, from_sibling):
    s, _, rr, cols = partial.shape
    rows = rr // 2

    def body(p_ref, r_ref, o_ref):
        c = lax.axis_index("c")
        o_ref[...] = (p_ref[c].astype(F32) + r_ref[1 - c].astype(F32)).astype(BF16)

    return _call(body, name="sum_pair", grid=(s, rr // rows),
                 in_specs=[pl.BlockSpec((None, 2, rows, cols), lambda k, i: (k, 0, i, 0)),
                           pl.BlockSpec((2, None, rows, cols), lambda k, i: (0, k, i, 0))],
                 out_specs=pl.BlockSpec((None, rows, cols), lambda k, i: (k, i, 0)),
                 out_shape=jax.ShapeDtypeStruct((s, rr, cols), BF16), sem=("parallel", "parallel"))(partial, from_sibling)


def _col_tile(n, k):
    if n % 896 == 0 and k <= 1024:
        return 896
    return min(n, 512)


def project(name, x, wt, comm):
    m, k = x.shape
    n = wt.shape[0]
    tn = D_MODEL
    steps = n // tn

    def body(x_ref, w_ref, o_ref):
        o_ref[...] = lax.dot_general(x_ref[...], w_ref[...], _NT, preferred_element_type=F32)

    def at():
        j = pl.program_id(0)
        return j == 0, j == (3 * steps) // 4, j == steps - 1

    (res,), moved = _call_with_comm(
        body, comm, at, (x, wt), name=name, grid=(steps,),
        in_specs=[_full((m, k)), pl.BlockSpec((tn, k), lambda j: (j, 0))],
        out_specs=[pl.BlockSpec((m, tn), lambda j: (0, j))], out_shape=[jax.ShapeDtypeStruct((m, n), F32)])
    return res, moved


def _piece_tiles(pieces):
    spans, start = [], 0
    for p in pieces:
        spans.append((start, p.shape[1] // D_MODEL))
        start += p.shape[1] // D_MODEL
    return spans, start


def _piece_spec(tm, span, rows_of, tile_of):
    first, count = span

    def index(i, j):
        t = tile_of(i, j) - first
        mine = (t >= 0) & (t < count)
        return jnp.where(mine, rows_of(i, j), 0), jnp.clip(t, 0, count - 1)

    return pl.BlockSpec((tm, D_MODEL), index)


def project_back(name, pieces, wt, after):
    m = pieces[0].shape[0]
    k = wt.shape[1]
    tm = m // 2
    spans, steps = _piece_tiles(pieces)

    def body(*refs):
        w_ref, o_ref = refs[len(pieces)], refs[len(pieces) + 2]
        j = pl.program_id(1)

        @pl.when(j == 0)
        def _():
            o_ref[...] = jnp.zeros_like(o_ref)

        for dy_ref, (first, count) in zip(refs, spans):
            @pl.when((j >= first) & (j < first + count))
            def _():
                o_ref[...] += jnp.dot(dy_ref[...], w_ref[...], preferred_element_type=F32)

    return _call(body, name=name, grid=(m // tm, steps),
                 in_specs=[_piece_spec(tm, s, lambda i, j: i, lambda i, j: j) for s in spans]
                 + [pl.BlockSpec((D_MODEL, k), lambda i, j: (j, 0)), _full(after.shape)],
                 out_specs=pl.BlockSpec((tm, k), lambda i, j: (i, 0)), out_shape=jax.ShapeDtypeStruct((m, k), F32),
                 sem=("parallel", "arbitrary"))(*pieces, wt, after)


def weight_grad_t(name, pieces, x):
    m = pieces[0].shape[0]
    k = x.shape[1]
    tm = m // 2
    spans, steps = _piece_tiles(pieces)

    def body(*refs):
        x_ref, o_ref, acc_ref = refs[len(pieces):]
        i, half = pl.program_id(0), pl.program_id(1)
        for dy_ref, (first, count) in zip(refs, spans):
            @pl.when((i >= first) & (i < first + count))
            def _():
                part = lax.dot_general(dy_ref[...], x_ref[...], (((0,), (0,)), ((), ())), preferred_element_type=F32)

                @pl.when(half == 0)
                def _():
                    acc_ref[...] = part

                @pl.when(half == 1)
                def _():
                    o_ref[...] = (acc_ref[...] + part).astype(BF16)

    return _call(body, name=name, grid=(steps, 2),
                 in_specs=[_piece_spec(tm, s, lambda i, j: j, lambda i, j: i) for s in spans]
                 + [pl.BlockSpec((tm, k), lambda i, j: (j, 0))],
                 out_specs=pl.BlockSpec((D_MODEL, k), lambda i, j: (i, 0)),
                 out_shape=jax.ShapeDtypeStruct((steps * D_MODEL, k), BF16),
                 scratch=[pltpu.VMEM((D_MODEL, k), F32)], sem=("parallel", "arbitrary"))(*pieces, x)


def mm_tn(name, x, dy):
    m, k = x.shape
    n = dy.shape[1]
    tm = m // 2
    tn = _col_tile(n, k)

    def body(x_ref, dy_ref, o_ref, acc_ref):
        part = lax.dot_general(x_ref[...].astype(BF16), dy_ref[...].astype(BF16), (((0,), (0,)), ((), ())),
                               preferred_element_type=F32)

        @pl.when(pl.program_id(1) == 0)
        def _():
            acc_ref[...] = part

        @pl.when(pl.program_id(1) == 1)
        def _():
            o_ref[...] = (acc_ref[...] + part).astype(BF16)

    return _call(body, name=name, grid=(n // tn, 2),
                 in_specs=[pl.BlockSpec((tm, k), lambda i, j: (j, 0)), pl.BlockSpec((tm, tn), lambda i, j: (j, i))],
                 out_specs=pl.BlockSpec((k, tn), lambda i, j: (0, i)), out_shape=jax.ShapeDtypeStruct((k, n), BF16),
                 scratch=[pltpu.VMEM((k, tn), F32)], sem=("parallel", "arbitrary"))(x, dy)


def _row_spec(width, col_block=0):
    return pl.BlockSpec((BLOCK, width), lambda i: (i, col_block))


def _x_spec():
    return pl.BlockSpec((None, BLOCK, D_MODEL), lambda i: (0, jnp.maximum(i - 1, 0), 0))


def prep(x, meta, g_pre):
    nb = x.shape[1] // BLOCK + 1

    def body(x_ref, meta_ref, g_ref, h_ref, u_ref):
        i = pl.program_id(0)

        @pl.when(i == 0)
        def _():
            h_ref[0:PAD_ROWS, :] = jnp.zeros((PAD_ROWS, D_MODEL), F32)
            h_ref[PAD_ROWS:BLOCK, :] = meta_ref[...]

        @pl.when(i > 0)
        def _():
            h_ref[...] = x_ref[...]

        u_ref[...] = _rms(h_ref[...], g_ref[...]).astype(BF16)

    return _call(body, name="prep", grid=(nb,), in_specs=[_x_spec(), _full((N_META, D_MODEL)), _full((1, D_MODEL))],
                 out_specs=[_row_spec(D_MODEL), _row_spec(D_MODEL)],
                 out_shape=[jax.ShapeDtypeStruct((nb * BLOCK, D_MODEL), F32),
                            jax.ShapeDtypeStruct((nb * BLOCK, D_MODEL), BF16)], sem=("parallel",))(x, meta, g_pre)


def prep_bwd(h, du, dres, g_pre):
    nb = h.shape[0] // BLOCK

    def body(h_ref, du_ref, dres_ref, g_ref, gx_ref, gm_ref, gg_ref):
        i = pl.program_id(0)
        _, vjp = jax.vjp(_rms, h_ref[...], g_ref[...])
        dh, dg = vjp(du_ref[...])

        @pl.when(i == 0)
        def _():
            gm_ref[...] = dh[PAD_ROWS:BLOCK, :]
            gg_ref[...] = dg

        @pl.when(i > 0)
        def _():
            gg_ref[...] += dg

        gx_ref[...] = dh + dres_ref[...]

    return _call(body, name="prep_bwd", grid=(nb,),
                 in_specs=[_row_spec(D_MODEL), _row_spec(D_MODEL), _row_spec(D_MODEL), _full((1, D_MODEL))],
                 out_specs=[_x_spec(), _full((N_META, D_MODEL)), _full((1, D_MODEL))],
                 out_shape=[jax.ShapeDtypeStruct((1, (nb - 1) * BLOCK, D_MODEL), F32),
                            jax.ShapeDtypeStruct((N_META, D_MODEL), F32), jax.ShapeDtypeStruct((1, D_MODEL), F32)],
                 sem=("arbitrary",))(h, du, dres, g_pre)


GROUP_W = SSM_INNER // SSM_GROUPS


def _gated_norm(y, z, g):
    t = y * _silu(z)
    return t * lax.rsqrt(jnp.mean(t * t, axis=-1, keepdims=True) + NORM_EPS) * g


def _gated_norm_groups(y, z, g):
    groups = [slice(k * GROUP_W, (k + 1) * GROUP_W) for k in range(SSM_GROUPS)]
    return jnp.concatenate([_gated_norm(y[:, s], z[:, s], g[:, s]) for s in groups], axis=1)


def _merge(ga, gs, ya, ys):
    return _sigmoid(ga) * ya + _sigmoid(gs) * ys


GATE_ATT_BLOCK = SEG["gate_att"][2] // D_MODEL
GATE_SSM_BLOCK = SEG["gate_ssm"][2] // D_MODEL


def _row_loss(out, g_post, x, target):
    diff = x + _rms(out, g_post) - target
    return 0.5 * jnp.sum(diff * diff) / D_MODEL


def tail(y_ssd, proj, a_att, x, target, woa, wos, wo, g_norm, g_post):
    nb = y_ssd.shape[0] // BLOCK
    rows = nb * BLOCK

    def body(y_ref, z_ref, ga_ref, gs_ref, a_ref, x_ref, t_ref, woa_ref, wos_ref, wo_ref, gn_ref, gp_ref,
             yn_ref, mg_ref, dout_ref, dya_ref, dys_ref, da_ref, dy_ref, dz_ref, dga_ref, dgs_ref, dres_ref,
             loss_ref, dgp_ref, dgn_ref):
        i = pl.program_id(0)
        yn, norm_vjp = jax.vjp(_gated_norm_groups, y_ref[...], z_ref[...], gn_ref[...])
        yn16 = yn.astype(BF16)
        y_ssm = jnp.dot(yn16, wos_ref[...], preferred_element_type=F32)
        y_att = jnp.dot(a_ref[...], woa_ref[...], preferred_element_type=F32)
        merged, merge_vjp = jax.vjp(_merge, ga_ref[...], gs_ref[...], y_att, y_ssm)
        merged16 = merged.astype(BF16)
        out = jnp.dot(merged16, wo_ref[...], preferred_element_type=F32)
        loss, loss_vjp = jax.vjp(_row_loss, out, gp_ref[...], x_ref[...], t_ref[...])
        counted = jnp.where(i > 0, 1.0, 0.0)
        dout, dgp, dres, _ = loss_vjp(counted)
        dout16 = dout.astype(BF16)
        dmerged = lax.dot_general(dout16, wo_ref[...], _NT, preferred_element_type=F32)
        dga, dgs, dya, dys = merge_vjp(dmerged)
        dya16, dys16 = dya.astype(BF16), dys.astype(BF16)
        dyn = lax.dot_general(dys16, wos_ref[...], _NT, preferred_element_type=F32)
        dy, dz, dgn = norm_vjp(dyn)

        yn_ref[...] = yn16
        mg_ref[...] = merged16
        dout_ref[...] = dout16
        dya_ref[...] = dya16
        dys_ref[...] = dys16
        da_ref[...] = lax.dot_general(dya16, woa_ref[...], _NT, preferred_element_type=F32)
        dy_ref[...] = dy
        dz_ref[...] = dz.astype(BF16)
        dga_ref[...] = dga.astype(BF16)
        dgs_ref[...] = dgs.astype(BF16)
        dres_ref[...] = dres

        @pl.when(i == 0)
        def _():
            loss_ref[...] = jnp.zeros_like(loss_ref)
            dgp_ref[...] = jnp.zeros_like(dgp_ref)
            dgn_ref[...] = jnp.zeros_like(dgn_ref)

        loss_ref[...] += loss * counted
        dgp_ref[...] += dgp
        dgn_ref[...] += dgn

    wide, narrow = _row_spec(SSM_INNER), _row_spec(D_MODEL)
    resident = pl.BlockSpec(memory_space=pltpu.VMEM)
    bf = lambda w: jax.ShapeDtypeStruct((rows, w), BF16)
    f32 = lambda w: jax.ShapeDtypeStruct((rows, w), F32)
    return _call(body, name="tail", grid=(nb,),
                 in_specs=[wide, wide, _row_spec(D_MODEL, GATE_ATT_BLOCK), _row_spec(D_MODEL, GATE_SSM_BLOCK), narrow,
                           _x_spec(), _x_spec(), resident, resident, resident, _full((1, SSM_INNER)),
                           _full((1, D_MODEL))],
                 out_specs=[wide, narrow, narrow, narrow, narrow, narrow, wide, wide, narrow, narrow, narrow,
                            _full((8, LANES)), _full((1, D_MODEL)), _full((1, SSM_INNER))],
                 out_shape=[bf(SSM_INNER), bf(D_MODEL), bf(D_MODEL), bf(D_MODEL), bf(D_MODEL), f32(D_MODEL),
                            f32(SSM_INNER), bf(SSM_INNER), bf(D_MODEL), bf(D_MODEL), f32(D_MODEL),
                            jax.ShapeDtypeStruct((8, LANES), F32), jax.ShapeDtypeStruct((1, D_MODEL), F32),
                            jax.ShapeDtypeStruct((1, SSM_INNER), F32)],
                 sem=("arbitrary",))(y_ssd, proj, proj, proj, a_att, x, target, woa, wos, wo, g_norm, g_post)


_NT = (((1,), (1,)), ((), ()))
ALIBI_SLOPES = tuple(2.0 ** (-8.0 * (h + 1) / ATT_Q_HEADS) for h in range(ATT_Q_HEADS))
KV_WIDTH = ATT_KV_HEADS * HEAD_DIM
Q_BLOCK = SEG["q"][2] // D_MODEL
Z_ATT_BLOCK = SEG["z_att"][2] // D_MODEL
K_BLOCK = SEG["k"][2] // KV_WIDTH
V_BLOCK = SEG["v"][2] // KV_WIDTH
META_ROW_BLOCK = PAD_ROWS // N_META


@jax.custom_vjp
def _swap_halves(x):
    return pltpu.roll(x, HEAD_DIM, 1)


_swap_halves.defvjp(lambda x: (pltpu.roll(x, HEAD_DIM, 1), None), lambda _, g: (pltpu.roll(g, HEAD_DIM, 1),))


def _both_halves(t, half):
    first = lax.broadcasted_iota(jnp.int32, t.shape, 1) < HEAD_DIM
    sw = _swap_halves(t)
    return jnp.where(first, t, sw) if half == 0 else jnp.where(first, sw, t)


def _attn_rows(q, z, kp, kc, vp, vc, km, vm, sinks, n):
    rows = ATT_GROUP * BLOCK
    i = lax.broadcasted_iota(jnp.int32, (rows, BLOCK), 0) & (BLOCK - 1)
    j = lax.broadcasted_iota(jnp.int32, (rows, BLOCK), 1)
    rel_c = (i - j).astype(F32)
    rel_p = rel_c + float(BLOCK)
    nv = jnp.zeros((rows, BLOCK), jnp.int32) + n
    ok_c = (i >= j) & (nv >= 1)
    ok_p = (j > i) & (nv >= 2)
    im = lax.broadcasted_iota(jnp.int32, (rows, N_META), 0) & (BLOCK - 1)
    jm = lax.broadcasted_iota(jnp.int32, (rows, N_META), 1)
    ok_m = ((jnp.zeros((rows, N_META), jnp.int32) + n) >= 1) | (im >= PAD_ROWS + jm)
    first = lax.broadcasted_iota(jnp.int32, (BLOCK, LANES), 1) < HEAD_DIM
    neg = -jnp.inf
    outs = []
    for kv in range(ATT_KV_HEADS):
        tile, half = divmod(kv, 2)
        lanes = slice(tile * LANES, (tile + 1) * LANES)
        kc2, kp2, km2 = (_both_halves(t[:, lanes], half).astype(BF16) for t in (kc, kp, km))
        vc2, vp2, vm2 = (_both_halves(t[:, lanes], half).astype(BF16) for t in (vc, vp, vm))
        qs, slope, sk = [], [], []
        for pair in range(ATT_GROUP // 2):
            c0 = (kv * ATT_GROUP + 2 * pair) * HEAD_DIM
            qp = q[:, c0:c0 + LANES] * HEAD_DIM ** -0.5
            qs += [jnp.where(first, qp, 0.0), jnp.where(first, 0.0, qp)]
        for g in range(ATT_GROUP):
            slope.append(jnp.full((BLOCK, 1), ALIBI_SLOPES[kv * ATT_GROUP + g], F32))
            sk.append(jnp.broadcast_to(sinks[kv * ATT_GROUP + g], (BLOCK, 1)))
        qs = jnp.concatenate(qs, axis=0).astype(BF16)
        slope = jnp.concatenate(slope, axis=0)
        sk = jnp.concatenate(sk, axis=0)
        sc = jnp.where(ok_c, lax.dot_general(qs, kc2, _NT, preferred_element_type=F32) - slope * rel_c, neg)
        sp = jnp.where(ok_p, lax.dot_general(qs, kp2, _NT, preferred_element_type=F32) - slope * rel_p, neg)
        sm = jnp.where(ok_m, lax.dot_general(qs, km2, _NT, preferred_element_type=F32), neg)
        mx = jnp.maximum(jnp.maximum(jnp.max(sc, axis=1, keepdims=True), jnp.max(sp, axis=1, keepdims=True)),
                         jnp.maximum(jnp.max(sm, axis=1, keepdims=True), sk))
        mx = lax.stop_gradient(mx)
        ec, ep, em, es = jnp.exp(sc - mx), jnp.exp(sp - mx), jnp.exp(sm - mx), jnp.exp(sk - mx)
        den = (es + jnp.sum(ec, axis=1, keepdims=True) + jnp.sum(ep, axis=1, keepdims=True)
               + jnp.sum(em, axis=1, keepdims=True))
        inv = 1.0 / den
        o = (jnp.dot((ec * inv).astype(BF16), vc2, preferred_element_type=F32)
             + jnp.dot((ep * inv).astype(BF16), vp2, preferred_element_type=F32)
             + jnp.dot((em * inv).astype(BF16), vm2, preferred_element_type=F32))
        for pair in range(ATT_GROUP // 2):
            r0 = 2 * pair * BLOCK
            outs.append(jnp.where(first, o[r0:r0 + BLOCK], o[r0 + BLOCK:r0 + 2 * BLOCK]))
    return jnp.concatenate(outs, axis=1) * _silu(z)


def _attn_specs(nb, steps_clamped):
    def blk(t):
        return jnp.minimum(t, nb - 1) if steps_clamped else t

    wide = lambda col: pl.BlockSpec((BLOCK, D_MODEL), lambda t: (blk(t), col))
    cur = lambda col: pl.BlockSpec((BLOCK, KV_WIDTH), lambda t: (blk(t), col))
    prev = lambda col: pl.BlockSpec((BLOCK, KV_WIDTH), lambda t: (jnp.maximum(blk(t) - 1, 0), col))
    meta = lambda col: pl.BlockSpec((N_META, KV_WIDTH), lambda t: (META_ROW_BLOCK, col))
    sinks = pl.BlockSpec((ATT_Q_HEADS, 1, 1), lambda t: (0, 0, 0))
    return [wide(Q_BLOCK), wide(Z_ATT_BLOCK), prev(K_BLOCK), cur(K_BLOCK), prev(V_BLOCK), cur(V_BLOCK),
            meta(K_BLOCK), meta(V_BLOCK), sinks]


def attn_fwd(proj, sinks):
    nb = proj.shape[0] // BLOCK

    def body(q_ref, z_ref, kp_ref, kc_ref, vp_ref, vc_ref, km_ref, vm_ref, sk_ref, o_ref):
        o_ref[...] = _attn_rows(q_ref[...], z_ref[...], kp_ref[...], kc_ref[...], vp_ref[...], vc_ref[...],
                                km_ref[...], vm_ref[...], tuple(sk_ref[h] for h in range(ATT_Q_HEADS)),
                                pl.program_id(0)).astype(BF16)

    return _call(body, name="attn_fwd", grid=(nb,), in_specs=_attn_specs(nb, False), out_specs=_row_spec(D_MODEL),
                 out_shape=jax.ShapeDtypeStruct((nb * BLOCK, D_MODEL), BF16), sem=("parallel",))(*([proj] * 8), sinks)


def attn_bwd(da, proj, sinks):
    nb = proj.shape[0] // BLOCK
    last = nb - 1
    wide = pl.BlockSpec((BLOCK, D_MODEL), lambda t: (jnp.minimum(t, last), 0))
    done = pl.BlockSpec((BLOCK, KV_WIDTH), lambda t: (jnp.maximum(t - 1, 0), 0))
    meta = _full((N_META, KV_WIDTH))
    par = _full((ATT_Q_HEADS, 1, 1))

    def body(da_ref, q_ref, z_ref, kp_ref, kc_ref, vp_ref, vc_ref, km_ref, vm_ref, sk_ref,
             dq_ref, dz_ref, dk_ref, dv_ref, dkm_ref, dvm_ref, dsk_ref, ck_ref, cv_ref):
        t = pl.program_id(0)

        @pl.when(t == 0)
        def _():
            ck_ref[...] = jnp.zeros_like(ck_ref)
            cv_ref[...] = jnp.zeros_like(cv_ref)
            dkm_ref[...] = jnp.zeros_like(dkm_ref)
            dvm_ref[...] = jnp.zeros_like(dvm_ref)
            dsk_ref[...] = jnp.zeros_like(dsk_ref)

        @pl.when(t < nb)
        def _():
            def f(q, z, kp, kc, vp, vc, km, vm, sk):
                return _attn_rows(q, z, kp, kc, vp, vc, km, vm, sk, t)

            _, vjp = jax.vjp(f, q_ref[...], z_ref[...], kp_ref[...], kc_ref[...], vp_ref[...], vc_ref[...],
                             km_ref[...], vm_ref[...], tuple(sk_ref[h] for h in range(ATT_Q_HEADS)))
            dq, dz, dkp, dkc, dvp, dvc, dkm, dvm, dsk = vjp(da_ref[...])
            dq_ref[...] = dq.astype(BF16)
            dz_ref[...] = dz.astype(BF16)
            for h in range(ATT_Q_HEADS):
                dsk_ref[h] += dsk[h]
            dk_ref[...] = ck_ref[...] + dkp
            dv_ref[...] = cv_ref[...] + dvp
            ck_ref[...] = dkc
            cv_ref[...] = dvc
            dkm_ref[...] += dkm
            dvm_ref[...] += dvm

        @pl.when(t == nb)
        def _():
            dk_ref[...] = ck_ref[...]
            dv_ref[...] = cv_ref[...]

    rows = nb * BLOCK
    return _call(body, name="attn_bwd", grid=(nb + 1,), in_specs=[wide] + _attn_specs(nb, True),
                 out_specs=[wide, wide, done, done, meta, meta, par],
                 out_shape=[jax.ShapeDtypeStruct((rows, D_MODEL), BF16), jax.ShapeDtypeStruct((rows, D_MODEL), BF16),
                            jax.ShapeDtypeStruct((rows, KV_WIDTH), F32), jax.ShapeDtypeStruct((rows, KV_WIDTH), F32),
                            jax.ShapeDtypeStruct((N_META, KV_WIDTH), F32), jax.ShapeDtypeStruct((N_META, KV_WIDTH), F32),
                            jax.ShapeDtypeStruct(sinks.shape, F32)],
                 scratch=[pltpu.VMEM((BLOCK, KV_WIDTH), F32), pltpu.VMEM((BLOCK, KV_WIDTH), F32)],
                 sem=("arbitrary",))(da, *([proj] * 8), sinks)


XBC_BLOCK0 = SEG["xbc"][2] // D_MODEL
CONV_COL_BLOCKS = CONV_DIM // D_MODEL
DT_TILE = SEG["dt"][2] // LANES


HALO = 8


def _conv_rows(length):
    return 544 if length % 544 == 0 else BLOCK


def _shift_rows(cur, before, j):
    if j == 0:
        return cur
    n = cur.shape[0]
    row = lax.broadcasted_iota(jnp.int32, cur.shape, 0)
    head = pltpu.roll(before, j, 0)
    if n > HALO:
        head = jnp.concatenate([head, jnp.zeros((n - HALO, cur.shape[1]), cur.dtype)], axis=0)
    return jnp.where(row >= j, pltpu.roll(cur, j, 0), head)


def _conv_pre(cur, before, w_ref, b_ref):
    pre = b_ref[...] + w_ref[CONV_WIDTH - 1:CONV_WIDTH, :] * cur
    for k in range(CONV_WIDTH - 1):
        pre = pre + w_ref[k:k + 1, :] * _shift_rows(cur, before, CONV_WIDTH - 1 - k)
    return pre


def _conv_specs(steps, rows, col0=0):
    first = XBC_BLOCK0 + col0
    halos = rows // HALO
    cur = pl.BlockSpec((rows, D_MODEL), lambda j, i: (i, first + j))
    before = pl.BlockSpec((HALO, D_MODEL), lambda j, i: (jnp.maximum(i * halos - 1, 0), first + j))
    after = pl.BlockSpec((HALO, D_MODEL), lambda j, i: (jnp.minimum(i + 1, steps - 1) * halos, first + j))
    return cur, before, after


def _valid_rows(i, rows):
    row = lax.broadcasted_iota(jnp.int32, (rows, D_MODEL), 0)
    return jnp.maximum((row >= PAD_ROWS).astype(F32), jnp.where(i > 0, 1.0, 0.0))


def conv_fwd(proj, conv_w, conv_b):
    rows = _conv_rows(proj.shape[0])
    steps = proj.shape[0] // rows
    cur, before, _ = _conv_specs(steps, rows)

    def body(c_ref, p_ref, w_ref, b_ref, o_ref):
        i = pl.program_id(1)
        pre = _conv_pre(c_ref[...], p_ref[...] * jnp.where(i > 0, 1.0, 0.0), w_ref, b_ref)
        o_ref[...] = _silu(pre) * _valid_rows(i, rows)

    return _call(body, name="conv_fwd", grid=(CONV_COL_BLOCKS, steps),
                 in_specs=[cur, before, pl.BlockSpec((CONV_WIDTH, D_MODEL), lambda j, i: (0, j)),
                           pl.BlockSpec((1, D_MODEL), lambda j, i: (0, j))],
                 out_specs=pl.BlockSpec((rows, D_MODEL), lambda j, i: (i, j)),
                 out_shape=jax.ShapeDtypeStruct((proj.shape[0], CONV_DIM), F32),
                 sem=("parallel", "parallel"))(proj, proj, conv_w, conv_b)


def conv_bwd(name, dparts, col0, proj, conv_w, conv_b):
    rows = _conv_rows(proj.shape[0])
    steps = proj.shape[0] // rows
    last = steps - 1
    ncol = sum(d.shape[1] for d in dparts) // D_MODEL
    np_ = len(dparts)
    cur, before, after = _conv_specs(steps, rows, col0)
    dcur = [pl.BlockSpec((rows, d.shape[1] // ncol), lambda j, i: (i, j)) for d in dparts]
    dafter = [pl.BlockSpec((HALO, d.shape[1] // ncol), lambda j, i: (jnp.minimum(i + 1, last) * (rows // HALO), j))
              for d in dparts]
    out_cur = pl.BlockSpec((rows, D_MODEL), lambda j, i: (i, j))
    wspec = pl.BlockSpec((CONV_WIDTH, D_MODEL), lambda j, i: (0, col0 + j))
    bspec = pl.BlockSpec((1, D_MODEL), lambda j, i: (0, col0 + j))
    wout = pl.BlockSpec((CONV_WIDTH, D_MODEL), lambda j, i: (0, j))
    bout = pl.BlockSpec((1, D_MODEL), lambda j, i: (0, j))

    def body(*refs):
        dc_refs, da_refs = refs[:np_], refs[np_:2 * np_]
        c_ref, p_ref, a_ref, w_ref, b_ref, du_ref, dw_ref, db_ref = refs[2 * np_:]
        i = pl.program_id(1)
        row = lax.broadcasted_iota(jnp.int32, (rows, D_MODEL), 0)
        curv = c_ref[...]
        beforev = p_ref[...] * jnp.where(i > 0, 1.0, 0.0)
        side_by_side = lambda rs: rs[0][...] if np_ == 1 else jnp.concatenate([r[...] for r in rs], axis=1)

        def dpre_of(pre, d):
            s = _sigmoid(pre)
            return d * (s * (1.0 + pre * (1.0 - s)))

        dp_c = dpre_of(_conv_pre(curv, beforev, w_ref, b_ref), side_by_side(dc_refs) * _valid_rows(i, rows))
        dp_a = dpre_of(_conv_pre(a_ref[...], curv[rows - HALO:], w_ref, b_ref),
                       side_by_side(da_refs) * jnp.where(i < last, 1.0, 0.0))
        du = w_ref[CONV_WIDTH - 1:CONV_WIDTH, :] * dp_c
        for j in range(1, CONV_WIDTH):
            tail = jnp.concatenate([jnp.zeros((rows - HALO, D_MODEL), F32), pltpu.roll(dp_a, HALO - j, 0)], axis=0)
            up = jnp.where(row < rows - j, pltpu.roll(dp_c, rows - j, 0), tail)
            du = du + w_ref[CONV_WIDTH - 1 - j:CONV_WIDTH - j, :] * up
        du_ref[...] = du.astype(BF16)

        @pl.when(i == 0)
        def _():
            dw_ref[...] = jnp.zeros_like(dw_ref)
            db_ref[...] = jnp.zeros_like(db_ref)

        for k in range(CONV_WIDTH):
            dw_ref[k:k + 1, :] += jnp.sum(dp_c * _shift_rows(curv, beforev, CONV_WIDTH - 1 - k), axis=0, keepdims=True)
        db_ref[...] += jnp.sum(dp_c, axis=0, keepdims=True)

    width = ncol * D_MODEL
    return _call(body, name=name, grid=(ncol, steps),
                 in_specs=dcur + dafter + [cur, before, after, wspec, bspec], out_specs=[out_cur, wout, bout],
                 out_shape=[jax.ShapeDtypeStruct((proj.shape[0], width), BF16),
                            jax.ShapeDtypeStruct((CONV_WIDTH, width), F32), jax.ShapeDtypeStruct((1, width), F32)],
                 sem=("parallel", "arbitrary"))(*dparts, *dparts, proj, proj, proj, conv_w, conv_b)


def _head_expand():
    e = np.zeros((LANES, SSM_INNER), np.float32)
    for h in range(SSM_HEADS):
        e[h, h * HEAD_DIM:(h + 1) * HEAD_DIM] = 1.0
    return jnp.asarray(e, dtype=BF16)


def _softplus(x):
    return jnp.maximum(x, 0.0) + jnp.log(1.0 + jnp.exp(-jnp.abs(x)))


def _bf16_parts(x):
    hi = x.astype(BF16)
    rest = x - hi.astype(F32)
    mid = rest.astype(BF16)
    return hi, mid, (rest - mid.astype(F32)).astype(BF16)


@jax.custom_vjp
def _times_01(x, m):
    return sum(jnp.dot(p, m, preferred_element_type=F32) for p in _bf16_parts(x))


def _times_01_bwd(m, g):
    return sum(lax.dot_general(p, m, _NT, preferred_element_type=F32) for p in _bf16_parts(g)), jnp.zeros_like(m)


_times_01.defvjp(lambda x, m: (_times_01(x, m), m), _times_01_bwd)


def _causal_ones():
    l = lax.broadcasted_iota(jnp.int32, (BLOCK, BLOCK), 0)
    s = lax.broadcasted_iota(jnp.int32, (BLOCK, BLOCK), 1)
    return (l >= s).astype(BF16)


@jax.custom_vjp
def _cumsum_rows(a):
    return sum(jnp.dot(_causal_ones(), p, preferred_element_type=F32) for p in _bf16_parts(a))


def _cumsum_rows_bwd(_, g):
    tn = (((0,), (0,)), ((), ()))
    return (sum(lax.dot_general(_causal_ones(), p, tn, preferred_element_type=F32) for p in _bf16_parts(g)),)


_cumsum_rows.defvjp(lambda a: (_cumsum_rows(a), None), _cumsum_rows_bwd)


def _ssd_group(xs, dt_tile, expand, bias, alog, dsk, bg, cg, state):
    l = lax.broadcasted_iota(jnp.int32, (BLOCK, BLOCK), 0)
    s = lax.broadcasted_iota(jnp.int32, (BLOCK, BLOCK), 1)
    causal = l >= s
    first_head = s < HEAD_DIM
    dt = _softplus(dt_tile + bias)
    a = dt * (-jnp.exp(alog))
    one_row = lambda v: jnp.broadcast_to(v, (HALO, LANES))
    per_lane = _times_01(jnp.concatenate([dt, _cumsum_rows(a), one_row(jnp.sum(a, axis=0, keepdims=True)),
                                          one_row(dsk)], axis=0), expand)
    dtx, cs = per_lane[0:BLOCK], per_lane[BLOCK:2 * BLOCK]
    tot, dsk = per_lane[2 * BLOCK:2 * BLOCK + 1], per_lane[2 * BLOCK + HALO:2 * BLOCK + HALO + 1]
    bb, cb16 = bg.astype(BF16), cg.astype(BF16)
    cb = lax.dot_general(cb16, bb, _NT, preferred_element_type=F32)
    xr = xs * dtx
    y_diag = []
    for p in range(GROUP_W // LANES):
        lanes = slice(p * LANES, (p + 1) * LANES)
        c_pair = cs[:, lanes]
        c_swap = _swap_halves(c_pair)
        m = []
        for c_head in (jnp.where(first_head, c_pair, c_swap), jnp.where(first_head, c_swap, c_pair)):
            m.append(cb * jnp.exp(jnp.where(causal, c_head - c_head.T, -jnp.inf)))
        x_pair = xr[:, lanes]
        x_diag = jnp.concatenate([jnp.where(first_head, x_pair, 0.0), jnp.where(first_head, 0.0, x_pair)], axis=0)
        y_diag.append(jnp.dot(jnp.concatenate(m, axis=1).astype(BF16), x_diag.astype(BF16),
                              preferred_element_type=F32))
    st = lax.dot_general(bb, (xr * jnp.exp(tot - cs)).astype(BF16), (((0,), (0,)), ((), ())),
                         preferred_element_type=F32)
    new_state = state * jnp.exp(tot) + st
    y_off = jnp.dot(cb16, state.astype(BF16), preferred_element_type=F32) * jnp.exp(cs)
    return jnp.concatenate(y_diag, axis=1) + y_off + dsk * xs, new_state


BC_WIDTH = SSM_GROUPS * SSM_STATE


def _ssd_specs(chunk):
    xs = pl.BlockSpec((BLOCK, SSM_INNER), lambda c: (chunk(c), 0))
    dt = pl.BlockSpec((BLOCK, LANES), lambda c: (chunk(c), DT_TILE))
    expand = _full((LANES, SSM_INNER))
    b = pl.BlockSpec((BLOCK, BC_WIDTH), lambda c: (chunk(c), SSM_INNER // BC_WIDTH))
    cc = pl.BlockSpec((BLOCK, BC_WIDTH), lambda c: (chunk(c), SSM_INNER // BC_WIDTH + 1))
    par = _full((1, LANES))
    state = pl.BlockSpec((None, SSM_STATE, SSM_INNER), lambda c: (chunk(c), 0, 0))
    return xs, dt, expand, b, cc, par, state


def _group_lanes(g):
    return slice(g * GROUP_W, (g + 1) * GROUP_W), slice(g * SSM_STATE, (g + 1) * SSM_STATE)


def ssd_fwd(xbc, proj, expand, bias, alog, dsk):
    nb = xbc.shape[0] // BLOCK
    xs, dt, ex, b, cc, par, state = _ssd_specs(lambda c: c)

    def body(x_ref, dt_ref, e_ref, bi_ref, al_ref, dk_ref, b_ref, c_ref, y_ref, sp_ref, st_ref):
        @pl.when(pl.program_id(0) == 0)
        def _():
            st_ref[...] = jnp.zeros_like(st_ref)

        for g in range(SSM_GROUPS):
            wide, tile = _group_lanes(g)
            entering = st_ref[:, wide]
            sp_ref[:, wide] = entering
            y_ref[:, wide], st_ref[:, wide] = _ssd_group(
                x_ref[:, wide], dt_ref[...], e_ref[:, wide], bi_ref[...], al_ref[...], dk_ref[...], b_ref[:, tile],
                c_ref[:, tile], entering)

    return _call(body, name="ssd_fwd", grid=(nb,), in_specs=[xs, dt, ex, par, par, par, b, cc],
                 out_specs=[xs, state],
                 out_shape=[jax.ShapeDtypeStruct((nb * BLOCK, SSM_INNER), F32),
                            jax.ShapeDtypeStruct((nb, SSM_STATE, SSM_INNER), F32)],
                 scratch=[pltpu.VMEM((SSM_STATE, SSM_INNER), F32)],
                 sem=("arbitrary",))(xbc, proj, expand, bias, alog, dsk, xbc, xbc)


def ssd_bwd(dy, xbc, proj, expand, bias, alog, dsk, states, comm):
    nb = xbc.shape[0] // BLOCK
    last = nb - 1
    xs, dt, ex, b, cc, par, state = _ssd_specs(lambda c: last - c)
    tile = pl.BlockSpec((BLOCK, LANES), lambda c: (last - c, 0))
    nspec = pl.BlockSpec((BLOCK, BC_WIDTH), lambda c: (last - c, 0))

    def body(dy_ref, x_ref, dt_ref, e_ref, bi_ref, al_ref, dk_ref, b_ref, c_ref, sp_ref,
             dx_ref, ddt_ref, db_ref, dc_ref, dbi_ref, dal_ref, ddk_ref, ds_ref):
        @pl.when(pl.program_id(0) == 0)
        def _():
            ds_ref[...] = jnp.zeros_like(ds_ref)
            dbi_ref[...] = jnp.zeros_like(dbi_ref)
            dal_ref[...] = jnp.zeros_like(dal_ref)
            ddk_ref[...] = jnp.zeros_like(ddk_ref)

        ddt, dbi, dal, ddk = 0.0, 0.0, 0.0, 0.0
        for g in range(SSM_GROUPS):
            wide, tile_lanes = _group_lanes(g)
            expand_rows = e_ref[:, wide]

            def f(xs, dt_tile, bias, alog, dsk, bg, cg, state):
                return _ssd_group(xs, dt_tile, expand_rows, bias, alog, dsk, bg, cg, state)

            _, vjp = jax.vjp(f, x_ref[:, wide], dt_ref[...], bi_ref[...], al_ref[...], dk_ref[...], b_ref[:, tile_lanes],
                             c_ref[:, tile_lanes], sp_ref[:, wide])
            (dx_ref[:, wide], ddt_g, dbi_g, dal_g, ddk_g, db_ref[:, tile_lanes], dc_ref[:, tile_lanes],
             ds_ref[:, wide]) = vjp((dy_ref[:, wide], ds_ref[:, wide]))
            ddt, dbi, dal, ddk = ddt + ddt_g, dbi + dbi_g, dal + dal_g, ddk + ddk_g
        ddt_ref[...] = ddt.astype(BF16)
        dbi_ref[...] += dbi
        dal_ref[...] += dal
        ddk_ref[...] += ddk

    def at():
        c = pl.program_id(0)
        return c == 0, c == 0, c == last

    par_shape = jax.ShapeDtypeStruct((1, LANES), F32)
    return _call_with_comm(
        body, comm, at, (dy, xbc, proj, expand, bias, alog, dsk, xbc, xbc, states), name="ssd_bwd",
        grid=(nb,), in_specs=[xs, xs, dt, ex, par, par, par, b, cc, state],
        out_specs=[xs, tile, nspec, nspec, par, par, par],
        out_shape=[jax.ShapeDtypeStruct((nb * BLOCK, SSM_INNER), F32), jax.ShapeDtypeStruct((nb * BLOCK, LANES), BF16),
                   jax.ShapeDtypeStruct((nb * BLOCK, BC_WIDTH), F32), jax.ShapeDtypeStruct((nb * BLOCK, BC_WIDTH), F32),
                   par_shape, par_shape, par_shape],
        scratch=[pltpu.VMEM((SSM_STATE, SSM_INNER), F32)])


SLAB_ROWS = 24
SLAB_META_ROW = 8


SLAB_LOSS_ROW = 7


def pack_small(dcw, dcb, dgpre, dgpost, dbias, dalog, ddsk, dsinks, dgn, dmeta, loss_tile):
    def body(cw, cb, gpre, gpost, dtb, al, dk, sk, gn, meta, loss, o_ref):
        o_ref[...] = jnp.zeros_like(o_ref)
        o_ref[SLAB_LOSS_ROW:SLAB_LOSS_ROW + 1, 0:LANES] = loss[0:1, :]
        o_ref[0:CONV_WIDTH, :] = cw[...]
        o_ref[4:5, :] = cb[...]
        o_ref[5:6, 0:1024] = gpre[...]
        o_ref[5:6, 1024:2048] = gpost[...]
        o_ref[5:6, 2048:2176] = dtb[...]
        o_ref[5:6, 2176:2304] = al[...]
        o_ref[5:6, 2304:2432] = dk[...]
        o_ref[5:6, 2432:2560] = sk[...]
        o_ref[6:7, 0:SSM_INNER] = gn[...]
        o_ref[SLAB_META_ROW:SLAB_META_ROW + N_META, 0:D_MODEL] = meta[...]

    args = (dcw, dcb, dgpre, dgpost, dbias, dalog, ddsk, dsinks, dgn, dmeta, loss_tile)
    return _call(body, name="pack_small", in_specs=[_full(a.shape) for a in args],
                 out_specs=_full((SLAB_ROWS, CONV_DIM)), out_shape=jax.ShapeDtypeStruct((SLAB_ROWS, CONV_DIM), F32))(*args)


def _lane_tile(v):
    return jnp.pad(v, ((0, 0), (0, LANES - v.shape[1])))


def kernel(x, meta_tokens, g_pre, w_in, conv_w, conv_b, dt_bias, a_log, d_skip, attn_sinks, g_ssm_norm, w_out_att, w_out_ssm, w_out, g_post, loss_target, m_meta_tokens, m_g_pre, m_w_in, m_conv_w, m_conv_b, m_dt_bias, m_a_log, m_d_skip, m_attn_sinks, m_g_ssm_norm, m_w_out_att, m_w_out_ssm, m_w_out, m_g_post, v_meta_tokens, v_g_pre, v_w_in, v_conv_w, v_conv_b, v_dt_bias, v_a_log, v_d_skip, v_attn_sinks, v_g_ssm_norm, v_w_out_att, v_w_out_ssm, v_w_out, v_g_post):
    chip = _chip_index()

    conv_w_rows = jnp.pad(conv_w[0], ((0, 2 * 8 - CONV_WIDTH), (0, 0)))
    w_in_t, m_w_in_t, v_w_in_t = w_in[0].T, m_w_in[0].T, v_w_in[0].T
    gathered_w_in, g_conv_w, g_meta = run_comm("gather_w_in",
                                               TwoLevelGather([pack_w_in(w_in_t), conv_w_rows, meta_tokens]))
    w_all_t = unpack_w_in(gathered_w_in)
    cw_full = g_conv_w[:, :CONV_WIDTH].transpose(1, 0, 2).reshape(CONV_WIDTH, CONV_DIM)
    meta_full = g_meta.transpose(1, 0, 2).reshape(N_META, D_MODEL)

    h, u = prep(x, meta_full, g_pre)
    proj, g_w_out = project("in_proj", u, w_all_t, TwoLevelGather(
        [w_out_att[0].astype(BF16), w_out_ssm[0].astype(BF16), w_out[0].astype(BF16)]))
    woa = g_w_out[0].reshape(D_MODEL, D_MODEL)
    wos = g_w_out[1].reshape(SSM_INNER, D_MODEL)
    wo = g_w_out[2].reshape(D_MODEL, D_MODEL)

    sinks3 = attn_sinks.reshape(ATT_Q_HEADS, 1, 1)
    a_att = attn_fwd(proj, sinks3)

    xbc = conv_fwd(proj, cw_full, conv_b)
    expand = _head_expand()
    head_pars = (_lane_tile(dt_bias), _lane_tile(a_log), _lane_tile(d_skip))
    y_ssd, states = ssd_fwd(xbc, proj, expand, *head_pars)

    (yn, merged, dout, dy_att, dy_ssm, da_att, dy_ssd, dz_ssm, dga, dgs, dres, loss_tile, dg_post, dgn) = tail(
        y_ssd, proj, a_att, x, loss_target, woa, wos, wo, g_ssm_norm, g_post)

    dwo = mm_tn("out_proj_dw", merged, dout)
    dwoa = mm_tn("att_out_dw", a_att, dy_att)
    dwos = mm_tn("ssm_out_dw", yn, dy_ssm)
    dq, dz_att, dk, dv, dkmeta, dvmeta, dsinks3 = attn_bwd(da_att, proj, sinks3)
    dk = dk.at[PAD_ROWS:BLOCK].add(dkmeta).astype(BF16)
    dv = dv.at[PAD_ROWS:BLOCK].add(dvmeta).astype(BF16)

    def pieces(g):
        return g.reshape(4, 2, g.shape[0] // 8, g.shape[1])

    def to_owner(g):
        return (lambda ref, dev: ref.at[_chip_of(dev), dev[2]], (g.shape[0] // 8, g.shape[1]))

    (dxs, ddt_tile, dbg, dcg, dbias, dalog, ddsk), sent_w_out = ssd_bwd(
        dy_ssd, xbc, proj, expand, *head_pars, states,
        DirectExchange([pieces(dwoa), pieces(dwos), pieces(dwo)], [to_owner(dwoa), to_owner(dwos), to_owner(dwo)],
                       ALL_MASKS, "dev", 8))
    dxs_raw, dcw_xs, dcb_xs = conv_bwd("conv_bwd_x", [dxs], 0, proj, cw_full, conv_b)
    dbc_raw, dcw_bc, dcb_bc = conv_bwd("conv_bwd_bc", [dbg, dcg], SSM_INNER // D_MODEL, proj, cw_full, conv_b)
    dcw = jnp.concatenate([dcw_xs, dcw_bc], axis=1)
    dcb = jnp.concatenate([dcb_xs, dcb_bc], axis=1)

    narrow = jnp.concatenate([dk, dv, ddt_tile, jnp.zeros((dk.shape[0], N_ACT - N_ALIGNED), BF16)], axis=1)
    dproj = [dz_ssm, dxs_raw, dbc_raw, dq, dz_att, dga, dgs, narrow]
    dw_all_t = weight_grad_t("in_proj_dw", dproj, u)

    half_rows = PACK_W // 2
    partial = pack_grad_w_in(dw_all_t).reshape(4, 2, half_rows, D_MODEL)
    from_sibling, = run_comm("pair_grads", DirectExchange(
        [partial], [(lambda ref, dev: ref.at[pl.ds(0, 4), dev[2]], (4, half_rows, D_MODEL))], SIBLING_MASK, "core", 2,
        keep_own=False))
    chip_sum = sum_pair(partial, from_sibling)
    *in_flight, started = chip_exchange_start(chip_sum)
    du = project_back("in_proj_dx", dproj, w_all_t, started)
    grad_x, dmeta, dg_pre = prep_bwd(h, du, dres, g_pre)
    sent_w_in = chip_exchange_wait(*in_flight, dg_pre)

    slab = pack_small(dcw, dcb, dg_pre, dg_post, dbias, dalog, ddsk, _lane_tile(dsinks3.reshape(1, ATT_Q_HEADS)), dgn,
                      dmeta, loss_tile)
    halves = [sum_slots("sum_" + nm, r)
              for nm, r in zip(("w_in", "w_out_att", "w_out_ssm", "w_out"), [sent_w_in] + list(sent_w_out))]
    shared = run_comm("share_grads", Both(DirectExchange(halves, [None] * 4, SIBLING_MASK, "core", 2),
                                          DirectExchange([slab], [None], ALL_MASKS, "dev", 8)))
    g_w_in_packed, g_woa, g_wos, g_wo = [f.reshape(2 * f.shape[1], f.shape[2]) for f in shared[:4]]
    small = sum_slots("sum_small", shared[4])
    loss = small[SLAB_LOSS_ROW, 0]

    g_w_in, d_w_in, nm_w_in, nv_w_in = [a.T for a in adamw_w_in(g_w_in_packed, w_in_t, m_w_in_t, v_w_in_t)]
    d_woa, nm_woa, nv_woa = adamw_rows("adamw_w_out_att", g_woa, w_out_att[0], m_w_out_att[0], v_w_out_att[0])
    d_wos, nm_wos, nv_wos = adamw_rows("adamw_w_out_ssm", g_wos, w_out_ssm[0], m_w_out_ssm[0], v_w_out_ssm[0])
    d_wo, nm_wo, nv_wo = adamw_rows("adamw_w_out", g_wo, w_out[0], m_w_out[0], v_w_out[0])

    cw_cols = CONV_DIM // 4
    meta_cols = D_MODEL // 4
    g_small = {
        "meta_tokens": lax.dynamic_slice(small, (SLAB_META_ROW, chip * meta_cols), (N_META, meta_cols)),
        "g_pre": small[5:6, 0:1024],
        "conv_w": lax.dynamic_slice(small, (0, chip * cw_cols), (CONV_WIDTH, cw_cols)),
        "conv_b": small[4:5, :],
        "dt_bias": small[5:6, 2048:2048 + SSM_HEADS],
        "a_log": small[5:6, 2176:2176 + SSM_HEADS],
        "d_skip": small[5:6, 2304:2304 + SSM_HEADS],
        "attn_sinks": small[5:6, 2432:2432 + ATT_Q_HEADS],
        "g_ssm_norm": small[6:7, 0:SSM_INNER],
        "g_post": small[5:6, 1024:2048],
    }
    names = list(g_small)
    w_small = dict(meta_tokens=meta_tokens, g_pre=g_pre, conv_w=conv_w[0], conv_b=conv_b, dt_bias=dt_bias, a_log=a_log,
                   d_skip=d_skip, attn_sinks=attn_sinks, g_ssm_norm=g_ssm_norm, g_post=g_post)
    m_small = dict(meta_tokens=m_meta_tokens, g_pre=m_g_pre, conv_w=m_conv_w[0], conv_b=m_conv_b, dt_bias=m_dt_bias,
                   a_log=m_a_log, d_skip=m_d_skip, attn_sinks=m_attn_sinks, g_ssm_norm=m_g_ssm_norm, g_post=m_g_post)
    v_small = dict(meta_tokens=v_meta_tokens, g_pre=v_g_pre, conv_w=v_conv_w[0], conv_b=v_conv_b, dt_bias=v_dt_bias,
                   a_log=v_a_log, d_skip=v_d_skip, attn_sinks=v_attn_sinks, g_ssm_norm=v_g_ssm_norm, g_post=v_g_post)
    upd = dict(zip(names, adamw_small([g_small[k] for k in names], [w_small[k] for k in names],
                                      [m_small[k] for k in names], [v_small[k] for k in names])))

    lead = {"conv_w"}

    def shaped(name, a):
        return a[None] if name in lead else a

    grads = dict(g_small, w_in=g_w_in, w_out_att=g_woa, w_out_ssm=g_wos, w_out=g_wo)
    deltas = dict({k: upd[k][0] for k in names}, w_in=d_w_in, w_out_att=d_woa, w_out_ssm=d_wos, w_out=d_wo)
    new_m = dict({k: upd[k][1] for k in names}, w_in=nm_w_in, w_out_att=nm_woa, w_out_ssm=nm_wos, w_out=nm_wo)
    new_v = dict({k: upd[k][2] for k in names}, w_in=nv_w_in, w_out_att=nv_woa, w_out_ssm=nv_wos, w_out=nv_wo)
    lead |= {"w_in", "w_out_att", "w_out_ssm", "w_out"}
    order = ["meta_tokens", "g_pre", "w_in", "conv_w", "conv_b", "dt_bias", "a_log", "d_skip", "attn_sinks",
             "g_ssm_norm", "w_out_att", "w_out_ssm", "w_out", "g_post"]
    outs = [loss, grad_x]
    for group in (grads, deltas, new_m, new_v):
        outs += [shaped(k, group[k]) for k in order]
    return tuple(outs)
```

```python
import functools

import numpy as np
import jax
import jax.numpy as jnp
from jax import lax
from jax.experimental import pallas as pl
from jax.experimental.pallas import tpu as pltpu

F32 = jnp.float32
BF16 = jnp.bfloat16
HI = lax.Precision.HIGHEST

D_MODEL = 1024
N_META = 16
BLOCK = 128
PAD_ROWS = BLOCK - N_META
NORM_EPS = 1e-6
HEAD_DIM = 64
ATT_Q_HEADS = 16
ATT_KV_HEADS = 4
ATT_GROUP = 4
SSM_INNER = 2048
SSM_HEADS = 32
SSM_GROUPS = 4
SSM_HEADS_PER_GROUP = 8
SSM_STATE = 128
CONV_WIDTH = 4
CONV_DIM = 3072
LANES = 128

ADAM_LR = 0.001
ADAM_B1 = 0.9
ADAM_B2 = 0.999
ADAM_EPS = 1e-08
ADAM_WD = 0.01
ADAM_STEP = 10

VMEM_LIMIT = 48 * 1024 * 1024

SHARD_W = 2440
PACK_W = 2560
SHARD_STRIDE = 2432
N_ALIGNED = 9856
N_ACT = 10240
SEG = {
    "q": (0, 1024, 5120), "k": (1024, 256, 9216), "v": (1280, 256, 9472), "z_att": (1536, 1024, 6144),
    "z_ssm": (2560, 2048, 0), "xbc": (4608, 3072, 2048), "dt": (7680, 128, 9728),
    "gate_att": (7808, 1024, 7168), "gate_ssm": (8832, 1024, 8192),
}
DT_STORED_START = 7680
DT_PAD = LANES - SSM_HEADS


def _act_col(aligned_col):
    for a0, w, p0 in SEG.values():
        if a0 <= aligned_col < a0 + w:
            return p0 + aligned_col - a0
    raise ValueError(aligned_col)


def _call(body, *, name, out_shape, in_specs, out_specs, grid=(), scratch=(), sem=None, aliases=None):
    return pl.pallas_call(
        body, out_shape=out_shape, grid=grid, in_specs=in_specs, out_specs=out_specs, scratch_shapes=list(scratch),
        name=name, input_output_aliases=aliases or {},
        compiler_params=pltpu.CompilerParams(dimension_semantics=sem, vmem_limit_bytes=VMEM_LIMIT))


def _full(shape):
    n = len(shape)
    return pl.BlockSpec(shape, lambda *_: (0,) * n)


def _chip_index():
    return lax.axis_index("x") * 2 + lax.axis_index("y")


_sigmoid = jax.nn.sigmoid


def _silu(z):
    return z * _sigmoid(z)


def _rms(x, g):
    return x * lax.rsqrt(jnp.mean(x * x, axis=-1, keepdims=True) + NORM_EPS) * g


def _peer(mask):
    x, y, c = lax.axis_index("x"), lax.axis_index("y"), lax.axis_index("c")
    return ((1 - x) if mask & 4 else x, (1 - y) if mask & 2 else y, (1 - c) if mask & 1 else c)


def _me():
    return lax.axis_index("x"), lax.axis_index("y"), lax.axis_index("c")


def _chip_of(dev):
    return 2 * dev[0] + dev[1]


CHIP_MASKS = (4, 2, 6)
ALL_MASKS = (1, 2, 3, 4, 5, 6, 7)
SIBLING_MASK = (1,)


def _remote(src, dst, send_sem, recv_sem, dev):
    return pltpu.make_async_remote_copy(src_ref=src, dst_ref=dst, send_sem=send_sem, recv_sem=recv_sem,
                                        device_id=dev, device_id_type=pl.DeviceIdType.MESH)


class _StagedCopy:
    def __init__(self, src, stage, dst, load_sem, store_sem):
        self.load = pltpu.make_async_copy(src, stage, load_sem)
        self.store = pltpu.make_async_copy(stage, dst, store_sem)

    def start(self):
        self.load.start()
        self.load.wait()
        self.store.start()

    def wait(self):
        self.store.wait()


class DirectExchange:
    def __init__(self, arrays, pieces, masks, slot_kind, nslots, keep_own=True):
        self.arrays, self.pieces, self.masks, self.slot_kind = list(arrays), list(pieces), masks, slot_kind
        self.keep_own = keep_own
        n, nk = len(arrays), len(masks)
        shapes = [a.shape if p is None else p[1] for a, p in zip(arrays, pieces)]
        self.out_shape = [jax.ShapeDtypeStruct((nslots,) + tuple(s), a.dtype) for s, a in zip(shapes, arrays)]
        self.scratch = [pltpu.SemaphoreType.DMA((n * nk,)), pltpu.SemaphoreType.DMA((n * nk,))]
        if keep_own:
            self.scratch += [pltpu.SemaphoreType.DMA((2 * n,))] + [pltpu.VMEM(s, a.dtype) for s, a in zip(shapes, arrays)]
        self.has_mid = False

    def _copies(self, ins, outs, scratch):
        send_sems, recv_sems = scratch[:2]
        me = _me()
        slot = {"chip": _chip_of(me), "dev": 4 * me[0] + 2 * me[1] + me[2], "core": me[2]}[self.slot_kind]
        nk = len(self.masks)

        def piece(a, dev):
            return ins[a] if self.pieces[a] is None else self.pieces[a][0](ins[a], dev)

        local = []
        if self.keep_own:
            local_sems, stages = scratch[2], scratch[3:]
            local = [_StagedCopy(piece(a, me), stages[a], outs[a].at[slot], local_sems.at[2 * a], local_sems.at[2 * a + 1])
                     for a in range(len(ins))]
        remote = []
        for a in range(len(ins)):
            for ki, mask in enumerate(self.masks):
                dev = _peer(mask)
                remote.append(_remote(piece(a, dev), outs[a].at[slot], send_sems.at[a * nk + ki],
                                      recv_sems.at[a * nk + ki], dev))
        return local, remote

    def start(self, ins, outs, scratch):
        local, remote = self._copies(ins, outs, scratch)
        for cp in remote + local:
            cp.start()

    def finish(self, ins, outs, scratch):
        local, remote = self._copies(ins, outs, scratch)
        for cp in remote + local:
            cp.wait()


SPLIT_ROWS = 16


class TwoLevelGather:
    def __init__(self, arrays):
        self.arrays = list(arrays)
        n = len(arrays)
        self.out_shape = [jax.ShapeDtypeStruct((4,) + a.shape, a.dtype) for a in arrays]
        self.scratch = ([pltpu.SemaphoreType.DMA((4 * n,)), pltpu.SemaphoreType.DMA((4 * n,)),
                         pltpu.SemaphoreType.DMA((3 * n,)), pltpu.SemaphoreType.DMA((3 * n,)),
                         pltpu.SemaphoreType.DMA((2 * n,))] + [pltpu.VMEM(a.shape, a.dtype) for a in arrays])
        self.has_mid = True

    def _copies(self, ins, outs, scratch):
        ici_send, ici_recv, fwd_send, fwd_recv, local_sems = scratch[:5]
        stages = scratch[5:]
        me = _me()
        sibling, in_x, in_y, diagonal = _peer(1), _peer(4), _peer(2), _peer(6)
        plan = []
        for a in range(len(ins)):
            half = ins[a].shape[0] // 2
            first = half // 2 if half % (2 * SPLIT_ROWS) == 0 else half
            mine = pl.ds(me[2] * half, half)
            local = _StagedCopy(ins[a], stages[a], outs[a].at[_chip_of(me)], local_sems.at[2 * a], local_sems.at[2 * a + 1])

            def ici(k, src, dst, dev):
                return _remote(src, dst, ici_send.at[4 * a + k], ici_recv.at[4 * a + k], dev)

            def d2d(k, chip):
                zone = outs[a].at[_chip_of(chip), mine]
                return _remote(zone, zone, fwd_send.at[3 * a + k], fwd_recv.at[3 * a + k], sibling)

            own_zone = outs[a].at[_chip_of(me), mine]
            from_x = outs[a].at[_chip_of(in_x), pl.ds(me[2] * half, first)]
            onward = [ici(2, from_x, from_x, in_y), None]
            if first < half:
                from_y = outs[a].at[_chip_of(in_y), pl.ds(me[2] * half + first, half - first)]
                onward[1] = ici(3, from_y, from_y, in_x)
            plan.append(dict(
                local=local,
                own=[ici(0, ins[a].at[mine], own_zone, in_x), ici(1, ins[a].at[mine], own_zone, in_y)],
                onward=onward, sibling=[d2d(0, in_x), d2d(1, in_y), d2d(2, diagonal)]))
        return plan

    def start(self, ins, outs, scratch):
        for p in self._copies(ins, outs, scratch):
            for cp in p["own"]:
                cp.start()
            p["local"].start()

    def mid(self, ins, outs, scratch):
        plan = self._copies(ins, outs, scratch)
        for p in plan:
            for k in range(2):
                p["own"][k].wait_recv()
                if p["onward"][k] is not None:
                    p["onward"][k].start()
                p["sibling"][k].start()
        for p in plan:
            for cp in p["onward"]:
                if cp is not None:
                    cp.wait_recv()
            p["sibling"][2].start()

    def finish(self, ins, outs, scratch):
        for p in self._copies(ins, outs, scratch):
            for cp in p["sibling"]:
                cp.wait_recv()
            for cp in p["own"] + p["sibling"] + [cp for cp in p["onward"] if cp is not None]:
                cp.wait_send()
            p["local"].wait()


class Both:
    def __init__(self, a, b):
        self.a, self.b = a, b
        self.arrays, self.out_shape = a.arrays + b.arrays, a.out_shape + b.out_shape
        self.scratch = a.scratch + b.scratch
        self.has_mid = False
        assert not (a.has_mid or b.has_mid)

    def _parts(self, ins, outs, sems):
        na, sa = len(self.a.arrays), len(self.a.scratch)
        return (ins[:na], outs[:na], sems[:sa]), (ins[na:], outs[na:], sems[sa:])

    def start(self, ins, outs, sems):
        pa, pb = self._parts(ins, outs, sems)
        self.a.start(*pa)
        self.b.start(*pb)

    def finish(self, ins, outs, sems):
        pa, pb = self._parts(ins, outs, sems)
        self.a.finish(*pa)
        self.b.finish(*pb)


_ANY = pl.BlockSpec(memory_space=pl.ANY)


def run_comm(name, comm):
    n = len(comm.arrays)

    def body(*refs):
        ins, outs, sems = refs[:n], refs[n:2 * n], refs[2 * n:]
        comm.start(ins, outs, sems)
        if comm.has_mid:
            comm.mid(ins, outs, sems)
        comm.finish(ins, outs, sems)

    return pl.pallas_call(body, name=name, out_shape=comm.out_shape, in_specs=[_ANY] * n, out_specs=[_ANY] * n,
                          scratch_shapes=comm.scratch,
                          compiler_params=pltpu.CompilerParams(vmem_limit_bytes=VMEM_LIMIT))(*comm.arrays)


_HBM = pl.BlockSpec(memory_space=pltpu.HBM)
_SEM = pl.BlockSpec(memory_space=pltpu.SEMAPHORE)
_SIDE_EFFECT = pltpu.SideEffectType.DATAFLOW_SIDE_EFFECTING


def chip_exchange_start(pieces):
    shape = pieces.shape
    nk = len(CHIP_MASKS)

    def body(src_ref, land_ref, send_sems, recv_sems, src_thru, land_thru, token, stage, local_sems):
        me = _me()
        for ki, mask in enumerate(CHIP_MASKS):
            dev = _peer(mask)
            _remote(src_ref.at[_chip_of(dev)], land_ref.at[_chip_of(me)], send_sems.at[ki], recv_sems.at[ki], dev).start()
        own = _StagedCopy(src_ref.at[_chip_of(me)], stage, land_ref.at[_chip_of(me)], local_sems.at[0], local_sems.at[1])
        own.start()
        own.wait()
        token[...] = jnp.zeros_like(token)

    return pl.pallas_call(
        body, name="chip_exchange_start",
        out_shape=(pltpu.SemaphoreType.DMA((nk,)), pltpu.SemaphoreType.DMA((nk,)), pltpu.HBM(shape, pieces.dtype),
                   pltpu.HBM(shape, pieces.dtype), jax.ShapeDtypeStruct((8, LANES), F32)),
        in_specs=(_HBM, _HBM), out_specs=(_SEM, _SEM, _HBM, _HBM, pl.BlockSpec(memory_space=pltpu.VMEM)),
        input_output_aliases={0: 2, 1: 3},
        scratch_shapes=[pltpu.VMEM(shape[1:], pieces.dtype), pltpu.SemaphoreType.DMA((2,))],
        compiler_params=pltpu.CompilerParams(has_side_effects=_SIDE_EFFECT, vmem_limit_bytes=VMEM_LIMIT),
    )(pltpu.with_memory_space_constraint(pieces, pltpu.HBM),
      pltpu.with_memory_space_constraint(lax.empty(shape, pieces.dtype), pltpu.HBM))


def chip_exchange_wait(send_sems, recv_sems, src_thru, land_thru, after):
    nk = len(CHIP_MASKS)

    def body(src_ref, land_ref, send_sems, recv_sems, after_ref, src_dead, land_out):
        me = _me()
        for ki, mask in enumerate(CHIP_MASKS):
            dev = _peer(mask)
            copy = _remote(src_ref.at[_chip_of(dev)], land_ref.at[_chip_of(me)], send_sems.at[ki], recv_sems.at[ki], dev)
            copy.wait_send()
            copy.wait_recv()

    return pl.pallas_call(
        body, name="chip_exchange_wait",
        out_shape=(pltpu.HBM(src_thru.shape, src_thru.dtype), pltpu.HBM(land_thru.shape, land_thru.dtype)),
        in_specs=(_HBM, _HBM, _SEM, _SEM, pl.BlockSpec(memory_space=pl.ANY)), out_specs=(_HBM, _HBM),
        input_output_aliases={0: 0, 1: 1},
        compiler_params=pltpu.CompilerParams(has_side_effects=_SIDE_EFFECT),
    )(src_thru, land_thru, send_sems, recv_sems, after)[1]


def _call_with_comm(body, comm, steps, args, *, name, out_shape, in_specs, out_specs, grid, scratch=()):
    ni, no, ns, nc = len(in_specs), len(out_specs), len(scratch), len(comm.arrays)

    def full_body(*refs):
        ins, cins = refs[:ni], refs[ni:ni + nc]
        outs, couts = refs[ni + nc:ni + nc + no], refs[ni + nc + no:ni + 2 * nc + no]
        scr, csems = refs[ni + 2 * nc + no:ni + 2 * nc + no + ns], refs[ni + 2 * nc + no + ns:]
        first, middle, last = steps()
        pl.when(first)(lambda: comm.start(cins, couts, csems))
        if comm.has_mid:
            pl.when(middle)(lambda: comm.mid(cins, couts, csems))
        body(*ins, *outs, *scr)
        pl.when(last)(lambda: comm.finish(cins, couts, csems))

    res = pl.pallas_call(
        full_body, name=name, out_shape=list(out_shape) + comm.out_shape, grid=grid,
        in_specs=list(in_specs) + [_ANY] * nc, out_specs=list(out_specs) + [_ANY] * nc,
        scratch_shapes=list(scratch) + comm.scratch,
        compiler_params=pltpu.CompilerParams(dimension_semantics=("arbitrary",) * len(grid),
                                             vmem_limit_bytes=VMEM_LIMIT))(*args, *comm.arrays)
    return res[:no], res[no:]


def _shard_pieces(chip):
    if chip < 3:
        return [(0, SHARD_W, 8 * chip)]
    behind_dt = DT_STORED_START + SSM_HEADS - 3 * SHARD_W
    return [(0, behind_dt, 24), (behind_dt, SHARD_W - behind_dt, behind_dt + 24 + DT_PAD)]


W_IN_COLS = 256


def pack_w_in(wt):
    def body(w_ref, o_ref, pad_ref):
        chip = _chip_index()
        pad_ref[...] = jnp.zeros_like(pad_ref)
        for cv in range(4):
            @pl.when(chip == cv)
            def _():
                for src, n, dst in _shard_pieces(cv):
                    pad_ref[dst:dst + n, :] = w_ref[src:src + n, :]
        o_ref[...] = pad_ref[...].astype(BF16)

    return _call(body, name="pack_w_in", grid=(D_MODEL // W_IN_COLS,),
                 in_specs=[pl.BlockSpec((SHARD_W, W_IN_COLS), lambda i: (0, i))],
                 out_specs=pl.BlockSpec((PACK_W, W_IN_COLS), lambda i: (0, i)),
                 out_shape=jax.ShapeDtypeStruct((PACK_W, D_MODEL), BF16),
                 scratch=[pltpu.VMEM((PACK_W, W_IN_COLS), F32)], sem=("parallel",))(wt)


def _tile_runs():
    runs, fix = [], []
    for t in range(N_ALIGNED // LANES):
        s = min(t // 19, 3)
        j = t - 19 * s
        p = _act_col(t * LANES)
        if runs and runs[-1][1] == s and runs[-1][0] + runs[-1][3] == p and runs[-1][2] + runs[-1][3] == j * LANES:
            runs[-1][3] += LANES
        else:
            runs.append([p, s, j * LANES, LANES])
        if j == 0 and s > 0:
            fix.append((p, s - 1))
    return runs, fix


def unpack_w_in(bg):
    runs, fix = _tile_runs()

    def body(b_ref, o_ref):
        for p, s, j, w in runs:
            o_ref[p:p + w, :] = b_ref[s, j:j + w, :]
        for p, s in fix:
            o_ref[p:p + LANES, :] = o_ref[p:p + LANES, :] + b_ref[s, SHARD_STRIDE:PACK_W, :]
        o_ref[N_ALIGNED:N_ACT, :] = jnp.zeros((N_ACT - N_ALIGNED, W_IN_COLS), BF16)

    return _call(body, name="unpack_w_in", grid=(D_MODEL // W_IN_COLS,),
                 in_specs=[pl.BlockSpec((4, PACK_W, W_IN_COLS), lambda i: (0, 0, i))],
                 out_specs=pl.BlockSpec((N_ACT, W_IN_COLS), lambda i: (0, i)),
                 out_shape=jax.ShapeDtypeStruct((N_ACT, D_MODEL), BF16), sem=("parallel",))(bg)


def pack_grad_w_in(dwt):
    def body(g_ref, o_ref):
        for s in range(4):
            for j in range(PACK_W // LANES):
                p = _act_col((19 * s + j) * LANES)
                o_ref[s, j * LANES:(j + 1) * LANES, :] = g_ref[p:p + LANES, :]

    return _call(body, name="pack_grad_w_in", grid=(D_MODEL // W_IN_COLS,),
                 in_specs=[pl.BlockSpec((N_ACT, W_IN_COLS), lambda i: (0, i))],
                 out_specs=pl.BlockSpec((4, PACK_W, W_IN_COLS), lambda i: (0, 0, i)),
                 out_shape=jax.ShapeDtypeStruct((4, PACK_W, D_MODEL), BF16), sem=("parallel",))(dwt)


def _adamw(w, g, m, v):
    m = ADAM_B1 * m + (1.0 - ADAM_B1) * g
    v = ADAM_B2 * v + (1.0 - ADAM_B2) * jnp.square(g)
    m_hat = m / (1.0 - ADAM_B1 ** ADAM_STEP)
    v_hat = v / (1.0 - ADAM_B2 ** ADAM_STEP)
    delta = -ADAM_LR * (m_hat / (jnp.sqrt(v_hat) + ADAM_EPS) + ADAM_WD * w)
    return delta, m, v


def adamw_w_in(g_packed, wt, mt, vt):
    cols = LANES

    def body(g_ref, w_ref, m_ref, v_ref, go_ref, d_ref, mo_ref, vo_ref):
        chip = _chip_index()
        for cv in range(4):
            @pl.when(chip == cv)
            def _():
                for dst, n, src in _shard_pieces(cv):
                    go_ref[dst:dst + n, :] = g_ref[src:src + n, :]
        d_ref[...], mo_ref[...], vo_ref[...] = _adamw(w_ref[...], go_ref[...], m_ref[...], v_ref[...])

    spec = pl.BlockSpec((SHARD_W, cols), lambda i: (0, i))
    shp = jax.ShapeDtypeStruct((SHARD_W, D_MODEL), F32)
    return _call(body, name="adamw_w_in", grid=(D_MODEL // cols,),
                 in_specs=[pl.BlockSpec((PACK_W, cols), lambda i: (0, i)), spec, spec, spec],
                 out_specs=[spec] * 4, out_shape=[shp] * 4, sem=("parallel",))(g_packed, wt, mt, vt)


def adamw_rows(name, g, w, m, v):
    r, c = g.shape
    rows = min(r, BLOCK)

    def body(g_ref, w_ref, m_ref, v_ref, d_ref, mo_ref, vo_ref):
        d_ref[...], mo_ref[...], vo_ref[...] = _adamw(w_ref[...], g_ref[...], m_ref[...], v_ref[...])

    spec = pl.BlockSpec((rows, c), lambda i: (i, 0))
    shp = jax.ShapeDtypeStruct((r, c), F32)
    return _call(body, name=name, grid=(r // rows,), in_specs=[spec] * 4, out_specs=[spec] * 3, out_shape=[shp] * 3,
                 sem=("parallel",))(g, w, m, v)


def adamw_small(gs, ws, ms, vs):
    n = len(gs)

    def body(*refs):
        g, w, m, v = refs[:n], refs[n:2 * n], refs[2 * n:3 * n], refs[3 * n:4 * n]
        outs = refs[4 * n:]
        for i in range(n):
            d, mn, vn = _adamw(w[i][...], g[i][...], m[i][...], v[i][...])
            outs[3 * i][...] = d
            outs[3 * i + 1][...] = mn
            outs[3 * i + 2][...] = vn

    specs = [_full(a.shape) for a in gs]
    res = _call(body, name="adamw_small", in_specs=specs * 4,
                out_specs=[s for s in specs for _ in range(3)],
                out_shape=[jax.ShapeDtypeStruct(a.shape, F32) for a in gs for _ in range(3)])(*gs, *ws, *ms, *vs)
    return [tuple(res[3 * i:3 * i + 3]) for i in range(n)]


def sum_slots(name, r):
    s, rr, c = r.shape
    rows = min(rr, BLOCK)

    def body(r_ref, o_ref):
        acc = r_ref[0].astype(F32)
        for k in range(1, s):
            acc = acc + r_ref[k].astype(F32)
        o_ref[...] = acc

    return _call(body, name=name, grid=(rr // rows,), in_specs=[pl.BlockSpec((s, rows, c), lambda i: (0, i, 0))],
                 out_specs=pl.BlockSpec((rows, c), lambda i: (i, 0)), out_shape=jax.ShapeDtypeStruct((rr, c), F32),
                 sem=("parallel",))(r)


def sum_pair(partial, from_sibling):
    s, _, rr, cols = partial.shape
    rows = rr // 2

    def body(p_ref, r_ref, o_ref):
        c = lax.axis_index("c")
        o_ref[...] = (p_ref[c].astype(F32) + r_ref[1 - c].astype(F32)).astype(BF16)

    return _call(body, name="sum_pair", grid=(s, rr // rows),
                 in_specs=[pl.BlockSpec((None, 2, rows, cols), lambda k, i: (k, 0, i, 0)),
                           pl.BlockSpec((2, None, rows, cols), lambda k, i: (0, k, i, 0))],
                 out_specs=pl.BlockSpec((None, rows, cols), lambda k, i: (k, i, 0)),
                 out_shape=jax.ShapeDtypeStruct((s, rr, cols), BF16), sem=("parallel", "parallel"))(partial, from_sibling)


def _col_tile(n, k):
    if n % 896 == 0 and k <= 1024:
        return 896
    return min(n, 512)


def project(name, x, wt, comm):
    m, k = x.shape
    n = wt.shape[0]
    tn = D_MODEL
    steps = n // tn

    def body(x_ref, w_ref, o_ref):
        o_ref[...] = lax.dot_general(x_ref[...], w_ref[...], _NT, preferred_element_type=F32)

    def at():
        j = pl.program_id(0)
        return j == 0, j == (3 * steps) // 4, j == steps - 1

    (res,), moved = _call_with_comm(
        body, comm, at, (x, wt), name=name, grid=(steps,),
        in_specs=[_full((m, k)), pl.BlockSpec((tn, k), lambda j: (j, 0))],
        out_specs=[pl.BlockSpec((m, tn), lambda j: (0, j))], out_shape=[jax.ShapeDtypeStruct((m, n), F32)])
    return res, moved


def _piece_tiles(pieces):
    spans, start = [], 0
    for p in pieces:
        spans.append((start, p.shape[1] // D_MODEL))
        start += p.shape[1] // D_MODEL
    return spans, start


def _piece_spec(tm, span, rows_of, tile_of):
    first, count = span

    def index(i, j):
        t = tile_of(i, j) - first
        mine = (t >= 0) & (t < count)
        return jnp.where(mine, rows_of(i, j), 0), jnp.clip(t, 0, count - 1)

    return pl.BlockSpec((tm, D_MODEL), index)


def project_back(name, pieces, wt, after):
    m = pieces[0].shape[0]
    k = wt.shape[1]
    tm = m // 2
    spans, steps = _piece_tiles(pieces)

    def body(*refs):
        w_ref, o_ref = refs[len(pieces)], refs[len(pieces) + 2]
        j = pl.program_id(1)

        @pl.when(j == 0)
        def _():
            o_ref[...] = jnp.zeros_like(o_ref)

        for dy_ref, (first, count) in zip(refs, spans):
            @pl.when((j >= first) & (j < first + count))
            def _():
                o_ref[...] += jnp.dot(dy_ref[...], w_ref[...], preferred_element_type=F32)

    return _call(body, name=name, grid=(m // tm, steps),
                 in_specs=[_piece_spec(tm, s, lambda i, j: i, lambda i, j: j) for s in spans]
                 + [pl.BlockSpec((D_MODEL, k), lambda i, j: (j, 0)), _full(after.shape)],
                 out_specs=pl.BlockSpec((tm, k), lambda i, j: (i, 0)), out_shape=jax.ShapeDtypeStruct((m, k), F32),
                 sem=("parallel", "arbitrary"))(*pieces, wt, after)


def weight_grad_t(name, pieces, x):
    m = pieces[0].shape[0]
    k = x.shape[1]
    tm = m // 2
    spans, steps = _piece_tiles(pieces)

    def body(*refs):
        x_ref, o_ref, acc_ref = refs[len(pieces):]
        i, half = pl.program_id(0), pl.program_id(1)
        for dy_ref, (first, count) in zip(refs, spans):
            @pl.when((i >= first) & (i < first + count))
            def _():
                part = lax.dot_general(dy_ref[...], x_ref[...], (((0,), (0,)), ((), ())), preferred_element_type=F32)

                @pl.when(half == 0)
                def _():
                    acc_ref[...] = part

                @pl.when(half == 1)
                def _():
                    o_ref[...] = (acc_ref[...] + part).astype(BF16)

    return _call(body, name=name, grid=(steps, 2),
                 in_specs=[_piece_spec(tm, s, lambda i, j: j, lambda i, j: i) for s in spans]
                 + [pl.BlockSpec((tm, k), lambda i, j: (j, 0))],
                 out_specs=pl.BlockSpec((D_MODEL, k), lambda i, j: (i, 0)),
                 out_shape=jax.ShapeDtypeStruct((steps * D_MODEL, k), BF16),
                 scratch=[pltpu.VMEM((D_MODEL, k), F32)], sem=("parallel", "arbitrary"))(*pieces, x)


def mm_tn(name, x, dy):
    m, k = x.shape
    n = dy.shape[1]
    tm = m // 2
    tn = _col_tile(n, k)

    def body(x_ref, dy_ref, o_ref, acc_ref):
        part = lax.dot_general(x_ref[...].astype(BF16), dy_ref[...].astype(BF16), (((0,), (0,)), ((), ())),
                               preferred_element_type=F32)

        @pl.when(pl.program_id(1) == 0)
        def _():
            acc_ref[...] = part

        @pl.when(pl.program_id(1) == 1)
        def _():
            o_ref[...] = (acc_ref[...] + part).astype(BF16)

    return _call(body, name=name, grid=(n // tn, 2),
                 in_specs=[pl.BlockSpec((tm, k), lambda i, j: (j, 0)), pl.BlockSpec((tm, tn), lambda i, j: (j, i))],
                 out_specs=pl.BlockSpec((k, tn), lambda i, j: (0, i)), out_shape=jax.ShapeDtypeStruct((k, n), BF16),
                 scratch=[pltpu.VMEM((k, tn), F32)], sem=("parallel", "arbitrary"))(x, dy)


def _row_spec(width, col_block=0):
    return pl.BlockSpec((BLOCK, width), lambda i: (i, col_block))


def _x_spec():
    return pl.BlockSpec((None, BLOCK, D_MODEL), lambda i: (0, jnp.maximum(i - 1, 0), 0))


def prep(x, meta, g_pre):
    nb = x.shape[1] // BLOCK + 1

    def body(x_ref, meta_ref, g_ref, h_ref, u_ref):
        i = pl.program_id(0)

        @pl.when(i == 0)
        def _():
            h_ref[0:PAD_ROWS, :] = jnp.zeros((PAD_ROWS, D_MODEL), F32)
            h_ref[PAD_ROWS:BLOCK, :] = meta_ref[...]

        @pl.when(i > 0)
        def _():
            h_ref[...] = x_ref[...]

        u_ref[...] = _rms(h_ref[...], g_ref[...]).astype(BF16)

    return _call(body, name="prep", grid=(nb,), in_specs=[_x_spec(), _full((N_META, D_MODEL)), _full((1, D_MODEL))],
                 out_specs=[_row_spec(D_MODEL), _row_spec(D_MODEL)],
                 out_shape=[jax.ShapeDtypeStruct((nb * BLOCK, D_MODEL), F32),
                            jax.ShapeDtypeStruct((nb * BLOCK, D_MODEL), BF16)], sem=("parallel",))(x, meta, g_pre)


def prep_bwd(h, du, dres, g_pre):
    nb = h.shape[0] // BLOCK

    def body(h_ref, du_ref, dres_ref, g_ref, gx_ref, gm_ref, gg_ref):
        i = pl.program_id(0)
        _, vjp = jax.vjp(_rms, h_ref[...], g_ref[...])
        dh, dg = vjp(du_ref[...])

        @pl.when(i == 0)
        def _():
            gm_ref[...] = dh[PAD_ROWS:BLOCK, :]
            gg_ref[...] = dg

        @pl.when(i > 0)
        def _():
            gg_ref[...] += dg

        gx_ref[...] = dh + dres_ref[...]

    return _call(body, name="prep_bwd", grid=(nb,),
                 in_specs=[_row_spec(D_MODEL), _row_spec(D_MODEL), _row_spec(D_MODEL), _full((1, D_MODEL))],
                 out_specs=[_x_spec(), _full((N_META, D_MODEL)), _full((1, D_MODEL))],
                 out_shape=[jax.ShapeDtypeStruct((1, (nb - 1) * BLOCK, D_MODEL), F32),
                            jax.ShapeDtypeStruct((N_META, D_MODEL), F32), jax.ShapeDtypeStruct((1, D_MODEL), F32)],
                 sem=("arbitrary",))(h, du, dres, g_pre)


GROUP_W = SSM_INNER // SSM_GROUPS


def _gated_norm(y, z, g):
    t = y * _silu(z)
    return t * lax.rsqrt(jnp.mean(t * t, axis=-1, keepdims=True) + NORM_EPS) * g


def _gated_norm_groups(y, z, g):
    groups = [slice(k * GROUP_W, (k + 1) * GROUP_W) for k in range(SSM_GROUPS)]
    return jnp.concatenate([_gated_norm(y[:, s], z[:, s], g[:, s]) for s in groups], axis=1)


def _merge(ga, gs, ya, ys):
    return _sigmoid(ga) * ya + _sigmoid(gs) * ys


GATE_ATT_BLOCK = SEG["gate_att"][2] // D_MODEL
GATE_SSM_BLOCK = SEG["gate_ssm"][2] // D_MODEL


def _row_loss(out, g_post, x, target):
    diff = x + _rms(out, g_post) - target
    return 0.5 * jnp.sum(diff * diff) / D_MODEL


def tail(y_ssd, proj, a_att, x, target, woa, wos, wo, g_norm, g_post):
    nb = y_ssd.shape[0] // BLOCK
    rows = nb * BLOCK

    def body(y_ref, z_ref, ga_ref, gs_ref, a_ref, x_ref, t_ref, woa_ref, wos_ref, wo_ref, gn_ref, gp_ref,
             yn_ref, mg_ref, dout_ref, dya_ref, dys_ref, da_ref, dy_ref, dz_ref, dga_ref, dgs_ref, dres_ref,
             loss_ref, dgp_ref, dgn_ref):
        i = pl.program_id(0)
        yn, norm_vjp = jax.vjp(_gated_norm_groups, y_ref[...], z_ref[...], gn_ref[...])
        yn16 = yn.astype(BF16)
        y_ssm = jnp.dot(yn16, wos_ref[...], preferred_element_type=F32)
        y_att = jnp.dot(a_ref[...], woa_ref[...], preferred_element_type=F32)
        merged, merge_vjp = jax.vjp(_merge, ga_ref[...], gs_ref[...], y_att, y_ssm)
        merged16 = merged.astype(BF16)
        out = jnp.dot(merged16, wo_ref[...], preferred_element_type=F32)
        loss, loss_vjp = jax.vjp(_row_loss, out, gp_ref[...], x_ref[...], t_ref[...])
        counted = jnp.where(i > 0, 1.0, 0.0)
        dout, dgp, dres, _ = loss_vjp(counted)
        dout16 = dout.astype(BF16)
        dmerged = lax.dot_general(dout16, wo_ref[...], _NT, preferred_element_type=F32)
        dga, dgs, dya, dys = merge_vjp(dmerged)
        dya16, dys16 = dya.astype(BF16), dys.astype(BF16)
        dyn = lax.dot_general(dys16, wos_ref[...], _NT, preferred_element_type=F32)
        dy, dz, dgn = norm_vjp(dyn)

        yn_ref[...] = yn16
        mg_ref[...] = merged16
        dout_ref[...] = dout16
        dya_ref[...] = dya16
        dys_ref[...] = dys16
        da_ref[...] = lax.dot_general(dya16, woa_ref[...], _NT, preferred_element_type=F32)
        dy_ref[...] = dy
        dz_ref[...] = dz.astype(BF16)
        dga_ref[...] = dga.astype(BF16)
        dgs_ref[...] = dgs.astype(BF16)
        dres_ref[...] = dres

        @pl.when(i == 0)
        def _():
            loss_ref[...] = jnp.zeros_like(loss_ref)
            dgp_ref[...] = jnp.zeros_like(dgp_ref)
            dgn_ref[...] = jnp.zeros_like(dgn_ref)

        loss_ref[...] += loss * counted
        dgp_ref[...] += dgp
        dgn_ref[...] += dgn

    wide, narrow = _row_spec(SSM_INNER), _row_spec(D_MODEL)
    resident = pl.BlockSpec(memory_space=pltpu.VMEM)
    bf = lambda w: jax.ShapeDtypeStruct((rows, w), BF16)
    f32 = lambda w: jax.ShapeDtypeStruct((rows, w), F32)
    return _call(body, name="tail", grid=(nb,),
                 in_specs=[wide, wide, _row_spec(D_MODEL, GATE_ATT_BLOCK), _row_spec(D_MODEL, GATE_SSM_BLOCK), narrow,
                           _x_spec(), _x_spec(), resident, resident, resident, _full((1, SSM_INNER)),
                           _full((1, D_MODEL))],
                 out_specs=[wide, narrow, narrow, narrow, narrow, narrow, wide, wide, narrow, narrow, narrow,
                            _full((8, LANES)), _full((1, D_MODEL)), _full((1, SSM_INNER))],
                 out_shape=[bf(SSM_INNER), bf(D_MODEL), bf(D_MODEL), bf(D_MODEL), bf(D_MODEL), f32(D_MODEL),
                            f32(SSM_INNER), bf(SSM_INNER), bf(D_MODEL), bf(D_MODEL), f32(D_MODEL),
                            jax.ShapeDtypeStruct((8, LANES), F32), jax.ShapeDtypeStruct((1, D_MODEL), F32),
                            jax.ShapeDtypeStruct((1, SSM_INNER), F32)],
                 sem=("arbitrary",))(y_ssd, proj, proj, proj, a_att, x, target, woa, wos, wo, g_norm, g_post)


_NT = (((1,), (1,)), ((), ()))
ALIBI_SLOPES = tuple(2.0 ** (-8.0 * (h + 1) / ATT_Q_HEADS) for h in range(ATT_Q_HEADS))
KV_WIDTH = ATT_KV_HEADS * HEAD_DIM
Q_BLOCK = SEG["q"][2] // D_MODEL
Z_ATT_BLOCK = SEG["z_att"][2] // D_MODEL
K_BLOCK = SEG["k"][2] // KV_WIDTH
V_BLOCK = SEG["v"][2] // KV_WIDTH
META_ROW_BLOCK = PAD_ROWS // N_META


@jax.custom_vjp
def _swap_halves(x):
    return pltpu.roll(x, HEAD_DIM, 1)


_swap_halves.defvjp(lambda x: (pltpu.roll(x, HEAD_DIM, 1), None), lambda _, g: (pltpu.roll(g, HEAD_DIM, 1),))


def _both_halves(t, half):
    first = lax.broadcasted_iota(jnp.int32, t.shape, 1) < HEAD_DIM
    sw = _swap_halves(t)
    return jnp.where(first, t, sw) if half == 0 else jnp.where(first, sw, t)


def _attn_rows(q, z, kp, kc, vp, vc, km, vm, sinks, n):
    rows = ATT_GROUP * BLOCK
    i = lax.broadcasted_iota(jnp.int32, (rows, BLOCK), 0) & (BLOCK - 1)
    j = lax.broadcasted_iota(jnp.int32, (rows, BLOCK), 1)
    rel_c = (i - j).astype(F32)
    rel_p = rel_c + float(BLOCK)
    nv = jnp.zeros((rows, BLOCK), jnp.int32) + n
    ok_c = (i >= j) & (nv >= 1)
    ok_p = (j > i) & (nv >= 2)
    im = lax.broadcasted_iota(jnp.int32, (rows, N_META), 0) & (BLOCK - 1)
    jm = lax.broadcasted_iota(jnp.int32, (rows, N_META), 1)
    ok_m = ((jnp.zeros((rows, N_META), jnp.int32) + n) >= 1) | (im >= PAD_ROWS + jm)
    first = lax.broadcasted_iota(jnp.int32, (BLOCK, LANES), 1) < HEAD_DIM
    neg = -jnp.inf
    outs = []
    for kv in range(ATT_KV_HEADS):
        tile, half = divmod(kv, 2)
        lanes = slice(tile * LANES, (tile + 1) * LANES)
        kc2, kp2, km2 = (_both_halves(t[:, lanes], half).astype(BF16) for t in (kc, kp, km))
        vc2, vp2, vm2 = (_both_halves(t[:, lanes], half).astype(BF16) for t in (vc, vp, vm))
        qs, slope, sk = [], [], []
        for pair in range(ATT_GROUP // 2):
            c0 = (kv * ATT_GROUP + 2 * pair) * HEAD_DIM
            qp = q[:, c0:c0 + LANES] * HEAD_DIM ** -0.5
            qs += [jnp.where(first, qp, 0.0), jnp.where(first, 0.0, qp)]
        for g in range(ATT_GROUP):
            slope.append(jnp.full((BLOCK, 1), ALIBI_SLOPES[kv * ATT_GROUP + g], F32))
            sk.append(jnp.broadcast_to(sinks[kv * ATT_GROUP + g], (BLOCK, 1)))
        qs = jnp.concatenate(qs, axis=0).astype(BF16)
        slope = jnp.concatenate(slope, axis=0)
        sk = jnp.concatenate(sk, axis=0)
        sc = jnp.where(ok_c, lax.dot_general(qs, kc2, _NT, preferred_element_type=F32) - slope * rel_c, neg)
        sp = jnp.where(ok_p, lax.dot_general(qs, kp2, _NT, preferred_element_type=F32) - slope * rel_p, neg)
        sm = jnp.where(ok_m, lax.dot_general(qs, km2, _NT, preferred_element_type=F32), neg)
        mx = jnp.maximum(jnp.maximum(jnp.max(sc, axis=1, keepdims=True), jnp.max(sp, axis=1, keepdims=True)),
                         jnp.maximum(jnp.max(sm, axis=1, keepdims=True), sk))
        mx = lax.stop_gradient(mx)
        ec, ep, em, es = jnp.exp(sc - mx), jnp.exp(sp - mx), jnp.exp(sm - mx), jnp.exp(sk - mx)
        den = (es + jnp.sum(ec, axis=1, keepdims=True) + jnp.sum(ep, axis=1, keepdims=True)
               + jnp.sum(em, axis=1, keepdims=True))
        inv = 1.0 / den
        o = (jnp.dot((ec * inv).astype(BF16), vc2, preferred_element_type=F32)
             + jnp.dot((ep * inv).astype(BF16), vp2, preferred_element_type=F32)
             + jnp.dot((em * inv).astype(BF16), vm2, preferred_element_type=F32))
        for pair in range(ATT_GROUP // 2):
            r0 = 2 * pair * BLOCK
            outs.append(jnp.where(first, o[r0:r0 + BLOCK], o[r0 + BLOCK:r0 + 2 * BLOCK]))
    return jnp.concatenate(outs, axis=1) * _silu(z)


def _attn_specs(nb, steps_clamped):
    def blk(t):
        return jnp.minimum(t, nb - 1) if steps_clamped else t

    wide = lambda col: pl.BlockSpec((BLOCK, D_MODEL), lambda t: (blk(t), col))
    cur = lambda col: pl.BlockSpec((BLOCK, KV_WIDTH), lambda t: (blk(t), col))
    prev = lambda col: pl.BlockSpec((BLOCK, KV_WIDTH), lambda t: (jnp.maximum(blk(t) - 1, 0), col))
    meta = lambda col: pl.BlockSpec((N_META, KV_WIDTH), lambda t: (META_ROW_BLOCK, col))
    sinks = pl.BlockSpec((ATT_Q_HEADS, 1, 1), lambda t: (0, 0, 0))
    return [wide(Q_BLOCK), wide(Z_ATT_BLOCK), prev(K_BLOCK), cur(K_BLOCK), prev(V_BLOCK), cur(V_BLOCK),
            meta(K_BLOCK), meta(V_BLOCK), sinks]


def attn_fwd(proj, sinks):
    nb = proj.shape[0] // BLOCK

    def body(q_ref, z_ref, kp_ref, kc_ref, vp_ref, vc_ref, km_ref, vm_ref, sk_ref, o_ref):
        o_ref[...] = _attn_rows(q_ref[...], z_ref[...], kp_ref[...], kc_ref[...], vp_ref[...], vc_ref[...],
                                km_ref[...], vm_ref[...], tuple(sk_ref[h] for h in range(ATT_Q_HEADS)),
                                pl.program_id(0)).astype(BF16)

    return _call(body, name="attn_fwd", grid=(nb,), in_specs=_attn_specs(nb, False), out_specs=_row_spec(D_MODEL),
                 out_shape=jax.ShapeDtypeStruct((nb * BLOCK, D_MODEL), BF16), sem=("parallel",))(*([proj] * 8), sinks)


def attn_bwd(da, proj, sinks):
    nb = proj.shape[0] // BLOCK
    last = nb - 1
    wide = pl.BlockSpec((BLOCK, D_MODEL), lambda t: (jnp.minimum(t, last), 0))
    done = pl.BlockSpec((BLOCK, KV_WIDTH), lambda t: (jnp.maximum(t - 1, 0), 0))
    meta = _full((N_META, KV_WIDTH))
    par = _full((ATT_Q_HEADS, 1, 1))

    def body(da_ref, q_ref, z_ref, kp_ref, kc_ref, vp_ref, vc_ref, km_ref, vm_ref, sk_ref,
             dq_ref, dz_ref, dk_ref, dv_ref, dkm_ref, dvm_ref, dsk_ref, ck_ref, cv_ref):
        t = pl.program_id(0)

        @pl.when(t == 0)
        def _():
            ck_ref[...] = jnp.zeros_like(ck_ref)
            cv_ref[...] = jnp.zeros_like(cv_ref)
            dkm_ref[...] = jnp.zeros_like(dkm_ref)
            dvm_ref[...] = jnp.zeros_like(dvm_ref)
            dsk_ref[...] = jnp.zeros_like(dsk_ref)

        @pl.when(t < nb)
        def _():
            def f(q, z, kp, kc, vp, vc, km, vm, sk):
                return _attn_rows(q, z, kp, kc, vp, vc, km, vm, sk, t)

            _, vjp = jax.vjp(f, q_ref[...], z_ref[...], kp_ref[...], kc_ref[...], vp_ref[...], vc_ref[...],
                             km_ref[...], vm_ref[...], tuple(sk_ref[h] for h in range(ATT_Q_HEADS)))
            dq, dz, dkp, dkc, dvp, dvc, dkm, dvm, dsk = vjp(da_ref[...])
            dq_ref[...] = dq.astype(BF16)
            dz_ref[...] = dz.astype(BF16)
            for h in range(ATT_Q_HEADS):
                dsk_ref[h] += dsk[h]
            dk_ref[...] = ck_ref[...] + dkp
            dv_ref[...] = cv_ref[...] + dvp
            ck_ref[...] = dkc
            cv_ref[...] = dvc
            dkm_ref[...] += dkm
            dvm_ref[...] += dvm

        @pl.when(t == nb)
        def _():
            dk_ref[...] = ck_ref[...]
            dv_ref[...] = cv_ref[...]

    rows = nb * BLOCK
    return _call(body, name="attn_bwd", grid=(nb + 1,), in_specs=[wide] + _attn_specs(nb, True),
                 out_specs=[wide, wide, done, done, meta, meta, par],
                 out_shape=[jax.ShapeDtypeStruct((rows, D_MODEL), BF16), jax.ShapeDtypeStruct((rows, D_MODEL), BF16),
                            jax.ShapeDtypeStruct((rows, KV_WIDTH), F32), jax.ShapeDtypeStruct((rows, KV_WIDTH), F32),
                            jax.ShapeDtypeStruct((N_META, KV_WIDTH), F32), jax.ShapeDtypeStruct((N_META, KV_WIDTH), F32),
                            jax.ShapeDtypeStruct(sinks.shape, F32)],
                 scratch=[pltpu.VMEM((BLOCK, KV_WIDTH), F32), pltpu.VMEM((BLOCK, KV_WIDTH), F32)],
                 sem=("arbitrary",))(da, *([proj] * 8), sinks)


XBC_BLOCK0 = SEG["xbc"][2] // D_MODEL
CONV_COL_BLOCKS = CONV_DIM // D_MODEL
DT_TILE = SEG["dt"][2] // LANES


HALO = 8


def _conv_rows(length):
    return 544 if length % 544 == 0 else BLOCK


def _shift_rows(cur, before, j):
    if j == 0:
        return cur
    n = cur.shape[0]
    row = lax.broadcasted_iota(jnp.int32, cur.shape, 0)
    head = pltpu.roll(before, j, 0)
    if n > HALO:
        head = jnp.concatenate([head, jnp.zeros((n - HALO, cur.shape[1]), cur.dtype)], axis=0)
    return jnp.where(row >= j, pltpu.roll(cur, j, 0), head)


def _conv_pre(cur, before, w_ref, b_ref):
    pre = b_ref[...] + w_ref[CONV_WIDTH - 1:CONV_WIDTH, :] * cur
    for k in range(CONV_WIDTH - 1):
        pre = pre + w_ref[k:k + 1, :] * _shift_rows(cur, before, CONV_WIDTH - 1 - k)
    return pre


def _conv_specs(steps, rows, col0=0):
    first = XBC_BLOCK0 + col0
    halos = rows // HALO
    cur = pl.BlockSpec((rows, D_MODEL), lambda j, i: (i, first + j))
    before = pl.BlockSpec((HALO, D_MODEL), lambda j, i: (jnp.maximum(i * halos - 1, 0), first + j))
    after = pl.BlockSpec((HALO, D_MODEL), lambda j, i: (jnp.minimum(i + 1, steps - 1) * halos, first + j))
    return cur, before, after


def _valid_rows(i, rows):
    row = lax.broadcasted_iota(jnp.int32, (rows, D_MODEL), 0)
    return jnp.maximum((row >= PAD_ROWS).astype(F32), jnp.where(i > 0, 1.0, 0.0))


def conv_fwd(proj, conv_w, conv_b):
    rows = _conv_rows(proj.shape[0])
    steps = proj.shape[0] // rows
    cur, before, _ = _conv_specs(steps, rows)

    def body(c_ref, p_ref, w_ref, b_ref, o_ref):
        i = pl.program_id(1)
        pre = _conv_pre(c_ref[...], p_ref[...] * jnp.where(i > 0, 1.0, 0.0), w_ref, b_ref)
        o_ref[...] = _silu(pre) * _valid_rows(i, rows)

    return _call(body, name="conv_fwd", grid=(CONV_COL_BLOCKS, steps),
                 in_specs=[cur, before, pl.BlockSpec((CONV_WIDTH, D_MODEL), lambda j, i: (0, j)),
                           pl.BlockSpec((1, D_MODEL), lambda j, i: (0, j))],
                 out_specs=pl.BlockSpec((rows, D_MODEL), lambda j, i: (i, j)),
                 out_shape=jax.ShapeDtypeStruct((proj.shape[0], CONV_DIM), F32),
                 sem=("parallel", "parallel"))(proj, proj, conv_w, conv_b)


def conv_bwd(name, dparts, col0, proj, conv_w, conv_b):
    rows = _conv_rows(proj.shape[0])
    steps = proj.shape[0] // rows
    last = steps - 1
    ncol = sum(d.shape[1] for d in dparts) // D_MODEL
    np_ = len(dparts)
    cur, before, after = _conv_specs(steps, rows, col0)
    dcur = [pl.BlockSpec((rows, d.shape[1] // ncol), lambda j, i: (i, j)) for d in dparts]
    dafter = [pl.BlockSpec((HALO, d.shape[1] // ncol), lambda j, i: (jnp.minimum(i + 1, last) * (rows // HALO), j))
              for d in dparts]
    out_cur = pl.BlockSpec((rows, D_MODEL), lambda j, i: (i, j))
    wspec = pl.BlockSpec((CONV_WIDTH, D_MODEL), lambda j, i: (0, col0 + j))
    bspec = pl.BlockSpec((1, D_MODEL), lambda j, i: (0, col0 + j))
    wout = pl.BlockSpec((CONV_WIDTH, D_MODEL), lambda j, i: (0, j))
    bout = pl.BlockSpec((1, D_MODEL), lambda j, i: (0, j))

    def body(*refs):
        dc_refs, da_refs = refs[:np_], refs[np_:2 * np_]
        c_ref, p_ref, a_ref, w_ref, b_ref, du_ref, dw_ref, db_ref = refs[2 * np_:]
        i = pl.program_id(1)
        row = lax.broadcasted_iota(jnp.int32, (rows, D_MODEL), 0)
        curv = c_ref[...]
        beforev = p_ref[...] * jnp.where(i > 0, 1.0, 0.0)
        side_by_side = lambda rs: rs[0][...] if np_ == 1 else jnp.concatenate([r[...] for r in rs], axis=1)

        def dpre_of(pre, d):
            s = _sigmoid(pre)
            return d * (s * (1.0 + pre * (1.0 - s)))

        dp_c = dpre_of(_conv_pre(curv, beforev, w_ref, b_ref), side_by_side(dc_refs) * _valid_rows(i, rows))
        dp_a = dpre_of(_conv_pre(a_ref[...], curv[rows - HALO:], w_ref, b_ref),
                       side_by_side(da_refs) * jnp.where(i < last, 1.0, 0.0))
        du = w_ref[CONV_WIDTH - 1:CONV_WIDTH, :] * dp_c
        for j in range(1, CONV_WIDTH):
            tail = jnp.concatenate([jnp.zeros((rows - HALO, D_MODEL), F32), pltpu.roll(dp_a, HALO - j, 0)], axis=0)
            up = jnp.where(row < rows - j, pltpu.roll(dp_c, rows - j, 0), tail)
            du = du + w_ref[CONV_WIDTH - 1 - j:CONV_WIDTH - j, :] * up
        du_ref[...] = du.astype(BF16)

        @pl.when(i == 0)
        def _():
            dw_ref[...] = jnp.zeros_like(dw_ref)
            db_ref[...] = jnp.zeros_like(db_ref)

        for k in range(CONV_WIDTH):
            dw_ref[k:k + 1, :] += jnp.sum(dp_c * _shift_rows(curv, beforev, CONV_WIDTH - 1 - k), axis=0, keepdims=True)
        db_ref[...] += jnp.sum(dp_c, axis=0, keepdims=True)

    width = ncol * D_MODEL
    return _call(body, name=name, grid=(ncol, steps),
                 in_specs=dcur + dafter + [cur, before, after, wspec, bspec], out_specs=[out_cur, wout, bout],
                 out_shape=[jax.ShapeDtypeStruct((proj.shape[0], width), BF16),
                            jax.ShapeDtypeStruct((CONV_WIDTH, width), F32), jax.ShapeDtypeStruct((1, width), F32)],
                 sem=("parallel", "arbitrary"))(*dparts, *dparts, proj, proj, proj, conv_w, conv_b)


def _head_expand():
    e = np.zeros((LANES, SSM_INNER), np.float32)
    for h in range(SSM_HEADS):
        e[h, h * HEAD_DIM:(h + 1) * HEAD_DIM] = 1.0
    return jnp.asarray(e, dtype=BF16)


def _softplus(x):
    return jnp.maximum(x, 0.0) + jnp.log(1.0 + jnp.exp(-jnp.abs(x)))


def _bf16_parts(x):
    hi = x.astype(BF16)
    rest = x - hi.astype(F32)
    mid = rest.astype(BF16)
    return hi, mid, (rest - mid.astype(F32)).astype(BF16)


@jax.custom_vjp
def _times_01(x, m):
    return sum(jnp.dot(p, m, preferred_element_type=F32) for p in _bf16_parts(x))


def _times_01_bwd(m, g):
    return sum(lax.dot_general(p, m, _NT, preferred_element_type=F32) for p in _bf16_parts(g)), jnp.zeros_like(m)


_times_01.defvjp(lambda x, m: (_times_01(x, m), m), _times_01_bwd)


def _causal_ones():
    l = lax.broadcasted_iota(jnp.int32, (BLOCK, BLOCK), 0)
    s = lax.broadcasted_iota(jnp.int32, (BLOCK, BLOCK), 1)
    return (l >= s).astype(BF16)


@jax.custom_vjp
def _cumsum_rows(a):
    return sum(jnp.dot(_causal_ones(), p, preferred_element_type=F32) for p in _bf16_parts(a))


def _cumsum_rows_bwd(_, g):
    tn = (((0,), (0,)), ((), ()))
    return (sum(lax.dot_general(_causal_ones(), p, tn, preferred_element_type=F32) for p in _bf16_parts(g)),)


_cumsum_rows.defvjp(lambda a: (_cumsum_rows(a), None), _cumsum_rows_bwd)


def _ssd_group(xs, dt_tile, expand, bias, alog, dsk, bg, cg, state):
    l = lax.broadcasted_iota(jnp.int32, (BLOCK, BLOCK), 0)
    s = lax.broadcasted_iota(jnp.int32, (BLOCK, BLOCK), 1)
    causal = l >= s
    first_head = s < HEAD_DIM
    dt = _softplus(dt_tile + bias)
    a = dt * (-jnp.exp(alog))
    one_row = lambda v: jnp.broadcast_to(v, (HALO, LANES))
    per_lane = _times_01(jnp.concatenate([dt, _cumsum_rows(a), one_row(jnp.sum(a, axis=0, keepdims=True)),
                                          one_row(dsk)], axis=0), expand)
    dtx, cs = per_lane[0:BLOCK], per_lane[BLOCK:2 * BLOCK]
    tot, dsk = per_lane[2 * BLOCK:2 * BLOCK + 1], per_lane[2 * BLOCK + HALO:2 * BLOCK + HALO + 1]
    bb, cb16 = bg.astype(BF16), cg.astype(BF16)
    cb = lax.dot_general(cb16, bb, _NT, preferred_element_type=F32)
    xr = xs * dtx
    y_diag = []
    for p in range(GROUP_W // LANES):
        lanes = slice(p * LANES, (p + 1) * LANES)
        c_pair = cs[:, lanes]
        c_swap = _swap_halves(c_pair)
        m = []
        for c_head in (jnp.where(first_head, c_pair, c_swap), jnp.where(first_head, c_swap, c_pair)):
            m.append(cb * jnp.exp(jnp.where(causal, c_head - c_head.T, -jnp.inf)))
        x_pair = xr[:, lanes]
        x_diag = jnp.concatenate([jnp.where(first_head, x_pair, 0.0), jnp.where(first_head, 0.0, x_pair)], axis=0)
        y_diag.append(jnp.dot(jnp.concatenate(m, axis=1).astype(BF16), x_diag.astype(BF16),
                              preferred_element_type=F32))
    st = lax.dot_general(bb, (xr * jnp.exp(tot - cs)).astype(BF16), (((0,), (0,)), ((), ())),
                         preferred_element_type=F32)
    new_state = state * jnp.exp(tot) + st
    y_off = jnp.dot(cb16, state.astype(BF16), preferred_element_type=F32) * jnp.exp(cs)
    return jnp.concatenate(y_diag, axis=1) + y_off + dsk * xs, new_state


BC_WIDTH = SSM_GROUPS * SSM_STATE


def _ssd_specs(chunk):
    xs = pl.BlockSpec((BLOCK, SSM_INNER), lambda c: (chunk(c), 0))
    dt = pl.BlockSpec((BLOCK, LANES), lambda c: (chunk(c), DT_TILE))
    expand = _full((LANES, SSM_INNER))
    b = pl.BlockSpec((BLOCK, BC_WIDTH), lambda c: (chunk(c), SSM_INNER // BC_WIDTH))
    cc = pl.BlockSpec((BLOCK, BC_WIDTH), lambda c: (chunk(c), SSM_INNER // BC_WIDTH + 1))
    par = _full((1, LANES))
    state = pl.BlockSpec((None, SSM_STATE, SSM_INNER), lambda c: (chunk(c), 0, 0))
    return xs, dt, expand, b, cc, par, state


def _group_lanes(g):
    return slice(g * GROUP_W, (g + 1) * GROUP_W), slice(g * SSM_STATE, (g + 1) * SSM_STATE)


def ssd_fwd(xbc, proj, expand, bias, alog, dsk):
    nb = xbc.shape[0] // BLOCK
    xs, dt, ex, b, cc, par, state = _ssd_specs(lambda c: c)

    def body(x_ref, dt_ref, e_ref, bi_ref, al_ref, dk_ref, b_ref, c_ref, y_ref, sp_ref, st_ref):
        @pl.when(pl.program_id(0) == 0)
        def _():
            st_ref[...] = jnp.zeros_like(st_ref)

        for g in range(SSM_GROUPS):
            wide, tile = _group_lanes(g)
            entering = st_ref[:, wide]
            sp_ref[:, wide] = entering
            y_ref[:, wide], st_ref[:, wide] = _ssd_group(
                x_ref[:, wide], dt_ref[...], e_ref[:, wide], bi_ref[...], al_ref[...], dk_ref[...], b_ref[:, tile],
                c_ref[:, tile], entering)

    return _call(body, name="ssd_fwd", grid=(nb,), in_specs=[xs, dt, ex, par, par, par, b, cc],
                 out_specs=[xs, state],
                 out_shape=[jax.ShapeDtypeStruct((nb * BLOCK, SSM_INNER), F32),
                            jax.ShapeDtypeStruct((nb, SSM_STATE, SSM_INNER), F32)],
                 scratch=[pltpu.VMEM((SSM_STATE, SSM_INNER), F32)],
                 sem=("arbitrary",))(xbc, proj, expand, bias, alog, dsk, xbc, xbc)


def ssd_bwd(dy, xbc, proj, expand, bias, alog, dsk, states, comm):
    nb = xbc.shape[0] // BLOCK
    last = nb - 1
    xs, dt, ex, b, cc, par, state = _ssd_specs(lambda c: last - c)
    tile = pl.BlockSpec((BLOCK, LANES), lambda c: (last - c, 0))
    nspec = pl.BlockSpec((BLOCK, BC_WIDTH), lambda c: (last - c, 0))

    def body(dy_ref, x_ref, dt_ref, e_ref, bi_ref, al_ref, dk_ref, b_ref, c_ref, sp_ref,
             dx_ref, ddt_ref, db_ref, dc_ref, dbi_ref, dal_ref, ddk_ref, ds_ref):
        @pl.when(pl.program_id(0) == 0)
        def _():
            ds_ref[...] = jnp.zeros_like(ds_ref)
            dbi_ref[...] = jnp.zeros_like(dbi_ref)
            dal_ref[...] = jnp.zeros_like(dal_ref)
            ddk_ref[...] = jnp.zeros_like(ddk_ref)

        ddt, dbi, dal, ddk = 0.0, 0.0, 0.0, 0.0
        for g in range(SSM_GROUPS):
            wide, tile_lanes = _group_lanes(g)
            expand_rows = e_ref[:, wide]

            def f(xs, dt_tile, bias, alog, dsk, bg, cg, state):
                return _ssd_group(xs, dt_tile, expand_rows, bias, alog, dsk, bg, cg, state)

            _, vjp = jax.vjp(f, x_ref[:, wide], dt_ref[...], bi_ref[...], al_ref[...], dk_ref[...], b_ref[:, tile_lanes],
                             c_ref[:, tile_lanes], sp_ref[:, wide])
            (dx_ref[:, wide], ddt_g, dbi_g, dal_g, ddk_g, db_ref[:, tile_lanes], dc_ref[:, tile_lanes],
             ds_ref[:, wide]) = vjp((dy_ref[:, wide], ds_ref[:, wide]))
            ddt, dbi, dal, ddk = ddt + ddt_g, dbi + dbi_g, dal + dal_g, ddk + ddk_g
        ddt_ref[...] = ddt.astype(BF16)
        dbi_ref[...] += dbi
        dal_ref[...] += dal
        ddk_ref[...] += ddk

    def at():
        c = pl.program_id(0)
        return c == 0, c == 0, c == last

    par_shape = jax.ShapeDtypeStruct((1, LANES), F32)
    return _call_with_comm(
        body, comm, at, (dy, xbc, proj, expand, bias, alog, dsk, xbc, xbc, states), name="ssd_bwd",
        grid=(nb,), in_specs=[xs, xs, dt, ex, par, par, par, b, cc, state],
        out_specs=[xs, tile, nspec, nspec, par, par, par],
        out_shape=[jax.ShapeDtypeStruct((nb * BLOCK, SSM_INNER), F32), jax.ShapeDtypeStruct((nb * BLOCK, LANES), BF16),
                   jax.ShapeDtypeStruct((nb * BLOCK, BC_WIDTH), F32), jax.ShapeDtypeStruct((nb * BLOCK, BC_WIDTH), F32),
                   par_shape, par_shape, par_shape],
        scratch=[pltpu.VMEM((SSM_STATE, SSM_INNER), F32)])


SLAB_ROWS = 24
SLAB_META_ROW = 8


SLAB_LOSS_ROW = 7


def pack_small(dcw, dcb, dgpre, dgpost, dbias, dalog, ddsk, dsinks, dgn, dmeta, loss_tile):
    def body(cw, cb, gpre, gpost, dtb, al, dk, sk, gn, meta, loss, o_ref):
        o_ref[...] = jnp.zeros_like(o_ref)
        o_ref[SLAB_LOSS_ROW:SLAB_LOSS_ROW + 1, 0:LANES] = loss[0:1, :]
        o_ref[0:CONV_WIDTH, :] = cw[...]
        o_ref[4:5, :] = cb[...]
        o_ref[5:6, 0:1024] = gpre[...]
        o_ref[5:6, 1024:2048] = gpost[...]
        o_ref[5:6, 2048:2176] = dtb[...]
        o_ref[5:6, 2176:2304] = al[...]
        o_ref[5:6, 2304:2432] = dk[...]
        o_ref[5:6, 2432:2560] = sk[...]
        o_ref[6:7, 0:SSM_INNER] = gn[...]
        o_ref[SLAB_META_ROW:SLAB_META_ROW + N_META, 0:D_MODEL] = meta[...]

    args = (dcw, dcb, dgpre, dgpost, dbias, dalog, ddsk, dsinks, dgn, dmeta, loss_tile)
    return _call(body, name="pack_small", in_specs=[_full(a.shape) for a in args],
                 out_specs=_full((SLAB_ROWS, CONV_DIM)), out_shape=jax.ShapeDtypeStruct((SLAB_ROWS, CONV_DIM), F32))(*args)


def _lane_tile(v):
    return jnp.pad(v, ((0, 0), (0, LANES - v.shape[1])))


def kernel(x, meta_tokens, g_pre, w_in, conv_w, conv_b, dt_bias, a_log, d_skip, attn_sinks, g_ssm_norm, w_out_att, w_out_ssm, w_out, g_post, loss_target, m_meta_tokens, m_g_pre, m_w_in, m_conv_w, m_conv_b, m_dt_bias, m_a_log, m_d_skip, m_attn_sinks, m_g_ssm_norm, m_w_out_att, m_w_out_ssm, m_w_out, m_g_post, v_meta_tokens, v_g_pre, v_w_in, v_conv_w, v_conv_b, v_dt_bias, v_a_log, v_d_skip, v_attn_sinks, v_g_ssm_norm, v_w_out_att, v_w_out_ssm, v_w_out, v_g_post):
    chip = _chip_index()

    conv_w_rows = jnp.pad(conv_w[0], ((0, 2 * 8 - CONV_WIDTH), (0, 0)))
    w_in_t, m_w_in_t, v_w_in_t = w_in[0].T, m_w_in[0].T, v_w_in[0].T
    gathered_w_in, g_conv_w, g_meta = run_comm("gather_w_in",
                                               TwoLevelGather([pack_w_in(w_in_t), conv_w_rows, meta_tokens]))
    w_all_t = unpack_w_in(gathered_w_in)
    cw_full = g_conv_w[:, :CONV_WIDTH].transpose(1, 0, 2).reshape(CONV_WIDTH, CONV_DIM)
    meta_full = g_meta.transpose(1, 0, 2).reshape(N_META, D_MODEL)

    h, u = prep(x, meta_full, g_pre)
    proj, g_w_out = project("in_proj", u, w_all_t, TwoLevelGather(
        [w_out_att[0].astype(BF16), w_out_ssm[0].astype(BF16), w_out[0].astype(BF16)]))
    woa = g_w_out[0].reshape(D_MODEL, D_MODEL)
    wos = g_w_out[1].reshape(SSM_INNER, D_MODEL)
    wo = g_w_out[2].reshape(D_MODEL, D_MODEL)

    sinks3 = attn_sinks.reshape(ATT_Q_HEADS, 1, 1)
    a_att = attn_fwd(proj, sinks3)

    xbc = conv_fwd(proj, cw_full, conv_b)
    expand = _head_expand()
    head_pars = (_lane_tile(dt_bias), _lane_tile(a_log), _lane_tile(d_skip))
    y_ssd, states = ssd_fwd(xbc, proj, expand, *head_pars)

    (yn, merged, dout, dy_att, dy_ssm, da_att, dy_ssd, dz_ssm, dga, dgs, dres, loss_tile, dg_post, dgn) = tail(
        y_ssd, proj, a_att, x, loss_target, woa, wos, wo, g_ssm_norm, g_post)

    dwo = mm_tn("out_proj_dw", merged, dout)
    dwoa = mm_tn("att_out_dw", a_att, dy_att)
    dwos = mm_tn("ssm_out_dw", yn, dy_ssm)
    dq, dz_att, dk, dv, dkmeta, dvmeta, dsinks3 = attn_bwd(da_att, proj, sinks3)
    dk = dk.at[PAD_ROWS:BLOCK].add(dkmeta).astype(BF16)
    dv = dv.at[PAD_ROWS:BLOCK].add(dvmeta).astype(BF16)

    def pieces(g):
        return g.reshape(4, 2, g.shape[0] // 8, g.shape[1])

    def to_owner(g):
        return (lambda ref, dev: ref.at[_chip_of(dev), dev[2]], (g.shape[0] // 8, g.shape[1]))

    (dxs, ddt_tile, dbg, dcg, dbias, dalog, ddsk), sent_w_out = ssd_bwd(
        dy_ssd, xbc, proj, expand, *head_pars, states,
        DirectExchange([pieces(dwoa), pieces(dwos), pieces(dwo)], [to_owner(dwoa), to_owner(dwos), to_owner(dwo)],
                       ALL_MASKS, "dev", 8))
    dxs_raw, dcw_xs, dcb_xs = conv_bwd("conv_bwd_x", [dxs], 0, proj, cw_full, conv_b)
    dbc_raw, dcw_bc, dcb_bc = conv_bwd("conv_bwd_bc", [dbg, dcg], SSM_INNER // D_MODEL, proj, cw_full, conv_b)
    dcw = jnp.concatenate([dcw_xs, dcw_bc], axis=1)
    dcb = jnp.concatenate([dcb_xs, dcb_bc], axis=1)

    narrow = jnp.concatenate([dk, dv, ddt_tile, jnp.zeros((dk.shape[0], N_ACT - N_ALIGNED), BF16)], axis=1)
    dproj = [dz_ssm, dxs_raw, dbc_raw, dq, dz_att, dga, dgs, narrow]
    dw_all_t = weight_grad_t("in_proj_dw", dproj, u)

    half_rows = PACK_W // 2
    partial = pack_grad_w_in(dw_all_t).reshape(4, 2, half_rows, D_MODEL)
    from_sibling, = run_comm("pair_grads", DirectExchange(
        [partial], [(lambda ref, dev: ref.at[pl.ds(0, 4), dev[2]], (4, half_rows, D_MODEL))], SIBLING_MASK, "core", 2,
        keep_own=False))
    chip_sum = sum_pair(partial, from_sibling)
    *in_flight, started = chip_exchange_start(chip_sum)
    du = project_back("in_proj_dx", dproj, w_all_t, started)
    grad_x, dmeta, dg_pre = prep_bwd(h, du, dres, g_pre)
    sent_w_in = chip_exchange_wait(*in_flight, dg_pre)

    slab = pack_small(dcw, dcb, dg_pre, dg_post, dbias, dalog, ddsk, _lane_tile(dsinks3.reshape(1, ATT_Q_HEADS)), dgn,
                      dmeta, loss_tile)
    halves = [sum_slots("sum_" + nm, r)
              for nm, r in zip(("w_in", "w_out_att", "w_out_ssm", "w_out"), [sent_w_in] + list(sent_w_out))]
    shared = run_comm("share_grads", Both(DirectExchange(halves, [None] * 4, SIBLING_MASK, "core", 2),
                                          DirectExchange([slab], [None], ALL_MASKS, "dev", 8)))
    g_w_in_packed, g_woa, g_wos, g_wo = [f.reshape(2 * f.shape[1], f.shape[2]) for f in shared[:4]]
    small = sum_slots("sum_small", shared[4])
    loss = small[SLAB_LOSS_ROW, 0]

    g_w_in, d_w_in, nm_w_in, nv_w_in = [a.T for a in adamw_w_in(g_w_in_packed, w_in_t, m_w_in_t, v_w_in_t)]
    d_woa, nm_woa, nv_woa = adamw_rows("adamw_w_out_att", g_woa, w_out_att[0], m_w_out_att[0], v_w_out_att[0])
    d_wos, nm_wos, nv_wos = adamw_rows("adamw_w_out_ssm", g_wos, w_out_ssm[0], m_w_out_ssm[0], v_w_out_ssm[0])
    d_wo, nm_wo, nv_wo = adamw_rows("adamw_w_out", g_wo, w_out[0], m_w_out[0], v_w_out[0])

    cw_cols = CONV_DIM // 4
    meta_cols = D_MODEL // 4
    g_small = {
        "meta_tokens": lax.dynamic_slice(small, (SLAB_META_ROW, chip * meta_cols), (N_META, meta_cols)),
        "g_pre": small[5:6, 0:1024],
        "conv_w": lax.dynamic_slice(small, (0, chip * cw_cols), (CONV_WIDTH, cw_cols)),
        "conv_b": small[4:5, :],
        "dt_bias": small[5:6, 2048:2048 + SSM_HEADS],
        "a_log": small[5:6, 2176:2176 + SSM_HEADS],
        "d_skip": small[5:6, 2304:2304 + SSM_HEADS],
        "attn_sinks": small[5:6, 2432:2432 + ATT_Q_HEADS],
        "g_ssm_norm": small[6:7, 0:SSM_INNER],
        "g_post": small[5:6, 1024:2048],
    }
    names = list(g_small)
    w_small = dict(meta_tokens=meta_tokens, g_pre=g_pre, conv_w=conv_w[0], conv_b=conv_b, dt_bias=dt_bias, a_log=a_log,
                   d_skip=d_skip, attn_sinks=attn_sinks, g_ssm_norm=g_ssm_norm, g_post=g_post)
    m_small = dict(meta_tokens=m_meta_tokens, g_pre=m_g_pre, conv_w=m_conv_w[0], conv_b=m_conv_b, dt_bias=m_dt_bias,
                   a_log=m_a_log, d_skip=m_d_skip, attn_sinks=m_attn_sinks, g_ssm_norm=m_g_ssm_norm, g_post=m_g_post)
    v_small = dict(meta_tokens=v_meta_tokens, g_pre=v_g_pre, conv_w=v_conv_w[0], conv_b=v_conv_b, dt_bias=v_dt_bias,
                   a_log=v_a_log, d_skip=v_d_skip, attn_sinks=v_attn_sinks, g_ssm_norm=v_g_ssm_norm, g_post=v_g_post)
    upd = dict(zip(names, adamw_small([g_small[k] for k in names], [w_small[k] for k in names],
                                      [m_small[k] for k in names], [v_small[k] for k in names])))

    lead = {"conv_w"}

    def shaped(name, a):
        return a[None] if name in lead else a

    grads = dict(g_small, w_in=g_w_in, w_out_att=g_woa, w_out_ssm=g_wos, w_out=g_wo)
    deltas = dict({k: upd[k][0] for k in names}, w_in=d_w_in, w_out_att=d_woa, w_out_ssm=d_wos, w_out=d_wo)
    new_m = dict({k: upd[k][1] for k in names}, w_in=nm_w_in, w_out_att=nm_woa, w_out_ssm=nm_wos, w_out=nm_wo)
    new_v = dict({k: upd[k][2] for k in names}, w_in=nv_w_in, w_out_att=nv_woa, w_out_ssm=nv_wos, w_out=nv_wo)
    lead |= {"w_in", "w_out_att", "w_out_ssm", "w_out"}
    order = ["meta_tokens", "g_pre", "w_in", "conv_w", "conv_b", "dt_bias", "a_log", "d_skip", "attn_sinks",
             "g_ssm_norm", "w_out_att", "w_out_ssm", "w_out", "g_post"]
    outs = [loss, grad_x]
    for group in (grads, deltas, new_m, new_v):
        outs += [shaped(k, group[k]) for k in order]
    return tuple(outs)
```

```python
import functools

import numpy as np
import jax
import jax.numpy as jnp
from jax import lax
from jax.experimental import pallas as pl
from jax.experimental.pallas import tpu as pltpu

F32 = jnp.float32
BF16 = jnp.bfloat16
HI = lax.Precision.HIGHEST

D_MODEL = 1024
N_META = 16
BLOCK = 128
PAD_ROWS = BLOCK - N_META
NORM_EPS = 1e-6
HEAD_DIM = 64
ATT_Q_HEADS = 16
ATT_KV_HEADS = 4
ATT_GROUP = 4
SSM_INNER = 2048
SSM_HEADS = 32
SSM_GROUPS = 4
SSM_HEADS_PER_GROUP = 8
SSM_STATE = 128
CONV_WIDTH = 4
CONV_DIM = 3072
LANES = 128

ADAM_LR = 0.001
ADAM_B1 = 0.9
ADAM_B2 = 0.999
ADAM_EPS = 1e-08
ADAM_WD = 0.01
ADAM_STEP = 10

VMEM_LIMIT = 48 * 1024 * 1024

SHARD_W = 2440
PACK_W = 2560
SHARD_STRIDE = 2432
N_ALIGNED = 9856
N_ACT = 10240
SEG = {
    "q": (0, 1024, 5120), "k": (1024, 256, 9216), "v": (1280, 256, 9472), "z_att": (1536, 1024, 6144),
    "z_ssm": (2560, 2048, 0), "xbc": (4608, 3072, 2048), "dt": (7680, 128, 9728),
    "gate_att": (7808, 1024, 7168), "gate_ssm": (8832, 1024, 8192),
}
DT_STORED_START = 7680
DT_PAD = LANES - SSM_HEADS


def _act_col(aligned_col):
    for a0, w, p0 in SEG.values():
        if a0 <= aligned_col < a0 + w:
            return p0 + aligned_col - a0
    raise ValueError(aligned_col)


def _call(body, *, name, out_shape, in_specs, out_specs, grid=(), scratch=(), sem=None, aliases=None):
    return pl.pallas_call(
        body, out_shape=out_shape, grid=grid, in_specs=in_specs, out_specs=out_specs, scratch_shapes=list(scratch),
        name=name, input_output_aliases=aliases or {},
        compiler_params=pltpu.CompilerParams(dimension_semantics=sem, vmem_limit_bytes=VMEM_LIMIT))


def _full(shape):
    n = len(shape)
    return pl.BlockSpec(shape, lambda *_: (0,) * n)


def _chip_index():
    return lax.axis_index("x") * 2 + lax.axis_index("y")


_sigmoid = jax.nn.sigmoid


def _silu(z):
    return z * _sigmoid(z)


def _rms(x, g):
    return x * lax.rsqrt(jnp.mean(x * x, axis=-1, keepdims=True) + NORM_EPS) * g


def _peer(mask):
    x, y, c = lax.axis_index("x"), lax.axis_index("y"), lax.axis_index("c")
    return ((1 - x) if mask & 4 else x, (1 - y) if mask & 2 else y, (1 - c) if mask & 1 else c)


def _me():
    return lax.axis_index("x"), lax.axis_index("y"), lax.axis_index("c")


def _chip_of(dev):
    return 2 * dev[0] + dev[1]


CHIP_MASKS = (4, 2, 6)
ALL_MASKS = (1, 2, 3, 4, 5, 6, 7)
SIBLING_MASK = (1,)


def _remote(src, dst, send_sem, recv_sem, dev):
    return pltpu.make_async_remote_copy(src_ref=src, dst_ref=dst, send_sem=send_sem, recv_sem=recv_sem,
                                        device_id=dev, device_id_type=pl.DeviceIdType.MESH)


class _StagedCopy:
    def __init__(self, src, stage, dst, load_sem, store_sem):
        self.load = pltpu.make_async_copy(src, stage, load_sem)
        self.store = pltpu.make_async_copy(stage, dst, store_sem)

    def start(self):
        self.load.start()
        self.load.wait()
        self.store.start()

    def wait(self):
        self.store.wait()


class DirectExchange:
    def __init__(self, arrays, pieces, masks, slot_kind, nslots, keep_own=True):
        self.arrays, self.pieces, self.masks, self.slot_kind = list(arrays), list(pieces), masks, slot_kind
        self.keep_own = keep_own
        n, nk = len(arrays), len(masks)
        shapes = [a.shape if p is None else p[1] for a, p in zip(arrays, pieces)]
        self.out_shape = [jax.ShapeDtypeStruct((nslots,) + tuple(s), a.dtype) for s, a in zip(shapes, arrays)]
        self.scratch = [pltpu.SemaphoreType.DMA((n * nk,)), pltpu.SemaphoreType.DMA((n * nk,))]
        if keep_own:
            self.scratch += [pltpu.SemaphoreType.DMA((2 * n,))] + [pltpu.VMEM(s, a.dtype) for s, a in zip(shapes, arrays)]
        self.has_mid = False

    def _copies(self, ins, outs, scratch):
        send_sems, recv_sems = scratch[:2]
        me = _me()
        slot = {"chip": _chip_of(me), "dev": 4 * me[0] + 2 * me[1] + me[2], "core": me[2]}[self.slot_kind]
        nk = len(self.masks)

        def piece(a, dev):
            return ins[a] if self.pieces[a] is None else self.pieces[a][0](ins[a], dev)

        local = []
        if self.keep_own:
            local_sems, stages = scratch[2], scratch[3:]
            local = [_StagedCopy(piece(a, me), stages[a], outs[a].at[slot], local_sems.at[2 * a], local_sems.at[2 * a + 1])
                     for a in range(len(ins))]
        remote = []
        for a in range(len(ins)):
            for ki, mask in enumerate(self.masks):
                dev = _peer(mask)
                remote.append(_remote(piece(a, dev), outs[a].at[slot], send_sems.at[a * nk + ki],
                                      recv_sems.at[a * nk + ki], dev))
        return local, remote

    def start(self, ins, outs, scratch):
        local, remote = self._copies(ins, outs, scratch)
        for cp in remote + local:
            cp.start()

    def finish(self, ins, outs, scratch):
        local, remote = self._copies(ins, outs, scratch)
        for cp in remote + local:
            cp.wait()


SPLIT_ROWS = 16


class TwoLevelGather:
    def __init__(self, arrays):
        self.arrays = list(arrays)
        n = len(arrays)
        self.out_shape = [jax.ShapeDtypeStruct((4,) + a.shape, a.dtype) for a in arrays]
        self.scratch = ([pltpu.SemaphoreType.DMA((4 * n,)), pltpu.SemaphoreType.DMA((4 * n,)),
                         pltpu.SemaphoreType.DMA((3 * n,)), pltpu.SemaphoreType.DMA((3 * n,)),
                         pltpu.SemaphoreType.DMA((2 * n,))] + [pltpu.VMEM(a.shape, a.dtype) for a in arrays])
        self.has_mid = True

    def _copies(self, ins, outs, scratch):
        ici_send, ici_recv, fwd_send, fwd_recv, local_sems = scratch[:5]
        stages = scratch[5:]
        me = _me()
        sibling, in_x, in_y, diagonal = _peer(1), _peer(4), _peer(2), _peer(6)
        plan = []
        for a in range(len(ins)):
            half = ins[a].shape[0] // 2
            first = half // 2 if half % (2 * SPLIT_ROWS) == 0 else half
            mine = pl.ds(me[2] * half, half)
            local = _StagedCopy(ins[a], stages[a], outs[a].at[_chip_of(me)], local_sems.at[2 * a], local_sems.at[2 * a + 1])

            def ici(k, src, dst, dev):
                return _remote(src, dst, ici_send.at[4 * a + k], ici_recv.at[4 * a + k], dev)

            def d2d(k, chip):
                zone = outs[a].at[_chip_of(chip), mine]
                return _remote(zone, zone, fwd_send.at[3 * a + k], fwd_recv.at[3 * a + k], sibling)

            own_zone = outs[a].at[_chip_of(me), mine]
            from_x = outs[a].at[_chip_of(in_x), pl.ds(me[2] * half, first)]
            onward = [ici(2, from_x, from_x, in_y), None]
            if first < half:
                from_y = outs[a].at[_chip_of(in_y), pl.ds(me[2] * half + first, half - first)]
                onward[1] = ici(3, from_y, from_y, in_x)
            plan.append(dict(
                local=local,
                own=[ici(0, ins[a].at[mine], own_zone, in_x), ici(1, ins[a].at[mine], own_zone, in_y)],
                onward=onward, sibling=[d2d(0, in_x), d2d(1, in_y), d2d(2, diagonal)]))
        return plan

    def start(self, ins, outs, scratch):
        for p in self._copies(ins, outs, scratch):
            for cp in p["own"]:
                cp.start()
            p["local"].start()

    def mid(self, ins, outs, scratch):
        plan = self._copies(ins, outs, scratch)
        for p in plan:
            for k in range(2):
                p["own"][k].wait_recv()
                if p["onward"][k] is not None:
                    p["onward"][k].start()
                p["sibling"][k].start()
        for p in plan:
            for cp in p["onward"]:
                if cp is not None:
                    cp.wait_recv()
            p["sibling"][2].start()

    def finish(self, ins, outs, scratch):
        for p in self._copies(ins, outs, scratch):
            for cp in p["sibling"]:
                cp.wait_recv()
            for cp in p["own"] + p["sibling"] + [cp for cp in p["onward"] if cp is not None]:
                cp.wait_send()
            p["local"].wait()


class Both:
    def __init__(self, a, b):
        self.a, self.b = a, b
        self.arrays, self.out_shape = a.arrays + b.arrays, a.out_shape + b.out_shape
        self.scratch = a.scratch + b.scratch
        self.has_mid = False
        assert not (a.has_mid or b.has_mid)

    def _parts(self, ins, outs, sems):
        na, sa = len(self.a.arrays), len(self.a.scratch)
        return (ins[:na], outs[:na], sems[:sa]), (ins[na:], outs[na:], sems[sa:])

    def start(self, ins, outs, sems):
        pa, pb = self._parts(ins, outs, sems)
        self.a.start(*pa)
        self.b.start(*pb)

    def finish(self, ins, outs, sems):
        pa, pb = self._parts(ins, outs, sems)
        self.a.finish(*pa)
        self.b.finish(*pb)


_ANY = pl.BlockSpec(memory_space=pl.ANY)


def run_comm(name, comm):
    n = len(comm.arrays)

    def body(*refs):
        ins, outs, sems = refs[:n], refs[n:2 * n], refs[2 * n:]
        comm.start(ins, outs, sems)
        if comm.has_mid:
            comm.mid(ins, outs, sems)
        comm.finish(ins, outs, sems)

    return pl.pallas_call(body, name=name, out_shape=comm.out_shape, in_specs=[_ANY] * n, out_specs=[_ANY] * n,
                          scratch_shapes=comm.scratch,
                          compiler_params=pltpu.CompilerParams(vmem_limit_bytes=VMEM_LIMIT))(*comm.arrays)


_HBM = pl.BlockSpec(memory_space=pltpu.HBM)
_SEM = pl.BlockSpec(memory_space=pltpu.SEMAPHORE)
_SIDE_EFFECT = pltpu.SideEffectType.DATAFLOW_SIDE_EFFECTING


def _chip_copies(srcs, lands, send_sems, recv_sems, by_target):
    me = _me()
    copies = []
    for a, (src, land) in enumerate(zip(srcs, lands)):
        for ki, mask in enumerate(CHIP_MASKS):
            dev = _peer(mask)
            k = a * len(CHIP_MASKS) + ki
            piece = src.at[_chip_of(dev)] if by_target else src
            copies.append(_remote(piece, land.at[_chip_of(me)], send_sems.at[k], recv_sems.at[k], dev))
    return copies


def chip_exchange_start(name, arrays, by_target):
    n = len(arrays)
    nsem = n * len(CHIP_MASKS)
    piece_shapes = [a.shape[1:] if by_target else a.shape for a in arrays]

    def body(*refs):
        srcs, lands = refs[:n], refs[n:2 * n]
        send_sems, recv_sems = refs[2 * n:2 * n + 2]
        token = refs[4 * n + 2]
        stages, local_sems = refs[4 * n + 3:5 * n + 3], refs[5 * n + 3]
        me = _me()
        for cp in _chip_copies(srcs, lands, send_sems, recv_sems, by_target):
            cp.start()
        for a in range(n):
            own = _StagedCopy(srcs[a].at[_chip_of(me)] if by_target else srcs[a], stages[a], lands[a].at[_chip_of(me)],
                              local_sems.at[2 * a], local_sems.at[2 * a + 1])
            own.start()
            own.wait()
        token[...] = jnp.zeros_like(token)

    lands = [lax.empty((4,) + tuple(s), a.dtype) for s, a in zip(piece_shapes, arrays)]
    hbm = lambda a: pltpu.HBM(a.shape, a.dtype)
    res = pl.pallas_call(
        body, name=name,
        out_shape=(pltpu.SemaphoreType.DMA((nsem,)), pltpu.SemaphoreType.DMA((nsem,)), *[hbm(a) for a in arrays],
                   *[hbm(l) for l in lands], jax.ShapeDtypeStruct((8, LANES), F32)),
        in_specs=(_HBM,) * (2 * n), out_specs=(_SEM, _SEM) + (_HBM,) * (2 * n) + (pl.BlockSpec(memory_space=pltpu.VMEM),),
        input_output_aliases={i: i + 2 for i in range(2 * n)},
        scratch_shapes=[pltpu.VMEM(tuple(s), a.dtype) for s, a in zip(piece_shapes, arrays)]
        + [pltpu.SemaphoreType.DMA((2 * n,))],
        compiler_params=pltpu.CompilerParams(has_side_effects=_SIDE_EFFECT, vmem_limit_bytes=VMEM_LIMIT),
    )(*[pltpu.with_memory_space_constraint(a, pltpu.HBM) for a in arrays + lands])
    return res[:-1], res[-1]


def chip_exchange_wait(name, in_flight, by_target, after):
    send_sems, recv_sems, *thru = in_flight
    n = len(thru) // 2

    def body(*refs):
        srcs, lands, (send, recv) = refs[:n], refs[n:2 * n], refs[2 * n:2 * n + 2]
        for cp in _chip_copies(srcs, lands, send, recv, by_target):
            cp.wait_send()
            cp.wait_recv()

    return pl.pallas_call(
        body, name=name, out_shape=tuple(pltpu.HBM(t.shape, t.dtype) for t in thru),
        in_specs=(_HBM,) * (2 * n) + (_SEM, _SEM, pl.BlockSpec(memory_space=pl.ANY)), out_specs=(_HBM,) * (2 * n),
        input_output_aliases={i: i for i in range(2 * n)},
        compiler_params=pltpu.CompilerParams(has_side_effects=_SIDE_EFFECT),
    )(*thru, send_sems, recv_sems, after)[n:]


def _call_with_comm(body, comm, steps, args, *, name, out_shape, in_specs, out_specs, grid, scratch=()):
    ni, no, ns, nc = len(in_specs), len(out_specs), len(scratch), len(comm.arrays)

    def full_body(*refs):
        ins, cins = refs[:ni], refs[ni:ni + nc]
        outs, couts = refs[ni + nc:ni + nc + no], refs[ni + nc + no:ni + 2 * nc + no]
        scr, csems = refs[ni + 2 * nc + no:ni + 2 * nc + no + ns], refs[ni + 2 * nc + no + ns:]
        first, middle, last = steps()
        pl.when(first)(lambda: comm.start(cins, couts, csems))
        if comm.has_mid:
            pl.when(middle)(lambda: comm.mid(cins, couts, csems))
        body(*ins, *outs, *scr)
        pl.when(last)(lambda: comm.finish(cins, couts, csems))

    res = pl.pallas_call(
        full_body, name=name, out_shape=list(out_shape) + comm.out_shape, grid=grid,
        in_specs=list(in_specs) + [_ANY] * nc, out_specs=list(out_specs) + [_ANY] * nc,
        scratch_shapes=list(scratch) + comm.scratch,
        compiler_params=pltpu.CompilerParams(dimension_semantics=("arbitrary",) * len(grid),
                                             vmem_limit_bytes=VMEM_LIMIT))(*args, *comm.arrays)
    return res[:no], res[no:]


def _shard_pieces(chip):
    if chip < 3:
        return [(0, SHARD_W, 8 * chip)]
    behind_dt = DT_STORED_START + SSM_HEADS - 3 * SHARD_W
    return [(0, behind_dt, 24), (behind_dt, SHARD_W - behind_dt, behind_dt + 24 + DT_PAD)]


W_IN_COLS = 256


def pack_w_in(wt):
    def body(w_ref, o_ref, pad_ref):
        chip = _chip_index()
        pad_ref[...] = jnp.zeros_like(pad_ref)
        for cv in range(4):
            @pl.when(chip == cv)
            def _():
                for src, n, dst in _shard_pieces(cv):
                    pad_ref[dst:dst + n, :] = w_ref[src:src + n, :]
        o_ref[...] = pad_ref[...].astype(BF16)

    return _call(body, name="pack_w_in", grid=(D_MODEL // W_IN_COLS,),
                 in_specs=[pl.BlockSpec((SHARD_W, W_IN_COLS), lambda i: (0, i))],
                 out_specs=pl.BlockSpec((PACK_W, W_IN_COLS), lambda i: (0, i)),
                 out_shape=jax.ShapeDtypeStruct((PACK_W, D_MODEL), BF16),
                 scratch=[pltpu.VMEM((PACK_W, W_IN_COLS), F32)], sem=("parallel",))(wt)


def _tile_runs():
    runs, fix = [], []
    for t in range(N_ALIGNED // LANES):
        s = min(t // 19, 3)
        j = t - 19 * s
        p = _act_col(t * LANES)
        if runs and runs[-1][1] == s and runs[-1][0] + runs[-1][3] == p and runs[-1][2] + runs[-1][3] == j * LANES:
            runs[-1][3] += LANES
        else:
            runs.append([p, s, j * LANES, LANES])
        if j == 0 and s > 0:
            fix.append((p, s - 1))
    return runs, fix


def unpack_w_in(bg):
    runs, fix = _tile_runs()

    def body(b_ref, o_ref):
        for p, s, j, w in runs:
            o_ref[p:p + w, :] = b_ref[s, j:j + w, :]
        for p, s in fix:
            o_ref[p:p + LANES, :] = o_ref[p:p + LANES, :] + b_ref[s, SHARD_STRIDE:PACK_W, :]
        o_ref[N_ALIGNED:N_ACT, :] = jnp.zeros((N_ACT - N_ALIGNED, W_IN_COLS), BF16)

    return _call(body, name="unpack_w_in", grid=(D_MODEL // W_IN_COLS,),
                 in_specs=[pl.BlockSpec((4, PACK_W, W_IN_COLS), lambda i: (0, 0, i))],
                 out_specs=pl.BlockSpec((N_ACT, W_IN_COLS), lambda i: (0, i)),
                 out_shape=jax.ShapeDtypeStruct((N_ACT, D_MODEL), BF16), sem=("parallel",))(bg)


def pack_grad_w_in(dwt):
    def body(g_ref, o_ref):
        for s in range(4):
            for j in range(PACK_W // LANES):
                p = _act_col((19 * s + j) * LANES)
                o_ref[s, j * LANES:(j + 1) * LANES, :] = g_ref[p:p + LANES, :]

    return _call(body, name="pack_grad_w_in", grid=(D_MODEL // W_IN_COLS,),
                 in_specs=[pl.BlockSpec((N_ACT, W_IN_COLS), lambda i: (0, i))],
                 out_specs=pl.BlockSpec((4, PACK_W, W_IN_COLS), lambda i: (0, 0, i)),
                 out_shape=jax.ShapeDtypeStruct((4, PACK_W, D_MODEL), BF16), sem=("parallel",))(dwt)


def _adamw(w, g, m, v):
    m = ADAM_B1 * m + (1.0 - ADAM_B1) * g
    v = ADAM_B2 * v + (1.0 - ADAM_B2) * jnp.square(g)
    m_hat = m / (1.0 - ADAM_B1 ** ADAM_STEP)
    v_hat = v / (1.0 - ADAM_B2 ** ADAM_STEP)
    delta = -ADAM_LR * (m_hat / (jnp.sqrt(v_hat) + ADAM_EPS) + ADAM_WD * w)
    return delta, m, v


def adamw_w_in(g_packed, wt, mt, vt):
    cols = LANES

    def body(g_ref, w_ref, m_ref, v_ref, go_ref, d_ref, mo_ref, vo_ref):
        chip = _chip_index()
        for cv in range(4):
            @pl.when(chip == cv)
            def _():
                for dst, n, src in _shard_pieces(cv):
                    go_ref[dst:dst + n, :] = g_ref[src:src + n, :]
        d_ref[...], mo_ref[...], vo_ref[...] = _adamw(w_ref[...], go_ref[...], m_ref[...], v_ref[...])

    spec = pl.BlockSpec((SHARD_W, cols), lambda i: (0, i))
    shp = jax.ShapeDtypeStruct((SHARD_W, D_MODEL), F32)
    return _call(body, name="adamw_w_in", grid=(D_MODEL // cols,),
                 in_specs=[pl.BlockSpec((PACK_W, cols), lambda i: (0, i)), spec, spec, spec],
                 out_specs=[spec] * 4, out_shape=[shp] * 4, sem=("parallel",))(g_packed, wt, mt, vt)


def adamw_rows(name, g, w, m, v):
    r, c = g.shape
    rows = min(r, BLOCK)

    def body(g_ref, w_ref, m_ref, v_ref, d_ref, mo_ref, vo_ref):
        d_ref[...], mo_ref[...], vo_ref[...] = _adamw(w_ref[...], g_ref[...], m_ref[...], v_ref[...])

    spec = pl.BlockSpec((rows, c), lambda i: (i, 0))
    shp = jax.ShapeDtypeStruct((r, c), F32)
    return _call(body, name=name, grid=(r // rows,), in_specs=[spec] * 4, out_specs=[spec] * 3, out_shape=[shp] * 3,
                 sem=("parallel",))(g, w, m, v)


def adamw_small(gs, ws, ms, vs):
    n = len(gs)

    def body(*refs):
        g, w, m, v = refs[:n], refs[n:2 * n], refs[2 * n:3 * n], refs[3 * n:4 * n]
        outs = refs[4 * n:]
        for i in range(n):
            d, mn, vn = _adamw(w[i][...], g[i][...], m[i][...], v[i][...])
            outs[3 * i][...] = d
            outs[3 * i + 1][...] = mn
            outs[3 * i + 2][...] = vn

    specs = [_full(a.shape) for a in gs]
    res = _call(body, name="adamw_small", in_specs=specs * 4,
                out_specs=[s for s in specs for _ in range(3)],
                out_shape=[jax.ShapeDtypeStruct(a.shape, F32) for a in gs for _ in range(3)])(*gs, *ws, *ms, *vs)
    return [tuple(res[3 * i:3 * i + 3]) for i in range(n)]


def sum_slots(name, r):
    s, rr, c = r.shape
    rows = min(rr, BLOCK)

    def body(r_ref, o_ref):
        acc = r_ref[0].astype(F32)
        for k in range(1, s):
            acc = acc + r_ref[k].astype(F32)
        o_ref[...] = acc

    return _call(body, name=name, grid=(rr // rows,), in_specs=[pl.BlockSpec((s, rows, c), lambda i: (0, i, 0))],
                 out_specs=pl.BlockSpec((rows, c), lambda i: (i, 0)), out_shape=jax.ShapeDtypeStruct((rr, c), F32),
                 sem=("parallel",))(r)


def sum_pair(partial, from_sibling):
    s, _, rr, cols = partial.shape
    rows = rr // 2

    def body(p_ref, r_ref, o_ref):
        c = lax.axis_index("c")
        o_ref[...] = (p_ref[c].astype(F32) + r_ref[1 - c].astype(F32)).astype(BF16)

    return _call(body, name="sum_pair", grid=(s, rr // rows),
                 in_specs=[pl.BlockSpec((None, 2, rows, cols), lambda k, i: (k, 0, i, 0)),
                           pl.BlockSpec((2, None, rows, cols), lambda k, i: (0, k, i, 0))],
                 out_specs=pl.BlockSpec((None, rows, cols), lambda k, i: (k, i, 0)),
                 out_shape=jax.ShapeDtypeStruct((s, rr, cols), BF16), sem=("parallel", "parallel"))(partial, from_sibling)


def _col_tile(n, k):
    if n % 896 == 0 and k <= 1024:
        return 896
    return min(n, 512)


def project(name, x, wt):
    m, k = x.shape
    n = wt.shape[0]
    tn = D_MODEL

    def body(x_ref, w_ref, o_ref):
        o_ref[...] = lax.dot_general(x_ref[...], w_ref[...], _NT, preferred_element_type=F32)

    return _call(body, name=name, grid=(n // tn,), in_specs=[_full((m, k)), pl.BlockSpec((tn, k), lambda j: (j, 0))],
                 out_specs=pl.BlockSpec((m, tn), lambda j: (0, j)), out_shape=jax.ShapeDtypeStruct((m, n), F32),
                 sem=("parallel",))(x, wt)


def _piece_tiles(pieces):
    spans, start = [], 0
    for p in pieces:
        spans.append((start, p.shape[1] // D_MODEL))
        start += p.shape[1] // D_MODEL
    return spans, start


def _piece_spec(tm, span, rows_of, tile_of):
    first, count = span

    def index(i, j):
        t = tile_of(i, j) - first
        mine = (t >= 0) & (t < count)
        return jnp.where(mine, rows_of(i, j), 0), jnp.clip(t, 0, count - 1)

    return pl.BlockSpec((tm, D_MODEL), index)


def project_back(name, pieces, wt, after):
    m = pieces[0].shape[0]
    k = wt.shape[1]
    tm = m // 2
    spans, steps = _piece_tiles(pieces)

    def body(*refs):
        w_ref, o_ref = refs[len(pieces)], refs[len(pieces) + 2]
        j = pl.program_id(1)

        @pl.when(j == 0)
        def _():
            o_ref[...] = jnp.zeros_like(o_ref)

        for dy_ref, (first, count) in zip(refs, spans):
            @pl.when((j >= first) & (j < first + count))
            def _():
                o_ref[...] += jnp.dot(dy_ref[...], w_ref[...], preferred_element_type=F32)

    return _call(body, name=name, grid=(m // tm, steps),
                 in_specs=[_piece_spec(tm, s, lambda i, j: i, lambda i, j: j) for s in spans]
                 + [pl.BlockSpec((D_MODEL, k), lambda i, j: (j, 0)), _full(after.shape)],
                 out_specs=pl.BlockSpec((tm, k), lambda i, j: (i, 0)), out_shape=jax.ShapeDtypeStruct((m, k), F32),
                 sem=("parallel", "arbitrary"))(*pieces, wt, after)


def weight_grad_t(name, pieces, x):
    m = pieces[0].shape[0]
    k = x.shape[1]
    tm = m // 2
    spans, steps = _piece_tiles(pieces)

    def body(*refs):
        x_ref, o_ref, acc_ref = refs[len(pieces):]
        i, half = pl.program_id(0), pl.program_id(1)
        for dy_ref, (first, count) in zip(refs, spans):
            @pl.when((i >= first) & (i < first + count))
            def _():
                part = lax.dot_general(dy_ref[...], x_ref[...], (((0,), (0,)), ((), ())), preferred_element_type=F32)

                @pl.when(half == 0)
                def _():
                    acc_ref[...] = part

                @pl.when(half == 1)
                def _():
                    o_ref[...] = (acc_ref[...] + part).astype(BF16)

    return _call(body, name=name, grid=(steps, 2),
                 in_specs=[_piece_spec(tm, s, lambda i, j: j, lambda i, j: i) for s in spans]
                 + [pl.BlockSpec((tm, k), lambda i, j: (j, 0))],
                 out_specs=pl.BlockSpec((D_MODEL, k), lambda i, j: (i, 0)),
                 out_shape=jax.ShapeDtypeStruct((steps * D_MODEL, k), BF16),
                 scratch=[pltpu.VMEM((D_MODEL, k), F32)], sem=("parallel", "arbitrary"))(*pieces, x)


def mm_tn(name, x, dy):
    m, k = x.shape
    n = dy.shape[1]
    tm = m // 2
    tn = _col_tile(n, k)

    def body(x_ref, dy_ref, o_ref, acc_ref):
        part = lax.dot_general(x_ref[...].astype(BF16), dy_ref[...].astype(BF16), (((0,), (0,)), ((), ())),
                               preferred_element_type=F32)

        @pl.when(pl.program_id(1) == 0)
        def _():
            acc_ref[...] = part

        @pl.when(pl.program_id(1) == 1)
        def _():
            o_ref[...] = (acc_ref[...] + part).astype(BF16)

    return _call(body, name=name, grid=(n // tn, 2),
                 in_specs=[pl.BlockSpec((tm, k), lambda i, j: (j, 0)), pl.BlockSpec((tm, tn), lambda i, j: (j, i))],
                 out_specs=pl.BlockSpec((k, tn), lambda i, j: (0, i)), out_shape=jax.ShapeDtypeStruct((k, n), BF16),
                 scratch=[pltpu.VMEM((k, tn), F32)], sem=("parallel", "arbitrary"))(x, dy)


def _row_spec(width, col_block=0):
    return pl.BlockSpec((BLOCK, width), lambda i: (i, col_block))


def _x_spec():
    return pl.BlockSpec((None, BLOCK, D_MODEL), lambda i: (0, jnp.maximum(i - 1, 0), 0))


def prep(x, meta, g_pre):
    nb = x.shape[1] // BLOCK + 1

    def body(x_ref, meta_ref, g_ref, h_ref, u_ref):
        i = pl.program_id(0)

        @pl.when(i == 0)
        def _():
            h_ref[0:PAD_ROWS, :] = jnp.zeros((PAD_ROWS, D_MODEL), F32)
            h_ref[PAD_ROWS:BLOCK, :] = meta_ref[...]

        @pl.when(i > 0)
        def _():
            h_ref[...] = x_ref[...]

        u_ref[...] = _rms(h_ref[...], g_ref[...]).astype(BF16)

    return _call(body, name="prep", grid=(nb,), in_specs=[_x_spec(), _full((N_META, D_MODEL)), _full((1, D_MODEL))],
                 out_specs=[_row_spec(D_MODEL), _row_spec(D_MODEL)],
                 out_shape=[jax.ShapeDtypeStruct((nb * BLOCK, D_MODEL), F32),
                            jax.ShapeDtypeStruct((nb * BLOCK, D_MODEL), BF16)], sem=("parallel",))(x, meta, g_pre)


def prep_bwd(h, du, dres, g_pre):
    nb = h.shape[0] // BLOCK

    def body(h_ref, du_ref, dres_ref, g_ref, gx_ref, gm_ref, gg_ref):
        i = pl.program_id(0)
        _, vjp = jax.vjp(_rms, h_ref[...], g_ref[...])
        dh, dg = vjp(du_ref[...])

        @pl.when(i == 0)
        def _():
            gm_ref[...] = dh[PAD_ROWS:BLOCK, :]
            gg_ref[...] = dg

        @pl.when(i > 0)
        def _():
            gg_ref[...] += dg

        gx_ref[...] = dh + dres_ref[...]

    return _call(body, name="prep_bwd", grid=(nb,),
                 in_specs=[_row_spec(D_MODEL), _row_spec(D_MODEL), _row_spec(D_MODEL), _full((1, D_MODEL))],
                 out_specs=[_x_spec(), _full((N_META, D_MODEL)), _full((1, D_MODEL))],
                 out_shape=[jax.ShapeDtypeStruct((1, (nb - 1) * BLOCK, D_MODEL), F32),
                            jax.ShapeDtypeStruct((N_META, D_MODEL), F32), jax.ShapeDtypeStruct((1, D_MODEL), F32)],
                 sem=("arbitrary",))(h, du, dres, g_pre)


GROUP_W = SSM_INNER // SSM_GROUPS


def _gated_norm(y, z, g):
    t = y * _silu(z)
    return t * lax.rsqrt(jnp.mean(t * t, axis=-1, keepdims=True) + NORM_EPS) * g


def _gated_norm_groups(y, z, g):
    groups = [slice(k * GROUP_W, (k + 1) * GROUP_W) for k in range(SSM_GROUPS)]
    return jnp.concatenate([_gated_norm(y[:, s], z[:, s], g[:, s]) for s in groups], axis=1)


def _merge(ga, gs, ya, ys):
    return _sigmoid(ga) * ya + _sigmoid(gs) * ys


GATE_ATT_BLOCK = SEG["gate_att"][2] // D_MODEL
GATE_SSM_BLOCK = SEG["gate_ssm"][2] // D_MODEL


def _row_loss(out, g_post, x, target):
    diff = x + _rms(out, g_post) - target
    return 0.5 * jnp.sum(diff * diff) / D_MODEL


def tail(y_ssd, proj, a_att, x, target, woa, wos, wo, g_norm, g_post):
    nb = y_ssd.shape[0] // BLOCK
    rows = nb * BLOCK

    def body(y_ref, z_ref, ga_ref, gs_ref, a_ref, x_ref, t_ref, woa_ref, wos_ref, wo_ref, gn_ref, gp_ref,
             yn_ref, mg_ref, dout_ref, dya_ref, dys_ref, da_ref, dy_ref, dz_ref, dga_ref, dgs_ref, dres_ref,
             loss_ref, dgp_ref, dgn_ref):
        i = pl.program_id(0)
        yn, norm_vjp = jax.vjp(_gated_norm_groups, y_ref[...], z_ref[...], gn_ref[...])
        yn16 = yn.astype(BF16)
        y_ssm = jnp.dot(yn16, wos_ref[...], preferred_element_type=F32)
        y_att = jnp.dot(a_ref[...], woa_ref[...], preferred_element_type=F32)
        merged, merge_vjp = jax.vjp(_merge, ga_ref[...], gs_ref[...], y_att, y_ssm)
        merged16 = merged.astype(BF16)
        out = jnp.dot(merged16, wo_ref[...], preferred_element_type=F32)
        loss, loss_vjp = jax.vjp(_row_loss, out, gp_ref[...], x_ref[...], t_ref[...])
        counted = jnp.where(i > 0, 1.0, 0.0)
        dout, dgp, dres, _ = loss_vjp(counted)
        dout16 = dout.astype(BF16)
        dmerged = lax.dot_general(dout16, wo_ref[...], _NT, preferred_element_type=F32)
        dga, dgs, dya, dys = merge_vjp(dmerged)
        dya16, dys16 = dya.astype(BF16), dys.astype(BF16)
        dyn = lax.dot_general(dys16, wos_ref[...], _NT, preferred_element_type=F32)
        dy, dz, dgn = norm_vjp(dyn)

        yn_ref[...] = yn16
        mg_ref[...] = merged16
        dout_ref[...] = dout16
        dya_ref[...] = dya16
        dys_ref[...] = dys16
        da_ref[...] = lax.dot_general(dya16, woa_ref[...], _NT, preferred_element_type=F32)
        dy_ref[...] = dy
        dz_ref[...] = dz.astype(BF16)
        dga_ref[...] = dga.astype(BF16)
        dgs_ref[...] = dgs.astype(BF16)
        dres_ref[...] = dres

        @pl.when(i == 0)
        def _():
            loss_ref[...] = jnp.zeros_like(loss_ref)
            dgp_ref[...] = jnp.zeros_like(dgp_ref)
            dgn_ref[...] = jnp.zeros_like(dgn_ref)

        loss_ref[...] += loss * counted
        dgp_ref[...] += dgp
        dgn_ref[...] += dgn

    wide, narrow = _row_spec(SSM_INNER), _row_spec(D_MODEL)
    resident = pl.BlockSpec(memory_space=pltpu.VMEM)
    bf = lambda w: jax.ShapeDtypeStruct((rows, w), BF16)
    f32 = lambda w: jax.ShapeDtypeStruct((rows, w), F32)
    return _call(body, name="tail", grid=(nb,),
                 in_specs=[wide, wide, _row_spec(D_MODEL, GATE_ATT_BLOCK), _row_spec(D_MODEL, GATE_SSM_BLOCK), narrow,
                           _x_spec(), _x_spec(), resident, resident, resident, _full((1, SSM_INNER)),
                           _full((1, D_MODEL))],
                 out_specs=[wide, narrow, narrow, narrow, narrow, narrow, wide, wide, narrow, narrow, narrow,
                            _full((8, LANES)), _full((1, D_MODEL)), _full((1, SSM_INNER))],
                 out_shape=[bf(SSM_INNER), bf(D_MODEL), bf(D_MODEL), bf(D_MODEL), bf(D_MODEL), f32(D_MODEL),
                            f32(SSM_INNER), bf(SSM_INNER), bf(D_MODEL), bf(D_MODEL), f32(D_MODEL),
                            jax.ShapeDtypeStruct((8, LANES), F32), jax.ShapeDtypeStruct((1, D_MODEL), F32),
                            jax.ShapeDtypeStruct((1, SSM_INNER), F32)],
                 sem=("arbitrary",))(y_ssd, proj, proj, proj, a_att, x, target, woa, wos, wo, g_norm, g_post)


_NT = (((1,), (1,)), ((), ()))
ALIBI_SLOPES = tuple(2.0 ** (-8.0 * (h + 1) / ATT_Q_HEADS) for h in range(ATT_Q_HEADS))
KV_WIDTH = ATT_KV_HEADS * HEAD_DIM
Q_BLOCK = SEG["q"][2] // D_MODEL
Z_ATT_BLOCK = SEG["z_att"][2] // D_MODEL
K_BLOCK = SEG["k"][2] // KV_WIDTH
V_BLOCK = SEG["v"][2] // KV_WIDTH
META_ROW_BLOCK = PAD_ROWS // N_META


@jax.custom_vjp
def _swap_halves(x):
    return pltpu.roll(x, HEAD_DIM, 1)


_swap_halves.defvjp(lambda x: (pltpu.roll(x, HEAD_DIM, 1), None), lambda _, g: (pltpu.roll(g, HEAD_DIM, 1),))


def _both_halves(t, half):
    first = lax.broadcasted_iota(jnp.int32, t.shape, 1) < HEAD_DIM
    sw = _swap_halves(t)
    return jnp.where(first, t, sw) if half == 0 else jnp.where(first, sw, t)


def _attn_rows(q, z, kp, kc, vp, vc, km, vm, sinks, n):
    rows = ATT_GROUP * BLOCK
    i = lax.broadcasted_iota(jnp.int32, (rows, BLOCK), 0) & (BLOCK - 1)
    j = lax.broadcasted_iota(jnp.int32, (rows, BLOCK), 1)
    rel_c = (i - j).astype(F32)
    rel_p = rel_c + float(BLOCK)
    nv = jnp.zeros((rows, BLOCK), jnp.int32) + n
    ok_c = (i >= j) & (nv >= 1)
    ok_p = (j > i) & (nv >= 2)
    im = lax.broadcasted_iota(jnp.int32, (rows, N_META), 0) & (BLOCK - 1)
    jm = lax.broadcasted_iota(jnp.int32, (rows, N_META), 1)
    ok_m = ((jnp.zeros((rows, N_META), jnp.int32) + n) >= 1) | (im >= PAD_ROWS + jm)
    first = lax.broadcasted_iota(jnp.int32, (BLOCK, LANES), 1) < HEAD_DIM
    neg = -jnp.inf
    outs = []
    for kv in range(ATT_KV_HEADS):
        tile, half = divmod(kv, 2)
        lanes = slice(tile * LANES, (tile + 1) * LANES)
        kc2, kp2, km2 = (_both_halves(t[:, lanes], half).astype(BF16) for t in (kc, kp, km))
        vc2, vp2, vm2 = (_both_halves(t[:, lanes], half).astype(BF16) for t in (vc, vp, vm))
        qs, slope, sk = [], [], []
        for pair in range(ATT_GROUP // 2):
            c0 = (kv * ATT_GROUP + 2 * pair) * HEAD_DIM
            qp = q[:, c0:c0 + LANES] * HEAD_DIM ** -0.5
            qs += [jnp.where(first, qp, 0.0), jnp.where(first, 0.0, qp)]
        for g in range(ATT_GROUP):
            slope.append(jnp.full((BLOCK, 1), ALIBI_SLOPES[kv * ATT_GROUP + g], F32))
            sk.append(jnp.broadcast_to(sinks[kv * ATT_GROUP + g], (BLOCK, 1)))
        qs = jnp.concatenate(qs, axis=0).astype(BF16)
        slope = jnp.concatenate(slope, axis=0)
        sk = jnp.concatenate(sk, axis=0)
        sc = jnp.where(ok_c, lax.dot_general(qs, kc2, _NT, preferred_element_type=F32) - slope * rel_c, neg)
        sp = jnp.where(ok_p, lax.dot_general(qs, kp2, _NT, preferred_element_type=F32) - slope * rel_p, neg)
        sm = jnp.where(ok_m, lax.dot_general(qs, km2, _NT, preferred_element_type=F32), neg)
        mx = jnp.maximum(jnp.maximum(jnp.max(sc, axis=1, keepdims=True), jnp.max(sp, axis=1, keepdims=True)),
                         jnp.maximum(jnp.max(sm, axis=1, keepdims=True), sk))
        mx = lax.stop_gradient(mx)
        ec, ep, em, es = jnp.exp(sc - mx), jnp.exp(sp - mx), jnp.exp(sm - mx), jnp.exp(sk - mx)
        den = (es + jnp.sum(ec, axis=1, keepdims=True) + jnp.sum(ep, axis=1, keepdims=True)
               + jnp.sum(em, axis=1, keepdims=True))
        inv = 1.0 / den
        o = (jnp.dot((ec * inv).astype(BF16), vc2, preferred_element_type=F32)
             + jnp.dot((ep * inv).astype(BF16), vp2, preferred_element_type=F32)
             + jnp.dot((em * inv).astype(BF16), vm2, preferred_element_type=F32))
        for pair in range(ATT_GROUP // 2):
            r0 = 2 * pair * BLOCK
            outs.append(jnp.where(first, o[r0:r0 + BLOCK], o[r0 + BLOCK:r0 + 2 * BLOCK]))
    return jnp.concatenate(outs, axis=1) * _silu(z)


def _attn_specs(nb, steps_clamped):
    def blk(t):
        return jnp.minimum(t, nb - 1) if steps_clamped else t

    wide = lambda col: pl.BlockSpec((BLOCK, D_MODEL), lambda t: (blk(t), col))
    cur = lambda col: pl.BlockSpec((BLOCK, KV_WIDTH), lambda t: (blk(t), col))
    prev = lambda col: pl.BlockSpec((BLOCK, KV_WIDTH), lambda t: (jnp.maximum(blk(t) - 1, 0), col))
    meta = lambda col: pl.BlockSpec((N_META, KV_WIDTH), lambda t: (META_ROW_BLOCK, col))
    sinks = pl.BlockSpec((ATT_Q_HEADS, 1, 1), lambda t: (0, 0, 0))
    return [wide(Q_BLOCK), wide(Z_ATT_BLOCK), prev(K_BLOCK), cur(K_BLOCK), prev(V_BLOCK), cur(V_BLOCK),
            meta(K_BLOCK), meta(V_BLOCK), sinks]


def attn_fwd(proj, sinks):
    nb = proj.shape[0] // BLOCK

    def body(q_ref, z_ref, kp_ref, kc_ref, vp_ref, vc_ref, km_ref, vm_ref, sk_ref, o_ref):
        o_ref[...] = _attn_rows(q_ref[...], z_ref[...], kp_ref[...], kc_ref[...], vp_ref[...], vc_ref[...],
                                km_ref[...], vm_ref[...], tuple(sk_ref[h] for h in range(ATT_Q_HEADS)),
                                pl.program_id(0)).astype(BF16)

    return _call(body, name="attn_fwd", grid=(nb,), in_specs=_attn_specs(nb, False), out_specs=_row_spec(D_MODEL),
                 out_shape=jax.ShapeDtypeStruct((nb * BLOCK, D_MODEL), BF16), sem=("parallel",))(*([proj] * 8), sinks)


def attn_bwd(da, proj, sinks):
    nb = proj.shape[0] // BLOCK
    last = nb - 1
    wide = pl.BlockSpec((BLOCK, D_MODEL), lambda t: (jnp.minimum(t, last), 0))
    done = pl.BlockSpec((BLOCK, KV_WIDTH), lambda t: (jnp.maximum(t - 1, 0), 0))
    meta = _full((N_META, KV_WIDTH))
    par = _full((ATT_Q_HEADS, 1, 1))

    def body(da_ref, q_ref, z_ref, kp_ref, kc_ref, vp_ref, vc_ref, km_ref, vm_ref, sk_ref,
             dq_ref, dz_ref, dk_ref, dv_ref, dkm_ref, dvm_ref, dsk_ref, ck_ref, cv_ref):
        t = pl.program_id(0)

        @pl.when(t == 0)
        def _():
            ck_ref[...] = jnp.zeros_like(ck_ref)
            cv_ref[...] = jnp.zeros_like(cv_ref)
            dkm_ref[...] = jnp.zeros_like(dkm_ref)
            dvm_ref[...] = jnp.zeros_like(dvm_ref)
            dsk_ref[...] = jnp.zeros_like(dsk_ref)

        @pl.when(t < nb)
        def _():
            def f(q, z, kp, kc, vp, vc, km, vm, sk):
                return _attn_rows(q, z, kp, kc, vp, vc, km, vm, sk, t)

            _, vjp = jax.vjp(f, q_ref[...], z_ref[...], kp_ref[...], kc_ref[...], vp_ref[...], vc_ref[...],
                             km_ref[...], vm_ref[...], tuple(sk_ref[h] for h in range(ATT_Q_HEADS)))
            dq, dz, dkp, dkc, dvp, dvc, dkm, dvm, dsk = vjp(da_ref[...])
            dq_ref[...] = dq.astype(BF16)
            dz_ref[...] = dz.astype(BF16)
            for h in range(ATT_Q_HEADS):
                dsk_ref[h] += dsk[h]
            dk_ref[...] = ck_ref[...] + dkp
            dv_ref[...] = cv_ref[...] + dvp
            ck_ref[...] = dkc
            cv_ref[...] = dvc
            dkm_ref[...] += dkm
            dvm_ref[...] += dvm

        @pl.when(t == nb)
        def _():
            dk_ref[...] = ck_ref[...]
            dv_ref[...] = cv_ref[...]

    rows = nb * BLOCK
    return _call(body, name="attn_bwd", grid=(nb + 1,), in_specs=[wide] + _attn_specs(nb, True),
                 out_specs=[wide, wide, done, done, meta, meta, par],
                 out_shape=[jax.ShapeDtypeStruct((rows, D_MODEL), BF16), jax.ShapeDtypeStruct((rows, D_MODEL), BF16),
                            jax.ShapeDtypeStruct((rows, KV_WIDTH), F32), jax.ShapeDtypeStruct((rows, KV_WIDTH), F32),
                            jax.ShapeDtypeStruct((N_META, KV_WIDTH), F32), jax.ShapeDtypeStruct((N_META, KV_WIDTH), F32),
                            jax.ShapeDtypeStruct(sinks.shape, F32)],
                 scratch=[pltpu.VMEM((BLOCK, KV_WIDTH), F32), pltpu.VMEM((BLOCK, KV_WIDTH), F32)],
                 sem=("arbitrary",))(da, *([proj] * 8), sinks)


XBC_BLOCK0 = SEG["xbc"][2] // D_MODEL
CONV_COL_BLOCKS = CONV_DIM // D_MODEL
DT_TILE = SEG["dt"][2] // LANES


HALO = 8


def _conv_rows(length):
    return 544 if length % 544 == 0 else BLOCK


def _shift_rows(cur, before, j):
    if j == 0:
        return cur
    n = cur.shape[0]
    row = lax.broadcasted_iota(jnp.int32, cur.shape, 0)
    head = pltpu.roll(before, j, 0)
    if n > HALO:
        head = jnp.concatenate([head, jnp.zeros((n - HALO, cur.shape[1]), cur.dtype)], axis=0)
    return jnp.where(row >= j, pltpu.roll(cur, j, 0), head)


def _conv_pre(cur, before, w_ref, b_ref):
    pre = b_ref[...] + w_ref[CONV_WIDTH - 1:CONV_WIDTH, :] * cur
    for k in range(CONV_WIDTH - 1):
        pre = pre + w_ref[k:k + 1, :] * _shift_rows(cur, before, CONV_WIDTH - 1 - k)
    return pre


def _conv_specs(steps, rows, col0=0):
    first = XBC_BLOCK0 + col0
    halos = rows // HALO
    cur = pl.BlockSpec((rows, D_MODEL), lambda j, i: (i, first + j))
    before = pl.BlockSpec((HALO, D_MODEL), lambda j, i: (jnp.maximum(i * halos - 1, 0), first + j))
    after = pl.BlockSpec((HALO, D_MODEL), lambda j, i: (jnp.minimum(i + 1, steps - 1) * halos, first + j))
    return cur, before, after


def _valid_rows(i, rows):
    row = lax.broadcasted_iota(jnp.int32, (rows, D_MODEL), 0)
    return jnp.maximum((row >= PAD_ROWS).astype(F32), jnp.where(i > 0, 1.0, 0.0))


def conv_fwd(proj, conv_w, conv_b):
    rows = _conv_rows(proj.shape[0])
    steps = proj.shape[0] // rows
    cur, before, _ = _conv_specs(steps, rows)

    def body(c_ref, p_ref, w_ref, b_ref, o_ref):
        i = pl.program_id(1)
        pre = _conv_pre(c_ref[...], p_ref[...] * jnp.where(i > 0, 1.0, 0.0), w_ref, b_ref)
        o_ref[...] = _silu(pre) * _valid_rows(i, rows)

    return _call(body, name="conv_fwd", grid=(CONV_COL_BLOCKS, steps),
                 in_specs=[cur, before, pl.BlockSpec((CONV_WIDTH, D_MODEL), lambda j, i: (0, j)),
                           pl.BlockSpec((1, D_MODEL), lambda j, i: (0, j))],
                 out_specs=pl.BlockSpec((rows, D_MODEL), lambda j, i: (i, j)),
                 out_shape=jax.ShapeDtypeStruct((proj.shape[0], CONV_DIM), F32),
                 sem=("parallel", "parallel"))(proj, proj, conv_w, conv_b)


def conv_bwd(name, dparts, col0, proj, conv_w, conv_b):
    rows = _conv_rows(proj.shape[0])
    steps = proj.shape[0] // rows
    last = steps - 1
    ncol = sum(d.shape[1] for d in dparts) // D_MODEL
    np_ = len(dparts)
    cur, before, after = _conv_specs(steps, rows, col0)
    dcur = [pl.BlockSpec((rows, d.shape[1] // ncol), lambda j, i: (i, j)) for d in dparts]
    dafter = [pl.BlockSpec((HALO, d.shape[1] // ncol), lambda j, i: (jnp.minimum(i + 1, last) * (rows // HALO), j))
              for d in dparts]
    out_cur = pl.BlockSpec((rows, D_MODEL), lambda j, i: (i, j))
    wspec = pl.BlockSpec((CONV_WIDTH, D_MODEL), lambda j, i: (0, col0 + j))
    bspec = pl.BlockSpec((1, D_MODEL), lambda j, i: (0, col0 + j))
    wout = pl.BlockSpec((CONV_WIDTH, D_MODEL), lambda j, i: (0, j))
    bout = pl.BlockSpec((1, D_MODEL), lambda j, i: (0, j))

    def body(*refs):
        dc_refs, da_refs = refs[:np_], refs[np_:2 * np_]
        c_ref, p_ref, a_ref, w_ref, b_ref, du_ref, dw_ref, db_ref = refs[2 * np_:]
        i = pl.program_id(1)
        row = lax.broadcasted_iota(jnp.int32, (rows, D_MODEL), 0)
        curv = c_ref[...]
        beforev = p_ref[...] * jnp.where(i > 0, 1.0, 0.0)
        side_by_side = lambda rs: rs[0][...] if np_ == 1 else jnp.concatenate([r[...] for r in rs], axis=1)

        def dpre_of(pre, d):
            s = _sigmoid(pre)
            return d * (s * (1.0 + pre * (1.0 - s)))

        dp_c = dpre_of(_conv_pre(curv, beforev, w_ref, b_ref), side_by_side(dc_refs) * _valid_rows(i, rows))
        dp_a = dpre_of(_conv_pre(a_ref[...], curv[rows - HALO:], w_ref, b_ref),
                       side_by_side(da_refs) * jnp.where(i < last, 1.0, 0.0))
        du = w_ref[CONV_WIDTH - 1:CONV_WIDTH, :] * dp_c
        for j in range(1, CONV_WIDTH):
            tail = jnp.concatenate([jnp.zeros((rows - HALO, D_MODEL), F32), pltpu.roll(dp_a, HALO - j, 0)], axis=0)
            up = jnp.where(row < rows - j, pltpu.roll(dp_c, rows - j, 0), tail)
            du = du + w_ref[CONV_WIDTH - 1 - j:CONV_WIDTH - j, :] * up
        du_ref[...] = du.astype(BF16)

        @pl.when(i == 0)
        def _():
            dw_ref[...] = jnp.zeros_like(dw_ref)
            db_ref[...] = jnp.zeros_like(db_ref)

        for k in range(CONV_WIDTH):
            dw_ref[k:k + 1, :] += jnp.sum(dp_c * _shift_rows(curv, beforev, CONV_WIDTH - 1 - k), axis=0, keepdims=True)
        db_ref[...] += jnp.sum(dp_c, axis=0, keepdims=True)

    width = ncol * D_MODEL
    return _call(body, name=name, grid=(ncol, steps),
                 in_specs=dcur + dafter + [cur, before, after, wspec, bspec], out_specs=[out_cur, wout, bout],
                 out_shape=[jax.ShapeDtypeStruct((proj.shape[0], width), BF16),
                            jax.ShapeDtypeStruct((CONV_WIDTH, width), F32), jax.ShapeDtypeStruct((1, width), F32)],
                 sem=("parallel", "arbitrary"))(*dparts, *dparts, proj, proj, proj, conv_w, conv_b)


def _head_expand():
    e = np.zeros((LANES, SSM_INNER), np.float32)
    for h in range(SSM_HEADS):
        e[h, h * HEAD_DIM:(h + 1) * HEAD_DIM] = 1.0
    return jnp.asarray(e, dtype=BF16)


def _softplus(x):
    return jnp.maximum(x, 0.0) + jnp.log(1.0 + jnp.exp(-jnp.abs(x)))


def _bf16_parts(x):
    hi = x.astype(BF16)
    rest = x - hi.astype(F32)
    mid = rest.astype(BF16)
    return hi, mid, (rest - mid.astype(F32)).astype(BF16)


@jax.custom_vjp
def _times_01(x, m):
    return sum(jnp.dot(p, m, preferred_element_type=F32) for p in _bf16_parts(x))


def _times_01_bwd(m, g):
    return sum(lax.dot_general(p, m, _NT, preferred_element_type=F32) for p in _bf16_parts(g)), jnp.zeros_like(m)


_times_01.defvjp(lambda x, m: (_times_01(x, m), m), _times_01_bwd)


def _causal_ones():
    l = lax.broadcasted_iota(jnp.int32, (BLOCK, BLOCK), 0)
    s = lax.broadcasted_iota(jnp.int32, (BLOCK, BLOCK), 1)
    return (l >= s).astype(BF16)


@jax.custom_vjp
def _cumsum_rows(a):
    return sum(jnp.dot(_causal_ones(), p, preferred_element_type=F32) for p in _bf16_parts(a))


def _cumsum_rows_bwd(_, g):
    tn = (((0,), (0,)), ((), ()))
    return (sum(lax.dot_general(_causal_ones(), p, tn, preferred_element_type=F32) for p in _bf16_parts(g)),)


_cumsum_rows.defvjp(lambda a: (_cumsum_rows(a), None), _cumsum_rows_bwd)


def _ssd_group(xs, dt_tile, expand, bias, alog, dsk, bg, cg, state):
    l = lax.broadcasted_iota(jnp.int32, (BLOCK, BLOCK), 0)
    s = lax.broadcasted_iota(jnp.int32, (BLOCK, BLOCK), 1)
    causal = l >= s
    first_head = s < HEAD_DIM
    dt = _softplus(dt_tile + bias)
    a = dt * (-jnp.exp(alog))
    one_row = lambda v: jnp.broadcast_to(v, (HALO, LANES))
    per_lane = _times_01(jnp.concatenate([dt, _cumsum_rows(a), one_row(jnp.sum(a, axis=0, keepdims=True)),
                                          one_row(dsk)], axis=0), expand)
    dtx, cs = per_lane[0:BLOCK], per_lane[BLOCK:2 * BLOCK]
    tot, dsk = per_lane[2 * BLOCK:2 * BLOCK + 1], per_lane[2 * BLOCK + HALO:2 * BLOCK + HALO + 1]
    bb, cb16 = bg.astype(BF16), cg.astype(BF16)
    cb = lax.dot_general(cb16, bb, _NT, preferred_element_type=F32)
    xr = xs * dtx
    y_diag = []
    for p in range(GROUP_W // LANES):
        lanes = slice(p * LANES, (p + 1) * LANES)
        c_pair = cs[:, lanes]
        c_swap = _swap_halves(c_pair)
        m = []
        for c_head in (jnp.where(first_head, c_pair, c_swap), jnp.where(first_head, c_swap, c_pair)):
            m.append(cb * jnp.exp(jnp.where(causal, c_head - c_head.T, -jnp.inf)))
        x_pair = xr[:, lanes]
        x_diag = jnp.concatenate([jnp.where(first_head, x_pair, 0.0), jnp.where(first_head, 0.0, x_pair)], axis=0)
        y_diag.append(jnp.dot(jnp.concatenate(m, axis=1).astype(BF16), x_diag.astype(BF16),
                              preferred_element_type=F32))
    st = lax.dot_general(bb, (xr * jnp.exp(tot - cs)).astype(BF16), (((0,), (0,)), ((), ())),
                         preferred_element_type=F32)
    new_state = state * jnp.exp(tot) + st
    y_off = jnp.dot(cb16, state.astype(BF16), preferred_element_type=F32) * jnp.exp(cs)
    return jnp.concatenate(y_diag, axis=1) + y_off + dsk * xs, new_state


BC_WIDTH = SSM_GROUPS * SSM_STATE


def _ssd_specs(chunk):
    xs = pl.BlockSpec((BLOCK, SSM_INNER), lambda c: (chunk(c), 0))
    dt = pl.BlockSpec((BLOCK, LANES), lambda c: (chunk(c), DT_TILE))
    expand = _full((LANES, SSM_INNER))
    b = pl.BlockSpec((BLOCK, BC_WIDTH), lambda c: (chunk(c), SSM_INNER // BC_WIDTH))
    cc = pl.BlockSpec((BLOCK, BC_WIDTH), lambda c: (chunk(c), SSM_INNER // BC_WIDTH + 1))
    par = _full((1, LANES))
    state = pl.BlockSpec((None, SSM_STATE, SSM_INNER), lambda c: (chunk(c), 0, 0))
    return xs, dt, expand, b, cc, par, state


def _group_lanes(g):
    return slice(g * GROUP_W, (g + 1) * GROUP_W), slice(g * SSM_STATE, (g + 1) * SSM_STATE)


def ssd_fwd(xbc, proj, expand, bias, alog, dsk):
    nb = xbc.shape[0] // BLOCK
    xs, dt, ex, b, cc, par, state = _ssd_specs(lambda c: c)

    def body(x_ref, dt_ref, e_ref, bi_ref, al_ref, dk_ref, b_ref, c_ref, y_ref, sp_ref, st_ref):
        @pl.when(pl.program_id(0) == 0)
        def _():
            st_ref[...] = jnp.zeros_like(st_ref)

        for g in range(SSM_GROUPS):
            wide, tile = _group_lanes(g)
            entering = st_ref[:, wide]
            sp_ref[:, wide] = entering
            y_ref[:, wide], st_ref[:, wide] = _ssd_group(
                x_ref[:, wide], dt_ref[...], e_ref[:, wide], bi_ref[...], al_ref[...], dk_ref[...], b_ref[:, tile],
                c_ref[:, tile], entering)

    return _call(body, name="ssd_fwd", grid=(nb,), in_specs=[xs, dt, ex, par, par, par, b, cc],
                 out_specs=[xs, state],
                 out_shape=[jax.ShapeDtypeStruct((nb * BLOCK, SSM_INNER), F32),
                            jax.ShapeDtypeStruct((nb, SSM_STATE, SSM_INNER), F32)],
                 scratch=[pltpu.VMEM((SSM_STATE, SSM_INNER), F32)],
                 sem=("arbitrary",))(xbc, proj, expand, bias, alog, dsk, xbc, xbc)


def ssd_bwd(dy, xbc, proj, expand, bias, alog, dsk, states, comm):
    nb = xbc.shape[0] // BLOCK
    last = nb - 1
    xs, dt, ex, b, cc, par, state = _ssd_specs(lambda c: last - c)
    tile = pl.BlockSpec((BLOCK, LANES), lambda c: (last - c, 0))
    nspec = pl.BlockSpec((BLOCK, BC_WIDTH), lambda c: (last - c, 0))

    def body(dy_ref, x_ref, dt_ref, e_ref, bi_ref, al_ref, dk_ref, b_ref, c_ref, sp_ref,
             dx_ref, ddt_ref, db_ref, dc_ref, dbi_ref, dal_ref, ddk_ref, ds_ref):
        @pl.when(pl.program_id(0) == 0)
        def _():
            ds_ref[...] = jnp.zeros_like(ds_ref)
            dbi_ref[...] = jnp.zeros_like(dbi_ref)
            dal_ref[...] = jnp.zeros_like(dal_ref)
            ddk_ref[...] = jnp.zeros_like(ddk_ref)

        ddt, dbi, dal, ddk = 0.0, 0.0, 0.0, 0.0
        for g in range(SSM_GROUPS):
            wide, tile_lanes = _group_lanes(g)
            expand_rows = e_ref[:, wide]

            def f(xs, dt_tile, bias, alog, dsk, bg, cg, state):
                return _ssd_group(xs, dt_tile, expand_rows, bias, alog, dsk, bg, cg, state)

            _, vjp = jax.vjp(f, x_ref[:, wide], dt_ref[...], bi_ref[...], al_ref[...], dk_ref[...], b_ref[:, tile_lanes],
                             c_ref[:, tile_lanes], sp_ref[:, wide])
            (dx_ref[:, wide], ddt_g, dbi_g, dal_g, ddk_g, db_ref[:, tile_lanes], dc_ref[:, tile_lanes],
             ds_ref[:, wide]) = vjp((dy_ref[:, wide], ds_ref[:, wide]))
            ddt, dbi, dal, ddk = ddt + ddt_g, dbi + dbi_g, dal + dal_g, ddk + ddk_g
        ddt_ref[...] = ddt.astype(BF16)
        dbi_ref[...] += dbi
        dal_ref[...] += dal
        ddk_ref[...] += ddk

    def at():
        c = pl.program_id(0)
        return c == 0, c == 0, c == last

    par_shape = jax.ShapeDtypeStruct((1, LANES), F32)
    return _call_with_comm(
        body, comm, at, (dy, xbc, proj, expand, bias, alog, dsk, xbc, xbc, states), name="ssd_bwd",
        grid=(nb,), in_specs=[xs, xs, dt, ex, par, par, par, b, cc, state],
        out_specs=[xs, tile, nspec, nspec, par, par, par],
        out_shape=[jax.ShapeDtypeStruct((nb * BLOCK, SSM_INNER), F32), jax.ShapeDtypeStruct((nb * BLOCK, LANES), BF16),
                   jax.ShapeDtypeStruct((nb * BLOCK, BC_WIDTH), F32), jax.ShapeDtypeStruct((nb * BLOCK, BC_WIDTH), F32),
                   par_shape, par_shape, par_shape],
        scratch=[pltpu.VMEM((SSM_STATE, SSM_INNER), F32)])


SLAB_ROWS = 24
SLAB_META_ROW = 8


SLAB_LOSS_ROW = 7


def pack_small(dcw, dcb, dgpre, dgpost, dbias, dalog, ddsk, dsinks, dgn, dmeta, loss_tile):
    def body(cw, cb, gpre, gpost, dtb, al, dk, sk, gn, meta, loss, o_ref):
        o_ref[...] = jnp.zeros_like(o_ref)
        o_ref[SLAB_LOSS_ROW:SLAB_LOSS_ROW + 1, 0:LANES] = loss[0:1, :]
        o_ref[0:CONV_WIDTH, :] = cw[...]
        o_ref[4:5, :] = cb[...]
        o_ref[5:6, 0:1024] = gpre[...]
        o_ref[5:6, 1024:2048] = gpost[...]
        o_ref[5:6, 2048:2176] = dtb[...]
        o_ref[5:6, 2176:2304] = al[...]
        o_ref[5:6, 2304:2432] = dk[...]
        o_ref[5:6, 2432:2560] = sk[...]
        o_ref[6:7, 0:SSM_INNER] = gn[...]
        o_ref[SLAB_META_ROW:SLAB_META_ROW + N_META, 0:D_MODEL] = meta[...]

    args = (dcw, dcb, dgpre, dgpost, dbias, dalog, ddsk, dsinks, dgn, dmeta, loss_tile)
    return _call(body, name="pack_small", in_specs=[_full(a.shape) for a in args],
                 out_specs=_full((SLAB_ROWS, CONV_DIM)), out_shape=jax.ShapeDtypeStruct((SLAB_ROWS, CONV_DIM), F32))(*args)


def _lane_tile(v):
    return jnp.pad(v, ((0, 0), (0, LANES - v.shape[1])))


def kernel(x, meta_tokens, g_pre, w_in, conv_w, conv_b, dt_bias, a_log, d_skip, attn_sinks, g_ssm_norm, w_out_att, w_out_ssm, w_out, g_post, loss_target, m_meta_tokens, m_g_pre, m_w_in, m_conv_w, m_conv_b, m_dt_bias, m_a_log, m_d_skip, m_attn_sinks, m_g_ssm_norm, m_w_out_att, m_w_out_ssm, m_w_out, m_g_post, v_meta_tokens, v_g_pre, v_w_in, v_conv_w, v_conv_b, v_dt_bias, v_a_log, v_d_skip, v_attn_sinks, v_g_ssm_norm, v_w_out_att, v_w_out_ssm, v_w_out, v_g_post):
    chip = _chip_index()

    conv_w_rows = jnp.pad(conv_w[0], ((0, 2 * 8 - CONV_WIDTH), (0, 0)))
    w_in_t, m_w_in_t, v_w_in_t = w_in[0].T, m_w_in[0].T, v_w_in[0].T
    gathered_w_in, g_conv_w, g_meta = run_comm("gather_w_in",
                                               TwoLevelGather([pack_w_in(w_in_t), conv_w_rows, meta_tokens]))
    w_all_t = unpack_w_in(gathered_w_in)
    cw_full = g_conv_w[:, :CONV_WIDTH].transpose(1, 0, 2).reshape(CONV_WIDTH, CONV_DIM)
    meta_full = g_meta.transpose(1, 0, 2).reshape(N_META, D_MODEL)
    behind_w_in = 0.0 * g_meta[0, 0, 0]
    w_out_flight, _ = chip_exchange_start(
        "gather_w_out_start", [(w[0] + behind_w_in).astype(BF16) for w in (w_out_att, w_out_ssm, w_out)], False)

    h, u = prep(x, meta_full, g_pre)
    proj = project("in_proj", u, w_all_t)

    sinks3 = attn_sinks.reshape(ATT_Q_HEADS, 1, 1)
    a_att = attn_fwd(proj, sinks3)

    xbc = conv_fwd(proj, cw_full, conv_b)
    expand = _head_expand()
    head_pars = (_lane_tile(dt_bias), _lane_tile(a_log), _lane_tile(d_skip))
    y_ssd, states = ssd_fwd(xbc, proj, expand, *head_pars)
    woa, wos, wo = [g.reshape(-1, D_MODEL) for g in chip_exchange_wait("gather_w_out_wait", w_out_flight, False, y_ssd)]

    (yn, merged, dout, dy_att, dy_ssm, da_att, dy_ssd, dz_ssm, dga, dgs, dres, loss_tile, dg_post, dgn) = tail(
        y_ssd, proj, a_att, x, loss_target, woa, wos, wo, g_ssm_norm, g_post)

    dwo = mm_tn("out_proj_dw", merged, dout)
    dwoa = mm_tn("att_out_dw", a_att, dy_att)
    dwos = mm_tn("ssm_out_dw", yn, dy_ssm)
    dq, dz_att, dk, dv, dkmeta, dvmeta, dsinks3 = attn_bwd(da_att, proj, sinks3)
    dk = dk.at[PAD_ROWS:BLOCK].add(dkmeta).astype(BF16)
    dv = dv.at[PAD_ROWS:BLOCK].add(dvmeta).astype(BF16)

    def pieces(g):
        return g.reshape(4, 2, g.shape[0] // 8, g.shape[1])

    def to_owner(g):
        return (lambda ref, dev: ref.at[_chip_of(dev), dev[2]], (g.shape[0] // 8, g.shape[1]))

    (dxs, ddt_tile, dbg, dcg, dbias, dalog, ddsk), sent_w_out = ssd_bwd(
        dy_ssd, xbc, proj, expand, *head_pars, states,
        DirectExchange([pieces(dwoa), pieces(dwos), pieces(dwo)], [to_owner(dwoa), to_owner(dwos), to_owner(dwo)],
                       ALL_MASKS, "dev", 8))
    dxs_raw, dcw_xs, dcb_xs = conv_bwd("conv_bwd_x", [dxs], 0, proj, cw_full, conv_b)
    dbc_raw, dcw_bc, dcb_bc = conv_bwd("conv_bwd_bc", [dbg, dcg], SSM_INNER // D_MODEL, proj, cw_full, conv_b)
    dcw = jnp.concatenate([dcw_xs, dcw_bc], axis=1)
    dcb = jnp.concatenate([dcb_xs, dcb_bc], axis=1)

    narrow = jnp.concatenate([dk, dv, ddt_tile, jnp.zeros((dk.shape[0], N_ACT - N_ALIGNED), BF16)], axis=1)
    dproj = [dz_ssm, dxs_raw, dbc_raw, dq, dz_att, dga, dgs, narrow]
    dw_all_t = weight_grad_t("in_proj_dw", dproj, u)

    half_rows = PACK_W // 2
    partial = pack_grad_w_in(dw_all_t).reshape(4, 2, half_rows, D_MODEL)
    from_sibling, = run_comm("pair_grads", DirectExchange(
        [partial], [(lambda ref, dev: ref.at[pl.ds(0, 4), dev[2]], (4, half_rows, D_MODEL))], SIBLING_MASK, "core", 2,
        keep_own=False))
    chip_sum = sum_pair(partial, from_sibling)
    grads_flight, started = chip_exchange_start("reduce_w_in_start", [chip_sum], True)
    du = project_back("in_proj_dx", dproj, w_all_t, started)
    grad_x, dmeta, dg_pre = prep_bwd(h, du, dres, g_pre)
    sent_w_in, = chip_exchange_wait("reduce_w_in_wait", grads_flight, True, dg_pre)

    slab = pack_small(dcw, dcb, dg_pre, dg_post, dbias, dalog, ddsk, _lane_tile(dsinks3.reshape(1, ATT_Q_HEADS)), dgn,
                      dmeta, loss_tile)
    halves = [sum_slots("sum_" + nm, r)
              for nm, r in zip(("w_in", "w_out_att", "w_out_ssm", "w_out"), [sent_w_in] + list(sent_w_out))]
    shared = run_comm("share_grads", Both(DirectExchange(halves, [None] * 4, SIBLING_MASK, "core", 2),
                                          DirectExchange([slab], [None], ALL_MASKS, "dev", 8)))
    g_w_in_packed, g_woa, g_wos, g_wo = [f.reshape(2 * f.shape[1], f.shape[2]) for f in shared[:4]]
    small = sum_slots("sum_small", shared[4])
    loss = small[SLAB_LOSS_ROW, 0]

    g_w_in, d_w_in, nm_w_in, nv_w_in = [a.T for a in adamw_w_in(g_w_in_packed, w_in_t, m_w_in_t, v_w_in_t)]
    d_woa, nm_woa, nv_woa = adamw_rows("adamw_w_out_att", g_woa, w_out_att[0], m_w_out_att[0], v_w_out_att[0])
    d_wos, nm_wos, nv_wos = adamw_rows("adamw_w_out_ssm", g_wos, w_out_ssm[0], m_w_out_ssm[0], v_w_out_ssm[0])
    d_wo, nm_wo, nv_wo = adamw_rows("adamw_w_out", g_wo, w_out[0], m_w_out[0], v_w_out[0])

    cw_cols = CONV_DIM // 4
    meta_cols = D_MODEL // 4
    g_small = {
        "meta_tokens": lax.dynamic_slice(small, (SLAB_META_ROW, chip * meta_cols), (N_META, meta_cols)),
        "g_pre": small[5:6, 0:1024],
        "conv_w": lax.dynamic_slice(small, (0, chip * cw_cols), (CONV_WIDTH, cw_cols)),
        "conv_b": small[4:5, :],
        "dt_bias": small[5:6, 2048:2048 + SSM_HEADS],
        "a_log": small[5:6, 2176:2176 + SSM_HEADS],
        "d_skip": small[5:6, 2304:2304 + SSM_HEADS],
        "attn_sinks": small[5:6, 2432:2432 + ATT_Q_HEADS],
        "g_ssm_norm": small[6:7, 0:SSM_INNER],
        "g_post": small[5:6, 1024:2048],
    }
    names = list(g_small)
    w_small = dict(meta_tokens=meta_tokens, g_pre=g_pre, conv_w=conv_w[0], conv_b=conv_b, dt_bias=dt_bias, a_log=a_log,
                   d_skip=d_skip, attn_sinks=attn_sinks, g_ssm_norm=g_ssm_norm, g_post=g_post)
    m_small = dict(meta_tokens=m_meta_tokens, g_pre=m_g_pre, conv_w=m_conv_w[0], conv_b=m_conv_b, dt_bias=m_dt_bias,
                   a_log=m_a_log, d_skip=m_d_skip, attn_sinks=m_attn_sinks, g_ssm_norm=m_g_ssm_norm, g_post=m_g_post)
    v_small = dict(meta_tokens=v_meta_tokens, g_pre=v_g_pre, conv_w=v_conv_w[0], conv_b=v_conv_b, dt_bias=v_dt_bias,
                   a_log=v_a_log, d_skip=v_d_skip, attn_sinks=v_attn_sinks, g_ssm_norm=v_g_ssm_norm, g_post=v_g_post)
    upd = dict(zip(names, adamw_small([g_small[k] for k in names], [w_small[k] for k in names],
                                      [m_small[k] for k in names], [v_small[k] for k in names])))

    lead = {"conv_w"}

    def shaped(name, a):
        return a[None] if name in lead else a

    grads = dict(g_small, w_in=g_w_in, w_out_att=g_woa, w_out_ssm=g_wos, w_out=g_wo)
    deltas = dict({k: upd[k][0] for k in names}, w_in=d_w_in, w_out_att=d_woa, w_out_ssm=d_wos, w_out=d_wo)
    new_m = dict({k: upd[k][1] for k in names}, w_in=nm_w_in, w_out_att=nm_woa, w_out_ssm=nm_wos, w_out=nm_wo)
    new_v = dict({k: upd[k][2] for k in names}, w_in=nv_w_in, w_out_att=nv_woa, w_out_ssm=nv_wos, w_out=nv_wo)
    lead |= {"w_in", "w_out_att", "w_out_ssm", "w_out"}
    order = ["meta_tokens", "g_pre", "w_in", "conv_w", "conv_b", "dt_bias", "a_log", "d_skip", "attn_sinks",
             "g_ssm_norm", "w_out_att", "w_out_ssm", "w_out", "g_post"]
    outs = [loss, grad_x]
    for group in (grads, deltas, new_m, new_v):
        outs += [shaped(k, group[k]) for k in order]
    return tuple(outs)
```

```python
import functools

import numpy as np
import jax
import jax.numpy as jnp
from jax import lax
from jax.experimental import pallas as pl
from jax.experimental.pallas import tpu as pltpu

F32 = jnp.float32
BF16 = jnp.bfloat16
HI = lax.Precision.HIGHEST

D_MODEL = 1024
N_META = 16
BLOCK = 128
PAD_ROWS = BLOCK - N_META
NORM_EPS = 1e-6
HEAD_DIM = 64
ATT_Q_HEADS = 16
ATT_KV_HEADS = 4
ATT_GROUP = 4
SSM_INNER = 2048
SSM_HEADS = 32
SSM_GROUPS = 4
SSM_HEADS_PER_GROUP = 8
SSM_STATE = 128
CONV_WIDTH = 4
CONV_DIM = 3072
LANES = 128

ADAM_LR = 0.001
ADAM_B1 = 0.9
ADAM_B2 = 0.999
ADAM_EPS = 1e-08
ADAM_WD = 0.01
ADAM_STEP = 10

VMEM_LIMIT = 48 * 1024 * 1024

SHARD_W = 2440
PACK_W = 2560
SHARD_STRIDE = 2432
N_ALIGNED = 9856
N_ACT = 10240
SEG = {
    "q": (0, 1024, 5120), "k": (1024, 256, 9216), "v": (1280, 256, 9472), "z_att": (1536, 1024, 6144),
    "z_ssm": (2560, 2048, 0), "xbc": (4608, 3072, 2048), "dt": (7680, 128, 9728),
    "gate_att": (7808, 1024, 7168), "gate_ssm": (8832, 1024, 8192),
}
DT_STORED_START = 7680
DT_PAD = LANES - SSM_HEADS


def _act_col(aligned_col):
    for a0, w, p0 in SEG.values():
        if a0 <= aligned_col < a0 + w:
            return p0 + aligned_col - a0
    raise ValueError(aligned_col)


def _call(body, *, name, out_shape, in_specs, out_specs, grid=(), scratch=(), sem=None, aliases=None):
    return pl.pallas_call(
        body, out_shape=out_shape, grid=grid, in_specs=in_specs, out_specs=out_specs, scratch_shapes=list(scratch),
        name=name, input_output_aliases=aliases or {},
        compiler_params=pltpu.CompilerParams(dimension_semantics=sem, vmem_limit_bytes=VMEM_LIMIT))


def _full(shape):
    n = len(shape)
    return pl.BlockSpec(shape, lambda *_: (0,) * n)


def _chip_index():
    return lax.axis_index("x") * 2 + lax.axis_index("y")


_sigmoid = jax.nn.sigmoid


def _silu(z):
    return z * _sigmoid(z)


def _rms(x, g):
    return x * lax.rsqrt(jnp.mean(x * x, axis=-1, keepdims=True) + NORM_EPS) * g


def _peer(mask):
    x, y, c = lax.axis_index("x"), lax.axis_index("y"), lax.axis_index("c")
    return ((1 - x) if mask & 4 else x, (1 - y) if mask & 2 else y, (1 - c) if mask & 1 else c)


def _me():
    return lax.axis_index("x"), lax.axis_index("y"), lax.axis_index("c")


def _chip_of(dev):
    return 2 * dev[0] + dev[1]


CHIP_MASKS = (4, 2, 6)
ALL_MASKS = (1, 2, 3, 4, 5, 6, 7)
SIBLING_MASK = (1,)


def _remote(src, dst, send_sem, recv_sem, dev):
    return pltpu.make_async_remote_copy(src_ref=src, dst_ref=dst, send_sem=send_sem, recv_sem=recv_sem,
                                        device_id=dev, device_id_type=pl.DeviceIdType.MESH)


class _StagedCopy:
    def __init__(self, src, stage, dst, load_sem, store_sem):
        self.load = pltpu.make_async_copy(src, stage, load_sem)
        self.store = pltpu.make_async_copy(stage, dst, store_sem)

    def start(self):
        self.load.start()
        self.load.wait()
        self.store.start()

    def wait(self):
        self.store.wait()


class DirectExchange:
    def __init__(self, arrays, pieces, masks, slot_kind, nslots, keep_own=True):
        self.arrays, self.pieces, self.masks, self.slot_kind = list(arrays), list(pieces), masks, slot_kind
        self.keep_own = keep_own
        n, nk = len(arrays), len(masks)
        shapes = [a.shape if p is None else p[1] for a, p in zip(arrays, pieces)]
        self.out_shape = [jax.ShapeDtypeStruct((nslots,) + tuple(s), a.dtype) for s, a in zip(shapes, arrays)]
        self.scratch = [pltpu.SemaphoreType.DMA((n * nk,)), pltpu.SemaphoreType.DMA((n * nk,))]
        if keep_own:
            self.scratch += [pltpu.SemaphoreType.DMA((2 * n,))] + [pltpu.VMEM(s, a.dtype) for s, a in zip(shapes, arrays)]
        self.has_mid = False

    def _copies(self, ins, outs, scratch):
        send_sems, recv_sems = scratch[:2]
        me = _me()
        slot = {"chip": _chip_of(me), "dev": 4 * me[0] + 2 * me[1] + me[2], "core": me[2]}[self.slot_kind]
        nk = len(self.masks)

        def piece(a, dev):
            return ins[a] if self.pieces[a] is None else self.pieces[a][0](ins[a], dev)

        local = []
        if self.keep_own:
            local_sems, stages = scratch[2], scratch[3:]
            local = [_StagedCopy(piece(a, me), stages[a], outs[a].at[slot], local_sems.at[2 * a], local_sems.at[2 * a + 1])
                     for a in range(len(ins))]
        remote = []
        for a in range(len(ins)):
            for ki, mask in enumerate(self.masks):
                dev = _peer(mask)
                remote.append(_remote(piece(a, dev), outs[a].at[slot], send_sems.at[a * nk + ki],
                                      recv_sems.at[a * nk + ki], dev))
        return local, remote

    def start(self, ins, outs, scratch):
        local, remote = self._copies(ins, outs, scratch)
        for cp in remote + local:
            cp.start()

    def finish(self, ins, outs, scratch):
        local, remote = self._copies(ins, outs, scratch)
        for cp in remote + local:
            cp.wait()


SPLIT_ROWS = 16


class TwoLevelGather:
    def __init__(self, arrays):
        self.arrays = list(arrays)
        n = len(arrays)
        self.out_shape = [jax.ShapeDtypeStruct((4,) + a.shape, a.dtype) for a in arrays]
        self.scratch = ([pltpu.SemaphoreType.DMA((4 * n,)), pltpu.SemaphoreType.DMA((4 * n,)),
                         pltpu.SemaphoreType.DMA((3 * n,)), pltpu.SemaphoreType.DMA((3 * n,)),
                         pltpu.SemaphoreType.DMA((2 * n,))] + [pltpu.VMEM(a.shape, a.dtype) for a in arrays])
        self.has_mid = True

    def _copies(self, ins, outs, scratch):
        ici_send, ici_recv, fwd_send, fwd_recv, local_sems = scratch[:5]
        stages = scratch[5:]
        me = _me()
        sibling, in_x, in_y, diagonal = _peer(1), _peer(4), _peer(2), _peer(6)
        plan = []
        for a in range(len(ins)):
            half = ins[a].shape[0] // 2
            first = half // 2 if half % (2 * SPLIT_ROWS) == 0 else half
            mine = pl.ds(me[2] * half, half)
            local = _StagedCopy(ins[a], stages[a], outs[a].at[_chip_of(me)], local_sems.at[2 * a], local_sems.at[2 * a + 1])

            def ici(k, src, dst, dev):
                return _remote(src, dst, ici_send.at[4 * a + k], ici_recv.at[4 * a + k], dev)

            def d2d(k, chip):
                zone = outs[a].at[_chip_of(chip), mine]
                return _remote(zone, zone, fwd_send.at[3 * a + k], fwd_recv.at[3 * a + k], sibling)

            own_zone = outs[a].at[_chip_of(me), mine]
            from_x = outs[a].at[_chip_of(in_x), pl.ds(me[2] * half, first)]
            onward = [ici(2, from_x, from_x, in_y), None]
            if first < half:
                from_y = outs[a].at[_chip_of(in_y), pl.ds(me[2] * half + first, half - first)]
                onward[1] = ici(3, from_y, from_y, in_x)
            plan.append(dict(
                local=local,
                own=[ici(0, ins[a].at[mine], own_zone, in_x), ici(1, ins[a].at[mine], own_zone, in_y)],
                onward=onward, sibling=[d2d(0, in_x), d2d(1, in_y), d2d(2, diagonal)]))
        return plan

    def start(self, ins, outs, scratch):
        for p in self._copies(ins, outs, scratch):
            for cp in p["own"]:
                cp.start()
            p["local"].start()

    def mid(self, ins, outs, scratch):
        plan = self._copies(ins, outs, scratch)
        for p in plan:
            for k in range(2):
                p["own"][k].wait_recv()
                if p["onward"][k] is not None:
                    p["onward"][k].start()
                p["sibling"][k].start()
        for p in plan:
            for cp in p["onward"]:
                if cp is not None:
                    cp.wait_recv()
            p["sibling"][2].start()

    def finish(self, ins, outs, scratch):
        for p in self._copies(ins, outs, scratch):
            for cp in p["sibling"]:
                cp.wait_recv()
            for cp in p["own"] + p["sibling"] + [cp for cp in p["onward"] if cp is not None]:
                cp.wait_send()
            p["local"].wait()


class Both:
    def __init__(self, a, b):
        self.a, self.b = a, b
        self.arrays, self.out_shape = a.arrays + b.arrays, a.out_shape + b.out_shape
        self.scratch = a.scratch + b.scratch
        self.has_mid = False
        assert not (a.has_mid or b.has_mid)

    def _parts(self, ins, outs, sems):
        na, sa = len(self.a.arrays), len(self.a.scratch)
        return (ins[:na], outs[:na], sems[:sa]), (ins[na:], outs[na:], sems[sa:])

    def start(self, ins, outs, sems):
        pa, pb = self._parts(ins, outs, sems)
        self.a.start(*pa)
        self.b.start(*pb)

    def finish(self, ins, outs, sems):
        pa, pb = self._parts(ins, outs, sems)
        self.a.finish(*pa)
        self.b.finish(*pb)


_ANY = pl.BlockSpec(memory_space=pl.ANY)


def run_comm(name, comm):
    n = len(comm.arrays)

    def body(*refs):
        ins, outs, sems = refs[:n], refs[n:2 * n], refs[2 * n:]
        comm.start(ins, outs, sems)
        if comm.has_mid:
            comm.mid(ins, outs, sems)
        comm.finish(ins, outs, sems)

    return pl.pallas_call(body, name=name, out_shape=comm.out_shape, in_specs=[_ANY] * n, out_specs=[_ANY] * n,
                          scratch_shapes=comm.scratch,
                          compiler_params=pltpu.CompilerParams(vmem_limit_bytes=VMEM_LIMIT))(*comm.arrays)


_HBM = pl.BlockSpec(memory_space=pltpu.HBM)
_SEM = pl.BlockSpec(memory_space=pltpu.SEMAPHORE)
_SIDE_EFFECT = pltpu.SideEffectType.DATAFLOW_SIDE_EFFECTING


def _chip_copies(srcs, lands, send_sems, recv_sems, by_target):
    me = _me()
    copies = []
    for a, (src, land) in enumerate(zip(srcs, lands)):
        for ki, mask in enumerate(CHIP_MASKS):
            dev = _peer(mask)
            k = a * len(CHIP_MASKS) + ki
            piece = src.at[_chip_of(dev)] if by_target else src
            copies.append(_remote(piece, land.at[_chip_of(me)], send_sems.at[k], recv_sems.at[k], dev))
    return copies


def chip_exchange_start(name, arrays, by_target):
    n = len(arrays)
    nsem = n * len(CHIP_MASKS)
    piece_shapes = [a.shape[1:] if by_target else a.shape for a in arrays]

    def body(*refs):
        srcs, lands = refs[:n], refs[n:2 * n]
        send_sems, recv_sems = refs[2 * n:2 * n + 2]
        token = refs[4 * n + 2]
        stages, local_sems = refs[4 * n + 3:5 * n + 3], refs[5 * n + 3]
        me = _me()
        for cp in _chip_copies(srcs, lands, send_sems, recv_sems, by_target):
            cp.start()
        for a in range(n):
            own = _StagedCopy(srcs[a].at[_chip_of(me)] if by_target else srcs[a], stages[a], lands[a].at[_chip_of(me)],
                              local_sems.at[2 * a], local_sems.at[2 * a + 1])
            own.start()
            own.wait()
        token[...] = jnp.zeros_like(token)

    lands = [lax.empty((4,) + tuple(s), a.dtype) for s, a in zip(piece_shapes, arrays)]
    hbm = lambda a: pltpu.HBM(a.shape, a.dtype)
    res = pl.pallas_call(
        body, name=name,
        out_shape=(pltpu.SemaphoreType.DMA((nsem,)), pltpu.SemaphoreType.DMA((nsem,)), *[hbm(a) for a in arrays],
                   *[hbm(l) for l in lands], jax.ShapeDtypeStruct((8, LANES), F32)),
        in_specs=(_HBM,) * (2 * n), out_specs=(_SEM, _SEM) + (_HBM,) * (2 * n) + (pl.BlockSpec(memory_space=pltpu.VMEM),),
        input_output_aliases={i: i + 2 for i in range(2 * n)},
        scratch_shapes=[pltpu.VMEM(tuple(s), a.dtype) for s, a in zip(piece_shapes, arrays)]
        + [pltpu.SemaphoreType.DMA((2 * n,))],
        compiler_params=pltpu.CompilerParams(has_side_effects=_SIDE_EFFECT, vmem_limit_bytes=VMEM_LIMIT),
    )(*[pltpu.with_memory_space_constraint(a, pltpu.HBM) for a in arrays + lands])
    return res[:-1], res[-1]


def chip_exchange_wait(name, in_flight, by_target, after):
    send_sems, recv_sems, *thru = in_flight
    n = len(thru) // 2

    def body(*refs):
        srcs, lands, (send, recv) = refs[:n], refs[n:2 * n], refs[2 * n:2 * n + 2]
        for cp in _chip_copies(srcs, lands, send, recv, by_target):
            cp.wait_send()
            cp.wait_recv()

    return pl.pallas_call(
        body, name=name, out_shape=tuple(pltpu.HBM(t.shape, t.dtype) for t in thru),
        in_specs=(_HBM,) * (2 * n) + (_SEM, _SEM, pl.BlockSpec(memory_space=pl.ANY)), out_specs=(_HBM,) * (2 * n),
        input_output_aliases={i: i for i in range(2 * n)},
        compiler_params=pltpu.CompilerParams(has_side_effects=_SIDE_EFFECT),
    )(*thru, send_sems, recv_sems, after)[n:]


def _call_with_comm(body, comm, steps, args, *, name, out_shape, in_specs, out_specs, grid, scratch=()):
    ni, no, ns, nc = len(in_specs), len(out_specs), len(scratch), len(comm.arrays)

    def full_body(*refs):
        ins, cins = refs[:ni], refs[ni:ni + nc]
        outs, couts = refs[ni + nc:ni + nc + no], refs[ni + nc + no:ni + 2 * nc + no]
        scr, csems = refs[ni + 2 * nc + no:ni + 2 * nc + no + ns], refs[ni + 2 * nc + no + ns:]
        first, middle, last = steps()
        pl.when(first)(lambda: comm.start(cins, couts, csems))
        if comm.has_mid:
            pl.when(middle)(lambda: comm.mid(cins, couts, csems))
        body(*ins, *outs, *scr)
        pl.when(last)(lambda: comm.finish(cins, couts, csems))

    res = pl.pallas_call(
        full_body, name=name, out_shape=list(out_shape) + comm.out_shape, grid=grid,
        in_specs=list(in_specs) + [_ANY] * nc, out_specs=list(out_specs) + [_ANY] * nc,
        scratch_shapes=list(scratch) + comm.scratch,
        compiler_params=pltpu.CompilerParams(dimension_semantics=("arbitrary",) * len(grid),
                                             vmem_limit_bytes=VMEM_LIMIT))(*args, *comm.arrays)
    return res[:no], res[no:]


def _shard_pieces(chip):
    if chip < 3:
        return [(0, SHARD_W, 8 * chip)]
    behind_dt = DT_STORED_START + SSM_HEADS - 3 * SHARD_W
    return [(0, behind_dt, 24), (behind_dt, SHARD_W - behind_dt, behind_dt + 24 + DT_PAD)]


W_IN_COLS = 256


def pack_w_in(wt):
    def body(w_ref, o_ref, pad_ref):
        chip = _chip_index()
        pad_ref[...] = jnp.zeros_like(pad_ref)
        for cv in range(4):
            @pl.when(chip == cv)
            def _():
                for src, n, dst in _shard_pieces(cv):
                    pad_ref[dst:dst + n, :] = w_ref[src:src + n, :]
        o_ref[...] = pad_ref[...].astype(BF16)

    return _call(body, name="pack_w_in", grid=(D_MODEL // W_IN_COLS,),
                 in_specs=[pl.BlockSpec((SHARD_W, W_IN_COLS), lambda i: (0, i))],
                 out_specs=pl.BlockSpec((PACK_W, W_IN_COLS), lambda i: (0, i)),
                 out_shape=jax.ShapeDtypeStruct((PACK_W, D_MODEL), BF16),
                 scratch=[pltpu.VMEM((PACK_W, W_IN_COLS), F32)], sem=("parallel",))(wt)


def _tile_runs():
    runs, fix = [], []
    for t in range(N_ALIGNED // LANES):
        s = min(t // 19, 3)
        j = t - 19 * s
        p = _act_col(t * LANES)
        if runs and runs[-1][1] == s and runs[-1][0] + runs[-1][3] == p and runs[-1][2] + runs[-1][3] == j * LANES:
            runs[-1][3] += LANES
        else:
            runs.append([p, s, j * LANES, LANES])
        if j == 0 and s > 0:
            fix.append((p, s - 1))
    return runs, fix


def unpack_w_in(bg):
    runs, fix = _tile_runs()

    def body(b_ref, o_ref):
        for p, s, j, w in runs:
            o_ref[p:p + w, :] = b_ref[s, j:j + w, :]
        for p, s in fix:
            o_ref[p:p + LANES, :] = o_ref[p:p + LANES, :] + b_ref[s, SHARD_STRIDE:PACK_W, :]
        o_ref[N_ALIGNED:N_ACT, :] = jnp.zeros((N_ACT - N_ALIGNED, W_IN_COLS), BF16)

    return _call(body, name="unpack_w_in", grid=(D_MODEL // W_IN_COLS,),
                 in_specs=[pl.BlockSpec((4, PACK_W, W_IN_COLS), lambda i: (0, 0, i))],
                 out_specs=pl.BlockSpec((N_ACT, W_IN_COLS), lambda i: (0, i)),
                 out_shape=jax.ShapeDtypeStruct((N_ACT, D_MODEL), BF16), sem=("parallel",))(bg)


def pack_grad_w_in(dwt):
    def body(g_ref, o_ref):
        for s in range(4):
            for j in range(PACK_W // LANES):
                p = _act_col((19 * s + j) * LANES)
                o_ref[s, j * LANES:(j + 1) * LANES, :] = g_ref[p:p + LANES, :]

    return _call(body, name="pack_grad_w_in", grid=(D_MODEL // W_IN_COLS,),
                 in_specs=[pl.BlockSpec((N_ACT, W_IN_COLS), lambda i: (0, i))],
                 out_specs=pl.BlockSpec((4, PACK_W, W_IN_COLS), lambda i: (0, 0, i)),
                 out_shape=jax.ShapeDtypeStruct((4, PACK_W, D_MODEL), BF16), sem=("parallel",))(dwt)


def _adamw(w, g, m, v):
    m = ADAM_B1 * m + (1.0 - ADAM_B1) * g
    v = ADAM_B2 * v + (1.0 - ADAM_B2) * jnp.square(g)
    m_hat = m / (1.0 - ADAM_B1 ** ADAM_STEP)
    v_hat = v / (1.0 - ADAM_B2 ** ADAM_STEP)
    delta = -ADAM_LR * (m_hat / (jnp.sqrt(v_hat) + ADAM_EPS) + ADAM_WD * w)
    return delta, m, v


def adamw_w_in(g_packed, wt, mt, vt):
    cols = LANES

    def body(g_ref, w_ref, m_ref, v_ref, go_ref, d_ref, mo_ref, vo_ref):
        chip = _chip_index()
        for cv in range(4):
            @pl.when(chip == cv)
            def _():
                for dst, n, src in _shard_pieces(cv):
                    go_ref[dst:dst + n, :] = g_ref[src:src + n, :]
        d_ref[...], mo_ref[...], vo_ref[...] = _adamw(w_ref[...], go_ref[...], m_ref[...], v_ref[...])

    spec = pl.BlockSpec((SHARD_W, cols), lambda i: (0, i))
    shp = jax.ShapeDtypeStruct((SHARD_W, D_MODEL), F32)
    return _call(body, name="adamw_w_in", grid=(D_MODEL // cols,),
                 in_specs=[pl.BlockSpec((PACK_W, cols), lambda i: (0, i)), spec, spec, spec],
                 out_specs=[spec] * 4, out_shape=[shp] * 4, sem=("parallel",))(g_packed, wt, mt, vt)


def adamw_rows(name, g, w, m, v):
    r, c = g.shape
    rows = min(r, BLOCK)

    def body(g_ref, w_ref, m_ref, v_ref, d_ref, mo_ref, vo_ref):
        d_ref[...], mo_ref[...], vo_ref[...] = _adamw(w_ref[...], g_ref[...], m_ref[...], v_ref[...])

    spec = pl.BlockSpec((rows, c), lambda i: (i, 0))
    shp = jax.ShapeDtypeStruct((r, c), F32)
    return _call(body, name=name, grid=(r // rows,), in_specs=[spec] * 4, out_specs=[spec] * 3, out_shape=[shp] * 3,
                 sem=("parallel",))(g, w, m, v)


def adamw_small(gs, ws, ms, vs):
    n = len(gs)

    def body(*refs):
        g, w, m, v = refs[:n], refs[n:2 * n], refs[2 * n:3 * n], refs[3 * n:4 * n]
        outs = refs[4 * n:]
        for i in range(n):
            d, mn, vn = _adamw(w[i][...], g[i][...], m[i][...], v[i][...])
            outs[3 * i][...] = d
            outs[3 * i + 1][...] = mn
            outs[3 * i + 2][...] = vn

    specs = [_full(a.shape) for a in gs]
    res = _call(body, name="adamw_small", in_specs=specs * 4,
                out_specs=[s for s in specs for _ in range(3)],
                out_shape=[jax.ShapeDtypeStruct(a.shape, F32) for a in gs for _ in range(3)])(*gs, *ws, *ms, *vs)
    return [tuple(res[3 * i:3 * i + 3]) for i in range(n)]


def sum_slots(name, r):
    s, rr, c = r.shape
    rows = min(rr, BLOCK)

    def body(r_ref, o_ref):
        acc = r_ref[0].astype(F32)
        for k in range(1, s):
            acc = acc + r_ref[k].astype(F32)
        o_ref[...] = acc

    return _call(body, name=name, grid=(rr // rows,), in_specs=[pl.BlockSpec((s, rows, c), lambda i: (0, i, 0))],
                 out_specs=pl.BlockSpec((rows, c), lambda i: (i, 0)), out_shape=jax.ShapeDtypeStruct((rr, c), F32),
                 sem=("parallel",))(r)


def sum_pair(partial, from_sibling):
    s, _, rr, cols = partial.shape
    rows = rr // 2

    def body(p_ref, r_ref, o_ref):
        c = lax.axis_index("c")
        o_ref[...] = (p_ref[c].astype(F32) + r_ref[1 - c].astype(F32)).astype(BF16)

    return _call(body, name="sum_pair", grid=(s, rr // rows),
                 in_specs=[pl.BlockSpec((None, 2, rows, cols), lambda k, i: (k, 0, i, 0)),
                           pl.BlockSpec((2, None, rows, cols), lambda k, i: (0, k, i, 0))],
                 out_specs=pl.BlockSpec((None, rows, cols), lambda k, i: (k, i, 0)),
                 out_shape=jax.ShapeDtypeStruct((s, rr, cols), BF16), sem=("parallel", "parallel"))(partial, from_sibling)


def _col_tile(n, k):
    if n % 896 == 0 and k <= 1024:
        return 896
    return min(n, 512)


def project(name, x, wt, after):
    m, k = x.shape
    n = wt.shape[0]
    tn = D_MODEL

    def body(x_ref, w_ref, after_ref, o_ref):
        o_ref[...] = lax.dot_general(x_ref[...], w_ref[...], _NT, preferred_element_type=F32)

    return _call(body, name=name, grid=(n // tn,),
                 in_specs=[_full((m, k)), pl.BlockSpec((tn, k), lambda j: (j, 0)), _full(after.shape)],
                 out_specs=pl.BlockSpec((m, tn), lambda j: (0, j)), out_shape=jax.ShapeDtypeStruct((m, n), F32),
                 sem=("parallel",))(x, wt, after)


def _piece_tiles(pieces):
    spans, start = [], 0
    for p in pieces:
        spans.append((start, p.shape[1] // D_MODEL))
        start += p.shape[1] // D_MODEL
    return spans, start


def _piece_spec(tm, span, rows_of, tile_of):
    first, count = span

    def index(i, j):
        t = tile_of(i, j) - first
        mine = (t >= 0) & (t < count)
        return jnp.where(mine, rows_of(i, j), 0), jnp.clip(t, 0, count - 1)

    return pl.BlockSpec((tm, D_MODEL), index)


def project_back(name, pieces, wt, after):
    m = pieces[0].shape[0]
    k = wt.shape[1]
    tm = m // 2
    spans, steps = _piece_tiles(pieces)

    def body(*refs):
        w_ref, o_ref = refs[len(pieces)], refs[len(pieces) + 2]
        j = pl.program_id(1)

        @pl.when(j == 0)
        def _():
            o_ref[...] = jnp.zeros_like(o_ref)

        for dy_ref, (first, count) in zip(refs, spans):
            @pl.when((j >= first) & (j < first + count))
            def _():
                o_ref[...] += jnp.dot(dy_ref[...], w_ref[...], preferred_element_type=F32)

    return _call(body, name=name, grid=(m // tm, steps),
                 in_specs=[_piece_spec(tm, s, lambda i, j: i, lambda i, j: j) for s in spans]
                 + [pl.BlockSpec((D_MODEL, k), lambda i, j: (j, 0)), _full(after.shape)],
                 out_specs=pl.BlockSpec((tm, k), lambda i, j: (i, 0)), out_shape=jax.ShapeDtypeStruct((m, k), F32),
                 sem=("parallel", "arbitrary"))(*pieces, wt, after)


def weight_grad_t(name, pieces, x):
    m = pieces[0].shape[0]
    k = x.shape[1]
    tm = m // 2
    spans, steps = _piece_tiles(pieces)

    def body(*refs):
        x_ref, o_ref, acc_ref = refs[len(pieces):]
        i, half = pl.program_id(0), pl.program_id(1)
        for dy_ref, (first, count) in zip(refs, spans):
            @pl.when((i >= first) & (i < first + count))
            def _():
                part = lax.dot_general(dy_ref[...], x_ref[...], (((0,), (0,)), ((), ())), preferred_element_type=F32)

                @pl.when(half == 0)
                def _():
                    acc_ref[...] = part

                @pl.when(half == 1)
                def _():
                    o_ref[...] = (acc_ref[...] + part).astype(BF16)

    return _call(body, name=name, grid=(steps, 2),
                 in_specs=[_piece_spec(tm, s, lambda i, j: j, lambda i, j: i) for s in spans]
                 + [pl.BlockSpec((tm, k), lambda i, j: (j, 0))],
                 out_specs=pl.BlockSpec((D_MODEL, k), lambda i, j: (i, 0)),
                 out_shape=jax.ShapeDtypeStruct((steps * D_MODEL, k), BF16),
                 scratch=[pltpu.VMEM((D_MODEL, k), F32)], sem=("parallel", "arbitrary"))(*pieces, x)


def mm_tn(name, x, dy):
    m, k = x.shape
    n = dy.shape[1]
    tm = m // 2
    tn = _col_tile(n, k)

    def body(x_ref, dy_ref, o_ref, acc_ref):
        part = lax.dot_general(x_ref[...].astype(BF16), dy_ref[...].astype(BF16), (((0,), (0,)), ((), ())),
                               preferred_element_type=F32)

        @pl.when(pl.program_id(1) == 0)
        def _():
            acc_ref[...] = part

        @pl.when(pl.program_id(1) == 1)
        def _():
            o_ref[...] = (acc_ref[...] + part).astype(BF16)

    return _call(body, name=name, grid=(n // tn, 2),
                 in_specs=[pl.BlockSpec((tm, k), lambda i, j: (j, 0)), pl.BlockSpec((tm, tn), lambda i, j: (j, i))],
                 out_specs=pl.BlockSpec((k, tn), lambda i, j: (0, i)), out_shape=jax.ShapeDtypeStruct((k, n), BF16),
                 scratch=[pltpu.VMEM((k, tn), F32)], sem=("parallel", "arbitrary"))(x, dy)


def _row_spec(width, col_block=0):
    return pl.BlockSpec((BLOCK, width), lambda i: (i, col_block))


def _x_spec():
    return pl.BlockSpec((None, BLOCK, D_MODEL), lambda i: (0, jnp.maximum(i - 1, 0), 0))


def prep(x, meta, g_pre):
    nb = x.shape[1] // BLOCK + 1

    def body(x_ref, meta_ref, g_ref, h_ref, u_ref):
        i = pl.program_id(0)

        @pl.when(i == 0)
        def _():
            h_ref[0:PAD_ROWS, :] = jnp.zeros((PAD_ROWS, D_MODEL), F32)
            h_ref[PAD_ROWS:BLOCK, :] = meta_ref[...]

        @pl.when(i > 0)
        def _():
            h_ref[...] = x_ref[...]

        u_ref[...] = _rms(h_ref[...], g_ref[...]).astype(BF16)

    return _call(body, name="prep", grid=(nb,), in_specs=[_x_spec(), _full((N_META, D_MODEL)), _full((1, D_MODEL))],
                 out_specs=[_row_spec(D_MODEL), _row_spec(D_MODEL)],
                 out_shape=[jax.ShapeDtypeStruct((nb * BLOCK, D_MODEL), F32),
                            jax.ShapeDtypeStruct((nb * BLOCK, D_MODEL), BF16)], sem=("parallel",))(x, meta, g_pre)


def prep_bwd(h, du, dres, g_pre):
    nb = h.shape[0] // BLOCK

    def body(h_ref, du_ref, dres_ref, g_ref, gx_ref, gm_ref, gg_ref):
        i = pl.program_id(0)
        _, vjp = jax.vjp(_rms, h_ref[...], g_ref[...])
        dh, dg = vjp(du_ref[...])

        @pl.when(i == 0)
        def _():
            gm_ref[...] = dh[PAD_ROWS:BLOCK, :]
            gg_ref[...] = dg

        @pl.when(i > 0)
        def _():
            gg_ref[...] += dg

        gx_ref[...] = dh + dres_ref[...]

    return _call(body, name="prep_bwd", grid=(nb,),
                 in_specs=[_row_spec(D_MODEL), _row_spec(D_MODEL), _row_spec(D_MODEL), _full((1, D_MODEL))],
                 out_specs=[_x_spec(), _full((N_META, D_MODEL)), _full((1, D_MODEL))],
                 out_shape=[jax.ShapeDtypeStruct((1, (nb - 1) * BLOCK, D_MODEL), F32),
                            jax.ShapeDtypeStruct((N_META, D_MODEL), F32), jax.ShapeDtypeStruct((1, D_MODEL), F32)],
                 sem=("arbitrary",))(h, du, dres, g_pre)


GROUP_W = SSM_INNER // SSM_GROUPS


def _gated_norm(y, z, g):
    t = y * _silu(z)
    return t * lax.rsqrt(jnp.mean(t * t, axis=-1, keepdims=True) + NORM_EPS) * g


def _gated_norm_groups(y, z, g):
    groups = [slice(k * GROUP_W, (k + 1) * GROUP_W) for k in range(SSM_GROUPS)]
    return jnp.concatenate([_gated_norm(y[:, s], z[:, s], g[:, s]) for s in groups], axis=1)


def _merge(ga, gs, ya, ys):
    return _sigmoid(ga) * ya + _sigmoid(gs) * ys


GATE_ATT_BLOCK = SEG["gate_att"][2] // D_MODEL
GATE_SSM_BLOCK = SEG["gate_ssm"][2] // D_MODEL


def _row_loss(out, g_post, x, target):
    diff = x + _rms(out, g_post) - target
    return 0.5 * jnp.sum(diff * diff) / D_MODEL


def tail(y_ssd, proj, a_att, x, target, woa, wos, wo, g_norm, g_post):
    nb = y_ssd.shape[0] // BLOCK
    rows = nb * BLOCK

    def body(y_ref, z_ref, ga_ref, gs_ref, a_ref, x_ref, t_ref, woa_ref, wos_ref, wo_ref, gn_ref, gp_ref,
             yn_ref, mg_ref, dout_ref, dya_ref, dys_ref, da_ref, dy_ref, dz_ref, dga_ref, dgs_ref, dres_ref,
             loss_ref, dgp_ref, dgn_ref):
        i = pl.program_id(0)
        yn, norm_vjp = jax.vjp(_gated_norm_groups, y_ref[...], z_ref[...], gn_ref[...])
        yn16 = yn.astype(BF16)
        y_ssm = jnp.dot(yn16, wos_ref[...], preferred_element_type=F32)
        y_att = jnp.dot(a_ref[...], woa_ref[...], preferred_element_type=F32)
        merged, merge_vjp = jax.vjp(_merge, ga_ref[...], gs_ref[...], y_att, y_ssm)
        merged16 = merged.astype(BF16)
        out = jnp.dot(merged16, wo_ref[...], preferred_element_type=F32)
        loss, loss_vjp = jax.vjp(_row_loss, out, gp_ref[...], x_ref[...], t_ref[...])
        counted = jnp.where(i > 0, 1.0, 0.0)
        dout, dgp, dres, _ = loss_vjp(counted)
        dout16 = dout.astype(BF16)
        dmerged = lax.dot_general(dout16, wo_ref[...], _NT, preferred_element_type=F32)
        dga, dgs, dya, dys = merge_vjp(dmerged)
        dya16, dys16 = dya.astype(BF16), dys.astype(BF16)
        dyn = lax.dot_general(dys16, wos_ref[...], _NT, preferred_element_type=F32)
        dy, dz, dgn = norm_vjp(dyn)

        yn_ref[...] = yn16
        mg_ref[...] = merged16
        dout_ref[...] = dout16
        dya_ref[...] = dya16
        dys_ref[...] = dys16
        da_ref[...] = lax.dot_general(dya16, woa_ref[...], _NT, preferred_element_type=F32)
        dy_ref[...] = dy
        dz_ref[...] = dz.astype(BF16)
        dga_ref[...] = dga.astype(BF16)
        dgs_ref[...] = dgs.astype(BF16)
        dres_ref[...] = dres

        @pl.when(i == 0)
        def _():
            loss_ref[...] = jnp.zeros_like(loss_ref)
            dgp_ref[...] = jnp.zeros_like(dgp_ref)
            dgn_ref[...] = jnp.zeros_like(dgn_ref)

        loss_ref[...] += loss * counted
        dgp_ref[...] += dgp
        dgn_ref[...] += dgn

    wide, narrow = _row_spec(SSM_INNER), _row_spec(D_MODEL)
    resident = pl.BlockSpec(memory_space=pltpu.VMEM)
    bf = lambda w: jax.ShapeDtypeStruct((rows, w), BF16)
    f32 = lambda w: jax.ShapeDtypeStruct((rows, w), F32)
    return _call(body, name="tail", grid=(nb,),
                 in_specs=[wide, wide, _row_spec(D_MODEL, GATE_ATT_BLOCK), _row_spec(D_MODEL, GATE_SSM_BLOCK), narrow,
                           _x_spec(), _x_spec(), resident, resident, resident, _full((1, SSM_INNER)),
                           _full((1, D_MODEL))],
                 out_specs=[wide, narrow, narrow, narrow, narrow, narrow, wide, wide, narrow, narrow, narrow,
                            _full((8, LANES)), _full((1, D_MODEL)), _full((1, SSM_INNER))],
                 out_shape=[bf(SSM_INNER), bf(D_MODEL), bf(D_MODEL), bf(D_MODEL), bf(D_MODEL), f32(D_MODEL),
                            f32(SSM_INNER), bf(SSM_INNER), bf(D_MODEL), bf(D_MODEL), f32(D_MODEL),
                            jax.ShapeDtypeStruct((8, LANES), F32), jax.ShapeDtypeStruct((1, D_MODEL), F32),
                            jax.ShapeDtypeStruct((1, SSM_INNER), F32)],
                 sem=("arbitrary",))(y_ssd, proj, proj, proj, a_att, x, target, woa, wos, wo, g_norm, g_post)


_NT = (((1,), (1,)), ((), ()))
ALIBI_SLOPES = tuple(2.0 ** (-8.0 * (h + 1) / ATT_Q_HEADS) for h in range(ATT_Q_HEADS))
KV_WIDTH = ATT_KV_HEADS * HEAD_DIM
Q_BLOCK = SEG["q"][2] // D_MODEL
Z_ATT_BLOCK = SEG["z_att"][2] // D_MODEL
K_BLOCK = SEG["k"][2] // KV_WIDTH
V_BLOCK = SEG["v"][2] // KV_WIDTH
META_ROW_BLOCK = PAD_ROWS // N_META


@jax.custom_vjp
def _swap_halves(x):
    return pltpu.roll(x, HEAD_DIM, 1)


_swap_halves.defvjp(lambda x: (pltpu.roll(x, HEAD_DIM, 1), None), lambda _, g: (pltpu.roll(g, HEAD_DIM, 1),))


def _both_halves(t, half):
    first = lax.broadcasted_iota(jnp.int32, t.shape, 1) < HEAD_DIM
    sw = _swap_halves(t)
    return jnp.where(first, t, sw) if half == 0 else jnp.where(first, sw, t)


def _attn_rows(q, z, kp, kc, vp, vc, km, vm, sinks, n):
    rows = ATT_GROUP * BLOCK
    i = lax.broadcasted_iota(jnp.int32, (rows, BLOCK), 0) & (BLOCK - 1)
    j = lax.broadcasted_iota(jnp.int32, (rows, BLOCK), 1)
    rel_c = (i - j).astype(F32)
    rel_p = rel_c + float(BLOCK)
    nv = jnp.zeros((rows, BLOCK), jnp.int32) + n
    ok_c = (i >= j) & (nv >= 1)
    ok_p = (j > i) & (nv >= 2)
    im = lax.broadcasted_iota(jnp.int32, (rows, N_META), 0) & (BLOCK - 1)
    jm = lax.broadcasted_iota(jnp.int32, (rows, N_META), 1)
    ok_m = ((jnp.zeros((rows, N_META), jnp.int32) + n) >= 1) | (im >= PAD_ROWS + jm)
    first = lax.broadcasted_iota(jnp.int32, (BLOCK, LANES), 1) < HEAD_DIM
    neg = -jnp.inf
    outs = []
    for kv in range(ATT_KV_HEADS):
        tile, half = divmod(kv, 2)
        lanes = slice(tile * LANES, (tile + 1) * LANES)
        kc2, kp2, km2 = (_both_halves(t[:, lanes], half).astype(BF16) for t in (kc, kp, km))
        vc2, vp2, vm2 = (_both_halves(t[:, lanes], half).astype(BF16) for t in (vc, vp, vm))
        qs, slope, sk = [], [], []
        for pair in range(ATT_GROUP // 2):
            c0 = (kv * ATT_GROUP + 2 * pair) * HEAD_DIM
            qp = q[:, c0:c0 + LANES] * HEAD_DIM ** -0.5
            qs += [jnp.where(first, qp, 0.0), jnp.where(first, 0.0, qp)]
        for g in range(ATT_GROUP):
            slope.append(jnp.full((BLOCK, 1), ALIBI_SLOPES[kv * ATT_GROUP + g], F32))
            sk.append(jnp.broadcast_to(sinks[kv * ATT_GROUP + g], (BLOCK, 1)))
        qs = jnp.concatenate(qs, axis=0).astype(BF16)
        slope = jnp.concatenate(slope, axis=0)
        sk = jnp.concatenate(sk, axis=0)
        sc = jnp.where(ok_c, lax.dot_general(qs, kc2, _NT, preferred_element_type=F32) - slope * rel_c, neg)
        sp = jnp.where(ok_p, lax.dot_general(qs, kp2, _NT, preferred_element_type=F32) - slope * rel_p, neg)
        sm = jnp.where(ok_m, lax.dot_general(qs, km2, _NT, preferred_element_type=F32), neg)
        mx = jnp.maximum(jnp.maximum(jnp.max(sc, axis=1, keepdims=True), jnp.max(sp, axis=1, keepdims=True)),
                         jnp.maximum(jnp.max(sm, axis=1, keepdims=True), sk))
        mx = lax.stop_gradient(mx)
        ec, ep, em, es = jnp.exp(sc - mx), jnp.exp(sp - mx), jnp.exp(sm - mx), jnp.exp(sk - mx)
        den = (es + jnp.sum(ec, axis=1, keepdims=True) + jnp.sum(ep, axis=1, keepdims=True)
               + jnp.sum(em, axis=1, keepdims=True))
        inv = 1.0 / den
        o = (jnp.dot((ec * inv).astype(BF16), vc2, preferred_element_type=F32)
             + jnp.dot((ep * inv).astype(BF16), vp2, preferred_element_type=F32)
             + jnp.dot((em * inv).astype(BF16), vm2, preferred_element_type=F32))
        for pair in range(ATT_GROUP // 2):
            r0 = 2 * pair * BLOCK
            outs.append(jnp.where(first, o[r0:r0 + BLOCK], o[r0 + BLOCK:r0 + 2 * BLOCK]))
    return jnp.concatenate(outs, axis=1) * _silu(z)


def _attn_specs(nb, steps_clamped):
    def blk(t):
        return jnp.minimum(t, nb - 1) if steps_clamped else t

    wide = lambda col: pl.BlockSpec((BLOCK, D_MODEL), lambda t: (blk(t), col))
    cur = lambda col: pl.BlockSpec((BLOCK, KV_WIDTH), lambda t: (blk(t), col))
    prev = lambda col: pl.BlockSpec((BLOCK, KV_WIDTH), lambda t: (jnp.maximum(blk(t) - 1, 0), col))
    meta = lambda col: pl.BlockSpec((N_META, KV_WIDTH), lambda t: (META_ROW_BLOCK, col))
    sinks = pl.BlockSpec((ATT_Q_HEADS, 1, 1), lambda t: (0, 0, 0))
    return [wide(Q_BLOCK), wide(Z_ATT_BLOCK), prev(K_BLOCK), cur(K_BLOCK), prev(V_BLOCK), cur(V_BLOCK),
            meta(K_BLOCK), meta(V_BLOCK), sinks]


def attn_fwd(proj, sinks):
    nb = proj.shape[0] // BLOCK

    def body(q_ref, z_ref, kp_ref, kc_ref, vp_ref, vc_ref, km_ref, vm_ref, sk_ref, o_ref):
        o_ref[...] = _attn_rows(q_ref[...], z_ref[...], kp_ref[...], kc_ref[...], vp_ref[...], vc_ref[...],
                                km_ref[...], vm_ref[...], tuple(sk_ref[h] for h in range(ATT_Q_HEADS)),
                                pl.program_id(0)).astype(BF16)

    return _call(body, name="attn_fwd", grid=(nb,), in_specs=_attn_specs(nb, False), out_specs=_row_spec(D_MODEL),
                 out_shape=jax.ShapeDtypeStruct((nb * BLOCK, D_MODEL), BF16), sem=("parallel",))(*([proj] * 8), sinks)


def attn_bwd(da, proj, sinks):
    nb = proj.shape[0] // BLOCK
    last = nb - 1
    wide = pl.BlockSpec((BLOCK, D_MODEL), lambda t: (jnp.minimum(t, last), 0))
    done = pl.BlockSpec((BLOCK, KV_WIDTH), lambda t: (jnp.maximum(t - 1, 0), 0))
    meta = _full((N_META, KV_WIDTH))
    par = _full((ATT_Q_HEADS, 1, 1))

    def body(da_ref, q_ref, z_ref, kp_ref, kc_ref, vp_ref, vc_ref, km_ref, vm_ref, sk_ref,
             dq_ref, dz_ref, dk_ref, dv_ref, dkm_ref, dvm_ref, dsk_ref, ck_ref, cv_ref):
        t = pl.program_id(0)

        @pl.when(t == 0)
        def _():
            ck_ref[...] = jnp.zeros_like(ck_ref)
            cv_ref[...] = jnp.zeros_like(cv_ref)
            dkm_ref[...] = jnp.zeros_like(dkm_ref)
            dvm_ref[...] = jnp.zeros_like(dvm_ref)
            dsk_ref[...] = jnp.zeros_like(dsk_ref)

        @pl.when(t < nb)
        def _():
            def f(q, z, kp, kc, vp, vc, km, vm, sk):
                return _attn_rows(q, z, kp, kc, vp, vc, km, vm, sk, t)

            _, vjp = jax.vjp(f, q_ref[...], z_ref[...], kp_ref[...], kc_ref[...], vp_ref[...], vc_ref[...],
                             km_ref[...], vm_ref[...], tuple(sk_ref[h] for h in range(ATT_Q_HEADS)))
            dq, dz, dkp, dkc, dvp, dvc, dkm, dvm, dsk = vjp(da_ref[...])
            dq_ref[...] = dq.astype(BF16)
            dz_ref[...] = dz.astype(BF16)
            for h in range(ATT_Q_HEADS):
                dsk_ref[h] += dsk[h]
            dk_ref[...] = ck_ref[...] + dkp
            dv_ref[...] = cv_ref[...] + dvp
            ck_ref[...] = dkc
            cv_ref[...] = dvc
            dkm_ref[...] += dkm
            dvm_ref[...] += dvm

        @pl.when(t == nb)
        def _():
            dk_ref[...] = ck_ref[...]
            dv_ref[...] = cv_ref[...]

    rows = nb * BLOCK
    return _call(body, name="attn_bwd", grid=(nb + 1,), in_specs=[wide] + _attn_specs(nb, True),
                 out_specs=[wide, wide, done, done, meta, meta, par],
                 out_shape=[jax.ShapeDtypeStruct((rows, D_MODEL), BF16), jax.ShapeDtypeStruct((rows, D_MODEL), BF16),
                            jax.ShapeDtypeStruct((rows, KV_WIDTH), F32), jax.ShapeDtypeStruct((rows, KV_WIDTH), F32),
                            jax.ShapeDtypeStruct((N_META, KV_WIDTH), F32), jax.ShapeDtypeStruct((N_META, KV_WIDTH), F32),
                            jax.ShapeDtypeStruct(sinks.shape, F32)],
                 scratch=[pltpu.VMEM((BLOCK, KV_WIDTH), F32), pltpu.VMEM((BLOCK, KV_WIDTH), F32)],
                 sem=("arbitrary",))(da, *([proj] * 8), sinks)


XBC_BLOCK0 = SEG["xbc"][2] // D_MODEL
CONV_COL_BLOCKS = CONV_DIM // D_MODEL
DT_TILE = SEG["dt"][2] // LANES


HALO = 8


def _conv_rows(length):
    return 544 if length % 544 == 0 else BLOCK


def _shift_rows(cur, before, j):
    if j == 0:
        return cur
    n = cur.shape[0]
    row = lax.broadcasted_iota(jnp.int32, cur.shape, 0)
    head = pltpu.roll(before, j, 0)
    if n > HALO:
        head = jnp.concatenate([head, jnp.zeros((n - HALO, cur.shape[1]), cur.dtype)], axis=0)
    return jnp.where(row >= j, pltpu.roll(cur, j, 0), head)


def _conv_pre(cur, before, w_ref, b_ref):
    pre = b_ref[...] + w_ref[CONV_WIDTH - 1:CONV_WIDTH, :] * cur
    for k in range(CONV_WIDTH - 1):
        pre = pre + w_ref[k:k + 1, :] * _shift_rows(cur, before, CONV_WIDTH - 1 - k)
    return pre


def _conv_specs(steps, rows, col0=0):
    first = XBC_BLOCK0 + col0
    halos = rows // HALO
    cur = pl.BlockSpec((rows, D_MODEL), lambda j, i: (i, first + j))
    before = pl.BlockSpec((HALO, D_MODEL), lambda j, i: (jnp.maximum(i * halos - 1, 0), first + j))
    after = pl.BlockSpec((HALO, D_MODEL), lambda j, i: (jnp.minimum(i + 1, steps - 1) * halos, first + j))
    return cur, before, after


def _valid_rows(i, rows):
    row = lax.broadcasted_iota(jnp.int32, (rows, D_MODEL), 0)
    return jnp.maximum((row >= PAD_ROWS).astype(F32), jnp.where(i > 0, 1.0, 0.0))


def conv_fwd(proj, conv_w, conv_b):
    rows = _conv_rows(proj.shape[0])
    steps = proj.shape[0] // rows
    cur, before, _ = _conv_specs(steps, rows)

    def body(c_ref, p_ref, w_ref, b_ref, o_ref):
        i = pl.program_id(1)
        pre = _conv_pre(c_ref[...], p_ref[...] * jnp.where(i > 0, 1.0, 0.0), w_ref, b_ref)
        o_ref[...] = _silu(pre) * _valid_rows(i, rows)

    return _call(body, name="conv_fwd", grid=(CONV_COL_BLOCKS, steps),
                 in_specs=[cur, before, pl.BlockSpec((CONV_WIDTH, D_MODEL), lambda j, i: (0, j)),
                           pl.BlockSpec((1, D_MODEL), lambda j, i: (0, j))],
                 out_specs=pl.BlockSpec((rows, D_MODEL), lambda j, i: (i, j)),
                 out_shape=jax.ShapeDtypeStruct((proj.shape[0], CONV_DIM), F32),
                 sem=("parallel", "parallel"))(proj, proj, conv_w, conv_b)


def conv_bwd(name, dparts, col0, proj, conv_w, conv_b):
    rows = _conv_rows(proj.shape[0])
    steps = proj.shape[0] // rows
    last = steps - 1
    ncol = sum(d.shape[1] for d in dparts) // D_MODEL
    np_ = len(dparts)
    cur, before, after = _conv_specs(steps, rows, col0)
    dcur = [pl.BlockSpec((rows, d.shape[1] // ncol), lambda j, i: (i, j)) for d in dparts]
    dafter = [pl.BlockSpec((HALO, d.shape[1] // ncol), lambda j, i: (jnp.minimum(i + 1, last) * (rows // HALO), j))
              for d in dparts]
    out_cur = pl.BlockSpec((rows, D_MODEL), lambda j, i: (i, j))
    wspec = pl.BlockSpec((CONV_WIDTH, D_MODEL), lambda j, i: (0, col0 + j))
    bspec = pl.BlockSpec((1, D_MODEL), lambda j, i: (0, col0 + j))
    wout = pl.BlockSpec((CONV_WIDTH, D_MODEL), lambda j, i: (0, j))
    bout = pl.BlockSpec((1, D_MODEL), lambda j, i: (0, j))

    def body(*refs):
        dc_refs, da_refs = refs[:np_], refs[np_:2 * np_]
        c_ref, p_ref, a_ref, w_ref, b_ref, du_ref, dw_ref, db_ref = refs[2 * np_:]
        i = pl.program_id(1)
        row = lax.broadcasted_iota(jnp.int32, (rows, D_MODEL), 0)
        curv = c_ref[...]
        beforev = p_ref[...] * jnp.where(i > 0, 1.0, 0.0)
        side_by_side = lambda rs: rs[0][...] if np_ == 1 else jnp.concatenate([r[...] for r in rs], axis=1)

        def dpre_of(pre, d):
            s = _sigmoid(pre)
            return d * (s * (1.0 + pre * (1.0 - s)))

        dp_c = dpre_of(_conv_pre(curv, beforev, w_ref, b_ref), side_by_side(dc_refs) * _valid_rows(i, rows))
        dp_a = dpre_of(_conv_pre(a_ref[...], curv[rows - HALO:], w_ref, b_ref),
                       side_by_side(da_refs) * jnp.where(i < last, 1.0, 0.0))
        du = w_ref[CONV_WIDTH - 1:CONV_WIDTH, :] * dp_c
        for j in range(1, CONV_WIDTH):
            tail = jnp.concatenate([jnp.zeros((rows - HALO, D_MODEL), F32), pltpu.roll(dp_a, HALO - j, 0)], axis=0)
            up = jnp.where(row < rows - j, pltpu.roll(dp_c, rows - j, 0), tail)
            du = du + w_ref[CONV_WIDTH - 1 - j:CONV_WIDTH - j, :] * up
        du_ref[...] = du.astype(BF16)

        @pl.when(i == 0)
        def _():
            dw_ref[...] = jnp.zeros_like(dw_ref)
            db_ref[...] = jnp.zeros_like(db_ref)

        for k in range(CONV_WIDTH):
            dw_ref[k:k + 1, :] += jnp.sum(dp_c * _shift_rows(curv, beforev, CONV_WIDTH - 1 - k), axis=0, keepdims=True)
        db_ref[...] += jnp.sum(dp_c, axis=0, keepdims=True)

    width = ncol * D_MODEL
    return _call(body, name=name, grid=(ncol, steps),
                 in_specs=dcur + dafter + [cur, before, after, wspec, bspec], out_specs=[out_cur, wout, bout],
                 out_shape=[jax.ShapeDtypeStruct((proj.shape[0], width), BF16),
                            jax.ShapeDtypeStruct((CONV_WIDTH, width), F32), jax.ShapeDtypeStruct((1, width), F32)],
                 sem=("parallel", "arbitrary"))(*dparts, *dparts, proj, proj, proj, conv_w, conv_b)


def _head_expand():
    e = np.zeros((LANES, SSM_INNER), np.float32)
    for h in range(SSM_HEADS):
        e[h, h * HEAD_DIM:(h + 1) * HEAD_DIM] = 1.0
    return jnp.asarray(e, dtype=BF16)


def _softplus(x):
    return jnp.maximum(x, 0.0) + jnp.log(1.0 + jnp.exp(-jnp.abs(x)))


def _bf16_parts(x):
    hi = x.astype(BF16)
    rest = x - hi.astype(F32)
    mid = rest.astype(BF16)
    return hi, mid, (rest - mid.astype(F32)).astype(BF16)


@jax.custom_vjp
def _times_01(x, m):
    return sum(jnp.dot(p, m, preferred_element_type=F32) for p in _bf16_parts(x))


def _times_01_bwd(m, g):
    return sum(lax.dot_general(p, m, _NT, preferred_element_type=F32) for p in _bf16_parts(g)), jnp.zeros_like(m)


_times_01.defvjp(lambda x, m: (_times_01(x, m), m), _times_01_bwd)


def _causal_ones():
    l = lax.broadcasted_iota(jnp.int32, (BLOCK, BLOCK), 0)
    s = lax.broadcasted_iota(jnp.int32, (BLOCK, BLOCK), 1)
    return (l >= s).astype(BF16)


@jax.custom_vjp
def _cumsum_rows(a):
    return sum(jnp.dot(_causal_ones(), p, preferred_element_type=F32) for p in _bf16_parts(a))


def _cumsum_rows_bwd(_, g):
    tn = (((0,), (0,)), ((), ()))
    return (sum(lax.dot_general(_causal_ones(), p, tn, preferred_element_type=F32) for p in _bf16_parts(g)),)


_cumsum_rows.defvjp(lambda a: (_cumsum_rows(a), None), _cumsum_rows_bwd)


def _ssd_group(xs, dt_tile, expand, bias, alog, dsk, bg, cg, state):
    l = lax.broadcasted_iota(jnp.int32, (BLOCK, BLOCK), 0)
    s = lax.broadcasted_iota(jnp.int32, (BLOCK, BLOCK), 1)
    causal = l >= s
    first_head = s < HEAD_DIM
    dt = _softplus(dt_tile + bias)
    a = dt * (-jnp.exp(alog))
    one_row = lambda v: jnp.broadcast_to(v, (HALO, LANES))
    per_lane = _times_01(jnp.concatenate([dt, _cumsum_rows(a), one_row(jnp.sum(a, axis=0, keepdims=True)),
                                          one_row(dsk)], axis=0), expand)
    dtx, cs = per_lane[0:BLOCK], per_lane[BLOCK:2 * BLOCK]
    tot, dsk = per_lane[2 * BLOCK:2 * BLOCK + 1], per_lane[2 * BLOCK + HALO:2 * BLOCK + HALO + 1]
    bb, cb16 = bg.astype(BF16), cg.astype(BF16)
    cb = lax.dot_general(cb16, bb, _NT, preferred_element_type=F32)
    xr = xs * dtx
    y_diag = []
    for p in range(GROUP_W // LANES):
        lanes = slice(p * LANES, (p + 1) * LANES)
        c_pair = cs[:, lanes]
        c_swap = _swap_halves(c_pair)
        m = []
        for c_head in (jnp.where(first_head, c_pair, c_swap), jnp.where(first_head, c_swap, c_pair)):
            m.append(cb * jnp.exp(jnp.where(causal, c_head - c_head.T, -jnp.inf)))
        x_pair = xr[:, lanes]
        x_diag = jnp.concatenate([jnp.where(first_head, x_pair, 0.0), jnp.where(first_head, 0.0, x_pair)], axis=0)
        y_diag.append(jnp.dot(jnp.concatenate(m, axis=1).astype(BF16), x_diag.astype(BF16),
                              preferred_element_type=F32))
    st = lax.dot_general(bb, (xr * jnp.exp(tot - cs)).astype(BF16), (((0,), (0,)), ((), ())),
                         preferred_element_type=F32)
    new_state = state * jnp.exp(tot) + st
    y_off = jnp.dot(cb16, state.astype(BF16), preferred_element_type=F32) * jnp.exp(cs)
    return jnp.concatenate(y_diag, axis=1) + y_off + dsk * xs, new_state


BC_WIDTH = SSM_GROUPS * SSM_STATE


def _ssd_specs(chunk):
    xs = pl.BlockSpec((BLOCK, SSM_INNER), lambda c: (chunk(c), 0))
    dt = pl.BlockSpec((BLOCK, LANES), lambda c: (chunk(c), DT_TILE))
    expand = _full((LANES, SSM_INNER))
    b = pl.BlockSpec((BLOCK, BC_WIDTH), lambda c: (chunk(c), SSM_INNER // BC_WIDTH))
    cc = pl.BlockSpec((BLOCK, BC_WIDTH), lambda c: (chunk(c), SSM_INNER // BC_WIDTH + 1))
    par = _full((1, LANES))
    state = pl.BlockSpec((None, SSM_STATE, SSM_INNER), lambda c: (chunk(c), 0, 0))
    return xs, dt, expand, b, cc, par, state


def _group_lanes(g):
    return slice(g * GROUP_W, (g + 1) * GROUP_W), slice(g * SSM_STATE, (g + 1) * SSM_STATE)


def ssd_fwd(xbc, proj, expand, bias, alog, dsk):
    nb = xbc.shape[0] // BLOCK
    xs, dt, ex, b, cc, par, state = _ssd_specs(lambda c: c)

    def body(x_ref, dt_ref, e_ref, bi_ref, al_ref, dk_ref, b_ref, c_ref, y_ref, sp_ref, st_ref):
        @pl.when(pl.program_id(0) == 0)
        def _():
            st_ref[...] = jnp.zeros_like(st_ref)

        for g in range(SSM_GROUPS):
            wide, tile = _group_lanes(g)
            entering = st_ref[:, wide]
            sp_ref[:, wide] = entering
            y_ref[:, wide], st_ref[:, wide] = _ssd_group(
                x_ref[:, wide], dt_ref[...], e_ref[:, wide], bi_ref[...], al_ref[...], dk_ref[...], b_ref[:, tile],
                c_ref[:, tile], entering)

    return _call(body, name="ssd_fwd", grid=(nb,), in_specs=[xs, dt, ex, par, par, par, b, cc],
                 out_specs=[xs, state],
                 out_shape=[jax.ShapeDtypeStruct((nb * BLOCK, SSM_INNER), F32),
                            jax.ShapeDtypeStruct((nb, SSM_STATE, SSM_INNER), F32)],
                 scratch=[pltpu.VMEM((SSM_STATE, SSM_INNER), F32)],
                 sem=("arbitrary",))(xbc, proj, expand, bias, alog, dsk, xbc, xbc)


def ssd_bwd(dy, xbc, proj, expand, bias, alog, dsk, states, comm):
    nb = xbc.shape[0] // BLOCK
    last = nb - 1
    xs, dt, ex, b, cc, par, state = _ssd_specs(lambda c: last - c)
    tile = pl.BlockSpec((BLOCK, LANES), lambda c: (last - c, 0))
    nspec = pl.BlockSpec((BLOCK, BC_WIDTH), lambda c: (last - c, 0))

    def body(dy_ref, x_ref, dt_ref, e_ref, bi_ref, al_ref, dk_ref, b_ref, c_ref, sp_ref,
             dx_ref, ddt_ref, db_ref, dc_ref, dbi_ref, dal_ref, ddk_ref, ds_ref):
        @pl.when(pl.program_id(0) == 0)
        def _():
            ds_ref[...] = jnp.zeros_like(ds_ref)
            dbi_ref[...] = jnp.zeros_like(dbi_ref)
            dal_ref[...] = jnp.zeros_like(dal_ref)
            ddk_ref[...] = jnp.zeros_like(ddk_ref)

        ddt, dbi, dal, ddk = 0.0, 0.0, 0.0, 0.0
        for g in range(SSM_GROUPS):
            wide, tile_lanes = _group_lanes(g)
            expand_rows = e_ref[:, wide]

            def f(xs, dt_tile, bias, alog, dsk, bg, cg, state):
                return _ssd_group(xs, dt_tile, expand_rows, bias, alog, dsk, bg, cg, state)

            _, vjp = jax.vjp(f, x_ref[:, wide], dt_ref[...], bi_ref[...], al_ref[...], dk_ref[...], b_ref[:, tile_lanes],
                             c_ref[:, tile_lanes], sp_ref[:, wide])
            (dx_ref[:, wide], ddt_g, dbi_g, dal_g, ddk_g, db_ref[:, tile_lanes], dc_ref[:, tile_lanes],
             ds_ref[:, wide]) = vjp((dy_ref[:, wide], ds_ref[:, wide]))
            ddt, dbi, dal, ddk = ddt + ddt_g, dbi + dbi_g, dal + dal_g, ddk + ddk_g
        ddt_ref[...] = ddt.astype(BF16)
        dbi_ref[...] += dbi
        dal_ref[...] += dal
        ddk_ref[...] += ddk

    def at():
        c = pl.program_id(0)
        return c == 0, c == 0, c == last

    par_shape = jax.ShapeDtypeStruct((1, LANES), F32)
    return _call_with_comm(
        body, comm, at, (dy, xbc, proj, expand, bias, alog, dsk, xbc, xbc, states), name="ssd_bwd",
        grid=(nb,), in_specs=[xs, xs, dt, ex, par, par, par, b, cc, state],
        out_specs=[xs, tile, nspec, nspec, par, par, par],
        out_shape=[jax.ShapeDtypeStruct((nb * BLOCK, SSM_INNER), F32), jax.ShapeDtypeStruct((nb * BLOCK, LANES), BF16),
                   jax.ShapeDtypeStruct((nb * BLOCK, BC_WIDTH), F32), jax.ShapeDtypeStruct((nb * BLOCK, BC_WIDTH), F32),
                   par_shape, par_shape, par_shape],
        scratch=[pltpu.VMEM((SSM_STATE, SSM_INNER), F32)])


SLAB_ROWS = 24
SLAB_META_ROW = 8


SLAB_LOSS_ROW = 7


def pack_small(dcw, dcb, dgpre, dgpost, dbias, dalog, ddsk, dsinks, dgn, dmeta, loss_tile):
    def body(cw, cb, gpre, gpost, dtb, al, dk, sk, gn, meta, loss, o_ref):
        o_ref[...] = jnp.zeros_like(o_ref)
        o_ref[SLAB_LOSS_ROW:SLAB_LOSS_ROW + 1, 0:LANES] = loss[0:1, :]
        o_ref[0:CONV_WIDTH, :] = cw[...]
        o_ref[4:5, :] = cb[...]
        o_ref[5:6, 0:1024] = gpre[...]
        o_ref[5:6, 1024:2048] = gpost[...]
        o_ref[5:6, 2048:2176] = dtb[...]
        o_ref[5:6, 2176:2304] = al[...]
        o_ref[5:6, 2304:2432] = dk[...]
        o_ref[5:6, 2432:2560] = sk[...]
        o_ref[6:7, 0:SSM_INNER] = gn[...]
        o_ref[SLAB_META_ROW:SLAB_META_ROW + N_META, 0:D_MODEL] = meta[...]

    args = (dcw, dcb, dgpre, dgpost, dbias, dalog, ddsk, dsinks, dgn, dmeta, loss_tile)
    return _call(body, name="pack_small", in_specs=[_full(a.shape) for a in args],
                 out_specs=_full((SLAB_ROWS, CONV_DIM)), out_shape=jax.ShapeDtypeStruct((SLAB_ROWS, CONV_DIM), F32))(*args)


def _lane_tile(v):
    return jnp.pad(v, ((0, 0), (0, LANES - v.shape[1])))


def kernel(x, meta_tokens, g_pre, w_in, conv_w, conv_b, dt_bias, a_log, d_skip, attn_sinks, g_ssm_norm, w_out_att, w_out_ssm, w_out, g_post, loss_target, m_meta_tokens, m_g_pre, m_w_in, m_conv_w, m_conv_b, m_dt_bias, m_a_log, m_d_skip, m_attn_sinks, m_g_ssm_norm, m_w_out_att, m_w_out_ssm, m_w_out, m_g_post, v_meta_tokens, v_g_pre, v_w_in, v_conv_w, v_conv_b, v_dt_bias, v_a_log, v_d_skip, v_attn_sinks, v_g_ssm_norm, v_w_out_att, v_w_out_ssm, v_w_out, v_g_post):
    chip = _chip_index()

    conv_w_rows = jnp.pad(conv_w[0], ((0, 2 * 8 - CONV_WIDTH), (0, 0)))
    w_in_t, m_w_in_t, v_w_in_t = w_in[0].T, m_w_in[0].T, v_w_in[0].T
    gathered_w_in, g_conv_w, g_meta = run_comm("gather_w_in",
                                               TwoLevelGather([pack_w_in(w_in_t), conv_w_rows, meta_tokens]))
    w_all_t = unpack_w_in(gathered_w_in)
    cw_full = g_conv_w[:, :CONV_WIDTH].transpose(1, 0, 2).reshape(CONV_WIDTH, CONV_DIM)
    meta_full = g_meta.transpose(1, 0, 2).reshape(N_META, D_MODEL)
    behind_w_in = 0.0 * g_meta[0, 0, 0]
    w_out_flight, w_out_started = chip_exchange_start(
        "gather_w_out_start", [(w[0] + behind_w_in).astype(BF16) for w in (w_out_att, w_out_ssm, w_out)], False)

    h, u = prep(x, meta_full, g_pre)
    proj = project("in_proj", u, w_all_t, w_out_started)

    sinks3 = attn_sinks.reshape(ATT_Q_HEADS, 1, 1)
    a_att = attn_fwd(proj, sinks3)

    xbc = conv_fwd(proj, cw_full, conv_b)
    expand = _head_expand()
    head_pars = (_lane_tile(dt_bias), _lane_tile(a_log), _lane_tile(d_skip))
    y_ssd, states = ssd_fwd(xbc, proj, expand, *head_pars)
    woa, wos, wo = [g.reshape(-1, D_MODEL) for g in chip_exchange_wait("gather_w_out_wait", w_out_flight, False, y_ssd)]

    (yn, merged, dout, dy_att, dy_ssm, da_att, dy_ssd, dz_ssm, dga, dgs, dres, loss_tile, dg_post, dgn) = tail(
        y_ssd, proj, a_att, x, loss_target, woa, wos, wo, g_ssm_norm, g_post)

    dwo = mm_tn("out_proj_dw", merged, dout)
    dwoa = mm_tn("att_out_dw", a_att, dy_att)
    dwos = mm_tn("ssm_out_dw", yn, dy_ssm)
    dq, dz_att, dk, dv, dkmeta, dvmeta, dsinks3 = attn_bwd(da_att, proj, sinks3)
    dk = dk.at[PAD_ROWS:BLOCK].add(dkmeta).astype(BF16)
    dv = dv.at[PAD_ROWS:BLOCK].add(dvmeta).astype(BF16)

    def pieces(g):
        return g.reshape(4, 2, g.shape[0] // 8, g.shape[1])

    def to_owner(g):
        return (lambda ref, dev: ref.at[_chip_of(dev), dev[2]], (g.shape[0] // 8, g.shape[1]))

    (dxs, ddt_tile, dbg, dcg, dbias, dalog, ddsk), sent_w_out = ssd_bwd(
        dy_ssd, xbc, proj, expand, *head_pars, states,
        DirectExchange([pieces(dwoa), pieces(dwos), pieces(dwo)], [to_owner(dwoa), to_owner(dwos), to_owner(dwo)],
                       ALL_MASKS, "dev", 8))
    dxs_raw, dcw_xs, dcb_xs = conv_bwd("conv_bwd_x", [dxs], 0, proj, cw_full, conv_b)
    dbc_raw, dcw_bc, dcb_bc = conv_bwd("conv_bwd_bc", [dbg, dcg], SSM_INNER // D_MODEL, proj, cw_full, conv_b)
    dcw = jnp.concatenate([dcw_xs, dcw_bc], axis=1)
    dcb = jnp.concatenate([dcb_xs, dcb_bc], axis=1)

    narrow = jnp.concatenate([dk, dv, ddt_tile, jnp.zeros((dk.shape[0], N_ACT - N_ALIGNED), BF16)], axis=1)
    dproj = [dz_ssm, dxs_raw, dbc_raw, dq, dz_att, dga, dgs, narrow]
    dw_all_t = weight_grad_t("in_proj_dw", dproj, u)

    half_rows = PACK_W // 2
    partial = pack_grad_w_in(dw_all_t).reshape(4, 2, half_rows, D_MODEL)
    from_sibling, = run_comm("pair_grads", DirectExchange(
        [partial], [(lambda ref, dev: ref.at[pl.ds(0, 4), dev[2]], (4, half_rows, D_MODEL))], SIBLING_MASK, "core", 2,
        keep_own=False))
    chip_sum = sum_pair(partial, from_sibling)
    grads_flight, started = chip_exchange_start("reduce_w_in_start", [chip_sum], True)
    du = project_back("in_proj_dx", dproj, w_all_t, started)
    grad_x, dmeta, dg_pre = prep_bwd(h, du, dres, g_pre)
    sent_w_in, = chip_exchange_wait("reduce_w_in_wait", grads_flight, True, dg_pre)

    slab = pack_small(dcw, dcb, dg_pre, dg_post, dbias, dalog, ddsk, _lane_tile(dsinks3.reshape(1, ATT_Q_HEADS)), dgn,
                      dmeta, loss_tile)
    halves = [sum_slots("sum_" + nm, r)
              for nm, r in zip(("w_in", "w_out_att", "w_out_ssm", "w_out"), [sent_w_in] + list(sent_w_out))]
    shared = run_comm("share_grads", Both(DirectExchange(halves, [None] * 4, SIBLING_MASK, "core", 2),
                                          DirectExchange([slab], [None], ALL_MASKS, "dev", 8)))
    g_w_in_packed, g_woa, g_wos, g_wo = [f.reshape(2 * f.shape[1], f.shape[2]) for f in shared[:4]]
    small = sum_slots("sum_small", shared[4])
    loss = small[SLAB_LOSS_ROW, 0]

    g_w_in, d_w_in, nm_w_in, nv_w_in = [a.T for a in adamw_w_in(g_w_in_packed, w_in_t, m_w_in_t, v_w_in_t)]
    d_woa, nm_woa, nv_woa = adamw_rows("adamw_w_out_att", g_woa, w_out_att[0], m_w_out_att[0], v_w_out_att[0])
    d_wos, nm_wos, nv_wos = adamw_rows("adamw_w_out_ssm", g_wos, w_out_ssm[0], m_w_out_ssm[0], v_w_out_ssm[0])
    d_wo, nm_wo, nv_wo = adamw_rows("adamw_w_out", g_wo, w_out[0], m_w_out[0], v_w_out[0])

    cw_cols = CONV_DIM // 4
    meta_cols = D_MODEL // 4
    g_small = {
        "meta_tokens": lax.dynamic_slice(small, (SLAB_META_ROW, chip * meta_cols), (N_META, meta_cols)),
        "g_pre": small[5:6, 0:1024],
        "conv_w": lax.dynamic_slice(small, (0, chip * cw_cols), (CONV_WIDTH, cw_cols)),
        "conv_b": small[4:5, :],
        "dt_bias": small[5:6, 2048:2048 + SSM_HEADS],
        "a_log": small[5:6, 2176:2176 + SSM_HEADS],
        "d_skip": small[5:6, 2304:2304 + SSM_HEADS],
        "attn_sinks": small[5:6, 2432:2432 + ATT_Q_HEADS],
        "g_ssm_norm": small[6:7, 0:SSM_INNER],
        "g_post": small[5:6, 1024:2048],
    }
    names = list(g_small)
    w_small = dict(meta_tokens=meta_tokens, g_pre=g_pre, conv_w=conv_w[0], conv_b=conv_b, dt_bias=dt_bias, a_log=a_log,
                   d_skip=d_skip, attn_sinks=attn_sinks, g_ssm_norm=g_ssm_norm, g_post=g_post)
    m_small = dict(meta_tokens=m_meta_tokens, g_pre=m_g_pre, conv_w=m_conv_w[0], conv_b=m_conv_b, dt_bias=m_dt_bias,
                   a_log=m_a_log, d_skip=m_d_skip, attn_sinks=m_attn_sinks, g_ssm_norm=m_g_ssm_norm, g_post=m_g_post)
    v_small = dict(meta_tokens=v_meta_tokens, g_pre=v_g_pre, conv_w=v_conv_w[0], conv_b=v_conv_b, dt_bias=v_dt_bias,
                   a_log=v_a_log, d_skip=v_d_skip, attn_sinks=v_attn_sinks, g_ssm_norm=v_g_ssm_norm, g_post=v_g_post)
    upd = dict(zip(names, adamw_small([g_small[k] for k in names], [w_small[k] for k in names],
                                      [m_small[k] for k in names], [v_small[k] for k in names])))

    lead = {"conv_w"}

    def shaped(name, a):
        return a[None] if name in lead else a

    grads = dict(g_small, w_in=g_w_in, w_out_att=g_woa, w_out_ssm=g_wos, w_out=g_wo)
    deltas = dict({k: upd[k][0] for k in names}, w_in=d_w_in, w_out_att=d_woa, w_out_ssm=d_wos, w_out=d_wo)
    new_m = dict({k: upd[k][1] for k in names}, w_in=nm_w_in, w_out_att=nm_woa, w_out_ssm=nm_wos, w_out=nm_wo)
    new_v = dict({k: upd[k][2] for k in names}, w_in=nv_w_in, w_out_att=nv_woa, w_out_ssm=nv_wos, w_out=nv_wo)
    lead |= {"w_in", "w_out_att", "w_out_ssm", "w_out"}
    order = ["meta_tokens", "g_pre", "w_in", "conv_w", "conv_b", "dt_bias", "a_log", "d_skip", "attn_sinks",
             "g_ssm_norm", "w_out_att", "w_out_ssm", "w_out", "g_post"]
    outs = [loss, grad_x]
    for group in (grads, deltas, new_m, new_v):
        outs += [shaped(k, group[k]) for k in order]
    return tuple(outs)
```

```python
import functools

import numpy as np
import jax
import jax.numpy as jnp
from jax import lax
from jax.experimental import pallas as pl
from jax.experimental.pallas import tpu as pltpu

F32 = jnp.float32
BF16 = jnp.bfloat16
HI = lax.Precision.HIGHEST

D_MODEL = 1024
N_META = 16
BLOCK = 128
PAD_ROWS = BLOCK - N_META
NORM_EPS = 1e-6
HEAD_DIM = 64
ATT_Q_HEADS = 16
ATT_KV_HEADS = 4
ATT_GROUP = 4
SSM_INNER = 2048
SSM_HEADS = 32
SSM_GROUPS = 4
SSM_HEADS_PER_GROUP = 8
SSM_STATE = 128
CONV_WIDTH = 4
CONV_DIM = 3072
LANES = 128

ADAM_LR = 0.001
ADAM_B1 = 0.9
ADAM_B2 = 0.999
ADAM_EPS = 1e-08
ADAM_WD = 0.01
ADAM_STEP = 10

VMEM_LIMIT = 48 * 1024 * 1024

SHARD_W = 2440
PACK_W = 2560
SHARD_STRIDE = 2432
N_ALIGNED = 9856
N_ACT = 10240
SEG = {
    "q": (0, 1024, 5120), "k": (1024, 256, 9216), "v": (1280, 256, 9472), "z_att": (1536, 1024, 6144),
    "z_ssm": (2560, 2048, 0), "xbc": (4608, 3072, 2048), "dt": (7680, 128, 9728),
    "gate_att": (7808, 1024, 7168), "gate_ssm": (8832, 1024, 8192),
}
DT_STORED_START = 7680
DT_PAD = LANES - SSM_HEADS


def _act_col(aligned_col):
    for a0, w, p0 in SEG.values():
        if a0 <= aligned_col < a0 + w:
            return p0 + aligned_col - a0
    raise ValueError(aligned_col)


def _call(body, *, name, out_shape, in_specs, out_specs, grid=(), scratch=(), sem=None, aliases=None):
    return pl.pallas_call(
        body, out_shape=out_shape, grid=grid, in_specs=in_specs, out_specs=out_specs, scratch_shapes=list(scratch),
        name=name, input_output_aliases=aliases or {},
        compiler_params=pltpu.CompilerParams(dimension_semantics=sem, vmem_limit_bytes=VMEM_LIMIT))


def _full(shape):
    n = len(shape)
    return pl.BlockSpec(shape, lambda *_: (0,) * n)


def _chip_index():
    return lax.axis_index("x") * 2 + lax.axis_index("y")


_sigmoid = jax.nn.sigmoid


def _silu(z):
    return z * _sigmoid(z)


def _rms(x, g):
    return x * lax.rsqrt(jnp.mean(x * x, axis=-1, keepdims=True) + NORM_EPS) * g


def _peer(mask):
    x, y, c = lax.axis_index("x"), lax.axis_index("y"), lax.axis_index("c")
    return ((1 - x) if mask & 4 else x, (1 - y) if mask & 2 else y, (1 - c) if mask & 1 else c)


def _me():
    return lax.axis_index("x"), lax.axis_index("y"), lax.axis_index("c")


def _chip_of(dev):
    return 2 * dev[0] + dev[1]


CHIP_MASKS = (4, 2, 6)
ALL_MASKS = (1, 2, 3, 4, 5, 6, 7)
SIBLING_MASK = (1,)


def _remote(src, dst, send_sem, recv_sem, dev):
    return pltpu.make_async_remote_copy(src_ref=src, dst_ref=dst, send_sem=send_sem, recv_sem=recv_sem,
                                        device_id=dev, device_id_type=pl.DeviceIdType.MESH)


class _StagedCopy:
    def __init__(self, src, stage, dst, load_sem, store_sem):
        self.load = pltpu.make_async_copy(src, stage, load_sem)
        self.store = pltpu.make_async_copy(stage, dst, store_sem)

    def start(self):
        self.load.start()
        self.load.wait()
        self.store.start()

    def wait(self):
        self.store.wait()


class DirectExchange:
    def __init__(self, arrays, pieces, masks, slot_kind, nslots, keep_own=True):
        self.arrays, self.pieces, self.masks, self.slot_kind = list(arrays), list(pieces), masks, slot_kind
        self.keep_own = keep_own
        n, nk = len(arrays), len(masks)
        shapes = [a.shape if p is None else p[1] for a, p in zip(arrays, pieces)]
        self.out_shape = [jax.ShapeDtypeStruct((nslots,) + tuple(s), a.dtype) for s, a in zip(shapes, arrays)]
        self.scratch = [pltpu.SemaphoreType.DMA((n * nk,)), pltpu.SemaphoreType.DMA((n * nk,))]
        if keep_own:
            self.scratch += [pltpu.SemaphoreType.DMA((2 * n,))] + [pltpu.VMEM(s, a.dtype) for s, a in zip(shapes, arrays)]
        self.has_mid = False

    def _copies(self, ins, outs, scratch):
        send_sems, recv_sems = scratch[:2]
        me = _me()
        slot = {"chip": _chip_of(me), "dev": 4 * me[0] + 2 * me[1] + me[2], "core": me[2]}[self.slot_kind]
        nk = len(self.masks)

        def piece(a, dev):
            return ins[a] if self.pieces[a] is None else self.pieces[a][0](ins[a], dev)

        local = []
        if self.keep_own:
            local_sems, stages = scratch[2], scratch[3:]
            local = [_StagedCopy(piece(a, me), stages[a], outs[a].at[slot], local_sems.at[2 * a], local_sems.at[2 * a + 1])
                     for a in range(len(ins))]
        remote = []
        for a in range(len(ins)):
            for ki, mask in enumerate(self.masks):
                dev = _peer(mask)
                remote.append(_remote(piece(a, dev), outs[a].at[slot], send_sems.at[a * nk + ki],
                                      recv_sems.at[a * nk + ki], dev))
        return local, remote

    def start(self, ins, outs, scratch):
        local, remote = self._copies(ins, outs, scratch)
        for cp in remote + local:
            cp.start()

    def finish(self, ins, outs, scratch):
        local, remote = self._copies(ins, outs, scratch)
        for cp in remote + local:
            cp.wait()


SPLIT_ROWS = 16


class TwoLevelGather:
    def __init__(self, arrays):
        self.arrays = list(arrays)
        n = len(arrays)
        self.out_shape = [jax.ShapeDtypeStruct((4,) + a.shape, a.dtype) for a in arrays]
        self.scratch = ([pltpu.SemaphoreType.DMA((4 * n,)), pltpu.SemaphoreType.DMA((4 * n,)),
                         pltpu.SemaphoreType.DMA((3 * n,)), pltpu.SemaphoreType.DMA((3 * n,)),
                         pltpu.SemaphoreType.DMA((2 * n,))] + [pltpu.VMEM(a.shape, a.dtype) for a in arrays])
        self.has_mid = True

    def _copies(self, ins, outs, scratch):
        ici_send, ici_recv, fwd_send, fwd_recv, local_sems = scratch[:5]
        stages = scratch[5:]
        me = _me()
        sibling, in_x, in_y, diagonal = _peer(1), _peer(4), _peer(2), _peer(6)
        plan = []
        for a in range(len(ins)):
            half = ins[a].shape[0] // 2
            first = half // 2 if half % (2 * SPLIT_ROWS) == 0 else half
            mine = pl.ds(me[2] * half, half)
            local = _StagedCopy(ins[a], stages[a], outs[a].at[_chip_of(me)], local_sems.at[2 * a], local_sems.at[2 * a + 1])

            def ici(k, src, dst, dev):
                return _remote(src, dst, ici_send.at[4 * a + k], ici_recv.at[4 * a + k], dev)

            def d2d(k, chip):
                zone = outs[a].at[_chip_of(chip), mine]
                return _remote(zone, zone, fwd_send.at[3 * a + k], fwd_recv.at[3 * a + k], sibling)

            own_zone = outs[a].at[_chip_of(me), mine]
            from_x = outs[a].at[_chip_of(in_x), pl.ds(me[2] * half, first)]
            onward = [ici(2, from_x, from_x, in_y), None]
            if first < half:
                from_y = outs[a].at[_chip_of(in_y), pl.ds(me[2] * half + first, half - first)]
                onward[1] = ici(3, from_y, from_y, in_x)
            plan.append(dict(
                local=local,
                own=[ici(0, ins[a].at[mine], own_zone, in_x), ici(1, ins[a].at[mine], own_zone, in_y)],
                onward=onward, sibling=[d2d(0, in_x), d2d(1, in_y), d2d(2, diagonal)]))
        return plan

    def start(self, ins, outs, scratch):
        for p in self._copies(ins, outs, scratch):
            for cp in p["own"]:
                cp.start()
            p["local"].start()

    def mid(self, ins, outs, scratch):
        plan = self._copies(ins, outs, scratch)
        for p in plan:
            for k in range(2):
                p["own"][k].wait_recv()
                if p["onward"][k] is not None:
                    p["onward"][k].start()
                p["sibling"][k].start()
        for p in plan:
            for cp in p["onward"]:
                if cp is not None:
                    cp.wait_recv()
            p["sibling"][2].start()

    def finish(self, ins, outs, scratch):
        for p in self._copies(ins, outs, scratch):
            for cp in p["sibling"]:
                cp.wait_recv()
            for cp in p["own"] + p["sibling"] + [cp for cp in p["onward"] if cp is not None]:
                cp.wait_send()
            p["local"].wait()


class Both:
    def __init__(self, a, b):
        self.a, self.b = a, b
        self.arrays, self.out_shape = a.arrays + b.arrays, a.out_shape + b.out_shape
        self.scratch = a.scratch + b.scratch
        self.has_mid = False
        assert not (a.has_mid or b.has_mid)

    def _parts(self, ins, outs, sems):
        na, sa = len(self.a.arrays), len(self.a.scratch)
        return (ins[:na], outs[:na], sems[:sa]), (ins[na:], outs[na:], sems[sa:])

    def start(self, ins, outs, sems):
        pa, pb = self._parts(ins, outs, sems)
        self.a.start(*pa)
        self.b.start(*pb)

    def finish(self, ins, outs, sems):
        pa, pb = self._parts(ins, outs, sems)
        self.a.finish(*pa)
        self.b.finish(*pb)


_ANY = pl.BlockSpec(memory_space=pl.ANY)


def run_comm(name, comm):
    n = len(comm.arrays)

    def body(*refs):
        ins, outs, sems = refs[:n], refs[n:2 * n], refs[2 * n:]
        comm.start(ins, outs, sems)
        if comm.has_mid:
            comm.mid(ins, outs, sems)
        comm.finish(ins, outs, sems)

    return pl.pallas_call(body, name=name, out_shape=comm.out_shape, in_specs=[_ANY] * n, out_specs=[_ANY] * n,
                          scratch_shapes=comm.scratch,
                          compiler_params=pltpu.CompilerParams(vmem_limit_bytes=VMEM_LIMIT))(*comm.arrays)


_HBM = pl.BlockSpec(memory_space=pltpu.HBM)
_SEM = pl.BlockSpec(memory_space=pltpu.SEMAPHORE)
_SIDE_EFFECT = pltpu.SideEffectType.DATAFLOW_SIDE_EFFECTING


def _chip_copies(srcs, lands, send_sems, recv_sems, by_target):
    me = _me()
    copies = []
    for a, (src, land) in enumerate(zip(srcs, lands)):
        for ki, mask in enumerate(CHIP_MASKS):
            dev = _peer(mask)
            k = a * len(CHIP_MASKS) + ki
            piece = src.at[_chip_of(dev)] if by_target else src
            copies.append(_remote(piece, land.at[_chip_of(me)], send_sems.at[k], recv_sems.at[k], dev))
    return copies


def chip_exchange_start(name, arrays, by_target):
    n = len(arrays)
    nsem = n * len(CHIP_MASKS)
    piece_shapes = [a.shape[1:] if by_target else a.shape for a in arrays]

    def body(*refs):
        srcs, lands = refs[:n], refs[n:2 * n]
        send_sems, recv_sems = refs[2 * n:2 * n + 2]
        token = refs[4 * n + 2]
        stages, local_sems = refs[4 * n + 3:5 * n + 3], refs[5 * n + 3]
        me = _me()
        for cp in _chip_copies(srcs, lands, send_sems, recv_sems, by_target):
            cp.start()
        for a in range(n):
            own = _StagedCopy(srcs[a].at[_chip_of(me)] if by_target else srcs[a], stages[a], lands[a].at[_chip_of(me)],
                              local_sems.at[2 * a], local_sems.at[2 * a + 1])
            own.start()
            own.wait()
        token[...] = jnp.zeros_like(token)

    lands = [lax.empty((4,) + tuple(s), a.dtype) for s, a in zip(piece_shapes, arrays)]
    hbm = lambda a: pltpu.HBM(a.shape, a.dtype)
    res = pl.pallas_call(
        body, name=name,
        out_shape=(pltpu.SemaphoreType.DMA((nsem,)), pltpu.SemaphoreType.DMA((nsem,)), *[hbm(a) for a in arrays],
                   *[hbm(l) for l in lands], jax.ShapeDtypeStruct((8, LANES), F32)),
        in_specs=(_HBM,) * (2 * n), out_specs=(_SEM, _SEM) + (_HBM,) * (2 * n) + (pl.BlockSpec(memory_space=pltpu.VMEM),),
        input_output_aliases={i: i + 2 for i in range(2 * n)},
        scratch_shapes=[pltpu.VMEM(tuple(s), a.dtype) for s, a in zip(piece_shapes, arrays)]
        + [pltpu.SemaphoreType.DMA((2 * n,))],
        compiler_params=pltpu.CompilerParams(has_side_effects=_SIDE_EFFECT, vmem_limit_bytes=VMEM_LIMIT),
    )(*[pltpu.with_memory_space_constraint(a, pltpu.HBM) for a in arrays + lands])
    return res[:-1], res[-1]


def chip_exchange_wait(name, in_flight, by_target, after):
    send_sems, recv_sems, *thru = in_flight
    n = len(thru) // 2

    def body(*refs):
        srcs, lands, (send, recv) = refs[:n], refs[n:2 * n], refs[2 * n:2 * n + 2]
        for cp in _chip_copies(srcs, lands, send, recv, by_target):
            cp.wait_send()
            cp.wait_recv()

    return pl.pallas_call(
        body, name=name, out_shape=tuple(pltpu.HBM(t.shape, t.dtype) for t in thru),
        in_specs=(_HBM,) * (2 * n) + (_SEM, _SEM, pl.BlockSpec(memory_space=pl.ANY)), out_specs=(_HBM,) * (2 * n),
        input_output_aliases={i: i for i in range(2 * n)},
        compiler_params=pltpu.CompilerParams(has_side_effects=_SIDE_EFFECT),
    )(*thru, send_sems, recv_sems, after)[n:]


def _call_with_comm(body, comm, steps, args, *, name, out_shape, in_specs, out_specs, grid, scratch=()):
    ni, no, ns, nc = len(in_specs), len(out_specs), len(scratch), len(comm.arrays)

    def full_body(*refs):
        ins, cins = refs[:ni], refs[ni:ni + nc]
        outs, couts = refs[ni + nc:ni + nc + no], refs[ni + nc + no:ni + 2 * nc + no]
        scr, csems = refs[ni + 2 * nc + no:ni + 2 * nc + no + ns], refs[ni + 2 * nc + no + ns:]
        first, middle, last = steps()
        pl.when(first)(lambda: comm.start(cins, couts, csems))
        if comm.has_mid:
            pl.when(middle)(lambda: comm.mid(cins, couts, csems))
        body(*ins, *outs, *scr)
        pl.when(last)(lambda: comm.finish(cins, couts, csems))

    res = pl.pallas_call(
        full_body, name=name, out_shape=list(out_shape) + comm.out_shape, grid=grid,
        in_specs=list(in_specs) + [_ANY] * nc, out_specs=list(out_specs) + [_ANY] * nc,
        scratch_shapes=list(scratch) + comm.scratch,
        compiler_params=pltpu.CompilerParams(dimension_semantics=("arbitrary",) * len(grid),
                                             vmem_limit_bytes=VMEM_LIMIT))(*args, *comm.arrays)
    return res[:no], res[no:]


def _shard_pieces(chip):
    if chip < 3:
        return [(0, SHARD_W, 8 * chip)]
    behind_dt = DT_STORED_START + SSM_HEADS - 3 * SHARD_W
    return [(0, behind_dt, 24), (behind_dt, SHARD_W - behind_dt, behind_dt + 24 + DT_PAD)]


W_IN_COLS = 256


def pack_w_in(wt):
    def body(w_ref, o_ref, pad_ref):
        chip = _chip_index()
        pad_ref[...] = jnp.zeros_like(pad_ref)
        for cv in range(4):
            @pl.when(chip == cv)
            def _():
                for src, n, dst in _shard_pieces(cv):
                    pad_ref[dst:dst + n, :] = w_ref[src:src + n, :]
        o_ref[...] = pad_ref[...].astype(BF16)

    return _call(body, name="pack_w_in", grid=(D_MODEL // W_IN_COLS,),
                 in_specs=[pl.BlockSpec((SHARD_W, W_IN_COLS), lambda i: (0, i))],
                 out_specs=pl.BlockSpec((PACK_W, W_IN_COLS), lambda i: (0, i)),
                 out_shape=jax.ShapeDtypeStruct((PACK_W, D_MODEL), BF16),
                 scratch=[pltpu.VMEM((PACK_W, W_IN_COLS), F32)], sem=("parallel",))(wt)


def _tile_runs():
    runs, fix = [], []
    for t in range(N_ALIGNED // LANES):
        s = min(t // 19, 3)
        j = t - 19 * s
        p = _act_col(t * LANES)
        if runs and runs[-1][1] == s and runs[-1][0] + runs[-1][3] == p and runs[-1][2] + runs[-1][3] == j * LANES:
            runs[-1][3] += LANES
        else:
            runs.append([p, s, j * LANES, LANES])
        if j == 0 and s > 0:
            fix.append((p, s - 1))
    return runs, fix


def unpack_w_in(bg):
    runs, fix = _tile_runs()

    def body(b_ref, o_ref):
        for p, s, j, w in runs:
            o_ref[p:p + w, :] = b_ref[s, j:j + w, :]
        for p, s in fix:
            o_ref[p:p + LANES, :] = o_ref[p:p + LANES, :] + b_ref[s, SHARD_STRIDE:PACK_W, :]
        o_ref[N_ALIGNED:N_ACT, :] = jnp.zeros((N_ACT - N_ALIGNED, W_IN_COLS), BF16)

    return _call(body, name="unpack_w_in", grid=(D_MODEL // W_IN_COLS,),
                 in_specs=[pl.BlockSpec((4, PACK_W, W_IN_COLS), lambda i: (0, 0, i))],
                 out_specs=pl.BlockSpec((N_ACT, W_IN_COLS), lambda i: (0, i)),
                 out_shape=jax.ShapeDtypeStruct((N_ACT, D_MODEL), BF16), sem=("parallel",))(bg)


def pack_grad_w_in(dwt):
    def body(g_ref, o_ref):
        for s in range(4):
            for j in range(PACK_W // LANES):
                p = _act_col((19 * s + j) * LANES)
                o_ref[s, j * LANES:(j + 1) * LANES, :] = g_ref[p:p + LANES, :]

    return _call(body, name="pack_grad_w_in", grid=(D_MODEL // W_IN_COLS,),
                 in_specs=[pl.BlockSpec((N_ACT, W_IN_COLS), lambda i: (0, i))],
                 out_specs=pl.BlockSpec((4, PACK_W, W_IN_COLS), lambda i: (0, 0, i)),
                 out_shape=jax.ShapeDtypeStruct((4, PACK_W, D_MODEL), BF16), sem=("parallel",))(dwt)


def _adamw(w, g, m, v):
    m = ADAM_B1 * m + (1.0 - ADAM_B1) * g
    v = ADAM_B2 * v + (1.0 - ADAM_B2) * jnp.square(g)
    m_hat = m / (1.0 - ADAM_B1 ** ADAM_STEP)
    v_hat = v / (1.0 - ADAM_B2 ** ADAM_STEP)
    delta = -ADAM_LR * (m_hat / (jnp.sqrt(v_hat) + ADAM_EPS) + ADAM_WD * w)
    return delta, m, v


def adamw_w_in(g_packed, wt, mt, vt):
    cols = LANES

    def body(g_ref, w_ref, m_ref, v_ref, go_ref, d_ref, mo_ref, vo_ref):
        chip = _chip_index()
        for cv in range(4):
            @pl.when(chip == cv)
            def _():
                for dst, n, src in _shard_pieces(cv):
                    go_ref[dst:dst + n, :] = g_ref[src:src + n, :]
        d_ref[...], mo_ref[...], vo_ref[...] = _adamw(w_ref[...], go_ref[...], m_ref[...], v_ref[...])

    spec = pl.BlockSpec((SHARD_W, cols), lambda i: (0, i))
    shp = jax.ShapeDtypeStruct((SHARD_W, D_MODEL), F32)
    return _call(body, name="adamw_w_in", grid=(D_MODEL // cols,),
                 in_specs=[pl.BlockSpec((PACK_W, cols), lambda i: (0, i)), spec, spec, spec],
                 out_specs=[spec] * 4, out_shape=[shp] * 4, sem=("parallel",))(g_packed, wt, mt, vt)


def adamw_rows(name, g, w, m, v):
    r, c = g.shape
    rows = min(r, BLOCK)

    def body(g_ref, w_ref, m_ref, v_ref, d_ref, mo_ref, vo_ref):
        d_ref[...], mo_ref[...], vo_ref[...] = _adamw(w_ref[...], g_ref[...], m_ref[...], v_ref[...])

    spec = pl.BlockSpec((rows, c), lambda i: (i, 0))
    shp = jax.ShapeDtypeStruct((r, c), F32)
    return _call(body, name=name, grid=(r // rows,), in_specs=[spec] * 4, out_specs=[spec] * 3, out_shape=[shp] * 3,
                 sem=("parallel",))(g, w, m, v)


def adamw_small(gs, ws, ms, vs):
    n = len(gs)

    def body(*refs):
        g, w, m, v = refs[:n], refs[n:2 * n], refs[2 * n:3 * n], refs[3 * n:4 * n]
        outs = refs[4 * n:]
        for i in range(n):
            d, mn, vn = _adamw(w[i][...], g[i][...], m[i][...], v[i][...])
            outs[3 * i][...] = d
            outs[3 * i + 1][...] = mn
            outs[3 * i + 2][...] = vn

    specs = [_full(a.shape) for a in gs]
    res = _call(body, name="adamw_small", in_specs=specs * 4,
                out_specs=[s for s in specs for _ in range(3)],
                out_shape=[jax.ShapeDtypeStruct(a.shape, F32) for a in gs for _ in range(3)])(*gs, *ws, *ms, *vs)
    return [tuple(res[3 * i:3 * i + 3]) for i in range(n)]


def sum_slots(name, r):
    s, rr, c = r.shape
    rows = min(rr, BLOCK)

    def body(r_ref, o_ref):
        acc = r_ref[0].astype(F32)
        for k in range(1, s):
            acc = acc + r_ref[k].astype(F32)
        o_ref[...] = acc

    return _call(body, name=name, grid=(rr // rows,), in_specs=[pl.BlockSpec((s, rows, c), lambda i: (0, i, 0))],
                 out_specs=pl.BlockSpec((rows, c), lambda i: (i, 0)), out_shape=jax.ShapeDtypeStruct((rr, c), F32),
                 sem=("parallel",))(r)


def sum_pair(partial, from_sibling):
    s, _, rr, cols = partial.shape
    rows = rr // 2

    def body(p_ref, r_ref, o_ref):
        c = lax.axis_index("c")
        o_ref[...] = (p_ref[c].astype(F32) + r_ref[1 - c].astype(F32)).astype(BF16)

    return _call(body, name="sum_pair", grid=(s, rr // rows),
                 in_specs=[pl.BlockSpec((None, 2, rows, cols), lambda k, i: (k, 0, i, 0)),
                           pl.BlockSpec((2, None, rows, cols), lambda k, i: (0, k, i, 0))],
                 out_specs=pl.BlockSpec((None, rows, cols), lambda k, i: (k, i, 0)),
                 out_shape=jax.ShapeDtypeStruct((s, rr, cols), BF16), sem=("parallel", "parallel"))(partial, from_sibling)


def _col_tile(n, k):
    if n % 896 == 0 and k <= 1024:
        return 896
    return min(n, 512)


def project(name, x, wt, after):
    m, k = x.shape
    n = wt.shape[0]
    tn = D_MODEL

    def body(x_ref, w_ref, after_ref, o_ref):
        o_ref[...] = lax.dot_general(x_ref[...], w_ref[...], _NT, preferred_element_type=F32)

    return _call(body, name=name, grid=(n // tn,),
                 in_specs=[_full((m, k)), pl.BlockSpec((tn, k), lambda j: (j, 0)), _full(after.shape)],
                 out_specs=pl.BlockSpec((m, tn), lambda j: (0, j)), out_shape=jax.ShapeDtypeStruct((m, n), F32),
                 sem=("parallel",))(x, wt, after)


def _piece_tiles(pieces):
    spans, start = [], 0
    for p in pieces:
        spans.append((start, p.shape[1] // D_MODEL))
        start += p.shape[1] // D_MODEL
    return spans, start


def _piece_spec(tm, span, rows_of, tile_of):
    first, count = span

    def index(i, j):
        t = tile_of(i, j) - first
        mine = (t >= 0) & (t < count)
        return jnp.where(mine, rows_of(i, j), 0), jnp.clip(t, 0, count - 1)

    return pl.BlockSpec((tm, D_MODEL), index)


def project_back(name, pieces, wt, after):
    m = pieces[0].shape[0]
    k = wt.shape[1]
    tm = m // 2
    spans, steps = _piece_tiles(pieces)

    def body(*refs):
        w_ref, o_ref = refs[len(pieces)], refs[len(pieces) + 2]
        j = pl.program_id(1)

        @pl.when(j == 0)
        def _():
            o_ref[...] = jnp.zeros_like(o_ref)

        for dy_ref, (first, count) in zip(refs, spans):
            @pl.when((j >= first) & (j < first + count))
            def _():
                o_ref[...] += jnp.dot(dy_ref[...], w_ref[...], preferred_element_type=F32)

    return _call(body, name=name, grid=(m // tm, steps),
                 in_specs=[_piece_spec(tm, s, lambda i, j: i, lambda i, j: j) for s in spans]
                 + [pl.BlockSpec((D_MODEL, k), lambda i, j: (j, 0)), _full(after.shape)],
                 out_specs=pl.BlockSpec((tm, k), lambda i, j: (i, 0)), out_shape=jax.ShapeDtypeStruct((m, k), F32),
                 sem=("parallel", "arbitrary"))(*pieces, wt, after)


def weight_grad_t(name, pieces, x):
    m = pieces[0].shape[0]
    k = x.shape[1]
    tm = m // 2
    spans, steps = _piece_tiles(pieces)

    def body(*refs):
        x_ref, o_ref, acc_ref = refs[len(pieces):]
        i, half = pl.program_id(0), pl.program_id(1)
        for dy_ref, (first, count) in zip(refs, spans):
            @pl.when((i >= first) & (i < first + count))
            def _():
                part = lax.dot_general(dy_ref[...], x_ref[...], (((0,), (0,)), ((), ())), preferred_element_type=F32)

                @pl.when(half == 0)
                def _():
                    acc_ref[...] = part

                @pl.when(half == 1)
                def _():
                    o_ref[...] = (acc_ref[...] + part).astype(BF16)

    return _call(body, name=name, grid=(steps, 2),
                 in_specs=[_piece_spec(tm, s, lambda i, j: j, lambda i, j: i) for s in spans]
                 + [pl.BlockSpec((tm, k), lambda i, j: (j, 0))],
                 out_specs=pl.BlockSpec((D_MODEL, k), lambda i, j: (i, 0)),
                 out_shape=jax.ShapeDtypeStruct((steps * D_MODEL, k), BF16),
                 scratch=[pltpu.VMEM((D_MODEL, k), F32)], sem=("parallel", "arbitrary"))(*pieces, x)


def mm_tn(name, x, dy):
    m, k = x.shape
    n = dy.shape[1]
    tm = m // 2
    tn = _col_tile(n, k)

    def body(x_ref, dy_ref, o_ref, acc_ref):
        part = lax.dot_general(x_ref[...].astype(BF16), dy_ref[...].astype(BF16), (((0,), (0,)), ((), ())),
                               preferred_element_type=F32)

        @pl.when(pl.program_id(1) == 0)
        def _():
            acc_ref[...] = part

        @pl.when(pl.program_id(1) == 1)
        def _():
            o_ref[...] = (acc_ref[...] + part).astype(BF16)

    return _call(body, name=name, grid=(n // tn, 2),
                 in_specs=[pl.BlockSpec((tm, k), lambda i, j: (j, 0)), pl.BlockSpec((tm, tn), lambda i, j: (j, i))],
                 out_specs=pl.BlockSpec((k, tn), lambda i, j: (0, i)), out_shape=jax.ShapeDtypeStruct((k, n), BF16),
                 scratch=[pltpu.VMEM((k, tn), F32)], sem=("parallel", "arbitrary"))(x, dy)


def _row_spec(width, col_block=0):
    return pl.BlockSpec((BLOCK, width), lambda i: (i, col_block))


def _x_spec():
    return pl.BlockSpec((None, BLOCK, D_MODEL), lambda i: (0, jnp.maximum(i - 1, 0), 0))


def prep(x, meta, g_pre):
    nb = x.shape[1] // BLOCK + 1

    def body(x_ref, meta_ref, g_ref, h_ref, u_ref):
        i = pl.program_id(0)

        @pl.when(i == 0)
        def _():
            h_ref[0:PAD_ROWS, :] = jnp.zeros((PAD_ROWS, D_MODEL), F32)
            h_ref[PAD_ROWS:BLOCK, :] = meta_ref[...]

        @pl.when(i > 0)
        def _():
            h_ref[...] = x_ref[...]

        u_ref[...] = _rms(h_ref[...], g_ref[...]).astype(BF16)

    return _call(body, name="prep", grid=(nb,), in_specs=[_x_spec(), _full((N_META, D_MODEL)), _full((1, D_MODEL))],
                 out_specs=[_row_spec(D_MODEL), _row_spec(D_MODEL)],
                 out_shape=[jax.ShapeDtypeStruct((nb * BLOCK, D_MODEL), F32),
                            jax.ShapeDtypeStruct((nb * BLOCK, D_MODEL), BF16)], sem=("parallel",))(x, meta, g_pre)


def prep_bwd(h, du, dres, g_pre):
    nb = h.shape[0] // BLOCK

    def body(h_ref, du_ref, dres_ref, g_ref, gx_ref, gm_ref, gg_ref):
        i = pl.program_id(0)
        _, vjp = jax.vjp(_rms, h_ref[...], g_ref[...])
        dh, dg = vjp(du_ref[...])

        @pl.when(i == 0)
        def _():
            gm_ref[...] = dh[PAD_ROWS:BLOCK, :]
            gg_ref[...] = dg

        @pl.when(i > 0)
        def _():
            gg_ref[...] += dg

        gx_ref[...] = dh + dres_ref[...]

    return _call(body, name="prep_bwd", grid=(nb,),
                 in_specs=[_row_spec(D_MODEL), _row_spec(D_MODEL), _row_spec(D_MODEL), _full((1, D_MODEL))],
                 out_specs=[_x_spec(), _full((N_META, D_MODEL)), _full((1, D_MODEL))],
                 out_shape=[jax.ShapeDtypeStruct((1, (nb - 1) * BLOCK, D_MODEL), F32),
                            jax.ShapeDtypeStruct((N_META, D_MODEL), F32), jax.ShapeDtypeStruct((1, D_MODEL), F32)],
                 sem=("arbitrary",))(h, du, dres, g_pre)


GROUP_W = SSM_INNER // SSM_GROUPS


def _gated_norm(y, z, g):
    t = y * _silu(z)
    return t * lax.rsqrt(jnp.mean(t * t, axis=-1, keepdims=True) + NORM_EPS) * g


def _gated_norm_groups(y, z, g):
    groups = [slice(k * GROUP_W, (k + 1) * GROUP_W) for k in range(SSM_GROUPS)]
    return jnp.concatenate([_gated_norm(y[:, s], z[:, s], g[:, s]) for s in groups], axis=1)


def _merge(ga, gs, ya, ys):
    return _sigmoid(ga) * ya + _sigmoid(gs) * ys


GATE_ATT_BLOCK = SEG["gate_att"][2] // D_MODEL
GATE_SSM_BLOCK = SEG["gate_ssm"][2] // D_MODEL


def _row_loss(out, g_post, x, target):
    diff = x + _rms(out, g_post) - target
    return 0.5 * jnp.sum(diff * diff) / D_MODEL


def tail(y_ssd, proj, a_att, x, target, woa, wos, wo, g_norm, g_post):
    nb = y_ssd.shape[0] // BLOCK
    rows = nb * BLOCK

    def body(y_ref, z_ref, ga_ref, gs_ref, a_ref, x_ref, t_ref, woa_ref, wos_ref, wo_ref, gn_ref, gp_ref,
             yn_ref, mg_ref, dout_ref, dya_ref, dys_ref, da_ref, dy_ref, dz_ref, dga_ref, dgs_ref, dres_ref,
             loss_ref, dgp_ref, dgn_ref):
        i = pl.program_id(0)
        yn, norm_vjp = jax.vjp(_gated_norm_groups, y_ref[...], z_ref[...], gn_ref[...])
        yn16 = yn.astype(BF16)
        y_ssm = jnp.dot(yn16, wos_ref[...], preferred_element_type=F32)
        y_att = jnp.dot(a_ref[...], woa_ref[...], preferred_element_type=F32)
        merged, merge_vjp = jax.vjp(_merge, ga_ref[...], gs_ref[...], y_att, y_ssm)
        merged16 = merged.astype(BF16)
        out = jnp.dot(merged16, wo_ref[...], preferred_element_type=F32)
        loss, loss_vjp = jax.vjp(_row_loss, out, gp_ref[...], x_ref[...], t_ref[...])
        counted = jnp.where(i > 0, 1.0, 0.0)
        dout, dgp, dres, _ = loss_vjp(counted)
        dout16 = dout.astype(BF16)
        dmerged = lax.dot_general(dout16, wo_ref[...], _NT, preferred_element_type=F32)
        dga, dgs, dya, dys = merge_vjp(dmerged)
        dya16, dys16 = dya.astype(BF16), dys.astype(BF16)
        dyn = lax.dot_general(dys16, wos_ref[...], _NT, preferred_element_type=F32)
        dy, dz, dgn = norm_vjp(dyn)

        yn_ref[...] = yn16
        mg_ref[...] = merged16
        dout_ref[...] = dout16
        dya_ref[...] = dya16
        dys_ref[...] = dys16
        da_ref[...] = lax.dot_general(dya16, woa_ref[...], _NT, preferred_element_type=F32)
        dy_ref[...] = dy
        dz_ref[...] = dz.astype(BF16)
        dga_ref[...] = dga.astype(BF16)
        dgs_ref[...] = dgs.astype(BF16)
        dres_ref[...] = dres

        @pl.when(i == 0)
        def _():
            loss_ref[...] = jnp.zeros_like(loss_ref)
            dgp_ref[...] = jnp.zeros_like(dgp_ref)
            dgn_ref[...] = jnp.zeros_like(dgn_ref)

        loss_ref[...] += loss * counted
        dgp_ref[...] += dgp
        dgn_ref[...] += dgn

    wide, narrow = _row_spec(SSM_INNER), _row_spec(D_MODEL)
    resident = pl.BlockSpec(memory_space=pltpu.VMEM)
    bf = lambda w: jax.ShapeDtypeStruct((rows, w), BF16)
    f32 = lambda w: jax.ShapeDtypeStruct((rows, w), F32)
    return _call(body, name="tail", grid=(nb,),
                 in_specs=[wide, wide, _row_spec(D_MODEL, GATE_ATT_BLOCK), _row_spec(D_MODEL, GATE_SSM_BLOCK), narrow,
                           _x_spec(), _x_spec(), resident, resident, resident, _full((1, SSM_INNER)),
                           _full((1, D_MODEL))],
                 out_specs=[wide, narrow, narrow, narrow, narrow, narrow, wide, wide, narrow, narrow, narrow,
                            _full((8, LANES)), _full((1, D_MODEL)), _full((1, SSM_INNER))],
                 out_shape=[bf(SSM_INNER), bf(D_MODEL), bf(D_MODEL), bf(D_MODEL), bf(D_MODEL), f32(D_MODEL),
                            f32(SSM_INNER), bf(SSM_INNER), bf(D_MODEL), bf(D_MODEL), f32(D_MODEL),
                            jax.ShapeDtypeStruct((8, LANES), F32), jax.ShapeDtypeStruct((1, D_MODEL), F32),
                            jax.ShapeDtypeStruct((1, SSM_INNER), F32)],
                 sem=("arbitrary",))(y_ssd, proj, proj, proj, a_att, x, target, woa, wos, wo, g_norm, g_post)


_NT = (((1,), (1,)), ((), ()))
ALIBI_SLOPES = tuple(2.0 ** (-8.0 * (h + 1) / ATT_Q_HEADS) for h in range(ATT_Q_HEADS))
KV_WIDTH = ATT_KV_HEADS * HEAD_DIM
Q_BLOCK = SEG["q"][2] // D_MODEL
Z_ATT_BLOCK = SEG["z_att"][2] // D_MODEL
K_BLOCK = SEG["k"][2] // KV_WIDTH
V_BLOCK = SEG["v"][2] // KV_WIDTH
META_ROW_BLOCK = PAD_ROWS // N_META


@jax.custom_vjp
def _swap_halves(x):
    return pltpu.roll(x, HEAD_DIM, 1)


_swap_halves.defvjp(lambda x: (pltpu.roll(x, HEAD_DIM, 1), None), lambda _, g: (pltpu.roll(g, HEAD_DIM, 1),))


def _both_halves(t, half):
    first = lax.broadcasted_iota(jnp.int32, t.shape, 1) < HEAD_DIM
    sw = _swap_halves(t)
    return jnp.where(first, t, sw) if half == 0 else jnp.where(first, sw, t)


def _attn_rows(q, z, kp, kc, vp, vc, km, vm, sinks, n):
    rows = ATT_GROUP * BLOCK
    i = lax.broadcasted_iota(jnp.int32, (rows, BLOCK), 0) & (BLOCK - 1)
    j = lax.broadcasted_iota(jnp.int32, (rows, BLOCK), 1)
    rel_c = (i - j).astype(F32)
    rel_p = rel_c + float(BLOCK)
    nv = jnp.zeros((rows, BLOCK), jnp.int32) + n
    ok_c = (i >= j) & (nv >= 1)
    ok_p = (j > i) & (nv >= 2)
    im = lax.broadcasted_iota(jnp.int32, (rows, N_META), 0) & (BLOCK - 1)
    jm = lax.broadcasted_iota(jnp.int32, (rows, N_META), 1)
    ok_m = ((jnp.zeros((rows, N_META), jnp.int32) + n) >= 1) | (im >= PAD_ROWS + jm)
    first = lax.broadcasted_iota(jnp.int32, (BLOCK, LANES), 1) < HEAD_DIM
    neg = -jnp.inf
    outs = []
    for kv in range(ATT_KV_HEADS):
        tile, half = divmod(kv, 2)
        lanes = slice(tile * LANES, (tile + 1) * LANES)
        kc2, kp2, km2 = (_both_halves(t[:, lanes], half).astype(BF16) for t in (kc, kp, km))
        vc2, vp2, vm2 = (_both_halves(t[:, lanes], half).astype(BF16) for t in (vc, vp, vm))
        qs, slope, sk = [], [], []
        for pair in range(ATT_GROUP // 2):
            c0 = (kv * ATT_GROUP + 2 * pair) * HEAD_DIM
            qp = q[:, c0:c0 + LANES] * HEAD_DIM ** -0.5
            qs += [jnp.where(first, qp, 0.0), jnp.where(first, 0.0, qp)]
        for g in range(ATT_GROUP):
            slope.append(jnp.full((BLOCK, 1), ALIBI_SLOPES[kv * ATT_GROUP + g], F32))
            sk.append(jnp.broadcast_to(sinks[kv * ATT_GROUP + g], (BLOCK, 1)))
        qs = jnp.concatenate(qs, axis=0).astype(BF16)
        slope = jnp.concatenate(slope, axis=0)
        sk = jnp.concatenate(sk, axis=0)
        sc = jnp.where(ok_c, lax.dot_general(qs, kc2, _NT, preferred_element_type=F32) - slope * rel_c, neg)
        sp = jnp.where(ok_p, lax.dot_general(qs, kp2, _NT, preferred_element_type=F32) - slope * rel_p, neg)
        sm = jnp.where(ok_m, lax.dot_general(qs, km2, _NT, preferred_element_type=F32), neg)
        mx = jnp.maximum(jnp.maximum(jnp.max(sc, axis=1, keepdims=True), jnp.max(sp, axis=1, keepdims=True)),
                         jnp.maximum(jnp.max(sm, axis=1, keepdims=True), sk))
        mx = lax.stop_gradient(mx)
        ec, ep, em, es = jnp.exp(sc - mx), jnp.exp(sp - mx), jnp.exp(sm - mx), jnp.exp(sk - mx)
        den = (es + jnp.sum(ec, axis=1, keepdims=True) + jnp.sum(ep, axis=1, keepdims=True)
               + jnp.sum(em, axis=1, keepdims=True))
        inv = 1.0 / den
        o = (jnp.dot((ec * inv).astype(BF16), vc2, preferred_element_type=F32)
             + jnp.dot((ep * inv).astype(BF16), vp2, preferred_element_type=F32)
             + jnp.dot((em * inv).astype(BF16), vm2, preferred_element_type=F32))
        for pair in range(ATT_GROUP // 2):
            r0 = 2 * pair * BLOCK
            outs.append(jnp.where(first, o[r0:r0 + BLOCK], o[r0 + BLOCK:r0 + 2 * BLOCK]))
    return jnp.concatenate(outs, axis=1) * _silu(z)


def _attn_specs(nb, steps_clamped):
    def blk(t):
        return jnp.minimum(t, nb - 1) if steps_clamped else t

    wide = lambda col: pl.BlockSpec((BLOCK, D_MODEL), lambda t: (blk(t), col))
    cur = lambda col: pl.BlockSpec((BLOCK, KV_WIDTH), lambda t: (blk(t), col))
    prev = lambda col: pl.BlockSpec((BLOCK, KV_WIDTH), lambda t: (jnp.maximum(blk(t) - 1, 0), col))
    meta = lambda col: pl.BlockSpec((N_META, KV_WIDTH), lambda t: (META_ROW_BLOCK, col))
    sinks = pl.BlockSpec((ATT_Q_HEADS, 1, 1), lambda t: (0, 0, 0))
    return [wide(Q_BLOCK), wide(Z_ATT_BLOCK), prev(K_BLOCK), cur(K_BLOCK), prev(V_BLOCK), cur(V_BLOCK),
            meta(K_BLOCK), meta(V_BLOCK), sinks]


def attn_fwd(proj, sinks):
    nb = proj.shape[0] // BLOCK

    def body(q_ref, z_ref, kp_ref, kc_ref, vp_ref, vc_ref, km_ref, vm_ref, sk_ref, o_ref):
        o_ref[...] = _attn_rows(q_ref[...], z_ref[...], kp_ref[...], kc_ref[...], vp_ref[...], vc_ref[...],
                                km_ref[...], vm_ref[...], tuple(sk_ref[h] for h in range(ATT_Q_HEADS)),
                                pl.program_id(0)).astype(BF16)

    return _call(body, name="attn_fwd", grid=(nb,), in_specs=_attn_specs(nb, False), out_specs=_row_spec(D_MODEL),
                 out_shape=jax.ShapeDtypeStruct((nb * BLOCK, D_MODEL), BF16), sem=("parallel",))(*([proj] * 8), sinks)


def attn_bwd(da, proj, sinks):
    nb = proj.shape[0] // BLOCK
    last = nb - 1
    wide = pl.BlockSpec((BLOCK, D_MODEL), lambda t: (jnp.minimum(t, last), 0))
    done = pl.BlockSpec((BLOCK, KV_WIDTH), lambda t: (jnp.maximum(t - 1, 0), 0))
    meta = _full((N_META, KV_WIDTH))
    par = _full((ATT_Q_HEADS, 1, 1))

    def body(da_ref, q_ref, z_ref, kp_ref, kc_ref, vp_ref, vc_ref, km_ref, vm_ref, sk_ref,
             dq_ref, dz_ref, dk_ref, dv_ref, dkm_ref, dvm_ref, dsk_ref, ck_ref, cv_ref):
        t = pl.program_id(0)

        @pl.when(t == 0)
        def _():
            ck_ref[...] = jnp.zeros_like(ck_ref)
            cv_ref[...] = jnp.zeros_like(cv_ref)
            dkm_ref[...] = jnp.zeros_like(dkm_ref)
            dvm_ref[...] = jnp.zeros_like(dvm_ref)
            dsk_ref[...] = jnp.zeros_like(dsk_ref)

        @pl.when(t < nb)
        def _():
            def f(q, z, kp, kc, vp, vc, km, vm, sk):
                return _attn_rows(q, z, kp, kc, vp, vc, km, vm, sk, t)

            _, vjp = jax.vjp(f, q_ref[...], z_ref[...], kp_ref[...], kc_ref[...], vp_ref[...], vc_ref[...],
                             km_ref[...], vm_ref[...], tuple(sk_ref[h] for h in range(ATT_Q_HEADS)))
            dq, dz, dkp, dkc, dvp, dvc, dkm, dvm, dsk = vjp(da_ref[...])
            dq_ref[...] = dq.astype(BF16)
            dz_ref[...] = dz.astype(BF16)
            for h in range(ATT_Q_HEADS):
                dsk_ref[h] += dsk[h]
            dk_ref[...] = ck_ref[...] + dkp
            dv_ref[...] = cv_ref[...] + dvp
            ck_ref[...] = dkc
            cv_ref[...] = dvc
            dkm_ref[...] += dkm
            dvm_ref[...] += dvm

        @pl.when(t == nb)
        def _():
            dk_ref[...] = ck_ref[...]
            dv_ref[...] = cv_ref[...]

    rows = nb * BLOCK
    return _call(body, name="attn_bwd", grid=(nb + 1,), in_specs=[wide] + _attn_specs(nb, True),
                 out_specs=[wide, wide, done, done, meta, meta, par],
                 out_shape=[jax.ShapeDtypeStruct((rows, D_MODEL), BF16), jax.ShapeDtypeStruct((rows, D_MODEL), BF16),
                            jax.ShapeDtypeStruct((rows, KV_WIDTH), F32), jax.ShapeDtypeStruct((rows, KV_WIDTH), F32),
                            jax.ShapeDtypeStruct((N_META, KV_WIDTH), F32), jax.ShapeDtypeStruct((N_META, KV_WIDTH), F32),
                            jax.ShapeDtypeStruct(sinks.shape, F32)],
                 scratch=[pltpu.VMEM((BLOCK, KV_WIDTH), F32), pltpu.VMEM((BLOCK, KV_WIDTH), F32)],
                 sem=("arbitrary",))(da, *([proj] * 8), sinks)


XBC_BLOCK0 = SEG["xbc"][2] // D_MODEL
CONV_COL_BLOCKS = CONV_DIM // D_MODEL
DT_TILE = SEG["dt"][2] // LANES


HALO = 8


def _conv_rows(length):
    return 544 if length % 544 == 0 else BLOCK


def _shift_rows(cur, before, j):
    if j == 0:
        return cur
    n = cur.shape[0]
    row = lax.broadcasted_iota(jnp.int32, cur.shape, 0)
    head = pltpu.roll(before, j, 0)
    if n > HALO:
        head = jnp.concatenate([head, jnp.zeros((n - HALO, cur.shape[1]), cur.dtype)], axis=0)
    return jnp.where(row >= j, pltpu.roll(cur, j, 0), head)


def _conv_pre(cur, before, w_ref, b_ref):
    pre = b_ref[...] + w_ref[CONV_WIDTH - 1:CONV_WIDTH, :] * cur
    for k in range(CONV_WIDTH - 1):
        pre = pre + w_ref[k:k + 1, :] * _shift_rows(cur, before, CONV_WIDTH - 1 - k)
    return pre


def _conv_specs(steps, rows, col0=0):
    first = XBC_BLOCK0 + col0
    halos = rows // HALO
    cur = pl.BlockSpec((rows, D_MODEL), lambda j, i: (i, first + j))
    before = pl.BlockSpec((HALO, D_MODEL), lambda j, i: (jnp.maximum(i * halos - 1, 0), first + j))
    after = pl.BlockSpec((HALO, D_MODEL), lambda j, i: (jnp.minimum(i + 1, steps - 1) * halos, first + j))
    return cur, before, after


def _valid_rows(i, rows):
    row = lax.broadcasted_iota(jnp.int32, (rows, D_MODEL), 0)
    return jnp.maximum((row >= PAD_ROWS).astype(F32), jnp.where(i > 0, 1.0, 0.0))


def conv_fwd(proj, conv_w, conv_b):
    rows = _conv_rows(proj.shape[0])
    steps = proj.shape[0] // rows
    cur, before, _ = _conv_specs(steps, rows)

    def body(c_ref, p_ref, w_ref, b_ref, o_ref):
        i = pl.program_id(1)
        pre = _conv_pre(c_ref[...], p_ref[...] * jnp.where(i > 0, 1.0, 0.0), w_ref, b_ref)
        o_ref[...] = _silu(pre) * _valid_rows(i, rows)

    return _call(body, name="conv_fwd", grid=(CONV_COL_BLOCKS, steps),
                 in_specs=[cur, before, pl.BlockSpec((CONV_WIDTH, D_MODEL), lambda j, i: (0, j)),
                           pl.BlockSpec((1, D_MODEL), lambda j, i: (0, j))],
                 out_specs=pl.BlockSpec((rows, D_MODEL), lambda j, i: (i, j)),
                 out_shape=jax.ShapeDtypeStruct((proj.shape[0], CONV_DIM), F32),
                 sem=("parallel", "parallel"))(proj, proj, conv_w, conv_b)


def conv_bwd(name, dparts, col0, proj, conv_w, conv_b):
    rows = _conv_rows(proj.shape[0])
    steps = proj.shape[0] // rows
    last = steps - 1
    ncol = sum(d.shape[1] for d in dparts) // D_MODEL
    np_ = len(dparts)
    cur, before, after = _conv_specs(steps, rows, col0)
    dcur = [pl.BlockSpec((rows, d.shape[1] // ncol), lambda j, i: (i, j)) for d in dparts]
    dafter = [pl.BlockSpec((HALO, d.shape[1] // ncol), lambda j, i: (jnp.minimum(i + 1, last) * (rows // HALO), j))
              for d in dparts]
    out_cur = pl.BlockSpec((rows, D_MODEL), lambda j, i: (i, j))
    wspec = pl.BlockSpec((CONV_WIDTH, D_MODEL), lambda j, i: (0, col0 + j))
    bspec = pl.BlockSpec((1, D_MODEL), lambda j, i: (0, col0 + j))
    wout = pl.BlockSpec((CONV_WIDTH, D_MODEL), lambda j, i: (0, j))
    bout = pl.BlockSpec((1, D_MODEL), lambda j, i: (0, j))

    def body(*refs):
        dc_refs, da_refs = refs[:np_], refs[np_:2 * np_]
        c_ref, p_ref, a_ref, w_ref, b_ref, du_ref, dw_ref, db_ref = refs[2 * np_:]
        i = pl.program_id(1)
        row = lax.broadcasted_iota(jnp.int32, (rows, D_MODEL), 0)
        curv = c_ref[...]
        beforev = p_ref[...] * jnp.where(i > 0, 1.0, 0.0)
        side_by_side = lambda rs: rs[0][...] if np_ == 1 else jnp.concatenate([r[...] for r in rs], axis=1)

        def dpre_of(pre, d):
            s = _sigmoid(pre)
            return d * (s * (1.0 + pre * (1.0 - s)))

        dp_c = dpre_of(_conv_pre(curv, beforev, w_ref, b_ref), side_by_side(dc_refs) * _valid_rows(i, rows))
        dp_a = dpre_of(_conv_pre(a_ref[...], curv[rows - HALO:], w_ref, b_ref),
                       side_by_side(da_refs) * jnp.where(i < last, 1.0, 0.0))
        du = w_ref[CONV_WIDTH - 1:CONV_WIDTH, :] * dp_c
        for j in range(1, CONV_WIDTH):
            tail = jnp.concatenate([jnp.zeros((rows - HALO, D_MODEL), F32), pltpu.roll(dp_a, HALO - j, 0)], axis=0)
            up = jnp.where(row < rows - j, pltpu.roll(dp_c, rows - j, 0), tail)
            du = du + w_ref[CONV_WIDTH - 1 - j:CONV_WIDTH - j, :] * up
        du_ref[...] = du.astype(BF16)

        @pl.when(i == 0)
        def _():
            dw_ref[...] = jnp.zeros_like(dw_ref)
            db_ref[...] = jnp.zeros_like(db_ref)

        for k in range(CONV_WIDTH):
            dw_ref[k:k + 1, :] += jnp.sum(dp_c * _shift_rows(curv, beforev, CONV_WIDTH - 1 - k), axis=0, keepdims=True)
        db_ref[...] += jnp.sum(dp_c, axis=0, keepdims=True)

    width = ncol * D_MODEL
    return _call(body, name=name, grid=(ncol, steps),
                 in_specs=dcur + dafter + [cur, before, after, wspec, bspec], out_specs=[out_cur, wout, bout],
                 out_shape=[jax.ShapeDtypeStruct((proj.shape[0], width), BF16),
                            jax.ShapeDtypeStruct((CONV_WIDTH, width), F32), jax.ShapeDtypeStruct((1, width), F32)],
                 sem=("parallel", "arbitrary"))(*dparts, *dparts, proj, proj, proj, conv_w, conv_b)


def _head_expand():
    e = np.zeros((LANES, SSM_INNER), np.float32)
    for h in range(SSM_HEADS):
        e[h, h * HEAD_DIM:(h + 1) * HEAD_DIM] = 1.0
    return jnp.asarray(e, dtype=BF16)


def _softplus(x):
    return jnp.maximum(x, 0.0) + jnp.log(1.0 + jnp.exp(-jnp.abs(x)))


def _bf16_parts(x):
    hi = x.astype(BF16)
    rest = x - hi.astype(F32)
    mid = rest.astype(BF16)
    return hi, mid, (rest - mid.astype(F32)).astype(BF16)


@jax.custom_vjp
def _times_01(x, m):
    return sum(jnp.dot(p, m, preferred_element_type=F32) for p in _bf16_parts(x))


def _times_01_bwd(m, g):
    return sum(lax.dot_general(p, m, _NT, preferred_element_type=F32) for p in _bf16_parts(g)), jnp.zeros_like(m)


_times_01.defvjp(lambda x, m: (_times_01(x, m), m), _times_01_bwd)


def _causal_ones():
    l = lax.broadcasted_iota(jnp.int32, (BLOCK, BLOCK), 0)
    s = lax.broadcasted_iota(jnp.int32, (BLOCK, BLOCK), 1)
    return (l >= s).astype(BF16)


@jax.custom_vjp
def _cumsum_rows(a):
    return sum(jnp.dot(_causal_ones(), p, preferred_element_type=F32) for p in _bf16_parts(a))


def _cumsum_rows_bwd(_, g):
    tn = (((0,), (0,)), ((), ()))
    return (sum(lax.dot_general(_causal_ones(), p, tn, preferred_element_type=F32) for p in _bf16_parts(g)),)


_cumsum_rows.defvjp(lambda a: (_cumsum_rows(a), None), _cumsum_rows_bwd)


def _ssd_heads(dt_tile, bias, alog, dsk, expand):
    dt = _softplus(dt_tile + bias)
    a = dt * (-jnp.exp(alog))
    one_row = lambda v: jnp.broadcast_to(v, (HALO, LANES))
    return _times_01(jnp.concatenate([dt, _cumsum_rows(a), one_row(jnp.sum(a, axis=0, keepdims=True)), one_row(dsk)],
                                     axis=0), expand)


HEADS_ROWS = 2 * BLOCK + 2 * HALO


def _ssd_group(xs, per_lane, bg, cg, state):
    l = lax.broadcasted_iota(jnp.int32, (BLOCK, BLOCK), 0)
    s = lax.broadcasted_iota(jnp.int32, (BLOCK, BLOCK), 1)
    causal = l >= s
    first_head = s < HEAD_DIM
    dtx, cs = per_lane[0:BLOCK], per_lane[BLOCK:2 * BLOCK]
    tot, dsk = per_lane[2 * BLOCK:2 * BLOCK + 1], per_lane[2 * BLOCK + HALO:2 * BLOCK + HALO + 1]
    bb, cb16 = bg.astype(BF16), cg.astype(BF16)
    cb = lax.dot_general(cb16, bb, _NT, preferred_element_type=F32)
    xr = xs * dtx
    y_diag = []
    for p in range(GROUP_W // LANES):
        lanes = slice(p * LANES, (p + 1) * LANES)
        c_pair = cs[:, lanes]
        c_swap = _swap_halves(c_pair)
        m = []
        for c_head in (jnp.where(first_head, c_pair, c_swap), jnp.where(first_head, c_swap, c_pair)):
            m.append(cb * jnp.exp(jnp.where(causal, c_head - c_head.T, -jnp.inf)))
        x_pair = xr[:, lanes]
        x_diag = jnp.concatenate([jnp.where(first_head, x_pair, 0.0), jnp.where(first_head, 0.0, x_pair)], axis=0)
        y_diag.append(jnp.dot(jnp.concatenate(m, axis=1).astype(BF16), x_diag.astype(BF16),
                              preferred_element_type=F32))
    st = lax.dot_general(bb, (xr * jnp.exp(tot - cs)).astype(BF16), (((0,), (0,)), ((), ())),
                         preferred_element_type=F32)
    new_state = state * jnp.exp(tot) + st
    y_off = jnp.dot(cb16, state.astype(BF16), preferred_element_type=F32) * jnp.exp(cs)
    return jnp.concatenate(y_diag, axis=1) + y_off + dsk * xs, new_state


BC_WIDTH = SSM_GROUPS * SSM_STATE


def _ssd_specs(chunk):
    xs = pl.BlockSpec((BLOCK, SSM_INNER), lambda c: (chunk(c), 0))
    dt = pl.BlockSpec((BLOCK, LANES), lambda c: (chunk(c), DT_TILE))
    expand = _full((LANES, SSM_INNER))
    b = pl.BlockSpec((BLOCK, BC_WIDTH), lambda c: (chunk(c), SSM_INNER // BC_WIDTH))
    cc = pl.BlockSpec((BLOCK, BC_WIDTH), lambda c: (chunk(c), SSM_INNER // BC_WIDTH + 1))
    par = _full((1, LANES))
    state = pl.BlockSpec((None, SSM_STATE, SSM_INNER), lambda c: (chunk(c), 0, 0))
    return xs, dt, expand, b, cc, par, state


def _group_lanes(g):
    return slice(g * GROUP_W, (g + 1) * GROUP_W), slice(g * SSM_STATE, (g + 1) * SSM_STATE)


def ssd_fwd(xbc, proj, expand, bias, alog, dsk):
    nb = xbc.shape[0] // BLOCK
    xs, dt, ex, b, cc, par, state = _ssd_specs(lambda c: c)

    def body(x_ref, dt_ref, e_ref, bi_ref, al_ref, dk_ref, b_ref, c_ref, y_ref, sp_ref, st_ref):
        @pl.when(pl.program_id(0) == 0)
        def _():
            st_ref[...] = jnp.zeros_like(st_ref)

        per_lane = _ssd_heads(dt_ref[...], bi_ref[...], al_ref[...], dk_ref[...], e_ref[...])
        for g in range(SSM_GROUPS):
            wide, tile = _group_lanes(g)
            entering = st_ref[:, wide]
            sp_ref[:, wide] = entering
            y_ref[:, wide], st_ref[:, wide] = _ssd_group(x_ref[:, wide], per_lane[:, wide], b_ref[:, tile],
                                                         c_ref[:, tile], entering)

    return _call(body, name="ssd_fwd", grid=(nb,), in_specs=[xs, dt, ex, par, par, par, b, cc],
                 out_specs=[xs, state],
                 out_shape=[jax.ShapeDtypeStruct((nb * BLOCK, SSM_INNER), F32),
                            jax.ShapeDtypeStruct((nb, SSM_STATE, SSM_INNER), F32)],
                 scratch=[pltpu.VMEM((SSM_STATE, SSM_INNER), F32)],
                 sem=("arbitrary",))(xbc, proj, expand, bias, alog, dsk, xbc, xbc)


def ssd_bwd(dy, xbc, proj, expand, bias, alog, dsk, states, comm):
    nb = xbc.shape[0] // BLOCK
    last = nb - 1
    xs, dt, ex, b, cc, par, state = _ssd_specs(lambda c: last - c)
    tile = pl.BlockSpec((BLOCK, LANES), lambda c: (last - c, 0))
    nspec = pl.BlockSpec((BLOCK, BC_WIDTH), lambda c: (last - c, 0))

    def body(dy_ref, x_ref, dt_ref, e_ref, bi_ref, al_ref, dk_ref, b_ref, c_ref, sp_ref,
             dx_ref, ddt_ref, db_ref, dc_ref, dbi_ref, dal_ref, ddk_ref, ds_ref):
        @pl.when(pl.program_id(0) == 0)
        def _():
            ds_ref[...] = jnp.zeros_like(ds_ref)
            dbi_ref[...] = jnp.zeros_like(dbi_ref)
            dal_ref[...] = jnp.zeros_like(dal_ref)
            ddk_ref[...] = jnp.zeros_like(ddk_ref)

        expand = e_ref[...]
        per_lane, heads_vjp = jax.vjp(lambda t, bi, al, dk: _ssd_heads(t, bi, al, dk, expand), dt_ref[...], bi_ref[...],
                                      al_ref[...], dk_ref[...])
        d_per_lane = []
        for g in range(SSM_GROUPS):
            wide, tile_lanes = _group_lanes(g)
            _, vjp = jax.vjp(_ssd_group, x_ref[:, wide], per_lane[:, wide], b_ref[:, tile_lanes], c_ref[:, tile_lanes],
                             sp_ref[:, wide])
            (dx_ref[:, wide], d_lanes, db_ref[:, tile_lanes], dc_ref[:, tile_lanes],
             ds_ref[:, wide]) = vjp((dy_ref[:, wide], ds_ref[:, wide]))
            d_per_lane.append(d_lanes)
        ddt, dbi, dal, ddk = heads_vjp(jnp.concatenate(d_per_lane, axis=1))
        ddt_ref[...] = ddt.astype(BF16)
        dbi_ref[...] += dbi
        dal_ref[...] += dal
        ddk_ref[...] += ddk

    def at():
        c = pl.program_id(0)
        return c == 0, c == 0, c == last

    par_shape = jax.ShapeDtypeStruct((1, LANES), F32)
    return _call_with_comm(
        body, comm, at, (dy, xbc, proj, expand, bias, alog, dsk, xbc, xbc, states), name="ssd_bwd",
        grid=(nb,), in_specs=[xs, xs, dt, ex, par, par, par, b, cc, state],
        out_specs=[xs, tile, nspec, nspec, par, par, par],
        out_shape=[jax.ShapeDtypeStruct((nb * BLOCK, SSM_INNER), F32), jax.ShapeDtypeStruct((nb * BLOCK, LANES), BF16),
                   jax.ShapeDtypeStruct((nb * BLOCK, BC_WIDTH), F32), jax.ShapeDtypeStruct((nb * BLOCK, BC_WIDTH), F32),
                   par_shape, par_shape, par_shape],
        scratch=[pltpu.VMEM((SSM_STATE, SSM_INNER), F32)])


SLAB_ROWS = 24
SLAB_META_ROW = 8


SLAB_LOSS_ROW = 7


def pack_small(dcw, dcb, dgpre, dgpost, dbias, dalog, ddsk, dsinks, dgn, dmeta, loss_tile):
    def body(cw, cb, gpre, gpost, dtb, al, dk, sk, gn, meta, loss, o_ref):
        o_ref[...] = jnp.zeros_like(o_ref)
        o_ref[SLAB_LOSS_ROW:SLAB_LOSS_ROW + 1, 0:LANES] = loss[0:1, :]
        o_ref[0:CONV_WIDTH, :] = cw[...]
        o_ref[4:5, :] = cb[...]
        o_ref[5:6, 0:1024] = gpre[...]
        o_ref[5:6, 1024:2048] = gpost[...]
        o_ref[5:6, 2048:2176] = dtb[...]
        o_ref[5:6, 2176:2304] = al[...]
        o_ref[5:6, 2304:2432] = dk[...]
        o_ref[5:6, 2432:2560] = sk[...]
        o_ref[6:7, 0:SSM_INNER] = gn[...]
        o_ref[SLAB_META_ROW:SLAB_META_ROW + N_META, 0:D_MODEL] = meta[...]

    args = (dcw, dcb, dgpre, dgpost, dbias, dalog, ddsk, dsinks, dgn, dmeta, loss_tile)
    return _call(body, name="pack_small", in_specs=[_full(a.shape) for a in args],
                 out_specs=_full((SLAB_ROWS, CONV_DIM)), out_shape=jax.ShapeDtypeStruct((SLAB_ROWS, CONV_DIM), F32))(*args)


def _lane_tile(v):
    return jnp.pad(v, ((0, 0), (0, LANES - v.shape[1])))


def kernel(x, meta_tokens, g_pre, w_in, conv_w, conv_b, dt_bias, a_log, d_skip, attn_sinks, g_ssm_norm, w_out_att, w_out_ssm, w_out, g_post, loss_target, m_meta_tokens, m_g_pre, m_w_in, m_conv_w, m_conv_b, m_dt_bias, m_a_log, m_d_skip, m_attn_sinks, m_g_ssm_norm, m_w_out_att, m_w_out_ssm, m_w_out, m_g_post, v_meta_tokens, v_g_pre, v_w_in, v_conv_w, v_conv_b, v_dt_bias, v_a_log, v_d_skip, v_attn_sinks, v_g_ssm_norm, v_w_out_att, v_w_out_ssm, v_w_out, v_g_post):
    chip = _chip_index()

    conv_w_rows = jnp.pad(conv_w[0], ((0, 2 * 8 - CONV_WIDTH), (0, 0)))
    w_in_t, m_w_in_t, v_w_in_t = w_in[0].T, m_w_in[0].T, v_w_in[0].T
    gathered_w_in, g_conv_w, g_meta = run_comm("gather_w_in",
                                               TwoLevelGather([pack_w_in(w_in_t), conv_w_rows, meta_tokens]))
    w_all_t = unpack_w_in(gathered_w_in)
    cw_full = g_conv_w[:, :CONV_WIDTH].transpose(1, 0, 2).reshape(CONV_WIDTH, CONV_DIM)
    meta_full = g_meta.transpose(1, 0, 2).reshape(N_META, D_MODEL)
    behind_w_in = 0.0 * g_meta[0, 0, 0]
    w_out_flight, w_out_started = chip_exchange_start(
        "gather_w_out_start", [(w[0] + behind_w_in).astype(BF16) for w in (w_out_att, w_out_ssm, w_out)], False)

    h, u = prep(x, meta_full, g_pre)
    proj = project("in_proj", u, w_all_t, w_out_started)

    sinks3 = attn_sinks.reshape(ATT_Q_HEADS, 1, 1)
    a_att = attn_fwd(proj, sinks3)

    xbc = conv_fwd(proj, cw_full, conv_b)
    expand = _head_expand()
    head_pars = (_lane_tile(dt_bias), _lane_tile(a_log), _lane_tile(d_skip))
    y_ssd, states = ssd_fwd(xbc, proj, expand, *head_pars)
    woa, wos, wo = [g.reshape(-1, D_MODEL) for g in chip_exchange_wait("gather_w_out_wait", w_out_flight, False, y_ssd)]

    (yn, merged, dout, dy_att, dy_ssm, da_att, dy_ssd, dz_ssm, dga, dgs, dres, loss_tile, dg_post, dgn) = tail(
        y_ssd, proj, a_att, x, loss_target, woa, wos, wo, g_ssm_norm, g_post)

    dwo = mm_tn("out_proj_dw", merged, dout)
    dwoa = mm_tn("att_out_dw", a_att, dy_att)
    dwos = mm_tn("ssm_out_dw", yn, dy_ssm)
    dq, dz_att, dk, dv, dkmeta, dvmeta, dsinks3 = attn_bwd(da_att, proj, sinks3)
    dk = dk.at[PAD_ROWS:BLOCK].add(dkmeta).astype(BF16)
    dv = dv.at[PAD_ROWS:BLOCK].add(dvmeta).astype(BF16)

    def pieces(g):
        return g.reshape(4, 2, g.shape[0] // 8, g.shape[1])

    def to_owner(g):
        return (lambda ref, dev: ref.at[_chip_of(dev), dev[2]], (g.shape[0] // 8, g.shape[1]))

    (dxs, ddt_tile, dbg, dcg, dbias, dalog, ddsk), sent_w_out = ssd_bwd(
        dy_ssd, xbc, proj, expand, *head_pars, states,
        DirectExchange([pieces(dwoa), pieces(dwos), pieces(dwo)], [to_owner(dwoa), to_owner(dwos), to_owner(dwo)],
                       ALL_MASKS, "dev", 8))
    dxs_raw, dcw_xs, dcb_xs = conv_bwd("conv_bwd_x", [dxs], 0, proj, cw_full, conv_b)
    dbc_raw, dcw_bc, dcb_bc = conv_bwd("conv_bwd_bc", [dbg, dcg], SSM_INNER // D_MODEL, proj, cw_full, conv_b)
    dcw = jnp.concatenate([dcw_xs, dcw_bc], axis=1)
    dcb = jnp.concatenate([dcb_xs, dcb_bc], axis=1)

    narrow = jnp.concatenate([dk, dv, ddt_tile, jnp.zeros((dk.shape[0], N_ACT - N_ALIGNED), BF16)], axis=1)
    dproj = [dz_ssm, dxs_raw, dbc_raw, dq, dz_att, dga, dgs, narrow]
    dw_all_t = weight_grad_t("in_proj_dw", dproj, u)

    half_rows = PACK_W // 2
    partial = pack_grad_w_in(dw_all_t).reshape(4, 2, half_rows, D_MODEL)
    from_sibling, = run_comm("pair_grads", DirectExchange(
        [partial], [(lambda ref, dev: ref.at[pl.ds(0, 4), dev[2]], (4, half_rows, D_MODEL))], SIBLING_MASK, "core", 2,
        keep_own=False))
    chip_sum = sum_pair(partial, from_sibling)
    grads_flight, started = chip_exchange_start("reduce_w_in_start", [chip_sum], True)
    du = project_back("in_proj_dx", dproj, w_all_t, started)
    grad_x, dmeta, dg_pre = prep_bwd(h, du, dres, g_pre)

    slab = pack_small(dcw, dcb, dg_pre, dg_post, dbias, dalog, ddsk, _lane_tile(dsinks3.reshape(1, ATT_Q_HEADS)), dgn,
                      dmeta, loss_tile)
    halves = [sum_slots("sum_" + nm, r) for nm, r in zip(("w_out_att", "w_out_ssm", "w_out"), sent_w_out)]
    shared = run_comm("share_w_out", Both(DirectExchange(halves, [None] * 3, SIBLING_MASK, "core", 2),
                                          DirectExchange([slab], [None], ALL_MASKS, "dev", 8)))
    g_woa, g_wos, g_wo = [f.reshape(2 * f.shape[1], f.shape[2]) for f in shared[:3]]
    small = sum_slots("sum_small", shared[3])
    loss = small[SLAB_LOSS_ROW, 0]
    d_woa, nm_woa, nv_woa = adamw_rows("adamw_w_out_att", g_woa, w_out_att[0], m_w_out_att[0], v_w_out_att[0])
    d_wos, nm_wos, nv_wos = adamw_rows("adamw_w_out_ssm", g_wos, w_out_ssm[0], m_w_out_ssm[0], v_w_out_ssm[0])
    d_wo, nm_wo, nv_wo = adamw_rows("adamw_w_out", g_wo, w_out[0], m_w_out[0], v_w_out[0])

    sent_w_in, = chip_exchange_wait("reduce_w_in_wait", grads_flight, True, d_wos)
    shared_w_in, = run_comm("share_w_in", DirectExchange([sum_slots("sum_w_in", sent_w_in)], [None], SIBLING_MASK,
                                                         "core", 2))
    g_w_in, d_w_in, nm_w_in, nv_w_in = [
        a.T for a in adamw_w_in(shared_w_in.reshape(PACK_W, D_MODEL), w_in_t, m_w_in_t, v_w_in_t)]

    cw_cols = CONV_DIM // 4
    meta_cols = D_MODEL // 4
    g_small = {
        "meta_tokens": lax.dynamic_slice(small, (SLAB_META_ROW, chip * meta_cols), (N_META, meta_cols)),
        "g_pre": small[5:6, 0:1024],
        "conv_w": lax.dynamic_slice(small, (0, chip * cw_cols), (CONV_WIDTH, cw_cols)),
        "conv_b": small[4:5, :],
        "dt_bias": small[5:6, 2048:2048 + SSM_HEADS],
        "a_log": small[5:6, 2176:2176 + SSM_HEADS],
        "d_skip": small[5:6, 2304:2304 + SSM_HEADS],
        "attn_sinks": small[5:6, 2432:2432 + ATT_Q_HEADS],
        "g_ssm_norm": small[6:7, 0:SSM_INNER],
        "g_post": small[5:6, 1024:2048],
    }
    names = list(g_small)
    w_small = dict(meta_tokens=meta_tokens, g_pre=g_pre, conv_w=conv_w[0], conv_b=conv_b, dt_bias=dt_bias, a_log=a_log,
                   d_skip=d_skip, attn_sinks=attn_sinks, g_ssm_norm=g_ssm_norm, g_post=g_post)
    m_small = dict(meta_tokens=m_meta_tokens, g_pre=m_g_pre, conv_w=m_conv_w[0], conv_b=m_conv_b, dt_bias=m_dt_bias,
                   a_log=m_a_log, d_skip=m_d_skip, attn_sinks=m_attn_sinks, g_ssm_norm=m_g_ssm_norm, g_post=m_g_post)
    v_small = dict(meta_tokens=v_meta_tokens, g_pre=v_g_pre, conv_w=v_conv_w[0], conv_b=v_conv_b, dt_bias=v_dt_bias,
                   a_log=v_a_log, d_skip=v_d_skip, attn_sinks=v_attn_sinks, g_ssm_norm=v_g_ssm_norm, g_post=v_g_post)
    upd = dict(zip(names, adamw_small([g_small[k] for k in names], [w_small[k] for k in names],
                                      [m_small[k] for k in names], [v_small[k] for k in names])))

    lead = {"conv_w"}

    def shaped(name, a):
        return a[None] if name in lead else a

    grads = dict(g_small, w_in=g_w_in, w_out_att=g_woa, w_out_ssm=g_wos, w_out=g_wo)
    deltas = dict({k: upd[k][0] for k in names}, w_in=d_w_in, w_out_att=d_woa, w_out_ssm=d_wos, w_out=d_wo)
    new_m = dict({k: upd[k][1] for k in names}, w_in=nm_w_in, w_out_att=nm_woa, w_out_ssm=nm_wos, w_out=nm_wo)
    new_v = dict({k: upd[k][2] for k in names}, w_in=nv_w_in, w_out_att=nv_woa, w_out_ssm=nv_wos, w_out=nv_wo)
    lead |= {"w_in", "w_out_att", "w_out_ssm", "w_out"}
    order = ["meta_tokens", "g_pre", "w_in", "conv_w", "conv_b", "dt_bias", "a_log", "d_skip", "attn_sinks",
             "g_ssm_norm", "w_out_att", "w_out_ssm", "w_out", "g_post"]
    outs = [loss, grad_x]
    for group in (grads, deltas, new_m, new_v):
        outs += [shaped(k, group[k]) for k in order]
    return tuple(outs)
```

```python
import functools

import numpy as np
import jax
import jax.numpy as jnp
from jax import lax
from jax.experimental import pallas as pl
from jax.experimental.pallas import tpu as pltpu

F32 = jnp.float32
BF16 = jnp.bfloat16
HI = lax.Precision.HIGHEST

D_MODEL = 1024
N_META = 16
BLOCK = 128
PAD_ROWS = BLOCK - N_META
NORM_EPS = 1e-6
HEAD_DIM = 64
ATT_Q_HEADS = 16
ATT_KV_HEADS = 4
ATT_GROUP = 4
SSM_INNER = 2048
SSM_HEADS = 32
SSM_GROUPS = 4
SSM_HEADS_PER_GROUP = 8
SSM_STATE = 128
CONV_WIDTH = 4
CONV_DIM = 3072
LANES = 128

ADAM_LR = 0.001
ADAM_B1 = 0.9
ADAM_B2 = 0.999
ADAM_EPS = 1e-08
ADAM_WD = 0.01
ADAM_STEP = 10

VMEM_LIMIT = 48 * 1024 * 1024

SHARD_W = 2440
PACK_W = 2560
SHARD_STRIDE = 2432
N_ALIGNED = 9856
N_ACT = 10240
SEG = {
    "q": (0, 1024, 5120), "k": (1024, 256, 9216), "v": (1280, 256, 9472), "z_att": (1536, 1024, 6144),
    "z_ssm": (2560, 2048, 0), "xbc": (4608, 3072, 2048), "dt": (7680, 128, 9728),
    "gate_att": (7808, 1024, 7168), "gate_ssm": (8832, 1024, 8192),
}
DT_STORED_START = 7680
DT_PAD = LANES - SSM_HEADS


def _act_col(aligned_col):
    for a0, w, p0 in SEG.values():
        if a0 <= aligned_col < a0 + w:
            return p0 + aligned_col - a0
    raise ValueError(aligned_col)


def _call(body, *, name, out_shape, in_specs, out_specs, grid=(), scratch=(), sem=None, aliases=None):
    return pl.pallas_call(
        body, out_shape=out_shape, grid=grid, in_specs=in_specs, out_specs=out_specs, scratch_shapes=list(scratch),
        name=name, input_output_aliases=aliases or {},
        compiler_params=pltpu.CompilerParams(dimension_semantics=sem, vmem_limit_bytes=VMEM_LIMIT))


def _full(shape):
    n = len(shape)
    return pl.BlockSpec(shape, lambda *_: (0,) * n)


def _chip_index():
    return lax.axis_index("x") * 2 + lax.axis_index("y")


_sigmoid = jax.nn.sigmoid


def _silu(z):
    return z * _sigmoid(z)


def _rms(x, g):
    return x * lax.rsqrt(jnp.mean(x * x, axis=-1, keepdims=True) + NORM_EPS) * g


def _peer(mask):
    x, y, c = lax.axis_index("x"), lax.axis_index("y"), lax.axis_index("c")
    return ((1 - x) if mask & 4 else x, (1 - y) if mask & 2 else y, (1 - c) if mask & 1 else c)


def _me():
    return lax.axis_index("x"), lax.axis_index("y"), lax.axis_index("c")


def _chip_of(dev):
    return 2 * dev[0] + dev[1]


CHIP_MASKS = (4, 2, 6)
ALL_MASKS = (1, 2, 3, 4, 5, 6, 7)
SIBLING_MASK = (1,)


def _remote(src, dst, send_sem, recv_sem, dev):
    return pltpu.make_async_remote_copy(src_ref=src, dst_ref=dst, send_sem=send_sem, recv_sem=recv_sem,
                                        device_id=dev, device_id_type=pl.DeviceIdType.MESH)


class _StagedCopy:
    def __init__(self, src, stage, dst, load_sem, store_sem):
        self.load = pltpu.make_async_copy(src, stage, load_sem)
        self.store = pltpu.make_async_copy(stage, dst, store_sem)

    def start(self):
        self.load.start()
        self.load.wait()
        self.store.start()

    def wait(self):
        self.store.wait()


class DirectExchange:
    def __init__(self, arrays, pieces, masks, slot_kind, nslots, keep_own=True):
        self.arrays, self.pieces, self.masks, self.slot_kind = list(arrays), list(pieces), masks, slot_kind
        self.keep_own = keep_own
        n, nk = len(arrays), len(masks)
        shapes = [a.shape if p is None else p[1] for a, p in zip(arrays, pieces)]
        self.out_shape = [jax.ShapeDtypeStruct((nslots,) + tuple(s), a.dtype) for s, a in zip(shapes, arrays)]
        self.scratch = [pltpu.SemaphoreType.DMA((n * nk,)), pltpu.SemaphoreType.DMA((n * nk,))]
        if keep_own:
            self.scratch += [pltpu.SemaphoreType.DMA((2 * n,))] + [pltpu.VMEM(s, a.dtype) for s, a in zip(shapes, arrays)]
        self.has_mid = False

    def _copies(self, ins, outs, scratch):
        send_sems, recv_sems = scratch[:2]
        me = _me()
        slot = {"chip": _chip_of(me), "dev": 4 * me[0] + 2 * me[1] + me[2], "core": me[2]}[self.slot_kind]
        nk = len(self.masks)

        def piece(a, dev):
            return ins[a] if self.pieces[a] is None else self.pieces[a][0](ins[a], dev)

        local = []
        if self.keep_own:
            local_sems, stages = scratch[2], scratch[3:]
            local = [_StagedCopy(piece(a, me), stages[a], outs[a].at[slot], local_sems.at[2 * a], local_sems.at[2 * a + 1])
                     for a in range(len(ins))]
        remote = []
        for a in range(len(ins)):
            for ki, mask in enumerate(self.masks):
                dev = _peer(mask)
                remote.append(_remote(piece(a, dev), outs[a].at[slot], send_sems.at[a * nk + ki],
                                      recv_sems.at[a * nk + ki], dev))
        return local, remote

    def start(self, ins, outs, scratch):
        local, remote = self._copies(ins, outs, scratch)
        for cp in remote + local:
            cp.start()

    def finish(self, ins, outs, scratch):
        local, remote = self._copies(ins, outs, scratch)
        for cp in remote + local:
            cp.wait()


SPLIT_ROWS = 16


class TwoLevelGather:
    def __init__(self, arrays):
        self.arrays = list(arrays)
        n = len(arrays)
        self.out_shape = [jax.ShapeDtypeStruct((4,) + a.shape, a.dtype) for a in arrays]
        self.scratch = ([pltpu.SemaphoreType.DMA((4 * n,)), pltpu.SemaphoreType.DMA((4 * n,)),
                         pltpu.SemaphoreType.DMA((3 * n,)), pltpu.SemaphoreType.DMA((3 * n,)),
                         pltpu.SemaphoreType.DMA((2 * n,))] + [pltpu.VMEM(a.shape, a.dtype) for a in arrays])
        self.has_mid = True

    def _copies(self, ins, outs, scratch):
        ici_send, ici_recv, fwd_send, fwd_recv, local_sems = scratch[:5]
        stages = scratch[5:]
        me = _me()
        sibling, in_x, in_y, diagonal = _peer(1), _peer(4), _peer(2), _peer(6)
        plan = []
        for a in range(len(ins)):
            half = ins[a].shape[0] // 2
            first = half // 2 if half % (2 * SPLIT_ROWS) == 0 else half
            mine = pl.ds(me[2] * half, half)
            local = _StagedCopy(ins[a], stages[a], outs[a].at[_chip_of(me)], local_sems.at[2 * a], local_sems.at[2 * a + 1])

            def ici(k, src, dst, dev):
                return _remote(src, dst, ici_send.at[4 * a + k], ici_recv.at[4 * a + k], dev)

            def d2d(k, chip):
                zone = outs[a].at[_chip_of(chip), mine]
                return _remote(zone, zone, fwd_send.at[3 * a + k], fwd_recv.at[3 * a + k], sibling)

            own_zone = outs[a].at[_chip_of(me), mine]
            from_x = outs[a].at[_chip_of(in_x), pl.ds(me[2] * half, first)]
            onward = [ici(2, from_x, from_x, in_y), None]
            if first < half:
                from_y = outs[a].at[_chip_of(in_y), pl.ds(me[2] * half + first, half - first)]
                onward[1] = ici(3, from_y, from_y, in_x)
            plan.append(dict(
                local=local,
                own=[ici(0, ins[a].at[mine], own_zone, in_x), ici(1, ins[a].at[mine], own_zone, in_y)],
                onward=onward, sibling=[d2d(0, in_x), d2d(1, in_y), d2d(2, diagonal)]))
        return plan

    def start(self, ins, outs, scratch):
        for p in self._copies(ins, outs, scratch):
            for cp in p["own"]:
                cp.start()
            p["local"].start()

    def mid(self, ins, outs, scratch):
        plan = self._copies(ins, outs, scratch)
        for p in plan:
            for k in range(2):
                p["own"][k].wait_recv()
                if p["onward"][k] is not None:
                    p["onward"][k].start()
                p["sibling"][k].start()
        for p in plan:
            for cp in p["onward"]:
                if cp is not None:
                    cp.wait_recv()
            p["sibling"][2].start()

    def finish(self, ins, outs, scratch):
        for p in self._copies(ins, outs, scratch):
            for cp in p["sibling"]:
                cp.wait_recv()
            for cp in p["own"] + p["sibling"] + [cp for cp in p["onward"] if cp is not None]:
                cp.wait_send()
            p["local"].wait()


class Both:
    def __init__(self, a, b):
        self.a, self.b = a, b
        self.arrays, self.out_shape = a.arrays + b.arrays, a.out_shape + b.out_shape
        self.scratch = a.scratch + b.scratch
        self.has_mid = False
        assert not (a.has_mid or b.has_mid)

    def _parts(self, ins, outs, sems):
        na, sa = len(self.a.arrays), len(self.a.scratch)
        return (ins[:na], outs[:na], sems[:sa]), (ins[na:], outs[na:], sems[sa:])

    def start(self, ins, outs, sems):
        pa, pb = self._parts(ins, outs, sems)
        self.a.start(*pa)
        self.b.start(*pb)

    def finish(self, ins, outs, sems):
        pa, pb = self._parts(ins, outs, sems)
        self.a.finish(*pa)
        self.b.finish(*pb)


_ANY = pl.BlockSpec(memory_space=pl.ANY)


def run_comm(name, comm):
    n = len(comm.arrays)

    def body(*refs):
        ins, outs, sems = refs[:n], refs[n:2 * n], refs[2 * n:]
        comm.start(ins, outs, sems)
        if comm.has_mid:
            comm.mid(ins, outs, sems)
        comm.finish(ins, outs, sems)

    return pl.pallas_call(body, name=name, out_shape=comm.out_shape, in_specs=[_ANY] * n, out_specs=[_ANY] * n,
                          scratch_shapes=comm.scratch,
                          compiler_params=pltpu.CompilerParams(vmem_limit_bytes=VMEM_LIMIT))(*comm.arrays)


_HBM = pl.BlockSpec(memory_space=pltpu.HBM)
_SEM = pl.BlockSpec(memory_space=pltpu.SEMAPHORE)
_SIDE_EFFECT = pltpu.SideEffectType.DATAFLOW_SIDE_EFFECTING


def _chip_copies(srcs, lands, send_sems, recv_sems, by_target):
    me = _me()
    copies = []
    for a, (src, land) in enumerate(zip(srcs, lands)):
        for ki, mask in enumerate(CHIP_MASKS):
            dev = _peer(mask)
            k = a * len(CHIP_MASKS) + ki
            piece = src.at[_chip_of(dev)] if by_target else src
            copies.append(_remote(piece, land.at[_chip_of(me)], send_sems.at[k], recv_sems.at[k], dev))
    return copies


def chip_exchange_start(name, arrays, by_target):
    n = len(arrays)
    nsem = n * len(CHIP_MASKS)
    piece_shapes = [a.shape[1:] if by_target else a.shape for a in arrays]

    def body(*refs):
        srcs, lands = refs[:n], refs[n:2 * n]
        send_sems, recv_sems = refs[2 * n:2 * n + 2]
        token = refs[4 * n + 2]
        stages, local_sems = refs[4 * n + 3:5 * n + 3], refs[5 * n + 3]
        me = _me()
        for cp in _chip_copies(srcs, lands, send_sems, recv_sems, by_target):
            cp.start()
        for a in range(n):
            own = _StagedCopy(srcs[a].at[_chip_of(me)] if by_target else srcs[a], stages[a], lands[a].at[_chip_of(me)],
                              local_sems.at[2 * a], local_sems.at[2 * a + 1])
            own.start()
            own.wait()
        token[...] = jnp.zeros_like(token)

    lands = [lax.empty((4,) + tuple(s), a.dtype) for s, a in zip(piece_shapes, arrays)]
    hbm = lambda a: pltpu.HBM(a.shape, a.dtype)
    res = pl.pallas_call(
        body, name=name,
        out_shape=(pltpu.SemaphoreType.DMA((nsem,)), pltpu.SemaphoreType.DMA((nsem,)), *[hbm(a) for a in arrays],
                   *[hbm(l) for l in lands], jax.ShapeDtypeStruct((8, LANES), F32)),
        in_specs=(_HBM,) * (2 * n), out_specs=(_SEM, _SEM) + (_HBM,) * (2 * n) + (pl.BlockSpec(memory_space=pltpu.VMEM),),
        input_output_aliases={i: i + 2 for i in range(2 * n)},
        scratch_shapes=[pltpu.VMEM(tuple(s), a.dtype) for s, a in zip(piece_shapes, arrays)]
        + [pltpu.SemaphoreType.DMA((2 * n,))],
        compiler_params=pltpu.CompilerParams(has_side_effects=_SIDE_EFFECT, vmem_limit_bytes=VMEM_LIMIT),
    )(*[pltpu.with_memory_space_constraint(a, pltpu.HBM) for a in arrays + lands])
    return res[:-1], res[-1]


def chip_exchange_wait(name, in_flight, by_target, after):
    send_sems, recv_sems, *thru = in_flight
    n = len(thru) // 2

    def body(*refs):
        srcs, lands, (send, recv) = refs[:n], refs[n:2 * n], refs[2 * n:2 * n + 2]
        for cp in _chip_copies(srcs, lands, send, recv, by_target):
            cp.wait_send()
            cp.wait_recv()

    return pl.pallas_call(
        body, name=name, out_shape=tuple(pltpu.HBM(t.shape, t.dtype) for t in thru),
        in_specs=(_HBM,) * (2 * n) + (_SEM, _SEM, pl.BlockSpec(memory_space=pl.ANY)), out_specs=(_HBM,) * (2 * n),
        input_output_aliases={i: i for i in range(2 * n)},
        compiler_params=pltpu.CompilerParams(has_side_effects=_SIDE_EFFECT),
    )(*thru, send_sems, recv_sems, after)[n:]


def _call_with_comm(body, comm, steps, args, *, name, out_shape, in_specs, out_specs, grid, scratch=()):
    ni, no, ns, nc = len(in_specs), len(out_specs), len(scratch), len(comm.arrays)

    def full_body(*refs):
        ins, cins = refs[:ni], refs[ni:ni + nc]
        outs, couts = refs[ni + nc:ni + nc + no], refs[ni + nc + no:ni + 2 * nc + no]
        scr, csems = refs[ni + 2 * nc + no:ni + 2 * nc + no + ns], refs[ni + 2 * nc + no + ns:]
        first, middle, last = steps()
        pl.when(first)(lambda: comm.start(cins, couts, csems))
        if comm.has_mid:
            pl.when(middle)(lambda: comm.mid(cins, couts, csems))
        body(*ins, *outs, *scr)
        pl.when(last)(lambda: comm.finish(cins, couts, csems))

    res = pl.pallas_call(
        full_body, name=name, out_shape=list(out_shape) + comm.out_shape, grid=grid,
        in_specs=list(in_specs) + [_ANY] * nc, out_specs=list(out_specs) + [_ANY] * nc,
        scratch_shapes=list(scratch) + comm.scratch,
        compiler_params=pltpu.CompilerParams(dimension_semantics=("arbitrary",) * len(grid),
                                             vmem_limit_bytes=VMEM_LIMIT))(*args, *comm.arrays)
    return res[:no], res[no:]


def _shard_pieces(chip):
    if chip < 3:
        return [(0, SHARD_W, 8 * chip)]
    behind_dt = DT_STORED_START + SSM_HEADS - 3 * SHARD_W
    return [(0, behind_dt, 24), (behind_dt, SHARD_W - behind_dt, behind_dt + 24 + DT_PAD)]


W_IN_COLS = 256


def pack_w_in(wt):
    def body(w_ref, o_ref, pad_ref):
        chip = _chip_index()
        pad_ref[...] = jnp.zeros_like(pad_ref)
        for cv in range(4):
            @pl.when(chip == cv)
            def _():
                for src, n, dst in _shard_pieces(cv):
                    pad_ref[dst:dst + n, :] = w_ref[src:src + n, :]
        o_ref[...] = pad_ref[...].astype(BF16)

    return _call(body, name="pack_w_in", grid=(D_MODEL // W_IN_COLS,),
                 in_specs=[pl.BlockSpec((SHARD_W, W_IN_COLS), lambda i: (0, i))],
                 out_specs=pl.BlockSpec((PACK_W, W_IN_COLS), lambda i: (0, i)),
                 out_shape=jax.ShapeDtypeStruct((PACK_W, D_MODEL), BF16),
                 scratch=[pltpu.VMEM((PACK_W, W_IN_COLS), F32)], sem=("parallel",))(wt)


def _tile_runs():
    runs, fix = [], []
    for t in range(N_ALIGNED // LANES):
        s = min(t // 19, 3)
        j = t - 19 * s
        p = _act_col(t * LANES)
        if runs and runs[-1][1] == s and runs[-1][0] + runs[-1][3] == p and runs[-1][2] + runs[-1][3] == j * LANES:
            runs[-1][3] += LANES
        else:
            runs.append([p, s, j * LANES, LANES])
        if j == 0 and s > 0:
            fix.append((p, s - 1))
    return runs, fix


def unpack_w_in(bg):
    runs, fix = _tile_runs()

    def body(b_ref, o_ref):
        for p, s, j, w in runs:
            o_ref[p:p + w, :] = b_ref[s, j:j + w, :]
        for p, s in fix:
            o_ref[p:p + LANES, :] = o_ref[p:p + LANES, :] + b_ref[s, SHARD_STRIDE:PACK_W, :]
        o_ref[N_ALIGNED:N_ACT, :] = jnp.zeros((N_ACT - N_ALIGNED, W_IN_COLS), BF16)

    return _call(body, name="unpack_w_in", grid=(D_MODEL // W_IN_COLS,),
                 in_specs=[pl.BlockSpec((4, PACK_W, W_IN_COLS), lambda i: (0, 0, i))],
                 out_specs=pl.BlockSpec((N_ACT, W_IN_COLS), lambda i: (0, i)),
                 out_shape=jax.ShapeDtypeStruct((N_ACT, D_MODEL), BF16), sem=("parallel",))(bg)


def pack_grad_w_in(dwt):
    def body(g_ref, o_ref):
        for s in range(4):
            for j in range(PACK_W // LANES):
                p = _act_col((19 * s + j) * LANES)
                o_ref[s, j * LANES:(j + 1) * LANES, :] = g_ref[p:p + LANES, :]

    return _call(body, name="pack_grad_w_in", grid=(D_MODEL // W_IN_COLS,),
                 in_specs=[pl.BlockSpec((N_ACT, W_IN_COLS), lambda i: (0, i))],
                 out_specs=pl.BlockSpec((4, PACK_W, W_IN_COLS), lambda i: (0, 0, i)),
                 out_shape=jax.ShapeDtypeStruct((4, PACK_W, D_MODEL), BF16), sem=("parallel",))(dwt)


def _adamw(w, g, m, v):
    m = ADAM_B1 * m + (1.0 - ADAM_B1) * g
    v = ADAM_B2 * v + (1.0 - ADAM_B2) * jnp.square(g)
    m_hat = m / (1.0 - ADAM_B1 ** ADAM_STEP)
    v_hat = v / (1.0 - ADAM_B2 ** ADAM_STEP)
    delta = -ADAM_LR * (m_hat / (jnp.sqrt(v_hat) + ADAM_EPS) + ADAM_WD * w)
    return delta, m, v


def adamw_w_in(g_packed, wt, mt, vt):
    cols = LANES

    def body(g_ref, w_ref, m_ref, v_ref, go_ref, d_ref, mo_ref, vo_ref):
        chip = _chip_index()
        for cv in range(4):
            @pl.when(chip == cv)
            def _():
                for dst, n, src in _shard_pieces(cv):
                    go_ref[dst:dst + n, :] = g_ref[src:src + n, :]
        d_ref[...], mo_ref[...], vo_ref[...] = _adamw(w_ref[...], go_ref[...], m_ref[...], v_ref[...])

    spec = pl.BlockSpec((SHARD_W, cols), lambda i: (0, i))
    shp = jax.ShapeDtypeStruct((SHARD_W, D_MODEL), F32)
    return _call(body, name="adamw_w_in", grid=(D_MODEL // cols,),
                 in_specs=[pl.BlockSpec((PACK_W, cols), lambda i: (0, i)), spec, spec, spec],
                 out_specs=[spec] * 4, out_shape=[shp] * 4, sem=("parallel",))(g_packed, wt, mt, vt)


def adamw_rows(name, g, w, m, v):
    r, c = g.shape
    rows = min(r, BLOCK)

    def body(g_ref, w_ref, m_ref, v_ref, d_ref, mo_ref, vo_ref):
        d_ref[...], mo_ref[...], vo_ref[...] = _adamw(w_ref[...], g_ref[...], m_ref[...], v_ref[...])

    spec = pl.BlockSpec((rows, c), lambda i: (i, 0))
    shp = jax.ShapeDtypeStruct((r, c), F32)
    return _call(body, name=name, grid=(r // rows,), in_specs=[spec] * 4, out_specs=[spec] * 3, out_shape=[shp] * 3,
                 sem=("parallel",))(g, w, m, v)


def adamw_small(gs, ws, ms, vs):
    n = len(gs)

    def body(*refs):
        g, w, m, v = refs[:n], refs[n:2 * n], refs[2 * n:3 * n], refs[3 * n:4 * n]
        outs = refs[4 * n:]
        for i in range(n):
            d, mn, vn = _adamw(w[i][...], g[i][...], m[i][...], v[i][...])
            outs[3 * i][...] = d
            outs[3 * i + 1][...] = mn
            outs[3 * i + 2][...] = vn

    specs = [_full(a.shape) for a in gs]
    res = _call(body, name="adamw_small", in_specs=specs * 4,
                out_specs=[s for s in specs for _ in range(3)],
                out_shape=[jax.ShapeDtypeStruct(a.shape, F32) for a in gs for _ in range(3)])(*gs, *ws, *ms, *vs)
    return [tuple(res[3 * i:3 * i + 3]) for i in range(n)]


def sum_slots(name, r):
    s, rr, c = r.shape
    rows = min(rr, BLOCK)

    def body(r_ref, o_ref):
        acc = r_ref[0].astype(F32)
        for k in range(1, s):
            acc = acc + r_ref[k].astype(F32)
        o_ref[...] = acc

    return _call(body, name=name, grid=(rr // rows,), in_specs=[pl.BlockSpec((s, rows, c), lambda i: (0, i, 0))],
                 out_specs=pl.BlockSpec((rows, c), lambda i: (i, 0)), out_shape=jax.ShapeDtypeStruct((rr, c), F32),
                 sem=("parallel",))(r)


def sum_pair(partial, from_sibling):
    s, _, rr, cols = partial.shape
    rows = rr // 2

    def body(p_ref, r_ref, o_ref):
        c = lax.axis_index("c")
        o_ref[...] = (p_ref[c].astype(F32) + r_ref[1 - c].astype(F32)).astype(BF16)

    return _call(body, name="sum_pair", grid=(s, rr // rows),
                 in_specs=[pl.BlockSpec((None, 2, rows, cols), lambda k, i: (k, 0, i, 0)),
                           pl.BlockSpec((2, None, rows, cols), lambda k, i: (0, k, i, 0))],
                 out_specs=pl.BlockSpec((None, rows, cols), lambda k, i: (k, i, 0)),
                 out_shape=jax.ShapeDtypeStruct((s, rr, cols), BF16), sem=("parallel", "parallel"))(partial, from_sibling)


def _col_tile(n, k):
    if n % 896 == 0 and k <= 1024:
        return 896
    return min(n, 512)


def project(name, x, wt, after):
    m, k = x.shape
    n = wt.shape[0]
    tn = D_MODEL

    def body(x_ref, w_ref, after_ref, o_ref):
        o_ref[...] = lax.dot_general(x_ref[...], w_ref[...], _NT, preferred_element_type=F32)

    return _call(body, name=name, grid=(n // tn,),
                 in_specs=[_full((m, k)), pl.BlockSpec((tn, k), lambda j: (j, 0)), _full(after.shape)],
                 out_specs=pl.BlockSpec((m, tn), lambda j: (0, j)), out_shape=jax.ShapeDtypeStruct((m, n), F32),
                 sem=("parallel",))(x, wt, after)


def _piece_tiles(pieces):
    spans, start = [], 0
    for p in pieces:
        spans.append((start, p.shape[1] // D_MODEL))
        start += p.shape[1] // D_MODEL
    return spans, start


def _piece_spec(tm, span, rows_of, tile_of):
    first, count = span

    def index(i, j):
        t = tile_of(i, j) - first
        mine = (t >= 0) & (t < count)
        return jnp.where(mine, rows_of(i, j), 0), jnp.clip(t, 0, count - 1)

    return pl.BlockSpec((tm, D_MODEL), index)


def project_back(name, pieces, wt, after):
    m = pieces[0].shape[0]
    k = wt.shape[1]
    tm = m // 2
    spans, steps = _piece_tiles(pieces)

    def body(*refs):
        w_ref, o_ref = refs[len(pieces)], refs[len(pieces) + 2]
        j = pl.program_id(1)

        @pl.when(j == 0)
        def _():
            o_ref[...] = jnp.zeros_like(o_ref)

        for dy_ref, (first, count) in zip(refs, spans):
            @pl.when((j >= first) & (j < first + count))
            def _():
                o_ref[...] += jnp.dot(dy_ref[...], w_ref[...], preferred_element_type=F32)

    return _call(body, name=name, grid=(m // tm, steps),
                 in_specs=[_piece_spec(tm, s, lambda i, j: i, lambda i, j: j) for s in spans]
                 + [pl.BlockSpec((D_MODEL, k), lambda i, j: (j, 0)), _full(after.shape)],
                 out_specs=pl.BlockSpec((tm, k), lambda i, j: (i, 0)), out_shape=jax.ShapeDtypeStruct((m, k), F32),
                 sem=("parallel", "arbitrary"))(*pieces, wt, after)


def weight_grad_t(name, pieces, x):
    m = pieces[0].shape[0]
    k = x.shape[1]
    tm = m // 2
    spans, steps = _piece_tiles(pieces)

    def body(*refs):
        x_ref, o_ref, acc_ref = refs[len(pieces):]
        i, half = pl.program_id(0), pl.program_id(1)
        for dy_ref, (first, count) in zip(refs, spans):
            @pl.when((i >= first) & (i < first + count))
            def _():
                part = lax.dot_general(dy_ref[...], x_ref[...], (((0,), (0,)), ((), ())), preferred_element_type=F32)

                @pl.when(half == 0)
                def _():
                    acc_ref[...] = part

                @pl.when(half == 1)
                def _():
                    o_ref[...] = (acc_ref[...] + part).astype(BF16)

    return _call(body, name=name, grid=(steps, 2),
                 in_specs=[_piece_spec(tm, s, lambda i, j: j, lambda i, j: i) for s in spans]
                 + [pl.BlockSpec((tm, k), lambda i, j: (j, 0))],
                 out_specs=pl.BlockSpec((D_MODEL, k), lambda i, j: (i, 0)),
                 out_shape=jax.ShapeDtypeStruct((steps * D_MODEL, k), BF16),
                 scratch=[pltpu.VMEM((D_MODEL, k), F32)], sem=("parallel", "arbitrary"))(*pieces, x)


def mm_tn(name, x, dy):
    m, k = x.shape
    n = dy.shape[1]
    tm = m // 2
    tn = _col_tile(n, k)

    def body(x_ref, dy_ref, o_ref, acc_ref):
        part = lax.dot_general(x_ref[...].astype(BF16), dy_ref[...].astype(BF16), (((0,), (0,)), ((), ())),
                               preferred_element_type=F32)

        @pl.when(pl.program_id(1) == 0)
        def _():
            acc_ref[...] = part

        @pl.when(pl.program_id(1) == 1)
        def _():
            o_ref[...] = (acc_ref[...] + part).astype(BF16)

    return _call(body, name=name, grid=(n // tn, 2),
                 in_specs=[pl.BlockSpec((tm, k), lambda i, j: (j, 0)), pl.BlockSpec((tm, tn), lambda i, j: (j, i))],
                 out_specs=pl.BlockSpec((k, tn), lambda i, j: (0, i)), out_shape=jax.ShapeDtypeStruct((k, n), BF16),
                 scratch=[pltpu.VMEM((k, tn), F32)], sem=("parallel", "arbitrary"))(x, dy)


def _row_spec(width, col_block=0):
    return pl.BlockSpec((BLOCK, width), lambda i: (i, col_block))


def _x_spec():
    return pl.BlockSpec((None, BLOCK, D_MODEL), lambda i: (0, jnp.maximum(i - 1, 0), 0))


def prep(x, meta, g_pre):
    nb = x.shape[1] // BLOCK + 1

    def body(x_ref, meta_ref, g_ref, h_ref, u_ref):
        i = pl.program_id(0)

        @pl.when(i == 0)
        def _():
            h_ref[0:PAD_ROWS, :] = jnp.zeros((PAD_ROWS, D_MODEL), F32)
            h_ref[PAD_ROWS:BLOCK, :] = meta_ref[...]

        @pl.when(i > 0)
        def _():
            h_ref[...] = x_ref[...]

        u_ref[...] = _rms(h_ref[...], g_ref[...]).astype(BF16)

    return _call(body, name="prep", grid=(nb,), in_specs=[_x_spec(), _full((N_META, D_MODEL)), _full((1, D_MODEL))],
                 out_specs=[_row_spec(D_MODEL), _row_spec(D_MODEL)],
                 out_shape=[jax.ShapeDtypeStruct((nb * BLOCK, D_MODEL), F32),
                            jax.ShapeDtypeStruct((nb * BLOCK, D_MODEL), BF16)], sem=("parallel",))(x, meta, g_pre)


def prep_bwd(h, du, dres, g_pre):
    nb = h.shape[0] // BLOCK

    def body(h_ref, du_ref, dres_ref, g_ref, gx_ref, gm_ref, gg_ref):
        i = pl.program_id(0)
        _, vjp = jax.vjp(_rms, h_ref[...], g_ref[...])
        dh, dg = vjp(du_ref[...])

        @pl.when(i == 0)
        def _():
            gm_ref[...] = dh[PAD_ROWS:BLOCK, :]
            gg_ref[...] = dg

        @pl.when(i > 0)
        def _():
            gg_ref[...] += dg

        gx_ref[...] = dh + dres_ref[...]

    return _call(body, name="prep_bwd", grid=(nb,),
                 in_specs=[_row_spec(D_MODEL), _row_spec(D_MODEL), _row_spec(D_MODEL), _full((1, D_MODEL))],
                 out_specs=[_x_spec(), _full((N_META, D_MODEL)), _full((1, D_MODEL))],
                 out_shape=[jax.ShapeDtypeStruct((1, (nb - 1) * BLOCK, D_MODEL), F32),
                            jax.ShapeDtypeStruct((N_META, D_MODEL), F32), jax.ShapeDtypeStruct((1, D_MODEL), F32)],
                 sem=("arbitrary",))(h, du, dres, g_pre)


GROUP_W = SSM_INNER // SSM_GROUPS


def _gated_norm(y, z, g):
    t = y * _silu(z)
    return t * lax.rsqrt(jnp.mean(t * t, axis=-1, keepdims=True) + NORM_EPS) * g


def _gated_norm_groups(y, z, g):
    groups = [slice(k * GROUP_W, (k + 1) * GROUP_W) for k in range(SSM_GROUPS)]
    return jnp.concatenate([_gated_norm(y[:, s], z[:, s], g[:, s]) for s in groups], axis=1)


def _merge(ga, gs, ya, ys):
    return _sigmoid(ga) * ya + _sigmoid(gs) * ys


GATE_ATT_BLOCK = SEG["gate_att"][2] // D_MODEL
GATE_SSM_BLOCK = SEG["gate_ssm"][2] // D_MODEL


def _row_loss(out, g_post, x, target):
    diff = x + _rms(out, g_post) - target
    return 0.5 * jnp.sum(diff * diff) / D_MODEL


def tail(y_ssd, proj, a_att, x, target, woa, wos, wo, g_norm, g_post):
    nb = y_ssd.shape[0] // BLOCK
    rows = nb * BLOCK

    def body(y_ref, z_ref, ga_ref, gs_ref, a_ref, x_ref, t_ref, woa_ref, wos_ref, wo_ref, gn_ref, gp_ref,
             yn_ref, mg_ref, dout_ref, dya_ref, dys_ref, da_ref, dy_ref, dz_ref, dga_ref, dgs_ref, dres_ref,
             loss_ref, dgp_ref, dgn_ref):
        i = pl.program_id(0)
        yn, norm_vjp = jax.vjp(_gated_norm_groups, y_ref[...], z_ref[...], gn_ref[...])
        yn16 = yn.astype(BF16)
        y_ssm = jnp.dot(yn16, wos_ref[...], preferred_element_type=F32)
        y_att = jnp.dot(a_ref[...], woa_ref[...], preferred_element_type=F32)
        merged, merge_vjp = jax.vjp(_merge, ga_ref[...], gs_ref[...], y_att, y_ssm)
        merged16 = merged.astype(BF16)
        out = jnp.dot(merged16, wo_ref[...], preferred_element_type=F32)
        loss, loss_vjp = jax.vjp(_row_loss, out, gp_ref[...], x_ref[...], t_ref[...])
        counted = jnp.where(i > 0, 1.0, 0.0)
        dout, dgp, dres, _ = loss_vjp(counted)
        dout16 = dout.astype(BF16)
        dmerged = lax.dot_general(dout16, wo_ref[...], _NT, preferred_element_type=F32)
        dga, dgs, dya, dys = merge_vjp(dmerged)
        dya16, dys16 = dya.astype(BF16), dys.astype(BF16)
        dyn = lax.dot_general(dys16, wos_ref[...], _NT, preferred_element_type=F32)
        dy, dz, dgn = norm_vjp(dyn)

        yn_ref[...] = yn16
        mg_ref[...] = merged16
        dout_ref[...] = dout16
        dya_ref[...] = dya16
        dys_ref[...] = dys16
        da_ref[...] = lax.dot_general(dya16, woa_ref[...], _NT, preferred_element_type=F32)
        dy_ref[...] = dy
        dz_ref[...] = dz.astype(BF16)
        dga_ref[...] = dga.astype(BF16)
        dgs_ref[...] = dgs.astype(BF16)
        dres_ref[...] = dres

        @pl.when(i == 0)
        def _():
            loss_ref[...] = jnp.zeros_like(loss_ref)
            dgp_ref[...] = jnp.zeros_like(dgp_ref)
            dgn_ref[...] = jnp.zeros_like(dgn_ref)

        loss_ref[...] += loss * counted
        dgp_ref[...] += dgp
        dgn_ref[...] += dgn

    wide, narrow = _row_spec(SSM_INNER), _row_spec(D_MODEL)
    resident = pl.BlockSpec(memory_space=pltpu.VMEM)
    bf = lambda w: jax.ShapeDtypeStruct((rows, w), BF16)
    f32 = lambda w: jax.ShapeDtypeStruct((rows, w), F32)
    return _call(body, name="tail", grid=(nb,),
                 in_specs=[wide, wide, _row_spec(D_MODEL, GATE_ATT_BLOCK), _row_spec(D_MODEL, GATE_SSM_BLOCK), narrow,
                           _x_spec(), _x_spec(), resident, resident, resident, _full((1, SSM_INNER)),
                           _full((1, D_MODEL))],
                 out_specs=[wide, narrow, narrow, narrow, narrow, narrow, wide, wide, narrow, narrow, narrow,
                            _full((8, LANES)), _full((1, D_MODEL)), _full((1, SSM_INNER))],
                 out_shape=[bf(SSM_INNER), bf(D_MODEL), bf(D_MODEL), bf(D_MODEL), bf(D_MODEL), f32(D_MODEL),
                            f32(SSM_INNER), bf(SSM_INNER), bf(D_MODEL), bf(D_MODEL), f32(D_MODEL),
                            jax.ShapeDtypeStruct((8, LANES), F32), jax.ShapeDtypeStruct((1, D_MODEL), F32),
                            jax.ShapeDtypeStruct((1, SSM_INNER), F32)],
                 sem=("arbitrary",))(y_ssd, proj, proj, proj, a_att, x, target, woa, wos, wo, g_norm, g_post)


_NT = (((1,), (1,)), ((), ()))
ALIBI_SLOPES = tuple(2.0 ** (-8.0 * (h + 1) / ATT_Q_HEADS) for h in range(ATT_Q_HEADS))
KV_WIDTH = ATT_KV_HEADS * HEAD_DIM
Q_BLOCK = SEG["q"][2] // D_MODEL
Z_ATT_BLOCK = SEG["z_att"][2] // D_MODEL
K_BLOCK = SEG["k"][2] // KV_WIDTH
V_BLOCK = SEG["v"][2] // KV_WIDTH
META_ROW_BLOCK = PAD_ROWS // N_META


@jax.custom_vjp
def _swap_halves(x):
    return pltpu.roll(x, HEAD_DIM, 1)


_swap_halves.defvjp(lambda x: (pltpu.roll(x, HEAD_DIM, 1), None), lambda _, g: (pltpu.roll(g, HEAD_DIM, 1),))


def _both_halves(t, half):
    first = lax.broadcasted_iota(jnp.int32, t.shape, 1) < HEAD_DIM
    sw = _swap_halves(t)
    return jnp.where(first, t, sw) if half == 0 else jnp.where(first, sw, t)


def _attn_rows(q, z, kp, kc, vp, vc, km, vm, sinks, n):
    rows = ATT_GROUP * BLOCK
    i = lax.broadcasted_iota(jnp.int32, (rows, BLOCK), 0) & (BLOCK - 1)
    j = lax.broadcasted_iota(jnp.int32, (rows, BLOCK), 1)
    rel_c = (i - j).astype(F32)
    rel_p = rel_c + float(BLOCK)
    nv = jnp.zeros((rows, BLOCK), jnp.int32) + n
    ok_c = (i >= j) & (nv >= 1)
    ok_p = (j > i) & (nv >= 2)
    im = lax.broadcasted_iota(jnp.int32, (rows, N_META), 0) & (BLOCK - 1)
    jm = lax.broadcasted_iota(jnp.int32, (rows, N_META), 1)
    ok_m = ((jnp.zeros((rows, N_META), jnp.int32) + n) >= 1) | (im >= PAD_ROWS + jm)
    first = lax.broadcasted_iota(jnp.int32, (BLOCK, LANES), 1) < HEAD_DIM
    neg = -jnp.inf
    outs = []
    for kv in range(ATT_KV_HEADS):
        tile, half = divmod(kv, 2)
        lanes = slice(tile * LANES, (tile + 1) * LANES)
        kc2, kp2, km2 = (_both_halves(t[:, lanes], half).astype(BF16) for t in (kc, kp, km))
        vc2, vp2, vm2 = (_both_halves(t[:, lanes], half).astype(BF16) for t in (vc, vp, vm))
        qs, slope, sk = [], [], []
        for pair in range(ATT_GROUP // 2):
            c0 = (kv * ATT_GROUP + 2 * pair) * HEAD_DIM
            qp = q[:, c0:c0 + LANES] * HEAD_DIM ** -0.5
            qs += [jnp.where(first, qp, 0.0), jnp.where(first, 0.0, qp)]
        for g in range(ATT_GROUP):
            slope.append(jnp.full((BLOCK, 1), ALIBI_SLOPES[kv * ATT_GROUP + g], F32))
            sk.append(jnp.broadcast_to(sinks[kv * ATT_GROUP + g], (BLOCK, 1)))
        qs = jnp.concatenate(qs, axis=0).astype(BF16)
        slope = jnp.concatenate(slope, axis=0)
        sk = jnp.concatenate(sk, axis=0)
        sc = jnp.where(ok_c, lax.dot_general(qs, kc2, _NT, preferred_element_type=F32) - slope * rel_c, neg)
        sp = jnp.where(ok_p, lax.dot_general(qs, kp2, _NT, preferred_element_type=F32) - slope * rel_p, neg)
        sm = jnp.where(ok_m, lax.dot_general(qs, km2, _NT, preferred_element_type=F32), neg)
        mx = jnp.maximum(jnp.maximum(jnp.max(sc, axis=1, keepdims=True), jnp.max(sp, axis=1, keepdims=True)),
                         jnp.maximum(jnp.max(sm, axis=1, keepdims=True), sk))
        mx = lax.stop_gradient(mx)
        ec, ep, em, es = jnp.exp(sc - mx), jnp.exp(sp - mx), jnp.exp(sm - mx), jnp.exp(sk - mx)
        den = (es + jnp.sum(ec, axis=1, keepdims=True) + jnp.sum(ep, axis=1, keepdims=True)
               + jnp.sum(em, axis=1, keepdims=True))
        inv = 1.0 / den
        o = (jnp.dot((ec * inv).astype(BF16), vc2, preferred_element_type=F32)
             + jnp.dot((ep * inv).astype(BF16), vp2, preferred_element_type=F32)
             + jnp.dot((em * inv).astype(BF16), vm2, preferred_element_type=F32))
        for pair in range(ATT_GROUP // 2):
            r0 = 2 * pair * BLOCK
            outs.append(jnp.where(first, o[r0:r0 + BLOCK], o[r0 + BLOCK:r0 + 2 * BLOCK]))
    return jnp.concatenate(outs, axis=1) * _silu(z)


def _attn_specs(nb, steps_clamped):
    def blk(t):
        return jnp.minimum(t, nb - 1) if steps_clamped else t

    wide = lambda col: pl.BlockSpec((BLOCK, D_MODEL), lambda t: (blk(t), col))
    cur = lambda col: pl.BlockSpec((BLOCK, KV_WIDTH), lambda t: (blk(t), col))
    prev = lambda col: pl.BlockSpec((BLOCK, KV_WIDTH), lambda t: (jnp.maximum(blk(t) - 1, 0), col))
    meta = lambda col: pl.BlockSpec((N_META, KV_WIDTH), lambda t: (META_ROW_BLOCK, col))
    sinks = pl.BlockSpec((ATT_Q_HEADS, 1, 1), lambda t: (0, 0, 0))
    return [wide(Q_BLOCK), wide(Z_ATT_BLOCK), prev(K_BLOCK), cur(K_BLOCK), prev(V_BLOCK), cur(V_BLOCK),
            meta(K_BLOCK), meta(V_BLOCK), sinks]


def attn_fwd(proj, sinks):
    nb = proj.shape[0] // BLOCK

    def body(q_ref, z_ref, kp_ref, kc_ref, vp_ref, vc_ref, km_ref, vm_ref, sk_ref, o_ref):
        o_ref[...] = _attn_rows(q_ref[...], z_ref[...], kp_ref[...], kc_ref[...], vp_ref[...], vc_ref[...],
                                km_ref[...], vm_ref[...], tuple(sk_ref[h] for h in range(ATT_Q_HEADS)),
                                pl.program_id(0)).astype(BF16)

    return _call(body, name="attn_fwd", grid=(nb,), in_specs=_attn_specs(nb, False), out_specs=_row_spec(D_MODEL),
                 out_shape=jax.ShapeDtypeStruct((nb * BLOCK, D_MODEL), BF16), sem=("parallel",))(*([proj] * 8), sinks)


def attn_bwd(da, proj, sinks):
    nb = proj.shape[0] // BLOCK
    last = nb - 1
    wide = pl.BlockSpec((BLOCK, D_MODEL), lambda t: (jnp.minimum(t, last), 0))
    done = pl.BlockSpec((BLOCK, KV_WIDTH), lambda t: (jnp.maximum(t - 1, 0), 0))
    meta = _full((N_META, KV_WIDTH))
    par = _full((ATT_Q_HEADS, 1, 1))

    def body(da_ref, q_ref, z_ref, kp_ref, kc_ref, vp_ref, vc_ref, km_ref, vm_ref, sk_ref,
             dq_ref, dz_ref, dk_ref, dv_ref, dkm_ref, dvm_ref, dsk_ref, ck_ref, cv_ref):
        t = pl.program_id(0)

        @pl.when(t == 0)
        def _():
            ck_ref[...] = jnp.zeros_like(ck_ref)
            cv_ref[...] = jnp.zeros_like(cv_ref)
            dkm_ref[...] = jnp.zeros_like(dkm_ref)
            dvm_ref[...] = jnp.zeros_like(dvm_ref)
            dsk_ref[...] = jnp.zeros_like(dsk_ref)

        @pl.when(t < nb)
        def _():
            def f(q, z, kp, kc, vp, vc, km, vm, sk):
                return _attn_rows(q, z, kp, kc, vp, vc, km, vm, sk, t)

            _, vjp = jax.vjp(f, q_ref[...], z_ref[...], kp_ref[...], kc_ref[...], vp_ref[...], vc_ref[...],
                             km_ref[...], vm_ref[...], tuple(sk_ref[h] for h in range(ATT_Q_HEADS)))
            dq, dz, dkp, dkc, dvp, dvc, dkm, dvm, dsk = vjp(da_ref[...])
            dq_ref[...] = dq.astype(BF16)
            dz_ref[...] = dz.astype(BF16)
            for h in range(ATT_Q_HEADS):
                dsk_ref[h] += dsk[h]
            dk_ref[...] = ck_ref[...] + dkp
            dv_ref[...] = cv_ref[...] + dvp
            ck_ref[...] = dkc
            cv_ref[...] = dvc
            dkm_ref[...] += dkm
            dvm_ref[...] += dvm

        @pl.when(t == nb)
        def _():
            dk_ref[...] = ck_ref[...]
            dv_ref[...] = cv_ref[...]

    rows = nb * BLOCK
    return _call(body, name="attn_bwd", grid=(nb + 1,), in_specs=[wide] + _attn_specs(nb, True),
                 out_specs=[wide, wide, done, done, meta, meta, par],
                 out_shape=[jax.ShapeDtypeStruct((rows, D_MODEL), BF16), jax.ShapeDtypeStruct((rows, D_MODEL), BF16),
                            jax.ShapeDtypeStruct((rows, KV_WIDTH), F32), jax.ShapeDtypeStruct((rows, KV_WIDTH), F32),
                            jax.ShapeDtypeStruct((N_META, KV_WIDTH), F32), jax.ShapeDtypeStruct((N_META, KV_WIDTH), F32),
                            jax.ShapeDtypeStruct(sinks.shape, F32)],
                 scratch=[pltpu.VMEM((BLOCK, KV_WIDTH), F32), pltpu.VMEM((BLOCK, KV_WIDTH), F32)],
                 sem=("arbitrary",))(da, *([proj] * 8), sinks)


XBC_BLOCK0 = SEG["xbc"][2] // D_MODEL
CONV_COL_BLOCKS = CONV_DIM // D_MODEL
DT_TILE = SEG["dt"][2] // LANES


HALO = 8


def _conv_rows(length):
    return 544 if length % 544 == 0 else BLOCK


def _shift_rows(cur, before, j):
    if j == 0:
        return cur
    n = cur.shape[0]
    row = lax.broadcasted_iota(jnp.int32, cur.shape, 0)
    head = pltpu.roll(before, j, 0)
    if n > HALO:
        head = jnp.concatenate([head, jnp.zeros((n - HALO, cur.shape[1]), cur.dtype)], axis=0)
    return jnp.where(row >= j, pltpu.roll(cur, j, 0), head)


def _conv_pre(cur, before, w_ref, b_ref):
    pre = b_ref[...] + w_ref[CONV_WIDTH - 1:CONV_WIDTH, :] * cur
    for k in range(CONV_WIDTH - 1):
        pre = pre + w_ref[k:k + 1, :] * _shift_rows(cur, before, CONV_WIDTH - 1 - k)
    return pre


def _conv_specs(steps, rows, col0=0):
    first = XBC_BLOCK0 + col0
    halos = rows // HALO
    cur = pl.BlockSpec((rows, D_MODEL), lambda j, i: (i, first + j))
    before = pl.BlockSpec((HALO, D_MODEL), lambda j, i: (jnp.maximum(i * halos - 1, 0), first + j))
    after = pl.BlockSpec((HALO, D_MODEL), lambda j, i: (jnp.minimum(i + 1, steps - 1) * halos, first + j))
    return cur, before, after


def _valid_rows(i, rows):
    row = lax.broadcasted_iota(jnp.int32, (rows, D_MODEL), 0)
    return jnp.maximum((row >= PAD_ROWS).astype(F32), jnp.where(i > 0, 1.0, 0.0))


def conv_fwd(proj, conv_w, conv_b):
    rows = _conv_rows(proj.shape[0])
    steps = proj.shape[0] // rows
    cur, before, _ = _conv_specs(steps, rows)

    def body(c_ref, p_ref, w_ref, b_ref, o_ref):
        i = pl.program_id(1)
        pre = _conv_pre(c_ref[...], p_ref[...] * jnp.where(i > 0, 1.0, 0.0), w_ref, b_ref)
        o_ref[...] = _silu(pre) * _valid_rows(i, rows)

    return _call(body, name="conv_fwd", grid=(CONV_COL_BLOCKS, steps),
                 in_specs=[cur, before, pl.BlockSpec((CONV_WIDTH, D_MODEL), lambda j, i: (0, j)),
                           pl.BlockSpec((1, D_MODEL), lambda j, i: (0, j))],
                 out_specs=pl.BlockSpec((rows, D_MODEL), lambda j, i: (i, j)),
                 out_shape=jax.ShapeDtypeStruct((proj.shape[0], CONV_DIM), F32),
                 sem=("parallel", "parallel"))(proj, proj, conv_w, conv_b)


def conv_bwd(name, dparts, col0, proj, conv_w, conv_b):
    rows = _conv_rows(proj.shape[0])
    steps = proj.shape[0] // rows
    last = steps - 1
    ncol = sum(d.shape[1] for d in dparts) // D_MODEL
    np_ = len(dparts)
    cur, before, after = _conv_specs(steps, rows, col0)
    dcur = [pl.BlockSpec((rows, d.shape[1] // ncol), lambda j, i: (i, j)) for d in dparts]
    dafter = [pl.BlockSpec((HALO, d.shape[1] // ncol), lambda j, i: (jnp.minimum(i + 1, last) * (rows // HALO), j))
              for d in dparts]
    out_cur = pl.BlockSpec((rows, D_MODEL), lambda j, i: (i, j))
    wspec = pl.BlockSpec((CONV_WIDTH, D_MODEL), lambda j, i: (0, col0 + j))
    bspec = pl.BlockSpec((1, D_MODEL), lambda j, i: (0, col0 + j))
    wout = pl.BlockSpec((CONV_WIDTH, D_MODEL), lambda j, i: (0, j))
    bout = pl.BlockSpec((1, D_MODEL), lambda j, i: (0, j))

    def body(*refs):
        dc_refs, da_refs = refs[:np_], refs[np_:2 * np_]
        c_ref, p_ref, a_ref, w_ref, b_ref, du_ref, dw_ref, db_ref = refs[2 * np_:]
        i = pl.program_id(1)
        row = lax.broadcasted_iota(jnp.int32, (rows, D_MODEL), 0)
        curv = c_ref[...]
        beforev = p_ref[...] * jnp.where(i > 0, 1.0, 0.0)
        side_by_side = lambda rs: rs[0][...] if np_ == 1 else jnp.concatenate([r[...] for r in rs], axis=1)

        def dpre_of(pre, d):
            s = _sigmoid(pre)
            return d * (s * (1.0 + pre * (1.0 - s)))

        dp_c = dpre_of(_conv_pre(curv, beforev, w_ref, b_ref), side_by_side(dc_refs) * _valid_rows(i, rows))
        dp_a = dpre_of(_conv_pre(a_ref[...], curv[rows - HALO:], w_ref, b_ref),
                       side_by_side(da_refs) * jnp.where(i < last, 1.0, 0.0))
        du = w_ref[CONV_WIDTH - 1:CONV_WIDTH, :] * dp_c
        for j in range(1, CONV_WIDTH):
            tail = jnp.concatenate([jnp.zeros((rows - HALO, D_MODEL), F32), pltpu.roll(dp_a, HALO - j, 0)], axis=0)
            up = jnp.where(row < rows - j, pltpu.roll(dp_c, rows - j, 0), tail)
            du = du + w_ref[CONV_WIDTH - 1 - j:CONV_WIDTH - j, :] * up
        du_ref[...] = du.astype(BF16)

        @pl.when(i == 0)
        def _():
            dw_ref[...] = jnp.zeros_like(dw_ref)
            db_ref[...] = jnp.zeros_like(db_ref)

        for k in range(CONV_WIDTH):
            dw_ref[k:k + 1, :] += jnp.sum(dp_c * _shift_rows(curv, beforev, CONV_WIDTH - 1 - k), axis=0, keepdims=True)
        db_ref[...] += jnp.sum(dp_c, axis=0, keepdims=True)

    width = ncol * D_MODEL
    return _call(body, name=name, grid=(ncol, steps),
                 in_specs=dcur + dafter + [cur, before, after, wspec, bspec], out_specs=[out_cur, wout, bout],
                 out_shape=[jax.ShapeDtypeStruct((proj.shape[0], width), BF16),
                            jax.ShapeDtypeStruct((CONV_WIDTH, width), F32), jax.ShapeDtypeStruct((1, width), F32)],
                 sem=("parallel", "arbitrary"))(*dparts, *dparts, proj, proj, proj, conv_w, conv_b)


def _head_expand():
    e = np.zeros((LANES, SSM_INNER), np.float32)
    for h in range(SSM_HEADS):
        e[h, h * HEAD_DIM:(h + 1) * HEAD_DIM] = 1.0
    return jnp.asarray(e, dtype=BF16)


def _softplus(x):
    return jnp.maximum(x, 0.0) + jnp.log(1.0 + jnp.exp(-jnp.abs(x)))


def _bf16_parts(x):
    hi = x.astype(BF16)
    rest = x - hi.astype(F32)
    mid = rest.astype(BF16)
    return hi, mid, (rest - mid.astype(F32)).astype(BF16)


@jax.custom_vjp
def _times_01(x, m):
    return sum(jnp.dot(p, m, preferred_element_type=F32) for p in _bf16_parts(x))


def _times_01_bwd(m, g):
    return sum(lax.dot_general(p, m, _NT, preferred_element_type=F32) for p in _bf16_parts(g)), jnp.zeros_like(m)


_times_01.defvjp(lambda x, m: (_times_01(x, m), m), _times_01_bwd)


def _causal_ones():
    l = lax.broadcasted_iota(jnp.int32, (BLOCK, BLOCK), 0)
    s = lax.broadcasted_iota(jnp.int32, (BLOCK, BLOCK), 1)
    return (l >= s).astype(BF16)


@jax.custom_vjp
def _cumsum_rows(a):
    return sum(jnp.dot(_causal_ones(), p, preferred_element_type=F32) for p in _bf16_parts(a))


def _cumsum_rows_bwd(_, g):
    tn = (((0,), (0,)), ((), ()))
    return (sum(lax.dot_general(_causal_ones(), p, tn, preferred_element_type=F32) for p in _bf16_parts(g)),)


_cumsum_rows.defvjp(lambda a: (_cumsum_rows(a), None), _cumsum_rows_bwd)


def _ssd_heads(dt_tile, bias, alog, dsk, expand):
    dt = _softplus(dt_tile + bias)
    a = dt * (-jnp.exp(alog))
    one_row = lambda v: jnp.broadcast_to(v, (HALO, LANES))
    return _times_01(jnp.concatenate([dt, _cumsum_rows(a), one_row(jnp.sum(a, axis=0, keepdims=True)), one_row(dsk)],
                                     axis=0), expand)


HEADS_ROWS = 2 * BLOCK + 2 * HALO


def _ssd_group(xs, per_lane, bg, cg, state):
    l = lax.broadcasted_iota(jnp.int32, (BLOCK, BLOCK), 0)
    s = lax.broadcasted_iota(jnp.int32, (BLOCK, BLOCK), 1)
    causal = l >= s
    first_head = s < HEAD_DIM
    dtx, cs = per_lane[0:BLOCK], per_lane[BLOCK:2 * BLOCK]
    tot, dsk = per_lane[2 * BLOCK:2 * BLOCK + 1], per_lane[2 * BLOCK + HALO:2 * BLOCK + HALO + 1]
    bb, cb16 = bg.astype(BF16), cg.astype(BF16)
    cb = lax.dot_general(cb16, bb, _NT, preferred_element_type=F32)
    xr = xs * dtx
    y_diag = []
    for p in range(GROUP_W // LANES):
        lanes = slice(p * LANES, (p + 1) * LANES)
        c_pair = cs[:, lanes]
        c_swap = _swap_halves(c_pair)
        m = []
        for c_head in (jnp.where(first_head, c_pair, c_swap), jnp.where(first_head, c_swap, c_pair)):
            m.append(cb * jnp.exp(jnp.where(causal, c_head - c_head.T, -jnp.inf)))
        x_pair = xr[:, lanes]
        x_diag = jnp.concatenate([jnp.where(first_head, x_pair, 0.0), jnp.where(first_head, 0.0, x_pair)], axis=0)
        y_diag.append(jnp.dot(jnp.concatenate(m, axis=1).astype(BF16), x_diag.astype(BF16),
                              preferred_element_type=F32))
    st = lax.dot_general(bb, (xr * jnp.exp(tot - cs)).astype(BF16), (((0,), (0,)), ((), ())),
                         preferred_element_type=F32)
    new_state = state * jnp.exp(tot) + st
    y_off = jnp.dot(cb16, state.astype(BF16), preferred_element_type=F32) * jnp.exp(cs)
    return jnp.concatenate(y_diag, axis=1) + y_off + dsk * xs, new_state


BC_WIDTH = SSM_GROUPS * SSM_STATE


def _ssd_specs(chunk):
    xs = pl.BlockSpec((BLOCK, SSM_INNER), lambda c: (chunk(c), 0))
    dt = pl.BlockSpec((BLOCK, LANES), lambda c: (chunk(c), DT_TILE))
    expand = _full((LANES, SSM_INNER))
    b = pl.BlockSpec((BLOCK, BC_WIDTH), lambda c: (chunk(c), SSM_INNER // BC_WIDTH))
    cc = pl.BlockSpec((BLOCK, BC_WIDTH), lambda c: (chunk(c), SSM_INNER // BC_WIDTH + 1))
    par = _full((1, LANES))
    state = pl.BlockSpec((None, SSM_STATE, SSM_INNER), lambda c: (chunk(c), 0, 0))
    return xs, dt, expand, b, cc, par, state


def _group_lanes(g):
    return slice(g * GROUP_W, (g + 1) * GROUP_W), slice(g * SSM_STATE, (g + 1) * SSM_STATE)


def ssd_fwd(xbc, proj, expand, bias, alog, dsk):
    nb = xbc.shape[0] // BLOCK
    xs, dt, ex, b, cc, par, state = _ssd_specs(lambda c: c)

    def body(x_ref, dt_ref, e_ref, bi_ref, al_ref, dk_ref, b_ref, c_ref, y_ref, sp_ref, st_ref):
        @pl.when(pl.program_id(0) == 0)
        def _():
            st_ref[...] = jnp.zeros_like(st_ref)

        per_lane = _ssd_heads(dt_ref[...], bi_ref[...], al_ref[...], dk_ref[...], e_ref[...])
        for g in range(SSM_GROUPS):
            wide, tile = _group_lanes(g)
            entering = st_ref[:, wide]
            sp_ref[:, wide] = entering
            y_ref[:, wide], st_ref[:, wide] = _ssd_group(x_ref[:, wide], per_lane[:, wide], b_ref[:, tile],
                                                         c_ref[:, tile], entering)

    return _call(body, name="ssd_fwd", grid=(nb,), in_specs=[xs, dt, ex, par, par, par, b, cc],
                 out_specs=[xs, state],
                 out_shape=[jax.ShapeDtypeStruct((nb * BLOCK, SSM_INNER), F32),
                            jax.ShapeDtypeStruct((nb, SSM_STATE, SSM_INNER), F32)],
                 scratch=[pltpu.VMEM((SSM_STATE, SSM_INNER), F32)],
                 sem=("arbitrary",))(xbc, proj, expand, bias, alog, dsk, xbc, xbc)


def ssd_bwd(dy, xbc, proj, expand, bias, alog, dsk, states, comm):
    nb = xbc.shape[0] // BLOCK
    last = nb - 1
    xs, dt, ex, b, cc, par, state = _ssd_specs(lambda c: last - c)
    tile = pl.BlockSpec((BLOCK, LANES), lambda c: (last - c, 0))
    nspec = pl.BlockSpec((BLOCK, BC_WIDTH), lambda c: (last - c, 0))

    def body(dy_ref, x_ref, dt_ref, e_ref, bi_ref, al_ref, dk_ref, b_ref, c_ref, sp_ref,
             dx_ref, ddt_ref, db_ref, dc_ref, dbi_ref, dal_ref, ddk_ref, ds_ref):
        @pl.when(pl.program_id(0) == 0)
        def _():
            ds_ref[...] = jnp.zeros_like(ds_ref)
            dbi_ref[...] = jnp.zeros_like(dbi_ref)
            dal_ref[...] = jnp.zeros_like(dal_ref)
            ddk_ref[...] = jnp.zeros_like(ddk_ref)

        expand = e_ref[...]
        per_lane, heads_vjp = jax.vjp(lambda t, bi, al, dk: _ssd_heads(t, bi, al, dk, expand), dt_ref[...], bi_ref[...],
                                      al_ref[...], dk_ref[...])
        d_per_lane = []
        for g in range(SSM_GROUPS):
            wide, tile_lanes = _group_lanes(g)
            _, vjp = jax.vjp(_ssd_group, x_ref[:, wide], per_lane[:, wide], b_ref[:, tile_lanes], c_ref[:, tile_lanes],
                             sp_ref[:, wide])
            (dx_ref[:, wide], d_lanes, db_ref[:, tile_lanes], dc_ref[:, tile_lanes],
             ds_ref[:, wide]) = vjp((dy_ref[:, wide], ds_ref[:, wide]))
            d_per_lane.append(d_lanes)
        ddt, dbi, dal, ddk = heads_vjp(jnp.concatenate(d_per_lane, axis=1))
        ddt_ref[...] = ddt.astype(BF16)
        dbi_ref[...] += dbi
        dal_ref[...] += dal
        ddk_ref[...] += ddk

    def at():
        c = pl.program_id(0)
        return c == 0, c == 0, c == last

    par_shape = jax.ShapeDtypeStruct((1, LANES), F32)
    return _call_with_comm(
        body, comm, at, (dy, xbc, proj, expand, bias, alog, dsk, xbc, xbc, states), name="ssd_bwd",
        grid=(nb,), in_specs=[xs, xs, dt, ex, par, par, par, b, cc, state],
        out_specs=[xs, tile, nspec, nspec, par, par, par],
        out_shape=[jax.ShapeDtypeStruct((nb * BLOCK, SSM_INNER), F32), jax.ShapeDtypeStruct((nb * BLOCK, LANES), BF16),
                   jax.ShapeDtypeStruct((nb * BLOCK, BC_WIDTH), F32), jax.ShapeDtypeStruct((nb * BLOCK, BC_WIDTH), F32),
                   par_shape, par_shape, par_shape],
        scratch=[pltpu.VMEM((SSM_STATE, SSM_INNER), F32)])


SLAB_ROWS = 24
SLAB_META_ROW = 8


SLAB_LOSS_ROW = 7


def pack_small(dcw, dcb, dgpre, dgpost, dbias, dalog, ddsk, dsinks, dgn, dmeta, loss_tile):
    def body(cw, cb, gpre, gpost, dtb, al, dk, sk, gn, meta, loss, o_ref):
        o_ref[...] = jnp.zeros_like(o_ref)
        o_ref[SLAB_LOSS_ROW:SLAB_LOSS_ROW + 1, 0:LANES] = loss[0:1, :]
        o_ref[0:CONV_WIDTH, :] = cw[...]
        o_ref[4:5, :] = cb[...]
        o_ref[5:6, 0:1024] = gpre[...]
        o_ref[5:6, 1024:2048] = gpost[...]
        o_ref[5:6, 2048:2176] = dtb[...]
        o_ref[5:6, 2176:2304] = al[...]
        o_ref[5:6, 2304:2432] = dk[...]
        o_ref[5:6, 2432:2560] = sk[...]
        o_ref[6:7, 0:SSM_INNER] = gn[...]
        o_ref[SLAB_META_ROW:SLAB_META_ROW + N_META, 0:D_MODEL] = meta[...]

    args = (dcw, dcb, dgpre, dgpost, dbias, dalog, ddsk, dsinks, dgn, dmeta, loss_tile)
    return _call(body, name="pack_small", in_specs=[_full(a.shape) for a in args],
                 out_specs=_full((SLAB_ROWS, CONV_DIM)), out_shape=jax.ShapeDtypeStruct((SLAB_ROWS, CONV_DIM), F32))(*args)


def _lane_tile(v):
    return jnp.pad(v, ((0, 0), (0, LANES - v.shape[1])))


def kernel(x, meta_tokens, g_pre, w_in, conv_w, conv_b, dt_bias, a_log, d_skip, attn_sinks, g_ssm_norm, w_out_att, w_out_ssm, w_out, g_post, loss_target, m_meta_tokens, m_g_pre, m_w_in, m_conv_w, m_conv_b, m_dt_bias, m_a_log, m_d_skip, m_attn_sinks, m_g_ssm_norm, m_w_out_att, m_w_out_ssm, m_w_out, m_g_post, v_meta_tokens, v_g_pre, v_w_in, v_conv_w, v_conv_b, v_dt_bias, v_a_log, v_d_skip, v_attn_sinks, v_g_ssm_norm, v_w_out_att, v_w_out_ssm, v_w_out, v_g_post):
    chip = _chip_index()

    conv_w_rows = jnp.pad(conv_w[0], ((0, 2 * 8 - CONV_WIDTH), (0, 0)))
    w_in_t, m_w_in_t, v_w_in_t = w_in[0].T, m_w_in[0].T, v_w_in[0].T
    gathered_w_in, g_conv_w, g_meta = run_comm("gather_w_in",
                                               TwoLevelGather([pack_w_in(w_in_t), conv_w_rows, meta_tokens]))
    w_all_t = unpack_w_in(gathered_w_in)
    cw_full = g_conv_w[:, :CONV_WIDTH].transpose(1, 0, 2).reshape(CONV_WIDTH, CONV_DIM)
    meta_full = g_meta.transpose(1, 0, 2).reshape(N_META, D_MODEL)
    behind_w_in = 0.0 * g_meta[0, 0, 0]
    w_out_flight, w_out_started = chip_exchange_start(
        "gather_w_out_start", [(w[0] + behind_w_in).astype(BF16) for w in (w_out_att, w_out_ssm, w_out)], False)

    h, u = prep(x, meta_full, g_pre)
    proj = project("in_proj", u, w_all_t, w_out_started)

    sinks3 = attn_sinks.reshape(ATT_Q_HEADS, 1, 1)
    a_att = attn_fwd(proj, sinks3)

    xbc = conv_fwd(proj, cw_full, conv_b)
    expand = _head_expand()
    head_pars = (_lane_tile(dt_bias), _lane_tile(a_log), _lane_tile(d_skip))
    y_ssd, states = ssd_fwd(xbc, proj, expand, *head_pars)
    woa, wos, wo = [g.reshape(-1, D_MODEL) for g in chip_exchange_wait("gather_w_out_wait", w_out_flight, False, y_ssd)]

    (yn, merged, dout, dy_att, dy_ssm, da_att, dy_ssd, dz_ssm, dga, dgs, dres, loss_tile, dg_post, dgn) = tail(
        y_ssd, proj, a_att, x, loss_target, woa, wos, wo, g_ssm_norm, g_post)

    dwo = mm_tn("out_proj_dw", merged, dout)
    dwoa = mm_tn("att_out_dw", a_att, dy_att)
    dwos = mm_tn("ssm_out_dw", yn, dy_ssm)
    dq, dz_att, dk, dv, dkmeta, dvmeta, dsinks3 = attn_bwd(da_att, proj, sinks3)
    dk = dk.at[PAD_ROWS:BLOCK].add(dkmeta).astype(BF16)
    dv = dv.at[PAD_ROWS:BLOCK].add(dvmeta).astype(BF16)

    def pieces(g):
        return g.reshape(4, 2, g.shape[0] // 8, g.shape[1])

    def to_owner(g):
        return (lambda ref, dev: ref.at[_chip_of(dev), dev[2]], (g.shape[0] // 8, g.shape[1]))

    (dxs, ddt_tile, dbg, dcg, dbias, dalog, ddsk), sent_w_out = ssd_bwd(
        dy_ssd, xbc, proj, expand, *head_pars, states,
        DirectExchange([pieces(dwoa), pieces(dwos), pieces(dwo)], [to_owner(dwoa), to_owner(dwos), to_owner(dwo)],
                       ALL_MASKS, "dev", 8))
    dxs_raw, dcw_xs, dcb_xs = conv_bwd("conv_bwd_x", [dxs], 0, proj, cw_full, conv_b)
    dbc_raw, dcw_bc, dcb_bc = conv_bwd("conv_bwd_bc", [dbg, dcg], SSM_INNER // D_MODEL, proj, cw_full, conv_b)
    dcw = jnp.concatenate([dcw_xs, dcw_bc], axis=1)
    dcb = jnp.concatenate([dcb_xs, dcb_bc], axis=1)

    narrow = jnp.concatenate([dk, dv, ddt_tile, jnp.zeros((dk.shape[0], N_ACT - N_ALIGNED), BF16)], axis=1)
    dproj = [dz_ssm, dxs_raw, dbc_raw, dq, dz_att, dga, dgs, narrow]
    dw_all_t = weight_grad_t("in_proj_dw", dproj, u)

    half_rows = PACK_W // 2
    partial = pack_grad_w_in(dw_all_t).reshape(4, 2, half_rows, D_MODEL)
    from_sibling, = run_comm("pair_grads", DirectExchange(
        [partial], [(lambda ref, dev: ref.at[pl.ds(0, 4), dev[2]], (4, half_rows, D_MODEL))], SIBLING_MASK, "core", 2,
        keep_own=False))
    chip_sum = sum_pair(partial, from_sibling)
    grads_flight, started = chip_exchange_start("reduce_w_in_start", [chip_sum], True)
    du = project_back("in_proj_dx", dproj, w_all_t, started)
    grad_x, dmeta, dg_pre = prep_bwd(h, du, dres, g_pre)

    halves = [sum_slots("sum_" + nm, r) for nm, r in zip(("w_out_att", "w_out_ssm", "w_out"), sent_w_out)]
    shared = run_comm("share_w_out", DirectExchange(halves, [None] * 3, SIBLING_MASK, "core", 2))
    g_woa, g_wos, g_wo = [f.reshape(2 * f.shape[1], f.shape[2]) for f in shared]
    d_woa, nm_woa, nv_woa = adamw_rows("adamw_w_out_att", g_woa, w_out_att[0], m_w_out_att[0], v_w_out_att[0])
    d_wos, nm_wos, nv_wos = adamw_rows("adamw_w_out_ssm", g_wos, w_out_ssm[0], m_w_out_ssm[0], v_w_out_ssm[0])
    d_wo, nm_wo, nv_wo = adamw_rows("adamw_w_out", g_wo, w_out[0], m_w_out[0], v_w_out[0])

    sent_w_in, = chip_exchange_wait("reduce_w_in_wait", grads_flight, True, d_wos)
    slab = pack_small(dcw, dcb, dg_pre, dg_post, dbias, dalog, ddsk, _lane_tile(dsinks3.reshape(1, ATT_Q_HEADS)), dgn,
                      dmeta, loss_tile)
    shared_w_in, slabs = run_comm("share_w_in", Both(
        DirectExchange([sum_slots("sum_w_in", sent_w_in)], [None], SIBLING_MASK, "core", 2),
        DirectExchange([slab], [None], ALL_MASKS, "dev", 8)))
    small = sum_slots("sum_small", slabs)
    loss = small[SLAB_LOSS_ROW, 0]
    g_w_in, d_w_in, nm_w_in, nv_w_in = [
        a.T for a in adamw_w_in(shared_w_in.reshape(PACK_W, D_MODEL), w_in_t, m_w_in_t, v_w_in_t)]

    cw_cols = CONV_DIM // 4
    meta_cols = D_MODEL // 4
    g_small = {
        "meta_tokens": lax.dynamic_slice(small, (SLAB_META_ROW, chip * meta_cols), (N_META, meta_cols)),
        "g_pre": small[5:6, 0:1024],
        "conv_w": lax.dynamic_slice(small, (0, chip * cw_cols), (CONV_WIDTH, cw_cols)),
        "conv_b": small[4:5, :],
        "dt_bias": small[5:6, 2048:2048 + SSM_HEADS],
        "a_log": small[5:6, 2176:2176 + SSM_HEADS],
        "d_skip": small[5:6, 2304:2304 + SSM_HEADS],
        "attn_sinks": small[5:6, 2432:2432 + ATT_Q_HEADS],
        "g_ssm_norm": small[6:7, 0:SSM_INNER],
        "g_post": small[5:6, 1024:2048],
    }
    names = list(g_small)
    w_small = dict(meta_tokens=meta_tokens, g_pre=g_pre, conv_w=conv_w[0], conv_b=conv_b, dt_bias=dt_bias, a_log=a_log,
                   d_skip=d_skip, attn_sinks=attn_sinks, g_ssm_norm=g_ssm_norm, g_post=g_post)
    m_small = dict(meta_tokens=m_meta_tokens, g_pre=m_g_pre, conv_w=m_conv_w[0], conv_b=m_conv_b, dt_bias=m_dt_bias,
                   a_log=m_a_log, d_skip=m_d_skip, attn_sinks=m_attn_sinks, g_ssm_norm=m_g_ssm_norm, g_post=m_g_post)
    v_small = dict(meta_tokens=v_meta_tokens, g_pre=v_g_pre, conv_w=v_conv_w[0], conv_b=v_conv_b, dt_bias=v_dt_bias,
                   a_log=v_a_log, d_skip=v_d_skip, attn_sinks=v_attn_sinks, g_ssm_norm=v_g_ssm_norm, g_post=v_g_post)
    upd = dict(zip(names, adamw_small([g_small[k] for k in names], [w_small[k] for k in names],
                                      [m_small[k] for k in names], [v_small[k] for k in names])))

    lead = {"conv_w"}

    def shaped(name, a):
        return a[None] if name in lead else a

    grads = dict(g_small, w_in=g_w_in, w_out_att=g_woa, w_out_ssm=g_wos, w_out=g_wo)
    deltas = dict({k: upd[k][0] for k in names}, w_in=d_w_in, w_out_att=d_woa, w_out_ssm=d_wos, w_out=d_wo)
    new_m = dict({k: upd[k][1] for k in names}, w_in=nm_w_in, w_out_att=nm_woa, w_out_ssm=nm_wos, w_out=nm_wo)
    new_v = dict({k: upd[k][2] for k in names}, w_in=nv_w_in, w_out_att=nv_woa, w_out_ssm=nv_wos, w_out=nv_wo)
    lead |= {"w_in", "w_out_att", "w_out_ssm", "w_out"}
    order = ["meta_tokens", "g_pre", "w_in", "conv_w", "conv_b", "dt_bias", "a_log", "d_skip", "attn_sinks",
             "g_ssm_norm", "w_out_att", "w_out_ssm", "w_out", "g_post"]
    outs = [loss, grad_x]
    for group in (grads, deltas, new_m, new_v):
        outs += [shaped(k, group[k]) for k in order]
    return tuple(outs)
```

```python
import functools

import numpy as np
import jax
import jax.numpy as jnp
from jax import lax
from jax.experimental import pallas as pl
from jax.experimental.pallas import tpu as pltpu

F32 = jnp.float32
BF16 = jnp.bfloat16
HI = lax.Precision.HIGHEST

D_MODEL = 1024
N_META = 16
BLOCK = 128
PAD_ROWS = BLOCK - N_META
NORM_EPS = 1e-6
HEAD_DIM = 64
ATT_Q_HEADS = 16
ATT_KV_HEADS = 4
ATT_GROUP = 4
SSM_INNER = 2048
SSM_HEADS = 32
SSM_GROUPS = 4
SSM_HEADS_PER_GROUP = 8
SSM_STATE = 128
CONV_WIDTH = 4
CONV_DIM = 3072
LANES = 128

ADAM_LR = 0.001
ADAM_B1 = 0.9
ADAM_B2 = 0.999
ADAM_EPS = 1e-08
ADAM_WD = 0.01
ADAM_STEP = 10

VMEM_LIMIT = 48 * 1024 * 1024

SHARD_W = 2440
PACK_W = 2560
SHARD_STRIDE = 2432
N_ALIGNED = 9856
N_ACT = 10240
SEG = {
    "q": (0, 1024, 5120), "k": (1024, 256, 9216), "v": (1280, 256, 9472), "z_att": (1536, 1024, 6144),
    "z_ssm": (2560, 2048, 0), "xbc": (4608, 3072, 2048), "dt": (7680, 128, 9728),
    "gate_att": (7808, 1024, 7168), "gate_ssm": (8832, 1024, 8192),
}
DT_STORED_START = 7680
DT_PAD = LANES - SSM_HEADS


def _act_col(aligned_col):
    for a0, w, p0 in SEG.values():
        if a0 <= aligned_col < a0 + w:
            return p0 + aligned_col - a0
    raise ValueError(aligned_col)


def _call(body, *, name, out_shape, in_specs, out_specs, grid=(), scratch=(), sem=None, aliases=None):
    return pl.pallas_call(
        body, out_shape=out_shape, grid=grid, in_specs=in_specs, out_specs=out_specs, scratch_shapes=list(scratch),
        name=name, input_output_aliases=aliases or {},
        compiler_params=pltpu.CompilerParams(dimension_semantics=sem, vmem_limit_bytes=VMEM_LIMIT))


def _full(shape):
    n = len(shape)
    return pl.BlockSpec(shape, lambda *_: (0,) * n)


def _chip_index():
    return lax.axis_index("x") * 2 + lax.axis_index("y")


_sigmoid = jax.nn.sigmoid


def _silu(z):
    return z * _sigmoid(z)


def _rms(x, g):
    return x * lax.rsqrt(jnp.mean(x * x, axis=-1, keepdims=True) + NORM_EPS) * g


def _peer(mask):
    x, y, c = lax.axis_index("x"), lax.axis_index("y"), lax.axis_index("c")
    return ((1 - x) if mask & 4 else x, (1 - y) if mask & 2 else y, (1 - c) if mask & 1 else c)


def _me():
    return lax.axis_index("x"), lax.axis_index("y"), lax.axis_index("c")


def _chip_of(dev):
    return 2 * dev[0] + dev[1]


CHIP_MASKS = (4, 2, 6)
ALL_MASKS = (1, 2, 3, 4, 5, 6, 7)
SIBLING_MASK = (1,)


def _remote(src, dst, send_sem, recv_sem, dev):
    return pltpu.make_async_remote_copy(src_ref=src, dst_ref=dst, send_sem=send_sem, recv_sem=recv_sem,
                                        device_id=dev, device_id_type=pl.DeviceIdType.MESH)


class _StagedCopy:
    def __init__(self, src, stage, dst, load_sem, store_sem):
        self.load = pltpu.make_async_copy(src, stage, load_sem)
        self.store = pltpu.make_async_copy(stage, dst, store_sem)

    def start(self):
        self.load.start()
        self.load.wait()
        self.store.start()

    def wait(self):
        self.store.wait()


class DirectExchange:
    def __init__(self, arrays, pieces, masks, slot_kind, nslots, keep_own=True):
        self.arrays, self.pieces, self.masks, self.slot_kind = list(arrays), list(pieces), masks, slot_kind
        self.keep_own = keep_own
        n, nk = len(arrays), len(masks)
        shapes = [a.shape if p is None else p[1] for a, p in zip(arrays, pieces)]
        self.out_shape = [jax.ShapeDtypeStruct((nslots,) + tuple(s), a.dtype) for s, a in zip(shapes, arrays)]
        self.scratch = [pltpu.SemaphoreType.DMA((n * nk,)), pltpu.SemaphoreType.DMA((n * nk,))]
        if keep_own:
            self.scratch += [pltpu.SemaphoreType.DMA((2 * n,))] + [pltpu.VMEM(s, a.dtype) for s, a in zip(shapes, arrays)]
        self.has_mid = False

    def _copies(self, ins, outs, scratch):
        send_sems, recv_sems = scratch[:2]
        me = _me()
        slot = {"chip": _chip_of(me), "dev": 4 * me[0] + 2 * me[1] + me[2], "core": me[2]}[self.slot_kind]
        nk = len(self.masks)

        def piece(a, dev):
            return ins[a] if self.pieces[a] is None else self.pieces[a][0](ins[a], dev)

        local = []
        if self.keep_own:
            local_sems, stages = scratch[2], scratch[3:]
            local = [_StagedCopy(piece(a, me), stages[a], outs[a].at[slot], local_sems.at[2 * a], local_sems.at[2 * a + 1])
                     for a in range(len(ins))]
        remote = []
        for a in range(len(ins)):
            for ki, mask in enumerate(self.masks):
                dev = _peer(mask)
                remote.append(_remote(piece(a, dev), outs[a].at[slot], send_sems.at[a * nk + ki],
                                      recv_sems.at[a * nk + ki], dev))
        return local, remote

    def start(self, ins, outs, scratch):
        local, remote = self._copies(ins, outs, scratch)
        for cp in remote + local:
            cp.start()

    def finish(self, ins, outs, scratch):
        local, remote = self._copies(ins, outs, scratch)
        for cp in remote + local:
            cp.wait()


SPLIT_ROWS = 16


class TwoLevelGather:
    def __init__(self, arrays):
        self.arrays = list(arrays)
        n = len(arrays)
        self.out_shape = [jax.ShapeDtypeStruct((4,) + a.shape, a.dtype) for a in arrays]
        self.scratch = ([pltpu.SemaphoreType.DMA((4 * n,)), pltpu.SemaphoreType.DMA((4 * n,)),
                         pltpu.SemaphoreType.DMA((3 * n,)), pltpu.SemaphoreType.DMA((3 * n,)),
                         pltpu.SemaphoreType.DMA((2 * n,))] + [pltpu.VMEM(a.shape, a.dtype) for a in arrays])
        self.has_mid = True

    def _copies(self, ins, outs, scratch):
        ici_send, ici_recv, fwd_send, fwd_recv, local_sems = scratch[:5]
        stages = scratch[5:]
        me = _me()
        sibling, in_x, in_y, diagonal = _peer(1), _peer(4), _peer(2), _peer(6)
        plan = []
        for a in range(len(ins)):
            half = ins[a].shape[0] // 2
            first = half // 2 if half % (2 * SPLIT_ROWS) == 0 else half
            mine = pl.ds(me[2] * half, half)
            local = _StagedCopy(ins[a], stages[a], outs[a].at[_chip_of(me)], local_sems.at[2 * a], local_sems.at[2 * a + 1])

            def ici(k, src, dst, dev):
                return _remote(src, dst, ici_send.at[4 * a + k], ici_recv.at[4 * a + k], dev)

            def d2d(k, chip):
                zone = outs[a].at[_chip_of(chip), mine]
                return _remote(zone, zone, fwd_send.at[3 * a + k], fwd_recv.at[3 * a + k], sibling)

            own_zone = outs[a].at[_chip_of(me), mine]
            from_x = outs[a].at[_chip_of(in_x), pl.ds(me[2] * half, first)]
            onward = [ici(2, from_x, from_x, in_y), None]
            if first < half:
                from_y = outs[a].at[_chip_of(in_y), pl.ds(me[2] * half + first, half - first)]
                onward[1] = ici(3, from_y, from_y, in_x)
            plan.append(dict(
                local=local,
                own=[ici(0, ins[a].at[mine], own_zone, in_x), ici(1, ins[a].at[mine], own_zone, in_y)],
                onward=onward, sibling=[d2d(0, in_x), d2d(1, in_y), d2d(2, diagonal)]))
        return plan

    def start(self, ins, outs, scratch):
        for p in self._copies(ins, outs, scratch):
            for cp in p["own"]:
                cp.start()
            p["local"].start()

    def mid(self, ins, outs, scratch):
        plan = self._copies(ins, outs, scratch)
        for p in plan:
            for k in range(2):
                p["own"][k].wait_recv()
                if p["onward"][k] is not None:
                    p["onward"][k].start()
                p["sibling"][k].start()
        for p in plan:
            for cp in p["onward"]:
                if cp is not None:
                    cp.wait_recv()
            p["sibling"][2].start()

    def finish(self, ins, outs, scratch):
        for p in self._copies(ins, outs, scratch):
            for cp in p["sibling"]:
                cp.wait_recv()
            for cp in p["own"] + p["sibling"] + [cp for cp in p["onward"] if cp is not None]:
                cp.wait_send()
            p["local"].wait()


class Both:
    def __init__(self, a, b):
        self.a, self.b = a, b
        self.arrays, self.out_shape = a.arrays + b.arrays, a.out_shape + b.out_shape
        self.scratch = a.scratch + b.scratch
        self.has_mid = False
        assert not (a.has_mid or b.has_mid)

    def _parts(self, ins, outs, sems):
        na, sa = len(self.a.arrays), len(self.a.scratch)
        return (ins[:na], outs[:na], sems[:sa]), (ins[na:], outs[na:], sems[sa:])

    def start(self, ins, outs, sems):
        pa, pb = self._parts(ins, outs, sems)
        self.a.start(*pa)
        self.b.start(*pb)

    def finish(self, ins, outs, sems):
        pa, pb = self._parts(ins, outs, sems)
        self.a.finish(*pa)
        self.b.finish(*pb)


_ANY = pl.BlockSpec(memory_space=pl.ANY)


def run_comm(name, comm):
    n = len(comm.arrays)

    def body(*refs):
        ins, outs, sems = refs[:n], refs[n:2 * n], refs[2 * n:]
        comm.start(ins, outs, sems)
        if comm.has_mid:
            comm.mid(ins, outs, sems)
        comm.finish(ins, outs, sems)

    return pl.pallas_call(body, name=name, out_shape=comm.out_shape, in_specs=[_ANY] * n, out_specs=[_ANY] * n,
                          scratch_shapes=comm.scratch,
                          compiler_params=pltpu.CompilerParams(vmem_limit_bytes=VMEM_LIMIT))(*comm.arrays)


_HBM = pl.BlockSpec(memory_space=pltpu.HBM)
_SEM = pl.BlockSpec(memory_space=pltpu.SEMAPHORE)
_SIDE_EFFECT = pltpu.SideEffectType.DATAFLOW_SIDE_EFFECTING


def _chip_copies(srcs, lands, send_sems, recv_sems, by_target):
    me = _me()
    copies = []
    for a, (src, land) in enumerate(zip(srcs, lands)):
        for ki, mask in enumerate(CHIP_MASKS):
            dev = _peer(mask)
            k = a * len(CHIP_MASKS) + ki
            piece = src.at[_chip_of(dev)] if by_target else src
            copies.append(_remote(piece, land.at[_chip_of(me)], send_sems.at[k], recv_sems.at[k], dev))
    return copies


def chip_exchange_start(name, arrays, by_target):
    n = len(arrays)
    nsem = n * len(CHIP_MASKS)
    piece_shapes = [a.shape[1:] if by_target else a.shape for a in arrays]

    def body(*refs):
        srcs, lands = refs[:n], refs[n:2 * n]
        send_sems, recv_sems = refs[2 * n:2 * n + 2]
        token = refs[4 * n + 2]
        stages, local_sems = refs[4 * n + 3:5 * n + 3], refs[5 * n + 3]
        me = _me()
        for cp in _chip_copies(srcs, lands, send_sems, recv_sems, by_target):
            cp.start()
        for a in range(n):
            own = _StagedCopy(srcs[a].at[_chip_of(me)] if by_target else srcs[a], stages[a], lands[a].at[_chip_of(me)],
                              local_sems.at[2 * a], local_sems.at[2 * a + 1])
            own.start()
            own.wait()
        token[...] = jnp.zeros_like(token)

    lands = [lax.empty((4,) + tuple(s), a.dtype) for s, a in zip(piece_shapes, arrays)]
    hbm = lambda a: pltpu.HBM(a.shape, a.dtype)
    res = pl.pallas_call(
        body, name=name,
        out_shape=(pltpu.SemaphoreType.DMA((nsem,)), pltpu.SemaphoreType.DMA((nsem,)), *[hbm(a) for a in arrays],
                   *[hbm(l) for l in lands], jax.ShapeDtypeStruct((8, LANES), F32)),
        in_specs=(_HBM,) * (2 * n), out_specs=(_SEM, _SEM) + (_HBM,) * (2 * n) + (pl.BlockSpec(memory_space=pltpu.VMEM),),
        input_output_aliases={i: i + 2 for i in range(2 * n)},
        scratch_shapes=[pltpu.VMEM(tuple(s), a.dtype) for s, a in zip(piece_shapes, arrays)]
        + [pltpu.SemaphoreType.DMA((2 * n,))],
        compiler_params=pltpu.CompilerParams(has_side_effects=_SIDE_EFFECT, vmem_limit_bytes=VMEM_LIMIT),
    )(*[pltpu.with_memory_space_constraint(a, pltpu.HBM) for a in arrays + lands])
    return res[:-1], res[-1]


def chip_exchange_wait(name, in_flight, by_target, after):
    send_sems, recv_sems, *thru = in_flight
    n = len(thru) // 2

    def body(*refs):
        srcs, lands, (send, recv) = refs[:n], refs[n:2 * n], refs[2 * n:2 * n + 2]
        for cp in _chip_copies(srcs, lands, send, recv, by_target):
            cp.wait_send()
            cp.wait_recv()

    return pl.pallas_call(
        body, name=name, out_shape=tuple(pltpu.HBM(t.shape, t.dtype) for t in thru),
        in_specs=(_HBM,) * (2 * n) + (_SEM, _SEM, pl.BlockSpec(memory_space=pl.ANY)), out_specs=(_HBM,) * (2 * n),
        input_output_aliases={i: i for i in range(2 * n)},
        compiler_params=pltpu.CompilerParams(has_side_effects=_SIDE_EFFECT),
    )(*thru, send_sems, recv_sems, after)[n:]


def _call_with_comm(body, comm, steps, args, *, name, out_shape, in_specs, out_specs, grid, scratch=()):
    ni, no, ns, nc = len(in_specs), len(out_specs), len(scratch), len(comm.arrays)

    def full_body(*refs):
        ins, cins = refs[:ni], refs[ni:ni + nc]
        outs, couts = refs[ni + nc:ni + nc + no], refs[ni + nc + no:ni + 2 * nc + no]
        scr, csems = refs[ni + 2 * nc + no:ni + 2 * nc + no + ns], refs[ni + 2 * nc + no + ns:]
        first, middle, last = steps()
        pl.when(first)(lambda: comm.start(cins, couts, csems))
        if comm.has_mid:
            pl.when(middle)(lambda: comm.mid(cins, couts, csems))
        body(*ins, *outs, *scr)
        pl.when(last)(lambda: comm.finish(cins, couts, csems))

    res = pl.pallas_call(
        full_body, name=name, out_shape=list(out_shape) + comm.out_shape, grid=grid,
        in_specs=list(in_specs) + [_ANY] * nc, out_specs=list(out_specs) + [_ANY] * nc,
        scratch_shapes=list(scratch) + comm.scratch,
        compiler_params=pltpu.CompilerParams(dimension_semantics=("arbitrary",) * len(grid),
                                             vmem_limit_bytes=VMEM_LIMIT))(*args, *comm.arrays)
    return res[:no], res[no:]


def _shard_pieces(chip):
    if chip < 3:
        return [(0, SHARD_W, 8 * chip)]
    behind_dt = DT_STORED_START + SSM_HEADS - 3 * SHARD_W
    return [(0, behind_dt, 24), (behind_dt, SHARD_W - behind_dt, behind_dt + 24 + DT_PAD)]


W_IN_COLS = 256


def pack_w_in(wt):
    def body(w_ref, o_ref, pad_ref):
        chip = _chip_index()
        pad_ref[...] = jnp.zeros_like(pad_ref)
        for cv in range(4):
            @pl.when(chip == cv)
            def _():
                for src, n, dst in _shard_pieces(cv):
                    pad_ref[dst:dst + n, :] = w_ref[src:src + n, :]
        o_ref[...] = pad_ref[...].astype(BF16)

    return _call(body, name="pack_w_in", grid=(D_MODEL // W_IN_COLS,),
                 in_specs=[pl.BlockSpec((SHARD_W, W_IN_COLS), lambda i: (0, i))],
                 out_specs=pl.BlockSpec((PACK_W, W_IN_COLS), lambda i: (0, i)),
                 out_shape=jax.ShapeDtypeStruct((PACK_W, D_MODEL), BF16),
                 scratch=[pltpu.VMEM((PACK_W, W_IN_COLS), F32)], sem=("parallel",))(wt)


def _tile_runs():
    runs, fix = [], []
    for t in range(N_ALIGNED // LANES):
        s = min(t // 19, 3)
        j = t - 19 * s
        p = _act_col(t * LANES)
        if runs and runs[-1][1] == s and runs[-1][0] + runs[-1][3] == p and runs[-1][2] + runs[-1][3] == j * LANES:
            runs[-1][3] += LANES
        else:
            runs.append([p, s, j * LANES, LANES])
        if j == 0 and s > 0:
            fix.append((p, s - 1))
    return runs, fix


def unpack_w_in(bg):
    runs, fix = _tile_runs()

    def body(b_ref, o_ref):
        for p, s, j, w in runs:
            o_ref[p:p + w, :] = b_ref[s, j:j + w, :]
        for p, s in fix:
            o_ref[p:p + LANES, :] = o_ref[p:p + LANES, :] + b_ref[s, SHARD_STRIDE:PACK_W, :]
        o_ref[N_ALIGNED:N_ACT, :] = jnp.zeros((N_ACT - N_ALIGNED, W_IN_COLS), BF16)

    return _call(body, name="unpack_w_in", grid=(D_MODEL // W_IN_COLS,),
                 in_specs=[pl.BlockSpec((4, PACK_W, W_IN_COLS), lambda i: (0, 0, i))],
                 out_specs=pl.BlockSpec((N_ACT, W_IN_COLS), lambda i: (0, i)),
                 out_shape=jax.ShapeDtypeStruct((N_ACT, D_MODEL), BF16), sem=("parallel",))(bg)


def pack_grad_w_in(dwt):
    def body(g_ref, o_ref):
        for s in range(4):
            for j in range(PACK_W // LANES):
                p = _act_col((19 * s + j) * LANES)
                o_ref[s, j * LANES:(j + 1) * LANES, :] = g_ref[p:p + LANES, :]

    return _call(body, name="pack_grad_w_in", grid=(D_MODEL // W_IN_COLS,),
                 in_specs=[pl.BlockSpec((N_ACT, W_IN_COLS), lambda i: (0, i))],
                 out_specs=pl.BlockSpec((4, PACK_W, W_IN_COLS), lambda i: (0, 0, i)),
                 out_shape=jax.ShapeDtypeStruct((4, PACK_W, D_MODEL), BF16), sem=("parallel",))(dwt)


def _adamw(w, g, m, v):
    m = ADAM_B1 * m + (1.0 - ADAM_B1) * g
    v = ADAM_B2 * v + (1.0 - ADAM_B2) * jnp.square(g)
    m_hat = m / (1.0 - ADAM_B1 ** ADAM_STEP)
    v_hat = v / (1.0 - ADAM_B2 ** ADAM_STEP)
    delta = -ADAM_LR * (m_hat / (jnp.sqrt(v_hat) + ADAM_EPS) + ADAM_WD * w)
    return delta, m, v


def adamw_w_in(g_packed, wt, mt, vt):
    cols = LANES

    def body(g_ref, w_ref, m_ref, v_ref, go_ref, d_ref, mo_ref, vo_ref):
        chip = _chip_index()
        for cv in range(4):
            @pl.when(chip == cv)
            def _():
                for dst, n, src in _shard_pieces(cv):
                    go_ref[dst:dst + n, :] = g_ref[src:src + n, :]
        d_ref[...], mo_ref[...], vo_ref[...] = _adamw(w_ref[...], go_ref[...], m_ref[...], v_ref[...])

    spec = pl.BlockSpec((SHARD_W, cols), lambda i: (0, i))
    shp = jax.ShapeDtypeStruct((SHARD_W, D_MODEL), F32)
    return _call(body, name="adamw_w_in", grid=(D_MODEL // cols,),
                 in_specs=[pl.BlockSpec((PACK_W, cols), lambda i: (0, i)), spec, spec, spec],
                 out_specs=[spec] * 4, out_shape=[shp] * 4, sem=("parallel",))(g_packed, wt, mt, vt)


def adamw_rows(name, g, w, m, v):
    r, c = g.shape
    rows = min(r, BLOCK)

    def body(g_ref, w_ref, m_ref, v_ref, d_ref, mo_ref, vo_ref):
        d_ref[...], mo_ref[...], vo_ref[...] = _adamw(w_ref[...], g_ref[...], m_ref[...], v_ref[...])

    spec = pl.BlockSpec((rows, c), lambda i: (i, 0))
    shp = jax.ShapeDtypeStruct((r, c), F32)
    return _call(body, name=name, grid=(r // rows,), in_specs=[spec] * 4, out_specs=[spec] * 3, out_shape=[shp] * 3,
                 sem=("parallel",))(g, w, m, v)


def adamw_small(gs, ws, ms, vs):
    n = len(gs)

    def body(*refs):
        g, w, m, v = refs[:n], refs[n:2 * n], refs[2 * n:3 * n], refs[3 * n:4 * n]
        outs = refs[4 * n:]
        for i in range(n):
            d, mn, vn = _adamw(w[i][...], g[i][...], m[i][...], v[i][...])
            outs[3 * i][...] = d
            outs[3 * i + 1][...] = mn
            outs[3 * i + 2][...] = vn

    specs = [_full(a.shape) for a in gs]
    res = _call(body, name="adamw_small", in_specs=specs * 4,
                out_specs=[s for s in specs for _ in range(3)],
                out_shape=[jax.ShapeDtypeStruct(a.shape, F32) for a in gs for _ in range(3)])(*gs, *ws, *ms, *vs)
    return [tuple(res[3 * i:3 * i + 3]) for i in range(n)]


def sum_slots(name, r):
    s, rr, c = r.shape
    rows = min(rr, BLOCK)

    def body(r_ref, o_ref):
        acc = r_ref[0].astype(F32)
        for k in range(1, s):
            acc = acc + r_ref[k].astype(F32)
        o_ref[...] = acc

    return _call(body, name=name, grid=(rr // rows,), in_specs=[pl.BlockSpec((s, rows, c), lambda i: (0, i, 0))],
                 out_specs=pl.BlockSpec((rows, c), lambda i: (i, 0)), out_shape=jax.ShapeDtypeStruct((rr, c), F32),
                 sem=("parallel",))(r)


def sum_pair(partial, from_sibling):
    s, _, rr, cols = partial.shape
    rows = rr // 2

    def body(p_ref, r_ref, o_ref):
        c = lax.axis_index("c")
        o_ref[...] = (p_ref[c].astype(F32) + r_ref[1 - c].astype(F32)).astype(BF16)

    return _call(body, name="sum_pair", grid=(s, rr // rows),
                 in_specs=[pl.BlockSpec((None, 2, rows, cols), lambda k, i: (k, 0, i, 0)),
                           pl.BlockSpec((2, None, rows, cols), lambda k, i: (0, k, i, 0))],
                 out_specs=pl.BlockSpec((None, rows, cols), lambda k, i: (k, i, 0)),
                 out_shape=jax.ShapeDtypeStruct((s, rr, cols), BF16), sem=("parallel", "parallel"))(partial, from_sibling)


def _col_tile(n, k):
    if n % 896 == 0 and k <= 1024:
        return 896
    return min(n, 512)


def project(name, x, wt, after):
    m, k = x.shape
    n = wt.shape[0]
    tn = D_MODEL

    def body(x_ref, w_ref, after_ref, o_ref):
        o_ref[...] = lax.dot_general(x_ref[...], w_ref[...], _NT, preferred_element_type=F32)

    return _call(body, name=name, grid=(n // tn,),
                 in_specs=[_full((m, k)), pl.BlockSpec((tn, k), lambda j: (j, 0)), _full(after.shape)],
                 out_specs=pl.BlockSpec((m, tn), lambda j: (0, j)), out_shape=jax.ShapeDtypeStruct((m, n), F32),
                 sem=("parallel",))(x, wt, after)


def _piece_tiles(pieces):
    spans, start = [], 0
    for p in pieces:
        spans.append((start, p.shape[1] // D_MODEL))
        start += p.shape[1] // D_MODEL
    return spans, start


def _piece_spec(tm, span, rows_of, tile_of):
    first, count = span

    def index(i, j):
        t = tile_of(i, j) - first
        mine = (t >= 0) & (t < count)
        return jnp.where(mine, rows_of(i, j), 0), jnp.clip(t, 0, count - 1)

    return pl.BlockSpec((tm, D_MODEL), index)


def project_back(name, pieces, wt, after):
    m = pieces[0].shape[0]
    k = wt.shape[1]
    tm = m // 2
    spans, steps = _piece_tiles(pieces)

    def body(*refs):
        w_ref, o_ref = refs[len(pieces)], refs[len(pieces) + 2]
        j = pl.program_id(1)

        @pl.when(j == 0)
        def _():
            o_ref[...] = jnp.zeros_like(o_ref)

        for dy_ref, (first, count) in zip(refs, spans):
            @pl.when((j >= first) & (j < first + count))
            def _():
                o_ref[...] += jnp.dot(dy_ref[...], w_ref[...], preferred_element_type=F32)

    return _call(body, name=name, grid=(m // tm, steps),
                 in_specs=[_piece_spec(tm, s, lambda i, j: i, lambda i, j: j) for s in spans]
                 + [pl.BlockSpec((D_MODEL, k), lambda i, j: (j, 0)), _full(after.shape)],
                 out_specs=pl.BlockSpec((tm, k), lambda i, j: (i, 0)), out_shape=jax.ShapeDtypeStruct((m, k), F32),
                 sem=("parallel", "arbitrary"))(*pieces, wt, after)


def weight_grad_t(name, pieces, x):
    m = pieces[0].shape[0]
    k = x.shape[1]
    tm = m // 2
    spans, steps = _piece_tiles(pieces)

    def body(*refs):
        x_ref, o_ref, acc_ref = refs[len(pieces):]
        i, half = pl.program_id(0), pl.program_id(1)
        for dy_ref, (first, count) in zip(refs, spans):
            @pl.when((i >= first) & (i < first + count))
            def _():
                part = lax.dot_general(dy_ref[...], x_ref[...], (((0,), (0,)), ((), ())), preferred_element_type=F32)

                @pl.when(half == 0)
                def _():
                    acc_ref[...] = part

                @pl.when(half == 1)
                def _():
                    o_ref[...] = (acc_ref[...] + part).astype(BF16)

    return _call(body, name=name, grid=(steps, 2),
                 in_specs=[_piece_spec(tm, s, lambda i, j: j, lambda i, j: i) for s in spans]
                 + [pl.BlockSpec((tm, k), lambda i, j: (j, 0))],
                 out_specs=pl.BlockSpec((D_MODEL, k), lambda i, j: (i, 0)),
                 out_shape=jax.ShapeDtypeStruct((steps * D_MODEL, k), BF16),
                 scratch=[pltpu.VMEM((D_MODEL, k), F32)], sem=("parallel", "arbitrary"))(*pieces, x)


def mm_tn(name, x, dy):
    m, k = x.shape
    n = dy.shape[1]
    tm = m // 2
    tn = _col_tile(n, k)

    def body(x_ref, dy_ref, o_ref, acc_ref):
        part = lax.dot_general(x_ref[...].astype(BF16), dy_ref[...].astype(BF16), (((0,), (0,)), ((), ())),
                               preferred_element_type=F32)

        @pl.when(pl.program_id(1) == 0)
        def _():
            acc_ref[...] = part

        @pl.when(pl.program_id(1) == 1)
        def _():
            o_ref[...] = (acc_ref[...] + part).astype(BF16)

    return _call(body, name=name, grid=(n // tn, 2),
                 in_specs=[pl.BlockSpec((tm, k), lambda i, j: (j, 0)), pl.BlockSpec((tm, tn), lambda i, j: (j, i))],
                 out_specs=pl.BlockSpec((k, tn), lambda i, j: (0, i)), out_shape=jax.ShapeDtypeStruct((k, n), BF16),
                 scratch=[pltpu.VMEM((k, tn), F32)], sem=("parallel", "arbitrary"))(x, dy)


def _row_spec(width, col_block=0):
    return pl.BlockSpec((BLOCK, width), lambda i: (i, col_block))


def _x_spec():
    return pl.BlockSpec((None, BLOCK, D_MODEL), lambda i: (0, jnp.maximum(i - 1, 0), 0))


def prep(x, meta, g_pre):
    nb = x.shape[1] // BLOCK + 1

    def body(x_ref, meta_ref, g_ref, h_ref, u_ref):
        i = pl.program_id(0)

        @pl.when(i == 0)
        def _():
            h_ref[0:PAD_ROWS, :] = jnp.zeros((PAD_ROWS, D_MODEL), F32)
            h_ref[PAD_ROWS:BLOCK, :] = meta_ref[...]

        @pl.when(i > 0)
        def _():
            h_ref[...] = x_ref[...]

        u_ref[...] = _rms(h_ref[...], g_ref[...]).astype(BF16)

    return _call(body, name="prep", grid=(nb,), in_specs=[_x_spec(), _full((N_META, D_MODEL)), _full((1, D_MODEL))],
                 out_specs=[_row_spec(D_MODEL), _row_spec(D_MODEL)],
                 out_shape=[jax.ShapeDtypeStruct((nb * BLOCK, D_MODEL), F32),
                            jax.ShapeDtypeStruct((nb * BLOCK, D_MODEL), BF16)], sem=("parallel",))(x, meta, g_pre)


def prep_bwd(h, du, dres, g_pre):
    nb = h.shape[0] // BLOCK

    def body(h_ref, du_ref, dres_ref, g_ref, gx_ref, gm_ref, gg_ref):
        i = pl.program_id(0)
        _, vjp = jax.vjp(_rms, h_ref[...], g_ref[...])
        dh, dg = vjp(du_ref[...])

        @pl.when(i == 0)
        def _():
            gm_ref[...] = dh[PAD_ROWS:BLOCK, :]
            gg_ref[...] = dg

        @pl.when(i > 0)
        def _():
            gg_ref[...] += dg

        gx_ref[...] = dh + dres_ref[...]

    return _call(body, name="prep_bwd", grid=(nb,),
                 in_specs=[_row_spec(D_MODEL), _row_spec(D_MODEL), _row_spec(D_MODEL), _full((1, D_MODEL))],
                 out_specs=[_x_spec(), _full((N_META, D_MODEL)), _full((1, D_MODEL))],
                 out_shape=[jax.ShapeDtypeStruct((1, (nb - 1) * BLOCK, D_MODEL), F32),
                            jax.ShapeDtypeStruct((N_META, D_MODEL), F32), jax.ShapeDtypeStruct((1, D_MODEL), F32)],
                 sem=("arbitrary",))(h, du, dres, g_pre)


GROUP_W = SSM_INNER // SSM_GROUPS


def _gated_norm(y, z, g):
    t = y * _silu(z)
    return t * lax.rsqrt(jnp.mean(t * t, axis=-1, keepdims=True) + NORM_EPS) * g


def _gated_norm_groups(y, z, g):
    groups = [slice(k * GROUP_W, (k + 1) * GROUP_W) for k in range(SSM_GROUPS)]
    return jnp.concatenate([_gated_norm(y[:, s], z[:, s], g[:, s]) for s in groups], axis=1)


def _merge(ga, gs, ya, ys):
    return _sigmoid(ga) * ya + _sigmoid(gs) * ys


GATE_ATT_BLOCK = SEG["gate_att"][2] // D_MODEL
GATE_SSM_BLOCK = SEG["gate_ssm"][2] // D_MODEL


def _row_loss(out, g_post, x, target):
    diff = x + _rms(out, g_post) - target
    return 0.5 * jnp.sum(diff * diff) / D_MODEL


def tail(y_ssd, proj, a_att, x, target, woa, wos, wo, g_norm, g_post):
    nb = y_ssd.shape[0] // BLOCK
    rows = nb * BLOCK

    def body(y_ref, z_ref, ga_ref, gs_ref, a_ref, x_ref, t_ref, woa_ref, wos_ref, wo_ref, gn_ref, gp_ref,
             yn_ref, mg_ref, dout_ref, dya_ref, dys_ref, da_ref, dy_ref, dz_ref, dga_ref, dgs_ref, dres_ref,
             loss_ref, dgp_ref, dgn_ref):
        i = pl.program_id(0)
        yn, norm_vjp = jax.vjp(_gated_norm_groups, y_ref[...], z_ref[...], gn_ref[...])
        yn16 = yn.astype(BF16)
        y_ssm = jnp.dot(yn16, wos_ref[...], preferred_element_type=F32)
        y_att = jnp.dot(a_ref[...], woa_ref[...], preferred_element_type=F32)
        merged, merge_vjp = jax.vjp(_merge, ga_ref[...], gs_ref[...], y_att, y_ssm)
        merged16 = merged.astype(BF16)
        out = jnp.dot(merged16, wo_ref[...], preferred_element_type=F32)
        loss, loss_vjp = jax.vjp(_row_loss, out, gp_ref[...], x_ref[...], t_ref[...])
        counted = jnp.where(i > 0, 1.0, 0.0)
        dout, dgp, dres, _ = loss_vjp(counted)
        dout16 = dout.astype(BF16)
        dmerged = lax.dot_general(dout16, wo_ref[...], _NT, preferred_element_type=F32)
        dga, dgs, dya, dys = merge_vjp(dmerged)
        dya16, dys16 = dya.astype(BF16), dys.astype(BF16)
        dyn = lax.dot_general(dys16, wos_ref[...], _NT, preferred_element_type=F32)
        dy, dz, dgn = norm_vjp(dyn)

        yn_ref[...] = yn16
        mg_ref[...] = merged16
        dout_ref[...] = dout16
        dya_ref[...] = dya16
        dys_ref[...] = dys16
        da_ref[...] = lax.dot_general(dya16, woa_ref[...], _NT, preferred_element_type=F32)
        dy_ref[...] = dy
        dz_ref[...] = dz.astype(BF16)
        dga_ref[...] = dga.astype(BF16)
        dgs_ref[...] = dgs.astype(BF16)
        dres_ref[...] = dres

        @pl.when(i == 0)
        def _():
            loss_ref[...] = jnp.zeros_like(loss_ref)
            dgp_ref[...] = jnp.zeros_like(dgp_ref)
            dgn_ref[...] = jnp.zeros_like(dgn_ref)

        loss_ref[...] += loss * counted
        dgp_ref[...] += dgp
        dgn_ref[...] += dgn

    wide, narrow = _row_spec(SSM_INNER), _row_spec(D_MODEL)
    resident = pl.BlockSpec(memory_space=pltpu.VMEM)
    bf = lambda w: jax.ShapeDtypeStruct((rows, w), BF16)
    f32 = lambda w: jax.ShapeDtypeStruct((rows, w), F32)
    return _call(body, name="tail", grid=(nb,),
                 in_specs=[wide, wide, _row_spec(D_MODEL, GATE_ATT_BLOCK), _row_spec(D_MODEL, GATE_SSM_BLOCK), narrow,
                           _x_spec(), _x_spec(), resident, resident, resident, _full((1, SSM_INNER)),
                           _full((1, D_MODEL))],
                 out_specs=[wide, narrow, narrow, narrow, narrow, narrow, wide, wide, narrow, narrow, narrow,
                            _full((8, LANES)), _full((1, D_MODEL)), _full((1, SSM_INNER))],
                 out_shape=[bf(SSM_INNER), bf(D_MODEL), bf(D_MODEL), bf(D_MODEL), bf(D_MODEL), f32(D_MODEL),
                            f32(SSM_INNER), bf(SSM_INNER), bf(D_MODEL), bf(D_MODEL), f32(D_MODEL),
                            jax.ShapeDtypeStruct((8, LANES), F32), jax.ShapeDtypeStruct((1, D_MODEL), F32),
                            jax.ShapeDtypeStruct((1, SSM_INNER), F32)],
                 sem=("arbitrary",))(y_ssd, proj, proj, proj, a_att, x, target, woa, wos, wo, g_norm, g_post)


_NT = (((1,), (1,)), ((), ()))
ALIBI_SLOPES = tuple(2.0 ** (-8.0 * (h + 1) / ATT_Q_HEADS) for h in range(ATT_Q_HEADS))
KV_WIDTH = ATT_KV_HEADS * HEAD_DIM
Q_BLOCK = SEG["q"][2] // D_MODEL
Z_ATT_BLOCK = SEG["z_att"][2] // D_MODEL
K_BLOCK = SEG["k"][2] // KV_WIDTH
V_BLOCK = SEG["v"][2] // KV_WIDTH
META_ROW_BLOCK = PAD_ROWS // N_META


@jax.custom_vjp
def _swap_halves(x):
    return pltpu.roll(x, HEAD_DIM, 1)


_swap_halves.defvjp(lambda x: (pltpu.roll(x, HEAD_DIM, 1), None), lambda _, g: (pltpu.roll(g, HEAD_DIM, 1),))


def _both_halves(t, half):
    first = lax.broadcasted_iota(jnp.int32, t.shape, 1) < HEAD_DIM
    sw = _swap_halves(t)
    return jnp.where(first, t, sw) if half == 0 else jnp.where(first, sw, t)


def _attn_rows(q, z, kp, kc, vp, vc, km, vm, sinks, n):
    rows = ATT_GROUP * BLOCK
    i = lax.broadcasted_iota(jnp.int32, (rows, BLOCK), 0) & (BLOCK - 1)
    j = lax.broadcasted_iota(jnp.int32, (rows, BLOCK), 1)
    rel_c = (i - j).astype(F32)
    rel_p = rel_c + float(BLOCK)
    nv = jnp.zeros((rows, BLOCK), jnp.int32) + n
    ok_c = (i >= j) & (nv >= 1)
    ok_p = (j > i) & (nv >= 2)
    im = lax.broadcasted_iota(jnp.int32, (rows, N_META), 0) & (BLOCK - 1)
    jm = lax.broadcasted_iota(jnp.int32, (rows, N_META), 1)
    ok_m = ((jnp.zeros((rows, N_META), jnp.int32) + n) >= 1) | (im >= PAD_ROWS + jm)
    first = lax.broadcasted_iota(jnp.int32, (BLOCK, LANES), 1) < HEAD_DIM
    neg = -jnp.inf
    outs = []
    for kv in range(ATT_KV_HEADS):
        tile, half = divmod(kv, 2)
        lanes = slice(tile * LANES, (tile + 1) * LANES)
        kc2, kp2, km2 = (_both_halves(t[:, lanes], half).astype(BF16) for t in (kc, kp, km))
        vc2, vp2, vm2 = (_both_halves(t[:, lanes], half).astype(BF16) for t in (vc, vp, vm))
        qs, slope, sk = [], [], []
        for pair in range(ATT_GROUP // 2):
            c0 = (kv * ATT_GROUP + 2 * pair) * HEAD_DIM
            qp = q[:, c0:c0 + LANES] * HEAD_DIM ** -0.5
            qs += [jnp.where(first, qp, 0.0), jnp.where(first, 0.0, qp)]
        for g in range(ATT_GROUP):
            slope.append(jnp.full((BLOCK, 1), ALIBI_SLOPES[kv * ATT_GROUP + g], F32))
            sk.append(jnp.broadcast_to(sinks[kv * ATT_GROUP + g], (BLOCK, 1)))
        qs = jnp.concatenate(qs, axis=0).astype(BF16)
        slope = jnp.concatenate(slope, axis=0)
        sk = jnp.concatenate(sk, axis=0)
        sc = jnp.where(ok_c, lax.dot_general(qs, kc2, _NT, preferred_element_type=F32) - slope * rel_c, neg)
        sp = jnp.where(ok_p, lax.dot_general(qs, kp2, _NT, preferred_element_type=F32) - slope * rel_p, neg)
        sm = jnp.where(ok_m, lax.dot_general(qs, km2, _NT, preferred_element_type=F32), neg)
        mx = jnp.maximum(jnp.maximum(jnp.max(sc, axis=1, keepdims=True), jnp.max(sp, axis=1, keepdims=True)),
                         jnp.maximum(jnp.max(sm, axis=1, keepdims=True), sk))
        mx = lax.stop_gradient(mx)
        ec, ep, em, es = jnp.exp(sc - mx), jnp.exp(sp - mx), jnp.exp(sm - mx), jnp.exp(sk - mx)
        den = (es + jnp.sum(ec, axis=1, keepdims=True) + jnp.sum(ep, axis=1, keepdims=True)
               + jnp.sum(em, axis=1, keepdims=True))
        inv = 1.0 / den
        o = (jnp.dot((ec * inv).astype(BF16), vc2, preferred_element_type=F32)
             + jnp.dot((ep * inv).astype(BF16), vp2, preferred_element_type=F32)
             + jnp.dot((em * inv).astype(BF16), vm2, preferred_element_type=F32))
        for pair in range(ATT_GROUP // 2):
            r0 = 2 * pair * BLOCK
            outs.append(jnp.where(first, o[r0:r0 + BLOCK], o[r0 + BLOCK:r0 + 2 * BLOCK]))
    return jnp.concatenate(outs, axis=1) * _silu(z)


def _attn_specs(nb, steps_clamped):
    def blk(t):
        return jnp.minimum(t, nb - 1) if steps_clamped else t

    wide = lambda col: pl.BlockSpec((BLOCK, D_MODEL), lambda t: (blk(t), col))
    cur = lambda col: pl.BlockSpec((BLOCK, KV_WIDTH), lambda t: (blk(t), col))
    prev = lambda col: pl.BlockSpec((BLOCK, KV_WIDTH), lambda t: (jnp.maximum(blk(t) - 1, 0), col))
    meta = lambda col: pl.BlockSpec((N_META, KV_WIDTH), lambda t: (META_ROW_BLOCK, col))
    sinks = pl.BlockSpec((ATT_Q_HEADS, 1, 1), lambda t: (0, 0, 0))
    return [wide(Q_BLOCK), wide(Z_ATT_BLOCK), prev(K_BLOCK), cur(K_BLOCK), prev(V_BLOCK), cur(V_BLOCK),
            meta(K_BLOCK), meta(V_BLOCK), sinks]


def attn_fwd(proj, sinks):
    nb = proj.shape[0] // BLOCK

    def body(q_ref, z_ref, kp_ref, kc_ref, vp_ref, vc_ref, km_ref, vm_ref, sk_ref, o_ref):
        o_ref[...] = _attn_rows(q_ref[...], z_ref[...], kp_ref[...], kc_ref[...], vp_ref[...], vc_ref[...],
                                km_ref[...], vm_ref[...], tuple(sk_ref[h] for h in range(ATT_Q_HEADS)),
                                pl.program_id(0)).astype(BF16)

    return _call(body, name="attn_fwd", grid=(nb,), in_specs=_attn_specs(nb, False), out_specs=_row_spec(D_MODEL),
                 out_shape=jax.ShapeDtypeStruct((nb * BLOCK, D_MODEL), BF16), sem=("parallel",))(*([proj] * 8), sinks)


def attn_bwd(da, proj, sinks):
    nb = proj.shape[0] // BLOCK
    last = nb - 1
    wide = pl.BlockSpec((BLOCK, D_MODEL), lambda t: (jnp.minimum(t, last), 0))
    done = pl.BlockSpec((BLOCK, KV_WIDTH), lambda t: (jnp.maximum(t - 1, 0), 0))
    meta = _full((N_META, KV_WIDTH))
    par = _full((ATT_Q_HEADS, 1, 1))

    def body(da_ref, q_ref, z_ref, kp_ref, kc_ref, vp_ref, vc_ref, km_ref, vm_ref, sk_ref,
             dq_ref, dz_ref, dk_ref, dv_ref, dkm_ref, dvm_ref, dsk_ref, ck_ref, cv_ref):
        t = pl.program_id(0)

        @pl.when(t == 0)
        def _():
            ck_ref[...] = jnp.zeros_like(ck_ref)
            cv_ref[...] = jnp.zeros_like(cv_ref)
            dkm_ref[...] = jnp.zeros_like(dkm_ref)
            dvm_ref[...] = jnp.zeros_like(dvm_ref)
            dsk_ref[...] = jnp.zeros_like(dsk_ref)

        @pl.when(t < nb)
        def _():
            def f(q, z, kp, kc, vp, vc, km, vm, sk):
                return _attn_rows(q, z, kp, kc, vp, vc, km, vm, sk, t)

            _, vjp = jax.vjp(f, q_ref[...], z_ref[...], kp_ref[...], kc_ref[...], vp_ref[...], vc_ref[...],
                             km_ref[...], vm_ref[...], tuple(sk_ref[h] for h in range(ATT_Q_HEADS)))
            dq, dz, dkp, dkc, dvp, dvc, dkm, dvm, dsk = vjp(da_ref[...])
            dq_ref[...] = dq.astype(BF16)
            dz_ref[...] = dz.astype(BF16)
            for h in range(ATT_Q_HEADS):
                dsk_ref[h] += dsk[h]
            dk_ref[...] = ck_ref[...] + dkp
            dv_ref[...] = cv_ref[...] + dvp
            ck_ref[...] = dkc
            cv_ref[...] = dvc
            dkm_ref[...] += dkm
            dvm_ref[...] += dvm

        @pl.when(t == nb)
        def _():
            dk_ref[...] = ck_ref[...]
            dv_ref[...] = cv_ref[...]

    rows = nb * BLOCK
    return _call(body, name="attn_bwd", grid=(nb + 1,), in_specs=[wide] + _attn_specs(nb, True),
                 out_specs=[wide, wide, done, done, meta, meta, par],
                 out_shape=[jax.ShapeDtypeStruct((rows, D_MODEL), BF16), jax.ShapeDtypeStruct((rows, D_MODEL), BF16),
                            jax.ShapeDtypeStruct((rows, KV_WIDTH), F32), jax.ShapeDtypeStruct((rows, KV_WIDTH), F32),
                            jax.ShapeDtypeStruct((N_META, KV_WIDTH), F32), jax.ShapeDtypeStruct((N_META, KV_WIDTH), F32),
                            jax.ShapeDtypeStruct(sinks.shape, F32)],
                 scratch=[pltpu.VMEM((BLOCK, KV_WIDTH), F32), pltpu.VMEM((BLOCK, KV_WIDTH), F32)],
                 sem=("arbitrary",))(da, *([proj] * 8), sinks)


XBC_BLOCK0 = SEG["xbc"][2] // D_MODEL
CONV_COL_BLOCKS = CONV_DIM // D_MODEL
DT_TILE = SEG["dt"][2] // LANES


HALO = 8


def _conv_rows(length):
    return 544 if length % 544 == 0 else BLOCK


def _shift_rows(cur, before, j):
    if j == 0:
        return cur
    n = cur.shape[0]
    row = lax.broadcasted_iota(jnp.int32, cur.shape, 0)
    head = pltpu.roll(before, j, 0)
    if n > HALO:
        head = jnp.concatenate([head, jnp.zeros((n - HALO, cur.shape[1]), cur.dtype)], axis=0)
    return jnp.where(row >= j, pltpu.roll(cur, j, 0), head)


def _conv_pre(cur, before, w_ref, b_ref):
    pre = b_ref[...] + w_ref[CONV_WIDTH - 1:CONV_WIDTH, :] * cur
    for k in range(CONV_WIDTH - 1):
        pre = pre + w_ref[k:k + 1, :] * _shift_rows(cur, before, CONV_WIDTH - 1 - k)
    return pre


def _conv_specs(steps, rows, col0=0):
    first = XBC_BLOCK0 + col0
    halos = rows // HALO
    cur = pl.BlockSpec((rows, D_MODEL), lambda j, i: (i, first + j))
    before = pl.BlockSpec((HALO, D_MODEL), lambda j, i: (jnp.maximum(i * halos - 1, 0), first + j))
    after = pl.BlockSpec((HALO, D_MODEL), lambda j, i: (jnp.minimum(i + 1, steps - 1) * halos, first + j))
    return cur, before, after


def _valid_rows(i, rows):
    row = lax.broadcasted_iota(jnp.int32, (rows, D_MODEL), 0)
    return jnp.maximum((row >= PAD_ROWS).astype(F32), jnp.where(i > 0, 1.0, 0.0))


def conv_fwd(proj, conv_w, conv_b):
    rows = _conv_rows(proj.shape[0])
    steps = proj.shape[0] // rows
    cur, before, _ = _conv_specs(steps, rows)

    def body(c_ref, p_ref, w_ref, b_ref, o_ref):
        i = pl.program_id(1)
        pre = _conv_pre(c_ref[...], p_ref[...] * jnp.where(i > 0, 1.0, 0.0), w_ref, b_ref)
        o_ref[...] = _silu(pre) * _valid_rows(i, rows)

    return _call(body, name="conv_fwd", grid=(CONV_COL_BLOCKS, steps),
                 in_specs=[cur, before, pl.BlockSpec((CONV_WIDTH, D_MODEL), lambda j, i: (0, j)),
                           pl.BlockSpec((1, D_MODEL), lambda j, i: (0, j))],
                 out_specs=pl.BlockSpec((rows, D_MODEL), lambda j, i: (i, j)),
                 out_shape=jax.ShapeDtypeStruct((proj.shape[0], CONV_DIM), F32),
                 sem=("parallel", "parallel"))(proj, proj, conv_w, conv_b)


def conv_bwd(name, dparts, col0, proj, conv_w, conv_b):
    rows = _conv_rows(proj.shape[0])
    steps = proj.shape[0] // rows
    last = steps - 1
    ncol = sum(d.shape[1] for d in dparts) // D_MODEL
    np_ = len(dparts)
    cur, before, after = _conv_specs(steps, rows, col0)
    dcur = [pl.BlockSpec((rows, d.shape[1] // ncol), lambda j, i: (i, j)) for d in dparts]
    dafter = [pl.BlockSpec((HALO, d.shape[1] // ncol), lambda j, i: (jnp.minimum(i + 1, last) * (rows // HALO), j))
              for d in dparts]
    out_cur = pl.BlockSpec((rows, D_MODEL), lambda j, i: (i, j))
    wspec = pl.BlockSpec((CONV_WIDTH, D_MODEL), lambda j, i: (0, col0 + j))
    bspec = pl.BlockSpec((1, D_MODEL), lambda j, i: (0, col0 + j))
    wout = pl.BlockSpec((CONV_WIDTH, D_MODEL), lambda j, i: (0, j))
    bout = pl.BlockSpec((1, D_MODEL), lambda j, i: (0, j))

    def body(*refs):
        dc_refs, da_refs = refs[:np_], refs[np_:2 * np_]
        c_ref, p_ref, a_ref, w_ref, b_ref, du_ref, dw_ref, db_ref = refs[2 * np_:]
        i = pl.program_id(1)
        row = lax.broadcasted_iota(jnp.int32, (rows, D_MODEL), 0)
        curv = c_ref[...]
        beforev = p_ref[...] * jnp.where(i > 0, 1.0, 0.0)
        side_by_side = lambda rs: rs[0][...] if np_ == 1 else jnp.concatenate([r[...] for r in rs], axis=1)

        def dpre_of(pre, d):
            s = _sigmoid(pre)
            return d * (s * (1.0 + pre * (1.0 - s)))

        dp_c = dpre_of(_conv_pre(curv, beforev, w_ref, b_ref), side_by_side(dc_refs) * _valid_rows(i, rows))
        dp_a = dpre_of(_conv_pre(a_ref[...], curv[rows - HALO:], w_ref, b_ref),
                       side_by_side(da_refs) * jnp.where(i < last, 1.0, 0.0))
        du = w_ref[CONV_WIDTH - 1:CONV_WIDTH, :] * dp_c
        for j in range(1, CONV_WIDTH):
            tail = jnp.concatenate([jnp.zeros((rows - HALO, D_MODEL), F32), pltpu.roll(dp_a, HALO - j, 0)], axis=0)
            up = jnp.where(row < rows - j, pltpu.roll(dp_c, rows - j, 0), tail)
            du = du + w_ref[CONV_WIDTH - 1 - j:CONV_WIDTH - j, :] * up
        du_ref[...] = du.astype(BF16)

        @pl.when(i == 0)
        def _():
            dw_ref[...] = jnp.zeros_like(dw_ref)
            db_ref[...] = jnp.zeros_like(db_ref)

        for k in range(CONV_WIDTH):
            dw_ref[k:k + 1, :] += jnp.sum(dp_c * _shift_rows(curv, beforev, CONV_WIDTH - 1 - k), axis=0, keepdims=True)
        db_ref[...] += jnp.sum(dp_c, axis=0, keepdims=True)

    width = ncol * D_MODEL
    return _call(body, name=name, grid=(ncol, steps),
                 in_specs=dcur + dafter + [cur, before, after, wspec, bspec], out_specs=[out_cur, wout, bout],
                 out_shape=[jax.ShapeDtypeStruct((proj.shape[0], width), BF16),
                            jax.ShapeDtypeStruct((CONV_WIDTH, width), F32), jax.ShapeDtypeStruct((1, width), F32)],
                 sem=("parallel", "arbitrary"))(*dparts, *dparts, proj, proj, proj, conv_w, conv_b)


def _head_expand():
    e = np.zeros((LANES, SSM_INNER), np.float32)
    for h in range(SSM_HEADS):
        e[h, h * HEAD_DIM:(h + 1) * HEAD_DIM] = 1.0
    return jnp.asarray(e, dtype=BF16)


def _softplus(x):
    return jnp.maximum(x, 0.0) + jnp.log(1.0 + jnp.exp(-jnp.abs(x)))


def _bf16_parts(x):
    hi = x.astype(BF16)
    rest = x - hi.astype(F32)
    mid = rest.astype(BF16)
    return hi, mid, (rest - mid.astype(F32)).astype(BF16)


@jax.custom_vjp
def _times_01(x, m):
    return sum(jnp.dot(p, m, preferred_element_type=F32) for p in _bf16_parts(x))


def _times_01_bwd(m, g):
    return sum(lax.dot_general(p, m, _NT, preferred_element_type=F32) for p in _bf16_parts(g)), jnp.zeros_like(m)


_times_01.defvjp(lambda x, m: (_times_01(x, m), m), _times_01_bwd)


def _causal_ones():
    l = lax.broadcasted_iota(jnp.int32, (BLOCK, BLOCK), 0)
    s = lax.broadcasted_iota(jnp.int32, (BLOCK, BLOCK), 1)
    return (l >= s).astype(BF16)


@jax.custom_vjp
def _cumsum_rows(a):
    return sum(jnp.dot(_causal_ones(), p, preferred_element_type=F32) for p in _bf16_parts(a))


def _cumsum_rows_bwd(_, g):
    tn = (((0,), (0,)), ((), ()))
    return (sum(lax.dot_general(_causal_ones(), p, tn, preferred_element_type=F32) for p in _bf16_parts(g)),)


_cumsum_rows.defvjp(lambda a: (_cumsum_rows(a), None), _cumsum_rows_bwd)


def _ssd_heads(dt_tile, bias, alog, dsk, expand):
    dt = _softplus(dt_tile + bias)
    a = dt * (-jnp.exp(alog))
    one_row = lambda v: jnp.broadcast_to(v, (HALO, LANES))
    return _times_01(jnp.concatenate([dt, _cumsum_rows(a), one_row(jnp.sum(a, axis=0, keepdims=True)), one_row(dsk)],
                                     axis=0), expand)


HEADS_ROWS = 2 * BLOCK + 2 * HALO


def _ssd_group(xs, per_lane, bg, cg, state):
    l = lax.broadcasted_iota(jnp.int32, (BLOCK, BLOCK), 0)
    s = lax.broadcasted_iota(jnp.int32, (BLOCK, BLOCK), 1)
    causal = l >= s
    first_head = s < HEAD_DIM
    dtx, cs = per_lane[0:BLOCK], per_lane[BLOCK:2 * BLOCK]
    tot, dsk = per_lane[2 * BLOCK:2 * BLOCK + 1], per_lane[2 * BLOCK + HALO:2 * BLOCK + HALO + 1]
    bb, cb16 = bg.astype(BF16), cg.astype(BF16)
    cb = lax.dot_general(cb16, bb, _NT, preferred_element_type=F32)
    xr = xs * dtx
    y_diag = []
    for p in range(GROUP_W // LANES):
        lanes = slice(p * LANES, (p + 1) * LANES)
        c_pair = cs[:, lanes]
        c_swap = _swap_halves(c_pair)
        m = []
        for c_head in (jnp.where(first_head, c_pair, c_swap), jnp.where(first_head, c_swap, c_pair)):
            m.append(cb * jnp.exp(jnp.where(causal, c_head - c_head.T, -jnp.inf)))
        x_pair = xr[:, lanes]
        x_diag = jnp.concatenate([jnp.where(first_head, x_pair, 0.0), jnp.where(first_head, 0.0, x_pair)], axis=0)
        y_diag.append(jnp.dot(jnp.concatenate(m, axis=1).astype(BF16), x_diag.astype(BF16),
                              preferred_element_type=F32))
    st = lax.dot_general(bb, (xr * jnp.exp(tot - cs)).astype(BF16), (((0,), (0,)), ((), ())),
                         preferred_element_type=F32)
    new_state = state * jnp.exp(tot) + st
    y_off = jnp.dot(cb16, state.astype(BF16), preferred_element_type=F32) * jnp.exp(cs)
    return jnp.concatenate(y_diag, axis=1) + y_off + dsk * xs, new_state


BC_WIDTH = SSM_GROUPS * SSM_STATE


def _ssd_specs(chunk):
    xs = pl.BlockSpec((BLOCK, SSM_INNER), lambda c: (chunk(c), 0))
    dt = pl.BlockSpec((BLOCK, LANES), lambda c: (chunk(c), DT_TILE))
    expand = _full((LANES, SSM_INNER))
    b = pl.BlockSpec((BLOCK, BC_WIDTH), lambda c: (chunk(c), SSM_INNER // BC_WIDTH))
    cc = pl.BlockSpec((BLOCK, BC_WIDTH), lambda c: (chunk(c), SSM_INNER // BC_WIDTH + 1))
    par = _full((1, LANES))
    state = pl.BlockSpec((None, SSM_STATE, SSM_INNER), lambda c: (chunk(c), 0, 0))
    return xs, dt, expand, b, cc, par, state


def _group_lanes(g):
    return slice(g * GROUP_W, (g + 1) * GROUP_W), slice(g * SSM_STATE, (g + 1) * SSM_STATE)


def ssd_fwd(xbc, proj, expand, bias, alog, dsk):
    nb = xbc.shape[0] // BLOCK
    xs, dt, ex, b, cc, par, state = _ssd_specs(lambda c: c)

    def body(x_ref, dt_ref, e_ref, bi_ref, al_ref, dk_ref, b_ref, c_ref, y_ref, sp_ref, st_ref):
        @pl.when(pl.program_id(0) == 0)
        def _():
            st_ref[...] = jnp.zeros_like(st_ref)

        per_lane = _ssd_heads(dt_ref[...], bi_ref[...], al_ref[...], dk_ref[...], e_ref[...])
        for g in range(SSM_GROUPS):
            wide, tile = _group_lanes(g)
            entering = st_ref[:, wide]
            sp_ref[:, wide] = entering
            y_ref[:, wide], st_ref[:, wide] = _ssd_group(x_ref[:, wide], per_lane[:, wide], b_ref[:, tile],
                                                         c_ref[:, tile], entering)

    return _call(body, name="ssd_fwd", grid=(nb,), in_specs=[xs, dt, ex, par, par, par, b, cc],
                 out_specs=[xs, state],
                 out_shape=[jax.ShapeDtypeStruct((nb * BLOCK, SSM_INNER), F32),
                            jax.ShapeDtypeStruct((nb, SSM_STATE, SSM_INNER), F32)],
                 scratch=[pltpu.VMEM((SSM_STATE, SSM_INNER), F32)],
                 sem=("arbitrary",))(xbc, proj, expand, bias, alog, dsk, xbc, xbc)


def ssd_bwd(dy, xbc, proj, expand, bias, alog, dsk, states, comm):
    nb = xbc.shape[0] // BLOCK
    last = nb - 1
    xs, dt, ex, b, cc, par, state = _ssd_specs(lambda c: last - c)
    tile = pl.BlockSpec((BLOCK, LANES), lambda c: (last - c, 0))
    nspec = pl.BlockSpec((BLOCK, BC_WIDTH), lambda c: (last - c, 0))

    def body(dy_ref, x_ref, dt_ref, e_ref, bi_ref, al_ref, dk_ref, b_ref, c_ref, sp_ref,
             dx_ref, ddt_ref, db_ref, dc_ref, dbi_ref, dal_ref, ddk_ref, ds_ref):
        @pl.when(pl.program_id(0) == 0)
        def _():
            ds_ref[...] = jnp.zeros_like(ds_ref)
            dbi_ref[...] = jnp.zeros_like(dbi_ref)
            dal_ref[...] = jnp.zeros_like(dal_ref)
            ddk_ref[...] = jnp.zeros_like(ddk_ref)

        expand = e_ref[...]
        per_lane, heads_vjp = jax.vjp(lambda t, bi, al, dk: _ssd_heads(t, bi, al, dk, expand), dt_ref[...], bi_ref[...],
                                      al_ref[...], dk_ref[...])
        d_per_lane = []
        for g in range(SSM_GROUPS):
            wide, tile_lanes = _group_lanes(g)
            _, vjp = jax.vjp(_ssd_group, x_ref[:, wide], per_lane[:, wide], b_ref[:, tile_lanes], c_ref[:, tile_lanes],
                             sp_ref[:, wide])
            (dx_ref[:, wide], d_lanes, db_ref[:, tile_lanes], dc_ref[:, tile_lanes],
             ds_ref[:, wide]) = vjp((dy_ref[:, wide], ds_ref[:, wide]))
            d_per_lane.append(d_lanes)
        ddt, dbi, dal, ddk = heads_vjp(jnp.concatenate(d_per_lane, axis=1))
        ddt_ref[...] = ddt.astype(BF16)
        dbi_ref[...] += dbi
        dal_ref[...] += dal
        ddk_ref[...] += ddk

    def at():
        c = pl.program_id(0)
        return c == 0, c == 0, c == last

    par_shape = jax.ShapeDtypeStruct((1, LANES), F32)
    return _call_with_comm(
        body, comm, at, (dy, xbc, proj, expand, bias, alog, dsk, xbc, xbc, states), name="ssd_bwd",
        grid=(nb,), in_specs=[xs, xs, dt, ex, par, par, par, b, cc, state],
        out_specs=[xs, tile, nspec, nspec, par, par, par],
        out_shape=[jax.ShapeDtypeStruct((nb * BLOCK, SSM_INNER), F32), jax.ShapeDtypeStruct((nb * BLOCK, LANES), BF16),
                   jax.ShapeDtypeStruct((nb * BLOCK, BC_WIDTH), F32), jax.ShapeDtypeStruct((nb * BLOCK, BC_WIDTH), F32),
                   par_shape, par_shape, par_shape],
        scratch=[pltpu.VMEM((SSM_STATE, SSM_INNER), F32)])


SLAB_ROWS = 16
SLAB_META_ROW = 8
SLAB_META_SHAPE = (8, 2 * D_MODEL)
SLAB_LOSS_ROW = 7


def pack_small(dcw, dcb, dgpre, dgpost, dbias, dalog, ddsk, dsinks, dgn, dmeta, loss_tile):
    def body(cw, cb, gpre, gpost, dtb, al, dk, sk, gn, meta, loss, o_ref):
        o_ref[...] = jnp.zeros_like(o_ref)
        o_ref[SLAB_LOSS_ROW:SLAB_LOSS_ROW + 1, 0:LANES] = loss[0:1, :]
        o_ref[0:CONV_WIDTH, :] = cw[...]
        o_ref[4:5, :] = cb[...]
        o_ref[5:6, 0:1024] = gpre[...]
        o_ref[5:6, 1024:2048] = gpost[...]
        o_ref[5:6, 2048:2176] = dtb[...]
        o_ref[5:6, 2176:2304] = al[...]
        o_ref[5:6, 2304:2432] = dk[...]
        o_ref[5:6, 2432:2560] = sk[...]
        o_ref[6:7, 0:SSM_INNER] = gn[...]
        o_ref[SLAB_META_ROW:SLAB_ROWS, 0:SLAB_META_SHAPE[1]] = meta[...]

    args = (dcw, dcb, dgpre, dgpost, dbias, dalog, ddsk, dsinks, dgn, dmeta, loss_tile)
    return _call(body, name="pack_small", in_specs=[_full(a.shape) for a in args],
                 out_specs=_full((SLAB_ROWS, CONV_DIM)), out_shape=jax.ShapeDtypeStruct((SLAB_ROWS, CONV_DIM), F32))(*args)


def _lane_tile(v):
    return jnp.pad(v, ((0, 0), (0, LANES - v.shape[1])))


def kernel(x, meta_tokens, g_pre, w_in, conv_w, conv_b, dt_bias, a_log, d_skip, attn_sinks, g_ssm_norm, w_out_att, w_out_ssm, w_out, g_post, loss_target, m_meta_tokens, m_g_pre, m_w_in, m_conv_w, m_conv_b, m_dt_bias, m_a_log, m_d_skip, m_attn_sinks, m_g_ssm_norm, m_w_out_att, m_w_out_ssm, m_w_out, m_g_post, v_meta_tokens, v_g_pre, v_w_in, v_conv_w, v_conv_b, v_dt_bias, v_a_log, v_d_skip, v_attn_sinks, v_g_ssm_norm, v_w_out_att, v_w_out_ssm, v_w_out, v_g_post):
    chip = _chip_index()

    conv_w_rows = jnp.pad(conv_w[0], ((0, 2 * 8 - CONV_WIDTH), (0, 0)))
    w_in_t, m_w_in_t, v_w_in_t = w_in[0].T, m_w_in[0].T, v_w_in[0].T
    gathered_w_in, g_conv_w, g_meta = run_comm("gather_w_in",
                                               TwoLevelGather([pack_w_in(w_in_t), conv_w_rows, meta_tokens]))
    w_all_t = unpack_w_in(gathered_w_in)
    cw_full = g_conv_w[:, :CONV_WIDTH].transpose(1, 0, 2).reshape(CONV_WIDTH, CONV_DIM)
    meta_full = g_meta.transpose(1, 0, 2).reshape(N_META, D_MODEL)
    behind_w_in = 0.0 * g_meta[0, 0, 0]
    w_out_flight, w_out_started = chip_exchange_start(
        "gather_w_out_start", [(w[0] + behind_w_in).astype(BF16) for w in (w_out_att, w_out_ssm, w_out)], False)

    h, u = prep(x, meta_full, g_pre)
    proj = project("in_proj", u, w_all_t, w_out_started)

    sinks3 = attn_sinks.reshape(ATT_Q_HEADS, 1, 1)
    a_att = attn_fwd(proj, sinks3)

    xbc = conv_fwd(proj, cw_full, conv_b)
    expand = _head_expand()
    head_pars = (_lane_tile(dt_bias), _lane_tile(a_log), _lane_tile(d_skip))
    y_ssd, states = ssd_fwd(xbc, proj, expand, *head_pars)
    woa, wos, wo = [g.reshape(-1, D_MODEL) for g in chip_exchange_wait("gather_w_out_wait", w_out_flight, False, y_ssd)]

    (yn, merged, dout, dy_att, dy_ssm, da_att, dy_ssd, dz_ssm, dga, dgs, dres, loss_tile, dg_post, dgn) = tail(
        y_ssd, proj, a_att, x, loss_target, woa, wos, wo, g_ssm_norm, g_post)

    dwo = mm_tn("out_proj_dw", merged, dout)
    dwoa = mm_tn("att_out_dw", a_att, dy_att)
    dwos = mm_tn("ssm_out_dw", yn, dy_ssm)
    dq, dz_att, dk, dv, dkmeta, dvmeta, dsinks3 = attn_bwd(da_att, proj, sinks3)
    dk = dk.at[PAD_ROWS:BLOCK].add(dkmeta).astype(BF16)
    dv = dv.at[PAD_ROWS:BLOCK].add(dvmeta).astype(BF16)

    def pieces(g):
        return g.reshape(4, 2, g.shape[0] // 8, g.shape[1])

    def to_owner(g):
        return (lambda ref, dev: ref.at[_chip_of(dev), dev[2]], (g.shape[0] // 8, g.shape[1]))

    (dxs, ddt_tile, dbg, dcg, dbias, dalog, ddsk), sent_w_out = ssd_bwd(
        dy_ssd, xbc, proj, expand, *head_pars, states,
        DirectExchange([pieces(dwoa), pieces(dwos), pieces(dwo)], [to_owner(dwoa), to_owner(dwos), to_owner(dwo)],
                       ALL_MASKS, "dev", 8))
    dxs_raw, dcw_xs, dcb_xs = conv_bwd("conv_bwd_x", [dxs], 0, proj, cw_full, conv_b)
    dbc_raw, dcw_bc, dcb_bc = conv_bwd("conv_bwd_bc", [dbg, dcg], SSM_INNER // D_MODEL, proj, cw_full, conv_b)
    dcw = jnp.concatenate([dcw_xs, dcw_bc], axis=1)
    dcb = jnp.concatenate([dcb_xs, dcb_bc], axis=1)

    narrow = jnp.concatenate([dk, dv, ddt_tile, jnp.zeros((dk.shape[0], N_ACT - N_ALIGNED), BF16)], axis=1)
    dproj = [dz_ssm, dxs_raw, dbc_raw, dq, dz_att, dga, dgs, narrow]
    dw_all_t = weight_grad_t("in_proj_dw", dproj, u)

    half_rows = PACK_W // 2
    partial = pack_grad_w_in(dw_all_t).reshape(4, 2, half_rows, D_MODEL)
    from_sibling, = run_comm("pair_grads", DirectExchange(
        [partial], [(lambda ref, dev: ref.at[pl.ds(0, 4), dev[2]], (4, half_rows, D_MODEL))], SIBLING_MASK, "core", 2,
        keep_own=False))
    chip_sum = sum_pair(partial, from_sibling)
    grads_flight, started = chip_exchange_start("reduce_w_in_start", [chip_sum], True)
    du = project_back("in_proj_dx", dproj, w_all_t, started)
    grad_x, dmeta, dg_pre = prep_bwd(h, du, dres, g_pre)

    halves = [sum_slots("sum_" + nm, r) for nm, r in zip(("w_out_att", "w_out_ssm", "w_out"), sent_w_out)]
    shared = run_comm("share_w_out", DirectExchange(halves, [None] * 3, SIBLING_MASK, "core", 2))
    g_woa, g_wos, g_wo = [f.reshape(2 * f.shape[1], f.shape[2]) for f in shared]
    d_woa, nm_woa, nv_woa = adamw_rows("adamw_w_out_att", g_woa, w_out_att[0], m_w_out_att[0], v_w_out_att[0])
    d_wos, nm_wos, nv_wos = adamw_rows("adamw_w_out_ssm", g_wos, w_out_ssm[0], m_w_out_ssm[0], v_w_out_ssm[0])
    d_wo, nm_wo, nv_wo = adamw_rows("adamw_w_out", g_wo, w_out[0], m_w_out[0], v_w_out[0])

    sent_w_in, = chip_exchange_wait("reduce_w_in_wait", grads_flight, True, d_wos)
    slab = pack_small(dcw, dcb, dg_pre, dg_post, dbias, dalog, ddsk, _lane_tile(dsinks3.reshape(1, ATT_Q_HEADS)), dgn,
                      dmeta.reshape(SLAB_META_SHAPE), loss_tile)
    shared_w_in, slabs = run_comm("share_w_in", Both(
        DirectExchange([sum_slots("sum_w_in", sent_w_in)], [None], SIBLING_MASK, "core", 2),
        DirectExchange([slab], [None], ALL_MASKS, "dev", 8)))
    small = sum_slots("sum_small", slabs)
    loss = small[SLAB_LOSS_ROW, 0]
    g_w_in, d_w_in, nm_w_in, nv_w_in = [
        a.T for a in adamw_w_in(shared_w_in.reshape(PACK_W, D_MODEL), w_in_t, m_w_in_t, v_w_in_t)]

    cw_cols = CONV_DIM // 4
    meta_cols = D_MODEL // 4
    g_small = {
        "meta_tokens": lax.dynamic_slice(
            small[SLAB_META_ROW:SLAB_ROWS, 0:SLAB_META_SHAPE[1]].reshape(N_META, D_MODEL), (0, chip * meta_cols),
            (N_META, meta_cols)),
        "g_pre": small[5:6, 0:1024],
        "conv_w": lax.dynamic_slice(small, (0, chip * cw_cols), (CONV_WIDTH, cw_cols)),
        "conv_b": small[4:5, :],
        "dt_bias": small[5:6, 2048:2048 + SSM_HEADS],
        "a_log": small[5:6, 2176:2176 + SSM_HEADS],
        "d_skip": small[5:6, 2304:2304 + SSM_HEADS],
        "attn_sinks": small[5:6, 2432:2432 + ATT_Q_HEADS],
        "g_ssm_norm": small[6:7, 0:SSM_INNER],
        "g_post": small[5:6, 1024:2048],
    }
    names = list(g_small)
    w_small = dict(meta_tokens=meta_tokens, g_pre=g_pre, conv_w=conv_w[0], conv_b=conv_b, dt_bias=dt_bias, a_log=a_log,
                   d_skip=d_skip, attn_sinks=attn_sinks, g_ssm_norm=g_ssm_norm, g_post=g_post)
    m_small = dict(meta_tokens=m_meta_tokens, g_pre=m_g_pre, conv_w=m_conv_w[0], conv_b=m_conv_b, dt_bias=m_dt_bias,
                   a_log=m_a_log, d_skip=m_d_skip, attn_sinks=m_attn_sinks, g_ssm_norm=m_g_ssm_norm, g_post=m_g_post)
    v_small = dict(meta_tokens=v_meta_tokens, g_pre=v_g_pre, conv_w=v_conv_w[0], conv_b=v_conv_b, dt_bias=v_dt_bias,
                   a_log=v_a_log, d_skip=v_d_skip, attn_sinks=v_attn_sinks, g_ssm_norm=v_g_ssm_norm, g_post=v_g_post)
    upd = dict(zip(names, adamw_small([g_small[k] for k in names], [w_small[k] for k in names],
                                      [m_small[k] for k in names], [v_small[k] for k in names])))

    lead = {"conv_w"}

    def shaped(name, a):
        return a[None] if name in lead else a

    grads = dict(g_small, w_in=g_w_in, w_out_att=g_woa, w_out_ssm=g_wos, w_out=g_wo)
    deltas = dict({k: upd[k][0] for k in names}, w_in=d_w_in, w_out_att=d_woa, w_out_ssm=d_wos, w_out=d_wo)
    new_m = dict({k: upd[k][1] for k in names}, w_in=nm_w_in, w_out_att=nm_woa, w_out_ssm=nm_wos, w_out=nm_wo)
    new_v = dict({k: upd[k][2] for k in names}, w_in=nv_w_in, w_out_att=nv_woa, w_out_ssm=nv_wos, w_out=nv_wo)
    lead |= {"w_in", "w_out_att", "w_out_ssm", "w_out"}
    order = ["meta_tokens", "g_pre", "w_in", "conv_w", "conv_b", "dt_bias", "a_log", "d_skip", "attn_sinks",
             "g_ssm_norm", "w_out_att", "w_out_ssm", "w_out", "g_post"]
    outs = [loss, grad_x]
    for group in (grads, deltas, new_m, new_v):
        outs += [shaped(k, group[k]) for k in order]
    return tuple(outs)
```

```python
import functools

import numpy as np
import jax
import jax.numpy as jnp
from jax import lax
from jax.experimental import pallas as pl
from jax.experimental.pallas import tpu as pltpu

F32 = jnp.float32
BF16 = jnp.bfloat16
HI = lax.Precision.HIGHEST

D_MODEL = 1024
N_META = 16
BLOCK = 128
PAD_ROWS = BLOCK - N_META
NORM_EPS = 1e-6
HEAD_DIM = 64
ATT_Q_HEADS = 16
ATT_KV_HEADS = 4
ATT_GROUP = 4
SSM_INNER = 2048
SSM_HEADS = 32
SSM_GROUPS = 4
SSM_HEADS_PER_GROUP = 8
SSM_STATE = 128
CONV_WIDTH = 4
CONV_DIM = 3072
LANES = 128

ADAM_LR = 0.001
ADAM_B1 = 0.9
ADAM_B2 = 0.999
ADAM_EPS = 1e-08
ADAM_WD = 0.01
ADAM_STEP = 10

VMEM_LIMIT = 48 * 1024 * 1024

SHARD_W = 2440
PACK_W = 2560
SHARD_STRIDE = 2432
N_ALIGNED = 9856
N_ACT = 10240
SEG = {
    "q": (0, 1024, 5120), "k": (1024, 256, 9216), "v": (1280, 256, 9472), "z_att": (1536, 1024, 6144),
    "z_ssm": (2560, 2048, 0), "xbc": (4608, 3072, 2048), "dt": (7680, 128, 9728),
    "gate_att": (7808, 1024, 7168), "gate_ssm": (8832, 1024, 8192),
}
DT_STORED_START = 7680
DT_PAD = LANES - SSM_HEADS


def _act_col(aligned_col):
    for a0, w, p0 in SEG.values():
        if a0 <= aligned_col < a0 + w:
            return p0 + aligned_col - a0
    raise ValueError(aligned_col)


def _call(body, *, name, out_shape, in_specs, out_specs, grid=(), scratch=(), sem=None, aliases=None):
    return pl.pallas_call(
        body, out_shape=out_shape, grid=grid, in_specs=in_specs, out_specs=out_specs, scratch_shapes=list(scratch),
        name=name, input_output_aliases=aliases or {},
        compiler_params=pltpu.CompilerParams(dimension_semantics=sem, vmem_limit_bytes=VMEM_LIMIT))


def _full(shape):
    n = len(shape)
    return pl.BlockSpec(shape, lambda *_: (0,) * n)


def _chip_index():
    return lax.axis_index("x") * 2 + lax.axis_index("y")


_sigmoid = jax.nn.sigmoid


def _silu(z):
    return z * _sigmoid(z)


def _rms(x, g):
    return x * lax.rsqrt(jnp.mean(x * x, axis=-1, keepdims=True) + NORM_EPS) * g


def _peer(mask):
    x, y, c = lax.axis_index("x"), lax.axis_index("y"), lax.axis_index("c")
    return ((1 - x) if mask & 4 else x, (1 - y) if mask & 2 else y, (1 - c) if mask & 1 else c)


def _me():
    return lax.axis_index("x"), lax.axis_index("y"), lax.axis_index("c")


def _chip_of(dev):
    return 2 * dev[0] + dev[1]


CHIP_MASKS = (4, 2, 6)
ALL_MASKS = (1, 2, 3, 4, 5, 6, 7)
SIBLING_MASK = (1,)


def _remote(src, dst, send_sem, recv_sem, dev):
    return pltpu.make_async_remote_copy(src_ref=src, dst_ref=dst, send_sem=send_sem, recv_sem=recv_sem,
                                        device_id=dev, device_id_type=pl.DeviceIdType.MESH)


class _StagedCopy:
    def __init__(self, src, stage, dst, load_sem, store_sem):
        self.load = pltpu.make_async_copy(src, stage, load_sem)
        self.store = pltpu.make_async_copy(stage, dst, store_sem)

    def start(self):
        self.load.start()
        self.load.wait()
        self.store.start()

    def wait(self):
        self.store.wait()


class DirectExchange:
    def __init__(self, arrays, pieces, masks, slot_kind, nslots, keep_own=True):
        self.arrays, self.pieces, self.masks, self.slot_kind = list(arrays), list(pieces), masks, slot_kind
        self.keep_own = keep_own
        n, nk = len(arrays), len(masks)
        shapes = [a.shape if p is None else p[1] for a, p in zip(arrays, pieces)]
        self.out_shape = [jax.ShapeDtypeStruct((nslots,) + tuple(s), a.dtype) for s, a in zip(shapes, arrays)]
        self.scratch = [pltpu.SemaphoreType.DMA((n * nk,)), pltpu.SemaphoreType.DMA((n * nk,))]
        if keep_own:
            self.scratch += [pltpu.SemaphoreType.DMA((2 * n,))] + [pltpu.VMEM(s, a.dtype) for s, a in zip(shapes, arrays)]
        self.has_mid = False

    def _copies(self, ins, outs, scratch):
        send_sems, recv_sems = scratch[:2]
        me = _me()
        slot = {"chip": _chip_of(me), "dev": 4 * me[0] + 2 * me[1] + me[2], "core": me[2]}[self.slot_kind]
        nk = len(self.masks)

        def piece(a, dev):
            return ins[a] if self.pieces[a] is None else self.pieces[a][0](ins[a], dev)

        local = []
        if self.keep_own:
            local_sems, stages = scratch[2], scratch[3:]
            local = [_StagedCopy(piece(a, me), stages[a], outs[a].at[slot], local_sems.at[2 * a], local_sems.at[2 * a + 1])
                     for a in range(len(ins))]
        remote = []
        for a in range(len(ins)):
            for ki, mask in enumerate(self.masks):
                dev = _peer(mask)
                remote.append(_remote(piece(a, dev), outs[a].at[slot], send_sems.at[a * nk + ki],
                                      recv_sems.at[a * nk + ki], dev))
        return local, remote

    def start(self, ins, outs, scratch):
        local, remote = self._copies(ins, outs, scratch)
        for cp in remote + local:
            cp.start()

    def finish(self, ins, outs, scratch):
        local, remote = self._copies(ins, outs, scratch)
        for cp in remote + local:
            cp.wait()


SPLIT_ROWS = 16


class TwoLevelGather:
    def __init__(self, arrays):
        self.arrays = list(arrays)
        n = len(arrays)
        self.out_shape = [jax.ShapeDtypeStruct((4,) + a.shape, a.dtype) for a in arrays]
        self.scratch = ([pltpu.SemaphoreType.DMA((4 * n,)), pltpu.SemaphoreType.DMA((4 * n,)),
                         pltpu.SemaphoreType.DMA((3 * n,)), pltpu.SemaphoreType.DMA((3 * n,)),
                         pltpu.SemaphoreType.DMA((2 * n,))] + [pltpu.VMEM(a.shape, a.dtype) for a in arrays])
        self.has_mid = True

    def _copies(self, ins, outs, scratch):
        ici_send, ici_recv, fwd_send, fwd_recv, local_sems = scratch[:5]
        stages = scratch[5:]
        me = _me()
        sibling, in_x, in_y, diagonal = _peer(1), _peer(4), _peer(2), _peer(6)
        plan = []
        for a in range(len(ins)):
            half = ins[a].shape[0] // 2
            first = half // 2 if half % (2 * SPLIT_ROWS) == 0 else half
            mine = pl.ds(me[2] * half, half)
            local = _StagedCopy(ins[a], stages[a], outs[a].at[_chip_of(me)], local_sems.at[2 * a], local_sems.at[2 * a + 1])

            def ici(k, src, dst, dev):
                return _remote(src, dst, ici_send.at[4 * a + k], ici_recv.at[4 * a + k], dev)

            def d2d(k, chip):
                zone = outs[a].at[_chip_of(chip), mine]
                return _remote(zone, zone, fwd_send.at[3 * a + k], fwd_recv.at[3 * a + k], sibling)

            own_zone = outs[a].at[_chip_of(me), mine]
            from_x = outs[a].at[_chip_of(in_x), pl.ds(me[2] * half, first)]
            onward = [ici(2, from_x, from_x, in_y), None]
            if first < half:
                from_y = outs[a].at[_chip_of(in_y), pl.ds(me[2] * half + first, half - first)]
                onward[1] = ici(3, from_y, from_y, in_x)
            plan.append(dict(
                local=local,
                own=[ici(0, ins[a].at[mine], own_zone, in_x), ici(1, ins[a].at[mine], own_zone, in_y)],
                onward=onward, sibling=[d2d(0, in_x), d2d(1, in_y), d2d(2, diagonal)]))
        return plan

    def start(self, ins, outs, scratch):
        for p in self._copies(ins, outs, scratch):
            for cp in p["own"]:
                cp.start()
            p["local"].start()

    def mid(self, ins, outs, scratch):
        plan = self._copies(ins, outs, scratch)
        for p in plan:
            for k in range(2):
                p["own"][k].wait_recv()
                if p["onward"][k] is not None:
                    p["onward"][k].start()
                p["sibling"][k].start()
        for p in plan:
            for cp in p["onward"]:
                if cp is not None:
                    cp.wait_recv()
            p["sibling"][2].start()

    def finish(self, ins, outs, scratch):
        for p in self._copies(ins, outs, scratch):
            for cp in p["sibling"]:
                cp.wait_recv()
            for cp in p["own"] + p["sibling"] + [cp for cp in p["onward"] if cp is not None]:
                cp.wait_send()
            p["local"].wait()


class Both:
    def __init__(self, a, b):
        self.a, self.b = a, b
        self.arrays, self.out_shape = a.arrays + b.arrays, a.out_shape + b.out_shape
        self.scratch = a.scratch + b.scratch
        self.has_mid = False
        assert not (a.has_mid or b.has_mid)

    def _parts(self, ins, outs, sems):
        na, sa = len(self.a.arrays), len(self.a.scratch)
        return (ins[:na], outs[:na], sems[:sa]), (ins[na:], outs[na:], sems[sa:])

    def start(self, ins, outs, sems):
        pa, pb = self._parts(ins, outs, sems)
        self.a.start(*pa)
        self.b.start(*pb)

    def finish(self, ins, outs, sems):
        pa, pb = self._parts(ins, outs, sems)
        self.a.finish(*pa)
        self.b.finish(*pb)


_ANY = pl.BlockSpec(memory_space=pl.ANY)


def run_comm(name, comm):
    n = len(comm.arrays)

    def body(*refs):
        ins, outs, sems = refs[:n], refs[n:2 * n], refs[2 * n:]
        comm.start(ins, outs, sems)
        if comm.has_mid:
            comm.mid(ins, outs, sems)
        comm.finish(ins, outs, sems)

    return pl.pallas_call(body, name=name, out_shape=comm.out_shape, in_specs=[_ANY] * n, out_specs=[_ANY] * n,
                          scratch_shapes=comm.scratch,
                          compiler_params=pltpu.CompilerParams(vmem_limit_bytes=VMEM_LIMIT))(*comm.arrays)


_HBM = pl.BlockSpec(memory_space=pltpu.HBM)
_SEM = pl.BlockSpec(memory_space=pltpu.SEMAPHORE)
_SIDE_EFFECT = pltpu.SideEffectType.DATAFLOW_SIDE_EFFECTING


def _chip_copies(srcs, lands, send_sems, recv_sems, by_target):
    me = _me()
    copies = []
    for a, (src, land) in enumerate(zip(srcs, lands)):
        for ki, mask in enumerate(CHIP_MASKS):
            dev = _peer(mask)
            k = a * len(CHIP_MASKS) + ki
            piece = src.at[_chip_of(dev)] if by_target else src
            copies.append(_remote(piece, land.at[_chip_of(me)], send_sems.at[k], recv_sems.at[k], dev))
    return copies


def chip_exchange_start(name, arrays, by_target):
    n = len(arrays)
    nsem = n * len(CHIP_MASKS)
    piece_shapes = [a.shape[1:] if by_target else a.shape for a in arrays]

    def body(*refs):
        srcs, lands = refs[:n], refs[n:2 * n]
        send_sems, recv_sems = refs[2 * n:2 * n + 2]
        token = refs[4 * n + 2]
        stages, local_sems = refs[4 * n + 3:5 * n + 3], refs[5 * n + 3]
        me = _me()
        for cp in _chip_copies(srcs, lands, send_sems, recv_sems, by_target):
            cp.start()
        for a in range(n):
            own = _StagedCopy(srcs[a].at[_chip_of(me)] if by_target else srcs[a], stages[a], lands[a].at[_chip_of(me)],
                              local_sems.at[2 * a], local_sems.at[2 * a + 1])
            own.start()
            own.wait()
        token[...] = jnp.zeros_like(token)

    lands = [lax.empty((4,) + tuple(s), a.dtype) for s, a in zip(piece_shapes, arrays)]
    hbm = lambda a: pltpu.HBM(a.shape, a.dtype)
    res = pl.pallas_call(
        body, name=name,
        out_shape=(pltpu.SemaphoreType.DMA((nsem,)), pltpu.SemaphoreType.DMA((nsem,)), *[hbm(a) for a in arrays],
                   *[hbm(l) for l in lands], jax.ShapeDtypeStruct((8, LANES), F32)),
        in_specs=(_HBM,) * (2 * n), out_specs=(_SEM, _SEM) + (_HBM,) * (2 * n) + (pl.BlockSpec(memory_space=pltpu.VMEM),),
        input_output_aliases={i: i + 2 for i in range(2 * n)},
        scratch_shapes=[pltpu.VMEM(tuple(s), a.dtype) for s, a in zip(piece_shapes, arrays)]
        + [pltpu.SemaphoreType.DMA((2 * n,))],
        compiler_params=pltpu.CompilerParams(has_side_effects=_SIDE_EFFECT, vmem_limit_bytes=VMEM_LIMIT),
    )(*[pltpu.with_memory_space_constraint(a, pltpu.HBM) for a in arrays + lands])
    return res[:-1], res[-1]


def chip_exchange_wait(name, in_flight, by_target, after):
    send_sems, recv_sems, *thru = in_flight
    n = len(thru) // 2

    def body(*refs):
        srcs, lands, (send, recv) = refs[:n], refs[n:2 * n], refs[2 * n:2 * n + 2]
        for cp in _chip_copies(srcs, lands, send, recv, by_target):
            cp.wait_send()
            cp.wait_recv()

    return pl.pallas_call(
        body, name=name, out_shape=tuple(pltpu.HBM(t.shape, t.dtype) for t in thru),
        in_specs=(_HBM,) * (2 * n) + (_SEM, _SEM, pl.BlockSpec(memory_space=pl.ANY)), out_specs=(_HBM,) * (2 * n),
        input_output_aliases={i: i for i in range(2 * n)},
        compiler_params=pltpu.CompilerParams(has_side_effects=_SIDE_EFFECT),
    )(*thru, send_sems, recv_sems, after)[n:]


def _call_with_comm(body, comm, steps, args, *, name, out_shape, in_specs, out_specs, grid, scratch=()):
    ni, no, ns, nc = len(in_specs), len(out_specs), len(scratch), len(comm.arrays)

    def full_body(*refs):
        ins, cins = refs[:ni], refs[ni:ni + nc]
        outs, couts = refs[ni + nc:ni + nc + no], refs[ni + nc + no:ni + 2 * nc + no]
        scr, csems = refs[ni + 2 * nc + no:ni + 2 * nc + no + ns], refs[ni + 2 * nc + no + ns:]
        first, middle, last = steps()
        pl.when(first)(lambda: comm.start(cins, couts, csems))
        if comm.has_mid:
            pl.when(middle)(lambda: comm.mid(cins, couts, csems))
        body(*ins, *outs, *scr)
        pl.when(last)(lambda: comm.finish(cins, couts, csems))

    res = pl.pallas_call(
        full_body, name=name, out_shape=list(out_shape) + comm.out_shape, grid=grid,
        in_specs=list(in_specs) + [_ANY] * nc, out_specs=list(out_specs) + [_ANY] * nc,
        scratch_shapes=list(scratch) + comm.scratch,
        compiler_params=pltpu.CompilerParams(dimension_semantics=("arbitrary",) * len(grid),
                                             vmem_limit_bytes=VMEM_LIMIT))(*args, *comm.arrays)
    return res[:no], res[no:]


def _shard_pieces(chip):
    if chip < 3:
        return [(0, SHARD_W, 8 * chip)]
    behind_dt = DT_STORED_START + SSM_HEADS - 3 * SHARD_W
    return [(0, behind_dt, 24), (behind_dt, SHARD_W - behind_dt, behind_dt + 24 + DT_PAD)]


W_IN_COLS = 256


def pack_w_in(wt):
    def body(w_ref, o_ref, pad_ref):
        chip = _chip_index()
        pad_ref[...] = jnp.zeros_like(pad_ref)
        for cv in range(4):
            @pl.when(chip == cv)
            def _():
                for src, n, dst in _shard_pieces(cv):
                    pad_ref[dst:dst + n, :] = w_ref[src:src + n, :]
        o_ref[...] = pad_ref[...].astype(BF16)

    return _call(body, name="pack_w_in", grid=(D_MODEL // W_IN_COLS,),
                 in_specs=[pl.BlockSpec((SHARD_W, W_IN_COLS), lambda i: (0, i))],
                 out_specs=pl.BlockSpec((PACK_W, W_IN_COLS), lambda i: (0, i)),
                 out_shape=jax.ShapeDtypeStruct((PACK_W, D_MODEL), BF16),
                 scratch=[pltpu.VMEM((PACK_W, W_IN_COLS), F32)], sem=("parallel",))(wt)


def _tile_runs():
    runs, fix = [], []
    for t in range(N_ALIGNED // LANES):
        s = min(t // 19, 3)
        j = t - 19 * s
        p = _act_col(t * LANES)
        if runs and runs[-1][1] == s and runs[-1][0] + runs[-1][3] == p and runs[-1][2] + runs[-1][3] == j * LANES:
            runs[-1][3] += LANES
        else:
            runs.append([p, s, j * LANES, LANES])
        if j == 0 and s > 0:
            fix.append((p, s - 1))
    return runs, fix


def unpack_w_in(bg):
    runs, fix = _tile_runs()

    def body(b_ref, o_ref):
        for p, s, j, w in runs:
            o_ref[p:p + w, :] = b_ref[s, j:j + w, :]
        for p, s in fix:
            o_ref[p:p + LANES, :] = o_ref[p:p + LANES, :] + b_ref[s, SHARD_STRIDE:PACK_W, :]
        o_ref[N_ALIGNED:N_ACT, :] = jnp.zeros((N_ACT - N_ALIGNED, W_IN_COLS), BF16)

    return _call(body, name="unpack_w_in", grid=(D_MODEL // W_IN_COLS,),
                 in_specs=[pl.BlockSpec((4, PACK_W, W_IN_COLS), lambda i: (0, 0, i))],
                 out_specs=pl.BlockSpec((N_ACT, W_IN_COLS), lambda i: (0, i)),
                 out_shape=jax.ShapeDtypeStruct((N_ACT, D_MODEL), BF16), sem=("parallel",))(bg)


def pack_grad_w_in(dwt):
    def body(g_ref, o_ref):
        for s in range(4):
            for j in range(PACK_W // LANES):
                p = _act_col((19 * s + j) * LANES)
                o_ref[s, j * LANES:(j + 1) * LANES, :] = g_ref[p:p + LANES, :]

    return _call(body, name="pack_grad_w_in", grid=(D_MODEL // W_IN_COLS,),
                 in_specs=[pl.BlockSpec((N_ACT, W_IN_COLS), lambda i: (0, i))],
                 out_specs=pl.BlockSpec((4, PACK_W, W_IN_COLS), lambda i: (0, 0, i)),
                 out_shape=jax.ShapeDtypeStruct((4, PACK_W, D_MODEL), BF16), sem=("parallel",))(dwt)


def _adamw(w, g, m, v):
    m = ADAM_B1 * m + (1.0 - ADAM_B1) * g
    v = ADAM_B2 * v + (1.0 - ADAM_B2) * jnp.square(g)
    m_hat = m / (1.0 - ADAM_B1 ** ADAM_STEP)
    v_hat = v / (1.0 - ADAM_B2 ** ADAM_STEP)
    delta = -ADAM_LR * (m_hat / (jnp.sqrt(v_hat) + ADAM_EPS) + ADAM_WD * w)
    return delta, m, v


def adamw_w_in(g_packed, wt, mt, vt):
    cols = LANES

    def body(g_ref, w_ref, m_ref, v_ref, go_ref, d_ref, mo_ref, vo_ref):
        chip = _chip_index()
        for cv in range(4):
            @pl.when(chip == cv)
            def _():
                for dst, n, src in _shard_pieces(cv):
                    go_ref[dst:dst + n, :] = g_ref[src:src + n, :]
        d_ref[...], mo_ref[...], vo_ref[...] = _adamw(w_ref[...], go_ref[...], m_ref[...], v_ref[...])

    spec = pl.BlockSpec((SHARD_W, cols), lambda i: (0, i))
    shp = jax.ShapeDtypeStruct((SHARD_W, D_MODEL), F32)
    return _call(body, name="adamw_w_in", grid=(D_MODEL // cols,),
                 in_specs=[pl.BlockSpec((PACK_W, cols), lambda i: (0, i)), spec, spec, spec],
                 out_specs=[spec] * 4, out_shape=[shp] * 4, sem=("parallel",))(g_packed, wt, mt, vt)


def adamw_rows(name, g, w, m, v):
    r, c = g.shape
    rows = min(r, BLOCK)

    def body(g_ref, w_ref, m_ref, v_ref, d_ref, mo_ref, vo_ref):
        d_ref[...], mo_ref[...], vo_ref[...] = _adamw(w_ref[...], g_ref[...], m_ref[...], v_ref[...])

    spec = pl.BlockSpec((rows, c), lambda i: (i, 0))
    shp = jax.ShapeDtypeStruct((r, c), F32)
    return _call(body, name=name, grid=(r // rows,), in_specs=[spec] * 4, out_specs=[spec] * 3, out_shape=[shp] * 3,
                 sem=("parallel",))(g, w, m, v)


def adamw_small(gs, ws, ms, vs):
    n = len(gs)

    def body(*refs):
        g, w, m, v = refs[:n], refs[n:2 * n], refs[2 * n:3 * n], refs[3 * n:4 * n]
        outs = refs[4 * n:]
        for i in range(n):
            d, mn, vn = _adamw(w[i][...], g[i][...], m[i][...], v[i][...])
            outs[3 * i][...] = d
            outs[3 * i + 1][...] = mn
            outs[3 * i + 2][...] = vn

    specs = [_full(a.shape) for a in gs]
    res = _call(body, name="adamw_small", in_specs=specs * 4,
                out_specs=[s for s in specs for _ in range(3)],
                out_shape=[jax.ShapeDtypeStruct(a.shape, F32) for a in gs for _ in range(3)])(*gs, *ws, *ms, *vs)
    return [tuple(res[3 * i:3 * i + 3]) for i in range(n)]


def sum_slots(name, r):
    s, rr, c = r.shape
    rows = min(rr, BLOCK)

    def body(r_ref, o_ref):
        acc = r_ref[0].astype(F32)
        for k in range(1, s):
            acc = acc + r_ref[k].astype(F32)
        o_ref[...] = acc

    return _call(body, name=name, grid=(rr // rows,), in_specs=[pl.BlockSpec((s, rows, c), lambda i: (0, i, 0))],
                 out_specs=pl.BlockSpec((rows, c), lambda i: (i, 0)), out_shape=jax.ShapeDtypeStruct((rr, c), F32),
                 sem=("parallel",))(r)


def sum_pair(partial, from_sibling):
    s, _, rr, cols = partial.shape
    rows = rr // 2

    def body(p_ref, r_ref, o_ref):
        c = lax.axis_index("c")
        o_ref[...] = (p_ref[c].astype(F32) + r_ref[1 - c].astype(F32)).astype(BF16)

    return _call(body, name="sum_pair", grid=(s, rr // rows),
                 in_specs=[pl.BlockSpec((None, 2, rows, cols), lambda k, i: (k, 0, i, 0)),
                           pl.BlockSpec((2, None, rows, cols), lambda k, i: (0, k, i, 0))],
                 out_specs=pl.BlockSpec((None, rows, cols), lambda k, i: (k, i, 0)),
                 out_shape=jax.ShapeDtypeStruct((s, rr, cols), BF16), sem=("parallel", "parallel"))(partial, from_sibling)


def _col_tile(n, k):
    if n % 896 == 0 and k <= 1024:
        return 896
    return min(n, 512)


def project(name, x, wt, after):
    m, k = x.shape
    n = wt.shape[0]
    tn = D_MODEL

    def body(x_ref, w_ref, after_ref, o_ref):
        o_ref[...] = lax.dot_general(x_ref[...], w_ref[...], _NT, preferred_element_type=F32)

    return _call(body, name=name, grid=(n // tn,),
                 in_specs=[_full((m, k)), pl.BlockSpec((tn, k), lambda j: (j, 0)), _full(after.shape)],
                 out_specs=pl.BlockSpec((m, tn), lambda j: (0, j)), out_shape=jax.ShapeDtypeStruct((m, n), F32),
                 sem=("parallel",))(x, wt, after)


def _piece_tiles(pieces):
    spans, start = [], 0
    for p in pieces:
        spans.append((start, p.shape[1] // D_MODEL))
        start += p.shape[1] // D_MODEL
    return spans, start


def _piece_spec(tm, span, rows_of, tile_of):
    first, count = span

    def index(i, j):
        t = tile_of(i, j) - first
        mine = (t >= 0) & (t < count)
        return jnp.where(mine, rows_of(i, j), 0), jnp.clip(t, 0, count - 1)

    return pl.BlockSpec((tm, D_MODEL), index)


def project_back(name, pieces, wt, after):
    m = pieces[0].shape[0]
    k = wt.shape[1]
    tm = m // 2
    spans, steps = _piece_tiles(pieces)

    def body(*refs):
        w_ref, o_ref = refs[len(pieces)], refs[len(pieces) + 2]
        j = pl.program_id(1)

        @pl.when(j == 0)
        def _():
            o_ref[...] = jnp.zeros_like(o_ref)

        for dy_ref, (first, count) in zip(refs, spans):
            @pl.when((j >= first) & (j < first + count))
            def _():
                o_ref[...] += jnp.dot(dy_ref[...], w_ref[...], preferred_element_type=F32)

    return _call(body, name=name, grid=(m // tm, steps),
                 in_specs=[_piece_spec(tm, s, lambda i, j: i, lambda i, j: j) for s in spans]
                 + [pl.BlockSpec((D_MODEL, k), lambda i, j: (j, 0)), _full(after.shape)],
                 out_specs=pl.BlockSpec((tm, k), lambda i, j: (i, 0)), out_shape=jax.ShapeDtypeStruct((m, k), F32),
                 sem=("parallel", "arbitrary"))(*pieces, wt, after)


def weight_grad_t(name, pieces, x):
    m = pieces[0].shape[0]
    k = x.shape[1]
    tm = m // 2
    spans, steps = _piece_tiles(pieces)

    def body(*refs):
        x_ref, o_ref, acc_ref = refs[len(pieces):]
        i, half = pl.program_id(0), pl.program_id(1)
        for dy_ref, (first, count) in zip(refs, spans):
            @pl.when((i >= first) & (i < first + count))
            def _():
                part = lax.dot_general(dy_ref[...], x_ref[...], (((0,), (0,)), ((), ())), preferred_element_type=F32)

                @pl.when(half == 0)
                def _():
                    acc_ref[...] = part

                @pl.when(half == 1)
                def _():
                    o_ref[...] = (acc_ref[...] + part).astype(BF16)

    return _call(body, name=name, grid=(steps, 2),
                 in_specs=[_piece_spec(tm, s, lambda i, j: j, lambda i, j: i) for s in spans]
                 + [pl.BlockSpec((tm, k), lambda i, j: (j, 0))],
                 out_specs=pl.BlockSpec((D_MODEL, k), lambda i, j: (i, 0)),
                 out_shape=jax.ShapeDtypeStruct((steps * D_MODEL, k), BF16),
                 scratch=[pltpu.VMEM((D_MODEL, k), F32)], sem=("parallel", "arbitrary"))(*pieces, x)


def mm_tn(name, x, dy):
    m, k = x.shape
    n = dy.shape[1]
    tm = m // 2
    tn = _col_tile(n, k)

    def body(x_ref, dy_ref, o_ref, acc_ref):
        part = lax.dot_general(x_ref[...].astype(BF16), dy_ref[...].astype(BF16), (((0,), (0,)), ((), ())),
                               preferred_element_type=F32)

        @pl.when(pl.program_id(1) == 0)
        def _():
            acc_ref[...] = part

        @pl.when(pl.program_id(1) == 1)
        def _():
            o_ref[...] = (acc_ref[...] + part).astype(BF16)

    return _call(body, name=name, grid=(n // tn, 2),
                 in_specs=[pl.BlockSpec((tm, k), lambda i, j: (j, 0)), pl.BlockSpec((tm, tn), lambda i, j: (j, i))],
                 out_specs=pl.BlockSpec((k, tn), lambda i, j: (0, i)), out_shape=jax.ShapeDtypeStruct((k, n), BF16),
                 scratch=[pltpu.VMEM((k, tn), F32)], sem=("parallel", "arbitrary"))(x, dy)


def _row_spec(width, col_block=0):
    return pl.BlockSpec((BLOCK, width), lambda i: (i, col_block))


def _x_spec():
    return pl.BlockSpec((None, BLOCK, D_MODEL), lambda i: (0, jnp.maximum(i - 1, 0), 0))


def prep(x, meta, g_pre):
    nb = x.shape[1] // BLOCK + 1

    def body(x_ref, meta_ref, g_ref, h_ref, u_ref):
        i = pl.program_id(0)

        @pl.when(i == 0)
        def _():
            h_ref[0:PAD_ROWS, :] = jnp.zeros((PAD_ROWS, D_MODEL), F32)
            h_ref[PAD_ROWS:BLOCK, :] = meta_ref[...]

        @pl.when(i > 0)
        def _():
            h_ref[...] = x_ref[...]

        u_ref[...] = _rms(h_ref[...], g_ref[...]).astype(BF16)

    return _call(body, name="prep", grid=(nb,), in_specs=[_x_spec(), _full((N_META, D_MODEL)), _full((1, D_MODEL))],
                 out_specs=[_row_spec(D_MODEL), _row_spec(D_MODEL)],
                 out_shape=[jax.ShapeDtypeStruct((nb * BLOCK, D_MODEL), F32),
                            jax.ShapeDtypeStruct((nb * BLOCK, D_MODEL), BF16)], sem=("parallel",))(x, meta, g_pre)


def prep_bwd(h, du, dres, g_pre):
    nb = h.shape[0] // BLOCK

    def body(h_ref, du_ref, dres_ref, g_ref, gx_ref, gm_ref, gg_ref):
        i = pl.program_id(0)
        _, vjp = jax.vjp(_rms, h_ref[...], g_ref[...])
        dh, dg = vjp(du_ref[...])

        @pl.when(i == 0)
        def _():
            gm_ref[...] = dh[PAD_ROWS:BLOCK, :]
            gg_ref[...] = dg

        @pl.when(i > 0)
        def _():
            gg_ref[...] += dg

        gx_ref[...] = dh + dres_ref[...]

    return _call(body, name="prep_bwd", grid=(nb,),
                 in_specs=[_row_spec(D_MODEL), _row_spec(D_MODEL), _row_spec(D_MODEL), _full((1, D_MODEL))],
                 out_specs=[_x_spec(), _full((N_META, D_MODEL)), _full((1, D_MODEL))],
                 out_shape=[jax.ShapeDtypeStruct((1, (nb - 1) * BLOCK, D_MODEL), F32),
                            jax.ShapeDtypeStruct((N_META, D_MODEL), F32), jax.ShapeDtypeStruct((1, D_MODEL), F32)],
                 sem=("arbitrary",))(h, du, dres, g_pre)


GROUP_W = SSM_INNER // SSM_GROUPS


def _gated_norm(y, z, g):
    t = y * _silu(z)
    return t * lax.rsqrt(jnp.mean(t * t, axis=-1, keepdims=True) + NORM_EPS) * g


def _gated_norm_groups(y, z, g):
    groups = [slice(k * GROUP_W, (k + 1) * GROUP_W) for k in range(SSM_GROUPS)]
    return jnp.concatenate([_gated_norm(y[:, s], z[:, s], g[:, s]) for s in groups], axis=1)


def _merge(ga, gs, ya, ys):
    return _sigmoid(ga) * ya + _sigmoid(gs) * ys


GATE_ATT_BLOCK = SEG["gate_att"][2] // D_MODEL
GATE_SSM_BLOCK = SEG["gate_ssm"][2] // D_MODEL


def _row_loss(out, g_post, x, target):
    diff = x + _rms(out, g_post) - target
    return 0.5 * jnp.sum(diff * diff) / D_MODEL


def tail(y_ssd, proj, a_att, x, target, woa, wos, wo, g_norm, g_post):
    nb = y_ssd.shape[0] // BLOCK
    rows = nb * BLOCK

    def body(y_ref, z_ref, ga_ref, gs_ref, a_ref, x_ref, t_ref, woa_ref, wos_ref, wo_ref, gn_ref, gp_ref,
             yn_ref, mg_ref, dout_ref, dya_ref, dys_ref, da_ref, dy_ref, dz_ref, dga_ref, dgs_ref, dres_ref,
             loss_ref, dgp_ref, dgn_ref):
        i = pl.program_id(0)
        yn, norm_vjp = jax.vjp(_gated_norm_groups, y_ref[...], z_ref[...], gn_ref[...])
        yn16 = yn.astype(BF16)
        y_ssm = jnp.dot(yn16, wos_ref[...], preferred_element_type=F32)
        y_att = jnp.dot(a_ref[...], woa_ref[...], preferred_element_type=F32)
        merged, merge_vjp = jax.vjp(_merge, ga_ref[...], gs_ref[...], y_att, y_ssm)
        merged16 = merged.astype(BF16)
        out = jnp.dot(merged16, wo_ref[...], preferred_element_type=F32)
        loss, loss_vjp = jax.vjp(_row_loss, out, gp_ref[...], x_ref[...], t_ref[...])
        counted = jnp.where(i > 0, 1.0, 0.0)
        dout, dgp, dres, _ = loss_vjp(counted)
        dout16 = dout.astype(BF16)
        dmerged = lax.dot_general(dout16, wo_ref[...], _NT, preferred_element_type=F32)
        dga, dgs, dya, dys = merge_vjp(dmerged)
        dya16, dys16 = dya.astype(BF16), dys.astype(BF16)
        dyn = lax.dot_general(dys16, wos_ref[...], _NT, preferred_element_type=F32)
        dy, dz, dgn = norm_vjp(dyn)

        yn_ref[...] = yn16
        mg_ref[...] = merged16
        dout_ref[...] = dout16
        dya_ref[...] = dya16
        dys_ref[...] = dys16
        da_ref[...] = lax.dot_general(dya16, woa_ref[...], _NT, preferred_element_type=F32)
        dy_ref[...] = dy
        dz_ref[...] = dz.astype(BF16)
        dga_ref[...] = dga.astype(BF16)
        dgs_ref[...] = dgs.astype(BF16)
        dres_ref[...] = dres

        @pl.when(i == 0)
        def _():
            loss_ref[...] = jnp.zeros_like(loss_ref)
            dgp_ref[...] = jnp.zeros_like(dgp_ref)
            dgn_ref[...] = jnp.zeros_like(dgn_ref)

        loss_ref[...] += loss * counted
        dgp_ref[...] += dgp
        dgn_ref[...] += dgn

    wide, narrow = _row_spec(SSM_INNER), _row_spec(D_MODEL)
    resident = pl.BlockSpec(memory_space=pltpu.VMEM)
    bf = lambda w: jax.ShapeDtypeStruct((rows, w), BF16)
    f32 = lambda w: jax.ShapeDtypeStruct((rows, w), F32)
    return _call(body, name="tail", grid=(nb,),
                 in_specs=[wide, wide, _row_spec(D_MODEL, GATE_ATT_BLOCK), _row_spec(D_MODEL, GATE_SSM_BLOCK), narrow,
                           _x_spec(), _x_spec(), resident, resident, resident, _full((1, SSM_INNER)),
                           _full((1, D_MODEL))],
                 out_specs=[wide, narrow, narrow, narrow, narrow, narrow, wide, wide, narrow, narrow, narrow,
                            _full((8, LANES)), _full((1, D_MODEL)), _full((1, SSM_INNER))],
                 out_shape=[bf(SSM_INNER), bf(D_MODEL), bf(D_MODEL), bf(D_MODEL), bf(D_MODEL), f32(D_MODEL),
                            f32(SSM_INNER), bf(SSM_INNER), bf(D_MODEL), bf(D_MODEL), f32(D_MODEL),
                            jax.ShapeDtypeStruct((8, LANES), F32), jax.ShapeDtypeStruct((1, D_MODEL), F32),
                            jax.ShapeDtypeStruct((1, SSM_INNER), F32)],
                 sem=("arbitrary",))(y_ssd, proj, proj, proj, a_att, x, target, woa, wos, wo, g_norm, g_post)


_NT = (((1,), (1,)), ((), ()))
ALIBI_SLOPES = tuple(2.0 ** (-8.0 * (h + 1) / ATT_Q_HEADS) for h in range(ATT_Q_HEADS))
KV_WIDTH = ATT_KV_HEADS * HEAD_DIM
Q_BLOCK = SEG["q"][2] // D_MODEL
Z_ATT_BLOCK = SEG["z_att"][2] // D_MODEL
K_BLOCK = SEG["k"][2] // KV_WIDTH
V_BLOCK = SEG["v"][2] // KV_WIDTH
META_ROW_BLOCK = PAD_ROWS // N_META


@jax.custom_vjp
def _swap_halves(x):
    return pltpu.roll(x, HEAD_DIM, 1)


_swap_halves.defvjp(lambda x: (pltpu.roll(x, HEAD_DIM, 1), None), lambda _, g: (pltpu.roll(g, HEAD_DIM, 1),))


def _both_halves(t, half):
    first = lax.broadcasted_iota(jnp.int32, t.shape, 1) < HEAD_DIM
    sw = _swap_halves(t)
    return jnp.where(first, t, sw) if half == 0 else jnp.where(first, sw, t)


def _attn_rows(q, z, kp, kc, vp, vc, km, vm, sinks, n):
    rows = ATT_GROUP * BLOCK
    i = lax.broadcasted_iota(jnp.int32, (rows, BLOCK), 0) & (BLOCK - 1)
    j = lax.broadcasted_iota(jnp.int32, (rows, BLOCK), 1)
    in_cur = i >= j
    rel = jnp.where(in_cur, i - j, i - j + BLOCK).astype(F32)
    ok_band = (jnp.zeros((rows, BLOCK), jnp.int32) + n) >= jnp.where(in_cur, 1, 2)
    im = lax.broadcasted_iota(jnp.int32, (rows, N_META), 0) & (BLOCK - 1)
    jm = lax.broadcasted_iota(jnp.int32, (rows, N_META), 1)
    ok_m = ((jnp.zeros((rows, N_META), jnp.int32) + n) >= 1) | (im >= PAD_ROWS + jm)
    first = lax.broadcasted_iota(jnp.int32, (BLOCK, LANES), 1) < HEAD_DIM
    neg = -jnp.inf
    outs = []
    for kv in range(ATT_KV_HEADS):
        tile, half = divmod(kv, 2)
        lanes = slice(tile * LANES, (tile + 1) * LANES)
        kc2, kp2, km2 = (_both_halves(t[:, lanes], half).astype(BF16) for t in (kc, kp, km))
        vc2, vp2, vm2 = (_both_halves(t[:, lanes], half).astype(BF16) for t in (vc, vp, vm))
        qs, slope, sk = [], [], []
        for pair in range(ATT_GROUP // 2):
            c0 = (kv * ATT_GROUP + 2 * pair) * HEAD_DIM
            qp = q[:, c0:c0 + LANES] * HEAD_DIM ** -0.5
            qs += [jnp.where(first, qp, 0.0), jnp.where(first, 0.0, qp)]
        for g in range(ATT_GROUP):
            slope.append(jnp.full((BLOCK, 1), ALIBI_SLOPES[kv * ATT_GROUP + g], F32))
            sk.append(jnp.broadcast_to(sinks[kv * ATT_GROUP + g], (BLOCK, 1)))
        qs = jnp.concatenate(qs, axis=0).astype(BF16)
        slope = jnp.concatenate(slope, axis=0)
        sk = jnp.concatenate(sk, axis=0)
        raw = jnp.where(in_cur, lax.dot_general(qs, kc2, _NT, preferred_element_type=F32),
                        lax.dot_general(qs, kp2, _NT, preferred_element_type=F32))
        sb = jnp.where(ok_band, raw - slope * rel, neg)
        sm = jnp.where(ok_m, lax.dot_general(qs, km2, _NT, preferred_element_type=F32), neg)
        mx = jnp.maximum(jnp.maximum(jnp.max(sb, axis=1, keepdims=True), jnp.max(sm, axis=1, keepdims=True)), sk)
        mx = lax.stop_gradient(mx)
        eb, em, es = jnp.exp(sb - mx), jnp.exp(sm - mx), jnp.exp(sk - mx)
        inv = 1.0 / (es + jnp.sum(eb, axis=1, keepdims=True) + jnp.sum(em, axis=1, keepdims=True))
        pb = eb * inv
        p_cur_prev = jnp.concatenate([jnp.where(in_cur, pb, 0.0), jnp.where(in_cur, 0.0, pb)], axis=1).astype(BF16)
        o = (jnp.dot(p_cur_prev, jnp.concatenate([vc2, vp2], axis=0), preferred_element_type=F32)
             + jnp.dot((em * inv).astype(BF16), vm2, preferred_element_type=F32))
        for pair in range(ATT_GROUP // 2):
            r0 = 2 * pair * BLOCK
            outs.append(jnp.where(first, o[r0:r0 + BLOCK], o[r0 + BLOCK:r0 + 2 * BLOCK]))
    return jnp.concatenate(outs, axis=1) * _silu(z)


def _attn_specs(nb, steps_clamped):
    def blk(t):
        return jnp.minimum(t, nb - 1) if steps_clamped else t

    wide = lambda col: pl.BlockSpec((BLOCK, D_MODEL), lambda t: (blk(t), col))
    cur = lambda col: pl.BlockSpec((BLOCK, KV_WIDTH), lambda t: (blk(t), col))
    prev = lambda col: pl.BlockSpec((BLOCK, KV_WIDTH), lambda t: (jnp.maximum(blk(t) - 1, 0), col))
    meta = lambda col: pl.BlockSpec((N_META, KV_WIDTH), lambda t: (META_ROW_BLOCK, col))
    sinks = pl.BlockSpec((ATT_Q_HEADS, 1, 1), lambda t: (0, 0, 0))
    return [wide(Q_BLOCK), wide(Z_ATT_BLOCK), prev(K_BLOCK), cur(K_BLOCK), prev(V_BLOCK), cur(V_BLOCK),
            meta(K_BLOCK), meta(V_BLOCK), sinks]


def attn_fwd(proj, sinks):
    nb = proj.shape[0] // BLOCK

    def body(q_ref, z_ref, kp_ref, kc_ref, vp_ref, vc_ref, km_ref, vm_ref, sk_ref, o_ref):
        o_ref[...] = _attn_rows(q_ref[...], z_ref[...], kp_ref[...], kc_ref[...], vp_ref[...], vc_ref[...],
                                km_ref[...], vm_ref[...], tuple(sk_ref[h] for h in range(ATT_Q_HEADS)),
                                pl.program_id(0)).astype(BF16)

    return _call(body, name="attn_fwd", grid=(nb,), in_specs=_attn_specs(nb, False), out_specs=_row_spec(D_MODEL),
                 out_shape=jax.ShapeDtypeStruct((nb * BLOCK, D_MODEL), BF16), sem=("parallel",))(*([proj] * 8), sinks)


def attn_bwd(da, proj, sinks):
    nb = proj.shape[0] // BLOCK
    last = nb - 1
    wide = pl.BlockSpec((BLOCK, D_MODEL), lambda t: (jnp.minimum(t, last), 0))
    done = pl.BlockSpec((BLOCK, KV_WIDTH), lambda t: (jnp.maximum(t - 1, 0), 0))
    meta = _full((N_META, KV_WIDTH))
    par = _full((ATT_Q_HEADS, 1, 1))

    def body(da_ref, q_ref, z_ref, kp_ref, kc_ref, vp_ref, vc_ref, km_ref, vm_ref, sk_ref,
             dq_ref, dz_ref, dk_ref, dv_ref, dkm_ref, dvm_ref, dsk_ref, ck_ref, cv_ref):
        t = pl.program_id(0)

        @pl.when(t == 0)
        def _():
            ck_ref[...] = jnp.zeros_like(ck_ref)
            cv_ref[...] = jnp.zeros_like(cv_ref)
            dkm_ref[...] = jnp.zeros_like(dkm_ref)
            dvm_ref[...] = jnp.zeros_like(dvm_ref)
            dsk_ref[...] = jnp.zeros_like(dsk_ref)

        @pl.when(t < nb)
        def _():
            def f(q, z, kp, kc, vp, vc, km, vm, sk):
                return _attn_rows(q, z, kp, kc, vp, vc, km, vm, sk, t)

            _, vjp = jax.vjp(f, q_ref[...], z_ref[...], kp_ref[...], kc_ref[...], vp_ref[...], vc_ref[...],
                             km_ref[...], vm_ref[...], tuple(sk_ref[h] for h in range(ATT_Q_HEADS)))
            dq, dz, dkp, dkc, dvp, dvc, dkm, dvm, dsk = vjp(da_ref[...])
            dq_ref[...] = dq.astype(BF16)
            dz_ref[...] = dz.astype(BF16)
            for h in range(ATT_Q_HEADS):
                dsk_ref[h] += dsk[h]
            dk_ref[...] = ck_ref[...] + dkp
            dv_ref[...] = cv_ref[...] + dvp
            ck_ref[...] = dkc
            cv_ref[...] = dvc
            dkm_ref[...] += dkm
            dvm_ref[...] += dvm

        @pl.when(t == nb)
        def _():
            dk_ref[...] = ck_ref[...]
            dv_ref[...] = cv_ref[...]

    rows = nb * BLOCK
    return _call(body, name="attn_bwd", grid=(nb + 1,), in_specs=[wide] + _attn_specs(nb, True),
                 out_specs=[wide, wide, done, done, meta, meta, par],
                 out_shape=[jax.ShapeDtypeStruct((rows, D_MODEL), BF16), jax.ShapeDtypeStruct((rows, D_MODEL), BF16),
                            jax.ShapeDtypeStruct((rows, KV_WIDTH), F32), jax.ShapeDtypeStruct((rows, KV_WIDTH), F32),
                            jax.ShapeDtypeStruct((N_META, KV_WIDTH), F32), jax.ShapeDtypeStruct((N_META, KV_WIDTH), F32),
                            jax.ShapeDtypeStruct(sinks.shape, F32)],
                 scratch=[pltpu.VMEM((BLOCK, KV_WIDTH), F32), pltpu.VMEM((BLOCK, KV_WIDTH), F32)],
                 sem=("arbitrary",))(da, *([proj] * 8), sinks)


XBC_BLOCK0 = SEG["xbc"][2] // D_MODEL
CONV_COL_BLOCKS = CONV_DIM // D_MODEL
DT_TILE = SEG["dt"][2] // LANES


HALO = 8


def _conv_rows(length):
    return 544 if length % 544 == 0 else BLOCK


def _shift_rows(cur, before, j):
    if j == 0:
        return cur
    n = cur.shape[0]
    row = lax.broadcasted_iota(jnp.int32, cur.shape, 0)
    head = pltpu.roll(before, j, 0)
    if n > HALO:
        head = jnp.concatenate([head, jnp.zeros((n - HALO, cur.shape[1]), cur.dtype)], axis=0)
    return jnp.where(row >= j, pltpu.roll(cur, j, 0), head)


def _conv_pre(cur, before, w_ref, b_ref):
    pre = b_ref[...] + w_ref[CONV_WIDTH - 1:CONV_WIDTH, :] * cur
    for k in range(CONV_WIDTH - 1):
        pre = pre + w_ref[k:k + 1, :] * _shift_rows(cur, before, CONV_WIDTH - 1 - k)
    return pre


def _conv_specs(steps, rows, col0=0):
    first = XBC_BLOCK0 + col0
    halos = rows // HALO
    cur = pl.BlockSpec((rows, D_MODEL), lambda j, i: (i, first + j))
    before = pl.BlockSpec((HALO, D_MODEL), lambda j, i: (jnp.maximum(i * halos - 1, 0), first + j))
    after = pl.BlockSpec((HALO, D_MODEL), lambda j, i: (jnp.minimum(i + 1, steps - 1) * halos, first + j))
    return cur, before, after


def _valid_rows(i, rows):
    row = lax.broadcasted_iota(jnp.int32, (rows, D_MODEL), 0)
    return jnp.maximum((row >= PAD_ROWS).astype(F32), jnp.where(i > 0, 1.0, 0.0))


def conv_fwd(proj, conv_w, conv_b):
    rows = _conv_rows(proj.shape[0])
    steps = proj.shape[0] // rows
    cur, before, _ = _conv_specs(steps, rows)

    def body(c_ref, p_ref, w_ref, b_ref, o_ref):
        i = pl.program_id(1)
        pre = _conv_pre(c_ref[...], p_ref[...] * jnp.where(i > 0, 1.0, 0.0), w_ref, b_ref)
        o_ref[...] = _silu(pre) * _valid_rows(i, rows)

    return _call(body, name="conv_fwd", grid=(CONV_COL_BLOCKS, steps),
                 in_specs=[cur, before, pl.BlockSpec((CONV_WIDTH, D_MODEL), lambda j, i: (0, j)),
                           pl.BlockSpec((1, D_MODEL), lambda j, i: (0, j))],
                 out_specs=pl.BlockSpec((rows, D_MODEL), lambda j, i: (i, j)),
                 out_shape=jax.ShapeDtypeStruct((proj.shape[0], CONV_DIM), F32),
                 sem=("parallel", "parallel"))(proj, proj, conv_w, conv_b)


def conv_bwd(name, dparts, col0, proj, conv_w, conv_b):
    rows = _conv_rows(proj.shape[0])
    steps = proj.shape[0] // rows
    last = steps - 1
    ncol = sum(d.shape[1] for d in dparts) // D_MODEL
    np_ = len(dparts)
    cur, before, after = _conv_specs(steps, rows, col0)
    dcur = [pl.BlockSpec((rows, d.shape[1] // ncol), lambda j, i: (i, j)) for d in dparts]
    dafter = [pl.BlockSpec((HALO, d.shape[1] // ncol), lambda j, i: (jnp.minimum(i + 1, last) * (rows // HALO), j))
              for d in dparts]
    out_cur = pl.BlockSpec((rows, D_MODEL), lambda j, i: (i, j))
    wspec = pl.BlockSpec((CONV_WIDTH, D_MODEL), lambda j, i: (0, col0 + j))
    bspec = pl.BlockSpec((1, D_MODEL), lambda j, i: (0, col0 + j))
    wout = pl.BlockSpec((CONV_WIDTH, D_MODEL), lambda j, i: (0, j))
    bout = pl.BlockSpec((1, D_MODEL), lambda j, i: (0, j))

    def body(*refs):
        dc_refs, da_refs = refs[:np_], refs[np_:2 * np_]
        c_ref, p_ref, a_ref, w_ref, b_ref, du_ref, dw_ref, db_ref = refs[2 * np_:]
        i = pl.program_id(1)
        row = lax.broadcasted_iota(jnp.int32, (rows, D_MODEL), 0)
        curv = c_ref[...]
        beforev = p_ref[...] * jnp.where(i > 0, 1.0, 0.0)
        side_by_side = lambda rs: rs[0][...] if np_ == 1 else jnp.concatenate([r[...] for r in rs], axis=1)

        def dpre_of(pre, d):
            s = _sigmoid(pre)
            return d * (s * (1.0 + pre * (1.0 - s)))

        dp_c = dpre_of(_conv_pre(curv, beforev, w_ref, b_ref), side_by_side(dc_refs) * _valid_rows(i, rows))
        dp_a = dpre_of(_conv_pre(a_ref[...], curv[rows - HALO:], w_ref, b_ref),
                       side_by_side(da_refs) * jnp.where(i < last, 1.0, 0.0))
        du = w_ref[CONV_WIDTH - 1:CONV_WIDTH, :] * dp_c
        for j in range(1, CONV_WIDTH):
            tail = jnp.concatenate([jnp.zeros((rows - HALO, D_MODEL), F32), pltpu.roll(dp_a, HALO - j, 0)], axis=0)
            up = jnp.where(row < rows - j, pltpu.roll(dp_c, rows - j, 0), tail)
            du = du + w_ref[CONV_WIDTH - 1 - j:CONV_WIDTH - j, :] * up
        du_ref[...] = du.astype(BF16)

        @pl.when(i == 0)
        def _():
            dw_ref[...] = jnp.zeros_like(dw_ref)
            db_ref[...] = jnp.zeros_like(db_ref)

        for k in range(CONV_WIDTH):
            dw_ref[k:k + 1, :] += jnp.sum(dp_c * _shift_rows(curv, beforev, CONV_WIDTH - 1 - k), axis=0, keepdims=True)
        db_ref[...] += jnp.sum(dp_c, axis=0, keepdims=True)

    width = ncol * D_MODEL
    return _call(body, name=name, grid=(ncol, steps),
                 in_specs=dcur + dafter + [cur, before, after, wspec, bspec], out_specs=[out_cur, wout, bout],
                 out_shape=[jax.ShapeDtypeStruct((proj.shape[0], width), BF16),
                            jax.ShapeDtypeStruct((CONV_WIDTH, width), F32), jax.ShapeDtypeStruct((1, width), F32)],
                 sem=("parallel", "arbitrary"))(*dparts, *dparts, proj, proj, proj, conv_w, conv_b)


def _head_expand():
    e = np.zeros((LANES, SSM_INNER), np.float32)
    for h in range(SSM_HEADS):
        e[h, h * HEAD_DIM:(h + 1) * HEAD_DIM] = 1.0
    return jnp.asarray(e, dtype=BF16)


def _softplus(x):
    return jnp.maximum(x, 0.0) + jnp.log(1.0 + jnp.exp(-jnp.abs(x)))


def _bf16_parts(x):
    hi = x.astype(BF16)
    rest = x - hi.astype(F32)
    mid = rest.astype(BF16)
    return hi, mid, (rest - mid.astype(F32)).astype(BF16)


@jax.custom_vjp
def _times_01(x, m):
    return sum(jnp.dot(p, m, preferred_element_type=F32) for p in _bf16_parts(x))


def _times_01_bwd(m, g):
    return sum(lax.dot_general(p, m, _NT, preferred_element_type=F32) for p in _bf16_parts(g)), jnp.zeros_like(m)


_times_01.defvjp(lambda x, m: (_times_01(x, m), m), _times_01_bwd)


def _causal_ones():
    l = lax.broadcasted_iota(jnp.int32, (BLOCK, BLOCK), 0)
    s = lax.broadcasted_iota(jnp.int32, (BLOCK, BLOCK), 1)
    return (l >= s).astype(BF16)


@jax.custom_vjp
def _cumsum_rows(a):
    return sum(jnp.dot(_causal_ones(), p, preferred_element_type=F32) for p in _bf16_parts(a))


def _cumsum_rows_bwd(_, g):
    tn = (((0,), (0,)), ((), ()))
    return (sum(lax.dot_general(_causal_ones(), p, tn, preferred_element_type=F32) for p in _bf16_parts(g)),)


_cumsum_rows.defvjp(lambda a: (_cumsum_rows(a), None), _cumsum_rows_bwd)


def _ssd_heads(dt_tile, bias, alog, dsk, expand):
    dt = _softplus(dt_tile + bias)
    a = dt * (-jnp.exp(alog))
    one_row = lambda v: jnp.broadcast_to(v, (HALO, LANES))
    return _times_01(jnp.concatenate([dt, _cumsum_rows(a), one_row(jnp.sum(a, axis=0, keepdims=True)), one_row(dsk)],
                                     axis=0), expand)


HEADS_ROWS = 2 * BLOCK + 2 * HALO


def _ssd_group(xs, per_lane, bg, cg, state):
    l = lax.broadcasted_iota(jnp.int32, (BLOCK, BLOCK), 0)
    s = lax.broadcasted_iota(jnp.int32, (BLOCK, BLOCK), 1)
    causal = l >= s
    first_head = s < HEAD_DIM
    dtx, cs = per_lane[0:BLOCK], per_lane[BLOCK:2 * BLOCK]
    tot, dsk = per_lane[2 * BLOCK:2 * BLOCK + 1], per_lane[2 * BLOCK + HALO:2 * BLOCK + HALO + 1]
    bb, cb16 = bg.astype(BF16), cg.astype(BF16)
    cb = lax.dot_general(cb16, bb, _NT, preferred_element_type=F32)
    xr = xs * dtx
    y_diag = []
    for p in range(GROUP_W // LANES):
        lanes = slice(p * LANES, (p + 1) * LANES)
        c_pair = cs[:, lanes]
        c_swap = _swap_halves(c_pair)
        m = []
        for c_head in (jnp.where(first_head, c_pair, c_swap), jnp.where(first_head, c_swap, c_pair)):
            m.append(cb * jnp.exp(jnp.where(causal, c_head - c_head.T, -jnp.inf)))
        x_pair = xr[:, lanes]
        x_diag = jnp.concatenate([jnp.where(first_head, x_pair, 0.0), jnp.where(first_head, 0.0, x_pair)], axis=0)
        y_diag.append(jnp.dot(jnp.concatenate(m, axis=1).astype(BF16), x_diag.astype(BF16),
                              preferred_element_type=F32))
    st = lax.dot_general(bb, (xr * jnp.exp(tot - cs)).astype(BF16), (((0,), (0,)), ((), ())),
                         preferred_element_type=F32)
    new_state = state * jnp.exp(tot) + st
    y_off = jnp.dot(cb16, state.astype(BF16), preferred_element_type=F32) * jnp.exp(cs)
    return jnp.concatenate(y_diag, axis=1) + y_off + dsk * xs, new_state


BC_WIDTH = SSM_GROUPS * SSM_STATE


def _ssd_specs(chunk):
    xs = pl.BlockSpec((BLOCK, SSM_INNER), lambda c: (chunk(c), 0))
    dt = pl.BlockSpec((BLOCK, LANES), lambda c: (chunk(c), DT_TILE))
    expand = _full((LANES, SSM_INNER))
    b = pl.BlockSpec((BLOCK, BC_WIDTH), lambda c: (chunk(c), SSM_INNER // BC_WIDTH))
    cc = pl.BlockSpec((BLOCK, BC_WIDTH), lambda c: (chunk(c), SSM_INNER // BC_WIDTH + 1))
    par = _full((1, LANES))
    state = pl.BlockSpec((None, SSM_STATE, SSM_INNER), lambda c: (chunk(c), 0, 0))
    return xs, dt, expand, b, cc, par, state


def _group_lanes(g):
    return slice(g * GROUP_W, (g + 1) * GROUP_W), slice(g * SSM_STATE, (g + 1) * SSM_STATE)


def ssd_fwd(xbc, proj, expand, bias, alog, dsk):
    nb = xbc.shape[0] // BLOCK
    xs, dt, ex, b, cc, par, state = _ssd_specs(lambda c: c)

    def body(x_ref, dt_ref, e_ref, bi_ref, al_ref, dk_ref, b_ref, c_ref, y_ref, sp_ref, st_ref):
        @pl.when(pl.program_id(0) == 0)
        def _():
            st_ref[...] = jnp.zeros_like(st_ref)

        per_lane = _ssd_heads(dt_ref[...], bi_ref[...], al_ref[...], dk_ref[...], e_ref[...])
        for g in range(SSM_GROUPS):
            wide, tile = _group_lanes(g)
            entering = st_ref[:, wide]
            sp_ref[:, wide] = entering
            y_ref[:, wide], st_ref[:, wide] = _ssd_group(x_ref[:, wide], per_lane[:, wide], b_ref[:, tile],
                                                         c_ref[:, tile], entering)

    return _call(body, name="ssd_fwd", grid=(nb,), in_specs=[xs, dt, ex, par, par, par, b, cc],
                 out_specs=[xs, state],
                 out_shape=[jax.ShapeDtypeStruct((nb * BLOCK, SSM_INNER), F32),
                            jax.ShapeDtypeStruct((nb, SSM_STATE, SSM_INNER), F32)],
                 scratch=[pltpu.VMEM((SSM_STATE, SSM_INNER), F32)],
                 sem=("arbitrary",))(xbc, proj, expand, bias, alog, dsk, xbc, xbc)


def ssd_bwd(dy, xbc, proj, expand, bias, alog, dsk, states, comm):
    nb = xbc.shape[0] // BLOCK
    last = nb - 1
    xs, dt, ex, b, cc, par, state = _ssd_specs(lambda c: last - c)
    tile = pl.BlockSpec((BLOCK, LANES), lambda c: (last - c, 0))
    nspec = pl.BlockSpec((BLOCK, BC_WIDTH), lambda c: (last - c, 0))

    def body(dy_ref, x_ref, dt_ref, e_ref, bi_ref, al_ref, dk_ref, b_ref, c_ref, sp_ref,
             dx_ref, ddt_ref, db_ref, dc_ref, dbi_ref, dal_ref, ddk_ref, ds_ref):
        @pl.when(pl.program_id(0) == 0)
        def _():
            ds_ref[...] = jnp.zeros_like(ds_ref)
            dbi_ref[...] = jnp.zeros_like(dbi_ref)
            dal_ref[...] = jnp.zeros_like(dal_ref)
            ddk_ref[...] = jnp.zeros_like(ddk_ref)

        expand = e_ref[...]
        per_lane, heads_vjp = jax.vjp(lambda t, bi, al, dk: _ssd_heads(t, bi, al, dk, expand), dt_ref[...], bi_ref[...],
                                      al_ref[...], dk_ref[...])
        d_per_lane = []
        for g in range(SSM_GROUPS):
            wide, tile_lanes = _group_lanes(g)
            _, vjp = jax.vjp(_ssd_group, x_ref[:, wide], per_lane[:, wide], b_ref[:, tile_lanes], c_ref[:, tile_lanes],
                             sp_ref[:, wide])
            (dx_ref[:, wide], d_lanes, db_ref[:, tile_lanes], dc_ref[:, tile_lanes],
             ds_ref[:, wide]) = vjp((dy_ref[:, wide], ds_ref[:, wide]))
            d_per_lane.append(d_lanes)
        ddt, dbi, dal, ddk = heads_vjp(jnp.concatenate(d_per_lane, axis=1))
        ddt_ref[...] = ddt.astype(BF16)
        dbi_ref[...] += dbi
        dal_ref[...] += dal
        ddk_ref[...] += ddk

    def at():
        c = pl.program_id(0)
        return c == 0, c == 0, c == last

    par_shape = jax.ShapeDtypeStruct((1, LANES), F32)
    return _call_with_comm(
        body, comm, at, (dy, xbc, proj, expand, bias, alog, dsk, xbc, xbc, states), name="ssd_bwd",
        grid=(nb,), in_specs=[xs, xs, dt, ex, par, par, par, b, cc, state],
        out_specs=[xs, tile, nspec, nspec, par, par, par],
        out_shape=[jax.ShapeDtypeStruct((nb * BLOCK, SSM_INNER), F32), jax.ShapeDtypeStruct((nb * BLOCK, LANES), BF16),
                   jax.ShapeDtypeStruct((nb * BLOCK, BC_WIDTH), F32), jax.ShapeDtypeStruct((nb * BLOCK, BC_WIDTH), F32),
                   par_shape, par_shape, par_shape],
        scratch=[pltpu.VMEM((SSM_STATE, SSM_INNER), F32)])


SLAB_ROWS = 16
SLAB_META_ROW = 8
SLAB_META_SHAPE = (8, 2 * D_MODEL)
SLAB_LOSS_ROW = 7


def pack_small(dcw, dcb, dgpre, dgpost, dbias, dalog, ddsk, dsinks, dgn, dmeta, loss_tile):
    def body(cw, cb, gpre, gpost, dtb, al, dk, sk, gn, meta, loss, o_ref):
        o_ref[...] = jnp.zeros_like(o_ref)
        o_ref[SLAB_LOSS_ROW:SLAB_LOSS_ROW + 1, 0:LANES] = loss[0:1, :]
        o_ref[0:CONV_WIDTH, :] = cw[...]
        o_ref[4:5, :] = cb[...]
        o_ref[5:6, 0:1024] = gpre[...]
        o_ref[5:6, 1024:2048] = gpost[...]
        o_ref[5:6, 2048:2176] = dtb[...]
        o_ref[5:6, 2176:2304] = al[...]
        o_ref[5:6, 2304:2432] = dk[...]
        o_ref[5:6, 2432:2560] = sk[...]
        o_ref[6:7, 0:SSM_INNER] = gn[...]
        o_ref[SLAB_META_ROW:SLAB_ROWS, 0:SLAB_META_SHAPE[1]] = meta[...]

    args = (dcw, dcb, dgpre, dgpost, dbias, dalog, ddsk, dsinks, dgn, dmeta, loss_tile)
    return _call(body, name="pack_small", in_specs=[_full(a.shape) for a in args],
                 out_specs=_full((SLAB_ROWS, CONV_DIM)), out_shape=jax.ShapeDtypeStruct((SLAB_ROWS, CONV_DIM), F32))(*args)


def _lane_tile(v):
    return jnp.pad(v, ((0, 0), (0, LANES - v.shape[1])))


def kernel(x, meta_tokens, g_pre, w_in, conv_w, conv_b, dt_bias, a_log, d_skip, attn_sinks, g_ssm_norm, w_out_att, w_out_ssm, w_out, g_post, loss_target, m_meta_tokens, m_g_pre, m_w_in, m_conv_w, m_conv_b, m_dt_bias, m_a_log, m_d_skip, m_attn_sinks, m_g_ssm_norm, m_w_out_att, m_w_out_ssm, m_w_out, m_g_post, v_meta_tokens, v_g_pre, v_w_in, v_conv_w, v_conv_b, v_dt_bias, v_a_log, v_d_skip, v_attn_sinks, v_g_ssm_norm, v_w_out_att, v_w_out_ssm, v_w_out, v_g_post):
    chip = _chip_index()

    conv_w_rows = jnp.pad(conv_w[0], ((0, 2 * 8 - CONV_WIDTH), (0, 0)))
    w_in_t, m_w_in_t, v_w_in_t = w_in[0].T, m_w_in[0].T, v_w_in[0].T
    gathered_w_in, g_conv_w, g_meta = run_comm("gather_w_in",
                                               TwoLevelGather([pack_w_in(w_in_t), conv_w_rows, meta_tokens]))
    w_all_t = unpack_w_in(gathered_w_in)
    cw_full = g_conv_w[:, :CONV_WIDTH].transpose(1, 0, 2).reshape(CONV_WIDTH, CONV_DIM)
    meta_full = g_meta.transpose(1, 0, 2).reshape(N_META, D_MODEL)
    behind_w_in = 0.0 * g_meta[0, 0, 0]
    w_out_flight, w_out_started = chip_exchange_start(
        "gather_w_out_start", [(w[0] + behind_w_in).astype(BF16) for w in (w_out_att, w_out_ssm, w_out)], False)

    h, u = prep(x, meta_full, g_pre)
    proj = project("in_proj", u, w_all_t, w_out_started)

    sinks3 = attn_sinks.reshape(ATT_Q_HEADS, 1, 1)
    a_att = attn_fwd(proj, sinks3)

    xbc = conv_fwd(proj, cw_full, conv_b)
    expand = _head_expand()
    head_pars = (_lane_tile(dt_bias), _lane_tile(a_log), _lane_tile(d_skip))
    y_ssd, states = ssd_fwd(xbc, proj, expand, *head_pars)
    woa, wos, wo = [g.reshape(-1, D_MODEL) for g in chip_exchange_wait("gather_w_out_wait", w_out_flight, False, y_ssd)]

    (yn, merged, dout, dy_att, dy_ssm, da_att, dy_ssd, dz_ssm, dga, dgs, dres, loss_tile, dg_post, dgn) = tail(
        y_ssd, proj, a_att, x, loss_target, woa, wos, wo, g_ssm_norm, g_post)

    dwo = mm_tn("out_proj_dw", merged, dout)
    dwoa = mm_tn("att_out_dw", a_att, dy_att)
    dwos = mm_tn("ssm_out_dw", yn, dy_ssm)
    dq, dz_att, dk, dv, dkmeta, dvmeta, dsinks3 = attn_bwd(da_att, proj, sinks3)
    dk = dk.at[PAD_ROWS:BLOCK].add(dkmeta).astype(BF16)
    dv = dv.at[PAD_ROWS:BLOCK].add(dvmeta).astype(BF16)

    def pieces(g):
        return g.reshape(4, 2, g.shape[0] // 8, g.shape[1])

    def to_owner(g):
        return (lambda ref, dev: ref.at[_chip_of(dev), dev[2]], (g.shape[0] // 8, g.shape[1]))

    (dxs, ddt_tile, dbg, dcg, dbias, dalog, ddsk), sent_w_out = ssd_bwd(
        dy_ssd, xbc, proj, expand, *head_pars, states,
        DirectExchange([pieces(dwoa), pieces(dwos), pieces(dwo)], [to_owner(dwoa), to_owner(dwos), to_owner(dwo)],
                       ALL_MASKS, "dev", 8))
    dxs_raw, dcw_xs, dcb_xs = conv_bwd("conv_bwd_x", [dxs], 0, proj, cw_full, conv_b)
    dbc_raw, dcw_bc, dcb_bc = conv_bwd("conv_bwd_bc", [dbg, dcg], SSM_INNER // D_MODEL, proj, cw_full, conv_b)
    dcw = jnp.concatenate([dcw_xs, dcw_bc], axis=1)
    dcb = jnp.concatenate([dcb_xs, dcb_bc], axis=1)

    narrow = jnp.concatenate([dk, dv, ddt_tile, jnp.zeros((dk.shape[0], N_ACT - N_ALIGNED), BF16)], axis=1)
    dproj = [dz_ssm, dxs_raw, dbc_raw, dq, dz_att, dga, dgs, narrow]
    dw_all_t = weight_grad_t("in_proj_dw", dproj, u)

    half_rows = PACK_W // 2
    partial = pack_grad_w_in(dw_all_t).reshape(4, 2, half_rows, D_MODEL)
    from_sibling, = run_comm("pair_grads", DirectExchange(
        [partial], [(lambda ref, dev: ref.at[pl.ds(0, 4), dev[2]], (4, half_rows, D_MODEL))], SIBLING_MASK, "core", 2,
        keep_own=False))
    chip_sum = sum_pair(partial, from_sibling)
    grads_flight, started = chip_exchange_start("reduce_w_in_start", [chip_sum], True)
    du = project_back("in_proj_dx", dproj, w_all_t, started)
    grad_x, dmeta, dg_pre = prep_bwd(h, du, dres, g_pre)

    halves = [sum_slots("sum_" + nm, r) for nm, r in zip(("w_out_att", "w_out_ssm", "w_out"), sent_w_out)]
    shared = run_comm("share_w_out", DirectExchange(halves, [None] * 3, SIBLING_MASK, "core", 2))
    g_woa, g_wos, g_wo = [f.reshape(2 * f.shape[1], f.shape[2]) for f in shared]
    d_woa, nm_woa, nv_woa = adamw_rows("adamw_w_out_att", g_woa, w_out_att[0], m_w_out_att[0], v_w_out_att[0])
    d_wos, nm_wos, nv_wos = adamw_rows("adamw_w_out_ssm", g_wos, w_out_ssm[0], m_w_out_ssm[0], v_w_out_ssm[0])
    d_wo, nm_wo, nv_wo = adamw_rows("adamw_w_out", g_wo, w_out[0], m_w_out[0], v_w_out[0])

    sent_w_in, = chip_exchange_wait("reduce_w_in_wait", grads_flight, True, d_wos)
    slab = pack_small(dcw, dcb, dg_pre, dg_post, dbias, dalog, ddsk, _lane_tile(dsinks3.reshape(1, ATT_Q_HEADS)), dgn,
                      dmeta.reshape(SLAB_META_SHAPE), loss_tile)
    shared_w_in, slabs = run_comm("share_w_in", Both(
        DirectExchange([sum_slots("sum_w_in", sent_w_in)], [None], SIBLING_MASK, "core", 2),
        DirectExchange([slab], [None], ALL_MASKS, "dev", 8)))
    small = sum_slots("sum_small", slabs)
    loss = small[SLAB_LOSS_ROW, 0]
    g_w_in, d_w_in, nm_w_in, nv_w_in = [
        a.T for a in adamw_w_in(shared_w_in.reshape(PACK_W, D_MODEL), w_in_t, m_w_in_t, v_w_in_t)]

    cw_cols = CONV_DIM // 4
    meta_cols = D_MODEL // 4
    g_small = {
        "meta_tokens": lax.dynamic_slice(
            small[SLAB_META_ROW:SLAB_ROWS, 0:SLAB_META_SHAPE[1]].reshape(N_META, D_MODEL), (0, chip * meta_cols),
            (N_META, meta_cols)),
        "g_pre": small[5:6, 0:1024],
        "conv_w": lax.dynamic_slice(small, (0, chip * cw_cols), (CONV_WIDTH, cw_cols)),
        "conv_b": small[4:5, :],
        "dt_bias": small[5:6, 2048:2048 + SSM_HEADS],
        "a_log": small[5:6, 2176:2176 + SSM_HEADS],
        "d_skip": small[5:6, 2304:2304 + SSM_HEADS],
        "attn_sinks": small[5:6, 2432:2432 + ATT_Q_HEADS],
        "g_ssm_norm": small[6:7, 0:SSM_INNER],
        "g_post": small[5:6, 1024:2048],
    }
    names = list(g_small)
    w_small = dict(meta_tokens=meta_tokens, g_pre=g_pre, conv_w=conv_w[0], conv_b=conv_b, dt_bias=dt_bias, a_log=a_log,
                   d_skip=d_skip, attn_sinks=attn_sinks, g_ssm_norm=g_ssm_norm, g_post=g_post)
    m_small = dict(meta_tokens=m_meta_tokens, g_pre=m_g_pre, conv_w=m_conv_w[0], conv_b=m_conv_b, dt_bias=m_dt_bias,
                   a_log=m_a_log, d_skip=m_d_skip, attn_sinks=m_attn_sinks, g_ssm_norm=m_g_ssm_norm, g_post=m_g_post)
    v_small = dict(meta_tokens=v_meta_tokens, g_pre=v_g_pre, conv_w=v_conv_w[0], conv_b=v_conv_b, dt_bias=v_dt_bias,
                   a_log=v_a_log, d_skip=v_d_skip, attn_sinks=v_attn_sinks, g_ssm_norm=v_g_ssm_norm, g_post=v_g_post)
    upd = dict(zip(names, adamw_small([g_small[k] for k in names], [w_small[k] for k in names],
                                      [m_small[k] for k in names], [v_small[k] for k in names])))

    lead = {"conv_w"}

    def shaped(name, a):
        return a[None] if name in lead else a

    grads = dict(g_small, w_in=g_w_in, w_out_att=g_woa, w_out_ssm=g_wos, w_out=g_wo)
    deltas = dict({k: upd[k][0] for k in names}, w_in=d_w_in, w_out_att=d_woa, w_out_ssm=d_wos, w_out=d_wo)
    new_m = dict({k: upd[k][1] for k in names}, w_in=nm_w_in, w_out_att=nm_woa, w_out_ssm=nm_wos, w_out=nm_wo)
    new_v = dict({k: upd[k][2] for k in names}, w_in=nv_w_in, w_out_att=nv_woa, w_out_ssm=nv_wos, w_out=nv_wo)
    lead |= {"w_in", "w_out_att", "w_out_ssm", "w_out"}
    order = ["meta_tokens", "g_pre", "w_in", "conv_w", "conv_b", "dt_bias", "a_log", "d_skip", "attn_sinks",
             "g_ssm_norm", "w_out_att", "w_out_ssm", "w_out", "g_post"]
    outs = [loss, grad_x]
    for group in (grads, deltas, new_m, new_v):
        outs += [shaped(k, group[k]) for k in order]
    return tuple(outs)
```

```python
import numpy as np
import jax
import jax.numpy as jnp
from jax import lax
from jax.experimental import pallas as pl
from jax.experimental.pallas import tpu as pltpu

F32 = jnp.float32
BF16 = jnp.bfloat16

D_MODEL = 1024
N_META = 16
BLOCK = 128
PAD_ROWS = BLOCK - N_META
NORM_EPS = 1e-6
HEAD_DIM = 64
ATT_Q_HEADS = 16
ATT_KV_HEADS = 4
ATT_GROUP = 4
SSM_INNER = 2048
SSM_HEADS = 32
SSM_GROUPS = 4
SSM_STATE = 128
CONV_WIDTH = 4
CONV_DIM = 3072
LANES = 128

ADAM_LR = 0.001
ADAM_B1 = 0.9
ADAM_B2 = 0.999
ADAM_EPS = 1e-08
ADAM_WD = 0.01
ADAM_STEP = 10

VMEM_LIMIT = 48 * 1024 * 1024

SHARD_W = 2440
PACK_W = 2560
SHARD_STRIDE = 2432
N_ALIGNED = 9856
N_ACT = 10240
SEG = {
    "q": (0, 1024, 5120), "k": (1024, 256, 9216), "v": (1280, 256, 9472), "z_att": (1536, 1024, 6144),
    "z_ssm": (2560, 2048, 0), "xbc": (4608, 3072, 2048), "dt": (7680, 128, 9728),
    "gate_att": (7808, 1024, 7168), "gate_ssm": (8832, 1024, 8192),
}
DT_STORED_START = 7680
DT_PAD = LANES - SSM_HEADS


def _act_col(aligned_col):
    for a0, w, p0 in SEG.values():
        if a0 <= aligned_col < a0 + w:
            return p0 + aligned_col - a0
    raise ValueError(aligned_col)


def _call(body, *, name, out_shape, in_specs, out_specs, grid=(), scratch=(), sem=None, aliases=None):
    return pl.pallas_call(
        body, out_shape=out_shape, grid=grid, in_specs=in_specs, out_specs=out_specs, scratch_shapes=list(scratch),
        name=name, input_output_aliases=aliases or {},
        compiler_params=pltpu.CompilerParams(dimension_semantics=sem, vmem_limit_bytes=VMEM_LIMIT))


def _full(shape):
    n = len(shape)
    return pl.BlockSpec(shape, lambda *_: (0,) * n)


def _chip_index():
    return lax.axis_index("x") * 2 + lax.axis_index("y")


@jax.custom_vjp
def _sigmoid(z):
    return pl.reciprocal(1.0 + jnp.exp(-z), approx=True)


def _sigmoid_fwd(z):
    s = _sigmoid(z)
    return s, s


_sigmoid.defvjp(_sigmoid_fwd, lambda s, g: (g * s * (1.0 - s),))


def _silu(z):
    return z * _sigmoid(z)


def _rms(x, g):
    return x * lax.rsqrt(jnp.mean(x * x, axis=-1, keepdims=True) + NORM_EPS) * g


def _peer(mask):
    x, y, c = lax.axis_index("x"), lax.axis_index("y"), lax.axis_index("c")
    return ((1 - x) if mask & 4 else x, (1 - y) if mask & 2 else y, (1 - c) if mask & 1 else c)


def _me():
    return lax.axis_index("x"), lax.axis_index("y"), lax.axis_index("c")


def _chip_of(dev):
    return 2 * dev[0] + dev[1]


CHIP_MASKS = (4, 2, 6)
ALL_MASKS = (1, 2, 3, 4, 5, 6, 7)
SIBLING_MASK = (1,)


def _remote(src, dst, send_sem, recv_sem, dev):
    return pltpu.make_async_remote_copy(src_ref=src, dst_ref=dst, send_sem=send_sem, recv_sem=recv_sem,
                                        device_id=dev, device_id_type=pl.DeviceIdType.MESH)


class _StagedCopy:
    def __init__(self, src, stage, dst, load_sem, store_sem):
        self.load = pltpu.make_async_copy(src, stage, load_sem)
        self.store = pltpu.make_async_copy(stage, dst, store_sem)

    def start(self):
        self.load.start()
        self.load.wait()
        self.store.start()

    def wait(self):
        self.store.wait()


class DirectExchange:
    def __init__(self, arrays, pieces, masks, slot_kind, nslots, keep_own=True):
        self.arrays, self.pieces, self.masks, self.slot_kind = list(arrays), list(pieces), masks, slot_kind
        self.keep_own = keep_own
        n, nk = len(arrays), len(masks)
        shapes = [a.shape if p is None else p[1] for a, p in zip(arrays, pieces)]
        self.out_shape = [jax.ShapeDtypeStruct((nslots,) + tuple(s), a.dtype) for s, a in zip(shapes, arrays)]
        self.scratch = [pltpu.SemaphoreType.DMA((n * nk,)), pltpu.SemaphoreType.DMA((n * nk,))]
        if keep_own:
            self.scratch += [pltpu.SemaphoreType.DMA((2 * n,))] + [pltpu.VMEM(s, a.dtype) for s, a in zip(shapes, arrays)]
        self.has_mid = False

    def _copies(self, ins, outs, scratch):
        send_sems, recv_sems = scratch[:2]
        me = _me()
        slot = {"chip": _chip_of(me), "dev": 4 * me[0] + 2 * me[1] + me[2], "core": me[2]}[self.slot_kind]
        nk = len(self.masks)

        def piece(a, dev):
            return ins[a] if self.pieces[a] is None else self.pieces[a][0](ins[a], dev)

        local = []
        if self.keep_own:
            local_sems, stages = scratch[2], scratch[3:]
            local = [_StagedCopy(piece(a, me), stages[a], outs[a].at[slot], local_sems.at[2 * a], local_sems.at[2 * a + 1])
                     for a in range(len(ins))]
        remote = []
        for a in range(len(ins)):
            for ki, mask in enumerate(self.masks):
                dev = _peer(mask)
                remote.append(_remote(piece(a, dev), outs[a].at[slot], send_sems.at[a * nk + ki],
                                      recv_sems.at[a * nk + ki], dev))
        return local, remote

    def start(self, ins, outs, scratch):
        local, remote = self._copies(ins, outs, scratch)
        for cp in remote + local:
            cp.start()

    def finish(self, ins, outs, scratch):
        local, remote = self._copies(ins, outs, scratch)
        for cp in remote + local:
            cp.wait()


SPLIT_ROWS = 16


class TwoLevelGather:
    def __init__(self, arrays):
        self.arrays = list(arrays)
        n = len(arrays)
        self.out_shape = [jax.ShapeDtypeStruct((4,) + a.shape, a.dtype) for a in arrays]
        self.scratch = ([pltpu.SemaphoreType.DMA((4 * n,)), pltpu.SemaphoreType.DMA((4 * n,)),
                         pltpu.SemaphoreType.DMA((3 * n,)), pltpu.SemaphoreType.DMA((3 * n,)),
                         pltpu.SemaphoreType.DMA((2 * n,))] + [pltpu.VMEM(a.shape, a.dtype) for a in arrays])
        self.has_mid = True

    def _copies(self, ins, outs, scratch):
        ici_send, ici_recv, fwd_send, fwd_recv, local_sems = scratch[:5]
        stages = scratch[5:]
        me = _me()
        sibling, in_x, in_y, diagonal = _peer(1), _peer(4), _peer(2), _peer(6)
        plan = []
        for a in range(len(ins)):
            half = ins[a].shape[0] // 2
            first = half // 2 if half % (2 * SPLIT_ROWS) == 0 else half
            mine = pl.ds(me[2] * half, half)
            local = _StagedCopy(ins[a], stages[a], outs[a].at[_chip_of(me)], local_sems.at[2 * a], local_sems.at[2 * a + 1])

            def ici(k, src, dst, dev):
                return _remote(src, dst, ici_send.at[4 * a + k], ici_recv.at[4 * a + k], dev)

            def d2d(k, chip):
                zone = outs[a].at[_chip_of(chip), mine]
                return _remote(zone, zone, fwd_send.at[3 * a + k], fwd_recv.at[3 * a + k], sibling)

            own_zone = outs[a].at[_chip_of(me), mine]
            from_x = outs[a].at[_chip_of(in_x), pl.ds(me[2] * half, first)]
            onward = [ici(2, from_x, from_x, in_y), None]
            if first < half:
                from_y = outs[a].at[_chip_of(in_y), pl.ds(me[2] * half + first, half - first)]
                onward[1] = ici(3, from_y, from_y, in_x)
            plan.append(dict(
                local=local,
                own=[ici(0, ins[a].at[mine], own_zone, in_x), ici(1, ins[a].at[mine], own_zone, in_y)],
                onward=onward, sibling=[d2d(0, in_x), d2d(1, in_y), d2d(2, diagonal)]))
        return plan

    def start(self, ins, outs, scratch):
        for p in self._copies(ins, outs, scratch):
            for cp in p["own"]:
                cp.start()
            p["local"].start()

    def mid(self, ins, outs, scratch):
        plan = self._copies(ins, outs, scratch)
        for p in plan:
            for k in range(2):
                p["own"][k].wait_recv()
                if p["onward"][k] is not None:
                    p["onward"][k].start()
                p["sibling"][k].start()
        for p in plan:
            for cp in p["onward"]:
                if cp is not None:
                    cp.wait_recv()
            p["sibling"][2].start()

    def finish(self, ins, outs, scratch):
        for p in self._copies(ins, outs, scratch):
            for cp in p["sibling"]:
                cp.wait_recv()
            for cp in p["own"] + p["sibling"] + [cp for cp in p["onward"] if cp is not None]:
                cp.wait_send()
            p["local"].wait()


class Both:
    def __init__(self, a, b):
        self.a, self.b = a, b
        self.arrays, self.out_shape = a.arrays + b.arrays, a.out_shape + b.out_shape
        self.scratch = a.scratch + b.scratch
        self.has_mid = False
        assert not (a.has_mid or b.has_mid)

    def _parts(self, ins, outs, sems):
        na, sa = len(self.a.arrays), len(self.a.scratch)
        return (ins[:na], outs[:na], sems[:sa]), (ins[na:], outs[na:], sems[sa:])

    def start(self, ins, outs, sems):
        pa, pb = self._parts(ins, outs, sems)
        self.a.start(*pa)
        self.b.start(*pb)

    def finish(self, ins, outs, sems):
        pa, pb = self._parts(ins, outs, sems)
        self.a.finish(*pa)
        self.b.finish(*pb)


_ANY = pl.BlockSpec(memory_space=pl.ANY)


def run_comm(name, comm):
    n = len(comm.arrays)

    def body(*refs):
        ins, outs, sems = refs[:n], refs[n:2 * n], refs[2 * n:]
        comm.start(ins, outs, sems)
        if comm.has_mid:
            comm.mid(ins, outs, sems)
        comm.finish(ins, outs, sems)

    return pl.pallas_call(body, name=name, out_shape=comm.out_shape, in_specs=[_ANY] * n, out_specs=[_ANY] * n,
                          scratch_shapes=comm.scratch,
                          compiler_params=pltpu.CompilerParams(vmem_limit_bytes=VMEM_LIMIT))(*comm.arrays)


_HBM = pl.BlockSpec(memory_space=pltpu.HBM)
_SEM = pl.BlockSpec(memory_space=pltpu.SEMAPHORE)
_SIDE_EFFECT = pltpu.SideEffectType.DATAFLOW_SIDE_EFFECTING


def _chip_copies(srcs, lands, send_sems, recv_sems, by_target):
    me = _me()
    copies = []
    for a, (src, land) in enumerate(zip(srcs, lands)):
        for ki, mask in enumerate(CHIP_MASKS):
            dev = _peer(mask)
            k = a * len(CHIP_MASKS) + ki
            piece = src.at[_chip_of(dev)] if by_target else src
            copies.append(_remote(piece, land.at[_chip_of(me)], send_sems.at[k], recv_sems.at[k], dev))
    return copies


def chip_exchange_start(name, arrays, by_target):
    n = len(arrays)
    nsem = n * len(CHIP_MASKS)
    piece_shapes = [a.shape[1:] if by_target else a.shape for a in arrays]

    def body(*refs):
        srcs, lands = refs[:n], refs[n:2 * n]
        send_sems, recv_sems = refs[2 * n:2 * n + 2]
        token = refs[4 * n + 2]
        stages, local_sems = refs[4 * n + 3:5 * n + 3], refs[5 * n + 3]
        me = _me()
        for cp in _chip_copies(srcs, lands, send_sems, recv_sems, by_target):
            cp.start()
        for a in range(n):
            own = _StagedCopy(srcs[a].at[_chip_of(me)] if by_target else srcs[a], stages[a], lands[a].at[_chip_of(me)],
                              local_sems.at[2 * a], local_sems.at[2 * a + 1])
            own.start()
            own.wait()
        token[...] = jnp.zeros_like(token)

    lands = [lax.empty((4,) + tuple(s), a.dtype) for s, a in zip(piece_shapes, arrays)]
    hbm = lambda a: pltpu.HBM(a.shape, a.dtype)
    res = pl.pallas_call(
        body, name=name,
        out_shape=(pltpu.SemaphoreType.DMA((nsem,)), pltpu.SemaphoreType.DMA((nsem,)), *[hbm(a) for a in arrays],
                   *[hbm(l) for l in lands], jax.ShapeDtypeStruct((8, LANES), F32)),
        in_specs=(_HBM,) * (2 * n), out_specs=(_SEM, _SEM) + (_HBM,) * (2 * n) + (pl.BlockSpec(memory_space=pltpu.VMEM),),
        input_output_aliases={i: i + 2 for i in range(2 * n)},
        scratch_shapes=[pltpu.VMEM(tuple(s), a.dtype) for s, a in zip(piece_shapes, arrays)]
        + [pltpu.SemaphoreType.DMA((2 * n,))],
        compiler_params=pltpu.CompilerParams(has_side_effects=_SIDE_EFFECT, vmem_limit_bytes=VMEM_LIMIT),
    )(*[pltpu.with_memory_space_constraint(a, pltpu.HBM) for a in arrays + lands])
    return res[:-1], res[-1]


def chip_exchange_wait(name, in_flight, by_target, after):
    send_sems, recv_sems, *thru = in_flight
    n = len(thru) // 2

    def body(*refs):
        srcs, lands, (send, recv) = refs[:n], refs[n:2 * n], refs[2 * n:2 * n + 2]
        for cp in _chip_copies(srcs, lands, send, recv, by_target):
            cp.wait_send()
            cp.wait_recv()

    return pl.pallas_call(
        body, name=name, out_shape=tuple(pltpu.HBM(t.shape, t.dtype) for t in thru),
        in_specs=(_HBM,) * (2 * n) + (_SEM, _SEM, pl.BlockSpec(memory_space=pl.ANY)), out_specs=(_HBM,) * (2 * n),
        input_output_aliases={i: i for i in range(2 * n)},
        compiler_params=pltpu.CompilerParams(has_side_effects=_SIDE_EFFECT),
    )(*thru, send_sems, recv_sems, after)[n:]


def _call_with_comm(body, comm, steps, args, *, name, out_shape, in_specs, out_specs, grid, scratch=()):
    ni, no, ns, nc = len(in_specs), len(out_specs), len(scratch), len(comm.arrays)

    def full_body(*refs):
        ins, cins = refs[:ni], refs[ni:ni + nc]
        outs, couts = refs[ni + nc:ni + nc + no], refs[ni + nc + no:ni + 2 * nc + no]
        scr, csems = refs[ni + 2 * nc + no:ni + 2 * nc + no + ns], refs[ni + 2 * nc + no + ns:]
        first, middle, last = steps()
        pl.when(first)(lambda: comm.start(cins, couts, csems))
        if comm.has_mid:
            pl.when(middle)(lambda: comm.mid(cins, couts, csems))
        body(*ins, *outs, *scr)
        pl.when(last)(lambda: comm.finish(cins, couts, csems))

    res = pl.pallas_call(
        full_body, name=name, out_shape=list(out_shape) + comm.out_shape, grid=grid,
        in_specs=list(in_specs) + [_ANY] * nc, out_specs=list(out_specs) + [_ANY] * nc,
        scratch_shapes=list(scratch) + comm.scratch,
        compiler_params=pltpu.CompilerParams(dimension_semantics=("arbitrary",) * len(grid),
                                             vmem_limit_bytes=VMEM_LIMIT))(*args, *comm.arrays)
    return res[:no], res[no:]


def _shard_pieces(chip):
    if chip < 3:
        return [(0, SHARD_W, 8 * chip)]
    behind_dt = DT_STORED_START + SSM_HEADS - 3 * SHARD_W
    return [(0, behind_dt, 24), (behind_dt, SHARD_W - behind_dt, behind_dt + 24 + DT_PAD)]


W_IN_COLS = 256


def pack_w_in(wt):
    def body(w_ref, o_ref, pad_ref):
        chip = _chip_index()
        pad_ref[...] = jnp.zeros_like(pad_ref)
        for cv in range(4):
            @pl.when(chip == cv)
            def _():
                for src, n, dst in _shard_pieces(cv):
                    pad_ref[dst:dst + n, :] = w_ref[src:src + n, :]
        o_ref[...] = pad_ref[...].astype(BF16)

    return _call(body, name="pack_w_in", grid=(D_MODEL // W_IN_COLS,),
                 in_specs=[pl.BlockSpec((SHARD_W, W_IN_COLS), lambda i: (0, i))],
                 out_specs=pl.BlockSpec((PACK_W, W_IN_COLS), lambda i: (0, i)),
                 out_shape=jax.ShapeDtypeStruct((PACK_W, D_MODEL), BF16),
                 scratch=[pltpu.VMEM((PACK_W, W_IN_COLS), F32)], sem=("parallel",))(wt)


def _tile_runs():
    runs, fix = [], []
    for t in range(N_ALIGNED // LANES):
        s = min(t // 19, 3)
        j = t - 19 * s
        p = _act_col(t * LANES)
        if runs and runs[-1][1] == s and runs[-1][0] + runs[-1][3] == p and runs[-1][2] + runs[-1][3] == j * LANES:
            runs[-1][3] += LANES
        else:
            runs.append([p, s, j * LANES, LANES])
        if j == 0 and s > 0:
            fix.append((p, s - 1))
    return runs, fix


def unpack_w_in(bg):
    runs, fix = _tile_runs()

    def body(b_ref, o_ref):
        for p, s, j, w in runs:
            o_ref[p:p + w, :] = b_ref[s, j:j + w, :]
        for p, s in fix:
            o_ref[p:p + LANES, :] = o_ref[p:p + LANES, :] + b_ref[s, SHARD_STRIDE:PACK_W, :]
        o_ref[N_ALIGNED:N_ACT, :] = jnp.zeros((N_ACT - N_ALIGNED, W_IN_COLS), BF16)

    return _call(body, name="unpack_w_in", grid=(D_MODEL // W_IN_COLS,),
                 in_specs=[pl.BlockSpec((4, PACK_W, W_IN_COLS), lambda i: (0, 0, i))],
                 out_specs=pl.BlockSpec((N_ACT, W_IN_COLS), lambda i: (0, i)),
                 out_shape=jax.ShapeDtypeStruct((N_ACT, D_MODEL), BF16), sem=("parallel",))(bg)


def pack_grad_w_in(dwt):
    def body(g_ref, o_ref):
        for s in range(4):
            for j in range(PACK_W // LANES):
                p = _act_col((19 * s + j) * LANES)
                o_ref[s, j * LANES:(j + 1) * LANES, :] = g_ref[p:p + LANES, :]

    return _call(body, name="pack_grad_w_in", grid=(D_MODEL // W_IN_COLS,),
                 in_specs=[pl.BlockSpec((N_ACT, W_IN_COLS), lambda i: (0, i))],
                 out_specs=pl.BlockSpec((4, PACK_W, W_IN_COLS), lambda i: (0, 0, i)),
                 out_shape=jax.ShapeDtypeStruct((4, PACK_W, D_MODEL), BF16), sem=("parallel",))(dwt)


def _adamw(w, g, m, v):
    m = ADAM_B1 * m + (1.0 - ADAM_B1) * g
    v = ADAM_B2 * v + (1.0 - ADAM_B2) * jnp.square(g)
    m_hat = m / (1.0 - ADAM_B1 ** ADAM_STEP)
    v_hat = v / (1.0 - ADAM_B2 ** ADAM_STEP)
    delta = -ADAM_LR * (m_hat / (jnp.sqrt(v_hat) + ADAM_EPS) + ADAM_WD * w)
    return delta, m, v


def adamw_w_in(g_packed, wt, mt, vt):
    cols = LANES

    def body(g_ref, w_ref, m_ref, v_ref, go_ref, d_ref, mo_ref, vo_ref):
        chip = _chip_index()
        for cv in range(4):
            @pl.when(chip == cv)
            def _():
                for dst, n, src in _shard_pieces(cv):
                    go_ref[dst:dst + n, :] = g_ref[src:src + n, :]
        d_ref[...], mo_ref[...], vo_ref[...] = _adamw(w_ref[...], go_ref[...], m_ref[...], v_ref[...])

    spec = pl.BlockSpec((SHARD_W, cols), lambda i: (0, i))
    shp = jax.ShapeDtypeStruct((SHARD_W, D_MODEL), F32)
    return _call(body, name="adamw_w_in", grid=(D_MODEL // cols,),
                 in_specs=[pl.BlockSpec((PACK_W, cols), lambda i: (0, i)), spec, spec, spec],
                 out_specs=[spec] * 4, out_shape=[shp] * 4, sem=("parallel",))(g_packed, wt, mt, vt)


def adamw_rows(name, g, w, m, v):
    r, c = g.shape
    rows = min(r, BLOCK)

    def body(g_ref, w_ref, m_ref, v_ref, d_ref, mo_ref, vo_ref):
        d_ref[...], mo_ref[...], vo_ref[...] = _adamw(w_ref[...], g_ref[...], m_ref[...], v_ref[...])

    spec = pl.BlockSpec((rows, c), lambda i: (i, 0))
    shp = jax.ShapeDtypeStruct((r, c), F32)
    return _call(body, name=name, grid=(r // rows,), in_specs=[spec] * 4, out_specs=[spec] * 3, out_shape=[shp] * 3,
                 sem=("parallel",))(g, w, m, v)


def adamw_small(gs, ws, ms, vs):
    n = len(gs)

    def body(*refs):
        g, w, m, v = refs[:n], refs[n:2 * n], refs[2 * n:3 * n], refs[3 * n:4 * n]
        outs = refs[4 * n:]
        for i in range(n):
            d, mn, vn = _adamw(w[i][...], g[i][...], m[i][...], v[i][...])
            outs[3 * i][...] = d
            outs[3 * i + 1][...] = mn
            outs[3 * i + 2][...] = vn

    specs = [_full(a.shape) for a in gs]
    res = _call(body, name="adamw_small", in_specs=specs * 4,
                out_specs=[s for s in specs for _ in range(3)],
                out_shape=[jax.ShapeDtypeStruct(a.shape, F32) for a in gs for _ in range(3)])(*gs, *ws, *ms, *vs)
    return [tuple(res[3 * i:3 * i + 3]) for i in range(n)]


def sum_slots(name, r):
    s, rr, c = r.shape
    rows = min(rr, BLOCK)

    def body(r_ref, o_ref):
        acc = r_ref[0].astype(F32)
        for k in range(1, s):
            acc = acc + r_ref[k].astype(F32)
        o_ref[...] = acc

    return _call(body, name=name, grid=(rr // rows,), in_specs=[pl.BlockSpec((s, rows, c), lambda i: (0, i, 0))],
                 out_specs=pl.BlockSpec((rows, c), lambda i: (i, 0)), out_shape=jax.ShapeDtypeStruct((rr, c), F32),
                 sem=("parallel",))(r)


def sum_pair(partial, from_sibling):
    s, _, rr, cols = partial.shape
    rows = rr // 2

    def body(p_ref, r_ref, o_ref):
        c = lax.axis_index("c")
        o_ref[...] = (p_ref[c].astype(F32) + r_ref[1 - c].astype(F32)).astype(BF16)

    return _call(body, name="sum_pair", grid=(s, rr // rows),
                 in_specs=[pl.BlockSpec((None, 2, rows, cols), lambda k, i: (k, 0, i, 0)),
                           pl.BlockSpec((2, None, rows, cols), lambda k, i: (0, k, i, 0))],
                 out_specs=pl.BlockSpec((None, rows, cols), lambda k, i: (k, i, 0)),
                 out_shape=jax.ShapeDtypeStruct((s, rr, cols), BF16), sem=("parallel", "parallel"))(partial, from_sibling)


def _col_tile(n, k):
    if n % 896 == 0 and k <= 1024:
        return 896
    return min(n, 512)


def project(name, x, wt, after):
    m, k = x.shape
    n = wt.shape[0]
    tn = D_MODEL

    def body(x_ref, w_ref, after_ref, o_ref):
        o_ref[...] = lax.dot_general(x_ref[...], w_ref[...], _NT, preferred_element_type=F32)

    return _call(body, name=name, grid=(n // tn,),
                 in_specs=[_full((m, k)), pl.BlockSpec((tn, k), lambda j: (j, 0)), _full(after.shape)],
                 out_specs=pl.BlockSpec((m, tn), lambda j: (0, j)), out_shape=jax.ShapeDtypeStruct((m, n), F32),
                 sem=("parallel",))(x, wt, after)


def _piece_tiles(pieces):
    spans, start = [], 0
    for p in pieces:
        spans.append((start, p.shape[1] // D_MODEL))
        start += p.shape[1] // D_MODEL
    return spans, start


def _piece_spec(tm, span, rows_of, tile_of):
    first, count = span

    def index(i, j):
        t = tile_of(i, j) - first
        mine = (t >= 0) & (t < count)
        return jnp.where(mine, rows_of(i, j), 0), jnp.clip(t, 0, count - 1)

    return pl.BlockSpec((tm, D_MODEL), index)


def project_back(name, pieces, wt, after):
    m = pieces[0].shape[0]
    k = wt.shape[1]
    tm = m // 2
    spans, steps = _piece_tiles(pieces)

    def body(*refs):
        w_ref, o_ref = refs[len(pieces)], refs[len(pieces) + 2]
        j = pl.program_id(1)

        @pl.when(j == 0)
        def _():
            o_ref[...] = jnp.zeros_like(o_ref)

        for dy_ref, (first, count) in zip(refs, spans):
            @pl.when((j >= first) & (j < first + count))
            def _():
                o_ref[...] += jnp.dot(dy_ref[...], w_ref[...], preferred_element_type=F32)

    return _call(body, name=name, grid=(m // tm, steps),
                 in_specs=[_piece_spec(tm, s, lambda i, j: i, lambda i, j: j) for s in spans]
                 + [pl.BlockSpec((D_MODEL, k), lambda i, j: (j, 0)), _full(after.shape)],
                 out_specs=pl.BlockSpec((tm, k), lambda i, j: (i, 0)), out_shape=jax.ShapeDtypeStruct((m, k), F32),
                 sem=("parallel", "arbitrary"))(*pieces, wt, after)


def weight_grad_t(name, pieces, x):
    m = pieces[0].shape[0]
    k = x.shape[1]
    tm = m // 2
    spans, steps = _piece_tiles(pieces)

    def body(*refs):
        x_ref, o_ref, acc_ref = refs[len(pieces):]
        i, half = pl.program_id(0), pl.program_id(1)
        for dy_ref, (first, count) in zip(refs, spans):
            @pl.when((i >= first) & (i < first + count))
            def _():
                part = lax.dot_general(dy_ref[...], x_ref[...], (((0,), (0,)), ((), ())), preferred_element_type=F32)

                @pl.when(half == 0)
                def _():
                    acc_ref[...] = part

                @pl.when(half == 1)
                def _():
                    o_ref[...] = (acc_ref[...] + part).astype(BF16)

    return _call(body, name=name, grid=(steps, 2),
                 in_specs=[_piece_spec(tm, s, lambda i, j: j, lambda i, j: i) for s in spans]
                 + [pl.BlockSpec((tm, k), lambda i, j: (j, 0))],
                 out_specs=pl.BlockSpec((D_MODEL, k), lambda i, j: (i, 0)),
                 out_shape=jax.ShapeDtypeStruct((steps * D_MODEL, k), BF16),
                 scratch=[pltpu.VMEM((D_MODEL, k), F32)], sem=("parallel", "arbitrary"))(*pieces, x)


def mm_tn(name, x, dy):
    m, k = x.shape
    n = dy.shape[1]
    tm = m // 2
    tn = _col_tile(n, k)

    def body(x_ref, dy_ref, o_ref, acc_ref):
        part = lax.dot_general(x_ref[...].astype(BF16), dy_ref[...].astype(BF16), (((0,), (0,)), ((), ())),
                               preferred_element_type=F32)

        @pl.when(pl.program_id(1) == 0)
        def _():
            acc_ref[...] = part

        @pl.when(pl.program_id(1) == 1)
        def _():
            o_ref[...] = (acc_ref[...] + part).astype(BF16)

    return _call(body, name=name, grid=(n // tn, 2),
                 in_specs=[pl.BlockSpec((tm, k), lambda i, j: (j, 0)), pl.BlockSpec((tm, tn), lambda i, j: (j, i))],
                 out_specs=pl.BlockSpec((k, tn), lambda i, j: (0, i)), out_shape=jax.ShapeDtypeStruct((k, n), BF16),
                 scratch=[pltpu.VMEM((k, tn), F32)], sem=("parallel", "arbitrary"))(x, dy)


def _row_spec(width, col_block=0):
    return pl.BlockSpec((BLOCK, width), lambda i: (i, col_block))


def _x_spec():
    return pl.BlockSpec((None, BLOCK, D_MODEL), lambda i: (0, jnp.maximum(i - 1, 0), 0))


def prep(x, meta, g_pre):
    nb = x.shape[1] // BLOCK + 1

    def body(x_ref, meta_ref, g_ref, h_ref, u_ref):
        i = pl.program_id(0)

        @pl.when(i == 0)
        def _():
            h_ref[0:PAD_ROWS, :] = jnp.zeros((PAD_ROWS, D_MODEL), F32)
            h_ref[PAD_ROWS:BLOCK, :] = meta_ref[...]

        @pl.when(i > 0)
        def _():
            h_ref[...] = x_ref[...]

        u_ref[...] = _rms(h_ref[...], g_ref[...]).astype(BF16)

    return _call(body, name="prep", grid=(nb,), in_specs=[_x_spec(), _full((N_META, D_MODEL)), _full((1, D_MODEL))],
                 out_specs=[_row_spec(D_MODEL), _row_spec(D_MODEL)],
                 out_shape=[jax.ShapeDtypeStruct((nb * BLOCK, D_MODEL), F32),
                            jax.ShapeDtypeStruct((nb * BLOCK, D_MODEL), BF16)], sem=("parallel",))(x, meta, g_pre)


def prep_bwd(h, du, dres, g_pre):
    nb = h.shape[0] // BLOCK

    def body(h_ref, du_ref, dres_ref, g_ref, gx_ref, gm_ref, gg_ref):
        i = pl.program_id(0)
        _, vjp = jax.vjp(_rms, h_ref[...], g_ref[...])
        dh, dg = vjp(du_ref[...])

        @pl.when(i == 0)
        def _():
            gm_ref[...] = dh[PAD_ROWS:BLOCK, :]
            gg_ref[...] = dg

        @pl.when(i > 0)
        def _():
            gg_ref[...] += dg

        gx_ref[...] = dh + dres_ref[...]

    return _call(body, name="prep_bwd", grid=(nb,),
                 in_specs=[_row_spec(D_MODEL), _row_spec(D_MODEL), _row_spec(D_MODEL), _full((1, D_MODEL))],
                 out_specs=[_x_spec(), _full((N_META, D_MODEL)), _full((1, D_MODEL))],
                 out_shape=[jax.ShapeDtypeStruct((1, (nb - 1) * BLOCK, D_MODEL), F32),
                            jax.ShapeDtypeStruct((N_META, D_MODEL), F32), jax.ShapeDtypeStruct((1, D_MODEL), F32)],
                 sem=("arbitrary",))(h, du, dres, g_pre)


GROUP_W = SSM_INNER // SSM_GROUPS


def _gated_norm(y, z, g):
    t = y * _silu(z)
    return t * lax.rsqrt(jnp.mean(t * t, axis=-1, keepdims=True) + NORM_EPS) * g


def _gated_norm_groups(y, z, g):
    groups = [slice(k * GROUP_W, (k + 1) * GROUP_W) for k in range(SSM_GROUPS)]
    return jnp.concatenate([_gated_norm(y[:, s], z[:, s], g[:, s]) for s in groups], axis=1)


def _merge(ga, gs, ya, ys):
    return _sigmoid(ga) * ya + _sigmoid(gs) * ys


GATE_ATT_BLOCK = SEG["gate_att"][2] // D_MODEL
GATE_SSM_BLOCK = SEG["gate_ssm"][2] // D_MODEL


def _row_loss(out, g_post, x, target):
    diff = x + _rms(out, g_post) - target
    return 0.5 * jnp.sum(diff * diff) / D_MODEL


def tail(y_ssd, proj, a_att, x, target, woa, wos, wo, g_norm, g_post):
    nb = y_ssd.shape[0] // BLOCK
    rows = nb * BLOCK

    def body(y_ref, z_ref, ga_ref, gs_ref, a_ref, x_ref, t_ref, woa_ref, wos_ref, wo_ref, gn_ref, gp_ref,
             yn_ref, mg_ref, dout_ref, dya_ref, dys_ref, da_ref, dy_ref, dz_ref, dga_ref, dgs_ref, dres_ref,
             loss_ref, dgp_ref, dgn_ref):
        i = pl.program_id(0)
        yn, norm_vjp = jax.vjp(_gated_norm_groups, y_ref[...], z_ref[...], gn_ref[...])
        yn16 = yn.astype(BF16)
        y_ssm = jnp.dot(yn16, wos_ref[...], preferred_element_type=F32)
        y_att = jnp.dot(a_ref[...], woa_ref[...], preferred_element_type=F32)
        merged, merge_vjp = jax.vjp(_merge, ga_ref[...], gs_ref[...], y_att, y_ssm)
        merged16 = merged.astype(BF16)
        out = jnp.dot(merged16, wo_ref[...], preferred_element_type=F32)
        loss, loss_vjp = jax.vjp(_row_loss, out, gp_ref[...], x_ref[...], t_ref[...])
        counted = jnp.where(i > 0, 1.0, 0.0)
        dout, dgp, dres, _ = loss_vjp(counted)
        dout16 = dout.astype(BF16)
        dmerged = lax.dot_general(dout16, wo_ref[...], _NT, preferred_element_type=F32)
        dga, dgs, dya, dys = merge_vjp(dmerged)
        dya16, dys16 = dya.astype(BF16), dys.astype(BF16)
        dyn = lax.dot_general(dys16, wos_ref[...], _NT, preferred_element_type=F32)
        dy, dz, dgn = norm_vjp(dyn)

        yn_ref[...] = yn16
        mg_ref[...] = merged16
        dout_ref[...] = dout16
        dya_ref[...] = dya16
        dys_ref[...] = dys16
        da_ref[...] = lax.dot_general(dya16, woa_ref[...], _NT, preferred_element_type=F32)
        dy_ref[...] = dy
        dz_ref[...] = dz.astype(BF16)
        dga_ref[...] = dga.astype(BF16)
        dgs_ref[...] = dgs.astype(BF16)
        dres_ref[...] = dres

        @pl.when(i == 0)
        def _():
            loss_ref[...] = jnp.zeros_like(loss_ref)
            dgp_ref[...] = jnp.zeros_like(dgp_ref)
            dgn_ref[...] = jnp.zeros_like(dgn_ref)

        loss_ref[...] += loss * counted
        dgp_ref[...] += dgp
        dgn_ref[...] += dgn

    wide, narrow = _row_spec(SSM_INNER), _row_spec(D_MODEL)
    resident = pl.BlockSpec(memory_space=pltpu.VMEM)
    bf = lambda w: jax.ShapeDtypeStruct((rows, w), BF16)
    f32 = lambda w: jax.ShapeDtypeStruct((rows, w), F32)
    return _call(body, name="tail", grid=(nb,),
                 in_specs=[wide, wide, _row_spec(D_MODEL, GATE_ATT_BLOCK), _row_spec(D_MODEL, GATE_SSM_BLOCK), narrow,
                           _x_spec(), _x_spec(), resident, resident, resident, _full((1, SSM_INNER)),
                           _full((1, D_MODEL))],
                 out_specs=[wide, narrow, narrow, narrow, narrow, narrow, wide, wide, narrow, narrow, narrow,
                            _full((8, LANES)), _full((1, D_MODEL)), _full((1, SSM_INNER))],
                 out_shape=[bf(SSM_INNER), bf(D_MODEL), bf(D_MODEL), bf(D_MODEL), bf(D_MODEL), f32(D_MODEL),
                            f32(SSM_INNER), bf(SSM_INNER), bf(D_MODEL), bf(D_MODEL), f32(D_MODEL),
                            jax.ShapeDtypeStruct((8, LANES), F32), jax.ShapeDtypeStruct((1, D_MODEL), F32),
                            jax.ShapeDtypeStruct((1, SSM_INNER), F32)],
                 sem=("arbitrary",))(y_ssd, proj, proj, proj, a_att, x, target, woa, wos, wo, g_norm, g_post)


_NT = (((1,), (1,)), ((), ()))
ALIBI_SLOPES = tuple(2.0 ** (-8.0 * (h + 1) / ATT_Q_HEADS) for h in range(ATT_Q_HEADS))
KV_WIDTH = ATT_KV_HEADS * HEAD_DIM
Q_BLOCK = SEG["q"][2] // D_MODEL
Z_ATT_BLOCK = SEG["z_att"][2] // D_MODEL
K_BLOCK = SEG["k"][2] // KV_WIDTH
V_BLOCK = SEG["v"][2] // KV_WIDTH
META_ROW_BLOCK = PAD_ROWS // N_META


@jax.custom_vjp
def _swap_halves(x):
    return pltpu.roll(x, HEAD_DIM, 1)


_swap_halves.defvjp(lambda x: (pltpu.roll(x, HEAD_DIM, 1), None), lambda _, g: (pltpu.roll(g, HEAD_DIM, 1),))


def _both_halves(t, half):
    first = lax.broadcasted_iota(jnp.int32, t.shape, 1) < HEAD_DIM
    sw = _swap_halves(t)
    return jnp.where(first, t, sw) if half == 0 else jnp.where(first, sw, t)


def _attn_rows(q, z, kp, kc, vp, vc, km, vm, sinks, n):
    rows = ATT_GROUP * BLOCK
    i = lax.broadcasted_iota(jnp.int32, (rows, BLOCK), 0) & (BLOCK - 1)
    j = lax.broadcasted_iota(jnp.int32, (rows, BLOCK), 1)
    rel_c = (i - j).astype(F32)
    rel_p = rel_c + float(BLOCK)
    nv = jnp.zeros((rows, BLOCK), jnp.int32) + n
    ok_c = (i >= j) & (nv >= 1)
    ok_p = (j > i) & (nv >= 2)
    im = lax.broadcasted_iota(jnp.int32, (rows, N_META), 0) & (BLOCK - 1)
    jm = lax.broadcasted_iota(jnp.int32, (rows, N_META), 1)
    ok_m = ((jnp.zeros((rows, N_META), jnp.int32) + n) >= 1) | (im >= PAD_ROWS + jm)
    first = lax.broadcasted_iota(jnp.int32, (BLOCK, LANES), 1) < HEAD_DIM
    neg = -jnp.inf
    outs = []
    for kv in range(ATT_KV_HEADS):
        tile, half = divmod(kv, 2)
        lanes = slice(tile * LANES, (tile + 1) * LANES)
        kc2, kp2, km2 = (_both_halves(t[:, lanes], half).astype(BF16) for t in (kc, kp, km))
        vc2, vp2, vm2 = (_both_halves(t[:, lanes], half).astype(BF16) for t in (vc, vp, vm))
        qs, slope, sk = [], [], []
        for pair in range(ATT_GROUP // 2):
            c0 = (kv * ATT_GROUP + 2 * pair) * HEAD_DIM
            qp = q[:, c0:c0 + LANES] * HEAD_DIM ** -0.5
            qs += [jnp.where(first, qp, 0.0), jnp.where(first, 0.0, qp)]
        for g in range(ATT_GROUP):
            slope.append(jnp.full((BLOCK, 1), ALIBI_SLOPES[kv * ATT_GROUP + g], F32))
            sk.append(jnp.broadcast_to(sinks[kv * ATT_GROUP + g], (BLOCK, 1)))
        qs = jnp.concatenate(qs, axis=0).astype(BF16)
        slope = jnp.concatenate(slope, axis=0)
        sk = jnp.concatenate(sk, axis=0)
        sc = jnp.where(ok_c, lax.dot_general(qs, kc2, _NT, preferred_element_type=F32) - slope * rel_c, neg)
        sp = jnp.where(ok_p, lax.dot_general(qs, kp2, _NT, preferred_element_type=F32) - slope * rel_p, neg)
        sm = jnp.where(ok_m, lax.dot_general(qs, km2, _NT, preferred_element_type=F32), neg)
        mx = jnp.maximum(jnp.maximum(jnp.max(sc, axis=1, keepdims=True), jnp.max(sp, axis=1, keepdims=True)),
                         jnp.maximum(jnp.max(sm, axis=1, keepdims=True), sk))
        mx = lax.stop_gradient(mx)
        ec, ep, em, es = jnp.exp(sc - mx), jnp.exp(sp - mx), jnp.exp(sm - mx), jnp.exp(sk - mx)
        den = (es + jnp.sum(ec, axis=1, keepdims=True) + jnp.sum(ep, axis=1, keepdims=True)
               + jnp.sum(em, axis=1, keepdims=True))
        inv = 1.0 / den
        o = (jnp.dot((ec * inv).astype(BF16), vc2, preferred_element_type=F32)
             + jnp.dot((ep * inv).astype(BF16), vp2, preferred_element_type=F32)
             + jnp.dot((em * inv).astype(BF16), vm2, preferred_element_type=F32))
        for pair in range(ATT_GROUP // 2):
            r0 = 2 * pair * BLOCK
            outs.append(jnp.where(first, o[r0:r0 + BLOCK], o[r0 + BLOCK:r0 + 2 * BLOCK]))
    return jnp.concatenate(outs, axis=1) * _silu(z)


def _attn_specs(nb, steps_clamped):
    def blk(t):
        return jnp.minimum(t, nb - 1) if steps_clamped else t

    wide = lambda col: pl.BlockSpec((BLOCK, D_MODEL), lambda t: (blk(t), col))
    cur = lambda col: pl.BlockSpec((BLOCK, KV_WIDTH), lambda t: (blk(t), col))
    prev = lambda col: pl.BlockSpec((BLOCK, KV_WIDTH), lambda t: (jnp.maximum(blk(t) - 1, 0), col))
    meta = lambda col: pl.BlockSpec((N_META, KV_WIDTH), lambda t: (META_ROW_BLOCK, col))
    sinks = pl.BlockSpec((ATT_Q_HEADS, 1, 1), lambda t: (0, 0, 0))
    return [wide(Q_BLOCK), wide(Z_ATT_BLOCK), prev(K_BLOCK), cur(K_BLOCK), prev(V_BLOCK), cur(V_BLOCK),
            meta(K_BLOCK), meta(V_BLOCK), sinks]


def attn_fwd(proj, sinks):
    nb = proj.shape[0] // BLOCK

    def body(q_ref, z_ref, kp_ref, kc_ref, vp_ref, vc_ref, km_ref, vm_ref, sk_ref, o_ref):
        o_ref[...] = _attn_rows(q_ref[...], z_ref[...], kp_ref[...], kc_ref[...], vp_ref[...], vc_ref[...],
                                km_ref[...], vm_ref[...], tuple(sk_ref[h] for h in range(ATT_Q_HEADS)),
                                pl.program_id(0)).astype(BF16)

    return _call(body, name="attn_fwd", grid=(nb,), in_specs=_attn_specs(nb, False), out_specs=_row_spec(D_MODEL),
                 out_shape=jax.ShapeDtypeStruct((nb * BLOCK, D_MODEL), BF16), sem=("parallel",))(*([proj] * 8), sinks)


def attn_bwd(da, proj, sinks):
    nb = proj.shape[0] // BLOCK
    last = nb - 1
    wide = pl.BlockSpec((BLOCK, D_MODEL), lambda t: (jnp.minimum(t, last), 0))
    done = pl.BlockSpec((BLOCK, KV_WIDTH), lambda t: (jnp.maximum(t - 1, 0), 0))
    meta = _full((N_META, KV_WIDTH))
    par = _full((ATT_Q_HEADS, 1, 1))

    def body(da_ref, q_ref, z_ref, kp_ref, kc_ref, vp_ref, vc_ref, km_ref, vm_ref, sk_ref,
             dq_ref, dz_ref, dk_ref, dv_ref, dkm_ref, dvm_ref, dsk_ref, ck_ref, cv_ref):
        t = pl.program_id(0)

        @pl.when(t == 0)
        def _():
            ck_ref[...] = jnp.zeros_like(ck_ref)
            cv_ref[...] = jnp.zeros_like(cv_ref)
            dkm_ref[...] = jnp.zeros_like(dkm_ref)
            dvm_ref[...] = jnp.zeros_like(dvm_ref)
            dsk_ref[...] = jnp.zeros_like(dsk_ref)

        @pl.when(t < nb)
        def _():
            def f(q, z, kp, kc, vp, vc, km, vm, sk):
                return _attn_rows(q, z, kp, kc, vp, vc, km, vm, sk, t)

            _, vjp = jax.vjp(f, q_ref[...], z_ref[...], kp_ref[...], kc_ref[...], vp_ref[...], vc_ref[...],
                             km_ref[...], vm_ref[...], tuple(sk_ref[h] for h in range(ATT_Q_HEADS)))
            dq, dz, dkp, dkc, dvp, dvc, dkm, dvm, dsk = vjp(da_ref[...])
            dq_ref[...] = dq.astype(BF16)
            dz_ref[...] = dz.astype(BF16)
            for h in range(ATT_Q_HEADS):
                dsk_ref[h] += dsk[h]
            dk_ref[...] = ck_ref[...] + dkp
            dv_ref[...] = cv_ref[...] + dvp
            ck_ref[...] = dkc
            cv_ref[...] = dvc
            dkm_ref[...] += dkm
            dvm_ref[...] += dvm

        @pl.when(t == nb)
        def _():
            dk_ref[...] = ck_ref[...]
            dv_ref[...] = cv_ref[...]

    rows = nb * BLOCK
    return _call(body, name="attn_bwd", grid=(nb + 1,), in_specs=[wide] + _attn_specs(nb, True),
                 out_specs=[wide, wide, done, done, meta, meta, par],
                 out_shape=[jax.ShapeDtypeStruct((rows, D_MODEL), BF16), jax.ShapeDtypeStruct((rows, D_MODEL), BF16),
                            jax.ShapeDtypeStruct((rows, KV_WIDTH), F32), jax.ShapeDtypeStruct((rows, KV_WIDTH), F32),
                            jax.ShapeDtypeStruct((N_META, KV_WIDTH), F32), jax.ShapeDtypeStruct((N_META, KV_WIDTH), F32),
                            jax.ShapeDtypeStruct(sinks.shape, F32)],
                 scratch=[pltpu.VMEM((BLOCK, KV_WIDTH), F32), pltpu.VMEM((BLOCK, KV_WIDTH), F32)],
                 sem=("arbitrary",))(da, *([proj] * 8), sinks)


XBC_BLOCK0 = SEG["xbc"][2] // D_MODEL
CONV_COL_BLOCKS = CONV_DIM // D_MODEL
DT_TILE = SEG["dt"][2] // LANES


HALO = 8


def _conv_rows(length):
    return 544 if length % 544 == 0 else BLOCK


def _shift_rows(cur, before, j):
    if j == 0:
        return cur
    n = cur.shape[0]
    row = lax.broadcasted_iota(jnp.int32, cur.shape, 0)
    head = pltpu.roll(before, j, 0)
    if n > HALO:
        head = jnp.concatenate([head, jnp.zeros((n - HALO, cur.shape[1]), cur.dtype)], axis=0)
    return jnp.where(row >= j, pltpu.roll(cur, j, 0), head)


def _conv_pre(cur, before, w_ref, b_ref):
    pre = b_ref[...] + w_ref[CONV_WIDTH - 1:CONV_WIDTH, :] * cur
    for k in range(CONV_WIDTH - 1):
        pre = pre + w_ref[k:k + 1, :] * _shift_rows(cur, before, CONV_WIDTH - 1 - k)
    return pre


def _conv_specs(steps, rows, col0=0):
    first = XBC_BLOCK0 + col0
    halos = rows // HALO
    cur = pl.BlockSpec((rows, D_MODEL), lambda j, i: (i, first + j))
    before = pl.BlockSpec((HALO, D_MODEL), lambda j, i: (jnp.maximum(i * halos - 1, 0), first + j))
    after = pl.BlockSpec((HALO, D_MODEL), lambda j, i: (jnp.minimum(i + 1, steps - 1) * halos, first + j))
    return cur, before, after


def _valid_rows(i, rows):
    row = lax.broadcasted_iota(jnp.int32, (rows, D_MODEL), 0)
    return jnp.maximum((row >= PAD_ROWS).astype(F32), jnp.where(i > 0, 1.0, 0.0))


def conv_fwd(proj, conv_w, conv_b):
    rows = _conv_rows(proj.shape[0])
    steps = proj.shape[0] // rows
    cur, before, _ = _conv_specs(steps, rows)

    def body(c_ref, p_ref, w_ref, b_ref, o_ref):
        i = pl.program_id(1)
        pre = _conv_pre(c_ref[...], p_ref[...] * jnp.where(i > 0, 1.0, 0.0), w_ref, b_ref)
        o_ref[...] = _silu(pre) * _valid_rows(i, rows)

    return _call(body, name="conv_fwd", grid=(CONV_COL_BLOCKS, steps),
                 in_specs=[cur, before, pl.BlockSpec((CONV_WIDTH, D_MODEL), lambda j, i: (0, j)),
                           pl.BlockSpec((1, D_MODEL), lambda j, i: (0, j))],
                 out_specs=pl.BlockSpec((rows, D_MODEL), lambda j, i: (i, j)),
                 out_shape=jax.ShapeDtypeStruct((proj.shape[0], CONV_DIM), F32),
                 sem=("parallel", "parallel"))(proj, proj, conv_w, conv_b)


def conv_bwd(name, dparts, col0, proj, conv_w, conv_b):
    rows = _conv_rows(proj.shape[0])
    steps = proj.shape[0] // rows
    last = steps - 1
    ncol = sum(d.shape[1] for d in dparts) // D_MODEL
    np_ = len(dparts)
    cur, before, after = _conv_specs(steps, rows, col0)
    dcur = [pl.BlockSpec((rows, d.shape[1] // ncol), lambda j, i: (i, j)) for d in dparts]
    dafter = [pl.BlockSpec((HALO, d.shape[1] // ncol), lambda j, i: (jnp.minimum(i + 1, last) * (rows // HALO), j))
              for d in dparts]
    out_cur = pl.BlockSpec((rows, D_MODEL), lambda j, i: (i, j))
    wspec = pl.BlockSpec((CONV_WIDTH, D_MODEL), lambda j, i: (0, col0 + j))
    bspec = pl.BlockSpec((1, D_MODEL), lambda j, i: (0, col0 + j))
    wout = pl.BlockSpec((CONV_WIDTH, D_MODEL), lambda j, i: (0, j))
    bout = pl.BlockSpec((1, D_MODEL), lambda j, i: (0, j))

    def body(*refs):
        dc_refs, da_refs = refs[:np_], refs[np_:2 * np_]
        c_ref, p_ref, a_ref, w_ref, b_ref, du_ref, dw_ref, db_ref = refs[2 * np_:]
        i = pl.program_id(1)
        row = lax.broadcasted_iota(jnp.int32, (rows, D_MODEL), 0)
        curv = c_ref[...]
        beforev = p_ref[...] * jnp.where(i > 0, 1.0, 0.0)
        side_by_side = lambda rs: rs[0][...] if np_ == 1 else jnp.concatenate([r[...] for r in rs], axis=1)

        def dpre_of(pre, d):
            s = _sigmoid(pre)
            return d * (s * (1.0 + pre * (1.0 - s)))

        dp_c = dpre_of(_conv_pre(curv, beforev, w_ref, b_ref), side_by_side(dc_refs) * _valid_rows(i, rows))
        dp_a = dpre_of(_conv_pre(a_ref[...], curv[rows - HALO:], w_ref, b_ref),
                       side_by_side(da_refs) * jnp.where(i < last, 1.0, 0.0))
        du = w_ref[CONV_WIDTH - 1:CONV_WIDTH, :] * dp_c
        for j in range(1, CONV_WIDTH):
            tail = jnp.concatenate([jnp.zeros((rows - HALO, D_MODEL), F32), pltpu.roll(dp_a, HALO - j, 0)], axis=0)
            up = jnp.where(row < rows - j, pltpu.roll(dp_c, rows - j, 0), tail)
            du = du + w_ref[CONV_WIDTH - 1 - j:CONV_WIDTH - j, :] * up
        du_ref[...] = du.astype(BF16)

        @pl.when(i == 0)
        def _():
            dw_ref[...] = jnp.zeros_like(dw_ref)
            db_ref[...] = jnp.zeros_like(db_ref)

        for k in range(CONV_WIDTH):
            dw_ref[k:k + 1, :] += jnp.sum(dp_c * _shift_rows(curv, beforev, CONV_WIDTH - 1 - k), axis=0, keepdims=True)
        db_ref[...] += jnp.sum(dp_c, axis=0, keepdims=True)

    width = ncol * D_MODEL
    return _call(body, name=name, grid=(ncol, steps),
                 in_specs=dcur + dafter + [cur, before, after, wspec, bspec], out_specs=[out_cur, wout, bout],
                 out_shape=[jax.ShapeDtypeStruct((proj.shape[0], width), BF16),
                            jax.ShapeDtypeStruct((CONV_WIDTH, width), F32), jax.ShapeDtypeStruct((1, width), F32)],
                 sem=("parallel", "arbitrary"))(*dparts, *dparts, proj, proj, proj, conv_w, conv_b)


def _head_expand():
    e = np.zeros((LANES, SSM_INNER), np.float32)
    for h in range(SSM_HEADS):
        e[h, h * HEAD_DIM:(h + 1) * HEAD_DIM] = 1.0
    return jnp.asarray(e, dtype=BF16)


def _softplus(x):
    return jnp.maximum(x, 0.0) + jnp.log(1.0 + jnp.exp(-jnp.abs(x)))


def _bf16_parts(x):
    hi = x.astype(BF16)
    rest = x - hi.astype(F32)
    mid = rest.astype(BF16)
    return hi, mid, (rest - mid.astype(F32)).astype(BF16)


@jax.custom_vjp
def _times_01(x, m):
    return sum(jnp.dot(p, m, preferred_element_type=F32) for p in _bf16_parts(x))


def _times_01_bwd(m, g):
    return sum(lax.dot_general(p, m, _NT, preferred_element_type=F32) for p in _bf16_parts(g)), jnp.zeros_like(m)


_times_01.defvjp(lambda x, m: (_times_01(x, m), m), _times_01_bwd)


def _causal_ones():
    l = lax.broadcasted_iota(jnp.int32, (BLOCK, BLOCK), 0)
    s = lax.broadcasted_iota(jnp.int32, (BLOCK, BLOCK), 1)
    return (l >= s).astype(BF16)


@jax.custom_vjp
def _cumsum_rows(a):
    return sum(jnp.dot(_causal_ones(), p, preferred_element_type=F32) for p in _bf16_parts(a))


def _cumsum_rows_bwd(_, g):
    tn = (((0,), (0,)), ((), ()))
    return (sum(lax.dot_general(_causal_ones(), p, tn, preferred_element_type=F32) for p in _bf16_parts(g)),)


_cumsum_rows.defvjp(lambda a: (_cumsum_rows(a), None), _cumsum_rows_bwd)


def _ssd_heads(dt_tile, bias, alog, dsk, expand):
    dt = _softplus(dt_tile + bias)
    a = dt * (-jnp.exp(alog))
    one_row = lambda v: jnp.broadcast_to(v, (HALO, LANES))
    return _times_01(jnp.concatenate([dt, _cumsum_rows(a), one_row(jnp.sum(a, axis=0, keepdims=True)), one_row(dsk)],
                                     axis=0), expand)


HEADS_ROWS = 2 * BLOCK + 2 * HALO


def _ssd_group(xs, per_lane, bg, cg, state):
    l = lax.broadcasted_iota(jnp.int32, (BLOCK, BLOCK), 0)
    s = lax.broadcasted_iota(jnp.int32, (BLOCK, BLOCK), 1)
    causal = l >= s
    first_head = s < HEAD_DIM
    dtx, cs = per_lane[0:BLOCK], per_lane[BLOCK:2 * BLOCK]
    tot, dsk = per_lane[2 * BLOCK:2 * BLOCK + 1], per_lane[2 * BLOCK + HALO:2 * BLOCK + HALO + 1]
    bb, cb16 = bg.astype(BF16), cg.astype(BF16)
    cb = lax.dot_general(cb16, bb, _NT, preferred_element_type=F32)
    xr = xs * dtx
    y_diag = []
    for p in range(GROUP_W // LANES):
        lanes = slice(p * LANES, (p + 1) * LANES)
        c_pair = cs[:, lanes]
        c_swap = _swap_halves(c_pair)
        m = []
        for c_head in (jnp.where(first_head, c_pair, c_swap), jnp.where(first_head, c_swap, c_pair)):
            m.append(cb * jnp.exp(jnp.where(causal, c_head - c_head.T, -jnp.inf)))
        x_pair = xr[:, lanes]
        x_diag = jnp.concatenate([jnp.where(first_head, x_pair, 0.0), jnp.where(first_head, 0.0, x_pair)], axis=0)
        y_diag.append(jnp.dot(jnp.concatenate(m, axis=1).astype(BF16), x_diag.astype(BF16),
                              preferred_element_type=F32))
    st = lax.dot_general(bb, (xr * jnp.exp(tot - cs)).astype(BF16), (((0,), (0,)), ((), ())),
                         preferred_element_type=F32)
    new_state = state * jnp.exp(tot) + st
    y_off = jnp.dot(cb16, state.astype(BF16), preferred_element_type=F32) * jnp.exp(cs)
    return jnp.concatenate(y_diag, axis=1) + y_off + dsk * xs, new_state


BC_WIDTH = SSM_GROUPS * SSM_STATE


def _ssd_specs(chunk):
    xs = pl.BlockSpec((BLOCK, SSM_INNER), lambda c: (chunk(c), 0))
    dt = pl.BlockSpec((BLOCK, LANES), lambda c: (chunk(c), DT_TILE))
    expand = _full((LANES, SSM_INNER))
    b = pl.BlockSpec((BLOCK, BC_WIDTH), lambda c: (chunk(c), SSM_INNER // BC_WIDTH))
    cc = pl.BlockSpec((BLOCK, BC_WIDTH), lambda c: (chunk(c), SSM_INNER // BC_WIDTH + 1))
    par = _full((1, LANES))
    state = pl.BlockSpec((None, SSM_STATE, SSM_INNER), lambda c: (chunk(c), 0, 0))
    return xs, dt, expand, b, cc, par, state


def _group_lanes(g):
    return slice(g * GROUP_W, (g + 1) * GROUP_W), slice(g * SSM_STATE, (g + 1) * SSM_STATE)


def ssd_fwd(xbc, proj, expand, bias, alog, dsk):
    nb = xbc.shape[0] // BLOCK
    xs, dt, ex, b, cc, par, state = _ssd_specs(lambda c: c)

    def body(x_ref, dt_ref, e_ref, bi_ref, al_ref, dk_ref, b_ref, c_ref, y_ref, sp_ref, st_ref):
        @pl.when(pl.program_id(0) == 0)
        def _():
            st_ref[...] = jnp.zeros_like(st_ref)

        per_lane = _ssd_heads(dt_ref[...], bi_ref[...], al_ref[...], dk_ref[...], e_ref[...])
        for g in range(SSM_GROUPS):
            wide, tile = _group_lanes(g)
            entering = st_ref[:, wide]
            sp_ref[:, wide] = entering
            y_ref[:, wide], st_ref[:, wide] = _ssd_group(x_ref[:, wide], per_lane[:, wide], b_ref[:, tile],
                                                         c_ref[:, tile], entering)

    return _call(body, name="ssd_fwd", grid=(nb,), in_specs=[xs, dt, ex, par, par, par, b, cc],
                 out_specs=[xs, state],
                 out_shape=[jax.ShapeDtypeStruct((nb * BLOCK, SSM_INNER), F32),
                            jax.ShapeDtypeStruct((nb, SSM_STATE, SSM_INNER), F32)],
                 scratch=[pltpu.VMEM((SSM_STATE, SSM_INNER), F32)],
                 sem=("arbitrary",))(xbc, proj, expand, bias, alog, dsk, xbc, xbc)


def ssd_bwd(dy, xbc, proj, expand, bias, alog, dsk, states, comm):
    nb = xbc.shape[0] // BLOCK
    last = nb - 1
    xs, dt, ex, b, cc, par, state = _ssd_specs(lambda c: last - c)
    tile = pl.BlockSpec((BLOCK, LANES), lambda c: (last - c, 0))
    nspec = pl.BlockSpec((BLOCK, BC_WIDTH), lambda c: (last - c, 0))

    def body(dy_ref, x_ref, dt_ref, e_ref, bi_ref, al_ref, dk_ref, b_ref, c_ref, sp_ref,
             dx_ref, ddt_ref, db_ref, dc_ref, dbi_ref, dal_ref, ddk_ref, ds_ref):
        @pl.when(pl.program_id(0) == 0)
        def _():
            ds_ref[...] = jnp.zeros_like(ds_ref)
            dbi_ref[...] = jnp.zeros_like(dbi_ref)
            dal_ref[...] = jnp.zeros_like(dal_ref)
            ddk_ref[...] = jnp.zeros_like(ddk_ref)

        expand = e_ref[...]
        per_lane, heads_vjp = jax.vjp(lambda t, bi, al, dk: _ssd_heads(t, bi, al, dk, expand), dt_ref[...], bi_ref[...],
                                      al_ref[...], dk_ref[...])
        d_per_lane = []
        for g in range(SSM_GROUPS):
            wide, tile_lanes = _group_lanes(g)
            _, vjp = jax.vjp(_ssd_group, x_ref[:, wide], per_lane[:, wide], b_ref[:, tile_lanes], c_ref[:, tile_lanes],
                             sp_ref[:, wide])
            (dx_ref[:, wide], d_lanes, db_ref[:, tile_lanes], dc_ref[:, tile_lanes],
             ds_ref[:, wide]) = vjp((dy_ref[:, wide], ds_ref[:, wide]))
            d_per_lane.append(d_lanes)
        ddt, dbi, dal, ddk = heads_vjp(jnp.concatenate(d_per_lane, axis=1))
        ddt_ref[...] = ddt.astype(BF16)
        dbi_ref[...] += dbi
        dal_ref[...] += dal
        ddk_ref[...] += ddk

    def at():
        c = pl.program_id(0)
        return c == 0, c == 0, c == last

    par_shape = jax.ShapeDtypeStruct((1, LANES), F32)
    return _call_with_comm(
        body, comm, at, (dy, xbc, proj, expand, bias, alog, dsk, xbc, xbc, states), name="ssd_bwd",
        grid=(nb,), in_specs=[xs, xs, dt, ex, par, par, par, b, cc, state],
        out_specs=[xs, tile, nspec, nspec, par, par, par],
        out_shape=[jax.ShapeDtypeStruct((nb * BLOCK, SSM_INNER), F32), jax.ShapeDtypeStruct((nb * BLOCK, LANES), BF16),
                   jax.ShapeDtypeStruct((nb * BLOCK, BC_WIDTH), F32), jax.ShapeDtypeStruct((nb * BLOCK, BC_WIDTH), F32),
                   par_shape, par_shape, par_shape],
        scratch=[pltpu.VMEM((SSM_STATE, SSM_INNER), F32)])


SLAB_ROWS = 16
SLAB_META_ROW = 8
SLAB_META_SHAPE = (8, 2 * D_MODEL)
SLAB_LOSS_ROW = 7


def pack_small(dcw, dcb, dgpre, dgpost, dbias, dalog, ddsk, dsinks, dgn, dmeta, loss_tile):
    def body(cw, cb, gpre, gpost, dtb, al, dk, sk, gn, meta, loss, o_ref):
        o_ref[...] = jnp.zeros_like(o_ref)
        o_ref[SLAB_LOSS_ROW:SLAB_LOSS_ROW + 1, 0:LANES] = loss[0:1, :]
        o_ref[0:CONV_WIDTH, :] = cw[...]
        o_ref[4:5, :] = cb[...]
        o_ref[5:6, 0:1024] = gpre[...]
        o_ref[5:6, 1024:2048] = gpost[...]
        o_ref[5:6, 2048:2176] = dtb[...]
        o_ref[5:6, 2176:2304] = al[...]
        o_ref[5:6, 2304:2432] = dk[...]
        o_ref[5:6, 2432:2560] = sk[...]
        o_ref[6:7, 0:SSM_INNER] = gn[...]
        o_ref[SLAB_META_ROW:SLAB_ROWS, 0:SLAB_META_SHAPE[1]] = meta[...]

    args = (dcw, dcb, dgpre, dgpost, dbias, dalog, ddsk, dsinks, dgn, dmeta, loss_tile)
    return _call(body, name="pack_small", in_specs=[_full(a.shape) for a in args],
                 out_specs=_full((SLAB_ROWS, CONV_DIM)), out_shape=jax.ShapeDtypeStruct((SLAB_ROWS, CONV_DIM), F32))(*args)


def _lane_tile(v):
    return jnp.pad(v, ((0, 0), (0, LANES - v.shape[1])))


def kernel(x, meta_tokens, g_pre, w_in, conv_w, conv_b, dt_bias, a_log, d_skip, attn_sinks, g_ssm_norm, w_out_att, w_out_ssm, w_out, g_post, loss_target, m_meta_tokens, m_g_pre, m_w_in, m_conv_w, m_conv_b, m_dt_bias, m_a_log, m_d_skip, m_attn_sinks, m_g_ssm_norm, m_w_out_att, m_w_out_ssm, m_w_out, m_g_post, v_meta_tokens, v_g_pre, v_w_in, v_conv_w, v_conv_b, v_dt_bias, v_a_log, v_d_skip, v_attn_sinks, v_g_ssm_norm, v_w_out_att, v_w_out_ssm, v_w_out, v_g_post):
    chip = _chip_index()

    conv_w_rows = jnp.pad(conv_w[0], ((0, 2 * 8 - CONV_WIDTH), (0, 0)))
    w_in_t, m_w_in_t, v_w_in_t = w_in[0].T, m_w_in[0].T, v_w_in[0].T
    gathered_w_in, g_conv_w, g_meta = run_comm("gather_w_in",
                                               TwoLevelGather([pack_w_in(w_in_t), conv_w_rows, meta_tokens]))
    w_all_t = unpack_w_in(gathered_w_in)
    cw_full = g_conv_w[:, :CONV_WIDTH].transpose(1, 0, 2).reshape(CONV_WIDTH, CONV_DIM)
    meta_full = g_meta.transpose(1, 0, 2).reshape(N_META, D_MODEL)
    behind_w_in = 0.0 * g_meta[0, 0, 0]
    w_out_flight, w_out_started = chip_exchange_start(
        "gather_w_out_start", [(w[0] + behind_w_in).astype(BF16) for w in (w_out_att, w_out_ssm, w_out)], False)

    h, u = prep(x, meta_full, g_pre)
    proj = project("in_proj", u, w_all_t, w_out_started)

    sinks3 = attn_sinks.reshape(ATT_Q_HEADS, 1, 1)
    a_att = attn_fwd(proj, sinks3)

    xbc = conv_fwd(proj, cw_full, conv_b)
    expand = _head_expand()
    head_pars = (_lane_tile(dt_bias), _lane_tile(a_log), _lane_tile(d_skip))
    y_ssd, states = ssd_fwd(xbc, proj, expand, *head_pars)
    woa, wos, wo = [g.reshape(-1, D_MODEL) for g in chip_exchange_wait("gather_w_out_wait", w_out_flight, False, y_ssd)]

    (yn, merged, dout, dy_att, dy_ssm, da_att, dy_ssd, dz_ssm, dga, dgs, dres, loss_tile, dg_post, dgn) = tail(
        y_ssd, proj, a_att, x, loss_target, woa, wos, wo, g_ssm_norm, g_post)

    dwo = mm_tn("out_proj_dw", merged, dout)
    dwoa = mm_tn("att_out_dw", a_att, dy_att)
    dwos = mm_tn("ssm_out_dw", yn, dy_ssm)
    dq, dz_att, dk, dv, dkmeta, dvmeta, dsinks3 = attn_bwd(da_att, proj, sinks3)
    dk = dk.at[PAD_ROWS:BLOCK].add(dkmeta).astype(BF16)
    dv = dv.at[PAD_ROWS:BLOCK].add(dvmeta).astype(BF16)

    def pieces(g):
        return g.reshape(4, 2, g.shape[0] // 8, g.shape[1])

    def to_owner(g):
        return (lambda ref, dev: ref.at[_chip_of(dev), dev[2]], (g.shape[0] // 8, g.shape[1]))

    (dxs, ddt_tile, dbg, dcg, dbias, dalog, ddsk), sent_w_out = ssd_bwd(
        dy_ssd, xbc, proj, expand, *head_pars, states,
        DirectExchange([pieces(dwoa), pieces(dwos), pieces(dwo)], [to_owner(dwoa), to_owner(dwos), to_owner(dwo)],
                       ALL_MASKS, "dev", 8))
    dxs_raw, dcw_xs, dcb_xs = conv_bwd("conv_bwd_x", [dxs], 0, proj, cw_full, conv_b)
    dbc_raw, dcw_bc, dcb_bc = conv_bwd("conv_bwd_bc", [dbg, dcg], SSM_INNER // D_MODEL, proj, cw_full, conv_b)
    dcw = jnp.concatenate([dcw_xs, dcw_bc], axis=1)
    dcb = jnp.concatenate([dcb_xs, dcb_bc], axis=1)

    narrow = jnp.concatenate([dk, dv, ddt_tile, jnp.zeros((dk.shape[0], N_ACT - N_ALIGNED), BF16)], axis=1)
    dproj = [dz_ssm, dxs_raw, dbc_raw, dq, dz_att, dga, dgs, narrow]
    dw_all_t = weight_grad_t("in_proj_dw", dproj, u)

    half_rows = PACK_W // 2
    partial = pack_grad_w_in(dw_all_t).reshape(4, 2, half_rows, D_MODEL)
    from_sibling, = run_comm("pair_grads", DirectExchange(
        [partial], [(lambda ref, dev: ref.at[pl.ds(0, 4), dev[2]], (4, half_rows, D_MODEL))], SIBLING_MASK, "core", 2,
        keep_own=False))
    chip_sum = sum_pair(partial, from_sibling)
    grads_flight, started = chip_exchange_start("reduce_w_in_start", [chip_sum], True)
    du = project_back("in_proj_dx", dproj, w_all_t, started)
    grad_x, dmeta, dg_pre = prep_bwd(h, du, dres, g_pre)

    halves = [sum_slots("sum_" + nm, r) for nm, r in zip(("w_out_att", "w_out_ssm", "w_out"), sent_w_out)]
    shared = run_comm("share_w_out", DirectExchange(halves, [None] * 3, SIBLING_MASK, "core", 2))
    g_woa, g_wos, g_wo = [f.reshape(2 * f.shape[1], f.shape[2]) for f in shared]
    d_woa, nm_woa, nv_woa = adamw_rows("adamw_w_out_att", g_woa, w_out_att[0], m_w_out_att[0], v_w_out_att[0])
    d_wos, nm_wos, nv_wos = adamw_rows("adamw_w_out_ssm", g_wos, w_out_ssm[0], m_w_out_ssm[0], v_w_out_ssm[0])
    d_wo, nm_wo, nv_wo = adamw_rows("adamw_w_out", g_wo, w_out[0], m_w_out[0], v_w_out[0])

    sent_w_in, = chip_exchange_wait("reduce_w_in_wait", grads_flight, True, d_wos)
    slab = pack_small(dcw, dcb, dg_pre, dg_post, dbias, dalog, ddsk, _lane_tile(dsinks3.reshape(1, ATT_Q_HEADS)), dgn,
                      dmeta.reshape(SLAB_META_SHAPE), loss_tile)
    shared_w_in, slabs = run_comm("share_w_in", Both(
        DirectExchange([sum_slots("sum_w_in", sent_w_in)], [None], SIBLING_MASK, "core", 2),
        DirectExchange([slab], [None], ALL_MASKS, "dev", 8)))
    small = sum_slots("sum_small", slabs)
    loss = small[SLAB_LOSS_ROW, 0]
    g_w_in, d_w_in, nm_w_in, nv_w_in = [
        a.T for a in adamw_w_in(shared_w_in.reshape(PACK_W, D_MODEL), w_in_t, m_w_in_t, v_w_in_t)]

    cw_cols = CONV_DIM // 4
    meta_cols = D_MODEL // 4
    g_small = {
        "meta_tokens": lax.dynamic_slice(
            small[SLAB_META_ROW:SLAB_ROWS, 0:SLAB_META_SHAPE[1]].reshape(N_META, D_MODEL), (0, chip * meta_cols),
            (N_META, meta_cols)),
        "g_pre": small[5:6, 0:1024],
        "conv_w": lax.dynamic_slice(small, (0, chip * cw_cols), (CONV_WIDTH, cw_cols)),
        "conv_b": small[4:5, :],
        "dt_bias": small[5:6, 2048:2048 + SSM_HEADS],
        "a_log": small[5:6, 2176:2176 + SSM_HEADS],
        "d_skip": small[5:6, 2304:2304 + SSM_HEADS],
        "attn_sinks": small[5:6, 2432:2432 + ATT_Q_HEADS],
        "g_ssm_norm": small[6:7, 0:SSM_INNER],
        "g_post": small[5:6, 1024:2048],
    }
    names = list(g_small)
    w_small = dict(meta_tokens=meta_tokens, g_pre=g_pre, conv_w=conv_w[0], conv_b=conv_b, dt_bias=dt_bias, a_log=a_log,
                   d_skip=d_skip, attn_sinks=attn_sinks, g_ssm_norm=g_ssm_norm, g_post=g_post)
    m_small = dict(meta_tokens=m_meta_tokens, g_pre=m_g_pre, conv_w=m_conv_w[0], conv_b=m_conv_b, dt_bias=m_dt_bias,
                   a_log=m_a_log, d_skip=m_d_skip, attn_sinks=m_attn_sinks, g_ssm_norm=m_g_ssm_norm, g_post=m_g_post)
    v_small = dict(meta_tokens=v_meta_tokens, g_pre=v_g_pre, conv_w=v_conv_w[0], conv_b=v_conv_b, dt_bias=v_dt_bias,
                   a_log=v_a_log, d_skip=v_d_skip, attn_sinks=v_attn_sinks, g_ssm_norm=v_g_ssm_norm, g_post=v_g_post)
    upd = dict(zip(names, adamw_small([g_small[k] for k in names], [w_small[k] for k in names],
                                      [m_small[k] for k in names], [v_small[k] for k in names])))

    lead = {"conv_w"}

    def shaped(name, a):
        return a[None] if name in lead else a

    grads = dict(g_small, w_in=g_w_in, w_out_att=g_woa, w_out_ssm=g_wos, w_out=g_wo)
    deltas = dict({k: upd[k][0] for k in names}, w_in=d_w_in, w_out_att=d_woa, w_out_ssm=d_wos, w_out=d_wo)
    new_m = dict({k: upd[k][1] for k in names}, w_in=nm_w_in, w_out_att=nm_woa, w_out_ssm=nm_wos, w_out=nm_wo)
    new_v = dict({k: upd[k][2] for k in names}, w_in=nv_w_in, w_out_att=nv_woa, w_out_ssm=nv_wos, w_out=nv_wo)
    lead |= {"w_in", "w_out_att", "w_out_ssm", "w_out"}
    order = ["meta_tokens", "g_pre", "w_in", "conv_w", "conv_b", "dt_bias", "a_log", "d_skip", "attn_sinks",
             "g_ssm_norm", "w_out_att", "w_out_ssm", "w_out", "g_post"]
    outs = [loss, grad_x]
    for group in (grads, deltas, new_m, new_v):
        outs += [shaped(k, group[k]) for k in order]
    return tuple(outs)
```

```python
import numpy as np
import jax
import jax.numpy as jnp
from jax import lax
from jax.experimental import pallas as pl
from jax.experimental.pallas import tpu as pltpu

F32 = jnp.float32
BF16 = jnp.bfloat16

D_MODEL = 1024
N_META = 16
BLOCK = 128
PAD_ROWS = BLOCK - N_META
NORM_EPS = 1e-6
HEAD_DIM = 64
ATT_Q_HEADS = 16
ATT_KV_HEADS = 4
ATT_GROUP = 4
SSM_INNER = 2048
SSM_HEADS = 32
SSM_GROUPS = 4
SSM_STATE = 128
CONV_WIDTH = 4
CONV_DIM = 3072
LANES = 128

ADAM_LR = 0.001
ADAM_B1 = 0.9
ADAM_B2 = 0.999
ADAM_EPS = 1e-08
ADAM_WD = 0.01
ADAM_STEP = 10

VMEM_LIMIT = 48 * 1024 * 1024

SHARD_W = 2440
PACK_W = 2560
SHARD_STRIDE = 2432
N_ALIGNED = 9856
N_ACT = 10240
SEG = {
    "q": (0, 1024, 5120), "k": (1024, 256, 9216), "v": (1280, 256, 9472), "z_att": (1536, 1024, 6144),
    "z_ssm": (2560, 2048, 0), "xbc": (4608, 3072, 2048), "dt": (7680, 128, 9728),
    "gate_att": (7808, 1024, 7168), "gate_ssm": (8832, 1024, 8192),
}
DT_STORED_START = 7680
DT_PAD = LANES - SSM_HEADS


def _act_col(aligned_col):
    for a0, w, p0 in SEG.values():
        if a0 <= aligned_col < a0 + w:
            return p0 + aligned_col - a0
    raise ValueError(aligned_col)


def _call(body, *, name, out_shape, in_specs, out_specs, grid=(), scratch=(), sem=None, aliases=None):
    return pl.pallas_call(
        body, out_shape=out_shape, grid=grid, in_specs=in_specs, out_specs=out_specs, scratch_shapes=list(scratch),
        name=name, input_output_aliases=aliases or {},
        compiler_params=pltpu.CompilerParams(dimension_semantics=sem, vmem_limit_bytes=VMEM_LIMIT))


def _full(shape):
    n = len(shape)
    return pl.BlockSpec(shape, lambda *_: (0,) * n)


def _chip_index():
    return lax.axis_index("x") * 2 + lax.axis_index("y")


_sigmoid = jax.nn.sigmoid


def _silu(z):
    return z * _sigmoid(z)


def _rms(x, g):
    return x * lax.rsqrt(jnp.mean(x * x, axis=-1, keepdims=True) + NORM_EPS) * g


def _peer(mask):
    x, y, c = lax.axis_index("x"), lax.axis_index("y"), lax.axis_index("c")
    return ((1 - x) if mask & 4 else x, (1 - y) if mask & 2 else y, (1 - c) if mask & 1 else c)


def _me():
    return lax.axis_index("x"), lax.axis_index("y"), lax.axis_index("c")


def _chip_of(dev):
    return 2 * dev[0] + dev[1]


CHIP_MASKS = (4, 2, 6)
ALL_MASKS = (1, 2, 3, 4, 5, 6, 7)
SIBLING_MASK = (1,)


def _remote(src, dst, send_sem, recv_sem, dev):
    return pltpu.make_async_remote_copy(src_ref=src, dst_ref=dst, send_sem=send_sem, recv_sem=recv_sem,
                                        device_id=dev, device_id_type=pl.DeviceIdType.MESH)


class _StagedCopy:
    def __init__(self, src, stage, dst, load_sem, store_sem):
        self.load = pltpu.make_async_copy(src, stage, load_sem)
        self.store = pltpu.make_async_copy(stage, dst, store_sem)

    def start(self):
        self.load.start()
        self.load.wait()
        self.store.start()

    def wait(self):
        self.store.wait()


class DirectExchange:
    def __init__(self, arrays, pieces, masks, slot_kind, nslots, keep_own=True):
        self.arrays, self.pieces, self.masks, self.slot_kind = list(arrays), list(pieces), masks, slot_kind
        self.keep_own = keep_own
        n, nk = len(arrays), len(masks)
        shapes = [a.shape if p is None else p[1] for a, p in zip(arrays, pieces)]
        self.out_shape = [jax.ShapeDtypeStruct((nslots,) + tuple(s), a.dtype) for s, a in zip(shapes, arrays)]
        self.scratch = [pltpu.SemaphoreType.DMA((n * nk,)), pltpu.SemaphoreType.DMA((n * nk,))]
        if keep_own:
            self.scratch += [pltpu.SemaphoreType.DMA((2 * n,))] + [pltpu.VMEM(s, a.dtype) for s, a in zip(shapes, arrays)]
        self.has_mid = False

    def _copies(self, ins, outs, scratch):
        send_sems, recv_sems = scratch[:2]
        me = _me()
        slot = {"chip": _chip_of(me), "dev": 4 * me[0] + 2 * me[1] + me[2], "core": me[2]}[self.slot_kind]
        nk = len(self.masks)

        def piece(a, dev):
            return ins[a] if self.pieces[a] is None else self.pieces[a][0](ins[a], dev)

        local = []
        if self.keep_own:
            local_sems, stages = scratch[2], scratch[3:]
            local = [_StagedCopy(piece(a, me), stages[a], outs[a].at[slot], local_sems.at[2 * a], local_sems.at[2 * a + 1])
                     for a in range(len(ins))]
        remote = []
        for a in range(len(ins)):
            for ki, mask in enumerate(self.masks):
                dev = _peer(mask)
                remote.append(_remote(piece(a, dev), outs[a].at[slot], send_sems.at[a * nk + ki],
                                      recv_sems.at[a * nk + ki], dev))
        return local, remote

    def start(self, ins, outs, scratch):
        local, remote = self._copies(ins, outs, scratch)
        for cp in remote + local:
            cp.start()

    def finish(self, ins, outs, scratch):
        local, remote = self._copies(ins, outs, scratch)
        for cp in remote + local:
            cp.wait()


SPLIT_ROWS = 16


class TwoLevelGather:
    def __init__(self, arrays):
        self.arrays = list(arrays)
        n = len(arrays)
        self.out_shape = [jax.ShapeDtypeStruct((4,) + a.shape, a.dtype) for a in arrays]
        self.scratch = ([pltpu.SemaphoreType.DMA((4 * n,)), pltpu.SemaphoreType.DMA((4 * n,)),
                         pltpu.SemaphoreType.DMA((3 * n,)), pltpu.SemaphoreType.DMA((3 * n,)),
                         pltpu.SemaphoreType.DMA((2 * n,))] + [pltpu.VMEM(a.shape, a.dtype) for a in arrays])
        self.has_mid = True

    def _copies(self, ins, outs, scratch):
        ici_send, ici_recv, fwd_send, fwd_recv, local_sems = scratch[:5]
        stages = scratch[5:]
        me = _me()
        sibling, in_x, in_y, diagonal = _peer(1), _peer(4), _peer(2), _peer(6)
        plan = []
        for a in range(len(ins)):
            half = ins[a].shape[0] // 2
            first = half // 2 if half % (2 * SPLIT_ROWS) == 0 else half
            mine = pl.ds(me[2] * half, half)
            local = _StagedCopy(ins[a], stages[a], outs[a].at[_chip_of(me)], local_sems.at[2 * a], local_sems.at[2 * a + 1])

            def ici(k, src, dst, dev):
                return _remote(src, dst, ici_send.at[4 * a + k], ici_recv.at[4 * a + k], dev)

            def d2d(k, chip):
                zone = outs[a].at[_chip_of(chip), mine]
                return _remote(zone, zone, fwd_send.at[3 * a + k], fwd_recv.at[3 * a + k], sibling)

            own_zone = outs[a].at[_chip_of(me), mine]
            from_x = outs[a].at[_chip_of(in_x), pl.ds(me[2] * half, first)]
            onward = [ici(2, from_x, from_x, in_y), None]
            if first < half:
                from_y = outs[a].at[_chip_of(in_y), pl.ds(me[2] * half + first, half - first)]
                onward[1] = ici(3, from_y, from_y, in_x)
            plan.append(dict(
                local=local,
                own=[ici(0, ins[a].at[mine], own_zone, in_x), ici(1, ins[a].at[mine], own_zone, in_y)],
                onward=onward, sibling=[d2d(0, in_x), d2d(1, in_y), d2d(2, diagonal)]))
        return plan

    def start(self, ins, outs, scratch):
        for p in self._copies(ins, outs, scratch):
            for cp in p["own"]:
                cp.start()
            p["local"].start()

    def mid(self, ins, outs, scratch):
        plan = self._copies(ins, outs, scratch)
        for p in plan:
            for k in range(2):
                p["own"][k].wait_recv()
                if p["onward"][k] is not None:
                    p["onward"][k].start()
                p["sibling"][k].start()
        for p in plan:
            for cp in p["onward"]:
                if cp is not None:
                    cp.wait_recv()
            p["sibling"][2].start()

    def finish(self, ins, outs, scratch):
        for p in self._copies(ins, outs, scratch):
            for cp in p["sibling"]:
                cp.wait_recv()
            for cp in p["own"] + p["sibling"] + [cp for cp in p["onward"] if cp is not None]:
                cp.wait_send()
            p["local"].wait()


class Both:
    def __init__(self, a, b):
        self.a, self.b = a, b
        self.arrays, self.out_shape = a.arrays + b.arrays, a.out_shape + b.out_shape
        self.scratch = a.scratch + b.scratch
        self.has_mid = False
        assert not (a.has_mid or b.has_mid)

    def _parts(self, ins, outs, sems):
        na, sa = len(self.a.arrays), len(self.a.scratch)
        return (ins[:na], outs[:na], sems[:sa]), (ins[na:], outs[na:], sems[sa:])

    def start(self, ins, outs, sems):
        pa, pb = self._parts(ins, outs, sems)
        self.a.start(*pa)
        self.b.start(*pb)

    def finish(self, ins, outs, sems):
        pa, pb = self._parts(ins, outs, sems)
        self.a.finish(*pa)
        self.b.finish(*pb)


_ANY = pl.BlockSpec(memory_space=pl.ANY)


def run_comm(name, comm):
    n = len(comm.arrays)

    def body(*refs):
        ins, outs, sems = refs[:n], refs[n:2 * n], refs[2 * n:]
        comm.start(ins, outs, sems)
        if comm.has_mid:
            comm.mid(ins, outs, sems)
        comm.finish(ins, outs, sems)

    return pl.pallas_call(body, name=name, out_shape=comm.out_shape, in_specs=[_ANY] * n, out_specs=[_ANY] * n,
                          scratch_shapes=comm.scratch,
                          compiler_params=pltpu.CompilerParams(vmem_limit_bytes=VMEM_LIMIT))(*comm.arrays)


_HBM = pl.BlockSpec(memory_space=pltpu.HBM)
_SEM = pl.BlockSpec(memory_space=pltpu.SEMAPHORE)
_SIDE_EFFECT = pltpu.SideEffectType.DATAFLOW_SIDE_EFFECTING


def _chip_copies(srcs, lands, send_sems, recv_sems, by_target):
    me = _me()
    copies = []
    for a, (src, land) in enumerate(zip(srcs, lands)):
        for ki, mask in enumerate(CHIP_MASKS):
            dev = _peer(mask)
            k = a * len(CHIP_MASKS) + ki
            piece = src.at[_chip_of(dev)] if by_target else src
            copies.append(_remote(piece, land.at[_chip_of(me)], send_sems.at[k], recv_sems.at[k], dev))
    return copies


def chip_exchange_start(name, arrays, by_target):
    n = len(arrays)
    nsem = n * len(CHIP_MASKS)
    piece_shapes = [a.shape[1:] if by_target else a.shape for a in arrays]

    def body(*refs):
        srcs, lands = refs[:n], refs[n:2 * n]
        send_sems, recv_sems = refs[2 * n:2 * n + 2]
        token = refs[4 * n + 2]
        stages, local_sems = refs[4 * n + 3:5 * n + 3], refs[5 * n + 3]
        me = _me()
        for cp in _chip_copies(srcs, lands, send_sems, recv_sems, by_target):
            cp.start()
        for a in range(n):
            own = _StagedCopy(srcs[a].at[_chip_of(me)] if by_target else srcs[a], stages[a], lands[a].at[_chip_of(me)],
                              local_sems.at[2 * a], local_sems.at[2 * a + 1])
            own.start()
            own.wait()
        token[...] = jnp.zeros_like(token)

    lands = [lax.empty((4,) + tuple(s), a.dtype) for s, a in zip(piece_shapes, arrays)]
    hbm = lambda a: pltpu.HBM(a.shape, a.dtype)
    res = pl.pallas_call(
        body, name=name,
        out_shape=(pltpu.SemaphoreType.DMA((nsem,)), pltpu.SemaphoreType.DMA((nsem,)), *[hbm(a) for a in arrays],
                   *[hbm(l) for l in lands], jax.ShapeDtypeStruct((8, LANES), F32)),
        in_specs=(_HBM,) * (2 * n), out_specs=(_SEM, _SEM) + (_HBM,) * (2 * n) + (pl.BlockSpec(memory_space=pltpu.VMEM),),
        input_output_aliases={i: i + 2 for i in range(2 * n)},
        scratch_shapes=[pltpu.VMEM(tuple(s), a.dtype) for s, a in zip(piece_shapes, arrays)]
        + [pltpu.SemaphoreType.DMA((2 * n,))],
        compiler_params=pltpu.CompilerParams(has_side_effects=_SIDE_EFFECT, vmem_limit_bytes=VMEM_LIMIT),
    )(*[pltpu.with_memory_space_constraint(a, pltpu.HBM) for a in arrays + lands])
    return res[:-1], res[-1]


def chip_exchange_wait(name, in_flight, by_target, after):
    send_sems, recv_sems, *thru = in_flight
    n = len(thru) // 2

    def body(*refs):
        srcs, lands, (send, recv) = refs[:n], refs[n:2 * n], refs[2 * n:2 * n + 2]
        for cp in _chip_copies(srcs, lands, send, recv, by_target):
            cp.wait_send()
            cp.wait_recv()

    return pl.pallas_call(
        body, name=name, out_shape=tuple(pltpu.HBM(t.shape, t.dtype) for t in thru),
        in_specs=(_HBM,) * (2 * n) + (_SEM, _SEM, pl.BlockSpec(memory_space=pl.ANY)), out_specs=(_HBM,) * (2 * n),
        input_output_aliases={i: i for i in range(2 * n)},
        compiler_params=pltpu.CompilerParams(has_side_effects=_SIDE_EFFECT),
    )(*thru, send_sems, recv_sems, after)[n:]


def _call_with_comm(body, comm, steps, args, *, name, out_shape, in_specs, out_specs, grid, scratch=()):
    ni, no, ns, nc = len(in_specs), len(out_specs), len(scratch), len(comm.arrays)

    def full_body(*refs):
        ins, cins = refs[:ni], refs[ni:ni + nc]
        outs, couts = refs[ni + nc:ni + nc + no], refs[ni + nc + no:ni + 2 * nc + no]
        scr, csems = refs[ni + 2 * nc + no:ni + 2 * nc + no + ns], refs[ni + 2 * nc + no + ns:]
        first, middle, last = steps()
        pl.when(first)(lambda: comm.start(cins, couts, csems))
        if comm.has_mid:
            pl.when(middle)(lambda: comm.mid(cins, couts, csems))
        body(*ins, *outs, *scr)
        pl.when(last)(lambda: comm.finish(cins, couts, csems))

    res = pl.pallas_call(
        full_body, name=name, out_shape=list(out_shape) + comm.out_shape, grid=grid,
        in_specs=list(in_specs) + [_ANY] * nc, out_specs=list(out_specs) + [_ANY] * nc,
        scratch_shapes=list(scratch) + comm.scratch,
        compiler_params=pltpu.CompilerParams(dimension_semantics=("arbitrary",) * len(grid),
                                             vmem_limit_bytes=VMEM_LIMIT))(*args, *comm.arrays)
    return res[:no], res[no:]


def _shard_pieces(chip):
    if chip < 3:
        return [(0, SHARD_W, 8 * chip)]
    behind_dt = DT_STORED_START + SSM_HEADS - 3 * SHARD_W
    return [(0, behind_dt, 24), (behind_dt, SHARD_W - behind_dt, behind_dt + 24 + DT_PAD)]


W_IN_COLS = 256


def pack_w_in(wt):
    def body(w_ref, o_ref, pad_ref):
        chip = _chip_index()
        pad_ref[...] = jnp.zeros_like(pad_ref)
        for cv in range(4):
            @pl.when(chip == cv)
            def _():
                for src, n, dst in _shard_pieces(cv):
                    pad_ref[dst:dst + n, :] = w_ref[src:src + n, :]
        o_ref[...] = pad_ref[...].astype(BF16)

    return _call(body, name="pack_w_in", grid=(D_MODEL // W_IN_COLS,),
                 in_specs=[pl.BlockSpec((SHARD_W, W_IN_COLS), lambda i: (0, i))],
                 out_specs=pl.BlockSpec((PACK_W, W_IN_COLS), lambda i: (0, i)),
                 out_shape=jax.ShapeDtypeStruct((PACK_W, D_MODEL), BF16),
                 scratch=[pltpu.VMEM((PACK_W, W_IN_COLS), F32)], sem=("parallel",))(wt)


def _tile_runs():
    runs, fix = [], []
    for t in range(N_ALIGNED // LANES):
        s = min(t // 19, 3)
        j = t - 19 * s
        p = _act_col(t * LANES)
        if runs and runs[-1][1] == s and runs[-1][0] + runs[-1][3] == p and runs[-1][2] + runs[-1][3] == j * LANES:
            runs[-1][3] += LANES
        else:
            runs.append([p, s, j * LANES, LANES])
        if j == 0 and s > 0:
            fix.append((p, s - 1))
    return runs, fix


def unpack_w_in(bg):
    runs, fix = _tile_runs()

    def body(b_ref, o_ref):
        for p, s, j, w in runs:
            o_ref[p:p + w, :] = b_ref[s, j:j + w, :]
        for p, s in fix:
            o_ref[p:p + LANES, :] = o_ref[p:p + LANES, :] + b_ref[s, SHARD_STRIDE:PACK_W, :]
        o_ref[N_ALIGNED:N_ACT, :] = jnp.zeros((N_ACT - N_ALIGNED, W_IN_COLS), BF16)

    return _call(body, name="unpack_w_in", grid=(D_MODEL // W_IN_COLS,),
                 in_specs=[pl.BlockSpec((4, PACK_W, W_IN_COLS), lambda i: (0, 0, i))],
                 out_specs=pl.BlockSpec((N_ACT, W_IN_COLS), lambda i: (0, i)),
                 out_shape=jax.ShapeDtypeStruct((N_ACT, D_MODEL), BF16), sem=("parallel",))(bg)


def _adamw(w, g, m, v):
    m = ADAM_B1 * m + (1.0 - ADAM_B1) * g
    v = ADAM_B2 * v + (1.0 - ADAM_B2) * jnp.square(g)
    m_hat = m / (1.0 - ADAM_B1 ** ADAM_STEP)
    v_hat = v / (1.0 - ADAM_B2 ** ADAM_STEP)
    delta = -ADAM_LR * (m_hat / (jnp.sqrt(v_hat) + ADAM_EPS) + ADAM_WD * w)
    return delta, m, v


def adamw_w_in(g_packed, wt, mt, vt):
    cols = LANES

    def body(g_ref, w_ref, m_ref, v_ref, go_ref, d_ref, mo_ref, vo_ref):
        chip = _chip_index()
        for cv in range(4):
            @pl.when(chip == cv)
            def _():
                for dst, n, src in _shard_pieces(cv):
                    go_ref[dst:dst + n, :] = g_ref[src:src + n, :]
        d_ref[...], mo_ref[...], vo_ref[...] = _adamw(w_ref[...], go_ref[...], m_ref[...], v_ref[...])

    spec = pl.BlockSpec((SHARD_W, cols), lambda i: (0, i))
    shp = jax.ShapeDtypeStruct((SHARD_W, D_MODEL), F32)
    return _call(body, name="adamw_w_in", grid=(D_MODEL // cols,),
                 in_specs=[pl.BlockSpec((PACK_W, cols), lambda i: (0, i)), spec, spec, spec],
                 out_specs=[spec] * 4, out_shape=[shp] * 4, sem=("parallel",))(g_packed, wt, mt, vt)


def adamw_rows(name, g, w, m, v):
    r, c = g.shape
    rows = min(r, BLOCK)

    def body(g_ref, w_ref, m_ref, v_ref, d_ref, mo_ref, vo_ref):
        d_ref[...], mo_ref[...], vo_ref[...] = _adamw(w_ref[...], g_ref[...], m_ref[...], v_ref[...])

    spec = pl.BlockSpec((rows, c), lambda i: (i, 0))
    shp = jax.ShapeDtypeStruct((r, c), F32)
    return _call(body, name=name, grid=(r // rows,), in_specs=[spec] * 4, out_specs=[spec] * 3, out_shape=[shp] * 3,
                 sem=("parallel",))(g, w, m, v)


def adamw_small(gs, ws, ms, vs):
    n = len(gs)

    def body(*refs):
        g, w, m, v = refs[:n], refs[n:2 * n], refs[2 * n:3 * n], refs[3 * n:4 * n]
        outs = refs[4 * n:]
        for i in range(n):
            d, mn, vn = _adamw(w[i][...], g[i][...], m[i][...], v[i][...])
            outs[3 * i][...] = d
            outs[3 * i + 1][...] = mn
            outs[3 * i + 2][...] = vn

    specs = [_full(a.shape) for a in gs]
    res = _call(body, name="adamw_small", in_specs=specs * 4,
                out_specs=[s for s in specs for _ in range(3)],
                out_shape=[jax.ShapeDtypeStruct(a.shape, F32) for a in gs for _ in range(3)])(*gs, *ws, *ms, *vs)
    return [tuple(res[3 * i:3 * i + 3]) for i in range(n)]


def sum_slots(name, r):
    s, rr, c = r.shape
    rows = min(rr, BLOCK)

    def body(r_ref, o_ref):
        acc = r_ref[0].astype(F32)
        for k in range(1, s):
            acc = acc + r_ref[k].astype(F32)
        o_ref[...] = acc

    return _call(body, name=name, grid=(rr // rows,), in_specs=[pl.BlockSpec((s, rows, c), lambda i: (0, i, 0))],
                 out_specs=pl.BlockSpec((rows, c), lambda i: (i, 0)), out_shape=jax.ShapeDtypeStruct((rr, c), F32),
                 sem=("parallel",))(r)


def sum_pair(partial, from_sibling):
    s, _, rr, cols = partial.shape
    rows = rr // 2

    def body(p_ref, r_ref, o_ref):
        c = lax.axis_index("c")
        o_ref[...] = (p_ref[c].astype(F32) + r_ref[1 - c].astype(F32)).astype(BF16)

    return _call(body, name="sum_pair", grid=(s, rr // rows),
                 in_specs=[pl.BlockSpec((None, 2, rows, cols), lambda k, i: (k, 0, i, 0)),
                           pl.BlockSpec((2, None, rows, cols), lambda k, i: (0, k, i, 0))],
                 out_specs=pl.BlockSpec((None, rows, cols), lambda k, i: (k, i, 0)),
                 out_shape=jax.ShapeDtypeStruct((s, rr, cols), BF16), sem=("parallel", "parallel"))(partial, from_sibling)


def _col_tile(n, k):
    if n % 896 == 0 and k <= 1024:
        return 896
    return min(n, 512)


def project(name, x, wt, after):
    m, k = x.shape
    n = wt.shape[0]
    tn = D_MODEL

    def body(x_ref, w_ref, after_ref, o_ref):
        o_ref[...] = lax.dot_general(x_ref[...], w_ref[...], _NT, preferred_element_type=F32)

    return _call(body, name=name, grid=(n // tn,),
                 in_specs=[_full((m, k)), pl.BlockSpec((tn, k), lambda j: (j, 0)), _full(after.shape)],
                 out_specs=pl.BlockSpec((m, tn), lambda j: (0, j)), out_shape=jax.ShapeDtypeStruct((m, n), F32),
                 sem=("parallel",))(x, wt, after)


def _piece_tiles(pieces):
    spans, start = [], 0
    for p in pieces:
        spans.append((start, p.shape[1] // D_MODEL))
        start += p.shape[1] // D_MODEL
    return spans, start


def _piece_spec(tm, span, rows_of, tile_of):
    first, count = span

    def index(i, j):
        t = tile_of(i, j) - first
        mine = (t >= 0) & (t < count)
        return jnp.where(mine, rows_of(i, j), 0), jnp.clip(t, 0, count - 1)

    return pl.BlockSpec((tm, D_MODEL), index)


def project_back(name, pieces, wt, after):
    m = pieces[0].shape[0]
    k = wt.shape[1]
    tm = m // 2
    spans, steps = _piece_tiles(pieces)

    def body(*refs):
        w_ref, o_ref = refs[len(pieces)], refs[len(pieces) + 2]
        j = pl.program_id(1)

        @pl.when(j == 0)
        def _():
            o_ref[...] = jnp.zeros_like(o_ref)

        for dy_ref, (first, count) in zip(refs, spans):
            @pl.when((j >= first) & (j < first + count))
            def _():
                o_ref[...] += jnp.dot(dy_ref[...], w_ref[...], preferred_element_type=F32)

    return _call(body, name=name, grid=(m // tm, steps),
                 in_specs=[_piece_spec(tm, s, lambda i, j: i, lambda i, j: j) for s in spans]
                 + [pl.BlockSpec((D_MODEL, k), lambda i, j: (j, 0)), _full(after.shape)],
                 out_specs=pl.BlockSpec((tm, k), lambda i, j: (i, 0)), out_shape=jax.ShapeDtypeStruct((m, k), F32),
                 sem=("parallel", "arbitrary"))(*pieces, wt, after)


def _slab_runs():
    runs = [[] for _ in range(N_ACT // D_MODEL)]
    for s in range(4):
        for j in range(PACK_W // LANES):
            tile, r = divmod(_act_col((19 * s + j) * LANES), D_MODEL)
            last = runs[tile][-1] if runs[tile] else None
            if last and last[2] == s and last[0] + last[1] == r and last[3] + last[1] == j * LANES:
                last[1] += LANES
            else:
                runs[tile].append([r, LANES, s, j * LANES])
    return runs


def weight_grad_t(name, pieces, x):
    m = pieces[0].shape[0]
    k = x.shape[1]
    tm = m // 2
    spans, steps = _piece_tiles(pieces)
    runs = _slab_runs()
    most = max(len(r) for r in runs)

    def body(*refs):
        x_ref, o_ref, acc_ref, tile_ref, sems = refs[len(pieces):]
        i, half = pl.program_id(0), pl.program_id(1)

        def copies(t):
            return [pltpu.make_async_copy(tile_ref.at[t % 2, r:r + n], o_ref.at[s, d:d + n], sems.at[(t % 2) * most + c])
                    for c, (r, n, s, d) in enumerate(runs[t])]

        for dy_ref, (first, count) in zip(refs, spans):
            for t in range(first, first + count):
                @pl.when(i == t)
                def _():
                    part = lax.dot_general(dy_ref[...], x_ref[...], (((0,), (0,)), ((), ())),
                                           preferred_element_type=F32)

                    @pl.when(half == 0)
                    def _():
                        acc_ref[...] = part

                    @pl.when(half == 1)
                    def _():
                        if t >= 2:
                            for cp in copies(t - 2):
                                cp.wait()
                        tile_ref[t % 2] = (acc_ref[...] + part).astype(BF16)
                        for cp in copies(t):
                            cp.start()
                        if t == steps - 1:
                            for cp in copies(t - 1) + copies(t):
                                cp.wait()

    return _call(body, name=name, grid=(steps, 2),
                 in_specs=[_piece_spec(tm, s, lambda i, j: j, lambda i, j: i) for s in spans]
                 + [pl.BlockSpec((tm, k), lambda i, j: (j, 0))],
                 out_specs=_ANY, out_shape=jax.ShapeDtypeStruct((4, PACK_W, k), BF16),
                 scratch=[pltpu.VMEM((D_MODEL, k), F32), pltpu.VMEM((2, D_MODEL, k), BF16),
                          pltpu.SemaphoreType.DMA((2 * most,))],
                 sem=("arbitrary", "arbitrary"))(*pieces, x)


def mm_tn(name, x, dy):
    m, k = x.shape
    n = dy.shape[1]
    tm = m // 2
    tn = _col_tile(n, k)

    def body(x_ref, dy_ref, o_ref, acc_ref):
        part = lax.dot_general(x_ref[...].astype(BF16), dy_ref[...].astype(BF16), (((0,), (0,)), ((), ())),
                               preferred_element_type=F32)

        @pl.when(pl.program_id(1) == 0)
        def _():
            acc_ref[...] = part

        @pl.when(pl.program_id(1) == 1)
        def _():
            o_ref[...] = (acc_ref[...] + part).astype(BF16)

    return _call(body, name=name, grid=(n // tn, 2),
                 in_specs=[pl.BlockSpec((tm, k), lambda i, j: (j, 0)), pl.BlockSpec((tm, tn), lambda i, j: (j, i))],
                 out_specs=pl.BlockSpec((k, tn), lambda i, j: (0, i)), out_shape=jax.ShapeDtypeStruct((k, n), BF16),
                 scratch=[pltpu.VMEM((k, tn), F32)], sem=("parallel", "arbitrary"))(x, dy)


def _row_spec(width, col_block=0):
    return pl.BlockSpec((BLOCK, width), lambda i: (i, col_block))


def _x_spec():
    return pl.BlockSpec((None, BLOCK, D_MODEL), lambda i: (0, jnp.maximum(i - 1, 0), 0))


def prep(x, meta, g_pre):
    nb = x.shape[1] // BLOCK + 1

    def body(x_ref, meta_ref, g_ref, h_ref, u_ref):
        i = pl.program_id(0)

        @pl.when(i == 0)
        def _():
            h_ref[0:PAD_ROWS, :] = jnp.zeros((PAD_ROWS, D_MODEL), F32)
            h_ref[PAD_ROWS:BLOCK, :] = meta_ref[...]

        @pl.when(i > 0)
        def _():
            h_ref[...] = x_ref[...]

        u_ref[...] = _rms(h_ref[...], g_ref[...]).astype(BF16)

    return _call(body, name="prep", grid=(nb,), in_specs=[_x_spec(), _full((N_META, D_MODEL)), _full((1, D_MODEL))],
                 out_specs=[_row_spec(D_MODEL), _row_spec(D_MODEL)],
                 out_shape=[jax.ShapeDtypeStruct((nb * BLOCK, D_MODEL), F32),
                            jax.ShapeDtypeStruct((nb * BLOCK, D_MODEL), BF16)], sem=("parallel",))(x, meta, g_pre)


def prep_bwd(h, du, dres, g_pre):
    nb = h.shape[0] // BLOCK

    def body(h_ref, du_ref, dres_ref, g_ref, gx_ref, gm_ref, gg_ref):
        i = pl.program_id(0)
        _, vjp = jax.vjp(_rms, h_ref[...], g_ref[...])
        dh, dg = vjp(du_ref[...])

        @pl.when(i == 0)
        def _():
            gm_ref[...] = dh[PAD_ROWS:BLOCK, :]
            gg_ref[...] = dg

        @pl.when(i > 0)
        def _():
            gg_ref[...] += dg

        gx_ref[...] = dh + dres_ref[...]

    return _call(body, name="prep_bwd", grid=(nb,),
                 in_specs=[_row_spec(D_MODEL), _row_spec(D_MODEL), _row_spec(D_MODEL), _full((1, D_MODEL))],
                 out_specs=[_x_spec(), _full((N_META, D_MODEL)), _full((1, D_MODEL))],
                 out_shape=[jax.ShapeDtypeStruct((1, (nb - 1) * BLOCK, D_MODEL), F32),
                            jax.ShapeDtypeStruct((N_META, D_MODEL), F32), jax.ShapeDtypeStruct((1, D_MODEL), F32)],
                 sem=("arbitrary",))(h, du, dres, g_pre)


GROUP_W = SSM_INNER // SSM_GROUPS


def _gated_norm(y, z, g):
    t = y * _silu(z)
    return t * lax.rsqrt(jnp.mean(t * t, axis=-1, keepdims=True) + NORM_EPS) * g


def _gated_norm_groups(y, z, g):
    groups = [slice(k * GROUP_W, (k + 1) * GROUP_W) for k in range(SSM_GROUPS)]
    return jnp.concatenate([_gated_norm(y[:, s], z[:, s], g[:, s]) for s in groups], axis=1)


def _merge(ga, gs, ya, ys):
    return _sigmoid(ga) * ya + _sigmoid(gs) * ys


GATE_ATT_BLOCK = SEG["gate_att"][2] // D_MODEL
GATE_SSM_BLOCK = SEG["gate_ssm"][2] // D_MODEL


def _row_loss(out, g_post, x, target):
    diff = x + _rms(out, g_post) - target
    return 0.5 * jnp.sum(diff * diff) / D_MODEL


def tail(y_ssd, proj, a_att, x, target, woa, wos, wo, g_norm, g_post):
    nb = y_ssd.shape[0] // BLOCK
    rows = nb * BLOCK

    def body(y_ref, z_ref, ga_ref, gs_ref, a_ref, x_ref, t_ref, woa_ref, wos_ref, wo_ref, gn_ref, gp_ref,
             yn_ref, mg_ref, dout_ref, dya_ref, dys_ref, da_ref, dy_ref, dz_ref, dga_ref, dgs_ref, dres_ref,
             loss_ref, dgp_ref, dgn_ref):
        i = pl.program_id(0)
        yn, norm_vjp = jax.vjp(_gated_norm_groups, y_ref[...], z_ref[...], gn_ref[...])
        yn16 = yn.astype(BF16)
        y_ssm = jnp.dot(yn16, wos_ref[...], preferred_element_type=F32)
        y_att = jnp.dot(a_ref[...], woa_ref[...], preferred_element_type=F32)
        merged, merge_vjp = jax.vjp(_merge, ga_ref[...], gs_ref[...], y_att, y_ssm)
        merged16 = merged.astype(BF16)
        out = jnp.dot(merged16, wo_ref[...], preferred_element_type=F32)
        loss, loss_vjp = jax.vjp(_row_loss, out, gp_ref[...], x_ref[...], t_ref[...])
        counted = jnp.where(i > 0, 1.0, 0.0)
        dout, dgp, dres, _ = loss_vjp(counted)
        dout16 = dout.astype(BF16)
        dmerged = lax.dot_general(dout16, wo_ref[...], _NT, preferred_element_type=F32)
        dga, dgs, dya, dys = merge_vjp(dmerged)
        dya16, dys16 = dya.astype(BF16), dys.astype(BF16)
        dyn = lax.dot_general(dys16, wos_ref[...], _NT, preferred_element_type=F32)
        dy, dz, dgn = norm_vjp(dyn)

        yn_ref[...] = yn16
        mg_ref[...] = merged16
        dout_ref[...] = dout16
        dya_ref[...] = dya16
        dys_ref[...] = dys16
        da_ref[...] = lax.dot_general(dya16, woa_ref[...], _NT, preferred_element_type=F32)
        dy_ref[...] = dy
        dz_ref[...] = dz.astype(BF16)
        dga_ref[...] = dga.astype(BF16)
        dgs_ref[...] = dgs.astype(BF16)
        dres_ref[...] = dres

        @pl.when(i == 0)
        def _():
            loss_ref[...] = jnp.zeros_like(loss_ref)
            dgp_ref[...] = jnp.zeros_like(dgp_ref)
            dgn_ref[...] = jnp.zeros_like(dgn_ref)

        loss_ref[...] += loss * counted
        dgp_ref[...] += dgp
        dgn_ref[...] += dgn

    wide, narrow = _row_spec(SSM_INNER), _row_spec(D_MODEL)
    resident = pl.BlockSpec(memory_space=pltpu.VMEM)
    bf = lambda w: jax.ShapeDtypeStruct((rows, w), BF16)
    f32 = lambda w: jax.ShapeDtypeStruct((rows, w), F32)
    return _call(body, name="tail", grid=(nb,),
                 in_specs=[wide, wide, _row_spec(D_MODEL, GATE_ATT_BLOCK), _row_spec(D_MODEL, GATE_SSM_BLOCK), narrow,
                           _x_spec(), _x_spec(), resident, resident, resident, _full((1, SSM_INNER)),
                           _full((1, D_MODEL))],
                 out_specs=[wide, narrow, narrow, narrow, narrow, narrow, wide, wide, narrow, narrow, narrow,
                            _full((8, LANES)), _full((1, D_MODEL)), _full((1, SSM_INNER))],
                 out_shape=[bf(SSM_INNER), bf(D_MODEL), bf(D_MODEL), bf(D_MODEL), bf(D_MODEL), f32(D_MODEL),
                            f32(SSM_INNER), bf(SSM_INNER), bf(D_MODEL), bf(D_MODEL), f32(D_MODEL),
                            jax.ShapeDtypeStruct((8, LANES), F32), jax.ShapeDtypeStruct((1, D_MODEL), F32),
                            jax.ShapeDtypeStruct((1, SSM_INNER), F32)],
                 sem=("arbitrary",))(y_ssd, proj, proj, proj, a_att, x, target, woa, wos, wo, g_norm, g_post)


_NT = (((1,), (1,)), ((), ()))
ALIBI_SLOPES = tuple(2.0 ** (-8.0 * (h + 1) / ATT_Q_HEADS) for h in range(ATT_Q_HEADS))
KV_WIDTH = ATT_KV_HEADS * HEAD_DIM
Q_BLOCK = SEG["q"][2] // D_MODEL
Z_ATT_BLOCK = SEG["z_att"][2] // D_MODEL
K_BLOCK = SEG["k"][2] // KV_WIDTH
V_BLOCK = SEG["v"][2] // KV_WIDTH
META_ROW_BLOCK = PAD_ROWS // N_META


@jax.custom_vjp
def _swap_halves(x):
    return pltpu.roll(x, HEAD_DIM, 1)


_swap_halves.defvjp(lambda x: (pltpu.roll(x, HEAD_DIM, 1), None), lambda _, g: (pltpu.roll(g, HEAD_DIM, 1),))


def _both_halves(t, half):
    first = lax.broadcasted_iota(jnp.int32, t.shape, 1) < HEAD_DIM
    sw = _swap_halves(t)
    return jnp.where(first, t, sw) if half == 0 else jnp.where(first, sw, t)


def _attn_rows(q, z, kp, kc, vp, vc, km, vm, sinks, n):
    rows = ATT_GROUP * BLOCK
    i = lax.broadcasted_iota(jnp.int32, (rows, BLOCK), 0) & (BLOCK - 1)
    j = lax.broadcasted_iota(jnp.int32, (rows, BLOCK), 1)
    rel_c = (i - j).astype(F32)
    rel_p = rel_c + float(BLOCK)
    nv = jnp.zeros((rows, BLOCK), jnp.int32) + n
    ok_c = (i >= j) & (nv >= 1)
    ok_p = (j > i) & (nv >= 2)
    im = lax.broadcasted_iota(jnp.int32, (rows, N_META), 0) & (BLOCK - 1)
    jm = lax.broadcasted_iota(jnp.int32, (rows, N_META), 1)
    ok_m = ((jnp.zeros((rows, N_META), jnp.int32) + n) >= 1) | (im >= PAD_ROWS + jm)
    first = lax.broadcasted_iota(jnp.int32, (BLOCK, LANES), 1) < HEAD_DIM
    neg = -jnp.inf
    outs = []
    for kv in range(ATT_KV_HEADS):
        tile, half = divmod(kv, 2)
        lanes = slice(tile * LANES, (tile + 1) * LANES)
        kc2, kp2, km2 = (_both_halves(t[:, lanes], half).astype(BF16) for t in (kc, kp, km))
        vc2, vp2, vm2 = (_both_halves(t[:, lanes], half).astype(BF16) for t in (vc, vp, vm))
        qs, slope, sk = [], [], []
        for pair in range(ATT_GROUP // 2):
            c0 = (kv * ATT_GROUP + 2 * pair) * HEAD_DIM
            qp = q[:, c0:c0 + LANES] * HEAD_DIM ** -0.5
            qs += [jnp.where(first, qp, 0.0), jnp.where(first, 0.0, qp)]
        for g in range(ATT_GROUP):
            slope.append(jnp.full((BLOCK, 1), ALIBI_SLOPES[kv * ATT_GROUP + g], F32))
            sk.append(jnp.broadcast_to(sinks[kv * ATT_GROUP + g], (BLOCK, 1)))
        qs = jnp.concatenate(qs, axis=0).astype(BF16)
        slope = jnp.concatenate(slope, axis=0)
        sk = jnp.concatenate(sk, axis=0)
        sc = jnp.where(ok_c, lax.dot_general(qs, kc2, _NT, preferred_element_type=F32) - slope * rel_c, neg)
        sp = jnp.where(ok_p, lax.dot_general(qs, kp2, _NT, preferred_element_type=F32) - slope * rel_p, neg)
        sm = jnp.where(ok_m, lax.dot_general(qs, km2, _NT, preferred_element_type=F32), neg)
        mx = jnp.maximum(jnp.maximum(jnp.max(sc, axis=1, keepdims=True), jnp.max(sp, axis=1, keepdims=True)),
                         jnp.maximum(jnp.max(sm, axis=1, keepdims=True), sk))
        mx = lax.stop_gradient(mx)
        ec, ep, em, es = jnp.exp(sc - mx), jnp.exp(sp - mx), jnp.exp(sm - mx), jnp.exp(sk - mx)
        den = (es + jnp.sum(ec, axis=1, keepdims=True) + jnp.sum(ep, axis=1, keepdims=True)
               + jnp.sum(em, axis=1, keepdims=True))
        inv = 1.0 / den
        o = (jnp.dot((ec * inv).astype(BF16), vc2, preferred_element_type=F32)
             + jnp.dot((ep * inv).astype(BF16), vp2, preferred_element_type=F32)
             + jnp.dot((em * inv).astype(BF16), vm2, preferred_element_type=F32))
        for pair in range(ATT_GROUP // 2):
            r0 = 2 * pair * BLOCK
            outs.append(jnp.where(first, o[r0:r0 + BLOCK], o[r0 + BLOCK:r0 + 2 * BLOCK]))
    return jnp.concatenate(outs, axis=1) * _silu(z)


def _attn_specs(nb, steps_clamped):
    def blk(t):
        return jnp.minimum(t, nb - 1) if steps_clamped else t

    wide = lambda col: pl.BlockSpec((BLOCK, D_MODEL), lambda t: (blk(t), col))
    cur = lambda col: pl.BlockSpec((BLOCK, KV_WIDTH), lambda t: (blk(t), col))
    prev = lambda col: pl.BlockSpec((BLOCK, KV_WIDTH), lambda t: (jnp.maximum(blk(t) - 1, 0), col))
    meta = lambda col: pl.BlockSpec((N_META, KV_WIDTH), lambda t: (META_ROW_BLOCK, col))
    sinks = pl.BlockSpec((ATT_Q_HEADS, 1, 1), lambda t: (0, 0, 0))
    return [wide(Q_BLOCK), wide(Z_ATT_BLOCK), prev(K_BLOCK), cur(K_BLOCK), prev(V_BLOCK), cur(V_BLOCK),
            meta(K_BLOCK), meta(V_BLOCK), sinks]


def attn_fwd(proj, sinks):
    nb = proj.shape[0] // BLOCK

    def body(q_ref, z_ref, kp_ref, kc_ref, vp_ref, vc_ref, km_ref, vm_ref, sk_ref, o_ref):
        o_ref[...] = _attn_rows(q_ref[...], z_ref[...], kp_ref[...], kc_ref[...], vp_ref[...], vc_ref[...],
                                km_ref[...], vm_ref[...], tuple(sk_ref[h] for h in range(ATT_Q_HEADS)),
                                pl.program_id(0)).astype(BF16)

    return _call(body, name="attn_fwd", grid=(nb,), in_specs=_attn_specs(nb, False), out_specs=_row_spec(D_MODEL),
                 out_shape=jax.ShapeDtypeStruct((nb * BLOCK, D_MODEL), BF16), sem=("parallel",))(*([proj] * 8), sinks)


def attn_bwd(da, proj, sinks):
    nb = proj.shape[0] // BLOCK
    last = nb - 1
    wide = pl.BlockSpec((BLOCK, D_MODEL), lambda t: (jnp.minimum(t, last), 0))
    done = pl.BlockSpec((BLOCK, KV_WIDTH), lambda t: (jnp.maximum(t - 1, 0), 0))
    meta = _full((N_META, KV_WIDTH))
    par = _full((ATT_Q_HEADS, 1, 1))

    def body(da_ref, q_ref, z_ref, kp_ref, kc_ref, vp_ref, vc_ref, km_ref, vm_ref, sk_ref,
             dq_ref, dz_ref, dk_ref, dv_ref, dkm_ref, dvm_ref, dsk_ref, ck_ref, cv_ref):
        t = pl.program_id(0)

        @pl.when(t == 0)
        def _():
            ck_ref[...] = jnp.zeros_like(ck_ref)
            cv_ref[...] = jnp.zeros_like(cv_ref)
            dkm_ref[...] = jnp.zeros_like(dkm_ref)
            dvm_ref[...] = jnp.zeros_like(dvm_ref)
            dsk_ref[...] = jnp.zeros_like(dsk_ref)

        @pl.when(t < nb)
        def _():
            def f(q, z, kp, kc, vp, vc, km, vm, sk):
                return _attn_rows(q, z, kp, kc, vp, vc, km, vm, sk, t)

            _, vjp = jax.vjp(f, q_ref[...], z_ref[...], kp_ref[...], kc_ref[...], vp_ref[...], vc_ref[...],
                             km_ref[...], vm_ref[...], tuple(sk_ref[h] for h in range(ATT_Q_HEADS)))
            dq, dz, dkp, dkc, dvp, dvc, dkm, dvm, dsk = vjp(da_ref[...])
            dq_ref[...] = dq.astype(BF16)
            dz_ref[...] = dz.astype(BF16)
            for h in range(ATT_Q_HEADS):
                dsk_ref[h] += dsk[h]
            dk_ref[...] = ck_ref[...] + dkp
            dv_ref[...] = cv_ref[...] + dvp
            ck_ref[...] = dkc
            cv_ref[...] = dvc
            dkm_ref[...] += dkm
            dvm_ref[...] += dvm

        @pl.when(t == nb)
        def _():
            dk_ref[...] = ck_ref[...]
            dv_ref[...] = cv_ref[...]

    rows = nb * BLOCK
    return _call(body, name="attn_bwd", grid=(nb + 1,), in_specs=[wide] + _attn_specs(nb, True),
                 out_specs=[wide, wide, done, done, meta, meta, par],
                 out_shape=[jax.ShapeDtypeStruct((rows, D_MODEL), BF16), jax.ShapeDtypeStruct((rows, D_MODEL), BF16),
                            jax.ShapeDtypeStruct((rows, KV_WIDTH), F32), jax.ShapeDtypeStruct((rows, KV_WIDTH), F32),
                            jax.ShapeDtypeStruct((N_META, KV_WIDTH), F32), jax.ShapeDtypeStruct((N_META, KV_WIDTH), F32),
                            jax.ShapeDtypeStruct(sinks.shape, F32)],
                 scratch=[pltpu.VMEM((BLOCK, KV_WIDTH), F32), pltpu.VMEM((BLOCK, KV_WIDTH), F32)],
                 sem=("arbitrary",))(da, *([proj] * 8), sinks)


XBC_BLOCK0 = SEG["xbc"][2] // D_MODEL
CONV_COL_BLOCKS = CONV_DIM // D_MODEL
DT_TILE = SEG["dt"][2] // LANES


HALO = 8


def _conv_rows(length):
    return 544 if length % 544 == 0 else BLOCK


def _shift_rows(cur, before, j):
    if j == 0:
        return cur
    n = cur.shape[0]
    row = lax.broadcasted_iota(jnp.int32, cur.shape, 0)
    head = pltpu.roll(before, j, 0)
    if n > HALO:
        head = jnp.concatenate([head, jnp.zeros((n - HALO, cur.shape[1]), cur.dtype)], axis=0)
    return jnp.where(row >= j, pltpu.roll(cur, j, 0), head)


def _conv_pre(cur, before, w_ref, b_ref):
    pre = b_ref[...] + w_ref[CONV_WIDTH - 1:CONV_WIDTH, :] * cur
    for k in range(CONV_WIDTH - 1):
        pre = pre + w_ref[k:k + 1, :] * _shift_rows(cur, before, CONV_WIDTH - 1 - k)
    return pre


def _conv_specs(steps, rows, col0=0):
    first = XBC_BLOCK0 + col0
    halos = rows // HALO
    cur = pl.BlockSpec((rows, D_MODEL), lambda j, i: (i, first + j))
    before = pl.BlockSpec((HALO, D_MODEL), lambda j, i: (jnp.maximum(i * halos - 1, 0), first + j))
    after = pl.BlockSpec((HALO, D_MODEL), lambda j, i: (jnp.minimum(i + 1, steps - 1) * halos, first + j))
    return cur, before, after


def _valid_rows(i, rows):
    row = lax.broadcasted_iota(jnp.int32, (rows, D_MODEL), 0)
    return jnp.maximum((row >= PAD_ROWS).astype(F32), jnp.where(i > 0, 1.0, 0.0))


def conv_fwd(proj, conv_w, conv_b):
    rows = _conv_rows(proj.shape[0])
    steps = proj.shape[0] // rows
    cur, before, _ = _conv_specs(steps, rows)

    def body(c_ref, p_ref, w_ref, b_ref, o_ref):
        i = pl.program_id(1)
        pre = _conv_pre(c_ref[...], p_ref[...] * jnp.where(i > 0, 1.0, 0.0), w_ref, b_ref)
        o_ref[...] = _silu(pre) * _valid_rows(i, rows)

    return _call(body, name="conv_fwd", grid=(CONV_COL_BLOCKS, steps),
                 in_specs=[cur, before, pl.BlockSpec((CONV_WIDTH, D_MODEL), lambda j, i: (0, j)),
                           pl.BlockSpec((1, D_MODEL), lambda j, i: (0, j))],
                 out_specs=pl.BlockSpec((rows, D_MODEL), lambda j, i: (i, j)),
                 out_shape=jax.ShapeDtypeStruct((proj.shape[0], CONV_DIM), F32),
                 sem=("parallel", "parallel"))(proj, proj, conv_w, conv_b)


def conv_bwd(name, dparts, col0, proj, conv_w, conv_b):
    rows = _conv_rows(proj.shape[0])
    steps = proj.shape[0] // rows
    last = steps - 1
    ncol = sum(d.shape[1] for d in dparts) // D_MODEL
    np_ = len(dparts)
    cur, before, after = _conv_specs(steps, rows, col0)
    dcur = [pl.BlockSpec((rows, d.shape[1] // ncol), lambda j, i: (i, j)) for d in dparts]
    dafter = [pl.BlockSpec((HALO, d.shape[1] // ncol), lambda j, i: (jnp.minimum(i + 1, last) * (rows // HALO), j))
              for d in dparts]
    out_cur = pl.BlockSpec((rows, D_MODEL), lambda j, i: (i, j))
    wspec = pl.BlockSpec((CONV_WIDTH, D_MODEL), lambda j, i: (0, col0 + j))
    bspec = pl.BlockSpec((1, D_MODEL), lambda j, i: (0, col0 + j))
    wout = pl.BlockSpec((CONV_WIDTH, D_MODEL), lambda j, i: (0, j))
    bout = pl.BlockSpec((1, D_MODEL), lambda j, i: (0, j))

    def body(*refs):
        dc_refs, da_refs = refs[:np_], refs[np_:2 * np_]
        c_ref, p_ref, a_ref, w_ref, b_ref, du_ref, dw_ref, db_ref = refs[2 * np_:]
        i = pl.program_id(1)
        row = lax.broadcasted_iota(jnp.int32, (rows, D_MODEL), 0)
        curv = c_ref[...]
        beforev = p_ref[...] * jnp.where(i > 0, 1.0, 0.0)
        side_by_side = lambda rs: rs[0][...] if np_ == 1 else jnp.concatenate([r[...] for r in rs], axis=1)

        def dpre_of(pre, d):
            s = _sigmoid(pre)
            return d * (s * (1.0 + pre * (1.0 - s)))

        dp_c = dpre_of(_conv_pre(curv, beforev, w_ref, b_ref), side_by_side(dc_refs) * _valid_rows(i, rows))
        dp_a = dpre_of(_conv_pre(a_ref[...], curv[rows - HALO:], w_ref, b_ref),
                       side_by_side(da_refs) * jnp.where(i < last, 1.0, 0.0))
        du = w_ref[CONV_WIDTH - 1:CONV_WIDTH, :] * dp_c
        for j in range(1, CONV_WIDTH):
            tail = jnp.concatenate([jnp.zeros((rows - HALO, D_MODEL), F32), pltpu.roll(dp_a, HALO - j, 0)], axis=0)
            up = jnp.where(row < rows - j, pltpu.roll(dp_c, rows - j, 0), tail)
            du = du + w_ref[CONV_WIDTH - 1 - j:CONV_WIDTH - j, :] * up
        du_ref[...] = du.astype(BF16)

        @pl.when(i == 0)
        def _():
            dw_ref[...] = jnp.zeros_like(dw_ref)
            db_ref[...] = jnp.zeros_like(db_ref)

        for k in range(CONV_WIDTH):
            dw_ref[k:k + 1, :] += jnp.sum(dp_c * _shift_rows(curv, beforev, CONV_WIDTH - 1 - k), axis=0, keepdims=True)
        db_ref[...] += jnp.sum(dp_c, axis=0, keepdims=True)

    width = ncol * D_MODEL
    return _call(body, name=name, grid=(ncol, steps),
                 in_specs=dcur + dafter + [cur, before, after, wspec, bspec], out_specs=[out_cur, wout, bout],
                 out_shape=[jax.ShapeDtypeStruct((proj.shape[0], width), BF16),
                            jax.ShapeDtypeStruct((CONV_WIDTH, width), F32), jax.ShapeDtypeStruct((1, width), F32)],
                 sem=("parallel", "arbitrary"))(*dparts, *dparts, proj, proj, proj, conv_w, conv_b)


def _head_expand():
    e = np.zeros((LANES, SSM_INNER), np.float32)
    for h in range(SSM_HEADS):
        e[h, h * HEAD_DIM:(h + 1) * HEAD_DIM] = 1.0
    return jnp.asarray(e, dtype=BF16)


def _softplus(x):
    return jnp.maximum(x, 0.0) + jnp.log(1.0 + jnp.exp(-jnp.abs(x)))


def _bf16_parts(x):
    hi = x.astype(BF16)
    rest = x - hi.astype(F32)
    mid = rest.astype(BF16)
    return hi, mid, (rest - mid.astype(F32)).astype(BF16)


@jax.custom_vjp
def _times_01(x, m):
    return sum(jnp.dot(p, m, preferred_element_type=F32) for p in _bf16_parts(x))


def _times_01_bwd(m, g):
    return sum(lax.dot_general(p, m, _NT, preferred_element_type=F32) for p in _bf16_parts(g)), jnp.zeros_like(m)


_times_01.defvjp(lambda x, m: (_times_01(x, m), m), _times_01_bwd)


def _causal_ones():
    l = lax.broadcasted_iota(jnp.int32, (BLOCK, BLOCK), 0)
    s = lax.broadcasted_iota(jnp.int32, (BLOCK, BLOCK), 1)
    return (l >= s).astype(BF16)


@jax.custom_vjp
def _cumsum_rows(a):
    return sum(jnp.dot(_causal_ones(), p, preferred_element_type=F32) for p in _bf16_parts(a))


def _cumsum_rows_bwd(_, g):
    tn = (((0,), (0,)), ((), ()))
    return (sum(lax.dot_general(_causal_ones(), p, tn, preferred_element_type=F32) for p in _bf16_parts(g)),)


_cumsum_rows.defvjp(lambda a: (_cumsum_rows(a), None), _cumsum_rows_bwd)


def _ssd_heads(dt_tile, bias, alog, dsk, expand):
    dt = _softplus(dt_tile + bias)
    a = dt * (-jnp.exp(alog))
    one_row = lambda v: jnp.broadcast_to(v, (HALO, LANES))
    return _times_01(jnp.concatenate([dt, _cumsum_rows(a), one_row(jnp.sum(a, axis=0, keepdims=True)), one_row(dsk)],
                                     axis=0), expand)


HEADS_ROWS = 2 * BLOCK + 2 * HALO


def _ssd_group(xs, per_lane, bg, cg, state):
    l = lax.broadcasted_iota(jnp.int32, (BLOCK, BLOCK), 0)
    s = lax.broadcasted_iota(jnp.int32, (BLOCK, BLOCK), 1)
    causal = l >= s
    first_head = s < HEAD_DIM
    dtx, cs = per_lane[0:BLOCK], per_lane[BLOCK:2 * BLOCK]
    tot, dsk = per_lane[2 * BLOCK:2 * BLOCK + 1], per_lane[2 * BLOCK + HALO:2 * BLOCK + HALO + 1]
    bb, cb16 = bg.astype(BF16), cg.astype(BF16)
    cb = lax.dot_general(cb16, bb, _NT, preferred_element_type=F32)
    xr = xs * dtx
    y_diag = []
    for p in range(GROUP_W // LANES):
        lanes = slice(p * LANES, (p + 1) * LANES)
        c_pair = cs[:, lanes]
        c_swap = _swap_halves(c_pair)
        m = []
        for c_head in (jnp.where(first_head, c_pair, c_swap), jnp.where(first_head, c_swap, c_pair)):
            m.append(cb * jnp.exp(jnp.where(causal, c_head - c_head.T, -jnp.inf)))
        x_pair = xr[:, lanes]
        x_diag = jnp.concatenate([jnp.where(first_head, x_pair, 0.0), jnp.where(first_head, 0.0, x_pair)], axis=0)
        y_diag.append(jnp.dot(jnp.concatenate(m, axis=1).astype(BF16), x_diag.astype(BF16),
                              preferred_element_type=F32))
    st = lax.dot_general(bb, (xr * jnp.exp(tot - cs)).astype(BF16), (((0,), (0,)), ((), ())),
                         preferred_element_type=F32)
    new_state = state * jnp.exp(tot) + st
    y_off = jnp.dot(cb16, state.astype(BF16), preferred_element_type=F32) * jnp.exp(cs)
    return jnp.concatenate(y_diag, axis=1) + y_off + dsk * xs, new_state


BC_WIDTH = SSM_GROUPS * SSM_STATE


def _ssd_specs(chunk):
    xs = pl.BlockSpec((BLOCK, SSM_INNER), lambda c: (chunk(c), 0))
    dt = pl.BlockSpec((BLOCK, LANES), lambda c: (chunk(c), DT_TILE))
    expand = _full((LANES, SSM_INNER))
    b = pl.BlockSpec((BLOCK, BC_WIDTH), lambda c: (chunk(c), SSM_INNER // BC_WIDTH))
    cc = pl.BlockSpec((BLOCK, BC_WIDTH), lambda c: (chunk(c), SSM_INNER // BC_WIDTH + 1))
    par = _full((1, LANES))
    state = pl.BlockSpec((None, SSM_STATE, SSM_INNER), lambda c: (chunk(c), 0, 0))
    return xs, dt, expand, b, cc, par, state


def _group_lanes(g):
    return slice(g * GROUP_W, (g + 1) * GROUP_W), slice(g * SSM_STATE, (g + 1) * SSM_STATE)


def ssd_fwd(xbc, proj, expand, bias, alog, dsk):
    nb = xbc.shape[0] // BLOCK
    xs, dt, ex, b, cc, par, state = _ssd_specs(lambda c: c)

    def body(x_ref, dt_ref, e_ref, bi_ref, al_ref, dk_ref, b_ref, c_ref, y_ref, sp_ref, st_ref):
        @pl.when(pl.program_id(0) == 0)
        def _():
            st_ref[...] = jnp.zeros_like(st_ref)

        per_lane = _ssd_heads(dt_ref[...], bi_ref[...], al_ref[...], dk_ref[...], e_ref[...])
        for g in range(SSM_GROUPS):
            wide, tile = _group_lanes(g)
            entering = st_ref[:, wide]
            sp_ref[:, wide] = entering
            y_ref[:, wide], st_ref[:, wide] = _ssd_group(x_ref[:, wide], per_lane[:, wide], b_ref[:, tile],
                                                         c_ref[:, tile], entering)

    return _call(body, name="ssd_fwd", grid=(nb,), in_specs=[xs, dt, ex, par, par, par, b, cc],
                 out_specs=[xs, state],
                 out_shape=[jax.ShapeDtypeStruct((nb * BLOCK, SSM_INNER), F32),
                            jax.ShapeDtypeStruct((nb, SSM_STATE, SSM_INNER), F32)],
                 scratch=[pltpu.VMEM((SSM_STATE, SSM_INNER), F32)],
                 sem=("arbitrary",))(xbc, proj, expand, bias, alog, dsk, xbc, xbc)


def ssd_bwd(dy, xbc, proj, expand, bias, alog, dsk, states, comm):
    nb = xbc.shape[0] // BLOCK
    last = nb - 1
    xs, dt, ex, b, cc, par, state = _ssd_specs(lambda c: last - c)
    tile = pl.BlockSpec((BLOCK, LANES), lambda c: (last - c, 0))
    nspec = pl.BlockSpec((BLOCK, BC_WIDTH), lambda c: (last - c, 0))

    def body(dy_ref, x_ref, dt_ref, e_ref, bi_ref, al_ref, dk_ref, b_ref, c_ref, sp_ref,
             dx_ref, ddt_ref, db_ref, dc_ref, dbi_ref, dal_ref, ddk_ref, ds_ref):
        @pl.when(pl.program_id(0) == 0)
        def _():
            ds_ref[...] = jnp.zeros_like(ds_ref)
            dbi_ref[...] = jnp.zeros_like(dbi_ref)
            dal_ref[...] = jnp.zeros_like(dal_ref)
            ddk_ref[...] = jnp.zeros_like(ddk_ref)

        expand = e_ref[...]
        per_lane, heads_vjp = jax.vjp(lambda t, bi, al, dk: _ssd_heads(t, bi, al, dk, expand), dt_ref[...], bi_ref[...],
                                      al_ref[...], dk_ref[...])
        d_per_lane = []
        for g in range(SSM_GROUPS):
            wide, tile_lanes = _group_lanes(g)
            _, vjp = jax.vjp(_ssd_group, x_ref[:, wide], per_lane[:, wide], b_ref[:, tile_lanes], c_ref[:, tile_lanes],
                             sp_ref[:, wide])
            (dx_ref[:, wide], d_lanes, db_ref[:, tile_lanes], dc_ref[:, tile_lanes],
             ds_ref[:, wide]) = vjp((dy_ref[:, wide], ds_ref[:, wide]))
            d_per_lane.append(d_lanes)
        ddt, dbi, dal, ddk = heads_vjp(jnp.concatenate(d_per_lane, axis=1))
        ddt_ref[...] = ddt.astype(BF16)
        dbi_ref[...] += dbi
        dal_ref[...] += dal
        ddk_ref[...] += ddk

    def at():
        c = pl.program_id(0)
        return c == 0, c == 0, c == last

    par_shape = jax.ShapeDtypeStruct((1, LANES), F32)
    return _call_with_comm(
        body, comm, at, (dy, xbc, proj, expand, bias, alog, dsk, xbc, xbc, states), name="ssd_bwd",
        grid=(nb,), in_specs=[xs, xs, dt, ex, par, par, par, b, cc, state],
        out_specs=[xs, tile, nspec, nspec, par, par, par],
        out_shape=[jax.ShapeDtypeStruct((nb * BLOCK, SSM_INNER), F32), jax.ShapeDtypeStruct((nb * BLOCK, LANES), BF16),
                   jax.ShapeDtypeStruct((nb * BLOCK, BC_WIDTH), F32), jax.ShapeDtypeStruct((nb * BLOCK, BC_WIDTH), F32),
                   par_shape, par_shape, par_shape],
        scratch=[pltpu.VMEM((SSM_STATE, SSM_INNER), F32)])


SLAB_ROWS = 16
SLAB_META_ROW = 8
SLAB_META_SHAPE = (8, 2 * D_MODEL)
SLAB_LOSS_ROW = 7


def pack_small(dcw, dcb, dgpre, dgpost, dbias, dalog, ddsk, dsinks, dgn, dmeta, loss_tile):
    def body(cw, cb, gpre, gpost, dtb, al, dk, sk, gn, meta, loss, o_ref):
        o_ref[...] = jnp.zeros_like(o_ref)
        o_ref[SLAB_LOSS_ROW:SLAB_LOSS_ROW + 1, 0:LANES] = loss[0:1, :]
        o_ref[0:CONV_WIDTH, :] = cw[...]
        o_ref[4:5, :] = cb[...]
        o_ref[5:6, 0:1024] = gpre[...]
        o_ref[5:6, 1024:2048] = gpost[...]
        o_ref[5:6, 2048:2176] = dtb[...]
        o_ref[5:6, 2176:2304] = al[...]
        o_ref[5:6, 2304:2432] = dk[...]
        o_ref[5:6, 2432:2560] = sk[...]
        o_ref[6:7, 0:SSM_INNER] = gn[...]
        o_ref[SLAB_META_ROW:SLAB_ROWS, 0:SLAB_META_SHAPE[1]] = meta[...]

    args = (dcw, dcb, dgpre, dgpost, dbias, dalog, ddsk, dsinks, dgn, dmeta, loss_tile)
    return _call(body, name="pack_small", in_specs=[_full(a.shape) for a in args],
                 out_specs=_full((SLAB_ROWS, CONV_DIM)), out_shape=jax.ShapeDtypeStruct((SLAB_ROWS, CONV_DIM), F32))(*args)


def _lane_tile(v):
    return jnp.pad(v, ((0, 0), (0, LANES - v.shape[1])))


def kernel(x, meta_tokens, g_pre, w_in, conv_w, conv_b, dt_bias, a_log, d_skip, attn_sinks, g_ssm_norm, w_out_att, w_out_ssm, w_out, g_post, loss_target, m_meta_tokens, m_g_pre, m_w_in, m_conv_w, m_conv_b, m_dt_bias, m_a_log, m_d_skip, m_attn_sinks, m_g_ssm_norm, m_w_out_att, m_w_out_ssm, m_w_out, m_g_post, v_meta_tokens, v_g_pre, v_w_in, v_conv_w, v_conv_b, v_dt_bias, v_a_log, v_d_skip, v_attn_sinks, v_g_ssm_norm, v_w_out_att, v_w_out_ssm, v_w_out, v_g_post):
    chip = _chip_index()

    conv_w_rows = jnp.pad(conv_w[0], ((0, 2 * 8 - CONV_WIDTH), (0, 0)))
    w_in_t, m_w_in_t, v_w_in_t = w_in[0].T, m_w_in[0].T, v_w_in[0].T
    gathered_w_in, g_conv_w, g_meta = run_comm("gather_w_in",
                                               TwoLevelGather([pack_w_in(w_in_t), conv_w_rows, meta_tokens]))
    w_all_t = unpack_w_in(gathered_w_in)
    cw_full = g_conv_w[:, :CONV_WIDTH].transpose(1, 0, 2).reshape(CONV_WIDTH, CONV_DIM)
    meta_full = g_meta.transpose(1, 0, 2).reshape(N_META, D_MODEL)
    behind_w_in = 0.0 * g_meta[0, 0, 0]
    w_out_flight, w_out_started = chip_exchange_start(
        "gather_w_out_start", [(w[0] + behind_w_in).astype(BF16) for w in (w_out_att, w_out_ssm, w_out)], False)

    h, u = prep(x, meta_full, g_pre)
    proj = project("in_proj", u, w_all_t, w_out_started)

    sinks3 = attn_sinks.reshape(ATT_Q_HEADS, 1, 1)
    a_att = attn_fwd(proj, sinks3)

    xbc = conv_fwd(proj, cw_full, conv_b)
    expand = _head_expand()
    head_pars = (_lane_tile(dt_bias), _lane_tile(a_log), _lane_tile(d_skip))
    y_ssd, states = ssd_fwd(xbc, proj, expand, *head_pars)
    woa, wos, wo = [g.reshape(-1, D_MODEL) for g in chip_exchange_wait("gather_w_out_wait", w_out_flight, False, y_ssd)]

    (yn, merged, dout, dy_att, dy_ssm, da_att, dy_ssd, dz_ssm, dga, dgs, dres, loss_tile, dg_post, dgn) = tail(
        y_ssd, proj, a_att, x, loss_target, woa, wos, wo, g_ssm_norm, g_post)

    dwo = mm_tn("out_proj_dw", merged, dout)
    dwoa = mm_tn("att_out_dw", a_att, dy_att)
    dwos = mm_tn("ssm_out_dw", yn, dy_ssm)
    dq, dz_att, dk, dv, dkmeta, dvmeta, dsinks3 = attn_bwd(da_att, proj, sinks3)
    dk = dk.at[PAD_ROWS:BLOCK].add(dkmeta).astype(BF16)
    dv = dv.at[PAD_ROWS:BLOCK].add(dvmeta).astype(BF16)

    def pieces(g):
        return g.reshape(4, 2, g.shape[0] // 8, g.shape[1])

    def to_owner(g):
        return (lambda ref, dev: ref.at[_chip_of(dev), dev[2]], (g.shape[0] // 8, g.shape[1]))

    (dxs, ddt_tile, dbg, dcg, dbias, dalog, ddsk), sent_w_out = ssd_bwd(
        dy_ssd, xbc, proj, expand, *head_pars, states,
        DirectExchange([pieces(dwoa), pieces(dwos), pieces(dwo)], [to_owner(dwoa), to_owner(dwos), to_owner(dwo)],
                       ALL_MASKS, "dev", 8))
    dxs_raw, dcw_xs, dcb_xs = conv_bwd("conv_bwd_x", [dxs], 0, proj, cw_full, conv_b)
    dbc_raw, dcw_bc, dcb_bc = conv_bwd("conv_bwd_bc", [dbg, dcg], SSM_INNER // D_MODEL, proj, cw_full, conv_b)
    dcw = jnp.concatenate([dcw_xs, dcw_bc], axis=1)
    dcb = jnp.concatenate([dcb_xs, dcb_bc], axis=1)

    narrow = jnp.concatenate([dk, dv, ddt_tile, jnp.zeros((dk.shape[0], N_ACT - N_ALIGNED), BF16)], axis=1)
    dproj = [dz_ssm, dxs_raw, dbc_raw, dq, dz_att, dga, dgs, narrow]
    half_rows = PACK_W // 2
    partial = weight_grad_t("in_proj_dw", dproj, u).reshape(4, 2, half_rows, D_MODEL)
    from_sibling, = run_comm("pair_grads", DirectExchange(
        [partial], [(lambda ref, dev: ref.at[pl.ds(0, 4), dev[2]], (4, half_rows, D_MODEL))], SIBLING_MASK, "core", 2,
        keep_own=False))
    chip_sum = sum_pair(partial, from_sibling)
    grads_flight, started = chip_exchange_start("reduce_w_in_start", [chip_sum], True)
    du = project_back("in_proj_dx", dproj, w_all_t, started)
    grad_x, dmeta, dg_pre = prep_bwd(h, du, dres, g_pre)

    halves = [sum_slots("sum_" + nm, r) for nm, r in zip(("w_out_att", "w_out_ssm", "w_out"), sent_w_out)]
    shared = run_comm("share_w_out", DirectExchange(halves, [None] * 3, SIBLING_MASK, "core", 2))
    g_woa, g_wos, g_wo = [f.reshape(2 * f.shape[1], f.shape[2]) for f in shared]
    d_woa, nm_woa, nv_woa = adamw_rows("adamw_w_out_att", g_woa, w_out_att[0], m_w_out_att[0], v_w_out_att[0])
    d_wos, nm_wos, nv_wos = adamw_rows("adamw_w_out_ssm", g_wos, w_out_ssm[0], m_w_out_ssm[0], v_w_out_ssm[0])
    d_wo, nm_wo, nv_wo = adamw_rows("adamw_w_out", g_wo, w_out[0], m_w_out[0], v_w_out[0])

    sent_w_in, = chip_exchange_wait("reduce_w_in_wait", grads_flight, True, d_wos)
    slab = pack_small(dcw, dcb, dg_pre, dg_post, dbias, dalog, ddsk, _lane_tile(dsinks3.reshape(1, ATT_Q_HEADS)), dgn,
                      dmeta.reshape(SLAB_META_SHAPE), loss_tile)
    shared_w_in, slabs = run_comm("share_w_in", Both(
        DirectExchange([sum_slots("sum_w_in", sent_w_in)], [None], SIBLING_MASK, "core", 2),
        DirectExchange([slab], [None], ALL_MASKS, "dev", 8)))
    small = sum_slots("sum_small", slabs)
    loss = small[SLAB_LOSS_ROW, 0]
    g_w_in, d_w_in, nm_w_in, nv_w_in = [
        a.T for a in adamw_w_in(shared_w_in.reshape(PACK_W, D_MODEL), w_in_t, m_w_in_t, v_w_in_t)]

    cw_cols = CONV_DIM // 4
    meta_cols = D_MODEL // 4
    g_small = {
        "meta_tokens": lax.dynamic_slice(
            small[SLAB_META_ROW:SLAB_ROWS, 0:SLAB_META_SHAPE[1]].reshape(N_META, D_MODEL), (0, chip * meta_cols),
            (N_META, meta_cols)),
        "g_pre": small[5:6, 0:1024],
        "conv_w": lax.dynamic_slice(small, (0, chip * cw_cols), (CONV_WIDTH, cw_cols)),
        "conv_b": small[4:5, :],
        "dt_bias": small[5:6, 2048:2048 + SSM_HEADS],
        "a_log": small[5:6, 2176:2176 + SSM_HEADS],
        "d_skip": small[5:6, 2304:2304 + SSM_HEADS],
        "attn_sinks": small[5:6, 2432:2432 + ATT_Q_HEADS],
        "g_ssm_norm": small[6:7, 0:SSM_INNER],
        "g_post": small[5:6, 1024:2048],
    }
    names = list(g_small)
    w_small = dict(meta_tokens=meta_tokens, g_pre=g_pre, conv_w=conv_w[0], conv_b=conv_b, dt_bias=dt_bias, a_log=a_log,
                   d_skip=d_skip, attn_sinks=attn_sinks, g_ssm_norm=g_ssm_norm, g_post=g_post)
    m_small = dict(meta_tokens=m_meta_tokens, g_pre=m_g_pre, conv_w=m_conv_w[0], conv_b=m_conv_b, dt_bias=m_dt_bias,
                   a_log=m_a_log, d_skip=m_d_skip, attn_sinks=m_attn_sinks, g_ssm_norm=m_g_ssm_norm, g_post=m_g_post)
    v_small = dict(meta_tokens=v_meta_tokens, g_pre=v_g_pre, conv_w=v_conv_w[0], conv_b=v_conv_b, dt_bias=v_dt_bias,
                   a_log=v_a_log, d_skip=v_d_skip, attn_sinks=v_attn_sinks, g_ssm_norm=v_g_ssm_norm, g_post=v_g_post)
    upd = dict(zip(names, adamw_small([g_small[k] for k in names], [w_small[k] for k in names],
                                      [m_small[k] for k in names], [v_small[k] for k in names])))

    lead = {"conv_w"}

    def shaped(name, a):
        return a[None] if name in lead else a

    grads = dict(g_small, w_in=g_w_in, w_out_att=g_woa, w_out_ssm=g_wos, w_out=g_wo)
    deltas = dict({k: upd[k][0] for k in names}, w_in=d_w_in, w_out_att=d_woa, w_out_ssm=d_wos, w_out=d_wo)
    new_m = dict({k: upd[k][1] for k in names}, w_in=nm_w_in, w_out_att=nm_woa, w_out_ssm=nm_wos, w_out=nm_wo)
    new_v = dict({k: upd[k][2] for k in names}, w_in=nv_w_in, w_out_att=nv_woa, w_out_ssm=nv_wos, w_out=nv_wo)
    lead |= {"w_in", "w_out_att", "w_out_ssm", "w_out"}
    order = ["meta_tokens", "g_pre", "w_in", "conv_w", "conv_b", "dt_bias", "a_log", "d_skip", "attn_sinks",
             "g_ssm_norm", "w_out_att", "w_out_ssm", "w_out", "g_post"]
    outs = [loss, grad_x]
    for group in (grads, deltas, new_m, new_v):
        outs += [shaped(k, group[k]) for k in order]
    return tuple(outs)
```

```python
import numpy as np
import jax
import jax.numpy as jnp
from jax import lax
from jax.experimental import pallas as pl
from jax.experimental.pallas import tpu as pltpu

F32 = jnp.float32
BF16 = jnp.bfloat16

D_MODEL = 1024
N_META = 16
BLOCK = 128
PAD_ROWS = BLOCK - N_META
NORM_EPS = 1e-6
HEAD_DIM = 64
ATT_Q_HEADS = 16
ATT_KV_HEADS = 4
ATT_GROUP = 4
SSM_INNER = 2048
SSM_HEADS = 32
SSM_GROUPS = 4
SSM_STATE = 128
CONV_WIDTH = 4
CONV_DIM = 3072
LANES = 128

ADAM_LR = 0.001
ADAM_B1 = 0.9
ADAM_B2 = 0.999
ADAM_EPS = 1e-08
ADAM_WD = 0.01
ADAM_STEP = 10

VMEM_LIMIT = 48 * 1024 * 1024

SHARD_W = 2440
PACK_W = 2560
SHARD_STRIDE = 2432
N_ALIGNED = 9856
N_ACT = 10240
SEG = {
    "q": (0, 1024, 5120), "k": (1024, 256, 9216), "v": (1280, 256, 9472), "z_att": (1536, 1024, 6144),
    "z_ssm": (2560, 2048, 0), "xbc": (4608, 3072, 2048), "dt": (7680, 128, 9728),
    "gate_att": (7808, 1024, 7168), "gate_ssm": (8832, 1024, 8192),
}
DT_STORED_START = 7680
DT_PAD = LANES - SSM_HEADS


def _act_col(aligned_col):
    for a0, w, p0 in SEG.values():
        if a0 <= aligned_col < a0 + w:
            return p0 + aligned_col - a0
    raise ValueError(aligned_col)


def _call(body, *, name, out_shape, in_specs, out_specs, grid=(), scratch=(), sem=None, aliases=None):
    return pl.pallas_call(
        body, out_shape=out_shape, grid=grid, in_specs=in_specs, out_specs=out_specs, scratch_shapes=list(scratch),
        name=name, input_output_aliases=aliases or {},
        compiler_params=pltpu.CompilerParams(dimension_semantics=sem, vmem_limit_bytes=VMEM_LIMIT))


def _full(shape):
    n = len(shape)
    return pl.BlockSpec(shape, lambda *_: (0,) * n)


def _chip_index():
    return lax.axis_index("x") * 2 + lax.axis_index("y")


_sigmoid = jax.nn.sigmoid


def _silu(z):
    return z * _sigmoid(z)


def _rms(x, g):
    return x * lax.rsqrt(jnp.mean(x * x, axis=-1, keepdims=True) + NORM_EPS) * g


def _peer(mask):
    x, y, c = lax.axis_index("x"), lax.axis_index("y"), lax.axis_index("c")
    return ((1 - x) if mask & 4 else x, (1 - y) if mask & 2 else y, (1 - c) if mask & 1 else c)


def _me():
    return lax.axis_index("x"), lax.axis_index("y"), lax.axis_index("c")


def _chip_of(dev):
    return 2 * dev[0] + dev[1]


CHIP_MASKS = (4, 2, 6)
ALL_MASKS = (1, 2, 3, 4, 5, 6, 7)
SIBLING_MASK = (1,)


def _remote(src, dst, send_sem, recv_sem, dev):
    return pltpu.make_async_remote_copy(src_ref=src, dst_ref=dst, send_sem=send_sem, recv_sem=recv_sem,
                                        device_id=dev, device_id_type=pl.DeviceIdType.MESH)


class _StagedCopy:
    def __init__(self, src, stage, dst, load_sem, store_sem):
        self.load = pltpu.make_async_copy(src, stage, load_sem)
        self.store = pltpu.make_async_copy(stage, dst, store_sem)

    def start(self):
        self.load.start()
        self.load.wait()
        self.store.start()

    def wait(self):
        self.store.wait()


class DirectExchange:
    def __init__(self, arrays, pieces, masks, slot_kind, nslots, keep_own=True):
        self.arrays, self.pieces, self.masks, self.slot_kind = list(arrays), list(pieces), masks, slot_kind
        self.keep_own = keep_own
        n, nk = len(arrays), len(masks)
        shapes = [a.shape if p is None else p[1] for a, p in zip(arrays, pieces)]
        self.out_shape = [jax.ShapeDtypeStruct((nslots,) + tuple(s), a.dtype) for s, a in zip(shapes, arrays)]
        self.scratch = [pltpu.SemaphoreType.DMA((n * nk,)), pltpu.SemaphoreType.DMA((n * nk,))]
        if keep_own:
            self.scratch += [pltpu.SemaphoreType.DMA((2 * n,))] + [pltpu.VMEM(s, a.dtype) for s, a in zip(shapes, arrays)]
        self.has_mid = False

    def _copies(self, ins, outs, scratch):
        send_sems, recv_sems = scratch[:2]
        me = _me()
        slot = {"chip": _chip_of(me), "dev": 4 * me[0] + 2 * me[1] + me[2], "core": me[2]}[self.slot_kind]
        nk = len(self.masks)

        def piece(a, dev):
            return ins[a] if self.pieces[a] is None else self.pieces[a][0](ins[a], dev)

        local = []
        if self.keep_own:
            local_sems, stages = scratch[2], scratch[3:]
            local = [_StagedCopy(piece(a, me), stages[a], outs[a].at[slot], local_sems.at[2 * a], local_sems.at[2 * a + 1])
                     for a in range(len(ins))]
        remote = []
        for a in range(len(ins)):
            for ki, mask in enumerate(self.masks):
                dev = _peer(mask)
                remote.append(_remote(piece(a, dev), outs[a].at[slot], send_sems.at[a * nk + ki],
                                      recv_sems.at[a * nk + ki], dev))
        return local, remote

    def start(self, ins, outs, scratch):
        local, remote = self._copies(ins, outs, scratch)
        for cp in remote + local:
            cp.start()

    def finish(self, ins, outs, scratch):
        local, remote = self._copies(ins, outs, scratch)
        for cp in remote + local:
            cp.wait()


SPLIT_ROWS = 16


class TwoLevelGather:
    def __init__(self, arrays):
        self.arrays = list(arrays)
        n = len(arrays)
        self.out_shape = [jax.ShapeDtypeStruct((4,) + a.shape, a.dtype) for a in arrays]
        self.scratch = ([pltpu.SemaphoreType.DMA((4 * n,)), pltpu.SemaphoreType.DMA((4 * n,)),
                         pltpu.SemaphoreType.DMA((3 * n,)), pltpu.SemaphoreType.DMA((3 * n,)),
                         pltpu.SemaphoreType.DMA((2 * n,))] + [pltpu.VMEM(a.shape, a.dtype) for a in arrays])
        self.has_mid = True

    def _copies(self, ins, outs, scratch):
        ici_send, ici_recv, fwd_send, fwd_recv, local_sems = scratch[:5]
        stages = scratch[5:]
        me = _me()
        sibling, in_x, in_y, diagonal = _peer(1), _peer(4), _peer(2), _peer(6)
        plan = []
        for a in range(len(ins)):
            half = ins[a].shape[0] // 2
            first = half // 2 if half % (2 * SPLIT_ROWS) == 0 else half
            mine = pl.ds(me[2] * half, half)
            local = _StagedCopy(ins[a], stages[a], outs[a].at[_chip_of(me)], local_sems.at[2 * a], local_sems.at[2 * a + 1])

            def ici(k, src, dst, dev):
                return _remote(src, dst, ici_send.at[4 * a + k], ici_recv.at[4 * a + k], dev)

            def d2d(k, chip):
                zone = outs[a].at[_chip_of(chip), mine]
                return _remote(zone, zone, fwd_send.at[3 * a + k], fwd_recv.at[3 * a + k], sibling)

            own_zone = outs[a].at[_chip_of(me), mine]
            from_x = outs[a].at[_chip_of(in_x), pl.ds(me[2] * half, first)]
            onward = [ici(2, from_x, from_x, in_y), None]
            if first < half:
                from_y = outs[a].at[_chip_of(in_y), pl.ds(me[2] * half + first, half - first)]
                onward[1] = ici(3, from_y, from_y, in_x)
            plan.append(dict(
                local=local,
                own=[ici(0, ins[a].at[mine], own_zone, in_x), ici(1, ins[a].at[mine], own_zone, in_y)],
                onward=onward, sibling=[d2d(0, in_x), d2d(1, in_y), d2d(2, diagonal)]))
        return plan

    def start(self, ins, outs, scratch):
        for p in self._copies(ins, outs, scratch):
            for cp in p["own"]:
                cp.start()
            p["local"].start()

    def mid(self, ins, outs, scratch):
        plan = self._copies(ins, outs, scratch)
        for p in plan:
            for k in range(2):
                p["own"][k].wait_recv()
                if p["onward"][k] is not None:
                    p["onward"][k].start()
                p["sibling"][k].start()
        for p in plan:
            for cp in p["onward"]:
                if cp is not None:
                    cp.wait_recv()
            p["sibling"][2].start()

    def finish(self, ins, outs, scratch):
        for p in self._copies(ins, outs, scratch):
            for cp in p["sibling"]:
                cp.wait_recv()
            for cp in p["own"] + p["sibling"] + [cp for cp in p["onward"] if cp is not None]:
                cp.wait_send()
            p["local"].wait()


class Both:
    def __init__(self, a, b):
        self.a, self.b = a, b
        self.arrays, self.out_shape = a.arrays + b.arrays, a.out_shape + b.out_shape
        self.scratch = a.scratch + b.scratch
        self.has_mid = False
        assert not (a.has_mid or b.has_mid)

    def _parts(self, ins, outs, sems):
        na, sa = len(self.a.arrays), len(self.a.scratch)
        return (ins[:na], outs[:na], sems[:sa]), (ins[na:], outs[na:], sems[sa:])

    def start(self, ins, outs, sems):
        pa, pb = self._parts(ins, outs, sems)
        self.a.start(*pa)
        self.b.start(*pb)

    def finish(self, ins, outs, sems):
        pa, pb = self._parts(ins, outs, sems)
        self.a.finish(*pa)
        self.b.finish(*pb)


_ANY = pl.BlockSpec(memory_space=pl.ANY)


def run_comm(name, comm):
    n = len(comm.arrays)

    def body(*refs):
        ins, outs, sems = refs[:n], refs[n:2 * n], refs[2 * n:]
        comm.start(ins, outs, sems)
        if comm.has_mid:
            comm.mid(ins, outs, sems)
        comm.finish(ins, outs, sems)

    return pl.pallas_call(body, name=name, out_shape=comm.out_shape, in_specs=[_ANY] * n, out_specs=[_ANY] * n,
                          scratch_shapes=comm.scratch,
                          compiler_params=pltpu.CompilerParams(vmem_limit_bytes=VMEM_LIMIT))(*comm.arrays)


_HBM = pl.BlockSpec(memory_space=pltpu.HBM)
_SEM = pl.BlockSpec(memory_space=pltpu.SEMAPHORE)
_SIDE_EFFECT = pltpu.SideEffectType.DATAFLOW_SIDE_EFFECTING


def _chip_copies(srcs, lands, send_sems, recv_sems, by_target):
    me = _me()
    copies = []
    for a, (src, land) in enumerate(zip(srcs, lands)):
        for ki, mask in enumerate(CHIP_MASKS):
            dev = _peer(mask)
            k = a * len(CHIP_MASKS) + ki
            piece = src.at[_chip_of(dev)] if by_target else src
            copies.append(_remote(piece, land.at[_chip_of(me)], send_sems.at[k], recv_sems.at[k], dev))
    return copies


def chip_exchange_start(name, arrays, by_target):
    n = len(arrays)
    nsem = n * len(CHIP_MASKS)
    piece_shapes = [a.shape[1:] if by_target else a.shape for a in arrays]

    def body(*refs):
        srcs, lands = refs[:n], refs[n:2 * n]
        send_sems, recv_sems = refs[2 * n:2 * n + 2]
        token = refs[4 * n + 2]
        stages, local_sems = refs[4 * n + 3:5 * n + 3], refs[5 * n + 3]
        me = _me()
        for cp in _chip_copies(srcs, lands, send_sems, recv_sems, by_target):
            cp.start()
        for a in range(n):
            own = _StagedCopy(srcs[a].at[_chip_of(me)] if by_target else srcs[a], stages[a], lands[a].at[_chip_of(me)],
                              local_sems.at[2 * a], local_sems.at[2 * a + 1])
            own.start()
            own.wait()
        token[...] = jnp.zeros_like(token)

    lands = [lax.empty((4,) + tuple(s), a.dtype) for s, a in zip(piece_shapes, arrays)]
    hbm = lambda a: pltpu.HBM(a.shape, a.dtype)
    res = pl.pallas_call(
        body, name=name,
        out_shape=(pltpu.SemaphoreType.DMA((nsem,)), pltpu.SemaphoreType.DMA((nsem,)), *[hbm(a) for a in arrays],
                   *[hbm(l) for l in lands], jax.ShapeDtypeStruct((8, LANES), F32)),
        in_specs=(_HBM,) * (2 * n), out_specs=(_SEM, _SEM) + (_HBM,) * (2 * n) + (pl.BlockSpec(memory_space=pltpu.VMEM),),
        input_output_aliases={i: i + 2 for i in range(2 * n)},
        scratch_shapes=[pltpu.VMEM(tuple(s), a.dtype) for s, a in zip(piece_shapes, arrays)]
        + [pltpu.SemaphoreType.DMA((2 * n,))],
        compiler_params=pltpu.CompilerParams(has_side_effects=_SIDE_EFFECT, vmem_limit_bytes=VMEM_LIMIT),
    )(*[pltpu.with_memory_space_constraint(a, pltpu.HBM) for a in arrays + lands])
    return res[:-1], res[-1]


def chip_exchange_wait(name, in_flight, by_target, after):
    send_sems, recv_sems, *thru = in_flight
    n = len(thru) // 2

    def body(*refs):
        srcs, lands, (send, recv) = refs[:n], refs[n:2 * n], refs[2 * n:2 * n + 2]
        for cp in _chip_copies(srcs, lands, send, recv, by_target):
            cp.wait_send()
            cp.wait_recv()

    return pl.pallas_call(
        body, name=name, out_shape=tuple(pltpu.HBM(t.shape, t.dtype) for t in thru),
        in_specs=(_HBM,) * (2 * n) + (_SEM, _SEM, pl.BlockSpec(memory_space=pl.ANY)), out_specs=(_HBM,) * (2 * n),
        input_output_aliases={i: i for i in range(2 * n)},
        compiler_params=pltpu.CompilerParams(has_side_effects=_SIDE_EFFECT),
    )(*thru, send_sems, recv_sems, after)[n:]


def _call_with_comm(body, comm, steps, args, *, name, out_shape, in_specs, out_specs, grid, scratch=()):
    ni, no, ns, nc = len(in_specs), len(out_specs), len(scratch), len(comm.arrays)

    def full_body(*refs):
        ins, cins = refs[:ni], refs[ni:ni + nc]
        outs, couts = refs[ni + nc:ni + nc + no], refs[ni + nc + no:ni + 2 * nc + no]
        scr, csems = refs[ni + 2 * nc + no:ni + 2 * nc + no + ns], refs[ni + 2 * nc + no + ns:]
        first, middle, last = steps()
        pl.when(first)(lambda: comm.start(cins, couts, csems))
        if comm.has_mid:
            pl.when(middle)(lambda: comm.mid(cins, couts, csems))
        body(*ins, *outs, *scr)
        pl.when(last)(lambda: comm.finish(cins, couts, csems))

    res = pl.pallas_call(
        full_body, name=name, out_shape=list(out_shape) + comm.out_shape, grid=grid,
        in_specs=list(in_specs) + [_ANY] * nc, out_specs=list(out_specs) + [_ANY] * nc,
        scratch_shapes=list(scratch) + comm.scratch,
        compiler_params=pltpu.CompilerParams(dimension_semantics=("arbitrary",) * len(grid),
                                             vmem_limit_bytes=VMEM_LIMIT))(*args, *comm.arrays)
    return res[:no], res[no:]


def _shard_pieces(chip):
    if chip < 3:
        return [(0, SHARD_W, 8 * chip)]
    behind_dt = DT_STORED_START + SSM_HEADS - 3 * SHARD_W
    return [(0, behind_dt, 24), (behind_dt, SHARD_W - behind_dt, behind_dt + 24 + DT_PAD)]


W_IN_COLS = 256


def pack_w_in(wt):
    def body(w_ref, o_ref, pad_ref):
        chip = _chip_index()
        pad_ref[...] = jnp.zeros_like(pad_ref)
        for cv in range(4):
            @pl.when(chip == cv)
            def _():
                for src, n, dst in _shard_pieces(cv):
                    pad_ref[dst:dst + n, :] = w_ref[src:src + n, :]
        o_ref[...] = pad_ref[...].astype(BF16)

    return _call(body, name="pack_w_in", grid=(D_MODEL // W_IN_COLS,),
                 in_specs=[pl.BlockSpec((SHARD_W, W_IN_COLS), lambda i: (0, i))],
                 out_specs=pl.BlockSpec((PACK_W, W_IN_COLS), lambda i: (0, i)),
                 out_shape=jax.ShapeDtypeStruct((PACK_W, D_MODEL), BF16),
                 scratch=[pltpu.VMEM((PACK_W, W_IN_COLS), F32)], sem=("parallel",))(wt)


def _w_tile_sources():
    runs = [[] for _ in range(N_ACT // D_MODEL)]
    fix = [[] for _ in range(N_ACT // D_MODEL)]
    for t in range(N_ALIGNED // LANES):
        s = min(t // 19, 3)
        j = t - 19 * s
        tile, r = divmod(_act_col(t * LANES), D_MODEL)
        last = runs[tile][-1] if runs[tile] else None
        if last and last[2] == s and last[0] + last[1] == r and last[3] + last[1] == j * LANES:
            last[1] += LANES
        else:
            runs[tile].append([r, LANES, s, j * LANES])
        if j == 0 and s > 0:
            fix[tile].append((r, s - 1))
    return runs, fix


def _adamw(w, g, m, v):
    m = ADAM_B1 * m + (1.0 - ADAM_B1) * g
    v = ADAM_B2 * v + (1.0 - ADAM_B2) * jnp.square(g)
    m_hat = m / (1.0 - ADAM_B1 ** ADAM_STEP)
    v_hat = v / (1.0 - ADAM_B2 ** ADAM_STEP)
    delta = -ADAM_LR * (m_hat / (jnp.sqrt(v_hat) + ADAM_EPS) + ADAM_WD * w)
    return delta, m, v


def adamw_w_in(g_packed, wt, mt, vt):
    cols = LANES

    def body(g_ref, w_ref, m_ref, v_ref, go_ref, d_ref, mo_ref, vo_ref):
        chip = _chip_index()
        for cv in range(4):
            @pl.when(chip == cv)
            def _():
                for dst, n, src in _shard_pieces(cv):
                    go_ref[dst:dst + n, :] = g_ref[src:src + n, :]
        d_ref[...], mo_ref[...], vo_ref[...] = _adamw(w_ref[...], go_ref[...], m_ref[...], v_ref[...])

    spec = pl.BlockSpec((SHARD_W, cols), lambda i: (0, i))
    shp = jax.ShapeDtypeStruct((SHARD_W, D_MODEL), F32)
    return _call(body, name="adamw_w_in", grid=(D_MODEL // cols,),
                 in_specs=[pl.BlockSpec((PACK_W, cols), lambda i: (0, i)), spec, spec, spec],
                 out_specs=[spec] * 4, out_shape=[shp] * 4, sem=("parallel",))(g_packed, wt, mt, vt)


def adamw_rows(name, g, w, m, v):
    r, c = g.shape
    rows = min(r, BLOCK)

    def body(g_ref, w_ref, m_ref, v_ref, d_ref, mo_ref, vo_ref):
        d_ref[...], mo_ref[...], vo_ref[...] = _adamw(w_ref[...], g_ref[...], m_ref[...], v_ref[...])

    spec = pl.BlockSpec((rows, c), lambda i: (i, 0))
    shp = jax.ShapeDtypeStruct((r, c), F32)
    return _call(body, name=name, grid=(r // rows,), in_specs=[spec] * 4, out_specs=[spec] * 3, out_shape=[shp] * 3,
                 sem=("parallel",))(g, w, m, v)


def adamw_small(gs, ws, ms, vs):
    n = len(gs)

    def body(*refs):
        g, w, m, v = refs[:n], refs[n:2 * n], refs[2 * n:3 * n], refs[3 * n:4 * n]
        outs = refs[4 * n:]
        for i in range(n):
            d, mn, vn = _adamw(w[i][...], g[i][...], m[i][...], v[i][...])
            outs[3 * i][...] = d
            outs[3 * i + 1][...] = mn
            outs[3 * i + 2][...] = vn

    specs = [_full(a.shape) for a in gs]
    res = _call(body, name="adamw_small", in_specs=specs * 4,
                out_specs=[s for s in specs for _ in range(3)],
                out_shape=[jax.ShapeDtypeStruct(a.shape, F32) for a in gs for _ in range(3)])(*gs, *ws, *ms, *vs)
    return [tuple(res[3 * i:3 * i + 3]) for i in range(n)]


def sum_slots(name, r):
    s, rr, c = r.shape
    rows = min(rr, BLOCK)

    def body(r_ref, o_ref):
        acc = r_ref[0].astype(F32)
        for k in range(1, s):
            acc = acc + r_ref[k].astype(F32)
        o_ref[...] = acc

    return _call(body, name=name, grid=(rr // rows,), in_specs=[pl.BlockSpec((s, rows, c), lambda i: (0, i, 0))],
                 out_specs=pl.BlockSpec((rows, c), lambda i: (i, 0)), out_shape=jax.ShapeDtypeStruct((rr, c), F32),
                 sem=("parallel",))(r)


def sum_pair(partial, from_sibling):
    s, _, rr, cols = partial.shape
    rows = rr // 2

    def body(p_ref, r_ref, o_ref):
        c = lax.axis_index("c")
        o_ref[...] = (p_ref[c].astype(F32) + r_ref[1 - c].astype(F32)).astype(BF16)

    return _call(body, name="sum_pair", grid=(s, rr // rows),
                 in_specs=[pl.BlockSpec((None, 2, rows, cols), lambda k, i: (k, 0, i, 0)),
                           pl.BlockSpec((2, None, rows, cols), lambda k, i: (0, k, i, 0))],
                 out_specs=pl.BlockSpec((None, rows, cols), lambda k, i: (k, i, 0)),
                 out_shape=jax.ShapeDtypeStruct((s, rr, cols), BF16), sem=("parallel", "parallel"))(partial, from_sibling)


def _col_tile(n, k):
    if n % 896 == 0 and k <= 1024:
        return 896
    return min(n, 512)


def project(name, x, slabs, after):
    m, k = x.shape
    steps = N_ACT // D_MODEL
    runs, fix = _w_tile_sources()
    most = max(len(r) for r in runs) + 1

    def body(x_ref, slab_ref, after_ref, o_ref, wt_ref, tile_ref, fix_ref, sems):
        j = pl.program_id(0)

        def loads(t):
            b = t % 2
            cps = [pltpu.make_async_copy(slab_ref.at[s, d:d + n], tile_ref.at[b, r:r + n], sems.at[b * most + c])
                   for c, (r, n, s, d) in enumerate(runs[t])]
            return cps + [pltpu.make_async_copy(slab_ref.at[s, SHARD_STRIDE:PACK_W], fix_ref.at[b],
                                                sems.at[b * most + len(runs[t])]) for _, s in fix[t]]

        for t in range(steps):
            @pl.when(j == t)
            def _():
                b = t % 2
                if t == 0:
                    for cp in loads(0):
                        cp.start()
                for cp in loads(t):
                    cp.wait()
                if t + 1 < steps:
                    for cp in loads(t + 1):
                        cp.start()
                for r, _ in fix[t]:
                    tile_ref[b, r:r + LANES, :] = tile_ref[b, r:r + LANES, :] + fix_ref[b]
                covered = sum(n for _, n, _, _ in runs[t])
                if covered < D_MODEL:
                    tile_ref[b, covered:D_MODEL, :] = jnp.zeros((D_MODEL - covered, k), BF16)
                w = tile_ref[b]
                wt_ref[...] = w
                o_ref[...] = lax.dot_general(x_ref[...], w, _NT, preferred_element_type=F32)

    return _call(body, name=name, grid=(steps,), in_specs=[_full((m, k)), _ANY, _full(after.shape)],
                 out_specs=[pl.BlockSpec((m, D_MODEL), lambda j: (0, j)), pl.BlockSpec((D_MODEL, k), lambda j: (j, 0))],
                 out_shape=[jax.ShapeDtypeStruct((m, N_ACT), F32), jax.ShapeDtypeStruct((N_ACT, k), BF16)],
                 scratch=[pltpu.VMEM((2, D_MODEL, k), BF16), pltpu.VMEM((2, LANES, k), BF16),
                          pltpu.SemaphoreType.DMA((2 * most,))], sem=("arbitrary",))(x, slabs, after)


def _piece_tiles(pieces):
    spans, start = [], 0
    for p in pieces:
        spans.append((start, p.shape[1] // D_MODEL))
        start += p.shape[1] // D_MODEL
    return spans, start


def _piece_spec(tm, span, rows_of, tile_of):
    first, count = span

    def index(i, j):
        t = tile_of(i, j) - first
        mine = (t >= 0) & (t < count)
        return jnp.where(mine, rows_of(i, j), 0), jnp.clip(t, 0, count - 1)

    return pl.BlockSpec((tm, D_MODEL), index)


def project_back(name, pieces, wt, after):
    m = pieces[0].shape[0]
    k = wt.shape[1]
    tm = m // 2
    spans, steps = _piece_tiles(pieces)

    def body(*refs):
        w_ref, o_ref = refs[len(pieces)], refs[len(pieces) + 2]
        j = pl.program_id(1)

        @pl.when(j == 0)
        def _():
            o_ref[...] = jnp.zeros_like(o_ref)

        for dy_ref, (first, count) in zip(refs, spans):
            @pl.when((j >= first) & (j < first + count))
            def _():
                o_ref[...] += jnp.dot(dy_ref[...], w_ref[...], preferred_element_type=F32)

    return _call(body, name=name, grid=(m // tm, steps),
                 in_specs=[_piece_spec(tm, s, lambda i, j: i, lambda i, j: j) for s in spans]
                 + [pl.BlockSpec((D_MODEL, k), lambda i, j: (j, 0)), _full(after.shape)],
                 out_specs=pl.BlockSpec((tm, k), lambda i, j: (i, 0)), out_shape=jax.ShapeDtypeStruct((m, k), F32),
                 sem=("parallel", "arbitrary"))(*pieces, wt, after)


def _slab_runs():
    runs = [[] for _ in range(N_ACT // D_MODEL)]
    for s in range(4):
        for j in range(PACK_W // LANES):
            tile, r = divmod(_act_col((19 * s + j) * LANES), D_MODEL)
            last = runs[tile][-1] if runs[tile] else None
            if last and last[2] == s and last[0] + last[1] == r and last[3] + last[1] == j * LANES:
                last[1] += LANES
            else:
                runs[tile].append([r, LANES, s, j * LANES])
    return runs


def weight_grad_t(name, pieces, x):
    m = pieces[0].shape[0]
    k = x.shape[1]
    tm = m // 2
    spans, steps = _piece_tiles(pieces)
    runs = _slab_runs()
    most = max(len(r) for r in runs)

    def body(*refs):
        x_ref, o_ref, acc_ref, tile_ref, sems = refs[len(pieces):]
        i, half = pl.program_id(0), pl.program_id(1)

        def copies(t):
            return [pltpu.make_async_copy(tile_ref.at[t % 2, r:r + n], o_ref.at[s, d:d + n], sems.at[(t % 2) * most + c])
                    for c, (r, n, s, d) in enumerate(runs[t])]

        for dy_ref, (first, count) in zip(refs, spans):
            for t in range(first, first + count):
                @pl.when(i == t)
                def _():
                    part = lax.dot_general(dy_ref[...], x_ref[...], (((0,), (0,)), ((), ())),
                                           preferred_element_type=F32)

                    @pl.when(half == 0)
                    def _():
                        acc_ref[...] = part

                    @pl.when(half == 1)
                    def _():
                        if t >= 2:
                            for cp in copies(t - 2):
                                cp.wait()
                        tile_ref[t % 2] = (acc_ref[...] + part).astype(BF16)
                        for cp in copies(t):
                            cp.start()
                        if t == steps - 1:
                            for cp in copies(t - 1) + copies(t):
                                cp.wait()

    return _call(body, name=name, grid=(steps, 2),
                 in_specs=[_piece_spec(tm, s, lambda i, j: j, lambda i, j: i) for s in spans]
                 + [pl.BlockSpec((tm, k), lambda i, j: (j, 0))],
                 out_specs=_ANY, out_shape=jax.ShapeDtypeStruct((4, PACK_W, k), BF16),
                 scratch=[pltpu.VMEM((D_MODEL, k), F32), pltpu.VMEM((2, D_MODEL, k), BF16),
                          pltpu.SemaphoreType.DMA((2 * most,))],
                 sem=("arbitrary", "arbitrary"))(*pieces, x)


def mm_tn(name, x, dy):
    m, k = x.shape
    n = dy.shape[1]
    tm = m // 2
    tn = _col_tile(n, k)

    def body(x_ref, dy_ref, o_ref, acc_ref):
        part = lax.dot_general(x_ref[...].astype(BF16), dy_ref[...].astype(BF16), (((0,), (0,)), ((), ())),
                               preferred_element_type=F32)

        @pl.when(pl.program_id(1) == 0)
        def _():
            acc_ref[...] = part

        @pl.when(pl.program_id(1) == 1)
        def _():
            o_ref[...] = (acc_ref[...] + part).astype(BF16)

    return _call(body, name=name, grid=(n // tn, 2),
                 in_specs=[pl.BlockSpec((tm, k), lambda i, j: (j, 0)), pl.BlockSpec((tm, tn), lambda i, j: (j, i))],
                 out_specs=pl.BlockSpec((k, tn), lambda i, j: (0, i)), out_shape=jax.ShapeDtypeStruct((k, n), BF16),
                 scratch=[pltpu.VMEM((k, tn), F32)], sem=("parallel", "arbitrary"))(x, dy)


def _row_spec(width, col_block=0):
    return pl.BlockSpec((BLOCK, width), lambda i: (i, col_block))


def _x_spec():
    return pl.BlockSpec((None, BLOCK, D_MODEL), lambda i: (0, jnp.maximum(i - 1, 0), 0))


def prep(x, meta, g_pre):
    nb = x.shape[1] // BLOCK + 1

    def body(x_ref, meta_ref, g_ref, h_ref, u_ref):
        i = pl.program_id(0)

        @pl.when(i == 0)
        def _():
            h_ref[0:PAD_ROWS, :] = jnp.zeros((PAD_ROWS, D_MODEL), F32)
            h_ref[PAD_ROWS:BLOCK, :] = meta_ref[...]

        @pl.when(i > 0)
        def _():
            h_ref[...] = x_ref[...]

        u_ref[...] = _rms(h_ref[...], g_ref[...]).astype(BF16)

    return _call(body, name="prep", grid=(nb,), in_specs=[_x_spec(), _full((N_META, D_MODEL)), _full((1, D_MODEL))],
                 out_specs=[_row_spec(D_MODEL), _row_spec(D_MODEL)],
                 out_shape=[jax.ShapeDtypeStruct((nb * BLOCK, D_MODEL), F32),
                            jax.ShapeDtypeStruct((nb * BLOCK, D_MODEL), BF16)], sem=("parallel",))(x, meta, g_pre)


def prep_bwd(h, du, dres, g_pre):
    nb = h.shape[0] // BLOCK

    def body(h_ref, du_ref, dres_ref, g_ref, gx_ref, gm_ref, gg_ref):
        i = pl.program_id(0)
        _, vjp = jax.vjp(_rms, h_ref[...], g_ref[...])
        dh, dg = vjp(du_ref[...])

        @pl.when(i == 0)
        def _():
            gm_ref[...] = dh[PAD_ROWS:BLOCK, :]
            gg_ref[...] = dg

        @pl.when(i > 0)
        def _():
            gg_ref[...] += dg

        gx_ref[...] = dh + dres_ref[...]

    return _call(body, name="prep_bwd", grid=(nb,),
                 in_specs=[_row_spec(D_MODEL), _row_spec(D_MODEL), _row_spec(D_MODEL), _full((1, D_MODEL))],
                 out_specs=[_x_spec(), _full((N_META, D_MODEL)), _full((1, D_MODEL))],
                 out_shape=[jax.ShapeDtypeStruct((1, (nb - 1) * BLOCK, D_MODEL), F32),
                            jax.ShapeDtypeStruct((N_META, D_MODEL), F32), jax.ShapeDtypeStruct((1, D_MODEL), F32)],
                 sem=("arbitrary",))(h, du, dres, g_pre)


GROUP_W = SSM_INNER // SSM_GROUPS


def _gated_norm(y, z, g):
    t = y * _silu(z)
    return t * lax.rsqrt(jnp.mean(t * t, axis=-1, keepdims=True) + NORM_EPS) * g


def _gated_norm_groups(y, z, g):
    groups = [slice(k * GROUP_W, (k + 1) * GROUP_W) for k in range(SSM_GROUPS)]
    return jnp.concatenate([_gated_norm(y[:, s], z[:, s], g[:, s]) for s in groups], axis=1)


def _merge(ga, gs, ya, ys):
    return _sigmoid(ga) * ya + _sigmoid(gs) * ys


GATE_ATT_BLOCK = SEG["gate_att"][2] // D_MODEL
GATE_SSM_BLOCK = SEG["gate_ssm"][2] // D_MODEL


def _row_loss(out, g_post, x, target):
    diff = x + _rms(out, g_post) - target
    return 0.5 * jnp.sum(diff * diff) / D_MODEL


def tail(y_ssd, proj, a_att, x, target, woa, wos, wo, g_norm, g_post):
    nb = y_ssd.shape[0] // BLOCK
    rows = nb * BLOCK

    def body(y_ref, z_ref, ga_ref, gs_ref, a_ref, x_ref, t_ref, woa_ref, wos_ref, wo_ref, gn_ref, gp_ref,
             yn_ref, mg_ref, dout_ref, dya_ref, dys_ref, da_ref, dy_ref, dz_ref, dga_ref, dgs_ref, dres_ref,
             loss_ref, dgp_ref, dgn_ref):
        i = pl.program_id(0)
        yn, norm_vjp = jax.vjp(_gated_norm_groups, y_ref[...], z_ref[...], gn_ref[...])
        yn16 = yn.astype(BF16)
        y_ssm = jnp.dot(yn16, wos_ref[...], preferred_element_type=F32)
        y_att = jnp.dot(a_ref[...], woa_ref[...], preferred_element_type=F32)
        merged, merge_vjp = jax.vjp(_merge, ga_ref[...], gs_ref[...], y_att, y_ssm)
        merged16 = merged.astype(BF16)
        out = jnp.dot(merged16, wo_ref[...], preferred_element_type=F32)
        loss, loss_vjp = jax.vjp(_row_loss, out, gp_ref[...], x_ref[...], t_ref[...])
        counted = jnp.where(i > 0, 1.0, 0.0)
        dout, dgp, dres, _ = loss_vjp(counted)
        dout16 = dout.astype(BF16)
        dmerged = lax.dot_general(dout16, wo_ref[...], _NT, preferred_element_type=F32)
        dga, dgs, dya, dys = merge_vjp(dmerged)
        dya16, dys16 = dya.astype(BF16), dys.astype(BF16)
        dyn = lax.dot_general(dys16, wos_ref[...], _NT, preferred_element_type=F32)
        dy, dz, dgn = norm_vjp(dyn)

        yn_ref[...] = yn16
        mg_ref[...] = merged16
        dout_ref[...] = dout16
        dya_ref[...] = dya16
        dys_ref[...] = dys16
        da_ref[...] = lax.dot_general(dya16, woa_ref[...], _NT, preferred_element_type=F32)
        dy_ref[...] = dy
        dz_ref[...] = dz.astype(BF16)
        dga_ref[...] = dga.astype(BF16)
        dgs_ref[...] = dgs.astype(BF16)
        dres_ref[...] = dres

        @pl.when(i == 0)
        def _():
            loss_ref[...] = jnp.zeros_like(loss_ref)
            dgp_ref[...] = jnp.zeros_like(dgp_ref)
            dgn_ref[...] = jnp.zeros_like(dgn_ref)

        loss_ref[...] += loss * counted
        dgp_ref[...] += dgp
        dgn_ref[...] += dgn

    wide, narrow = _row_spec(SSM_INNER), _row_spec(D_MODEL)
    resident = pl.BlockSpec(memory_space=pltpu.VMEM)
    bf = lambda w: jax.ShapeDtypeStruct((rows, w), BF16)
    f32 = lambda w: jax.ShapeDtypeStruct((rows, w), F32)
    return _call(body, name="tail", grid=(nb,),
                 in_specs=[wide, wide, _row_spec(D_MODEL, GATE_ATT_BLOCK), _row_spec(D_MODEL, GATE_SSM_BLOCK), narrow,
                           _x_spec(), _x_spec(), resident, resident, resident, _full((1, SSM_INNER)),
                           _full((1, D_MODEL))],
                 out_specs=[wide, narrow, narrow, narrow, narrow, narrow, wide, wide, narrow, narrow, narrow,
                            _full((8, LANES)), _full((1, D_MODEL)), _full((1, SSM_INNER))],
                 out_shape=[bf(SSM_INNER), bf(D_MODEL), bf(D_MODEL), bf(D_MODEL), bf(D_MODEL), f32(D_MODEL),
                            f32(SSM_INNER), bf(SSM_INNER), bf(D_MODEL), bf(D_MODEL), f32(D_MODEL),
                            jax.ShapeDtypeStruct((8, LANES), F32), jax.ShapeDtypeStruct((1, D_MODEL), F32),
                            jax.ShapeDtypeStruct((1, SSM_INNER), F32)],
                 sem=("arbitrary",))(y_ssd, proj, proj, proj, a_att, x, target, woa, wos, wo, g_norm, g_post)


_NT = (((1,), (1,)), ((), ()))
ALIBI_SLOPES = tuple(2.0 ** (-8.0 * (h + 1) / ATT_Q_HEADS) for h in range(ATT_Q_HEADS))
KV_WIDTH = ATT_KV_HEADS * HEAD_DIM
Q_BLOCK = SEG["q"][2] // D_MODEL
Z_ATT_BLOCK = SEG["z_att"][2] // D_MODEL
K_BLOCK = SEG["k"][2] // KV_WIDTH
V_BLOCK = SEG["v"][2] // KV_WIDTH
META_ROW_BLOCK = PAD_ROWS // N_META


@jax.custom_vjp
def _swap_halves(x):
    return pltpu.roll(x, HEAD_DIM, 1)


_swap_halves.defvjp(lambda x: (pltpu.roll(x, HEAD_DIM, 1), None), lambda _, g: (pltpu.roll(g, HEAD_DIM, 1),))


def _both_halves(t, half):
    first = lax.broadcasted_iota(jnp.int32, t.shape, 1) < HEAD_DIM
    sw = _swap_halves(t)
    return jnp.where(first, t, sw) if half == 0 else jnp.where(first, sw, t)


def _attn_rows(q, z, kp, kc, vp, vc, km, vm, sinks, n):
    rows = ATT_GROUP * BLOCK
    i = lax.broadcasted_iota(jnp.int32, (rows, BLOCK), 0) & (BLOCK - 1)
    j = lax.broadcasted_iota(jnp.int32, (rows, BLOCK), 1)
    rel_c = (i - j).astype(F32)
    rel_p = rel_c + float(BLOCK)
    nv = jnp.zeros((rows, BLOCK), jnp.int32) + n
    ok_c = (i >= j) & (nv >= 1)
    ok_p = (j > i) & (nv >= 2)
    im = lax.broadcasted_iota(jnp.int32, (rows, N_META), 0) & (BLOCK - 1)
    jm = lax.broadcasted_iota(jnp.int32, (rows, N_META), 1)
    ok_m = ((jnp.zeros((rows, N_META), jnp.int32) + n) >= 1) | (im >= PAD_ROWS + jm)
    first = lax.broadcasted_iota(jnp.int32, (BLOCK, LANES), 1) < HEAD_DIM
    neg = -jnp.inf
    outs = []
    for kv in range(ATT_KV_HEADS):
        tile, half = divmod(kv, 2)
        lanes = slice(tile * LANES, (tile + 1) * LANES)
        kc2, kp2, km2 = (_both_halves(t[:, lanes], half).astype(BF16) for t in (kc, kp, km))
        vc2, vp2, vm2 = (_both_halves(t[:, lanes], half).astype(BF16) for t in (vc, vp, vm))
        qs, slope, sk = [], [], []
        for pair in range(ATT_GROUP // 2):
            c0 = (kv * ATT_GROUP + 2 * pair) * HEAD_DIM
            qp = q[:, c0:c0 + LANES] * HEAD_DIM ** -0.5
            qs += [jnp.where(first, qp, 0.0), jnp.where(first, 0.0, qp)]
        for g in range(ATT_GROUP):
            slope.append(jnp.full((BLOCK, 1), ALIBI_SLOPES[kv * ATT_GROUP + g], F32))
            sk.append(jnp.broadcast_to(sinks[kv * ATT_GROUP + g], (BLOCK, 1)))
        qs = jnp.concatenate(qs, axis=0).astype(BF16)
        slope = jnp.concatenate(slope, axis=0)
        sk = jnp.concatenate(sk, axis=0)
        sc = jnp.where(ok_c, lax.dot_general(qs, kc2, _NT, preferred_element_type=F32) - slope * rel_c, neg)
        sp = jnp.where(ok_p, lax.dot_general(qs, kp2, _NT, preferred_element_type=F32) - slope * rel_p, neg)
        sm = jnp.where(ok_m, lax.dot_general(qs, km2, _NT, preferred_element_type=F32), neg)
        mx = jnp.maximum(jnp.maximum(jnp.max(sc, axis=1, keepdims=True), jnp.max(sp, axis=1, keepdims=True)),
                         jnp.maximum(jnp.max(sm, axis=1, keepdims=True), sk))
        mx = lax.stop_gradient(mx)
        ec, ep, em, es = jnp.exp(sc - mx), jnp.exp(sp - mx), jnp.exp(sm - mx), jnp.exp(sk - mx)
        den = (es + jnp.sum(ec, axis=1, keepdims=True) + jnp.sum(ep, axis=1, keepdims=True)
               + jnp.sum(em, axis=1, keepdims=True))
        inv = 1.0 / den
        o = (jnp.dot((ec * inv).astype(BF16), vc2, preferred_element_type=F32)
             + jnp.dot((ep * inv).astype(BF16), vp2, preferred_element_type=F32)
             + jnp.dot((em * inv).astype(BF16), vm2, preferred_element_type=F32))
        for pair in range(ATT_GROUP // 2):
            r0 = 2 * pair * BLOCK
            outs.append(jnp.where(first, o[r0:r0 + BLOCK], o[r0 + BLOCK:r0 + 2 * BLOCK]))
    return jnp.concatenate(outs, axis=1) * _silu(z)


def _attn_specs(nb, steps_clamped):
    def blk(t):
        return jnp.minimum(t, nb - 1) if steps_clamped else t

    wide = lambda col: pl.BlockSpec((BLOCK, D_MODEL), lambda t: (blk(t), col))
    cur = lambda col: pl.BlockSpec((BLOCK, KV_WIDTH), lambda t: (blk(t), col))
    prev = lambda col: pl.BlockSpec((BLOCK, KV_WIDTH), lambda t: (jnp.maximum(blk(t) - 1, 0), col))
    meta = lambda col: pl.BlockSpec((N_META, KV_WIDTH), lambda t: (META_ROW_BLOCK, col))
    sinks = pl.BlockSpec((ATT_Q_HEADS, 1, 1), lambda t: (0, 0, 0))
    return [wide(Q_BLOCK), wide(Z_ATT_BLOCK), prev(K_BLOCK), cur(K_BLOCK), prev(V_BLOCK), cur(V_BLOCK),
            meta(K_BLOCK), meta(V_BLOCK), sinks]


def attn_fwd(proj, sinks):
    nb = proj.shape[0] // BLOCK

    def body(q_ref, z_ref, kp_ref, kc_ref, vp_ref, vc_ref, km_ref, vm_ref, sk_ref, o_ref):
        o_ref[...] = _attn_rows(q_ref[...], z_ref[...], kp_ref[...], kc_ref[...], vp_ref[...], vc_ref[...],
                                km_ref[...], vm_ref[...], tuple(sk_ref[h] for h in range(ATT_Q_HEADS)),
                                pl.program_id(0)).astype(BF16)

    return _call(body, name="attn_fwd", grid=(nb,), in_specs=_attn_specs(nb, False), out_specs=_row_spec(D_MODEL),
                 out_shape=jax.ShapeDtypeStruct((nb * BLOCK, D_MODEL), BF16), sem=("parallel",))(*([proj] * 8), sinks)


def attn_bwd(da, proj, sinks):
    nb = proj.shape[0] // BLOCK
    last = nb - 1
    wide = pl.BlockSpec((BLOCK, D_MODEL), lambda t: (jnp.minimum(t, last), 0))
    done = pl.BlockSpec((BLOCK, KV_WIDTH), lambda t: (jnp.maximum(t - 1, 0), 0))
    meta = _full((N_META, KV_WIDTH))
    par = _full((ATT_Q_HEADS, 1, 1))

    def body(da_ref, q_ref, z_ref, kp_ref, kc_ref, vp_ref, vc_ref, km_ref, vm_ref, sk_ref,
             dq_ref, dz_ref, dk_ref, dv_ref, dkm_ref, dvm_ref, dsk_ref, ck_ref, cv_ref):
        t = pl.program_id(0)

        @pl.when(t == 0)
        def _():
            ck_ref[...] = jnp.zeros_like(ck_ref)
            cv_ref[...] = jnp.zeros_like(cv_ref)
            dkm_ref[...] = jnp.zeros_like(dkm_ref)
            dvm_ref[...] = jnp.zeros_like(dvm_ref)
            dsk_ref[...] = jnp.zeros_like(dsk_ref)

        @pl.when(t < nb)
        def _():
            def f(q, z, kp, kc, vp, vc, km, vm, sk):
                return _attn_rows(q, z, kp, kc, vp, vc, km, vm, sk, t)

            _, vjp = jax.vjp(f, q_ref[...], z_ref[...], kp_ref[...], kc_ref[...], vp_ref[...], vc_ref[...],
                             km_ref[...], vm_ref[...], tuple(sk_ref[h] for h in range(ATT_Q_HEADS)))
            dq, dz, dkp, dkc, dvp, dvc, dkm, dvm, dsk = vjp(da_ref[...])
            dq_ref[...] = dq.astype(BF16)
            dz_ref[...] = dz.astype(BF16)
            for h in range(ATT_Q_HEADS):
                dsk_ref[h] += dsk[h]
            dk_ref[...] = ck_ref[...] + dkp
            dv_ref[...] = cv_ref[...] + dvp
            ck_ref[...] = dkc
            cv_ref[...] = dvc
            dkm_ref[...] += dkm
            dvm_ref[...] += dvm

        @pl.when(t == nb)
        def _():
            dk_ref[...] = ck_ref[...]
            dv_ref[...] = cv_ref[...]

    rows = nb * BLOCK
    return _call(body, name="attn_bwd", grid=(nb + 1,), in_specs=[wide] + _attn_specs(nb, True),
                 out_specs=[wide, wide, done, done, meta, meta, par],
                 out_shape=[jax.ShapeDtypeStruct((rows, D_MODEL), BF16), jax.ShapeDtypeStruct((rows, D_MODEL), BF16),
                            jax.ShapeDtypeStruct((rows, KV_WIDTH), F32), jax.ShapeDtypeStruct((rows, KV_WIDTH), F32),
                            jax.ShapeDtypeStruct((N_META, KV_WIDTH), F32), jax.ShapeDtypeStruct((N_META, KV_WIDTH), F32),
                            jax.ShapeDtypeStruct(sinks.shape, F32)],
                 scratch=[pltpu.VMEM((BLOCK, KV_WIDTH), F32), pltpu.VMEM((BLOCK, KV_WIDTH), F32)],
                 sem=("arbitrary",))(da, *([proj] * 8), sinks)


XBC_BLOCK0 = SEG["xbc"][2] // D_MODEL
CONV_COL_BLOCKS = CONV_DIM // D_MODEL
DT_TILE = SEG["dt"][2] // LANES


HALO = 8


def _conv_rows(length):
    return 544 if length % 544 == 0 else BLOCK


def _shift_rows(cur, before, j):
    if j == 0:
        return cur
    n = cur.shape[0]
    row = lax.broadcasted_iota(jnp.int32, cur.shape, 0)
    head = pltpu.roll(before, j, 0)
    if n > HALO:
        head = jnp.concatenate([head, jnp.zeros((n - HALO, cur.shape[1]), cur.dtype)], axis=0)
    return jnp.where(row >= j, pltpu.roll(cur, j, 0), head)


def _conv_pre(cur, before, w_ref, b_ref):
    pre = b_ref[...] + w_ref[CONV_WIDTH - 1:CONV_WIDTH, :] * cur
    for k in range(CONV_WIDTH - 1):
        pre = pre + w_ref[k:k + 1, :] * _shift_rows(cur, before, CONV_WIDTH - 1 - k)
    return pre


def _conv_specs(steps, rows, col0=0):
    first = XBC_BLOCK0 + col0
    halos = rows // HALO
    cur = pl.BlockSpec((rows, D_MODEL), lambda j, i: (i, first + j))
    before = pl.BlockSpec((HALO, D_MODEL), lambda j, i: (jnp.maximum(i * halos - 1, 0), first + j))
    after = pl.BlockSpec((HALO, D_MODEL), lambda j, i: (jnp.minimum(i + 1, steps - 1) * halos, first + j))
    return cur, before, after


def _valid_rows(i, rows):
    row = lax.broadcasted_iota(jnp.int32, (rows, D_MODEL), 0)
    return jnp.maximum((row >= PAD_ROWS).astype(F32), jnp.where(i > 0, 1.0, 0.0))


def conv_fwd(proj, conv_w, conv_b):
    rows = _conv_rows(proj.shape[0])
    steps = proj.shape[0] // rows
    cur, before, _ = _conv_specs(steps, rows)

    def body(c_ref, p_ref, w_ref, b_ref, o_ref):
        i = pl.program_id(1)
        pre = _conv_pre(c_ref[...], p_ref[...] * jnp.where(i > 0, 1.0, 0.0), w_ref, b_ref)
        o_ref[...] = _silu(pre) * _valid_rows(i, rows)

    return _call(body, name="conv_fwd", grid=(CONV_COL_BLOCKS, steps),
                 in_specs=[cur, before, pl.BlockSpec((CONV_WIDTH, D_MODEL), lambda j, i: (0, j)),
                           pl.BlockSpec((1, D_MODEL), lambda j, i: (0, j))],
                 out_specs=pl.BlockSpec((rows, D_MODEL), lambda j, i: (i, j)),
                 out_shape=jax.ShapeDtypeStruct((proj.shape[0], CONV_DIM), F32),
                 sem=("parallel", "parallel"))(proj, proj, conv_w, conv_b)


def conv_bwd(name, dparts, col0, proj, conv_w, conv_b):
    rows = _conv_rows(proj.shape[0])
    steps = proj.shape[0] // rows
    last = steps - 1
    ncol = sum(d.shape[1] for d in dparts) // D_MODEL
    np_ = len(dparts)
    cur, before, after = _conv_specs(steps, rows, col0)
    dcur = [pl.BlockSpec((rows, d.shape[1] // ncol), lambda j, i: (i, j)) for d in dparts]
    dafter = [pl.BlockSpec((HALO, d.shape[1] // ncol), lambda j, i: (jnp.minimum(i + 1, last) * (rows // HALO), j))
              for d in dparts]
    out_cur = pl.BlockSpec((rows, D_MODEL), lambda j, i: (i, j))
    wspec = pl.BlockSpec((CONV_WIDTH, D_MODEL), lambda j, i: (0, col0 + j))
    bspec = pl.BlockSpec((1, D_MODEL), lambda j, i: (0, col0 + j))
    wout = pl.BlockSpec((CONV_WIDTH, D_MODEL), lambda j, i: (0, j))
    bout = pl.BlockSpec((1, D_MODEL), lambda j, i: (0, j))

    def body(*refs):
        dc_refs, da_refs = refs[:np_], refs[np_:2 * np_]
        c_ref, p_ref, a_ref, w_ref, b_ref, du_ref, dw_ref, db_ref = refs[2 * np_:]
        i = pl.program_id(1)
        row = lax.broadcasted_iota(jnp.int32, (rows, D_MODEL), 0)
        curv = c_ref[...]
        beforev = p_ref[...] * jnp.where(i > 0, 1.0, 0.0)
        side_by_side = lambda rs: rs[0][...] if np_ == 1 else jnp.concatenate([r[...] for r in rs], axis=1)

        def dpre_of(pre, d):
            s = _sigmoid(pre)
            return d * (s * (1.0 + pre * (1.0 - s)))

        dp_c = dpre_of(_conv_pre(curv, beforev, w_ref, b_ref), side_by_side(dc_refs) * _valid_rows(i, rows))
        dp_a = dpre_of(_conv_pre(a_ref[...], curv[rows - HALO:], w_ref, b_ref),
                       side_by_side(da_refs) * jnp.where(i < last, 1.0, 0.0))
        du = w_ref[CONV_WIDTH - 1:CONV_WIDTH, :] * dp_c
        for j in range(1, CONV_WIDTH):
            tail = jnp.concatenate([jnp.zeros((rows - HALO, D_MODEL), F32), pltpu.roll(dp_a, HALO - j, 0)], axis=0)
            up = jnp.where(row < rows - j, pltpu.roll(dp_c, rows - j, 0), tail)
            du = du + w_ref[CONV_WIDTH - 1 - j:CONV_WIDTH - j, :] * up
        du_ref[...] = du.astype(BF16)

        @pl.when(i == 0)
        def _():
            dw_ref[...] = jnp.zeros_like(dw_ref)
            db_ref[...] = jnp.zeros_like(db_ref)

        for k in range(CONV_WIDTH):
            dw_ref[k:k + 1, :] += jnp.sum(dp_c * _shift_rows(curv, beforev, CONV_WIDTH - 1 - k), axis=0, keepdims=True)
        db_ref[...] += jnp.sum(dp_c, axis=0, keepdims=True)

    width = ncol * D_MODEL
    return _call(body, name=name, grid=(ncol, steps),
                 in_specs=dcur + dafter + [cur, before, after, wspec, bspec], out_specs=[out_cur, wout, bout],
                 out_shape=[jax.ShapeDtypeStruct((proj.shape[0], width), BF16),
                            jax.ShapeDtypeStruct((CONV_WIDTH, width), F32), jax.ShapeDtypeStruct((1, width), F32)],
                 sem=("parallel", "arbitrary"))(*dparts, *dparts, proj, proj, proj, conv_w, conv_b)


def _head_expand():
    e = np.zeros((LANES, SSM_INNER), np.float32)
    for h in range(SSM_HEADS):
        e[h, h * HEAD_DIM:(h + 1) * HEAD_DIM] = 1.0
    return jnp.asarray(e, dtype=BF16)


def _softplus(x):
    return jnp.maximum(x, 0.0) + jnp.log(1.0 + jnp.exp(-jnp.abs(x)))


def _bf16_parts(x):
    hi = x.astype(BF16)
    rest = x - hi.astype(F32)
    mid = rest.astype(BF16)
    return hi, mid, (rest - mid.astype(F32)).astype(BF16)


@jax.custom_vjp
def _times_01(x, m):
    return sum(jnp.dot(p, m, preferred_element_type=F32) for p in _bf16_parts(x))


def _times_01_bwd(m, g):
    return sum(lax.dot_general(p, m, _NT, preferred_element_type=F32) for p in _bf16_parts(g)), jnp.zeros_like(m)


_times_01.defvjp(lambda x, m: (_times_01(x, m), m), _times_01_bwd)


def _causal_ones():
    l = lax.broadcasted_iota(jnp.int32, (BLOCK, BLOCK), 0)
    s = lax.broadcasted_iota(jnp.int32, (BLOCK, BLOCK), 1)
    return (l >= s).astype(BF16)


@jax.custom_vjp
def _cumsum_rows(a):
    return sum(jnp.dot(_causal_ones(), p, preferred_element_type=F32) for p in _bf16_parts(a))


def _cumsum_rows_bwd(_, g):
    tn = (((0,), (0,)), ((), ()))
    return (sum(lax.dot_general(_causal_ones(), p, tn, preferred_element_type=F32) for p in _bf16_parts(g)),)


_cumsum_rows.defvjp(lambda a: (_cumsum_rows(a), None), _cumsum_rows_bwd)


def _ssd_heads(dt_tile, bias, alog, dsk, expand):
    dt = _softplus(dt_tile + bias)
    a = dt * (-jnp.exp(alog))
    one_row = lambda v: jnp.broadcast_to(v, (HALO, LANES))
    return _times_01(jnp.concatenate([dt, _cumsum_rows(a), one_row(jnp.sum(a, axis=0, keepdims=True)), one_row(dsk)],
                                     axis=0), expand)


HEADS_ROWS = 2 * BLOCK + 2 * HALO


def _ssd_group(xs, per_lane, bg, cg, state):
    l = lax.broadcasted_iota(jnp.int32, (BLOCK, BLOCK), 0)
    s = lax.broadcasted_iota(jnp.int32, (BLOCK, BLOCK), 1)
    causal = l >= s
    first_head = s < HEAD_DIM
    dtx, cs = per_lane[0:BLOCK], per_lane[BLOCK:2 * BLOCK]
    tot, dsk = per_lane[2 * BLOCK:2 * BLOCK + 1], per_lane[2 * BLOCK + HALO:2 * BLOCK + HALO + 1]
    bb, cb16 = bg.astype(BF16), cg.astype(BF16)
    cb = lax.dot_general(cb16, bb, _NT, preferred_element_type=F32)
    xr = xs * dtx
    y_diag = []
    for p in range(GROUP_W // LANES):
        lanes = slice(p * LANES, (p + 1) * LANES)
        c_pair = cs[:, lanes]
        c_swap = _swap_halves(c_pair)
        m = []
        for c_head in (jnp.where(first_head, c_pair, c_swap), jnp.where(first_head, c_swap, c_pair)):
            m.append(cb * jnp.exp(jnp.where(causal, c_head - c_head.T, -jnp.inf)))
        x_pair = xr[:, lanes]
        x_diag = jnp.concatenate([jnp.where(first_head, x_pair, 0.0), jnp.where(first_head, 0.0, x_pair)], axis=0)
        y_diag.append(jnp.dot(jnp.concatenate(m, axis=1).astype(BF16), x_diag.astype(BF16),
                              preferred_element_type=F32))
    st = lax.dot_general(bb, (xr * jnp.exp(tot - cs)).astype(BF16), (((0,), (0,)), ((), ())),
                         preferred_element_type=F32)
    new_state = state * jnp.exp(tot) + st
    y_off = jnp.dot(cb16, state.astype(BF16), preferred_element_type=F32) * jnp.exp(cs)
    return jnp.concatenate(y_diag, axis=1) + y_off + dsk * xs, new_state


BC_WIDTH = SSM_GROUPS * SSM_STATE


def _ssd_specs(chunk):
    xs = pl.BlockSpec((BLOCK, SSM_INNER), lambda c: (chunk(c), 0))
    dt = pl.BlockSpec((BLOCK, LANES), lambda c: (chunk(c), DT_TILE))
    expand = _full((LANES, SSM_INNER))
    b = pl.BlockSpec((BLOCK, BC_WIDTH), lambda c: (chunk(c), SSM_INNER // BC_WIDTH))
    cc = pl.BlockSpec((BLOCK, BC_WIDTH), lambda c: (chunk(c), SSM_INNER // BC_WIDTH + 1))
    par = _full((1, LANES))
    state = pl.BlockSpec((None, SSM_STATE, SSM_INNER), lambda c: (chunk(c), 0, 0))
    return xs, dt, expand, b, cc, par, state


def _group_lanes(g):
    return slice(g * GROUP_W, (g + 1) * GROUP_W), slice(g * SSM_STATE, (g + 1) * SSM_STATE)


def ssd_fwd(xbc, proj, expand, bias, alog, dsk):
    nb = xbc.shape[0] // BLOCK
    xs, dt, ex, b, cc, par, state = _ssd_specs(lambda c: c)

    def body(x_ref, dt_ref, e_ref, bi_ref, al_ref, dk_ref, b_ref, c_ref, y_ref, sp_ref, st_ref):
        @pl.when(pl.program_id(0) == 0)
        def _():
            st_ref[...] = jnp.zeros_like(st_ref)

        per_lane = _ssd_heads(dt_ref[...], bi_ref[...], al_ref[...], dk_ref[...], e_ref[...])
        for g in range(SSM_GROUPS):
            wide, tile = _group_lanes(g)
            entering = st_ref[:, wide]
            sp_ref[:, wide] = entering
            y_ref[:, wide], st_ref[:, wide] = _ssd_group(x_ref[:, wide], per_lane[:, wide], b_ref[:, tile],
                                                         c_ref[:, tile], entering)

    return _call(body, name="ssd_fwd", grid=(nb,), in_specs=[xs, dt, ex, par, par, par, b, cc],
                 out_specs=[xs, state],
                 out_shape=[jax.ShapeDtypeStruct((nb * BLOCK, SSM_INNER), F32),
                            jax.ShapeDtypeStruct((nb, SSM_STATE, SSM_INNER), F32)],
                 scratch=[pltpu.VMEM((SSM_STATE, SSM_INNER), F32)],
                 sem=("arbitrary",))(xbc, proj, expand, bias, alog, dsk, xbc, xbc)


def ssd_bwd(dy, xbc, proj, expand, bias, alog, dsk, states, comm):
    nb = xbc.shape[0] // BLOCK
    last = nb - 1
    xs, dt, ex, b, cc, par, state = _ssd_specs(lambda c: last - c)
    tile = pl.BlockSpec((BLOCK, LANES), lambda c: (last - c, 0))
    nspec = pl.BlockSpec((BLOCK, BC_WIDTH), lambda c: (last - c, 0))

    def body(dy_ref, x_ref, dt_ref, e_ref, bi_ref, al_ref, dk_ref, b_ref, c_ref, sp_ref,
             dx_ref, ddt_ref, db_ref, dc_ref, dbi_ref, dal_ref, ddk_ref, ds_ref):
        @pl.when(pl.program_id(0) == 0)
        def _():
            ds_ref[...] = jnp.zeros_like(ds_ref)
            dbi_ref[...] = jnp.zeros_like(dbi_ref)
            dal_ref[...] = jnp.zeros_like(dal_ref)
            ddk_ref[...] = jnp.zeros_like(ddk_ref)

        expand = e_ref[...]
        per_lane, heads_vjp = jax.vjp(lambda t, bi, al, dk: _ssd_heads(t, bi, al, dk, expand), dt_ref[...], bi_ref[...],
                                      al_ref[...], dk_ref[...])
        d_per_lane = []
        for g in range(SSM_GROUPS):
            wide, tile_lanes = _group_lanes(g)
            _, vjp = jax.vjp(_ssd_group, x_ref[:, wide], per_lane[:, wide], b_ref[:, tile_lanes], c_ref[:, tile_lanes],
                             sp_ref[:, wide])
            (dx_ref[:, wide], d_lanes, db_ref[:, tile_lanes], dc_ref[:, tile_lanes],
             ds_ref[:, wide]) = vjp((dy_ref[:, wide], ds_ref[:, wide]))
            d_per_lane.append(d_lanes)
        ddt, dbi, dal, ddk = heads_vjp(jnp.concatenate(d_per_lane, axis=1))
        ddt_ref[...] = ddt.astype(BF16)
        dbi_ref[...] += dbi
        dal_ref[...] += dal
        ddk_ref[...] += ddk

    def at():
        c = pl.program_id(0)
        return c == 0, c == 0, c == last

    par_shape = jax.ShapeDtypeStruct((1, LANES), F32)
    return _call_with_comm(
        body, comm, at, (dy, xbc, proj, expand, bias, alog, dsk, xbc, xbc, states), name="ssd_bwd",
        grid=(nb,), in_specs=[xs, xs, dt, ex, par, par, par, b, cc, state],
        out_specs=[xs, tile, nspec, nspec, par, par, par],
        out_shape=[jax.ShapeDtypeStruct((nb * BLOCK, SSM_INNER), F32), jax.ShapeDtypeStruct((nb * BLOCK, LANES), BF16),
                   jax.ShapeDtypeStruct((nb * BLOCK, BC_WIDTH), F32), jax.ShapeDtypeStruct((nb * BLOCK, BC_WIDTH), F32),
                   par_shape, par_shape, par_shape],
        scratch=[pltpu.VMEM((SSM_STATE, SSM_INNER), F32)])


SLAB_ROWS = 16
SLAB_META_ROW = 8
SLAB_META_SHAPE = (8, 2 * D_MODEL)
SLAB_LOSS_ROW = 7


def pack_small(dcw, dcb, dgpre, dgpost, dbias, dalog, ddsk, dsinks, dgn, dmeta, loss_tile):
    def body(cw, cb, gpre, gpost, dtb, al, dk, sk, gn, meta, loss, o_ref):
        o_ref[...] = jnp.zeros_like(o_ref)
        o_ref[SLAB_LOSS_ROW:SLAB_LOSS_ROW + 1, 0:LANES] = loss[0:1, :]
        o_ref[0:CONV_WIDTH, :] = cw[...]
        o_ref[4:5, :] = cb[...]
        o_ref[5:6, 0:1024] = gpre[...]
        o_ref[5:6, 1024:2048] = gpost[...]
        o_ref[5:6, 2048:2176] = dtb[...]
        o_ref[5:6, 2176:2304] = al[...]
        o_ref[5:6, 2304:2432] = dk[...]
        o_ref[5:6, 2432:2560] = sk[...]
        o_ref[6:7, 0:SSM_INNER] = gn[...]
        o_ref[SLAB_META_ROW:SLAB_ROWS, 0:SLAB_META_SHAPE[1]] = meta[...]

    args = (dcw, dcb, dgpre, dgpost, dbias, dalog, ddsk, dsinks, dgn, dmeta, loss_tile)
    return _call(body, name="pack_small", in_specs=[_full(a.shape) for a in args],
                 out_specs=_full((SLAB_ROWS, CONV_DIM)), out_shape=jax.ShapeDtypeStruct((SLAB_ROWS, CONV_DIM), F32))(*args)


def _lane_tile(v):
    return jnp.pad(v, ((0, 0), (0, LANES - v.shape[1])))


def kernel(x, meta_tokens, g_pre, w_in, conv_w, conv_b, dt_bias, a_log, d_skip, attn_sinks, g_ssm_norm, w_out_att, w_out_ssm, w_out, g_post, loss_target, m_meta_tokens, m_g_pre, m_w_in, m_conv_w, m_conv_b, m_dt_bias, m_a_log, m_d_skip, m_attn_sinks, m_g_ssm_norm, m_w_out_att, m_w_out_ssm, m_w_out, m_g_post, v_meta_tokens, v_g_pre, v_w_in, v_conv_w, v_conv_b, v_dt_bias, v_a_log, v_d_skip, v_attn_sinks, v_g_ssm_norm, v_w_out_att, v_w_out_ssm, v_w_out, v_g_post):
    chip = _chip_index()

    conv_w_rows = jnp.pad(conv_w[0], ((0, 2 * 8 - CONV_WIDTH), (0, 0)))
    w_in_t, m_w_in_t, v_w_in_t = w_in[0].T, m_w_in[0].T, v_w_in[0].T
    gathered_w_in, g_conv_w, g_meta = run_comm("gather_w_in",
                                               TwoLevelGather([pack_w_in(w_in_t), conv_w_rows, meta_tokens]))
    cw_full = g_conv_w[:, :CONV_WIDTH].transpose(1, 0, 2).reshape(CONV_WIDTH, CONV_DIM)
    meta_full = g_meta.transpose(1, 0, 2).reshape(N_META, D_MODEL)
    behind_w_in = 0.0 * g_meta[0, 0, 0]
    w_out_flight, w_out_started = chip_exchange_start(
        "gather_w_out_start", [(w[0] + behind_w_in).astype(BF16) for w in (w_out_att, w_out_ssm, w_out)], False)

    h, u = prep(x, meta_full, g_pre)
    proj, w_all_t = project("in_proj", u, gathered_w_in, w_out_started)

    sinks3 = attn_sinks.reshape(ATT_Q_HEADS, 1, 1)
    a_att = attn_fwd(proj, sinks3)

    xbc = conv_fwd(proj, cw_full, conv_b)
    expand = _head_expand()
    head_pars = (_lane_tile(dt_bias), _lane_tile(a_log), _lane_tile(d_skip))
    y_ssd, states = ssd_fwd(xbc, proj, expand, *head_pars)
    woa, wos, wo = [g.reshape(-1, D_MODEL) for g in chip_exchange_wait("gather_w_out_wait", w_out_flight, False, y_ssd)]

    (yn, merged, dout, dy_att, dy_ssm, da_att, dy_ssd, dz_ssm, dga, dgs, dres, loss_tile, dg_post, dgn) = tail(
        y_ssd, proj, a_att, x, loss_target, woa, wos, wo, g_ssm_norm, g_post)

    dwo = mm_tn("out_proj_dw", merged, dout)
    dwoa = mm_tn("att_out_dw", a_att, dy_att)
    dwos = mm_tn("ssm_out_dw", yn, dy_ssm)
    dq, dz_att, dk, dv, dkmeta, dvmeta, dsinks3 = attn_bwd(da_att, proj, sinks3)
    dk = dk.at[PAD_ROWS:BLOCK].add(dkmeta).astype(BF16)
    dv = dv.at[PAD_ROWS:BLOCK].add(dvmeta).astype(BF16)

    def pieces(g):
        return g.reshape(4, 2, g.shape[0] // 8, g.shape[1])

    def to_owner(g):
        return (lambda ref, dev: ref.at[_chip_of(dev), dev[2]], (g.shape[0] // 8, g.shape[1]))

    (dxs, ddt_tile, dbg, dcg, dbias, dalog, ddsk), sent_w_out = ssd_bwd(
        dy_ssd, xbc, proj, expand, *head_pars, states,
        DirectExchange([pieces(dwoa), pieces(dwos), pieces(dwo)], [to_owner(dwoa), to_owner(dwos), to_owner(dwo)],
                       ALL_MASKS, "dev", 8))
    dxs_raw, dcw_xs, dcb_xs = conv_bwd("conv_bwd_x", [dxs], 0, proj, cw_full, conv_b)
    dbc_raw, dcw_bc, dcb_bc = conv_bwd("conv_bwd_bc", [dbg, dcg], SSM_INNER // D_MODEL, proj, cw_full, conv_b)
    dcw = jnp.concatenate([dcw_xs, dcw_bc], axis=1)
    dcb = jnp.concatenate([dcb_xs, dcb_bc], axis=1)

    narrow = jnp.concatenate([dk, dv, ddt_tile, jnp.zeros((dk.shape[0], N_ACT - N_ALIGNED), BF16)], axis=1)
    dproj = [dz_ssm, dxs_raw, dbc_raw, dq, dz_att, dga, dgs, narrow]
    half_rows = PACK_W // 2
    partial = weight_grad_t("in_proj_dw", dproj, u).reshape(4, 2, half_rows, D_MODEL)
    from_sibling, = run_comm("pair_grads", DirectExchange(
        [partial], [(lambda ref, dev: ref.at[pl.ds(0, 4), dev[2]], (4, half_rows, D_MODEL))], SIBLING_MASK, "core", 2,
        keep_own=False))
    chip_sum = sum_pair(partial, from_sibling)
    grads_flight, started = chip_exchange_start("reduce_w_in_start", [chip_sum], True)
    du = project_back("in_proj_dx", dproj, w_all_t, started)
    grad_x, dmeta, dg_pre = prep_bwd(h, du, dres, g_pre)

    halves = [sum_slots("sum_" + nm, r) for nm, r in zip(("w_out_att", "w_out_ssm", "w_out"), sent_w_out)]
    shared = run_comm("share_w_out", DirectExchange(halves, [None] * 3, SIBLING_MASK, "core", 2))
    g_woa, g_wos, g_wo = [f.reshape(2 * f.shape[1], f.shape[2]) for f in shared]
    d_woa, nm_woa, nv_woa = adamw_rows("adamw_w_out_att", g_woa, w_out_att[0], m_w_out_att[0], v_w_out_att[0])
    d_wos, nm_wos, nv_wos = adamw_rows("adamw_w_out_ssm", g_wos, w_out_ssm[0], m_w_out_ssm[0], v_w_out_ssm[0])
    d_wo, nm_wo, nv_wo = adamw_rows("adamw_w_out", g_wo, w_out[0], m_w_out[0], v_w_out[0])

    sent_w_in, = chip_exchange_wait("reduce_w_in_wait", grads_flight, True, d_wos)
    slab = pack_small(dcw, dcb, dg_pre, dg_post, dbias, dalog, ddsk, _lane_tile(dsinks3.reshape(1, ATT_Q_HEADS)), dgn,
                      dmeta.reshape(SLAB_META_SHAPE), loss_tile)
    shared_w_in, slabs = run_comm("share_w_in", Both(
        DirectExchange([sum_slots("sum_w_in", sent_w_in)], [None], SIBLING_MASK, "core", 2),
        DirectExchange([slab], [None], ALL_MASKS, "dev", 8)))
    small = sum_slots("sum_small", slabs)
    loss = small[SLAB_LOSS_ROW, 0]
    g_w_in, d_w_in, nm_w_in, nv_w_in = [
        a.T for a in adamw_w_in(shared_w_in.reshape(PACK_W, D_MODEL), w_in_t, m_w_in_t, v_w_in_t)]

    cw_cols = CONV_DIM // 4
    meta_cols = D_MODEL // 4
    g_small = {
        "meta_tokens": lax.dynamic_slice(
            small[SLAB_META_ROW:SLAB_ROWS, 0:SLAB_META_SHAPE[1]].reshape(N_META, D_MODEL), (0, chip * meta_cols),
            (N_META, meta_cols)),
        "g_pre": small[5:6, 0:1024],
        "conv_w": lax.dynamic_slice(small, (0, chip * cw_cols), (CONV_WIDTH, cw_cols)),
        "conv_b": small[4:5, :],
        "dt_bias": small[5:6, 2048:2048 + SSM_HEADS],
        "a_log": small[5:6, 2176:2176 + SSM_HEADS],
        "d_skip": small[5:6, 2304:2304 + SSM_HEADS],
        "attn_sinks": small[5:6, 2432:2432 + ATT_Q_HEADS],
        "g_ssm_norm": small[6:7, 0:SSM_INNER],
        "g_post": small[5:6, 1024:2048],
    }
    names = list(g_small)
    w_small = dict(meta_tokens=meta_tokens, g_pre=g_pre, conv_w=conv_w[0], conv_b=conv_b, dt_bias=dt_bias, a_log=a_log,
                   d_skip=d_skip, attn_sinks=attn_sinks, g_ssm_norm=g_ssm_norm, g_post=g_post)
    m_small = dict(meta_tokens=m_meta_tokens, g_pre=m_g_pre, conv_w=m_conv_w[0], conv_b=m_conv_b, dt_bias=m_dt_bias,
                   a_log=m_a_log, d_skip=m_d_skip, attn_sinks=m_attn_sinks, g_ssm_norm=m_g_ssm_norm, g_post=m_g_post)
    v_small = dict(meta_tokens=v_meta_tokens, g_pre=v_g_pre, conv_w=v_conv_w[0], conv_b=v_conv_b, dt_bias=v_dt_bias,
                   a_log=v_a_log, d_skip=v_d_skip, attn_sinks=v_attn_sinks, g_ssm_norm=v_g_ssm_norm, g_post=v_g_post)
    upd = dict(zip(names, adamw_small([g_small[k] for k in names], [w_small[k] for k in names],
                                      [m_small[k] for k in names], [v_small[k] for k in names])))

    lead = {"conv_w"}

    def shaped(name, a):
        return a[None] if name in lead else a

    grads = dict(g_small, w_in=g_w_in, w_out_att=g_woa, w_out_ssm=g_wos, w_out=g_wo)
    deltas = dict({k: upd[k][0] for k in names}, w_in=d_w_in, w_out_att=d_woa, w_out_ssm=d_wos, w_out=d_wo)
    new_m = dict({k: upd[k][1] for k in names}, w_in=nm_w_in, w_out_att=nm_woa, w_out_ssm=nm_wos, w_out=nm_wo)
    new_v = dict({k: upd[k][2] for k in names}, w_in=nv_w_in, w_out_att=nv_woa, w_out_ssm=nv_wos, w_out=nv_wo)
    lead |= {"w_in", "w_out_att", "w_out_ssm", "w_out"}
    order = ["meta_tokens", "g_pre", "w_in", "conv_w", "conv_b", "dt_bias", "a_log", "d_skip", "attn_sinks",
             "g_ssm_norm", "w_out_att", "w_out_ssm", "w_out", "g_post"]
    outs = [loss, grad_x]
    for group in (grads, deltas, new_m, new_v):
        outs += [shaped(k, group[k]) for k in order]
    return tuple(outs)
```

```python
import numpy as np
import jax
import jax.numpy as jnp
from jax import lax
from jax.experimental import pallas as pl
from jax.experimental.pallas import tpu as pltpu

F32 = jnp.float32
BF16 = jnp.bfloat16

D_MODEL = 1024
N_META = 16
BLOCK = 128
PAD_ROWS = BLOCK - N_META
NORM_EPS = 1e-6
HEAD_DIM = 64
ATT_Q_HEADS = 16
ATT_KV_HEADS = 4
ATT_GROUP = 4
SSM_INNER = 2048
SSM_HEADS = 32
SSM_GROUPS = 4
SSM_STATE = 128
CONV_WIDTH = 4
CONV_DIM = 3072
LANES = 128

ADAM_LR = 0.001
ADAM_B1 = 0.9
ADAM_B2 = 0.999
ADAM_EPS = 1e-08
ADAM_WD = 0.01
ADAM_STEP = 10

VMEM_LIMIT = 48 * 1024 * 1024

SHARD_W = 2440
PACK_W = 2560
SHARD_STRIDE = 2432
N_ALIGNED = 9856
N_ACT = 10240
SEG = {
    "q": (0, 1024, 5120), "k": (1024, 256, 9216), "v": (1280, 256, 9472), "z_att": (1536, 1024, 6144),
    "z_ssm": (2560, 2048, 0), "xbc": (4608, 3072, 2048), "dt": (7680, 128, 9728),
    "gate_att": (7808, 1024, 7168), "gate_ssm": (8832, 1024, 8192),
}
DT_STORED_START = 7680
DT_PAD = LANES - SSM_HEADS


def _act_col(aligned_col):
    for a0, w, p0 in SEG.values():
        if a0 <= aligned_col < a0 + w:
            return p0 + aligned_col - a0
    raise ValueError(aligned_col)


def _call(body, *, name, out_shape, in_specs, out_specs, grid=(), scratch=(), sem=None, aliases=None):
    return pl.pallas_call(
        body, out_shape=out_shape, grid=grid, in_specs=in_specs, out_specs=out_specs, scratch_shapes=list(scratch),
        name=name, input_output_aliases=aliases or {},
        compiler_params=pltpu.CompilerParams(dimension_semantics=sem, vmem_limit_bytes=VMEM_LIMIT))


def _full(shape):
    n = len(shape)
    return pl.BlockSpec(shape, lambda *_: (0,) * n)


def _chip_index():
    return lax.axis_index("x") * 2 + lax.axis_index("y")


_sigmoid = jax.nn.sigmoid


def _silu(z):
    return z * _sigmoid(z)


def _rms(x, g):
    return x * lax.rsqrt(jnp.mean(x * x, axis=-1, keepdims=True) + NORM_EPS) * g


def _peer(mask):
    x, y, c = lax.axis_index("x"), lax.axis_index("y"), lax.axis_index("c")
    return ((1 - x) if mask & 4 else x, (1 - y) if mask & 2 else y, (1 - c) if mask & 1 else c)


def _me():
    return lax.axis_index("x"), lax.axis_index("y"), lax.axis_index("c")


def _chip_of(dev):
    return 2 * dev[0] + dev[1]


CHIP_MASKS = (4, 2, 6)
ALL_MASKS = (1, 2, 3, 4, 5, 6, 7)
SIBLING_MASK = (1,)


def _remote(src, dst, send_sem, recv_sem, dev):
    return pltpu.make_async_remote_copy(src_ref=src, dst_ref=dst, send_sem=send_sem, recv_sem=recv_sem,
                                        device_id=dev, device_id_type=pl.DeviceIdType.MESH)


class _StagedCopy:
    def __init__(self, src, stage, dst, load_sem, store_sem):
        self.load = pltpu.make_async_copy(src, stage, load_sem)
        self.store = pltpu.make_async_copy(stage, dst, store_sem)

    def start(self):
        self.load.start()
        self.load.wait()
        self.store.start()

    def wait(self):
        self.store.wait()


class DirectExchange:
    def __init__(self, arrays, pieces, masks, slot_kind, nslots, keep_own=True):
        self.arrays, self.pieces, self.masks, self.slot_kind = list(arrays), list(pieces), masks, slot_kind
        self.keep_own = keep_own
        n, nk = len(arrays), len(masks)
        shapes = [a.shape if p is None else p[1] for a, p in zip(arrays, pieces)]
        self.out_shape = [jax.ShapeDtypeStruct((nslots,) + tuple(s), a.dtype) for s, a in zip(shapes, arrays)]
        self.scratch = [pltpu.SemaphoreType.DMA((n * nk,)), pltpu.SemaphoreType.DMA((n * nk,))]
        if keep_own:
            self.scratch += [pltpu.SemaphoreType.DMA((2 * n,))] + [pltpu.VMEM(s, a.dtype) for s, a in zip(shapes, arrays)]
        self.has_mid = False

    def _copies(self, ins, outs, scratch):
        send_sems, recv_sems = scratch[:2]
        me = _me()
        slot = {"chip": _chip_of(me), "dev": 4 * me[0] + 2 * me[1] + me[2], "core": me[2]}[self.slot_kind]
        nk = len(self.masks)

        def piece(a, dev):
            return ins[a] if self.pieces[a] is None else self.pieces[a][0](ins[a], dev)

        local = []
        if self.keep_own:
            local_sems, stages = scratch[2], scratch[3:]
            local = [_StagedCopy(piece(a, me), stages[a], outs[a].at[slot], local_sems.at[2 * a], local_sems.at[2 * a + 1])
                     for a in range(len(ins))]
        remote = []
        for a in range(len(ins)):
            for ki, mask in enumerate(self.masks):
                dev = _peer(mask)
                remote.append(_remote(piece(a, dev), outs[a].at[slot], send_sems.at[a * nk + ki],
                                      recv_sems.at[a * nk + ki], dev))
        return local, remote

    def start(self, ins, outs, scratch):
        local, remote = self._copies(ins, outs, scratch)
        for cp in remote + local:
            cp.start()

    def finish(self, ins, outs, scratch):
        local, remote = self._copies(ins, outs, scratch)
        for cp in remote + local:
            cp.wait()


SPLIT_ROWS = 16


class TwoLevelGather:
    def __init__(self, arrays):
        self.arrays = list(arrays)
        n = len(arrays)
        self.out_shape = [jax.ShapeDtypeStruct((4,) + a.shape, a.dtype) for a in arrays]
        self.scratch = ([pltpu.SemaphoreType.DMA((4 * n,)), pltpu.SemaphoreType.DMA((4 * n,)),
                         pltpu.SemaphoreType.DMA((3 * n,)), pltpu.SemaphoreType.DMA((3 * n,)),
                         pltpu.SemaphoreType.DMA((2 * n,))] + [pltpu.VMEM(a.shape, a.dtype) for a in arrays])
        self.has_mid = True

    def _copies(self, ins, outs, scratch):
        ici_send, ici_recv, fwd_send, fwd_recv, local_sems = scratch[:5]
        stages = scratch[5:]
        me = _me()
        sibling, in_x, in_y, diagonal = _peer(1), _peer(4), _peer(2), _peer(6)
        plan = []
        for a in range(len(ins)):
            half = ins[a].shape[0] // 2
            first = half // 2 if half % (2 * SPLIT_ROWS) == 0 else half
            mine = pl.ds(me[2] * half, half)
            local = _StagedCopy(ins[a], stages[a], outs[a].at[_chip_of(me)], local_sems.at[2 * a], local_sems.at[2 * a + 1])

            def ici(k, src, dst, dev):
                return _remote(src, dst, ici_send.at[4 * a + k], ici_recv.at[4 * a + k], dev)

            def d2d(k, chip):
                zone = outs[a].at[_chip_of(chip), mine]
                return _remote(zone, zone, fwd_send.at[3 * a + k], fwd_recv.at[3 * a + k], sibling)

            own_zone = outs[a].at[_chip_of(me), mine]
            from_x = outs[a].at[_chip_of(in_x), pl.ds(me[2] * half, first)]
            onward = [ici(2, from_x, from_x, in_y), None]
            if first < half:
                from_y = outs[a].at[_chip_of(in_y), pl.ds(me[2] * half + first, half - first)]
                onward[1] = ici(3, from_y, from_y, in_x)
            plan.append(dict(
                local=local,
                own=[ici(0, ins[a].at[mine], own_zone, in_x), ici(1, ins[a].at[mine], own_zone, in_y)],
                onward=onward, sibling=[d2d(0, in_x), d2d(1, in_y), d2d(2, diagonal)]))
        return plan

    def start(self, ins, outs, scratch):
        for p in self._copies(ins, outs, scratch):
            for cp in p["own"]:
                cp.start()
            p["local"].start()

    def mid(self, ins, outs, scratch):
        plan = self._copies(ins, outs, scratch)
        for p in plan:
            for k in range(2):
                p["own"][k].wait_recv()
                if p["onward"][k] is not None:
                    p["onward"][k].start()
                p["sibling"][k].start()
        for p in plan:
            for cp in p["onward"]:
                if cp is not None:
                    cp.wait_recv()
            p["sibling"][2].start()

    def finish(self, ins, outs, scratch):
        for p in self._copies(ins, outs, scratch):
            for cp in p["sibling"]:
                cp.wait_recv()
            for cp in p["own"] + p["sibling"] + [cp for cp in p["onward"] if cp is not None]:
                cp.wait_send()
            p["local"].wait()


class Both:
    def __init__(self, a, b):
        self.a, self.b = a, b
        self.arrays, self.out_shape = a.arrays + b.arrays, a.out_shape + b.out_shape
        self.scratch = a.scratch + b.scratch
        self.has_mid = False
        assert not (a.has_mid or b.has_mid)

    def _parts(self, ins, outs, sems):
        na, sa = len(self.a.arrays), len(self.a.scratch)
        return (ins[:na], outs[:na], sems[:sa]), (ins[na:], outs[na:], sems[sa:])

    def start(self, ins, outs, sems):
        pa, pb = self._parts(ins, outs, sems)
        self.a.start(*pa)
        self.b.start(*pb)

    def finish(self, ins, outs, sems):
        pa, pb = self._parts(ins, outs, sems)
        self.a.finish(*pa)
        self.b.finish(*pb)


_ANY = pl.BlockSpec(memory_space=pl.ANY)


def run_comm(name, comm):
    n = len(comm.arrays)

    def body(*refs):
        ins, outs, sems = refs[:n], refs[n:2 * n], refs[2 * n:]
        comm.start(ins, outs, sems)
        if comm.has_mid:
            comm.mid(ins, outs, sems)
        comm.finish(ins, outs, sems)

    return pl.pallas_call(body, name=name, out_shape=comm.out_shape, in_specs=[_ANY] * n, out_specs=[_ANY] * n,
                          scratch_shapes=comm.scratch,
                          compiler_params=pltpu.CompilerParams(vmem_limit_bytes=VMEM_LIMIT))(*comm.arrays)


_HBM = pl.BlockSpec(memory_space=pltpu.HBM)
_SEM = pl.BlockSpec(memory_space=pltpu.SEMAPHORE)
_SIDE_EFFECT = pltpu.SideEffectType.DATAFLOW_SIDE_EFFECTING


def _chip_copies(srcs, lands, send_sems, recv_sems, by_target):
    me = _me()
    copies = []
    for a, (src, land) in enumerate(zip(srcs, lands)):
        for ki, mask in enumerate(CHIP_MASKS):
            dev = _peer(mask)
            k = a * len(CHIP_MASKS) + ki
            piece = src.at[_chip_of(dev)] if by_target else src
            copies.append(_remote(piece, land.at[_chip_of(me)], send_sems.at[k], recv_sems.at[k], dev))
    return copies


def chip_exchange_start(name, arrays, by_target):
    n = len(arrays)
    nsem = n * len(CHIP_MASKS)
    piece_shapes = [a.shape[1:] if by_target else a.shape for a in arrays]

    def body(*refs):
        srcs, lands = refs[:n], refs[n:2 * n]
        send_sems, recv_sems = refs[2 * n:2 * n + 2]
        token = refs[4 * n + 2]
        stages, local_sems = refs[4 * n + 3:5 * n + 3], refs[5 * n + 3]
        me = _me()
        for cp in _chip_copies(srcs, lands, send_sems, recv_sems, by_target):
            cp.start()
        own = [_StagedCopy(srcs[a].at[_chip_of(me)] if by_target else srcs[a], stages[a], lands[a].at[_chip_of(me)],
                           local_sems.at[2 * a], local_sems.at[2 * a + 1]) for a in range(n)]
        for cp in own:
            cp.load.start()
        for cp in own:
            cp.load.wait()
            cp.store.start()
        for cp in own:
            cp.store.wait()
        token[...] = jnp.zeros_like(token)

    lands = [lax.empty((4,) + tuple(s), a.dtype) for s, a in zip(piece_shapes, arrays)]
    hbm = lambda a: pltpu.HBM(a.shape, a.dtype)
    res = pl.pallas_call(
        body, name=name,
        out_shape=(pltpu.SemaphoreType.DMA((nsem,)), pltpu.SemaphoreType.DMA((nsem,)), *[hbm(a) for a in arrays],
                   *[hbm(l) for l in lands], jax.ShapeDtypeStruct((8, LANES), F32)),
        in_specs=(_HBM,) * (2 * n), out_specs=(_SEM, _SEM) + (_HBM,) * (2 * n) + (pl.BlockSpec(memory_space=pltpu.VMEM),),
        input_output_aliases={i: i + 2 for i in range(2 * n)},
        scratch_shapes=[pltpu.VMEM(tuple(s), a.dtype) for s, a in zip(piece_shapes, arrays)]
        + [pltpu.SemaphoreType.DMA((2 * n,))],
        compiler_params=pltpu.CompilerParams(has_side_effects=_SIDE_EFFECT, vmem_limit_bytes=VMEM_LIMIT),
    )(*[pltpu.with_memory_space_constraint(a, pltpu.HBM) for a in arrays + lands])
    return res[:-1], res[-1]


def chip_exchange_wait(name, in_flight, by_target, after):
    send_sems, recv_sems, *thru = in_flight
    n = len(thru) // 2

    def body(*refs):
        srcs, lands, (send, recv) = refs[:n], refs[n:2 * n], refs[2 * n:2 * n + 2]
        for cp in _chip_copies(srcs, lands, send, recv, by_target):
            cp.wait_send()
            cp.wait_recv()

    return pl.pallas_call(
        body, name=name, out_shape=tuple(pltpu.HBM(t.shape, t.dtype) for t in thru),
        in_specs=(_HBM,) * (2 * n) + (_SEM, _SEM, pl.BlockSpec(memory_space=pl.ANY)), out_specs=(_HBM,) * (2 * n),
        input_output_aliases={i: i for i in range(2 * n)},
        compiler_params=pltpu.CompilerParams(has_side_effects=_SIDE_EFFECT),
    )(*thru, send_sems, recv_sems, after)[n:]


def _call_with_comm(body, comm, steps, args, *, name, out_shape, in_specs, out_specs, grid, scratch=()):
    ni, no, ns, nc = len(in_specs), len(out_specs), len(scratch), len(comm.arrays)

    def full_body(*refs):
        ins, cins = refs[:ni], refs[ni:ni + nc]
        outs, couts = refs[ni + nc:ni + nc + no], refs[ni + nc + no:ni + 2 * nc + no]
        scr, csems = refs[ni + 2 * nc + no:ni + 2 * nc + no + ns], refs[ni + 2 * nc + no + ns:]
        first, middle, last = steps()
        pl.when(first)(lambda: comm.start(cins, couts, csems))
        if comm.has_mid:
            pl.when(middle)(lambda: comm.mid(cins, couts, csems))
        body(*ins, *outs, *scr)
        pl.when(last)(lambda: comm.finish(cins, couts, csems))

    res = pl.pallas_call(
        full_body, name=name, out_shape=list(out_shape) + comm.out_shape, grid=grid,
        in_specs=list(in_specs) + [_ANY] * nc, out_specs=list(out_specs) + [_ANY] * nc,
        scratch_shapes=list(scratch) + comm.scratch,
        compiler_params=pltpu.CompilerParams(dimension_semantics=("arbitrary",) * len(grid),
                                             vmem_limit_bytes=VMEM_LIMIT))(*args, *comm.arrays)
    return res[:no], res[no:]


def _shard_pieces(chip):
    if chip < 3:
        return [(0, SHARD_W, 8 * chip)]
    behind_dt = DT_STORED_START + SSM_HEADS - 3 * SHARD_W
    return [(0, behind_dt, 24), (behind_dt, SHARD_W - behind_dt, behind_dt + 24 + DT_PAD)]


W_IN_COLS = 256


def pack_w_in(wt):
    def body(w_ref, o_ref, pad_ref):
        chip = _chip_index()
        pad_ref[...] = jnp.zeros_like(pad_ref)
        for cv in range(4):
            @pl.when(chip == cv)
            def _():
                for src, n, dst in _shard_pieces(cv):
                    pad_ref[dst:dst + n, :] = w_ref[src:src + n, :]
        o_ref[...] = pad_ref[...].astype(BF16)

    return _call(body, name="pack_w_in", grid=(D_MODEL // W_IN_COLS,),
                 in_specs=[pl.BlockSpec((SHARD_W, W_IN_COLS), lambda i: (0, i))],
                 out_specs=pl.BlockSpec((PACK_W, W_IN_COLS), lambda i: (0, i)),
                 out_shape=jax.ShapeDtypeStruct((PACK_W, D_MODEL), BF16),
                 scratch=[pltpu.VMEM((PACK_W, W_IN_COLS), F32)], sem=("parallel",))(wt)


def _tile_runs():
    runs, fix = [], []
    for t in range(N_ALIGNED // LANES):
        s = min(t // 19, 3)
        j = t - 19 * s
        p = _act_col(t * LANES)
        if runs and runs[-1][1] == s and runs[-1][0] + runs[-1][3] == p and runs[-1][2] + runs[-1][3] == j * LANES:
            runs[-1][3] += LANES
        else:
            runs.append([p, s, j * LANES, LANES])
        if j == 0 and s > 0:
            fix.append((p, s - 1))
    return runs, fix


def unpack_w_in(bg):
    runs, fix = _tile_runs()

    def body(b_ref, o_ref):
        for p, s, j, w in runs:
            o_ref[p:p + w, :] = b_ref[s, j:j + w, :]
        for p, s in fix:
            o_ref[p:p + LANES, :] = o_ref[p:p + LANES, :] + b_ref[s, SHARD_STRIDE:PACK_W, :]
        o_ref[N_ALIGNED:N_ACT, :] = jnp.zeros((N_ACT - N_ALIGNED, W_IN_COLS), BF16)

    return _call(body, name="unpack_w_in", grid=(D_MODEL // W_IN_COLS,),
                 in_specs=[pl.BlockSpec((4, PACK_W, W_IN_COLS), lambda i: (0, 0, i))],
                 out_specs=pl.BlockSpec((N_ACT, W_IN_COLS), lambda i: (0, i)),
                 out_shape=jax.ShapeDtypeStruct((N_ACT, D_MODEL), BF16), sem=("parallel",))(bg)


def _adamw(w, g, m, v):
    m = ADAM_B1 * m + (1.0 - ADAM_B1) * g
    v = ADAM_B2 * v + (1.0 - ADAM_B2) * jnp.square(g)
    m_hat = m / (1.0 - ADAM_B1 ** ADAM_STEP)
    v_hat = v / (1.0 - ADAM_B2 ** ADAM_STEP)
    delta = -ADAM_LR * (m_hat / (jnp.sqrt(v_hat) + ADAM_EPS) + ADAM_WD * w)
    return delta, m, v


def adamw_w_in(g_packed, wt, mt, vt):
    cols = LANES

    def body(g_ref, w_ref, m_ref, v_ref, go_ref, d_ref, mo_ref, vo_ref):
        chip = _chip_index()
        for cv in range(4):
            @pl.when(chip == cv)
            def _():
                for dst, n, src in _shard_pieces(cv):
                    go_ref[dst:dst + n, :] = g_ref[src:src + n, :]
        d_ref[...], mo_ref[...], vo_ref[...] = _adamw(w_ref[...], go_ref[...], m_ref[...], v_ref[...])

    spec = pl.BlockSpec((SHARD_W, cols), lambda i: (0, i))
    shp = jax.ShapeDtypeStruct((SHARD_W, D_MODEL), F32)
    return _call(body, name="adamw_w_in", grid=(D_MODEL // cols,),
                 in_specs=[pl.BlockSpec((PACK_W, cols), lambda i: (0, i)), spec, spec, spec],
                 out_specs=[spec] * 4, out_shape=[shp] * 4, sem=("parallel",))(g_packed, wt, mt, vt)


def adamw_rows(name, g, w, m, v):
    r, c = g.shape
    rows = min(r, BLOCK)

    def body(g_ref, w_ref, m_ref, v_ref, d_ref, mo_ref, vo_ref):
        d_ref[...], mo_ref[...], vo_ref[...] = _adamw(w_ref[...], g_ref[...], m_ref[...], v_ref[...])

    spec = pl.BlockSpec((rows, c), lambda i: (i, 0))
    shp = jax.ShapeDtypeStruct((r, c), F32)
    return _call(body, name=name, grid=(r // rows,), in_specs=[spec] * 4, out_specs=[spec] * 3, out_shape=[shp] * 3,
                 sem=("parallel",))(g, w, m, v)


def adamw_small(gs, ws, ms, vs):
    n = len(gs)

    def body(*refs):
        g, w, m, v = refs[:n], refs[n:2 * n], refs[2 * n:3 * n], refs[3 * n:4 * n]
        outs = refs[4 * n:]
        for i in range(n):
            d, mn, vn = _adamw(w[i][...], g[i][...], m[i][...], v[i][...])
            outs[3 * i][...] = d
            outs[3 * i + 1][...] = mn
            outs[3 * i + 2][...] = vn

    specs = [_full(a.shape) for a in gs]
    res = _call(body, name="adamw_small", in_specs=specs * 4,
                out_specs=[s for s in specs for _ in range(3)],
                out_shape=[jax.ShapeDtypeStruct(a.shape, F32) for a in gs for _ in range(3)])(*gs, *ws, *ms, *vs)
    return [tuple(res[3 * i:3 * i + 3]) for i in range(n)]


def sum_slots(name, r):
    s, rr, c = r.shape
    rows = min(rr, BLOCK)

    def body(r_ref, o_ref):
        acc = r_ref[0].astype(F32)
        for k in range(1, s):
            acc = acc + r_ref[k].astype(F32)
        o_ref[...] = acc

    return _call(body, name=name, grid=(rr // rows,), in_specs=[pl.BlockSpec((s, rows, c), lambda i: (0, i, 0))],
                 out_specs=pl.BlockSpec((rows, c), lambda i: (i, 0)), out_shape=jax.ShapeDtypeStruct((rr, c), F32),
                 sem=("parallel",))(r)


def sum_pair(partial, from_sibling):
    s, _, rr, cols = partial.shape
    rows = rr // 2

    def body(p_ref, r_ref, o_ref):
        c = lax.axis_index("c")
        o_ref[...] = (p_ref[c].astype(F32) + r_ref[1 - c].astype(F32)).astype(BF16)

    return _call(body, name="sum_pair", grid=(s, rr // rows),
                 in_specs=[pl.BlockSpec((None, 2, rows, cols), lambda k, i: (k, 0, i, 0)),
                           pl.BlockSpec((2, None, rows, cols), lambda k, i: (0, k, i, 0))],
                 out_specs=pl.BlockSpec((None, rows, cols), lambda k, i: (k, i, 0)),
                 out_shape=jax.ShapeDtypeStruct((s, rr, cols), BF16), sem=("parallel", "parallel"))(partial, from_sibling)


def _col_tile(n, k):
    if n % 896 == 0 and k <= 1024:
        return 896
    return min(n, 512)


def project(name, x, wt, after):
    m, k = x.shape
    n = wt.shape[0]
    tn = D_MODEL

    def body(x_ref, w_ref, after_ref, o_ref):
        o_ref[...] = lax.dot_general(x_ref[...], w_ref[...], _NT, preferred_element_type=F32)

    return _call(body, name=name, grid=(n // tn,),
                 in_specs=[_full((m, k)), pl.BlockSpec((tn, k), lambda j: (j, 0)), _full(after.shape)],
                 out_specs=pl.BlockSpec((m, tn), lambda j: (0, j)), out_shape=jax.ShapeDtypeStruct((m, n), F32),
                 sem=("parallel",))(x, wt, after)


def _piece_tiles(pieces):
    spans, start = [], 0
    for p in pieces:
        spans.append((start, p.shape[1] // D_MODEL))
        start += p.shape[1] // D_MODEL
    return spans, start


def _piece_spec(tm, span, rows_of, tile_of):
    first, count = span

    def index(i, j):
        t = tile_of(i, j) - first
        mine = (t >= 0) & (t < count)
        return jnp.where(mine, rows_of(i, j), 0), jnp.clip(t, 0, count - 1)

    return pl.BlockSpec((tm, D_MODEL), index)


def project_back(name, pieces, wt, after):
    m = pieces[0].shape[0]
    k = wt.shape[1]
    tm = m // 2
    spans, steps = _piece_tiles(pieces)

    def body(*refs):
        w_ref, o_ref = refs[len(pieces)], refs[len(pieces) + 2]
        j = pl.program_id(1)

        @pl.when(j == 0)
        def _():
            o_ref[...] = jnp.zeros_like(o_ref)

        for dy_ref, (first, count) in zip(refs, spans):
            @pl.when((j >= first) & (j < first + count))
            def _():
                o_ref[...] += jnp.dot(dy_ref[...], w_ref[...], preferred_element_type=F32)

    return _call(body, name=name, grid=(m // tm, steps),
                 in_specs=[_piece_spec(tm, s, lambda i, j: i, lambda i, j: j) for s in spans]
                 + [pl.BlockSpec((D_MODEL, k), lambda i, j: (j, 0)), _full(after.shape)],
                 out_specs=pl.BlockSpec((tm, k), lambda i, j: (i, 0)), out_shape=jax.ShapeDtypeStruct((m, k), F32),
                 sem=("parallel", "arbitrary"))(*pieces, wt, after)


def _slab_runs():
    runs = [[] for _ in range(N_ACT // D_MODEL)]
    for s in range(4):
        for j in range(PACK_W // LANES):
            tile, r = divmod(_act_col((19 * s + j) * LANES), D_MODEL)
            last = runs[tile][-1] if runs[tile] else None
            if last and last[2] == s and last[0] + last[1] == r and last[3] + last[1] == j * LANES:
                last[1] += LANES
            else:
                runs[tile].append([r, LANES, s, j * LANES])
    return runs


def weight_grad_t(name, pieces, x):
    m = pieces[0].shape[0]
    k = x.shape[1]
    tm = m // 2
    spans, steps = _piece_tiles(pieces)
    runs = _slab_runs()
    most = max(len(r) for r in runs)

    def body(*refs):
        x_ref, o_ref, acc_ref, tile_ref, sems = refs[len(pieces):]
        i, half = pl.program_id(0), pl.program_id(1)

        def copies(t):
            return [pltpu.make_async_copy(tile_ref.at[t % 2, r:r + n], o_ref.at[s, d:d + n], sems.at[(t % 2) * most + c])
                    for c, (r, n, s, d) in enumerate(runs[t])]

        for dy_ref, (first, count) in zip(refs, spans):
            for t in range(first, first + count):
                @pl.when(i == t)
                def _():
                    part = lax.dot_general(dy_ref[...], x_ref[...], (((0,), (0,)), ((), ())),
                                           preferred_element_type=F32)

                    @pl.when(half == 0)
                    def _():
                        acc_ref[...] = part

                    @pl.when(half == 1)
                    def _():
                        if t >= 2:
                            for cp in copies(t - 2):
                                cp.wait()
                        tile_ref[t % 2] = (acc_ref[...] + part).astype(BF16)
                        for cp in copies(t):
                            cp.start()
                        if t == steps - 1:
                            for cp in copies(t - 1) + copies(t):
                                cp.wait()

    return _call(body, name=name, grid=(steps, 2),
                 in_specs=[_piece_spec(tm, s, lambda i, j: j, lambda i, j: i) for s in spans]
                 + [pl.BlockSpec((tm, k), lambda i, j: (j, 0))],
                 out_specs=_ANY, out_shape=jax.ShapeDtypeStruct((4, PACK_W, k), BF16),
                 scratch=[pltpu.VMEM((D_MODEL, k), F32), pltpu.VMEM((2, D_MODEL, k), BF16),
                          pltpu.SemaphoreType.DMA((2 * most,))],
                 sem=("arbitrary", "arbitrary"))(*pieces, x)


def mm_tn(name, x, dy):
    m, k = x.shape
    n = dy.shape[1]
    tn = _col_tile(n, k)

    def body(x_ref, dy_ref, o_ref):
        o_ref[...] = lax.dot_general(x_ref[...], dy_ref[...], (((0,), (0,)), ((), ())),
                                     preferred_element_type=F32).astype(BF16)

    return _call(body, name=name, grid=(n // tn,),
                 in_specs=[_full((m, k)), pl.BlockSpec((m, tn), lambda i: (0, i))],
                 out_specs=pl.BlockSpec((k, tn), lambda i: (0, i)), out_shape=jax.ShapeDtypeStruct((k, n), BF16),
                 sem=("parallel",))(x, dy)


def _row_spec(width, col_block=0):
    return pl.BlockSpec((BLOCK, width), lambda i: (i, col_block))


def _x_spec():
    return pl.BlockSpec((None, BLOCK, D_MODEL), lambda i: (0, jnp.maximum(i - 1, 0), 0))


def prep(x, meta, g_pre):
    nb = x.shape[1] // BLOCK + 1

    def body(x_ref, meta_ref, g_ref, h_ref, u_ref):
        i = pl.program_id(0)

        @pl.when(i == 0)
        def _():
            h_ref[0:PAD_ROWS, :] = jnp.zeros((PAD_ROWS, D_MODEL), F32)
            h_ref[PAD_ROWS:BLOCK, :] = meta_ref[...]

        @pl.when(i > 0)
        def _():
            h_ref[...] = x_ref[...]

        u_ref[...] = _rms(h_ref[...], g_ref[...]).astype(BF16)

    return _call(body, name="prep", grid=(nb,), in_specs=[_x_spec(), _full((N_META, D_MODEL)), _full((1, D_MODEL))],
                 out_specs=[_row_spec(D_MODEL), _row_spec(D_MODEL)],
                 out_shape=[jax.ShapeDtypeStruct((nb * BLOCK, D_MODEL), F32),
                            jax.ShapeDtypeStruct((nb * BLOCK, D_MODEL), BF16)], sem=("parallel",))(x, meta, g_pre)


def prep_bwd(h, du, dres, g_pre):
    nb = h.shape[0] // BLOCK

    def body(h_ref, du_ref, dres_ref, g_ref, gx_ref, gm_ref, gg_ref):
        i = pl.program_id(0)
        _, vjp = jax.vjp(_rms, h_ref[...], g_ref[...])
        dh, dg = vjp(du_ref[...])

        @pl.when(i == 0)
        def _():
            gm_ref[...] = dh[PAD_ROWS:BLOCK, :]
            gg_ref[...] = dg

        @pl.when(i > 0)
        def _():
            gg_ref[...] += dg

        gx_ref[...] = dh + dres_ref[...]

    return _call(body, name="prep_bwd", grid=(nb,),
                 in_specs=[_row_spec(D_MODEL), _row_spec(D_MODEL), _row_spec(D_MODEL), _full((1, D_MODEL))],
                 out_specs=[_x_spec(), _full((N_META, D_MODEL)), _full((1, D_MODEL))],
                 out_shape=[jax.ShapeDtypeStruct((1, (nb - 1) * BLOCK, D_MODEL), F32),
                            jax.ShapeDtypeStruct((N_META, D_MODEL), F32), jax.ShapeDtypeStruct((1, D_MODEL), F32)],
                 sem=("arbitrary",))(h, du, dres, g_pre)


GROUP_W = SSM_INNER // SSM_GROUPS


def _gated_norm(y, z, g):
    t = y * _silu(z)
    return t * lax.rsqrt(jnp.mean(t * t, axis=-1, keepdims=True) + NORM_EPS) * g


def _gated_norm_groups(y, z, g):
    groups = [slice(k * GROUP_W, (k + 1) * GROUP_W) for k in range(SSM_GROUPS)]
    return jnp.concatenate([_gated_norm(y[:, s], z[:, s], g[:, s]) for s in groups], axis=1)


def _merge(ga, gs, ya, ys):
    return _sigmoid(ga) * ya + _sigmoid(gs) * ys


GATE_ATT_BLOCK = SEG["gate_att"][2] // D_MODEL
GATE_SSM_BLOCK = SEG["gate_ssm"][2] // D_MODEL


def _row_loss(out, g_post, x, target):
    diff = x + _rms(out, g_post) - target
    return 0.5 * jnp.sum(diff * diff) / D_MODEL


def tail(y_ssd, proj, a_att, x, target, woa, wos, wo, g_norm, g_post):
    nb = y_ssd.shape[0] // BLOCK
    rows = nb * BLOCK

    def body(y_ref, z_ref, ga_ref, gs_ref, a_ref, x_ref, t_ref, woa_ref, wos_ref, wo_ref, gn_ref, gp_ref,
             yn_ref, mg_ref, dout_ref, dya_ref, dys_ref, da_ref, dy_ref, dz_ref, dga_ref, dgs_ref, dres_ref,
             loss_ref, dgp_ref, dgn_ref):
        i = pl.program_id(0)
        yn, norm_vjp = jax.vjp(_gated_norm_groups, y_ref[...], z_ref[...], gn_ref[...])
        yn16 = yn.astype(BF16)
        y_ssm = jnp.dot(yn16, wos_ref[...], preferred_element_type=F32)
        y_att = jnp.dot(a_ref[...], woa_ref[...], preferred_element_type=F32)
        merged, merge_vjp = jax.vjp(_merge, ga_ref[...], gs_ref[...], y_att, y_ssm)
        merged16 = merged.astype(BF16)
        out = jnp.dot(merged16, wo_ref[...], preferred_element_type=F32)
        loss, loss_vjp = jax.vjp(_row_loss, out, gp_ref[...], x_ref[...], t_ref[...])
        counted = jnp.where(i > 0, 1.0, 0.0)
        dout, dgp, dres, _ = loss_vjp(counted)
        dout16 = dout.astype(BF16)
        dmerged = lax.dot_general(dout16, wo_ref[...], _NT, preferred_element_type=F32)
        dga, dgs, dya, dys = merge_vjp(dmerged)
        dya16, dys16 = dya.astype(BF16), dys.astype(BF16)
        dyn = lax.dot_general(dys16, wos_ref[...], _NT, preferred_element_type=F32)
        dy, dz, dgn = norm_vjp(dyn)

        yn_ref[...] = yn16
        mg_ref[...] = merged16
        dout_ref[...] = dout16
        dya_ref[...] = dya16
        dys_ref[...] = dys16
        da_ref[...] = lax.dot_general(dya16, woa_ref[...], _NT, preferred_element_type=F32)
        dy_ref[...] = dy
        dz_ref[...] = dz.astype(BF16)
        dga_ref[...] = dga.astype(BF16)
        dgs_ref[...] = dgs.astype(BF16)
        dres_ref[...] = dres

        @pl.when(i == 0)
        def _():
            loss_ref[...] = jnp.zeros_like(loss_ref)
            dgp_ref[...] = jnp.zeros_like(dgp_ref)
            dgn_ref[...] = jnp.zeros_like(dgn_ref)

        loss_ref[...] += loss * counted
        dgp_ref[...] += dgp
        dgn_ref[...] += dgn

    wide, narrow = _row_spec(SSM_INNER), _row_spec(D_MODEL)
    resident = pl.BlockSpec(memory_space=pltpu.VMEM)
    bf = lambda w: jax.ShapeDtypeStruct((rows, w), BF16)
    f32 = lambda w: jax.ShapeDtypeStruct((rows, w), F32)
    return _call(body, name="tail", grid=(nb,),
                 in_specs=[wide, wide, _row_spec(D_MODEL, GATE_ATT_BLOCK), _row_spec(D_MODEL, GATE_SSM_BLOCK), narrow,
                           _x_spec(), _x_spec(), resident, resident, resident, _full((1, SSM_INNER)),
                           _full((1, D_MODEL))],
                 out_specs=[wide, narrow, narrow, narrow, narrow, narrow, wide, wide, narrow, narrow, narrow,
                            _full((8, LANES)), _full((1, D_MODEL)), _full((1, SSM_INNER))],
                 out_shape=[bf(SSM_INNER), bf(D_MODEL), bf(D_MODEL), bf(D_MODEL), bf(D_MODEL), f32(D_MODEL),
                            f32(SSM_INNER), bf(SSM_INNER), bf(D_MODEL), bf(D_MODEL), f32(D_MODEL),
                            jax.ShapeDtypeStruct((8, LANES), F32), jax.ShapeDtypeStruct((1, D_MODEL), F32),
                            jax.ShapeDtypeStruct((1, SSM_INNER), F32)],
                 sem=("arbitrary",))(y_ssd, proj, proj, proj, a_att, x, target, woa, wos, wo, g_norm, g_post)


_NT = (((1,), (1,)), ((), ()))
ALIBI_SLOPES = tuple(2.0 ** (-8.0 * (h + 1) / ATT_Q_HEADS) for h in range(ATT_Q_HEADS))
KV_WIDTH = ATT_KV_HEADS * HEAD_DIM
Q_BLOCK = SEG["q"][2] // D_MODEL
Z_ATT_BLOCK = SEG["z_att"][2] // D_MODEL
K_BLOCK = SEG["k"][2] // KV_WIDTH
V_BLOCK = SEG["v"][2] // KV_WIDTH
META_ROW_BLOCK = PAD_ROWS // N_META


@jax.custom_vjp
def _swap_halves(x):
    return pltpu.roll(x, HEAD_DIM, 1)


_swap_halves.defvjp(lambda x: (pltpu.roll(x, HEAD_DIM, 1), None), lambda _, g: (pltpu.roll(g, HEAD_DIM, 1),))


def _both_halves(t, half):
    first = lax.broadcasted_iota(jnp.int32, t.shape, 1) < HEAD_DIM
    sw = _swap_halves(t)
    return jnp.where(first, t, sw) if half == 0 else jnp.where(first, sw, t)


def _attn_rows(q, z, kp, kc, vp, vc, km, vm, sinks, n):
    rows = ATT_GROUP * BLOCK
    i = lax.broadcasted_iota(jnp.int32, (rows, BLOCK), 0) & (BLOCK - 1)
    j = lax.broadcasted_iota(jnp.int32, (rows, BLOCK), 1)
    rel_c = (i - j).astype(F32)
    rel_p = rel_c + float(BLOCK)
    nv = jnp.zeros((rows, BLOCK), jnp.int32) + n
    ok_c = (i >= j) & (nv >= 1)
    ok_p = (j > i) & (nv >= 2)
    im = lax.broadcasted_iota(jnp.int32, (rows, N_META), 0) & (BLOCK - 1)
    jm = lax.broadcasted_iota(jnp.int32, (rows, N_META), 1)
    ok_m = ((jnp.zeros((rows, N_META), jnp.int32) + n) >= 1) | (im >= PAD_ROWS + jm)
    first = lax.broadcasted_iota(jnp.int32, (BLOCK, LANES), 1) < HEAD_DIM
    neg = -jnp.inf
    outs = []
    for kv in range(ATT_KV_HEADS):
        tile, half = divmod(kv, 2)
        lanes = slice(tile * LANES, (tile + 1) * LANES)
        kc2, kp2, km2 = (_both_halves(t[:, lanes], half).astype(BF16) for t in (kc, kp, km))
        vc2, vp2, vm2 = (_both_halves(t[:, lanes], half).astype(BF16) for t in (vc, vp, vm))
        qs, slope, sk = [], [], []
        for pair in range(ATT_GROUP // 2):
            c0 = (kv * ATT_GROUP + 2 * pair) * HEAD_DIM
            qp = q[:, c0:c0 + LANES] * HEAD_DIM ** -0.5
            qs += [jnp.where(first, qp, 0.0), jnp.where(first, 0.0, qp)]
        for g in range(ATT_GROUP):
            slope.append(jnp.full((BLOCK, 1), ALIBI_SLOPES[kv * ATT_GROUP + g], F32))
            sk.append(jnp.broadcast_to(sinks[kv * ATT_GROUP + g], (BLOCK, 1)))
        qs = jnp.concatenate(qs, axis=0).astype(BF16)
        slope = jnp.concatenate(slope, axis=0)
        sk = jnp.concatenate(sk, axis=0)
        sc = jnp.where(ok_c, lax.dot_general(qs, kc2, _NT, preferred_element_type=F32) - slope * rel_c, neg)
        sp = jnp.where(ok_p, lax.dot_general(qs, kp2, _NT, preferred_element_type=F32) - slope * rel_p, neg)
        sm = jnp.where(ok_m, lax.dot_general(qs, km2, _NT, preferred_element_type=F32), neg)
        mx = jnp.maximum(jnp.maximum(jnp.max(sc, axis=1, keepdims=True), jnp.max(sp, axis=1, keepdims=True)),
                         jnp.maximum(jnp.max(sm, axis=1, keepdims=True), sk))
        mx = lax.stop_gradient(mx)
        ec, ep, em, es = jnp.exp(sc - mx), jnp.exp(sp - mx), jnp.exp(sm - mx), jnp.exp(sk - mx)
        den = (es + jnp.sum(ec, axis=1, keepdims=True) + jnp.sum(ep, axis=1, keepdims=True)
               + jnp.sum(em, axis=1, keepdims=True))
        inv = 1.0 / den
        o = (jnp.dot((ec * inv).astype(BF16), vc2, preferred_element_type=F32)
             + jnp.dot((ep * inv).astype(BF16), vp2, preferred_element_type=F32)
             + jnp.dot((em * inv).astype(BF16), vm2, preferred_element_type=F32))
        for pair in range(ATT_GROUP // 2):
            r0 = 2 * pair * BLOCK
            outs.append(jnp.where(first, o[r0:r0 + BLOCK], o[r0 + BLOCK:r0 + 2 * BLOCK]))
    return jnp.concatenate(outs, axis=1) * _silu(z)


def _attn_specs(nb, steps_clamped):
    def blk(t):
        return jnp.minimum(t, nb - 1) if steps_clamped else t

    wide = lambda col: pl.BlockSpec((BLOCK, D_MODEL), lambda t: (blk(t), col))
    cur = lambda col: pl.BlockSpec((BLOCK, KV_WIDTH), lambda t: (blk(t), col))
    prev = lambda col: pl.BlockSpec((BLOCK, KV_WIDTH), lambda t: (jnp.maximum(blk(t) - 1, 0), col))
    meta = lambda col: pl.BlockSpec((N_META, KV_WIDTH), lambda t: (META_ROW_BLOCK, col))
    sinks = pl.BlockSpec((ATT_Q_HEADS, 1, 1), lambda t: (0, 0, 0))
    return [wide(Q_BLOCK), wide(Z_ATT_BLOCK), prev(K_BLOCK), cur(K_BLOCK), prev(V_BLOCK), cur(V_BLOCK),
            meta(K_BLOCK), meta(V_BLOCK), sinks]


def attn_fwd(proj, sinks):
    nb = proj.shape[0] // BLOCK

    def body(q_ref, z_ref, kp_ref, kc_ref, vp_ref, vc_ref, km_ref, vm_ref, sk_ref, o_ref):
        o_ref[...] = _attn_rows(q_ref[...], z_ref[...], kp_ref[...], kc_ref[...], vp_ref[...], vc_ref[...],
                                km_ref[...], vm_ref[...], tuple(sk_ref[h] for h in range(ATT_Q_HEADS)),
                                pl.program_id(0)).astype(BF16)

    return _call(body, name="attn_fwd", grid=(nb,), in_specs=_attn_specs(nb, False), out_specs=_row_spec(D_MODEL),
                 out_shape=jax.ShapeDtypeStruct((nb * BLOCK, D_MODEL), BF16), sem=("parallel",))(*([proj] * 8), sinks)


def attn_bwd(da, proj, sinks):
    nb = proj.shape[0] // BLOCK
    last = nb - 1
    wide = pl.BlockSpec((BLOCK, D_MODEL), lambda t: (jnp.minimum(t, last), 0))
    done = pl.BlockSpec((BLOCK, KV_WIDTH), lambda t: (jnp.maximum(t - 1, 0), 0))
    meta = _full((N_META, KV_WIDTH))
    par = _full((ATT_Q_HEADS, 1, 1))

    def body(da_ref, q_ref, z_ref, kp_ref, kc_ref, vp_ref, vc_ref, km_ref, vm_ref, sk_ref,
             dq_ref, dz_ref, dk_ref, dv_ref, dkm_ref, dvm_ref, dsk_ref, ck_ref, cv_ref):
        t = pl.program_id(0)

        @pl.when(t == 0)
        def _():
            ck_ref[...] = jnp.zeros_like(ck_ref)
            cv_ref[...] = jnp.zeros_like(cv_ref)
            dkm_ref[...] = jnp.zeros_like(dkm_ref)
            dvm_ref[...] = jnp.zeros_like(dvm_ref)
            dsk_ref[...] = jnp.zeros_like(dsk_ref)

        @pl.when(t < nb)
        def _():
            def f(q, z, kp, kc, vp, vc, km, vm, sk):
                return _attn_rows(q, z, kp, kc, vp, vc, km, vm, sk, t)

            _, vjp = jax.vjp(f, q_ref[...], z_ref[...], kp_ref[...], kc_ref[...], vp_ref[...], vc_ref[...],
                             km_ref[...], vm_ref[...], tuple(sk_ref[h] for h in range(ATT_Q_HEADS)))
            dq, dz, dkp, dkc, dvp, dvc, dkm, dvm, dsk = vjp(da_ref[...])
            dq_ref[...] = dq.astype(BF16)
            dz_ref[...] = dz.astype(BF16)
            for h in range(ATT_Q_HEADS):
                dsk_ref[h] += dsk[h]
            dk_ref[...] = ck_ref[...] + dkp
            dv_ref[...] = cv_ref[...] + dvp
            ck_ref[...] = dkc
            cv_ref[...] = dvc
            dkm_ref[...] += dkm
            dvm_ref[...] += dvm

        @pl.when(t == nb)
        def _():
            dk_ref[...] = ck_ref[...]
            dv_ref[...] = cv_ref[...]

    rows = nb * BLOCK
    return _call(body, name="attn_bwd", grid=(nb + 1,), in_specs=[wide] + _attn_specs(nb, True),
                 out_specs=[wide, wide, done, done, meta, meta, par],
                 out_shape=[jax.ShapeDtypeStruct((rows, D_MODEL), BF16), jax.ShapeDtypeStruct((rows, D_MODEL), BF16),
                            jax.ShapeDtypeStruct((rows, KV_WIDTH), F32), jax.ShapeDtypeStruct((rows, KV_WIDTH), F32),
                            jax.ShapeDtypeStruct((N_META, KV_WIDTH), F32), jax.ShapeDtypeStruct((N_META, KV_WIDTH), F32),
                            jax.ShapeDtypeStruct(sinks.shape, F32)],
                 scratch=[pltpu.VMEM((BLOCK, KV_WIDTH), F32), pltpu.VMEM((BLOCK, KV_WIDTH), F32)],
                 sem=("arbitrary",))(da, *([proj] * 8), sinks)


XBC_BLOCK0 = SEG["xbc"][2] // D_MODEL
CONV_COL_BLOCKS = CONV_DIM // D_MODEL
DT_TILE = SEG["dt"][2] // LANES


HALO = 8


def _conv_rows(length):
    return 544 if length % 544 == 0 else BLOCK


def _shift_rows(cur, before, j):
    if j == 0:
        return cur
    n = cur.shape[0]
    row = lax.broadcasted_iota(jnp.int32, cur.shape, 0)
    head = pltpu.roll(before, j, 0)
    if n > HALO:
        head = jnp.concatenate([head, jnp.zeros((n - HALO, cur.shape[1]), cur.dtype)], axis=0)
    return jnp.where(row >= j, pltpu.roll(cur, j, 0), head)


def _conv_pre(cur, before, w_ref, b_ref):
    pre = b_ref[...] + w_ref[CONV_WIDTH - 1:CONV_WIDTH, :] * cur
    for k in range(CONV_WIDTH - 1):
        pre = pre + w_ref[k:k + 1, :] * _shift_rows(cur, before, CONV_WIDTH - 1 - k)
    return pre


def _conv_specs(steps, rows, col0=0):
    first = XBC_BLOCK0 + col0
    halos = rows // HALO
    cur = pl.BlockSpec((rows, D_MODEL), lambda j, i: (i, first + j))
    before = pl.BlockSpec((HALO, D_MODEL), lambda j, i: (jnp.maximum(i * halos - 1, 0), first + j))
    after = pl.BlockSpec((HALO, D_MODEL), lambda j, i: (jnp.minimum(i + 1, steps - 1) * halos, first + j))
    return cur, before, after


def _valid_rows(i, rows):
    row = lax.broadcasted_iota(jnp.int32, (rows, D_MODEL), 0)
    return jnp.maximum((row >= PAD_ROWS).astype(F32), jnp.where(i > 0, 1.0, 0.0))


def conv_fwd(proj, conv_w, conv_b):
    rows = _conv_rows(proj.shape[0])
    steps = proj.shape[0] // rows
    cur, before, _ = _conv_specs(steps, rows)

    def body(c_ref, p_ref, w_ref, b_ref, o_ref):
        i = pl.program_id(1)
        pre = _conv_pre(c_ref[...], p_ref[...] * jnp.where(i > 0, 1.0, 0.0), w_ref, b_ref)
        o_ref[...] = _silu(pre) * _valid_rows(i, rows)

    return _call(body, name="conv_fwd", grid=(CONV_COL_BLOCKS, steps),
                 in_specs=[cur, before, pl.BlockSpec((CONV_WIDTH, D_MODEL), lambda j, i: (0, j)),
                           pl.BlockSpec((1, D_MODEL), lambda j, i: (0, j))],
                 out_specs=pl.BlockSpec((rows, D_MODEL), lambda j, i: (i, j)),
                 out_shape=jax.ShapeDtypeStruct((proj.shape[0], CONV_DIM), F32),
                 sem=("parallel", "parallel"))(proj, proj, conv_w, conv_b)


def conv_bwd(name, dparts, col0, proj, conv_w, conv_b):
    rows = _conv_rows(proj.shape[0])
    steps = proj.shape[0] // rows
    last = steps - 1
    ncol = sum(d.shape[1] for d in dparts) // D_MODEL
    np_ = len(dparts)
    cur, before, after = _conv_specs(steps, rows, col0)
    dcur = [pl.BlockSpec((rows, d.shape[1] // ncol), lambda j, i: (i, j)) for d in dparts]
    dafter = [pl.BlockSpec((HALO, d.shape[1] // ncol), lambda j, i: (jnp.minimum(i + 1, last) * (rows // HALO), j))
              for d in dparts]
    out_cur = pl.BlockSpec((rows, D_MODEL), lambda j, i: (i, j))
    wspec = pl.BlockSpec((CONV_WIDTH, D_MODEL), lambda j, i: (0, col0 + j))
    bspec = pl.BlockSpec((1, D_MODEL), lambda j, i: (0, col0 + j))
    wout = pl.BlockSpec((CONV_WIDTH, D_MODEL), lambda j, i: (0, j))
    bout = pl.BlockSpec((1, D_MODEL), lambda j, i: (0, j))

    def body(*refs):
        dc_refs, da_refs = refs[:np_], refs[np_:2 * np_]
        c_ref, p_ref, a_ref, w_ref, b_ref, du_ref, dw_ref, db_ref = refs[2 * np_:]
        i = pl.program_id(1)
        row = lax.broadcasted_iota(jnp.int32, (rows, D_MODEL), 0)
        curv = c_ref[...]
        beforev = p_ref[...] * jnp.where(i > 0, 1.0, 0.0)
        side_by_side = lambda rs: rs[0][...] if np_ == 1 else jnp.concatenate([r[...] for r in rs], axis=1)

        def dpre_of(pre, d):
            s = _sigmoid(pre)
            return d * (s * (1.0 + pre * (1.0 - s)))

        dp_c = dpre_of(_conv_pre(curv, beforev, w_ref, b_ref), side_by_side(dc_refs) * _valid_rows(i, rows))
        dp_a = dpre_of(_conv_pre(a_ref[...], curv[rows - HALO:], w_ref, b_ref),
                       side_by_side(da_refs) * jnp.where(i < last, 1.0, 0.0))
        du = w_ref[CONV_WIDTH - 1:CONV_WIDTH, :] * dp_c
        for j in range(1, CONV_WIDTH):
            tail = jnp.concatenate([jnp.zeros((rows - HALO, D_MODEL), F32), pltpu.roll(dp_a, HALO - j, 0)], axis=0)
            up = jnp.where(row < rows - j, pltpu.roll(dp_c, rows - j, 0), tail)
            du = du + w_ref[CONV_WIDTH - 1 - j:CONV_WIDTH - j, :] * up
        du_ref[...] = du.astype(BF16)

        @pl.when(i == 0)
        def _():
            dw_ref[...] = jnp.zeros_like(dw_ref)
            db_ref[...] = jnp.zeros_like(db_ref)

        for k in range(CONV_WIDTH):
            dw_ref[k:k + 1, :] += jnp.sum(dp_c * _shift_rows(curv, beforev, CONV_WIDTH - 1 - k), axis=0, keepdims=True)
        db_ref[...] += jnp.sum(dp_c, axis=0, keepdims=True)

    width = ncol * D_MODEL
    return _call(body, name=name, grid=(ncol, steps),
                 in_specs=dcur + dafter + [cur, before, after, wspec, bspec], out_specs=[out_cur, wout, bout],
                 out_shape=[jax.ShapeDtypeStruct((proj.shape[0], width), BF16),
                            jax.ShapeDtypeStruct((CONV_WIDTH, width), F32), jax.ShapeDtypeStruct((1, width), F32)],
                 sem=("parallel", "arbitrary"))(*dparts, *dparts, proj, proj, proj, conv_w, conv_b)


def _head_expand():
    e = np.zeros((LANES, SSM_INNER), np.float32)
    for h in range(SSM_HEADS):
        e[h, h * HEAD_DIM:(h + 1) * HEAD_DIM] = 1.0
    return jnp.asarray(e, dtype=BF16)


def _softplus(x):
    return jnp.maximum(x, 0.0) + jnp.log(1.0 + jnp.exp(-jnp.abs(x)))


def _bf16_parts(x):
    hi = x.astype(BF16)
    rest = x - hi.astype(F32)
    mid = rest.astype(BF16)
    return hi, mid, (rest - mid.astype(F32)).astype(BF16)


@jax.custom_vjp
def _times_01(x, m):
    return sum(jnp.dot(p, m, preferred_element_type=F32) for p in _bf16_parts(x))


def _times_01_bwd(m, g):
    return sum(lax.dot_general(p, m, _NT, preferred_element_type=F32) for p in _bf16_parts(g)), jnp.zeros_like(m)


_times_01.defvjp(lambda x, m: (_times_01(x, m), m), _times_01_bwd)


def _causal_ones():
    l = lax.broadcasted_iota(jnp.int32, (BLOCK, BLOCK), 0)
    s = lax.broadcasted_iota(jnp.int32, (BLOCK, BLOCK), 1)
    return (l >= s).astype(BF16)


@jax.custom_vjp
def _cumsum_rows(a):
    return sum(jnp.dot(_causal_ones(), p, preferred_element_type=F32) for p in _bf16_parts(a))


def _cumsum_rows_bwd(_, g):
    tn = (((0,), (0,)), ((), ()))
    return (sum(lax.dot_general(_causal_ones(), p, tn, preferred_element_type=F32) for p in _bf16_parts(g)),)


_cumsum_rows.defvjp(lambda a: (_cumsum_rows(a), None), _cumsum_rows_bwd)


def _ssd_heads(dt_tile, bias, alog, dsk, expand):
    dt = _softplus(dt_tile + bias)
    a = dt * (-jnp.exp(alog))
    one_row = lambda v: jnp.broadcast_to(v, (HALO, LANES))
    return _times_01(jnp.concatenate([dt, _cumsum_rows(a), one_row(jnp.sum(a, axis=0, keepdims=True)), one_row(dsk)],
                                     axis=0), expand)


HEADS_ROWS = 2 * BLOCK + 2 * HALO


def _ssd_group(xs, per_lane, bg, cg, state):
    l = lax.broadcasted_iota(jnp.int32, (BLOCK, BLOCK), 0)
    s = lax.broadcasted_iota(jnp.int32, (BLOCK, BLOCK), 1)
    causal = l >= s
    first_head = s < HEAD_DIM
    dtx, cs = per_lane[0:BLOCK], per_lane[BLOCK:2 * BLOCK]
    tot, dsk = per_lane[2 * BLOCK:2 * BLOCK + 1], per_lane[2 * BLOCK + HALO:2 * BLOCK + HALO + 1]
    bb, cb16 = bg.astype(BF16), cg.astype(BF16)
    cb = lax.dot_general(cb16, bb, _NT, preferred_element_type=F32)
    xr = xs * dtx
    y_diag = []
    for p in range(GROUP_W // LANES):
        lanes = slice(p * LANES, (p + 1) * LANES)
        c_pair = cs[:, lanes]
        c_swap = _swap_halves(c_pair)
        m = []
        for c_head in (jnp.where(first_head, c_pair, c_swap), jnp.where(first_head, c_swap, c_pair)):
            m.append(cb * jnp.exp(jnp.where(causal, c_head - c_head.T, -jnp.inf)))
        x_pair = xr[:, lanes]
        x_diag = jnp.concatenate([jnp.where(first_head, x_pair, 0.0), jnp.where(first_head, 0.0, x_pair)], axis=0)
        y_diag.append(jnp.dot(jnp.concatenate(m, axis=1).astype(BF16), x_diag.astype(BF16),
                              preferred_element_type=F32))
    st = lax.dot_general(bb, (xr * jnp.exp(tot - cs)).astype(BF16), (((0,), (0,)), ((), ())),
                         preferred_element_type=F32)
    new_state = state * jnp.exp(tot) + st
    y_off = jnp.dot(cb16, state.astype(BF16), preferred_element_type=F32) * jnp.exp(cs)
    return jnp.concatenate(y_diag, axis=1) + y_off + dsk * xs, new_state


BC_WIDTH = SSM_GROUPS * SSM_STATE


def _ssd_specs(chunk):
    xs = pl.BlockSpec((BLOCK, SSM_INNER), lambda c: (chunk(c), 0))
    dt = pl.BlockSpec((BLOCK, LANES), lambda c: (chunk(c), DT_TILE))
    expand = _full((LANES, SSM_INNER))
    b = pl.BlockSpec((BLOCK, BC_WIDTH), lambda c: (chunk(c), SSM_INNER // BC_WIDTH))
    cc = pl.BlockSpec((BLOCK, BC_WIDTH), lambda c: (chunk(c), SSM_INNER // BC_WIDTH + 1))
    par = _full((1, LANES))
    state = pl.BlockSpec((None, SSM_STATE, SSM_INNER), lambda c: (chunk(c), 0, 0))
    return xs, dt, expand, b, cc, par, state


def _group_lanes(g):
    return slice(g * GROUP_W, (g + 1) * GROUP_W), slice(g * SSM_STATE, (g + 1) * SSM_STATE)


def ssd_fwd(xbc, proj, expand, bias, alog, dsk):
    nb = xbc.shape[0] // BLOCK
    xs, dt, ex, b, cc, par, state = _ssd_specs(lambda c: c)

    def body(x_ref, dt_ref, e_ref, bi_ref, al_ref, dk_ref, b_ref, c_ref, y_ref, sp_ref, st_ref):
        @pl.when(pl.program_id(0) == 0)
        def _():
            st_ref[...] = jnp.zeros_like(st_ref)

        per_lane = _ssd_heads(dt_ref[...], bi_ref[...], al_ref[...], dk_ref[...], e_ref[...])
        for g in range(SSM_GROUPS):
            wide, tile = _group_lanes(g)
            entering = st_ref[:, wide]
            sp_ref[:, wide] = entering
            y_ref[:, wide], st_ref[:, wide] = _ssd_group(x_ref[:, wide], per_lane[:, wide], b_ref[:, tile],
                                                         c_ref[:, tile], entering)

    return _call(body, name="ssd_fwd", grid=(nb,), in_specs=[xs, dt, ex, par, par, par, b, cc],
                 out_specs=[xs, state],
                 out_shape=[jax.ShapeDtypeStruct((nb * BLOCK, SSM_INNER), F32),
                            jax.ShapeDtypeStruct((nb, SSM_STATE, SSM_INNER), F32)],
                 scratch=[pltpu.VMEM((SSM_STATE, SSM_INNER), F32)],
                 sem=("arbitrary",))(xbc, proj, expand, bias, alog, dsk, xbc, xbc)


def ssd_bwd(dy, xbc, proj, expand, bias, alog, dsk, states, comm):
    nb = xbc.shape[0] // BLOCK
    last = nb - 1
    xs, dt, ex, b, cc, par, state = _ssd_specs(lambda c: last - c)
    tile = pl.BlockSpec((BLOCK, LANES), lambda c: (last - c, 0))
    nspec = pl.BlockSpec((BLOCK, BC_WIDTH), lambda c: (last - c, 0))

    def body(dy_ref, x_ref, dt_ref, e_ref, bi_ref, al_ref, dk_ref, b_ref, c_ref, sp_ref,
             dx_ref, ddt_ref, db_ref, dc_ref, dbi_ref, dal_ref, ddk_ref, ds_ref):
        @pl.when(pl.program_id(0) == 0)
        def _():
            ds_ref[...] = jnp.zeros_like(ds_ref)
            dbi_ref[...] = jnp.zeros_like(dbi_ref)
            dal_ref[...] = jnp.zeros_like(dal_ref)
            ddk_ref[...] = jnp.zeros_like(ddk_ref)

        expand = e_ref[...]
        per_lane, heads_vjp = jax.vjp(lambda t, bi, al, dk: _ssd_heads(t, bi, al, dk, expand), dt_ref[...], bi_ref[...],
                                      al_ref[...], dk_ref[...])
        d_per_lane = []
        for g in range(SSM_GROUPS):
            wide, tile_lanes = _group_lanes(g)
            _, vjp = jax.vjp(_ssd_group, x_ref[:, wide], per_lane[:, wide], b_ref[:, tile_lanes], c_ref[:, tile_lanes],
                             sp_ref[:, wide])
            (dx_ref[:, wide], d_lanes, db_ref[:, tile_lanes], dc_ref[:, tile_lanes],
             ds_ref[:, wide]) = vjp((dy_ref[:, wide], ds_ref[:, wide]))
            d_per_lane.append(d_lanes)
        ddt, dbi, dal, ddk = heads_vjp(jnp.concatenate(d_per_lane, axis=1))
        ddt_ref[...] = ddt.astype(BF16)
        dbi_ref[...] += dbi
        dal_ref[...] += dal
        ddk_ref[...] += ddk

    def at():
        c = pl.program_id(0)
        return c == 0, c == 0, c == last

    par_shape = jax.ShapeDtypeStruct((1, LANES), F32)
    return _call_with_comm(
        body, comm, at, (dy, xbc, proj, expand, bias, alog, dsk, xbc, xbc, states), name="ssd_bwd",
        grid=(nb,), in_specs=[xs, xs, dt, ex, par, par, par, b, cc, state],
        out_specs=[xs, tile, nspec, nspec, par, par, par],
        out_shape=[jax.ShapeDtypeStruct((nb * BLOCK, SSM_INNER), F32), jax.ShapeDtypeStruct((nb * BLOCK, LANES), BF16),
                   jax.ShapeDtypeStruct((nb * BLOCK, BC_WIDTH), F32), jax.ShapeDtypeStruct((nb * BLOCK, BC_WIDTH), F32),
                   par_shape, par_shape, par_shape],
        scratch=[pltpu.VMEM((SSM_STATE, SSM_INNER), F32)])


SLAB_ROWS = 16
SLAB_META_ROW = 8
SLAB_META_SHAPE = (8, 2 * D_MODEL)
SLAB_LOSS_ROW = 7


def pack_small(dcw, dcb, dgpre, dgpost, dbias, dalog, ddsk, dsinks, dgn, dmeta, loss_tile):
    def body(cw, cb, gpre, gpost, dtb, al, dk, sk, gn, meta, loss, o_ref):
        o_ref[...] = jnp.zeros_like(o_ref)
        o_ref[SLAB_LOSS_ROW:SLAB_LOSS_ROW + 1, 0:LANES] = loss[0:1, :]
        o_ref[0:CONV_WIDTH, :] = cw[...]
        o_ref[4:5, :] = cb[...]
        o_ref[5:6, 0:1024] = gpre[...]
        o_ref[5:6, 1024:2048] = gpost[...]
        o_ref[5:6, 2048:2176] = dtb[...]
        o_ref[5:6, 2176:2304] = al[...]
        o_ref[5:6, 2304:2432] = dk[...]
        o_ref[5:6, 2432:2560] = sk[...]
        o_ref[6:7, 0:SSM_INNER] = gn[...]
        o_ref[SLAB_META_ROW:SLAB_ROWS, 0:SLAB_META_SHAPE[1]] = meta[...]

    args = (dcw, dcb, dgpre, dgpost, dbias, dalog, ddsk, dsinks, dgn, dmeta, loss_tile)
    return _call(body, name="pack_small", in_specs=[_full(a.shape) for a in args],
                 out_specs=_full((SLAB_ROWS, CONV_DIM)), out_shape=jax.ShapeDtypeStruct((SLAB_ROWS, CONV_DIM), F32))(*args)


def _lane_tile(v):
    return jnp.pad(v, ((0, 0), (0, LANES - v.shape[1])))


def kernel(x, meta_tokens, g_pre, w_in, conv_w, conv_b, dt_bias, a_log, d_skip, attn_sinks, g_ssm_norm, w_out_att, w_out_ssm, w_out, g_post, loss_target, m_meta_tokens, m_g_pre, m_w_in, m_conv_w, m_conv_b, m_dt_bias, m_a_log, m_d_skip, m_attn_sinks, m_g_ssm_norm, m_w_out_att, m_w_out_ssm, m_w_out, m_g_post, v_meta_tokens, v_g_pre, v_w_in, v_conv_w, v_conv_b, v_dt_bias, v_a_log, v_d_skip, v_attn_sinks, v_g_ssm_norm, v_w_out_att, v_w_out_ssm, v_w_out, v_g_post):
    chip = _chip_index()

    conv_w_rows = jnp.pad(conv_w[0], ((0, 2 * 8 - CONV_WIDTH), (0, 0)))
    w_in_t, m_w_in_t, v_w_in_t = w_in[0].T, m_w_in[0].T, v_w_in[0].T
    gathered_w_in, g_conv_w, g_meta = run_comm("gather_w_in",
                                               TwoLevelGather([pack_w_in(w_in_t), conv_w_rows, meta_tokens]))
    w_all_t = unpack_w_in(gathered_w_in)
    cw_full = g_conv_w[:, :CONV_WIDTH].transpose(1, 0, 2).reshape(CONV_WIDTH, CONV_DIM)
    meta_full = g_meta.transpose(1, 0, 2).reshape(N_META, D_MODEL)
    behind_w_in = 0.0 * g_meta[0, 0, 0]
    w_out_flight, w_out_started = chip_exchange_start(
        "gather_w_out_start", [(w[0] + behind_w_in).astype(BF16) for w in (w_out_att, w_out_ssm, w_out)], False)

    h, u = prep(x, meta_full, g_pre)
    proj = project("in_proj", u, w_all_t, w_out_started)

    sinks3 = attn_sinks.reshape(ATT_Q_HEADS, 1, 1)
    a_att = attn_fwd(proj, sinks3)

    xbc = conv_fwd(proj, cw_full, conv_b)
    expand = _head_expand()
    head_pars = (_lane_tile(dt_bias), _lane_tile(a_log), _lane_tile(d_skip))
    y_ssd, states = ssd_fwd(xbc, proj, expand, *head_pars)
    woa, wos, wo = [g.reshape(-1, D_MODEL) for g in chip_exchange_wait("gather_w_out_wait", w_out_flight, False, y_ssd)]

    (yn, merged, dout, dy_att, dy_ssm, da_att, dy_ssd, dz_ssm, dga, dgs, dres, loss_tile, dg_post, dgn) = tail(
        y_ssd, proj, a_att, x, loss_target, woa, wos, wo, g_ssm_norm, g_post)

    dwo = mm_tn("out_proj_dw", merged, dout)
    dwoa = mm_tn("att_out_dw", a_att, dy_att)
    dwos = mm_tn("ssm_out_dw", yn, dy_ssm)
    dq, dz_att, dk, dv, dkmeta, dvmeta, dsinks3 = attn_bwd(da_att, proj, sinks3)
    dk = dk.at[PAD_ROWS:BLOCK].add(dkmeta).astype(BF16)
    dv = dv.at[PAD_ROWS:BLOCK].add(dvmeta).astype(BF16)

    def pieces(g):
        return g.reshape(4, 2, g.shape[0] // 8, g.shape[1])

    def to_owner(g):
        return (lambda ref, dev: ref.at[_chip_of(dev), dev[2]], (g.shape[0] // 8, g.shape[1]))

    (dxs, ddt_tile, dbg, dcg, dbias, dalog, ddsk), sent_w_out = ssd_bwd(
        dy_ssd, xbc, proj, expand, *head_pars, states,
        DirectExchange([pieces(dwoa), pieces(dwos), pieces(dwo)], [to_owner(dwoa), to_owner(dwos), to_owner(dwo)],
                       ALL_MASKS, "dev", 8))
    dxs_raw, dcw_xs, dcb_xs = conv_bwd("conv_bwd_x", [dxs], 0, proj, cw_full, conv_b)
    dbc_raw, dcw_bc, dcb_bc = conv_bwd("conv_bwd_bc", [dbg, dcg], SSM_INNER // D_MODEL, proj, cw_full, conv_b)
    dcw = jnp.concatenate([dcw_xs, dcw_bc], axis=1)
    dcb = jnp.concatenate([dcb_xs, dcb_bc], axis=1)

    narrow = jnp.concatenate([dk, dv, ddt_tile, jnp.zeros((dk.shape[0], N_ACT - N_ALIGNED), BF16)], axis=1)
    dproj = [dz_ssm, dxs_raw, dbc_raw, dq, dz_att, dga, dgs, narrow]
    half_rows = PACK_W // 2
    partial = weight_grad_t("in_proj_dw", dproj, u).reshape(4, 2, half_rows, D_MODEL)
    from_sibling, = run_comm("pair_grads", DirectExchange(
        [partial], [(lambda ref, dev: ref.at[pl.ds(0, 4), dev[2]], (4, half_rows, D_MODEL))], SIBLING_MASK, "core", 2,
        keep_own=False))
    chip_sum = sum_pair(partial, from_sibling)
    grads_flight, started = chip_exchange_start("reduce_w_in_start", [chip_sum], True)
    du = project_back("in_proj_dx", dproj, w_all_t, started)
    grad_x, dmeta, dg_pre = prep_bwd(h, du, dres, g_pre)

    halves = [sum_slots("sum_" + nm, r) for nm, r in zip(("w_out_att", "w_out_ssm", "w_out"), sent_w_out)]
    shared = run_comm("share_w_out", DirectExchange(halves, [None] * 3, SIBLING_MASK, "core", 2))
    g_woa, g_wos, g_wo = [f.reshape(2 * f.shape[1], f.shape[2]) for f in shared]
    d_woa, nm_woa, nv_woa = adamw_rows("adamw_w_out_att", g_woa, w_out_att[0], m_w_out_att[0], v_w_out_att[0])
    d_wos, nm_wos, nv_wos = adamw_rows("adamw_w_out_ssm", g_wos, w_out_ssm[0], m_w_out_ssm[0], v_w_out_ssm[0])
    d_wo, nm_wo, nv_wo = adamw_rows("adamw_w_out", g_wo, w_out[0], m_w_out[0], v_w_out[0])

    sent_w_in, = chip_exchange_wait("reduce_w_in_wait", grads_flight, True, d_wos)
    slab = pack_small(dcw, dcb, dg_pre, dg_post, dbias, dalog, ddsk, _lane_tile(dsinks3.reshape(1, ATT_Q_HEADS)), dgn,
                      dmeta.reshape(SLAB_META_SHAPE), loss_tile)
    shared_w_in, slabs = run_comm("share_w_in", Both(
        DirectExchange([sum_slots("sum_w_in", sent_w_in)], [None], SIBLING_MASK, "core", 2),
        DirectExchange([slab], [None], ALL_MASKS, "dev", 8)))
    small = sum_slots("sum_small", slabs)
    loss = small[SLAB_LOSS_ROW, 0]
    g_w_in, d_w_in, nm_w_in, nv_w_in = [
        a.T for a in adamw_w_in(shared_w_in.reshape(PACK_W, D_MODEL), w_in_t, m_w_in_t, v_w_in_t)]

    cw_cols = CONV_DIM // 4
    meta_cols = D_MODEL // 4
    g_small = {
        "meta_tokens": lax.dynamic_slice(
            small[SLAB_META_ROW:SLAB_ROWS, 0:SLAB_META_SHAPE[1]].reshape(N_META, D_MODEL), (0, chip * meta_cols),
            (N_META, meta_cols)),
        "g_pre": small[5:6, 0:1024],
        "conv_w": lax.dynamic_slice(small, (0, chip * cw_cols), (CONV_WIDTH, cw_cols)),
        "conv_b": small[4:5, :],
        "dt_bias": small[5:6, 2048:2048 + SSM_HEADS],
        "a_log": small[5:6, 2176:2176 + SSM_HEADS],
        "d_skip": small[5:6, 2304:2304 + SSM_HEADS],
        "attn_sinks": small[5:6, 2432:2432 + ATT_Q_HEADS],
        "g_ssm_norm": small[6:7, 0:SSM_INNER],
        "g_post": small[5:6, 1024:2048],
    }
    names = list(g_small)
    w_small = dict(meta_tokens=meta_tokens, g_pre=g_pre, conv_w=conv_w[0], conv_b=conv_b, dt_bias=dt_bias, a_log=a_log,
                   d_skip=d_skip, attn_sinks=attn_sinks, g_ssm_norm=g_ssm_norm, g_post=g_post)
    m_small = dict(meta_tokens=m_meta_tokens, g_pre=m_g_pre, conv_w=m_conv_w[0], conv_b=m_conv_b, dt_bias=m_dt_bias,
                   a_log=m_a_log, d_skip=m_d_skip, attn_sinks=m_attn_sinks, g_ssm_norm=m_g_ssm_norm, g_post=m_g_post)
    v_small = dict(meta_tokens=v_meta_tokens, g_pre=v_g_pre, conv_w=v_conv_w[0], conv_b=v_conv_b, dt_bias=v_dt_bias,
                   a_log=v_a_log, d_skip=v_d_skip, attn_sinks=v_attn_sinks, g_ssm_norm=v_g_ssm_norm, g_post=v_g_post)
    upd = dict(zip(names, adamw_small([g_small[k] for k in names], [w_small[k] for k in names],
                                      [m_small[k] for k in names], [v_small[k] for k in names])))

    lead = {"conv_w"}

    def shaped(name, a):
        return a[None] if name in lead else a

    grads = dict(g_small, w_in=g_w_in, w_out_att=g_woa, w_out_ssm=g_wos, w_out=g_wo)
    deltas = dict({k: upd[k][0] for k in names}, w_in=d_w_in, w_out_att=d_woa, w_out_ssm=d_wos, w_out=d_wo)
    new_m = dict({k: upd[k][1] for k in names}, w_in=nm_w_in, w_out_att=nm_woa, w_out_ssm=nm_wos, w_out=nm_wo)
    new_v = dict({k: upd[k][2] for k in names}, w_in=nv_w_in, w_out_att=nv_woa, w_out_ssm=nv_wos, w_out=nv_wo)
    lead |= {"w_in", "w_out_att", "w_out_ssm", "w_out"}
    order = ["meta_tokens", "g_pre", "w_in", "conv_w", "conv_b", "dt_bias", "a_log", "d_skip", "attn_sinks",
             "g_ssm_norm", "w_out_att", "w_out_ssm", "w_out", "g_post"]
    outs = [loss, grad_x]
    for group in (grads, deltas, new_m, new_v):
        outs += [shaped(k, group[k]) for k in order]
    return tuple(outs)
```

```python
import numpy as np
import jax
import jax.numpy as jnp
from jax import lax
from jax.experimental import pallas as pl
from jax.experimental.pallas import tpu as pltpu

F32 = jnp.float32
BF16 = jnp.bfloat16

D_MODEL = 1024
N_META = 16
BLOCK = 128
PAD_ROWS = BLOCK - N_META
NORM_EPS = 1e-6
HEAD_DIM = 64
ATT_Q_HEADS = 16
ATT_KV_HEADS = 4
ATT_GROUP = 4
SSM_INNER = 2048
SSM_HEADS = 32
SSM_GROUPS = 4
SSM_STATE = 128
CONV_WIDTH = 4
CONV_DIM = 3072
LANES = 128

ADAM_LR = 0.001
ADAM_B1 = 0.9
ADAM_B2 = 0.999
ADAM_EPS = 1e-08
ADAM_WD = 0.01
ADAM_STEP = 10

VMEM_LIMIT = 48 * 1024 * 1024

SHARD_W = 2440
PACK_W = 2560
SHARD_STRIDE = 2432
N_ALIGNED = 9856
N_ACT = 10240
SEG = {
    "q": (0, 1024, 5120), "k": (1024, 256, 9216), "v": (1280, 256, 9472), "z_att": (1536, 1024, 6144),
    "z_ssm": (2560, 2048, 0), "xbc": (4608, 3072, 2048), "dt": (7680, 128, 9728),
    "gate_att": (7808, 1024, 7168), "gate_ssm": (8832, 1024, 8192),
}
DT_STORED_START = 7680
DT_PAD = LANES - SSM_HEADS


def _act_col(aligned_col):
    for a0, w, p0 in SEG.values():
        if a0 <= aligned_col < a0 + w:
            return p0 + aligned_col - a0
    raise ValueError(aligned_col)


def _call(body, *, name, out_shape, in_specs, out_specs, grid=(), scratch=(), sem=None, aliases=None):
    return pl.pallas_call(
        body, out_shape=out_shape, grid=grid, in_specs=in_specs, out_specs=out_specs, scratch_shapes=list(scratch),
        name=name, input_output_aliases=aliases or {},
        compiler_params=pltpu.CompilerParams(dimension_semantics=sem, vmem_limit_bytes=VMEM_LIMIT))


def _full(shape):
    n = len(shape)
    return pl.BlockSpec(shape, lambda *_: (0,) * n)


def _chip_index():
    return lax.axis_index("x") * 2 + lax.axis_index("y")


_sigmoid = jax.nn.sigmoid


def _silu(z):
    return z * _sigmoid(z)


def _rms(x, g):
    return x * lax.rsqrt(jnp.mean(x * x, axis=-1, keepdims=True) + NORM_EPS) * g


def _peer(mask):
    x, y, c = lax.axis_index("x"), lax.axis_index("y"), lax.axis_index("c")
    return ((1 - x) if mask & 4 else x, (1 - y) if mask & 2 else y, (1 - c) if mask & 1 else c)


def _me():
    return lax.axis_index("x"), lax.axis_index("y"), lax.axis_index("c")


def _chip_of(dev):
    return 2 * dev[0] + dev[1]


CHIP_MASKS = (4, 2, 6)
ALL_MASKS = (1, 2, 3, 4, 5, 6, 7)
SIBLING_MASK = (1,)


def _remote(src, dst, send_sem, recv_sem, dev):
    return pltpu.make_async_remote_copy(src_ref=src, dst_ref=dst, send_sem=send_sem, recv_sem=recv_sem,
                                        device_id=dev, device_id_type=pl.DeviceIdType.MESH)


class _StagedCopy:
    def __init__(self, src, stage, dst, load_sem, store_sem):
        self.load = pltpu.make_async_copy(src, stage, load_sem)
        self.store = pltpu.make_async_copy(stage, dst, store_sem)

    def start(self):
        self.load.start()
        self.load.wait()
        self.store.start()

    def wait(self):
        self.store.wait()


class DirectExchange:
    def __init__(self, arrays, pieces, masks, slot_kind, nslots, keep_own=True):
        self.arrays, self.pieces, self.masks, self.slot_kind = list(arrays), list(pieces), masks, slot_kind
        self.keep_own = keep_own
        n, nk = len(arrays), len(masks)
        shapes = [a.shape if p is None else p[1] for a, p in zip(arrays, pieces)]
        self.out_shape = [jax.ShapeDtypeStruct((nslots,) + tuple(s), a.dtype) for s, a in zip(shapes, arrays)]
        self.scratch = [pltpu.SemaphoreType.DMA((n * nk,)), pltpu.SemaphoreType.DMA((n * nk,))]
        if keep_own:
            self.scratch += [pltpu.SemaphoreType.DMA((2 * n,))] + [pltpu.VMEM(s, a.dtype) for s, a in zip(shapes, arrays)]
        self.has_mid = False

    def _copies(self, ins, outs, scratch):
        send_sems, recv_sems = scratch[:2]
        me = _me()
        slot = {"chip": _chip_of(me), "dev": 4 * me[0] + 2 * me[1] + me[2], "core": me[2]}[self.slot_kind]
        nk = len(self.masks)

        def piece(a, dev):
            return ins[a] if self.pieces[a] is None else self.pieces[a][0](ins[a], dev)

        local = []
        if self.keep_own:
            local_sems, stages = scratch[2], scratch[3:]
            local = [_StagedCopy(piece(a, me), stages[a], outs[a].at[slot], local_sems.at[2 * a], local_sems.at[2 * a + 1])
                     for a in range(len(ins))]
        remote = []
        for a in range(len(ins)):
            for ki, mask in enumerate(self.masks):
                dev = _peer(mask)
                remote.append(_remote(piece(a, dev), outs[a].at[slot], send_sems.at[a * nk + ki],
                                      recv_sems.at[a * nk + ki], dev))
        return local, remote

    def start(self, ins, outs, scratch):
        local, remote = self._copies(ins, outs, scratch)
        for cp in remote + local:
            cp.start()

    def finish(self, ins, outs, scratch):
        local, remote = self._copies(ins, outs, scratch)
        for cp in remote + local:
            cp.wait()


SPLIT_ROWS = 16


class TwoLevelGather:
    def __init__(self, arrays):
        self.arrays = list(arrays)
        n = len(arrays)
        self.out_shape = [jax.ShapeDtypeStruct((4,) + a.shape, a.dtype) for a in arrays]
        self.scratch = ([pltpu.SemaphoreType.DMA((4 * n,)), pltpu.SemaphoreType.DMA((4 * n,)),
                         pltpu.SemaphoreType.DMA((3 * n,)), pltpu.SemaphoreType.DMA((3 * n,)),
                         pltpu.SemaphoreType.DMA((2 * n,))] + [pltpu.VMEM(a.shape, a.dtype) for a in arrays])
        self.has_mid = True

    def _copies(self, ins, outs, scratch):
        ici_send, ici_recv, fwd_send, fwd_recv, local_sems = scratch[:5]
        stages = scratch[5:]
        me = _me()
        sibling, in_x, in_y, diagonal = _peer(1), _peer(4), _peer(2), _peer(6)
        plan = []
        for a in range(len(ins)):
            half = ins[a].shape[0] // 2
            first = half // 2 if half % (2 * SPLIT_ROWS) == 0 else half
            mine = pl.ds(me[2] * half, half)
            local = _StagedCopy(ins[a], stages[a], outs[a].at[_chip_of(me)], local_sems.at[2 * a], local_sems.at[2 * a + 1])

            def ici(k, src, dst, dev):
                return _remote(src, dst, ici_send.at[4 * a + k], ici_recv.at[4 * a + k], dev)

            def d2d(k, chip):
                zone = outs[a].at[_chip_of(chip), mine]
                return _remote(zone, zone, fwd_send.at[3 * a + k], fwd_recv.at[3 * a + k], sibling)

            own_zone = outs[a].at[_chip_of(me), mine]
            from_x = outs[a].at[_chip_of(in_x), pl.ds(me[2] * half, first)]
            onward = [ici(2, from_x, from_x, in_y), None]
            if first < half:
                from_y = outs[a].at[_chip_of(in_y), pl.ds(me[2] * half + first, half - first)]
                onward[1] = ici(3, from_y, from_y, in_x)
            plan.append(dict(
                local=local,
                own=[ici(0, ins[a].at[mine], own_zone, in_x), ici(1, ins[a].at[mine], own_zone, in_y)],
                onward=onward, sibling=[d2d(0, in_x), d2d(1, in_y), d2d(2, diagonal)]))
        return plan

    def start(self, ins, outs, scratch):
        for p in self._copies(ins, outs, scratch):
            for cp in p["own"]:
                cp.start()
            p["local"].start()

    def mid(self, ins, outs, scratch):
        plan = self._copies(ins, outs, scratch)
        for p in plan:
            for k in range(2):
                p["own"][k].wait_recv()
                if p["onward"][k] is not None:
                    p["onward"][k].start()
                p["sibling"][k].start()
        for p in plan:
            for cp in p["onward"]:
                if cp is not None:
                    cp.wait_recv()
            p["sibling"][2].start()

    def finish(self, ins, outs, scratch):
        for p in self._copies(ins, outs, scratch):
            for cp in p["sibling"]:
                cp.wait_recv()
            for cp in p["own"] + p["sibling"] + [cp for cp in p["onward"] if cp is not None]:
                cp.wait_send()
            p["local"].wait()


class Both:
    def __init__(self, a, b):
        self.a, self.b = a, b
        self.arrays, self.out_shape = a.arrays + b.arrays, a.out_shape + b.out_shape
        self.scratch = a.scratch + b.scratch
        self.has_mid = False
        assert not (a.has_mid or b.has_mid)

    def _parts(self, ins, outs, sems):
        na, sa = len(self.a.arrays), len(self.a.scratch)
        return (ins[:na], outs[:na], sems[:sa]), (ins[na:], outs[na:], sems[sa:])

    def start(self, ins, outs, sems):
        pa, pb = self._parts(ins, outs, sems)
        self.a.start(*pa)
        self.b.start(*pb)

    def finish(self, ins, outs, sems):
        pa, pb = self._parts(ins, outs, sems)
        self.a.finish(*pa)
        self.b.finish(*pb)


_ANY = pl.BlockSpec(memory_space=pl.ANY)


def run_comm(name, comm):
    n = len(comm.arrays)

    def body(*refs):
        ins, outs, sems = refs[:n], refs[n:2 * n], refs[2 * n:]
        comm.start(ins, outs, sems)
        if comm.has_mid:
            comm.mid(ins, outs, sems)
        comm.finish(ins, outs, sems)

    return pl.pallas_call(body, name=name, out_shape=comm.out_shape, in_specs=[_ANY] * n, out_specs=[_ANY] * n,
                          scratch_shapes=comm.scratch,
                          compiler_params=pltpu.CompilerParams(vmem_limit_bytes=VMEM_LIMIT))(*comm.arrays)


_HBM = pl.BlockSpec(memory_space=pltpu.HBM)
_SEM = pl.BlockSpec(memory_space=pltpu.SEMAPHORE)
_SIDE_EFFECT = pltpu.SideEffectType.DATAFLOW_SIDE_EFFECTING


def _chip_copies(srcs, lands, send_sems, recv_sems, by_target):
    me = _me()
    copies = []
    for a, (src, land) in enumerate(zip(srcs, lands)):
        for ki, mask in enumerate(CHIP_MASKS):
            dev = _peer(mask)
            k = a * len(CHIP_MASKS) + ki
            piece = src.at[_chip_of(dev)] if by_target else src
            copies.append(_remote(piece, land.at[_chip_of(me)], send_sems.at[k], recv_sems.at[k], dev))
    return copies


def chip_exchange_start(name, arrays, by_target):
    n = len(arrays)
    nsem = n * len(CHIP_MASKS)
    piece_shapes = [a.shape[1:] if by_target else a.shape for a in arrays]

    def body(*refs):
        srcs, lands = refs[:n], refs[n:2 * n]
        send_sems, recv_sems = refs[2 * n:2 * n + 2]
        token = refs[4 * n + 2]
        stages, local_sems = refs[4 * n + 3:5 * n + 3], refs[5 * n + 3]
        me = _me()
        for cp in _chip_copies(srcs, lands, send_sems, recv_sems, by_target):
            cp.start()
        own = [_StagedCopy(srcs[a].at[_chip_of(me)] if by_target else srcs[a], stages[a], lands[a].at[_chip_of(me)],
                           local_sems.at[2 * a], local_sems.at[2 * a + 1]) for a in range(n)]
        for cp in own:
            cp.load.start()
        for cp in own:
            cp.load.wait()
            cp.store.start()
        for cp in own:
            cp.store.wait()
        token[...] = jnp.zeros_like(token)

    lands = [lax.empty((4,) + tuple(s), a.dtype) for s, a in zip(piece_shapes, arrays)]
    hbm = lambda a: pltpu.HBM(a.shape, a.dtype)
    res = pl.pallas_call(
        body, name=name,
        out_shape=(pltpu.SemaphoreType.DMA((nsem,)), pltpu.SemaphoreType.DMA((nsem,)), *[hbm(a) for a in arrays],
                   *[hbm(l) for l in lands], jax.ShapeDtypeStruct((8, LANES), F32)),
        in_specs=(_HBM,) * (2 * n), out_specs=(_SEM, _SEM) + (_HBM,) * (2 * n) + (pl.BlockSpec(memory_space=pltpu.VMEM),),
        input_output_aliases={i: i + 2 for i in range(2 * n)},
        scratch_shapes=[pltpu.VMEM(tuple(s), a.dtype) for s, a in zip(piece_shapes, arrays)]
        + [pltpu.SemaphoreType.DMA((2 * n,))],
        compiler_params=pltpu.CompilerParams(has_side_effects=_SIDE_EFFECT, vmem_limit_bytes=VMEM_LIMIT),
    )(*[pltpu.with_memory_space_constraint(a, pltpu.HBM) for a in arrays + lands])
    return res[:-1], res[-1]


def chip_exchange_wait(name, in_flight, by_target, after):
    send_sems, recv_sems, *thru = in_flight
    n = len(thru) // 2

    def body(*refs):
        srcs, lands, (send, recv) = refs[:n], refs[n:2 * n], refs[2 * n:2 * n + 2]
        for cp in _chip_copies(srcs, lands, send, recv, by_target):
            cp.wait_send()
            cp.wait_recv()

    return pl.pallas_call(
        body, name=name, out_shape=tuple(pltpu.HBM(t.shape, t.dtype) for t in thru),
        in_specs=(_HBM,) * (2 * n) + (_SEM, _SEM, pl.BlockSpec(memory_space=pl.ANY)), out_specs=(_HBM,) * (2 * n),
        input_output_aliases={i: i for i in range(2 * n)},
        compiler_params=pltpu.CompilerParams(has_side_effects=_SIDE_EFFECT),
    )(*thru, send_sems, recv_sems, after)[n:]


def _call_with_comm(body, comm, steps, args, *, name, out_shape, in_specs, out_specs, grid, scratch=()):
    ni, no, ns, nc = len(in_specs), len(out_specs), len(scratch), len(comm.arrays)

    def full_body(*refs):
        ins, cins = refs[:ni], refs[ni:ni + nc]
        outs, couts = refs[ni + nc:ni + nc + no], refs[ni + nc + no:ni + 2 * nc + no]
        scr, csems = refs[ni + 2 * nc + no:ni + 2 * nc + no + ns], refs[ni + 2 * nc + no + ns:]
        first, middle, last = steps()
        pl.when(first)(lambda: comm.start(cins, couts, csems))
        if comm.has_mid:
            pl.when(middle)(lambda: comm.mid(cins, couts, csems))
        body(*ins, *outs, *scr)
        pl.when(last)(lambda: comm.finish(cins, couts, csems))

    res = pl.pallas_call(
        full_body, name=name, out_shape=list(out_shape) + comm.out_shape, grid=grid,
        in_specs=list(in_specs) + [_ANY] * nc, out_specs=list(out_specs) + [_ANY] * nc,
        scratch_shapes=list(scratch) + comm.scratch,
        compiler_params=pltpu.CompilerParams(dimension_semantics=("arbitrary",) * len(grid),
                                             vmem_limit_bytes=VMEM_LIMIT))(*args, *comm.arrays)
    return res[:no], res[no:]


def _shard_pieces(chip):
    if chip < 3:
        return [(0, SHARD_W, 8 * chip)]
    behind_dt = DT_STORED_START + SSM_HEADS - 3 * SHARD_W
    return [(0, behind_dt, 24), (behind_dt, SHARD_W - behind_dt, behind_dt + 24 + DT_PAD)]


W_IN_COLS = 256


def pack_w_in(wt):
    def body(w_ref, o_ref, pad_ref):
        chip = _chip_index()
        pad_ref[...] = jnp.zeros_like(pad_ref)
        for cv in range(4):
            @pl.when(chip == cv)
            def _():
                for src, n, dst in _shard_pieces(cv):
                    pad_ref[dst:dst + n, :] = w_ref[src:src + n, :]
        o_ref[...] = pad_ref[...].astype(BF16)

    return _call(body, name="pack_w_in", grid=(D_MODEL // W_IN_COLS,),
                 in_specs=[pl.BlockSpec((SHARD_W, W_IN_COLS), lambda i: (0, i))],
                 out_specs=pl.BlockSpec((PACK_W, W_IN_COLS), lambda i: (0, i)),
                 out_shape=jax.ShapeDtypeStruct((PACK_W, D_MODEL), BF16),
                 scratch=[pltpu.VMEM((PACK_W, W_IN_COLS), F32)], sem=("parallel",))(wt)


def _tile_runs():
    runs, fix = [], []
    for t in range(N_ALIGNED // LANES):
        s = min(t // 19, 3)
        j = t - 19 * s
        p = _act_col(t * LANES)
        if runs and runs[-1][1] == s and runs[-1][0] + runs[-1][3] == p and runs[-1][2] + runs[-1][3] == j * LANES:
            runs[-1][3] += LANES
        else:
            runs.append([p, s, j * LANES, LANES])
        if j == 0 and s > 0:
            fix.append((p, s - 1))
    return runs, fix


def unpack_w_in(bg):
    runs, fix = _tile_runs()

    def body(b_ref, o_ref):
        for p, s, j, w in runs:
            o_ref[p:p + w, :] = b_ref[s, j:j + w, :]
        for p, s in fix:
            o_ref[p:p + LANES, :] = o_ref[p:p + LANES, :] + b_ref[s, SHARD_STRIDE:PACK_W, :]
        o_ref[N_ALIGNED:N_ACT, :] = jnp.zeros((N_ACT - N_ALIGNED, W_IN_COLS), BF16)

    return _call(body, name="unpack_w_in", grid=(D_MODEL // W_IN_COLS,),
                 in_specs=[pl.BlockSpec((4, PACK_W, W_IN_COLS), lambda i: (0, 0, i))],
                 out_specs=pl.BlockSpec((N_ACT, W_IN_COLS), lambda i: (0, i)),
                 out_shape=jax.ShapeDtypeStruct((N_ACT, D_MODEL), BF16), sem=("parallel",))(bg)


def _adamw(w, g, m, v):
    m = ADAM_B1 * m + (1.0 - ADAM_B1) * g
    v = ADAM_B2 * v + (1.0 - ADAM_B2) * jnp.square(g)
    m_hat = m / (1.0 - ADAM_B1 ** ADAM_STEP)
    v_hat = v / (1.0 - ADAM_B2 ** ADAM_STEP)
    delta = -ADAM_LR * (m_hat / (jnp.sqrt(v_hat) + ADAM_EPS) + ADAM_WD * w)
    return delta, m, v


def adamw_w_in(g_packed, wt, mt, vt):
    cols = LANES

    def body(g_ref, w_ref, m_ref, v_ref, go_ref, d_ref, mo_ref, vo_ref):
        chip = _chip_index()
        for cv in range(4):
            @pl.when(chip == cv)
            def _():
                for dst, n, src in _shard_pieces(cv):
                    go_ref[dst:dst + n, :] = g_ref[src:src + n, :]
        d_ref[...], mo_ref[...], vo_ref[...] = _adamw(w_ref[...], go_ref[...], m_ref[...], v_ref[...])

    spec = pl.BlockSpec((SHARD_W, cols), lambda i: (0, i))
    shp = jax.ShapeDtypeStruct((SHARD_W, D_MODEL), F32)
    return _call(body, name="adamw_w_in", grid=(D_MODEL // cols,),
                 in_specs=[pl.BlockSpec((PACK_W, cols), lambda i: (0, i)), spec, spec, spec],
                 out_specs=[spec] * 4, out_shape=[shp] * 4, sem=("parallel",))(g_packed, wt, mt, vt)


def adamw_rows(name, g, w, m, v):
    r, c = g.shape
    rows = min(r, BLOCK)

    def body(g_ref, w_ref, m_ref, v_ref, d_ref, mo_ref, vo_ref):
        d_ref[...], mo_ref[...], vo_ref[...] = _adamw(w_ref[...], g_ref[...], m_ref[...], v_ref[...])

    spec = pl.BlockSpec((rows, c), lambda i: (i, 0))
    shp = jax.ShapeDtypeStruct((r, c), F32)
    return _call(body, name=name, grid=(r // rows,), in_specs=[spec] * 4, out_specs=[spec] * 3, out_shape=[shp] * 3,
                 sem=("parallel",))(g, w, m, v)


def adamw_small(gs, ws, ms, vs):
    n = len(gs)

    def body(*refs):
        g, w, m, v = refs[:n], refs[n:2 * n], refs[2 * n:3 * n], refs[3 * n:4 * n]
        outs = refs[4 * n:]
        for i in range(n):
            d, mn, vn = _adamw(w[i][...], g[i][...], m[i][...], v[i][...])
            outs[3 * i][...] = d
            outs[3 * i + 1][...] = mn
            outs[3 * i + 2][...] = vn

    specs = [_full(a.shape) for a in gs]
    res = _call(body, name="adamw_small", in_specs=specs * 4,
                out_specs=[s for s in specs for _ in range(3)],
                out_shape=[jax.ShapeDtypeStruct(a.shape, F32) for a in gs for _ in range(3)])(*gs, *ws, *ms, *vs)
    return [tuple(res[3 * i:3 * i + 3]) for i in range(n)]


def sum_slots(name, r):
    s, rr, c = r.shape
    rows = min(rr, BLOCK)

    def body(r_ref, o_ref):
        acc = r_ref[0].astype(F32)
        for k in range(1, s):
            acc = acc + r_ref[k].astype(F32)
        o_ref[...] = acc

    return _call(body, name=name, grid=(rr // rows,), in_specs=[pl.BlockSpec((s, rows, c), lambda i: (0, i, 0))],
                 out_specs=pl.BlockSpec((rows, c), lambda i: (i, 0)), out_shape=jax.ShapeDtypeStruct((rr, c), F32),
                 sem=("parallel",))(r)


def sum_pair(partial, from_sibling):
    s, _, rr, cols = partial.shape
    rows = rr // 2

    def body(c_ref, p_ref, r_ref, o_ref):
        o_ref[...] = (p_ref[...].astype(F32) + r_ref[...].astype(F32)).astype(BF16)

    core = lax.axis_index("c").astype(jnp.int32).reshape(1)
    return pl.pallas_call(
        body, name="sum_pair", out_shape=jax.ShapeDtypeStruct((s, rr, cols), BF16),
        grid_spec=pltpu.PrefetchScalarGridSpec(
            num_scalar_prefetch=1, grid=(s, rr // rows),
            in_specs=[pl.BlockSpec((None, None, rows, cols), lambda k, i, c: (k, c[0], i, 0)),
                      pl.BlockSpec((None, None, rows, cols), lambda k, i, c: (1 - c[0], k, i, 0))],
            out_specs=pl.BlockSpec((None, rows, cols), lambda k, i, c: (k, i, 0))),
        compiler_params=pltpu.CompilerParams(dimension_semantics=("parallel", "parallel"),
                                             vmem_limit_bytes=VMEM_LIMIT))(core, partial, from_sibling)


def _col_tile(n, k):
    if n % 896 == 0 and k <= 1024:
        return 896
    return min(n, 512)


def project(name, x, wt, after):
    m, k = x.shape
    n = wt.shape[0]
    tn = D_MODEL

    def body(x_ref, w_ref, after_ref, o_ref):
        o_ref[...] = lax.dot_general(x_ref[...], w_ref[...], _NT, preferred_element_type=F32)

    return _call(body, name=name, grid=(n // tn,),
                 in_specs=[_full((m, k)), pl.BlockSpec((tn, k), lambda j: (j, 0)), _full(after.shape)],
                 out_specs=pl.BlockSpec((m, tn), lambda j: (0, j)), out_shape=jax.ShapeDtypeStruct((m, n), F32),
                 sem=("parallel",))(x, wt, after)


def _piece_tiles(pieces, width):
    spans, start = [], 0
    for p in pieces:
        spans.append((start, p.shape[1] // width))
        start += p.shape[1] // width
    return spans, start


def _piece_spec(block, span, rows_of, tile_of):
    first, count = span

    def index(*pos):
        t = tile_of(*pos) - first
        mine = (t >= 0) & (t < count)
        return jnp.where(mine, rows_of(*pos), 0), jnp.clip(t, 0, count - 1)

    return pl.BlockSpec(block, index)


def project_back(name, pieces, wt, after):
    m = pieces[0].shape[0]
    k = wt.shape[1]
    tm = m // 2
    spans, steps = _piece_tiles(pieces, D_MODEL)

    def body(*refs):
        w_ref, o_ref = refs[len(pieces)], refs[len(pieces) + 2]
        j = pl.program_id(1)

        @pl.when(j == 0)
        def _():
            o_ref[...] = jnp.zeros_like(o_ref)

        for dy_ref, (first, count) in zip(refs, spans):
            @pl.when((j >= first) & (j < first + count))
            def _():
                o_ref[...] += jnp.dot(dy_ref[...], w_ref[...], preferred_element_type=F32)

    return _call(body, name=name, grid=(m // tm, steps),
                 in_specs=[_piece_spec((tm, D_MODEL), s, lambda i, j: i, lambda i, j: j) for s in spans]
                 + [pl.BlockSpec((D_MODEL, k), lambda i, j: (j, 0)), _full(after.shape)],
                 out_specs=pl.BlockSpec((tm, k), lambda i, j: (i, 0)), out_shape=jax.ShapeDtypeStruct((m, k), F32),
                 sem=("parallel", "arbitrary"))(*pieces, wt, after)


GRAD_TILE = 512


def _slab_runs():
    runs = [[] for _ in range(N_ACT // GRAD_TILE)]
    for s in range(4):
        for j in range(PACK_W // LANES):
            tile, r = divmod(_act_col((19 * s + j) * LANES), GRAD_TILE)
            last = runs[tile][-1] if runs[tile] else None
            if last and last[2] == s and last[0] + last[1] == r and last[3] + last[1] == j * LANES:
                last[1] += LANES
            else:
                runs[tile].append([r, LANES, s, j * LANES])
    return runs


def weight_grad_t(name, pieces, x):
    m, k = x.shape
    spans, steps = _piece_tiles(pieces, GRAD_TILE)
    runs = _slab_runs()
    most = max(len(r) for r in runs)

    def body(*refs):
        x_ref, o_ref, tile_ref, sems = refs[len(pieces):]
        i = pl.program_id(0)

        def copies(t):
            return [pltpu.make_async_copy(tile_ref.at[t % 2, r:r + n], o_ref.at[s, d:d + n], sems.at[(t % 2) * most + c])
                    for c, (r, n, s, d) in enumerate(runs[t])]

        for t in range(2, steps):
            @pl.when(i == t)
            def _():
                for cp in copies(t - 2):
                    cp.wait()

        for dy_ref, (first, count) in zip(refs, spans):
            @pl.when((i >= first) & (i < first + count))
            def _():
                tile_ref[i % 2] = lax.dot_general(dy_ref[...], x_ref[...], (((0,), (0,)), ((), ())),
                                                  preferred_element_type=F32).astype(BF16)

        for t in range(steps):
            @pl.when(i == t)
            def _():
                for cp in copies(t):
                    cp.start()
                if t == steps - 1:
                    for cp in copies(t - 1) + copies(t):
                        cp.wait()

    return _call(body, name=name, grid=(steps,),
                 in_specs=[_piece_spec((m, GRAD_TILE), s, lambda i: 0, lambda i: i) for s in spans]
                 + [pl.BlockSpec(memory_space=pltpu.VMEM)],
                 out_specs=_ANY, out_shape=jax.ShapeDtypeStruct((4, PACK_W, k), BF16),
                 scratch=[pltpu.VMEM((2, GRAD_TILE, k), BF16), pltpu.SemaphoreType.DMA((2 * most,))],
                 sem=("arbitrary",))(*pieces, x)


def mm_tn(name, x, dy):
    m, k = x.shape
    n = dy.shape[1]
    tn = _col_tile(n, k)

    def body(x_ref, dy_ref, o_ref):
        o_ref[...] = lax.dot_general(x_ref[...], dy_ref[...], (((0,), (0,)), ((), ())),
                                     preferred_element_type=F32).astype(BF16)

    return _call(body, name=name, grid=(n // tn,),
                 in_specs=[_full((m, k)), pl.BlockSpec((m, tn), lambda i: (0, i))],
                 out_specs=pl.BlockSpec((k, tn), lambda i: (0, i)), out_shape=jax.ShapeDtypeStruct((k, n), BF16),
                 sem=("parallel",))(x, dy)


def _row_spec(width, col_block=0):
    return pl.BlockSpec((BLOCK, width), lambda i: (i, col_block))


def _x_spec():
    return pl.BlockSpec((None, BLOCK, D_MODEL), lambda i: (0, jnp.maximum(i - 1, 0), 0))


def prep(x, meta, g_pre):
    nb = x.shape[1] // BLOCK + 1

    def body(x_ref, meta_ref, g_ref, h_ref, u_ref):
        i = pl.program_id(0)

        @pl.when(i == 0)
        def _():
            h_ref[0:PAD_ROWS, :] = jnp.zeros((PAD_ROWS, D_MODEL), F32)
            h_ref[PAD_ROWS:BLOCK, :] = meta_ref[...]

        @pl.when(i > 0)
        def _():
            h_ref[...] = x_ref[...]

        u_ref[...] = _rms(h_ref[...], g_ref[...]).astype(BF16)

    return _call(body, name="prep", grid=(nb,), in_specs=[_x_spec(), _full((N_META, D_MODEL)), _full((1, D_MODEL))],
                 out_specs=[_row_spec(D_MODEL), _row_spec(D_MODEL)],
                 out_shape=[jax.ShapeDtypeStruct((nb * BLOCK, D_MODEL), F32),
                            jax.ShapeDtypeStruct((nb * BLOCK, D_MODEL), BF16)], sem=("parallel",))(x, meta, g_pre)


def prep_bwd(h, du, dres, g_pre):
    seq = h.shape[0] - BLOCK
    rows = min(seq, 4 * BLOCK)

    def body(h0_ref, du0_ref, h_ref, du_ref, dres_ref, g_ref, gx_ref, gm_ref, gg_ref):
        i = pl.program_id(0)
        _, vjp = jax.vjp(_rms, h_ref[...], g_ref[...])
        dh, dg = vjp(du_ref[...])
        gx_ref[...] = dh + dres_ref[...]

        @pl.when(i == 0)
        def _():
            _, vjp0 = jax.vjp(_rms, h0_ref[...], g_ref[...])
            dh0, dg0 = vjp0(du0_ref[...])
            gm_ref[...] = dh0[PAD_ROWS:BLOCK, :]
            gg_ref[...] = dg0 + dg

        @pl.when(i > 0)
        def _():
            gg_ref[...] += dg

    seq_rows = pl.BlockSpec((pl.Element(rows), pl.Element(D_MODEL)), lambda i: (pl.multiple_of(BLOCK + rows * i, BLOCK), 0))
    first =pl.BlockSpec((BLOCK, D_MODEL), lambda i: (0, 0))
    return _call(body, name="prep_bwd", grid=(seq // rows,),
                 in_specs=[first, first, seq_rows, seq_rows, seq_rows, _full((1, D_MODEL))],
                 out_specs=[pl.BlockSpec((None, rows, D_MODEL), lambda i: (0, i, 0)), _full((N_META, D_MODEL)),
                            _full((1, D_MODEL))],
                 out_shape=[jax.ShapeDtypeStruct((1, seq, D_MODEL), F32),
                            jax.ShapeDtypeStruct((N_META, D_MODEL), F32), jax.ShapeDtypeStruct((1, D_MODEL), F32)],
                 sem=("arbitrary",))(h, du, h, du, dres, g_pre)


GROUP_W = SSM_INNER // SSM_GROUPS


def _gated_norm(y, z, g):
    t = y * _silu(z)
    return t * lax.rsqrt(jnp.mean(t * t, axis=-1, keepdims=True) + NORM_EPS) * g


def _gated_norm_groups(y, z, g):
    groups = [slice(k * GROUP_W, (k + 1) * GROUP_W) for k in range(SSM_GROUPS)]
    return jnp.concatenate([_gated_norm(y[:, s], z[:, s], g[:, s]) for s in groups], axis=1)


def _merge(ga, gs, ya, ys):
    return _sigmoid(ga) * ya + _sigmoid(gs) * ys


GATE_ATT_BLOCK = SEG["gate_att"][2] // D_MODEL
GATE_SSM_BLOCK = SEG["gate_ssm"][2] // D_MODEL


def _row_loss(out, g_post, x, target):
    diff = x + _rms(out, g_post) - target
    return 0.5 * jnp.sum(diff * diff) / D_MODEL


def tail(y_ssd, proj, a_att, x, target, woa, wos, wo, g_norm, g_post):
    nb = y_ssd.shape[0] // BLOCK
    rows = nb * BLOCK

    def body(y_ref, z_ref, ga_ref, gs_ref, a_ref, x_ref, t_ref, woa_ref, wos_ref, wo_ref, gn_ref, gp_ref,
             yn_ref, mg_ref, dout_ref, dya_ref, dys_ref, da_ref, dy_ref, dz_ref, dga_ref, dgs_ref, dres_ref,
             loss_ref, dgp_ref, dgn_ref):
        i = pl.program_id(0)
        yn, norm_vjp = jax.vjp(_gated_norm_groups, y_ref[...], z_ref[...], gn_ref[...])
        yn16 = yn.astype(BF16)
        y_ssm = jnp.dot(yn16, wos_ref[...], preferred_element_type=F32)
        y_att = jnp.dot(a_ref[...], woa_ref[...], preferred_element_type=F32)
        merged, merge_vjp = jax.vjp(_merge, ga_ref[...], gs_ref[...], y_att, y_ssm)
        merged16 = merged.astype(BF16)
        out = jnp.dot(merged16, wo_ref[...], preferred_element_type=F32)
        loss, loss_vjp = jax.vjp(_row_loss, out, gp_ref[...], x_ref[...], t_ref[...])
        counted = jnp.where(i > 0, 1.0, 0.0)
        dout, dgp, dres, _ = loss_vjp(counted)
        dout16 = dout.astype(BF16)
        dmerged = lax.dot_general(dout16, wo_ref[...], _NT, preferred_element_type=F32)
        dga, dgs, dya, dys = merge_vjp(dmerged)
        dya16, dys16 = dya.astype(BF16), dys.astype(BF16)
        dyn = lax.dot_general(dys16, wos_ref[...], _NT, preferred_element_type=F32)
        dy, dz, dgn = norm_vjp(dyn)

        yn_ref[...] = yn16
        mg_ref[...] = merged16
        dout_ref[...] = dout16
        dya_ref[...] = dya16
        dys_ref[...] = dys16
        da_ref[...] = lax.dot_general(dya16, woa_ref[...], _NT, preferred_element_type=F32)
        dy_ref[...] = dy
        dz_ref[...] = dz.astype(BF16)
        dga_ref[...] = dga.astype(BF16)
        dgs_ref[...] = dgs.astype(BF16)
        dres_ref[...] = dres

        @pl.when(i == 0)
        def _():
            loss_ref[...] = jnp.zeros_like(loss_ref)
            dgp_ref[...] = jnp.zeros_like(dgp_ref)
            dgn_ref[...] = jnp.zeros_like(dgn_ref)

        loss_ref[...] += loss * counted
        dgp_ref[...] += dgp
        dgn_ref[...] += dgn

    wide, narrow = _row_spec(SSM_INNER), _row_spec(D_MODEL)
    resident = pl.BlockSpec(memory_space=pltpu.VMEM)
    bf = lambda w: jax.ShapeDtypeStruct((rows, w), BF16)
    f32 = lambda w: jax.ShapeDtypeStruct((rows, w), F32)
    return _call(body, name="tail", grid=(nb,),
                 in_specs=[wide, wide, _row_spec(D_MODEL, GATE_ATT_BLOCK), _row_spec(D_MODEL, GATE_SSM_BLOCK), narrow,
                           _x_spec(), _x_spec(), resident, resident, resident, _full((1, SSM_INNER)),
                           _full((1, D_MODEL))],
                 out_specs=[wide, narrow, narrow, narrow, narrow, narrow, wide, wide, narrow, narrow, narrow,
                            _full((8, LANES)), _full((1, D_MODEL)), _full((1, SSM_INNER))],
                 out_shape=[bf(SSM_INNER), bf(D_MODEL), bf(D_MODEL), bf(D_MODEL), bf(D_MODEL), f32(D_MODEL),
                            f32(SSM_INNER), bf(SSM_INNER), bf(D_MODEL), bf(D_MODEL), f32(D_MODEL),
                            jax.ShapeDtypeStruct((8, LANES), F32), jax.ShapeDtypeStruct((1, D_MODEL), F32),
                            jax.ShapeDtypeStruct((1, SSM_INNER), F32)],
                 sem=("arbitrary",))(y_ssd, proj, proj, proj, a_att, x, target, woa, wos, wo, g_norm, g_post)


_NT = (((1,), (1,)), ((), ()))
ALIBI_SLOPES = tuple(2.0 ** (-8.0 * (h + 1) / ATT_Q_HEADS) for h in range(ATT_Q_HEADS))
KV_WIDTH = ATT_KV_HEADS * HEAD_DIM
Q_BLOCK = SEG["q"][2] // D_MODEL
Z_ATT_BLOCK = SEG["z_att"][2] // D_MODEL
K_BLOCK = SEG["k"][2] // KV_WIDTH
V_BLOCK = SEG["v"][2] // KV_WIDTH
META_ROW_BLOCK = PAD_ROWS // N_META


@jax.custom_vjp
def _swap_halves(x):
    return pltpu.roll(x, HEAD_DIM, 1)


_swap_halves.defvjp(lambda x: (pltpu.roll(x, HEAD_DIM, 1), None), lambda _, g: (pltpu.roll(g, HEAD_DIM, 1),))


def _both_halves(t, half):
    first = lax.broadcasted_iota(jnp.int32, t.shape, 1) < HEAD_DIM
    sw = _swap_halves(t)
    return jnp.where(first, t, sw) if half == 0 else jnp.where(first, sw, t)


def _attn_rows(q, z, kp, kc, vp, vc, km, vm, sinks, n):
    rows = ATT_GROUP * BLOCK
    i = lax.broadcasted_iota(jnp.int32, (rows, BLOCK), 0) & (BLOCK - 1)
    j = lax.broadcasted_iota(jnp.int32, (rows, BLOCK), 1)
    rel_c = (i - j).astype(F32)
    rel_p = rel_c + float(BLOCK)
    nv = jnp.zeros((rows, BLOCK), jnp.int32) + n
    ok_c = (i >= j) & (nv >= 1)
    ok_p = (j > i) & (nv >= 2)
    im = lax.broadcasted_iota(jnp.int32, (rows, N_META), 0) & (BLOCK - 1)
    jm = lax.broadcasted_iota(jnp.int32, (rows, N_META), 1)
    ok_m = ((jnp.zeros((rows, N_META), jnp.int32) + n) >= 1) | (im >= PAD_ROWS + jm)
    first = lax.broadcasted_iota(jnp.int32, (BLOCK, LANES), 1) < HEAD_DIM
    neg = -jnp.inf
    outs = []
    for kv in range(ATT_KV_HEADS):
        tile, half = divmod(kv, 2)
        lanes = slice(tile * LANES, (tile + 1) * LANES)
        kc2, kp2, km2 = (_both_halves(t[:, lanes], half).astype(BF16) for t in (kc, kp, km))
        vc2, vp2, vm2 = (_both_halves(t[:, lanes], half).astype(BF16) for t in (vc, vp, vm))
        qs, slope, sk = [], [], []
        for pair in range(ATT_GROUP // 2):
            c0 = (kv * ATT_GROUP + 2 * pair) * HEAD_DIM
            qp = q[:, c0:c0 + LANES] * HEAD_DIM ** -0.5
            qs += [jnp.where(first, qp, 0.0), jnp.where(first, 0.0, qp)]
        for g in range(ATT_GROUP):
            slope.append(jnp.full((BLOCK, 1), ALIBI_SLOPES[kv * ATT_GROUP + g], F32))
            sk.append(jnp.broadcast_to(sinks[kv * ATT_GROUP + g], (BLOCK, 1)))
        qs = jnp.concatenate(qs, axis=0).astype(BF16)
        slope = jnp.concatenate(slope, axis=0)
        sk = jnp.concatenate(sk, axis=0)
        sc = jnp.where(ok_c, lax.dot_general(qs, kc2, _NT, preferred_element_type=F32) - slope * rel_c, neg)
        sp = jnp.where(ok_p, lax.dot_general(qs, kp2, _NT, preferred_element_type=F32) - slope * rel_p, neg)
        sm = jnp.where(ok_m, lax.dot_general(qs, km2, _NT, preferred_element_type=F32), neg)
        mx = jnp.maximum(jnp.maximum(jnp.max(sc, axis=1, keepdims=True), jnp.max(sp, axis=1, keepdims=True)),
                         jnp.maximum(jnp.max(sm, axis=1, keepdims=True), sk))
        mx = lax.stop_gradient(mx)
        ec, ep, em, es = jnp.exp(sc - mx), jnp.exp(sp - mx), jnp.exp(sm - mx), jnp.exp(sk - mx)
        den = (es + jnp.sum(ec, axis=1, keepdims=True) + jnp.sum(ep, axis=1, keepdims=True)
               + jnp.sum(em, axis=1, keepdims=True))
        inv = 1.0 / den
        o = (jnp.dot((ec * inv).astype(BF16), vc2, preferred_element_type=F32)
             + jnp.dot((ep * inv).astype(BF16), vp2, preferred_element_type=F32)
             + jnp.dot((em * inv).astype(BF16), vm2, preferred_element_type=F32))
        for pair in range(ATT_GROUP // 2):
            r0 = 2 * pair * BLOCK
            outs.append(jnp.where(first, o[r0:r0 + BLOCK], o[r0 + BLOCK:r0 + 2 * BLOCK]))
    return jnp.concatenate(outs, axis=1) * _silu(z)


def _attn_specs(nb, steps_clamped):
    def blk(t):
        return jnp.minimum(t, nb - 1) if steps_clamped else t

    wide = lambda col: pl.BlockSpec((BLOCK, D_MODEL), lambda t: (blk(t), col))
    cur = lambda col: pl.BlockSpec((BLOCK, KV_WIDTH), lambda t: (blk(t), col))
    prev = lambda col: pl.BlockSpec((BLOCK, KV_WIDTH), lambda t: (jnp.maximum(blk(t) - 1, 0), col))
    meta = lambda col: pl.BlockSpec((N_META, KV_WIDTH), lambda t: (META_ROW_BLOCK, col))
    sinks = pl.BlockSpec((ATT_Q_HEADS, 1, 1), lambda t: (0, 0, 0))
    return [wide(Q_BLOCK), wide(Z_ATT_BLOCK), prev(K_BLOCK), cur(K_BLOCK), prev(V_BLOCK), cur(V_BLOCK),
            meta(K_BLOCK), meta(V_BLOCK), sinks]


def attn_fwd(proj, sinks):
    nb = proj.shape[0] // BLOCK

    def body(q_ref, z_ref, kp_ref, kc_ref, vp_ref, vc_ref, km_ref, vm_ref, sk_ref, o_ref):
        o_ref[...] = _attn_rows(q_ref[...], z_ref[...], kp_ref[...], kc_ref[...], vp_ref[...], vc_ref[...],
                                km_ref[...], vm_ref[...], tuple(sk_ref[h] for h in range(ATT_Q_HEADS)),
                                pl.program_id(0)).astype(BF16)

    return _call(body, name="attn_fwd", grid=(nb,), in_specs=_attn_specs(nb, False), out_specs=_row_spec(D_MODEL),
                 out_shape=jax.ShapeDtypeStruct((nb * BLOCK, D_MODEL), BF16), sem=("parallel",))(*([proj] * 8), sinks)


def attn_bwd(da, proj, sinks):
    nb = proj.shape[0] // BLOCK
    last = nb - 1
    wide = pl.BlockSpec((BLOCK, D_MODEL), lambda t: (jnp.minimum(t, last), 0))
    done = pl.BlockSpec((BLOCK, KV_WIDTH), lambda t: (jnp.maximum(t - 1, 0), 0))
    meta = _full((N_META, KV_WIDTH))
    par = _full((ATT_Q_HEADS, 1, 1))

    def body(da_ref, q_ref, z_ref, kp_ref, kc_ref, vp_ref, vc_ref, km_ref, vm_ref, sk_ref,
             dq_ref, dz_ref, dk_ref, dv_ref, dkm_ref, dvm_ref, dsk_ref, ck_ref, cv_ref):
        t = pl.program_id(0)

        @pl.when(t == 0)
        def _():
            ck_ref[...] = jnp.zeros_like(ck_ref)
            cv_ref[...] = jnp.zeros_like(cv_ref)
            dkm_ref[...] = jnp.zeros_like(dkm_ref)
            dvm_ref[...] = jnp.zeros_like(dvm_ref)
            dsk_ref[...] = jnp.zeros_like(dsk_ref)

        @pl.when(t < nb)
        def _():
            def f(q, z, kp, kc, vp, vc, km, vm, sk):
                return _attn_rows(q, z, kp, kc, vp, vc, km, vm, sk, t)

            _, vjp = jax.vjp(f, q_ref[...], z_ref[...], kp_ref[...], kc_ref[...], vp_ref[...], vc_ref[...],
                             km_ref[...], vm_ref[...], tuple(sk_ref[h] for h in range(ATT_Q_HEADS)))
            dq, dz, dkp, dkc, dvp, dvc, dkm, dvm, dsk = vjp(da_ref[...])
            dq_ref[...] = dq.astype(BF16)
            dz_ref[...] = dz.astype(BF16)
            for h in range(ATT_Q_HEADS):
                dsk_ref[h] += dsk[h]
            dk_ref[...] = ck_ref[...] + dkp
            dv_ref[...] = cv_ref[...] + dvp
            ck_ref[...] = dkc
            cv_ref[...] = dvc
            dkm_ref[...] += dkm
            dvm_ref[...] += dvm

        @pl.when(t == nb)
        def _():
            dk_ref[...] = ck_ref[...]
            dv_ref[...] = cv_ref[...]

    rows = nb * BLOCK
    return _call(body, name="attn_bwd", grid=(nb + 1,), in_specs=[wide] + _attn_specs(nb, True),
                 out_specs=[wide, wide, done, done, meta, meta, par],
                 out_shape=[jax.ShapeDtypeStruct((rows, D_MODEL), BF16), jax.ShapeDtypeStruct((rows, D_MODEL), BF16),
                            jax.ShapeDtypeStruct((rows, KV_WIDTH), F32), jax.ShapeDtypeStruct((rows, KV_WIDTH), F32),
                            jax.ShapeDtypeStruct((N_META, KV_WIDTH), F32), jax.ShapeDtypeStruct((N_META, KV_WIDTH), F32),
                            jax.ShapeDtypeStruct(sinks.shape, F32)],
                 scratch=[pltpu.VMEM((BLOCK, KV_WIDTH), F32), pltpu.VMEM((BLOCK, KV_WIDTH), F32)],
                 sem=("arbitrary",))(da, *([proj] * 8), sinks)


XBC_BLOCK0 = SEG["xbc"][2] // D_MODEL
CONV_COL_BLOCKS = CONV_DIM // D_MODEL
DT_TILE = SEG["dt"][2] // LANES


HALO = 8


def _conv_rows(length):
    return 544 if length % 544 == 0 else BLOCK


def _shift_rows(cur, before, j):
    if j == 0:
        return cur
    n = cur.shape[0]
    row = lax.broadcasted_iota(jnp.int32, cur.shape, 0)
    head = pltpu.roll(before, j, 0)
    if n > HALO:
        head = jnp.concatenate([head, jnp.zeros((n - HALO, cur.shape[1]), cur.dtype)], axis=0)
    return jnp.where(row >= j, pltpu.roll(cur, j, 0), head)


def _conv_pre(cur, before, w_ref, b_ref):
    pre = b_ref[...] + w_ref[CONV_WIDTH - 1:CONV_WIDTH, :] * cur
    for k in range(CONV_WIDTH - 1):
        pre = pre + w_ref[k:k + 1, :] * _shift_rows(cur, before, CONV_WIDTH - 1 - k)
    return pre


def _conv_specs(steps, rows, col0=0):
    first = XBC_BLOCK0 + col0
    halos = rows // HALO
    cur = pl.BlockSpec((rows, D_MODEL), lambda j, i: (i, first + j))
    before = pl.BlockSpec((HALO, D_MODEL), lambda j, i: (jnp.maximum(i * halos - 1, 0), first + j))
    after = pl.BlockSpec((HALO, D_MODEL), lambda j, i: (jnp.minimum(i + 1, steps - 1) * halos, first + j))
    return cur, before, after


def _valid_rows(i, rows):
    row = lax.broadcasted_iota(jnp.int32, (rows, D_MODEL), 0)
    return jnp.maximum((row >= PAD_ROWS).astype(F32), jnp.where(i > 0, 1.0, 0.0))


def conv_fwd(proj, conv_w, conv_b):
    rows = _conv_rows(proj.shape[0])
    steps = proj.shape[0] // rows
    cur, before, _ = _conv_specs(steps, rows)

    def body(c_ref, p_ref, w_ref, b_ref, o_ref):
        i = pl.program_id(1)
        pre = _conv_pre(c_ref[...], p_ref[...] * jnp.where(i > 0, 1.0, 0.0), w_ref, b_ref)
        o_ref[...] = _silu(pre) * _valid_rows(i, rows)

    return _call(body, name="conv_fwd", grid=(CONV_COL_BLOCKS, steps),
                 in_specs=[cur, before, pl.BlockSpec((CONV_WIDTH, D_MODEL), lambda j, i: (0, j)),
                           pl.BlockSpec((1, D_MODEL), lambda j, i: (0, j))],
                 out_specs=pl.BlockSpec((rows, D_MODEL), lambda j, i: (i, j)),
                 out_shape=jax.ShapeDtypeStruct((proj.shape[0], CONV_DIM), F32),
                 sem=("parallel", "parallel"))(proj, proj, conv_w, conv_b)


def conv_bwd(name, dparts, col0, proj, conv_w, conv_b):
    rows = _conv_rows(proj.shape[0])
    steps = proj.shape[0] // rows
    last = steps - 1
    ncol = sum(d.shape[1] for d in dparts) // D_MODEL
    np_ = len(dparts)
    cur, before, after = _conv_specs(steps, rows, col0)
    dcur = [pl.BlockSpec((rows, d.shape[1] // ncol), lambda j, i: (i, j)) for d in dparts]
    dafter = [pl.BlockSpec((HALO, d.shape[1] // ncol), lambda j, i: (jnp.minimum(i + 1, last) * (rows // HALO), j))
              for d in dparts]
    out_cur = pl.BlockSpec((rows, D_MODEL), lambda j, i: (i, j))
    wspec = pl.BlockSpec((CONV_WIDTH, D_MODEL), lambda j, i: (0, col0 + j))
    bspec = pl.BlockSpec((1, D_MODEL), lambda j, i: (0, col0 + j))
    wout = pl.BlockSpec((CONV_WIDTH, D_MODEL), lambda j, i: (0, j))
    bout = pl.BlockSpec((1, D_MODEL), lambda j, i: (0, j))

    def body(*refs):
        dc_refs, da_refs = refs[:np_], refs[np_:2 * np_]
        c_ref, p_ref, a_ref, w_ref, b_ref, du_ref, dw_ref, db_ref = refs[2 * np_:]
        i = pl.program_id(1)
        row = lax.broadcasted_iota(jnp.int32, (rows, D_MODEL), 0)
        curv = c_ref[...]
        beforev = p_ref[...] * jnp.where(i > 0, 1.0, 0.0)
        side_by_side = lambda rs: rs[0][...] if np_ == 1 else jnp.concatenate([r[...] for r in rs], axis=1)

        def dpre_of(pre, d):
            s = _sigmoid(pre)
            return d * (s * (1.0 + pre * (1.0 - s)))

        dp_c = dpre_of(_conv_pre(curv, beforev, w_ref, b_ref), side_by_side(dc_refs) * _valid_rows(i, rows))
        dp_a = dpre_of(_conv_pre(a_ref[...], curv[rows - HALO:], w_ref, b_ref),
                       side_by_side(da_refs) * jnp.where(i < last, 1.0, 0.0))
        du = w_ref[CONV_WIDTH - 1:CONV_WIDTH, :] * dp_c
        for j in range(1, CONV_WIDTH):
            tail = jnp.concatenate([jnp.zeros((rows - HALO, D_MODEL), F32), pltpu.roll(dp_a, HALO - j, 0)], axis=0)
            up = jnp.where(row < rows - j, pltpu.roll(dp_c, rows - j, 0), tail)
            du = du + w_ref[CONV_WIDTH - 1 - j:CONV_WIDTH - j, :] * up
        du_ref[...] = du.astype(BF16)

        @pl.when(i == 0)
        def _():
            dw_ref[...] = jnp.zeros_like(dw_ref)
            db_ref[...] = jnp.zeros_like(db_ref)

        for k in range(CONV_WIDTH):
            dw_ref[k:k + 1, :] += jnp.sum(dp_c * _shift_rows(curv, beforev, CONV_WIDTH - 1 - k), axis=0, keepdims=True)
        db_ref[...] += jnp.sum(dp_c, axis=0, keepdims=True)

    width = ncol * D_MODEL
    return _call(body, name=name, grid=(ncol, steps),
                 in_specs=dcur + dafter + [cur, before, after, wspec, bspec], out_specs=[out_cur, wout, bout],
                 out_shape=[jax.ShapeDtypeStruct((proj.shape[0], width), BF16),
                            jax.ShapeDtypeStruct((CONV_WIDTH, width), F32), jax.ShapeDtypeStruct((1, width), F32)],
                 sem=("parallel", "arbitrary"))(*dparts, *dparts, proj, proj, proj, conv_w, conv_b)


def _head_expand():
    e = np.zeros((LANES, SSM_INNER), np.float32)
    for h in range(SSM_HEADS):
        e[h, h * HEAD_DIM:(h + 1) * HEAD_DIM] = 1.0
    return jnp.asarray(e, dtype=BF16)


def _softplus(x):
    return jnp.maximum(x, 0.0) + jnp.log(1.0 + jnp.exp(-jnp.abs(x)))


def _bf16_parts(x):
    hi = x.astype(BF16)
    rest = x - hi.astype(F32)
    mid = rest.astype(BF16)
    return hi, mid, (rest - mid.astype(F32)).astype(BF16)


@jax.custom_vjp
def _times_01(x, m):
    return sum(jnp.dot(p, m, preferred_element_type=F32) for p in _bf16_parts(x))


def _times_01_bwd(m, g):
    return sum(lax.dot_general(p, m, _NT, preferred_element_type=F32) for p in _bf16_parts(g)), jnp.zeros_like(m)


_times_01.defvjp(lambda x, m: (_times_01(x, m), m), _times_01_bwd)


def _causal_ones():
    l = lax.broadcasted_iota(jnp.int32, (BLOCK, BLOCK), 0)
    s = lax.broadcasted_iota(jnp.int32, (BLOCK, BLOCK), 1)
    return (l >= s).astype(BF16)


@jax.custom_vjp
def _cumsum_rows(a):
    return sum(jnp.dot(_causal_ones(), p, preferred_element_type=F32) for p in _bf16_parts(a))


def _cumsum_rows_bwd(_, g):
    tn = (((0,), (0,)), ((), ()))
    return (sum(lax.dot_general(_causal_ones(), p, tn, preferred_element_type=F32) for p in _bf16_parts(g)),)


_cumsum_rows.defvjp(lambda a: (_cumsum_rows(a), None), _cumsum_rows_bwd)


def _ssd_heads(dt_tile, bias, alog, dsk, expand):
    dt = _softplus(dt_tile + bias)
    a = dt * (-jnp.exp(alog))
    one_row = lambda v: jnp.broadcast_to(v, (HALO, LANES))
    return _times_01(jnp.concatenate([dt, _cumsum_rows(a), one_row(jnp.sum(a, axis=0, keepdims=True)), one_row(dsk)],
                                     axis=0), expand)


HEADS_ROWS = 2 * BLOCK + 2 * HALO


def _ssd_group(xs, per_lane, bg, cg, state):
    l = lax.broadcasted_iota(jnp.int32, (BLOCK, BLOCK), 0)
    s = lax.broadcasted_iota(jnp.int32, (BLOCK, BLOCK), 1)
    causal = l >= s
    first_head = s < HEAD_DIM
    dtx, cs = per_lane[0:BLOCK], per_lane[BLOCK:2 * BLOCK]
    tot, dsk = per_lane[2 * BLOCK:2 * BLOCK + 1], per_lane[2 * BLOCK + HALO:2 * BLOCK + HALO + 1]
    bb, cb16 = bg.astype(BF16), cg.astype(BF16)
    cb = lax.dot_general(cb16, bb, _NT, preferred_element_type=F32)
    xr = xs * dtx
    y_diag = []
    for p in range(GROUP_W // LANES):
        lanes = slice(p * LANES, (p + 1) * LANES)
        c_pair = cs[:, lanes]
        c_swap = _swap_halves(c_pair)
        m = []
        for c_head in (jnp.where(first_head, c_pair, c_swap), jnp.where(first_head, c_swap, c_pair)):
            m.append(cb * jnp.exp(jnp.where(causal, c_head - c_head.T, -jnp.inf)))
        x_pair = xr[:, lanes]
        x_diag = jnp.concatenate([jnp.where(first_head, x_pair, 0.0), jnp.where(first_head, 0.0, x_pair)], axis=0)
        y_diag.append(jnp.dot(jnp.concatenate(m, axis=1).astype(BF16), x_diag.astype(BF16),
                              preferred_element_type=F32))
    st = lax.dot_general(bb, (xr * jnp.exp(tot - cs)).astype(BF16), (((0,), (0,)), ((), ())),
                         preferred_element_type=F32)
    new_state = state * jnp.exp(tot) + st
    y_off = jnp.dot(cb16, state.astype(BF16), preferred_element_type=F32) * jnp.exp(cs)
    return jnp.concatenate(y_diag, axis=1) + y_off + dsk * xs, new_state


BC_WIDTH = SSM_GROUPS * SSM_STATE


def _ssd_specs(chunk):
    xs = pl.BlockSpec((BLOCK, SSM_INNER), lambda c: (chunk(c), 0))
    dt = pl.BlockSpec((BLOCK, LANES), lambda c: (chunk(c), DT_TILE))
    expand = _full((LANES, SSM_INNER))
    b = pl.BlockSpec((BLOCK, BC_WIDTH), lambda c: (chunk(c), SSM_INNER // BC_WIDTH))
    cc = pl.BlockSpec((BLOCK, BC_WIDTH), lambda c: (chunk(c), SSM_INNER // BC_WIDTH + 1))
    par = _full((1, LANES))
    state = pl.BlockSpec((None, SSM_STATE, SSM_INNER), lambda c: (chunk(c), 0, 0))
    return xs, dt, expand, b, cc, par, state


def _group_lanes(g):
    return slice(g * GROUP_W, (g + 1) * GROUP_W), slice(g * SSM_STATE, (g + 1) * SSM_STATE)


def ssd_fwd(xbc, proj, expand, bias, alog, dsk):
    nb = xbc.shape[0] // BLOCK
    xs, dt, ex, b, cc, par, state = _ssd_specs(lambda c: c)

    def body(x_ref, dt_ref, e_ref, bi_ref, al_ref, dk_ref, b_ref, c_ref, y_ref, sp_ref, st_ref):
        @pl.when(pl.program_id(0) == 0)
        def _():
            st_ref[...] = jnp.zeros_like(st_ref)

        per_lane = _ssd_heads(dt_ref[...], bi_ref[...], al_ref[...], dk_ref[...], e_ref[...])
        for g in range(SSM_GROUPS):
            wide, tile = _group_lanes(g)
            entering = st_ref[:, wide]
            sp_ref[:, wide] = entering
            y_ref[:, wide], st_ref[:, wide] = _ssd_group(x_ref[:, wide], per_lane[:, wide], b_ref[:, tile],
                                                         c_ref[:, tile], entering)

    return _call(body, name="ssd_fwd", grid=(nb,), in_specs=[xs, dt, ex, par, par, par, b, cc],
                 out_specs=[xs, state],
                 out_shape=[jax.ShapeDtypeStruct((nb * BLOCK, SSM_INNER), F32),
                            jax.ShapeDtypeStruct((nb, SSM_STATE, SSM_INNER), F32)],
                 scratch=[pltpu.VMEM((SSM_STATE, SSM_INNER), F32)],
                 sem=("arbitrary",))(xbc, proj, expand, bias, alog, dsk, xbc, xbc)


def ssd_bwd(dy, xbc, proj, expand, bias, alog, dsk, states, comm):
    nb = xbc.shape[0] // BLOCK
    last = nb - 1
    xs, dt, ex, b, cc, par, state = _ssd_specs(lambda c: last - c)
    tile = pl.BlockSpec((BLOCK, LANES), lambda c: (last - c, 0))
    nspec = pl.BlockSpec((BLOCK, BC_WIDTH), lambda c: (last - c, 0))

    def body(dy_ref, x_ref, dt_ref, e_ref, bi_ref, al_ref, dk_ref, b_ref, c_ref, sp_ref,
             dx_ref, ddt_ref, db_ref, dc_ref, dbi_ref, dal_ref, ddk_ref, ds_ref):
        @pl.when(pl.program_id(0) == 0)
        def _():
            ds_ref[...] = jnp.zeros_like(ds_ref)
            dbi_ref[...] = jnp.zeros_like(dbi_ref)
            dal_ref[...] = jnp.zeros_like(dal_ref)
            ddk_ref[...] = jnp.zeros_like(ddk_ref)

        expand = e_ref[...]
        per_lane, heads_vjp = jax.vjp(lambda t, bi, al, dk: _ssd_heads(t, bi, al, dk, expand), dt_ref[...], bi_ref[...],
                                      al_ref[...], dk_ref[...])
        d_per_lane = []
        for g in range(SSM_GROUPS):
            wide, tile_lanes = _group_lanes(g)
            _, vjp = jax.vjp(_ssd_group, x_ref[:, wide], per_lane[:, wide], b_ref[:, tile_lanes], c_ref[:, tile_lanes],
                             sp_ref[:, wide])
            (dx_ref[:, wide], d_lanes, db_ref[:, tile_lanes], dc_ref[:, tile_lanes],
             ds_ref[:, wide]) = vjp((dy_ref[:, wide], ds_ref[:, wide]))
            d_per_lane.append(d_lanes)
        ddt, dbi, dal, ddk = heads_vjp(jnp.concatenate(d_per_lane, axis=1))
        ddt_ref[...] = ddt.astype(BF16)
        dbi_ref[...] += dbi
        dal_ref[...] += dal
        ddk_ref[...] += ddk

    def at():
        c = pl.program_id(0)
        return c == 0, c == 0, c == last

    par_shape = jax.ShapeDtypeStruct((1, LANES), F32)
    return _call_with_comm(
        body, comm, at, (dy, xbc, proj, expand, bias, alog, dsk, xbc, xbc, states), name="ssd_bwd",
        grid=(nb,), in_specs=[xs, xs, dt, ex, par, par, par, b, cc, state],
        out_specs=[xs, tile, nspec, nspec, par, par, par],
        out_shape=[jax.ShapeDtypeStruct((nb * BLOCK, SSM_INNER), F32), jax.ShapeDtypeStruct((nb * BLOCK, LANES), BF16),
                   jax.ShapeDtypeStruct((nb * BLOCK, BC_WIDTH), F32), jax.ShapeDtypeStruct((nb * BLOCK, BC_WIDTH), F32),
                   par_shape, par_shape, par_shape],
        scratch=[pltpu.VMEM((SSM_STATE, SSM_INNER), F32)])


SLAB_ROWS = 16
SLAB_META_ROW = 8
SLAB_META_SHAPE = (8, 2 * D_MODEL)
SLAB_LOSS_ROW = 7


def pack_small(dcw, dcb, dgpre, dgpost, dbias, dalog, ddsk, dsinks, dgn, dmeta, loss_tile):
    def body(cw, cb, gpre, gpost, dtb, al, dk, sk, gn, meta, loss, o_ref):
        o_ref[...] = jnp.zeros_like(o_ref)
        o_ref[SLAB_LOSS_ROW:SLAB_LOSS_ROW + 1, 0:LANES] = loss[0:1, :]
        o_ref[0:CONV_WIDTH, :] = cw[...]
        o_ref[4:5, :] = cb[...]
        o_ref[5:6, 0:1024] = gpre[...]
        o_ref[5:6, 1024:2048] = gpost[...]
        o_ref[5:6, 2048:2176] = dtb[...]
        o_ref[5:6, 2176:2304] = al[...]
        o_ref[5:6, 2304:2432] = dk[...]
        o_ref[5:6, 2432:2560] = sk[...]
        o_ref[6:7, 0:SSM_INNER] = gn[...]
        o_ref[SLAB_META_ROW:SLAB_ROWS, 0:SLAB_META_SHAPE[1]] = meta[...]

    args = (dcw, dcb, dgpre, dgpost, dbias, dalog, ddsk, dsinks, dgn, dmeta, loss_tile)
    return _call(body, name="pack_small", in_specs=[_full(a.shape) for a in args],
                 out_specs=_full((SLAB_ROWS, CONV_DIM)), out_shape=jax.ShapeDtypeStruct((SLAB_ROWS, CONV_DIM), F32))(*args)


def _lane_tile(v):
    return jnp.pad(v, ((0, 0), (0, LANES - v.shape[1])))


def kernel(x, meta_tokens, g_pre, w_in, conv_w, conv_b, dt_bias, a_log, d_skip, attn_sinks, g_ssm_norm, w_out_att, w_out_ssm, w_out, g_post, loss_target, m_meta_tokens, m_g_pre, m_w_in, m_conv_w, m_conv_b, m_dt_bias, m_a_log, m_d_skip, m_attn_sinks, m_g_ssm_norm, m_w_out_att, m_w_out_ssm, m_w_out, m_g_post, v_meta_tokens, v_g_pre, v_w_in, v_conv_w, v_conv_b, v_dt_bias, v_a_log, v_d_skip, v_attn_sinks, v_g_ssm_norm, v_w_out_att, v_w_out_ssm, v_w_out, v_g_post):
    chip = _chip_index()

    conv_w_rows = jnp.pad(conv_w[0], ((0, 2 * 8 - CONV_WIDTH), (0, 0)))
    w_in_t, m_w_in_t, v_w_in_t = w_in[0].T, m_w_in[0].T, v_w_in[0].T
    gathered_w_in, g_conv_w, g_meta = run_comm("gather_w_in",
                                               TwoLevelGather([pack_w_in(w_in_t), conv_w_rows, meta_tokens]))
    w_all_t = unpack_w_in(gathered_w_in)
    cw_full = g_conv_w[:, :CONV_WIDTH].transpose(1, 0, 2).reshape(CONV_WIDTH, CONV_DIM)
    meta_full = g_meta.transpose(1, 0, 2).reshape(N_META, D_MODEL)
    behind_w_in = 0.0 * g_meta[0, 0, 0]
    w_out_flight, w_out_started = chip_exchange_start(
        "gather_w_out_start", [(w[0] + behind_w_in).astype(BF16) for w in (w_out_att, w_out_ssm, w_out)], False)

    h, u = prep(x, meta_full, g_pre)
    proj = project("in_proj", u, w_all_t, w_out_started)

    sinks3 = attn_sinks.reshape(ATT_Q_HEADS, 1, 1)
    a_att = attn_fwd(proj, sinks3)

    xbc = conv_fwd(proj, cw_full, conv_b)
    expand = _head_expand()
    head_pars = (_lane_tile(dt_bias), _lane_tile(a_log), _lane_tile(d_skip))
    y_ssd, states = ssd_fwd(xbc, proj, expand, *head_pars)
    woa, wos, wo = [g.reshape(-1, D_MODEL) for g in chip_exchange_wait("gather_w_out_wait", w_out_flight, False, y_ssd)]

    (yn, merged, dout, dy_att, dy_ssm, da_att, dy_ssd, dz_ssm, dga, dgs, dres, loss_tile, dg_post, dgn) = tail(
        y_ssd, proj, a_att, x, loss_target, woa, wos, wo, g_ssm_norm, g_post)

    dwo = mm_tn("out_proj_dw", merged, dout)
    dwoa = mm_tn("att_out_dw", a_att, dy_att)
    dwos = mm_tn("ssm_out_dw", yn, dy_ssm)
    dq, dz_att, dk, dv, dkmeta, dvmeta, dsinks3 = attn_bwd(da_att, proj, sinks3)
    dk = dk.at[PAD_ROWS:BLOCK].add(dkmeta).astype(BF16)
    dv = dv.at[PAD_ROWS:BLOCK].add(dvmeta).astype(BF16)

    def pieces(g):
        return g.reshape(4, 2, g.shape[0] // 8, g.shape[1])

    def to_owner(g):
        return (lambda ref, dev: ref.at[_chip_of(dev), dev[2]], (g.shape[0] // 8, g.shape[1]))

    (dxs, ddt_tile, dbg, dcg, dbias, dalog, ddsk), sent_w_out = ssd_bwd(
        dy_ssd, xbc, proj, expand, *head_pars, states,
        DirectExchange([pieces(dwoa), pieces(dwos), pieces(dwo)], [to_owner(dwoa), to_owner(dwos), to_owner(dwo)],
                       ALL_MASKS, "dev", 8))
    dxs_raw, dcw_xs, dcb_xs = conv_bwd("conv_bwd_x", [dxs], 0, proj, cw_full, conv_b)
    dbc_raw, dcw_bc, dcb_bc = conv_bwd("conv_bwd_bc", [dbg, dcg], SSM_INNER // D_MODEL, proj, cw_full, conv_b)
    dcw = jnp.concatenate([dcw_xs, dcw_bc], axis=1)
    dcb = jnp.concatenate([dcb_xs, dcb_bc], axis=1)

    narrow = jnp.concatenate([dk, dv, ddt_tile, jnp.zeros((dk.shape[0], N_ACT - N_ALIGNED), BF16)], axis=1)
    dproj = [dz_ssm, dxs_raw, dbc_raw, dq, dz_att, dga, dgs, narrow]
    half_rows = PACK_W // 2
    partial = weight_grad_t("in_proj_dw", dproj, u).reshape(4, 2, half_rows, D_MODEL)
    from_sibling, = run_comm("pair_grads", DirectExchange(
        [partial], [(lambda ref, dev: ref.at[pl.ds(0, 4), dev[2]], (4, half_rows, D_MODEL))], SIBLING_MASK, "core", 2,
        keep_own=False))
    chip_sum = sum_pair(partial, from_sibling)
    grads_flight, started = chip_exchange_start("reduce_w_in_start", [chip_sum], True)
    du = project_back("in_proj_dx", dproj, w_all_t, started)
    grad_x, dmeta, dg_pre = prep_bwd(h, du, dres, g_pre)

    halves = [sum_slots("sum_" + nm, r) for nm, r in zip(("w_out_att", "w_out_ssm", "w_out"), sent_w_out)]
    shared = run_comm("share_w_out", DirectExchange(halves, [None] * 3, SIBLING_MASK, "core", 2))
    g_woa, g_wos, g_wo = [f.reshape(2 * f.shape[1], f.shape[2]) for f in shared]
    d_woa, nm_woa, nv_woa = adamw_rows("adamw_w_out_att", g_woa, w_out_att[0], m_w_out_att[0], v_w_out_att[0])
    d_wos, nm_wos, nv_wos = adamw_rows("adamw_w_out_ssm", g_wos, w_out_ssm[0], m_w_out_ssm[0], v_w_out_ssm[0])
    d_wo, nm_wo, nv_wo = adamw_rows("adamw_w_out", g_wo, w_out[0], m_w_out[0], v_w_out[0])

    sent_w_in, = chip_exchange_wait("reduce_w_in_wait", grads_flight, True, d_wos)
    slab = pack_small(dcw, dcb, dg_pre, dg_post, dbias, dalog, ddsk, _lane_tile(dsinks3.reshape(1, ATT_Q_HEADS)), dgn,
                      dmeta.reshape(SLAB_META_SHAPE), loss_tile)
    shared_w_in, slabs = run_comm("share_w_in", Both(
        DirectExchange([sum_slots("sum_w_in", sent_w_in)], [None], SIBLING_MASK, "core", 2),
        DirectExchange([slab], [None], ALL_MASKS, "dev", 8)))
    small = sum_slots("sum_small", slabs)
    loss = small[SLAB_LOSS_ROW, 0]
    g_w_in, d_w_in, nm_w_in, nv_w_in = [
        a.T for a in adamw_w_in(shared_w_in.reshape(PACK_W, D_MODEL), w_in_t, m_w_in_t, v_w_in_t)]

    cw_cols = CONV_DIM // 4
    meta_cols = D_MODEL // 4
    g_small = {
        "meta_tokens": lax.dynamic_slice(
            small[SLAB_META_ROW:SLAB_ROWS, 0:SLAB_META_SHAPE[1]].reshape(N_META, D_MODEL), (0, chip * meta_cols),
            (N_META, meta_cols)),
        "g_pre": small[5:6, 0:1024],
        "conv_w": lax.dynamic_slice(small, (0, chip * cw_cols), (CONV_WIDTH, cw_cols)),
        "conv_b": small[4:5, :],
        "dt_bias": small[5:6, 2048:2048 + SSM_HEADS],
        "a_log": small[5:6, 2176:2176 + SSM_HEADS],
        "d_skip": small[5:6, 2304:2304 + SSM_HEADS],
        "attn_sinks": small[5:6, 2432:2432 + ATT_Q_HEADS],
        "g_ssm_norm": small[6:7, 0:SSM_INNER],
        "g_post": small[5:6, 1024:2048],
    }
    names = list(g_small)
    w_small = dict(meta_tokens=meta_tokens, g_pre=g_pre, conv_w=conv_w[0], conv_b=conv_b, dt_bias=dt_bias, a_log=a_log,
                   d_skip=d_skip, attn_sinks=attn_sinks, g_ssm_norm=g_ssm_norm, g_post=g_post)
    m_small = dict(meta_tokens=m_meta_tokens, g_pre=m_g_pre, conv_w=m_conv_w[0], conv_b=m_conv_b, dt_bias=m_dt_bias,
                   a_log=m_a_log, d_skip=m_d_skip, attn_sinks=m_attn_sinks, g_ssm_norm=m_g_ssm_norm, g_post=m_g_post)
    v_small = dict(meta_tokens=v_meta_tokens, g_pre=v_g_pre, conv_w=v_conv_w[0], conv_b=v_conv_b, dt_bias=v_dt_bias,
                   a_log=v_a_log, d_skip=v_d_skip, attn_sinks=v_attn_sinks, g_ssm_norm=v_g_ssm_norm, g_post=v_g_post)
    upd = dict(zip(names, adamw_small([g_small[k] for k in names], [w_small[k] for k in names],
                                      [m_small[k] for k in names], [v_small[k] for k in names])))

    lead = {"conv_w"}

    def shaped(name, a):
        return a[None] if name in lead else a

    grads = dict(g_small, w_in=g_w_in, w_out_att=g_woa, w_out_ssm=g_wos, w_out=g_wo)
    deltas = dict({k: upd[k][0] for k in names}, w_in=d_w_in, w_out_att=d_woa, w_out_ssm=d_wos, w_out=d_wo)
    new_m = dict({k: upd[k][1] for k in names}, w_in=nm_w_in, w_out_att=nm_woa, w_out_ssm=nm_wos, w_out=nm_wo)
    new_v = dict({k: upd[k][2] for k in names}, w_in=nv_w_in, w_out_att=nv_woa, w_out_ssm=nv_wos, w_out=nv_wo)
    lead |= {"w_in", "w_out_att", "w_out_ssm", "w_out"}
    order = ["meta_tokens", "g_pre", "w_in", "conv_w", "conv_b", "dt_bias", "a_log", "d_skip", "attn_sinks",
             "g_ssm_norm", "w_out_att", "w_out_ssm", "w_out", "g_post"]
    outs = [loss, grad_x]
    for group in (grads, deltas, new_m, new_v):
        outs += [shaped(k, group[k]) for k in order]
    return tuple(outs)
```

```python
import numpy as np
import jax
import jax.numpy as jnp
from jax import lax
from jax.experimental import pallas as pl
from jax.experimental.pallas import tpu as pltpu

F32 = jnp.float32
BF16 = jnp.bfloat16

D_MODEL = 1024
N_META = 16
BLOCK = 128
PAD_ROWS = BLOCK - N_META
NORM_EPS = 1e-6
HEAD_DIM = 64
ATT_Q_HEADS = 16
ATT_KV_HEADS = 4
ATT_GROUP = 4
SSM_INNER = 2048
SSM_HEADS = 32
SSM_GROUPS = 4
SSM_STATE = 128
CONV_WIDTH = 4
CONV_DIM = 3072
LANES = 128

ADAM_LR = 0.001
ADAM_B1 = 0.9
ADAM_B2 = 0.999
ADAM_EPS = 1e-08
ADAM_WD = 0.01
ADAM_STEP = 10

VMEM_LIMIT = 48 * 1024 * 1024

SHARD_W = 2440
PACK_W = 2560
SHARD_STRIDE = 2432
N_ALIGNED = 9856
N_ACT = 10240
SEG = {
    "q": (0, 1024, 5120), "k": (1024, 256, 9216), "v": (1280, 256, 9472), "z_att": (1536, 1024, 6144),
    "z_ssm": (2560, 2048, 0), "xbc": (4608, 3072, 2048), "dt": (7680, 128, 9728),
    "gate_att": (7808, 1024, 7168), "gate_ssm": (8832, 1024, 8192),
}
DT_STORED_START = 7680
DT_PAD = LANES - SSM_HEADS


def _act_col(aligned_col):
    for a0, w, p0 in SEG.values():
        if a0 <= aligned_col < a0 + w:
            return p0 + aligned_col - a0
    raise ValueError(aligned_col)


def _call(body, *, name, out_shape, in_specs, out_specs, grid=(), scratch=(), sem=None, aliases=None):
    return pl.pallas_call(
        body, out_shape=out_shape, grid=grid, in_specs=in_specs, out_specs=out_specs, scratch_shapes=list(scratch),
        name=name, input_output_aliases=aliases or {},
        compiler_params=pltpu.CompilerParams(dimension_semantics=sem, vmem_limit_bytes=VMEM_LIMIT))


def _full(shape):
    n = len(shape)
    return pl.BlockSpec(shape, lambda *_: (0,) * n)


def _chip_index():
    return lax.axis_index("x") * 2 + lax.axis_index("y")


_sigmoid = jax.nn.sigmoid


def _silu(z):
    return z * _sigmoid(z)


def _rms(x, g):
    return x * lax.rsqrt(jnp.mean(x * x, axis=-1, keepdims=True) + NORM_EPS) * g


def _peer(mask):
    x, y, c = lax.axis_index("x"), lax.axis_index("y"), lax.axis_index("c")
    return ((1 - x) if mask & 4 else x, (1 - y) if mask & 2 else y, (1 - c) if mask & 1 else c)


def _me():
    return lax.axis_index("x"), lax.axis_index("y"), lax.axis_index("c")


def _chip_of(dev):
    return 2 * dev[0] + dev[1]


CHIP_MASKS = (4, 2, 6)
ALL_MASKS = (1, 2, 3, 4, 5, 6, 7)
SIBLING_MASK = (1,)


def _remote(src, dst, send_sem, recv_sem, dev):
    return pltpu.make_async_remote_copy(src_ref=src, dst_ref=dst, send_sem=send_sem, recv_sem=recv_sem,
                                        device_id=dev, device_id_type=pl.DeviceIdType.MESH)


class _StagedCopy:
    def __init__(self, src, stage, dst, load_sem, store_sem):
        self.load = pltpu.make_async_copy(src, stage, load_sem)
        self.store = pltpu.make_async_copy(stage, dst, store_sem)

    def start(self):
        self.load.start()
        self.load.wait()
        self.store.start()

    def wait(self):
        self.store.wait()


class DirectExchange:
    def __init__(self, arrays, pieces, masks, slot_kind, nslots, keep_own=True):
        self.arrays, self.pieces, self.masks, self.slot_kind = list(arrays), list(pieces), masks, slot_kind
        self.keep_own = keep_own
        n, nk = len(arrays), len(masks)
        shapes = [a.shape if p is None else p[1] for a, p in zip(arrays, pieces)]
        self.out_shape = [jax.ShapeDtypeStruct((nslots,) + tuple(s), a.dtype) for s, a in zip(shapes, arrays)]
        self.scratch = [pltpu.SemaphoreType.DMA((n * nk,)), pltpu.SemaphoreType.DMA((n * nk,))]
        if keep_own:
            self.scratch += [pltpu.SemaphoreType.DMA((2 * n,))] + [pltpu.VMEM(s, a.dtype) for s, a in zip(shapes, arrays)]
        self.has_mid = False

    def _copies(self, ins, outs, scratch):
        send_sems, recv_sems = scratch[:2]
        me = _me()
        slot = {"chip": _chip_of(me), "dev": 4 * me[0] + 2 * me[1] + me[2], "core": me[2]}[self.slot_kind]
        nk = len(self.masks)

        def piece(a, dev):
            return ins[a] if self.pieces[a] is None else self.pieces[a][0](ins[a], dev)

        local = []
        if self.keep_own:
            local_sems, stages = scratch[2], scratch[3:]
            local = [_StagedCopy(piece(a, me), stages[a], outs[a].at[slot], local_sems.at[2 * a], local_sems.at[2 * a + 1])
                     for a in range(len(ins))]
        remote = []
        for a in range(len(ins)):
            for ki, mask in enumerate(self.masks):
                dev = _peer(mask)
                remote.append(_remote(piece(a, dev), outs[a].at[slot], send_sems.at[a * nk + ki],
                                      recv_sems.at[a * nk + ki], dev))
        return local, remote

    def start(self, ins, outs, scratch):
        local, remote = self._copies(ins, outs, scratch)
        for cp in remote + local:
            cp.start()

    def finish(self, ins, outs, scratch):
        local, remote = self._copies(ins, outs, scratch)
        for cp in remote + local:
            cp.wait()


SPLIT_ROWS = 16


class TwoLevelGather:
    def __init__(self, arrays):
        self.arrays = list(arrays)
        n = len(arrays)
        self.out_shape = [jax.ShapeDtypeStruct((4,) + a.shape, a.dtype) for a in arrays]
        self.scratch = ([pltpu.SemaphoreType.DMA((4 * n,)), pltpu.SemaphoreType.DMA((4 * n,)),
                         pltpu.SemaphoreType.DMA((3 * n,)), pltpu.SemaphoreType.DMA((3 * n,)),
                         pltpu.SemaphoreType.DMA((2 * n,))] + [pltpu.VMEM(a.shape, a.dtype) for a in arrays])
        self.has_mid = True

    def _copies(self, ins, outs, scratch):
        ici_send, ici_recv, fwd_send, fwd_recv, local_sems = scratch[:5]
        stages = scratch[5:]
        me = _me()
        sibling, in_x, in_y, diagonal = _peer(1), _peer(4), _peer(2), _peer(6)
        plan = []
        for a in range(len(ins)):
            half = ins[a].shape[0] // 2
            first = half // 2 if half % (2 * SPLIT_ROWS) == 0 else half
            mine = pl.ds(me[2] * half, half)
            local = _StagedCopy(ins[a], stages[a], outs[a].at[_chip_of(me)], local_sems.at[2 * a], local_sems.at[2 * a + 1])

            def ici(k, src, dst, dev):
                return _remote(src, dst, ici_send.at[4 * a + k], ici_recv.at[4 * a + k], dev)

            def d2d(k, chip):
                zone = outs[a].at[_chip_of(chip), mine]
                return _remote(zone, zone, fwd_send.at[3 * a + k], fwd_recv.at[3 * a + k], sibling)

            own_zone = outs[a].at[_chip_of(me), mine]
            from_x = outs[a].at[_chip_of(in_x), pl.ds(me[2] * half, first)]
            onward = [ici(2, from_x, from_x, in_y), None]
            if first < half:
                from_y = outs[a].at[_chip_of(in_y), pl.ds(me[2] * half + first, half - first)]
                onward[1] = ici(3, from_y, from_y, in_x)
            plan.append(dict(
                local=local,
                own=[ici(0, ins[a].at[mine], own_zone, in_x), ici(1, ins[a].at[mine], own_zone, in_y)],
                onward=onward, sibling=[d2d(0, in_x), d2d(1, in_y), d2d(2, diagonal)]))
        return plan

    def start(self, ins, outs, scratch):
        for p in self._copies(ins, outs, scratch):
            for cp in p["own"]:
                cp.start()
            p["local"].start()

    def mid(self, ins, outs, scratch):
        plan = self._copies(ins, outs, scratch)
        for p in plan:
            for k in range(2):
                p["own"][k].wait_recv()
                if p["onward"][k] is not None:
                    p["onward"][k].start()
                p["sibling"][k].start()
        for p in plan:
            for cp in p["onward"]:
                if cp is not None:
                    cp.wait_recv()
            p["sibling"][2].start()

    def finish(self, ins, outs, scratch):
        for p in self._copies(ins, outs, scratch):
            for cp in p["sibling"]:
                cp.wait_recv()
            for cp in p["own"] + p["sibling"] + [cp for cp in p["onward"] if cp is not None]:
                cp.wait_send()
            p["local"].wait()


class Both:
    def __init__(self, a, b):
        self.a, self.b = a, b
        self.arrays, self.out_shape = a.arrays + b.arrays, a.out_shape + b.out_shape
        self.scratch = a.scratch + b.scratch
        self.has_mid = False
        assert not (a.has_mid or b.has_mid)

    def _parts(self, ins, outs, sems):
        na, sa = len(self.a.arrays), len(self.a.scratch)
        return (ins[:na], outs[:na], sems[:sa]), (ins[na:], outs[na:], sems[sa:])

    def start(self, ins, outs, sems):
        pa, pb = self._parts(ins, outs, sems)
        self.a.start(*pa)
        self.b.start(*pb)

    def finish(self, ins, outs, sems):
        pa, pb = self._parts(ins, outs, sems)
        self.a.finish(*pa)
        self.b.finish(*pb)


_ANY = pl.BlockSpec(memory_space=pl.ANY)


def run_comm(name, comm):
    n = len(comm.arrays)

    def body(*refs):
        ins, outs, sems = refs[:n], refs[n:2 * n], refs[2 * n:]
        comm.start(ins, outs, sems)
        if comm.has_mid:
            comm.mid(ins, outs, sems)
        comm.finish(ins, outs, sems)

    return pl.pallas_call(body, name=name, out_shape=comm.out_shape, in_specs=[_ANY] * n, out_specs=[_ANY] * n,
                          scratch_shapes=comm.scratch,
                          compiler_params=pltpu.CompilerParams(vmem_limit_bytes=VMEM_LIMIT))(*comm.arrays)


_HBM = pl.BlockSpec(memory_space=pltpu.HBM)
_SEM = pl.BlockSpec(memory_space=pltpu.SEMAPHORE)
_SIDE_EFFECT = pltpu.SideEffectType.DATAFLOW_SIDE_EFFECTING


def _chip_copies(srcs, lands, send_sems, recv_sems, by_target):
    me = _me()
    copies = []
    for a, (src, land) in enumerate(zip(srcs, lands)):
        for ki, mask in enumerate(CHIP_MASKS):
            dev = _peer(mask)
            k = a * len(CHIP_MASKS) + ki
            piece = src.at[_chip_of(dev)] if by_target else src
            copies.append(_remote(piece, land.at[_chip_of(me)], send_sems.at[k], recv_sems.at[k], dev))
    return copies


def chip_exchange_start(name, arrays, by_target):
    n = len(arrays)
    nsem = n * len(CHIP_MASKS)
    piece_shapes = [a.shape[1:] if by_target else a.shape for a in arrays]

    def body(*refs):
        srcs, lands = refs[:n], refs[n:2 * n]
        send_sems, recv_sems = refs[2 * n:2 * n + 2]
        token = refs[4 * n + 2]
        stages, local_sems = refs[4 * n + 3:5 * n + 3], refs[5 * n + 3]
        me = _me()
        for cp in _chip_copies(srcs, lands, send_sems, recv_sems, by_target):
            cp.start()
        own = [_StagedCopy(srcs[a].at[_chip_of(me)] if by_target else srcs[a], stages[a], lands[a].at[_chip_of(me)],
                           local_sems.at[2 * a], local_sems.at[2 * a + 1]) for a in range(n)]
        for cp in own:
            cp.load.start()
        for cp in own:
            cp.load.wait()
            cp.store.start()
        for cp in own:
            cp.store.wait()
        token[...] = jnp.zeros_like(token)

    lands = [lax.empty((4,) + tuple(s), a.dtype) for s, a in zip(piece_shapes, arrays)]
    hbm = lambda a: pltpu.HBM(a.shape, a.dtype)
    res = pl.pallas_call(
        body, name=name,
        out_shape=(pltpu.SemaphoreType.DMA((nsem,)), pltpu.SemaphoreType.DMA((nsem,)), *[hbm(a) for a in arrays],
                   *[hbm(l) for l in lands], jax.ShapeDtypeStruct((8, LANES), F32)),
        in_specs=(_HBM,) * (2 * n), out_specs=(_SEM, _SEM) + (_HBM,) * (2 * n) + (pl.BlockSpec(memory_space=pltpu.VMEM),),
        input_output_aliases={i: i + 2 for i in range(2 * n)},
        scratch_shapes=[pltpu.VMEM(tuple(s), a.dtype) for s, a in zip(piece_shapes, arrays)]
        + [pltpu.SemaphoreType.DMA((2 * n,))],
        compiler_params=pltpu.CompilerParams(has_side_effects=_SIDE_EFFECT, vmem_limit_bytes=VMEM_LIMIT),
    )(*[pltpu.with_memory_space_constraint(a, pltpu.HBM) for a in arrays + lands])
    return res[:-1], res[-1]


def chip_exchange_wait(name, in_flight, by_target, after):
    send_sems, recv_sems, *thru = in_flight
    n = len(thru) // 2

    def body(*refs):
        srcs, lands, (send, recv) = refs[:n], refs[n:2 * n], refs[2 * n:2 * n + 2]
        for cp in _chip_copies(srcs, lands, send, recv, by_target):
            cp.wait_send()
            cp.wait_recv()

    return pl.pallas_call(
        body, name=name, out_shape=tuple(pltpu.HBM(t.shape, t.dtype) for t in thru),
        in_specs=(_HBM,) * (2 * n) + (_SEM, _SEM, pl.BlockSpec(memory_space=pl.ANY)), out_specs=(_HBM,) * (2 * n),
        input_output_aliases={i: i for i in range(2 * n)},
        compiler_params=pltpu.CompilerParams(has_side_effects=_SIDE_EFFECT),
    )(*thru, send_sems, recv_sems, after)[n:]


def _call_with_comm(body, comm, steps, args, *, name, out_shape, in_specs, out_specs, grid, scratch=()):
    ni, no, ns, nc = len(in_specs), len(out_specs), len(scratch), len(comm.arrays)

    def full_body(*refs):
        ins, cins = refs[:ni], refs[ni:ni + nc]
        outs, couts = refs[ni + nc:ni + nc + no], refs[ni + nc + no:ni + 2 * nc + no]
        scr, csems = refs[ni + 2 * nc + no:ni + 2 * nc + no + ns], refs[ni + 2 * nc + no + ns:]
        first, middle, last = steps()
        pl.when(first)(lambda: comm.start(cins, couts, csems))
        if comm.has_mid:
            pl.when(middle)(lambda: comm.mid(cins, couts, csems))
        body(*ins, *outs, *scr)
        pl.when(last)(lambda: comm.finish(cins, couts, csems))

    res = pl.pallas_call(
        full_body, name=name, out_shape=list(out_shape) + comm.out_shape, grid=grid,
        in_specs=list(in_specs) + [_ANY] * nc, out_specs=list(out_specs) + [_ANY] * nc,
        scratch_shapes=list(scratch) + comm.scratch,
        compiler_params=pltpu.CompilerParams(dimension_semantics=("arbitrary",) * len(grid),
                                             vmem_limit_bytes=VMEM_LIMIT))(*args, *comm.arrays)
    return res[:no], res[no:]


def _shard_pieces(chip):
    if chip < 3:
        return [(0, SHARD_W, 8 * chip)]
    behind_dt = DT_STORED_START + SSM_HEADS - 3 * SHARD_W
    return [(0, behind_dt, 24), (behind_dt, SHARD_W - behind_dt, behind_dt + 24 + DT_PAD)]


W_IN_COLS = 256


def pack_w_in(wt):
    def body(w_ref, o_ref, pad_ref):
        chip = _chip_index()
        pad_ref[...] = jnp.zeros_like(pad_ref)
        for cv in range(4):
            @pl.when(chip == cv)
            def _():
                for src, n, dst in _shard_pieces(cv):
                    pad_ref[dst:dst + n, :] = w_ref[src:src + n, :]
        o_ref[...] = pad_ref[...].astype(BF16)

    return _call(body, name="pack_w_in", grid=(D_MODEL // W_IN_COLS,),
                 in_specs=[pl.BlockSpec((SHARD_W, W_IN_COLS), lambda i: (0, i))],
                 out_specs=pl.BlockSpec((PACK_W, W_IN_COLS), lambda i: (0, i)),
                 out_shape=jax.ShapeDtypeStruct((PACK_W, D_MODEL), BF16),
                 scratch=[pltpu.VMEM((PACK_W, W_IN_COLS), F32)], sem=("parallel",))(wt)


def _tile_runs():
    runs, fix = [], []
    for t in range(N_ALIGNED // LANES):
        s = min(t // 19, 3)
        j = t - 19 * s
        p = _act_col(t * LANES)
        if runs and runs[-1][1] == s and runs[-1][0] + runs[-1][3] == p and runs[-1][2] + runs[-1][3] == j * LANES:
            runs[-1][3] += LANES
        else:
            runs.append([p, s, j * LANES, LANES])
        if j == 0 and s > 0:
            fix.append((p, s - 1))
    return runs, fix


def unpack_w_in(bg):
    runs, fix = _tile_runs()

    def body(b_ref, o_ref):
        for p, s, j, w in runs:
            o_ref[p:p + w, :] = b_ref[s, j:j + w, :]
        for p, s in fix:
            o_ref[p:p + LANES, :] = o_ref[p:p + LANES, :] + b_ref[s, SHARD_STRIDE:PACK_W, :]
        o_ref[N_ALIGNED:N_ACT, :] = jnp.zeros((N_ACT - N_ALIGNED, W_IN_COLS), BF16)

    return _call(body, name="unpack_w_in", grid=(D_MODEL // W_IN_COLS,),
                 in_specs=[pl.BlockSpec((4, PACK_W, W_IN_COLS), lambda i: (0, 0, i))],
                 out_specs=pl.BlockSpec((N_ACT, W_IN_COLS), lambda i: (0, i)),
                 out_shape=jax.ShapeDtypeStruct((N_ACT, D_MODEL), BF16), sem=("parallel",))(bg)


def _adamw(w, g, m, v):
    m = ADAM_B1 * m + (1.0 - ADAM_B1) * g
    v = ADAM_B2 * v + (1.0 - ADAM_B2) * jnp.square(g)
    m_hat = m / (1.0 - ADAM_B1 ** ADAM_STEP)
    v_hat = v / (1.0 - ADAM_B2 ** ADAM_STEP)
    delta = -ADAM_LR * (m_hat / (jnp.sqrt(v_hat) + ADAM_EPS) + ADAM_WD * w)
    return delta, m, v


def adamw_w_in(g_packed, wt, mt, vt):
    cols = LANES

    def body(g_ref, w_ref, m_ref, v_ref, go_ref, d_ref, mo_ref, vo_ref):
        chip = _chip_index()
        for cv in range(4):
            @pl.when(chip == cv)
            def _():
                for dst, n, src in _shard_pieces(cv):
                    go_ref[dst:dst + n, :] = g_ref[src:src + n, :]
        d_ref[...], mo_ref[...], vo_ref[...] = _adamw(w_ref[...], go_ref[...], m_ref[...], v_ref[...])

    spec = pl.BlockSpec((SHARD_W, cols), lambda i: (0, i))
    shp = jax.ShapeDtypeStruct((SHARD_W, D_MODEL), F32)
    return _call(body, name="adamw_w_in", grid=(D_MODEL // cols,),
                 in_specs=[pl.BlockSpec((PACK_W, cols), lambda i: (0, i)), spec, spec, spec],
                 out_specs=[spec] * 4, out_shape=[shp] * 4, sem=("parallel",))(g_packed, wt, mt, vt)


def adamw_rows(name, g, w, m, v):
    r, c = g.shape
    rows = min(r, BLOCK)

    def body(g_ref, w_ref, m_ref, v_ref, d_ref, mo_ref, vo_ref):
        d_ref[...], mo_ref[...], vo_ref[...] = _adamw(w_ref[...], g_ref[...], m_ref[...], v_ref[...])

    spec = pl.BlockSpec((rows, c), lambda i: (i, 0))
    shp = jax.ShapeDtypeStruct((r, c), F32)
    return _call(body, name=name, grid=(r // rows,), in_specs=[spec] * 4, out_specs=[spec] * 3, out_shape=[shp] * 3,
                 sem=("parallel",))(g, w, m, v)


def adamw_small(gs, ws, ms, vs):
    n = len(gs)

    def body(*refs):
        g, w, m, v = refs[:n], refs[n:2 * n], refs[2 * n:3 * n], refs[3 * n:4 * n]
        outs = refs[4 * n:]
        for i in range(n):
            d, mn, vn = _adamw(w[i][...], g[i][...], m[i][...], v[i][...])
            outs[3 * i][...] = d
            outs[3 * i + 1][...] = mn
            outs[3 * i + 2][...] = vn

    specs = [_full(a.shape) for a in gs]
    res = _call(body, name="adamw_small", in_specs=specs * 4,
                out_specs=[s for s in specs for _ in range(3)],
                out_shape=[jax.ShapeDtypeStruct(a.shape, F32) for a in gs for _ in range(3)])(*gs, *ws, *ms, *vs)
    return [tuple(res[3 * i:3 * i + 3]) for i in range(n)]


def sum_slots(name, r):
    s, rr, c = r.shape
    rows = min(rr, BLOCK)

    def body(r_ref, o_ref):
        acc = r_ref[0].astype(F32)
        for k in range(1, s):
            acc = acc + r_ref[k].astype(F32)
        o_ref[...] = acc

    return _call(body, name=name, grid=(rr // rows,), in_specs=[pl.BlockSpec((s, rows, c), lambda i: (0, i, 0))],
                 out_specs=pl.BlockSpec((rows, c), lambda i: (i, 0)), out_shape=jax.ShapeDtypeStruct((rr, c), F32),
                 sem=("parallel",))(r)


def sum_pair(partial, from_sibling):
    s, _, rr, cols = partial.shape
    rows = rr // 2

    def body(c_ref, p_ref, r_ref, o_ref):
        o_ref[...] = (p_ref[...].astype(F32) + r_ref[...].astype(F32)).astype(BF16)

    core = lax.axis_index("c").astype(jnp.int32).reshape(1)
    return pl.pallas_call(
        body, name="sum_pair", out_shape=jax.ShapeDtypeStruct((s, rr, cols), BF16),
        grid_spec=pltpu.PrefetchScalarGridSpec(
            num_scalar_prefetch=1, grid=(s, rr // rows),
            in_specs=[pl.BlockSpec((None, None, rows, cols), lambda k, i, c: (k, c[0], i, 0)),
                      pl.BlockSpec((None, rows, cols), lambda k, i, c: (k, i, 0))],
            out_specs=pl.BlockSpec((None, rows, cols), lambda k, i, c: (k, i, 0))),
        compiler_params=pltpu.CompilerParams(dimension_semantics=("parallel", "parallel"),
                                             vmem_limit_bytes=VMEM_LIMIT))(core, partial, from_sibling)


def _col_tile(n, k):
    if n % 896 == 0 and k <= 1024:
        return 896
    return min(n, 512)


def project(name, x, wt, after):
    m, k = x.shape
    n = wt.shape[0]
    tn = D_MODEL

    def body(x_ref, w_ref, after_ref, o_ref):
        o_ref[...] = lax.dot_general(x_ref[...], w_ref[...], _NT, preferred_element_type=F32)

    return _call(body, name=name, grid=(n // tn,),
                 in_specs=[_full((m, k)), pl.BlockSpec((tn, k), lambda j: (j, 0)), _full(after.shape)],
                 out_specs=pl.BlockSpec((m, tn), lambda j: (0, j)), out_shape=jax.ShapeDtypeStruct((m, n), F32),
                 sem=("parallel",))(x, wt, after)


def _piece_tiles(pieces, width):
    spans, start = [], 0
    for p in pieces:
        spans.append((start, p.shape[1] // width))
        start += p.shape[1] // width
    return spans, start


def _piece_spec(block, span, rows_of, tile_of):
    first, count = span

    def index(*pos):
        t = tile_of(*pos) - first
        mine = (t >= 0) & (t < count)
        return jnp.where(mine, rows_of(*pos), 0), jnp.clip(t, 0, count - 1)

    return pl.BlockSpec(block, index)


def project_back(name, pieces, wt, after):
    m = pieces[0].shape[0]
    k = wt.shape[1]
    tm = m // 2
    spans, steps = _piece_tiles(pieces, D_MODEL)

    def body(*refs):
        w_ref, o_ref = refs[len(pieces)], refs[len(pieces) + 2]
        j = pl.program_id(1)

        @pl.when(j == 0)
        def _():
            o_ref[...] = jnp.zeros_like(o_ref)

        for dy_ref, (first, count) in zip(refs, spans):
            @pl.when((j >= first) & (j < first + count))
            def _():
                o_ref[...] += jnp.dot(dy_ref[...], w_ref[...], preferred_element_type=F32)

    return _call(body, name=name, grid=(m // tm, steps),
                 in_specs=[_piece_spec((tm, D_MODEL), s, lambda i, j: i, lambda i, j: j) for s in spans]
                 + [pl.BlockSpec((D_MODEL, k), lambda i, j: (j, 0)), _full(after.shape)],
                 out_specs=pl.BlockSpec((tm, k), lambda i, j: (i, 0)), out_shape=jax.ShapeDtypeStruct((m, k), F32),
                 sem=("parallel", "arbitrary"))(*pieces, wt, after)


GRAD_TILE = 512


def _slab_runs():
    runs = [[] for _ in range(N_ACT // GRAD_TILE)]
    for s in range(4):
        for j in range(PACK_W // LANES):
            tile, r = divmod(_act_col((19 * s + j) * LANES), GRAD_TILE)
            last = runs[tile][-1] if runs[tile] else None
            if last and last[2] == s and last[0] + last[1] == r and last[3] + last[1] == j * LANES:
                last[1] += LANES
            else:
                runs[tile].append([r, LANES, s, j * LANES])
    return runs


def weight_grad_t(name, pieces, x):
    m, k = x.shape
    spans, steps = _piece_tiles(pieces, GRAD_TILE)
    runs = _slab_runs()
    most = max(len(r) for r in runs)
    half_rows = PACK_W // 2
    lead = spans[-1][0]

    def tile_at(step):
        return (step + lead) % steps

    landed = {}
    for step in range(steps):
        for _, _, s, _ in runs[tile_at(step)]:
            landed[s] = step + 2

    def body(*refs):
        x_ref, o_ref, land_ref, tile_ref, sems, send_sems, recv_sems = refs[len(pieces):]
        i = pl.program_id(0)
        other = 1 - lax.axis_index("c")

        def copies(step):
            return [pltpu.make_async_copy(tile_ref.at[step % 2, r:r + n], o_ref.at[s, d:d + n],
                                          sems.at[(step % 2) * most + c])
                    for c, (r, n, s, d) in enumerate(runs[tile_at(step)])]

        def to_sibling(s):
            return _remote(o_ref.at[s, pl.ds(other * half_rows, half_rows)], land_ref.at[s], send_sems.at[s],
                           recv_sems.at[s], _peer(1))

        for step in range(2, steps):
            @pl.when(i == step)
            def _():
                for cp in copies(step - 2):
                    cp.wait()

        tile = tile_at(i)
        for dy_ref, (first, count) in zip(refs, spans):
            @pl.when((tile >= first) & (tile < first + count))
            def _():
                tile_ref[i % 2] = lax.dot_general(dy_ref[...], x_ref[...], (((0,), (0,)), ((), ())),
                                                  preferred_element_type=F32).astype(BF16)

        for step in range(steps):
            @pl.when(i == step)
            def _():
                for cp in copies(step):
                    cp.start()
                if step < steps - 1:
                    for s in range(4):
                        if landed[s] == step:
                            to_sibling(s).start()
                else:
                    for cp in copies(step - 1) + copies(step):
                        cp.wait()
                    for s in range(4):
                        if landed[s] >= steps - 1:
                            to_sibling(s).start()
                    for s in range(4):
                        to_sibling(s).wait()

    return _call(body, name=name, grid=(steps,),
                 in_specs=[_piece_spec((m, GRAD_TILE), s, lambda i: 0, tile_at) for s in spans]
                 + [pl.BlockSpec(memory_space=pltpu.VMEM)],
                 out_specs=[_ANY, _ANY],
                 out_shape=[jax.ShapeDtypeStruct((4, PACK_W, k), BF16), jax.ShapeDtypeStruct((4, half_rows, k), BF16)],
                 scratch=[pltpu.VMEM((2, GRAD_TILE, k), BF16), pltpu.SemaphoreType.DMA((2 * most,)),
                          pltpu.SemaphoreType.DMA((4,)), pltpu.SemaphoreType.DMA((4,))],
                 sem=("arbitrary",))(*pieces, x)


def mm_tn(name, x, dy):
    m, k = x.shape
    n = dy.shape[1]
    tn = _col_tile(n, k)

    def body(x_ref, dy_ref, o_ref):
        o_ref[...] = lax.dot_general(x_ref[...], dy_ref[...], (((0,), (0,)), ((), ())),
                                     preferred_element_type=F32).astype(BF16)

    return _call(body, name=name, grid=(n // tn,),
                 in_specs=[_full((m, k)), pl.BlockSpec((m, tn), lambda i: (0, i))],
                 out_specs=pl.BlockSpec((k, tn), lambda i: (0, i)), out_shape=jax.ShapeDtypeStruct((k, n), BF16),
                 sem=("parallel",))(x, dy)


def _row_spec(width, col_block=0):
    return pl.BlockSpec((BLOCK, width), lambda i: (i, col_block))


def _x_spec():
    return pl.BlockSpec((None, BLOCK, D_MODEL), lambda i: (0, jnp.maximum(i - 1, 0), 0))


def prep(x, meta, g_pre):
    nb = x.shape[1] // BLOCK + 1

    def body(x_ref, meta_ref, g_ref, h_ref, u_ref):
        i = pl.program_id(0)

        @pl.when(i == 0)
        def _():
            h_ref[0:PAD_ROWS, :] = jnp.zeros((PAD_ROWS, D_MODEL), F32)
            h_ref[PAD_ROWS:BLOCK, :] = meta_ref[...]

        @pl.when(i > 0)
        def _():
            h_ref[...] = x_ref[...]

        u_ref[...] = _rms(h_ref[...], g_ref[...]).astype(BF16)

    return _call(body, name="prep", grid=(nb,), in_specs=[_x_spec(), _full((N_META, D_MODEL)), _full((1, D_MODEL))],
                 out_specs=[_row_spec(D_MODEL), _row_spec(D_MODEL)],
                 out_shape=[jax.ShapeDtypeStruct((nb * BLOCK, D_MODEL), F32),
                            jax.ShapeDtypeStruct((nb * BLOCK, D_MODEL), BF16)], sem=("parallel",))(x, meta, g_pre)


def prep_bwd(h, du, dres, g_pre):
    seq = h.shape[0] - BLOCK
    rows = min(seq, 4 * BLOCK)

    def body(h0_ref, du0_ref, h_ref, du_ref, dres_ref, g_ref, gx_ref, gm_ref, gg_ref):
        i = pl.program_id(0)
        _, vjp = jax.vjp(_rms, h_ref[...], g_ref[...])
        dh, dg = vjp(du_ref[...])
        gx_ref[...] = dh + dres_ref[...]

        @pl.when(i == 0)
        def _():
            _, vjp0 = jax.vjp(_rms, h0_ref[...], g_ref[...])
            dh0, dg0 = vjp0(du0_ref[...])
            gm_ref[...] = dh0[PAD_ROWS:BLOCK, :]
            gg_ref[...] = dg0 + dg

        @pl.when(i > 0)
        def _():
            gg_ref[...] += dg

    seq_rows = pl.BlockSpec((pl.Element(rows), pl.Element(D_MODEL)), lambda i: (pl.multiple_of(BLOCK + rows * i, BLOCK), 0))
    first = pl.BlockSpec((BLOCK, D_MODEL), lambda i: (0, 0))
    return _call(body, name="prep_bwd", grid=(seq // rows,),
                 in_specs=[first, first, seq_rows, seq_rows, seq_rows, _full((1, D_MODEL))],
                 out_specs=[pl.BlockSpec((None, rows, D_MODEL), lambda i: (0, i, 0)), _full((N_META, D_MODEL)),
                            _full((1, D_MODEL))],
                 out_shape=[jax.ShapeDtypeStruct((1, seq, D_MODEL), F32),
                            jax.ShapeDtypeStruct((N_META, D_MODEL), F32), jax.ShapeDtypeStruct((1, D_MODEL), F32)],
                 sem=("arbitrary",))(h, du, h, du, dres, g_pre)


GROUP_W = SSM_INNER // SSM_GROUPS


def _gated_norm(y, z, g):
    t = y * _silu(z)
    return t * lax.rsqrt(jnp.mean(t * t, axis=-1, keepdims=True) + NORM_EPS) * g


def _gated_norm_groups(y, z, g):
    groups = [slice(k * GROUP_W, (k + 1) * GROUP_W) for k in range(SSM_GROUPS)]
    return jnp.concatenate([_gated_norm(y[:, s], z[:, s], g[:, s]) for s in groups], axis=1)


def _merge(ga, gs, ya, ys):
    return _sigmoid(ga) * ya + _sigmoid(gs) * ys


GATE_ATT_BLOCK = SEG["gate_att"][2] // D_MODEL
GATE_SSM_BLOCK = SEG["gate_ssm"][2] // D_MODEL


def _row_loss(out, g_post, x, target):
    diff = x + _rms(out, g_post) - target
    return 0.5 * jnp.sum(diff * diff) / D_MODEL


def tail(y_ssd, proj, a_att, x, target, woa, wos, wo, g_norm, g_post):
    nb = y_ssd.shape[0] // BLOCK
    rows = nb * BLOCK

    def body(y_ref, z_ref, ga_ref, gs_ref, a_ref, x_ref, t_ref, woa_ref, wos_ref, wo_ref, gn_ref, gp_ref,
             yn_ref, mg_ref, dout_ref, dya_ref, dys_ref, da_ref, dy_ref, dz_ref, dga_ref, dgs_ref, dres_ref,
             loss_ref, dgp_ref, dgn_ref):
        i = pl.program_id(0)
        yn, norm_vjp = jax.vjp(_gated_norm_groups, y_ref[...], z_ref[...], gn_ref[...])
        yn16 = yn.astype(BF16)
        y_ssm = jnp.dot(yn16, wos_ref[...], preferred_element_type=F32)
        y_att = jnp.dot(a_ref[...], woa_ref[...], preferred_element_type=F32)
        merged, merge_vjp = jax.vjp(_merge, ga_ref[...], gs_ref[...], y_att, y_ssm)
        merged16 = merged.astype(BF16)
        out = jnp.dot(merged16, wo_ref[...], preferred_element_type=F32)
        loss, loss_vjp = jax.vjp(_row_loss, out, gp_ref[...], x_ref[...], t_ref[...])
        counted = jnp.where(i > 0, 1.0, 0.0)
        dout, dgp, dres, _ = loss_vjp(counted)
        dout16 = dout.astype(BF16)
        dmerged = lax.dot_general(dout16, wo_ref[...], _NT, preferred_element_type=F32)
        dga, dgs, dya, dys = merge_vjp(dmerged)
        dya16, dys16 = dya.astype(BF16), dys.astype(BF16)
        dyn = lax.dot_general(dys16, wos_ref[...], _NT, preferred_element_type=F32)
        dy, dz, dgn = norm_vjp(dyn)

        yn_ref[...] = yn16
        mg_ref[...] = merged16
        dout_ref[...] = dout16
        dya_ref[...] = dya16
        dys_ref[...] = dys16
        da_ref[...] = lax.dot_general(dya16, woa_ref[...], _NT, preferred_element_type=F32)
        dy_ref[...] = dy
        dz_ref[...] = dz.astype(BF16)
        dga_ref[...] = dga.astype(BF16)
        dgs_ref[...] = dgs.astype(BF16)
        dres_ref[...] = dres

        @pl.when(i == 0)
        def _():
            loss_ref[...] = jnp.zeros_like(loss_ref)
            dgp_ref[...] = jnp.zeros_like(dgp_ref)
            dgn_ref[...] = jnp.zeros_like(dgn_ref)

        loss_ref[...] += loss * counted
        dgp_ref[...] += dgp
        dgn_ref[...] += dgn

    wide, narrow = _row_spec(SSM_INNER), _row_spec(D_MODEL)
    resident = pl.BlockSpec(memory_space=pltpu.VMEM)
    bf = lambda w: jax.ShapeDtypeStruct((rows, w), BF16)
    f32 = lambda w: jax.ShapeDtypeStruct((rows, w), F32)
    return _call(body, name="tail", grid=(nb,),
                 in_specs=[wide, wide, _row_spec(D_MODEL, GATE_ATT_BLOCK), _row_spec(D_MODEL, GATE_SSM_BLOCK), narrow,
                           _x_spec(), _x_spec(), resident, resident, resident, _full((1, SSM_INNER)),
                           _full((1, D_MODEL))],
                 out_specs=[wide, narrow, narrow, narrow, narrow, narrow, wide, wide, narrow, narrow, narrow,
                            _full((8, LANES)), _full((1, D_MODEL)), _full((1, SSM_INNER))],
                 out_shape=[bf(SSM_INNER), bf(D_MODEL), bf(D_MODEL), bf(D_MODEL), bf(D_MODEL), f32(D_MODEL),
                            f32(SSM_INNER), bf(SSM_INNER), bf(D_MODEL), bf(D_MODEL), f32(D_MODEL),
                            jax.ShapeDtypeStruct((8, LANES), F32), jax.ShapeDtypeStruct((1, D_MODEL), F32),
                            jax.ShapeDtypeStruct((1, SSM_INNER), F32)],
                 sem=("arbitrary",))(y_ssd, proj, proj, proj, a_att, x, target, woa, wos, wo, g_norm, g_post)


_NT = (((1,), (1,)), ((), ()))
ALIBI_SLOPES = tuple(2.0 ** (-8.0 * (h + 1) / ATT_Q_HEADS) for h in range(ATT_Q_HEADS))
KV_WIDTH = ATT_KV_HEADS * HEAD_DIM
Q_BLOCK = SEG["q"][2] // D_MODEL
Z_ATT_BLOCK = SEG["z_att"][2] // D_MODEL
K_BLOCK = SEG["k"][2] // KV_WIDTH
V_BLOCK = SEG["v"][2] // KV_WIDTH
META_ROW_BLOCK = PAD_ROWS // N_META


@jax.custom_vjp
def _swap_halves(x):
    return pltpu.roll(x, HEAD_DIM, 1)


_swap_halves.defvjp(lambda x: (pltpu.roll(x, HEAD_DIM, 1), None), lambda _, g: (pltpu.roll(g, HEAD_DIM, 1),))


def _both_halves(t, half):
    first = lax.broadcasted_iota(jnp.int32, t.shape, 1) < HEAD_DIM
    sw = _swap_halves(t)
    return jnp.where(first, t, sw) if half == 0 else jnp.where(first, sw, t)


def _attn_rows(q, z, kp, kc, vp, vc, km, vm, sinks, n):
    rows = ATT_GROUP * BLOCK
    i = lax.broadcasted_iota(jnp.int32, (rows, BLOCK), 0) & (BLOCK - 1)
    j = lax.broadcasted_iota(jnp.int32, (rows, BLOCK), 1)
    rel_c = (i - j).astype(F32)
    rel_p = rel_c + float(BLOCK)
    nv = jnp.zeros((rows, BLOCK), jnp.int32) + n
    ok_c = (i >= j) & (nv >= 1)
    ok_p = (j > i) & (nv >= 2)
    im = lax.broadcasted_iota(jnp.int32, (rows, N_META), 0) & (BLOCK - 1)
    jm = lax.broadcasted_iota(jnp.int32, (rows, N_META), 1)
    ok_m = ((jnp.zeros((rows, N_META), jnp.int32) + n) >= 1) | (im >= PAD_ROWS + jm)
    first = lax.broadcasted_iota(jnp.int32, (BLOCK, LANES), 1) < HEAD_DIM
    neg = -jnp.inf
    outs = []
    for kv in range(ATT_KV_HEADS):
        tile, half = divmod(kv, 2)
        lanes = slice(tile * LANES, (tile + 1) * LANES)
        kc2, kp2, km2 = (_both_halves(t[:, lanes], half).astype(BF16) for t in (kc, kp, km))
        vc2, vp2, vm2 = (_both_halves(t[:, lanes], half).astype(BF16) for t in (vc, vp, vm))
        qs, slope, sk = [], [], []
        for pair in range(ATT_GROUP // 2):
            c0 = (kv * ATT_GROUP + 2 * pair) * HEAD_DIM
            qp = q[:, c0:c0 + LANES] * HEAD_DIM ** -0.5
            qs += [jnp.where(first, qp, 0.0), jnp.where(first, 0.0, qp)]
        for g in range(ATT_GROUP):
            slope.append(jnp.full((BLOCK, 1), ALIBI_SLOPES[kv * ATT_GROUP + g], F32))
            sk.append(jnp.broadcast_to(sinks[kv * ATT_GROUP + g], (BLOCK, 1)))
        qs = jnp.concatenate(qs, axis=0).astype(BF16)
        slope = jnp.concatenate(slope, axis=0)
        sk = jnp.concatenate(sk, axis=0)
        sc = jnp.where(ok_c, lax.dot_general(qs, kc2, _NT, preferred_element_type=F32) - slope * rel_c, neg)
        sp = jnp.where(ok_p, lax.dot_general(qs, kp2, _NT, preferred_element_type=F32) - slope * rel_p, neg)
        sm = jnp.where(ok_m, lax.dot_general(qs, km2, _NT, preferred_element_type=F32), neg)
        mx = jnp.maximum(jnp.maximum(jnp.max(sc, axis=1, keepdims=True), jnp.max(sp, axis=1, keepdims=True)),
                         jnp.maximum(jnp.max(sm, axis=1, keepdims=True), sk))
        mx = lax.stop_gradient(mx)
        ec, ep, em, es = jnp.exp(sc - mx), jnp.exp(sp - mx), jnp.exp(sm - mx), jnp.exp(sk - mx)
        den = (es + jnp.sum(ec, axis=1, keepdims=True) + jnp.sum(ep, axis=1, keepdims=True)
               + jnp.sum(em, axis=1, keepdims=True))
        inv = 1.0 / den
        o = (jnp.dot((ec * inv).astype(BF16), vc2, preferred_element_type=F32)
             + jnp.dot((ep * inv).astype(BF16), vp2, preferred_element_type=F32)
             + jnp.dot((em * inv).astype(BF16), vm2, preferred_element_type=F32))
        for pair in range(ATT_GROUP // 2):
            r0 = 2 * pair * BLOCK
            outs.append(jnp.where(first, o[r0:r0 + BLOCK], o[r0 + BLOCK:r0 + 2 * BLOCK]))
    return jnp.concatenate(outs, axis=1) * _silu(z)


def _attn_specs(nb, steps_clamped):
    def blk(t):
        return jnp.minimum(t, nb - 1) if steps_clamped else t

    wide = lambda col: pl.BlockSpec((BLOCK, D_MODEL), lambda t: (blk(t), col))
    cur = lambda col: pl.BlockSpec((BLOCK, KV_WIDTH), lambda t: (blk(t), col))
    prev = lambda col: pl.BlockSpec((BLOCK, KV_WIDTH), lambda t: (jnp.maximum(blk(t) - 1, 0), col))
    meta = lambda col: pl.BlockSpec((N_META, KV_WIDTH), lambda t: (META_ROW_BLOCK, col))
    sinks = pl.BlockSpec((ATT_Q_HEADS, 1, 1), lambda t: (0, 0, 0))
    return [wide(Q_BLOCK), wide(Z_ATT_BLOCK), prev(K_BLOCK), cur(K_BLOCK), prev(V_BLOCK), cur(V_BLOCK),
            meta(K_BLOCK), meta(V_BLOCK), sinks]


def attn_fwd(proj, sinks):
    nb = proj.shape[0] // BLOCK

    def body(q_ref, z_ref, kp_ref, kc_ref, vp_ref, vc_ref, km_ref, vm_ref, sk_ref, o_ref):
        o_ref[...] = _attn_rows(q_ref[...], z_ref[...], kp_ref[...], kc_ref[...], vp_ref[...], vc_ref[...],
                                km_ref[...], vm_ref[...], tuple(sk_ref[h] for h in range(ATT_Q_HEADS)),
                                pl.program_id(0)).astype(BF16)

    return _call(body, name="attn_fwd", grid=(nb,), in_specs=_attn_specs(nb, False), out_specs=_row_spec(D_MODEL),
                 out_shape=jax.ShapeDtypeStruct((nb * BLOCK, D_MODEL), BF16), sem=("parallel",))(*([proj] * 8), sinks)


def attn_bwd(da, proj, sinks):
    nb = proj.shape[0] // BLOCK
    last = nb - 1
    wide = pl.BlockSpec((BLOCK, D_MODEL), lambda t: (jnp.minimum(t, last), 0))
    done = pl.BlockSpec((BLOCK, KV_WIDTH), lambda t: (jnp.maximum(t - 1, 0), 0))
    meta = _full((N_META, KV_WIDTH))
    par = _full((ATT_Q_HEADS, 1, 1))

    def body(da_ref, q_ref, z_ref, kp_ref, kc_ref, vp_ref, vc_ref, km_ref, vm_ref, sk_ref,
             dq_ref, dz_ref, dk_ref, dv_ref, dkm_ref, dvm_ref, dsk_ref, ck_ref, cv_ref):
        t = pl.program_id(0)

        @pl.when(t == 0)
        def _():
            ck_ref[...] = jnp.zeros_like(ck_ref)
            cv_ref[...] = jnp.zeros_like(cv_ref)
            dkm_ref[...] = jnp.zeros_like(dkm_ref)
            dvm_ref[...] = jnp.zeros_like(dvm_ref)
            dsk_ref[...] = jnp.zeros_like(dsk_ref)

        @pl.when(t < nb)
        def _():
            def f(q, z, kp, kc, vp, vc, km, vm, sk):
                return _attn_rows(q, z, kp, kc, vp, vc, km, vm, sk, t)

            _, vjp = jax.vjp(f, q_ref[...], z_ref[...], kp_ref[...], kc_ref[...], vp_ref[...], vc_ref[...],
                             km_ref[...], vm_ref[...], tuple(sk_ref[h] for h in range(ATT_Q_HEADS)))
            dq, dz, dkp, dkc, dvp, dvc, dkm, dvm, dsk = vjp(da_ref[...])
            dq_ref[...] = dq.astype(BF16)
            dz_ref[...] = dz.astype(BF16)
            for h in range(ATT_Q_HEADS):
                dsk_ref[h] += dsk[h]
            dk_ref[...] = ck_ref[...] + dkp
            dv_ref[...] = cv_ref[...] + dvp
            ck_ref[...] = dkc
            cv_ref[...] = dvc
            dkm_ref[...] += dkm
            dvm_ref[...] += dvm

        @pl.when(t == nb)
        def _():
            dk_ref[...] = ck_ref[...]
            dv_ref[...] = cv_ref[...]

    rows = nb * BLOCK
    return _call(body, name="attn_bwd", grid=(nb + 1,), in_specs=[wide] + _attn_specs(nb, True),
                 out_specs=[wide, wide, done, done, meta, meta, par],
                 out_shape=[jax.ShapeDtypeStruct((rows, D_MODEL), BF16), jax.ShapeDtypeStruct((rows, D_MODEL), BF16),
                            jax.ShapeDtypeStruct((rows, KV_WIDTH), F32), jax.ShapeDtypeStruct((rows, KV_WIDTH), F32),
                            jax.ShapeDtypeStruct((N_META, KV_WIDTH), F32), jax.ShapeDtypeStruct((N_META, KV_WIDTH), F32),
                            jax.ShapeDtypeStruct(sinks.shape, F32)],
                 scratch=[pltpu.VMEM((BLOCK, KV_WIDTH), F32), pltpu.VMEM((BLOCK, KV_WIDTH), F32)],
                 sem=("arbitrary",))(da, *([proj] * 8), sinks)


XBC_BLOCK0 = SEG["xbc"][2] // D_MODEL
CONV_COL_BLOCKS = CONV_DIM // D_MODEL
DT_TILE = SEG["dt"][2] // LANES


HALO = 8


def _conv_rows(length):
    return 544 if length % 544 == 0 else BLOCK


def _shift_rows(cur, before, j):
    if j == 0:
        return cur
    n = cur.shape[0]
    row = lax.broadcasted_iota(jnp.int32, cur.shape, 0)
    head = pltpu.roll(before, j, 0)
    if n > HALO:
        head = jnp.concatenate([head, jnp.zeros((n - HALO, cur.shape[1]), cur.dtype)], axis=0)
    return jnp.where(row >= j, pltpu.roll(cur, j, 0), head)


def _conv_pre(cur, before, w_ref, b_ref):
    pre = b_ref[...] + w_ref[CONV_WIDTH - 1:CONV_WIDTH, :] * cur
    for k in range(CONV_WIDTH - 1):
        pre = pre + w_ref[k:k + 1, :] * _shift_rows(cur, before, CONV_WIDTH - 1 - k)
    return pre


def _conv_specs(steps, rows, col0=0):
    first = XBC_BLOCK0 + col0
    halos = rows // HALO
    cur = pl.BlockSpec((rows, D_MODEL), lambda j, i: (i, first + j))
    before = pl.BlockSpec((HALO, D_MODEL), lambda j, i: (jnp.maximum(i * halos - 1, 0), first + j))
    after = pl.BlockSpec((HALO, D_MODEL), lambda j, i: (jnp.minimum(i + 1, steps - 1) * halos, first + j))
    return cur, before, after


def _valid_rows(i, rows):
    row = lax.broadcasted_iota(jnp.int32, (rows, D_MODEL), 0)
    return jnp.maximum((row >= PAD_ROWS).astype(F32), jnp.where(i > 0, 1.0, 0.0))


def conv_fwd(proj, conv_w, conv_b):
    rows = _conv_rows(proj.shape[0])
    steps = proj.shape[0] // rows
    cur, before, _ = _conv_specs(steps, rows)

    def body(c_ref, p_ref, w_ref, b_ref, o_ref):
        i = pl.program_id(1)
        pre = _conv_pre(c_ref[...], p_ref[...] * jnp.where(i > 0, 1.0, 0.0), w_ref, b_ref)
        o_ref[...] = _silu(pre) * _valid_rows(i, rows)

    return _call(body, name="conv_fwd", grid=(CONV_COL_BLOCKS, steps),
                 in_specs=[cur, before, pl.BlockSpec((CONV_WIDTH, D_MODEL), lambda j, i: (0, j)),
                           pl.BlockSpec((1, D_MODEL), lambda j, i: (0, j))],
                 out_specs=pl.BlockSpec((rows, D_MODEL), lambda j, i: (i, j)),
                 out_shape=jax.ShapeDtypeStruct((proj.shape[0], CONV_DIM), F32),
                 sem=("parallel", "parallel"))(proj, proj, conv_w, conv_b)


def conv_bwd(name, dparts, col0, proj, conv_w, conv_b):
    rows = _conv_rows(proj.shape[0])
    steps = proj.shape[0] // rows
    last = steps - 1
    ncol = sum(d.shape[1] for d in dparts) // D_MODEL
    np_ = len(dparts)
    cur, before, after = _conv_specs(steps, rows, col0)
    dcur = [pl.BlockSpec((rows, d.shape[1] // ncol), lambda j, i: (i, j)) for d in dparts]
    dafter = [pl.BlockSpec((HALO, d.shape[1] // ncol), lambda j, i: (jnp.minimum(i + 1, last) * (rows // HALO), j))
              for d in dparts]
    out_cur = pl.BlockSpec((rows, D_MODEL), lambda j, i: (i, j))
    wspec = pl.BlockSpec((CONV_WIDTH, D_MODEL), lambda j, i: (0, col0 + j))
    bspec = pl.BlockSpec((1, D_MODEL), lambda j, i: (0, col0 + j))
    wout = pl.BlockSpec((CONV_WIDTH, D_MODEL), lambda j, i: (0, j))
    bout = pl.BlockSpec((1, D_MODEL), lambda j, i: (0, j))

    def body(*refs):
        dc_refs, da_refs = refs[:np_], refs[np_:2 * np_]
        c_ref, p_ref, a_ref, w_ref, b_ref, du_ref, dw_ref, db_ref = refs[2 * np_:]
        i = pl.program_id(1)
        row = lax.broadcasted_iota(jnp.int32, (rows, D_MODEL), 0)
        curv = c_ref[...]
        beforev = p_ref[...] * jnp.where(i > 0, 1.0, 0.0)
        side_by_side = lambda rs: rs[0][...] if np_ == 1 else jnp.concatenate([r[...] for r in rs], axis=1)

        def dpre_of(pre, d):
            s = _sigmoid(pre)
            return d * (s * (1.0 + pre * (1.0 - s)))

        dp_c = dpre_of(_conv_pre(curv, beforev, w_ref, b_ref), side_by_side(dc_refs) * _valid_rows(i, rows))
        dp_a = dpre_of(_conv_pre(a_ref[...], curv[rows - HALO:], w_ref, b_ref),
                       side_by_side(da_refs) * jnp.where(i < last, 1.0, 0.0))
        du = w_ref[CONV_WIDTH - 1:CONV_WIDTH, :] * dp_c
        for j in range(1, CONV_WIDTH):
            tail = jnp.concatenate([jnp.zeros((rows - HALO, D_MODEL), F32), pltpu.roll(dp_a, HALO - j, 0)], axis=0)
            up = jnp.where(row < rows - j, pltpu.roll(dp_c, rows - j, 0), tail)
            du = du + w_ref[CONV_WIDTH - 1 - j:CONV_WIDTH - j, :] * up
        du_ref[...] = du.astype(BF16)

        @pl.when(i == 0)
        def _():
            dw_ref[...] = jnp.zeros_like(dw_ref)
            db_ref[...] = jnp.zeros_like(db_ref)

        for k in range(CONV_WIDTH):
            dw_ref[k:k + 1, :] += jnp.sum(dp_c * _shift_rows(curv, beforev, CONV_WIDTH - 1 - k), axis=0, keepdims=True)
        db_ref[...] += jnp.sum(dp_c, axis=0, keepdims=True)

    width = ncol * D_MODEL
    return _call(body, name=name, grid=(ncol, steps),
                 in_specs=dcur + dafter + [cur, before, after, wspec, bspec], out_specs=[out_cur, wout, bout],
                 out_shape=[jax.ShapeDtypeStruct((proj.shape[0], width), BF16),
                            jax.ShapeDtypeStruct((CONV_WIDTH, width), F32), jax.ShapeDtypeStruct((1, width), F32)],
                 sem=("parallel", "arbitrary"))(*dparts, *dparts, proj, proj, proj, conv_w, conv_b)


def _head_expand():
    e = np.zeros((LANES, SSM_INNER), np.float32)
    for h in range(SSM_HEADS):
        e[h, h * HEAD_DIM:(h + 1) * HEAD_DIM] = 1.0
    return jnp.asarray(e, dtype=BF16)


def _softplus(x):
    return jnp.maximum(x, 0.0) + jnp.log(1.0 + jnp.exp(-jnp.abs(x)))


def _bf16_parts(x):
    hi = x.astype(BF16)
    rest = x - hi.astype(F32)
    mid = rest.astype(BF16)
    return hi, mid, (rest - mid.astype(F32)).astype(BF16)


@jax.custom_vjp
def _times_01(x, m):
    return sum(jnp.dot(p, m, preferred_element_type=F32) for p in _bf16_parts(x))


def _times_01_bwd(m, g):
    return sum(lax.dot_general(p, m, _NT, preferred_element_type=F32) for p in _bf16_parts(g)), jnp.zeros_like(m)


_times_01.defvjp(lambda x, m: (_times_01(x, m), m), _times_01_bwd)


def _causal_ones():
    l = lax.broadcasted_iota(jnp.int32, (BLOCK, BLOCK), 0)
    s = lax.broadcasted_iota(jnp.int32, (BLOCK, BLOCK), 1)
    return (l >= s).astype(BF16)


@jax.custom_vjp
def _cumsum_rows(a):
    return sum(jnp.dot(_causal_ones(), p, preferred_element_type=F32) for p in _bf16_parts(a))


def _cumsum_rows_bwd(_, g):
    tn = (((0,), (0,)), ((), ()))
    return (sum(lax.dot_general(_causal_ones(), p, tn, preferred_element_type=F32) for p in _bf16_parts(g)),)


_cumsum_rows.defvjp(lambda a: (_cumsum_rows(a), None), _cumsum_rows_bwd)


def _ssd_heads(dt_tile, bias, alog, dsk, expand):
    dt = _softplus(dt_tile + bias)
    a = dt * (-jnp.exp(alog))
    one_row = lambda v: jnp.broadcast_to(v, (HALO, LANES))
    return _times_01(jnp.concatenate([dt, _cumsum_rows(a), one_row(jnp.sum(a, axis=0, keepdims=True)), one_row(dsk)],
                                     axis=0), expand)


HEADS_ROWS = 2 * BLOCK + 2 * HALO


def _ssd_group(xs, per_lane, bg, cg, state):
    l = lax.broadcasted_iota(jnp.int32, (BLOCK, BLOCK), 0)
    s = lax.broadcasted_iota(jnp.int32, (BLOCK, BLOCK), 1)
    causal = l >= s
    first_head = s < HEAD_DIM
    dtx, cs = per_lane[0:BLOCK], per_lane[BLOCK:2 * BLOCK]
    tot, dsk = per_lane[2 * BLOCK:2 * BLOCK + 1], per_lane[2 * BLOCK + HALO:2 * BLOCK + HALO + 1]
    bb, cb16 = bg.astype(BF16), cg.astype(BF16)
    cb = lax.dot_general(cb16, bb, _NT, preferred_element_type=F32)
    xr = xs * dtx
    y_diag = []
    for p in range(GROUP_W // LANES):
        lanes = slice(p * LANES, (p + 1) * LANES)
        c_pair = cs[:, lanes]
        c_swap = _swap_halves(c_pair)
        m = []
        for c_head in (jnp.where(first_head, c_pair, c_swap), jnp.where(first_head, c_swap, c_pair)):
            m.append(cb * jnp.exp(jnp.where(causal, c_head - c_head.T, -jnp.inf)))
        x_pair = xr[:, lanes]
        x_diag = jnp.concatenate([jnp.where(first_head, x_pair, 0.0), jnp.where(first_head, 0.0, x_pair)], axis=0)
        y_diag.append(jnp.dot(jnp.concatenate(m, axis=1).astype(BF16), x_diag.astype(BF16),
                              preferred_element_type=F32))
    st = lax.dot_general(bb, (xr * jnp.exp(tot - cs)).astype(BF16), (((0,), (0,)), ((), ())),
                         preferred_element_type=F32)
    new_state = state * jnp.exp(tot) + st
    y_off = jnp.dot(cb16, state.astype(BF16), preferred_element_type=F32) * jnp.exp(cs)
    return jnp.concatenate(y_diag, axis=1) + y_off + dsk * xs, new_state


BC_WIDTH = SSM_GROUPS * SSM_STATE


def _ssd_specs(chunk):
    xs = pl.BlockSpec((BLOCK, SSM_INNER), lambda c: (chunk(c), 0))
    dt = pl.BlockSpec((BLOCK, LANES), lambda c: (chunk(c), DT_TILE))
    expand = _full((LANES, SSM_INNER))
    b = pl.BlockSpec((BLOCK, BC_WIDTH), lambda c: (chunk(c), SSM_INNER // BC_WIDTH))
    cc = pl.BlockSpec((BLOCK, BC_WIDTH), lambda c: (chunk(c), SSM_INNER // BC_WIDTH + 1))
    par = _full((1, LANES))
    state = pl.BlockSpec((None, SSM_STATE, SSM_INNER), lambda c: (chunk(c), 0, 0))
    return xs, dt, expand, b, cc, par, state


def _group_lanes(g):
    return slice(g * GROUP_W, (g + 1) * GROUP_W), slice(g * SSM_STATE, (g + 1) * SSM_STATE)


def ssd_fwd(xbc, proj, expand, bias, alog, dsk):
    nb = xbc.shape[0] // BLOCK
    xs, dt, ex, b, cc, par, state = _ssd_specs(lambda c: c)

    def body(x_ref, dt_ref, e_ref, bi_ref, al_ref, dk_ref, b_ref, c_ref, y_ref, sp_ref, st_ref):
        @pl.when(pl.program_id(0) == 0)
        def _():
            st_ref[...] = jnp.zeros_like(st_ref)

        per_lane = _ssd_heads(dt_ref[...], bi_ref[...], al_ref[...], dk_ref[...], e_ref[...])
        for g in range(SSM_GROUPS):
            wide, tile = _group_lanes(g)
            entering = st_ref[:, wide]
            sp_ref[:, wide] = entering
            y_ref[:, wide], st_ref[:, wide] = _ssd_group(x_ref[:, wide], per_lane[:, wide], b_ref[:, tile],
                                                         c_ref[:, tile], entering)

    return _call(body, name="ssd_fwd", grid=(nb,), in_specs=[xs, dt, ex, par, par, par, b, cc],
                 out_specs=[xs, state],
                 out_shape=[jax.ShapeDtypeStruct((nb * BLOCK, SSM_INNER), F32),
                            jax.ShapeDtypeStruct((nb, SSM_STATE, SSM_INNER), F32)],
                 scratch=[pltpu.VMEM((SSM_STATE, SSM_INNER), F32)],
                 sem=("arbitrary",))(xbc, proj, expand, bias, alog, dsk, xbc, xbc)


def ssd_bwd(dy, xbc, proj, expand, bias, alog, dsk, states, comm):
    nb = xbc.shape[0] // BLOCK
    last = nb - 1
    xs, dt, ex, b, cc, par, state = _ssd_specs(lambda c: last - c)
    tile = pl.BlockSpec((BLOCK, LANES), lambda c: (last - c, 0))
    nspec = pl.BlockSpec((BLOCK, BC_WIDTH), lambda c: (last - c, 0))

    def body(dy_ref, x_ref, dt_ref, e_ref, bi_ref, al_ref, dk_ref, b_ref, c_ref, sp_ref,
             dx_ref, ddt_ref, db_ref, dc_ref, dbi_ref, dal_ref, ddk_ref, ds_ref):
        @pl.when(pl.program_id(0) == 0)
        def _():
            ds_ref[...] = jnp.zeros_like(ds_ref)
            dbi_ref[...] = jnp.zeros_like(dbi_ref)
            dal_ref[...] = jnp.zeros_like(dal_ref)
            ddk_ref[...] = jnp.zeros_like(ddk_ref)

        expand = e_ref[...]
        per_lane, heads_vjp = jax.vjp(lambda t, bi, al, dk: _ssd_heads(t, bi, al, dk, expand), dt_ref[...], bi_ref[...],
                                      al_ref[...], dk_ref[...])
        d_per_lane = []
        for g in range(SSM_GROUPS):
            wide, tile_lanes = _group_lanes(g)
            _, vjp = jax.vjp(_ssd_group, x_ref[:, wide], per_lane[:, wide], b_ref[:, tile_lanes], c_ref[:, tile_lanes],
                             sp_ref[:, wide])
            (dx_ref[:, wide], d_lanes, db_ref[:, tile_lanes], dc_ref[:, tile_lanes],
             ds_ref[:, wide]) = vjp((dy_ref[:, wide], ds_ref[:, wide]))
            d_per_lane.append(d_lanes)
        ddt, dbi, dal, ddk = heads_vjp(jnp.concatenate(d_per_lane, axis=1))
        ddt_ref[...] = ddt.astype(BF16)
        dbi_ref[...] += dbi
        dal_ref[...] += dal
        ddk_ref[...] += ddk

    def at():
        c = pl.program_id(0)
        return c == 0, c == 0, c == last

    par_shape = jax.ShapeDtypeStruct((1, LANES), F32)
    return _call_with_comm(
        body, comm, at, (dy, xbc, proj, expand, bias, alog, dsk, xbc, xbc, states), name="ssd_bwd",
        grid=(nb,), in_specs=[xs, xs, dt, ex, par, par, par, b, cc, state],
        out_specs=[xs, tile, nspec, nspec, par, par, par],
        out_shape=[jax.ShapeDtypeStruct((nb * BLOCK, SSM_INNER), F32), jax.ShapeDtypeStruct((nb * BLOCK, LANES), BF16),
                   jax.ShapeDtypeStruct((nb * BLOCK, BC_WIDTH), F32), jax.ShapeDtypeStruct((nb * BLOCK, BC_WIDTH), F32),
                   par_shape, par_shape, par_shape],
        scratch=[pltpu.VMEM((SSM_STATE, SSM_INNER), F32)])


SLAB_ROWS = 16
SLAB_META_ROW = 8
SLAB_META_SHAPE = (8, 2 * D_MODEL)
SLAB_LOSS_ROW = 7


def pack_small(dcw, dcb, dgpre, dgpost, dbias, dalog, ddsk, dsinks, dgn, dmeta, loss_tile):
    def body(cw, cb, gpre, gpost, dtb, al, dk, sk, gn, meta, loss, o_ref):
        o_ref[...] = jnp.zeros_like(o_ref)
        o_ref[SLAB_LOSS_ROW:SLAB_LOSS_ROW + 1, 0:LANES] = loss[0:1, :]
        o_ref[0:CONV_WIDTH, :] = cw[...]
        o_ref[4:5, :] = cb[...]
        o_ref[5:6, 0:1024] = gpre[...]
        o_ref[5:6, 1024:2048] = gpost[...]
        o_ref[5:6, 2048:2176] = dtb[...]
        o_ref[5:6, 2176:2304] = al[...]
        o_ref[5:6, 2304:2432] = dk[...]
        o_ref[5:6, 2432:2560] = sk[...]
        o_ref[6:7, 0:SSM_INNER] = gn[...]
        o_ref[SLAB_META_ROW:SLAB_ROWS, 0:SLAB_META_SHAPE[1]] = meta[...]

    args = (dcw, dcb, dgpre, dgpost, dbias, dalog, ddsk, dsinks, dgn, dmeta, loss_tile)
    return _call(body, name="pack_small", in_specs=[_full(a.shape) for a in args],
                 out_specs=_full((SLAB_ROWS, CONV_DIM)), out_shape=jax.ShapeDtypeStruct((SLAB_ROWS, CONV_DIM), F32))(*args)


def _lane_tile(v):
    return jnp.pad(v, ((0, 0), (0, LANES - v.shape[1])))


def kernel(x, meta_tokens, g_pre, w_in, conv_w, conv_b, dt_bias, a_log, d_skip, attn_sinks, g_ssm_norm, w_out_att, w_out_ssm, w_out, g_post, loss_target, m_meta_tokens, m_g_pre, m_w_in, m_conv_w, m_conv_b, m_dt_bias, m_a_log, m_d_skip, m_attn_sinks, m_g_ssm_norm, m_w_out_att, m_w_out_ssm, m_w_out, m_g_post, v_meta_tokens, v_g_pre, v_w_in, v_conv_w, v_conv_b, v_dt_bias, v_a_log, v_d_skip, v_attn_sinks, v_g_ssm_norm, v_w_out_att, v_w_out_ssm, v_w_out, v_g_post):
    chip = _chip_index()

    conv_w_rows = jnp.pad(conv_w[0], ((0, 2 * 8 - CONV_WIDTH), (0, 0)))
    w_in_t, m_w_in_t, v_w_in_t = w_in[0].T, m_w_in[0].T, v_w_in[0].T
    gathered_w_in, g_conv_w, g_meta = run_comm("gather_w_in",
                                               TwoLevelGather([pack_w_in(w_in_t), conv_w_rows, meta_tokens]))
    w_all_t = unpack_w_in(gathered_w_in)
    cw_full = g_conv_w[:, :CONV_WIDTH].transpose(1, 0, 2).reshape(CONV_WIDTH, CONV_DIM)
    meta_full = g_meta.transpose(1, 0, 2).reshape(N_META, D_MODEL)
    behind_w_in = 0.0 * g_meta[0, 0, 0]
    w_out_flight, w_out_started = chip_exchange_start(
        "gather_w_out_start", [(w[0] + behind_w_in).astype(BF16) for w in (w_out_att, w_out_ssm, w_out)], False)

    h, u = prep(x, meta_full, g_pre)
    proj = project("in_proj", u, w_all_t, w_out_started)

    sinks3 = attn_sinks.reshape(ATT_Q_HEADS, 1, 1)
    a_att = attn_fwd(proj, sinks3)

    xbc = conv_fwd(proj, cw_full, conv_b)
    expand = _head_expand()
    head_pars = (_lane_tile(dt_bias), _lane_tile(a_log), _lane_tile(d_skip))
    y_ssd, states = ssd_fwd(xbc, proj, expand, *head_pars)
    woa, wos, wo = [g.reshape(-1, D_MODEL) for g in chip_exchange_wait("gather_w_out_wait", w_out_flight, False, y_ssd)]

    (yn, merged, dout, dy_att, dy_ssm, da_att, dy_ssd, dz_ssm, dga, dgs, dres, loss_tile, dg_post, dgn) = tail(
        y_ssd, proj, a_att, x, loss_target, woa, wos, wo, g_ssm_norm, g_post)

    dwo = mm_tn("out_proj_dw", merged, dout)
    dwoa = mm_tn("att_out_dw", a_att, dy_att)
    dwos = mm_tn("ssm_out_dw", yn, dy_ssm)
    dq, dz_att, dk, dv, dkmeta, dvmeta, dsinks3 = attn_bwd(da_att, proj, sinks3)
    dk = dk.at[PAD_ROWS:BLOCK].add(dkmeta).astype(BF16)
    dv = dv.at[PAD_ROWS:BLOCK].add(dvmeta).astype(BF16)

    def pieces(g):
        return g.reshape(4, 2, g.shape[0] // 8, g.shape[1])

    def to_owner(g):
        return (lambda ref, dev: ref.at[_chip_of(dev), dev[2]], (g.shape[0] // 8, g.shape[1]))

    (dxs, ddt_tile, dbg, dcg, dbias, dalog, ddsk), sent_w_out = ssd_bwd(
        dy_ssd, xbc, proj, expand, *head_pars, states,
        DirectExchange([pieces(dwoa), pieces(dwos), pieces(dwo)], [to_owner(dwoa), to_owner(dwos), to_owner(dwo)],
                       ALL_MASKS, "dev", 8))
    dxs_raw, dcw_xs, dcb_xs = conv_bwd("conv_bwd_x", [dxs], 0, proj, cw_full, conv_b)
    dbc_raw, dcw_bc, dcb_bc = conv_bwd("conv_bwd_bc", [dbg, dcg], SSM_INNER // D_MODEL, proj, cw_full, conv_b)
    dcw = jnp.concatenate([dcw_xs, dcw_bc], axis=1)
    dcb = jnp.concatenate([dcb_xs, dcb_bc], axis=1)

    narrow = jnp.concatenate([dk, dv, ddt_tile, jnp.zeros((dk.shape[0], N_ACT - N_ALIGNED), BF16)], axis=1)
    dproj = [dz_ssm, dxs_raw, dbc_raw, dq, dz_att, dga, dgs, narrow]
    partial, from_sibling = weight_grad_t("in_proj_dw", dproj, u)
    chip_sum = sum_pair(partial.reshape(4, 2, PACK_W // 2, D_MODEL), from_sibling)
    grads_flight, started = chip_exchange_start("reduce_w_in_start", [chip_sum], True)
    du = project_back("in_proj_dx", dproj, w_all_t, started)
    grad_x, dmeta, dg_pre = prep_bwd(h, du, dres, g_pre)

    halves = [sum_slots("sum_" + nm, r) for nm, r in zip(("w_out_att", "w_out_ssm", "w_out"), sent_w_out)]
    shared = run_comm("share_w_out", DirectExchange(halves, [None] * 3, SIBLING_MASK, "core", 2))
    g_woa, g_wos, g_wo = [f.reshape(2 * f.shape[1], f.shape[2]) for f in shared]
    d_woa, nm_woa, nv_woa = adamw_rows("adamw_w_out_att", g_woa, w_out_att[0], m_w_out_att[0], v_w_out_att[0])
    d_wos, nm_wos, nv_wos = adamw_rows("adamw_w_out_ssm", g_wos, w_out_ssm[0], m_w_out_ssm[0], v_w_out_ssm[0])
    d_wo, nm_wo, nv_wo = adamw_rows("adamw_w_out", g_wo, w_out[0], m_w_out[0], v_w_out[0])

    sent_w_in, = chip_exchange_wait("reduce_w_in_wait", grads_flight, True, d_wos)
    slab = pack_small(dcw, dcb, dg_pre, dg_post, dbias, dalog, ddsk, _lane_tile(dsinks3.reshape(1, ATT_Q_HEADS)), dgn,
                      dmeta.reshape(SLAB_META_SHAPE), loss_tile)
    shared_w_in, slabs = run_comm("share_w_in", Both(
        DirectExchange([sum_slots("sum_w_in", sent_w_in)], [None], SIBLING_MASK, "core", 2),
        DirectExchange([slab], [None], ALL_MASKS, "dev", 8)))
    small = sum_slots("sum_small", slabs)
    loss = small[SLAB_LOSS_ROW, 0]
    g_w_in, d_w_in, nm_w_in, nv_w_in = [
        a.T for a in adamw_w_in(shared_w_in.reshape(PACK_W, D_MODEL), w_in_t, m_w_in_t, v_w_in_t)]

    cw_cols = CONV_DIM // 4
    meta_cols = D_MODEL // 4
    g_small = {
        "meta_tokens": lax.dynamic_slice(
            small[SLAB_META_ROW:SLAB_ROWS, 0:SLAB_META_SHAPE[1]].reshape(N_META, D_MODEL), (0, chip * meta_cols),
            (N_META, meta_cols)),
        "g_pre": small[5:6, 0:1024],
        "conv_w": lax.dynamic_slice(small, (0, chip * cw_cols), (CONV_WIDTH, cw_cols)),
        "conv_b": small[4:5, :],
        "dt_bias": small[5:6, 2048:2048 + SSM_HEADS],
        "a_log": small[5:6, 2176:2176 + SSM_HEADS],
        "d_skip": small[5:6, 2304:2304 + SSM_HEADS],
        "attn_sinks": small[5:6, 2432:2432 + ATT_Q_HEADS],
        "g_ssm_norm": small[6:7, 0:SSM_INNER],
        "g_post": small[5:6, 1024:2048],
    }
    names = list(g_small)
    w_small = dict(meta_tokens=meta_tokens, g_pre=g_pre, conv_w=conv_w[0], conv_b=conv_b, dt_bias=dt_bias, a_log=a_log,
                   d_skip=d_skip, attn_sinks=attn_sinks, g_ssm_norm=g_ssm_norm, g_post=g_post)
    m_small = dict(meta_tokens=m_meta_tokens, g_pre=m_g_pre, conv_w=m_conv_w[0], conv_b=m_conv_b, dt_bias=m_dt_bias,
                   a_log=m_a_log, d_skip=m_d_skip, attn_sinks=m_attn_sinks, g_ssm_norm=m_g_ssm_norm, g_post=m_g_post)
    v_small = dict(meta_tokens=v_meta_tokens, g_pre=v_g_pre, conv_w=v_conv_w[0], conv_b=v_conv_b, dt_bias=v_dt_bias,
                   a_log=v_a_log, d_skip=v_d_skip, attn_sinks=v_attn_sinks, g_ssm_norm=v_g_ssm_norm, g_post=v_g_post)
    upd = dict(zip(names, adamw_small([g_small[k] for k in names], [w_small[k] for k in names],
                                      [m_small[k] for k in names], [v_small[k] for k in names])))

    lead = {"conv_w"}

    def shaped(name, a):
        return a[None] if name in lead else a

    grads = dict(g_small, w_in=g_w_in, w_out_att=g_woa, w_out_ssm=g_wos, w_out=g_wo)
    deltas = dict({k: upd[k][0] for k in names}, w_in=d_w_in, w_out_att=d_woa, w_out_ssm=d_wos, w_out=d_wo)
    new_m = dict({k: upd[k][1] for k in names}, w_in=nm_w_in, w_out_att=nm_woa, w_out_ssm=nm_wos, w_out=nm_wo)
    new_v = dict({k: upd[k][2] for k in names}, w_in=nv_w_in, w_out_att=nv_woa, w_out_ssm=nv_wos, w_out=nv_wo)
    lead |= {"w_in", "w_out_att", "w_out_ssm", "w_out"}
    order = ["meta_tokens", "g_pre", "w_in", "conv_w", "conv_b", "dt_bias", "a_log", "d_skip", "attn_sinks",
             "g_ssm_norm", "w_out_att", "w_out_ssm", "w_out", "g_post"]
    outs = [loss, grad_x]
    for group in (grads, deltas, new_m, new_v):
        outs += [shaped(k, group[k]) for k in order]
    return tuple(outs)
```

```python
import numpy as np
import jax
import jax.numpy as jnp
from jax import lax
from jax.experimental import pallas as pl
from jax.experimental.pallas import tpu as pltpu

F32 = jnp.float32
BF16 = jnp.bfloat16

D_MODEL = 1024
N_META = 16
BLOCK = 128
PAD_ROWS = BLOCK - N_META
NORM_EPS = 1e-6
HEAD_DIM = 64
ATT_Q_HEADS = 16
ATT_KV_HEADS = 4
ATT_GROUP = 4
SSM_INNER = 2048
SSM_HEADS = 32
SSM_GROUPS = 4
SSM_STATE = 128
CONV_WIDTH = 4
CONV_DIM = 3072
LANES = 128

ADAM_LR = 0.001
ADAM_B1 = 0.9
ADAM_B2 = 0.999
ADAM_EPS = 1e-08
ADAM_WD = 0.01
ADAM_STEP = 10

VMEM_LIMIT = 48 * 1024 * 1024

SHARD_W = 2440
PACK_W = 2560
SHARD_STRIDE = 2432
N_ALIGNED = 9856
N_ACT = 10240
SEG = {
    "q": (0, 1024, 5120), "k": (1024, 256, 9216), "v": (1280, 256, 9472), "z_att": (1536, 1024, 6144),
    "z_ssm": (2560, 2048, 0), "xbc": (4608, 3072, 2048), "dt": (7680, 128, 9728),
    "gate_att": (7808, 1024, 7168), "gate_ssm": (8832, 1024, 8192),
}
DT_STORED_START = 7680
DT_PAD = LANES - SSM_HEADS


def _act_col(aligned_col):
    for a0, w, p0 in SEG.values():
        if a0 <= aligned_col < a0 + w:
            return p0 + aligned_col - a0
    raise ValueError(aligned_col)


def _call(body, *, name, out_shape, in_specs, out_specs, grid=(), scratch=(), sem=None, aliases=None):
    return pl.pallas_call(
        body, out_shape=out_shape, grid=grid, in_specs=in_specs, out_specs=out_specs, scratch_shapes=list(scratch),
        name=name, input_output_aliases=aliases or {},
        compiler_params=pltpu.CompilerParams(dimension_semantics=sem, vmem_limit_bytes=VMEM_LIMIT))


def _full(shape):
    n = len(shape)
    return pl.BlockSpec(shape, lambda *_: (0,) * n)


def _chip_index():
    return lax.axis_index("x") * 2 + lax.axis_index("y")


_sigmoid = jax.nn.sigmoid


def _silu(z):
    return z * _sigmoid(z)


def _rms(x, g):
    return x * lax.rsqrt(jnp.mean(x * x, axis=-1, keepdims=True) + NORM_EPS) * g


def _peer(mask):
    x, y, c = lax.axis_index("x"), lax.axis_index("y"), lax.axis_index("c")
    return ((1 - x) if mask & 4 else x, (1 - y) if mask & 2 else y, (1 - c) if mask & 1 else c)


def _me():
    return lax.axis_index("x"), lax.axis_index("y"), lax.axis_index("c")


def _chip_of(dev):
    return 2 * dev[0] + dev[1]


CHIP_MASKS = (4, 2, 6)
ALL_MASKS = (1, 2, 3, 4, 5, 6, 7)
SIBLING_MASK = (1,)


def _remote(src, dst, send_sem, recv_sem, dev):
    return pltpu.make_async_remote_copy(src_ref=src, dst_ref=dst, send_sem=send_sem, recv_sem=recv_sem,
                                        device_id=dev, device_id_type=pl.DeviceIdType.MESH)


class _StagedCopy:
    def __init__(self, src, stage, dst, load_sem, store_sem):
        self.load = pltpu.make_async_copy(src, stage, load_sem)
        self.store = pltpu.make_async_copy(stage, dst, store_sem)

    def start(self):
        self.load.start()
        self.load.wait()
        self.store.start()

    def wait(self):
        self.store.wait()


class DirectExchange:
    def __init__(self, arrays, pieces, masks, slot_kind, nslots, keep_own=True):
        self.arrays, self.pieces, self.masks, self.slot_kind = list(arrays), list(pieces), masks, slot_kind
        self.keep_own = keep_own
        n, nk = len(arrays), len(masks)
        shapes = [a.shape if p is None else p[1] for a, p in zip(arrays, pieces)]
        self.out_shape = [jax.ShapeDtypeStruct((nslots,) + tuple(s), a.dtype) for s, a in zip(shapes, arrays)]
        self.scratch = [pltpu.SemaphoreType.DMA((n * nk,)), pltpu.SemaphoreType.DMA((n * nk,))]
        if keep_own:
            self.scratch += [pltpu.SemaphoreType.DMA((2 * n,))] + [pltpu.VMEM(s, a.dtype) for s, a in zip(shapes, arrays)]
        self.has_mid = False

    def _copies(self, ins, outs, scratch):
        send_sems, recv_sems = scratch[:2]
        me = _me()
        slot = {"chip": _chip_of(me), "dev": 4 * me[0] + 2 * me[1] + me[2], "core": me[2]}[self.slot_kind]
        nk = len(self.masks)

        def piece(a, dev):
            return ins[a] if self.pieces[a] is None else self.pieces[a][0](ins[a], dev)

        local = []
        if self.keep_own:
            local_sems, stages = scratch[2], scratch[3:]
            local = [_StagedCopy(piece(a, me), stages[a], outs[a].at[slot], local_sems.at[2 * a], local_sems.at[2 * a + 1])
                     for a in range(len(ins))]
        remote = []
        for a in range(len(ins)):
            for ki, mask in enumerate(self.masks):
                dev = _peer(mask)
                remote.append(_remote(piece(a, dev), outs[a].at[slot], send_sems.at[a * nk + ki],
                                      recv_sems.at[a * nk + ki], dev))
        return local, remote

    def start(self, ins, outs, scratch):
        local, remote = self._copies(ins, outs, scratch)
        for cp in remote + local:
            cp.start()

    def finish(self, ins, outs, scratch):
        local, remote = self._copies(ins, outs, scratch)
        for cp in remote + local:
            cp.wait()


SPLIT_ROWS = 16


class TwoLevelGather:
    def __init__(self, arrays):
        self.arrays = list(arrays)
        n = len(arrays)
        self.out_shape = [jax.ShapeDtypeStruct((4,) + a.shape, a.dtype) for a in arrays]
        self.scratch = ([pltpu.SemaphoreType.DMA((4 * n,)), pltpu.SemaphoreType.DMA((4 * n,)),
                         pltpu.SemaphoreType.DMA((3 * n,)), pltpu.SemaphoreType.DMA((3 * n,)),
                         pltpu.SemaphoreType.DMA((2 * n,))] + [pltpu.VMEM(a.shape, a.dtype) for a in arrays])
        self.has_mid = True

    def _copies(self, ins, outs, scratch):
        ici_send, ici_recv, fwd_send, fwd_recv, local_sems = scratch[:5]
        stages = scratch[5:]
        me = _me()
        sibling, in_x, in_y, diagonal = _peer(1), _peer(4), _peer(2), _peer(6)
        plan = []
        for a in range(len(ins)):
            half = ins[a].shape[0] // 2
            first = half // 2 if half % (2 * SPLIT_ROWS) == 0 else half
            mine = pl.ds(me[2] * half, half)
            local = _StagedCopy(ins[a], stages[a], outs[a].at[_chip_of(me)], local_sems.at[2 * a], local_sems.at[2 * a + 1])

            def ici(k, src, dst, dev):
                return _remote(src, dst, ici_send.at[4 * a + k], ici_recv.at[4 * a + k], dev)

            def d2d(k, chip):
                zone = outs[a].at[_chip_of(chip), mine]
                return _remote(zone, zone, fwd_send.at[3 * a + k], fwd_recv.at[3 * a + k], sibling)

            own_zone = outs[a].at[_chip_of(me), mine]
            from_x = outs[a].at[_chip_of(in_x), pl.ds(me[2] * half, first)]
            onward = [ici(2, from_x, from_x, in_y), None]
            if first < half:
                from_y = outs[a].at[_chip_of(in_y), pl.ds(me[2] * half + first, half - first)]
                onward[1] = ici(3, from_y, from_y, in_x)
            plan.append(dict(
                local=local,
                own=[ici(0, ins[a].at[mine], own_zone, in_x), ici(1, ins[a].at[mine], own_zone, in_y)],
                onward=onward, sibling=[d2d(0, in_x), d2d(1, in_y), d2d(2, diagonal)]))
        return plan

    def start(self, ins, outs, scratch):
        for p in self._copies(ins, outs, scratch):
            for cp in p["own"]:
                cp.start()
            p["local"].start()

    def mid(self, ins, outs, scratch):
        plan = self._copies(ins, outs, scratch)
        for p in plan:
            for k in range(2):
                p["own"][k].wait_recv()
                if p["onward"][k] is not None:
                    p["onward"][k].start()
                p["sibling"][k].start()
        for p in plan:
            for cp in p["onward"]:
                if cp is not None:
                    cp.wait_recv()
            p["sibling"][2].start()

    def finish(self, ins, outs, scratch):
        for p in self._copies(ins, outs, scratch):
            for cp in p["sibling"]:
                cp.wait_recv()
            for cp in p["own"] + p["sibling"] + [cp for cp in p["onward"] if cp is not None]:
                cp.wait_send()
            p["local"].wait()


class Both:
    def __init__(self, a, b):
        self.a, self.b = a, b
        self.arrays, self.out_shape = a.arrays + b.arrays, a.out_shape + b.out_shape
        self.scratch = a.scratch + b.scratch
        self.has_mid = False
        assert not (a.has_mid or b.has_mid)

    def _parts(self, ins, outs, sems):
        na, sa = len(self.a.arrays), len(self.a.scratch)
        return (ins[:na], outs[:na], sems[:sa]), (ins[na:], outs[na:], sems[sa:])

    def start(self, ins, outs, sems):
        pa, pb = self._parts(ins, outs, sems)
        self.a.start(*pa)
        self.b.start(*pb)

    def finish(self, ins, outs, sems):
        pa, pb = self._parts(ins, outs, sems)
        self.a.finish(*pa)
        self.b.finish(*pb)


_ANY = pl.BlockSpec(memory_space=pl.ANY)


def run_comm(name, comm):
    n = len(comm.arrays)

    def body(*refs):
        ins, outs, sems = refs[:n], refs[n:2 * n], refs[2 * n:]
        comm.start(ins, outs, sems)
        if comm.has_mid:
            comm.mid(ins, outs, sems)
        comm.finish(ins, outs, sems)

    return pl.pallas_call(body, name=name, out_shape=comm.out_shape, in_specs=[_ANY] * n, out_specs=[_ANY] * n,
                          scratch_shapes=comm.scratch,
                          compiler_params=pltpu.CompilerParams(vmem_limit_bytes=VMEM_LIMIT))(*comm.arrays)


_HBM = pl.BlockSpec(memory_space=pltpu.HBM)
_SEM = pl.BlockSpec(memory_space=pltpu.SEMAPHORE)
_SIDE_EFFECT = pltpu.SideEffectType.DATAFLOW_SIDE_EFFECTING


def _chip_copies(srcs, lands, send_sems, recv_sems, by_target):
    me = _me()
    copies = []
    for a, (src, land) in enumerate(zip(srcs, lands)):
        for ki, mask in enumerate(CHIP_MASKS):
            dev = _peer(mask)
            k = a * len(CHIP_MASKS) + ki
            piece = src.at[_chip_of(dev)] if by_target else src
            copies.append(_remote(piece, land.at[_chip_of(me)], send_sems.at[k], recv_sems.at[k], dev))
    return copies


def chip_exchange_start(name, arrays, by_target):
    n = len(arrays)
    nsem = n * len(CHIP_MASKS)
    piece_shapes = [a.shape[1:] if by_target else a.shape for a in arrays]

    def body(*refs):
        srcs, lands = refs[:n], refs[n:2 * n]
        send_sems, recv_sems = refs[2 * n:2 * n + 2]
        token = refs[4 * n + 2]
        stages, local_sems = refs[4 * n + 3:5 * n + 3], refs[5 * n + 3]
        me = _me()
        for cp in _chip_copies(srcs, lands, send_sems, recv_sems, by_target):
            cp.start()
        own = [_StagedCopy(srcs[a].at[_chip_of(me)] if by_target else srcs[a], stages[a], lands[a].at[_chip_of(me)],
                           local_sems.at[2 * a], local_sems.at[2 * a + 1]) for a in range(n)]
        for cp in own:
            cp.load.start()
        for cp in own:
            cp.load.wait()
            cp.store.start()
        for cp in own:
            cp.store.wait()
        token[...] = jnp.zeros_like(token)

    lands = [lax.empty((4,) + tuple(s), a.dtype) for s, a in zip(piece_shapes, arrays)]
    hbm = lambda a: pltpu.HBM(a.shape, a.dtype)
    res = pl.pallas_call(
        body, name=name,
        out_shape=(pltpu.SemaphoreType.DMA((nsem,)), pltpu.SemaphoreType.DMA((nsem,)), *[hbm(a) for a in arrays],
                   *[hbm(l) for l in lands], jax.ShapeDtypeStruct((8, LANES), F32)),
        in_specs=(_HBM,) * (2 * n), out_specs=(_SEM, _SEM) + (_HBM,) * (2 * n) + (pl.BlockSpec(memory_space=pltpu.VMEM),),
        input_output_aliases={i: i + 2 for i in range(2 * n)},
        scratch_shapes=[pltpu.VMEM(tuple(s), a.dtype) for s, a in zip(piece_shapes, arrays)]
        + [pltpu.SemaphoreType.DMA((2 * n,))],
        compiler_params=pltpu.CompilerParams(has_side_effects=_SIDE_EFFECT, vmem_limit_bytes=VMEM_LIMIT),
    )(*[pltpu.with_memory_space_constraint(a, pltpu.HBM) for a in arrays + lands])
    return res[:-1], res[-1]


def chip_exchange_wait(name, in_flight, by_target, after):
    send_sems, recv_sems, *thru = in_flight
    n = len(thru) // 2

    def body(*refs):
        srcs, lands, (send, recv) = refs[:n], refs[n:2 * n], refs[2 * n:2 * n + 2]
        for cp in _chip_copies(srcs, lands, send, recv, by_target):
            cp.wait_send()
            cp.wait_recv()

    return pl.pallas_call(
        body, name=name, out_shape=tuple(pltpu.HBM(t.shape, t.dtype) for t in thru),
        in_specs=(_HBM,) * (2 * n) + (_SEM, _SEM, pl.BlockSpec(memory_space=pl.ANY)), out_specs=(_HBM,) * (2 * n),
        input_output_aliases={i: i for i in range(2 * n)},
        compiler_params=pltpu.CompilerParams(has_side_effects=_SIDE_EFFECT),
    )(*thru, send_sems, recv_sems, after)[n:]


def _call_with_comm(body, comm, steps, args, *, name, out_shape, in_specs, out_specs, grid, scratch=(),
                    after_finish=None):
    ni, no, ns, nc = len(in_specs), len(out_specs), len(scratch), len(comm.arrays)

    def full_body(*refs):
        ins, cins = refs[:ni], refs[ni:ni + nc]
        outs, couts = refs[ni + nc:ni + nc + no], refs[ni + nc + no:ni + 2 * nc + no]
        scr, csems = refs[ni + 2 * nc + no:ni + 2 * nc + no + ns], refs[ni + 2 * nc + no + ns:]
        first, middle, last = steps()
        pl.when(first)(lambda: comm.start(cins, couts, csems))
        if comm.has_mid:
            pl.when(middle)(lambda: comm.mid(cins, couts, csems))
        body(*ins, *outs, *scr)

        @pl.when(last)
        def _():
            comm.finish(cins, couts, csems)
            if after_finish is not None:
                after_finish(ins, couts, outs, scr)

    res = pl.pallas_call(
        full_body, name=name, out_shape=list(out_shape) + comm.out_shape, grid=grid,
        in_specs=list(in_specs) + [_ANY] * nc, out_specs=list(out_specs) + [_ANY] * nc,
        scratch_shapes=list(scratch) + comm.scratch,
        compiler_params=pltpu.CompilerParams(dimension_semantics=("arbitrary",) * len(grid),
                                             vmem_limit_bytes=VMEM_LIMIT))(*args, *comm.arrays)
    return res[:no], res[no:]


def _shard_pieces(chip):
    if chip < 3:
        return [(0, SHARD_W, 8 * chip)]
    behind_dt = DT_STORED_START + SSM_HEADS - 3 * SHARD_W
    return [(0, behind_dt, 24), (behind_dt, SHARD_W - behind_dt, behind_dt + 24 + DT_PAD)]


W_IN_COLS = 256


def pack_w_in(wt):
    def body(w_ref, o_ref, pad_ref):
        chip = _chip_index()
        pad_ref[...] = jnp.zeros_like(pad_ref)
        for cv in range(4):
            @pl.when(chip == cv)
            def _():
                for src, n, dst in _shard_pieces(cv):
                    pad_ref[dst:dst + n, :] = w_ref[src:src + n, :]
        o_ref[...] = pad_ref[...].astype(BF16)

    return _call(body, name="pack_w_in", grid=(D_MODEL // W_IN_COLS,),
                 in_specs=[pl.BlockSpec((SHARD_W, W_IN_COLS), lambda i: (0, i))],
                 out_specs=pl.BlockSpec((PACK_W, W_IN_COLS), lambda i: (0, i)),
                 out_shape=jax.ShapeDtypeStruct((PACK_W, D_MODEL), BF16),
                 scratch=[pltpu.VMEM((PACK_W, W_IN_COLS), F32)], sem=("parallel",))(wt)


def _tile_runs():
    runs, fix = [], []
    for t in range(N_ALIGNED // LANES):
        s = min(t // 19, 3)
        j = t - 19 * s
        p = _act_col(t * LANES)
        if runs and runs[-1][1] == s and runs[-1][0] + runs[-1][3] == p and runs[-1][2] + runs[-1][3] == j * LANES:
            runs[-1][3] += LANES
        else:
            runs.append([p, s, j * LANES, LANES])
        if j == 0 and s > 0:
            fix.append((p, s - 1))
    return runs, fix


def unpack_w_in(bg):
    runs, fix = _tile_runs()

    def body(b_ref, o_ref):
        for p, s, j, w in runs:
            o_ref[p:p + w, :] = b_ref[s, j:j + w, :]
        for p, s in fix:
            o_ref[p:p + LANES, :] = o_ref[p:p + LANES, :] + b_ref[s, SHARD_STRIDE:PACK_W, :]
        o_ref[N_ALIGNED:N_ACT, :] = jnp.zeros((N_ACT - N_ALIGNED, W_IN_COLS), BF16)

    return _call(body, name="unpack_w_in", grid=(D_MODEL // W_IN_COLS,),
                 in_specs=[pl.BlockSpec((4, PACK_W, W_IN_COLS), lambda i: (0, 0, i))],
                 out_specs=pl.BlockSpec((N_ACT, W_IN_COLS), lambda i: (0, i)),
                 out_shape=jax.ShapeDtypeStruct((N_ACT, D_MODEL), BF16), sem=("parallel",))(bg)


def _adamw(w, g, m, v):
    m = ADAM_B1 * m + (1.0 - ADAM_B1) * g
    v = ADAM_B2 * v + (1.0 - ADAM_B2) * jnp.square(g)
    m_hat = m / (1.0 - ADAM_B1 ** ADAM_STEP)
    v_hat = v / (1.0 - ADAM_B2 ** ADAM_STEP)
    delta = -ADAM_LR * (m_hat / (jnp.sqrt(v_hat) + ADAM_EPS) + ADAM_WD * w)
    return delta, m, v


def adamw_w_in(g_packed, wt, mt, vt):
    cols = LANES

    def body(g_ref, w_ref, m_ref, v_ref, go_ref, d_ref, mo_ref, vo_ref):
        chip = _chip_index()
        for cv in range(4):
            @pl.when(chip == cv)
            def _():
                for dst, n, src in _shard_pieces(cv):
                    go_ref[dst:dst + n, :] = g_ref[src:src + n, :]
        d_ref[...], mo_ref[...], vo_ref[...] = _adamw(w_ref[...], go_ref[...], m_ref[...], v_ref[...])

    spec = pl.BlockSpec((SHARD_W, cols), lambda i: (0, i))
    shp = jax.ShapeDtypeStruct((SHARD_W, D_MODEL), F32)
    return _call(body, name="adamw_w_in", grid=(D_MODEL // cols,),
                 in_specs=[pl.BlockSpec((PACK_W, cols), lambda i: (0, i)), spec, spec, spec],
                 out_specs=[spec] * 4, out_shape=[shp] * 4, sem=("parallel",))(g_packed, wt, mt, vt)


def adamw_rows(name, g, w, m, v):
    r, c = g.shape
    rows = min(r, BLOCK)

    def body(g_ref, w_ref, m_ref, v_ref, d_ref, mo_ref, vo_ref):
        d_ref[...], mo_ref[...], vo_ref[...] = _adamw(w_ref[...], g_ref[...], m_ref[...], v_ref[...])

    spec = pl.BlockSpec((rows, c), lambda i: (i, 0))
    shp = jax.ShapeDtypeStruct((r, c), F32)
    return _call(body, name=name, grid=(r // rows,), in_specs=[spec] * 4, out_specs=[spec] * 3, out_shape=[shp] * 3,
                 sem=("parallel",))(g, w, m, v)


def adamw_small(gs, ws, ms, vs):
    n = len(gs)

    def body(*refs):
        g, w, m, v = refs[:n], refs[n:2 * n], refs[2 * n:3 * n], refs[3 * n:4 * n]
        outs = refs[4 * n:]
        for i in range(n):
            d, mn, vn = _adamw(w[i][...], g[i][...], m[i][...], v[i][...])
            outs[3 * i][...] = d
            outs[3 * i + 1][...] = mn
            outs[3 * i + 2][...] = vn

    specs = [_full(a.shape) for a in gs]
    res = _call(body, name="adamw_small", in_specs=specs * 4,
                out_specs=[s for s in specs for _ in range(3)],
                out_shape=[jax.ShapeDtypeStruct(a.shape, F32) for a in gs for _ in range(3)])(*gs, *ws, *ms, *vs)
    return [tuple(res[3 * i:3 * i + 3]) for i in range(n)]


def sum_slots(name, r):
    s, rr, c = r.shape
    rows = min(rr, BLOCK)

    def body(r_ref, o_ref):
        acc = r_ref[0].astype(F32)
        for k in range(1, s):
            acc = acc + r_ref[k].astype(F32)
        o_ref[...] = acc

    return _call(body, name=name, grid=(rr // rows,), in_specs=[pl.BlockSpec((s, rows, c), lambda i: (0, i, 0))],
                 out_specs=pl.BlockSpec((rows, c), lambda i: (i, 0)), out_shape=jax.ShapeDtypeStruct((rr, c), F32),
                 sem=("parallel",))(r)


def sum_pair(partial, from_sibling):
    s, _, rr, cols = partial.shape
    rows = rr // 2

    def body(c_ref, p_ref, r_ref, o_ref):
        o_ref[...] = (p_ref[...].astype(F32) + r_ref[...].astype(F32)).astype(BF16)

    core = lax.axis_index("c").astype(jnp.int32).reshape(1)
    return pl.pallas_call(
        body, name="sum_pair", out_shape=jax.ShapeDtypeStruct((s, rr, cols), BF16),
        grid_spec=pltpu.PrefetchScalarGridSpec(
            num_scalar_prefetch=1, grid=(s, rr // rows),
            in_specs=[pl.BlockSpec((None, None, rows, cols), lambda k, i, c: (k, c[0], i, 0)),
                      pl.BlockSpec((None, rows, cols), lambda k, i, c: (k, i, 0))],
            out_specs=pl.BlockSpec((None, rows, cols), lambda k, i, c: (k, i, 0))),
        compiler_params=pltpu.CompilerParams(dimension_semantics=("parallel", "parallel"),
                                             vmem_limit_bytes=VMEM_LIMIT))(core, partial, from_sibling)


def _col_tile(n, k):
    if n % 896 == 0 and k <= 1024:
        return 896
    return min(n, 512)


def project(name, x, wt, after):
    m, k = x.shape
    n = wt.shape[0]
    tn = D_MODEL

    def body(x_ref, w_ref, after_ref, o_ref):
        o_ref[...] = lax.dot_general(x_ref[...], w_ref[...], _NT, preferred_element_type=F32)

    return _call(body, name=name, grid=(n // tn,),
                 in_specs=[_full((m, k)), pl.BlockSpec((tn, k), lambda j: (j, 0)), _full(after.shape)],
                 out_specs=pl.BlockSpec((m, tn), lambda j: (0, j)), out_shape=jax.ShapeDtypeStruct((m, n), F32),
                 sem=("parallel",))(x, wt, after)


def _piece_tiles(pieces, width):
    spans, start = [], 0
    for p in pieces:
        spans.append((start, p.shape[1] // width))
        start += p.shape[1] // width
    return spans, start


def _piece_spec(block, span, rows_of, tile_of):
    first, count = span

    def index(*pos):
        t = tile_of(*pos) - first
        mine = (t >= 0) & (t < count)
        return jnp.where(mine, rows_of(*pos), 0), jnp.clip(t, 0, count - 1)

    return pl.BlockSpec(block, index)


def project_back(name, pieces, wt, after):
    m = pieces[0].shape[0]
    k = wt.shape[1]
    tm = m // 2
    spans, steps = _piece_tiles(pieces, D_MODEL)

    def body(*refs):
        w_ref, o_ref = refs[len(pieces)], refs[len(pieces) + 2]
        j = pl.program_id(1)

        @pl.when(j == 0)
        def _():
            o_ref[...] = jnp.zeros_like(o_ref)

        for dy_ref, (first, count) in zip(refs, spans):
            @pl.when((j >= first) & (j < first + count))
            def _():
                o_ref[...] += jnp.dot(dy_ref[...], w_ref[...], preferred_element_type=F32)

    return _call(body, name=name, grid=(m // tm, steps),
                 in_specs=[_piece_spec((tm, D_MODEL), s, lambda i, j: i, lambda i, j: j) for s in spans]
                 + [pl.BlockSpec((D_MODEL, k), lambda i, j: (j, 0)), _full(after.shape)],
                 out_specs=pl.BlockSpec((tm, k), lambda i, j: (i, 0)), out_shape=jax.ShapeDtypeStruct((m, k), F32),
                 sem=("parallel", "arbitrary"))(*pieces, wt, after)


GRAD_TILE = 512


def _slab_runs():
    runs = [[] for _ in range(N_ACT // GRAD_TILE)]
    for s in range(4):
        for j in range(PACK_W // LANES):
            tile, r = divmod(_act_col((19 * s + j) * LANES), GRAD_TILE)
            last = runs[tile][-1] if runs[tile] else None
            if last and last[2] == s and last[0] + last[1] == r and last[3] + last[1] == j * LANES:
                last[1] += LANES
            else:
                runs[tile].append([r, LANES, s, j * LANES])
    return runs


def weight_grad_t(name, pieces, x):
    m, k = x.shape
    spans, steps = _piece_tiles(pieces, GRAD_TILE)
    runs = _slab_runs()
    most = max(len(r) for r in runs)
    half_rows = PACK_W // 2
    lead = spans[-1][0]

    def tile_at(step):
        return (step + lead) % steps

    landed = {}
    for step in range(steps):
        for _, _, s, _ in runs[tile_at(step)]:
            landed[s] = step + 2

    def body(*refs):
        x_ref, o_ref, land_ref, tile_ref, sems, send_sems, recv_sems = refs[len(pieces):]
        i = pl.program_id(0)
        other = 1 - lax.axis_index("c")

        def copies(step):
            return [pltpu.make_async_copy(tile_ref.at[step % 2, r:r + n], o_ref.at[s, d:d + n],
                                          sems.at[(step % 2) * most + c])
                    for c, (r, n, s, d) in enumerate(runs[tile_at(step)])]

        def to_sibling(s):
            return _remote(o_ref.at[s, pl.ds(other * half_rows, half_rows)], land_ref.at[s], send_sems.at[s],
                           recv_sems.at[s], _peer(1))

        for step in range(2, steps):
            @pl.when(i == step)
            def _():
                for cp in copies(step - 2):
                    cp.wait()

        tile = tile_at(i)
        for dy_ref, (first, count) in zip(refs, spans):
            @pl.when((tile >= first) & (tile < first + count))
            def _():
                tile_ref[i % 2] = lax.dot_general(dy_ref[...], x_ref[...], (((0,), (0,)), ((), ())),
                                                  preferred_element_type=F32).astype(BF16)

        for step in range(steps):
            @pl.when(i == step)
            def _():
                for cp in copies(step):
                    cp.start()
                if step < steps - 1:
                    for s in range(4):
                        if landed[s] == step:
                            to_sibling(s).start()
                else:
                    for cp in copies(step - 1) + copies(step):
                        cp.wait()
                    for s in range(4):
                        if landed[s] >= steps - 1:
                            to_sibling(s).start()
                    for s in range(4):
                        to_sibling(s).wait()

    return _call(body, name=name, grid=(steps,),
                 in_specs=[_piece_spec((m, GRAD_TILE), s, lambda i: 0, tile_at) for s in spans]
                 + [pl.BlockSpec(memory_space=pltpu.VMEM)],
                 out_specs=[_ANY, _ANY],
                 out_shape=[jax.ShapeDtypeStruct((4, PACK_W, k), BF16), jax.ShapeDtypeStruct((4, half_rows, k), BF16)],
                 scratch=[pltpu.VMEM((2, GRAD_TILE, k), BF16), pltpu.SemaphoreType.DMA((2 * most,)),
                          pltpu.SemaphoreType.DMA((4,)), pltpu.SemaphoreType.DMA((4,))],
                 sem=("arbitrary",))(*pieces, x)


def mm_tn(name, x, dy):
    m, k = x.shape
    n = dy.shape[1]
    tn = _col_tile(n, k)

    def body(x_ref, dy_ref, o_ref):
        o_ref[...] = lax.dot_general(x_ref[...], dy_ref[...], (((0,), (0,)), ((), ())),
                                     preferred_element_type=F32).astype(BF16)

    return _call(body, name=name, grid=(n // tn,),
                 in_specs=[_full((m, k)), pl.BlockSpec((m, tn), lambda i: (0, i))],
                 out_specs=pl.BlockSpec((k, tn), lambda i: (0, i)), out_shape=jax.ShapeDtypeStruct((k, n), BF16),
                 sem=("parallel",))(x, dy)


def _row_spec(width, col_block=0):
    return pl.BlockSpec((BLOCK, width), lambda i: (i, col_block))


def _x_spec():
    return pl.BlockSpec((None, BLOCK, D_MODEL), lambda i: (0, jnp.maximum(i - 1, 0), 0))


def prep(name, x, g_pre, gather):
    nb = x.shape[1] // BLOCK + 1
    shards, _, shard_cols = gather.out_shape[-1].shape

    def body(x_ref, g_ref, h_ref, u_ref, meta_ref, sems):
        @pl.when(pl.program_id(0) < nb - 1)
        def _():
            h_ref[...] = x_ref[...]
            u_ref[...] = _rms(x_ref[...], g_ref[...]).astype(BF16)

    def first_block(ins, gathered, outs, scratch):
        g_ref, (h_ref, u_ref), (meta_ref, sems) = ins[1], outs, scratch
        copies = [pltpu.make_async_copy(gathered[-1].at[t], meta_ref.at[t], sems.at[t]) for t in range(shards)]
        for cp in copies:
            cp.start()
        for cp in copies:
            cp.wait()
        h_ref[0:PAD_ROWS, :] = jnp.zeros((PAD_ROWS, D_MODEL), F32)
        h_ref[PAD_ROWS:BLOCK, :] = jnp.concatenate([meta_ref[t] for t in range(shards)], axis=1)
        u_ref[...] = _rms(h_ref[...], g_ref[...]).astype(BF16)

    def at():
        i = pl.program_id(0)
        return i == 0, i == nb // 2, i == nb - 1

    rows = pl.BlockSpec((BLOCK, D_MODEL), lambda i: ((i + 1) % nb, 0))
    return _call_with_comm(
        body, gather, at, (x, g_pre), name=name, grid=(nb,),
        in_specs=[pl.BlockSpec((None, BLOCK, D_MODEL), lambda i: (0, jnp.minimum(i, nb - 2), 0)), _full((1, D_MODEL))],
        out_specs=[rows, rows],
        out_shape=[jax.ShapeDtypeStruct((nb * BLOCK, D_MODEL), F32), jax.ShapeDtypeStruct((nb * BLOCK, D_MODEL), BF16)],
        scratch=[pltpu.VMEM((shards, N_META, shard_cols), F32), pltpu.SemaphoreType.DMA((shards,))],
        after_finish=first_block)


def prep_bwd(h, du, dres, g_pre):
    seq = h.shape[0] - BLOCK
    rows = min(seq, 4 * BLOCK)

    def body(h0_ref, du0_ref, h_ref, du_ref, dres_ref, g_ref, gx_ref, gm_ref, gg_ref):
        i = pl.program_id(0)
        _, vjp = jax.vjp(_rms, h_ref[...], g_ref[...])
        dh, dg = vjp(du_ref[...])
        gx_ref[...] = dh + dres_ref[...]

        @pl.when(i == 0)
        def _():
            _, vjp0 = jax.vjp(_rms, h0_ref[...], g_ref[...])
            dh0, dg0 = vjp0(du0_ref[...])
            gm_ref[...] = dh0[PAD_ROWS:BLOCK, :]
            gg_ref[...] = dg0 + dg

        @pl.when(i > 0)
        def _():
            gg_ref[...] += dg

    seq_rows = pl.BlockSpec((pl.Element(rows), pl.Element(D_MODEL)), lambda i: (pl.multiple_of(BLOCK + rows * i, BLOCK), 0))
    first = pl.BlockSpec((BLOCK, D_MODEL), lambda i: (0, 0))
    return _call(body, name="prep_bwd", grid=(seq // rows,),
                 in_specs=[first, first, seq_rows, seq_rows, seq_rows, _full((1, D_MODEL))],
                 out_specs=[pl.BlockSpec((None, rows, D_MODEL), lambda i: (0, i, 0)), _full((N_META, D_MODEL)),
                            _full((1, D_MODEL))],
                 out_shape=[jax.ShapeDtypeStruct((1, seq, D_MODEL), F32),
                            jax.ShapeDtypeStruct((N_META, D_MODEL), F32), jax.ShapeDtypeStruct((1, D_MODEL), F32)],
                 sem=("arbitrary",))(h, du, h, du, dres, g_pre)


GROUP_W = SSM_INNER // SSM_GROUPS


def _gated_norm(y, z, g):
    t = y * _silu(z)
    return t * lax.rsqrt(jnp.mean(t * t, axis=-1, keepdims=True) + NORM_EPS) * g


def _gated_norm_groups(y, z, g):
    groups = [slice(k * GROUP_W, (k + 1) * GROUP_W) for k in range(SSM_GROUPS)]
    return jnp.concatenate([_gated_norm(y[:, s], z[:, s], g[:, s]) for s in groups], axis=1)


def _merge(ga, gs, ya, ys):
    return _sigmoid(ga) * ya + _sigmoid(gs) * ys


GATE_ATT_BLOCK = SEG["gate_att"][2] // D_MODEL
GATE_SSM_BLOCK = SEG["gate_ssm"][2] // D_MODEL


def _row_loss(out, g_post, x, target):
    diff = x + _rms(out, g_post) - target
    return 0.5 * jnp.sum(diff * diff) / D_MODEL


def tail(y_ssd, proj, a_att, x, target, woa, wos, wo, g_norm, g_post):
    nb = y_ssd.shape[0] // BLOCK
    rows = nb * BLOCK

    def body(y_ref, z_ref, ga_ref, gs_ref, a_ref, x_ref, t_ref, woa_ref, wos_ref, wo_ref, gn_ref, gp_ref,
             yn_ref, mg_ref, dout_ref, dya_ref, dys_ref, da_ref, dy_ref, dz_ref, dga_ref, dgs_ref, dres_ref,
             loss_ref, dgp_ref, dgn_ref):
        i = pl.program_id(0)
        yn, norm_vjp = jax.vjp(_gated_norm_groups, y_ref[...], z_ref[...], gn_ref[...])
        yn16 = yn.astype(BF16)
        y_ssm = jnp.dot(yn16, wos_ref[...], preferred_element_type=F32)
        y_att = jnp.dot(a_ref[...], woa_ref[...], preferred_element_type=F32)
        merged, merge_vjp = jax.vjp(_merge, ga_ref[...], gs_ref[...], y_att, y_ssm)
        merged16 = merged.astype(BF16)
        out = jnp.dot(merged16, wo_ref[...], preferred_element_type=F32)
        loss, loss_vjp = jax.vjp(_row_loss, out, gp_ref[...], x_ref[...], t_ref[...])
        counted = jnp.where(i > 0, 1.0, 0.0)
        dout, dgp, dres, _ = loss_vjp(counted)
        dout16 = dout.astype(BF16)
        dmerged = lax.dot_general(dout16, wo_ref[...], _NT, preferred_element_type=F32)
        dga, dgs, dya, dys = merge_vjp(dmerged)
        dya16, dys16 = dya.astype(BF16), dys.astype(BF16)
        dyn = lax.dot_general(dys16, wos_ref[...], _NT, preferred_element_type=F32)
        dy, dz, dgn = norm_vjp(dyn)

        yn_ref[...] = yn16
        mg_ref[...] = merged16
        dout_ref[...] = dout16
        dya_ref[...] = dya16
        dys_ref[...] = dys16
        da_ref[...] = lax.dot_general(dya16, woa_ref[...], _NT, preferred_element_type=F32)
        dy_ref[...] = dy
        dz_ref[...] = dz.astype(BF16)
        dga_ref[...] = dga.astype(BF16)
        dgs_ref[...] = dgs.astype(BF16)
        dres_ref[...] = dres

        @pl.when(i == 0)
        def _():
            loss_ref[...] = jnp.zeros_like(loss_ref)
            dgp_ref[...] = jnp.zeros_like(dgp_ref)
            dgn_ref[...] = jnp.zeros_like(dgn_ref)

        loss_ref[...] += loss * counted
        dgp_ref[...] += dgp
        dgn_ref[...] += dgn

    wide, narrow = _row_spec(SSM_INNER), _row_spec(D_MODEL)
    resident = pl.BlockSpec(memory_space=pltpu.VMEM)
    bf = lambda w: jax.ShapeDtypeStruct((rows, w), BF16)
    f32 = lambda w: jax.ShapeDtypeStruct((rows, w), F32)
    return _call(body, name="tail", grid=(nb,),
                 in_specs=[wide, wide, _row_spec(D_MODEL, GATE_ATT_BLOCK), _row_spec(D_MODEL, GATE_SSM_BLOCK), narrow,
                           _x_spec(), _x_spec(), resident, resident, resident, _full((1, SSM_INNER)),
                           _full((1, D_MODEL))],
                 out_specs=[wide, narrow, narrow, narrow, narrow, narrow, wide, wide, narrow, narrow, narrow,
                            _full((8, LANES)), _full((1, D_MODEL)), _full((1, SSM_INNER))],
                 out_shape=[bf(SSM_INNER), bf(D_MODEL), bf(D_MODEL), bf(D_MODEL), bf(D_MODEL), f32(D_MODEL),
                            f32(SSM_INNER), bf(SSM_INNER), bf(D_MODEL), bf(D_MODEL), f32(D_MODEL),
                            jax.ShapeDtypeStruct((8, LANES), F32), jax.ShapeDtypeStruct((1, D_MODEL), F32),
                            jax.ShapeDtypeStruct((1, SSM_INNER), F32)],
                 sem=("arbitrary",))(y_ssd, proj, proj, proj, a_att, x, target, woa, wos, wo, g_norm, g_post)


_NT = (((1,), (1,)), ((), ()))
ALIBI_SLOPES = tuple(2.0 ** (-8.0 * (h + 1) / ATT_Q_HEADS) for h in range(ATT_Q_HEADS))
KV_WIDTH = ATT_KV_HEADS * HEAD_DIM
Q_BLOCK = SEG["q"][2] // D_MODEL
Z_ATT_BLOCK = SEG["z_att"][2] // D_MODEL
K_BLOCK = SEG["k"][2] // KV_WIDTH
V_BLOCK = SEG["v"][2] // KV_WIDTH
META_ROW_BLOCK = PAD_ROWS // N_META


@jax.custom_vjp
def _swap_halves(x):
    return pltpu.roll(x, HEAD_DIM, 1)


_swap_halves.defvjp(lambda x: (pltpu.roll(x, HEAD_DIM, 1), None), lambda _, g: (pltpu.roll(g, HEAD_DIM, 1),))


def _both_halves(t, half):
    first = lax.broadcasted_iota(jnp.int32, t.shape, 1) < HEAD_DIM
    sw = _swap_halves(t)
    return jnp.where(first, t, sw) if half == 0 else jnp.where(first, sw, t)


def _attn_rows(q, z, kp, kc, vp, vc, km, vm, sinks, n):
    rows = ATT_GROUP * BLOCK
    i = lax.broadcasted_iota(jnp.int32, (rows, BLOCK), 0) & (BLOCK - 1)
    j = lax.broadcasted_iota(jnp.int32, (rows, BLOCK), 1)
    rel_c = (i - j).astype(F32)
    rel_p = rel_c + float(BLOCK)
    nv = jnp.zeros((rows, BLOCK), jnp.int32) + n
    ok_c = (i >= j) & (nv >= 1)
    ok_p = (j > i) & (nv >= 2)
    im = lax.broadcasted_iota(jnp.int32, (rows, N_META), 0) & (BLOCK - 1)
    jm = lax.broadcasted_iota(jnp.int32, (rows, N_META), 1)
    ok_m = ((jnp.zeros((rows, N_META), jnp.int32) + n) >= 1) | (im >= PAD_ROWS + jm)
    first = lax.broadcasted_iota(jnp.int32, (BLOCK, LANES), 1) < HEAD_DIM
    neg = -jnp.inf
    outs = []
    for kv in range(ATT_KV_HEADS):
        tile, half = divmod(kv, 2)
        lanes = slice(tile * LANES, (tile + 1) * LANES)
        kc2, kp2, km2 = (_both_halves(t[:, lanes], half).astype(BF16) for t in (kc, kp, km))
        vc2, vp2, vm2 = (_both_halves(t[:, lanes], half).astype(BF16) for t in (vc, vp, vm))
        qs, slope, sk = [], [], []
        for pair in range(ATT_GROUP // 2):
            c0 = (kv * ATT_GROUP + 2 * pair) * HEAD_DIM
            qp = q[:, c0:c0 + LANES] * HEAD_DIM ** -0.5
            qs += [jnp.where(first, qp, 0.0), jnp.where(first, 0.0, qp)]
        for g in range(ATT_GROUP):
            slope.append(jnp.full((BLOCK, 1), ALIBI_SLOPES[kv * ATT_GROUP + g], F32))
            sk.append(jnp.broadcast_to(sinks[kv * ATT_GROUP + g], (BLOCK, 1)))
        qs = jnp.concatenate(qs, axis=0).astype(BF16)
        slope = jnp.concatenate(slope, axis=0)
        sk = jnp.concatenate(sk, axis=0)
        sc = jnp.where(ok_c, lax.dot_general(qs, kc2, _NT, preferred_element_type=F32) - slope * rel_c, neg)
        sp = jnp.where(ok_p, lax.dot_general(qs, kp2, _NT, preferred_element_type=F32) - slope * rel_p, neg)
        sm = jnp.where(ok_m, lax.dot_general(qs, km2, _NT, preferred_element_type=F32), neg)
        mx = jnp.maximum(jnp.maximum(jnp.max(sc, axis=1, keepdims=True), jnp.max(sp, axis=1, keepdims=True)),
                         jnp.maximum(jnp.max(sm, axis=1, keepdims=True), sk))
        mx = lax.stop_gradient(mx)
        ec, ep, em, es = jnp.exp(sc - mx), jnp.exp(sp - mx), jnp.exp(sm - mx), jnp.exp(sk - mx)
        den = (es + jnp.sum(ec, axis=1, keepdims=True) + jnp.sum(ep, axis=1, keepdims=True)
               + jnp.sum(em, axis=1, keepdims=True))
        inv = 1.0 / den
        o = (jnp.dot((ec * inv).astype(BF16), vc2, preferred_element_type=F32)
             + jnp.dot((ep * inv).astype(BF16), vp2, preferred_element_type=F32)
             + jnp.dot((em * inv).astype(BF16), vm2, preferred_element_type=F32))
        for pair in range(ATT_GROUP // 2):
            r0 = 2 * pair * BLOCK
            outs.append(jnp.where(first, o[r0:r0 + BLOCK], o[r0 + BLOCK:r0 + 2 * BLOCK]))
    return jnp.concatenate(outs, axis=1) * _silu(z)


def _attn_specs(nb, steps_clamped):
    def blk(t):
        return jnp.minimum(t, nb - 1) if steps_clamped else t

    wide = lambda col: pl.BlockSpec((BLOCK, D_MODEL), lambda t: (blk(t), col))
    cur = lambda col: pl.BlockSpec((BLOCK, KV_WIDTH), lambda t: (blk(t), col))
    prev = lambda col: pl.BlockSpec((BLOCK, KV_WIDTH), lambda t: (jnp.maximum(blk(t) - 1, 0), col))
    meta = lambda col: pl.BlockSpec((N_META, KV_WIDTH), lambda t: (META_ROW_BLOCK, col))
    sinks = pl.BlockSpec((ATT_Q_HEADS, 1, 1), lambda t: (0, 0, 0))
    return [wide(Q_BLOCK), wide(Z_ATT_BLOCK), prev(K_BLOCK), cur(K_BLOCK), prev(V_BLOCK), cur(V_BLOCK),
            meta(K_BLOCK), meta(V_BLOCK), sinks]


def attn_fwd(proj, sinks):
    nb = proj.shape[0] // BLOCK

    def body(q_ref, z_ref, kp_ref, kc_ref, vp_ref, vc_ref, km_ref, vm_ref, sk_ref, o_ref):
        o_ref[...] = _attn_rows(q_ref[...], z_ref[...], kp_ref[...], kc_ref[...], vp_ref[...], vc_ref[...],
                                km_ref[...], vm_ref[...], tuple(sk_ref[h] for h in range(ATT_Q_HEADS)),
                                pl.program_id(0)).astype(BF16)

    return _call(body, name="attn_fwd", grid=(nb,), in_specs=_attn_specs(nb, False), out_specs=_row_spec(D_MODEL),
                 out_shape=jax.ShapeDtypeStruct((nb * BLOCK, D_MODEL), BF16), sem=("parallel",))(*([proj] * 8), sinks)


def attn_bwd(da, proj, sinks):
    nb = proj.shape[0] // BLOCK
    last = nb - 1
    wide = pl.BlockSpec((BLOCK, D_MODEL), lambda t: (jnp.minimum(t, last), 0))
    done = pl.BlockSpec((BLOCK, KV_WIDTH), lambda t: (jnp.maximum(t - 1, 0), 0))
    meta = _full((N_META, KV_WIDTH))
    par = _full((ATT_Q_HEADS, 1, 1))

    def body(da_ref, q_ref, z_ref, kp_ref, kc_ref, vp_ref, vc_ref, km_ref, vm_ref, sk_ref,
             dq_ref, dz_ref, dk_ref, dv_ref, dkm_ref, dvm_ref, dsk_ref, ck_ref, cv_ref):
        t = pl.program_id(0)

        @pl.when(t == 0)
        def _():
            ck_ref[...] = jnp.zeros_like(ck_ref)
            cv_ref[...] = jnp.zeros_like(cv_ref)
            dkm_ref[...] = jnp.zeros_like(dkm_ref)
            dvm_ref[...] = jnp.zeros_like(dvm_ref)
            dsk_ref[...] = jnp.zeros_like(dsk_ref)

        @pl.when(t < nb)
        def _():
            def f(q, z, kp, kc, vp, vc, km, vm, sk):
                return _attn_rows(q, z, kp, kc, vp, vc, km, vm, sk, t)

            _, vjp = jax.vjp(f, q_ref[...], z_ref[...], kp_ref[...], kc_ref[...], vp_ref[...], vc_ref[...],
                             km_ref[...], vm_ref[...], tuple(sk_ref[h] for h in range(ATT_Q_HEADS)))
            dq, dz, dkp, dkc, dvp, dvc, dkm, dvm, dsk = vjp(da_ref[...])
            dq_ref[...] = dq.astype(BF16)
            dz_ref[...] = dz.astype(BF16)
            for h in range(ATT_Q_HEADS):
                dsk_ref[h] += dsk[h]
            dk_ref[...] = ck_ref[...] + dkp
            dv_ref[...] = cv_ref[...] + dvp
            ck_ref[...] = dkc
            cv_ref[...] = dvc
            dkm_ref[...] += dkm
            dvm_ref[...] += dvm

        @pl.when(t == nb)
        def _():
            dk_ref[...] = ck_ref[...]
            dv_ref[...] = cv_ref[...]

    rows = nb * BLOCK
    return _call(body, name="attn_bwd", grid=(nb + 1,), in_specs=[wide] + _attn_specs(nb, True),
                 out_specs=[wide, wide, done, done, meta, meta, par],
                 out_shape=[jax.ShapeDtypeStruct((rows, D_MODEL), BF16), jax.ShapeDtypeStruct((rows, D_MODEL), BF16),
                            jax.ShapeDtypeStruct((rows, KV_WIDTH), F32), jax.ShapeDtypeStruct((rows, KV_WIDTH), F32),
                            jax.ShapeDtypeStruct((N_META, KV_WIDTH), F32), jax.ShapeDtypeStruct((N_META, KV_WIDTH), F32),
                            jax.ShapeDtypeStruct(sinks.shape, F32)],
                 scratch=[pltpu.VMEM((BLOCK, KV_WIDTH), F32), pltpu.VMEM((BLOCK, KV_WIDTH), F32)],
                 sem=("arbitrary",))(da, *([proj] * 8), sinks)


XBC_BLOCK0 = SEG["xbc"][2] // D_MODEL
CONV_COL_BLOCKS = CONV_DIM // D_MODEL
DT_TILE = SEG["dt"][2] // LANES


HALO = 8


def _conv_rows(length):
    return 544 if length % 544 == 0 else BLOCK


def _shift_rows(cur, before, j):
    if j == 0:
        return cur
    n = cur.shape[0]
    row = lax.broadcasted_iota(jnp.int32, cur.shape, 0)
    head = pltpu.roll(before, j, 0)
    if n > HALO:
        head = jnp.concatenate([head, jnp.zeros((n - HALO, cur.shape[1]), cur.dtype)], axis=0)
    return jnp.where(row >= j, pltpu.roll(cur, j, 0), head)


def _conv_pre(cur, before, w_ref, b_ref):
    pre = b_ref[...] + w_ref[CONV_WIDTH - 1:CONV_WIDTH, :] * cur
    for k in range(CONV_WIDTH - 1):
        pre = pre + w_ref[k:k + 1, :] * _shift_rows(cur, before, CONV_WIDTH - 1 - k)
    return pre


def _conv_specs(steps, rows, col0=0):
    first = XBC_BLOCK0 + col0
    halos = rows // HALO
    cur = pl.BlockSpec((rows, D_MODEL), lambda j, i: (i, first + j))
    before = pl.BlockSpec((HALO, D_MODEL), lambda j, i: (jnp.maximum(i * halos - 1, 0), first + j))
    after = pl.BlockSpec((HALO, D_MODEL), lambda j, i: (jnp.minimum(i + 1, steps - 1) * halos, first + j))
    return cur, before, after


def _valid_rows(i, rows):
    row = lax.broadcasted_iota(jnp.int32, (rows, D_MODEL), 0)
    return jnp.maximum((row >= PAD_ROWS).astype(F32), jnp.where(i > 0, 1.0, 0.0))


def conv_fwd(proj, conv_w, conv_b):
    rows = _conv_rows(proj.shape[0])
    steps = proj.shape[0] // rows
    cur, before, _ = _conv_specs(steps, rows)

    def body(c_ref, p_ref, w_ref, b_ref, o_ref):
        i = pl.program_id(1)
        pre = _conv_pre(c_ref[...], p_ref[...] * jnp.where(i > 0, 1.0, 0.0), w_ref, b_ref)
        o_ref[...] = _silu(pre) * _valid_rows(i, rows)

    return _call(body, name="conv_fwd", grid=(CONV_COL_BLOCKS, steps),
                 in_specs=[cur, before, pl.BlockSpec((CONV_WIDTH, D_MODEL), lambda j, i: (0, j)),
                           pl.BlockSpec((1, D_MODEL), lambda j, i: (0, j))],
                 out_specs=pl.BlockSpec((rows, D_MODEL), lambda j, i: (i, j)),
                 out_shape=jax.ShapeDtypeStruct((proj.shape[0], CONV_DIM), F32),
                 sem=("parallel", "parallel"))(proj, proj, conv_w, conv_b)


def conv_bwd(name, dparts, col0, proj, conv_w, conv_b):
    rows = _conv_rows(proj.shape[0])
    steps = proj.shape[0] // rows
    last = steps - 1
    ncol = sum(d.shape[1] for d in dparts) // D_MODEL
    np_ = len(dparts)
    cur, before, after = _conv_specs(steps, rows, col0)
    dcur = [pl.BlockSpec((rows, d.shape[1] // ncol), lambda j, i: (i, j)) for d in dparts]
    dafter = [pl.BlockSpec((HALO, d.shape[1] // ncol), lambda j, i: (jnp.minimum(i + 1, last) * (rows // HALO), j))
              for d in dparts]
    out_cur = pl.BlockSpec((rows, D_MODEL), lambda j, i: (i, j))
    wspec = pl.BlockSpec((CONV_WIDTH, D_MODEL), lambda j, i: (0, col0 + j))
    bspec = pl.BlockSpec((1, D_MODEL), lambda j, i: (0, col0 + j))
    wout = pl.BlockSpec((CONV_WIDTH, D_MODEL), lambda j, i: (0, j))
    bout = pl.BlockSpec((1, D_MODEL), lambda j, i: (0, j))

    def body(*refs):
        dc_refs, da_refs = refs[:np_], refs[np_:2 * np_]
        c_ref, p_ref, a_ref, w_ref, b_ref, du_ref, dw_ref, db_ref = refs[2 * np_:]
        i = pl.program_id(1)
        row = lax.broadcasted_iota(jnp.int32, (rows, D_MODEL), 0)
        curv = c_ref[...]
        beforev = p_ref[...] * jnp.where(i > 0, 1.0, 0.0)
        side_by_side = lambda rs: rs[0][...] if np_ == 1 else jnp.concatenate([r[...] for r in rs], axis=1)

        def dpre_of(pre, d):
            s = _sigmoid(pre)
            return d * (s * (1.0 + pre * (1.0 - s)))

        dp_c = dpre_of(_conv_pre(curv, beforev, w_ref, b_ref), side_by_side(dc_refs) * _valid_rows(i, rows))
        dp_a = dpre_of(_conv_pre(a_ref[...], curv[rows - HALO:], w_ref, b_ref),
                       side_by_side(da_refs) * jnp.where(i < last, 1.0, 0.0))
        du = w_ref[CONV_WIDTH - 1:CONV_WIDTH, :] * dp_c
        for j in range(1, CONV_WIDTH):
            tail = jnp.concatenate([jnp.zeros((rows - HALO, D_MODEL), F32), pltpu.roll(dp_a, HALO - j, 0)], axis=0)
            up = jnp.where(row < rows - j, pltpu.roll(dp_c, rows - j, 0), tail)
            du = du + w_ref[CONV_WIDTH - 1 - j:CONV_WIDTH - j, :] * up
        du_ref[...] = du.astype(BF16)

        @pl.when(i == 0)
        def _():
            dw_ref[...] = jnp.zeros_like(dw_ref)
            db_ref[...] = jnp.zeros_like(db_ref)

        for k in range(CONV_WIDTH):
            dw_ref[k:k + 1, :] += jnp.sum(dp_c * _shift_rows(curv, beforev, CONV_WIDTH - 1 - k), axis=0, keepdims=True)
        db_ref[...] += jnp.sum(dp_c, axis=0, keepdims=True)

    width = ncol * D_MODEL
    return _call(body, name=name, grid=(ncol, steps),
                 in_specs=dcur + dafter + [cur, before, after, wspec, bspec], out_specs=[out_cur, wout, bout],
                 out_shape=[jax.ShapeDtypeStruct((proj.shape[0], width), BF16),
                            jax.ShapeDtypeStruct((CONV_WIDTH, width), F32), jax.ShapeDtypeStruct((1, width), F32)],
                 sem=("parallel", "arbitrary"))(*dparts, *dparts, proj, proj, proj, conv_w, conv_b)


def _head_expand():
    e = np.zeros((LANES, SSM_INNER), np.float32)
    for h in range(SSM_HEADS):
        e[h, h * HEAD_DIM:(h + 1) * HEAD_DIM] = 1.0
    return jnp.asarray(e, dtype=BF16)


def _softplus(x):
    return jnp.maximum(x, 0.0) + jnp.log(1.0 + jnp.exp(-jnp.abs(x)))


def _bf16_parts(x):
    hi = x.astype(BF16)
    rest = x - hi.astype(F32)
    mid = rest.astype(BF16)
    return hi, mid, (rest - mid.astype(F32)).astype(BF16)


@jax.custom_vjp
def _times_01(x, m):
    return sum(jnp.dot(p, m, preferred_element_type=F32) for p in _bf16_parts(x))


def _times_01_bwd(m, g):
    return sum(lax.dot_general(p, m, _NT, preferred_element_type=F32) for p in _bf16_parts(g)), jnp.zeros_like(m)


_times_01.defvjp(lambda x, m: (_times_01(x, m), m), _times_01_bwd)


def _causal_ones():
    l = lax.broadcasted_iota(jnp.int32, (BLOCK, BLOCK), 0)
    s = lax.broadcasted_iota(jnp.int32, (BLOCK, BLOCK), 1)
    return (l >= s).astype(BF16)


@jax.custom_vjp
def _cumsum_rows(a):
    return sum(jnp.dot(_causal_ones(), p, preferred_element_type=F32) for p in _bf16_parts(a))


def _cumsum_rows_bwd(_, g):
    tn = (((0,), (0,)), ((), ()))
    return (sum(lax.dot_general(_causal_ones(), p, tn, preferred_element_type=F32) for p in _bf16_parts(g)),)


_cumsum_rows.defvjp(lambda a: (_cumsum_rows(a), None), _cumsum_rows_bwd)


def _ssd_heads(dt_tile, bias, alog, dsk, expand):
    dt = _softplus(dt_tile + bias)
    a = dt * (-jnp.exp(alog))
    one_row = lambda v: jnp.broadcast_to(v, (HALO, LANES))
    return _times_01(jnp.concatenate([dt, _cumsum_rows(a), one_row(jnp.sum(a, axis=0, keepdims=True)), one_row(dsk)],
                                     axis=0), expand)


HEADS_ROWS = 2 * BLOCK + 2 * HALO


def _ssd_group(xs, per_lane, bg, cg, state):
    l = lax.broadcasted_iota(jnp.int32, (BLOCK, BLOCK), 0)
    s = lax.broadcasted_iota(jnp.int32, (BLOCK, BLOCK), 1)
    causal = l >= s
    first_head = s < HEAD_DIM
    dtx, cs = per_lane[0:BLOCK], per_lane[BLOCK:2 * BLOCK]
    tot, dsk = per_lane[2 * BLOCK:2 * BLOCK + 1], per_lane[2 * BLOCK + HALO:2 * BLOCK + HALO + 1]
    bb, cb16 = bg.astype(BF16), cg.astype(BF16)
    cb = lax.dot_general(cb16, bb, _NT, preferred_element_type=F32)
    xr = xs * dtx
    y_diag = []
    for p in range(GROUP_W // LANES):
        lanes = slice(p * LANES, (p + 1) * LANES)
        c_pair = cs[:, lanes]
        c_swap = _swap_halves(c_pair)
        m = []
        for c_head in (jnp.where(first_head, c_pair, c_swap), jnp.where(first_head, c_swap, c_pair)):
            m.append(cb * jnp.exp(jnp.where(causal, c_head - c_head.T, -jnp.inf)))
        x_pair = xr[:, lanes]
        x_diag = jnp.concatenate([jnp.where(first_head, x_pair, 0.0), jnp.where(first_head, 0.0, x_pair)], axis=0)
        y_diag.append(jnp.dot(jnp.concatenate(m, axis=1).astype(BF16), x_diag.astype(BF16),
                              preferred_element_type=F32))
    st = lax.dot_general(bb, (xr * jnp.exp(tot - cs)).astype(BF16), (((0,), (0,)), ((), ())),
                         preferred_element_type=F32)
    new_state = state * jnp.exp(tot) + st
    y_off = jnp.dot(cb16, state.astype(BF16), preferred_element_type=F32) * jnp.exp(cs)
    return jnp.concatenate(y_diag, axis=1) + y_off + dsk * xs, new_state


BC_WIDTH = SSM_GROUPS * SSM_STATE


def _ssd_specs(chunk):
    xs = pl.BlockSpec((BLOCK, SSM_INNER), lambda c: (chunk(c), 0))
    dt = pl.BlockSpec((BLOCK, LANES), lambda c: (chunk(c), DT_TILE))
    expand = _full((LANES, SSM_INNER))
    b = pl.BlockSpec((BLOCK, BC_WIDTH), lambda c: (chunk(c), SSM_INNER // BC_WIDTH))
    cc = pl.BlockSpec((BLOCK, BC_WIDTH), lambda c: (chunk(c), SSM_INNER // BC_WIDTH + 1))
    par = _full((1, LANES))
    state = pl.BlockSpec((None, SSM_STATE, SSM_INNER), lambda c: (chunk(c), 0, 0))
    return xs, dt, expand, b, cc, par, state


def _group_lanes(g):
    return slice(g * GROUP_W, (g + 1) * GROUP_W), slice(g * SSM_STATE, (g + 1) * SSM_STATE)


def ssd_fwd(xbc, proj, expand, bias, alog, dsk):
    nb = xbc.shape[0] // BLOCK
    xs, dt, ex, b, cc, par, state = _ssd_specs(lambda c: c)

    def body(x_ref, dt_ref, e_ref, bi_ref, al_ref, dk_ref, b_ref, c_ref, y_ref, sp_ref, st_ref):
        @pl.when(pl.program_id(0) == 0)
        def _():
            st_ref[...] = jnp.zeros_like(st_ref)

        per_lane = _ssd_heads(dt_ref[...], bi_ref[...], al_ref[...], dk_ref[...], e_ref[...])
        for g in range(SSM_GROUPS):
            wide, tile = _group_lanes(g)
            entering = st_ref[:, wide]
            sp_ref[:, wide] = entering
            y_ref[:, wide], st_ref[:, wide] = _ssd_group(x_ref[:, wide], per_lane[:, wide], b_ref[:, tile],
                                                         c_ref[:, tile], entering)

    return _call(body, name="ssd_fwd", grid=(nb,), in_specs=[xs, dt, ex, par, par, par, b, cc],
                 out_specs=[xs, state],
                 out_shape=[jax.ShapeDtypeStruct((nb * BLOCK, SSM_INNER), F32),
                            jax.ShapeDtypeStruct((nb, SSM_STATE, SSM_INNER), F32)],
                 scratch=[pltpu.VMEM((SSM_STATE, SSM_INNER), F32)],
                 sem=("arbitrary",))(xbc, proj, expand, bias, alog, dsk, xbc, xbc)


def ssd_bwd(dy, xbc, proj, expand, bias, alog, dsk, states, comm):
    nb = xbc.shape[0] // BLOCK
    last = nb - 1
    xs, dt, ex, b, cc, par, state = _ssd_specs(lambda c: last - c)
    tile = pl.BlockSpec((BLOCK, LANES), lambda c: (last - c, 0))
    nspec = pl.BlockSpec((BLOCK, BC_WIDTH), lambda c: (last - c, 0))

    def body(dy_ref, x_ref, dt_ref, e_ref, bi_ref, al_ref, dk_ref, b_ref, c_ref, sp_ref,
             dx_ref, ddt_ref, db_ref, dc_ref, dbi_ref, dal_ref, ddk_ref, ds_ref):
        @pl.when(pl.program_id(0) == 0)
        def _():
            ds_ref[...] = jnp.zeros_like(ds_ref)
            dbi_ref[...] = jnp.zeros_like(dbi_ref)
            dal_ref[...] = jnp.zeros_like(dal_ref)
            ddk_ref[...] = jnp.zeros_like(ddk_ref)

        expand = e_ref[...]
        per_lane, heads_vjp = jax.vjp(lambda t, bi, al, dk: _ssd_heads(t, bi, al, dk, expand), dt_ref[...], bi_ref[...],
                                      al_ref[...], dk_ref[...])
        d_per_lane = []
        for g in range(SSM_GROUPS):
            wide, tile_lanes = _group_lanes(g)
            _, vjp = jax.vjp(_ssd_group, x_ref[:, wide], per_lane[:, wide], b_ref[:, tile_lanes], c_ref[:, tile_lanes],
                             sp_ref[:, wide])
            (dx_ref[:, wide], d_lanes, db_ref[:, tile_lanes], dc_ref[:, tile_lanes],
             ds_ref[:, wide]) = vjp((dy_ref[:, wide], ds_ref[:, wide]))
            d_per_lane.append(d_lanes)
        ddt, dbi, dal, ddk = heads_vjp(jnp.concatenate(d_per_lane, axis=1))
        ddt_ref[...] = ddt.astype(BF16)
        dbi_ref[...] += dbi
        dal_ref[...] += dal
        ddk_ref[...] += ddk

    def at():
        c = pl.program_id(0)
        return c == 0, c == 0, c == last

    par_shape = jax.ShapeDtypeStruct((1, LANES), F32)
    return _call_with_comm(
        body, comm, at, (dy, xbc, proj, expand, bias, alog, dsk, xbc, xbc, states), name="ssd_bwd",
        grid=(nb,), in_specs=[xs, xs, dt, ex, par, par, par, b, cc, state],
        out_specs=[xs, tile, nspec, nspec, par, par, par],
        out_shape=[jax.ShapeDtypeStruct((nb * BLOCK, SSM_INNER), F32), jax.ShapeDtypeStruct((nb * BLOCK, LANES), BF16),
                   jax.ShapeDtypeStruct((nb * BLOCK, BC_WIDTH), F32), jax.ShapeDtypeStruct((nb * BLOCK, BC_WIDTH), F32),
                   par_shape, par_shape, par_shape],
        scratch=[pltpu.VMEM((SSM_STATE, SSM_INNER), F32)])


SLAB_ROWS = 16
SLAB_META_ROW = 8
SLAB_META_SHAPE = (8, 2 * D_MODEL)
SLAB_LOSS_ROW = 7


def pack_small(dcw, dcb, dgpre, dgpost, dbias, dalog, ddsk, dsinks, dgn, dmeta, loss_tile):
    def body(cw, cb, gpre, gpost, dtb, al, dk, sk, gn, meta, loss, o_ref):
        o_ref[...] = jnp.zeros_like(o_ref)
        o_ref[SLAB_LOSS_ROW:SLAB_LOSS_ROW + 1, 0:LANES] = loss[0:1, :]
        o_ref[0:CONV_WIDTH, :] = cw[...]
        o_ref[4:5, :] = cb[...]
        o_ref[5:6, 0:1024] = gpre[...]
        o_ref[5:6, 1024:2048] = gpost[...]
        o_ref[5:6, 2048:2176] = dtb[...]
        o_ref[5:6, 2176:2304] = al[...]
        o_ref[5:6, 2304:2432] = dk[...]
        o_ref[5:6, 2432:2560] = sk[...]
        o_ref[6:7, 0:SSM_INNER] = gn[...]
        o_ref[SLAB_META_ROW:SLAB_ROWS, 0:SLAB_META_SHAPE[1]] = meta[...]

    args = (dcw, dcb, dgpre, dgpost, dbias, dalog, ddsk, dsinks, dgn, dmeta, loss_tile)
    return _call(body, name="pack_small", in_specs=[_full(a.shape) for a in args],
                 out_specs=_full((SLAB_ROWS, CONV_DIM)), out_shape=jax.ShapeDtypeStruct((SLAB_ROWS, CONV_DIM), F32))(*args)


def _lane_tile(v):
    return jnp.pad(v, ((0, 0), (0, LANES - v.shape[1])))


def kernel(x, meta_tokens, g_pre, w_in, conv_w, conv_b, dt_bias, a_log, d_skip, attn_sinks, g_ssm_norm, w_out_att, w_out_ssm, w_out, g_post, loss_target, m_meta_tokens, m_g_pre, m_w_in, m_conv_w, m_conv_b, m_dt_bias, m_a_log, m_d_skip, m_attn_sinks, m_g_ssm_norm, m_w_out_att, m_w_out_ssm, m_w_out, m_g_post, v_meta_tokens, v_g_pre, v_w_in, v_conv_w, v_conv_b, v_dt_bias, v_a_log, v_d_skip, v_attn_sinks, v_g_ssm_norm, v_w_out_att, v_w_out_ssm, v_w_out, v_g_post):
    chip = _chip_index()

    conv_w_rows = jnp.pad(conv_w[0], ((0, 2 * 8 - CONV_WIDTH), (0, 0)))
    w_in_t, m_w_in_t, v_w_in_t = w_in[0].T, m_w_in[0].T, v_w_in[0].T
    (h, u), (gathered_w_in, g_conv_w, g_meta) = prep(
        "gather_w_in", x, g_pre, TwoLevelGather([pack_w_in(w_in_t), conv_w_rows, meta_tokens]))
    w_all_t = unpack_w_in(gathered_w_in)
    cw_full = g_conv_w[:, :CONV_WIDTH].transpose(1, 0, 2).reshape(CONV_WIDTH, CONV_DIM)
    behind_w_in = 0.0 * g_meta[0, 0, 0]
    w_out_flight, w_out_started = chip_exchange_start(
        "gather_w_out_start", [(w[0] + behind_w_in).astype(BF16) for w in (w_out_att, w_out_ssm, w_out)], False)

    proj = project("in_proj", u, w_all_t, w_out_started)

    sinks3 = attn_sinks.reshape(ATT_Q_HEADS, 1, 1)
    a_att = attn_fwd(proj, sinks3)

    xbc = conv_fwd(proj, cw_full, conv_b)
    expand = _head_expand()
    head_pars = (_lane_tile(dt_bias), _lane_tile(a_log), _lane_tile(d_skip))
    y_ssd, states = ssd_fwd(xbc, proj, expand, *head_pars)
    woa, wos, wo = [g.reshape(-1, D_MODEL) for g in chip_exchange_wait("gather_w_out_wait", w_out_flight, False, y_ssd)]

    (yn, merged, dout, dy_att, dy_ssm, da_att, dy_ssd, dz_ssm, dga, dgs, dres, loss_tile, dg_post, dgn) = tail(
        y_ssd, proj, a_att, x, loss_target, woa, wos, wo, g_ssm_norm, g_post)

    dwo = mm_tn("out_proj_dw", merged, dout)
    dwoa = mm_tn("att_out_dw", a_att, dy_att)
    dwos = mm_tn("ssm_out_dw", yn, dy_ssm)
    dq, dz_att, dk, dv, dkmeta, dvmeta, dsinks3 = attn_bwd(da_att, proj, sinks3)
    dk = dk.at[PAD_ROWS:BLOCK].add(dkmeta).astype(BF16)
    dv = dv.at[PAD_ROWS:BLOCK].add(dvmeta).astype(BF16)

    def pieces(g):
        return g.reshape(4, 2, g.shape[0] // 8, g.shape[1])

    def to_owner(g):
        return (lambda ref, dev: ref.at[_chip_of(dev), dev[2]], (g.shape[0] // 8, g.shape[1]))

    (dxs, ddt_tile, dbg, dcg, dbias, dalog, ddsk), sent_w_out = ssd_bwd(
        dy_ssd, xbc, proj, expand, *head_pars, states,
        DirectExchange([pieces(dwoa), pieces(dwos), pieces(dwo)], [to_owner(dwoa), to_owner(dwos), to_owner(dwo)],
                       ALL_MASKS, "dev", 8))
    dxs_raw, dcw_xs, dcb_xs = conv_bwd("conv_bwd_x", [dxs], 0, proj, cw_full, conv_b)
    dbc_raw, dcw_bc, dcb_bc = conv_bwd("conv_bwd_bc", [dbg, dcg], SSM_INNER // D_MODEL, proj, cw_full, conv_b)
    dcw = jnp.concatenate([dcw_xs, dcw_bc], axis=1)
    dcb = jnp.concatenate([dcb_xs, dcb_bc], axis=1)

    narrow = jnp.concatenate([dk, dv, ddt_tile, jnp.zeros((dk.shape[0], N_ACT - N_ALIGNED), BF16)], axis=1)
    dproj = [dz_ssm, dxs_raw, dbc_raw, dq, dz_att, dga, dgs, narrow]
    partial, from_sibling = weight_grad_t("in_proj_dw", dproj, u)
    chip_sum = sum_pair(partial.reshape(4, 2, PACK_W // 2, D_MODEL), from_sibling)
    grads_flight, started = chip_exchange_start("reduce_w_in_start", [chip_sum], True)
    du = project_back("in_proj_dx", dproj, w_all_t, started)
    grad_x, dmeta, dg_pre = prep_bwd(h, du, dres, g_pre)

    halves = [sum_slots("sum_" + nm, r) for nm, r in zip(("w_out_att", "w_out_ssm", "w_out"), sent_w_out)]
    shared = run_comm("share_w_out", DirectExchange(halves, [None] * 3, SIBLING_MASK, "core", 2))
    g_woa, g_wos, g_wo = [f.reshape(2 * f.shape[1], f.shape[2]) for f in shared]
    d_woa, nm_woa, nv_woa = adamw_rows("adamw_w_out_att", g_woa, w_out_att[0], m_w_out_att[0], v_w_out_att[0])
    d_wos, nm_wos, nv_wos = adamw_rows("adamw_w_out_ssm", g_wos, w_out_ssm[0], m_w_out_ssm[0], v_w_out_ssm[0])
    d_wo, nm_wo, nv_wo = adamw_rows("adamw_w_out", g_wo, w_out[0], m_w_out[0], v_w_out[0])

    sent_w_in, = chip_exchange_wait("reduce_w_in_wait", grads_flight, True, d_wos)
    slab = pack_small(dcw, dcb, dg_pre, dg_post, dbias, dalog, ddsk, _lane_tile(dsinks3.reshape(1, ATT_Q_HEADS)), dgn,
                      dmeta.reshape(SLAB_META_SHAPE), loss_tile)
    shared_w_in, slabs = run_comm("share_w_in", Both(
        DirectExchange([sum_slots("sum_w_in", sent_w_in)], [None], SIBLING_MASK, "core", 2),
        DirectExchange([slab], [None], ALL_MASKS, "dev", 8)))
    small = sum_slots("sum_small", slabs)
    loss = small[SLAB_LOSS_ROW, 0]
    g_w_in, d_w_in, nm_w_in, nv_w_in = [
        a.T for a in adamw_w_in(shared_w_in.reshape(PACK_W, D_MODEL), w_in_t, m_w_in_t, v_w_in_t)]

    cw_cols = CONV_DIM // 4
    meta_cols = D_MODEL // 4
    g_small = {
        "meta_tokens": lax.dynamic_slice(
            small[SLAB_META_ROW:SLAB_ROWS, 0:SLAB_META_SHAPE[1]].reshape(N_META, D_MODEL), (0, chip * meta_cols),
            (N_META, meta_cols)),
        "g_pre": small[5:6, 0:1024],
        "conv_w": lax.dynamic_slice(small, (0, chip * cw_cols), (CONV_WIDTH, cw_cols)),
        "conv_b": small[4:5, :],
        "dt_bias": small[5:6, 2048:2048 + SSM_HEADS],
        "a_log": small[5:6, 2176:2176 + SSM_HEADS],
        "d_skip": small[5:6, 2304:2304 + SSM_HEADS],
        "attn_sinks": small[5:6, 2432:2432 + ATT_Q_HEADS],
        "g_ssm_norm": small[6:7, 0:SSM_INNER],
        "g_post": small[5:6, 1024:2048],
    }
    names = list(g_small)
    w_small = dict(meta_tokens=meta_tokens, g_pre=g_pre, conv_w=conv_w[0], conv_b=conv_b, dt_bias=dt_bias, a_log=a_log,
                   d_skip=d_skip, attn_sinks=attn_sinks, g_ssm_norm=g_ssm_norm, g_post=g_post)
    m_small = dict(meta_tokens=m_meta_tokens, g_pre=m_g_pre, conv_w=m_conv_w[0], conv_b=m_conv_b, dt_bias=m_dt_bias,
                   a_log=m_a_log, d_skip=m_d_skip, attn_sinks=m_attn_sinks, g_ssm_norm=m_g_ssm_norm, g_post=m_g_post)
    v_small = dict(meta_tokens=v_meta_tokens, g_pre=v_g_pre, conv_w=v_conv_w[0], conv_b=v_conv_b, dt_bias=v_dt_bias,
                   a_log=v_a_log, d_skip=v_d_skip, attn_sinks=v_attn_sinks, g_ssm_norm=v_g_ssm_norm, g_post=v_g_post)
    upd = dict(zip(names, adamw_small([g_small[k] for k in names], [w_small[k] for k in names],
                                      [m_small[k] for k in names], [v_small[k] for k in names])))

    lead = {"conv_w"}

    def shaped(name, a):
        return a[None] if name in lead else a

    grads = dict(g_small, w_in=g_w_in, w_out_att=g_woa, w_out_ssm=g_wos, w_out=g_wo)
    deltas = dict({k: upd[k][0] for k in names}, w_in=d_w_in, w_out_att=d_woa, w_out_ssm=d_wos, w_out=d_wo)
    new_m = dict({k: upd[k][1] for k in names}, w_in=nm_w_in, w_out_att=nm_woa, w_out_ssm=nm_wos, w_out=nm_wo)
    new_v = dict({k: upd[k][2] for k in names}, w_in=nv_w_in, w_out_att=nv_woa, w_out_ssm=nv_wos, w_out=nv_wo)
    lead |= {"w_in", "w_out_att", "w_out_ssm", "w_out"}
    order = ["meta_tokens", "g_pre", "w_in", "conv_w", "conv_b", "dt_bias", "a_log", "d_skip", "attn_sinks",
             "g_ssm_norm", "w_out_att", "w_out_ssm", "w_out", "g_post"]
    outs = [loss, grad_x]
    for group in (grads, deltas, new_m, new_v):
        outs += [shaped(k, group[k]) for k in order]
    return tuple(outs)
```

```python
import numpy as np
import jax
import jax.numpy as jnp
from jax import lax
from jax.experimental import pallas as pl
from jax.experimental.pallas import tpu as pltpu

F32 = jnp.float32
BF16 = jnp.bfloat16

D_MODEL = 1024
N_META = 16
BLOCK = 128
PAD_ROWS = BLOCK - N_META
NORM_EPS = 1e-6
HEAD_DIM = 64
ATT_Q_HEADS = 16
ATT_KV_HEADS = 4
ATT_GROUP = 4
SSM_INNER = 2048
SSM_HEADS = 32
SSM_GROUPS = 4
SSM_STATE = 128
CONV_WIDTH = 4
CONV_DIM = 3072
LANES = 128

ADAM_LR = 0.001
ADAM_B1 = 0.9
ADAM_B2 = 0.999
ADAM_EPS = 1e-08
ADAM_WD = 0.01
ADAM_STEP = 10

VMEM_LIMIT = 48 * 1024 * 1024

SHARD_W = 2440
PACK_W = 2560
SHARD_STRIDE = 2432
N_ALIGNED = 9856
N_ACT = 10240
SEG = {
    "q": (0, 1024, 5120), "k": (1024, 256, 9216), "v": (1280, 256, 9472), "z_att": (1536, 1024, 6144),
    "z_ssm": (2560, 2048, 0), "xbc": (4608, 3072, 2048), "dt": (7680, 128, 9728),
    "gate_att": (7808, 1024, 7168), "gate_ssm": (8832, 1024, 8192),
}
DT_STORED_START = 7680
DT_PAD = LANES - SSM_HEADS


def _act_col(aligned_col):
    for a0, w, p0 in SEG.values():
        if a0 <= aligned_col < a0 + w:
            return p0 + aligned_col - a0
    raise ValueError(aligned_col)


def _call(body, *, name, out_shape, in_specs, out_specs, grid=(), scratch=(), sem=None, aliases=None):
    return pl.pallas_call(
        body, out_shape=out_shape, grid=grid, in_specs=in_specs, out_specs=out_specs, scratch_shapes=list(scratch),
        name=name, input_output_aliases=aliases or {},
        compiler_params=pltpu.CompilerParams(dimension_semantics=sem, vmem_limit_bytes=VMEM_LIMIT))


def _full(shape):
    n = len(shape)
    return pl.BlockSpec(shape, lambda *_: (0,) * n)


def _chip_index():
    return lax.axis_index("x") * 2 + lax.axis_index("y")


_sigmoid = jax.nn.sigmoid


def _silu(z):
    return z * _sigmoid(z)


def _rms(x, g):
    return x * lax.rsqrt(jnp.mean(x * x, axis=-1, keepdims=True) + NORM_EPS) * g


def _peer(mask):
    x, y, c = lax.axis_index("x"), lax.axis_index("y"), lax.axis_index("c")
    return ((1 - x) if mask & 4 else x, (1 - y) if mask & 2 else y, (1 - c) if mask & 1 else c)


def _me():
    return lax.axis_index("x"), lax.axis_index("y"), lax.axis_index("c")


def _chip_of(dev):
    return 2 * dev[0] + dev[1]


CHIP_MASKS = (4, 2, 6)
ALL_MASKS = (1, 2, 3, 4, 5, 6, 7)
SIBLING_MASK = (1,)


def _remote(src, dst, send_sem, recv_sem, dev):
    return pltpu.make_async_remote_copy(src_ref=src, dst_ref=dst, send_sem=send_sem, recv_sem=recv_sem,
                                        device_id=dev, device_id_type=pl.DeviceIdType.MESH)


class _StagedCopy:
    def __init__(self, src, stage, dst, load_sem, store_sem):
        self.load = pltpu.make_async_copy(src, stage, load_sem)
        self.store = pltpu.make_async_copy(stage, dst, store_sem)

    def start(self):
        self.load.start()
        self.load.wait()
        self.store.start()

    def wait(self):
        self.store.wait()


class DirectExchange:
    def __init__(self, arrays, pieces, masks, slot_kind, nslots, keep_own=True):
        self.arrays, self.pieces, self.masks, self.slot_kind = list(arrays), list(pieces), masks, slot_kind
        self.keep_own = keep_own
        n, nk = len(arrays), len(masks)
        shapes = [a.shape if p is None else p[1] for a, p in zip(arrays, pieces)]
        self.out_shape = [jax.ShapeDtypeStruct((nslots,) + tuple(s), a.dtype) for s, a in zip(shapes, arrays)]
        self.scratch = [pltpu.SemaphoreType.DMA((n * nk,)), pltpu.SemaphoreType.DMA((n * nk,))]
        if keep_own:
            self.scratch += [pltpu.SemaphoreType.DMA((2 * n,))] + [pltpu.VMEM(s, a.dtype) for s, a in zip(shapes, arrays)]
        self.has_mid = False

    def _copies(self, ins, outs, scratch):
        send_sems, recv_sems = scratch[:2]
        me = _me()
        slot = {"chip": _chip_of(me), "dev": 4 * me[0] + 2 * me[1] + me[2], "core": me[2]}[self.slot_kind]
        nk = len(self.masks)

        def piece(a, dev):
            return ins[a] if self.pieces[a] is None else self.pieces[a][0](ins[a], dev)

        local = []
        if self.keep_own:
            local_sems, stages = scratch[2], scratch[3:]
            local = [_StagedCopy(piece(a, me), stages[a], outs[a].at[slot], local_sems.at[2 * a], local_sems.at[2 * a + 1])
                     for a in range(len(ins))]
        remote = []
        for a in range(len(ins)):
            for ki, mask in enumerate(self.masks):
                dev = _peer(mask)
                remote.append(_remote(piece(a, dev), outs[a].at[slot], send_sems.at[a * nk + ki],
                                      recv_sems.at[a * nk + ki], dev))
        return local, remote

    def start(self, ins, outs, scratch):
        local, remote = self._copies(ins, outs, scratch)
        for cp in remote + local:
            cp.start()

    def finish(self, ins, outs, scratch):
        local, remote = self._copies(ins, outs, scratch)
        for cp in remote + local:
            cp.wait()


SPLIT_ROWS = 16


class TwoLevelGather:
    def __init__(self, arrays):
        self.arrays = list(arrays)
        n = len(arrays)
        self.out_shape = [jax.ShapeDtypeStruct((4,) + a.shape, a.dtype) for a in arrays]
        self.scratch = ([pltpu.SemaphoreType.DMA((4 * n,)), pltpu.SemaphoreType.DMA((4 * n,)),
                         pltpu.SemaphoreType.DMA((3 * n,)), pltpu.SemaphoreType.DMA((3 * n,)),
                         pltpu.SemaphoreType.DMA((2 * n,))] + [pltpu.VMEM(a.shape, a.dtype) for a in arrays])
        self.has_mid = True

    def _copies(self, ins, outs, scratch):
        ici_send, ici_recv, fwd_send, fwd_recv, local_sems = scratch[:5]
        stages = scratch[5:]
        me = _me()
        sibling, in_x, in_y, diagonal = _peer(1), _peer(4), _peer(2), _peer(6)
        plan = []
        for a in range(len(ins)):
            half = ins[a].shape[0] // 2
            first = half // 2 if half % (2 * SPLIT_ROWS) == 0 else half
            mine = pl.ds(me[2] * half, half)
            local = _StagedCopy(ins[a], stages[a], outs[a].at[_chip_of(me)], local_sems.at[2 * a], local_sems.at[2 * a + 1])

            def ici(k, src, dst, dev):
                return _remote(src, dst, ici_send.at[4 * a + k], ici_recv.at[4 * a + k], dev)

            def d2d(k, chip):
                zone = outs[a].at[_chip_of(chip), mine]
                return _remote(zone, zone, fwd_send.at[3 * a + k], fwd_recv.at[3 * a + k], sibling)

            own_zone = outs[a].at[_chip_of(me), mine]
            from_x = outs[a].at[_chip_of(in_x), pl.ds(me[2] * half, first)]
            onward = [ici(2, from_x, from_x, in_y), None]
            if first < half:
                from_y = outs[a].at[_chip_of(in_y), pl.ds(me[2] * half + first, half - first)]
                onward[1] = ici(3, from_y, from_y, in_x)
            plan.append(dict(
                local=local,
                own=[ici(0, ins[a].at[mine], own_zone, in_x), ici(1, ins[a].at[mine], own_zone, in_y)],
                onward=onward, sibling=[d2d(0, in_x), d2d(1, in_y), d2d(2, diagonal)]))
        return plan

    def start(self, ins, outs, scratch):
        for p in self._copies(ins, outs, scratch):
            for cp in p["own"]:
                cp.start()
            p["local"].start()

    def mid(self, ins, outs, scratch):
        plan = self._copies(ins, outs, scratch)
        for p in plan:
            for k in range(2):
                p["own"][k].wait_recv()
                if p["onward"][k] is not None:
                    p["onward"][k].start()
                p["sibling"][k].start()
        for p in plan:
            for cp in p["onward"]:
                if cp is not None:
                    cp.wait_recv()
            p["sibling"][2].start()

    def finish(self, ins, outs, scratch):
        for p in self._copies(ins, outs, scratch):
            for cp in p["sibling"]:
                cp.wait_recv()
            for cp in p["own"] + p["sibling"] + [cp for cp in p["onward"] if cp is not None]:
                cp.wait_send()
            p["local"].wait()


class Both:
    def __init__(self, a, b):
        self.a, self.b = a, b
        self.arrays, self.out_shape = a.arrays + b.arrays, a.out_shape + b.out_shape
        self.scratch = a.scratch + b.scratch
        self.has_mid = False
        assert not (a.has_mid or b.has_mid)

    def _parts(self, ins, outs, sems):
        na, sa = len(self.a.arrays), len(self.a.scratch)
        return (ins[:na], outs[:na], sems[:sa]), (ins[na:], outs[na:], sems[sa:])

    def start(self, ins, outs, sems):
        pa, pb = self._parts(ins, outs, sems)
        self.a.start(*pa)
        self.b.start(*pb)

    def finish(self, ins, outs, sems):
        pa, pb = self._parts(ins, outs, sems)
        self.a.finish(*pa)
        self.b.finish(*pb)


_ANY = pl.BlockSpec(memory_space=pl.ANY)


def run_comm(name, comm):
    n = len(comm.arrays)

    def body(*refs):
        ins, outs, sems = refs[:n], refs[n:2 * n], refs[2 * n:]
        comm.start(ins, outs, sems)
        if comm.has_mid:
            comm.mid(ins, outs, sems)
        comm.finish(ins, outs, sems)

    return pl.pallas_call(body, name=name, out_shape=comm.out_shape, in_specs=[_ANY] * n, out_specs=[_ANY] * n,
                          scratch_shapes=comm.scratch,
                          compiler_params=pltpu.CompilerParams(vmem_limit_bytes=VMEM_LIMIT))(*comm.arrays)


_HBM = pl.BlockSpec(memory_space=pltpu.HBM)
_SEM = pl.BlockSpec(memory_space=pltpu.SEMAPHORE)
_SIDE_EFFECT = pltpu.SideEffectType.DATAFLOW_SIDE_EFFECTING


def _chip_copies(srcs, lands, send_sems, recv_sems, by_target):
    me = _me()
    copies = []
    for a, (src, land) in enumerate(zip(srcs, lands)):
        for ki, mask in enumerate(CHIP_MASKS):
            dev = _peer(mask)
            k = a * len(CHIP_MASKS) + ki
            piece = src.at[_chip_of(dev)] if by_target else src
            copies.append(_remote(piece, land.at[_chip_of(me)], send_sems.at[k], recv_sems.at[k], dev))
    return copies


def chip_exchange_start(name, arrays, by_target):
    n = len(arrays)
    nsem = n * len(CHIP_MASKS)
    piece_shapes = [a.shape[1:] if by_target else a.shape for a in arrays]

    def body(*refs):
        srcs, lands = refs[:n], refs[n:2 * n]
        send_sems, recv_sems = refs[2 * n:2 * n + 2]
        token = refs[4 * n + 2]
        stages, local_sems = refs[4 * n + 3:5 * n + 3], refs[5 * n + 3]
        me = _me()
        for cp in _chip_copies(srcs, lands, send_sems, recv_sems, by_target):
            cp.start()
        own = [_StagedCopy(srcs[a].at[_chip_of(me)] if by_target else srcs[a], stages[a], lands[a].at[_chip_of(me)],
                           local_sems.at[2 * a], local_sems.at[2 * a + 1]) for a in range(n)]
        for cp in own:
            cp.load.start()
        for cp in own:
            cp.load.wait()
            cp.store.start()
        for cp in own:
            cp.store.wait()
        token[...] = jnp.zeros_like(token)

    lands = [lax.empty((4,) + tuple(s), a.dtype) for s, a in zip(piece_shapes, arrays)]
    hbm = lambda a: pltpu.HBM(a.shape, a.dtype)
    res = pl.pallas_call(
        body, name=name,
        out_shape=(pltpu.SemaphoreType.DMA((nsem,)), pltpu.SemaphoreType.DMA((nsem,)), *[hbm(a) for a in arrays],
                   *[hbm(l) for l in lands], jax.ShapeDtypeStruct((8, LANES), F32)),
        in_specs=(_HBM,) * (2 * n), out_specs=(_SEM, _SEM) + (_HBM,) * (2 * n) + (pl.BlockSpec(memory_space=pltpu.VMEM),),
        input_output_aliases={i: i + 2 for i in range(2 * n)},
        scratch_shapes=[pltpu.VMEM(tuple(s), a.dtype) for s, a in zip(piece_shapes, arrays)]
        + [pltpu.SemaphoreType.DMA((2 * n,))],
        compiler_params=pltpu.CompilerParams(has_side_effects=_SIDE_EFFECT, vmem_limit_bytes=VMEM_LIMIT),
    )(*[pltpu.with_memory_space_constraint(a, pltpu.HBM) for a in arrays + lands])
    return res[:-1], res[-1]


def pair_sum_exchange_start(name, partial, from_sibling):
    shards, _, rows, cols = partial.shape
    nsem = len(CHIP_MASKS)

    def body(p_ref, r_ref, _, send_sems, recv_sems, sum_ref, land_ref, token, p_buf, r_buf, o_buf, sems):
        me = _me()
        targets = [_chip_of(_peer(mask)) for mask in CHIP_MASKS] + [_chip_of(me)]
        remote = _chip_copies([sum_ref], [land_ref], send_sems, recv_sems, True)

        def loads(k):
            return [pltpu.make_async_copy(p_ref.at[targets[k], me[2]], p_buf.at[k % 2], sems.at[2 * (k % 2)]),
                    pltpu.make_async_copy(r_ref.at[targets[k]], r_buf.at[k % 2], sems.at[2 * (k % 2) + 1])]

        for cp in loads(0):
            cp.start()
        for k in range(shards):
            if k + 1 < shards:
                for cp in loads(k + 1):
                    cp.start()
            for cp in loads(k):
                cp.wait()
            o_buf[k % 2] = (p_buf[k % 2].astype(F32) + r_buf[k % 2].astype(F32)).astype(BF16)
            summed = sum_ref.at[targets[k]] if k < nsem else land_ref.at[targets[k]]
            store = pltpu.make_async_copy(o_buf.at[k % 2], summed, sems.at[4])
            store.start()
            store.wait()
            if k < nsem:
                remote[k].start()
        token[...] = jnp.zeros_like(token)

    piece = jax.ShapeDtypeStruct((shards, rows, cols), BF16)
    land = lax.empty(piece.shape, BF16)
    buf = pltpu.VMEM((2, rows, cols), BF16)
    res = pl.pallas_call(
        body, name=name,
        out_shape=(pltpu.SemaphoreType.DMA((nsem,)), pltpu.SemaphoreType.DMA((nsem,)), pltpu.HBM(piece.shape, BF16),
                   pltpu.HBM(piece.shape, BF16), jax.ShapeDtypeStruct((8, LANES), F32)),
        in_specs=(_HBM,) * 3, out_specs=(_SEM, _SEM, _HBM, _HBM, pl.BlockSpec(memory_space=pltpu.VMEM)),
        input_output_aliases={2: 3},
        scratch_shapes=[buf, buf, buf, pltpu.SemaphoreType.DMA((5,))],
        compiler_params=pltpu.CompilerParams(has_side_effects=_SIDE_EFFECT, vmem_limit_bytes=VMEM_LIMIT),
    )(*[pltpu.with_memory_space_constraint(a, pltpu.HBM) for a in (partial, from_sibling, land)])
    return res[:-1], res[-1]


def chip_exchange_wait(name, in_flight, by_target, after):
    send_sems, recv_sems, *thru = in_flight
    n = len(thru) // 2

    def body(*refs):
        srcs, lands, (send, recv) = refs[:n], refs[n:2 * n], refs[2 * n:2 * n + 2]
        for cp in _chip_copies(srcs, lands, send, recv, by_target):
            cp.wait_send()
            cp.wait_recv()

    return pl.pallas_call(
        body, name=name, out_shape=tuple(pltpu.HBM(t.shape, t.dtype) for t in thru),
        in_specs=(_HBM,) * (2 * n) + (_SEM, _SEM, pl.BlockSpec(memory_space=pl.ANY)), out_specs=(_HBM,) * (2 * n),
        input_output_aliases={i: i for i in range(2 * n)},
        compiler_params=pltpu.CompilerParams(has_side_effects=_SIDE_EFFECT),
    )(*thru, send_sems, recv_sems, after)[n:]


def _call_with_comm(body, comm, steps, args, *, name, out_shape, in_specs, out_specs, grid, scratch=(),
                    after_finish=None):
    ni, no, ns, nc = len(in_specs), len(out_specs), len(scratch), len(comm.arrays)

    def full_body(*refs):
        ins, cins = refs[:ni], refs[ni:ni + nc]
        outs, couts = refs[ni + nc:ni + nc + no], refs[ni + nc + no:ni + 2 * nc + no]
        scr, csems = refs[ni + 2 * nc + no:ni + 2 * nc + no + ns], refs[ni + 2 * nc + no + ns:]
        first, middle, last = steps()
        pl.when(first)(lambda: comm.start(cins, couts, csems))
        if comm.has_mid:
            pl.when(middle)(lambda: comm.mid(cins, couts, csems))
        body(*ins, *outs, *scr)

        @pl.when(last)
        def _():
            comm.finish(cins, couts, csems)
            if after_finish is not None:
                after_finish(ins, couts, outs, scr)

    res = pl.pallas_call(
        full_body, name=name, out_shape=list(out_shape) + comm.out_shape, grid=grid,
        in_specs=list(in_specs) + [_ANY] * nc, out_specs=list(out_specs) + [_ANY] * nc,
        scratch_shapes=list(scratch) + comm.scratch,
        compiler_params=pltpu.CompilerParams(dimension_semantics=("arbitrary",) * len(grid),
                                             vmem_limit_bytes=VMEM_LIMIT))(*args, *comm.arrays)
    return res[:no], res[no:]


def _shard_pieces(chip):
    if chip < 3:
        return [(0, SHARD_W, 8 * chip)]
    behind_dt = DT_STORED_START + SSM_HEADS - 3 * SHARD_W
    return [(0, behind_dt, 24), (behind_dt, SHARD_W - behind_dt, behind_dt + 24 + DT_PAD)]


W_IN_COLS = 256


def pack_w_in(wt):
    def body(w_ref, o_ref, pad_ref):
        chip = _chip_index()
        pad_ref[...] = jnp.zeros_like(pad_ref)
        for cv in range(4):
            @pl.when(chip == cv)
            def _():
                for src, n, dst in _shard_pieces(cv):
                    pad_ref[dst:dst + n, :] = w_ref[src:src + n, :]
        o_ref[...] = pad_ref[...].astype(BF16)

    return _call(body, name="pack_w_in", grid=(D_MODEL // W_IN_COLS,),
                 in_specs=[pl.BlockSpec((SHARD_W, W_IN_COLS), lambda i: (0, i))],
                 out_specs=pl.BlockSpec((PACK_W, W_IN_COLS), lambda i: (0, i)),
                 out_shape=jax.ShapeDtypeStruct((PACK_W, D_MODEL), BF16),
                 scratch=[pltpu.VMEM((PACK_W, W_IN_COLS), F32)], sem=("parallel",))(wt)


def _tile_runs():
    runs, fix = [], []
    for t in range(N_ALIGNED // LANES):
        s = min(t // 19, 3)
        j = t - 19 * s
        p = _act_col(t * LANES)
        if runs and runs[-1][1] == s and runs[-1][0] + runs[-1][3] == p and runs[-1][2] + runs[-1][3] == j * LANES:
            runs[-1][3] += LANES
        else:
            runs.append([p, s, j * LANES, LANES])
        if j == 0 and s > 0:
            fix.append((p, s - 1))
    return runs, fix


def unpack_w_in(bg):
    runs, fix = _tile_runs()

    def body(b_ref, o_ref):
        for p, s, j, w in runs:
            o_ref[p:p + w, :] = b_ref[s, j:j + w, :]
        for p, s in fix:
            o_ref[p:p + LANES, :] = o_ref[p:p + LANES, :] + b_ref[s, SHARD_STRIDE:PACK_W, :]
        o_ref[N_ALIGNED:N_ACT, :] = jnp.zeros((N_ACT - N_ALIGNED, W_IN_COLS), BF16)

    return _call(body, name="unpack_w_in", grid=(D_MODEL // W_IN_COLS,),
                 in_specs=[pl.BlockSpec((4, PACK_W, W_IN_COLS), lambda i: (0, 0, i))],
                 out_specs=pl.BlockSpec((N_ACT, W_IN_COLS), lambda i: (0, i)),
                 out_shape=jax.ShapeDtypeStruct((N_ACT, D_MODEL), BF16), sem=("parallel",))(bg)


def _adamw(w, g, m, v):
    m = ADAM_B1 * m + (1.0 - ADAM_B1) * g
    v = ADAM_B2 * v + (1.0 - ADAM_B2) * jnp.square(g)
    m_hat = m / (1.0 - ADAM_B1 ** ADAM_STEP)
    v_hat = v / (1.0 - ADAM_B2 ** ADAM_STEP)
    delta = -ADAM_LR * (m_hat / (jnp.sqrt(v_hat) + ADAM_EPS) + ADAM_WD * w)
    return delta, m, v


def adamw_w_in(g_packed, wt, mt, vt):
    cols = LANES

    def body(g_ref, w_ref, m_ref, v_ref, go_ref, d_ref, mo_ref, vo_ref):
        chip = _chip_index()
        for cv in range(4):
            @pl.when(chip == cv)
            def _():
                for dst, n, src in _shard_pieces(cv):
                    go_ref[dst:dst + n, :] = g_ref[src:src + n, :]
        d_ref[...], mo_ref[...], vo_ref[...] = _adamw(w_ref[...], go_ref[...], m_ref[...], v_ref[...])

    spec = pl.BlockSpec((SHARD_W, cols), lambda i: (0, i))
    shp = jax.ShapeDtypeStruct((SHARD_W, D_MODEL), F32)
    return _call(body, name="adamw_w_in", grid=(D_MODEL // cols,),
                 in_specs=[pl.BlockSpec((PACK_W, cols), lambda i: (0, i)), spec, spec, spec],
                 out_specs=[spec] * 4, out_shape=[shp] * 4, sem=("parallel",))(g_packed, wt, mt, vt)


def adamw_rows(name, g, w, m, v):
    r, c = g.shape
    rows = min(r, BLOCK)

    def body(g_ref, w_ref, m_ref, v_ref, d_ref, mo_ref, vo_ref):
        d_ref[...], mo_ref[...], vo_ref[...] = _adamw(w_ref[...], g_ref[...], m_ref[...], v_ref[...])

    spec = pl.BlockSpec((rows, c), lambda i: (i, 0))
    shp = jax.ShapeDtypeStruct((r, c), F32)
    return _call(body, name=name, grid=(r // rows,), in_specs=[spec] * 4, out_specs=[spec] * 3, out_shape=[shp] * 3,
                 sem=("parallel",))(g, w, m, v)


def adamw_small(gs, ws, ms, vs):
    n = len(gs)

    def body(*refs):
        g, w, m, v = refs[:n], refs[n:2 * n], refs[2 * n:3 * n], refs[3 * n:4 * n]
        outs = refs[4 * n:]
        for i in range(n):
            d, mn, vn = _adamw(w[i][...], g[i][...], m[i][...], v[i][...])
            outs[3 * i][...] = d
            outs[3 * i + 1][...] = mn
            outs[3 * i + 2][...] = vn

    specs = [_full(a.shape) for a in gs]
    res = _call(body, name="adamw_small", in_specs=specs * 4,
                out_specs=[s for s in specs for _ in range(3)],
                out_shape=[jax.ShapeDtypeStruct(a.shape, F32) for a in gs for _ in range(3)])(*gs, *ws, *ms, *vs)
    return [tuple(res[3 * i:3 * i + 3]) for i in range(n)]


def sum_slots(name, r):
    s, rr, c = r.shape
    rows = min(rr, BLOCK)

    def body(r_ref, o_ref):
        acc = r_ref[0].astype(F32)
        for k in range(1, s):
            acc = acc + r_ref[k].astype(F32)
        o_ref[...] = acc

    return _call(body, name=name, grid=(rr // rows,), in_specs=[pl.BlockSpec((s, rows, c), lambda i: (0, i, 0))],
                 out_specs=pl.BlockSpec((rows, c), lambda i: (i, 0)), out_shape=jax.ShapeDtypeStruct((rr, c), F32),
                 sem=("parallel",))(r)


def sum_pair(partial, from_sibling):
    s, _, rr, cols = partial.shape
    rows = rr // 2

    def body(c_ref, p_ref, r_ref, o_ref):
        o_ref[...] = (p_ref[...].astype(F32) + r_ref[...].astype(F32)).astype(BF16)

    core = lax.axis_index("c").astype(jnp.int32).reshape(1)
    return pl.pallas_call(
        body, name="sum_pair", out_shape=jax.ShapeDtypeStruct((s, rr, cols), BF16),
        grid_spec=pltpu.PrefetchScalarGridSpec(
            num_scalar_prefetch=1, grid=(s, rr // rows),
            in_specs=[pl.BlockSpec((None, None, rows, cols), lambda k, i, c: (k, c[0], i, 0)),
                      pl.BlockSpec((None, rows, cols), lambda k, i, c: (k, i, 0))],
            out_specs=pl.BlockSpec((None, rows, cols), lambda k, i, c: (k, i, 0))),
        compiler_params=pltpu.CompilerParams(dimension_semantics=("parallel", "parallel"),
                                             vmem_limit_bytes=VMEM_LIMIT))(core, partial, from_sibling)


def _col_tile(n, k):
    if n % 896 == 0 and k <= 1024:
        return 896
    return min(n, 512)


def project(name, x, wt, after):
    m, k = x.shape
    n = wt.shape[0]
    tn = D_MODEL

    def body(x_ref, w_ref, after_ref, o_ref):
        o_ref[...] = lax.dot_general(x_ref[...], w_ref[...], _NT, preferred_element_type=F32)

    return _call(body, name=name, grid=(n // tn,),
                 in_specs=[_full((m, k)), pl.BlockSpec((tn, k), lambda j: (j, 0)), _full(after.shape)],
                 out_specs=pl.BlockSpec((m, tn), lambda j: (0, j)), out_shape=jax.ShapeDtypeStruct((m, n), F32),
                 sem=("parallel",))(x, wt, after)


def _piece_tiles(pieces, width):
    spans, start = [], 0
    for p in pieces:
        spans.append((start, p.shape[1] // width))
        start += p.shape[1] // width
    return spans, start


def _piece_spec(block, span, rows_of, tile_of):
    first, count = span

    def index(*pos):
        t = tile_of(*pos) - first
        mine = (t >= 0) & (t < count)
        return jnp.where(mine, rows_of(*pos), 0), jnp.clip(t, 0, count - 1)

    return pl.BlockSpec(block, index)


def project_back(name, pieces, wt, after):
    m = pieces[0].shape[0]
    k = wt.shape[1]
    tm = m // 2
    spans, steps = _piece_tiles(pieces, D_MODEL)

    def body(*refs):
        w_ref, o_ref = refs[len(pieces)], refs[len(pieces) + 2]
        j = pl.program_id(1)

        @pl.when(j == 0)
        def _():
            o_ref[...] = jnp.zeros_like(o_ref)

        for dy_ref, (first, count) in zip(refs, spans):
            @pl.when((j >= first) & (j < first + count))
            def _():
                o_ref[...] += jnp.dot(dy_ref[...], w_ref[...], preferred_element_type=F32)

    return _call(body, name=name, grid=(m // tm, steps),
                 in_specs=[_piece_spec((tm, D_MODEL), s, lambda i, j: i, lambda i, j: j) for s in spans]
                 + [pl.BlockSpec((D_MODEL, k), lambda i, j: (j, 0)), _full(after.shape)],
                 out_specs=pl.BlockSpec((tm, k), lambda i, j: (i, 0)), out_shape=jax.ShapeDtypeStruct((m, k), F32),
                 sem=("parallel", "arbitrary"))(*pieces, wt, after)


GRAD_TILE = 512


def _slab_runs():
    runs = [[] for _ in range(N_ACT // GRAD_TILE)]
    for s in range(4):
        for j in range(PACK_W // LANES):
            tile, r = divmod(_act_col((19 * s + j) * LANES), GRAD_TILE)
            last = runs[tile][-1] if runs[tile] else None
            if last and last[2] == s and last[0] + last[1] == r and last[3] + last[1] == j * LANES:
                last[1] += LANES
            else:
                runs[tile].append([r, LANES, s, j * LANES])
    return runs


def weight_grad_t(name, pieces, x):
    m, k = x.shape
    spans, steps = _piece_tiles(pieces, GRAD_TILE)
    runs = _slab_runs()
    most = max(len(r) for r in runs)
    half_rows = PACK_W // 2
    lead = spans[-1][0]

    def tile_at(step):
        return (step + lead) % steps

    landed = {}
    for step in range(steps):
        for _, _, s, _ in runs[tile_at(step)]:
            landed[s] = step + 2

    def body(*refs):
        x_ref, o_ref, land_ref, tile_ref, sems, send_sems, recv_sems = refs[len(pieces):]
        i = pl.program_id(0)
        other = 1 - lax.axis_index("c")

        def copies(step):
            return [pltpu.make_async_copy(tile_ref.at[step % 2, r:r + n], o_ref.at[s, d:d + n],
                                          sems.at[(step % 2) * most + c])
                    for c, (r, n, s, d) in enumerate(runs[tile_at(step)])]

        def to_sibling(s):
            return _remote(o_ref.at[s, pl.ds(other * half_rows, half_rows)], land_ref.at[s], send_sems.at[s],
                           recv_sems.at[s], _peer(1))

        for step in range(2, steps):
            @pl.when(i == step)
            def _():
                for cp in copies(step - 2):
                    cp.wait()

        tile = tile_at(i)
        for dy_ref, (first, count) in zip(refs, spans):
            @pl.when((tile >= first) & (tile < first + count))
            def _():
                tile_ref[i % 2] = lax.dot_general(dy_ref[...], x_ref[...], (((0,), (0,)), ((), ())),
                                                  preferred_element_type=F32).astype(BF16)

        for step in range(steps):
            @pl.when(i == step)
            def _():
                for cp in copies(step):
                    cp.start()
                if step < steps - 1:
                    for s in range(4):
                        if landed[s] == step:
                            to_sibling(s).start()
                else:
                    for cp in copies(step - 1) + copies(step):
                        cp.wait()
                    for s in range(4):
                        if landed[s] >= steps - 1:
                            to_sibling(s).start()
                    for s in range(4):
                        to_sibling(s).wait()

    return _call(body, name=name, grid=(steps,),
                 in_specs=[_piece_spec((m, GRAD_TILE), s, lambda i: 0, tile_at) for s in spans]
                 + [pl.BlockSpec(memory_space=pltpu.VMEM)],
                 out_specs=[_ANY, _ANY],
                 out_shape=[jax.ShapeDtypeStruct((4, PACK_W, k), BF16), jax.ShapeDtypeStruct((4, half_rows, k), BF16)],
                 scratch=[pltpu.VMEM((2, GRAD_TILE, k), BF16), pltpu.SemaphoreType.DMA((2 * most,)),
                          pltpu.SemaphoreType.DMA((4,)), pltpu.SemaphoreType.DMA((4,))],
                 sem=("arbitrary",))(*pieces, x)


def mm_tn(name, x, dy):
    m, k = x.shape
    n = dy.shape[1]
    tn = _col_tile(n, k)

    def body(x_ref, dy_ref, o_ref):
        o_ref[...] = lax.dot_general(x_ref[...], dy_ref[...], (((0,), (0,)), ((), ())),
                                     preferred_element_type=F32).astype(BF16)

    return _call(body, name=name, grid=(n // tn,),
                 in_specs=[_full((m, k)), pl.BlockSpec((m, tn), lambda i: (0, i))],
                 out_specs=pl.BlockSpec((k, tn), lambda i: (0, i)), out_shape=jax.ShapeDtypeStruct((k, n), BF16),
                 sem=("parallel",))(x, dy)


def _row_spec(width, col_block=0):
    return pl.BlockSpec((BLOCK, width), lambda i: (i, col_block))


def _x_spec():
    return pl.BlockSpec((None, BLOCK, D_MODEL), lambda i: (0, jnp.maximum(i - 1, 0), 0))


def prep(name, x, g_pre, gather):
    nb = x.shape[1] // BLOCK + 1
    shards, _, shard_cols = gather.out_shape[-1].shape

    def body(x_ref, g_ref, h_ref, u_ref, meta_ref, sems):
        @pl.when(pl.program_id(0) < nb - 1)
        def _():
            h_ref[...] = x_ref[...]
            u_ref[...] = _rms(x_ref[...], g_ref[...]).astype(BF16)

    def first_block(ins, gathered, outs, scratch):
        g_ref, (h_ref, u_ref), (meta_ref, sems) = ins[1], outs, scratch
        copies = [pltpu.make_async_copy(gathered[-1].at[t], meta_ref.at[t], sems.at[t]) for t in range(shards)]
        for cp in copies:
            cp.start()
        for cp in copies:
            cp.wait()
        h_ref[0:PAD_ROWS, :] = jnp.zeros((PAD_ROWS, D_MODEL), F32)
        h_ref[PAD_ROWS:BLOCK, :] = jnp.concatenate([meta_ref[t] for t in range(shards)], axis=1)
        u_ref[...] = _rms(h_ref[...], g_ref[...]).astype(BF16)

    def at():
        i = pl.program_id(0)
        return i == 0, i == nb // 2, i == nb - 1

    rows = pl.BlockSpec((BLOCK, D_MODEL), lambda i: ((i + 1) % nb, 0))
    return _call_with_comm(
        body, gather, at, (x, g_pre), name=name, grid=(nb,),
        in_specs=[pl.BlockSpec((None, BLOCK, D_MODEL), lambda i: (0, jnp.minimum(i, nb - 2), 0)), _full((1, D_MODEL))],
        out_specs=[rows, rows],
        out_shape=[jax.ShapeDtypeStruct((nb * BLOCK, D_MODEL), F32), jax.ShapeDtypeStruct((nb * BLOCK, D_MODEL), BF16)],
        scratch=[pltpu.VMEM((shards, N_META, shard_cols), F32), pltpu.SemaphoreType.DMA((shards,))],
        after_finish=first_block)


def prep_bwd(h, du, dres, g_pre):
    seq = h.shape[0] - BLOCK
    rows = min(seq, 4 * BLOCK)

    def body(h0_ref, du0_ref, h_ref, du_ref, dres_ref, g_ref, gx_ref, gm_ref, gg_ref):
        i = pl.program_id(0)
        _, vjp = jax.vjp(_rms, h_ref[...], g_ref[...])
        dh, dg = vjp(du_ref[...])
        gx_ref[...] = dh + dres_ref[...]

        @pl.when(i == 0)
        def _():
            _, vjp0 = jax.vjp(_rms, h0_ref[...], g_ref[...])
            dh0, dg0 = vjp0(du0_ref[...])
            gm_ref[...] = dh0[PAD_ROWS:BLOCK, :]
            gg_ref[...] = dg0 + dg

        @pl.when(i > 0)
        def _():
            gg_ref[...] += dg

    seq_rows = pl.BlockSpec((pl.Element(rows), pl.Element(D_MODEL)), lambda i: (pl.multiple_of(BLOCK + rows * i, BLOCK), 0))
    first = pl.BlockSpec((BLOCK, D_MODEL), lambda i: (0, 0))
    return _call(body, name="prep_bwd", grid=(seq // rows,),
                 in_specs=[first, first, seq_rows, seq_rows, seq_rows, _full((1, D_MODEL))],
                 out_specs=[pl.BlockSpec((None, rows, D_MODEL), lambda i: (0, i, 0)), _full((N_META, D_MODEL)),
                            _full((1, D_MODEL))],
                 out_shape=[jax.ShapeDtypeStruct((1, seq, D_MODEL), F32),
                            jax.ShapeDtypeStruct((N_META, D_MODEL), F32), jax.ShapeDtypeStruct((1, D_MODEL), F32)],
                 sem=("arbitrary",))(h, du, h, du, dres, g_pre)


GROUP_W = SSM_INNER // SSM_GROUPS


def _gated_norm(y, z, g):
    t = y * _silu(z)
    return t * lax.rsqrt(jnp.mean(t * t, axis=-1, keepdims=True) + NORM_EPS) * g


def _gated_norm_groups(y, z, g):
    groups = [slice(k * GROUP_W, (k + 1) * GROUP_W) for k in range(SSM_GROUPS)]
    return jnp.concatenate([_gated_norm(y[:, s], z[:, s], g[:, s]) for s in groups], axis=1)


def _merge(ga, gs, ya, ys):
    return _sigmoid(ga) * ya + _sigmoid(gs) * ys


GATE_ATT_BLOCK = SEG["gate_att"][2] // D_MODEL
GATE_SSM_BLOCK = SEG["gate_ssm"][2] // D_MODEL


def _row_loss(out, g_post, x, target):
    diff = x + _rms(out, g_post) - target
    return 0.5 * jnp.sum(diff * diff) / D_MODEL


def tail(y_ssd, proj, a_att, x, target, woa, wos, wo, g_norm, g_post):
    nb = y_ssd.shape[0] // BLOCK
    rows = nb * BLOCK

    def body(y_ref, z_ref, ga_ref, gs_ref, a_ref, x_ref, t_ref, woa_ref, wos_ref, wo_ref, gn_ref, gp_ref,
             yn_ref, mg_ref, dout_ref, dya_ref, dys_ref, da_ref, dy_ref, dz_ref, dga_ref, dgs_ref, dres_ref,
             loss_ref, dgp_ref, dgn_ref):
        i = pl.program_id(0)
        yn, norm_vjp = jax.vjp(_gated_norm_groups, y_ref[...], z_ref[...], gn_ref[...])
        yn16 = yn.astype(BF16)
        y_ssm = jnp.dot(yn16, wos_ref[...], preferred_element_type=F32)
        y_att = jnp.dot(a_ref[...], woa_ref[...], preferred_element_type=F32)
        merged, merge_vjp = jax.vjp(_merge, ga_ref[...], gs_ref[...], y_att, y_ssm)
        merged16 = merged.astype(BF16)
        out = jnp.dot(merged16, wo_ref[...], preferred_element_type=F32)
        loss, loss_vjp = jax.vjp(_row_loss, out, gp_ref[...], x_ref[...], t_ref[...])
        counted = jnp.where(i > 0, 1.0, 0.0)
        dout, dgp, dres, _ = loss_vjp(counted)
        dout16 = dout.astype(BF16)
        dmerged = lax.dot_general(dout16, wo_ref[...], _NT, preferred_element_type=F32)
        dga, dgs, dya, dys = merge_vjp(dmerged)
        dya16, dys16 = dya.astype(BF16), dys.astype(BF16)
        dyn = lax.dot_general(dys16, wos_ref[...], _NT, preferred_element_type=F32)
        dy, dz, dgn = norm_vjp(dyn)

        yn_ref[...] = yn16
        mg_ref[...] = merged16
        dout_ref[...] = dout16
        dya_ref[...] = dya16
        dys_ref[...] = dys16
        da_ref[...] = lax.dot_general(dya16, woa_ref[...], _NT, preferred_element_type=F32)
        dy_ref[...] = dy
        dz_ref[...] = dz.astype(BF16)
        dga_ref[...] = dga.astype(BF16)
        dgs_ref[...] = dgs.astype(BF16)
        dres_ref[...] = dres

        @pl.when(i == 0)
        def _():
            loss_ref[...] = jnp.zeros_like(loss_ref)
            dgp_ref[...] = jnp.zeros_like(dgp_ref)
            dgn_ref[...] = jnp.zeros_like(dgn_ref)

        loss_ref[...] += loss * counted
        dgp_ref[...] += dgp
        dgn_ref[...] += dgn

    wide, narrow = _row_spec(SSM_INNER), _row_spec(D_MODEL)
    resident = pl.BlockSpec(memory_space=pltpu.VMEM)
    bf = lambda w: jax.ShapeDtypeStruct((rows, w), BF16)
    f32 = lambda w: jax.ShapeDtypeStruct((rows, w), F32)
    return _call(body, name="tail", grid=(nb,),
                 in_specs=[wide, wide, _row_spec(D_MODEL, GATE_ATT_BLOCK), _row_spec(D_MODEL, GATE_SSM_BLOCK), narrow,
                           _x_spec(), _x_spec(), resident, resident, resident, _full((1, SSM_INNER)),
                           _full((1, D_MODEL))],
                 out_specs=[wide, narrow, narrow, narrow, narrow, narrow, wide, wide, narrow, narrow, narrow,
                            _full((8, LANES)), _full((1, D_MODEL)), _full((1, SSM_INNER))],
                 out_shape=[bf(SSM_INNER), bf(D_MODEL), bf(D_MODEL), bf(D_MODEL), bf(D_MODEL), f32(D_MODEL),
                            f32(SSM_INNER), bf(SSM_INNER), bf(D_MODEL), bf(D_MODEL), f32(D_MODEL),
                            jax.ShapeDtypeStruct((8, LANES), F32), jax.ShapeDtypeStruct((1, D_MODEL), F32),
                            jax.ShapeDtypeStruct((1, SSM_INNER), F32)],
                 sem=("arbitrary",))(y_ssd, proj, proj, proj, a_att, x, target, woa, wos, wo, g_norm, g_post)


_NT = (((1,), (1,)), ((), ()))
ALIBI_SLOPES = tuple(2.0 ** (-8.0 * (h + 1) / ATT_Q_HEADS) for h in range(ATT_Q_HEADS))
KV_WIDTH = ATT_KV_HEADS * HEAD_DIM
Q_BLOCK = SEG["q"][2] // D_MODEL
Z_ATT_BLOCK = SEG["z_att"][2] // D_MODEL
K_BLOCK = SEG["k"][2] // KV_WIDTH
V_BLOCK = SEG["v"][2] // KV_WIDTH
META_ROW_BLOCK = PAD_ROWS // N_META


@jax.custom_vjp
def _swap_halves(x):
    return pltpu.roll(x, HEAD_DIM, 1)


_swap_halves.defvjp(lambda x: (pltpu.roll(x, HEAD_DIM, 1), None), lambda _, g: (pltpu.roll(g, HEAD_DIM, 1),))


def _both_halves(t, half):
    first = lax.broadcasted_iota(jnp.int32, t.shape, 1) < HEAD_DIM
    sw = _swap_halves(t)
    return jnp.where(first, t, sw) if half == 0 else jnp.where(first, sw, t)


def _attn_rows(q, z, kp, kc, vp, vc, km, vm, sinks, n):
    rows = ATT_GROUP * BLOCK
    i = lax.broadcasted_iota(jnp.int32, (rows, BLOCK), 0) & (BLOCK - 1)
    j = lax.broadcasted_iota(jnp.int32, (rows, BLOCK), 1)
    rel_c = (i - j).astype(F32)
    rel_p = rel_c + float(BLOCK)
    nv = jnp.zeros((rows, BLOCK), jnp.int32) + n
    ok_c = (i >= j) & (nv >= 1)
    ok_p = (j > i) & (nv >= 2)
    im = lax.broadcasted_iota(jnp.int32, (rows, N_META), 0) & (BLOCK - 1)
    jm = lax.broadcasted_iota(jnp.int32, (rows, N_META), 1)
    ok_m = ((jnp.zeros((rows, N_META), jnp.int32) + n) >= 1) | (im >= PAD_ROWS + jm)
    first = lax.broadcasted_iota(jnp.int32, (BLOCK, LANES), 1) < HEAD_DIM
    neg = -jnp.inf
    outs = []
    for kv in range(ATT_KV_HEADS):
        tile, half = divmod(kv, 2)
        lanes = slice(tile * LANES, (tile + 1) * LANES)
        kc2, kp2, km2 = (_both_halves(t[:, lanes], half).astype(BF16) for t in (kc, kp, km))
        vc2, vp2, vm2 = (_both_halves(t[:, lanes], half).astype(BF16) for t in (vc, vp, vm))
        qs, slope, sk = [], [], []
        for pair in range(ATT_GROUP // 2):
            c0 = (kv * ATT_GROUP + 2 * pair) * HEAD_DIM
            qp = q[:, c0:c0 + LANES] * HEAD_DIM ** -0.5
            qs += [jnp.where(first, qp, 0.0), jnp.where(first, 0.0, qp)]
        for g in range(ATT_GROUP):
            slope.append(jnp.full((BLOCK, 1), ALIBI_SLOPES[kv * ATT_GROUP + g], F32))
            sk.append(jnp.broadcast_to(sinks[kv * ATT_GROUP + g], (BLOCK, 1)))
        qs = jnp.concatenate(qs, axis=0).astype(BF16)
        slope = jnp.concatenate(slope, axis=0)
        sk = jnp.concatenate(sk, axis=0)
        sc = jnp.where(ok_c, lax.dot_general(qs, kc2, _NT, preferred_element_type=F32) - slope * rel_c, neg)
        sp = jnp.where(ok_p, lax.dot_general(qs, kp2, _NT, preferred_element_type=F32) - slope * rel_p, neg)
        sm = jnp.where(ok_m, lax.dot_general(qs, km2, _NT, preferred_element_type=F32), neg)
        mx = jnp.maximum(jnp.maximum(jnp.max(sc, axis=1, keepdims=True), jnp.max(sp, axis=1, keepdims=True)),
                         jnp.maximum(jnp.max(sm, axis=1, keepdims=True), sk))
        mx = lax.stop_gradient(mx)
        ec, ep, em, es = jnp.exp(sc - mx), jnp.exp(sp - mx), jnp.exp(sm - mx), jnp.exp(sk - mx)
        den = (es + jnp.sum(ec, axis=1, keepdims=True) + jnp.sum(ep, axis=1, keepdims=True)
               + jnp.sum(em, axis=1, keepdims=True))
        inv = 1.0 / den
        o = (jnp.dot((ec * inv).astype(BF16), vc2, preferred_element_type=F32)
             + jnp.dot((ep * inv).astype(BF16), vp2, preferred_element_type=F32)
             + jnp.dot((em * inv).astype(BF16), vm2, preferred_element_type=F32))
        for pair in range(ATT_GROUP // 2):
            r0 = 2 * pair * BLOCK
            outs.append(jnp.where(first, o[r0:r0 + BLOCK], o[r0 + BLOCK:r0 + 2 * BLOCK]))
    return jnp.concatenate(outs, axis=1) * _silu(z)


def _attn_specs(nb, steps_clamped):
    def blk(t):
        return jnp.minimum(t, nb - 1) if steps_clamped else t

    wide = lambda col: pl.BlockSpec((BLOCK, D_MODEL), lambda t: (blk(t), col))
    cur = lambda col: pl.BlockSpec((BLOCK, KV_WIDTH), lambda t: (blk(t), col))
    prev = lambda col: pl.BlockSpec((BLOCK, KV_WIDTH), lambda t: (jnp.maximum(blk(t) - 1, 0), col))
    meta = lambda col: pl.BlockSpec((N_META, KV_WIDTH), lambda t: (META_ROW_BLOCK, col))
    sinks = pl.BlockSpec((ATT_Q_HEADS, 1, 1), lambda t: (0, 0, 0))
    return [wide(Q_BLOCK), wide(Z_ATT_BLOCK), prev(K_BLOCK), cur(K_BLOCK), prev(V_BLOCK), cur(V_BLOCK),
            meta(K_BLOCK), meta(V_BLOCK), sinks]


def attn_fwd(proj, sinks):
    nb = proj.shape[0] // BLOCK

    def body(q_ref, z_ref, kp_ref, kc_ref, vp_ref, vc_ref, km_ref, vm_ref, sk_ref, o_ref):
        o_ref[...] = _attn_rows(q_ref[...], z_ref[...], kp_ref[...], kc_ref[...], vp_ref[...], vc_ref[...],
                                km_ref[...], vm_ref[...], tuple(sk_ref[h] for h in range(ATT_Q_HEADS)),
                                pl.program_id(0)).astype(BF16)

    return _call(body, name="attn_fwd", grid=(nb,), in_specs=_attn_specs(nb, False), out_specs=_row_spec(D_MODEL),
                 out_shape=jax.ShapeDtypeStruct((nb * BLOCK, D_MODEL), BF16), sem=("parallel",))(*([proj] * 8), sinks)


def attn_bwd(da, proj, sinks):
    nb = proj.shape[0] // BLOCK
    last = nb - 1
    wide = pl.BlockSpec((BLOCK, D_MODEL), lambda t: (jnp.minimum(t, last), 0))
    done = pl.BlockSpec((BLOCK, KV_WIDTH), lambda t: (jnp.maximum(t - 1, 0), 0))
    meta = _full((N_META, KV_WIDTH))
    par = _full((ATT_Q_HEADS, 1, 1))

    def body(da_ref, q_ref, z_ref, kp_ref, kc_ref, vp_ref, vc_ref, km_ref, vm_ref, sk_ref,
             dq_ref, dz_ref, dk_ref, dv_ref, dkm_ref, dvm_ref, dsk_ref, ck_ref, cv_ref):
        t = pl.program_id(0)

        @pl.when(t == 0)
        def _():
            ck_ref[...] = jnp.zeros_like(ck_ref)
            cv_ref[...] = jnp.zeros_like(cv_ref)
            dkm_ref[...] = jnp.zeros_like(dkm_ref)
            dvm_ref[...] = jnp.zeros_like(dvm_ref)
            dsk_ref[...] = jnp.zeros_like(dsk_ref)

        @pl.when(t < nb)
        def _():
            def f(q, z, kp, kc, vp, vc, km, vm, sk):
                return _attn_rows(q, z, kp, kc, vp, vc, km, vm, sk, t)

            _, vjp = jax.vjp(f, q_ref[...], z_ref[...], kp_ref[...], kc_ref[...], vp_ref[...], vc_ref[...],
                             km_ref[...], vm_ref[...], tuple(sk_ref[h] for h in range(ATT_Q_HEADS)))
            dq, dz, dkp, dkc, dvp, dvc, dkm, dvm, dsk = vjp(da_ref[...])
            dq_ref[...] = dq.astype(BF16)
            dz_ref[...] = dz.astype(BF16)
            for h in range(ATT_Q_HEADS):
                dsk_ref[h] += dsk[h]
            dk_ref[...] = ck_ref[...] + dkp
            dv_ref[...] = cv_ref[...] + dvp
            ck_ref[...] = dkc
            cv_ref[...] = dvc
            dkm_ref[...] += dkm
            dvm_ref[...] += dvm

        @pl.when(t == nb)
        def _():
            dk_ref[...] = ck_ref[...]
            dv_ref[...] = cv_ref[...]

    rows = nb * BLOCK
    return _call(body, name="attn_bwd", grid=(nb + 1,), in_specs=[wide] + _attn_specs(nb, True),
                 out_specs=[wide, wide, done, done, meta, meta, par],
                 out_shape=[jax.ShapeDtypeStruct((rows, D_MODEL), BF16), jax.ShapeDtypeStruct((rows, D_MODEL), BF16),
                            jax.ShapeDtypeStruct((rows, KV_WIDTH), F32), jax.ShapeDtypeStruct((rows, KV_WIDTH), F32),
                            jax.ShapeDtypeStruct((N_META, KV_WIDTH), F32), jax.ShapeDtypeStruct((N_META, KV_WIDTH), F32),
                            jax.ShapeDtypeStruct(sinks.shape, F32)],
                 scratch=[pltpu.VMEM((BLOCK, KV_WIDTH), F32), pltpu.VMEM((BLOCK, KV_WIDTH), F32)],
                 sem=("arbitrary",))(da, *([proj] * 8), sinks)


XBC_BLOCK0 = SEG["xbc"][2] // D_MODEL
CONV_COL_BLOCKS = CONV_DIM // D_MODEL
DT_TILE = SEG["dt"][2] // LANES


HALO = 8


def _conv_rows(length):
    return 544 if length % 544 == 0 else BLOCK


def _shift_rows(cur, before, j):
    if j == 0:
        return cur
    n = cur.shape[0]
    row = lax.broadcasted_iota(jnp.int32, cur.shape, 0)
    head = pltpu.roll(before, j, 0)
    if n > HALO:
        head = jnp.concatenate([head, jnp.zeros((n - HALO, cur.shape[1]), cur.dtype)], axis=0)
    return jnp.where(row >= j, pltpu.roll(cur, j, 0), head)


def _conv_pre(cur, before, w_ref, b_ref):
    pre = b_ref[...] + w_ref[CONV_WIDTH - 1:CONV_WIDTH, :] * cur
    for k in range(CONV_WIDTH - 1):
        pre = pre + w_ref[k:k + 1, :] * _shift_rows(cur, before, CONV_WIDTH - 1 - k)
    return pre


def _conv_specs(steps, rows, col0=0):
    first = XBC_BLOCK0 + col0
    halos = rows // HALO
    cur = pl.BlockSpec((rows, D_MODEL), lambda j, i: (i, first + j))
    before = pl.BlockSpec((HALO, D_MODEL), lambda j, i: (jnp.maximum(i * halos - 1, 0), first + j))
    after = pl.BlockSpec((HALO, D_MODEL), lambda j, i: (jnp.minimum(i + 1, steps - 1) * halos, first + j))
    return cur, before, after


def _valid_rows(i, rows):
    row = lax.broadcasted_iota(jnp.int32, (rows, D_MODEL), 0)
    return jnp.maximum((row >= PAD_ROWS).astype(F32), jnp.where(i > 0, 1.0, 0.0))


def conv_fwd(proj, conv_w, conv_b):
    rows = _conv_rows(proj.shape[0])
    steps = proj.shape[0] // rows
    cur, before, _ = _conv_specs(steps, rows)

    def body(c_ref, p_ref, w_ref, b_ref, o_ref):
        i = pl.program_id(1)
        pre = _conv_pre(c_ref[...], p_ref[...] * jnp.where(i > 0, 1.0, 0.0), w_ref, b_ref)
        o_ref[...] = _silu(pre) * _valid_rows(i, rows)

    return _call(body, name="conv_fwd", grid=(CONV_COL_BLOCKS, steps),
                 in_specs=[cur, before, pl.BlockSpec((CONV_WIDTH, D_MODEL), lambda j, i: (0, j)),
                           pl.BlockSpec((1, D_MODEL), lambda j, i: (0, j))],
                 out_specs=pl.BlockSpec((rows, D_MODEL), lambda j, i: (i, j)),
                 out_shape=jax.ShapeDtypeStruct((proj.shape[0], CONV_DIM), F32),
                 sem=("parallel", "parallel"))(proj, proj, conv_w, conv_b)


def conv_bwd(name, dparts, col0, proj, conv_w, conv_b):
    rows = _conv_rows(proj.shape[0])
    steps = proj.shape[0] // rows
    last = steps - 1
    ncol = sum(d.shape[1] for d in dparts) // D_MODEL
    np_ = len(dparts)
    cur, before, after = _conv_specs(steps, rows, col0)
    dcur = [pl.BlockSpec((rows, d.shape[1] // ncol), lambda j, i: (i, j)) for d in dparts]
    dafter = [pl.BlockSpec((HALO, d.shape[1] // ncol), lambda j, i: (jnp.minimum(i + 1, last) * (rows // HALO), j))
              for d in dparts]
    out_cur = pl.BlockSpec((rows, D_MODEL), lambda j, i: (i, j))
    wspec = pl.BlockSpec((CONV_WIDTH, D_MODEL), lambda j, i: (0, col0 + j))
    bspec = pl.BlockSpec((1, D_MODEL), lambda j, i: (0, col0 + j))
    wout = pl.BlockSpec((CONV_WIDTH, D_MODEL), lambda j, i: (0, j))
    bout = pl.BlockSpec((1, D_MODEL), lambda j, i: (0, j))

    def body(*refs):
        dc_refs, da_refs = refs[:np_], refs[np_:2 * np_]
        c_ref, p_ref, a_ref, w_ref, b_ref, du_ref, dw_ref, db_ref = refs[2 * np_:]
        i = pl.program_id(1)
        row = lax.broadcasted_iota(jnp.int32, (rows, D_MODEL), 0)
        curv = c_ref[...]
        beforev = p_ref[...] * jnp.where(i > 0, 1.0, 0.0)
        side_by_side = lambda rs: rs[0][...] if np_ == 1 else jnp.concatenate([r[...] for r in rs], axis=1)

        def dpre_of(pre, d):
            s = _sigmoid(pre)
            return d * (s * (1.0 + pre * (1.0 - s)))

        dp_c = dpre_of(_conv_pre(curv, beforev, w_ref, b_ref), side_by_side(dc_refs) * _valid_rows(i, rows))
        dp_a = dpre_of(_conv_pre(a_ref[...], curv[rows - HALO:], w_ref, b_ref),
                       side_by_side(da_refs) * jnp.where(i < last, 1.0, 0.0))
        du = w_ref[CONV_WIDTH - 1:CONV_WIDTH, :] * dp_c
        for j in range(1, CONV_WIDTH):
            tail = jnp.concatenate([jnp.zeros((rows - HALO, D_MODEL), F32), pltpu.roll(dp_a, HALO - j, 0)], axis=0)
            up = jnp.where(row < rows - j, pltpu.roll(dp_c, rows - j, 0), tail)
            du = du + w_ref[CONV_WIDTH - 1 - j:CONV_WIDTH - j, :] * up
        du_ref[...] = du.astype(BF16)

        @pl.when(i == 0)
        def _():
            dw_ref[...] = jnp.zeros_like(dw_ref)
            db_ref[...] = jnp.zeros_like(db_ref)

        for k in range(CONV_WIDTH):
            dw_ref[k:k + 1, :] += jnp.sum(dp_c * _shift_rows(curv, beforev, CONV_WIDTH - 1 - k), axis=0, keepdims=True)
        db_ref[...] += jnp.sum(dp_c, axis=0, keepdims=True)

    width = ncol * D_MODEL
    return _call(body, name=name, grid=(ncol, steps),
                 in_specs=dcur + dafter + [cur, before, after, wspec, bspec], out_specs=[out_cur, wout, bout],
                 out_shape=[jax.ShapeDtypeStruct((proj.shape[0], width), BF16),
                            jax.ShapeDtypeStruct((CONV_WIDTH, width), F32), jax.ShapeDtypeStruct((1, width), F32)],
                 sem=("parallel", "arbitrary"))(*dparts, *dparts, proj, proj, proj, conv_w, conv_b)


def _head_expand():
    e = np.zeros((LANES, SSM_INNER), np.float32)
    for h in range(SSM_HEADS):
        e[h, h * HEAD_DIM:(h + 1) * HEAD_DIM] = 1.0
    return jnp.asarray(e, dtype=BF16)


def _softplus(x):
    return jnp.maximum(x, 0.0) + jnp.log(1.0 + jnp.exp(-jnp.abs(x)))


def _bf16_parts(x):
    hi = x.astype(BF16)
    rest = x - hi.astype(F32)
    mid = rest.astype(BF16)
    return hi, mid, (rest - mid.astype(F32)).astype(BF16)


@jax.custom_vjp
def _times_01(x, m):
    return sum(jnp.dot(p, m, preferred_element_type=F32) for p in _bf16_parts(x))


def _times_01_bwd(m, g):
    return sum(lax.dot_general(p, m, _NT, preferred_element_type=F32) for p in _bf16_parts(g)), jnp.zeros_like(m)


_times_01.defvjp(lambda x, m: (_times_01(x, m), m), _times_01_bwd)


def _causal_ones():
    l = lax.broadcasted_iota(jnp.int32, (BLOCK, BLOCK), 0)
    s = lax.broadcasted_iota(jnp.int32, (BLOCK, BLOCK), 1)
    return (l >= s).astype(BF16)


@jax.custom_vjp
def _cumsum_rows(a):
    return sum(jnp.dot(_causal_ones(), p, preferred_element_type=F32) for p in _bf16_parts(a))


def _cumsum_rows_bwd(_, g):
    tn = (((0,), (0,)), ((), ()))
    return (sum(lax.dot_general(_causal_ones(), p, tn, preferred_element_type=F32) for p in _bf16_parts(g)),)


_cumsum_rows.defvjp(lambda a: (_cumsum_rows(a), None), _cumsum_rows_bwd)


def _ssd_heads(dt_tile, bias, alog, dsk, expand):
    dt = _softplus(dt_tile + bias)
    a = dt * (-jnp.exp(alog))
    one_row = lambda v: jnp.broadcast_to(v, (HALO, LANES))
    return _times_01(jnp.concatenate([dt, _cumsum_rows(a), one_row(jnp.sum(a, axis=0, keepdims=True)), one_row(dsk)],
                                     axis=0), expand)


HEADS_ROWS = 2 * BLOCK + 2 * HALO


def _ssd_group(xs, per_lane, bg, cg, state):
    l = lax.broadcasted_iota(jnp.int32, (BLOCK, BLOCK), 0)
    s = lax.broadcasted_iota(jnp.int32, (BLOCK, BLOCK), 1)
    causal = l >= s
    first_head = s < HEAD_DIM
    dtx, cs = per_lane[0:BLOCK], per_lane[BLOCK:2 * BLOCK]
    tot, dsk = per_lane[2 * BLOCK:2 * BLOCK + 1], per_lane[2 * BLOCK + HALO:2 * BLOCK + HALO + 1]
    bb, cb16 = bg.astype(BF16), cg.astype(BF16)
    cb = lax.dot_general(cb16, bb, _NT, preferred_element_type=F32)
    xr = xs * dtx
    y_diag = []
    for p in range(GROUP_W // LANES):
        lanes = slice(p * LANES, (p + 1) * LANES)
        c_pair = cs[:, lanes]
        c_swap = _swap_halves(c_pair)
        m = []
        for c_head in (jnp.where(first_head, c_pair, c_swap), jnp.where(first_head, c_swap, c_pair)):
            m.append(cb * jnp.exp(jnp.where(causal, c_head - c_head.T, -jnp.inf)))
        x_pair = xr[:, lanes]
        x_diag = jnp.concatenate([jnp.where(first_head, x_pair, 0.0), jnp.where(first_head, 0.0, x_pair)], axis=0)
        y_diag.append(jnp.dot(jnp.concatenate(m, axis=1).astype(BF16), x_diag.astype(BF16),
                              preferred_element_type=F32))
    st = lax.dot_general(bb, (xr * jnp.exp(tot - cs)).astype(BF16), (((0,), (0,)), ((), ())),
                         preferred_element_type=F32)
    new_state = state * jnp.exp(tot) + st
    y_off = jnp.dot(cb16, state.astype(BF16), preferred_element_type=F32) * jnp.exp(cs)
    return jnp.concatenate(y_diag, axis=1) + y_off + dsk * xs, new_state


BC_WIDTH = SSM_GROUPS * SSM_STATE


def _ssd_specs(chunk):
    xs = pl.BlockSpec((BLOCK, SSM_INNER), lambda c: (chunk(c), 0))
    dt = pl.BlockSpec((BLOCK, LANES), lambda c: (chunk(c), DT_TILE))
    expand = _full((LANES, SSM_INNER))
    b = pl.BlockSpec((BLOCK, BC_WIDTH), lambda c: (chunk(c), SSM_INNER // BC_WIDTH))
    cc = pl.BlockSpec((BLOCK, BC_WIDTH), lambda c: (chunk(c), SSM_INNER // BC_WIDTH + 1))
    par = _full((1, LANES))
    state = pl.BlockSpec((None, SSM_STATE, SSM_INNER), lambda c: (chunk(c), 0, 0))
    return xs, dt, expand, b, cc, par, state


def _group_lanes(g):
    return slice(g * GROUP_W, (g + 1) * GROUP_W), slice(g * SSM_STATE, (g + 1) * SSM_STATE)


def ssd_fwd(xbc, proj, expand, bias, alog, dsk):
    nb = xbc.shape[0] // BLOCK
    xs, dt, ex, b, cc, par, state = _ssd_specs(lambda c: c)

    def body(x_ref, dt_ref, e_ref, bi_ref, al_ref, dk_ref, b_ref, c_ref, y_ref, sp_ref, st_ref):
        @pl.when(pl.program_id(0) == 0)
        def _():
            st_ref[...] = jnp.zeros_like(st_ref)

        per_lane = _ssd_heads(dt_ref[...], bi_ref[...], al_ref[...], dk_ref[...], e_ref[...])
        for g in range(SSM_GROUPS):
            wide, tile = _group_lanes(g)
            entering = st_ref[:, wide]
            sp_ref[:, wide] = entering
            y_ref[:, wide], st_ref[:, wide] = _ssd_group(x_ref[:, wide], per_lane[:, wide], b_ref[:, tile],
                                                         c_ref[:, tile], entering)

    return _call(body, name="ssd_fwd", grid=(nb,), in_specs=[xs, dt, ex, par, par, par, b, cc],
                 out_specs=[xs, state],
                 out_shape=[jax.ShapeDtypeStruct((nb * BLOCK, SSM_INNER), F32),
                            jax.ShapeDtypeStruct((nb, SSM_STATE, SSM_INNER), F32)],
                 scratch=[pltpu.VMEM((SSM_STATE, SSM_INNER), F32)],
                 sem=("arbitrary",))(xbc, proj, expand, bias, alog, dsk, xbc, xbc)


def ssd_bwd(dy, xbc, proj, expand, bias, alog, dsk, states, comm):
    nb = xbc.shape[0] // BLOCK
    last = nb - 1
    xs, dt, ex, b, cc, par, state = _ssd_specs(lambda c: last - c)
    tile = pl.BlockSpec((BLOCK, LANES), lambda c: (last - c, 0))
    nspec = pl.BlockSpec((BLOCK, BC_WIDTH), lambda c: (last - c, 0))

    def body(dy_ref, x_ref, dt_ref, e_ref, bi_ref, al_ref, dk_ref, b_ref, c_ref, sp_ref,
             dx_ref, ddt_ref, db_ref, dc_ref, dbi_ref, dal_ref, ddk_ref, ds_ref):
        @pl.when(pl.program_id(0) == 0)
        def _():
            ds_ref[...] = jnp.zeros_like(ds_ref)
            dbi_ref[...] = jnp.zeros_like(dbi_ref)
            dal_ref[...] = jnp.zeros_like(dal_ref)
            ddk_ref[...] = jnp.zeros_like(ddk_ref)

        expand = e_ref[...]
        per_lane, heads_vjp = jax.vjp(lambda t, bi, al, dk: _ssd_heads(t, bi, al, dk, expand), dt_ref[...], bi_ref[...],
                                      al_ref[...], dk_ref[...])
        d_per_lane = []
        for g in range(SSM_GROUPS):
            wide, tile_lanes = _group_lanes(g)
            _, vjp = jax.vjp(_ssd_group, x_ref[:, wide], per_lane[:, wide], b_ref[:, tile_lanes], c_ref[:, tile_lanes],
                             sp_ref[:, wide])
            (dx_ref[:, wide], d_lanes, db_ref[:, tile_lanes], dc_ref[:, tile_lanes],
             ds_ref[:, wide]) = vjp((dy_ref[:, wide], ds_ref[:, wide]))
            d_per_lane.append(d_lanes)
        ddt, dbi, dal, ddk = heads_vjp(jnp.concatenate(d_per_lane, axis=1))
        ddt_ref[...] = ddt.astype(BF16)
        dbi_ref[...] += dbi
        dal_ref[...] += dal
        ddk_ref[...] += ddk

    def at():
        c = pl.program_id(0)
        return c == 0, c == 0, c == last

    par_shape = jax.ShapeDtypeStruct((1, LANES), F32)
    return _call_with_comm(
        body, comm, at, (dy, xbc, proj, expand, bias, alog, dsk, xbc, xbc, states), name="ssd_bwd",
        grid=(nb,), in_specs=[xs, xs, dt, ex, par, par, par, b, cc, state],
        out_specs=[xs, tile, nspec, nspec, par, par, par],
        out_shape=[jax.ShapeDtypeStruct((nb * BLOCK, SSM_INNER), F32), jax.ShapeDtypeStruct((nb * BLOCK, LANES), BF16),
                   jax.ShapeDtypeStruct((nb * BLOCK, BC_WIDTH), F32), jax.ShapeDtypeStruct((nb * BLOCK, BC_WIDTH), F32),
                   par_shape, par_shape, par_shape],
        scratch=[pltpu.VMEM((SSM_STATE, SSM_INNER), F32)])


SLAB_ROWS = 16
SLAB_META_ROW = 8
SLAB_META_SHAPE = (8, 2 * D_MODEL)
SLAB_LOSS_ROW = 7


def pack_small(dcw, dcb, dgpre, dgpost, dbias, dalog, ddsk, dsinks, dgn, dmeta, loss_tile):
    def body(cw, cb, gpre, gpost, dtb, al, dk, sk, gn, meta, loss, o_ref):
        o_ref[...] = jnp.zeros_like(o_ref)
        o_ref[SLAB_LOSS_ROW:SLAB_LOSS_ROW + 1, 0:LANES] = loss[0:1, :]
        o_ref[0:CONV_WIDTH, :] = cw[...]
        o_ref[4:5, :] = cb[...]
        o_ref[5:6, 0:1024] = gpre[...]
        o_ref[5:6, 1024:2048] = gpost[...]
        o_ref[5:6, 2048:2176] = dtb[...]
        o_ref[5:6, 2176:2304] = al[...]
        o_ref[5:6, 2304:2432] = dk[...]
        o_ref[5:6, 2432:2560] = sk[...]
        o_ref[6:7, 0:SSM_INNER] = gn[...]
        o_ref[SLAB_META_ROW:SLAB_ROWS, 0:SLAB_META_SHAPE[1]] = meta[...]

    args = (dcw, dcb, dgpre, dgpost, dbias, dalog, ddsk, dsinks, dgn, dmeta, loss_tile)
    return _call(body, name="pack_small", in_specs=[_full(a.shape) for a in args],
                 out_specs=_full((SLAB_ROWS, CONV_DIM)), out_shape=jax.ShapeDtypeStruct((SLAB_ROWS, CONV_DIM), F32))(*args)


def _lane_tile(v):
    return jnp.pad(v, ((0, 0), (0, LANES - v.shape[1])))


def kernel(x, meta_tokens, g_pre, w_in, conv_w, conv_b, dt_bias, a_log, d_skip, attn_sinks, g_ssm_norm, w_out_att, w_out_ssm, w_out, g_post, loss_target, m_meta_tokens, m_g_pre, m_w_in, m_conv_w, m_conv_b, m_dt_bias, m_a_log, m_d_skip, m_attn_sinks, m_g_ssm_norm, m_w_out_att, m_w_out_ssm, m_w_out, m_g_post, v_meta_tokens, v_g_pre, v_w_in, v_conv_w, v_conv_b, v_dt_bias, v_a_log, v_d_skip, v_attn_sinks, v_g_ssm_norm, v_w_out_att, v_w_out_ssm, v_w_out, v_g_post):
    chip = _chip_index()

    conv_w_rows = jnp.pad(conv_w[0], ((0, 2 * 8 - CONV_WIDTH), (0, 0)))
    w_in_t, m_w_in_t, v_w_in_t = w_in[0].T, m_w_in[0].T, v_w_in[0].T
    (h, u), (gathered_w_in, g_conv_w, g_meta) = prep(
        "gather_w_in", x, g_pre, TwoLevelGather([pack_w_in(w_in_t), conv_w_rows, meta_tokens]))
    w_all_t = unpack_w_in(gathered_w_in)
    cw_full = g_conv_w[:, :CONV_WIDTH].transpose(1, 0, 2).reshape(CONV_WIDTH, CONV_DIM)
    behind_w_in = 0.0 * g_meta[0, 0, 0]
    w_out_flight, w_out_started = chip_exchange_start(
        "gather_w_out_start", [(w[0] + behind_w_in).astype(BF16) for w in (w_out_att, w_out_ssm, w_out)], False)

    proj = project("in_proj", u, w_all_t, w_out_started)

    sinks3 = attn_sinks.reshape(ATT_Q_HEADS, 1, 1)
    a_att = attn_fwd(proj, sinks3)

    xbc = conv_fwd(proj, cw_full, conv_b)
    expand = _head_expand()
    head_pars = (_lane_tile(dt_bias), _lane_tile(a_log), _lane_tile(d_skip))
    y_ssd, states = ssd_fwd(xbc, proj, expand, *head_pars)
    woa, wos, wo = [g.reshape(-1, D_MODEL) for g in chip_exchange_wait("gather_w_out_wait", w_out_flight, False, y_ssd)]

    (yn, merged, dout, dy_att, dy_ssm, da_att, dy_ssd, dz_ssm, dga, dgs, dres, loss_tile, dg_post, dgn) = tail(
        y_ssd, proj, a_att, x, loss_target, woa, wos, wo, g_ssm_norm, g_post)

    dwo = mm_tn("out_proj_dw", merged, dout)
    dwoa = mm_tn("att_out_dw", a_att, dy_att)
    dwos = mm_tn("ssm_out_dw", yn, dy_ssm)
    dq, dz_att, dk, dv, dkmeta, dvmeta, dsinks3 = attn_bwd(da_att, proj, sinks3)
    dk = dk.at[PAD_ROWS:BLOCK].add(dkmeta).astype(BF16)
    dv = dv.at[PAD_ROWS:BLOCK].add(dvmeta).astype(BF16)

    def pieces(g):
        return g.reshape(4, 2, g.shape[0] // 8, g.shape[1])

    def to_owner(g):
        return (lambda ref, dev: ref.at[_chip_of(dev), dev[2]], (g.shape[0] // 8, g.shape[1]))

    (dxs, ddt_tile, dbg, dcg, dbias, dalog, ddsk), sent_w_out = ssd_bwd(
        dy_ssd, xbc, proj, expand, *head_pars, states,
        DirectExchange([pieces(dwoa), pieces(dwos), pieces(dwo)], [to_owner(dwoa), to_owner(dwos), to_owner(dwo)],
                       ALL_MASKS, "dev", 8))
    dxs_raw, dcw_xs, dcb_xs = conv_bwd("conv_bwd_x", [dxs], 0, proj, cw_full, conv_b)
    dbc_raw, dcw_bc, dcb_bc = conv_bwd("conv_bwd_bc", [dbg, dcg], SSM_INNER // D_MODEL, proj, cw_full, conv_b)
    dcw = jnp.concatenate([dcw_xs, dcw_bc], axis=1)
    dcb = jnp.concatenate([dcb_xs, dcb_bc], axis=1)

    narrow = jnp.concatenate([dk, dv, ddt_tile, jnp.zeros((dk.shape[0], N_ACT - N_ALIGNED), BF16)], axis=1)
    dproj = [dz_ssm, dxs_raw, dbc_raw, dq, dz_att, dga, dgs, narrow]
    partial, from_sibling = weight_grad_t("in_proj_dw", dproj, u)
    grads_flight, started = pair_sum_exchange_start(
        "reduce_w_in_start", partial.reshape(4, 2, PACK_W // 2, D_MODEL), from_sibling)
    du = project_back("in_proj_dx", dproj, w_all_t, started)
    grad_x, dmeta, dg_pre = prep_bwd(h, du, dres, g_pre)

    halves = [sum_slots("sum_" + nm, r) for nm, r in zip(("w_out_att", "w_out_ssm", "w_out"), sent_w_out)]
    shared = run_comm("share_w_out", DirectExchange(halves, [None] * 3, SIBLING_MASK, "core", 2))
    g_woa, g_wos, g_wo = [f.reshape(2 * f.shape[1], f.shape[2]) for f in shared]
    d_woa, nm_woa, nv_woa = adamw_rows("adamw_w_out_att", g_woa, w_out_att[0], m_w_out_att[0], v_w_out_att[0])
    d_wos, nm_wos, nv_wos = adamw_rows("adamw_w_out_ssm", g_wos, w_out_ssm[0], m_w_out_ssm[0], v_w_out_ssm[0])
    d_wo, nm_wo, nv_wo = adamw_rows("adamw_w_out", g_wo, w_out[0], m_w_out[0], v_w_out[0])

    sent_w_in, = chip_exchange_wait("reduce_w_in_wait", grads_flight, True, d_wos)
    slab = pack_small(dcw, dcb, dg_pre, dg_post, dbias, dalog, ddsk, _lane_tile(dsinks3.reshape(1, ATT_Q_HEADS)), dgn,
                      dmeta.reshape(SLAB_META_SHAPE), loss_tile)
    shared_w_in, slabs = run_comm("share_w_in", Both(
        DirectExchange([sum_slots("sum_w_in", sent_w_in)], [None], SIBLING_MASK, "core", 2),
        DirectExchange([slab], [None], ALL_MASKS, "dev", 8)))
    small = sum_slots("sum_small", slabs)
    loss = small[SLAB_LOSS_ROW, 0]
    g_w_in, d_w_in, nm_w_in, nv_w_in = [
        a.T for a in adamw_w_in(shared_w_in.reshape(PACK_W, D_MODEL), w_in_t, m_w_in_t, v_w_in_t)]

    cw_cols = CONV_DIM // 4
    meta_cols = D_MODEL // 4
    g_small = {
        "meta_tokens": lax.dynamic_slice(
            small[SLAB_META_ROW:SLAB_ROWS, 0:SLAB_META_SHAPE[1]].reshape(N_META, D_MODEL), (0, chip * meta_cols),
            (N_META, meta_cols)),
        "g_pre": small[5:6, 0:1024],
        "conv_w": lax.dynamic_slice(small, (0, chip * cw_cols), (CONV_WIDTH, cw_cols)),
        "conv_b": small[4:5, :],
        "dt_bias": small[5:6, 2048:2048 + SSM_HEADS],
        "a_log": small[5:6, 2176:2176 + SSM_HEADS],
        "d_skip": small[5:6, 2304:2304 + SSM_HEADS],
        "attn_sinks": small[5:6, 2432:2432 + ATT_Q_HEADS],
        "g_ssm_norm": small[6:7, 0:SSM_INNER],
        "g_post": small[5:6, 1024:2048],
    }
    names = list(g_small)
    w_small = dict(meta_tokens=meta_tokens, g_pre=g_pre, conv_w=conv_w[0], conv_b=conv_b, dt_bias=dt_bias, a_log=a_log,
                   d_skip=d_skip, attn_sinks=attn_sinks, g_ssm_norm=g_ssm_norm, g_post=g_post)
    m_small = dict(meta_tokens=m_meta_tokens, g_pre=m_g_pre, conv_w=m_conv_w[0], conv_b=m_conv_b, dt_bias=m_dt_bias,
                   a_log=m_a_log, d_skip=m_d_skip, attn_sinks=m_attn_sinks, g_ssm_norm=m_g_ssm_norm, g_post=m_g_post)
    v_small = dict(meta_tokens=v_meta_tokens, g_pre=v_g_pre, conv_w=v_conv_w[0], conv_b=v_conv_b, dt_bias=v_dt_bias,
                   a_log=v_a_log, d_skip=v_d_skip, attn_sinks=v_attn_sinks, g_ssm_norm=v_g_ssm_norm, g_post=v_g_post)
    upd = dict(zip(names, adamw_small([g_small[k] for k in names], [w_small[k] for k in names],
                                      [m_small[k] for k in names], [v_small[k] for k in names])))

    lead = {"conv_w"}

    def shaped(name, a):
        return a[None] if name in lead else a

    grads = dict(g_small, w_in=g_w_in, w_out_att=g_woa, w_out_ssm=g_wos, w_out=g_wo)
    deltas = dict({k: upd[k][0] for k in names}, w_in=d_w_in, w_out_att=d_woa, w_out_ssm=d_wos, w_out=d_wo)
    new_m = dict({k: upd[k][1] for k in names}, w_in=nm_w_in, w_out_att=nm_woa, w_out_ssm=nm_wos, w_out=nm_wo)
    new_v = dict({k: upd[k][2] for k in names}, w_in=nv_w_in, w_out_att=nv_woa, w_out_ssm=nv_wos, w_out=nv_wo)
    lead |= {"w_in", "w_out_att", "w_out_ssm", "w_out"}
    order = ["meta_tokens", "g_pre", "w_in", "conv_w", "conv_b", "dt_bias", "a_log", "d_skip", "attn_sinks",
             "g_ssm_norm", "w_out_att", "w_out_ssm", "w_out", "g_post"]
    outs = [loss, grad_x]
    for group in (grads, deltas, new_m, new_v):
        outs += [shaped(k, group[k]) for k in order]
    return tuple(outs)
```

```python
import numpy as np
import jax
import jax.numpy as jnp
from jax import lax
from jax.experimental import pallas as pl
from jax.experimental.pallas import tpu as pltpu

F32 = jnp.float32
BF16 = jnp.bfloat16

D_MODEL = 1024
N_META = 16
BLOCK = 128
PAD_ROWS = BLOCK - N_META
NORM_EPS = 1e-6
HEAD_DIM = 64
ATT_Q_HEADS = 16
ATT_KV_HEADS = 4
ATT_GROUP = 4
SSM_INNER = 2048
SSM_HEADS = 32
SSM_GROUPS = 4
SSM_STATE = 128
CONV_WIDTH = 4
CONV_DIM = 3072
LANES = 128

ADAM_LR = 0.001
ADAM_B1 = 0.9
ADAM_B2 = 0.999
ADAM_EPS = 1e-08
ADAM_WD = 0.01
ADAM_STEP = 10

VMEM_LIMIT = 48 * 1024 * 1024

SHARD_W = 2440
PACK_W = 2560
SHARD_STRIDE = 2432
N_ALIGNED = 9856
N_ACT = 10240
SEG = {
    "q": (0, 1024, 5120), "k": (1024, 256, 9216), "v": (1280, 256, 9472), "z_att": (1536, 1024, 6144),
    "z_ssm": (2560, 2048, 0), "xbc": (4608, 3072, 2048), "dt": (7680, 128, 9728),
    "gate_att": (7808, 1024, 7168), "gate_ssm": (8832, 1024, 8192),
}
DT_STORED_START = 7680
DT_PAD = LANES - SSM_HEADS


def _act_col(aligned_col):
    for a0, w, p0 in SEG.values():
        if a0 <= aligned_col < a0 + w:
            return p0 + aligned_col - a0
    raise ValueError(aligned_col)


def _call(body, *, name, out_shape, in_specs, out_specs, grid=(), scratch=(), sem=None, aliases=None):
    return pl.pallas_call(
        body, out_shape=out_shape, grid=grid, in_specs=in_specs, out_specs=out_specs, scratch_shapes=list(scratch),
        name=name, input_output_aliases=aliases or {},
        compiler_params=pltpu.CompilerParams(dimension_semantics=sem, vmem_limit_bytes=VMEM_LIMIT))


def _full(shape):
    n = len(shape)
    return pl.BlockSpec(shape, lambda *_: (0,) * n)


def _chip_index():
    return lax.axis_index("x") * 2 + lax.axis_index("y")


_sigmoid = jax.nn.sigmoid


def _silu(z):
    return z * _sigmoid(z)


def _rms(x, g):
    return x * lax.rsqrt(jnp.mean(x * x, axis=-1, keepdims=True) + NORM_EPS) * g


def _peer(mask):
    x, y, c = lax.axis_index("x"), lax.axis_index("y"), lax.axis_index("c")
    return ((1 - x) if mask & 4 else x, (1 - y) if mask & 2 else y, (1 - c) if mask & 1 else c)


def _me():
    return lax.axis_index("x"), lax.axis_index("y"), lax.axis_index("c")


def _chip_of(dev):
    return 2 * dev[0] + dev[1]


CHIP_MASKS = (4, 2, 6)
ALL_MASKS = (1, 2, 3, 4, 5, 6, 7)
SIBLING_MASK = (1,)


def _remote(src, dst, send_sem, recv_sem, dev):
    return pltpu.make_async_remote_copy(src_ref=src, dst_ref=dst, send_sem=send_sem, recv_sem=recv_sem,
                                        device_id=dev, device_id_type=pl.DeviceIdType.MESH)


class _StagedCopy:
    def __init__(self, src, stage, dst, load_sem, store_sem):
        self.load = pltpu.make_async_copy(src, stage, load_sem)
        self.store = pltpu.make_async_copy(stage, dst, store_sem)

    def start(self):
        self.load.start()
        self.load.wait()
        self.store.start()

    def wait(self):
        self.store.wait()


class DirectExchange:
    def __init__(self, arrays, pieces, masks, slot_kind, nslots, keep_own=True):
        self.arrays, self.pieces, self.masks, self.slot_kind = list(arrays), list(pieces), masks, slot_kind
        self.keep_own = keep_own
        n, nk = len(arrays), len(masks)
        shapes = [a.shape if p is None else p[1] for a, p in zip(arrays, pieces)]
        self.out_shape = [jax.ShapeDtypeStruct((nslots,) + tuple(s), a.dtype) for s, a in zip(shapes, arrays)]
        self.scratch = [pltpu.SemaphoreType.DMA((n * nk,)), pltpu.SemaphoreType.DMA((n * nk,))]
        if keep_own:
            self.scratch += [pltpu.SemaphoreType.DMA((2 * n,))] + [pltpu.VMEM(s, a.dtype) for s, a in zip(shapes, arrays)]
        self.has_mid = False

    def _copies(self, ins, outs, scratch):
        send_sems, recv_sems = scratch[:2]
        me = _me()
        slot = {"chip": _chip_of(me), "dev": 4 * me[0] + 2 * me[1] + me[2], "core": me[2]}[self.slot_kind]
        nk = len(self.masks)

        def piece(a, dev):
            return ins[a] if self.pieces[a] is None else self.pieces[a][0](ins[a], dev)

        local = []
        if self.keep_own:
            local_sems, stages = scratch[2], scratch[3:]
            local = [_StagedCopy(piece(a, me), stages[a], outs[a].at[slot], local_sems.at[2 * a], local_sems.at[2 * a + 1])
                     for a in range(len(ins))]
        remote = []
        for a in range(len(ins)):
            for ki, mask in enumerate(self.masks):
                dev = _peer(mask)
                remote.append(_remote(piece(a, dev), outs[a].at[slot], send_sems.at[a * nk + ki],
                                      recv_sems.at[a * nk + ki], dev))
        return local, remote

    def start(self, ins, outs, scratch):
        local, remote = self._copies(ins, outs, scratch)
        for cp in remote + local:
            cp.start()

    def finish(self, ins, outs, scratch):
        local, remote = self._copies(ins, outs, scratch)
        for cp in remote + local:
            cp.wait()


SPLIT_ROWS = 16


class TwoLevelGather:
    def __init__(self, arrays):
        self.arrays = list(arrays)
        n = len(arrays)
        self.out_shape = [jax.ShapeDtypeStruct((4,) + a.shape, a.dtype) for a in arrays]
        self.scratch = ([pltpu.SemaphoreType.DMA((4 * n,)), pltpu.SemaphoreType.DMA((4 * n,)),
                         pltpu.SemaphoreType.DMA((3 * n,)), pltpu.SemaphoreType.DMA((3 * n,)),
                         pltpu.SemaphoreType.DMA((2 * n,))] + [pltpu.VMEM(a.shape, a.dtype) for a in arrays])
        self.has_mid = True

    def _copies(self, ins, outs, scratch):
        ici_send, ici_recv, fwd_send, fwd_recv, local_sems = scratch[:5]
        stages = scratch[5:]
        me = _me()
        sibling, in_x, in_y, diagonal = _peer(1), _peer(4), _peer(2), _peer(6)
        plan = []
        for a in range(len(ins)):
            half = ins[a].shape[0] // 2
            first = half // 2 if half % (2 * SPLIT_ROWS) == 0 else half
            mine = pl.ds(me[2] * half, half)
            local = _StagedCopy(ins[a], stages[a], outs[a].at[_chip_of(me)], local_sems.at[2 * a], local_sems.at[2 * a + 1])

            def ici(k, src, dst, dev):
                return _remote(src, dst, ici_send.at[4 * a + k], ici_recv.at[4 * a + k], dev)

            def d2d(k, chip):
                zone = outs[a].at[_chip_of(chip), mine]
                return _remote(zone, zone, fwd_send.at[3 * a + k], fwd_recv.at[3 * a + k], sibling)

            own_zone = outs[a].at[_chip_of(me), mine]
            from_x = outs[a].at[_chip_of(in_x), pl.ds(me[2] * half, first)]
            onward = [ici(2, from_x, from_x, in_y), None]
            if first < half:
                from_y = outs[a].at[_chip_of(in_y), pl.ds(me[2] * half + first, half - first)]
                onward[1] = ici(3, from_y, from_y, in_x)
            plan.append(dict(
                local=local,
                own=[ici(0, ins[a].at[mine], own_zone, in_x), ici(1, ins[a].at[mine], own_zone, in_y)],
                onward=onward, sibling=[d2d(0, in_x), d2d(1, in_y), d2d(2, diagonal)]))
        return plan

    def start(self, ins, outs, scratch):
        for p in self._copies(ins, outs, scratch):
            for cp in p["own"]:
                cp.start()
            p["local"].start()

    def mid(self, ins, outs, scratch):
        plan = self._copies(ins, outs, scratch)
        for p in plan:
            for k in range(2):
                p["own"][k].wait_recv()
                if p["onward"][k] is not None:
                    p["onward"][k].start()
                p["sibling"][k].start()
        for p in plan:
            for cp in p["onward"]:
                if cp is not None:
                    cp.wait_recv()
            p["sibling"][2].start()

    def finish(self, ins, outs, scratch):
        for p in self._copies(ins, outs, scratch):
            for cp in p["sibling"]:
                cp.wait_recv()
            for cp in p["own"] + p["sibling"] + [cp for cp in p["onward"] if cp is not None]:
                cp.wait_send()
            p["local"].wait()


class Both:
    def __init__(self, a, b):
        self.a, self.b = a, b
        self.arrays, self.out_shape = a.arrays + b.arrays, a.out_shape + b.out_shape
        self.scratch = a.scratch + b.scratch
        self.has_mid = False
        assert not (a.has_mid or b.has_mid)

    def _parts(self, ins, outs, sems):
        na, sa = len(self.a.arrays), len(self.a.scratch)
        return (ins[:na], outs[:na], sems[:sa]), (ins[na:], outs[na:], sems[sa:])

    def start(self, ins, outs, sems):
        pa, pb = self._parts(ins, outs, sems)
        self.a.start(*pa)
        self.b.start(*pb)

    def finish(self, ins, outs, sems):
        pa, pb = self._parts(ins, outs, sems)
        self.a.finish(*pa)
        self.b.finish(*pb)


_ANY = pl.BlockSpec(memory_space=pl.ANY)


def run_comm(name, comm):
    n = len(comm.arrays)

    def body(*refs):
        ins, outs, sems = refs[:n], refs[n:2 * n], refs[2 * n:]
        comm.start(ins, outs, sems)
        if comm.has_mid:
            comm.mid(ins, outs, sems)
        comm.finish(ins, outs, sems)

    return pl.pallas_call(body, name=name, out_shape=comm.out_shape, in_specs=[_ANY] * n, out_specs=[_ANY] * n,
                          scratch_shapes=comm.scratch,
                          compiler_params=pltpu.CompilerParams(vmem_limit_bytes=VMEM_LIMIT))(*comm.arrays)


_HBM = pl.BlockSpec(memory_space=pltpu.HBM)
_SEM = pl.BlockSpec(memory_space=pltpu.SEMAPHORE)
_SIDE_EFFECT = pltpu.SideEffectType.DATAFLOW_SIDE_EFFECTING


def _chip_copies(srcs, lands, send_sems, recv_sems, by_target):
    me = _me()
    copies = []
    for a, (src, land) in enumerate(zip(srcs, lands)):
        for ki, mask in enumerate(CHIP_MASKS):
            dev = _peer(mask)
            k = a * len(CHIP_MASKS) + ki
            piece = src.at[_chip_of(dev)] if by_target else src
            copies.append(_remote(piece, land.at[_chip_of(me)], send_sems.at[k], recv_sems.at[k], dev))
    return copies


def chip_exchange_start(name, arrays, by_target):
    n = len(arrays)
    nsem = n * len(CHIP_MASKS)
    piece_shapes = [a.shape[1:] if by_target else a.shape for a in arrays]

    def body(*refs):
        srcs, lands = refs[:n], refs[n:2 * n]
        send_sems, recv_sems = refs[2 * n:2 * n + 2]
        token = refs[4 * n + 2]
        stages, local_sems = refs[4 * n + 3:5 * n + 3], refs[5 * n + 3]
        me = _me()
        for cp in _chip_copies(srcs, lands, send_sems, recv_sems, by_target):
            cp.start()
        own = [_StagedCopy(srcs[a].at[_chip_of(me)] if by_target else srcs[a], stages[a], lands[a].at[_chip_of(me)],
                           local_sems.at[2 * a], local_sems.at[2 * a + 1]) for a in range(n)]
        for cp in own:
            cp.load.start()
        for cp in own:
            cp.load.wait()
            cp.store.start()
        for cp in own:
            cp.store.wait()
        token[...] = jnp.zeros_like(token)

    lands = [lax.empty((4,) + tuple(s), a.dtype) for s, a in zip(piece_shapes, arrays)]
    hbm = lambda a: pltpu.HBM(a.shape, a.dtype)
    res = pl.pallas_call(
        body, name=name,
        out_shape=(pltpu.SemaphoreType.DMA((nsem,)), pltpu.SemaphoreType.DMA((nsem,)), *[hbm(a) for a in arrays],
                   *[hbm(l) for l in lands], jax.ShapeDtypeStruct((8, LANES), F32)),
        in_specs=(_HBM,) * (2 * n), out_specs=(_SEM, _SEM) + (_HBM,) * (2 * n) + (pl.BlockSpec(memory_space=pltpu.VMEM),),
        input_output_aliases={i: i + 2 for i in range(2 * n)},
        scratch_shapes=[pltpu.VMEM(tuple(s), a.dtype) for s, a in zip(piece_shapes, arrays)]
        + [pltpu.SemaphoreType.DMA((2 * n,))],
        compiler_params=pltpu.CompilerParams(has_side_effects=_SIDE_EFFECT, vmem_limit_bytes=VMEM_LIMIT),
    )(*[pltpu.with_memory_space_constraint(a, pltpu.HBM) for a in arrays + lands])
    return res[:-1], res[-1]


def pair_sum_exchange_start(name, partial, from_sibling):
    shards, _, rows, cols = partial.shape
    nsem = len(CHIP_MASKS)

    def body(p_ref, r_ref, _, send_sems, recv_sems, sum_ref, land_ref, token, p_buf, r_buf, o_buf, sems):
        me = _me()
        targets = [_chip_of(_peer(mask)) for mask in CHIP_MASKS] + [_chip_of(me)]
        remote = _chip_copies([sum_ref], [land_ref], send_sems, recv_sems, True)

        def loads(k):
            return [pltpu.make_async_copy(p_ref.at[targets[k], me[2]], p_buf.at[k % 2], sems.at[2 * (k % 2)]),
                    pltpu.make_async_copy(r_ref.at[targets[k]], r_buf.at[k % 2], sems.at[2 * (k % 2) + 1])]

        for cp in loads(0):
            cp.start()
        for k in range(shards):
            if k + 1 < shards:
                for cp in loads(k + 1):
                    cp.start()
            for cp in loads(k):
                cp.wait()
            o_buf[k % 2] = (p_buf[k % 2].astype(F32) + r_buf[k % 2].astype(F32)).astype(BF16)
            summed = sum_ref.at[targets[k]] if k < nsem else land_ref.at[targets[k]]
            store = pltpu.make_async_copy(o_buf.at[k % 2], summed, sems.at[4])
            store.start()
            store.wait()
            if k < nsem:
                remote[k].start()
        token[...] = jnp.zeros_like(token)

    piece = jax.ShapeDtypeStruct((shards, rows, cols), BF16)
    land = lax.empty(piece.shape, BF16)
    buf = pltpu.VMEM((2, rows, cols), BF16)
    res = pl.pallas_call(
        body, name=name,
        out_shape=(pltpu.SemaphoreType.DMA((nsem,)), pltpu.SemaphoreType.DMA((nsem,)), pltpu.HBM(piece.shape, BF16),
                   pltpu.HBM(piece.shape, BF16), jax.ShapeDtypeStruct((8, LANES), F32)),
        in_specs=(_HBM,) * 3, out_specs=(_SEM, _SEM, _HBM, _HBM, pl.BlockSpec(memory_space=pltpu.VMEM)),
        input_output_aliases={2: 3},
        scratch_shapes=[buf, buf, buf, pltpu.SemaphoreType.DMA((5,))],
        compiler_params=pltpu.CompilerParams(has_side_effects=_SIDE_EFFECT, vmem_limit_bytes=VMEM_LIMIT),
    )(*[pltpu.with_memory_space_constraint(a, pltpu.HBM) for a in (partial, from_sibling, land)])
    return res[:-1], res[-1]


def chip_exchange_wait(name, in_flight, by_target, after):
    send_sems, recv_sems, *thru = in_flight
    n = len(thru) // 2

    def body(*refs):
        srcs, lands, (send, recv) = refs[:n], refs[n:2 * n], refs[2 * n:2 * n + 2]
        for cp in _chip_copies(srcs, lands, send, recv, by_target):
            cp.wait_send()
            cp.wait_recv()

    return pl.pallas_call(
        body, name=name, out_shape=tuple(pltpu.HBM(t.shape, t.dtype) for t in thru),
        in_specs=(_HBM,) * (2 * n) + (_SEM, _SEM, pl.BlockSpec(memory_space=pl.ANY)), out_specs=(_HBM,) * (2 * n),
        input_output_aliases={i: i for i in range(2 * n)},
        compiler_params=pltpu.CompilerParams(has_side_effects=_SIDE_EFFECT),
    )(*thru, send_sems, recv_sems, after)[n:]


def _call_with_comm(body, comm, steps, args, *, name, out_shape, in_specs, out_specs, grid, scratch=(),
                    after_finish=None):
    ni, no, ns, nc = len(in_specs), len(out_specs), len(scratch), len(comm.arrays)

    def full_body(*refs):
        ins, cins = refs[:ni], refs[ni:ni + nc]
        outs, couts = refs[ni + nc:ni + nc + no], refs[ni + nc + no:ni + 2 * nc + no]
        scr, csems = refs[ni + 2 * nc + no:ni + 2 * nc + no + ns], refs[ni + 2 * nc + no + ns:]
        first, middle, last = steps()
        pl.when(first)(lambda: comm.start(cins, couts, csems))
        if comm.has_mid:
            pl.when(middle)(lambda: comm.mid(cins, couts, csems))
        body(*ins, *outs, *scr)

        @pl.when(last)
        def _():
            comm.finish(cins, couts, csems)
            if after_finish is not None:
                after_finish(ins, couts, outs, scr)

    res = pl.pallas_call(
        full_body, name=name, out_shape=list(out_shape) + comm.out_shape, grid=grid,
        in_specs=list(in_specs) + [_ANY] * nc, out_specs=list(out_specs) + [_ANY] * nc,
        scratch_shapes=list(scratch) + comm.scratch,
        compiler_params=pltpu.CompilerParams(dimension_semantics=("arbitrary",) * len(grid),
                                             vmem_limit_bytes=VMEM_LIMIT))(*args, *comm.arrays)
    return res[:no], res[no:]


def _shard_pieces(chip):
    if chip < 3:
        return [(0, SHARD_W, 8 * chip)]
    behind_dt = DT_STORED_START + SSM_HEADS - 3 * SHARD_W
    return [(0, behind_dt, 24), (behind_dt, SHARD_W - behind_dt, behind_dt + 24 + DT_PAD)]


W_IN_COLS = 256


def pack_w_in(wt):
    def body(w_ref, o_ref, pad_ref):
        chip = _chip_index()
        pad_ref[...] = jnp.zeros_like(pad_ref)
        for cv in range(4):
            @pl.when(chip == cv)
            def _():
                for src, n, dst in _shard_pieces(cv):
                    pad_ref[dst:dst + n, :] = w_ref[src:src + n, :]
        o_ref[...] = pad_ref[...].astype(BF16)

    return _call(body, name="pack_w_in", grid=(D_MODEL // W_IN_COLS,),
                 in_specs=[pl.BlockSpec((SHARD_W, W_IN_COLS), lambda i: (0, i))],
                 out_specs=pl.BlockSpec((PACK_W, W_IN_COLS), lambda i: (0, i)),
                 out_shape=jax.ShapeDtypeStruct((PACK_W, D_MODEL), BF16),
                 scratch=[pltpu.VMEM((PACK_W, W_IN_COLS), F32)], sem=("parallel",))(wt)


def _tile_runs():
    runs, fix = [], []
    for t in range(N_ALIGNED // LANES):
        s = min(t // 19, 3)
        j = t - 19 * s
        p = _act_col(t * LANES)
        if runs and runs[-1][1] == s and runs[-1][0] + runs[-1][3] == p and runs[-1][2] + runs[-1][3] == j * LANES:
            runs[-1][3] += LANES
        else:
            runs.append([p, s, j * LANES, LANES])
        if j == 0 and s > 0:
            fix.append((p, s - 1))
    return runs, fix


def unpack_w_in(bg):
    runs, fix = _tile_runs()

    def body(b_ref, o_ref):
        for p, s, j, w in runs:
            o_ref[p:p + w, :] = b_ref[s, j:j + w, :]
        for p, s in fix:
            o_ref[p:p + LANES, :] = o_ref[p:p + LANES, :] + b_ref[s, SHARD_STRIDE:PACK_W, :]
        o_ref[N_ALIGNED:N_ACT, :] = jnp.zeros((N_ACT - N_ALIGNED, W_IN_COLS), BF16)

    return _call(body, name="unpack_w_in", grid=(D_MODEL // W_IN_COLS,),
                 in_specs=[pl.BlockSpec((4, PACK_W, W_IN_COLS), lambda i: (0, 0, i))],
                 out_specs=pl.BlockSpec((N_ACT, W_IN_COLS), lambda i: (0, i)),
                 out_shape=jax.ShapeDtypeStruct((N_ACT, D_MODEL), BF16), sem=("parallel",))(bg)


def _adamw(w, g, m, v):
    m = ADAM_B1 * m + (1.0 - ADAM_B1) * g
    v = ADAM_B2 * v + (1.0 - ADAM_B2) * jnp.square(g)
    m_hat = m / (1.0 - ADAM_B1 ** ADAM_STEP)
    v_hat = v / (1.0 - ADAM_B2 ** ADAM_STEP)
    delta = -ADAM_LR * (m_hat / (jnp.sqrt(v_hat) + ADAM_EPS) + ADAM_WD * w)
    return delta, m, v


def adamw_w_in(g_packed, wt, mt, vt):
    cols = LANES

    def body(g_ref, w_ref, m_ref, v_ref, go_ref, d_ref, mo_ref, vo_ref):
        chip = _chip_index()
        for cv in range(4):
            @pl.when(chip == cv)
            def _():
                for dst, n, src in _shard_pieces(cv):
                    go_ref[dst:dst + n, :] = g_ref[src:src + n, :]
        d_ref[...], mo_ref[...], vo_ref[...] = _adamw(w_ref[...], go_ref[...], m_ref[...], v_ref[...])

    spec = pl.BlockSpec((SHARD_W, cols), lambda i: (0, i))
    shp = jax.ShapeDtypeStruct((SHARD_W, D_MODEL), F32)
    return _call(body, name="adamw_w_in", grid=(D_MODEL // cols,),
                 in_specs=[pl.BlockSpec((PACK_W, cols), lambda i: (0, i)), spec, spec, spec],
                 out_specs=[spec] * 4, out_shape=[shp] * 4, sem=("parallel",))(g_packed, wt, mt, vt)


def adamw_rows(name, g, w, m, v):
    r, c = g.shape
    rows = min(r, BLOCK)

    def body(g_ref, w_ref, m_ref, v_ref, d_ref, mo_ref, vo_ref):
        d_ref[...], mo_ref[...], vo_ref[...] = _adamw(w_ref[...], g_ref[...], m_ref[...], v_ref[...])

    spec = pl.BlockSpec((rows, c), lambda i: (i, 0))
    shp = jax.ShapeDtypeStruct((r, c), F32)
    return _call(body, name=name, grid=(r // rows,), in_specs=[spec] * 4, out_specs=[spec] * 3, out_shape=[shp] * 3,
                 sem=("parallel",))(g, w, m, v)


def adamw_small(gs, ws, ms, vs):
    n = len(gs)

    def body(*refs):
        g, w, m, v = refs[:n], refs[n:2 * n], refs[2 * n:3 * n], refs[3 * n:4 * n]
        outs = refs[4 * n:]
        for i in range(n):
            d, mn, vn = _adamw(w[i][...], g[i][...], m[i][...], v[i][...])
            outs[3 * i][...] = d
            outs[3 * i + 1][...] = mn
            outs[3 * i + 2][...] = vn

    specs = [_full(a.shape) for a in gs]
    res = _call(body, name="adamw_small", in_specs=specs * 4,
                out_specs=[s for s in specs for _ in range(3)],
                out_shape=[jax.ShapeDtypeStruct(a.shape, F32) for a in gs for _ in range(3)])(*gs, *ws, *ms, *vs)
    return [tuple(res[3 * i:3 * i + 3]) for i in range(n)]


def sum_slots(name, r):
    s, rr, c = r.shape
    rows = min(rr, BLOCK)

    def body(r_ref, o_ref):
        acc = r_ref[0].astype(F32)
        for k in range(1, s):
            acc = acc + r_ref[k].astype(F32)
        o_ref[...] = acc

    return _call(body, name=name, grid=(rr // rows,), in_specs=[pl.BlockSpec((s, rows, c), lambda i: (0, i, 0))],
                 out_specs=pl.BlockSpec((rows, c), lambda i: (i, 0)), out_shape=jax.ShapeDtypeStruct((rr, c), F32),
                 sem=("parallel",))(r)


def sum_pair(partial, from_sibling):
    s, _, rr, cols = partial.shape
    rows = rr // 2

    def body(c_ref, p_ref, r_ref, o_ref):
        o_ref[...] = (p_ref[...].astype(F32) + r_ref[...].astype(F32)).astype(BF16)

    core = lax.axis_index("c").astype(jnp.int32).reshape(1)
    return pl.pallas_call(
        body, name="sum_pair", out_shape=jax.ShapeDtypeStruct((s, rr, cols), BF16),
        grid_spec=pltpu.PrefetchScalarGridSpec(
            num_scalar_prefetch=1, grid=(s, rr // rows),
            in_specs=[pl.BlockSpec((None, None, rows, cols), lambda k, i, c: (k, c[0], i, 0)),
                      pl.BlockSpec((None, rows, cols), lambda k, i, c: (k, i, 0))],
            out_specs=pl.BlockSpec((None, rows, cols), lambda k, i, c: (k, i, 0))),
        compiler_params=pltpu.CompilerParams(dimension_semantics=("parallel", "parallel"),
                                             vmem_limit_bytes=VMEM_LIMIT))(core, partial, from_sibling)


def _col_tile(n, k):
    if n % 896 == 0 and k <= 1024:
        return 896
    return min(n, 512)


def project(name, x, wt, after):
    m, k = x.shape
    n = wt.shape[0]
    tn = D_MODEL

    def body(x_ref, w_ref, after_ref, o_ref):
        o_ref[...] = lax.dot_general(x_ref[...], w_ref[...], _NT, preferred_element_type=F32)

    return _call(body, name=name, grid=(n // tn,),
                 in_specs=[_full((m, k)), pl.BlockSpec((tn, k), lambda j: (j, 0)), _full(after.shape)],
                 out_specs=pl.BlockSpec((m, tn), lambda j: (0, j)), out_shape=jax.ShapeDtypeStruct((m, n), F32),
                 sem=("parallel",))(x, wt, after)


def _piece_tiles(pieces, width):
    spans, start = [], 0
    for p in pieces:
        spans.append((start, p.shape[1] // width))
        start += p.shape[1] // width
    return spans, start


def _piece_spec(block, span, rows_of, tile_of):
    first, count = span

    def index(*pos):
        t = tile_of(*pos) - first
        mine = (t >= 0) & (t < count)
        return jnp.where(mine, rows_of(*pos), 0), jnp.clip(t, 0, count - 1)

    return pl.BlockSpec(block, index)


def project_back(name, pieces, wt, after):
    m = pieces[0].shape[0]
    k = wt.shape[1]
    tm = m // 2
    spans, steps = _piece_tiles(pieces, D_MODEL)

    def body(*refs):
        w_ref, o_ref = refs[len(pieces)], refs[len(pieces) + 2]
        j = pl.program_id(1)

        @pl.when(j == 0)
        def _():
            o_ref[...] = jnp.zeros_like(o_ref)

        for dy_ref, (first, count) in zip(refs, spans):
            @pl.when((j >= first) & (j < first + count))
            def _():
                o_ref[...] += jnp.dot(dy_ref[...], w_ref[...], preferred_element_type=F32)

    return _call(body, name=name, grid=(m // tm, steps),
                 in_specs=[_piece_spec((tm, D_MODEL), s, lambda i, j: i, lambda i, j: j) for s in spans]
                 + [pl.BlockSpec((D_MODEL, k), lambda i, j: (j, 0)), _full(after.shape)],
                 out_specs=pl.BlockSpec((tm, k), lambda i, j: (i, 0)), out_shape=jax.ShapeDtypeStruct((m, k), F32),
                 sem=("parallel", "arbitrary"))(*pieces, wt, after)


GRAD_TILE = 512


def _slab_runs():
    runs = [[] for _ in range(N_ACT // GRAD_TILE)]
    for s in range(4):
        for j in range(PACK_W // LANES):
            tile, r = divmod(_act_col((19 * s + j) * LANES), GRAD_TILE)
            last = runs[tile][-1] if runs[tile] else None
            if last and last[2] == s and last[0] + last[1] == r and last[3] + last[1] == j * LANES:
                last[1] += LANES
            else:
                runs[tile].append([r, LANES, s, j * LANES])
    return runs


def weight_grad_t(name, pieces, order, x):
    m, k = x.shape
    spans, steps = _piece_tiles(pieces, GRAD_TILE)
    runs = _slab_runs()
    most = max(len(r) for r in runs)
    half_rows = PACK_W // 2
    when, tile_at = [None] * len(pieces), []
    for p in order:
        when[p] = (len(tile_at), spans[p][1])
        tile_at += range(spans[p][0], spans[p][0] + spans[p][1])

    landed = {}
    for step in range(steps):
        for _, _, s, _ in runs[tile_at[step]]:
            landed[s] = step + 2

    def body(*refs):
        x_ref, o_ref, land_ref, tile_ref, sems, send_sems, recv_sems = refs[len(pieces):]
        i = pl.program_id(0)
        other = 1 - lax.axis_index("c")

        def copies(step):
            return [pltpu.make_async_copy(tile_ref.at[step % 2, r:r + n], o_ref.at[s, d:d + n],
                                          sems.at[(step % 2) * most + c])
                    for c, (r, n, s, d) in enumerate(runs[tile_at[step]])]

        def to_sibling(s):
            return _remote(o_ref.at[s, pl.ds(other * half_rows, half_rows)], land_ref.at[s], send_sems.at[s],
                           recv_sems.at[s], _peer(1))

        for step in range(2, steps):
            @pl.when(i == step)
            def _():
                for cp in copies(step - 2):
                    cp.wait()

        for dy_ref, (first, count) in zip(refs, when):
            @pl.when((i >= first) & (i < first + count))
            def _():
                tile_ref[i % 2] = lax.dot_general(dy_ref[...], x_ref[...], (((0,), (0,)), ((), ())),
                                                  preferred_element_type=F32).astype(BF16)

        for step in range(steps):
            @pl.when(i == step)
            def _():
                for cp in copies(step):
                    cp.start()
                if step < steps - 1:
                    for s in range(4):
                        if landed[s] == step:
                            to_sibling(s).start()
                else:
                    for cp in copies(step - 1) + copies(step):
                        cp.wait()
                    for s in range(4):
                        if landed[s] >= steps - 1:
                            to_sibling(s).start()
                    for s in range(4):
                        to_sibling(s).wait()

    return _call(body, name=name, grid=(steps,),
                 in_specs=[_piece_spec((m, GRAD_TILE), s, lambda i: 0, lambda i: i) for s in when]
                 + [pl.BlockSpec(memory_space=pltpu.VMEM)],
                 out_specs=[_ANY, _ANY],
                 out_shape=[jax.ShapeDtypeStruct((4, PACK_W, k), BF16), jax.ShapeDtypeStruct((4, half_rows, k), BF16)],
                 scratch=[pltpu.VMEM((2, GRAD_TILE, k), BF16), pltpu.SemaphoreType.DMA((2 * most,)),
                          pltpu.SemaphoreType.DMA((4,)), pltpu.SemaphoreType.DMA((4,))],
                 sem=("arbitrary",))(*pieces, x)


def mm_tn(name, x, dy):
    m, k = x.shape
    n = dy.shape[1]
    tn = _col_tile(n, k)

    def body(x_ref, dy_ref, o_ref):
        o_ref[...] = lax.dot_general(x_ref[...], dy_ref[...], (((0,), (0,)), ((), ())),
                                     preferred_element_type=F32).astype(BF16)

    return _call(body, name=name, grid=(n // tn,),
                 in_specs=[_full((m, k)), pl.BlockSpec((m, tn), lambda i: (0, i))],
                 out_specs=pl.BlockSpec((k, tn), lambda i: (0, i)), out_shape=jax.ShapeDtypeStruct((k, n), BF16),
                 sem=("parallel",))(x, dy)


def _row_spec(width, col_block=0):
    return pl.BlockSpec((BLOCK, width), lambda i: (i, col_block))


def _x_spec():
    return pl.BlockSpec((None, BLOCK, D_MODEL), lambda i: (0, jnp.maximum(i - 1, 0), 0))


def prep(name, x, g_pre, gather):
    nb = x.shape[1] // BLOCK + 1
    shards, _, shard_cols = gather.out_shape[-1].shape

    def body(x_ref, g_ref, h_ref, u_ref, meta_ref, sems):
        @pl.when(pl.program_id(0) < nb - 1)
        def _():
            h_ref[...] = x_ref[...]
            u_ref[...] = _rms(x_ref[...], g_ref[...]).astype(BF16)

    def first_block(ins, gathered, outs, scratch):
        g_ref, (h_ref, u_ref), (meta_ref, sems) = ins[1], outs, scratch
        copies = [pltpu.make_async_copy(gathered[-1].at[t], meta_ref.at[t], sems.at[t]) for t in range(shards)]
        for cp in copies:
            cp.start()
        for cp in copies:
            cp.wait()
        h_ref[0:PAD_ROWS, :] = jnp.zeros((PAD_ROWS, D_MODEL), F32)
        h_ref[PAD_ROWS:BLOCK, :] = jnp.concatenate([meta_ref[t] for t in range(shards)], axis=1)
        u_ref[...] = _rms(h_ref[...], g_ref[...]).astype(BF16)

    def at():
        i = pl.program_id(0)
        return i == 0, i == nb // 2, i == nb - 1

    rows = pl.BlockSpec((BLOCK, D_MODEL), lambda i: ((i + 1) % nb, 0))
    return _call_with_comm(
        body, gather, at, (x, g_pre), name=name, grid=(nb,),
        in_specs=[pl.BlockSpec((None, BLOCK, D_MODEL), lambda i: (0, jnp.minimum(i, nb - 2), 0)), _full((1, D_MODEL))],
        out_specs=[rows, rows],
        out_shape=[jax.ShapeDtypeStruct((nb * BLOCK, D_MODEL), F32), jax.ShapeDtypeStruct((nb * BLOCK, D_MODEL), BF16)],
        scratch=[pltpu.VMEM((shards, N_META, shard_cols), F32), pltpu.SemaphoreType.DMA((shards,))],
        after_finish=first_block)


def prep_bwd(h, du, dres, g_pre):
    seq = h.shape[0] - BLOCK
    rows = min(seq, 4 * BLOCK)

    def body(h0_ref, du0_ref, h_ref, du_ref, dres_ref, g_ref, gx_ref, gm_ref, gg_ref):
        i = pl.program_id(0)
        _, vjp = jax.vjp(_rms, h_ref[...], g_ref[...])
        dh, dg = vjp(du_ref[...])
        gx_ref[...] = dh + dres_ref[...]

        @pl.when(i == 0)
        def _():
            _, vjp0 = jax.vjp(_rms, h0_ref[...], g_ref[...])
            dh0, dg0 = vjp0(du0_ref[...])
            gm_ref[...] = dh0[PAD_ROWS:BLOCK, :]
            gg_ref[...] = dg0 + dg

        @pl.when(i > 0)
        def _():
            gg_ref[...] += dg

    seq_rows = pl.BlockSpec((pl.Element(rows), pl.Element(D_MODEL)), lambda i: (pl.multiple_of(BLOCK + rows * i, BLOCK), 0))
    first = pl.BlockSpec((BLOCK, D_MODEL), lambda i: (0, 0))
    return _call(body, name="prep_bwd", grid=(seq // rows,),
                 in_specs=[first, first, seq_rows, seq_rows, seq_rows, _full((1, D_MODEL))],
                 out_specs=[pl.BlockSpec((None, rows, D_MODEL), lambda i: (0, i, 0)), _full((N_META, D_MODEL)),
                            _full((1, D_MODEL))],
                 out_shape=[jax.ShapeDtypeStruct((1, seq, D_MODEL), F32),
                            jax.ShapeDtypeStruct((N_META, D_MODEL), F32), jax.ShapeDtypeStruct((1, D_MODEL), F32)],
                 sem=("arbitrary",))(h, du, h, du, dres, g_pre)


GROUP_W = SSM_INNER // SSM_GROUPS


def _gated_norm(y, z, g):
    t = y * _silu(z)
    return t * lax.rsqrt(jnp.mean(t * t, axis=-1, keepdims=True) + NORM_EPS) * g


def _gated_norm_groups(y, z, g):
    groups = [slice(k * GROUP_W, (k + 1) * GROUP_W) for k in range(SSM_GROUPS)]
    return jnp.concatenate([_gated_norm(y[:, s], z[:, s], g[:, s]) for s in groups], axis=1)


def _merge(ga, gs, ya, ys):
    return _sigmoid(ga) * ya + _sigmoid(gs) * ys


GATE_ATT_BLOCK = SEG["gate_att"][2] // D_MODEL
GATE_SSM_BLOCK = SEG["gate_ssm"][2] // D_MODEL


def _row_loss(out, g_post, x, target):
    diff = x + _rms(out, g_post) - target
    return 0.5 * jnp.sum(diff * diff) / D_MODEL


def tail(y_ssd, proj, a_att, x, target, woa, wos, wo, g_norm, g_post):
    nb = y_ssd.shape[0] // BLOCK
    rows = nb * BLOCK

    def body(y_ref, z_ref, ga_ref, gs_ref, a_ref, x_ref, t_ref, woa_ref, wos_ref, wo_ref, gn_ref, gp_ref,
             yn_ref, mg_ref, dout_ref, dya_ref, dys_ref, da_ref, dy_ref, dz_ref, dga_ref, dgs_ref, dres_ref,
             loss_ref, dgp_ref, dgn_ref):
        i = pl.program_id(0)
        yn, norm_vjp = jax.vjp(_gated_norm_groups, y_ref[...], z_ref[...], gn_ref[...])
        yn16 = yn.astype(BF16)
        y_ssm = jnp.dot(yn16, wos_ref[...], preferred_element_type=F32)
        y_att = jnp.dot(a_ref[...], woa_ref[...], preferred_element_type=F32)
        merged, merge_vjp = jax.vjp(_merge, ga_ref[...], gs_ref[...], y_att, y_ssm)
        merged16 = merged.astype(BF16)
        out = jnp.dot(merged16, wo_ref[...], preferred_element_type=F32)
        loss, loss_vjp = jax.vjp(_row_loss, out, gp_ref[...], x_ref[...], t_ref[...])
        counted = jnp.where(i > 0, 1.0, 0.0)
        dout, dgp, dres, _ = loss_vjp(counted)
        dout16 = dout.astype(BF16)
        dmerged = lax.dot_general(dout16, wo_ref[...], _NT, preferred_element_type=F32)
        dga, dgs, dya, dys = merge_vjp(dmerged)
        dya16, dys16 = dya.astype(BF16), dys.astype(BF16)
        dyn = lax.dot_general(dys16, wos_ref[...], _NT, preferred_element_type=F32)
        dy, dz, dgn = norm_vjp(dyn)

        yn_ref[...] = yn16
        mg_ref[...] = merged16
        dout_ref[...] = dout16
        dya_ref[...] = dya16
        dys_ref[...] = dys16
        da_ref[...] = lax.dot_general(dya16, woa_ref[...], _NT, preferred_element_type=F32)
        dy_ref[...] = dy
        dz_ref[...] = dz.astype(BF16)
        dga_ref[...] = dga.astype(BF16)
        dgs_ref[...] = dgs.astype(BF16)
        dres_ref[...] = dres

        @pl.when(i == 0)
        def _():
            loss_ref[...] = jnp.zeros_like(loss_ref)
            dgp_ref[...] = jnp.zeros_like(dgp_ref)
            dgn_ref[...] = jnp.zeros_like(dgn_ref)

        loss_ref[...] += loss * counted
        dgp_ref[...] += dgp
        dgn_ref[...] += dgn

    wide, narrow = _row_spec(SSM_INNER), _row_spec(D_MODEL)
    resident = pl.BlockSpec(memory_space=pltpu.VMEM)
    bf = lambda w: jax.ShapeDtypeStruct((rows, w), BF16)
    f32 = lambda w: jax.ShapeDtypeStruct((rows, w), F32)
    return _call(body, name="tail", grid=(nb,),
                 in_specs=[wide, wide, _row_spec(D_MODEL, GATE_ATT_BLOCK), _row_spec(D_MODEL, GATE_SSM_BLOCK), narrow,
                           _x_spec(), _x_spec(), resident, resident, resident, _full((1, SSM_INNER)),
                           _full((1, D_MODEL))],
                 out_specs=[wide, narrow, narrow, narrow, narrow, narrow, wide, wide, narrow, narrow, narrow,
                            _full((8, LANES)), _full((1, D_MODEL)), _full((1, SSM_INNER))],
                 out_shape=[bf(SSM_INNER), bf(D_MODEL), bf(D_MODEL), bf(D_MODEL), bf(D_MODEL), f32(D_MODEL),
                            f32(SSM_INNER), bf(SSM_INNER), bf(D_MODEL), bf(D_MODEL), f32(D_MODEL),
                            jax.ShapeDtypeStruct((8, LANES), F32), jax.ShapeDtypeStruct((1, D_MODEL), F32),
                            jax.ShapeDtypeStruct((1, SSM_INNER), F32)],
                 sem=("arbitrary",))(y_ssd, proj, proj, proj, a_att, x, target, woa, wos, wo, g_norm, g_post)


_NT = (((1,), (1,)), ((), ()))
ALIBI_SLOPES = tuple(2.0 ** (-8.0 * (h + 1) / ATT_Q_HEADS) for h in range(ATT_Q_HEADS))
KV_WIDTH = ATT_KV_HEADS * HEAD_DIM
Q_BLOCK = SEG["q"][2] // D_MODEL
Z_ATT_BLOCK = SEG["z_att"][2] // D_MODEL
K_BLOCK = SEG["k"][2] // KV_WIDTH
V_BLOCK = SEG["v"][2] // KV_WIDTH
META_ROW_BLOCK = PAD_ROWS // N_META


@jax.custom_vjp
def _swap_halves(x):
    return pltpu.roll(x, HEAD_DIM, 1)


_swap_halves.defvjp(lambda x: (pltpu.roll(x, HEAD_DIM, 1), None), lambda _, g: (pltpu.roll(g, HEAD_DIM, 1),))


def _both_halves(t, half):
    first = lax.broadcasted_iota(jnp.int32, t.shape, 1) < HEAD_DIM
    sw = _swap_halves(t)
    return jnp.where(first, t, sw) if half == 0 else jnp.where(first, sw, t)


def _attn_rows(q, z, kp, kc, vp, vc, km, vm, sinks, n):
    rows = ATT_GROUP * BLOCK
    i = lax.broadcasted_iota(jnp.int32, (rows, BLOCK), 0) & (BLOCK - 1)
    j = lax.broadcasted_iota(jnp.int32, (rows, BLOCK), 1)
    rel_c = (i - j).astype(F32)
    rel_p = rel_c + float(BLOCK)
    nv = jnp.zeros((rows, BLOCK), jnp.int32) + n
    ok_c = (i >= j) & (nv >= 1)
    ok_p = (j > i) & (nv >= 2)
    im = lax.broadcasted_iota(jnp.int32, (rows, N_META), 0) & (BLOCK - 1)
    jm = lax.broadcasted_iota(jnp.int32, (rows, N_META), 1)
    ok_m = ((jnp.zeros((rows, N_META), jnp.int32) + n) >= 1) | (im >= PAD_ROWS + jm)
    first = lax.broadcasted_iota(jnp.int32, (BLOCK, LANES), 1) < HEAD_DIM
    neg = -jnp.inf
    outs = []
    for kv in range(ATT_KV_HEADS):
        tile, half = divmod(kv, 2)
        lanes = slice(tile * LANES, (tile + 1) * LANES)
        kc2, kp2, km2 = (_both_halves(t[:, lanes], half).astype(BF16) for t in (kc, kp, km))
        vc2, vp2, vm2 = (_both_halves(t[:, lanes], half).astype(BF16) for t in (vc, vp, vm))
        qs, slope, sk = [], [], []
        for pair in range(ATT_GROUP // 2):
            c0 = (kv * ATT_GROUP + 2 * pair) * HEAD_DIM
            qp = q[:, c0:c0 + LANES] * HEAD_DIM ** -0.5
            qs += [jnp.where(first, qp, 0.0), jnp.where(first, 0.0, qp)]
        for g in range(ATT_GROUP):
            slope.append(jnp.full((BLOCK, 1), ALIBI_SLOPES[kv * ATT_GROUP + g], F32))
            sk.append(jnp.broadcast_to(sinks[kv * ATT_GROUP + g], (BLOCK, 1)))
        qs = jnp.concatenate(qs, axis=0).astype(BF16)
        slope = jnp.concatenate(slope, axis=0)
        sk = jnp.concatenate(sk, axis=0)
        sc = jnp.where(ok_c, lax.dot_general(qs, kc2, _NT, preferred_element_type=F32) - slope * rel_c, neg)
        sp = jnp.where(ok_p, lax.dot_general(qs, kp2, _NT, preferred_element_type=F32) - slope * rel_p, neg)
        sm = jnp.where(ok_m, lax.dot_general(qs, km2, _NT, preferred_element_type=F32), neg)
        mx = jnp.maximum(jnp.maximum(jnp.max(sc, axis=1, keepdims=True), jnp.max(sp, axis=1, keepdims=True)),
                         jnp.maximum(jnp.max(sm, axis=1, keepdims=True), sk))
        mx = lax.stop_gradient(mx)
        ec, ep, em, es = jnp.exp(sc - mx), jnp.exp(sp - mx), jnp.exp(sm - mx), jnp.exp(sk - mx)
        den = (es + jnp.sum(ec, axis=1, keepdims=True) + jnp.sum(ep, axis=1, keepdims=True)
               + jnp.sum(em, axis=1, keepdims=True))
        inv = 1.0 / den
        o = (jnp.dot((ec * inv).astype(BF16), vc2, preferred_element_type=F32)
             + jnp.dot((ep * inv).astype(BF16), vp2, preferred_element_type=F32)
             + jnp.dot((em * inv).astype(BF16), vm2, preferred_element_type=F32))
        for pair in range(ATT_GROUP // 2):
            r0 = 2 * pair * BLOCK
            outs.append(jnp.where(first, o[r0:r0 + BLOCK], o[r0 + BLOCK:r0 + 2 * BLOCK]))
    return jnp.concatenate(outs, axis=1) * _silu(z)


def _attn_specs(nb, steps_clamped):
    def blk(t):
        return jnp.minimum(t, nb - 1) if steps_clamped else t

    wide = lambda col: pl.BlockSpec((BLOCK, D_MODEL), lambda t: (blk(t), col))
    cur = lambda col: pl.BlockSpec((BLOCK, KV_WIDTH), lambda t: (blk(t), col))
    prev = lambda col: pl.BlockSpec((BLOCK, KV_WIDTH), lambda t: (jnp.maximum(blk(t) - 1, 0), col))
    meta = lambda col: pl.BlockSpec((N_META, KV_WIDTH), lambda t: (META_ROW_BLOCK, col))
    sinks = pl.BlockSpec((ATT_Q_HEADS, 1, 1), lambda t: (0, 0, 0))
    return [wide(Q_BLOCK), wide(Z_ATT_BLOCK), prev(K_BLOCK), cur(K_BLOCK), prev(V_BLOCK), cur(V_BLOCK),
            meta(K_BLOCK), meta(V_BLOCK), sinks]


def attn_fwd(proj, sinks):
    nb = proj.shape[0] // BLOCK

    def body(q_ref, z_ref, kp_ref, kc_ref, vp_ref, vc_ref, km_ref, vm_ref, sk_ref, o_ref):
        o_ref[...] = _attn_rows(q_ref[...], z_ref[...], kp_ref[...], kc_ref[...], vp_ref[...], vc_ref[...],
                                km_ref[...], vm_ref[...], tuple(sk_ref[h] for h in range(ATT_Q_HEADS)),
                                pl.program_id(0)).astype(BF16)

    return _call(body, name="attn_fwd", grid=(nb,), in_specs=_attn_specs(nb, False), out_specs=_row_spec(D_MODEL),
                 out_shape=jax.ShapeDtypeStruct((nb * BLOCK, D_MODEL), BF16), sem=("parallel",))(*([proj] * 8), sinks)


def attn_bwd(da, proj, sinks):
    nb = proj.shape[0] // BLOCK
    last = nb - 1
    wide = pl.BlockSpec((BLOCK, D_MODEL), lambda t: (jnp.minimum(t, last), 0))
    done = pl.BlockSpec((BLOCK, KV_WIDTH), lambda t: (jnp.maximum(t - 1, 0), 0))
    meta = _full((N_META, KV_WIDTH))
    par = _full((ATT_Q_HEADS, 1, 1))

    def body(da_ref, q_ref, z_ref, kp_ref, kc_ref, vp_ref, vc_ref, km_ref, vm_ref, sk_ref,
             dq_ref, dz_ref, dk_ref, dv_ref, dkm_ref, dvm_ref, dsk_ref, ck_ref, cv_ref):
        t = pl.program_id(0)

        @pl.when(t == 0)
        def _():
            ck_ref[...] = jnp.zeros_like(ck_ref)
            cv_ref[...] = jnp.zeros_like(cv_ref)
            dkm_ref[...] = jnp.zeros_like(dkm_ref)
            dvm_ref[...] = jnp.zeros_like(dvm_ref)
            dsk_ref[...] = jnp.zeros_like(dsk_ref)

        @pl.when(t < nb)
        def _():
            def f(q, z, kp, kc, vp, vc, km, vm, sk):
                return _attn_rows(q, z, kp, kc, vp, vc, km, vm, sk, t)

            _, vjp = jax.vjp(f, q_ref[...], z_ref[...], kp_ref[...], kc_ref[...], vp_ref[...], vc_ref[...],
                             km_ref[...], vm_ref[...], tuple(sk_ref[h] for h in range(ATT_Q_HEADS)))
            dq, dz, dkp, dkc, dvp, dvc, dkm, dvm, dsk = vjp(da_ref[...])
            dq_ref[...] = dq.astype(BF16)
            dz_ref[...] = dz.astype(BF16)
            for h in range(ATT_Q_HEADS):
                dsk_ref[h] += dsk[h]
            dk_ref[...] = ck_ref[...] + dkp
            dv_ref[...] = cv_ref[...] + dvp
            ck_ref[...] = dkc
            cv_ref[...] = dvc
            dkm_ref[...] += dkm
            dvm_ref[...] += dvm

        @pl.when(t == nb)
        def _():
            dk_ref[...] = ck_ref[...]
            dv_ref[...] = cv_ref[...]

    rows = nb * BLOCK
    return _call(body, name="attn_bwd", grid=(nb + 1,), in_specs=[wide] + _attn_specs(nb, True),
                 out_specs=[wide, wide, done, done, meta, meta, par],
                 out_shape=[jax.ShapeDtypeStruct((rows, D_MODEL), BF16), jax.ShapeDtypeStruct((rows, D_MODEL), BF16),
                            jax.ShapeDtypeStruct((rows, KV_WIDTH), F32), jax.ShapeDtypeStruct((rows, KV_WIDTH), F32),
                            jax.ShapeDtypeStruct((N_META, KV_WIDTH), F32), jax.ShapeDtypeStruct((N_META, KV_WIDTH), F32),
                            jax.ShapeDtypeStruct(sinks.shape, F32)],
                 scratch=[pltpu.VMEM((BLOCK, KV_WIDTH), F32), pltpu.VMEM((BLOCK, KV_WIDTH), F32)],
                 sem=("arbitrary",))(da, *([proj] * 8), sinks)


XBC_BLOCK0 = SEG["xbc"][2] // D_MODEL
CONV_COL_BLOCKS = CONV_DIM // D_MODEL
DT_TILE = SEG["dt"][2] // LANES


HALO = 8


def _conv_rows(length):
    return 544 if length % 544 == 0 else BLOCK


def _shift_rows(cur, before, j):
    if j == 0:
        return cur
    n = cur.shape[0]
    row = lax.broadcasted_iota(jnp.int32, cur.shape, 0)
    head = pltpu.roll(before, j, 0)
    if n > HALO:
        head = jnp.concatenate([head, jnp.zeros((n - HALO, cur.shape[1]), cur.dtype)], axis=0)
    return jnp.where(row >= j, pltpu.roll(cur, j, 0), head)


def _conv_pre(cur, before, w_ref, b_ref):
    pre = b_ref[...] + w_ref[CONV_WIDTH - 1:CONV_WIDTH, :] * cur
    for k in range(CONV_WIDTH - 1):
        pre = pre + w_ref[k:k + 1, :] * _shift_rows(cur, before, CONV_WIDTH - 1 - k)
    return pre


def _conv_specs(steps, rows, col0=0):
    first = XBC_BLOCK0 + col0
    halos = rows // HALO
    cur = pl.BlockSpec((rows, D_MODEL), lambda j, i: (i, first + j))
    before = pl.BlockSpec((HALO, D_MODEL), lambda j, i: (jnp.maximum(i * halos - 1, 0), first + j))
    after = pl.BlockSpec((HALO, D_MODEL), lambda j, i: (jnp.minimum(i + 1, steps - 1) * halos, first + j))
    return cur, before, after


def _valid_rows(i, rows):
    row = lax.broadcasted_iota(jnp.int32, (rows, D_MODEL), 0)
    return jnp.maximum((row >= PAD_ROWS).astype(F32), jnp.where(i > 0, 1.0, 0.0))


def conv_fwd(proj, conv_w, conv_b):
    rows = _conv_rows(proj.shape[0])
    steps = proj.shape[0] // rows
    cur, before, _ = _conv_specs(steps, rows)

    def body(c_ref, p_ref, w_ref, b_ref, o_ref):
        i = pl.program_id(1)
        pre = _conv_pre(c_ref[...], p_ref[...] * jnp.where(i > 0, 1.0, 0.0), w_ref, b_ref)
        o_ref[...] = _silu(pre) * _valid_rows(i, rows)

    return _call(body, name="conv_fwd", grid=(CONV_COL_BLOCKS, steps),
                 in_specs=[cur, before, pl.BlockSpec((CONV_WIDTH, D_MODEL), lambda j, i: (0, j)),
                           pl.BlockSpec((1, D_MODEL), lambda j, i: (0, j))],
                 out_specs=pl.BlockSpec((rows, D_MODEL), lambda j, i: (i, j)),
                 out_shape=jax.ShapeDtypeStruct((proj.shape[0], CONV_DIM), F32),
                 sem=("parallel", "parallel"))(proj, proj, conv_w, conv_b)


def conv_bwd(name, dparts, col0, proj, conv_w, conv_b):
    rows = _conv_rows(proj.shape[0])
    steps = proj.shape[0] // rows
    last = steps - 1
    ncol = sum(d.shape[1] for d in dparts) // D_MODEL
    np_ = len(dparts)
    cur, before, after = _conv_specs(steps, rows, col0)
    dcur = [pl.BlockSpec((rows, d.shape[1] // ncol), lambda j, i: (i, j)) for d in dparts]
    dafter = [pl.BlockSpec((HALO, d.shape[1] // ncol), lambda j, i: (jnp.minimum(i + 1, last) * (rows // HALO), j))
              for d in dparts]
    out_cur = pl.BlockSpec((rows, D_MODEL), lambda j, i: (i, j))
    wspec = pl.BlockSpec((CONV_WIDTH, D_MODEL), lambda j, i: (0, col0 + j))
    bspec = pl.BlockSpec((1, D_MODEL), lambda j, i: (0, col0 + j))
    wout = pl.BlockSpec((CONV_WIDTH, D_MODEL), lambda j, i: (0, j))
    bout = pl.BlockSpec((1, D_MODEL), lambda j, i: (0, j))

    def body(*refs):
        dc_refs, da_refs = refs[:np_], refs[np_:2 * np_]
        c_ref, p_ref, a_ref, w_ref, b_ref, du_ref, dw_ref, db_ref = refs[2 * np_:]
        i = pl.program_id(1)
        row = lax.broadcasted_iota(jnp.int32, (rows, D_MODEL), 0)
        curv = c_ref[...]
        beforev = p_ref[...] * jnp.where(i > 0, 1.0, 0.0)
        side_by_side = lambda rs: rs[0][...] if np_ == 1 else jnp.concatenate([r[...] for r in rs], axis=1)

        def dpre_of(pre, d):
            s = _sigmoid(pre)
            return d * (s * (1.0 + pre * (1.0 - s)))

        dp_c = dpre_of(_conv_pre(curv, beforev, w_ref, b_ref), side_by_side(dc_refs) * _valid_rows(i, rows))
        dp_a = dpre_of(_conv_pre(a_ref[...], curv[rows - HALO:], w_ref, b_ref),
                       side_by_side(da_refs) * jnp.where(i < last, 1.0, 0.0))
        du = w_ref[CONV_WIDTH - 1:CONV_WIDTH, :] * dp_c
        for j in range(1, CONV_WIDTH):
            tail = jnp.concatenate([jnp.zeros((rows - HALO, D_MODEL), F32), pltpu.roll(dp_a, HALO - j, 0)], axis=0)
            up = jnp.where(row < rows - j, pltpu.roll(dp_c, rows - j, 0), tail)
            du = du + w_ref[CONV_WIDTH - 1 - j:CONV_WIDTH - j, :] * up
        du_ref[...] = du.astype(BF16)

        @pl.when(i == 0)
        def _():
            dw_ref[...] = jnp.zeros_like(dw_ref)
            db_ref[...] = jnp.zeros_like(db_ref)

        for k in range(CONV_WIDTH):
            dw_ref[k:k + 1, :] += jnp.sum(dp_c * _shift_rows(curv, beforev, CONV_WIDTH - 1 - k), axis=0, keepdims=True)
        db_ref[...] += jnp.sum(dp_c, axis=0, keepdims=True)

    width = ncol * D_MODEL
    return _call(body, name=name, grid=(ncol, steps),
                 in_specs=dcur + dafter + [cur, before, after, wspec, bspec], out_specs=[out_cur, wout, bout],
                 out_shape=[jax.ShapeDtypeStruct((proj.shape[0], width), BF16),
                            jax.ShapeDtypeStruct((CONV_WIDTH, width), F32), jax.ShapeDtypeStruct((1, width), F32)],
                 sem=("parallel", "arbitrary"))(*dparts, *dparts, proj, proj, proj, conv_w, conv_b)


def _head_expand():
    e = np.zeros((LANES, SSM_INNER), np.float32)
    for h in range(SSM_HEADS):
        e[h, h * HEAD_DIM:(h + 1) * HEAD_DIM] = 1.0
    return jnp.asarray(e, dtype=BF16)


def _softplus(x):
    return jnp.maximum(x, 0.0) + jnp.log(1.0 + jnp.exp(-jnp.abs(x)))


def _bf16_parts(x):
    hi = x.astype(BF16)
    rest = x - hi.astype(F32)
    mid = rest.astype(BF16)
    return hi, mid, (rest - mid.astype(F32)).astype(BF16)


@jax.custom_vjp
def _times_01(x, m):
    return sum(jnp.dot(p, m, preferred_element_type=F32) for p in _bf16_parts(x))


def _times_01_bwd(m, g):
    return sum(lax.dot_general(p, m, _NT, preferred_element_type=F32) for p in _bf16_parts(g)), jnp.zeros_like(m)


_times_01.defvjp(lambda x, m: (_times_01(x, m), m), _times_01_bwd)


def _causal_ones():
    l = lax.broadcasted_iota(jnp.int32, (BLOCK, BLOCK), 0)
    s = lax.broadcasted_iota(jnp.int32, (BLOCK, BLOCK), 1)
    return (l >= s).astype(BF16)


@jax.custom_vjp
def _cumsum_rows(a):
    return sum(jnp.dot(_causal_ones(), p, preferred_element_type=F32) for p in _bf16_parts(a))


def _cumsum_rows_bwd(_, g):
    tn = (((0,), (0,)), ((), ()))
    return (sum(lax.dot_general(_causal_ones(), p, tn, preferred_element_type=F32) for p in _bf16_parts(g)),)


_cumsum_rows.defvjp(lambda a: (_cumsum_rows(a), None), _cumsum_rows_bwd)


def _ssd_heads(dt_tile, bias, alog, dsk, expand):
    dt = _softplus(dt_tile + bias)
    a = dt * (-jnp.exp(alog))
    one_row = lambda v: jnp.broadcast_to(v, (HALO, LANES))
    return _times_01(jnp.concatenate([dt, _cumsum_rows(a), one_row(jnp.sum(a, axis=0, keepdims=True)), one_row(dsk)],
                                     axis=0), expand)


HEADS_ROWS = 2 * BLOCK + 2 * HALO


def _ssd_group(xs, per_lane, bg, cg, state):
    l = lax.broadcasted_iota(jnp.int32, (BLOCK, BLOCK), 0)
    s = lax.broadcasted_iota(jnp.int32, (BLOCK, BLOCK), 1)
    causal = l >= s
    first_head = s < HEAD_DIM
    dtx, cs = per_lane[0:BLOCK], per_lane[BLOCK:2 * BLOCK]
    tot, dsk = per_lane[2 * BLOCK:2 * BLOCK + 1], per_lane[2 * BLOCK + HALO:2 * BLOCK + HALO + 1]
    bb, cb16 = bg.astype(BF16), cg.astype(BF16)
    cb = lax.dot_general(cb16, bb, _NT, preferred_element_type=F32)
    xr = xs * dtx
    y_diag = []
    for p in range(GROUP_W // LANES):
        lanes = slice(p * LANES, (p + 1) * LANES)
        c_pair = cs[:, lanes]
        c_swap = _swap_halves(c_pair)
        m = []
        for c_head in (jnp.where(first_head, c_pair, c_swap), jnp.where(first_head, c_swap, c_pair)):
            m.append(cb * jnp.exp(jnp.where(causal, c_head - c_head.T, -jnp.inf)))
        x_pair = xr[:, lanes]
        x_diag = jnp.concatenate([jnp.where(first_head, x_pair, 0.0), jnp.where(first_head, 0.0, x_pair)], axis=0)
        y_diag.append(jnp.dot(jnp.concatenate(m, axis=1).astype(BF16), x_diag.astype(BF16),
                              preferred_element_type=F32))
    st = lax.dot_general(bb, (xr * jnp.exp(tot - cs)).astype(BF16), (((0,), (0,)), ((), ())),
                         preferred_element_type=F32)
    new_state = state * jnp.exp(tot) + st
    y_off = jnp.dot(cb16, state.astype(BF16), preferred_element_type=F32) * jnp.exp(cs)
    return jnp.concatenate(y_diag, axis=1) + y_off + dsk * xs, new_state


BC_WIDTH = SSM_GROUPS * SSM_STATE


def _ssd_specs(chunk):
    xs = pl.BlockSpec((BLOCK, SSM_INNER), lambda c: (chunk(c), 0))
    dt = pl.BlockSpec((BLOCK, LANES), lambda c: (chunk(c), DT_TILE))
    expand = _full((LANES, SSM_INNER))
    b = pl.BlockSpec((BLOCK, BC_WIDTH), lambda c: (chunk(c), SSM_INNER // BC_WIDTH))
    cc = pl.BlockSpec((BLOCK, BC_WIDTH), lambda c: (chunk(c), SSM_INNER // BC_WIDTH + 1))
    par = _full((1, LANES))
    state = pl.BlockSpec((None, SSM_STATE, SSM_INNER), lambda c: (chunk(c), 0, 0))
    return xs, dt, expand, b, cc, par, state


def _group_lanes(g):
    return slice(g * GROUP_W, (g + 1) * GROUP_W), slice(g * SSM_STATE, (g + 1) * SSM_STATE)


def ssd_fwd(xbc, proj, expand, bias, alog, dsk):
    nb = xbc.shape[0] // BLOCK
    xs, dt, ex, b, cc, par, state = _ssd_specs(lambda c: c)

    def body(x_ref, dt_ref, e_ref, bi_ref, al_ref, dk_ref, b_ref, c_ref, y_ref, sp_ref, st_ref):
        @pl.when(pl.program_id(0) == 0)
        def _():
            st_ref[...] = jnp.zeros_like(st_ref)

        per_lane = _ssd_heads(dt_ref[...], bi_ref[...], al_ref[...], dk_ref[...], e_ref[...])
        for g in range(SSM_GROUPS):
            wide, tile = _group_lanes(g)
            entering = st_ref[:, wide]
            sp_ref[:, wide] = entering
            y_ref[:, wide], st_ref[:, wide] = _ssd_group(x_ref[:, wide], per_lane[:, wide], b_ref[:, tile],
                                                         c_ref[:, tile], entering)

    return _call(body, name="ssd_fwd", grid=(nb,), in_specs=[xs, dt, ex, par, par, par, b, cc],
                 out_specs=[xs, state],
                 out_shape=[jax.ShapeDtypeStruct((nb * BLOCK, SSM_INNER), F32),
                            jax.ShapeDtypeStruct((nb, SSM_STATE, SSM_INNER), F32)],
                 scratch=[pltpu.VMEM((SSM_STATE, SSM_INNER), F32)],
                 sem=("arbitrary",))(xbc, proj, expand, bias, alog, dsk, xbc, xbc)


def ssd_bwd(dy, xbc, proj, expand, bias, alog, dsk, states, comm):
    nb = xbc.shape[0] // BLOCK
    last = nb - 1
    xs, dt, ex, b, cc, par, state = _ssd_specs(lambda c: last - c)
    tile = pl.BlockSpec((BLOCK, LANES), lambda c: (last - c, 0))
    nspec = pl.BlockSpec((BLOCK, BC_WIDTH), lambda c: (last - c, 0))

    def body(dy_ref, x_ref, dt_ref, e_ref, bi_ref, al_ref, dk_ref, b_ref, c_ref, sp_ref,
             dx_ref, ddt_ref, db_ref, dc_ref, dbi_ref, dal_ref, ddk_ref, ds_ref):
        @pl.when(pl.program_id(0) == 0)
        def _():
            ds_ref[...] = jnp.zeros_like(ds_ref)
            dbi_ref[...] = jnp.zeros_like(dbi_ref)
            dal_ref[...] = jnp.zeros_like(dal_ref)
            ddk_ref[...] = jnp.zeros_like(ddk_ref)

        expand = e_ref[...]
        per_lane, heads_vjp = jax.vjp(lambda t, bi, al, dk: _ssd_heads(t, bi, al, dk, expand), dt_ref[...], bi_ref[...],
                                      al_ref[...], dk_ref[...])
        d_per_lane = []
        for g in range(SSM_GROUPS):
            wide, tile_lanes = _group_lanes(g)
            _, vjp = jax.vjp(_ssd_group, x_ref[:, wide], per_lane[:, wide], b_ref[:, tile_lanes], c_ref[:, tile_lanes],
                             sp_ref[:, wide])
            (dx_ref[:, wide], d_lanes, db_ref[:, tile_lanes], dc_ref[:, tile_lanes],
             ds_ref[:, wide]) = vjp((dy_ref[:, wide], ds_ref[:, wide]))
            d_per_lane.append(d_lanes)
        ddt, dbi, dal, ddk = heads_vjp(jnp.concatenate(d_per_lane, axis=1))
        ddt_ref[...] = ddt.astype(BF16)
        dbi_ref[...] += dbi
        dal_ref[...] += dal
        ddk_ref[...] += ddk

    def at():
        c = pl.program_id(0)
        return c == 0, c == 0, c == last

    par_shape = jax.ShapeDtypeStruct((1, LANES), F32)
    return _call_with_comm(
        body, comm, at, (dy, xbc, proj, expand, bias, alog, dsk, xbc, xbc, states), name="ssd_bwd",
        grid=(nb,), in_specs=[xs, xs, dt, ex, par, par, par, b, cc, state],
        out_specs=[xs, tile, nspec, nspec, par, par, par],
        out_shape=[jax.ShapeDtypeStruct((nb * BLOCK, SSM_INNER), F32), jax.ShapeDtypeStruct((nb * BLOCK, LANES), BF16),
                   jax.ShapeDtypeStruct((nb * BLOCK, BC_WIDTH), F32), jax.ShapeDtypeStruct((nb * BLOCK, BC_WIDTH), F32),
                   par_shape, par_shape, par_shape],
        scratch=[pltpu.VMEM((SSM_STATE, SSM_INNER), F32)])


SLAB_ROWS = 16
SLAB_META_ROW = 8
SLAB_META_SHAPE = (8, 2 * D_MODEL)
SLAB_LOSS_ROW = 7


def pack_small(dcw, dcb, dgpre, dgpost, dbias, dalog, ddsk, dsinks, dgn, dmeta, loss_tile):
    def body(cw, cb, gpre, gpost, dtb, al, dk, sk, gn, meta, loss, o_ref):
        o_ref[...] = jnp.zeros_like(o_ref)
        o_ref[SLAB_LOSS_ROW:SLAB_LOSS_ROW + 1, 0:LANES] = loss[0:1, :]
        o_ref[0:CONV_WIDTH, :] = cw[...]
        o_ref[4:5, :] = cb[...]
        o_ref[5:6, 0:1024] = gpre[...]
        o_ref[5:6, 1024:2048] = gpost[...]
        o_ref[5:6, 2048:2176] = dtb[...]
        o_ref[5:6, 2176:2304] = al[...]
        o_ref[5:6, 2304:2432] = dk[...]
        o_ref[5:6, 2432:2560] = sk[...]
        o_ref[6:7, 0:SSM_INNER] = gn[...]
        o_ref[SLAB_META_ROW:SLAB_ROWS, 0:SLAB_META_SHAPE[1]] = meta[...]

    args = (dcw, dcb, dgpre, dgpost, dbias, dalog, ddsk, dsinks, dgn, dmeta, loss_tile)
    return _call(body, name="pack_small", in_specs=[_full(a.shape) for a in args],
                 out_specs=_full((SLAB_ROWS, CONV_DIM)), out_shape=jax.ShapeDtypeStruct((SLAB_ROWS, CONV_DIM), F32))(*args)


def _lane_tile(v):
    return jnp.pad(v, ((0, 0), (0, LANES - v.shape[1])))


def kernel(x, meta_tokens, g_pre, w_in, conv_w, conv_b, dt_bias, a_log, d_skip, attn_sinks, g_ssm_norm, w_out_att, w_out_ssm, w_out, g_post, loss_target, m_meta_tokens, m_g_pre, m_w_in, m_conv_w, m_conv_b, m_dt_bias, m_a_log, m_d_skip, m_attn_sinks, m_g_ssm_norm, m_w_out_att, m_w_out_ssm, m_w_out, m_g_post, v_meta_tokens, v_g_pre, v_w_in, v_conv_w, v_conv_b, v_dt_bias, v_a_log, v_d_skip, v_attn_sinks, v_g_ssm_norm, v_w_out_att, v_w_out_ssm, v_w_out, v_g_post):
    chip = _chip_index()

    conv_w_rows = jnp.pad(conv_w[0], ((0, 2 * 8 - CONV_WIDTH), (0, 0)))
    w_in_t, m_w_in_t, v_w_in_t = w_in[0].T, m_w_in[0].T, v_w_in[0].T
    (h, u), (gathered_w_in, g_conv_w, g_meta) = prep(
        "gather_w_in", x, g_pre, TwoLevelGather([pack_w_in(w_in_t), conv_w_rows, meta_tokens]))
    w_all_t = unpack_w_in(gathered_w_in)
    cw_full = g_conv_w[:, :CONV_WIDTH].transpose(1, 0, 2).reshape(CONV_WIDTH, CONV_DIM)
    behind_w_in = 0.0 * g_meta[0, 0, 0]
    w_out_flight, w_out_started = chip_exchange_start(
        "gather_w_out_start", [(w[0] + behind_w_in).astype(BF16) for w in (w_out_att, w_out_ssm, w_out)], False)

    proj = project("in_proj", u, w_all_t, w_out_started)

    sinks3 = attn_sinks.reshape(ATT_Q_HEADS, 1, 1)
    a_att = attn_fwd(proj, sinks3)

    xbc = conv_fwd(proj, cw_full, conv_b)
    expand = _head_expand()
    head_pars = (_lane_tile(dt_bias), _lane_tile(a_log), _lane_tile(d_skip))
    y_ssd, states = ssd_fwd(xbc, proj, expand, *head_pars)
    woa, wos, wo = [g.reshape(-1, D_MODEL) for g in chip_exchange_wait("gather_w_out_wait", w_out_flight, False, y_ssd)]

    (yn, merged, dout, dy_att, dy_ssm, da_att, dy_ssd, dz_ssm, dga, dgs, dres, loss_tile, dg_post, dgn) = tail(
        y_ssd, proj, a_att, x, loss_target, woa, wos, wo, g_ssm_norm, g_post)

    dwo = mm_tn("out_proj_dw", merged, dout)
    dwoa = mm_tn("att_out_dw", a_att, dy_att)
    dwos = mm_tn("ssm_out_dw", yn, dy_ssm)
    dq, dz_att, dk, dv, dkmeta, dvmeta, dsinks3 = attn_bwd(da_att, proj, sinks3)
    dk = dk.at[PAD_ROWS:BLOCK].add(dkmeta).astype(BF16)
    dv = dv.at[PAD_ROWS:BLOCK].add(dvmeta).astype(BF16)

    def pieces(g):
        return g.reshape(4, 2, g.shape[0] // 8, g.shape[1])

    def to_owner(g):
        return (lambda ref, dev: ref.at[_chip_of(dev), dev[2]], (g.shape[0] // 8, g.shape[1]))

    (dxs, ddt_tile, dbg, dcg, dbias, dalog, ddsk), sent_w_out = ssd_bwd(
        dy_ssd, xbc, proj, expand, *head_pars, states,
        DirectExchange([pieces(dwoa), pieces(dwos), pieces(dwo)], [to_owner(dwoa), to_owner(dwos), to_owner(dwo)],
                       ALL_MASKS, "dev", 8))
    dxs_raw, dcw_xs, dcb_xs = conv_bwd("conv_bwd_x", [dxs], 0, proj, cw_full, conv_b)
    dbc_raw, dcw_bc, dcb_bc = conv_bwd("conv_bwd_bc", [dbg, dcg], SSM_INNER // D_MODEL, proj, cw_full, conv_b)
    dcw = jnp.concatenate([dcw_xs, dcw_bc], axis=1)
    dcb = jnp.concatenate([dcb_xs, dcb_bc], axis=1)

    narrow = jnp.concatenate([dk, dv, ddt_tile, jnp.zeros((dk.shape[0], N_ACT - N_ALIGNED), BF16)], axis=1)
    dproj = [dz_ssm, dxs_raw, dbc_raw, dq, dz_att, dga, dgs, narrow]
    by_shard = [7, 3, 4, 0, 1, 2, 5, 6]
    partial, from_sibling = weight_grad_t("in_proj_dw", dproj, by_shard, u)
    grads_flight, started = pair_sum_exchange_start(
        "reduce_w_in_start", partial.reshape(4, 2, PACK_W // 2, D_MODEL), from_sibling)
    du = project_back("in_proj_dx", dproj, w_all_t, started)
    grad_x, dmeta, dg_pre = prep_bwd(h, du, dres, g_pre)

    halves = [sum_slots("sum_" + nm, r) for nm, r in zip(("w_out_att", "w_out_ssm", "w_out"), sent_w_out)]
    shared = run_comm("share_w_out", DirectExchange(halves, [None] * 3, SIBLING_MASK, "core", 2))
    g_woa, g_wos, g_wo = [f.reshape(2 * f.shape[1], f.shape[2]) for f in shared]
    d_woa, nm_woa, nv_woa = adamw_rows("adamw_w_out_att", g_woa, w_out_att[0], m_w_out_att[0], v_w_out_att[0])
    d_wos, nm_wos, nv_wos = adamw_rows("adamw_w_out_ssm", g_wos, w_out_ssm[0], m_w_out_ssm[0], v_w_out_ssm[0])
    d_wo, nm_wo, nv_wo = adamw_rows("adamw_w_out", g_wo, w_out[0], m_w_out[0], v_w_out[0])

    sent_w_in, = chip_exchange_wait("reduce_w_in_wait", grads_flight, True, d_wos)
    slab = pack_small(dcw, dcb, dg_pre, dg_post, dbias, dalog, ddsk, _lane_tile(dsinks3.reshape(1, ATT_Q_HEADS)), dgn,
                      dmeta.reshape(SLAB_META_SHAPE), loss_tile)
    shared_w_in, slabs = run_comm("share_w_in", Both(
        DirectExchange([sum_slots("sum_w_in", sent_w_in)], [None], SIBLING_MASK, "core", 2),
        DirectExchange([slab], [None], ALL_MASKS, "dev", 8)))
    small = sum_slots("sum_small", slabs)
    loss = small[SLAB_LOSS_ROW, 0]
    g_w_in, d_w_in, nm_w_in, nv_w_in = [
        a.T for a in adamw_w_in(shared_w_in.reshape(PACK_W, D_MODEL), w_in_t, m_w_in_t, v_w_in_t)]

    cw_cols = CONV_DIM // 4
    meta_cols = D_MODEL // 4
    g_small = {
        "meta_tokens": lax.dynamic_slice(
            small[SLAB_META_ROW:SLAB_ROWS, 0:SLAB_META_SHAPE[1]].reshape(N_META, D_MODEL), (0, chip * meta_cols),
            (N_META, meta_cols)),
        "g_pre": small[5:6, 0:1024],
        "conv_w": lax.dynamic_slice(small, (0, chip * cw_cols), (CONV_WIDTH, cw_cols)),
        "conv_b": small[4:5, :],
        "dt_bias": small[5:6, 2048:2048 + SSM_HEADS],
        "a_log": small[5:6, 2176:2176 + SSM_HEADS],
        "d_skip": small[5:6, 2304:2304 + SSM_HEADS],
        "attn_sinks": small[5:6, 2432:2432 + ATT_Q_HEADS],
        "g_ssm_norm": small[6:7, 0:SSM_INNER],
        "g_post": small[5:6, 1024:2048],
    }
    names = list(g_small)
    w_small = dict(meta_tokens=meta_tokens, g_pre=g_pre, conv_w=conv_w[0], conv_b=conv_b, dt_bias=dt_bias, a_log=a_log,
                   d_skip=d_skip, attn_sinks=attn_sinks, g_ssm_norm=g_ssm_norm, g_post=g_post)
    m_small = dict(meta_tokens=m_meta_tokens, g_pre=m_g_pre, conv_w=m_conv_w[0], conv_b=m_conv_b, dt_bias=m_dt_bias,
                   a_log=m_a_log, d_skip=m_d_skip, attn_sinks=m_attn_sinks, g_ssm_norm=m_g_ssm_norm, g_post=m_g_post)
    v_small = dict(meta_tokens=v_meta_tokens, g_pre=v_g_pre, conv_w=v_conv_w[0], conv_b=v_conv_b, dt_bias=v_dt_bias,
                   a_log=v_a_log, d_skip=v_d_skip, attn_sinks=v_attn_sinks, g_ssm_norm=v_g_ssm_norm, g_post=v_g_post)
    upd = dict(zip(names, adamw_small([g_small[k] for k in names], [w_small[k] for k in names],
                                      [m_small[k] for k in names], [v_small[k] for k in names])))

    lead = {"conv_w"}

    def shaped(name, a):
        return a[None] if name in lead else a

    grads = dict(g_small, w_in=g_w_in, w_out_att=g_woa, w_out_ssm=g_wos, w_out=g_wo)
    deltas = dict({k: upd[k][0] for k in names}, w_in=d_w_in, w_out_att=d_woa, w_out_ssm=d_wos, w_out=d_wo)
    new_m = dict({k: upd[k][1] for k in names}, w_in=nm_w_in, w_out_att=nm_woa, w_out_ssm=nm_wos, w_out=nm_wo)
    new_v = dict({k: upd[k][2] for k in names}, w_in=nv_w_in, w_out_att=nv_woa, w_out_ssm=nv_wos, w_out=nv_wo)
    lead |= {"w_in", "w_out_att", "w_out_ssm", "w_out"}
    order = ["meta_tokens", "g_pre", "w_in", "conv_w", "conv_b", "dt_bias", "a_log", "d_skip", "attn_sinks",
             "g_ssm_norm", "w_out_att", "w_out_ssm", "w_out", "g_post"]
    outs = [loss, grad_x]
    for group in (grads, deltas, new_m, new_v):
        outs += [shaped(k, group[k]) for k in order]
    return tuple(outs)
```

```python
import numpy as np
import jax
import jax.numpy as jnp
from jax import lax
from jax.experimental import pallas as pl
from jax.experimental.pallas import tpu as pltpu

F32 = jnp.float32
BF16 = jnp.bfloat16

D_MODEL = 1024
N_META = 16
BLOCK = 128
PAD_ROWS = BLOCK - N_META
NORM_EPS = 1e-6
HEAD_DIM = 64
ATT_Q_HEADS = 16
ATT_KV_HEADS = 4
ATT_GROUP = 4
SSM_INNER = 2048
SSM_HEADS = 32
SSM_GROUPS = 4
SSM_STATE = 128
CONV_WIDTH = 4
CONV_DIM = 3072
LANES = 128

ADAM_LR = 0.001
ADAM_B1 = 0.9
ADAM_B2 = 0.999
ADAM_EPS = 1e-08
ADAM_WD = 0.01
ADAM_STEP = 10

VMEM_LIMIT = 48 * 1024 * 1024

SHARD_W = 2440
PACK_W = 2560
SHARD_STRIDE = 2432
N_ALIGNED = 9856
N_ACT = 10240
SEG = {
    "q": (0, 1024, 5120), "k": (1024, 256, 9216), "v": (1280, 256, 9472), "z_att": (1536, 1024, 6144),
    "z_ssm": (2560, 2048, 0), "xbc": (4608, 3072, 2048), "dt": (7680, 128, 9728),
    "gate_att": (7808, 1024, 7168), "gate_ssm": (8832, 1024, 8192),
}
DT_STORED_START = 7680
DT_PAD = LANES - SSM_HEADS


def _act_col(aligned_col):
    for a0, w, p0 in SEG.values():
        if a0 <= aligned_col < a0 + w:
            return p0 + aligned_col - a0
    raise ValueError(aligned_col)


def _call(body, *, name, out_shape, in_specs, out_specs, grid=(), scratch=(), sem=None, aliases=None):
    return pl.pallas_call(
        body, out_shape=out_shape, grid=grid, in_specs=in_specs, out_specs=out_specs, scratch_shapes=list(scratch),
        name=name, input_output_aliases=aliases or {},
        compiler_params=pltpu.CompilerParams(dimension_semantics=sem, vmem_limit_bytes=VMEM_LIMIT))


def _full(shape):
    n = len(shape)
    return pl.BlockSpec(shape, lambda *_: (0,) * n)


def _chip_index():
    return lax.axis_index("x") * 2 + lax.axis_index("y")


_sigmoid = jax.nn.sigmoid


def _silu(z):
    return z * _sigmoid(z)


def _rms(x, g):
    return x * lax.rsqrt(jnp.mean(x * x, axis=-1, keepdims=True) + NORM_EPS) * g


def _peer(mask):
    x, y, c = lax.axis_index("x"), lax.axis_index("y"), lax.axis_index("c")
    return ((1 - x) if mask & 4 else x, (1 - y) if mask & 2 else y, (1 - c) if mask & 1 else c)


def _me():
    return lax.axis_index("x"), lax.axis_index("y"), lax.axis_index("c")


def _chip_of(dev):
    return 2 * dev[0] + dev[1]


CHIP_MASKS = (4, 2, 6)
ALL_MASKS = (1, 2, 3, 4, 5, 6, 7)
SIBLING_MASK = (1,)


def _remote(src, dst, send_sem, recv_sem, dev):
    return pltpu.make_async_remote_copy(src_ref=src, dst_ref=dst, send_sem=send_sem, recv_sem=recv_sem,
                                        device_id=dev, device_id_type=pl.DeviceIdType.MESH)


class _StagedCopy:
    def __init__(self, src, stage, dst, load_sem, store_sem):
        self.load = pltpu.make_async_copy(src, stage, load_sem)
        self.store = pltpu.make_async_copy(stage, dst, store_sem)

    def start(self):
        self.load.start()
        self.load.wait()
        self.store.start()

    def wait(self):
        self.store.wait()


class DirectExchange:
    def __init__(self, arrays, pieces, masks, slot_kind, nslots, keep_own=True):
        self.arrays, self.pieces, self.masks, self.slot_kind = list(arrays), list(pieces), masks, slot_kind
        self.keep_own = keep_own
        n, nk = len(arrays), len(masks)
        shapes = [a.shape if p is None else p[1] for a, p in zip(arrays, pieces)]
        self.out_shape = [jax.ShapeDtypeStruct((nslots,) + tuple(s), a.dtype) for s, a in zip(shapes, arrays)]
        self.scratch = [pltpu.SemaphoreType.DMA((n * nk,)), pltpu.SemaphoreType.DMA((n * nk,))]
        if keep_own:
            self.scratch += [pltpu.SemaphoreType.DMA((2 * n,))] + [pltpu.VMEM(s, a.dtype) for s, a in zip(shapes, arrays)]
        self.has_mid = False

    def _copies(self, ins, outs, scratch):
        send_sems, recv_sems = scratch[:2]
        me = _me()
        slot = {"chip": _chip_of(me), "dev": 4 * me[0] + 2 * me[1] + me[2], "core": me[2]}[self.slot_kind]
        nk = len(self.masks)

        def piece(a, dev):
            return ins[a] if self.pieces[a] is None else self.pieces[a][0](ins[a], dev)

        local = []
        if self.keep_own:
            local_sems, stages = scratch[2], scratch[3:]
            local = [_StagedCopy(piece(a, me), stages[a], outs[a].at[slot], local_sems.at[2 * a], local_sems.at[2 * a + 1])
                     for a in range(len(ins))]
        remote = []
        for a in range(len(ins)):
            for ki, mask in enumerate(self.masks):
                dev = _peer(mask)
                remote.append(_remote(piece(a, dev), outs[a].at[slot], send_sems.at[a * nk + ki],
                                      recv_sems.at[a * nk + ki], dev))
        return local, remote

    def start(self, ins, outs, scratch):
        local, remote = self._copies(ins, outs, scratch)
        for cp in remote + local:
            cp.start()

    def finish(self, ins, outs, scratch):
        local, remote = self._copies(ins, outs, scratch)
        for cp in remote + local:
            cp.wait()


SPLIT_ROWS = 16


class TwoLevelGather:
    def __init__(self, arrays):
        self.arrays = list(arrays)
        n = len(arrays)
        self.out_shape = [jax.ShapeDtypeStruct((4,) + a.shape, a.dtype) for a in arrays]
        self.scratch = ([pltpu.SemaphoreType.DMA((4 * n,)), pltpu.SemaphoreType.DMA((4 * n,)),
                         pltpu.SemaphoreType.DMA((3 * n,)), pltpu.SemaphoreType.DMA((3 * n,)),
                         pltpu.SemaphoreType.DMA((2 * n,))] + [pltpu.VMEM(a.shape, a.dtype) for a in arrays])
        self.has_mid = True

    def _copies(self, ins, outs, scratch):
        ici_send, ici_recv, fwd_send, fwd_recv, local_sems = scratch[:5]
        stages = scratch[5:]
        me = _me()
        sibling, in_x, in_y, diagonal = _peer(1), _peer(4), _peer(2), _peer(6)
        plan = []
        for a in range(len(ins)):
            half = ins[a].shape[0] // 2
            first = half // 2 if half % (2 * SPLIT_ROWS) == 0 else half
            mine = pl.ds(me[2] * half, half)
            local = _StagedCopy(ins[a], stages[a], outs[a].at[_chip_of(me)], local_sems.at[2 * a], local_sems.at[2 * a + 1])

            def ici(k, src, dst, dev):
                return _remote(src, dst, ici_send.at[4 * a + k], ici_recv.at[4 * a + k], dev)

            def d2d(k, chip):
                zone = outs[a].at[_chip_of(chip), mine]
                return _remote(zone, zone, fwd_send.at[3 * a + k], fwd_recv.at[3 * a + k], sibling)

            own_zone = outs[a].at[_chip_of(me), mine]
            from_x = outs[a].at[_chip_of(in_x), pl.ds(me[2] * half, first)]
            onward = [ici(2, from_x, from_x, in_y), None]
            if first < half:
                from_y = outs[a].at[_chip_of(in_y), pl.ds(me[2] * half + first, half - first)]
                onward[1] = ici(3, from_y, from_y, in_x)
            plan.append(dict(
                local=local,
                own=[ici(0, ins[a].at[mine], own_zone, in_x), ici(1, ins[a].at[mine], own_zone, in_y)],
                onward=onward, sibling=[d2d(0, in_x), d2d(1, in_y), d2d(2, diagonal)]))
        return plan

    def start(self, ins, outs, scratch):
        for p in self._copies(ins, outs, scratch):
            for cp in p["own"]:
                cp.start()
            p["local"].start()

    def mid(self, ins, outs, scratch):
        plan = self._copies(ins, outs, scratch)
        for p in plan:
            for k in range(2):
                p["own"][k].wait_recv()
                if p["onward"][k] is not None:
                    p["onward"][k].start()
                p["sibling"][k].start()
        for p in plan:
            for cp in p["onward"]:
                if cp is not None:
                    cp.wait_recv()
            p["sibling"][2].start()

    def finish(self, ins, outs, scratch):
        for p in self._copies(ins, outs, scratch):
            for cp in p["sibling"]:
                cp.wait_recv()
            for cp in p["own"] + p["sibling"] + [cp for cp in p["onward"] if cp is not None]:
                cp.wait_send()
            p["local"].wait()


class Both:
    def __init__(self, a, b):
        self.a, self.b = a, b
        self.arrays, self.out_shape = a.arrays + b.arrays, a.out_shape + b.out_shape
        self.scratch = a.scratch + b.scratch
        self.has_mid = False
        assert not (a.has_mid or b.has_mid)

    def _parts(self, ins, outs, sems):
        na, sa = len(self.a.arrays), len(self.a.scratch)
        return (ins[:na], outs[:na], sems[:sa]), (ins[na:], outs[na:], sems[sa:])

    def start(self, ins, outs, sems):
        pa, pb = self._parts(ins, outs, sems)
        self.a.start(*pa)
        self.b.start(*pb)

    def finish(self, ins, outs, sems):
        pa, pb = self._parts(ins, outs, sems)
        self.a.finish(*pa)
        self.b.finish(*pb)


class PairOfSums:
    def __init__(self, array):
        s, rows, cols = array.shape
        self.arrays = [array]
        self.out_shape = [jax.ShapeDtypeStruct((2, rows, cols), F32)]
        self.scratch = [pltpu.SemaphoreType.DMA((s,)), pltpu.SemaphoreType.DMA((3,)), pltpu.VMEM(array.shape, array.dtype),
                        pltpu.VMEM((rows, cols), F32)]
        self.has_mid = False

    def _copies(self, ins, outs, scratch):
        load_sems, sems, slots, total = scratch
        mine = outs[0].at[lax.axis_index("c")]
        loads = [pltpu.make_async_copy(ins[0].at[k], slots.at[k], load_sems.at[k]) for k in range(slots.shape[0])]
        return loads, [_remote(total, mine, sems.at[0], sems.at[1], _peer(1)), pltpu.make_async_copy(total, mine, sems.at[2])]

    def start(self, ins, outs, scratch):
        loads, stores = self._copies(ins, outs, scratch)
        slots, total = scratch[2:]
        for cp in loads:
            cp.start()
        for cp in loads:
            cp.wait()
        acc = slots[0].astype(F32)
        for k in range(1, slots.shape[0]):
            acc = acc + slots[k].astype(F32)
        total[...] = acc
        for cp in stores:
            cp.start()

    def finish(self, ins, outs, scratch):
        for cp in self._copies(ins, outs, scratch)[1]:
            cp.wait()


_ANY = pl.BlockSpec(memory_space=pl.ANY)


def run_comm(name, comm):
    n = len(comm.arrays)

    def body(*refs):
        ins, outs, sems = refs[:n], refs[n:2 * n], refs[2 * n:]
        comm.start(ins, outs, sems)
        if comm.has_mid:
            comm.mid(ins, outs, sems)
        comm.finish(ins, outs, sems)

    return pl.pallas_call(body, name=name, out_shape=comm.out_shape, in_specs=[_ANY] * n, out_specs=[_ANY] * n,
                          scratch_shapes=comm.scratch,
                          compiler_params=pltpu.CompilerParams(vmem_limit_bytes=VMEM_LIMIT))(*comm.arrays)


_HBM = pl.BlockSpec(memory_space=pltpu.HBM)
_SEM = pl.BlockSpec(memory_space=pltpu.SEMAPHORE)
_SIDE_EFFECT = pltpu.SideEffectType.DATAFLOW_SIDE_EFFECTING


def _chip_copies(srcs, lands, send_sems, recv_sems, by_target):
    me = _me()
    copies = []
    for a, (src, land) in enumerate(zip(srcs, lands)):
        for ki, mask in enumerate(CHIP_MASKS):
            dev = _peer(mask)
            k = a * len(CHIP_MASKS) + ki
            piece = src.at[_chip_of(dev)] if by_target else src
            copies.append(_remote(piece, land.at[_chip_of(me)], send_sems.at[k], recv_sems.at[k], dev))
    return copies


def chip_exchange_start(name, arrays, by_target):
    n = len(arrays)
    nsem = n * len(CHIP_MASKS)
    piece_shapes = [a.shape[1:] if by_target else a.shape for a in arrays]

    def body(*refs):
        srcs, lands = refs[:n], refs[n:2 * n]
        send_sems, recv_sems = refs[2 * n:2 * n + 2]
        token = refs[4 * n + 2]
        stages, local_sems = refs[4 * n + 3:5 * n + 3], refs[5 * n + 3]
        me = _me()
        for cp in _chip_copies(srcs, lands, send_sems, recv_sems, by_target):
            cp.start()
        own = [_StagedCopy(srcs[a].at[_chip_of(me)] if by_target else srcs[a], stages[a], lands[a].at[_chip_of(me)],
                           local_sems.at[2 * a], local_sems.at[2 * a + 1]) for a in range(n)]
        for cp in own:
            cp.load.start()
        for cp in own:
            cp.load.wait()
            cp.store.start()
        for cp in own:
            cp.store.wait()
        token[...] = jnp.zeros_like(token)

    lands = [lax.empty((4,) + tuple(s), a.dtype) for s, a in zip(piece_shapes, arrays)]
    hbm = lambda a: pltpu.HBM(a.shape, a.dtype)
    res = pl.pallas_call(
        body, name=name,
        out_shape=(pltpu.SemaphoreType.DMA((nsem,)), pltpu.SemaphoreType.DMA((nsem,)), *[hbm(a) for a in arrays],
                   *[hbm(l) for l in lands], jax.ShapeDtypeStruct((8, LANES), F32)),
        in_specs=(_HBM,) * (2 * n), out_specs=(_SEM, _SEM) + (_HBM,) * (2 * n) + (pl.BlockSpec(memory_space=pltpu.VMEM),),
        input_output_aliases={i: i + 2 for i in range(2 * n)},
        scratch_shapes=[pltpu.VMEM(tuple(s), a.dtype) for s, a in zip(piece_shapes, arrays)]
        + [pltpu.SemaphoreType.DMA((2 * n,))],
        compiler_params=pltpu.CompilerParams(has_side_effects=_SIDE_EFFECT, vmem_limit_bytes=VMEM_LIMIT),
    )(*[pltpu.with_memory_space_constraint(a, pltpu.HBM) for a in arrays + lands])
    return res[:-1], res[-1]


def pair_sum_exchange_start(name, partial, from_sibling):
    shards, _, rows, cols = partial.shape
    nsem = len(CHIP_MASKS)

    def body(p_ref, r_ref, _, send_sems, recv_sems, sum_ref, land_ref, token, p_buf, r_buf, o_buf, sems):
        me = _me()
        targets = [_chip_of(_peer(mask)) for mask in CHIP_MASKS] + [_chip_of(me)]
        remote = _chip_copies([sum_ref], [land_ref], send_sems, recv_sems, True)

        def loads(k):
            return [pltpu.make_async_copy(p_ref.at[targets[k], me[2]], p_buf.at[k % 2], sems.at[2 * (k % 2)]),
                    pltpu.make_async_copy(r_ref.at[targets[k]], r_buf.at[k % 2], sems.at[2 * (k % 2) + 1])]

        for cp in loads(0):
            cp.start()
        for k in range(shards):
            if k + 1 < shards:
                for cp in loads(k + 1):
                    cp.start()
            for cp in loads(k):
                cp.wait()
            o_buf[k % 2] = (p_buf[k % 2].astype(F32) + r_buf[k % 2].astype(F32)).astype(BF16)
            summed = sum_ref.at[targets[k]] if k < nsem else land_ref.at[targets[k]]
            store = pltpu.make_async_copy(o_buf.at[k % 2], summed, sems.at[4])
            store.start()
            store.wait()
            if k < nsem:
                remote[k].start()
        token[...] = jnp.zeros_like(token)

    piece = jax.ShapeDtypeStruct((shards, rows, cols), BF16)
    land = lax.empty(piece.shape, BF16)
    buf = pltpu.VMEM((2, rows, cols), BF16)
    res = pl.pallas_call(
        body, name=name,
        out_shape=(pltpu.SemaphoreType.DMA((nsem,)), pltpu.SemaphoreType.DMA((nsem,)), pltpu.HBM(piece.shape, BF16),
                   pltpu.HBM(piece.shape, BF16), jax.ShapeDtypeStruct((8, LANES), F32)),
        in_specs=(_HBM,) * 3, out_specs=(_SEM, _SEM, _HBM, _HBM, pl.BlockSpec(memory_space=pltpu.VMEM)),
        input_output_aliases={2: 3},
        scratch_shapes=[buf, buf, buf, pltpu.SemaphoreType.DMA((5,))],
        compiler_params=pltpu.CompilerParams(has_side_effects=_SIDE_EFFECT, vmem_limit_bytes=VMEM_LIMIT),
    )(*[pltpu.with_memory_space_constraint(a, pltpu.HBM) for a in (partial, from_sibling, land)])
    return res[:-1], res[-1]


def chip_exchange_wait(name, in_flight, by_target, after):
    send_sems, recv_sems, *thru = in_flight
    n = len(thru) // 2

    def body(*refs):
        srcs, lands, (send, recv) = refs[:n], refs[n:2 * n], refs[2 * n:2 * n + 2]
        for cp in _chip_copies(srcs, lands, send, recv, by_target):
            cp.wait_send()
            cp.wait_recv()

    return pl.pallas_call(
        body, name=name, out_shape=tuple(pltpu.HBM(t.shape, t.dtype) for t in thru),
        in_specs=(_HBM,) * (2 * n) + (_SEM, _SEM, pl.BlockSpec(memory_space=pl.ANY)), out_specs=(_HBM,) * (2 * n),
        input_output_aliases={i: i for i in range(2 * n)},
        compiler_params=pltpu.CompilerParams(has_side_effects=_SIDE_EFFECT),
    )(*thru, send_sems, recv_sems, after)[n:]


def _call_with_comm(body, comm, steps, args, *, name, out_shape, in_specs, out_specs, grid, scratch=(),
                    after_finish=None):
    ni, no, ns, nc = len(in_specs), len(out_specs), len(scratch), len(comm.arrays)

    def full_body(*refs):
        ins, cins = refs[:ni], refs[ni:ni + nc]
        outs, couts = refs[ni + nc:ni + nc + no], refs[ni + nc + no:ni + 2 * nc + no]
        scr, csems = refs[ni + 2 * nc + no:ni + 2 * nc + no + ns], refs[ni + 2 * nc + no + ns:]
        first, middle, last = steps()
        pl.when(first)(lambda: comm.start(cins, couts, csems))
        if comm.has_mid:
            pl.when(middle)(lambda: comm.mid(cins, couts, csems))
        body(*ins, *outs, *scr)

        @pl.when(last)
        def _():
            comm.finish(cins, couts, csems)
            if after_finish is not None:
                after_finish(ins, couts, outs, scr)

    res = pl.pallas_call(
        full_body, name=name, out_shape=list(out_shape) + comm.out_shape, grid=grid,
        in_specs=list(in_specs) + [_ANY] * nc, out_specs=list(out_specs) + [_ANY] * nc,
        scratch_shapes=list(scratch) + comm.scratch,
        compiler_params=pltpu.CompilerParams(dimension_semantics=("arbitrary",) * len(grid),
                                             vmem_limit_bytes=VMEM_LIMIT))(*args, *comm.arrays)
    return res[:no], res[no:]


def _shard_pieces(chip):
    if chip < 3:
        return [(0, SHARD_W, 8 * chip)]
    behind_dt = DT_STORED_START + SSM_HEADS - 3 * SHARD_W
    return [(0, behind_dt, 24), (behind_dt, SHARD_W - behind_dt, behind_dt + 24 + DT_PAD)]


W_IN_COLS = 256


def pack_w_in(wt):
    def body(w_ref, o_ref, pad_ref):
        chip = _chip_index()
        pad_ref[...] = jnp.zeros_like(pad_ref)
        for cv in range(4):
            @pl.when(chip == cv)
            def _():
                for src, n, dst in _shard_pieces(cv):
                    pad_ref[dst:dst + n, :] = w_ref[src:src + n, :]
        o_ref[...] = pad_ref[...].astype(BF16)

    return _call(body, name="pack_w_in", grid=(D_MODEL // W_IN_COLS,),
                 in_specs=[pl.BlockSpec((SHARD_W, W_IN_COLS), lambda i: (0, i))],
                 out_specs=pl.BlockSpec((PACK_W, W_IN_COLS), lambda i: (0, i)),
                 out_shape=jax.ShapeDtypeStruct((PACK_W, D_MODEL), BF16),
                 scratch=[pltpu.VMEM((PACK_W, W_IN_COLS), F32)], sem=("parallel",))(wt)


def _tile_runs():
    runs, fix = [], []
    for t in range(N_ALIGNED // LANES):
        s = min(t // 19, 3)
        j = t - 19 * s
        p = _act_col(t * LANES)
        if runs and runs[-1][1] == s and runs[-1][0] + runs[-1][3] == p and runs[-1][2] + runs[-1][3] == j * LANES:
            runs[-1][3] += LANES
        else:
            runs.append([p, s, j * LANES, LANES])
        if j == 0 and s > 0:
            fix.append((p, s - 1))
    return runs, fix


def unpack_w_in(bg):
    runs, fix = _tile_runs()

    def body(b_ref, o_ref):
        for p, s, j, w in runs:
            o_ref[p:p + w, :] = b_ref[s, j:j + w, :]
        for p, s in fix:
            o_ref[p:p + LANES, :] = o_ref[p:p + LANES, :] + b_ref[s, SHARD_STRIDE:PACK_W, :]
        o_ref[N_ALIGNED:N_ACT, :] = jnp.zeros((N_ACT - N_ALIGNED, W_IN_COLS), BF16)

    return _call(body, name="unpack_w_in", grid=(D_MODEL // W_IN_COLS,),
                 in_specs=[pl.BlockSpec((4, PACK_W, W_IN_COLS), lambda i: (0, 0, i))],
                 out_specs=pl.BlockSpec((N_ACT, W_IN_COLS), lambda i: (0, i)),
                 out_shape=jax.ShapeDtypeStruct((N_ACT, D_MODEL), BF16), sem=("parallel",))(bg)


def _adamw(w, g, m, v):
    m = ADAM_B1 * m + (1.0 - ADAM_B1) * g
    v = ADAM_B2 * v + (1.0 - ADAM_B2) * jnp.square(g)
    m_hat = m / (1.0 - ADAM_B1 ** ADAM_STEP)
    v_hat = v / (1.0 - ADAM_B2 ** ADAM_STEP)
    delta = -ADAM_LR * (m_hat / (jnp.sqrt(v_hat) + ADAM_EPS) + ADAM_WD * w)
    return delta, m, v


def adamw_w_in(g_packed, wt, mt, vt):
    cols = LANES

    def body(g_ref, w_ref, m_ref, v_ref, go_ref, d_ref, mo_ref, vo_ref):
        chip = _chip_index()
        for cv in range(4):
            @pl.when(chip == cv)
            def _():
                for dst, n, src in _shard_pieces(cv):
                    go_ref[dst:dst + n, :] = g_ref[src:src + n, :]
        d_ref[...], mo_ref[...], vo_ref[...] = _adamw(w_ref[...], go_ref[...], m_ref[...], v_ref[...])

    spec = pl.BlockSpec((SHARD_W, cols), lambda i: (0, i))
    shp = jax.ShapeDtypeStruct((SHARD_W, D_MODEL), F32)
    return _call(body, name="adamw_w_in", grid=(D_MODEL // cols,),
                 in_specs=[pl.BlockSpec((PACK_W, cols), lambda i: (0, i)), spec, spec, spec],
                 out_specs=[spec] * 4, out_shape=[shp] * 4, sem=("parallel",))(g_packed, wt, mt, vt)


def adamw_rows(name, g, w, m, v):
    r, c = g.shape
    rows = min(r, BLOCK)

    def body(g_ref, w_ref, m_ref, v_ref, d_ref, mo_ref, vo_ref):
        d_ref[...], mo_ref[...], vo_ref[...] = _adamw(w_ref[...], g_ref[...], m_ref[...], v_ref[...])

    spec = pl.BlockSpec((rows, c), lambda i: (i, 0))
    shp = jax.ShapeDtypeStruct((r, c), F32)
    return _call(body, name=name, grid=(r // rows,), in_specs=[spec] * 4, out_specs=[spec] * 3, out_shape=[shp] * 3,
                 sem=("parallel",))(g, w, m, v)


def adamw_small(gs, ws, ms, vs):
    n = len(gs)

    def body(*refs):
        g, w, m, v = refs[:n], refs[n:2 * n], refs[2 * n:3 * n], refs[3 * n:4 * n]
        outs = refs[4 * n:]
        for i in range(n):
            d, mn, vn = _adamw(w[i][...], g[i][...], m[i][...], v[i][...])
            outs[3 * i][...] = d
            outs[3 * i + 1][...] = mn
            outs[3 * i + 2][...] = vn

    specs = [_full(a.shape) for a in gs]
    res = _call(body, name="adamw_small", in_specs=specs * 4,
                out_specs=[s for s in specs for _ in range(3)],
                out_shape=[jax.ShapeDtypeStruct(a.shape, F32) for a in gs for _ in range(3)])(*gs, *ws, *ms, *vs)
    return [tuple(res[3 * i:3 * i + 3]) for i in range(n)]


def sum_slots(name, r):
    s, rr, c = r.shape
    rows = min(rr, BLOCK)

    def body(r_ref, o_ref):
        acc = r_ref[0].astype(F32)
        for k in range(1, s):
            acc = acc + r_ref[k].astype(F32)
        o_ref[...] = acc

    return _call(body, name=name, grid=(rr // rows,), in_specs=[pl.BlockSpec((s, rows, c), lambda i: (0, i, 0))],
                 out_specs=pl.BlockSpec((rows, c), lambda i: (i, 0)), out_shape=jax.ShapeDtypeStruct((rr, c), F32),
                 sem=("parallel",))(r)


def sum_pair(partial, from_sibling):
    s, _, rr, cols = partial.shape
    rows = rr // 2

    def body(c_ref, p_ref, r_ref, o_ref):
        o_ref[...] = (p_ref[...].astype(F32) + r_ref[...].astype(F32)).astype(BF16)

    core = lax.axis_index("c").astype(jnp.int32).reshape(1)
    return pl.pallas_call(
        body, name="sum_pair", out_shape=jax.ShapeDtypeStruct((s, rr, cols), BF16),
        grid_spec=pltpu.PrefetchScalarGridSpec(
            num_scalar_prefetch=1, grid=(s, rr // rows),
            in_specs=[pl.BlockSpec((None, None, rows, cols), lambda k, i, c: (k, c[0], i, 0)),
                      pl.BlockSpec((None, rows, cols), lambda k, i, c: (k, i, 0))],
            out_specs=pl.BlockSpec((None, rows, cols), lambda k, i, c: (k, i, 0))),
        compiler_params=pltpu.CompilerParams(dimension_semantics=("parallel", "parallel"),
                                             vmem_limit_bytes=VMEM_LIMIT))(core, partial, from_sibling)


def _col_tile(n, k):
    if n % 896 == 0 and k <= 1024:
        return 896
    return min(n, 512)


def project(name, x, wt, after):
    m, k = x.shape
    n = wt.shape[0]
    tn = D_MODEL

    def body(x_ref, w_ref, after_ref, o_ref):
        o_ref[...] = lax.dot_general(x_ref[...], w_ref[...], _NT, preferred_element_type=F32)

    return _call(body, name=name, grid=(n // tn,),
                 in_specs=[_full((m, k)), pl.BlockSpec((tn, k), lambda j: (j, 0)), _full(after.shape)],
                 out_specs=pl.BlockSpec((m, tn), lambda j: (0, j)), out_shape=jax.ShapeDtypeStruct((m, n), F32),
                 sem=("parallel",))(x, wt, after)


def _piece_tiles(pieces, width):
    spans, start = [], 0
    for p in pieces:
        spans.append((start, p.shape[1] // width))
        start += p.shape[1] // width
    return spans, start


def _piece_spec(block, span, rows_of, tile_of):
    first, count = span

    def index(*pos):
        t = tile_of(*pos) - first
        mine = (t >= 0) & (t < count)
        return jnp.where(mine, rows_of(*pos), 0), jnp.clip(t, 0, count - 1)

    return pl.BlockSpec(block, index)


def project_back(name, pieces, wt, after):
    m = pieces[0].shape[0]
    k = wt.shape[1]
    tm = m // 2
    spans, steps = _piece_tiles(pieces, D_MODEL)

    def body(*refs):
        w_ref, o_ref = refs[len(pieces)], refs[len(pieces) + 2]
        j = pl.program_id(1)

        @pl.when(j == 0)
        def _():
            o_ref[...] = jnp.zeros_like(o_ref)

        for dy_ref, (first, count) in zip(refs, spans):
            @pl.when((j >= first) & (j < first + count))
            def _():
                o_ref[...] += jnp.dot(dy_ref[...], w_ref[...], preferred_element_type=F32)

    return _call(body, name=name, grid=(m // tm, steps),
                 in_specs=[_piece_spec((tm, D_MODEL), s, lambda i, j: i, lambda i, j: j) for s in spans]
                 + [pl.BlockSpec((D_MODEL, k), lambda i, j: (j, 0)), _full(after.shape)],
                 out_specs=pl.BlockSpec((tm, k), lambda i, j: (i, 0)), out_shape=jax.ShapeDtypeStruct((m, k), F32),
                 sem=("parallel", "arbitrary"))(*pieces, wt, after)


GRAD_TILE = 512


def _slab_runs():
    runs = [[] for _ in range(N_ACT // GRAD_TILE)]
    for s in range(4):
        for j in range(PACK_W // LANES):
            tile, r = divmod(_act_col((19 * s + j) * LANES), GRAD_TILE)
            last = runs[tile][-1] if runs[tile] else None
            if last and last[2] == s and last[0] + last[1] == r and last[3] + last[1] == j * LANES:
                last[1] += LANES
            else:
                runs[tile].append([r, LANES, s, j * LANES])
    return runs


def weight_grad_t(name, pieces, order, x):
    m, k = x.shape
    spans, steps = _piece_tiles(pieces, GRAD_TILE)
    runs = _slab_runs()
    most = max(len(r) for r in runs)
    half_rows = PACK_W // 2
    when, tile_at = [None] * len(pieces), []
    for p in order:
        when[p] = (len(tile_at), spans[p][1])
        tile_at += range(spans[p][0], spans[p][0] + spans[p][1])

    landed = {}
    for step in range(steps):
        for _, _, s, _ in runs[tile_at[step]]:
            landed[s] = step + 2

    def body(*refs):
        x_ref, o_ref, land_ref, tile_ref, sems, send_sems, recv_sems = refs[len(pieces):]
        i = pl.program_id(0)
        other = 1 - lax.axis_index("c")

        def copies(step):
            return [pltpu.make_async_copy(tile_ref.at[step % 2, r:r + n], o_ref.at[s, d:d + n],
                                          sems.at[(step % 2) * most + c])
                    for c, (r, n, s, d) in enumerate(runs[tile_at[step]])]

        def to_sibling(s):
            return _remote(o_ref.at[s, pl.ds(other * half_rows, half_rows)], land_ref.at[s], send_sems.at[s],
                           recv_sems.at[s], _peer(1))

        for step in range(2, steps):
            @pl.when(i == step)
            def _():
                for cp in copies(step - 2):
                    cp.wait()

        for dy_ref, (first, count) in zip(refs, when):
            @pl.when((i >= first) & (i < first + count))
            def _():
                tile_ref[i % 2] = lax.dot_general(dy_ref[...], x_ref[...], (((0,), (0,)), ((), ())),
                                                  preferred_element_type=F32).astype(BF16)

        for step in range(steps):
            @pl.when(i == step)
            def _():
                for cp in copies(step):
                    cp.start()
                if step < steps - 1:
                    for s in range(4):
                        if landed[s] == step:
                            to_sibling(s).start()
                else:
                    for cp in copies(step - 1) + copies(step):
                        cp.wait()
                    for s in range(4):
                        if landed[s] >= steps - 1:
                            to_sibling(s).start()
                    for s in range(4):
                        to_sibling(s).wait()

    return _call(body, name=name, grid=(steps,),
                 in_specs=[_piece_spec((m, GRAD_TILE), s, lambda i: 0, lambda i: i) for s in when]
                 + [pl.BlockSpec(memory_space=pltpu.VMEM)],
                 out_specs=[_ANY, _ANY],
                 out_shape=[jax.ShapeDtypeStruct((4, PACK_W, k), BF16), jax.ShapeDtypeStruct((4, half_rows, k), BF16)],
                 scratch=[pltpu.VMEM((2, GRAD_TILE, k), BF16), pltpu.SemaphoreType.DMA((2 * most,)),
                          pltpu.SemaphoreType.DMA((4,)), pltpu.SemaphoreType.DMA((4,))],
                 sem=("arbitrary",))(*pieces, x)


def mm_tn(name, x, dy):
    m, k = x.shape
    n = dy.shape[1]
    tn = _col_tile(n, k)

    def body(x_ref, dy_ref, o_ref):
        o_ref[...] = lax.dot_general(x_ref[...], dy_ref[...], (((0,), (0,)), ((), ())),
                                     preferred_element_type=F32).astype(BF16)

    return _call(body, name=name, grid=(n // tn,),
                 in_specs=[_full((m, k)), pl.BlockSpec((m, tn), lambda i: (0, i))],
                 out_specs=pl.BlockSpec((k, tn), lambda i: (0, i)), out_shape=jax.ShapeDtypeStruct((k, n), BF16),
                 sem=("parallel",))(x, dy)


def _row_spec(width, col_block=0):
    return pl.BlockSpec((BLOCK, width), lambda i: (i, col_block))


def _x_spec():
    return pl.BlockSpec((None, BLOCK, D_MODEL), lambda i: (0, jnp.maximum(i - 1, 0), 0))


def prep(name, x, g_pre, gather):
    nb = x.shape[1] // BLOCK + 1
    shards, _, shard_cols = gather.out_shape[-1].shape

    def body(x_ref, g_ref, h_ref, u_ref, meta_ref, sems):
        @pl.when(pl.program_id(0) < nb - 1)
        def _():
            h_ref[...] = x_ref[...]
            u_ref[...] = _rms(x_ref[...], g_ref[...]).astype(BF16)

    def first_block(ins, gathered, outs, scratch):
        g_ref, (h_ref, u_ref), (meta_ref, sems) = ins[1], outs, scratch
        copies = [pltpu.make_async_copy(gathered[-1].at[t], meta_ref.at[t], sems.at[t]) for t in range(shards)]
        for cp in copies:
            cp.start()
        for cp in copies:
            cp.wait()
        h_ref[0:PAD_ROWS, :] = jnp.zeros((PAD_ROWS, D_MODEL), F32)
        h_ref[PAD_ROWS:BLOCK, :] = jnp.concatenate([meta_ref[t] for t in range(shards)], axis=1)
        u_ref[...] = _rms(h_ref[...], g_ref[...]).astype(BF16)

    def at():
        i = pl.program_id(0)
        return i == 0, i == nb // 2, i == nb - 1

    rows = pl.BlockSpec((BLOCK, D_MODEL), lambda i: ((i + 1) % nb, 0))
    return _call_with_comm(
        body, gather, at, (x, g_pre), name=name, grid=(nb,),
        in_specs=[pl.BlockSpec((None, BLOCK, D_MODEL), lambda i: (0, jnp.minimum(i, nb - 2), 0)), _full((1, D_MODEL))],
        out_specs=[rows, rows],
        out_shape=[jax.ShapeDtypeStruct((nb * BLOCK, D_MODEL), F32), jax.ShapeDtypeStruct((nb * BLOCK, D_MODEL), BF16)],
        scratch=[pltpu.VMEM((shards, N_META, shard_cols), F32), pltpu.SemaphoreType.DMA((shards,))],
        after_finish=first_block)


def prep_bwd(h, du, dres, g_pre):
    seq = h.shape[0] - BLOCK
    rows = min(seq, 4 * BLOCK)

    def body(h0_ref, du0_ref, h_ref, du_ref, dres_ref, g_ref, gx_ref, gm_ref, gg_ref):
        i = pl.program_id(0)
        _, vjp = jax.vjp(_rms, h_ref[...], g_ref[...])
        dh, dg = vjp(du_ref[...])
        gx_ref[...] = dh + dres_ref[...]

        @pl.when(i == 0)
        def _():
            _, vjp0 = jax.vjp(_rms, h0_ref[...], g_ref[...])
            dh0, dg0 = vjp0(du0_ref[...])
            gm_ref[...] = dh0[PAD_ROWS:BLOCK, :]
            gg_ref[...] = dg0 + dg

        @pl.when(i > 0)
        def _():
            gg_ref[...] += dg

    seq_rows = pl.BlockSpec((pl.Element(rows), pl.Element(D_MODEL)), lambda i: (pl.multiple_of(BLOCK + rows * i, BLOCK), 0))
    first = pl.BlockSpec((BLOCK, D_MODEL), lambda i: (0, 0))
    return _call(body, name="prep_bwd", grid=(seq // rows,),
                 in_specs=[first, first, seq_rows, seq_rows, seq_rows, _full((1, D_MODEL))],
                 out_specs=[pl.BlockSpec((None, rows, D_MODEL), lambda i: (0, i, 0)), _full((N_META, D_MODEL)),
                            _full((1, D_MODEL))],
                 out_shape=[jax.ShapeDtypeStruct((1, seq, D_MODEL), F32),
                            jax.ShapeDtypeStruct((N_META, D_MODEL), F32), jax.ShapeDtypeStruct((1, D_MODEL), F32)],
                 sem=("arbitrary",))(h, du, h, du, dres, g_pre)


GROUP_W = SSM_INNER // SSM_GROUPS


def _gated_norm(y, z, g):
    t = y * _silu(z)
    return t * lax.rsqrt(jnp.mean(t * t, axis=-1, keepdims=True) + NORM_EPS) * g


def _gated_norm_groups(y, z, g):
    groups = [slice(k * GROUP_W, (k + 1) * GROUP_W) for k in range(SSM_GROUPS)]
    return jnp.concatenate([_gated_norm(y[:, s], z[:, s], g[:, s]) for s in groups], axis=1)


def _merge(ga, gs, ya, ys):
    return _sigmoid(ga) * ya + _sigmoid(gs) * ys


GATE_ATT_BLOCK = SEG["gate_att"][2] // D_MODEL
GATE_SSM_BLOCK = SEG["gate_ssm"][2] // D_MODEL


def _row_loss(out, g_post, x, target):
    diff = x + _rms(out, g_post) - target
    return 0.5 * jnp.sum(diff * diff) / D_MODEL


def tail(y_ssd, proj, a_att, x, target, woa, wos, wo, g_norm, g_post):
    nb = y_ssd.shape[0] // BLOCK
    rows = nb * BLOCK

    def body(y_ref, z_ref, ga_ref, gs_ref, a_ref, x_ref, t_ref, woa_ref, wos_ref, wo_ref, gn_ref, gp_ref,
             yn_ref, mg_ref, dout_ref, dya_ref, dys_ref, da_ref, dy_ref, dz_ref, dga_ref, dgs_ref, dres_ref,
             loss_ref, dgp_ref, dgn_ref):
        i = pl.program_id(0)
        yn, norm_vjp = jax.vjp(_gated_norm_groups, y_ref[...], z_ref[...], gn_ref[...])
        yn16 = yn.astype(BF16)
        y_ssm = jnp.dot(yn16, wos_ref[...], preferred_element_type=F32)
        y_att = jnp.dot(a_ref[...], woa_ref[...], preferred_element_type=F32)
        merged, merge_vjp = jax.vjp(_merge, ga_ref[...], gs_ref[...], y_att, y_ssm)
        merged16 = merged.astype(BF16)
        out = jnp.dot(merged16, wo_ref[...], preferred_element_type=F32)
        loss, loss_vjp = jax.vjp(_row_loss, out, gp_ref[...], x_ref[...], t_ref[...])
        counted = jnp.where(i > 0, 1.0, 0.0)
        dout, dgp, dres, _ = loss_vjp(counted)
        dout16 = dout.astype(BF16)
        dmerged = lax.dot_general(dout16, wo_ref[...], _NT, preferred_element_type=F32)
        dga, dgs, dya, dys = merge_vjp(dmerged)
        dya16, dys16 = dya.astype(BF16), dys.astype(BF16)
        dyn = lax.dot_general(dys16, wos_ref[...], _NT, preferred_element_type=F32)
        dy, dz, dgn = norm_vjp(dyn)

        yn_ref[...] = yn16
        mg_ref[...] = merged16
        dout_ref[...] = dout16
        dya_ref[...] = dya16
        dys_ref[...] = dys16
        da_ref[...] = lax.dot_general(dya16, woa_ref[...], _NT, preferred_element_type=F32)
        dy_ref[...] = dy
        dz_ref[...] = dz.astype(BF16)
        dga_ref[...] = dga.astype(BF16)
        dgs_ref[...] = dgs.astype(BF16)
        dres_ref[...] = dres

        @pl.when(i == 0)
        def _():
            loss_ref[...] = jnp.zeros_like(loss_ref)
            dgp_ref[...] = jnp.zeros_like(dgp_ref)
            dgn_ref[...] = jnp.zeros_like(dgn_ref)

        loss_ref[...] += loss * counted
        dgp_ref[...] += dgp
        dgn_ref[...] += dgn

    wide, narrow = _row_spec(SSM_INNER), _row_spec(D_MODEL)
    resident = pl.BlockSpec(memory_space=pltpu.VMEM)
    bf = lambda w: jax.ShapeDtypeStruct((rows, w), BF16)
    f32 = lambda w: jax.ShapeDtypeStruct((rows, w), F32)
    return _call(body, name="tail", grid=(nb,),
                 in_specs=[wide, wide, _row_spec(D_MODEL, GATE_ATT_BLOCK), _row_spec(D_MODEL, GATE_SSM_BLOCK), narrow,
                           _x_spec(), _x_spec(), resident, resident, resident, _full((1, SSM_INNER)),
                           _full((1, D_MODEL))],
                 out_specs=[wide, narrow, narrow, narrow, narrow, narrow, wide, wide, narrow, narrow, narrow,
                            _full((8, LANES)), _full((1, D_MODEL)), _full((1, SSM_INNER))],
                 out_shape=[bf(SSM_INNER), bf(D_MODEL), bf(D_MODEL), bf(D_MODEL), bf(D_MODEL), f32(D_MODEL),
                            f32(SSM_INNER), bf(SSM_INNER), bf(D_MODEL), bf(D_MODEL), f32(D_MODEL),
                            jax.ShapeDtypeStruct((8, LANES), F32), jax.ShapeDtypeStruct((1, D_MODEL), F32),
                            jax.ShapeDtypeStruct((1, SSM_INNER), F32)],
                 sem=("arbitrary",))(y_ssd, proj, proj, proj, a_att, x, target, woa, wos, wo, g_norm, g_post)


_NT = (((1,), (1,)), ((), ()))
ALIBI_SLOPES = tuple(2.0 ** (-8.0 * (h + 1) / ATT_Q_HEADS) for h in range(ATT_Q_HEADS))
KV_WIDTH = ATT_KV_HEADS * HEAD_DIM
Q_BLOCK = SEG["q"][2] // D_MODEL
Z_ATT_BLOCK = SEG["z_att"][2] // D_MODEL
K_BLOCK = SEG["k"][2] // KV_WIDTH
V_BLOCK = SEG["v"][2] // KV_WIDTH
META_ROW_BLOCK = PAD_ROWS // N_META


@jax.custom_vjp
def _swap_halves(x):
    return pltpu.roll(x, HEAD_DIM, 1)


_swap_halves.defvjp(lambda x: (pltpu.roll(x, HEAD_DIM, 1), None), lambda _, g: (pltpu.roll(g, HEAD_DIM, 1),))


def _both_halves(t, half):
    first = lax.broadcasted_iota(jnp.int32, t.shape, 1) < HEAD_DIM
    sw = _swap_halves(t)
    return jnp.where(first, t, sw) if half == 0 else jnp.where(first, sw, t)


def _attn_rows(q, z, kp, kc, vp, vc, km, vm, sinks, n):
    rows = ATT_GROUP * BLOCK
    i = lax.broadcasted_iota(jnp.int32, (rows, BLOCK), 0) & (BLOCK - 1)
    j = lax.broadcasted_iota(jnp.int32, (rows, BLOCK), 1)
    rel_c = (i - j).astype(F32)
    rel_p = rel_c + float(BLOCK)
    nv = jnp.zeros((rows, BLOCK), jnp.int32) + n
    ok_c = (i >= j) & (nv >= 1)
    ok_p = (j > i) & (nv >= 2)
    im = lax.broadcasted_iota(jnp.int32, (rows, N_META), 0) & (BLOCK - 1)
    jm = lax.broadcasted_iota(jnp.int32, (rows, N_META), 1)
    ok_m = ((jnp.zeros((rows, N_META), jnp.int32) + n) >= 1) | (im >= PAD_ROWS + jm)
    first = lax.broadcasted_iota(jnp.int32, (BLOCK, LANES), 1) < HEAD_DIM
    neg = -jnp.inf
    outs = []
    for kv in range(ATT_KV_HEADS):
        tile, half = divmod(kv, 2)
        lanes = slice(tile * LANES, (tile + 1) * LANES)
        kc2, kp2, km2 = (_both_halves(t[:, lanes], half).astype(BF16) for t in (kc, kp, km))
        vc2, vp2, vm2 = (_both_halves(t[:, lanes], half).astype(BF16) for t in (vc, vp, vm))
        qs, slope, sk = [], [], []
        for pair in range(ATT_GROUP // 2):
            c0 = (kv * ATT_GROUP + 2 * pair) * HEAD_DIM
            qp = q[:, c0:c0 + LANES] * HEAD_DIM ** -0.5
            qs += [jnp.where(first, qp, 0.0), jnp.where(first, 0.0, qp)]
        for g in range(ATT_GROUP):
            slope.append(jnp.full((BLOCK, 1), ALIBI_SLOPES[kv * ATT_GROUP + g], F32))
            sk.append(jnp.broadcast_to(sinks[kv * ATT_GROUP + g], (BLOCK, 1)))
        qs = jnp.concatenate(qs, axis=0).astype(BF16)
        slope = jnp.concatenate(slope, axis=0)
        sk = jnp.concatenate(sk, axis=0)
        sc = jnp.where(ok_c, lax.dot_general(qs, kc2, _NT, preferred_element_type=F32) - slope * rel_c, neg)
        sp = jnp.where(ok_p, lax.dot_general(qs, kp2, _NT, preferred_element_type=F32) - slope * rel_p, neg)
        sm = jnp.where(ok_m, lax.dot_general(qs, km2, _NT, preferred_element_type=F32), neg)
        mx = jnp.maximum(jnp.maximum(jnp.max(sc, axis=1, keepdims=True), jnp.max(sp, axis=1, keepdims=True)),
                         jnp.maximum(jnp.max(sm, axis=1, keepdims=True), sk))
        mx = lax.stop_gradient(mx)
        ec, ep, em, es = jnp.exp(sc - mx), jnp.exp(sp - mx), jnp.exp(sm - mx), jnp.exp(sk - mx)
        den = (es + jnp.sum(ec, axis=1, keepdims=True) + jnp.sum(ep, axis=1, keepdims=True)
               + jnp.sum(em, axis=1, keepdims=True))
        inv = 1.0 / den
        o = (jnp.dot((ec * inv).astype(BF16), vc2, preferred_element_type=F32)
             + jnp.dot((ep * inv).astype(BF16), vp2, preferred_element_type=F32)
             + jnp.dot((em * inv).astype(BF16), vm2, preferred_element_type=F32))
        for pair in range(ATT_GROUP // 2):
            r0 = 2 * pair * BLOCK
            outs.append(jnp.where(first, o[r0:r0 + BLOCK], o[r0 + BLOCK:r0 + 2 * BLOCK]))
    return jnp.concatenate(outs, axis=1) * _silu(z)


def _attn_specs(nb, steps_clamped):
    def blk(t):
        return jnp.minimum(t, nb - 1) if steps_clamped else t

    wide = lambda col: pl.BlockSpec((BLOCK, D_MODEL), lambda t: (blk(t), col))
    cur = lambda col: pl.BlockSpec((BLOCK, KV_WIDTH), lambda t: (blk(t), col))
    prev = lambda col: pl.BlockSpec((BLOCK, KV_WIDTH), lambda t: (jnp.maximum(blk(t) - 1, 0), col))
    meta = lambda col: pl.BlockSpec((N_META, KV_WIDTH), lambda t: (META_ROW_BLOCK, col))
    sinks = pl.BlockSpec((ATT_Q_HEADS, 1, 1), lambda t: (0, 0, 0))
    return [wide(Q_BLOCK), wide(Z_ATT_BLOCK), prev(K_BLOCK), cur(K_BLOCK), prev(V_BLOCK), cur(V_BLOCK),
            meta(K_BLOCK), meta(V_BLOCK), sinks]


def attn_fwd(proj, sinks):
    nb = proj.shape[0] // BLOCK

    def body(q_ref, z_ref, kp_ref, kc_ref, vp_ref, vc_ref, km_ref, vm_ref, sk_ref, o_ref):
        o_ref[...] = _attn_rows(q_ref[...], z_ref[...], kp_ref[...], kc_ref[...], vp_ref[...], vc_ref[...],
                                km_ref[...], vm_ref[...], tuple(sk_ref[h] for h in range(ATT_Q_HEADS)),
                                pl.program_id(0)).astype(BF16)

    return _call(body, name="attn_fwd", grid=(nb,), in_specs=_attn_specs(nb, False), out_specs=_row_spec(D_MODEL),
                 out_shape=jax.ShapeDtypeStruct((nb * BLOCK, D_MODEL), BF16), sem=("parallel",))(*([proj] * 8), sinks)


def attn_bwd(da, proj, sinks):
    nb = proj.shape[0] // BLOCK
    last = nb - 1
    wide = pl.BlockSpec((BLOCK, D_MODEL), lambda t: (jnp.minimum(t, last), 0))
    done = pl.BlockSpec((BLOCK, KV_WIDTH), lambda t: (jnp.maximum(t - 1, 0), 0))
    meta = _full((N_META, KV_WIDTH))
    par = _full((ATT_Q_HEADS, 1, 1))

    def body(da_ref, q_ref, z_ref, kp_ref, kc_ref, vp_ref, vc_ref, km_ref, vm_ref, sk_ref,
             dq_ref, dz_ref, dk_ref, dv_ref, dkm_ref, dvm_ref, dsk_ref, ck_ref, cv_ref):
        t = pl.program_id(0)

        @pl.when(t == 0)
        def _():
            ck_ref[...] = jnp.zeros_like(ck_ref)
            cv_ref[...] = jnp.zeros_like(cv_ref)
            dkm_ref[...] = jnp.zeros_like(dkm_ref)
            dvm_ref[...] = jnp.zeros_like(dvm_ref)
            dsk_ref[...] = jnp.zeros_like(dsk_ref)

        @pl.when(t < nb)
        def _():
            def f(q, z, kp, kc, vp, vc, km, vm, sk):
                return _attn_rows(q, z, kp, kc, vp, vc, km, vm, sk, t)

            _, vjp = jax.vjp(f, q_ref[...], z_ref[...], kp_ref[...], kc_ref[...], vp_ref[...], vc_ref[...],
                             km_ref[...], vm_ref[...], tuple(sk_ref[h] for h in range(ATT_Q_HEADS)))
            dq, dz, dkp, dkc, dvp, dvc, dkm, dvm, dsk = vjp(da_ref[...])
            dq_ref[...] = dq.astype(BF16)
            dz_ref[...] = dz.astype(BF16)
            for h in range(ATT_Q_HEADS):
                dsk_ref[h] += dsk[h]
            dk_ref[...] = ck_ref[...] + dkp
            dv_ref[...] = cv_ref[...] + dvp
            ck_ref[...] = dkc
            cv_ref[...] = dvc
            dkm_ref[...] += dkm
            dvm_ref[...] += dvm

        @pl.when(t == nb)
        def _():
            dk_ref[...] = ck_ref[...]
            dv_ref[...] = cv_ref[...]

    rows = nb * BLOCK
    return _call(body, name="attn_bwd", grid=(nb + 1,), in_specs=[wide] + _attn_specs(nb, True),
                 out_specs=[wide, wide, done, done, meta, meta, par],
                 out_shape=[jax.ShapeDtypeStruct((rows, D_MODEL), BF16), jax.ShapeDtypeStruct((rows, D_MODEL), BF16),
                            jax.ShapeDtypeStruct((rows, KV_WIDTH), F32), jax.ShapeDtypeStruct((rows, KV_WIDTH), F32),
                            jax.ShapeDtypeStruct((N_META, KV_WIDTH), F32), jax.ShapeDtypeStruct((N_META, KV_WIDTH), F32),
                            jax.ShapeDtypeStruct(sinks.shape, F32)],
                 scratch=[pltpu.VMEM((BLOCK, KV_WIDTH), F32), pltpu.VMEM((BLOCK, KV_WIDTH), F32)],
                 sem=("arbitrary",))(da, *([proj] * 8), sinks)


XBC_BLOCK0 = SEG["xbc"][2] // D_MODEL
CONV_COL_BLOCKS = CONV_DIM // D_MODEL
DT_TILE = SEG["dt"][2] // LANES


HALO = 8


def _conv_rows(length):
    return 544 if length % 544 == 0 else BLOCK


def _shift_rows(cur, before, j):
    if j == 0:
        return cur
    n = cur.shape[0]
    row = lax.broadcasted_iota(jnp.int32, cur.shape, 0)
    head = pltpu.roll(before, j, 0)
    if n > HALO:
        head = jnp.concatenate([head, jnp.zeros((n - HALO, cur.shape[1]), cur.dtype)], axis=0)
    return jnp.where(row >= j, pltpu.roll(cur, j, 0), head)


def _conv_pre(cur, before, w_ref, b_ref):
    pre = b_ref[...] + w_ref[CONV_WIDTH - 1:CONV_WIDTH, :] * cur
    for k in range(CONV_WIDTH - 1):
        pre = pre + w_ref[k:k + 1, :] * _shift_rows(cur, before, CONV_WIDTH - 1 - k)
    return pre


def _conv_specs(steps, rows, col0=0):
    first = XBC_BLOCK0 + col0
    halos = rows // HALO
    cur = pl.BlockSpec((rows, D_MODEL), lambda j, i: (i, first + j))
    before = pl.BlockSpec((HALO, D_MODEL), lambda j, i: (jnp.maximum(i * halos - 1, 0), first + j))
    after = pl.BlockSpec((HALO, D_MODEL), lambda j, i: (jnp.minimum(i + 1, steps - 1) * halos, first + j))
    return cur, before, after


def _valid_rows(i, rows):
    row = lax.broadcasted_iota(jnp.int32, (rows, D_MODEL), 0)
    return jnp.maximum((row >= PAD_ROWS).astype(F32), jnp.where(i > 0, 1.0, 0.0))


def conv_fwd(proj, conv_w, conv_b):
    rows = _conv_rows(proj.shape[0])
    steps = proj.shape[0] // rows
    cur, before, _ = _conv_specs(steps, rows)

    def body(c_ref, p_ref, w_ref, b_ref, o_ref):
        i = pl.program_id(1)
        pre = _conv_pre(c_ref[...], p_ref[...] * jnp.where(i > 0, 1.0, 0.0), w_ref, b_ref)
        o_ref[...] = _silu(pre) * _valid_rows(i, rows)

    return _call(body, name="conv_fwd", grid=(CONV_COL_BLOCKS, steps),
                 in_specs=[cur, before, pl.BlockSpec((CONV_WIDTH, D_MODEL), lambda j, i: (0, j)),
                           pl.BlockSpec((1, D_MODEL), lambda j, i: (0, j))],
                 out_specs=pl.BlockSpec((rows, D_MODEL), lambda j, i: (i, j)),
                 out_shape=jax.ShapeDtypeStruct((proj.shape[0], CONV_DIM), F32),
                 sem=("parallel", "parallel"))(proj, proj, conv_w, conv_b)


def conv_bwd(name, dparts, col0, proj, conv_w, conv_b):
    rows = _conv_rows(proj.shape[0])
    steps = proj.shape[0] // rows
    last = steps - 1
    ncol = sum(d.shape[1] for d in dparts) // D_MODEL
    np_ = len(dparts)
    cur, before, after = _conv_specs(steps, rows, col0)
    dcur = [pl.BlockSpec((rows, d.shape[1] // ncol), lambda j, i: (i, j)) for d in dparts]
    dafter = [pl.BlockSpec((HALO, d.shape[1] // ncol), lambda j, i: (jnp.minimum(i + 1, last) * (rows // HALO), j))
              for d in dparts]
    out_cur = pl.BlockSpec((rows, D_MODEL), lambda j, i: (i, j))
    wspec = pl.BlockSpec((CONV_WIDTH, D_MODEL), lambda j, i: (0, col0 + j))
    bspec = pl.BlockSpec((1, D_MODEL), lambda j, i: (0, col0 + j))
    wout = pl.BlockSpec((CONV_WIDTH, D_MODEL), lambda j, i: (0, j))
    bout = pl.BlockSpec((1, D_MODEL), lambda j, i: (0, j))

    def body(*refs):
        dc_refs, da_refs = refs[:np_], refs[np_:2 * np_]
        c_ref, p_ref, a_ref, w_ref, b_ref, du_ref, dw_ref, db_ref = refs[2 * np_:]
        i = pl.program_id(1)
        row = lax.broadcasted_iota(jnp.int32, (rows, D_MODEL), 0)
        curv = c_ref[...]
        beforev = p_ref[...] * jnp.where(i > 0, 1.0, 0.0)
        side_by_side = lambda rs: rs[0][...] if np_ == 1 else jnp.concatenate([r[...] for r in rs], axis=1)

        def dpre_of(pre, d):
            s = _sigmoid(pre)
            return d * (s * (1.0 + pre * (1.0 - s)))

        dp_c = dpre_of(_conv_pre(curv, beforev, w_ref, b_ref), side_by_side(dc_refs) * _valid_rows(i, rows))
        dp_a = dpre_of(_conv_pre(a_ref[...], curv[rows - HALO:], w_ref, b_ref),
                       side_by_side(da_refs) * jnp.where(i < last, 1.0, 0.0))
        du = w_ref[CONV_WIDTH - 1:CONV_WIDTH, :] * dp_c
        for j in range(1, CONV_WIDTH):
            tail = jnp.concatenate([jnp.zeros((rows - HALO, D_MODEL), F32), pltpu.roll(dp_a, HALO - j, 0)], axis=0)
            up = jnp.where(row < rows - j, pltpu.roll(dp_c, rows - j, 0), tail)
            du = du + w_ref[CONV_WIDTH - 1 - j:CONV_WIDTH - j, :] * up
        du_ref[...] = du.astype(BF16)

        @pl.when(i == 0)
        def _():
            dw_ref[...] = jnp.zeros_like(dw_ref)
            db_ref[...] = jnp.zeros_like(db_ref)

        for k in range(CONV_WIDTH):
            dw_ref[k:k + 1, :] += jnp.sum(dp_c * _shift_rows(curv, beforev, CONV_WIDTH - 1 - k), axis=0, keepdims=True)
        db_ref[...] += jnp.sum(dp_c, axis=0, keepdims=True)

    width = ncol * D_MODEL
    return _call(body, name=name, grid=(ncol, steps),
                 in_specs=dcur + dafter + [cur, before, after, wspec, bspec], out_specs=[out_cur, wout, bout],
                 out_shape=[jax.ShapeDtypeStruct((proj.shape[0], width), BF16),
                            jax.ShapeDtypeStruct((CONV_WIDTH, width), F32), jax.ShapeDtypeStruct((1, width), F32)],
                 sem=("parallel", "arbitrary"))(*dparts, *dparts, proj, proj, proj, conv_w, conv_b)


def _head_expand():
    e = np.zeros((LANES, SSM_INNER), np.float32)
    for h in range(SSM_HEADS):
        e[h, h * HEAD_DIM:(h + 1) * HEAD_DIM] = 1.0
    return jnp.asarray(e, dtype=BF16)


def _softplus(x):
    return jnp.maximum(x, 0.0) + jnp.log(1.0 + jnp.exp(-jnp.abs(x)))


def _bf16_parts(x):
    hi = x.astype(BF16)
    rest = x - hi.astype(F32)
    mid = rest.astype(BF16)
    return hi, mid, (rest - mid.astype(F32)).astype(BF16)


@jax.custom_vjp
def _times_01(x, m):
    return sum(jnp.dot(p, m, preferred_element_type=F32) for p in _bf16_parts(x))


def _times_01_bwd(m, g):
    return sum(lax.dot_general(p, m, _NT, preferred_element_type=F32) for p in _bf16_parts(g)), jnp.zeros_like(m)


_times_01.defvjp(lambda x, m: (_times_01(x, m), m), _times_01_bwd)


def _causal_ones():
    l = lax.broadcasted_iota(jnp.int32, (BLOCK, BLOCK), 0)
    s = lax.broadcasted_iota(jnp.int32, (BLOCK, BLOCK), 1)
    return (l >= s).astype(BF16)


@jax.custom_vjp
def _cumsum_rows(a):
    return sum(jnp.dot(_causal_ones(), p, preferred_element_type=F32) for p in _bf16_parts(a))


def _cumsum_rows_bwd(_, g):
    tn = (((0,), (0,)), ((), ()))
    return (sum(lax.dot_general(_causal_ones(), p, tn, preferred_element_type=F32) for p in _bf16_parts(g)),)


_cumsum_rows.defvjp(lambda a: (_cumsum_rows(a), None), _cumsum_rows_bwd)


def _ssd_heads(dt_tile, bias, alog, dsk, expand):
    dt = _softplus(dt_tile + bias)
    a = dt * (-jnp.exp(alog))
    one_row = lambda v: jnp.broadcast_to(v, (HALO, LANES))
    return _times_01(jnp.concatenate([dt, _cumsum_rows(a), one_row(jnp.sum(a, axis=0, keepdims=True)), one_row(dsk)],
                                     axis=0), expand)


HEADS_ROWS = 2 * BLOCK + 2 * HALO


def _ssd_group(xs, per_lane, bg, cg, state):
    l = lax.broadcasted_iota(jnp.int32, (BLOCK, BLOCK), 0)
    s = lax.broadcasted_iota(jnp.int32, (BLOCK, BLOCK), 1)
    causal = l >= s
    first_head = s < HEAD_DIM
    dtx, cs = per_lane[0:BLOCK], per_lane[BLOCK:2 * BLOCK]
    tot, dsk = per_lane[2 * BLOCK:2 * BLOCK + 1], per_lane[2 * BLOCK + HALO:2 * BLOCK + HALO + 1]
    bb, cb16 = bg.astype(BF16), cg.astype(BF16)
    cb = lax.dot_general(cb16, bb, _NT, preferred_element_type=F32)
    xr = xs * dtx
    y_diag = []
    for p in range(GROUP_W // LANES):
        lanes = slice(p * LANES, (p + 1) * LANES)
        c_pair = cs[:, lanes]
        c_swap = _swap_halves(c_pair)
        m = []
        for c_head in (jnp.where(first_head, c_pair, c_swap), jnp.where(first_head, c_swap, c_pair)):
            m.append(cb * jnp.exp(jnp.where(causal, c_head - c_head.T, -jnp.inf)))
        x_pair = xr[:, lanes]
        x_diag = jnp.concatenate([jnp.where(first_head, x_pair, 0.0), jnp.where(first_head, 0.0, x_pair)], axis=0)
        y_diag.append(jnp.dot(jnp.concatenate(m, axis=1).astype(BF16), x_diag.astype(BF16),
                              preferred_element_type=F32))
    st = lax.dot_general(bb, (xr * jnp.exp(tot - cs)).astype(BF16), (((0,), (0,)), ((), ())),
                         preferred_element_type=F32)
    new_state = state * jnp.exp(tot) + st
    y_off = jnp.dot(cb16, state.astype(BF16), preferred_element_type=F32) * jnp.exp(cs)
    return jnp.concatenate(y_diag, axis=1) + y_off + dsk * xs, new_state


BC_WIDTH = SSM_GROUPS * SSM_STATE


def _ssd_specs(chunk):
    xs = pl.BlockSpec((BLOCK, SSM_INNER), lambda c: (chunk(c), 0))
    dt = pl.BlockSpec((BLOCK, LANES), lambda c: (chunk(c), DT_TILE))
    expand = _full((LANES, SSM_INNER))
    b = pl.BlockSpec((BLOCK, BC_WIDTH), lambda c: (chunk(c), SSM_INNER // BC_WIDTH))
    cc = pl.BlockSpec((BLOCK, BC_WIDTH), lambda c: (chunk(c), SSM_INNER // BC_WIDTH + 1))
    par = _full((1, LANES))
    state = pl.BlockSpec((None, SSM_STATE, SSM_INNER), lambda c: (chunk(c), 0, 0))
    return xs, dt, expand, b, cc, par, state


def _group_lanes(g):
    return slice(g * GROUP_W, (g + 1) * GROUP_W), slice(g * SSM_STATE, (g + 1) * SSM_STATE)


def ssd_fwd(xbc, proj, expand, bias, alog, dsk):
    nb = xbc.shape[0] // BLOCK
    xs, dt, ex, b, cc, par, state = _ssd_specs(lambda c: c)

    def body(x_ref, dt_ref, e_ref, bi_ref, al_ref, dk_ref, b_ref, c_ref, y_ref, sp_ref, st_ref):
        @pl.when(pl.program_id(0) == 0)
        def _():
            st_ref[...] = jnp.zeros_like(st_ref)

        per_lane = _ssd_heads(dt_ref[...], bi_ref[...], al_ref[...], dk_ref[...], e_ref[...])
        for g in range(SSM_GROUPS):
            wide, tile = _group_lanes(g)
            entering = st_ref[:, wide]
            sp_ref[:, wide] = entering
            y_ref[:, wide], st_ref[:, wide] = _ssd_group(x_ref[:, wide], per_lane[:, wide], b_ref[:, tile],
                                                         c_ref[:, tile], entering)

    return _call(body, name="ssd_fwd", grid=(nb,), in_specs=[xs, dt, ex, par, par, par, b, cc],
                 out_specs=[xs, state],
                 out_shape=[jax.ShapeDtypeStruct((nb * BLOCK, SSM_INNER), F32),
                            jax.ShapeDtypeStruct((nb, SSM_STATE, SSM_INNER), F32)],
                 scratch=[pltpu.VMEM((SSM_STATE, SSM_INNER), F32)],
                 sem=("arbitrary",))(xbc, proj, expand, bias, alog, dsk, xbc, xbc)


def ssd_bwd(dy, xbc, proj, expand, bias, alog, dsk, states, comm):
    nb = xbc.shape[0] // BLOCK
    last = nb - 1
    xs, dt, ex, b, cc, par, state = _ssd_specs(lambda c: last - c)
    tile = pl.BlockSpec((BLOCK, LANES), lambda c: (last - c, 0))
    nspec = pl.BlockSpec((BLOCK, BC_WIDTH), lambda c: (last - c, 0))

    def body(dy_ref, x_ref, dt_ref, e_ref, bi_ref, al_ref, dk_ref, b_ref, c_ref, sp_ref,
             dx_ref, ddt_ref, db_ref, dc_ref, dbi_ref, dal_ref, ddk_ref, ds_ref):
        @pl.when(pl.program_id(0) == 0)
        def _():
            ds_ref[...] = jnp.zeros_like(ds_ref)
            dbi_ref[...] = jnp.zeros_like(dbi_ref)
            dal_ref[...] = jnp.zeros_like(dal_ref)
            ddk_ref[...] = jnp.zeros_like(ddk_ref)

        expand = e_ref[...]
        per_lane, heads_vjp = jax.vjp(lambda t, bi, al, dk: _ssd_heads(t, bi, al, dk, expand), dt_ref[...], bi_ref[...],
                                      al_ref[...], dk_ref[...])
        d_per_lane = []
        for g in range(SSM_GROUPS):
            wide, tile_lanes = _group_lanes(g)
            _, vjp = jax.vjp(_ssd_group, x_ref[:, wide], per_lane[:, wide], b_ref[:, tile_lanes], c_ref[:, tile_lanes],
                             sp_ref[:, wide])
            (dx_ref[:, wide], d_lanes, db_ref[:, tile_lanes], dc_ref[:, tile_lanes],
             ds_ref[:, wide]) = vjp((dy_ref[:, wide], ds_ref[:, wide]))
            d_per_lane.append(d_lanes)
        ddt, dbi, dal, ddk = heads_vjp(jnp.concatenate(d_per_lane, axis=1))
        ddt_ref[...] = ddt.astype(BF16)
        dbi_ref[...] += dbi
        dal_ref[...] += dal
        ddk_ref[...] += ddk

    def at():
        c = pl.program_id(0)
        return c == 0, c == 0, c == last

    par_shape = jax.ShapeDtypeStruct((1, LANES), F32)
    return _call_with_comm(
        body, comm, at, (dy, xbc, proj, expand, bias, alog, dsk, xbc, xbc, states), name="ssd_bwd",
        grid=(nb,), in_specs=[xs, xs, dt, ex, par, par, par, b, cc, state],
        out_specs=[xs, tile, nspec, nspec, par, par, par],
        out_shape=[jax.ShapeDtypeStruct((nb * BLOCK, SSM_INNER), F32), jax.ShapeDtypeStruct((nb * BLOCK, LANES), BF16),
                   jax.ShapeDtypeStruct((nb * BLOCK, BC_WIDTH), F32), jax.ShapeDtypeStruct((nb * BLOCK, BC_WIDTH), F32),
                   par_shape, par_shape, par_shape],
        scratch=[pltpu.VMEM((SSM_STATE, SSM_INNER), F32)])


SLAB_ROWS = 16
SLAB_META_ROW = 8
SLAB_META_SHAPE = (8, 2 * D_MODEL)
SLAB_LOSS_ROW = 7


def pack_small(dcw, dcb, dgpre, dgpost, dbias, dalog, ddsk, dsinks, dgn, dmeta, loss_tile):
    def body(cw, cb, gpre, gpost, dtb, al, dk, sk, gn, meta, loss, o_ref):
        o_ref[...] = jnp.zeros_like(o_ref)
        o_ref[SLAB_LOSS_ROW:SLAB_LOSS_ROW + 1, 0:LANES] = loss[0:1, :]
        o_ref[0:CONV_WIDTH, :] = cw[...]
        o_ref[4:5, :] = cb[...]
        o_ref[5:6, 0:1024] = gpre[...]
        o_ref[5:6, 1024:2048] = gpost[...]
        o_ref[5:6, 2048:2176] = dtb[...]
        o_ref[5:6, 2176:2304] = al[...]
        o_ref[5:6, 2304:2432] = dk[...]
        o_ref[5:6, 2432:2560] = sk[...]
        o_ref[6:7, 0:SSM_INNER] = gn[...]
        o_ref[SLAB_META_ROW:SLAB_ROWS, 0:SLAB_META_SHAPE[1]] = meta[...]

    args = (dcw, dcb, dgpre, dgpost, dbias, dalog, ddsk, dsinks, dgn, dmeta, loss_tile)
    return _call(body, name="pack_small", in_specs=[_full(a.shape) for a in args],
                 out_specs=_full((SLAB_ROWS, CONV_DIM)), out_shape=jax.ShapeDtypeStruct((SLAB_ROWS, CONV_DIM), F32))(*args)


def _lane_tile(v):
    return jnp.pad(v, ((0, 0), (0, LANES - v.shape[1])))


def kernel(x, meta_tokens, g_pre, w_in, conv_w, conv_b, dt_bias, a_log, d_skip, attn_sinks, g_ssm_norm, w_out_att, w_out_ssm, w_out, g_post, loss_target, m_meta_tokens, m_g_pre, m_w_in, m_conv_w, m_conv_b, m_dt_bias, m_a_log, m_d_skip, m_attn_sinks, m_g_ssm_norm, m_w_out_att, m_w_out_ssm, m_w_out, m_g_post, v_meta_tokens, v_g_pre, v_w_in, v_conv_w, v_conv_b, v_dt_bias, v_a_log, v_d_skip, v_attn_sinks, v_g_ssm_norm, v_w_out_att, v_w_out_ssm, v_w_out, v_g_post):
    chip = _chip_index()

    conv_w_rows = jnp.pad(conv_w[0], ((0, 2 * 8 - CONV_WIDTH), (0, 0)))
    w_in_t, m_w_in_t, v_w_in_t = w_in[0].T, m_w_in[0].T, v_w_in[0].T
    (h, u), (gathered_w_in, g_conv_w, g_meta) = prep(
        "gather_w_in", x, g_pre, TwoLevelGather([pack_w_in(w_in_t), conv_w_rows, meta_tokens]))
    w_all_t = unpack_w_in(gathered_w_in)
    cw_full = g_conv_w[:, :CONV_WIDTH].transpose(1, 0, 2).reshape(CONV_WIDTH, CONV_DIM)
    behind_w_in = 0.0 * g_meta[0, 0, 0]
    w_out_flight, w_out_started = chip_exchange_start(
        "gather_w_out_start", [(w[0] + behind_w_in).astype(BF16) for w in (w_out_att, w_out_ssm, w_out)], False)

    proj = project("in_proj", u, w_all_t, w_out_started)

    sinks3 = attn_sinks.reshape(ATT_Q_HEADS, 1, 1)
    a_att = attn_fwd(proj, sinks3)

    xbc = conv_fwd(proj, cw_full, conv_b)
    expand = _head_expand()
    head_pars = (_lane_tile(dt_bias), _lane_tile(a_log), _lane_tile(d_skip))
    y_ssd, states = ssd_fwd(xbc, proj, expand, *head_pars)
    woa, wos, wo = [g.reshape(-1, D_MODEL) for g in chip_exchange_wait("gather_w_out_wait", w_out_flight, False, y_ssd)]

    (yn, merged, dout, dy_att, dy_ssm, da_att, dy_ssd, dz_ssm, dga, dgs, dres, loss_tile, dg_post, dgn) = tail(
        y_ssd, proj, a_att, x, loss_target, woa, wos, wo, g_ssm_norm, g_post)

    dwo = mm_tn("out_proj_dw", merged, dout)
    dwoa = mm_tn("att_out_dw", a_att, dy_att)
    dwos = mm_tn("ssm_out_dw", yn, dy_ssm)
    dq, dz_att, dk, dv, dkmeta, dvmeta, dsinks3 = attn_bwd(da_att, proj, sinks3)
    dk = dk.at[PAD_ROWS:BLOCK].add(dkmeta).astype(BF16)
    dv = dv.at[PAD_ROWS:BLOCK].add(dvmeta).astype(BF16)

    def pieces(g):
        return g.reshape(4, 2, g.shape[0] // 8, g.shape[1])

    def to_owner(g):
        return (lambda ref, dev: ref.at[_chip_of(dev), dev[2]], (g.shape[0] // 8, g.shape[1]))

    (dxs, ddt_tile, dbg, dcg, dbias, dalog, ddsk), sent_w_out = ssd_bwd(
        dy_ssd, xbc, proj, expand, *head_pars, states,
        DirectExchange([pieces(dwoa), pieces(dwos), pieces(dwo)], [to_owner(dwoa), to_owner(dwos), to_owner(dwo)],
                       ALL_MASKS, "dev", 8))
    dxs_raw, dcw_xs, dcb_xs = conv_bwd("conv_bwd_x", [dxs], 0, proj, cw_full, conv_b)
    dbc_raw, dcw_bc, dcb_bc = conv_bwd("conv_bwd_bc", [dbg, dcg], SSM_INNER // D_MODEL, proj, cw_full, conv_b)
    dcw = jnp.concatenate([dcw_xs, dcw_bc], axis=1)
    dcb = jnp.concatenate([dcb_xs, dcb_bc], axis=1)

    narrow = jnp.concatenate([dk, dv, ddt_tile, jnp.zeros((dk.shape[0], N_ACT - N_ALIGNED), BF16)], axis=1)
    dproj = [dz_ssm, dxs_raw, dbc_raw, dq, dz_att, dga, dgs, narrow]
    by_shard = [7, 3, 4, 0, 1, 2, 5, 6]
    partial, from_sibling = weight_grad_t("in_proj_dw", dproj, by_shard, u)
    grads_flight, started = pair_sum_exchange_start(
        "reduce_w_in_start", partial.reshape(4, 2, PACK_W // 2, D_MODEL), from_sibling)
    du = project_back("in_proj_dx", dproj, w_all_t, started)
    grad_x, dmeta, dg_pre = prep_bwd(h, du, dres, g_pre)

    halves = [sum_slots("sum_" + nm, r) for nm, r in zip(("w_out_att", "w_out_ssm", "w_out"), sent_w_out)]
    shared = run_comm("share_w_out", DirectExchange(halves, [None] * 3, SIBLING_MASK, "core", 2))
    g_woa, g_wos, g_wo = [f.reshape(2 * f.shape[1], f.shape[2]) for f in shared]
    d_woa, nm_woa, nv_woa = adamw_rows("adamw_w_out_att", g_woa, w_out_att[0], m_w_out_att[0], v_w_out_att[0])
    d_wos, nm_wos, nv_wos = adamw_rows("adamw_w_out_ssm", g_wos, w_out_ssm[0], m_w_out_ssm[0], v_w_out_ssm[0])
    d_wo, nm_wo, nv_wo = adamw_rows("adamw_w_out", g_wo, w_out[0], m_w_out[0], v_w_out[0])

    sent_w_in, = chip_exchange_wait("reduce_w_in_wait", grads_flight, True, d_wos)
    slab = pack_small(dcw, dcb, dg_pre, dg_post, dbias, dalog, ddsk, _lane_tile(dsinks3.reshape(1, ATT_Q_HEADS)), dgn,
                      dmeta.reshape(SLAB_META_SHAPE), loss_tile)
    slabs, shared_w_in = run_comm("share_w_in", Both(DirectExchange([slab], [None], ALL_MASKS, "dev", 8),
                                                     PairOfSums(sent_w_in)))
    small = sum_slots("sum_small", slabs)
    loss = small[SLAB_LOSS_ROW, 0]
    g_w_in, d_w_in, nm_w_in, nv_w_in = [
        a.T for a in adamw_w_in(shared_w_in.reshape(PACK_W, D_MODEL), w_in_t, m_w_in_t, v_w_in_t)]

    cw_cols = CONV_DIM // 4
    meta_cols = D_MODEL // 4
    g_small = {
        "meta_tokens": lax.dynamic_slice(
            small[SLAB_META_ROW:SLAB_ROWS, 0:SLAB_META_SHAPE[1]].reshape(N_META, D_MODEL), (0, chip * meta_cols),
            (N_META, meta_cols)),
        "g_pre": small[5:6, 0:1024],
        "conv_w": lax.dynamic_slice(small, (0, chip * cw_cols), (CONV_WIDTH, cw_cols)),
        "conv_b": small[4:5, :],
        "dt_bias": small[5:6, 2048:2048 + SSM_HEADS],
        "a_log": small[5:6, 2176:2176 + SSM_HEADS],
        "d_skip": small[5:6, 2304:2304 + SSM_HEADS],
        "attn_sinks": small[5:6, 2432:2432 + ATT_Q_HEADS],
        "g_ssm_norm": small[6:7, 0:SSM_INNER],
        "g_post": small[5:6, 1024:2048],
    }
    names = list(g_small)
    w_small = dict(meta_tokens=meta_tokens, g_pre=g_pre, conv_w=conv_w[0], conv_b=conv_b, dt_bias=dt_bias, a_log=a_log,
                   d_skip=d_skip, attn_sinks=attn_sinks, g_ssm_norm=g_ssm_norm, g_post=g_post)
    m_small = dict(meta_tokens=m_meta_tokens, g_pre=m_g_pre, conv_w=m_conv_w[0], conv_b=m_conv_b, dt_bias=m_dt_bias,
                   a_log=m_a_log, d_skip=m_d_skip, attn_sinks=m_attn_sinks, g_ssm_norm=m_g_ssm_norm, g_post=m_g_post)
    v_small = dict(meta_tokens=v_meta_tokens, g_pre=v_g_pre, conv_w=v_conv_w[0], conv_b=v_conv_b, dt_bias=v_dt_bias,
                   a_log=v_a_log, d_skip=v_d_skip, attn_sinks=v_attn_sinks, g_ssm_norm=v_g_ssm_norm, g_post=v_g_post)
    upd = dict(zip(names, adamw_small([g_small[k] for k in names], [w_small[k] for k in names],
                                      [m_small[k] for k in names], [v_small[k] for k in names])))

    lead = {"conv_w"}

    def shaped(name, a):
        return a[None] if name in lead else a

    grads = dict(g_small, w_in=g_w_in, w_out_att=g_woa, w_out_ssm=g_wos, w_out=g_wo)
    deltas = dict({k: upd[k][0] for k in names}, w_in=d_w_in, w_out_att=d_woa, w_out_ssm=d_wos, w_out=d_wo)
    new_m = dict({k: upd[k][1] for k in names}, w_in=nm_w_in, w_out_att=nm_woa, w_out_ssm=nm_wos, w_out=nm_wo)
    new_v = dict({k: upd[k][2] for k in names}, w_in=nv_w_in, w_out_att=nv_woa, w_out_ssm=nv_wos, w_out=nv_wo)
    lead |= {"w_in", "w_out_att", "w_out_ssm", "w_out"}
    order = ["meta_tokens", "g_pre", "w_in", "conv_w", "conv_b", "dt_bias", "a_log", "d_skip", "attn_sinks",
             "g_ssm_norm", "w_out_att", "w_out_ssm", "w_out", "g_post"]
    outs = [loss, grad_x]
    for group in (grads, deltas, new_m, new_v):
        outs += [shaped(k, group[k]) for k in order]
    return tuple(outs)
```

```python
import numpy as np
import jax
import jax.numpy as jnp
from jax import lax
from jax.experimental import pallas as pl
from jax.experimental.pallas import tpu as pltpu

F32 = jnp.float32
BF16 = jnp.bfloat16

D_MODEL = 1024
N_META = 16
BLOCK = 128
PAD_ROWS = BLOCK - N_META
NORM_EPS = 1e-6
HEAD_DIM = 64
ATT_Q_HEADS = 16
ATT_KV_HEADS = 4
ATT_GROUP = 4
SSM_INNER = 2048
SSM_HEADS = 32
SSM_GROUPS = 4
SSM_STATE = 128
CONV_WIDTH = 4
CONV_DIM = 3072
LANES = 128

ADAM_LR = 0.001
ADAM_B1 = 0.9
ADAM_B2 = 0.999
ADAM_EPS = 1e-08
ADAM_WD = 0.01
ADAM_STEP = 10

VMEM_LIMIT = 48 * 1024 * 1024

SHARD_W = 2440
PACK_W = 2560
SHARD_STRIDE = 2432
N_ALIGNED = 9856
N_ACT = 10240
SEG = {
    "q": (0, 1024, 5120), "k": (1024, 256, 9216), "v": (1280, 256, 9472), "z_att": (1536, 1024, 6144),
    "z_ssm": (2560, 2048, 0), "xbc": (4608, 3072, 2048), "dt": (7680, 128, 9728),
    "gate_att": (7808, 1024, 7168), "gate_ssm": (8832, 1024, 8192),
}
DT_STORED_START = 7680
DT_PAD = LANES - SSM_HEADS


def _act_col(aligned_col):
    for a0, w, p0 in SEG.values():
        if a0 <= aligned_col < a0 + w:
            return p0 + aligned_col - a0
    raise ValueError(aligned_col)


def _call(body, *, name, out_shape, in_specs, out_specs, grid=(), scratch=(), sem=None, aliases=None):
    return pl.pallas_call(
        body, out_shape=out_shape, grid=grid, in_specs=in_specs, out_specs=out_specs, scratch_shapes=list(scratch),
        name=name, input_output_aliases=aliases or {},
        compiler_params=pltpu.CompilerParams(dimension_semantics=sem, vmem_limit_bytes=VMEM_LIMIT))


def _full(shape):
    n = len(shape)
    return pl.BlockSpec(shape, lambda *_: (0,) * n)


def _chip_index():
    return lax.axis_index("x") * 2 + lax.axis_index("y")


_sigmoid = jax.nn.sigmoid


def _silu(z):
    return z * _sigmoid(z)


def _rms(x, g):
    return x * lax.rsqrt(jnp.mean(x * x, axis=-1, keepdims=True) + NORM_EPS) * g


def _peer(mask):
    x, y, c = lax.axis_index("x"), lax.axis_index("y"), lax.axis_index("c")
    return ((1 - x) if mask & 4 else x, (1 - y) if mask & 2 else y, (1 - c) if mask & 1 else c)


def _me():
    return lax.axis_index("x"), lax.axis_index("y"), lax.axis_index("c")


def _chip_of(dev):
    return 2 * dev[0] + dev[1]


CHIP_MASKS = (4, 2, 6)
ALL_MASKS = (1, 2, 3, 4, 5, 6, 7)
SIBLING_MASK = (1,)


def _remote(src, dst, send_sem, recv_sem, dev):
    return pltpu.make_async_remote_copy(src_ref=src, dst_ref=dst, send_sem=send_sem, recv_sem=recv_sem,
                                        device_id=dev, device_id_type=pl.DeviceIdType.MESH)


class _StagedCopy:
    def __init__(self, src, stage, dst, load_sem, store_sem):
        self.load = pltpu.make_async_copy(src, stage, load_sem)
        self.store = pltpu.make_async_copy(stage, dst, store_sem)

    def start(self):
        self.load.start()
        self.load.wait()
        self.store.start()

    def wait(self):
        self.store.wait()


class DirectExchange:
    def __init__(self, arrays, pieces, masks, slot_kind, nslots, keep_own=True):
        self.arrays, self.pieces, self.masks, self.slot_kind = list(arrays), list(pieces), masks, slot_kind
        self.keep_own = keep_own
        n, nk = len(arrays), len(masks)
        shapes = [a.shape if p is None else p[1] for a, p in zip(arrays, pieces)]
        self.out_shape = [jax.ShapeDtypeStruct((nslots,) + tuple(s), a.dtype) for s, a in zip(shapes, arrays)]
        self.scratch = [pltpu.SemaphoreType.DMA((n * nk,)), pltpu.SemaphoreType.DMA((n * nk,))]
        if keep_own:
            self.scratch += [pltpu.SemaphoreType.DMA((2 * n,))] + [pltpu.VMEM(s, a.dtype) for s, a in zip(shapes, arrays)]
        self.has_mid = False

    def _copies(self, ins, outs, scratch):
        send_sems, recv_sems = scratch[:2]
        me = _me()
        slot = {"chip": _chip_of(me), "dev": 4 * me[0] + 2 * me[1] + me[2], "core": me[2]}[self.slot_kind]
        nk = len(self.masks)

        def piece(a, dev):
            return ins[a] if self.pieces[a] is None else self.pieces[a][0](ins[a], dev)

        local = []
        if self.keep_own:
            local_sems, stages = scratch[2], scratch[3:]
            local = [_StagedCopy(piece(a, me), stages[a], outs[a].at[slot], local_sems.at[2 * a], local_sems.at[2 * a + 1])
                     for a in range(len(ins))]
        remote = []
        for a in range(len(ins)):
            for ki, mask in enumerate(self.masks):
                dev = _peer(mask)
                remote.append(_remote(piece(a, dev), outs[a].at[slot], send_sems.at[a * nk + ki],
                                      recv_sems.at[a * nk + ki], dev))
        return local, remote

    def start(self, ins, outs, scratch):
        local, remote = self._copies(ins, outs, scratch)
        for cp in remote + local:
            cp.start()

    def finish(self, ins, outs, scratch):
        local, remote = self._copies(ins, outs, scratch)
        for cp in remote + local:
            cp.wait()


SPLIT_ROWS = 16


class TwoLevelGather:
    def __init__(self, arrays):
        self.arrays = list(arrays)
        n = len(arrays)
        self.out_shape = [jax.ShapeDtypeStruct((4,) + a.shape, a.dtype) for a in arrays]
        self.scratch = ([pltpu.SemaphoreType.DMA((4 * n,)), pltpu.SemaphoreType.DMA((4 * n,)),
                         pltpu.SemaphoreType.DMA((3 * n,)), pltpu.SemaphoreType.DMA((3 * n,)),
                         pltpu.SemaphoreType.DMA((2 * n,))] + [pltpu.VMEM(a.shape, a.dtype) for a in arrays])
        self.has_mid = True

    def _copies(self, ins, outs, scratch):
        ici_send, ici_recv, fwd_send, fwd_recv, local_sems = scratch[:5]
        stages = scratch[5:]
        me = _me()
        sibling, in_x, in_y, diagonal = _peer(1), _peer(4), _peer(2), _peer(6)
        plan = []
        for a in range(len(ins)):
            half = ins[a].shape[0] // 2
            first = half // 2 if half % (2 * SPLIT_ROWS) == 0 else half
            mine = pl.ds(me[2] * half, half)
            local = _StagedCopy(ins[a], stages[a], outs[a].at[_chip_of(me)], local_sems.at[2 * a], local_sems.at[2 * a + 1])

            def ici(k, src, dst, dev):
                return _remote(src, dst, ici_send.at[4 * a + k], ici_recv.at[4 * a + k], dev)

            def d2d(k, chip):
                zone = outs[a].at[_chip_of(chip), mine]
                return _remote(zone, zone, fwd_send.at[3 * a + k], fwd_recv.at[3 * a + k], sibling)

            own_zone = outs[a].at[_chip_of(me), mine]
            from_x = outs[a].at[_chip_of(in_x), pl.ds(me[2] * half, first)]
            onward = [ici(2, from_x, from_x, in_y), None]
            if first < half:
                from_y = outs[a].at[_chip_of(in_y), pl.ds(me[2] * half + first, half - first)]
                onward[1] = ici(3, from_y, from_y, in_x)
            plan.append(dict(
                local=local,
                own=[ici(0, ins[a].at[mine], own_zone, in_x), ici(1, ins[a].at[mine], own_zone, in_y)],
                onward=onward, sibling=[d2d(0, in_x), d2d(1, in_y), d2d(2, diagonal)]))
        return plan

    def start(self, ins, outs, scratch):
        for p in self._copies(ins, outs, scratch):
            for cp in p["own"]:
                cp.start()
            p["local"].start()

    def mid(self, ins, outs, scratch):
        plan = self._copies(ins, outs, scratch)
        for p in plan:
            for k in range(2):
                p["own"][k].wait_recv()
                if p["onward"][k] is not None:
                    p["onward"][k].start()
                p["sibling"][k].start()
        for p in plan:
            for cp in p["onward"]:
                if cp is not None:
                    cp.wait_recv()
            p["sibling"][2].start()

    def finish(self, ins, outs, scratch):
        for p in self._copies(ins, outs, scratch):
            for cp in p["sibling"]:
                cp.wait_recv()
            for cp in p["own"] + p["sibling"] + [cp for cp in p["onward"] if cp is not None]:
                cp.wait_send()
            p["local"].wait()


class Both:
    def __init__(self, a, b):
        self.a, self.b = a, b
        self.arrays, self.out_shape = a.arrays + b.arrays, a.out_shape + b.out_shape
        self.scratch = a.scratch + b.scratch
        self.has_mid = False
        assert not (a.has_mid or b.has_mid)

    def _parts(self, ins, outs, sems):
        na, sa = len(self.a.arrays), len(self.a.scratch)
        return (ins[:na], outs[:na], sems[:sa]), (ins[na:], outs[na:], sems[sa:])

    def start(self, ins, outs, sems):
        pa, pb = self._parts(ins, outs, sems)
        self.a.start(*pa)
        self.b.start(*pb)

    def finish(self, ins, outs, sems):
        pa, pb = self._parts(ins, outs, sems)
        self.a.finish(*pa)
        self.b.finish(*pb)


class PairOfSums:
    def __init__(self, array):
        s, rows, cols = array.shape
        self.arrays = [array]
        self.out_shape = [jax.ShapeDtypeStruct((2, rows, cols), F32)]
        self.scratch = [pltpu.SemaphoreType.DMA((s,)), pltpu.SemaphoreType.DMA((3,)), pltpu.VMEM(array.shape, array.dtype),
                        pltpu.VMEM((rows, cols), F32)]
        self.has_mid = False

    def _copies(self, ins, outs, scratch):
        load_sems, sems, slots, total = scratch
        mine = outs[0].at[lax.axis_index("c")]
        loads = [pltpu.make_async_copy(ins[0].at[k], slots.at[k], load_sems.at[k]) for k in range(slots.shape[0])]
        return loads, [_remote(total, mine, sems.at[0], sems.at[1], _peer(1)), pltpu.make_async_copy(total, mine, sems.at[2])]

    def start(self, ins, outs, scratch):
        loads, stores = self._copies(ins, outs, scratch)
        slots, total = scratch[2:]
        for cp in loads:
            cp.start()
        for cp in loads:
            cp.wait()
        acc = slots[0].astype(F32)
        for k in range(1, slots.shape[0]):
            acc = acc + slots[k].astype(F32)
        total[...] = acc
        for cp in stores:
            cp.start()

    def finish(self, ins, outs, scratch):
        for cp in self._copies(ins, outs, scratch)[1]:
            cp.wait()


_ANY = pl.BlockSpec(memory_space=pl.ANY)


def run_comm(name, comm):
    n = len(comm.arrays)

    def body(*refs):
        ins, outs, sems = refs[:n], refs[n:2 * n], refs[2 * n:]
        comm.start(ins, outs, sems)
        if comm.has_mid:
            comm.mid(ins, outs, sems)
        comm.finish(ins, outs, sems)

    return pl.pallas_call(body, name=name, out_shape=comm.out_shape, in_specs=[_ANY] * n, out_specs=[_ANY] * n,
                          scratch_shapes=comm.scratch,
                          compiler_params=pltpu.CompilerParams(vmem_limit_bytes=VMEM_LIMIT))(*comm.arrays)


_HBM = pl.BlockSpec(memory_space=pltpu.HBM)
_SEM = pl.BlockSpec(memory_space=pltpu.SEMAPHORE)
_SIDE_EFFECT = pltpu.SideEffectType.DATAFLOW_SIDE_EFFECTING


def _chip_copies(srcs, lands, send_sems, recv_sems, by_target):
    me = _me()
    copies = []
    for a, (src, land) in enumerate(zip(srcs, lands)):
        for ki, mask in enumerate(CHIP_MASKS):
            dev = _peer(mask)
            k = a * len(CHIP_MASKS) + ki
            piece = src.at[_chip_of(dev)] if by_target else src
            copies.append(_remote(piece, land.at[_chip_of(me)], send_sems.at[k], recv_sems.at[k], dev))
    return copies


def chip_exchange_start(name, arrays, by_target):
    n = len(arrays)
    nsem = n * len(CHIP_MASKS)
    piece_shapes = [a.shape[1:] if by_target else a.shape for a in arrays]

    def body(*refs):
        srcs, lands = refs[:n], refs[n:2 * n]
        send_sems, recv_sems = refs[2 * n:2 * n + 2]
        token = refs[4 * n + 2]
        stages, local_sems = refs[4 * n + 3:5 * n + 3], refs[5 * n + 3]
        me = _me()
        for cp in _chip_copies(srcs, lands, send_sems, recv_sems, by_target):
            cp.start()
        own = [_StagedCopy(srcs[a].at[_chip_of(me)] if by_target else srcs[a], stages[a], lands[a].at[_chip_of(me)],
                           local_sems.at[2 * a], local_sems.at[2 * a + 1]) for a in range(n)]
        for cp in own:
            cp.load.start()
        for cp in own:
            cp.load.wait()
            cp.store.start()
        for cp in own:
            cp.store.wait()
        token[...] = jnp.zeros_like(token)

    lands = [lax.empty((4,) + tuple(s), a.dtype) for s, a in zip(piece_shapes, arrays)]
    hbm = lambda a: pltpu.HBM(a.shape, a.dtype)
    res = pl.pallas_call(
        body, name=name,
        out_shape=(pltpu.SemaphoreType.DMA((nsem,)), pltpu.SemaphoreType.DMA((nsem,)), *[hbm(a) for a in arrays],
                   *[hbm(l) for l in lands], jax.ShapeDtypeStruct((8, LANES), F32)),
        in_specs=(_HBM,) * (2 * n), out_specs=(_SEM, _SEM) + (_HBM,) * (2 * n) + (pl.BlockSpec(memory_space=pltpu.VMEM),),
        input_output_aliases={i: i + 2 for i in range(2 * n)},
        scratch_shapes=[pltpu.VMEM(tuple(s), a.dtype) for s, a in zip(piece_shapes, arrays)]
        + [pltpu.SemaphoreType.DMA((2 * n,))],
        compiler_params=pltpu.CompilerParams(has_side_effects=_SIDE_EFFECT, vmem_limit_bytes=VMEM_LIMIT),
    )(*[pltpu.with_memory_space_constraint(a, pltpu.HBM) for a in arrays + lands])
    return res[:-1], res[-1]


def pair_sum_exchange_start(name, partial, from_sibling):
    shards, _, rows, cols = partial.shape
    nsem = len(CHIP_MASKS)

    def body(p_ref, r_ref, _, send_sems, recv_sems, sum_ref, land_ref, token, p_buf, r_buf, o_buf, sems):
        me = _me()
        targets = [_chip_of(_peer(mask)) for mask in CHIP_MASKS] + [_chip_of(me)]
        remote = _chip_copies([sum_ref], [land_ref], send_sems, recv_sems, True)

        def loads(k):
            return [pltpu.make_async_copy(p_ref.at[targets[k], me[2]], p_buf.at[k % 2], sems.at[2 * (k % 2)]),
                    pltpu.make_async_copy(r_ref.at[targets[k]], r_buf.at[k % 2], sems.at[2 * (k % 2) + 1])]

        for cp in loads(0):
            cp.start()
        for k in range(shards):
            if k + 1 < shards:
                for cp in loads(k + 1):
                    cp.start()
            for cp in loads(k):
                cp.wait()
            o_buf[k % 2] = (p_buf[k % 2].astype(F32) + r_buf[k % 2].astype(F32)).astype(BF16)
            summed = sum_ref.at[targets[k]] if k < nsem else land_ref.at[targets[k]]
            store = pltpu.make_async_copy(o_buf.at[k % 2], summed, sems.at[4])
            store.start()
            store.wait()
            if k < nsem:
                remote[k].start()
        token[...] = jnp.zeros_like(token)

    piece = jax.ShapeDtypeStruct((shards, rows, cols), BF16)
    land = lax.empty(piece.shape, BF16)
    buf = pltpu.VMEM((2, rows, cols), BF16)
    res = pl.pallas_call(
        body, name=name,
        out_shape=(pltpu.SemaphoreType.DMA((nsem,)), pltpu.SemaphoreType.DMA((nsem,)), pltpu.HBM(piece.shape, BF16),
                   pltpu.HBM(piece.shape, BF16), jax.ShapeDtypeStruct((8, LANES), F32)),
        in_specs=(_HBM,) * 3, out_specs=(_SEM, _SEM, _HBM, _HBM, pl.BlockSpec(memory_space=pltpu.VMEM)),
        input_output_aliases={2: 3},
        scratch_shapes=[buf, buf, buf, pltpu.SemaphoreType.DMA((5,))],
        compiler_params=pltpu.CompilerParams(has_side_effects=_SIDE_EFFECT, vmem_limit_bytes=VMEM_LIMIT),
    )(*[pltpu.with_memory_space_constraint(a, pltpu.HBM) for a in (partial, from_sibling, land)])
    return res[:-1], res[-1]


def chip_exchange_wait(name, in_flight, by_target, after):
    send_sems, recv_sems, *thru = in_flight
    n = len(thru) // 2

    def body(*refs):
        srcs, lands, (send, recv) = refs[:n], refs[n:2 * n], refs[2 * n:2 * n + 2]
        for cp in _chip_copies(srcs, lands, send, recv, by_target):
            cp.wait_send()
            cp.wait_recv()

    return pl.pallas_call(
        body, name=name, out_shape=tuple(pltpu.HBM(t.shape, t.dtype) for t in thru),
        in_specs=(_HBM,) * (2 * n) + (_SEM, _SEM, pl.BlockSpec(memory_space=pl.ANY)), out_specs=(_HBM,) * (2 * n),
        input_output_aliases={i: i for i in range(2 * n)},
        compiler_params=pltpu.CompilerParams(has_side_effects=_SIDE_EFFECT),
    )(*thru, send_sems, recv_sems, after)[n:]


def _call_with_comm(body, comm, steps, args, *, name, out_shape, in_specs, out_specs, grid, scratch=(),
                    after_finish=None):
    ni, no, ns, nc = len(in_specs), len(out_specs), len(scratch), len(comm.arrays)

    def full_body(*refs):
        ins, cins = refs[:ni], refs[ni:ni + nc]
        outs, couts = refs[ni + nc:ni + nc + no], refs[ni + nc + no:ni + 2 * nc + no]
        scr, csems = refs[ni + 2 * nc + no:ni + 2 * nc + no + ns], refs[ni + 2 * nc + no + ns:]
        first, middle, last = steps()
        pl.when(first)(lambda: comm.start(cins, couts, csems))
        if comm.has_mid:
            pl.when(middle)(lambda: comm.mid(cins, couts, csems))
        body(*ins, *outs, *scr)

        @pl.when(last)
        def _():
            comm.finish(cins, couts, csems)
            if after_finish is not None:
                after_finish(ins, couts, outs, scr)

    res = pl.pallas_call(
        full_body, name=name, out_shape=list(out_shape) + comm.out_shape, grid=grid,
        in_specs=list(in_specs) + [_ANY] * nc, out_specs=list(out_specs) + [_ANY] * nc,
        scratch_shapes=list(scratch) + comm.scratch,
        compiler_params=pltpu.CompilerParams(dimension_semantics=("arbitrary",) * len(grid),
                                             vmem_limit_bytes=VMEM_LIMIT))(*args, *comm.arrays)
    return res[:no], res[no:]


def _shard_pieces(chip):
    if chip < 3:
        return [(0, SHARD_W, 8 * chip)]
    behind_dt = DT_STORED_START + SSM_HEADS - 3 * SHARD_W
    return [(0, behind_dt, 24), (behind_dt, SHARD_W - behind_dt, behind_dt + 24 + DT_PAD)]


W_IN_COLS = 256


def pack_w_in(wt):
    def body(w_ref, o_ref, pad_ref):
        chip = _chip_index()
        pad_ref[...] = jnp.zeros_like(pad_ref)
        for cv in range(4):
            @pl.when(chip == cv)
            def _():
                for src, n, dst in _shard_pieces(cv):
                    pad_ref[dst:dst + n, :] = w_ref[src:src + n, :]
        o_ref[...] = pad_ref[...].astype(BF16)

    return _call(body, name="pack_w_in", grid=(D_MODEL // W_IN_COLS,),
                 in_specs=[pl.BlockSpec((SHARD_W, W_IN_COLS), lambda i: (0, i))],
                 out_specs=pl.BlockSpec((PACK_W, W_IN_COLS), lambda i: (0, i)),
                 out_shape=jax.ShapeDtypeStruct((PACK_W, D_MODEL), BF16),
                 scratch=[pltpu.VMEM((PACK_W, W_IN_COLS), F32)], sem=("parallel",))(wt)


def _tile_runs():
    runs, fix = [], []
    for t in range(N_ALIGNED // LANES):
        s = min(t // 19, 3)
        j = t - 19 * s
        p = _act_col(t * LANES)
        if runs and runs[-1][1] == s and runs[-1][0] + runs[-1][3] == p and runs[-1][2] + runs[-1][3] == j * LANES:
            runs[-1][3] += LANES
        else:
            runs.append([p, s, j * LANES, LANES])
        if j == 0 and s > 0:
            fix.append((p, s - 1))
    return runs, fix


def unpack_w_in(bg):
    runs, fix = _tile_runs()

    def body(b_ref, o_ref):
        for p, s, j, w in runs:
            o_ref[p:p + w, :] = b_ref[s, j:j + w, :]
        for p, s in fix:
            o_ref[p:p + LANES, :] = o_ref[p:p + LANES, :] + b_ref[s, SHARD_STRIDE:PACK_W, :]
        o_ref[N_ALIGNED:N_ACT, :] = jnp.zeros((N_ACT - N_ALIGNED, W_IN_COLS), BF16)

    return _call(body, name="unpack_w_in", grid=(D_MODEL // W_IN_COLS,),
                 in_specs=[pl.BlockSpec((4, PACK_W, W_IN_COLS), lambda i: (0, 0, i))],
                 out_specs=pl.BlockSpec((N_ACT, W_IN_COLS), lambda i: (0, i)),
                 out_shape=jax.ShapeDtypeStruct((N_ACT, D_MODEL), BF16), sem=("parallel",))(bg)


def _adamw(w, g, m, v):
    m = ADAM_B1 * m + (1.0 - ADAM_B1) * g
    v = ADAM_B2 * v + (1.0 - ADAM_B2) * jnp.square(g)
    m_hat = m / (1.0 - ADAM_B1 ** ADAM_STEP)
    v_hat = v / (1.0 - ADAM_B2 ** ADAM_STEP)
    delta = -ADAM_LR * (m_hat / (jnp.sqrt(v_hat) + ADAM_EPS) + ADAM_WD * w)
    return delta, m, v


def adamw_w_in(g_packed, wt, mt, vt):
    cols = LANES

    def body(g_ref, w_ref, m_ref, v_ref, go_ref, d_ref, mo_ref, vo_ref):
        chip = _chip_index()
        for cv in range(4):
            @pl.when(chip == cv)
            def _():
                for dst, n, src in _shard_pieces(cv):
                    go_ref[dst:dst + n, :] = g_ref[src:src + n, :]
        d_ref[...], mo_ref[...], vo_ref[...] = _adamw(w_ref[...], go_ref[...], m_ref[...], v_ref[...])

    spec = pl.BlockSpec((SHARD_W, cols), lambda i: (0, i))
    shp = jax.ShapeDtypeStruct((SHARD_W, D_MODEL), F32)
    return _call(body, name="adamw_w_in", grid=(D_MODEL // cols,),
                 in_specs=[pl.BlockSpec((PACK_W, cols), lambda i: (0, i)), spec, spec, spec],
                 out_specs=[spec] * 4, out_shape=[shp] * 4, sem=("parallel",))(g_packed, wt, mt, vt)


def adamw_rows(name, g, w, m, v):
    r, c = g.shape
    rows = min(r, BLOCK)

    def body(g_ref, w_ref, m_ref, v_ref, d_ref, mo_ref, vo_ref):
        d_ref[...], mo_ref[...], vo_ref[...] = _adamw(w_ref[...], g_ref[...], m_ref[...], v_ref[...])

    spec = pl.BlockSpec((rows, c), lambda i: (i, 0))
    shp = jax.ShapeDtypeStruct((r, c), F32)
    return _call(body, name=name, grid=(r // rows,), in_specs=[spec] * 4, out_specs=[spec] * 3, out_shape=[shp] * 3,
                 sem=("parallel",))(g, w, m, v)


def adamw_small(gs, ws, ms, vs):
    n = len(gs)

    def body(*refs):
        g, w, m, v = refs[:n], refs[n:2 * n], refs[2 * n:3 * n], refs[3 * n:4 * n]
        outs = refs[4 * n:]
        for i in range(n):
            d, mn, vn = _adamw(w[i][...], g[i][...], m[i][...], v[i][...])
            outs[3 * i][...] = d
            outs[3 * i + 1][...] = mn
            outs[3 * i + 2][...] = vn

    specs = [_full(a.shape) for a in gs]
    res = _call(body, name="adamw_small", in_specs=specs * 4,
                out_specs=[s for s in specs for _ in range(3)],
                out_shape=[jax.ShapeDtypeStruct(a.shape, F32) for a in gs for _ in range(3)])(*gs, *ws, *ms, *vs)
    return [tuple(res[3 * i:3 * i + 3]) for i in range(n)]


def sum_slots(name, r, after):
    s, rr, c = r.shape
    rows = min(rr, BLOCK)

    def body(r_ref, after_ref, o_ref):
        acc = r_ref[0].astype(F32)
        for k in range(1, s):
            acc = acc + r_ref[k].astype(F32)
        o_ref[...] = acc

    return _call(body, name=name, grid=(rr // rows,),
                 in_specs=[pl.BlockSpec((s, rows, c), lambda i: (0, i, 0)), _full(after.shape)],
                 out_specs=pl.BlockSpec((rows, c), lambda i: (i, 0)), out_shape=jax.ShapeDtypeStruct((rr, c), F32),
                 sem=("parallel",))(r, after)


def sum_pair(partial, from_sibling):
    s, _, rr, cols = partial.shape
    rows = rr // 2

    def body(c_ref, p_ref, r_ref, o_ref):
        o_ref[...] = (p_ref[...].astype(F32) + r_ref[...].astype(F32)).astype(BF16)

    core = lax.axis_index("c").astype(jnp.int32).reshape(1)
    return pl.pallas_call(
        body, name="sum_pair", out_shape=jax.ShapeDtypeStruct((s, rr, cols), BF16),
        grid_spec=pltpu.PrefetchScalarGridSpec(
            num_scalar_prefetch=1, grid=(s, rr // rows),
            in_specs=[pl.BlockSpec((None, None, rows, cols), lambda k, i, c: (k, c[0], i, 0)),
                      pl.BlockSpec((None, rows, cols), lambda k, i, c: (k, i, 0))],
            out_specs=pl.BlockSpec((None, rows, cols), lambda k, i, c: (k, i, 0))),
        compiler_params=pltpu.CompilerParams(dimension_semantics=("parallel", "parallel"),
                                             vmem_limit_bytes=VMEM_LIMIT))(core, partial, from_sibling)


def _col_tile(n, k):
    if n % 896 == 0 and k <= 1024:
        return 896
    return min(n, 512)


def project(name, x, wt, after):
    m, k = x.shape
    n = wt.shape[0]
    tn = D_MODEL

    def body(x_ref, w_ref, after_ref, o_ref):
        o_ref[...] = lax.dot_general(x_ref[...], w_ref[...], _NT, preferred_element_type=F32)

    return _call(body, name=name, grid=(n // tn,),
                 in_specs=[_full((m, k)), pl.BlockSpec((tn, k), lambda j: (j, 0)), _full(after.shape)],
                 out_specs=pl.BlockSpec((m, tn), lambda j: (0, j)), out_shape=jax.ShapeDtypeStruct((m, n), F32),
                 sem=("parallel",))(x, wt, after)


def _piece_tiles(pieces, width):
    spans, start = [], 0
    for p in pieces:
        spans.append((start, p.shape[1] // width))
        start += p.shape[1] // width
    return spans, start


def _piece_spec(block, span, rows_of, tile_of):
    first, count = span

    def index(*pos):
        t = tile_of(*pos) - first
        mine = (t >= 0) & (t < count)
        return jnp.where(mine, rows_of(*pos), 0), jnp.clip(t, 0, count - 1)

    return pl.BlockSpec(block, index)


def project_back(name, pieces, wt, after):
    m = pieces[0].shape[0]
    k = wt.shape[1]
    tm = m // 2
    spans, steps = _piece_tiles(pieces, D_MODEL)

    def body(*refs):
        w_ref, o_ref = refs[len(pieces)], refs[len(pieces) + 2]
        j = pl.program_id(1)

        @pl.when(j == 0)
        def _():
            o_ref[...] = jnp.zeros_like(o_ref)

        for dy_ref, (first, count) in zip(refs, spans):
            @pl.when((j >= first) & (j < first + count))
            def _():
                o_ref[...] += jnp.dot(dy_ref[...], w_ref[...], preferred_element_type=F32)

    return _call(body, name=name, grid=(m // tm, steps),
                 in_specs=[_piece_spec((tm, D_MODEL), s, lambda i, j: i, lambda i, j: j) for s in spans]
                 + [pl.BlockSpec((D_MODEL, k), lambda i, j: (j, 0)), _full(after.shape)],
                 out_specs=pl.BlockSpec((tm, k), lambda i, j: (i, 0)), out_shape=jax.ShapeDtypeStruct((m, k), F32),
                 sem=("parallel", "arbitrary"))(*pieces, wt, after)


GRAD_TILE = 512


def _slab_runs():
    runs = [[] for _ in range(N_ACT // GRAD_TILE)]
    for s in range(4):
        for j in range(PACK_W // LANES):
            tile, r = divmod(_act_col((19 * s + j) * LANES), GRAD_TILE)
            last = runs[tile][-1] if runs[tile] else None
            if last and last[2] == s and last[0] + last[1] == r and last[3] + last[1] == j * LANES:
                last[1] += LANES
            else:
                runs[tile].append([r, LANES, s, j * LANES])
    return runs


def weight_grad_t(name, pieces, order, x):
    m, k = x.shape
    spans, steps = _piece_tiles(pieces, GRAD_TILE)
    runs = _slab_runs()
    most = max(len(r) for r in runs)
    half_rows = PACK_W // 2
    when, tile_at = [None] * len(pieces), []
    for p in order:
        when[p] = (len(tile_at), spans[p][1])
        tile_at += range(spans[p][0], spans[p][0] + spans[p][1])

    landed = {}
    for step in range(steps):
        for _, _, s, _ in runs[tile_at[step]]:
            landed[s] = step + 2

    def body(*refs):
        x_ref, o_ref, land_ref, tile_ref, sems, send_sems, recv_sems = refs[len(pieces):]
        i = pl.program_id(0)
        other = 1 - lax.axis_index("c")

        def copies(step):
            return [pltpu.make_async_copy(tile_ref.at[step % 2, r:r + n], o_ref.at[s, d:d + n],
                                          sems.at[(step % 2) * most + c])
                    for c, (r, n, s, d) in enumerate(runs[tile_at[step]])]

        def to_sibling(s):
            return _remote(o_ref.at[s, pl.ds(other * half_rows, half_rows)], land_ref.at[s], send_sems.at[s],
                           recv_sems.at[s], _peer(1))

        for step in range(2, steps):
            @pl.when(i == step)
            def _():
                for cp in copies(step - 2):
                    cp.wait()

        for dy_ref, (first, count) in zip(refs, when):
            @pl.when((i >= first) & (i < first + count))
            def _():
                tile_ref[i % 2] = lax.dot_general(dy_ref[...], x_ref[...], (((0,), (0,)), ((), ())),
                                                  preferred_element_type=F32).astype(BF16)

        for step in range(steps):
            @pl.when(i == step)
            def _():
                for cp in copies(step):
                    cp.start()
                if step < steps - 1:
                    for s in range(4):
                        if landed[s] == step:
                            to_sibling(s).start()
                else:
                    for cp in copies(step - 1) + copies(step):
                        cp.wait()
                    for s in range(4):
                        if landed[s] >= steps - 1:
                            to_sibling(s).start()
                    for s in range(4):
                        to_sibling(s).wait()

    return _call(body, name=name, grid=(steps,),
                 in_specs=[_piece_spec((m, GRAD_TILE), s, lambda i: 0, lambda i: i) for s in when]
                 + [pl.BlockSpec(memory_space=pltpu.VMEM)],
                 out_specs=[_ANY, _ANY],
                 out_shape=[jax.ShapeDtypeStruct((4, PACK_W, k), BF16), jax.ShapeDtypeStruct((4, half_rows, k), BF16)],
                 scratch=[pltpu.VMEM((2, GRAD_TILE, k), BF16), pltpu.SemaphoreType.DMA((2 * most,)),
                          pltpu.SemaphoreType.DMA((4,)), pltpu.SemaphoreType.DMA((4,))],
                 sem=("arbitrary",))(*pieces, x)


def mm_tn(name, x, dy):
    m, k = x.shape
    n = dy.shape[1]
    tn = _col_tile(n, k)

    def body(x_ref, dy_ref, o_ref):
        o_ref[...] = lax.dot_general(x_ref[...], dy_ref[...], (((0,), (0,)), ((), ())),
                                     preferred_element_type=F32).astype(BF16)

    return _call(body, name=name, grid=(n // tn,),
                 in_specs=[_full((m, k)), pl.BlockSpec((m, tn), lambda i: (0, i))],
                 out_specs=pl.BlockSpec((k, tn), lambda i: (0, i)), out_shape=jax.ShapeDtypeStruct((k, n), BF16),
                 sem=("parallel",))(x, dy)


def _row_spec(width, col_block=0):
    return pl.BlockSpec((BLOCK, width), lambda i: (i, col_block))


def _x_spec():
    return pl.BlockSpec((None, BLOCK, D_MODEL), lambda i: (0, jnp.maximum(i - 1, 0), 0))


def prep(name, x, g_pre, gather):
    nb = x.shape[1] // BLOCK + 1
    shards, _, shard_cols = gather.out_shape[-1].shape

    def body(x_ref, g_ref, h_ref, u_ref, meta_ref, sems):
        @pl.when(pl.program_id(0) < nb - 1)
        def _():
            h_ref[...] = x_ref[...]
            u_ref[...] = _rms(x_ref[...], g_ref[...]).astype(BF16)

    def first_block(ins, gathered, outs, scratch):
        g_ref, (h_ref, u_ref), (meta_ref, sems) = ins[1], outs, scratch
        copies = [pltpu.make_async_copy(gathered[-1].at[t], meta_ref.at[t], sems.at[t]) for t in range(shards)]
        for cp in copies:
            cp.start()
        for cp in copies:
            cp.wait()
        h_ref[0:PAD_ROWS, :] = jnp.zeros((PAD_ROWS, D_MODEL), F32)
        h_ref[PAD_ROWS:BLOCK, :] = jnp.concatenate([meta_ref[t] for t in range(shards)], axis=1)
        u_ref[...] = _rms(h_ref[...], g_ref[...]).astype(BF16)

    def at():
        i = pl.program_id(0)
        return i == 0, i == nb // 2, i == nb - 1

    rows = pl.BlockSpec((BLOCK, D_MODEL), lambda i: ((i + 1) % nb, 0))
    return _call_with_comm(
        body, gather, at, (x, g_pre), name=name, grid=(nb,),
        in_specs=[pl.BlockSpec((None, BLOCK, D_MODEL), lambda i: (0, jnp.minimum(i, nb - 2), 0)), _full((1, D_MODEL))],
        out_specs=[rows, rows],
        out_shape=[jax.ShapeDtypeStruct((nb * BLOCK, D_MODEL), F32), jax.ShapeDtypeStruct((nb * BLOCK, D_MODEL), BF16)],
        scratch=[pltpu.VMEM((shards, N_META, shard_cols), F32), pltpu.SemaphoreType.DMA((shards,))],
        after_finish=first_block)


def prep_bwd(h, du, dres, g_pre):
    seq = h.shape[0] - BLOCK
    rows = min(seq, 4 * BLOCK)

    def body(h0_ref, du0_ref, h_ref, du_ref, dres_ref, g_ref, gx_ref, gm_ref, gg_ref):
        i = pl.program_id(0)
        _, vjp = jax.vjp(_rms, h_ref[...], g_ref[...])
        dh, dg = vjp(du_ref[...])
        gx_ref[...] = dh + dres_ref[...]

        @pl.when(i == 0)
        def _():
            _, vjp0 = jax.vjp(_rms, h0_ref[...], g_ref[...])
            dh0, dg0 = vjp0(du0_ref[...])
            gm_ref[...] = dh0[PAD_ROWS:BLOCK, :]
            gg_ref[...] = dg0 + dg

        @pl.when(i > 0)
        def _():
            gg_ref[...] += dg

    seq_rows = pl.BlockSpec((pl.Element(rows), pl.Element(D_MODEL)), lambda i: (pl.multiple_of(BLOCK + rows * i, BLOCK), 0))
    first = pl.BlockSpec((BLOCK, D_MODEL), lambda i: (0, 0))
    return _call(body, name="prep_bwd", grid=(seq // rows,),
                 in_specs=[first, first, seq_rows, seq_rows, seq_rows, _full((1, D_MODEL))],
                 out_specs=[pl.BlockSpec((None, rows, D_MODEL), lambda i: (0, i, 0)), _full((N_META, D_MODEL)),
                            _full((1, D_MODEL))],
                 out_shape=[jax.ShapeDtypeStruct((1, seq, D_MODEL), F32),
                            jax.ShapeDtypeStruct((N_META, D_MODEL), F32), jax.ShapeDtypeStruct((1, D_MODEL), F32)],
                 sem=("arbitrary",))(h, du, h, du, dres, g_pre)


GROUP_W = SSM_INNER // SSM_GROUPS


def _gated_norm(y, z, g):
    t = y * _silu(z)
    return t * lax.rsqrt(jnp.mean(t * t, axis=-1, keepdims=True) + NORM_EPS) * g


def _gated_norm_groups(y, z, g):
    groups = [slice(k * GROUP_W, (k + 1) * GROUP_W) for k in range(SSM_GROUPS)]
    return jnp.concatenate([_gated_norm(y[:, s], z[:, s], g[:, s]) for s in groups], axis=1)


def _merge(ga, gs, ya, ys):
    return _sigmoid(ga) * ya + _sigmoid(gs) * ys


GATE_ATT_BLOCK = SEG["gate_att"][2] // D_MODEL
GATE_SSM_BLOCK = SEG["gate_ssm"][2] // D_MODEL


def _row_loss(out, g_post, x, target):
    diff = x + _rms(out, g_post) - target
    return 0.5 * jnp.sum(diff * diff) / D_MODEL


def tail(y_ssd, proj, a_att, x, target, woa, wos, wo, g_norm, g_post):
    nb = y_ssd.shape[0] // BLOCK
    rows = nb * BLOCK

    def body(y_ref, z_ref, ga_ref, gs_ref, a_ref, x_ref, t_ref, woa_ref, wos_ref, wo_ref, gn_ref, gp_ref,
             yn_ref, mg_ref, dout_ref, dya_ref, dys_ref, da_ref, dy_ref, dz_ref, dga_ref, dgs_ref, dres_ref,
             loss_ref, dgp_ref, dgn_ref):
        i = pl.program_id(0)
        yn, norm_vjp = jax.vjp(_gated_norm_groups, y_ref[...], z_ref[...], gn_ref[...])
        yn16 = yn.astype(BF16)
        y_ssm = jnp.dot(yn16, wos_ref[...], preferred_element_type=F32)
        y_att = jnp.dot(a_ref[...], woa_ref[...], preferred_element_type=F32)
        merged, merge_vjp = jax.vjp(_merge, ga_ref[...], gs_ref[...], y_att, y_ssm)
        merged16 = merged.astype(BF16)
        out = jnp.dot(merged16, wo_ref[...], preferred_element_type=F32)
        loss, loss_vjp = jax.vjp(_row_loss, out, gp_ref[...], x_ref[...], t_ref[...])
        counted = jnp.where(i > 0, 1.0, 0.0)
        dout, dgp, dres, _ = loss_vjp(counted)
        dout16 = dout.astype(BF16)
        dmerged = lax.dot_general(dout16, wo_ref[...], _NT, preferred_element_type=F32)
        dga, dgs, dya, dys = merge_vjp(dmerged)
        dya16, dys16 = dya.astype(BF16), dys.astype(BF16)
        dyn = lax.dot_general(dys16, wos_ref[...], _NT, preferred_element_type=F32)
        dy, dz, dgn = norm_vjp(dyn)

        yn_ref[...] = yn16
        mg_ref[...] = merged16
        dout_ref[...] = dout16
        dya_ref[...] = dya16
        dys_ref[...] = dys16
        da_ref[...] = lax.dot_general(dya16, woa_ref[...], _NT, preferred_element_type=F32)
        dy_ref[...] = dy
        dz_ref[...] = dz.astype(BF16)
        dga_ref[...] = dga.astype(BF16)
        dgs_ref[...] = dgs.astype(BF16)
        dres_ref[...] = dres

        @pl.when(i == 0)
        def _():
            loss_ref[...] = jnp.zeros_like(loss_ref)
            dgp_ref[...] = jnp.zeros_like(dgp_ref)
            dgn_ref[...] = jnp.zeros_like(dgn_ref)

        loss_ref[...] += loss * counted
        dgp_ref[...] += dgp
        dgn_ref[...] += dgn

    wide, narrow = _row_spec(SSM_INNER), _row_spec(D_MODEL)
    resident = pl.BlockSpec(memory_space=pltpu.VMEM)
    bf = lambda w: jax.ShapeDtypeStruct((rows, w), BF16)
    f32 = lambda w: jax.ShapeDtypeStruct((rows, w), F32)
    return _call(body, name="tail", grid=(nb,),
                 in_specs=[wide, wide, _row_spec(D_MODEL, GATE_ATT_BLOCK), _row_spec(D_MODEL, GATE_SSM_BLOCK), narrow,
                           _x_spec(), _x_spec(), resident, resident, resident, _full((1, SSM_INNER)),
                           _full((1, D_MODEL))],
                 out_specs=[wide, narrow, narrow, narrow, narrow, narrow, wide, wide, narrow, narrow, narrow,
                            _full((8, LANES)), _full((1, D_MODEL)), _full((1, SSM_INNER))],
                 out_shape=[bf(SSM_INNER), bf(D_MODEL), bf(D_MODEL), bf(D_MODEL), bf(D_MODEL), f32(D_MODEL),
                            f32(SSM_INNER), bf(SSM_INNER), bf(D_MODEL), bf(D_MODEL), f32(D_MODEL),
                            jax.ShapeDtypeStruct((8, LANES), F32), jax.ShapeDtypeStruct((1, D_MODEL), F32),
                            jax.ShapeDtypeStruct((1, SSM_INNER), F32)],
                 sem=("arbitrary",))(y_ssd, proj, proj, proj, a_att, x, target, woa, wos, wo, g_norm, g_post)


_NT = (((1,), (1,)), ((), ()))
ALIBI_SLOPES = tuple(2.0 ** (-8.0 * (h + 1) / ATT_Q_HEADS) for h in range(ATT_Q_HEADS))
KV_WIDTH = ATT_KV_HEADS * HEAD_DIM
Q_BLOCK = SEG["q"][2] // D_MODEL
Z_ATT_BLOCK = SEG["z_att"][2] // D_MODEL
K_BLOCK = SEG["k"][2] // KV_WIDTH
V_BLOCK = SEG["v"][2] // KV_WIDTH
META_ROW_BLOCK = PAD_ROWS // N_META


@jax.custom_vjp
def _swap_halves(x):
    return pltpu.roll(x, HEAD_DIM, 1)


_swap_halves.defvjp(lambda x: (pltpu.roll(x, HEAD_DIM, 1), None), lambda _, g: (pltpu.roll(g, HEAD_DIM, 1),))


def _both_halves(t, half):
    first = lax.broadcasted_iota(jnp.int32, t.shape, 1) < HEAD_DIM
    sw = _swap_halves(t)
    return jnp.where(first, t, sw) if half == 0 else jnp.where(first, sw, t)


def _attn_rows(q, z, kp, kc, vp, vc, km, vm, sinks, n):
    rows = ATT_GROUP * BLOCK
    i = lax.broadcasted_iota(jnp.int32, (rows, BLOCK), 0) & (BLOCK - 1)
    j = lax.broadcasted_iota(jnp.int32, (rows, BLOCK), 1)
    rel_c = (i - j).astype(F32)
    rel_p = rel_c + float(BLOCK)
    nv = jnp.zeros((rows, BLOCK), jnp.int32) + n
    ok_c = (i >= j) & (nv >= 1)
    ok_p = (j > i) & (nv >= 2)
    im = lax.broadcasted_iota(jnp.int32, (rows, N_META), 0) & (BLOCK - 1)
    jm = lax.broadcasted_iota(jnp.int32, (rows, N_META), 1)
    ok_m = ((jnp.zeros((rows, N_META), jnp.int32) + n) >= 1) | (im >= PAD_ROWS + jm)
    first = lax.broadcasted_iota(jnp.int32, (BLOCK, LANES), 1) < HEAD_DIM
    neg = -jnp.inf
    outs = []
    for kv in range(ATT_KV_HEADS):
        tile, half = divmod(kv, 2)
        lanes = slice(tile * LANES, (tile + 1) * LANES)
        kc2, kp2, km2 = (_both_halves(t[:, lanes], half).astype(BF16) for t in (kc, kp, km))
        vc2, vp2, vm2 = (_both_halves(t[:, lanes], half).astype(BF16) for t in (vc, vp, vm))
        qs, slope, sk = [], [], []
        for pair in range(ATT_GROUP // 2):
            c0 = (kv * ATT_GROUP + 2 * pair) * HEAD_DIM
            qp = q[:, c0:c0 + LANES] * HEAD_DIM ** -0.5
            qs += [jnp.where(first, qp, 0.0), jnp.where(first, 0.0, qp)]
        for g in range(ATT_GROUP):
            slope.append(jnp.full((BLOCK, 1), ALIBI_SLOPES[kv * ATT_GROUP + g], F32))
            sk.append(jnp.broadcast_to(sinks[kv * ATT_GROUP + g], (BLOCK, 1)))
        qs = jnp.concatenate(qs, axis=0).astype(BF16)
        slope = jnp.concatenate(slope, axis=0)
        sk = jnp.concatenate(sk, axis=0)
        sc = jnp.where(ok_c, lax.dot_general(qs, kc2, _NT, preferred_element_type=F32) - slope * rel_c, neg)
        sp = jnp.where(ok_p, lax.dot_general(qs, kp2, _NT, preferred_element_type=F32) - slope * rel_p, neg)
        sm = jnp.where(ok_m, lax.dot_general(qs, km2, _NT, preferred_element_type=F32), neg)
        mx = jnp.maximum(jnp.maximum(jnp.max(sc, axis=1, keepdims=True), jnp.max(sp, axis=1, keepdims=True)),
                         jnp.maximum(jnp.max(sm, axis=1, keepdims=True), sk))
        mx = lax.stop_gradient(mx)
        ec, ep, em, es = jnp.exp(sc - mx), jnp.exp(sp - mx), jnp.exp(sm - mx), jnp.exp(sk - mx)
        den = (es + jnp.sum(ec, axis=1, keepdims=True) + jnp.sum(ep, axis=1, keepdims=True)
               + jnp.sum(em, axis=1, keepdims=True))
        inv = 1.0 / den
        o = (jnp.dot((ec * inv).astype(BF16), vc2, preferred_element_type=F32)
             + jnp.dot((ep * inv).astype(BF16), vp2, preferred_element_type=F32)
             + jnp.dot((em * inv).astype(BF16), vm2, preferred_element_type=F32))
        for pair in range(ATT_GROUP // 2):
            r0 = 2 * pair * BLOCK
            outs.append(jnp.where(first, o[r0:r0 + BLOCK], o[r0 + BLOCK:r0 + 2 * BLOCK]))
    return jnp.concatenate(outs, axis=1) * _silu(z)


def _attn_specs(nb, steps_clamped):
    def blk(t):
        return jnp.minimum(t, nb - 1) if steps_clamped else t

    wide = lambda col: pl.BlockSpec((BLOCK, D_MODEL), lambda t: (blk(t), col))
    cur = lambda col: pl.BlockSpec((BLOCK, KV_WIDTH), lambda t: (blk(t), col))
    prev = lambda col: pl.BlockSpec((BLOCK, KV_WIDTH), lambda t: (jnp.maximum(blk(t) - 1, 0), col))
    meta = lambda col: pl.BlockSpec((N_META, KV_WIDTH), lambda t: (META_ROW_BLOCK, col))
    sinks = pl.BlockSpec((ATT_Q_HEADS, 1, 1), lambda t: (0, 0, 0))
    return [wide(Q_BLOCK), wide(Z_ATT_BLOCK), prev(K_BLOCK), cur(K_BLOCK), prev(V_BLOCK), cur(V_BLOCK),
            meta(K_BLOCK), meta(V_BLOCK), sinks]


def attn_fwd(proj, sinks):
    nb = proj.shape[0] // BLOCK

    def body(q_ref, z_ref, kp_ref, kc_ref, vp_ref, vc_ref, km_ref, vm_ref, sk_ref, o_ref):
        o_ref[...] = _attn_rows(q_ref[...], z_ref[...], kp_ref[...], kc_ref[...], vp_ref[...], vc_ref[...],
                                km_ref[...], vm_ref[...], tuple(sk_ref[h] for h in range(ATT_Q_HEADS)),
                                pl.program_id(0)).astype(BF16)

    return _call(body, name="attn_fwd", grid=(nb,), in_specs=_attn_specs(nb, False), out_specs=_row_spec(D_MODEL),
                 out_shape=jax.ShapeDtypeStruct((nb * BLOCK, D_MODEL), BF16), sem=("parallel",))(*([proj] * 8), sinks)


def attn_bwd(da, proj, sinks):
    nb = proj.shape[0] // BLOCK
    last = nb - 1
    wide = pl.BlockSpec((BLOCK, D_MODEL), lambda t: (jnp.minimum(t, last), 0))
    done = pl.BlockSpec((BLOCK, KV_WIDTH), lambda t: (jnp.maximum(t - 1, 0), 0))
    meta = _full((N_META, KV_WIDTH))
    par = _full((ATT_Q_HEADS, 1, 1))

    def body(da_ref, q_ref, z_ref, kp_ref, kc_ref, vp_ref, vc_ref, km_ref, vm_ref, sk_ref,
             dq_ref, dz_ref, dk_ref, dv_ref, dkm_ref, dvm_ref, dsk_ref, ck_ref, cv_ref):
        t = pl.program_id(0)

        @pl.when(t == 0)
        def _():
            ck_ref[...] = jnp.zeros_like(ck_ref)
            cv_ref[...] = jnp.zeros_like(cv_ref)
            dkm_ref[...] = jnp.zeros_like(dkm_ref)
            dvm_ref[...] = jnp.zeros_like(dvm_ref)
            dsk_ref[...] = jnp.zeros_like(dsk_ref)

        @pl.when(t < nb)
        def _():
            def f(q, z, kp, kc, vp, vc, km, vm, sk):
                return _attn_rows(q, z, kp, kc, vp, vc, km, vm, sk, t)

            _, vjp = jax.vjp(f, q_ref[...], z_ref[...], kp_ref[...], kc_ref[...], vp_ref[...], vc_ref[...],
                             km_ref[...], vm_ref[...], tuple(sk_ref[h] for h in range(ATT_Q_HEADS)))
            dq, dz, dkp, dkc, dvp, dvc, dkm, dvm, dsk = vjp(da_ref[...])
            dq_ref[...] = dq.astype(BF16)
            dz_ref[...] = dz.astype(BF16)
            for h in range(ATT_Q_HEADS):
                dsk_ref[h] += dsk[h]
            dk_ref[...] = ck_ref[...] + dkp
            dv_ref[...] = cv_ref[...] + dvp
            ck_ref[...] = dkc
            cv_ref[...] = dvc
            dkm_ref[...] += dkm
            dvm_ref[...] += dvm

        @pl.when(t == nb)
        def _():
            dk_ref[...] = ck_ref[...]
            dv_ref[...] = cv_ref[...]

    rows = nb * BLOCK
    return _call(body, name="attn_bwd", grid=(nb + 1,), in_specs=[wide] + _attn_specs(nb, True),
                 out_specs=[wide, wide, done, done, meta, meta, par],
                 out_shape=[jax.ShapeDtypeStruct((rows, D_MODEL), BF16), jax.ShapeDtypeStruct((rows, D_MODEL), BF16),
                            jax.ShapeDtypeStruct((rows, KV_WIDTH), F32), jax.ShapeDtypeStruct((rows, KV_WIDTH), F32),
                            jax.ShapeDtypeStruct((N_META, KV_WIDTH), F32), jax.ShapeDtypeStruct((N_META, KV_WIDTH), F32),
                            jax.ShapeDtypeStruct(sinks.shape, F32)],
                 scratch=[pltpu.VMEM((BLOCK, KV_WIDTH), F32), pltpu.VMEM((BLOCK, KV_WIDTH), F32)],
                 sem=("arbitrary",))(da, *([proj] * 8), sinks)


XBC_BLOCK0 = SEG["xbc"][2] // D_MODEL
CONV_COL_BLOCKS = CONV_DIM // D_MODEL
DT_TILE = SEG["dt"][2] // LANES


HALO = 8


def _conv_rows(length):
    return 544 if length % 544 == 0 else BLOCK


def _shift_rows(cur, before, j):
    if j == 0:
        return cur
    n = cur.shape[0]
    row = lax.broadcasted_iota(jnp.int32, cur.shape, 0)
    head = pltpu.roll(before, j, 0)
    if n > HALO:
        head = jnp.concatenate([head, jnp.zeros((n - HALO, cur.shape[1]), cur.dtype)], axis=0)
    return jnp.where(row >= j, pltpu.roll(cur, j, 0), head)


def _conv_pre(cur, before, w_ref, b_ref):
    pre = b_ref[...] + w_ref[CONV_WIDTH - 1:CONV_WIDTH, :] * cur
    for k in range(CONV_WIDTH - 1):
        pre = pre + w_ref[k:k + 1, :] * _shift_rows(cur, before, CONV_WIDTH - 1 - k)
    return pre


def _conv_specs(steps, rows, col0=0):
    first = XBC_BLOCK0 + col0
    halos = rows // HALO
    cur = pl.BlockSpec((rows, D_MODEL), lambda j, i: (i, first + j))
    before = pl.BlockSpec((HALO, D_MODEL), lambda j, i: (jnp.maximum(i * halos - 1, 0), first + j))
    after = pl.BlockSpec((HALO, D_MODEL), lambda j, i: (jnp.minimum(i + 1, steps - 1) * halos, first + j))
    return cur, before, after


def _valid_rows(i, rows):
    row = lax.broadcasted_iota(jnp.int32, (rows, D_MODEL), 0)
    return jnp.maximum((row >= PAD_ROWS).astype(F32), jnp.where(i > 0, 1.0, 0.0))


def conv_fwd(proj, conv_w, conv_b):
    rows = _conv_rows(proj.shape[0])
    steps = proj.shape[0] // rows
    cur, before, _ = _conv_specs(steps, rows)

    def body(c_ref, p_ref, w_ref, b_ref, o_ref):
        i = pl.program_id(1)
        pre = _conv_pre(c_ref[...], p_ref[...] * jnp.where(i > 0, 1.0, 0.0), w_ref, b_ref)
        o_ref[...] = _silu(pre) * _valid_rows(i, rows)

    return _call(body, name="conv_fwd", grid=(CONV_COL_BLOCKS, steps),
                 in_specs=[cur, before, pl.BlockSpec((CONV_WIDTH, D_MODEL), lambda j, i: (0, j)),
                           pl.BlockSpec((1, D_MODEL), lambda j, i: (0, j))],
                 out_specs=pl.BlockSpec((rows, D_MODEL), lambda j, i: (i, j)),
                 out_shape=jax.ShapeDtypeStruct((proj.shape[0], CONV_DIM), F32),
                 sem=("parallel", "parallel"))(proj, proj, conv_w, conv_b)


def conv_bwd(name, dparts, col0, proj, conv_w, conv_b):
    rows = _conv_rows(proj.shape[0])
    steps = proj.shape[0] // rows
    last = steps - 1
    ncol = sum(d.shape[1] for d in dparts) // D_MODEL
    np_ = len(dparts)
    cur, before, after = _conv_specs(steps, rows, col0)
    dcur = [pl.BlockSpec((rows, d.shape[1] // ncol), lambda j, i: (i, j)) for d in dparts]
    dafter = [pl.BlockSpec((HALO, d.shape[1] // ncol), lambda j, i: (jnp.minimum(i + 1, last) * (rows // HALO), j))
              for d in dparts]
    out_cur = pl.BlockSpec((rows, D_MODEL), lambda j, i: (i, j))
    wspec = pl.BlockSpec((CONV_WIDTH, D_MODEL), lambda j, i: (0, col0 + j))
    bspec = pl.BlockSpec((1, D_MODEL), lambda j, i: (0, col0 + j))
    wout = pl.BlockSpec((CONV_WIDTH, D_MODEL), lambda j, i: (0, j))
    bout = pl.BlockSpec((1, D_MODEL), lambda j, i: (0, j))

    def body(*refs):
        dc_refs, da_refs = refs[:np_], refs[np_:2 * np_]
        c_ref, p_ref, a_ref, w_ref, b_ref, du_ref, dw_ref, db_ref = refs[2 * np_:]
        i = pl.program_id(1)
        row = lax.broadcasted_iota(jnp.int32, (rows, D_MODEL), 0)
        curv = c_ref[...]
        beforev = p_ref[...] * jnp.where(i > 0, 1.0, 0.0)
        side_by_side = lambda rs: rs[0][...] if np_ == 1 else jnp.concatenate([r[...] for r in rs], axis=1)

        def dpre_of(pre, d):
            s = _sigmoid(pre)
            return d * (s * (1.0 + pre * (1.0 - s)))

        dp_c = dpre_of(_conv_pre(curv, beforev, w_ref, b_ref), side_by_side(dc_refs) * _valid_rows(i, rows))
        dp_a = dpre_of(_conv_pre(a_ref[...], curv[rows - HALO:], w_ref, b_ref),
                       side_by_side(da_refs) * jnp.where(i < last, 1.0, 0.0))
        du = w_ref[CONV_WIDTH - 1:CONV_WIDTH, :] * dp_c
        for j in range(1, CONV_WIDTH):
            tail = jnp.concatenate([jnp.zeros((rows - HALO, D_MODEL), F32), pltpu.roll(dp_a, HALO - j, 0)], axis=0)
            up = jnp.where(row < rows - j, pltpu.roll(dp_c, rows - j, 0), tail)
            du = du + w_ref[CONV_WIDTH - 1 - j:CONV_WIDTH - j, :] * up
        du_ref[...] = du.astype(BF16)

        @pl.when(i == 0)
        def _():
            dw_ref[...] = jnp.zeros_like(dw_ref)
            db_ref[...] = jnp.zeros_like(db_ref)

        for k in range(CONV_WIDTH):
            dw_ref[k:k + 1, :] += jnp.sum(dp_c * _shift_rows(curv, beforev, CONV_WIDTH - 1 - k), axis=0, keepdims=True)
        db_ref[...] += jnp.sum(dp_c, axis=0, keepdims=True)

    width = ncol * D_MODEL
    return _call(body, name=name, grid=(ncol, steps),
                 in_specs=dcur + dafter + [cur, before, after, wspec, bspec], out_specs=[out_cur, wout, bout],
                 out_shape=[jax.ShapeDtypeStruct((proj.shape[0], width), BF16),
                            jax.ShapeDtypeStruct((CONV_WIDTH, width), F32), jax.ShapeDtypeStruct((1, width), F32)],
                 sem=("parallel", "arbitrary"))(*dparts, *dparts, proj, proj, proj, conv_w, conv_b)


def _head_expand():
    e = np.zeros((LANES, SSM_INNER), np.float32)
    for h in range(SSM_HEADS):
        e[h, h * HEAD_DIM:(h + 1) * HEAD_DIM] = 1.0
    return jnp.asarray(e, dtype=BF16)


def _softplus(x):
    return jnp.maximum(x, 0.0) + jnp.log(1.0 + jnp.exp(-jnp.abs(x)))


def _bf16_parts(x):
    hi = x.astype(BF16)
    rest = x - hi.astype(F32)
    mid = rest.astype(BF16)
    return hi, mid, (rest - mid.astype(F32)).astype(BF16)


@jax.custom_vjp
def _times_01(x, m):
    return sum(jnp.dot(p, m, preferred_element_type=F32) for p in _bf16_parts(x))


def _times_01_bwd(m, g):
    return sum(lax.dot_general(p, m, _NT, preferred_element_type=F32) for p in _bf16_parts(g)), jnp.zeros_like(m)


_times_01.defvjp(lambda x, m: (_times_01(x, m), m), _times_01_bwd)


def _causal_ones():
    l = lax.broadcasted_iota(jnp.int32, (BLOCK, BLOCK), 0)
    s = lax.broadcasted_iota(jnp.int32, (BLOCK, BLOCK), 1)
    return (l >= s).astype(BF16)


@jax.custom_vjp
def _cumsum_rows(a):
    return sum(jnp.dot(_causal_ones(), p, preferred_element_type=F32) for p in _bf16_parts(a))


def _cumsum_rows_bwd(_, g):
    tn = (((0,), (0,)), ((), ()))
    return (sum(lax.dot_general(_causal_ones(), p, tn, preferred_element_type=F32) for p in _bf16_parts(g)),)


_cumsum_rows.defvjp(lambda a: (_cumsum_rows(a), None), _cumsum_rows_bwd)


def _ssd_heads(dt_tile, bias, alog, dsk, expand):
    dt = _softplus(dt_tile + bias)
    a = dt * (-jnp.exp(alog))
    one_row = lambda v: jnp.broadcast_to(v, (HALO, LANES))
    return _times_01(jnp.concatenate([dt, _cumsum_rows(a), one_row(jnp.sum(a, axis=0, keepdims=True)), one_row(dsk)],
                                     axis=0), expand)


HEADS_ROWS = 2 * BLOCK + 2 * HALO


def _ssd_group(xs, per_lane, bg, cg, state):
    l = lax.broadcasted_iota(jnp.int32, (BLOCK, BLOCK), 0)
    s = lax.broadcasted_iota(jnp.int32, (BLOCK, BLOCK), 1)
    causal = l >= s
    first_head = s < HEAD_DIM
    dtx, cs = per_lane[0:BLOCK], per_lane[BLOCK:2 * BLOCK]
    tot, dsk = per_lane[2 * BLOCK:2 * BLOCK + 1], per_lane[2 * BLOCK + HALO:2 * BLOCK + HALO + 1]
    bb, cb16 = bg.astype(BF16), cg.astype(BF16)
    cb = lax.dot_general(cb16, bb, _NT, preferred_element_type=F32)
    xr = xs * dtx
    y_diag = []
    for p in range(GROUP_W // LANES):
        lanes = slice(p * LANES, (p + 1) * LANES)
        c_pair = cs[:, lanes]
        c_swap = _swap_halves(c_pair)
        m = []
        for c_head in (jnp.where(first_head, c_pair, c_swap), jnp.where(first_head, c_swap, c_pair)):
            m.append(cb * jnp.exp(jnp.where(causal, c_head - c_head.T, -jnp.inf)))
        x_pair = xr[:, lanes]
        x_diag = jnp.concatenate([jnp.where(first_head, x_pair, 0.0), jnp.where(first_head, 0.0, x_pair)], axis=0)
        y_diag.append(jnp.dot(jnp.concatenate(m, axis=1).astype(BF16), x_diag.astype(BF16),
                              preferred_element_type=F32))
    st = lax.dot_general(bb, (xr * jnp.exp(tot - cs)).astype(BF16), (((0,), (0,)), ((), ())),
                         preferred_element_type=F32)
    new_state = state * jnp.exp(tot) + st
    y_off = jnp.dot(cb16, state.astype(BF16), preferred_element_type=F32) * jnp.exp(cs)
    return jnp.concatenate(y_diag, axis=1) + y_off + dsk * xs, new_state


BC_WIDTH = SSM_GROUPS * SSM_STATE


def _ssd_specs(chunk):
    xs = pl.BlockSpec((BLOCK, SSM_INNER), lambda c: (chunk(c), 0))
    dt = pl.BlockSpec((BLOCK, LANES), lambda c: (chunk(c), DT_TILE))
    expand = _full((LANES, SSM_INNER))
    b = pl.BlockSpec((BLOCK, BC_WIDTH), lambda c: (chunk(c), SSM_INNER // BC_WIDTH))
    cc = pl.BlockSpec((BLOCK, BC_WIDTH), lambda c: (chunk(c), SSM_INNER // BC_WIDTH + 1))
    par = _full((1, LANES))
    state = pl.BlockSpec((None, SSM_STATE, SSM_INNER), lambda c: (chunk(c), 0, 0))
    return xs, dt, expand, b, cc, par, state


def _group_lanes(g):
    return slice(g * GROUP_W, (g + 1) * GROUP_W), slice(g * SSM_STATE, (g + 1) * SSM_STATE)


def ssd_fwd(xbc, proj, expand, bias, alog, dsk):
    nb = xbc.shape[0] // BLOCK
    xs, dt, ex, b, cc, par, state = _ssd_specs(lambda c: c)

    def body(x_ref, dt_ref, e_ref, bi_ref, al_ref, dk_ref, b_ref, c_ref, y_ref, sp_ref, st_ref):
        @pl.when(pl.program_id(0) == 0)
        def _():
            st_ref[...] = jnp.zeros_like(st_ref)

        per_lane = _ssd_heads(dt_ref[...], bi_ref[...], al_ref[...], dk_ref[...], e_ref[...])
        for g in range(SSM_GROUPS):
            wide, tile = _group_lanes(g)
            entering = st_ref[:, wide]
            sp_ref[:, wide] = entering
            y_ref[:, wide], st_ref[:, wide] = _ssd_group(x_ref[:, wide], per_lane[:, wide], b_ref[:, tile],
                                                         c_ref[:, tile], entering)

    return _call(body, name="ssd_fwd", grid=(nb,), in_specs=[xs, dt, ex, par, par, par, b, cc],
                 out_specs=[xs, state],
                 out_shape=[jax.ShapeDtypeStruct((nb * BLOCK, SSM_INNER), F32),
                            jax.ShapeDtypeStruct((nb, SSM_STATE, SSM_INNER), F32)],
                 scratch=[pltpu.VMEM((SSM_STATE, SSM_INNER), F32)],
                 sem=("arbitrary",))(xbc, proj, expand, bias, alog, dsk, xbc, xbc)


def ssd_bwd(dy, xbc, proj, expand, bias, alog, dsk, states, comm):
    nb = xbc.shape[0] // BLOCK
    last = nb - 1
    xs, dt, ex, b, cc, par, state = _ssd_specs(lambda c: last - c)
    tile = pl.BlockSpec((BLOCK, LANES), lambda c: (last - c, 0))
    nspec = pl.BlockSpec((BLOCK, BC_WIDTH), lambda c: (last - c, 0))

    def body(dy_ref, x_ref, dt_ref, e_ref, bi_ref, al_ref, dk_ref, b_ref, c_ref, sp_ref,
             dx_ref, ddt_ref, db_ref, dc_ref, dbi_ref, dal_ref, ddk_ref, ds_ref):
        @pl.when(pl.program_id(0) == 0)
        def _():
            ds_ref[...] = jnp.zeros_like(ds_ref)
            dbi_ref[...] = jnp.zeros_like(dbi_ref)
            dal_ref[...] = jnp.zeros_like(dal_ref)
            ddk_ref[...] = jnp.zeros_like(ddk_ref)

        expand = e_ref[...]
        per_lane, heads_vjp = jax.vjp(lambda t, bi, al, dk: _ssd_heads(t, bi, al, dk, expand), dt_ref[...], bi_ref[...],
                                      al_ref[...], dk_ref[...])
        d_per_lane = []
        for g in range(SSM_GROUPS):
            wide, tile_lanes = _group_lanes(g)
            _, vjp = jax.vjp(_ssd_group, x_ref[:, wide], per_lane[:, wide], b_ref[:, tile_lanes], c_ref[:, tile_lanes],
                             sp_ref[:, wide])
            (dx_ref[:, wide], d_lanes, db_ref[:, tile_lanes], dc_ref[:, tile_lanes],
             ds_ref[:, wide]) = vjp((dy_ref[:, wide], ds_ref[:, wide]))
            d_per_lane.append(d_lanes)
        ddt, dbi, dal, ddk = heads_vjp(jnp.concatenate(d_per_lane, axis=1))
        ddt_ref[...] = ddt.astype(BF16)
        dbi_ref[...] += dbi
        dal_ref[...] += dal
        ddk_ref[...] += ddk

    def at():
        c = pl.program_id(0)
        return c == 0, c == 0, c == last

    par_shape = jax.ShapeDtypeStruct((1, LANES), F32)
    return _call_with_comm(
        body, comm, at, (dy, xbc, proj, expand, bias, alog, dsk, xbc, xbc, states), name="ssd_bwd",
        grid=(nb,), in_specs=[xs, xs, dt, ex, par, par, par, b, cc, state],
        out_specs=[xs, tile, nspec, nspec, par, par, par],
        out_shape=[jax.ShapeDtypeStruct((nb * BLOCK, SSM_INNER), F32), jax.ShapeDtypeStruct((nb * BLOCK, LANES), BF16),
                   jax.ShapeDtypeStruct((nb * BLOCK, BC_WIDTH), F32), jax.ShapeDtypeStruct((nb * BLOCK, BC_WIDTH), F32),
                   par_shape, par_shape, par_shape],
        scratch=[pltpu.VMEM((SSM_STATE, SSM_INNER), F32)])


SLAB_ROWS = 16
SLAB_META_ROW = 8
SLAB_META_SHAPE = (8, 2 * D_MODEL)
SLAB_LOSS_ROW = 7


def pack_small(dcw, dcb, dgpre, dgpost, dbias, dalog, ddsk, dsinks, dgn, dmeta, loss_tile):
    def body(cw, cb, gpre, gpost, dtb, al, dk, sk, gn, meta, loss, o_ref):
        o_ref[...] = jnp.zeros_like(o_ref)
        o_ref[SLAB_LOSS_ROW:SLAB_LOSS_ROW + 1, 0:LANES] = loss[0:1, :]
        o_ref[0:CONV_WIDTH, :] = cw[...]
        o_ref[4:5, :] = cb[...]
        o_ref[5:6, 0:1024] = gpre[...]
        o_ref[5:6, 1024:2048] = gpost[...]
        o_ref[5:6, 2048:2176] = dtb[...]
        o_ref[5:6, 2176:2304] = al[...]
        o_ref[5:6, 2304:2432] = dk[...]
        o_ref[5:6, 2432:2560] = sk[...]
        o_ref[6:7, 0:SSM_INNER] = gn[...]
        o_ref[SLAB_META_ROW:SLAB_ROWS, 0:SLAB_META_SHAPE[1]] = meta[...]

    args = (dcw, dcb, dgpre, dgpost, dbias, dalog, ddsk, dsinks, dgn, dmeta, loss_tile)
    return _call(body, name="pack_small", in_specs=[_full(a.shape) for a in args],
                 out_specs=_full((SLAB_ROWS, CONV_DIM)), out_shape=jax.ShapeDtypeStruct((SLAB_ROWS, CONV_DIM), F32))(*args)


def _lane_tile(v):
    return jnp.pad(v, ((0, 0), (0, LANES - v.shape[1])))


def kernel(x, meta_tokens, g_pre, w_in, conv_w, conv_b, dt_bias, a_log, d_skip, attn_sinks, g_ssm_norm, w_out_att, w_out_ssm, w_out, g_post, loss_target, m_meta_tokens, m_g_pre, m_w_in, m_conv_w, m_conv_b, m_dt_bias, m_a_log, m_d_skip, m_attn_sinks, m_g_ssm_norm, m_w_out_att, m_w_out_ssm, m_w_out, m_g_post, v_meta_tokens, v_g_pre, v_w_in, v_conv_w, v_conv_b, v_dt_bias, v_a_log, v_d_skip, v_attn_sinks, v_g_ssm_norm, v_w_out_att, v_w_out_ssm, v_w_out, v_g_post):
    chip = _chip_index()

    conv_w_rows = jnp.pad(conv_w[0], ((0, 2 * 8 - CONV_WIDTH), (0, 0)))
    w_in_t, m_w_in_t, v_w_in_t = w_in[0].T, m_w_in[0].T, v_w_in[0].T
    (h, u), (gathered_w_in, g_conv_w, g_meta) = prep(
        "gather_w_in", x, g_pre, TwoLevelGather([pack_w_in(w_in_t), conv_w_rows, meta_tokens]))
    w_all_t = unpack_w_in(gathered_w_in)
    cw_full = g_conv_w[:, :CONV_WIDTH].transpose(1, 0, 2).reshape(CONV_WIDTH, CONV_DIM)
    behind_w_in = 0.0 * g_meta[0, 0, 0]
    w_out_flight, w_out_started = chip_exchange_start(
        "gather_w_out_start", [(w[0] + behind_w_in).astype(BF16) for w in (w_out_att, w_out_ssm, w_out)], False)

    proj = project("in_proj", u, w_all_t, w_out_started)

    sinks3 = attn_sinks.reshape(ATT_Q_HEADS, 1, 1)
    a_att = attn_fwd(proj, sinks3)

    xbc = conv_fwd(proj, cw_full, conv_b)
    expand = _head_expand()
    head_pars = (_lane_tile(dt_bias), _lane_tile(a_log), _lane_tile(d_skip))
    y_ssd, states = ssd_fwd(xbc, proj, expand, *head_pars)
    woa, wos, wo = [g.reshape(-1, D_MODEL) for g in chip_exchange_wait("gather_w_out_wait", w_out_flight, False, y_ssd)]

    (yn, merged, dout, dy_att, dy_ssm, da_att, dy_ssd, dz_ssm, dga, dgs, dres, loss_tile, dg_post, dgn) = tail(
        y_ssd, proj, a_att, x, loss_target, woa, wos, wo, g_ssm_norm, g_post)

    dwo = mm_tn("out_proj_dw", merged, dout)
    dwoa = mm_tn("att_out_dw", a_att, dy_att)
    dwos = mm_tn("ssm_out_dw", yn, dy_ssm)
    dq, dz_att, dk, dv, dkmeta, dvmeta, dsinks3 = attn_bwd(da_att, proj, sinks3)
    dk = dk.at[PAD_ROWS:BLOCK].add(dkmeta).astype(BF16)
    dv = dv.at[PAD_ROWS:BLOCK].add(dvmeta).astype(BF16)

    def pieces(g):
        return g.reshape(4, 2, g.shape[0] // 8, g.shape[1])

    def to_owner(g):
        return (lambda ref, dev: ref.at[_chip_of(dev), dev[2]], (g.shape[0] // 8, g.shape[1]))

    (dxs, ddt_tile, dbg, dcg, dbias, dalog, ddsk), sent_w_out = ssd_bwd(
        dy_ssd, xbc, proj, expand, *head_pars, states,
        DirectExchange([pieces(dwoa), pieces(dwos), pieces(dwo)], [to_owner(dwoa), to_owner(dwos), to_owner(dwo)],
                       ALL_MASKS, "dev", 8))
    dxs_raw, dcw_xs, dcb_xs = conv_bwd("conv_bwd_x", [dxs], 0, proj, cw_full, conv_b)
    dbc_raw, dcw_bc, dcb_bc = conv_bwd("conv_bwd_bc", [dbg, dcg], SSM_INNER // D_MODEL, proj, cw_full, conv_b)
    dcw = jnp.concatenate([dcw_xs, dcw_bc], axis=1)
    dcb = jnp.concatenate([dcb_xs, dcb_bc], axis=1)

    narrow = jnp.concatenate([dk, dv, ddt_tile, jnp.zeros((dk.shape[0], N_ACT - N_ALIGNED), BF16)], axis=1)
    dproj = [dz_ssm, dxs_raw, dbc_raw, dq, dz_att, dga, dgs, narrow]
    by_shard = [7, 3, 4, 0, 1, 2, 5, 6]
    partial, from_sibling = weight_grad_t("in_proj_dw", dproj, by_shard, u)
    grads_flight, started = pair_sum_exchange_start(
        "reduce_w_in_start", partial.reshape(4, 2, PACK_W // 2, D_MODEL), from_sibling)
    du = project_back("in_proj_dx", dproj, w_all_t, started)
    grad_x, dmeta, dg_pre = prep_bwd(h, du, dres, g_pre)

    halves = [sum_slots("sum_" + nm, r, started) for nm, r in zip(("w_out_att", "w_out_ssm", "w_out"), sent_w_out)]
    shared = run_comm("share_w_out", DirectExchange(halves, [None] * 3, SIBLING_MASK, "core", 2))
    g_woa, g_wos, g_wo = [f.reshape(2 * f.shape[1], f.shape[2]) for f in shared]
    d_woa, nm_woa, nv_woa = adamw_rows("adamw_w_out_att", g_woa, w_out_att[0], m_w_out_att[0], v_w_out_att[0])
    d_wos, nm_wos, nv_wos = adamw_rows("adamw_w_out_ssm", g_wos, w_out_ssm[0], m_w_out_ssm[0], v_w_out_ssm[0])
    d_wo, nm_wo, nv_wo = adamw_rows("adamw_w_out", g_wo, w_out[0], m_w_out[0], v_w_out[0])

    window_done = dg_pre + 0.0 * (d_woa[0:1] + d_wos[0:1] + d_wo[0:1])
    sent_w_in, = chip_exchange_wait("reduce_w_in_wait", grads_flight, True, window_done)
    slab = pack_small(dcw, dcb, dg_pre, dg_post, dbias, dalog, ddsk, _lane_tile(dsinks3.reshape(1, ATT_Q_HEADS)), dgn,
                      dmeta.reshape(SLAB_META_SHAPE), loss_tile)
    slabs, shared_w_in = run_comm("share_w_in", Both(DirectExchange([slab], [None], ALL_MASKS, "dev", 8),
                                                     PairOfSums(sent_w_in)))
    small = sum_slots("sum_small", slabs, started)
    loss = small[SLAB_LOSS_ROW, 0]
    g_w_in, d_w_in, nm_w_in, nv_w_in = [
        a.T for a in adamw_w_in(shared_w_in.reshape(PACK_W, D_MODEL), w_in_t, m_w_in_t, v_w_in_t)]

    cw_cols = CONV_DIM // 4
    meta_cols = D_MODEL // 4
    g_small = {
        "meta_tokens": lax.dynamic_slice(
            small[SLAB_META_ROW:SLAB_ROWS, 0:SLAB_META_SHAPE[1]].reshape(N_META, D_MODEL), (0, chip * meta_cols),
            (N_META, meta_cols)),
        "g_pre": small[5:6, 0:1024],
        "conv_w": lax.dynamic_slice(small, (0, chip * cw_cols), (CONV_WIDTH, cw_cols)),
        "conv_b": small[4:5, :],
        "dt_bias": small[5:6, 2048:2048 + SSM_HEADS],
        "a_log": small[5:6, 2176:2176 + SSM_HEADS],
        "d_skip": small[5:6, 2304:2304 + SSM_HEADS],
        "attn_sinks": small[5:6, 2432:2432 + ATT_Q_HEADS],
        "g_ssm_norm": small[6:7, 0:SSM_INNER],
        "g_post": small[5:6, 1024:2048],
    }
    names = list(g_small)
    w_small = dict(meta_tokens=meta_tokens, g_pre=g_pre, conv_w=conv_w[0], conv_b=conv_b, dt_bias=dt_bias, a_log=a_log,
                   d_skip=d_skip, attn_sinks=attn_sinks, g_ssm_norm=g_ssm_norm, g_post=g_post)
    m_small = dict(meta_tokens=m_meta_tokens, g_pre=m_g_pre, conv_w=m_conv_w[0], conv_b=m_conv_b, dt_bias=m_dt_bias,
                   a_log=m_a_log, d_skip=m_d_skip, attn_sinks=m_attn_sinks, g_ssm_norm=m_g_ssm_norm, g_post=m_g_post)
    v_small = dict(meta_tokens=v_meta_tokens, g_pre=v_g_pre, conv_w=v_conv_w[0], conv_b=v_conv_b, dt_bias=v_dt_bias,
                   a_log=v_a_log, d_skip=v_d_skip, attn_sinks=v_attn_sinks, g_ssm_norm=v_g_ssm_norm, g_post=v_g_post)
    upd = dict(zip(names, adamw_small([g_small[k] for k in names], [w_small[k] for k in names],
                                      [m_small[k] for k in names], [v_small[k] for k in names])))

    lead = {"conv_w"}

    def shaped(name, a):
        return a[None] if name in lead else a

    grads = dict(g_small, w_in=g_w_in, w_out_att=g_woa, w_out_ssm=g_wos, w_out=g_wo)
    deltas = dict({k: upd[k][0] for k in names}, w_in=d_w_in, w_out_att=d_woa, w_out_ssm=d_wos, w_out=d_wo)
    new_m = dict({k: upd[k][1] for k in names}, w_in=nm_w_in, w_out_att=nm_woa, w_out_ssm=nm_wos, w_out=nm_wo)
    new_v = dict({k: upd[k][2] for k in names}, w_in=nv_w_in, w_out_att=nv_woa, w_out_ssm=nv_wos, w_out=nv_wo)
    lead |= {"w_in", "w_out_att", "w_out_ssm", "w_out"}
    order = ["meta_tokens", "g_pre", "w_in", "conv_w", "conv_b", "dt_bias", "a_log", "d_skip", "attn_sinks",
             "g_ssm_norm", "w_out_att", "w_out_ssm", "w_out", "g_post"]
    outs = [loss, grad_x]
    for group in (grads, deltas, new_m, new_v):
        outs += [shaped(k, group[k]) for k in order]
    return tuple(outs)
```

```python
import numpy as np
import jax
import jax.numpy as jnp
from jax import lax
from jax.experimental import pallas as pl
from jax.experimental.pallas import tpu as pltpu

F32 = jnp.float32
BF16 = jnp.bfloat16

D_MODEL = 1024
N_META = 16
BLOCK = 128
PAD_ROWS = BLOCK - N_META
NORM_EPS = 1e-6
HEAD_DIM = 64
ATT_Q_HEADS = 16
ATT_KV_HEADS = 4
ATT_GROUP = 4
SSM_INNER = 2048
SSM_HEADS = 32
SSM_GROUPS = 4
SSM_STATE = 128
CONV_WIDTH = 4
CONV_DIM = 3072
LANES = 128

ADAM_LR = 0.001
ADAM_B1 = 0.9
ADAM_B2 = 0.999
ADAM_EPS = 1e-08
ADAM_WD = 0.01
ADAM_STEP = 10

VMEM_LIMIT = 48 * 1024 * 1024

SHARD_W = 2440
PACK_W = 2560
SHARD_STRIDE = 2432
N_ALIGNED = 9856
N_ACT = 10240
SEG = {
    "q": (0, 1024, 5120), "k": (1024, 256, 9216), "v": (1280, 256, 9472), "z_att": (1536, 1024, 6144),
    "z_ssm": (2560, 2048, 0), "xbc": (4608, 3072, 2048), "dt": (7680, 128, 9728),
    "gate_att": (7808, 1024, 7168), "gate_ssm": (8832, 1024, 8192),
}
DT_STORED_START = 7680
DT_PAD = LANES - SSM_HEADS


def _act_col(aligned_col):
    for a0, w, p0 in SEG.values():
        if a0 <= aligned_col < a0 + w:
            return p0 + aligned_col - a0
    raise ValueError(aligned_col)


def _call(body, *, name, out_shape, in_specs, out_specs, grid=(), scratch=(), sem=None, aliases=None):
    return pl.pallas_call(
        body, out_shape=out_shape, grid=grid, in_specs=in_specs, out_specs=out_specs, scratch_shapes=list(scratch),
        name=name, input_output_aliases=aliases or {},
        compiler_params=pltpu.CompilerParams(dimension_semantics=sem, vmem_limit_bytes=VMEM_LIMIT))


def _full(shape):
    n = len(shape)
    return pl.BlockSpec(shape, lambda *_: (0,) * n)


def _chip_index():
    return lax.axis_index("x") * 2 + lax.axis_index("y")


_sigmoid = jax.nn.sigmoid


def _silu(z):
    return z * _sigmoid(z)


def _rms(x, g):
    return x * lax.rsqrt(jnp.mean(x * x, axis=-1, keepdims=True) + NORM_EPS) * g


def _peer(mask):
    x, y, c = lax.axis_index("x"), lax.axis_index("y"), lax.axis_index("c")
    return ((1 - x) if mask & 4 else x, (1 - y) if mask & 2 else y, (1 - c) if mask & 1 else c)


def _me():
    return lax.axis_index("x"), lax.axis_index("y"), lax.axis_index("c")


def _chip_of(dev):
    return 2 * dev[0] + dev[1]


CHIP_MASKS = (4, 2, 6)
ALL_MASKS = (1, 2, 3, 4, 5, 6, 7)
SIBLING_MASK = (1,)


def _remote(src, dst, send_sem, recv_sem, dev):
    return pltpu.make_async_remote_copy(src_ref=src, dst_ref=dst, send_sem=send_sem, recv_sem=recv_sem,
                                        device_id=dev, device_id_type=pl.DeviceIdType.MESH)


class _StagedCopy:
    def __init__(self, src, stage, dst, load_sem, store_sem):
        self.load = pltpu.make_async_copy(src, stage, load_sem)
        self.store = pltpu.make_async_copy(stage, dst, store_sem)

    def start(self):
        self.load.start()
        self.load.wait()
        self.store.start()

    def wait(self):
        self.store.wait()


class DirectExchange:
    def __init__(self, arrays, pieces, masks, slot_kind, nslots):
        self.arrays, self.pieces, self.masks, self.slot_kind = list(arrays), list(pieces), masks, slot_kind
        n, nk = len(arrays), len(masks)
        shapes = [a.shape if p is None else p[1] for a, p in zip(arrays, pieces)]
        self.out_shape = [jax.ShapeDtypeStruct((nslots,) + tuple(s), a.dtype) for s, a in zip(shapes, arrays)]
        self.scratch = ([pltpu.SemaphoreType.DMA((n * nk,)), pltpu.SemaphoreType.DMA((n * nk,)),
                         pltpu.SemaphoreType.DMA((2 * n,))] + [pltpu.VMEM(s, a.dtype) for s, a in zip(shapes, arrays)])
        self.has_mid = False

    def _copies(self, ins, outs, scratch):
        send_sems, recv_sems = scratch[:2]
        me = _me()
        slot = {"chip": _chip_of(me), "dev": 4 * me[0] + 2 * me[1] + me[2], "core": me[2]}[self.slot_kind]
        nk = len(self.masks)

        def piece(a, dev):
            return ins[a] if self.pieces[a] is None else self.pieces[a][0](ins[a], dev)

        local_sems, stages = scratch[2], scratch[3:]
        local = [_StagedCopy(piece(a, me), stages[a], outs[a].at[slot], local_sems.at[2 * a], local_sems.at[2 * a + 1])
                 for a in range(len(ins))]
        remote = []
        for a in range(len(ins)):
            for ki, mask in enumerate(self.masks):
                dev = _peer(mask)
                remote.append(_remote(piece(a, dev), outs[a].at[slot], send_sems.at[a * nk + ki],
                                      recv_sems.at[a * nk + ki], dev))
        return local, remote

    def start(self, ins, outs, scratch):
        local, remote = self._copies(ins, outs, scratch)
        for cp in remote + local:
            cp.start()

    def finish(self, ins, outs, scratch):
        local, remote = self._copies(ins, outs, scratch)
        for cp in remote + local:
            cp.wait()


SPLIT_ROWS = 16


class TwoLevelGather:
    def __init__(self, arrays):
        self.arrays = list(arrays)
        n = len(arrays)
        self.out_shape = [jax.ShapeDtypeStruct((4,) + a.shape, a.dtype) for a in arrays]
        self.scratch = ([pltpu.SemaphoreType.DMA((4 * n,)), pltpu.SemaphoreType.DMA((4 * n,)),
                         pltpu.SemaphoreType.DMA((3 * n,)), pltpu.SemaphoreType.DMA((3 * n,)),
                         pltpu.SemaphoreType.DMA((2 * n,))] + [pltpu.VMEM(a.shape, a.dtype) for a in arrays])
        self.has_mid = True

    def _copies(self, ins, outs, scratch):
        ici_send, ici_recv, fwd_send, fwd_recv, local_sems = scratch[:5]
        stages = scratch[5:]
        me = _me()
        sibling, in_x, in_y, diagonal = _peer(1), _peer(4), _peer(2), _peer(6)
        plan = []
        for a in range(len(ins)):
            half = ins[a].shape[0] // 2
            first = half // 2 if half % (2 * SPLIT_ROWS) == 0 else half
            mine = pl.ds(me[2] * half, half)
            local = _StagedCopy(ins[a], stages[a], outs[a].at[_chip_of(me)], local_sems.at[2 * a], local_sems.at[2 * a + 1])

            def ici(k, src, dst, dev):
                return _remote(src, dst, ici_send.at[4 * a + k], ici_recv.at[4 * a + k], dev)

            def d2d(k, chip):
                zone = outs[a].at[_chip_of(chip), mine]
                return _remote(zone, zone, fwd_send.at[3 * a + k], fwd_recv.at[3 * a + k], sibling)

            own_zone = outs[a].at[_chip_of(me), mine]
            from_x = outs[a].at[_chip_of(in_x), pl.ds(me[2] * half, first)]
            onward = [ici(2, from_x, from_x, in_y), None]
            if first < half:
                from_y = outs[a].at[_chip_of(in_y), pl.ds(me[2] * half + first, half - first)]
                onward[1] = ici(3, from_y, from_y, in_x)
            plan.append(dict(
                local=local,
                own=[ici(0, ins[a].at[mine], own_zone, in_x), ici(1, ins[a].at[mine], own_zone, in_y)],
                onward=onward, sibling=[d2d(0, in_x), d2d(1, in_y), d2d(2, diagonal)]))
        return plan

    def start(self, ins, outs, scratch):
        for p in self._copies(ins, outs, scratch):
            for cp in p["own"]:
                cp.start()
            p["local"].start()

    def mid(self, ins, outs, scratch):
        plan = self._copies(ins, outs, scratch)
        for p in plan:
            for k in range(2):
                p["own"][k].wait_recv()
                if p["onward"][k] is not None:
                    p["onward"][k].start()
                p["sibling"][k].start()
        for p in plan:
            for cp in p["onward"]:
                if cp is not None:
                    cp.wait_recv()
            p["sibling"][2].start()

    def finish(self, ins, outs, scratch):
        for p in self._copies(ins, outs, scratch):
            for cp in p["sibling"]:
                cp.wait_recv()
            for cp in p["own"] + p["sibling"] + [cp for cp in p["onward"] if cp is not None]:
                cp.wait_send()
            p["local"].wait()


class Both:
    def __init__(self, a, b):
        self.a, self.b = a, b
        self.arrays, self.out_shape = a.arrays + b.arrays, a.out_shape + b.out_shape
        self.scratch = a.scratch + b.scratch
        self.has_mid = False
        assert not (a.has_mid or b.has_mid)

    def _parts(self, ins, outs, sems):
        na, sa = len(self.a.arrays), len(self.a.scratch)
        return (ins[:na], outs[:na], sems[:sa]), (ins[na:], outs[na:], sems[sa:])

    def start(self, ins, outs, sems):
        pa, pb = self._parts(ins, outs, sems)
        self.a.start(*pa)
        self.b.start(*pb)

    def finish(self, ins, outs, sems):
        pa, pb = self._parts(ins, outs, sems)
        self.a.finish(*pa)
        self.b.finish(*pb)


class PairOfSums:
    def __init__(self, array):
        s, rows, cols = array.shape
        self.arrays = [array]
        self.out_shape = [jax.ShapeDtypeStruct((2, rows, cols), F32)]
        self.scratch = [pltpu.SemaphoreType.DMA((s,)), pltpu.SemaphoreType.DMA((3,)), pltpu.VMEM(array.shape, array.dtype),
                        pltpu.VMEM((rows, cols), F32)]
        self.has_mid = False

    def _copies(self, ins, outs, scratch):
        load_sems, sems, slots, total = scratch
        mine = outs[0].at[lax.axis_index("c")]
        loads = [pltpu.make_async_copy(ins[0].at[k], slots.at[k], load_sems.at[k]) for k in range(slots.shape[0])]
        return loads, [_remote(total, mine, sems.at[0], sems.at[1], _peer(1)), pltpu.make_async_copy(total, mine, sems.at[2])]

    def start(self, ins, outs, scratch):
        loads, stores = self._copies(ins, outs, scratch)
        slots, total = scratch[2:]
        for cp in loads:
            cp.start()
        for cp in loads:
            cp.wait()
        acc = slots[0].astype(F32)
        for k in range(1, slots.shape[0]):
            acc = acc + slots[k].astype(F32)
        total[...] = acc
        for cp in stores:
            cp.start()

    def finish(self, ins, outs, scratch):
        for cp in self._copies(ins, outs, scratch)[1]:
            cp.wait()


_ANY = pl.BlockSpec(memory_space=pl.ANY)


def run_comm(name, comm):
    n = len(comm.arrays)

    def body(*refs):
        ins, outs, sems = refs[:n], refs[n:2 * n], refs[2 * n:]
        comm.start(ins, outs, sems)
        if comm.has_mid:
            comm.mid(ins, outs, sems)
        comm.finish(ins, outs, sems)

    return pl.pallas_call(body, name=name, out_shape=comm.out_shape, in_specs=[_ANY] * n, out_specs=[_ANY] * n,
                          scratch_shapes=comm.scratch,
                          compiler_params=pltpu.CompilerParams(vmem_limit_bytes=VMEM_LIMIT))(*comm.arrays)


_HBM = pl.BlockSpec(memory_space=pltpu.HBM)
_SEM = pl.BlockSpec(memory_space=pltpu.SEMAPHORE)
_SIDE_EFFECT = pltpu.SideEffectType.DATAFLOW_SIDE_EFFECTING


def _chip_copies(srcs, lands, send_sems, recv_sems, by_target):
    me = _me()
    copies = []
    for a, (src, land) in enumerate(zip(srcs, lands)):
        for ki, mask in enumerate(CHIP_MASKS):
            dev = _peer(mask)
            k = a * len(CHIP_MASKS) + ki
            piece = src.at[_chip_of(dev)] if by_target else src
            copies.append(_remote(piece, land.at[_chip_of(me)], send_sems.at[k], recv_sems.at[k], dev))
    return copies


def chip_exchange_start(name, arrays, by_target):
    n = len(arrays)
    nsem = n * len(CHIP_MASKS)
    piece_shapes = [a.shape[1:] if by_target else a.shape for a in arrays]

    def body(*refs):
        srcs, lands = refs[:n], refs[n:2 * n]
        send_sems, recv_sems = refs[2 * n:2 * n + 2]
        token = refs[4 * n + 2]
        stages, local_sems = refs[4 * n + 3:5 * n + 3], refs[5 * n + 3]
        me = _me()
        for cp in _chip_copies(srcs, lands, send_sems, recv_sems, by_target):
            cp.start()
        own = [_StagedCopy(srcs[a].at[_chip_of(me)] if by_target else srcs[a], stages[a], lands[a].at[_chip_of(me)],
                           local_sems.at[2 * a], local_sems.at[2 * a + 1]) for a in range(n)]
        for cp in own:
            cp.load.start()
        for cp in own:
            cp.load.wait()
            cp.store.start()
        for cp in own:
            cp.store.wait()
        token[...] = jnp.zeros_like(token)

    lands = [lax.empty((4,) + tuple(s), a.dtype) for s, a in zip(piece_shapes, arrays)]
    hbm = lambda a: pltpu.HBM(a.shape, a.dtype)
    res = pl.pallas_call(
        body, name=name,
        out_shape=(pltpu.SemaphoreType.DMA((nsem,)), pltpu.SemaphoreType.DMA((nsem,)), *[hbm(a) for a in arrays],
                   *[hbm(l) for l in lands], jax.ShapeDtypeStruct((8, LANES), F32)),
        in_specs=(_HBM,) * (2 * n), out_specs=(_SEM, _SEM) + (_HBM,) * (2 * n) + (pl.BlockSpec(memory_space=pltpu.VMEM),),
        input_output_aliases={i: i + 2 for i in range(2 * n)},
        scratch_shapes=[pltpu.VMEM(tuple(s), a.dtype) for s, a in zip(piece_shapes, arrays)]
        + [pltpu.SemaphoreType.DMA((2 * n,))],
        compiler_params=pltpu.CompilerParams(has_side_effects=_SIDE_EFFECT, vmem_limit_bytes=VMEM_LIMIT),
    )(*[pltpu.with_memory_space_constraint(a, pltpu.HBM) for a in arrays + lands])
    return res[:-1], res[-1]


def pair_sum_exchange_start(name, partial, from_sibling):
    shards, _, rows, cols = partial.shape
    nsem = len(CHIP_MASKS)

    def body(p_ref, r_ref, _, send_sems, recv_sems, sum_ref, land_ref, token, p_buf, r_buf, o_buf, sems):
        me = _me()
        targets = [_chip_of(_peer(mask)) for mask in CHIP_MASKS] + [_chip_of(me)]
        remote = _chip_copies([sum_ref], [land_ref], send_sems, recv_sems, True)

        def loads(k):
            return [pltpu.make_async_copy(p_ref.at[targets[k], me[2]], p_buf.at[k % 2], sems.at[2 * (k % 2)]),
                    pltpu.make_async_copy(r_ref.at[targets[k]], r_buf.at[k % 2], sems.at[2 * (k % 2) + 1])]

        for cp in loads(0):
            cp.start()
        for k in range(shards):
            if k + 1 < shards:
                for cp in loads(k + 1):
                    cp.start()
            for cp in loads(k):
                cp.wait()
            o_buf[k % 2] = (p_buf[k % 2].astype(F32) + r_buf[k % 2].astype(F32)).astype(BF16)
            summed = sum_ref.at[targets[k]] if k < nsem else land_ref.at[targets[k]]
            store = pltpu.make_async_copy(o_buf.at[k % 2], summed, sems.at[4])
            store.start()
            store.wait()
            if k < nsem:
                remote[k].start()
        token[...] = jnp.zeros_like(token)

    piece = jax.ShapeDtypeStruct((shards, rows, cols), BF16)
    land = lax.empty(piece.shape, BF16)
    buf = pltpu.VMEM((2, rows, cols), BF16)
    res = pl.pallas_call(
        body, name=name,
        out_shape=(pltpu.SemaphoreType.DMA((nsem,)), pltpu.SemaphoreType.DMA((nsem,)), pltpu.HBM(piece.shape, BF16),
                   pltpu.HBM(piece.shape, BF16), jax.ShapeDtypeStruct((8, LANES), F32)),
        in_specs=(_HBM,) * 3, out_specs=(_SEM, _SEM, _HBM, _HBM, pl.BlockSpec(memory_space=pltpu.VMEM)),
        input_output_aliases={2: 3},
        scratch_shapes=[buf, buf, buf, pltpu.SemaphoreType.DMA((5,))],
        compiler_params=pltpu.CompilerParams(has_side_effects=_SIDE_EFFECT, vmem_limit_bytes=VMEM_LIMIT),
    )(*[pltpu.with_memory_space_constraint(a, pltpu.HBM) for a in (partial, from_sibling, land)])
    return res[:-1], res[-1]


def chip_exchange_wait(name, in_flight, by_target, after):
    send_sems, recv_sems, *thru = in_flight
    n = len(thru) // 2

    def body(*refs):
        srcs, lands, (send, recv) = refs[:n], refs[n:2 * n], refs[2 * n:2 * n + 2]
        for cp in _chip_copies(srcs, lands, send, recv, by_target):
            cp.wait_send()
            cp.wait_recv()

    return pl.pallas_call(
        body, name=name, out_shape=tuple(pltpu.HBM(t.shape, t.dtype) for t in thru),
        in_specs=(_HBM,) * (2 * n) + (_SEM, _SEM, pl.BlockSpec(memory_space=pl.ANY)), out_specs=(_HBM,) * (2 * n),
        input_output_aliases={i: i for i in range(2 * n)},
        compiler_params=pltpu.CompilerParams(has_side_effects=_SIDE_EFFECT),
    )(*thru, send_sems, recv_sems, after)[n:]


def _call_with_comm(body, comm, steps, args, *, name, out_shape, in_specs, out_specs, grid, scratch=(),
                    after_finish=None):
    ni, no, ns, nc = len(in_specs), len(out_specs), len(scratch), len(comm.arrays)

    def full_body(*refs):
        ins, cins = refs[:ni], refs[ni:ni + nc]
        outs, couts = refs[ni + nc:ni + nc + no], refs[ni + nc + no:ni + 2 * nc + no]
        scr, csems = refs[ni + 2 * nc + no:ni + 2 * nc + no + ns], refs[ni + 2 * nc + no + ns:]
        first, middle, last = steps()
        pl.when(first)(lambda: comm.start(cins, couts, csems))
        if comm.has_mid:
            pl.when(middle)(lambda: comm.mid(cins, couts, csems))
        body(*ins, *outs, *scr)

        @pl.when(last)
        def _():
            comm.finish(cins, couts, csems)
            if after_finish is not None:
                after_finish(ins, couts, outs, scr)

    res = pl.pallas_call(
        full_body, name=name, out_shape=list(out_shape) + comm.out_shape, grid=grid,
        in_specs=list(in_specs) + [_ANY] * nc, out_specs=list(out_specs) + [_ANY] * nc,
        scratch_shapes=list(scratch) + comm.scratch,
        compiler_params=pltpu.CompilerParams(dimension_semantics=("arbitrary",) * len(grid),
                                             vmem_limit_bytes=VMEM_LIMIT))(*args, *comm.arrays)
    return res[:no], res[no:]


def _shard_pieces(chip):
    if chip < 3:
        return [(0, SHARD_W, 8 * chip)]
    behind_dt = DT_STORED_START + SSM_HEADS - 3 * SHARD_W
    return [(0, behind_dt, 24), (behind_dt, SHARD_W - behind_dt, behind_dt + 24 + DT_PAD)]


W_IN_COLS = 256


def pack_w_in(wt):
    def body(w_ref, o_ref, pad_ref):
        chip = _chip_index()
        pad_ref[...] = jnp.zeros_like(pad_ref)
        for cv in range(4):
            @pl.when(chip == cv)
            def _():
                for src, n, dst in _shard_pieces(cv):
                    pad_ref[dst:dst + n, :] = w_ref[src:src + n, :]
        o_ref[...] = pad_ref[...].astype(BF16)

    return _call(body, name="pack_w_in", grid=(D_MODEL // W_IN_COLS,),
                 in_specs=[pl.BlockSpec((SHARD_W, W_IN_COLS), lambda i: (0, i))],
                 out_specs=pl.BlockSpec((PACK_W, W_IN_COLS), lambda i: (0, i)),
                 out_shape=jax.ShapeDtypeStruct((PACK_W, D_MODEL), BF16),
                 scratch=[pltpu.VMEM((PACK_W, W_IN_COLS), F32)], sem=("parallel",))(wt)


def _tile_runs():
    runs, fix = [], []
    for t in range(N_ALIGNED // LANES):
        s = min(t // 19, 3)
        j = t - 19 * s
        p = _act_col(t * LANES)
        if runs and runs[-1][1] == s and runs[-1][0] + runs[-1][3] == p and runs[-1][2] + runs[-1][3] == j * LANES:
            runs[-1][3] += LANES
        else:
            runs.append([p, s, j * LANES, LANES])
        if j == 0 and s > 0:
            fix.append((p, s - 1))
    return runs, fix


def unpack_w_in(bg):
    runs, fix = _tile_runs()

    def body(b_ref, o_ref):
        for p, s, j, w in runs:
            o_ref[p:p + w, :] = b_ref[s, j:j + w, :]
        for p, s in fix:
            o_ref[p:p + LANES, :] = o_ref[p:p + LANES, :] + b_ref[s, SHARD_STRIDE:PACK_W, :]
        o_ref[N_ALIGNED:N_ACT, :] = jnp.zeros((N_ACT - N_ALIGNED, W_IN_COLS), BF16)

    return _call(body, name="unpack_w_in", grid=(D_MODEL // W_IN_COLS,),
                 in_specs=[pl.BlockSpec((4, PACK_W, W_IN_COLS), lambda i: (0, 0, i))],
                 out_specs=pl.BlockSpec((N_ACT, W_IN_COLS), lambda i: (0, i)),
                 out_shape=jax.ShapeDtypeStruct((N_ACT, D_MODEL), BF16), sem=("parallel",))(bg)


def _adamw(w, g, m, v):
    m = ADAM_B1 * m + (1.0 - ADAM_B1) * g
    v = ADAM_B2 * v + (1.0 - ADAM_B2) * jnp.square(g)
    m_hat = m / (1.0 - ADAM_B1 ** ADAM_STEP)
    v_hat = v / (1.0 - ADAM_B2 ** ADAM_STEP)
    delta = -ADAM_LR * (m_hat / (jnp.sqrt(v_hat) + ADAM_EPS) + ADAM_WD * w)
    return delta, m, v


def adamw_w_in(g_packed, wt, mt, vt):
    cols = LANES

    def body(g_ref, w_ref, m_ref, v_ref, go_ref, d_ref, mo_ref, vo_ref):
        chip = _chip_index()
        for cv in range(4):
            @pl.when(chip == cv)
            def _():
                for dst, n, src in _shard_pieces(cv):
                    go_ref[dst:dst + n, :] = g_ref[src:src + n, :]
        d_ref[...], mo_ref[...], vo_ref[...] = _adamw(w_ref[...], go_ref[...], m_ref[...], v_ref[...])

    spec = pl.BlockSpec((SHARD_W, cols), lambda i: (0, i))
    shp = jax.ShapeDtypeStruct((SHARD_W, D_MODEL), F32)
    return _call(body, name="adamw_w_in", grid=(D_MODEL // cols,),
                 in_specs=[pl.BlockSpec((PACK_W, cols), lambda i: (0, i)), spec, spec, spec],
                 out_specs=[spec] * 4, out_shape=[shp] * 4, sem=("parallel",))(g_packed, wt, mt, vt)


def adamw_rows(name, g, w, m, v):
    r, c = g.shape
    rows = min(r, BLOCK)

    def body(g_ref, w_ref, m_ref, v_ref, d_ref, mo_ref, vo_ref):
        d_ref[...], mo_ref[...], vo_ref[...] = _adamw(w_ref[...], g_ref[...], m_ref[...], v_ref[...])

    spec = pl.BlockSpec((rows, c), lambda i: (i, 0))
    shp = jax.ShapeDtypeStruct((r, c), F32)
    return _call(body, name=name, grid=(r // rows,), in_specs=[spec] * 4, out_specs=[spec] * 3, out_shape=[shp] * 3,
                 sem=("parallel",))(g, w, m, v)


def adamw_small(gs, ws, ms, vs):
    n = len(gs)

    def body(*refs):
        g, w, m, v = refs[:n], refs[n:2 * n], refs[2 * n:3 * n], refs[3 * n:4 * n]
        outs = refs[4 * n:]
        for i in range(n):
            d, mn, vn = _adamw(w[i][...], g[i][...], m[i][...], v[i][...])
            outs[3 * i][...] = d
            outs[3 * i + 1][...] = mn
            outs[3 * i + 2][...] = vn

    specs = [_full(a.shape) for a in gs]
    res = _call(body, name="adamw_small", in_specs=specs * 4,
                out_specs=[s for s in specs for _ in range(3)],
                out_shape=[jax.ShapeDtypeStruct(a.shape, F32) for a in gs for _ in range(3)])(*gs, *ws, *ms, *vs)
    return [tuple(res[3 * i:3 * i + 3]) for i in range(n)]


def sum_slots(name, r, after):
    s, rr, c = r.shape
    rows = min(rr, BLOCK)

    def body(r_ref, after_ref, o_ref):
        acc = r_ref[0].astype(F32)
        for k in range(1, s):
            acc = acc + r_ref[k].astype(F32)
        o_ref[...] = acc

    return _call(body, name=name, grid=(rr // rows,),
                 in_specs=[pl.BlockSpec((s, rows, c), lambda i: (0, i, 0)), _full(after.shape)],
                 out_specs=pl.BlockSpec((rows, c), lambda i: (i, 0)), out_shape=jax.ShapeDtypeStruct((rr, c), F32),
                 sem=("parallel",))(r, after)


def _col_tile(n, k):
    if n % 896 == 0 and k <= 1024:
        return 896
    return min(n, 512)


def project(name, x, wt, after):
    m, k = x.shape
    n = wt.shape[0]
    tn = D_MODEL

    steps = n // tn
    slots = 3

    def body(x_ref, w_ref, after_ref, o_ref, ring, sems):
        j = pl.program_id(0)

        def fetch(step):
            rows = pl.ds(pl.multiple_of(step * tn, tn), tn)
            return pltpu.make_async_copy(w_ref.at[rows], ring.at[step % slots], sems.at[step % slots])

        @pl.when(j == 0)
        def _():
            for step in range(slots - 1):
                fetch(step).start()

        @pl.when(j + slots - 1 < steps)
        def _():
            fetch(j + slots - 1).start()

        fetch(j).wait()
        o_ref[...] = lax.dot_general(x_ref[...], ring[j % slots], _NT, preferred_element_type=F32)

    assert steps >= slots
    return _call(body, name=name, grid=(steps,),
                 in_specs=[_full((m, k)), _ANY, _full(after.shape)],
                 out_specs=pl.BlockSpec((m, tn), lambda j: (0, j)), out_shape=jax.ShapeDtypeStruct((m, n), F32),
                 scratch=[pltpu.VMEM((slots, tn, k), wt.dtype), pltpu.SemaphoreType.DMA((slots,))],
                 sem=("arbitrary",))(x, wt, after)


def _piece_tiles(pieces, width):
    spans, start = [], 0
    for p in pieces:
        spans.append((start, p.shape[1] // width))
        start += p.shape[1] // width
    return spans, start


def _piece_spec(block, span, rows_of, tile_of):
    first, count = span

    def index(*pos):
        t = tile_of(*pos) - first
        mine = (t >= 0) & (t < count)
        return jnp.where(mine, rows_of(*pos), 0), jnp.clip(t, 0, count - 1)

    return pl.BlockSpec(block, index)


def project_back(name, pieces, wt, after):
    m = pieces[0].shape[0]
    k = wt.shape[1]
    tm = m // 2
    spans, steps = _piece_tiles(pieces, D_MODEL)

    def body(*refs):
        w_ref, o_ref = refs[len(pieces)], refs[len(pieces) + 2]
        j = pl.program_id(1)

        @pl.when(j == 0)
        def _():
            o_ref[...] = jnp.zeros_like(o_ref)

        for dy_ref, (first, count) in zip(refs, spans):
            @pl.when((j >= first) & (j < first + count))
            def _():
                o_ref[...] += jnp.dot(dy_ref[...], w_ref[...], preferred_element_type=F32)

    return _call(body, name=name, grid=(m // tm, steps),
                 in_specs=[_piece_spec((tm, D_MODEL), s, lambda i, j: i, lambda i, j: j) for s in spans]
                 + [pl.BlockSpec((D_MODEL, k), lambda i, j: (j, 0)), _full(after.shape)],
                 out_specs=pl.BlockSpec((tm, k), lambda i, j: (i, 0)), out_shape=jax.ShapeDtypeStruct((m, k), F32),
                 sem=("parallel", "arbitrary"))(*pieces, wt, after)


GRAD_TILE = 512


def _slab_runs():
    runs = [[] for _ in range(N_ACT // GRAD_TILE)]
    for s in range(4):
        for j in range(PACK_W // LANES):
            tile, r = divmod(_act_col((19 * s + j) * LANES), GRAD_TILE)
            last = runs[tile][-1] if runs[tile] else None
            if last and last[2] == s and last[0] + last[1] == r and last[3] + last[1] == j * LANES:
                last[1] += LANES
            else:
                runs[tile].append([r, LANES, s, j * LANES])
    return runs


def weight_grad_t(name, pieces, order, x):
    m, k = x.shape
    spans, steps = _piece_tiles(pieces, GRAD_TILE)
    runs = _slab_runs()
    most = max(len(r) for r in runs)
    half_rows = PACK_W // 2
    when, tile_at = [None] * len(pieces), []
    for p in order:
        when[p] = (len(tile_at), spans[p][1])
        tile_at += range(spans[p][0], spans[p][0] + spans[p][1])

    landed = {}
    for step in range(steps):
        for _, _, s, _ in runs[tile_at[step]]:
            landed[s] = step + 2

    def body(*refs):
        x_ref, o_ref, land_ref, tile_ref, sems, send_sems, recv_sems = refs[len(pieces):]
        i = pl.program_id(0)
        other = 1 - lax.axis_index("c")

        def copies(step):
            return [pltpu.make_async_copy(tile_ref.at[step % 2, r:r + n], o_ref.at[s, d:d + n],
                                          sems.at[(step % 2) * most + c])
                    for c, (r, n, s, d) in enumerate(runs[tile_at[step]])]

        def to_sibling(s):
            return _remote(o_ref.at[s, pl.ds(other * half_rows, half_rows)], land_ref.at[s], send_sems.at[s],
                           recv_sems.at[s], _peer(1))

        for step in range(2, steps):
            @pl.when(i == step)
            def _():
                for cp in copies(step - 2):
                    cp.wait()

        for dy_ref, (first, count) in zip(refs, when):
            @pl.when((i >= first) & (i < first + count))
            def _():
                tile_ref[i % 2] = lax.dot_general(dy_ref[...], x_ref[...], (((0,), (0,)), ((), ())),
                                                  preferred_element_type=F32).astype(BF16)

        for step in range(steps):
            @pl.when(i == step)
            def _():
                for cp in copies(step):
                    cp.start()
                if step < steps - 1:
                    for s in range(4):
                        if landed[s] == step:
                            to_sibling(s).start()
                else:
                    for cp in copies(step - 1) + copies(step):
                        cp.wait()
                    for s in range(4):
                        if landed[s] >= steps - 1:
                            to_sibling(s).start()
                    for s in range(4):
                        to_sibling(s).wait()

    return _call(body, name=name, grid=(steps,),
                 in_specs=[_piece_spec((m, GRAD_TILE), s, lambda i: 0, lambda i: i) for s in when]
                 + [pl.BlockSpec(memory_space=pltpu.VMEM)],
                 out_specs=[_ANY, _ANY],
                 out_shape=[jax.ShapeDtypeStruct((4, PACK_W, k), BF16), jax.ShapeDtypeStruct((4, half_rows, k), BF16)],
                 scratch=[pltpu.VMEM((2, GRAD_TILE, k), BF16), pltpu.SemaphoreType.DMA((2 * most,)),
                          pltpu.SemaphoreType.DMA((4,)), pltpu.SemaphoreType.DMA((4,))],
                 sem=("arbitrary",))(*pieces, x)


def mm_tn(name, x, dy):
    m, k = x.shape
    n = dy.shape[1]
    tn = _col_tile(n, k)

    def body(x_ref, dy_ref, o_ref):
        o_ref[...] = lax.dot_general(x_ref[...], dy_ref[...], (((0,), (0,)), ((), ())),
                                     preferred_element_type=F32).astype(BF16)

    return _call(body, name=name, grid=(n // tn,),
                 in_specs=[_full((m, k)), pl.BlockSpec((m, tn), lambda i: (0, i))],
                 out_specs=pl.BlockSpec((k, tn), lambda i: (0, i)), out_shape=jax.ShapeDtypeStruct((k, n), BF16),
                 sem=("parallel",))(x, dy)


def _row_spec(width, col_block=0):
    return pl.BlockSpec((BLOCK, width), lambda i: (i, col_block))


def _x_spec():
    return pl.BlockSpec((None, BLOCK, D_MODEL), lambda i: (0, jnp.maximum(i - 1, 0), 0))


def prep(name, x, g_pre, gather):
    nb = x.shape[1] // BLOCK + 1
    shards, _, shard_cols = gather.out_shape[-1].shape

    def body(x_ref, g_ref, h_ref, u_ref, meta_ref, sems):
        @pl.when(pl.program_id(0) < nb - 1)
        def _():
            h_ref[...] = x_ref[...]
            u_ref[...] = _rms(x_ref[...], g_ref[...]).astype(BF16)

    def first_block(ins, gathered, outs, scratch):
        g_ref, (h_ref, u_ref), (meta_ref, sems) = ins[1], outs, scratch
        copies = [pltpu.make_async_copy(gathered[-1].at[t], meta_ref.at[t], sems.at[t]) for t in range(shards)]
        for cp in copies:
            cp.start()
        for cp in copies:
            cp.wait()
        h_ref[0:PAD_ROWS, :] = jnp.zeros((PAD_ROWS, D_MODEL), F32)
        h_ref[PAD_ROWS:BLOCK, :] = jnp.concatenate([meta_ref[t] for t in range(shards)], axis=1)
        u_ref[...] = _rms(h_ref[...], g_ref[...]).astype(BF16)

    def at():
        i = pl.program_id(0)
        return i == 0, i == nb // 2, i == nb - 1

    rows = pl.BlockSpec((BLOCK, D_MODEL), lambda i: ((i + 1) % nb, 0))
    return _call_with_comm(
        body, gather, at, (x, g_pre), name=name, grid=(nb,),
        in_specs=[pl.BlockSpec((None, BLOCK, D_MODEL), lambda i: (0, jnp.minimum(i, nb - 2), 0)), _full((1, D_MODEL))],
        out_specs=[rows, rows],
        out_shape=[jax.ShapeDtypeStruct((nb * BLOCK, D_MODEL), F32), jax.ShapeDtypeStruct((nb * BLOCK, D_MODEL), BF16)],
        scratch=[pltpu.VMEM((shards, N_META, shard_cols), F32), pltpu.SemaphoreType.DMA((shards,))],
        after_finish=first_block)


def prep_bwd(h, du, dres, g_pre):
    seq = h.shape[0] - BLOCK
    rows = min(seq, 4 * BLOCK)

    def body(h0_ref, du0_ref, h_ref, du_ref, dres_ref, g_ref, gx_ref, gm_ref, gg_ref):
        i = pl.program_id(0)
        _, vjp = jax.vjp(_rms, h_ref[...], g_ref[...])
        dh, dg = vjp(du_ref[...])
        gx_ref[...] = dh + dres_ref[...]

        @pl.when(i == 0)
        def _():
            _, vjp0 = jax.vjp(_rms, h0_ref[...], g_ref[...])
            dh0, dg0 = vjp0(du0_ref[...])
            gm_ref[...] = dh0[PAD_ROWS:BLOCK, :]
            gg_ref[...] = dg0 + dg

        @pl.when(i > 0)
        def _():
            gg_ref[...] += dg

    seq_rows = pl.BlockSpec((pl.Element(rows), pl.Element(D_MODEL)), lambda i: (pl.multiple_of(BLOCK + rows * i, BLOCK), 0))
    first = pl.BlockSpec((BLOCK, D_MODEL), lambda i: (0, 0))
    return _call(body, name="prep_bwd", grid=(seq // rows,),
                 in_specs=[first, first, seq_rows, seq_rows, seq_rows, _full((1, D_MODEL))],
                 out_specs=[pl.BlockSpec((None, rows, D_MODEL), lambda i: (0, i, 0)), _full((N_META, D_MODEL)),
                            _full((1, D_MODEL))],
                 out_shape=[jax.ShapeDtypeStruct((1, seq, D_MODEL), F32),
                            jax.ShapeDtypeStruct((N_META, D_MODEL), F32), jax.ShapeDtypeStruct((1, D_MODEL), F32)],
                 sem=("arbitrary",))(h, du, h, du, dres, g_pre)


GROUP_W = SSM_INNER // SSM_GROUPS


def _gated_norm(y, z, g):
    t = y * _silu(z)
    return t * lax.rsqrt(jnp.mean(t * t, axis=-1, keepdims=True) + NORM_EPS) * g


def _gated_norm_groups(y, z, g):
    groups = [slice(k * GROUP_W, (k + 1) * GROUP_W) for k in range(SSM_GROUPS)]
    return jnp.concatenate([_gated_norm(y[:, s], z[:, s], g[:, s]) for s in groups], axis=1)


def _merge(ga, gs, ya, ys):
    return _sigmoid(ga) * ya + _sigmoid(gs) * ys


GATE_ATT_BLOCK = SEG["gate_att"][2] // D_MODEL
GATE_SSM_BLOCK = SEG["gate_ssm"][2] // D_MODEL


def _row_loss(out, g_post, x, target):
    diff = x + _rms(out, g_post) - target
    return 0.5 * jnp.sum(diff * diff) / D_MODEL


def tail(y_ssd, proj, a_att, x, target, woa, wos, wo, g_norm, g_post):
    nb = y_ssd.shape[0] // BLOCK
    rows = nb * BLOCK

    def body(y_ref, z_ref, ga_ref, gs_ref, a_ref, x_ref, t_ref, woa_ref, wos_ref, wo_ref, gn_ref, gp_ref,
             yn_ref, mg_ref, dout_ref, dya_ref, dys_ref, da_ref, dy_ref, dz_ref, dga_ref, dgs_ref, dres_ref,
             loss_ref, dgp_ref, dgn_ref):
        i = pl.program_id(0)
        yn, norm_vjp = jax.vjp(_gated_norm_groups, y_ref[...], z_ref[...], gn_ref[...])
        yn16 = yn.astype(BF16)
        y_ssm = jnp.dot(yn16, wos_ref[...], preferred_element_type=F32)
        y_att = jnp.dot(a_ref[...], woa_ref[...], preferred_element_type=F32)
        merged, merge_vjp = jax.vjp(_merge, ga_ref[...], gs_ref[...], y_att, y_ssm)
        merged16 = merged.astype(BF16)
        out = jnp.dot(merged16, wo_ref[...], preferred_element_type=F32)
        loss, loss_vjp = jax.vjp(_row_loss, out, gp_ref[...], x_ref[...], t_ref[...])
        counted = jnp.where(i > 0, 1.0, 0.0)
        dout, dgp, dres, _ = loss_vjp(counted)
        dout16 = dout.astype(BF16)
        dmerged = lax.dot_general(dout16, wo_ref[...], _NT, preferred_element_type=F32)
        dga, dgs, dya, dys = merge_vjp(dmerged)
        dya16, dys16 = dya.astype(BF16), dys.astype(BF16)
        dyn = lax.dot_general(dys16, wos_ref[...], _NT, preferred_element_type=F32)
        dy, dz, dgn = norm_vjp(dyn)

        yn_ref[...] = yn16
        mg_ref[...] = merged16
        dout_ref[...] = dout16
        dya_ref[...] = dya16
        dys_ref[...] = dys16
        da_ref[...] = lax.dot_general(dya16, woa_ref[...], _NT, preferred_element_type=F32)
        dy_ref[...] = dy
        dz_ref[...] = dz.astype(BF16)
        dga_ref[...] = dga.astype(BF16)
        dgs_ref[...] = dgs.astype(BF16)
        dres_ref[...] = dres

        @pl.when(i == 0)
        def _():
            loss_ref[...] = jnp.zeros_like(loss_ref)
            dgp_ref[...] = jnp.zeros_like(dgp_ref)
            dgn_ref[...] = jnp.zeros_like(dgn_ref)

        loss_ref[...] += loss * counted
        dgp_ref[...] += dgp
        dgn_ref[...] += dgn

    wide, narrow = _row_spec(SSM_INNER), _row_spec(D_MODEL)
    resident = pl.BlockSpec(memory_space=pltpu.VMEM)
    bf = lambda w: jax.ShapeDtypeStruct((rows, w), BF16)
    f32 = lambda w: jax.ShapeDtypeStruct((rows, w), F32)
    return _call(body, name="tail", grid=(nb,),
                 in_specs=[wide, wide, _row_spec(D_MODEL, GATE_ATT_BLOCK), _row_spec(D_MODEL, GATE_SSM_BLOCK), narrow,
                           _x_spec(), _x_spec(), resident, resident, resident, _full((1, SSM_INNER)),
                           _full((1, D_MODEL))],
                 out_specs=[wide, narrow, narrow, narrow, narrow, narrow, wide, wide, narrow, narrow, narrow,
                            _full((8, LANES)), _full((1, D_MODEL)), _full((1, SSM_INNER))],
                 out_shape=[bf(SSM_INNER), bf(D_MODEL), bf(D_MODEL), bf(D_MODEL), bf(D_MODEL), f32(D_MODEL),
                            f32(SSM_INNER), bf(SSM_INNER), bf(D_MODEL), bf(D_MODEL), f32(D_MODEL),
                            jax.ShapeDtypeStruct((8, LANES), F32), jax.ShapeDtypeStruct((1, D_MODEL), F32),
                            jax.ShapeDtypeStruct((1, SSM_INNER), F32)],
                 sem=("arbitrary",))(y_ssd, proj, proj, proj, a_att, x, target, woa, wos, wo, g_norm, g_post)


_NT = (((1,), (1,)), ((), ()))
ALIBI_SLOPES = tuple(2.0 ** (-8.0 * (h + 1) / ATT_Q_HEADS) for h in range(ATT_Q_HEADS))
KV_WIDTH = ATT_KV_HEADS * HEAD_DIM
Q_BLOCK = SEG["q"][2] // D_MODEL
Z_ATT_BLOCK = SEG["z_att"][2] // D_MODEL
K_BLOCK = SEG["k"][2] // KV_WIDTH
V_BLOCK = SEG["v"][2] // KV_WIDTH
META_ROW_BLOCK = PAD_ROWS // N_META


@jax.custom_vjp
def _swap_halves(x):
    return pltpu.roll(x, HEAD_DIM, 1)


_swap_halves.defvjp(lambda x: (pltpu.roll(x, HEAD_DIM, 1), None), lambda _, g: (pltpu.roll(g, HEAD_DIM, 1),))


def _both_halves(t, half):
    first = lax.broadcasted_iota(jnp.int32, t.shape, 1) < HEAD_DIM
    sw = _swap_halves(t)
    return jnp.where(first, t, sw) if half == 0 else jnp.where(first, sw, t)


def _attn_rows(q, z, kp, kc, vp, vc, km, vm, sinks, n):
    rows = ATT_GROUP * BLOCK
    i = lax.broadcasted_iota(jnp.int32, (rows, BLOCK), 0) & (BLOCK - 1)
    j = lax.broadcasted_iota(jnp.int32, (rows, BLOCK), 1)
    rel_c = (i - j).astype(F32)
    rel_p = rel_c + float(BLOCK)
    nv = jnp.zeros((rows, BLOCK), jnp.int32) + n
    ok_c = (i >= j) & (nv >= 1)
    ok_p = (j > i) & (nv >= 2)
    im = lax.broadcasted_iota(jnp.int32, (rows, N_META), 0) & (BLOCK - 1)
    jm = lax.broadcasted_iota(jnp.int32, (rows, N_META), 1)
    ok_m = ((jnp.zeros((rows, N_META), jnp.int32) + n) >= 1) | (im >= PAD_ROWS + jm)
    first = lax.broadcasted_iota(jnp.int32, (BLOCK, LANES), 1) < HEAD_DIM
    neg = -jnp.inf
    outs = []
    for kv in range(ATT_KV_HEADS):
        tile, half = divmod(kv, 2)
        lanes = slice(tile * LANES, (tile + 1) * LANES)
        kc2, kp2, km2 = (_both_halves(t[:, lanes], half).astype(BF16) for t in (kc, kp, km))
        vc2, vp2, vm2 = (_both_halves(t[:, lanes], half).astype(BF16) for t in (vc, vp, vm))
        qs, slope, sk = [], [], []
        for pair in range(ATT_GROUP // 2):
            c0 = (kv * ATT_GROUP + 2 * pair) * HEAD_DIM
            qp = q[:, c0:c0 + LANES] * HEAD_DIM ** -0.5
            qs += [jnp.where(first, qp, 0.0), jnp.where(first, 0.0, qp)]
        for g in range(ATT_GROUP):
            slope.append(jnp.full((BLOCK, 1), ALIBI_SLOPES[kv * ATT_GROUP + g], F32))
            sk.append(jnp.broadcast_to(sinks[kv * ATT_GROUP + g], (BLOCK, 1)))
        qs = jnp.concatenate(qs, axis=0).astype(BF16)
        slope = jnp.concatenate(slope, axis=0)
        sk = jnp.concatenate(sk, axis=0)
        sc = jnp.where(ok_c, lax.dot_general(qs, kc2, _NT, preferred_element_type=F32) - slope * rel_c, neg)
        sp = jnp.where(ok_p, lax.dot_general(qs, kp2, _NT, preferred_element_type=F32) - slope * rel_p, neg)
        sm = jnp.where(ok_m, lax.dot_general(qs, km2, _NT, preferred_element_type=F32), neg)
        mx = jnp.maximum(jnp.maximum(jnp.max(sc, axis=1, keepdims=True), jnp.max(sp, axis=1, keepdims=True)),
                         jnp.maximum(jnp.max(sm, axis=1, keepdims=True), sk))
        mx = lax.stop_gradient(mx)
        ec, ep, em, es = jnp.exp(sc - mx), jnp.exp(sp - mx), jnp.exp(sm - mx), jnp.exp(sk - mx)
        den = (es + jnp.sum(ec, axis=1, keepdims=True) + jnp.sum(ep, axis=1, keepdims=True)
               + jnp.sum(em, axis=1, keepdims=True))
        inv = 1.0 / den
        o = (jnp.dot((ec * inv).astype(BF16), vc2, preferred_element_type=F32)
             + jnp.dot((ep * inv).astype(BF16), vp2, preferred_element_type=F32)
             + jnp.dot((em * inv).astype(BF16), vm2, preferred_element_type=F32))
        for pair in range(ATT_GROUP // 2):
            r0 = 2 * pair * BLOCK
            outs.append(jnp.where(first, o[r0:r0 + BLOCK], o[r0 + BLOCK:r0 + 2 * BLOCK]))
    return jnp.concatenate(outs, axis=1) * _silu(z)


def _attn_specs(nb, steps_clamped):
    def blk(t):
        return jnp.minimum(t, nb - 1) if steps_clamped else t

    wide = lambda col: pl.BlockSpec((BLOCK, D_MODEL), lambda t: (blk(t), col))
    cur = lambda col: pl.BlockSpec((BLOCK, KV_WIDTH), lambda t: (blk(t), col))
    prev = lambda col: pl.BlockSpec((BLOCK, KV_WIDTH), lambda t: (jnp.maximum(blk(t) - 1, 0), col))
    meta = lambda col: pl.BlockSpec((N_META, KV_WIDTH), lambda t: (META_ROW_BLOCK, col))
    sinks = pl.BlockSpec((ATT_Q_HEADS, 1, 1), lambda t: (0, 0, 0))
    return [wide(Q_BLOCK), wide(Z_ATT_BLOCK), prev(K_BLOCK), cur(K_BLOCK), prev(V_BLOCK), cur(V_BLOCK),
            meta(K_BLOCK), meta(V_BLOCK), sinks]


def attn_fwd(proj, sinks):
    nb = proj.shape[0] // BLOCK

    def body(q_ref, z_ref, kp_ref, kc_ref, vp_ref, vc_ref, km_ref, vm_ref, sk_ref, o_ref):
        o_ref[...] = _attn_rows(q_ref[...], z_ref[...], kp_ref[...], kc_ref[...], vp_ref[...], vc_ref[...],
                                km_ref[...], vm_ref[...], tuple(sk_ref[h] for h in range(ATT_Q_HEADS)),
                                pl.program_id(0)).astype(BF16)

    return _call(body, name="attn_fwd", grid=(nb,), in_specs=_attn_specs(nb, False), out_specs=_row_spec(D_MODEL),
                 out_shape=jax.ShapeDtypeStruct((nb * BLOCK, D_MODEL), BF16), sem=("parallel",))(*([proj] * 8), sinks)


def attn_bwd(da, proj, sinks):
    nb = proj.shape[0] // BLOCK
    last = nb - 1
    wide = pl.BlockSpec((BLOCK, D_MODEL), lambda t: (jnp.minimum(t, last), 0))
    done = pl.BlockSpec((BLOCK, KV_WIDTH), lambda t: (jnp.maximum(t - 1, 0), 0))
    meta = _full((N_META, KV_WIDTH))
    par = _full((ATT_Q_HEADS, 1, 1))

    def body(da_ref, q_ref, z_ref, kp_ref, kc_ref, vp_ref, vc_ref, km_ref, vm_ref, sk_ref,
             dq_ref, dz_ref, dk_ref, dv_ref, dkm_ref, dvm_ref, dsk_ref, ck_ref, cv_ref):
        t = pl.program_id(0)

        @pl.when(t == 0)
        def _():
            ck_ref[...] = jnp.zeros_like(ck_ref)
            cv_ref[...] = jnp.zeros_like(cv_ref)
            dkm_ref[...] = jnp.zeros_like(dkm_ref)
            dvm_ref[...] = jnp.zeros_like(dvm_ref)
            dsk_ref[...] = jnp.zeros_like(dsk_ref)

        @pl.when(t < nb)
        def _():
            def f(q, z, kp, kc, vp, vc, km, vm, sk):
                return _attn_rows(q, z, kp, kc, vp, vc, km, vm, sk, t)

            _, vjp = jax.vjp(f, q_ref[...], z_ref[...], kp_ref[...], kc_ref[...], vp_ref[...], vc_ref[...],
                             km_ref[...], vm_ref[...], tuple(sk_ref[h] for h in range(ATT_Q_HEADS)))
            dq, dz, dkp, dkc, dvp, dvc, dkm, dvm, dsk = vjp(da_ref[...])
            dq_ref[...] = dq.astype(BF16)
            dz_ref[...] = dz.astype(BF16)
            for h in range(ATT_Q_HEADS):
                dsk_ref[h] += dsk[h]
            dk_ref[...] = ck_ref[...] + dkp
            dv_ref[...] = cv_ref[...] + dvp
            ck_ref[...] = dkc
            cv_ref[...] = dvc
            dkm_ref[...] += dkm
            dvm_ref[...] += dvm

        @pl.when(t == nb)
        def _():
            dk_ref[...] = ck_ref[...]
            dv_ref[...] = cv_ref[...]

    rows = nb * BLOCK
    return _call(body, name="attn_bwd", grid=(nb + 1,), in_specs=[wide] + _attn_specs(nb, True),
                 out_specs=[wide, wide, done, done, meta, meta, par],
                 out_shape=[jax.ShapeDtypeStruct((rows, D_MODEL), BF16), jax.ShapeDtypeStruct((rows, D_MODEL), BF16),
                            jax.ShapeDtypeStruct((rows, KV_WIDTH), F32), jax.ShapeDtypeStruct((rows, KV_WIDTH), F32),
                            jax.ShapeDtypeStruct((N_META, KV_WIDTH), F32), jax.ShapeDtypeStruct((N_META, KV_WIDTH), F32),
                            jax.ShapeDtypeStruct(sinks.shape, F32)],
                 scratch=[pltpu.VMEM((BLOCK, KV_WIDTH), F32), pltpu.VMEM((BLOCK, KV_WIDTH), F32)],
                 sem=("arbitrary",))(da, *([proj] * 8), sinks)


XBC_BLOCK0 = SEG["xbc"][2] // D_MODEL
CONV_COL_BLOCKS = CONV_DIM // D_MODEL
DT_TILE = SEG["dt"][2] // LANES


HALO = 8


def _conv_rows(length):
    return 544 if length % 544 == 0 else BLOCK


def _shift_rows(cur, before, j):
    if j == 0:
        return cur
    n = cur.shape[0]
    row = lax.broadcasted_iota(jnp.int32, cur.shape, 0)
    head = pltpu.roll(before, j, 0)
    if n > HALO:
        head = jnp.concatenate([head, jnp.zeros((n - HALO, cur.shape[1]), cur.dtype)], axis=0)
    return jnp.where(row >= j, pltpu.roll(cur, j, 0), head)


def _conv_pre(cur, before, w_ref, b_ref):
    pre = b_ref[...] + w_ref[CONV_WIDTH - 1:CONV_WIDTH, :] * cur
    for k in range(CONV_WIDTH - 1):
        pre = pre + w_ref[k:k + 1, :] * _shift_rows(cur, before, CONV_WIDTH - 1 - k)
    return pre


def _conv_specs(steps, rows, col0=0):
    first = XBC_BLOCK0 + col0
    halos = rows // HALO
    cur = pl.BlockSpec((rows, D_MODEL), lambda j, i: (i, first + j))
    before = pl.BlockSpec((HALO, D_MODEL), lambda j, i: (jnp.maximum(i * halos - 1, 0), first + j))
    after = pl.BlockSpec((HALO, D_MODEL), lambda j, i: (jnp.minimum(i + 1, steps - 1) * halos, first + j))
    return cur, before, after


def _valid_rows(i, rows):
    row = lax.broadcasted_iota(jnp.int32, (rows, D_MODEL), 0)
    return jnp.maximum((row >= PAD_ROWS).astype(F32), jnp.where(i > 0, 1.0, 0.0))


def conv_fwd(proj, conv_w, conv_b):
    rows = _conv_rows(proj.shape[0])
    steps = proj.shape[0] // rows
    cur, before, _ = _conv_specs(steps, rows)

    def body(c_ref, p_ref, w_ref, b_ref, o_ref):
        i = pl.program_id(1)
        pre = _conv_pre(c_ref[...], p_ref[...] * jnp.where(i > 0, 1.0, 0.0), w_ref, b_ref)
        o_ref[...] = _silu(pre) * _valid_rows(i, rows)

    return _call(body, name="conv_fwd", grid=(CONV_COL_BLOCKS, steps),
                 in_specs=[cur, before, pl.BlockSpec((CONV_WIDTH, D_MODEL), lambda j, i: (0, j)),
                           pl.BlockSpec((1, D_MODEL), lambda j, i: (0, j))],
                 out_specs=pl.BlockSpec((rows, D_MODEL), lambda j, i: (i, j)),
                 out_shape=jax.ShapeDtypeStruct((proj.shape[0], CONV_DIM), F32),
                 sem=("parallel", "parallel"))(proj, proj, conv_w, conv_b)


def conv_bwd(name, dparts, col0, proj, conv_w, conv_b):
    rows = _conv_rows(proj.shape[0])
    steps = proj.shape[0] // rows
    last = steps - 1
    ncol = sum(d.shape[1] for d in dparts) // D_MODEL
    np_ = len(dparts)
    cur, before, after = _conv_specs(steps, rows, col0)
    dcur = [pl.BlockSpec((rows, d.shape[1] // ncol), lambda j, i: (i, j)) for d in dparts]
    dafter = [pl.BlockSpec((HALO, d.shape[1] // ncol), lambda j, i: (jnp.minimum(i + 1, last) * (rows // HALO), j))
              for d in dparts]
    out_cur = pl.BlockSpec((rows, D_MODEL), lambda j, i: (i, j))
    wspec = pl.BlockSpec((CONV_WIDTH, D_MODEL), lambda j, i: (0, col0 + j))
    bspec = pl.BlockSpec((1, D_MODEL), lambda j, i: (0, col0 + j))
    wout = pl.BlockSpec((CONV_WIDTH, D_MODEL), lambda j, i: (0, j))
    bout = pl.BlockSpec((1, D_MODEL), lambda j, i: (0, j))

    def body(*refs):
        dc_refs, da_refs = refs[:np_], refs[np_:2 * np_]
        c_ref, p_ref, a_ref, w_ref, b_ref, du_ref, dw_ref, db_ref = refs[2 * np_:]
        i = pl.program_id(1)
        row = lax.broadcasted_iota(jnp.int32, (rows, D_MODEL), 0)
        curv = c_ref[...]
        beforev = p_ref[...] * jnp.where(i > 0, 1.0, 0.0)
        side_by_side = lambda rs: rs[0][...] if np_ == 1 else jnp.concatenate([r[...] for r in rs], axis=1)

        def dpre_of(pre, d):
            s = _sigmoid(pre)
            return d * (s * (1.0 + pre * (1.0 - s)))

        dp_c = dpre_of(_conv_pre(curv, beforev, w_ref, b_ref), side_by_side(dc_refs) * _valid_rows(i, rows))
        dp_a = dpre_of(_conv_pre(a_ref[...], curv[rows - HALO:], w_ref, b_ref),
                       side_by_side(da_refs) * jnp.where(i < last, 1.0, 0.0))
        du = w_ref[CONV_WIDTH - 1:CONV_WIDTH, :] * dp_c
        for j in range(1, CONV_WIDTH):
            tail = jnp.concatenate([jnp.zeros((rows - HALO, D_MODEL), F32), pltpu.roll(dp_a, HALO - j, 0)], axis=0)
            up = jnp.where(row < rows - j, pltpu.roll(dp_c, rows - j, 0), tail)
            du = du + w_ref[CONV_WIDTH - 1 - j:CONV_WIDTH - j, :] * up
        du_ref[...] = du.astype(BF16)

        @pl.when(i == 0)
        def _():
            dw_ref[...] = jnp.zeros_like(dw_ref)
            db_ref[...] = jnp.zeros_like(db_ref)

        for k in range(CONV_WIDTH):
            dw_ref[k:k + 1, :] += jnp.sum(dp_c * _shift_rows(curv, beforev, CONV_WIDTH - 1 - k), axis=0, keepdims=True)
        db_ref[...] += jnp.sum(dp_c, axis=0, keepdims=True)

    width = ncol * D_MODEL
    return _call(body, name=name, grid=(ncol, steps),
                 in_specs=dcur + dafter + [cur, before, after, wspec, bspec], out_specs=[out_cur, wout, bout],
                 out_shape=[jax.ShapeDtypeStruct((proj.shape[0], width), BF16),
                            jax.ShapeDtypeStruct((CONV_WIDTH, width), F32), jax.ShapeDtypeStruct((1, width), F32)],
                 sem=("parallel", "arbitrary"))(*dparts, *dparts, proj, proj, proj, conv_w, conv_b)


def _head_expand():
    e = np.zeros((LANES, SSM_INNER), np.float32)
    for h in range(SSM_HEADS):
        e[h, h * HEAD_DIM:(h + 1) * HEAD_DIM] = 1.0
    return jnp.asarray(e, dtype=BF16)


def _softplus(x):
    return jnp.maximum(x, 0.0) + jnp.log(1.0 + jnp.exp(-jnp.abs(x)))


def _bf16_parts(x):
    hi = x.astype(BF16)
    rest = x - hi.astype(F32)
    mid = rest.astype(BF16)
    return hi, mid, (rest - mid.astype(F32)).astype(BF16)


@jax.custom_vjp
def _times_01(x, m):
    return sum(jnp.dot(p, m, preferred_element_type=F32) for p in _bf16_parts(x))


def _times_01_bwd(m, g):
    return sum(lax.dot_general(p, m, _NT, preferred_element_type=F32) for p in _bf16_parts(g)), jnp.zeros_like(m)


_times_01.defvjp(lambda x, m: (_times_01(x, m), m), _times_01_bwd)


def _causal_ones():
    l = lax.broadcasted_iota(jnp.int32, (BLOCK, BLOCK), 0)
    s = lax.broadcasted_iota(jnp.int32, (BLOCK, BLOCK), 1)
    return (l >= s).astype(BF16)


@jax.custom_vjp
def _cumsum_rows(a):
    return sum(jnp.dot(_causal_ones(), p, preferred_element_type=F32) for p in _bf16_parts(a))


def _cumsum_rows_bwd(_, g):
    tn = (((0,), (0,)), ((), ()))
    return (sum(lax.dot_general(_causal_ones(), p, tn, preferred_element_type=F32) for p in _bf16_parts(g)),)


_cumsum_rows.defvjp(lambda a: (_cumsum_rows(a), None), _cumsum_rows_bwd)


def _ssd_heads(dt_tile, bias, alog, dsk, expand):
    dt = _softplus(dt_tile + bias)
    a = dt * (-jnp.exp(alog))
    one_row = lambda v: jnp.broadcast_to(v, (HALO, LANES))
    return _times_01(jnp.concatenate([dt, _cumsum_rows(a), one_row(jnp.sum(a, axis=0, keepdims=True)), one_row(dsk)],
                                     axis=0), expand)


HEADS_ROWS = 2 * BLOCK + 2 * HALO


def _ssd_group(xs, per_lane, bg, cg, state):
    l = lax.broadcasted_iota(jnp.int32, (BLOCK, BLOCK), 0)
    s = lax.broadcasted_iota(jnp.int32, (BLOCK, BLOCK), 1)
    causal = l >= s
    first_head = s < HEAD_DIM
    dtx, cs = per_lane[0:BLOCK], per_lane[BLOCK:2 * BLOCK]
    tot, dsk = per_lane[2 * BLOCK:2 * BLOCK + 1], per_lane[2 * BLOCK + HALO:2 * BLOCK + HALO + 1]
    bb, cb16 = bg.astype(BF16), cg.astype(BF16)
    cb = lax.dot_general(cb16, bb, _NT, preferred_element_type=F32)
    xr = xs * dtx
    y_diag = []
    for p in range(GROUP_W // LANES):
        lanes = slice(p * LANES, (p + 1) * LANES)
        c_pair = cs[:, lanes]
        c_swap = _swap_halves(c_pair)
        m = []
        for c_head in (jnp.where(first_head, c_pair, c_swap), jnp.where(first_head, c_swap, c_pair)):
            m.append(cb * jnp.exp(jnp.where(causal, c_head - c_head.T, -jnp.inf)))
        x_pair = xr[:, lanes]
        x_diag = jnp.concatenate([jnp.where(first_head, x_pair, 0.0), jnp.where(first_head, 0.0, x_pair)], axis=0)
        y_diag.append(jnp.dot(jnp.concatenate(m, axis=1).astype(BF16), x_diag.astype(BF16),
                              preferred_element_type=F32))
    st = lax.dot_general(bb, (xr * jnp.exp(tot - cs)).astype(BF16), (((0,), (0,)), ((), ())),
                         preferred_element_type=F32)
    new_state = state * jnp.exp(tot) + st
    y_off = jnp.dot(cb16, state.astype(BF16), preferred_element_type=F32) * jnp.exp(cs)
    return jnp.concatenate(y_diag, axis=1) + y_off + dsk * xs, new_state


BC_WIDTH = SSM_GROUPS * SSM_STATE


def _ssd_specs(chunk):
    xs = pl.BlockSpec((BLOCK, SSM_INNER), lambda c: (chunk(c), 0))
    dt = pl.BlockSpec((BLOCK, LANES), lambda c: (chunk(c), DT_TILE))
    expand = _full((LANES, SSM_INNER))
    b = pl.BlockSpec((BLOCK, BC_WIDTH), lambda c: (chunk(c), SSM_INNER // BC_WIDTH))
    cc = pl.BlockSpec((BLOCK, BC_WIDTH), lambda c: (chunk(c), SSM_INNER // BC_WIDTH + 1))
    par = _full((1, LANES))
    state = pl.BlockSpec((None, SSM_STATE, SSM_INNER), lambda c: (chunk(c), 0, 0))
    return xs, dt, expand, b, cc, par, state


def _group_lanes(g):
    return slice(g * GROUP_W, (g + 1) * GROUP_W), slice(g * SSM_STATE, (g + 1) * SSM_STATE)


def ssd_fwd(xbc, proj, expand, bias, alog, dsk):
    nb = xbc.shape[0] // BLOCK
    xs, dt, ex, b, cc, par, state = _ssd_specs(lambda c: c)

    def body(x_ref, dt_ref, e_ref, bi_ref, al_ref, dk_ref, b_ref, c_ref, y_ref, sp_ref, st_ref):
        @pl.when(pl.program_id(0) == 0)
        def _():
            st_ref[...] = jnp.zeros_like(st_ref)

        per_lane = _ssd_heads(dt_ref[...], bi_ref[...], al_ref[...], dk_ref[...], e_ref[...])
        for g in range(SSM_GROUPS):
            wide, tile = _group_lanes(g)
            entering = st_ref[:, wide]
            sp_ref[:, wide] = entering
            y_ref[:, wide], st_ref[:, wide] = _ssd_group(x_ref[:, wide], per_lane[:, wide], b_ref[:, tile],
                                                         c_ref[:, tile], entering)

    return _call(body, name="ssd_fwd", grid=(nb,), in_specs=[xs, dt, ex, par, par, par, b, cc],
                 out_specs=[xs, state],
                 out_shape=[jax.ShapeDtypeStruct((nb * BLOCK, SSM_INNER), F32),
                            jax.ShapeDtypeStruct((nb, SSM_STATE, SSM_INNER), F32)],
                 scratch=[pltpu.VMEM((SSM_STATE, SSM_INNER), F32)],
                 sem=("arbitrary",))(xbc, proj, expand, bias, alog, dsk, xbc, xbc)


def ssd_bwd(dy, xbc, proj, expand, bias, alog, dsk, states, comm):
    nb = xbc.shape[0] // BLOCK
    last = nb - 1
    xs, dt, ex, b, cc, par, state = _ssd_specs(lambda c: last - c)
    tile = pl.BlockSpec((BLOCK, LANES), lambda c: (last - c, 0))
    nspec = pl.BlockSpec((BLOCK, BC_WIDTH), lambda c: (last - c, 0))

    def body(dy_ref, x_ref, dt_ref, e_ref, bi_ref, al_ref, dk_ref, b_ref, c_ref, sp_ref,
             dx_ref, ddt_ref, db_ref, dc_ref, dbi_ref, dal_ref, ddk_ref, ds_ref):
        @pl.when(pl.program_id(0) == 0)
        def _():
            ds_ref[...] = jnp.zeros_like(ds_ref)
            dbi_ref[...] = jnp.zeros_like(dbi_ref)
            dal_ref[...] = jnp.zeros_like(dal_ref)
            ddk_ref[...] = jnp.zeros_like(ddk_ref)

        expand = e_ref[...]
        per_lane, heads_vjp = jax.vjp(lambda t, bi, al, dk: _ssd_heads(t, bi, al, dk, expand), dt_ref[...], bi_ref[...],
                                      al_ref[...], dk_ref[...])
        d_per_lane = []
        for g in range(SSM_GROUPS):
            wide, tile_lanes = _group_lanes(g)
            _, vjp = jax.vjp(_ssd_group, x_ref[:, wide], per_lane[:, wide], b_ref[:, tile_lanes], c_ref[:, tile_lanes],
                             sp_ref[:, wide])
            (dx_ref[:, wide], d_lanes, db_ref[:, tile_lanes], dc_ref[:, tile_lanes],
             ds_ref[:, wide]) = vjp((dy_ref[:, wide], ds_ref[:, wide]))
            d_per_lane.append(d_lanes)
        ddt, dbi, dal, ddk = heads_vjp(jnp.concatenate(d_per_lane, axis=1))
        ddt_ref[...] = ddt.astype(BF16)
        dbi_ref[...] += dbi
        dal_ref[...] += dal
        ddk_ref[...] += ddk

    def at():
        c = pl.program_id(0)
        return c == 0, c == 0, c == last

    par_shape = jax.ShapeDtypeStruct((1, LANES), F32)
    return _call_with_comm(
        body, comm, at, (dy, xbc, proj, expand, bias, alog, dsk, xbc, xbc, states), name="ssd_bwd",
        grid=(nb,), in_specs=[xs, xs, dt, ex, par, par, par, b, cc, state],
        out_specs=[xs, tile, nspec, nspec, par, par, par],
        out_shape=[jax.ShapeDtypeStruct((nb * BLOCK, SSM_INNER), F32), jax.ShapeDtypeStruct((nb * BLOCK, LANES), BF16),
                   jax.ShapeDtypeStruct((nb * BLOCK, BC_WIDTH), F32), jax.ShapeDtypeStruct((nb * BLOCK, BC_WIDTH), F32),
                   par_shape, par_shape, par_shape],
        scratch=[pltpu.VMEM((SSM_STATE, SSM_INNER), F32)])


SLAB_ROWS = 16
SLAB_META_ROW = 8
SLAB_META_SHAPE = (8, 2 * D_MODEL)
SLAB_LOSS_ROW = 7


def pack_small(dcw, dcb, dgpre, dgpost, dbias, dalog, ddsk, dsinks, dgn, dmeta, loss_tile):
    def body(cw, cb, gpre, gpost, dtb, al, dk, sk, gn, meta, loss, o_ref):
        o_ref[...] = jnp.zeros_like(o_ref)
        o_ref[SLAB_LOSS_ROW:SLAB_LOSS_ROW + 1, 0:LANES] = loss[0:1, :]
        o_ref[0:CONV_WIDTH, :] = cw[...]
        o_ref[4:5, :] = cb[...]
        o_ref[5:6, 0:1024] = gpre[...]
        o_ref[5:6, 1024:2048] = gpost[...]
        o_ref[5:6, 2048:2176] = dtb[...]
        o_ref[5:6, 2176:2304] = al[...]
        o_ref[5:6, 2304:2432] = dk[...]
        o_ref[5:6, 2432:2560] = sk[...]
        o_ref[6:7, 0:SSM_INNER] = gn[...]
        o_ref[SLAB_META_ROW:SLAB_ROWS, 0:SLAB_META_SHAPE[1]] = meta[...]

    args = (dcw, dcb, dgpre, dgpost, dbias, dalog, ddsk, dsinks, dgn, dmeta, loss_tile)
    return _call(body, name="pack_small", in_specs=[_full(a.shape) for a in args],
                 out_specs=_full((SLAB_ROWS, CONV_DIM)), out_shape=jax.ShapeDtypeStruct((SLAB_ROWS, CONV_DIM), F32))(*args)


def _lane_tile(v):
    return jnp.pad(v, ((0, 0), (0, LANES - v.shape[1])))


def kernel(x, meta_tokens, g_pre, w_in, conv_w, conv_b, dt_bias, a_log, d_skip, attn_sinks, g_ssm_norm, w_out_att, w_out_ssm, w_out, g_post, loss_target, m_meta_tokens, m_g_pre, m_w_in, m_conv_w, m_conv_b, m_dt_bias, m_a_log, m_d_skip, m_attn_sinks, m_g_ssm_norm, m_w_out_att, m_w_out_ssm, m_w_out, m_g_post, v_meta_tokens, v_g_pre, v_w_in, v_conv_w, v_conv_b, v_dt_bias, v_a_log, v_d_skip, v_attn_sinks, v_g_ssm_norm, v_w_out_att, v_w_out_ssm, v_w_out, v_g_post):
    chip = _chip_index()

    conv_w_rows = jnp.pad(conv_w[0], ((0, 2 * 8 - CONV_WIDTH), (0, 0)))
    w_in_t, m_w_in_t, v_w_in_t = w_in[0].T, m_w_in[0].T, v_w_in[0].T
    (h, u), (gathered_w_in, g_conv_w, g_meta) = prep(
        "gather_w_in", x, g_pre, TwoLevelGather([pack_w_in(w_in_t), conv_w_rows, meta_tokens]))
    w_all_t = unpack_w_in(gathered_w_in)
    cw_full = g_conv_w[:, :CONV_WIDTH].transpose(1, 0, 2).reshape(CONV_WIDTH, CONV_DIM)
    behind_w_in = 0.0 * g_meta[0, 0, 0]
    w_out_flight, w_out_started = chip_exchange_start(
        "gather_w_out_start", [(w[0] + behind_w_in).astype(BF16) for w in (w_out_att, w_out_ssm, w_out)], False)

    proj = project("in_proj", u, w_all_t, w_out_started)

    sinks3 = attn_sinks.reshape(ATT_Q_HEADS, 1, 1)
    a_att = attn_fwd(proj, sinks3)

    xbc = conv_fwd(proj, cw_full, conv_b)
    expand = _head_expand()
    head_pars = (_lane_tile(dt_bias), _lane_tile(a_log), _lane_tile(d_skip))
    y_ssd, states = ssd_fwd(xbc, proj, expand, *head_pars)
    woa, wos, wo = [g.reshape(-1, D_MODEL) for g in chip_exchange_wait("gather_w_out_wait", w_out_flight, False, y_ssd)]

    (yn, merged, dout, dy_att, dy_ssm, da_att, dy_ssd, dz_ssm, dga, dgs, dres, loss_tile, dg_post, dgn) = tail(
        y_ssd, proj, a_att, x, loss_target, woa, wos, wo, g_ssm_norm, g_post)

    dwo = mm_tn("out_proj_dw", merged, dout)
    dwoa = mm_tn("att_out_dw", a_att, dy_att)
    dwos = mm_tn("ssm_out_dw", yn, dy_ssm)
    dq, dz_att, dk, dv, dkmeta, dvmeta, dsinks3 = attn_bwd(da_att, proj, sinks3)
    dk = dk.at[PAD_ROWS:BLOCK].add(dkmeta).astype(BF16)
    dv = dv.at[PAD_ROWS:BLOCK].add(dvmeta).astype(BF16)

    def pieces(g):
        return g.reshape(4, 2, g.shape[0] // 8, g.shape[1])

    def to_owner(g):
        return (lambda ref, dev: ref.at[_chip_of(dev), dev[2]], (g.shape[0] // 8, g.shape[1]))

    (dxs, ddt_tile, dbg, dcg, dbias, dalog, ddsk), sent_w_out = ssd_bwd(
        dy_ssd, xbc, proj, expand, *head_pars, states,
        DirectExchange([pieces(dwoa), pieces(dwos), pieces(dwo)], [to_owner(dwoa), to_owner(dwos), to_owner(dwo)],
                       ALL_MASKS, "dev", 8))
    dxs_raw, dcw_xs, dcb_xs = conv_bwd("conv_bwd_x", [dxs], 0, proj, cw_full, conv_b)
    dbc_raw, dcw_bc, dcb_bc = conv_bwd("conv_bwd_bc", [dbg, dcg], SSM_INNER // D_MODEL, proj, cw_full, conv_b)
    dcw = jnp.concatenate([dcw_xs, dcw_bc], axis=1)
    dcb = jnp.concatenate([dcb_xs, dcb_bc], axis=1)

    narrow = jnp.concatenate([dk, dv, ddt_tile, jnp.zeros((dk.shape[0], N_ACT - N_ALIGNED), BF16)], axis=1)
    dproj = [dz_ssm, dxs_raw, dbc_raw, dq, dz_att, dga, dgs, narrow]
    by_shard = [7, 3, 4, 0, 1, 2, 5, 6]
    partial, from_sibling = weight_grad_t("in_proj_dw", dproj, by_shard, u)
    grads_flight, started = pair_sum_exchange_start(
        "reduce_w_in_start", partial.reshape(4, 2, PACK_W // 2, D_MODEL), from_sibling)
    du = project_back("in_proj_dx", dproj, w_all_t, started)
    grad_x, dmeta, dg_pre = prep_bwd(h, du, dres, g_pre)

    halves = [sum_slots("sum_" + nm, r, started) for nm, r in zip(("w_out_att", "w_out_ssm", "w_out"), sent_w_out)]
    shared = run_comm("share_w_out", DirectExchange(halves, [None] * 3, SIBLING_MASK, "core", 2))
    g_woa, g_wos, g_wo = [f.reshape(2 * f.shape[1], f.shape[2]) for f in shared]
    d_woa, nm_woa, nv_woa = adamw_rows("adamw_w_out_att", g_woa, w_out_att[0], m_w_out_att[0], v_w_out_att[0])
    d_wos, nm_wos, nv_wos = adamw_rows("adamw_w_out_ssm", g_wos, w_out_ssm[0], m_w_out_ssm[0], v_w_out_ssm[0])
    d_wo, nm_wo, nv_wo = adamw_rows("adamw_w_out", g_wo, w_out[0], m_w_out[0], v_w_out[0])

    window_done = dg_pre + 0.0 * (d_woa[0:1] + d_wos[0:1] + d_wo[0:1])
    sent_w_in, = chip_exchange_wait("reduce_w_in_wait", grads_flight, True, window_done)
    slab = pack_small(dcw, dcb, dg_pre, dg_post, dbias, dalog, ddsk, _lane_tile(dsinks3.reshape(1, ATT_Q_HEADS)), dgn,
                      dmeta.reshape(SLAB_META_SHAPE), loss_tile)
    slabs, shared_w_in = run_comm("share_w_in", Both(DirectExchange([slab], [None], ALL_MASKS, "dev", 8),
                                                     PairOfSums(sent_w_in)))
    small = sum_slots("sum_small", slabs, started)
    loss = small[SLAB_LOSS_ROW, 0]
    g_w_in, d_w_in, nm_w_in, nv_w_in = [
        a.T for a in adamw_w_in(shared_w_in.reshape(PACK_W, D_MODEL), w_in_t, m_w_in_t, v_w_in_t)]

    cw_cols = CONV_DIM // 4
    meta_cols = D_MODEL // 4
    g_small = {
        "meta_tokens": lax.dynamic_slice(
            small[SLAB_META_ROW:SLAB_ROWS, 0:SLAB_META_SHAPE[1]].reshape(N_META, D_MODEL), (0, chip * meta_cols),
            (N_META, meta_cols)),
        "g_pre": small[5:6, 0:1024],
        "conv_w": lax.dynamic_slice(small, (0, chip * cw_cols), (CONV_WIDTH, cw_cols)),
        "conv_b": small[4:5, :],
        "dt_bias": small[5:6, 2048:2048 + SSM_HEADS],
        "a_log": small[5:6, 2176:2176 + SSM_HEADS],
        "d_skip": small[5:6, 2304:2304 + SSM_HEADS],
        "attn_sinks": small[5:6, 2432:2432 + ATT_Q_HEADS],
        "g_ssm_norm": small[6:7, 0:SSM_INNER],
        "g_post": small[5:6, 1024:2048],
    }
    names = list(g_small)
    w_small = dict(meta_tokens=meta_tokens, g_pre=g_pre, conv_w=conv_w[0], conv_b=conv_b, dt_bias=dt_bias, a_log=a_log,
                   d_skip=d_skip, attn_sinks=attn_sinks, g_ssm_norm=g_ssm_norm, g_post=g_post)
    m_small = dict(meta_tokens=m_meta_tokens, g_pre=m_g_pre, conv_w=m_conv_w[0], conv_b=m_conv_b, dt_bias=m_dt_bias,
                   a_log=m_a_log, d_skip=m_d_skip, attn_sinks=m_attn_sinks, g_ssm_norm=m_g_ssm_norm, g_post=m_g_post)
    v_small = dict(meta_tokens=v_meta_tokens, g_pre=v_g_pre, conv_w=v_conv_w[0], conv_b=v_conv_b, dt_bias=v_dt_bias,
                   a_log=v_a_log, d_skip=v_d_skip, attn_sinks=v_attn_sinks, g_ssm_norm=v_g_ssm_norm, g_post=v_g_post)
    upd = dict(zip(names, adamw_small([g_small[k] for k in names], [w_small[k] for k in names],
                                      [m_small[k] for k in names], [v_small[k] for k in names])))

    lead = {"conv_w"}

    def shaped(name, a):
        return a[None] if name in lead else a

    grads = dict(g_small, w_in=g_w_in, w_out_att=g_woa, w_out_ssm=g_wos, w_out=g_wo)
    deltas = dict({k: upd[k][0] for k in names}, w_in=d_w_in, w_out_att=d_woa, w_out_ssm=d_wos, w_out=d_wo)
    new_m = dict({k: upd[k][1] for k in names}, w_in=nm_w_in, w_out_att=nm_woa, w_out_ssm=nm_wos, w_out=nm_wo)
    new_v = dict({k: upd[k][2] for k in names}, w_in=nv_w_in, w_out_att=nv_woa, w_out_ssm=nv_wos, w_out=nv_wo)
    lead |= {"w_in", "w_out_att", "w_out_ssm", "w_out"}
    order = ["meta_tokens", "g_pre", "w_in", "conv_w", "conv_b", "dt_bias", "a_log", "d_skip", "attn_sinks",
             "g_ssm_norm", "w_out_att", "w_out_ssm", "w_out", "g_post"]
    outs = [loss, grad_x]
    for group in (grads, deltas, new_m, new_v):
        outs += [shaped(k, group[k]) for k in order]
    return tuple(outs)
```

```python
import numpy as np
import jax
import jax.numpy as jnp
from jax import lax
from jax.experimental import pallas as pl
from jax.experimental.pallas import tpu as pltpu

F32 = jnp.float32
BF16 = jnp.bfloat16

D_MODEL = 1024
N_META = 16
BLOCK = 128
PAD_ROWS = BLOCK - N_META
NORM_EPS = 1e-6
HEAD_DIM = 64
ATT_Q_HEADS = 16
ATT_KV_HEADS = 4
ATT_GROUP = 4
SSM_INNER = 2048
SSM_HEADS = 32
SSM_GROUPS = 4
SSM_STATE = 128
CONV_WIDTH = 4
CONV_DIM = 3072
LANES = 128

ADAM_LR = 0.001
ADAM_B1 = 0.9
ADAM_B2 = 0.999
ADAM_EPS = 1e-08
ADAM_WD = 0.01
ADAM_STEP = 10

VMEM_LIMIT = 48 * 1024 * 1024

SHARD_W = 2440
PACK_W = 2560
SHARD_STRIDE = 2432
N_ALIGNED = 9856
N_ACT = 10240
SEG = {
    "q": (0, 1024, 5120), "k": (1024, 256, 9216), "v": (1280, 256, 9472), "z_att": (1536, 1024, 6144),
    "z_ssm": (2560, 2048, 0), "xbc": (4608, 3072, 2048), "dt": (7680, 128, 9728),
    "gate_att": (7808, 1024, 7168), "gate_ssm": (8832, 1024, 8192),
}
DT_STORED_START = 7680
DT_PAD = LANES - SSM_HEADS


def _act_col(aligned_col):
    for a0, w, p0 in SEG.values():
        if a0 <= aligned_col < a0 + w:
            return p0 + aligned_col - a0
    raise ValueError(aligned_col)


def _call(body, *, name, out_shape, in_specs, out_specs, grid=(), scratch=(), sem=None, aliases=None):
    return pl.pallas_call(
        body, out_shape=out_shape, grid=grid, in_specs=in_specs, out_specs=out_specs, scratch_shapes=list(scratch),
        name=name, input_output_aliases=aliases or {},
        compiler_params=pltpu.CompilerParams(dimension_semantics=sem, vmem_limit_bytes=VMEM_LIMIT))


def _full(shape):
    n = len(shape)
    return pl.BlockSpec(shape, lambda *_: (0,) * n)


def _chip_index():
    return lax.axis_index("x") * 2 + lax.axis_index("y")


_sigmoid = jax.nn.sigmoid


def _silu(z):
    return z * _sigmoid(z)


def _rms(x, g):
    return x * lax.rsqrt(jnp.mean(x * x, axis=-1, keepdims=True) + NORM_EPS) * g


def _peer(mask):
    x, y, c = lax.axis_index("x"), lax.axis_index("y"), lax.axis_index("c")
    return ((1 - x) if mask & 4 else x, (1 - y) if mask & 2 else y, (1 - c) if mask & 1 else c)


def _me():
    return lax.axis_index("x"), lax.axis_index("y"), lax.axis_index("c")


def _chip_of(dev):
    return 2 * dev[0] + dev[1]


CHIP_MASKS = (4, 2, 6)
ALL_MASKS = (1, 2, 3, 4, 5, 6, 7)
SIBLING_MASK = (1,)


def _remote(src, dst, send_sem, recv_sem, dev):
    return pltpu.make_async_remote_copy(src_ref=src, dst_ref=dst, send_sem=send_sem, recv_sem=recv_sem,
                                        device_id=dev, device_id_type=pl.DeviceIdType.MESH)


class _StagedCopy:
    def __init__(self, src, stage, dst, load_sem, store_sem):
        self.load = pltpu.make_async_copy(src, stage, load_sem)
        self.store = pltpu.make_async_copy(stage, dst, store_sem)

    def start(self):
        self.load.start()
        self.load.wait()
        self.store.start()

    def wait(self):
        self.store.wait()


class DirectExchange:
    def __init__(self, arrays, pieces, masks, slot_kind, nslots):
        self.arrays, self.pieces, self.masks, self.slot_kind = list(arrays), list(pieces), masks, slot_kind
        n, nk = len(arrays), len(masks)
        shapes = [a.shape if p is None else p[1] for a, p in zip(arrays, pieces)]
        self.out_shape = [jax.ShapeDtypeStruct((nslots,) + tuple(s), a.dtype) for s, a in zip(shapes, arrays)]
        self.scratch = ([pltpu.SemaphoreType.DMA((n * nk,)), pltpu.SemaphoreType.DMA((n * nk,)),
                         pltpu.SemaphoreType.DMA((2 * n,))] + [pltpu.VMEM(s, a.dtype) for s, a in zip(shapes, arrays)])
        self.has_mid = False

    def _copies(self, ins, outs, scratch):
        send_sems, recv_sems = scratch[:2]
        me = _me()
        slot = {"chip": _chip_of(me), "dev": 4 * me[0] + 2 * me[1] + me[2], "core": me[2]}[self.slot_kind]
        nk = len(self.masks)

        def piece(a, dev):
            return ins[a] if self.pieces[a] is None else self.pieces[a][0](ins[a], dev)

        local_sems, stages = scratch[2], scratch[3:]
        local = [_StagedCopy(piece(a, me), stages[a], outs[a].at[slot], local_sems.at[2 * a], local_sems.at[2 * a + 1])
                 for a in range(len(ins))]
        remote = []
        for a in range(len(ins)):
            for ki, mask in enumerate(self.masks):
                dev = _peer(mask)
                remote.append(_remote(piece(a, dev), outs[a].at[slot], send_sems.at[a * nk + ki],
                                      recv_sems.at[a * nk + ki], dev))
        return local, remote

    def start(self, ins, outs, scratch):
        local, remote = self._copies(ins, outs, scratch)
        for cp in remote + local:
            cp.start()

    def finish(self, ins, outs, scratch):
        local, remote = self._copies(ins, outs, scratch)
        for cp in remote + local:
            cp.wait()


SPLIT_ROWS = 16


class TwoLevelGather:
    def __init__(self, arrays):
        self.arrays = list(arrays)
        n = len(arrays)
        self.out_shape = [jax.ShapeDtypeStruct((4,) + a.shape, a.dtype) for a in arrays]
        self.scratch = ([pltpu.SemaphoreType.DMA((4 * n,)), pltpu.SemaphoreType.DMA((4 * n,)),
                         pltpu.SemaphoreType.DMA((3 * n,)), pltpu.SemaphoreType.DMA((3 * n,)),
                         pltpu.SemaphoreType.DMA((2 * n,))] + [pltpu.VMEM(a.shape, a.dtype) for a in arrays])
        self.has_mid = True

    def _copies(self, ins, outs, scratch):
        ici_send, ici_recv, fwd_send, fwd_recv, local_sems = scratch[:5]
        stages = scratch[5:]
        me = _me()
        sibling, in_x, in_y, diagonal = _peer(1), _peer(4), _peer(2), _peer(6)
        plan = []
        for a in range(len(ins)):
            half = ins[a].shape[0] // 2
            first = half // 2 if half % (2 * SPLIT_ROWS) == 0 else half
            mine = pl.ds(me[2] * half, half)
            local = _StagedCopy(ins[a], stages[a], outs[a].at[_chip_of(me)], local_sems.at[2 * a], local_sems.at[2 * a + 1])

            def ici(k, src, dst, dev):
                return _remote(src, dst, ici_send.at[4 * a + k], ici_recv.at[4 * a + k], dev)

            def d2d(k, chip):
                zone = outs[a].at[_chip_of(chip), mine]
                return _remote(zone, zone, fwd_send.at[3 * a + k], fwd_recv.at[3 * a + k], sibling)

            own_zone = outs[a].at[_chip_of(me), mine]
            from_x = outs[a].at[_chip_of(in_x), pl.ds(me[2] * half, first)]
            onward = [ici(2, from_x, from_x, in_y), None]
            if first < half:
                from_y = outs[a].at[_chip_of(in_y), pl.ds(me[2] * half + first, half - first)]
                onward[1] = ici(3, from_y, from_y, in_x)
            plan.append(dict(
                local=local,
                own=[ici(0, ins[a].at[mine], own_zone, in_x), ici(1, ins[a].at[mine], own_zone, in_y)],
                onward=onward, sibling=[d2d(0, in_x), d2d(1, in_y), d2d(2, diagonal)]))
        return plan

    def start(self, ins, outs, scratch):
        for p in self._copies(ins, outs, scratch):
            for cp in p["own"]:
                cp.start()
            p["local"].start()

    def mid(self, ins, outs, scratch):
        plan = self._copies(ins, outs, scratch)
        for p in plan:
            for k in range(2):
                p["own"][k].wait_recv()
                if p["onward"][k] is not None:
                    p["onward"][k].start()
                p["sibling"][k].start()
        for p in plan:
            for cp in p["onward"]:
                if cp is not None:
                    cp.wait_recv()
            p["sibling"][2].start()

    def finish(self, ins, outs, scratch):
        for p in self._copies(ins, outs, scratch):
            for cp in p["sibling"]:
                cp.wait_recv()
            for cp in p["own"] + p["sibling"] + [cp for cp in p["onward"] if cp is not None]:
                cp.wait_send()
            p["local"].wait()


class Both:
    def __init__(self, a, b):
        self.a, self.b = a, b
        self.arrays, self.out_shape = a.arrays + b.arrays, a.out_shape + b.out_shape
        self.scratch = a.scratch + b.scratch
        self.has_mid = False
        assert not (a.has_mid or b.has_mid)

    def _parts(self, ins, outs, sems):
        na, sa = len(self.a.arrays), len(self.a.scratch)
        return (ins[:na], outs[:na], sems[:sa]), (ins[na:], outs[na:], sems[sa:])

    def start(self, ins, outs, sems):
        pa, pb = self._parts(ins, outs, sems)
        self.a.start(*pa)
        self.b.start(*pb)

    def finish(self, ins, outs, sems):
        pa, pb = self._parts(ins, outs, sems)
        self.a.finish(*pa)
        self.b.finish(*pb)


class PairOfSums:
    def __init__(self, array):
        s, rows, cols = array.shape
        self.arrays = [array]
        self.out_shape = [jax.ShapeDtypeStruct((2, rows, cols), F32)]
        self.scratch = [pltpu.SemaphoreType.DMA((s,)), pltpu.SemaphoreType.DMA((3,)), pltpu.VMEM(array.shape, array.dtype),
                        pltpu.VMEM((rows, cols), F32)]
        self.has_mid = False

    def _copies(self, ins, outs, scratch):
        load_sems, sems, slots, total = scratch
        mine = outs[0].at[lax.axis_index("c")]
        loads = [pltpu.make_async_copy(ins[0].at[k], slots.at[k], load_sems.at[k]) for k in range(slots.shape[0])]
        return loads, [_remote(total, mine, sems.at[0], sems.at[1], _peer(1)), pltpu.make_async_copy(total, mine, sems.at[2])]

    def start(self, ins, outs, scratch):
        loads, stores = self._copies(ins, outs, scratch)
        slots, total = scratch[2:]
        for cp in loads:
            cp.start()
        for cp in loads:
            cp.wait()
        acc = slots[0].astype(F32)
        for k in range(1, slots.shape[0]):
            acc = acc + slots[k].astype(F32)
        total[...] = acc
        for cp in stores:
            cp.start()

    def finish(self, ins, outs, scratch):
        for cp in self._copies(ins, outs, scratch)[1]:
            cp.wait()


_ANY = pl.BlockSpec(memory_space=pl.ANY)


def run_comm(name, comm):
    n = len(comm.arrays)

    def body(*refs):
        ins, outs, sems = refs[:n], refs[n:2 * n], refs[2 * n:]
        comm.start(ins, outs, sems)
        if comm.has_mid:
            comm.mid(ins, outs, sems)
        comm.finish(ins, outs, sems)

    return pl.pallas_call(body, name=name, out_shape=comm.out_shape, in_specs=[_ANY] * n, out_specs=[_ANY] * n,
                          scratch_shapes=comm.scratch,
                          compiler_params=pltpu.CompilerParams(vmem_limit_bytes=VMEM_LIMIT))(*comm.arrays)


_HBM = pl.BlockSpec(memory_space=pltpu.HBM)
_SEM = pl.BlockSpec(memory_space=pltpu.SEMAPHORE)
_SIDE_EFFECT = pltpu.SideEffectType.DATAFLOW_SIDE_EFFECTING


def _chip_copies(srcs, lands, send_sems, recv_sems, by_target):
    me = _me()
    copies = []
    for a, (src, land) in enumerate(zip(srcs, lands)):
        for ki, mask in enumerate(CHIP_MASKS):
            dev = _peer(mask)
            k = a * len(CHIP_MASKS) + ki
            piece = src.at[_chip_of(dev)] if by_target else src
            copies.append(_remote(piece, land.at[_chip_of(me)], send_sems.at[k], recv_sems.at[k], dev))
    return copies


def chip_exchange_start(name, arrays, by_target):
    n = len(arrays)
    nsem = n * len(CHIP_MASKS)
    piece_shapes = [a.shape[1:] if by_target else a.shape for a in arrays]

    def body(*refs):
        srcs, lands = refs[:n], refs[n:2 * n]
        send_sems, recv_sems = refs[2 * n:2 * n + 2]
        token = refs[4 * n + 2]
        stages, local_sems = refs[4 * n + 3:5 * n + 3], refs[5 * n + 3]
        me = _me()
        for cp in _chip_copies(srcs, lands, send_sems, recv_sems, by_target):
            cp.start()
        own = [_StagedCopy(srcs[a].at[_chip_of(me)] if by_target else srcs[a], stages[a], lands[a].at[_chip_of(me)],
                           local_sems.at[2 * a], local_sems.at[2 * a + 1]) for a in range(n)]
        for cp in own:
            cp.load.start()
        for cp in own:
            cp.load.wait()
            cp.store.start()
        for cp in own:
            cp.store.wait()
        token[...] = jnp.zeros_like(token)

    lands = [lax.empty((4,) + tuple(s), a.dtype) for s, a in zip(piece_shapes, arrays)]
    hbm = lambda a: pltpu.HBM(a.shape, a.dtype)
    res = pl.pallas_call(
        body, name=name,
        out_shape=(pltpu.SemaphoreType.DMA((nsem,)), pltpu.SemaphoreType.DMA((nsem,)), *[hbm(a) for a in arrays],
                   *[hbm(l) for l in lands], jax.ShapeDtypeStruct((8, LANES), F32)),
        in_specs=(_HBM,) * (2 * n), out_specs=(_SEM, _SEM) + (_HBM,) * (2 * n) + (pl.BlockSpec(memory_space=pltpu.VMEM),),
        input_output_aliases={i: i + 2 for i in range(2 * n)},
        scratch_shapes=[pltpu.VMEM(tuple(s), a.dtype) for s, a in zip(piece_shapes, arrays)]
        + [pltpu.SemaphoreType.DMA((2 * n,))],
        compiler_params=pltpu.CompilerParams(has_side_effects=_SIDE_EFFECT, vmem_limit_bytes=VMEM_LIMIT),
    )(*[pltpu.with_memory_space_constraint(a, pltpu.HBM) for a in arrays + lands])
    return res[:-1], res[-1]


def pair_sum_exchange_start(name, partial, from_sibling):
    shards, _, rows, cols = partial.shape
    nsem = len(CHIP_MASKS)

    def body(p_ref, r_ref, _, send_sems, recv_sems, sum_ref, land_ref, token, p_buf, r_buf, o_buf, sems):
        me = _me()
        targets = [_chip_of(_peer(mask)) for mask in CHIP_MASKS] + [_chip_of(me)]
        remote = _chip_copies([sum_ref], [land_ref], send_sems, recv_sems, True)

        def loads(k):
            return [pltpu.make_async_copy(p_ref.at[targets[k], me[2]], p_buf.at[k % 2], sems.at[2 * (k % 2)]),
                    pltpu.make_async_copy(r_ref.at[targets[k]], r_buf.at[k % 2], sems.at[2 * (k % 2) + 1])]

        for cp in loads(0):
            cp.start()
        for k in range(shards):
            if k + 1 < shards:
                for cp in loads(k + 1):
                    cp.start()
            for cp in loads(k):
                cp.wait()
            o_buf[k % 2] = (p_buf[k % 2].astype(F32) + r_buf[k % 2].astype(F32)).astype(BF16)
            summed = sum_ref.at[targets[k]] if k < nsem else land_ref.at[targets[k]]
            store = pltpu.make_async_copy(o_buf.at[k % 2], summed, sems.at[4])
            store.start()
            store.wait()
            if k < nsem:
                remote[k].start()
        token[...] = jnp.zeros_like(token)

    piece = jax.ShapeDtypeStruct((shards, rows, cols), BF16)
    land = lax.empty(piece.shape, BF16)
    buf = pltpu.VMEM((2, rows, cols), BF16)
    res = pl.pallas_call(
        body, name=name,
        out_shape=(pltpu.SemaphoreType.DMA((nsem,)), pltpu.SemaphoreType.DMA((nsem,)), pltpu.HBM(piece.shape, BF16),
                   pltpu.HBM(piece.shape, BF16), jax.ShapeDtypeStruct((8, LANES), F32)),
        in_specs=(_HBM,) * 3, out_specs=(_SEM, _SEM, _HBM, _HBM, pl.BlockSpec(memory_space=pltpu.VMEM)),
        input_output_aliases={2: 3},
        scratch_shapes=[buf, buf, buf, pltpu.SemaphoreType.DMA((5,))],
        compiler_params=pltpu.CompilerParams(has_side_effects=_SIDE_EFFECT, vmem_limit_bytes=VMEM_LIMIT),
    )(*[pltpu.with_memory_space_constraint(a, pltpu.HBM) for a in (partial, from_sibling, land)])
    return res[:-1], res[-1]


def chip_exchange_wait(name, in_flight, by_target, after):
    send_sems, recv_sems, *thru = in_flight
    n = len(thru) // 2

    def body(*refs):
        srcs, lands, (send, recv) = refs[:n], refs[n:2 * n], refs[2 * n:2 * n + 2]
        for cp in _chip_copies(srcs, lands, send, recv, by_target):
            cp.wait_send()
            cp.wait_recv()

    return pl.pallas_call(
        body, name=name, out_shape=tuple(pltpu.HBM(t.shape, t.dtype) for t in thru),
        in_specs=(_HBM,) * (2 * n) + (_SEM, _SEM, pl.BlockSpec(memory_space=pl.ANY)), out_specs=(_HBM,) * (2 * n),
        input_output_aliases={i: i for i in range(2 * n)},
        compiler_params=pltpu.CompilerParams(has_side_effects=_SIDE_EFFECT),
    )(*thru, send_sems, recv_sems, after)[n:]


def _call_with_comm(body, comm, steps, args, *, name, out_shape, in_specs, out_specs, grid, scratch=(),
                    after_finish=None):
    ni, no, ns, nc = len(in_specs), len(out_specs), len(scratch), len(comm.arrays)

    def full_body(*refs):
        ins, cins = refs[:ni], refs[ni:ni + nc]
        outs, couts = refs[ni + nc:ni + nc + no], refs[ni + nc + no:ni + 2 * nc + no]
        scr, csems = refs[ni + 2 * nc + no:ni + 2 * nc + no + ns], refs[ni + 2 * nc + no + ns:]
        first, middle, last = steps()
        pl.when(first)(lambda: comm.start(cins, couts, csems))
        if comm.has_mid:
            pl.when(middle)(lambda: comm.mid(cins, couts, csems))
        body(*ins, *outs, *scr)

        @pl.when(last)
        def _():
            comm.finish(cins, couts, csems)
            if after_finish is not None:
                after_finish(ins, couts, outs, scr)

    res = pl.pallas_call(
        full_body, name=name, out_shape=list(out_shape) + comm.out_shape, grid=grid,
        in_specs=list(in_specs) + [_ANY] * nc, out_specs=list(out_specs) + [_ANY] * nc,
        scratch_shapes=list(scratch) + comm.scratch,
        compiler_params=pltpu.CompilerParams(dimension_semantics=("arbitrary",) * len(grid),
                                             vmem_limit_bytes=VMEM_LIMIT))(*args, *comm.arrays)
    return res[:no], res[no:]


def _shard_pieces(chip):
    if chip < 3:
        return [(0, SHARD_W, 8 * chip)]
    behind_dt = DT_STORED_START + SSM_HEADS - 3 * SHARD_W
    return [(0, behind_dt, 24), (behind_dt, SHARD_W - behind_dt, behind_dt + 24 + DT_PAD)]


W_IN_COLS = 256


def pack_w_in(wt):
    def body(w_ref, o_ref, pad_ref):
        chip = _chip_index()
        pad_ref[...] = jnp.zeros_like(pad_ref)
        for cv in range(4):
            @pl.when(chip == cv)
            def _():
                for src, n, dst in _shard_pieces(cv):
                    pad_ref[dst:dst + n, :] = w_ref[src:src + n, :]
        o_ref[...] = pad_ref[...].astype(BF16)

    return _call(body, name="pack_w_in", grid=(D_MODEL // W_IN_COLS,),
                 in_specs=[pl.BlockSpec((SHARD_W, W_IN_COLS), lambda i: (0, i))],
                 out_specs=pl.BlockSpec((PACK_W, W_IN_COLS), lambda i: (0, i)),
                 out_shape=jax.ShapeDtypeStruct((PACK_W, D_MODEL), BF16),
                 scratch=[pltpu.VMEM((PACK_W, W_IN_COLS), F32)], sem=("parallel",))(wt)


def _tile_runs():
    runs, fix = [], []
    for t in range(N_ALIGNED // LANES):
        s = min(t // 19, 3)
        j = t - 19 * s
        p = _act_col(t * LANES)
        if runs and runs[-1][1] == s and runs[-1][0] + runs[-1][3] == p and runs[-1][2] + runs[-1][3] == j * LANES:
            runs[-1][3] += LANES
        else:
            runs.append([p, s, j * LANES, LANES])
        if j == 0 and s > 0:
            fix.append((p, s - 1))
    return runs, fix


def unpack_w_in(bg):
    runs, fix = _tile_runs()

    def body(b_ref, o_ref):
        for p, s, j, w in runs:
            o_ref[p:p + w, :] = b_ref[s, j:j + w, :]
        for p, s in fix:
            o_ref[p:p + LANES, :] = o_ref[p:p + LANES, :] + b_ref[s, SHARD_STRIDE:PACK_W, :]
        o_ref[N_ALIGNED:N_ACT, :] = jnp.zeros((N_ACT - N_ALIGNED, W_IN_COLS), BF16)

    return _call(body, name="unpack_w_in", grid=(D_MODEL // W_IN_COLS,),
                 in_specs=[pl.BlockSpec((4, PACK_W, W_IN_COLS), lambda i: (0, 0, i))],
                 out_specs=pl.BlockSpec((N_ACT, W_IN_COLS), lambda i: (0, i)),
                 out_shape=jax.ShapeDtypeStruct((N_ACT, D_MODEL), BF16), sem=("parallel",))(bg)


def _adamw(w, g, m, v):
    m = ADAM_B1 * m + (1.0 - ADAM_B1) * g
    v = ADAM_B2 * v + (1.0 - ADAM_B2) * jnp.square(g)
    m_hat = m / (1.0 - ADAM_B1 ** ADAM_STEP)
    v_hat = v / (1.0 - ADAM_B2 ** ADAM_STEP)
    delta = -ADAM_LR * (m_hat / (jnp.sqrt(v_hat) + ADAM_EPS) + ADAM_WD * w)
    return delta, m, v


def adamw_w_in(g_packed, wt, mt, vt):
    cols = LANES

    def body(g_ref, w_ref, m_ref, v_ref, go_ref, d_ref, mo_ref, vo_ref):
        chip = _chip_index()
        for cv in range(4):
            @pl.when(chip == cv)
            def _():
                for dst, n, src in _shard_pieces(cv):
                    go_ref[dst:dst + n, :] = g_ref[src:src + n, :]
        d_ref[...], mo_ref[...], vo_ref[...] = _adamw(w_ref[...], go_ref[...], m_ref[...], v_ref[...])

    spec = pl.BlockSpec((SHARD_W, cols), lambda i: (0, i))
    shp = jax.ShapeDtypeStruct((SHARD_W, D_MODEL), F32)
    return _call(body, name="adamw_w_in", grid=(D_MODEL // cols,),
                 in_specs=[pl.BlockSpec((PACK_W, cols), lambda i: (0, i)), spec, spec, spec],
                 out_specs=[spec] * 4, out_shape=[shp] * 4, sem=("parallel",))(g_packed, wt, mt, vt)


def adamw_rows(name, g, w, m, v):
    r, c = g.shape
    rows = min(r, BLOCK)

    def body(g_ref, w_ref, m_ref, v_ref, d_ref, mo_ref, vo_ref):
        d_ref[...], mo_ref[...], vo_ref[...] = _adamw(w_ref[...], g_ref[...], m_ref[...], v_ref[...])

    spec = pl.BlockSpec((rows, c), lambda i: (i, 0))
    shp = jax.ShapeDtypeStruct((r, c), F32)
    return _call(body, name=name, grid=(r // rows,), in_specs=[spec] * 4, out_specs=[spec] * 3, out_shape=[shp] * 3,
                 sem=("parallel",))(g, w, m, v)


def adamw_small(gs, ws, ms, vs):
    n = len(gs)

    def body(*refs):
        g, w, m, v = refs[:n], refs[n:2 * n], refs[2 * n:3 * n], refs[3 * n:4 * n]
        outs = refs[4 * n:]
        for i in range(n):
            d, mn, vn = _adamw(w[i][...], g[i][...], m[i][...], v[i][...])
            outs[3 * i][...] = d
            outs[3 * i + 1][...] = mn
            outs[3 * i + 2][...] = vn

    specs = [_full(a.shape) for a in gs]
    res = _call(body, name="adamw_small", in_specs=specs * 4,
                out_specs=[s for s in specs for _ in range(3)],
                out_shape=[jax.ShapeDtypeStruct(a.shape, F32) for a in gs for _ in range(3)])(*gs, *ws, *ms, *vs)
    return [tuple(res[3 * i:3 * i + 3]) for i in range(n)]


def sum_slots(name, r, after):
    s, rr, c = r.shape
    rows = min(rr, BLOCK)

    def body(r_ref, after_ref, o_ref):
        acc = r_ref[0].astype(F32)
        for k in range(1, s):
            acc = acc + r_ref[k].astype(F32)
        o_ref[...] = acc

    return _call(body, name=name, grid=(rr // rows,),
                 in_specs=[pl.BlockSpec((s, rows, c), lambda i: (0, i, 0)), _full(after.shape)],
                 out_specs=pl.BlockSpec((rows, c), lambda i: (i, 0)), out_shape=jax.ShapeDtypeStruct((rr, c), F32),
                 sem=("parallel",))(r, after)


GRAD_COLS = 256


def project(name, x, wt, after):
    m, k = x.shape
    n = wt.shape[0]
    tn = D_MODEL

    def body(x_ref, w_ref, after_ref, o_ref):
        o_ref[...] = lax.dot_general(x_ref[...], w_ref[...], _NT, preferred_element_type=F32)

    return _call(body, name=name, grid=(n // tn,),
                 in_specs=[_full((m, k)), pl.BlockSpec((tn, k), lambda j: (j, 0)), _full(after.shape)],
                 out_specs=pl.BlockSpec((m, tn), lambda j: (0, j)), out_shape=jax.ShapeDtypeStruct((m, n), F32),
                 sem=("parallel",))(x, wt, after)


def _piece_tiles(pieces, width):
    spans, start = [], 0
    for p in pieces:
        spans.append((start, p.shape[1] // width))
        start += p.shape[1] // width
    return spans, start


def _piece_spec(block, span, rows_of, tile_of):
    first, count = span

    def index(*pos):
        t = tile_of(*pos) - first
        mine = (t >= 0) & (t < count)
        return jnp.where(mine, rows_of(*pos), 0), jnp.clip(t, 0, count - 1)

    return pl.BlockSpec(block, index)


def project_back(name, pieces, wt, after):
    m = pieces[0].shape[0]
    k = wt.shape[1]
    tm = m // 2
    spans, steps = _piece_tiles(pieces, D_MODEL)

    def body(*refs):
        w_ref, o_ref = refs[len(pieces)], refs[len(pieces) + 2]
        j = pl.program_id(1)

        @pl.when(j == 0)
        def _():
            o_ref[...] = jnp.zeros_like(o_ref)

        for dy_ref, (first, count) in zip(refs, spans):
            @pl.when((j >= first) & (j < first + count))
            def _():
                o_ref[...] += jnp.dot(dy_ref[...], w_ref[...], preferred_element_type=F32)

    return _call(body, name=name, grid=(m // tm, steps),
                 in_specs=[_piece_spec((tm, D_MODEL), s, lambda i, j: i, lambda i, j: j) for s in spans]
                 + [pl.BlockSpec((D_MODEL, k), lambda i, j: (j, 0)), _full(after.shape)],
                 out_specs=pl.BlockSpec((tm, k), lambda i, j: (i, 0)), out_shape=jax.ShapeDtypeStruct((m, k), F32),
                 sem=("parallel", "arbitrary"))(*pieces, wt, after)


GRAD_TILE = 512


def _slab_runs():
    runs = [[] for _ in range(N_ACT // GRAD_TILE)]
    for s in range(4):
        for j in range(PACK_W // LANES):
            tile, r = divmod(_act_col((19 * s + j) * LANES), GRAD_TILE)
            last = runs[tile][-1] if runs[tile] else None
            if last and last[2] == s and last[0] + last[1] == r and last[3] + last[1] == j * LANES:
                last[1] += LANES
            else:
                runs[tile].append([r, LANES, s, j * LANES])
    return runs


def weight_grad_t(name, pieces, order, x):
    m, k = x.shape
    spans, steps = _piece_tiles(pieces, GRAD_TILE)
    runs = _slab_runs()
    most = max(len(r) for r in runs)
    half_rows = PACK_W // 2
    when, tile_at = [None] * len(pieces), []
    for p in order:
        when[p] = (len(tile_at), spans[p][1])
        tile_at += range(spans[p][0], spans[p][0] + spans[p][1])

    landed = {}
    for step in range(steps):
        for _, _, s, _ in runs[tile_at[step]]:
            landed[s] = step + 2

    def body(*refs):
        x_ref, o_ref, land_ref, tile_ref, sems, send_sems, recv_sems = refs[len(pieces):]
        i = pl.program_id(0)
        other = 1 - lax.axis_index("c")

        def copies(step):
            return [pltpu.make_async_copy(tile_ref.at[step % 2, r:r + n], o_ref.at[s, d:d + n],
                                          sems.at[(step % 2) * most + c])
                    for c, (r, n, s, d) in enumerate(runs[tile_at[step]])]

        def to_sibling(s):
            return _remote(o_ref.at[s, pl.ds(other * half_rows, half_rows)], land_ref.at[s], send_sems.at[s],
                           recv_sems.at[s], _peer(1))

        for step in range(2, steps):
            @pl.when(i == step)
            def _():
                for cp in copies(step - 2):
                    cp.wait()

        for dy_ref, (first, count) in zip(refs, when):
            @pl.when((i >= first) & (i < first + count))
            def _():
                tile_ref[i % 2] = lax.dot_general(dy_ref[...], x_ref[...], (((0,), (0,)), ((), ())),
                                                  preferred_element_type=F32).astype(BF16)

        for step in range(steps):
            @pl.when(i == step)
            def _():
                for cp in copies(step):
                    cp.start()
                if step < steps - 1:
                    for s in range(4):
                        if landed[s] == step:
                            to_sibling(s).start()
                else:
                    for cp in copies(step - 1) + copies(step):
                        cp.wait()
                    for s in range(4):
                        if landed[s] >= steps - 1:
                            to_sibling(s).start()
                    for s in range(4):
                        to_sibling(s).wait()

    return _call(body, name=name, grid=(steps,),
                 in_specs=[_piece_spec((m, GRAD_TILE), s, lambda i: 0, lambda i: i) for s in when]
                 + [pl.BlockSpec(memory_space=pltpu.VMEM)],
                 out_specs=[_ANY, _ANY],
                 out_shape=[jax.ShapeDtypeStruct((4, PACK_W, k), BF16), jax.ShapeDtypeStruct((4, half_rows, k), BF16)],
                 scratch=[pltpu.VMEM((2, GRAD_TILE, k), BF16), pltpu.SemaphoreType.DMA((2 * most,)),
                          pltpu.SemaphoreType.DMA((4,)), pltpu.SemaphoreType.DMA((4,))],
                 sem=("arbitrary",))(*pieces, x)


def mm_tn(name, x, dy):
    m, k = x.shape
    n = dy.shape[1]
    tn = min(n, GRAD_COLS)

    def body(x_ref, dy_ref, o_ref):
        o_ref[...] = lax.dot_general(x_ref[...], dy_ref[...], (((0,), (0,)), ((), ())),
                                     preferred_element_type=F32).astype(BF16)

    return _call(body, name=name, grid=(n // tn,),
                 in_specs=[_full((m, k)), pl.BlockSpec((m, tn), lambda i: (0, i))],
                 out_specs=pl.BlockSpec((k, tn), lambda i: (0, i)), out_shape=jax.ShapeDtypeStruct((k, n), BF16),
                 sem=("parallel",))(x, dy)


def _row_spec(width, col_block=0):
    return pl.BlockSpec((BLOCK, width), lambda i: (i, col_block))


def _x_spec():
    return pl.BlockSpec((None, BLOCK, D_MODEL), lambda i: (0, jnp.maximum(i - 1, 0), 0))


def prep(name, x, g_pre, gather):
    nb = x.shape[1] // BLOCK + 1
    shards, _, shard_cols = gather.out_shape[-1].shape

    def body(x_ref, g_ref, h_ref, u_ref, meta_ref, sems):
        @pl.when(pl.program_id(0) < nb - 1)
        def _():
            h_ref[...] = x_ref[...]
            u_ref[...] = _rms(x_ref[...], g_ref[...]).astype(BF16)

    def first_block(ins, gathered, outs, scratch):
        g_ref, (h_ref, u_ref), (meta_ref, sems) = ins[1], outs, scratch
        copies = [pltpu.make_async_copy(gathered[-1].at[t], meta_ref.at[t], sems.at[t]) for t in range(shards)]
        for cp in copies:
            cp.start()
        for cp in copies:
            cp.wait()
        h_ref[0:PAD_ROWS, :] = jnp.zeros((PAD_ROWS, D_MODEL), F32)
        h_ref[PAD_ROWS:BLOCK, :] = jnp.concatenate([meta_ref[t] for t in range(shards)], axis=1)
        u_ref[...] = _rms(h_ref[...], g_ref[...]).astype(BF16)

    def at():
        i = pl.program_id(0)
        return i == 0, i == nb // 2, i == nb - 1

    rows = pl.BlockSpec((BLOCK, D_MODEL), lambda i: ((i + 1) % nb, 0))
    return _call_with_comm(
        body, gather, at, (x, g_pre), name=name, grid=(nb,),
        in_specs=[pl.BlockSpec((None, BLOCK, D_MODEL), lambda i: (0, jnp.minimum(i, nb - 2), 0)), _full((1, D_MODEL))],
        out_specs=[rows, rows],
        out_shape=[jax.ShapeDtypeStruct((nb * BLOCK, D_MODEL), F32), jax.ShapeDtypeStruct((nb * BLOCK, D_MODEL), BF16)],
        scratch=[pltpu.VMEM((shards, N_META, shard_cols), F32), pltpu.SemaphoreType.DMA((shards,))],
        after_finish=first_block)


def prep_bwd(h, du, dres, g_pre):
    seq = h.shape[0] - BLOCK
    rows = min(seq, 4 * BLOCK)

    def body(h0_ref, du0_ref, h_ref, du_ref, dres_ref, g_ref, gx_ref, gm_ref, gg_ref):
        i = pl.program_id(0)
        _, vjp = jax.vjp(_rms, h_ref[...], g_ref[...])
        dh, dg = vjp(du_ref[...])
        gx_ref[...] = dh + dres_ref[...]

        @pl.when(i == 0)
        def _():
            _, vjp0 = jax.vjp(_rms, h0_ref[...], g_ref[...])
            dh0, dg0 = vjp0(du0_ref[...])
            gm_ref[...] = dh0[PAD_ROWS:BLOCK, :]
            gg_ref[...] = dg0 + dg

        @pl.when(i > 0)
        def _():
            gg_ref[...] += dg

    seq_rows = pl.BlockSpec((pl.Element(rows), pl.Element(D_MODEL)), lambda i: (pl.multiple_of(BLOCK + rows * i, BLOCK), 0))
    first = pl.BlockSpec((BLOCK, D_MODEL), lambda i: (0, 0))
    return _call(body, name="prep_bwd", grid=(seq // rows,),
                 in_specs=[first, first, seq_rows, seq_rows, seq_rows, _full((1, D_MODEL))],
                 out_specs=[pl.BlockSpec((None, rows, D_MODEL), lambda i: (0, i, 0)), _full((N_META, D_MODEL)),
                            _full((1, D_MODEL))],
                 out_shape=[jax.ShapeDtypeStruct((1, seq, D_MODEL), F32),
                            jax.ShapeDtypeStruct((N_META, D_MODEL), F32), jax.ShapeDtypeStruct((1, D_MODEL), F32)],
                 sem=("arbitrary",))(h, du, h, du, dres, g_pre)


GROUP_W = SSM_INNER // SSM_GROUPS


def _gated_norm(y, z, g):
    t = y * _silu(z)
    return t * lax.rsqrt(jnp.mean(t * t, axis=-1, keepdims=True) + NORM_EPS) * g


def _gated_norm_groups(y, z, g):
    groups = [slice(k * GROUP_W, (k + 1) * GROUP_W) for k in range(SSM_GROUPS)]
    return jnp.concatenate([_gated_norm(y[:, s], z[:, s], g[:, s]) for s in groups], axis=1)


def _merge(ga, gs, ya, ys):
    return _sigmoid(ga) * ya + _sigmoid(gs) * ys


GATE_ATT_BLOCK = SEG["gate_att"][2] // D_MODEL
GATE_SSM_BLOCK = SEG["gate_ssm"][2] // D_MODEL


def _row_loss(out, g_post, x, target):
    diff = x + _rms(out, g_post) - target
    return 0.5 * jnp.sum(diff * diff) / D_MODEL


def tail(y_ssd, proj, a_att, x, target, woa, wos, wo, g_norm, g_post):
    nb = y_ssd.shape[0] // BLOCK
    rows = nb * BLOCK

    def body(y_ref, z_ref, ga_ref, gs_ref, a_ref, x_ref, t_ref, woa_ref, wos_ref, wo_ref, gn_ref, gp_ref,
             yn_ref, mg_ref, dout_ref, dya_ref, dys_ref, da_ref, dy_ref, dz_ref, dga_ref, dgs_ref, dres_ref,
             loss_ref, dgp_ref, dgn_ref):
        i = pl.program_id(0)
        yn, norm_vjp = jax.vjp(_gated_norm_groups, y_ref[...], z_ref[...], gn_ref[...])
        yn16 = yn.astype(BF16)
        y_ssm = jnp.dot(yn16, wos_ref[...], preferred_element_type=F32)
        y_att = jnp.dot(a_ref[...], woa_ref[...], preferred_element_type=F32)
        merged, merge_vjp = jax.vjp(_merge, ga_ref[...], gs_ref[...], y_att, y_ssm)
        merged16 = merged.astype(BF16)
        out = jnp.dot(merged16, wo_ref[...], preferred_element_type=F32)
        loss, loss_vjp = jax.vjp(_row_loss, out, gp_ref[...], x_ref[...], t_ref[...])
        counted = jnp.where(i > 0, 1.0, 0.0)
        dout, dgp, dres, _ = loss_vjp(counted)
        dout16 = dout.astype(BF16)
        dmerged = lax.dot_general(dout16, wo_ref[...], _NT, preferred_element_type=F32)
        dga, dgs, dya, dys = merge_vjp(dmerged)
        dya16, dys16 = dya.astype(BF16), dys.astype(BF16)
        dyn = lax.dot_general(dys16, wos_ref[...], _NT, preferred_element_type=F32)
        dy, dz, dgn = norm_vjp(dyn)

        yn_ref[...] = yn16
        mg_ref[...] = merged16
        dout_ref[...] = dout16
        dya_ref[...] = dya16
        dys_ref[...] = dys16
        da_ref[...] = lax.dot_general(dya16, woa_ref[...], _NT, preferred_element_type=F32)
        dy_ref[...] = dy
        dz_ref[...] = dz.astype(BF16)
        dga_ref[...] = dga.astype(BF16)
        dgs_ref[...] = dgs.astype(BF16)
        dres_ref[...] = dres

        @pl.when(i == 0)
        def _():
            loss_ref[...] = jnp.zeros_like(loss_ref)
            dgp_ref[...] = jnp.zeros_like(dgp_ref)
            dgn_ref[...] = jnp.zeros_like(dgn_ref)

        loss_ref[...] += loss * counted
        dgp_ref[...] += dgp
        dgn_ref[...] += dgn

    wide, narrow = _row_spec(SSM_INNER), _row_spec(D_MODEL)
    resident = pl.BlockSpec(memory_space=pltpu.VMEM)
    bf = lambda w: jax.ShapeDtypeStruct((rows, w), BF16)
    f32 = lambda w: jax.ShapeDtypeStruct((rows, w), F32)
    return _call(body, name="tail", grid=(nb,),
                 in_specs=[wide, wide, _row_spec(D_MODEL, GATE_ATT_BLOCK), _row_spec(D_MODEL, GATE_SSM_BLOCK), narrow,
                           _x_spec(), _x_spec(), resident, resident, resident, _full((1, SSM_INNER)),
                           _full((1, D_MODEL))],
                 out_specs=[wide, narrow, narrow, narrow, narrow, narrow, wide, wide, narrow, narrow, narrow,
                            _full((8, LANES)), _full((1, D_MODEL)), _full((1, SSM_INNER))],
                 out_shape=[bf(SSM_INNER), bf(D_MODEL), bf(D_MODEL), bf(D_MODEL), bf(D_MODEL), f32(D_MODEL),
                            f32(SSM_INNER), bf(SSM_INNER), bf(D_MODEL), bf(D_MODEL), f32(D_MODEL),
                            jax.ShapeDtypeStruct((8, LANES), F32), jax.ShapeDtypeStruct((1, D_MODEL), F32),
                            jax.ShapeDtypeStruct((1, SSM_INNER), F32)],
                 sem=("arbitrary",))(y_ssd, proj, proj, proj, a_att, x, target, woa, wos, wo, g_norm, g_post)


_NT = (((1,), (1,)), ((), ()))
ALIBI_SLOPES = tuple(2.0 ** (-8.0 * (h + 1) / ATT_Q_HEADS) for h in range(ATT_Q_HEADS))
KV_WIDTH = ATT_KV_HEADS * HEAD_DIM
Q_BLOCK = SEG["q"][2] // D_MODEL
Z_ATT_BLOCK = SEG["z_att"][2] // D_MODEL
K_BLOCK = SEG["k"][2] // KV_WIDTH
V_BLOCK = SEG["v"][2] // KV_WIDTH
META_ROW_BLOCK = PAD_ROWS // N_META


@jax.custom_vjp
def _swap_halves(x):
    return pltpu.roll(x, HEAD_DIM, 1)


_swap_halves.defvjp(lambda x: (pltpu.roll(x, HEAD_DIM, 1), None), lambda _, g: (pltpu.roll(g, HEAD_DIM, 1),))


def _both_halves(t, half):
    first = lax.broadcasted_iota(jnp.int32, t.shape, 1) < HEAD_DIM
    sw = _swap_halves(t)
    return jnp.where(first, t, sw) if half == 0 else jnp.where(first, sw, t)


def _attn_rows(q, z, kp, kc, vp, vc, km, vm, sinks, n):
    rows = ATT_GROUP * BLOCK
    i = lax.broadcasted_iota(jnp.int32, (rows, BLOCK), 0) & (BLOCK - 1)
    j = lax.broadcasted_iota(jnp.int32, (rows, BLOCK), 1)
    rel_c = (i - j).astype(F32)
    rel_p = rel_c + float(BLOCK)
    nv = jnp.zeros((rows, BLOCK), jnp.int32) + n
    ok_c = (i >= j) & (nv >= 1)
    ok_p = (j > i) & (nv >= 2)
    im = lax.broadcasted_iota(jnp.int32, (rows, N_META), 0) & (BLOCK - 1)
    jm = lax.broadcasted_iota(jnp.int32, (rows, N_META), 1)
    ok_m = ((jnp.zeros((rows, N_META), jnp.int32) + n) >= 1) | (im >= PAD_ROWS + jm)
    first = lax.broadcasted_iota(jnp.int32, (BLOCK, LANES), 1) < HEAD_DIM
    neg = -jnp.inf
    outs = []
    for kv in range(ATT_KV_HEADS):
        tile, half = divmod(kv, 2)
        lanes = slice(tile * LANES, (tile + 1) * LANES)
        kc2, kp2, km2 = (_both_halves(t[:, lanes], half).astype(BF16) for t in (kc, kp, km))
        vc2, vp2, vm2 = (_both_halves(t[:, lanes], half).astype(BF16) for t in (vc, vp, vm))
        qs, slope, sk = [], [], []
        for pair in range(ATT_GROUP // 2):
            c0 = (kv * ATT_GROUP + 2 * pair) * HEAD_DIM
            qp = q[:, c0:c0 + LANES] * HEAD_DIM ** -0.5
            qs += [jnp.where(first, qp, 0.0), jnp.where(first, 0.0, qp)]
        for g in range(ATT_GROUP):
            slope.append(jnp.full((BLOCK, 1), ALIBI_SLOPES[kv * ATT_GROUP + g], F32))
            sk.append(jnp.broadcast_to(sinks[kv * ATT_GROUP + g], (BLOCK, 1)))
        qs = jnp.concatenate(qs, axis=0).astype(BF16)
        slope = jnp.concatenate(slope, axis=0)
        sk = jnp.concatenate(sk, axis=0)
        sc = jnp.where(ok_c, lax.dot_general(qs, kc2, _NT, preferred_element_type=F32) - slope * rel_c, neg)
        sp = jnp.where(ok_p, lax.dot_general(qs, kp2, _NT, preferred_element_type=F32) - slope * rel_p, neg)
        sm = jnp.where(ok_m, lax.dot_general(qs, km2, _NT, preferred_element_type=F32), neg)
        mx = jnp.maximum(jnp.maximum(jnp.max(sc, axis=1, keepdims=True), jnp.max(sp, axis=1, keepdims=True)),
                         jnp.maximum(jnp.max(sm, axis=1, keepdims=True), sk))
        mx = lax.stop_gradient(mx)
        ec, ep, em, es = jnp.exp(sc - mx), jnp.exp(sp - mx), jnp.exp(sm - mx), jnp.exp(sk - mx)
        den = (es + jnp.sum(ec, axis=1, keepdims=True) + jnp.sum(ep, axis=1, keepdims=True)
               + jnp.sum(em, axis=1, keepdims=True))
        inv = 1.0 / den
        o = (jnp.dot((ec * inv).astype(BF16), vc2, preferred_element_type=F32)
             + jnp.dot((ep * inv).astype(BF16), vp2, preferred_element_type=F32)
             + jnp.dot((em * inv).astype(BF16), vm2, preferred_element_type=F32))
        for pair in range(ATT_GROUP // 2):
            r0 = 2 * pair * BLOCK
            outs.append(jnp.where(first, o[r0:r0 + BLOCK], o[r0 + BLOCK:r0 + 2 * BLOCK]))
    return jnp.concatenate(outs, axis=1) * _silu(z)


def _attn_specs(nb, steps_clamped):
    def blk(t):
        return jnp.minimum(t, nb - 1) if steps_clamped else t

    wide = lambda col: pl.BlockSpec((BLOCK, D_MODEL), lambda t: (blk(t), col))
    cur = lambda col: pl.BlockSpec((BLOCK, KV_WIDTH), lambda t: (blk(t), col))
    prev = lambda col: pl.BlockSpec((BLOCK, KV_WIDTH), lambda t: (jnp.maximum(blk(t) - 1, 0), col))
    meta = lambda col: pl.BlockSpec((N_META, KV_WIDTH), lambda t: (META_ROW_BLOCK, col))
    sinks = pl.BlockSpec((ATT_Q_HEADS, 1, 1), lambda t: (0, 0, 0))
    return [wide(Q_BLOCK), wide(Z_ATT_BLOCK), prev(K_BLOCK), cur(K_BLOCK), prev(V_BLOCK), cur(V_BLOCK),
            meta(K_BLOCK), meta(V_BLOCK), sinks]


def attn_fwd(proj, sinks):
    nb = proj.shape[0] // BLOCK

    def body(q_ref, z_ref, kp_ref, kc_ref, vp_ref, vc_ref, km_ref, vm_ref, sk_ref, o_ref):
        o_ref[...] = _attn_rows(q_ref[...], z_ref[...], kp_ref[...], kc_ref[...], vp_ref[...], vc_ref[...],
                                km_ref[...], vm_ref[...], tuple(sk_ref[h] for h in range(ATT_Q_HEADS)),
                                pl.program_id(0)).astype(BF16)

    return _call(body, name="attn_fwd", grid=(nb,), in_specs=_attn_specs(nb, False), out_specs=_row_spec(D_MODEL),
                 out_shape=jax.ShapeDtypeStruct((nb * BLOCK, D_MODEL), BF16), sem=("parallel",))(*([proj] * 8), sinks)


def attn_bwd(da, proj, sinks):
    nb = proj.shape[0] // BLOCK
    last = nb - 1
    wide = pl.BlockSpec((BLOCK, D_MODEL), lambda t: (jnp.minimum(t, last), 0))
    done = pl.BlockSpec((BLOCK, KV_WIDTH), lambda t: (jnp.maximum(t - 1, 0), 0))
    meta = _full((N_META, KV_WIDTH))
    par = _full((ATT_Q_HEADS, 1, 1))

    def body(da_ref, q_ref, z_ref, kp_ref, kc_ref, vp_ref, vc_ref, km_ref, vm_ref, sk_ref,
             dq_ref, dz_ref, dk_ref, dv_ref, dkm_ref, dvm_ref, dsk_ref, ck_ref, cv_ref):
        t = pl.program_id(0)

        @pl.when(t == 0)
        def _():
            ck_ref[...] = jnp.zeros_like(ck_ref)
            cv_ref[...] = jnp.zeros_like(cv_ref)
            dkm_ref[...] = jnp.zeros_like(dkm_ref)
            dvm_ref[...] = jnp.zeros_like(dvm_ref)
            dsk_ref[...] = jnp.zeros_like(dsk_ref)

        @pl.when(t < nb)
        def _():
            def f(q, z, kp, kc, vp, vc, km, vm, sk):
                return _attn_rows(q, z, kp, kc, vp, vc, km, vm, sk, t)

            _, vjp = jax.vjp(f, q_ref[...], z_ref[...], kp_ref[...], kc_ref[...], vp_ref[...], vc_ref[...],
                             km_ref[...], vm_ref[...], tuple(sk_ref[h] for h in range(ATT_Q_HEADS)))
            dq, dz, dkp, dkc, dvp, dvc, dkm, dvm, dsk = vjp(da_ref[...])
            dq_ref[...] = dq.astype(BF16)
            dz_ref[...] = dz.astype(BF16)
            for h in range(ATT_Q_HEADS):
                dsk_ref[h] += dsk[h]
            dk_ref[...] = ck_ref[...] + dkp
            dv_ref[...] = cv_ref[...] + dvp
            ck_ref[...] = dkc
            cv_ref[...] = dvc
            dkm_ref[...] += dkm
            dvm_ref[...] += dvm

        @pl.when(t == nb)
        def _():
            dk_ref[...] = ck_ref[...]
            dv_ref[...] = cv_ref[...]

    rows = nb * BLOCK
    return _call(body, name="attn_bwd", grid=(nb + 1,), in_specs=[wide] + _attn_specs(nb, True),
                 out_specs=[wide, wide, done, done, meta, meta, par],
                 out_shape=[jax.ShapeDtypeStruct((rows, D_MODEL), BF16), jax.ShapeDtypeStruct((rows, D_MODEL), BF16),
                            jax.ShapeDtypeStruct((rows, KV_WIDTH), F32), jax.ShapeDtypeStruct((rows, KV_WIDTH), F32),
                            jax.ShapeDtypeStruct((N_META, KV_WIDTH), F32), jax.ShapeDtypeStruct((N_META, KV_WIDTH), F32),
                            jax.ShapeDtypeStruct(sinks.shape, F32)],
                 scratch=[pltpu.VMEM((BLOCK, KV_WIDTH), F32), pltpu.VMEM((BLOCK, KV_WIDTH), F32)],
                 sem=("arbitrary",))(da, *([proj] * 8), sinks)


XBC_BLOCK0 = SEG["xbc"][2] // D_MODEL
CONV_COL_BLOCKS = CONV_DIM // D_MODEL
DT_TILE = SEG["dt"][2] // LANES


HALO = 8


def _conv_rows(length):
    return 544 if length % 544 == 0 else BLOCK


def _shift_rows(cur, before, j):
    if j == 0:
        return cur
    n = cur.shape[0]
    row = lax.broadcasted_iota(jnp.int32, cur.shape, 0)
    head = pltpu.roll(before, j, 0)
    if n > HALO:
        head = jnp.concatenate([head, jnp.zeros((n - HALO, cur.shape[1]), cur.dtype)], axis=0)
    return jnp.where(row >= j, pltpu.roll(cur, j, 0), head)


def _conv_pre(cur, before, w_ref, b_ref):
    pre = b_ref[...] + w_ref[CONV_WIDTH - 1:CONV_WIDTH, :] * cur
    for k in range(CONV_WIDTH - 1):
        pre = pre + w_ref[k:k + 1, :] * _shift_rows(cur, before, CONV_WIDTH - 1 - k)
    return pre


def _conv_specs(steps, rows, col0=0):
    first = XBC_BLOCK0 + col0
    halos = rows // HALO
    cur = pl.BlockSpec((rows, D_MODEL), lambda j, i: (i, first + j))
    before = pl.BlockSpec((HALO, D_MODEL), lambda j, i: (jnp.maximum(i * halos - 1, 0), first + j))
    after = pl.BlockSpec((HALO, D_MODEL), lambda j, i: (jnp.minimum(i + 1, steps - 1) * halos, first + j))
    return cur, before, after


def _valid_rows(i, rows):
    row = lax.broadcasted_iota(jnp.int32, (rows, D_MODEL), 0)
    return jnp.maximum((row >= PAD_ROWS).astype(F32), jnp.where(i > 0, 1.0, 0.0))


def conv_fwd(proj, conv_w, conv_b):
    rows = _conv_rows(proj.shape[0])
    steps = proj.shape[0] // rows
    cur, before, _ = _conv_specs(steps, rows)

    def body(c_ref, p_ref, w_ref, b_ref, o_ref):
        i = pl.program_id(1)
        pre = _conv_pre(c_ref[...], p_ref[...] * jnp.where(i > 0, 1.0, 0.0), w_ref, b_ref)
        o_ref[...] = _silu(pre) * _valid_rows(i, rows)

    return _call(body, name="conv_fwd", grid=(CONV_COL_BLOCKS, steps),
                 in_specs=[cur, before, pl.BlockSpec((CONV_WIDTH, D_MODEL), lambda j, i: (0, j)),
                           pl.BlockSpec((1, D_MODEL), lambda j, i: (0, j))],
                 out_specs=pl.BlockSpec((rows, D_MODEL), lambda j, i: (i, j)),
                 out_shape=jax.ShapeDtypeStruct((proj.shape[0], CONV_DIM), F32),
                 sem=("parallel", "parallel"))(proj, proj, conv_w, conv_b)


def conv_bwd(name, dparts, col0, proj, conv_w, conv_b):
    rows = _conv_rows(proj.shape[0])
    steps = proj.shape[0] // rows
    last = steps - 1
    ncol = sum(d.shape[1] for d in dparts) // D_MODEL
    np_ = len(dparts)
    cur, before, after = _conv_specs(steps, rows, col0)
    dcur = [pl.BlockSpec((rows, d.shape[1] // ncol), lambda j, i: (i, j)) for d in dparts]
    dafter = [pl.BlockSpec((HALO, d.shape[1] // ncol), lambda j, i: (jnp.minimum(i + 1, last) * (rows // HALO), j))
              for d in dparts]
    out_cur = pl.BlockSpec((rows, D_MODEL), lambda j, i: (i, j))
    wspec = pl.BlockSpec((CONV_WIDTH, D_MODEL), lambda j, i: (0, col0 + j))
    bspec = pl.BlockSpec((1, D_MODEL), lambda j, i: (0, col0 + j))
    wout = pl.BlockSpec((CONV_WIDTH, D_MODEL), lambda j, i: (0, j))
    bout = pl.BlockSpec((1, D_MODEL), lambda j, i: (0, j))

    def body(*refs):
        dc_refs, da_refs = refs[:np_], refs[np_:2 * np_]
        c_ref, p_ref, a_ref, w_ref, b_ref, du_ref, dw_ref, db_ref = refs[2 * np_:]
        i = pl.program_id(1)
        row = lax.broadcasted_iota(jnp.int32, (rows, D_MODEL), 0)
        curv = c_ref[...]
        beforev = p_ref[...] * jnp.where(i > 0, 1.0, 0.0)
        side_by_side = lambda rs: rs[0][...] if np_ == 1 else jnp.concatenate([r[...] for r in rs], axis=1)

        def dpre_of(pre, d):
            s = _sigmoid(pre)
            return d * (s * (1.0 + pre * (1.0 - s)))

        dp_c = dpre_of(_conv_pre(curv, beforev, w_ref, b_ref), side_by_side(dc_refs) * _valid_rows(i, rows))
        dp_a = dpre_of(_conv_pre(a_ref[...], curv[rows - HALO:], w_ref, b_ref),
                       side_by_side(da_refs) * jnp.where(i < last, 1.0, 0.0))
        du = w_ref[CONV_WIDTH - 1:CONV_WIDTH, :] * dp_c
        for j in range(1, CONV_WIDTH):
            tail = jnp.concatenate([jnp.zeros((rows - HALO, D_MODEL), F32), pltpu.roll(dp_a, HALO - j, 0)], axis=0)
            up = jnp.where(row < rows - j, pltpu.roll(dp_c, rows - j, 0), tail)
            du = du + w_ref[CONV_WIDTH - 1 - j:CONV_WIDTH - j, :] * up
        du_ref[...] = du.astype(BF16)

        @pl.when(i == 0)
        def _():
            dw_ref[...] = jnp.zeros_like(dw_ref)
            db_ref[...] = jnp.zeros_like(db_ref)

        for k in range(CONV_WIDTH):
            dw_ref[k:k + 1, :] += jnp.sum(dp_c * _shift_rows(curv, beforev, CONV_WIDTH - 1 - k), axis=0, keepdims=True)
        db_ref[...] += jnp.sum(dp_c, axis=0, keepdims=True)

    width = ncol * D_MODEL
    return _call(body, name=name, grid=(ncol, steps),
                 in_specs=dcur + dafter + [cur, before, after, wspec, bspec], out_specs=[out_cur, wout, bout],
                 out_shape=[jax.ShapeDtypeStruct((proj.shape[0], width), BF16),
                            jax.ShapeDtypeStruct((CONV_WIDTH, width), F32), jax.ShapeDtypeStruct((1, width), F32)],
                 sem=("parallel", "arbitrary"))(*dparts, *dparts, proj, proj, proj, conv_w, conv_b)


def _head_expand():
    e = np.zeros((LANES, SSM_INNER), np.float32)
    for h in range(SSM_HEADS):
        e[h, h * HEAD_DIM:(h + 1) * HEAD_DIM] = 1.0
    return jnp.asarray(e, dtype=BF16)


def _softplus(x):
    return jnp.maximum(x, 0.0) + jnp.log(1.0 + jnp.exp(-jnp.abs(x)))


def _bf16_parts(x):
    hi = x.astype(BF16)
    rest = x - hi.astype(F32)
    mid = rest.astype(BF16)
    return hi, mid, (rest - mid.astype(F32)).astype(BF16)


@jax.custom_vjp
def _times_01(x, m):
    return sum(jnp.dot(p, m, preferred_element_type=F32) for p in _bf16_parts(x))


def _times_01_bwd(m, g):
    return sum(lax.dot_general(p, m, _NT, preferred_element_type=F32) for p in _bf16_parts(g)), jnp.zeros_like(m)


_times_01.defvjp(lambda x, m: (_times_01(x, m), m), _times_01_bwd)


def _causal_ones():
    l = lax.broadcasted_iota(jnp.int32, (BLOCK, BLOCK), 0)
    s = lax.broadcasted_iota(jnp.int32, (BLOCK, BLOCK), 1)
    return (l >= s).astype(BF16)


@jax.custom_vjp
def _cumsum_rows(a):
    return sum(jnp.dot(_causal_ones(), p, preferred_element_type=F32) for p in _bf16_parts(a))


def _cumsum_rows_bwd(_, g):
    tn = (((0,), (0,)), ((), ()))
    return (sum(lax.dot_general(_causal_ones(), p, tn, preferred_element_type=F32) for p in _bf16_parts(g)),)


_cumsum_rows.defvjp(lambda a: (_cumsum_rows(a), None), _cumsum_rows_bwd)


def _ssd_heads(dt_tile, bias, alog, dsk, expand):
    dt = _softplus(dt_tile + bias)
    a = dt * (-jnp.exp(alog))
    one_row = lambda v: jnp.broadcast_to(v, (HALO, LANES))
    return _times_01(jnp.concatenate([dt, _cumsum_rows(a), one_row(jnp.sum(a, axis=0, keepdims=True)), one_row(dsk)],
                                     axis=0), expand)


HEADS_ROWS = 2 * BLOCK + 2 * HALO


def _ssd_group(xs, per_lane, bg, cg, state):
    l = lax.broadcasted_iota(jnp.int32, (BLOCK, BLOCK), 0)
    s = lax.broadcasted_iota(jnp.int32, (BLOCK, BLOCK), 1)
    causal = l >= s
    first_head = s < HEAD_DIM
    dtx, cs = per_lane[0:BLOCK], per_lane[BLOCK:2 * BLOCK]
    tot, dsk = per_lane[2 * BLOCK:2 * BLOCK + 1], per_lane[2 * BLOCK + HALO:2 * BLOCK + HALO + 1]
    bb, cb16 = bg.astype(BF16), cg.astype(BF16)
    cb = lax.dot_general(cb16, bb, _NT, preferred_element_type=F32)
    xr = xs * dtx
    y_diag = []
    for p in range(GROUP_W // LANES):
        lanes = slice(p * LANES, (p + 1) * LANES)
        c_pair = cs[:, lanes]
        c_swap = _swap_halves(c_pair)
        m = []
        for c_head in (jnp.where(first_head, c_pair, c_swap), jnp.where(first_head, c_swap, c_pair)):
            m.append(cb * jnp.exp(jnp.where(causal, c_head - c_head.T, -jnp.inf)))
        x_pair = xr[:, lanes]
        x_diag = jnp.concatenate([jnp.where(first_head, x_pair, 0.0), jnp.where(first_head, 0.0, x_pair)], axis=0)
        y_diag.append(jnp.dot(jnp.concatenate(m, axis=1).astype(BF16), x_diag.astype(BF16),
                              preferred_element_type=F32))
    st = lax.dot_general(bb, (xr * jnp.exp(tot - cs)).astype(BF16), (((0,), (0,)), ((), ())),
                         preferred_element_type=F32)
    new_state = state * jnp.exp(tot) + st
    y_off = jnp.dot(cb16, state.astype(BF16), preferred_element_type=F32) * jnp.exp(cs)
    return jnp.concatenate(y_diag, axis=1) + y_off + dsk * xs, new_state


BC_WIDTH = SSM_GROUPS * SSM_STATE


def _ssd_specs(chunk):
    xs = pl.BlockSpec((BLOCK, SSM_INNER), lambda c: (chunk(c), 0))
    dt = pl.BlockSpec((BLOCK, LANES), lambda c: (chunk(c), DT_TILE))
    expand = _full((LANES, SSM_INNER))
    b = pl.BlockSpec((BLOCK, BC_WIDTH), lambda c: (chunk(c), SSM_INNER // BC_WIDTH))
    cc = pl.BlockSpec((BLOCK, BC_WIDTH), lambda c: (chunk(c), SSM_INNER // BC_WIDTH + 1))
    par = _full((1, LANES))
    state = pl.BlockSpec((None, SSM_STATE, SSM_INNER), lambda c: (chunk(c), 0, 0))
    return xs, dt, expand, b, cc, par, state


def _group_lanes(g):
    return slice(g * GROUP_W, (g + 1) * GROUP_W), slice(g * SSM_STATE, (g + 1) * SSM_STATE)


def ssd_fwd(xbc, proj, expand, bias, alog, dsk):
    nb = xbc.shape[0] // BLOCK
    xs, dt, ex, b, cc, par, state = _ssd_specs(lambda c: c)

    def body(x_ref, dt_ref, e_ref, bi_ref, al_ref, dk_ref, b_ref, c_ref, y_ref, sp_ref, st_ref):
        @pl.when(pl.program_id(0) == 0)
        def _():
            st_ref[...] = jnp.zeros_like(st_ref)

        per_lane = _ssd_heads(dt_ref[...], bi_ref[...], al_ref[...], dk_ref[...], e_ref[...])
        for g in range(SSM_GROUPS):
            wide, tile = _group_lanes(g)
            entering = st_ref[:, wide]
            sp_ref[:, wide] = entering
            y_ref[:, wide], st_ref[:, wide] = _ssd_group(x_ref[:, wide], per_lane[:, wide], b_ref[:, tile],
                                                         c_ref[:, tile], entering)

    return _call(body, name="ssd_fwd", grid=(nb,), in_specs=[xs, dt, ex, par, par, par, b, cc],
                 out_specs=[xs, state],
                 out_shape=[jax.ShapeDtypeStruct((nb * BLOCK, SSM_INNER), F32),
                            jax.ShapeDtypeStruct((nb, SSM_STATE, SSM_INNER), F32)],
                 scratch=[pltpu.VMEM((SSM_STATE, SSM_INNER), F32)],
                 sem=("arbitrary",))(xbc, proj, expand, bias, alog, dsk, xbc, xbc)


def ssd_bwd(dy, xbc, proj, expand, bias, alog, dsk, states, comm):
    nb = xbc.shape[0] // BLOCK
    last = nb - 1
    xs, dt, ex, b, cc, par, state = _ssd_specs(lambda c: last - c)
    tile = pl.BlockSpec((BLOCK, LANES), lambda c: (last - c, 0))
    nspec = pl.BlockSpec((BLOCK, BC_WIDTH), lambda c: (last - c, 0))

    def body(dy_ref, x_ref, dt_ref, e_ref, bi_ref, al_ref, dk_ref, b_ref, c_ref, sp_ref,
             dx_ref, ddt_ref, db_ref, dc_ref, dbi_ref, dal_ref, ddk_ref, ds_ref):
        @pl.when(pl.program_id(0) == 0)
        def _():
            ds_ref[...] = jnp.zeros_like(ds_ref)
            dbi_ref[...] = jnp.zeros_like(dbi_ref)
            dal_ref[...] = jnp.zeros_like(dal_ref)
            ddk_ref[...] = jnp.zeros_like(ddk_ref)

        expand = e_ref[...]
        per_lane, heads_vjp = jax.vjp(lambda t, bi, al, dk: _ssd_heads(t, bi, al, dk, expand), dt_ref[...], bi_ref[...],
                                      al_ref[...], dk_ref[...])
        d_per_lane = []
        for g in range(SSM_GROUPS):
            wide, tile_lanes = _group_lanes(g)
            _, vjp = jax.vjp(_ssd_group, x_ref[:, wide], per_lane[:, wide], b_ref[:, tile_lanes], c_ref[:, tile_lanes],
                             sp_ref[:, wide])
            (dx_ref[:, wide], d_lanes, db_ref[:, tile_lanes], dc_ref[:, tile_lanes],
             ds_ref[:, wide]) = vjp((dy_ref[:, wide], ds_ref[:, wide]))
            d_per_lane.append(d_lanes)
        ddt, dbi, dal, ddk = heads_vjp(jnp.concatenate(d_per_lane, axis=1))
        ddt_ref[...] = ddt.astype(BF16)
        dbi_ref[...] += dbi
        dal_ref[...] += dal
        ddk_ref[...] += ddk

    def at():
        c = pl.program_id(0)
        return c == 0, c == 0, c == last

    par_shape = jax.ShapeDtypeStruct((1, LANES), F32)
    return _call_with_comm(
        body, comm, at, (dy, xbc, proj, expand, bias, alog, dsk, xbc, xbc, states), name="ssd_bwd",
        grid=(nb,), in_specs=[xs, xs, dt, ex, par, par, par, b, cc, state],
        out_specs=[xs, tile, nspec, nspec, par, par, par],
        out_shape=[jax.ShapeDtypeStruct((nb * BLOCK, SSM_INNER), F32), jax.ShapeDtypeStruct((nb * BLOCK, LANES), BF16),
                   jax.ShapeDtypeStruct((nb * BLOCK, BC_WIDTH), F32), jax.ShapeDtypeStruct((nb * BLOCK, BC_WIDTH), F32),
                   par_shape, par_shape, par_shape],
        scratch=[pltpu.VMEM((SSM_STATE, SSM_INNER), F32)])


SLAB_ROWS = 16
SLAB_META_ROW = 8
SLAB_META_SHAPE = (8, 2 * D_MODEL)
SLAB_LOSS_ROW = 7


def pack_small(dcw, dcb, dgpre, dgpost, dbias, dalog, ddsk, dsinks, dgn, dmeta, loss_tile):
    def body(cw, cb, gpre, gpost, dtb, al, dk, sk, gn, meta, loss, o_ref):
        o_ref[...] = jnp.zeros_like(o_ref)
        o_ref[SLAB_LOSS_ROW:SLAB_LOSS_ROW + 1, 0:LANES] = loss[0:1, :]
        o_ref[0:CONV_WIDTH, :] = cw[...]
        o_ref[4:5, :] = cb[...]
        o_ref[5:6, 0:1024] = gpre[...]
        o_ref[5:6, 1024:2048] = gpost[...]
        o_ref[5:6, 2048:2176] = dtb[...]
        o_ref[5:6, 2176:2304] = al[...]
        o_ref[5:6, 2304:2432] = dk[...]
        o_ref[5:6, 2432:2560] = sk[...]
        o_ref[6:7, 0:SSM_INNER] = gn[...]
        o_ref[SLAB_META_ROW:SLAB_ROWS, 0:SLAB_META_SHAPE[1]] = meta[...]

    args = (dcw, dcb, dgpre, dgpost, dbias, dalog, ddsk, dsinks, dgn, dmeta, loss_tile)
    return _call(body, name="pack_small", in_specs=[_full(a.shape) for a in args],
                 out_specs=_full((SLAB_ROWS, CONV_DIM)), out_shape=jax.ShapeDtypeStruct((SLAB_ROWS, CONV_DIM), F32))(*args)


def _lane_tile(v):
    return jnp.pad(v, ((0, 0), (0, LANES - v.shape[1])))


def kernel(x, meta_tokens, g_pre, w_in, conv_w, conv_b, dt_bias, a_log, d_skip, attn_sinks, g_ssm_norm, w_out_att, w_out_ssm, w_out, g_post, loss_target, m_meta_tokens, m_g_pre, m_w_in, m_conv_w, m_conv_b, m_dt_bias, m_a_log, m_d_skip, m_attn_sinks, m_g_ssm_norm, m_w_out_att, m_w_out_ssm, m_w_out, m_g_post, v_meta_tokens, v_g_pre, v_w_in, v_conv_w, v_conv_b, v_dt_bias, v_a_log, v_d_skip, v_attn_sinks, v_g_ssm_norm, v_w_out_att, v_w_out_ssm, v_w_out, v_g_post):
    chip = _chip_index()

    conv_w_rows = jnp.pad(conv_w[0], ((0, 2 * 8 - CONV_WIDTH), (0, 0)))
    w_in_t, m_w_in_t, v_w_in_t = w_in[0].T, m_w_in[0].T, v_w_in[0].T
    (h, u), (gathered_w_in, g_conv_w, g_meta) = prep(
        "gather_w_in", x, g_pre, TwoLevelGather([pack_w_in(w_in_t), conv_w_rows, meta_tokens]))
    w_all_t = unpack_w_in(gathered_w_in)
    cw_full = g_conv_w[:, :CONV_WIDTH].transpose(1, 0, 2).reshape(CONV_WIDTH, CONV_DIM)
    behind_w_in = 0.0 * g_meta[0, 0, 0]
    w_out_flight, w_out_started = chip_exchange_start(
        "gather_w_out_start", [(w[0] + behind_w_in).astype(BF16) for w in (w_out_att, w_out_ssm, w_out)], False)

    proj = project("in_proj", u, w_all_t, w_out_started)

    sinks3 = attn_sinks.reshape(ATT_Q_HEADS, 1, 1)
    a_att = attn_fwd(proj, sinks3)

    xbc = conv_fwd(proj, cw_full, conv_b)
    expand = _head_expand()
    head_pars = (_lane_tile(dt_bias), _lane_tile(a_log), _lane_tile(d_skip))
    y_ssd, states = ssd_fwd(xbc, proj, expand, *head_pars)
    woa, wos, wo = [g.reshape(-1, D_MODEL) for g in chip_exchange_wait("gather_w_out_wait", w_out_flight, False, y_ssd)]

    (yn, merged, dout, dy_att, dy_ssm, da_att, dy_ssd, dz_ssm, dga, dgs, dres, loss_tile, dg_post, dgn) = tail(
        y_ssd, proj, a_att, x, loss_target, woa, wos, wo, g_ssm_norm, g_post)

    dwo = mm_tn("out_proj_dw", merged, dout)
    dwoa = mm_tn("att_out_dw", a_att, dy_att)
    dwos = mm_tn("ssm_out_dw", yn, dy_ssm)
    dq, dz_att, dk, dv, dkmeta, dvmeta, dsinks3 = attn_bwd(da_att, proj, sinks3)
    dk = dk.at[PAD_ROWS:BLOCK].add(dkmeta).astype(BF16)
    dv = dv.at[PAD_ROWS:BLOCK].add(dvmeta).astype(BF16)

    def pieces(g):
        return g.reshape(4, 2, g.shape[0] // 8, g.shape[1])

    def to_owner(g):
        return (lambda ref, dev: ref.at[_chip_of(dev), dev[2]], (g.shape[0] // 8, g.shape[1]))

    (dxs, ddt_tile, dbg, dcg, dbias, dalog, ddsk), sent_w_out = ssd_bwd(
        dy_ssd, xbc, proj, expand, *head_pars, states,
        DirectExchange([pieces(dwoa), pieces(dwos), pieces(dwo)], [to_owner(dwoa), to_owner(dwos), to_owner(dwo)],
                       ALL_MASKS, "dev", 8))
    dxs_raw, dcw_xs, dcb_xs = conv_bwd("conv_bwd_x", [dxs], 0, proj, cw_full, conv_b)
    dbc_raw, dcw_bc, dcb_bc = conv_bwd("conv_bwd_bc", [dbg, dcg], SSM_INNER // D_MODEL, proj, cw_full, conv_b)
    dcw = jnp.concatenate([dcw_xs, dcw_bc], axis=1)
    dcb = jnp.concatenate([dcb_xs, dcb_bc], axis=1)

    narrow = jnp.concatenate([dk, dv, ddt_tile, jnp.zeros((dk.shape[0], N_ACT - N_ALIGNED), BF16)], axis=1)
    dproj = [dz_ssm, dxs_raw, dbc_raw, dq, dz_att, dga, dgs, narrow]
    by_shard = [7, 3, 4, 0, 1, 2, 5, 6]
    partial, from_sibling = weight_grad_t("in_proj_dw", dproj, by_shard, u)
    grads_flight, started = pair_sum_exchange_start(
        "reduce_w_in_start", partial.reshape(4, 2, PACK_W // 2, D_MODEL), from_sibling)
    du = project_back("in_proj_dx", dproj, w_all_t, started)
    grad_x, dmeta, dg_pre = prep_bwd(h, du, dres, g_pre)

    halves = [sum_slots("sum_" + nm, r, started) for nm, r in zip(("w_out_att", "w_out_ssm", "w_out"), sent_w_out)]
    shared = run_comm("share_w_out", DirectExchange(halves, [None] * 3, SIBLING_MASK, "core", 2))
    g_woa, g_wos, g_wo = [f.reshape(2 * f.shape[1], f.shape[2]) for f in shared]
    d_woa, nm_woa, nv_woa = adamw_rows("adamw_w_out_att", g_woa, w_out_att[0], m_w_out_att[0], v_w_out_att[0])
    d_wos, nm_wos, nv_wos = adamw_rows("adamw_w_out_ssm", g_wos, w_out_ssm[0], m_w_out_ssm[0], v_w_out_ssm[0])
    d_wo, nm_wo, nv_wo = adamw_rows("adamw_w_out", g_wo, w_out[0], m_w_out[0], v_w_out[0])

    window_done = dg_pre + 0.0 * (d_woa[0:1] + d_wos[0:1] + d_wo[0:1])
    sent_w_in, = chip_exchange_wait("reduce_w_in_wait", grads_flight, True, window_done)
    slab = pack_small(dcw, dcb, dg_pre, dg_post, dbias, dalog, ddsk, _lane_tile(dsinks3.reshape(1, ATT_Q_HEADS)), dgn,
                      dmeta.reshape(SLAB_META_SHAPE), loss_tile)
    slabs, shared_w_in = run_comm("share_w_in", Both(DirectExchange([slab], [None], ALL_MASKS, "dev", 8),
                                                     PairOfSums(sent_w_in)))
    small = sum_slots("sum_small", slabs, started)
    loss = small[SLAB_LOSS_ROW, 0]
    g_w_in, d_w_in, nm_w_in, nv_w_in = [
        a.T for a in adamw_w_in(shared_w_in.reshape(PACK_W, D_MODEL), w_in_t, m_w_in_t, v_w_in_t)]

    cw_cols = CONV_DIM // 4
    meta_cols = D_MODEL // 4
    g_small = {
        "meta_tokens": lax.dynamic_slice(
            small[SLAB_META_ROW:SLAB_ROWS, 0:SLAB_META_SHAPE[1]].reshape(N_META, D_MODEL), (0, chip * meta_cols),
            (N_META, meta_cols)),
        "g_pre": small[5:6, 0:1024],
        "conv_w": lax.dynamic_slice(small, (0, chip * cw_cols), (CONV_WIDTH, cw_cols)),
        "conv_b": small[4:5, :],
        "dt_bias": small[5:6, 2048:2048 + SSM_HEADS],
        "a_log": small[5:6, 2176:2176 + SSM_HEADS],
        "d_skip": small[5:6, 2304:2304 + SSM_HEADS],
        "attn_sinks": small[5:6, 2432:2432 + ATT_Q_HEADS],
        "g_ssm_norm": small[6:7, 0:SSM_INNER],
        "g_post": small[5:6, 1024:2048],
    }
    names = list(g_small)
    w_small = dict(meta_tokens=meta_tokens, g_pre=g_pre, conv_w=conv_w[0], conv_b=conv_b, dt_bias=dt_bias, a_log=a_log,
                   d_skip=d_skip, attn_sinks=attn_sinks, g_ssm_norm=g_ssm_norm, g_post=g_post)
    m_small = dict(meta_tokens=m_meta_tokens, g_pre=m_g_pre, conv_w=m_conv_w[0], conv_b=m_conv_b, dt_bias=m_dt_bias,
                   a_log=m_a_log, d_skip=m_d_skip, attn_sinks=m_attn_sinks, g_ssm_norm=m_g_ssm_norm, g_post=m_g_post)
    v_small = dict(meta_tokens=v_meta_tokens, g_pre=v_g_pre, conv_w=v_conv_w[0], conv_b=v_conv_b, dt_bias=v_dt_bias,
                   a_log=v_a_log, d_skip=v_d_skip, attn_sinks=v_attn_sinks, g_ssm_norm=v_g_ssm_norm, g_post=v_g_post)
    upd = dict(zip(names, adamw_small([g_small[k] for k in names], [w_small[k] for k in names],
                                      [m_small[k] for k in names], [v_small[k] for k in names])))

    lead = {"conv_w"}

    def shaped(name, a):
        return a[None] if name in lead else a

    grads = dict(g_small, w_in=g_w_in, w_out_att=g_woa, w_out_ssm=g_wos, w_out=g_wo)
    deltas = dict({k: upd[k][0] for k in names}, w_in=d_w_in, w_out_att=d_woa, w_out_ssm=d_wos, w_out=d_wo)
    new_m = dict({k: upd[k][1] for k in names}, w_in=nm_w_in, w_out_att=nm_woa, w_out_ssm=nm_wos, w_out=nm_wo)
    new_v = dict({k: upd[k][2] for k in names}, w_in=nv_w_in, w_out_att=nv_woa, w_out_ssm=nv_wos, w_out=nv_wo)
    lead |= {"w_in", "w_out_att", "w_out_ssm", "w_out"}
    order = ["meta_tokens", "g_pre", "w_in", "conv_w", "conv_b", "dt_bias", "a_log", "d_skip", "attn_sinks",
             "g_ssm_norm", "w_out_att", "w_out_ssm", "w_out", "g_post"]
    outs = [loss, grad_x]
    for group in (grads, deltas, new_m, new_v):
        outs += [shaped(k, group[k]) for k in order]
    return tuple(outs)
```
